```python
import math
import jax, jax.numpy as jnp
from jax import lax
import numpy as np

D_MODEL = 1024
BATCH = 8
SEQ = 4096
DEPTH = 1

DN_HEADS = 4
DN_DK = 128
DN_DV = 128
CONV_K = 4
CHUNK = 64
MLA_HEADS = 4
QK_NOPE = 128
QK_ROPE = 64
V_HEAD = 128
Q_LORA = 512
KV_LORA = 256
ROPE_THETA = 10000.0
Q_BLOCK = 128
D_FF = -(-8 * D_MODEL // (3 * 256)) * 256

DN_QK = DN_HEADS * DN_DK
DN_VW = DN_HEADS * DN_DV
DN_CONV_CH = 2 * DN_QK + DN_VW
MLA_Q_DIM = QK_NOPE + QK_ROPE
MLA_VW = MLA_HEADS * V_HEAD
MIX_WIDTH = DN_VW + MLA_VW
SPLIT_Z = DN_CONV_CH
SPLIT_BETA = SPLIT_Z + DN_VW
SPLIT_A = SPLIT_BETA + DN_HEADS
SPLIT_CQ = SPLIT_A + DN_HEADS
SPLIT_CKV = SPLIT_CQ + Q_LORA
SPLIT_KR = SPLIT_CKV + KV_LORA
N_IN = SPLIT_KR + QK_ROPE

DEEPNORM_ALPHA = (2.0 * DEPTH) ** 0.25
DEEPNORM_BETA = (8.0 * DEPTH) ** -0.25

kernel_name = "hybrid_gdn_mla_deepnorm_adaln"


def _layernorm(x, g, b, eps=1e-5):
    xf = x.astype(jnp.float32)
    mu = jnp.mean(xf, axis=-1, keepdims=True)
    var = jnp.mean(jnp.square(xf - mu), axis=-1, keepdims=True)
    y = (xf - mu) * lax.rsqrt(var + eps)
    return (y * g.astype(jnp.float32) + b.astype(jnp.float32)).astype(x.dtype)


def _rmsnorm(x, g, eps=1e-6):
    xf = x.astype(jnp.float32)
    y = xf * lax.rsqrt(jnp.mean(jnp.square(xf), axis=-1, keepdims=True) + eps)
    return (y * g.astype(jnp.float32)).astype(x.dtype)


def _l2norm(x, eps=1e-6):
    xf = x.astype(jnp.float32)
    return xf * lax.rsqrt(jnp.sum(jnp.square(xf), axis=-1, keepdims=True) + eps)


def _rope(t, cos, sin):
    t1, t2 = jnp.split(t.astype(jnp.float32), 2, axis=-1)
    return jnp.concatenate([t1 * cos - t2 * sin, t2 * cos + t1 * sin], axis=-1).astype(t.dtype)


def _gated_delta_rule(q, k, v, g, beta):
    b_, s_, h_, dk = q.shape
    dv = v.shape[-1]
    nc = s_ // CHUNK

    def to_chunks(t):
        return t.reshape(b_, nc, CHUNK, h_, -1).transpose(0, 3, 1, 2, 4)

    q, k, v = to_chunks(q), to_chunks(k), to_chunks(v)
    g = g.reshape(b_, nc, CHUNK, h_).transpose(0, 3, 1, 2)
    beta = beta.reshape(b_, nc, CHUNK, h_).transpose(0, 3, 1, 2)
    gc = jnp.cumsum(g, axis=-1)
    idx = jnp.arange(CHUNK)
    incl = idx[:, None] >= idx[None, :]
    strict = idx[:, None] > idx[None, :]
    diff = gc[..., :, None] - gc[..., None, :]
    decay = jnp.where(incl, jnp.exp(jnp.where(incl, diff, 0.0)), 0.0)
    kb = k * beta[..., None]
    a_mat = jnp.where(strict, jnp.einsum('bhnid,bhnjd->bhnij', kb, k) * decay, 0.0)
    lhs = a_mat + jnp.eye(CHUNK, dtype=a_mat.dtype)
    rhs = jnp.concatenate([kb * jnp.exp(gc)[..., None], v * beta[..., None]], axis=-1)
    wu = lax.linalg.triangular_solve(lhs, rhs, left_side=True, lower=True)
    w_c, u_c = wu[..., :dk], wu[..., dk:]
    attn = jnp.where(incl, jnp.einsum('bhnid,bhnjd->bhnij', q, k) * decay, 0.0)
    qg = q * jnp.exp(gc)[..., None]
    g_last = gc[..., -1]
    kd = k * jnp.exp(g_last[..., None] - gc)[..., None]

    def step(state, inp):
        qg_n, w_n, u_n, attn_n, kd_n, gl_n = inp
        v_new = u_n - jnp.einsum('bhcd,bhde->bhce', w_n, state)
        o = jnp.einsum('bhcd,bhde->bhce', qg_n, state) + jnp.einsum('bhij,bhje->bhie', attn_n, v_new)
        state = state * jnp.exp(gl_n)[..., None, None] + jnp.einsum('bhcd,bhce->bhde', kd_n, v_new)
        return state, o

    xs = (jnp.moveaxis(qg, 2, 0), jnp.moveaxis(w_c, 2, 0), jnp.moveaxis(u_c, 2, 0),
          jnp.moveaxis(attn, 2, 0), jnp.moveaxis(kd, 2, 0), jnp.moveaxis(g_last, 2, 0))
    state0 = jnp.zeros((b_, h_, dk, dv), jnp.float32)
    _, o = lax.scan(step, state0, xs)
    return o.transpose(1, 0, 3, 2, 4).reshape(b_, s_, h_, dv)


def _mla_attention(q_nope, q_rope, k_nope, k_rope, v):
    b_, s_, h_, dn = q_nope.shape
    dv = v.shape[-1]
    nq = s_ // Q_BLOCK
    scale = 1.0 / math.sqrt(QK_NOPE + QK_ROPE)
    qn_b = q_nope.reshape(b_, nq, Q_BLOCK, h_, dn).transpose(1, 0, 2, 3, 4)
    qr_b = q_rope.reshape(b_, nq, Q_BLOCK, h_, -1).transpose(1, 0, 2, 3, 4)
    starts = jnp.arange(nq, dtype=jnp.int32) * Q_BLOCK
    key_idx = jnp.arange(s_, dtype=jnp.int32)

    def block(args):
        qn, qr, start = args
        sc = (jnp.einsum('bqhd,bkhd->bhqk', qn, k_nope)
              + jnp.einsum('bqhd,bkd->bhqk', qr, k_rope)).astype(jnp.float32) * scale
        q_idx = start + jnp.arange(Q_BLOCK, dtype=jnp.int32)
        mask = key_idx[None, :] <= q_idx[:, None]
        p = jax.nn.softmax(jnp.where(mask, sc, -jnp.inf), axis=-1).astype(v.dtype)
        return jnp.einsum('bhqk,bkhd->bqhd', p, v)

    out = lax.map(block, (qn_b, qr_b, starts))
    return out.transpose(1, 0, 2, 3, 4).reshape(b_, s_, h_ * dv)


def _hybrid_mixer(h, cos, sin, w_in, conv_w, a_log, dt_bias, dn_norm_g,
                  q_norm_g, w_uq, kv_norm_g, w_ukv, w_o):
    b_, s_, _ = h.shape
    proj = h @ w_in
    qkv, z, b_raw, a_raw, cq, ckv, kr = jnp.split(
        proj, [SPLIT_Z, SPLIT_BETA, SPLIT_A, SPLIT_CQ, SPLIT_CKV, SPLIT_KR], axis=-1)

    qkv = lax.conv_general_dilated(qkv, conv_w, window_strides=(1,), padding=[(CONV_K - 1, 0)],
                                   dimension_numbers=('NWC', 'WIO', 'NWC'),
                                   feature_group_count=DN_CONV_CH)
    qkv = jax.nn.silu(qkv)
    q_dn, k_dn, v_dn = jnp.split(qkv, [DN_QK, 2 * DN_QK], axis=-1)
    q_dn = _l2norm(q_dn.reshape(b_, s_, DN_HEADS, DN_DK)) * (DN_DK ** -0.5)
    k_dn = _l2norm(k_dn.reshape(b_, s_, DN_HEADS, DN_DK))
    v_dn = v_dn.reshape(b_, s_, DN_HEADS, DN_DV).astype(jnp.float32)
    beta = jax.nn.sigmoid(b_raw.astype(jnp.float32))
    g = -jnp.exp(a_log.astype(jnp.float32)) * jax.nn.softplus(
        a_raw.astype(jnp.float32) + dt_bias.astype(jnp.float32))
    o_dn = _gated_delta_rule(q_dn, k_dn, v_dn, g, beta).astype(h.dtype)
    z = z.reshape(b_, s_, DN_HEADS, DN_DV)
    o_dn = (_rmsnorm(o_dn, dn_norm_g) * jax.nn.silu(z)).reshape(b_, s_, DN_VW)

    q_m = (_rmsnorm(cq, q_norm_g) @ w_uq).reshape(b_, s_, MLA_HEADS, MLA_Q_DIM)
    q_nope, q_rope = q_m[..., :QK_NOPE], q_m[..., QK_NOPE:]
    q_rope = _rope(q_rope, cos[:, :, None, :], sin[:, :, None, :])
    kv = (_rmsnorm(ckv, kv_norm_g) @ w_ukv).reshape(b_, s_, MLA_HEADS, QK_NOPE + V_HEAD)
    k_nope, v_m = kv[..., :QK_NOPE], kv[..., QK_NOPE:]
    k_rope = _rope(kr, cos, sin)
    o_mla = _mla_attention(q_nope, q_rope, k_nope, k_rope, v_m)

    return jnp.concatenate([o_dn, o_mla], axis=-1) @ w_o


def _fwd_setup_inputs(seed: int = 0) -> dict:
    key = jax.random.key(seed)
    ks = jax.random.split(key, 24)
    f32 = jnp.float32
    nrm = lambda k, shape, s: jax.random.normal(k, shape, f32) * s
    x = jax.random.normal(ks[0], (BATCH, SEQ, D_MODEL), f32)
    c = jax.random.normal(ks[1], (BATCH, D_MODEL), f32)
    positions = (jnp.arange(SEQ, dtype=jnp.int32)[None, :]
                 + jax.random.randint(ks[2], (BATCH, 1), 0, SEQ, dtype=jnp.int32))
    w_ada = nrm(ks[3], (DEPTH, D_MODEL, 6 * D_MODEL), D_MODEL ** -0.5)
    b_ada = nrm(ks[4], (DEPTH, 6 * D_MODEL), 0.02)
    w_in = nrm(ks[5], (DEPTH, D_MODEL, N_IN), D_MODEL ** -0.5)
    conv_w = nrm(ks[6], (DEPTH, CONV_K, 1, DN_CONV_CH), CONV_K ** -0.5)
    a_log = jnp.log(jax.random.uniform(ks[7], (DEPTH, DN_HEADS), f32, 1.0, 16.0))
    dt = jnp.exp(jax.random.uniform(ks[8], (DEPTH, DN_HEADS), f32, math.log(1e-3), math.log(1e-1)))
    dt_bias = dt + jnp.log(-jnp.expm1(-dt))
    dn_norm_g = 1.0 + nrm(ks[9], (DEPTH, DN_DV), 0.02)
    q_norm_g = 1.0 + nrm(ks[10], (DEPTH, Q_LORA), 0.02)
    w_uq = nrm(ks[11], (DEPTH, Q_LORA, MLA_HEADS * MLA_Q_DIM), Q_LORA ** -0.5)
    kv_norm_g = 1.0 + nrm(ks[12], (DEPTH, KV_LORA), 0.02)
    w_ukv = nrm(ks[13], (DEPTH, KV_LORA, MLA_HEADS * (QK_NOPE + V_HEAD)), KV_LORA ** -0.5)
    w_o = nrm(ks[14], (DEPTH, MIX_WIDTH, D_MODEL), MIX_WIDTH ** -0.5 * DEEPNORM_BETA)
    ln1_g = 1.0 + nrm(ks[15], (DEPTH, D_MODEL), 0.02)
    ln1_b = nrm(ks[16], (DEPTH, D_MODEL), 0.02)
    w_gate = nrm(ks[17], (DEPTH, D_MODEL, D_FF), D_MODEL ** -0.5)
    w_up = nrm(ks[18], (DEPTH, D_MODEL, D_FF), D_MODEL ** -0.5)
    w_down = nrm(ks[19], (DEPTH, D_FF, D_MODEL), D_FF ** -0.5 * DEEPNORM_BETA)
    ln2_g = 1.0 + nrm(ks[20], (DEPTH, D_MODEL), 0.02)
    ln2_b = nrm(ks[21], (DEPTH, D_MODEL), 0.02)
    return {"x": x, "c": c, "positions": positions, "w_ada": w_ada, "b_ada": b_ada,
            "w_in": w_in, "conv_w": conv_w, "a_log": a_log, "dt_bias": dt_bias,
            "dn_norm_g": dn_norm_g, "q_norm_g": q_norm_g, "w_uq": w_uq,
            "kv_norm_g": kv_norm_g, "w_ukv": w_ukv, "w_o": w_o,
            "ln1_g": ln1_g, "ln1_b": ln1_b, "w_gate": w_gate, "w_up": w_up,
            "w_down": w_down, "ln2_g": ln2_g, "ln2_b": ln2_b}


def _fwd_reference(x, c, positions, w_ada, b_ada, w_in, conv_w, a_log, dt_bias, dn_norm_g,
              q_norm_g, w_uq, kv_norm_g, w_ukv, w_o, ln1_g, ln1_b, w_gate, w_up,
              w_down, ln2_g, ln2_b):
    inv_freq = 1.0 / (ROPE_THETA ** (jnp.arange(0, QK_ROPE, 2, dtype=jnp.float32) / QK_ROPE))
    ang = positions.astype(jnp.float32)[..., None] * inv_freq
    cos, sin = jnp.cos(ang), jnp.sin(ang)
    c_act = jax.nn.silu(c)
    for l in range(DEPTH):
        mod = (c_act @ w_ada[l] + b_ada[l])[:, None, :]
        sh_m, sc_m, gt_m, sh_f, sc_f, gt_f = jnp.split(mod, 6, axis=-1)
        h = x * (1.0 + sc_m) + sh_m
        mix = _hybrid_mixer(h, cos, sin, w_in[l], conv_w[l], a_log[l], dt_bias[l], dn_norm_g[l],
                            q_norm_g[l], w_uq[l], kv_norm_g[l], w_ukv[l], w_o[l])
        x = _layernorm(DEEPNORM_ALPHA * x + gt_m * mix, ln1_g[l], ln1_b[l])
        h = x * (1.0 + sc_f) + sh_f
        ff = (jax.nn.silu(h @ w_gate[l]) * (h @ w_up[l])) @ w_down[l]
        x = _layernorm(DEEPNORM_ALPHA * x + gt_f * ff, ln2_g[l], ln2_b[l])
    return x


import jax as _jax
import jax.numpy as _jnp

TWIN_FORMAT = 'train_step'
FWD_PARAMS = ['x', 'c', 'positions', 'w_ada', 'b_ada', 'w_in', 'conv_w', 'a_log', 'dt_bias', 'dn_norm_g', 'q_norm_g', 'w_uq', 'kv_norm_g', 'w_ukv', 'w_o', 'ln1_g', 'ln1_b', 'w_gate', 'w_up', 'w_down', 'ln2_g', 'ln2_b']
TWIN_WEIGHTS = ['w_ada', 'b_ada', 'w_in', 'conv_w', 'a_log', 'dt_bias', 'dn_norm_g', 'q_norm_g', 'w_uq', 'kv_norm_g', 'w_ukv', 'w_o', 'ln1_g', 'ln1_b', 'w_gate', 'w_up', 'w_down', 'ln2_g', 'ln2_b']
TWIN_DIFF_INPUT = 'x'
TWIN_INPUTS = ['x', 'c', 'positions', 'w_ada', 'b_ada', 'w_in', 'conv_w', 'a_log', 'dt_bias', 'dn_norm_g', 'q_norm_g', 'w_uq', 'kv_norm_g', 'w_ukv', 'w_o', 'ln1_g', 'ln1_b', 'w_gate', 'w_up', 'w_down', 'ln2_g', 'ln2_b', 'loss_target', 'm_w_ada', 'm_b_ada', 'm_w_in', 'm_conv_w', 'm_a_log', 'm_dt_bias', 'm_dn_norm_g', 'm_q_norm_g', 'm_w_uq', 'm_kv_norm_g', 'm_w_ukv', 'm_w_o', 'm_ln1_g', 'm_ln1_b', 'm_w_gate', 'm_w_up', 'm_w_down', 'm_ln2_g', 'm_ln2_b', 'v_w_ada', 'v_b_ada', 'v_w_in', 'v_conv_w', 'v_a_log', 'v_dt_bias', 'v_dn_norm_g', 'v_q_norm_g', 'v_w_uq', 'v_kv_norm_g', 'v_w_ukv', 'v_w_o', 'v_ln1_g', 'v_ln1_b', 'v_w_gate', 'v_w_up', 'v_w_down', 'v_ln2_g', 'v_ln2_b']
TWIN_OUTPUTS = ['loss', 'grad_x', 'grad_w_ada', 'grad_b_ada', 'grad_w_in', 'grad_conv_w', 'grad_a_log', 'grad_dt_bias', 'grad_dn_norm_g', 'grad_q_norm_g', 'grad_w_uq', 'grad_kv_norm_g', 'grad_w_ukv', 'grad_w_o', 'grad_ln1_g', 'grad_ln1_b', 'grad_w_gate', 'grad_w_up', 'grad_w_down', 'grad_ln2_g', 'grad_ln2_b', 'delta_w_ada', 'delta_b_ada', 'delta_w_in', 'delta_conv_w', 'delta_a_log', 'delta_dt_bias', 'delta_dn_norm_g', 'delta_q_norm_g', 'delta_w_uq', 'delta_kv_norm_g', 'delta_w_ukv', 'delta_w_o', 'delta_ln1_g', 'delta_ln1_b', 'delta_w_gate', 'delta_w_up', 'delta_w_down', 'delta_ln2_g', 'delta_ln2_b', 'new_m_w_ada', 'new_m_b_ada', 'new_m_w_in', 'new_m_conv_w', 'new_m_a_log', 'new_m_dt_bias', 'new_m_dn_norm_g', 'new_m_q_norm_g', 'new_m_w_uq', 'new_m_kv_norm_g', 'new_m_w_ukv', 'new_m_w_o', 'new_m_ln1_g', 'new_m_ln1_b', 'new_m_w_gate', 'new_m_w_up', 'new_m_w_down', 'new_m_ln2_g', 'new_m_ln2_b', 'new_v_w_ada', 'new_v_b_ada', 'new_v_w_in', 'new_v_conv_w', 'new_v_a_log', 'new_v_dt_bias', 'new_v_dn_norm_g', 'new_v_q_norm_g', 'new_v_w_uq', 'new_v_kv_norm_g', 'new_v_w_ukv', 'new_v_w_o', 'new_v_ln1_g', 'new_v_ln1_b', 'new_v_w_gate', 'new_v_w_up', 'new_v_w_down', 'new_v_ln2_g', 'new_v_ln2_b']
TWIN_LEAF_KINDS = {'loss': 'loss', 'grad_x': 'grad_x', 'grad_w_ada': 'grad_w', 'grad_b_ada': 'grad_w', 'grad_w_in': 'grad_w', 'grad_conv_w': 'grad_w', 'grad_a_log': 'grad_w', 'grad_dt_bias': 'grad_w', 'grad_dn_norm_g': 'grad_w', 'grad_q_norm_g': 'grad_w', 'grad_w_uq': 'grad_w', 'grad_kv_norm_g': 'grad_w', 'grad_w_ukv': 'grad_w', 'grad_w_o': 'grad_w', 'grad_ln1_g': 'grad_w', 'grad_ln1_b': 'grad_w', 'grad_w_gate': 'grad_w', 'grad_w_up': 'grad_w', 'grad_w_down': 'grad_w', 'grad_ln2_g': 'grad_w', 'grad_ln2_b': 'grad_w', 'delta_w_ada': 'delta_w', 'delta_b_ada': 'delta_w', 'delta_w_in': 'delta_w', 'delta_conv_w': 'delta_w', 'delta_a_log': 'delta_w', 'delta_dt_bias': 'delta_w', 'delta_dn_norm_g': 'delta_w', 'delta_q_norm_g': 'delta_w', 'delta_w_uq': 'delta_w', 'delta_kv_norm_g': 'delta_w', 'delta_w_ukv': 'delta_w', 'delta_w_o': 'delta_w', 'delta_ln1_g': 'delta_w', 'delta_ln1_b': 'delta_w', 'delta_w_gate': 'delta_w', 'delta_w_up': 'delta_w', 'delta_w_down': 'delta_w', 'delta_ln2_g': 'delta_w', 'delta_ln2_b': 'delta_w', 'new_m_w_ada': 'new_m', 'new_m_b_ada': 'new_m', 'new_m_w_in': 'new_m', 'new_m_conv_w': 'new_m', 'new_m_a_log': 'new_m', 'new_m_dt_bias': 'new_m', 'new_m_dn_norm_g': 'new_m', 'new_m_q_norm_g': 'new_m', 'new_m_w_uq': 'new_m', 'new_m_kv_norm_g': 'new_m', 'new_m_w_ukv': 'new_m', 'new_m_w_o': 'new_m', 'new_m_ln1_g': 'new_m', 'new_m_ln1_b': 'new_m', 'new_m_w_gate': 'new_m', 'new_m_w_up': 'new_m', 'new_m_w_down': 'new_m', 'new_m_ln2_g': 'new_m', 'new_m_ln2_b': 'new_m', 'new_v_w_ada': 'new_v', 'new_v_b_ada': 'new_v', 'new_v_w_in': 'new_v', 'new_v_conv_w': 'new_v', 'new_v_a_log': 'new_v', 'new_v_dt_bias': 'new_v', 'new_v_dn_norm_g': 'new_v', 'new_v_q_norm_g': 'new_v', 'new_v_w_uq': 'new_v', 'new_v_kv_norm_g': 'new_v', 'new_v_w_ukv': 'new_v', 'new_v_w_o': 'new_v', 'new_v_ln1_g': 'new_v', 'new_v_ln1_b': 'new_v', 'new_v_w_gate': 'new_v', 'new_v_w_up': 'new_v', 'new_v_w_down': 'new_v', 'new_v_ln2_g': 'new_v', 'new_v_ln2_b': 'new_v'}


def _forward(args):
    return _fwd_reference(*[args[k] for k in FWD_PARAMS])


def _output_shape():
    out = _jax.eval_shape(lambda: _forward(_fwd_setup_inputs(0)))
    return out.shape, out.dtype

N_MICROBATCH = 1
ADAM_LR = 0.001
ADAM_B1 = 0.9
ADAM_B2 = 0.999
ADAM_EPS = 1e-08
ADAM_WD = 0.01
ADAM_STEP = 10
PER_EXAMPLE_BATCH_AXIS = {'x': 0, 'c': 0, 'positions': 0, 'loss_target': 0}
SHARED_INPUTS = []
_WEIGHT_DTYPES = {'w_ada': _jnp.float32, 'b_ada': _jnp.float32, 'w_in': _jnp.float32, 'conv_w': _jnp.float32, 'a_log': _jnp.float32, 'dt_bias': _jnp.float32, 'dn_norm_g': _jnp.float32, 'q_norm_g': _jnp.float32, 'w_uq': _jnp.float32, 'kv_norm_g': _jnp.float32, 'w_ukv': _jnp.float32, 'w_o': _jnp.float32, 'ln1_g': _jnp.float32, 'ln1_b': _jnp.float32, 'w_gate': _jnp.float32, 'w_up': _jnp.float32, 'w_down': _jnp.float32, 'ln2_g': _jnp.float32, 'ln2_b': _jnp.float32}
MOMENT_SCALE = {'w_ada': 4.657675e-02, 'b_ada': 7.734688e-02, 'w_in': 3.771825e-02, 'conv_w': 4.147166e-02, 'a_log': 1.705570e-01, 'dt_bias': 1.610937e-01, 'dn_norm_g': 9.842917e-02, 'q_norm_g': 9.606238e-03, 'w_uq': 7.758650e-03, 'kv_norm_g': 6.915539e-02, 'w_ukv': 2.487120e-02, 'w_o': 7.388278e-02, 'ln1_g': 8.859488e-01, 'ln1_b': 4.666188e-01, 'w_gate': 3.695486e-02, 'w_up': 3.606385e-02, 'w_down': 1.007089e-01, 'ln2_g': 3.214464e+01, 'ln2_b': 2.326310e+00}


def _to_microbatches(a, axis):
    t = _jnp.moveaxis(a, axis, 0)
    t = t.reshape((N_MICROBATCH, t.shape[0] // N_MICROBATCH) + t.shape[1:])
    return _jnp.moveaxis(t, 1, axis + 1)


def setup_inputs(seed: int = 0) -> dict:
    inp = _fwd_setup_inputs(seed)
    key = _jax.random.fold_in(_jax.random.key(seed), 7919)
    shape, _ = _output_shape()
    out = dict(inp)
    out["loss_target"] = _jax.random.normal(_jax.random.fold_in(key, 0), shape, _jnp.float32)
    for i, name in enumerate(TWIN_WEIGHTS):
        w = inp[name].astype(_jnp.float32)
        if MOMENT_SCALE is None:
            s = _jnp.sqrt(_jnp.mean(_jnp.square(w)) + 1e-30)
        else:
            s = MOMENT_SCALE[name]
        km, kv = _jax.random.split(_jax.random.fold_in(key, i + 1))
        out[name] = w
        out["m_" + name] = s * _jax.random.normal(km, w.shape, _jnp.float32)
        out["v_" + name] = (s * s) * _jax.random.uniform(kv, w.shape, _jnp.float32, 0.5, 1.5)
    if N_MICROBATCH > 1:
        for name, axis in PER_EXAMPLE_BATCH_AXIS.items():
            out[name] = _to_microbatches(out[name], axis)
    return {'x': out['x'], 'c': out['c'], 'positions': out['positions'], 'w_ada': out['w_ada'], 'b_ada': out['b_ada'], 'w_in': out['w_in'], 'conv_w': out['conv_w'], 'a_log': out['a_log'], 'dt_bias': out['dt_bias'], 'dn_norm_g': out['dn_norm_g'], 'q_norm_g': out['q_norm_g'], 'w_uq': out['w_uq'], 'kv_norm_g': out['kv_norm_g'], 'w_ukv': out['w_ukv'], 'w_o': out['w_o'], 'ln1_g': out['ln1_g'], 'ln1_b': out['ln1_b'], 'w_gate': out['w_gate'], 'w_up': out['w_up'], 'w_down': out['w_down'], 'ln2_g': out['ln2_g'], 'ln2_b': out['ln2_b'], 'loss_target': out['loss_target'], 'm_w_ada': out['m_w_ada'], 'm_b_ada': out['m_b_ada'], 'm_w_in': out['m_w_in'], 'm_conv_w': out['m_conv_w'], 'm_a_log': out['m_a_log'], 'm_dt_bias': out['m_dt_bias'], 'm_dn_norm_g': out['m_dn_norm_g'], 'm_q_norm_g': out['m_q_norm_g'], 'm_w_uq': out['m_w_uq'], 'm_kv_norm_g': out['m_kv_norm_g'], 'm_w_ukv': out['m_w_ukv'], 'm_w_o': out['m_w_o'], 'm_ln1_g': out['m_ln1_g'], 'm_ln1_b': out['m_ln1_b'], 'm_w_gate': out['m_w_gate'], 'm_w_up': out['m_w_up'], 'm_w_down': out['m_w_down'], 'm_ln2_g': out['m_ln2_g'], 'm_ln2_b': out['m_ln2_b'], 'v_w_ada': out['v_w_ada'], 'v_b_ada': out['v_b_ada'], 'v_w_in': out['v_w_in'], 'v_conv_w': out['v_conv_w'], 'v_a_log': out['v_a_log'], 'v_dt_bias': out['v_dt_bias'], 'v_dn_norm_g': out['v_dn_norm_g'], 'v_q_norm_g': out['v_q_norm_g'], 'v_w_uq': out['v_w_uq'], 'v_kv_norm_g': out['v_kv_norm_g'], 'v_w_ukv': out['v_w_ukv'], 'v_w_o': out['v_w_o'], 'v_ln1_g': out['v_ln1_g'], 'v_ln1_b': out['v_ln1_b'], 'v_w_gate': out['v_w_gate'], 'v_w_up': out['v_w_up'], 'v_w_down': out['v_w_down'], 'v_ln2_g': out['v_ln2_g'], 'v_ln2_b': out['v_ln2_b']}


def _loss(weights, diff, rest, loss_target):
    with _jax.named_scope("forward"):
        args = {**rest, TWIN_DIFF_INPUT: diff, **{k: w.astype(_WEIGHT_DTYPES[k]) for k, w in weights.items()}}
        y = _forward(args)
    with _jax.named_scope("loss_head"):
        err = _jnp.square(y.astype(_jnp.float32) - loss_target)
        return 0.5 * _jnp.sum(_jnp.mean(err, axis=-1)) if err.ndim else 0.5 * err


def _adamw(w, g, m, v):
    m = ADAM_B1 * m + (1.0 - ADAM_B1) * g
    v = ADAM_B2 * v + (1.0 - ADAM_B2) * _jnp.square(g)
    m_hat = m / (1.0 - ADAM_B1 ** ADAM_STEP)
    v_hat = v / (1.0 - ADAM_B2 ** ADAM_STEP)
    delta = -ADAM_LR * (m_hat / (_jnp.sqrt(v_hat) + ADAM_EPS) + ADAM_WD * w)
    return delta, m, v


def reference(x, c, positions, w_ada, b_ada, w_in, conv_w, a_log, dt_bias, dn_norm_g, q_norm_g, w_uq, kv_norm_g, w_ukv, w_o, ln1_g, ln1_b, w_gate, w_up, w_down, ln2_g, ln2_b, loss_target, m_w_ada, m_b_ada, m_w_in, m_conv_w, m_a_log, m_dt_bias, m_dn_norm_g, m_q_norm_g, m_w_uq, m_kv_norm_g, m_w_ukv, m_w_o, m_ln1_g, m_ln1_b, m_w_gate, m_w_up, m_w_down, m_ln2_g, m_ln2_b, v_w_ada, v_b_ada, v_w_in, v_conv_w, v_a_log, v_dt_bias, v_dn_norm_g, v_q_norm_g, v_w_uq, v_kv_norm_g, v_w_ukv, v_w_o, v_ln1_g, v_ln1_b, v_w_gate, v_w_up, v_w_down, v_ln2_g, v_ln2_b):
    given = dict(x=x, c=c, positions=positions, w_ada=w_ada, b_ada=b_ada, w_in=w_in, conv_w=conv_w, a_log=a_log, dt_bias=dt_bias, dn_norm_g=dn_norm_g, q_norm_g=q_norm_g, w_uq=w_uq, kv_norm_g=kv_norm_g, w_ukv=w_ukv, w_o=w_o, ln1_g=ln1_g, ln1_b=ln1_b, w_gate=w_gate, w_up=w_up, w_down=w_down, ln2_g=ln2_g, ln2_b=ln2_b, loss_target=loss_target, m_w_ada=m_w_ada, m_b_ada=m_b_ada, m_w_in=m_w_in, m_conv_w=m_conv_w, m_a_log=m_a_log, m_dt_bias=m_dt_bias, m_dn_norm_g=m_dn_norm_g, m_q_norm_g=m_q_norm_g, m_w_uq=m_w_uq, m_kv_norm_g=m_kv_norm_g, m_w_ukv=m_w_ukv, m_w_o=m_w_o, m_ln1_g=m_ln1_g, m_ln1_b=m_ln1_b, m_w_gate=m_w_gate, m_w_up=m_w_up, m_w_down=m_w_down, m_ln2_g=m_ln2_g, m_ln2_b=m_ln2_b, v_w_ada=v_w_ada, v_b_ada=v_b_ada, v_w_in=v_w_in, v_conv_w=v_conv_w, v_a_log=v_a_log, v_dt_bias=v_dt_bias, v_dn_norm_g=v_dn_norm_g, v_q_norm_g=v_q_norm_g, v_w_uq=v_w_uq, v_kv_norm_g=v_kv_norm_g, v_w_ukv=v_w_ukv, v_w_o=v_w_o, v_ln1_g=v_ln1_g, v_ln1_b=v_ln1_b, v_w_gate=v_w_gate, v_w_up=v_w_up, v_w_down=v_w_down, v_ln2_g=v_ln2_g, v_ln2_b=v_ln2_b)
    weights = {n: given[n] for n in TWIN_WEIGHTS}
    shared = {n: given[n] for n in SHARED_INPUTS}
    per_example = {n: given[n] for n in ['x', 'c', 'positions']}
    grad_fn = _jax.value_and_grad(_loss, argnums=(0, 1))

    def one_microbatch(ex, loss_target):
        ex = dict(ex)
        diff = ex.pop(TWIN_DIFF_INPUT)
        return grad_fn(weights, diff, {**shared, **ex}, loss_target)

    if N_MICROBATCH == 1:
        loss, (grad_w, grad_x) = one_microbatch(per_example, given["loss_target"])
    else:
        def body(carry, xs):
            loss_sum, grad_sum = carry
            l_k, (gw_k, gx_k) = one_microbatch(xs[0], xs[1])
            with _jax.named_scope("update"):
                return (loss_sum + l_k, _jax.tree.map(_jnp.add, grad_sum, gw_k)), gx_k

        init = (_jnp.zeros((), _jnp.float32), _jax.tree.map(_jnp.zeros_like, weights))
        (loss, grad_w), grad_x = _jax.lax.scan(body, init, (per_example, given["loss_target"]))
    with _jax.named_scope("update"):
        delta_w, new_m, new_v = {}, {}, {}
        for n in TWIN_WEIGHTS:
            delta_w[n], new_m[n], new_v[n] = _adamw(weights[n], grad_w[n], given["m_" + n], given["v_" + n])
    return (loss, grad_x, *[grad_w[n] for n in TWIN_WEIGHTS], *[delta_w[n] for n in TWIN_WEIGHTS],
            *[new_m[n] for n in TWIN_WEIGHTS], *[new_v[n] for n in TWIN_WEIGHTS])
```

```python
import functools
import math

import jax
import jax.numpy as jnp
from jax import lax
from jax.experimental import pallas as pl
from jax.experimental.pallas import tpu as pltpu

F32 = jnp.float32
BF = jnp.bfloat16
HI = lax.Precision.HIGHEST

N_DEV = 8
DN_HEADS = 4
DN_DK = 128
DN_DV = 128
CONV_K = 4
CHUNK = 64
MLA_HEADS = 4
QK_NOPE = 128
QK_ROPE = 64
V_HEAD = 128
Q_LORA = 512
KV_LORA = 256
ROPE_THETA = 10000.0
DEPTH = 1
DEEPNORM_ALPHA = (2.0 * DEPTH) ** 0.25
LANE = 128
CONV_HALO = 8

DN_QK = DN_HEADS * DN_DK
DN_VW = DN_HEADS * DN_DV
DN_CONV_CH = 2 * DN_QK + DN_VW
MLA_VW = MLA_HEADS * V_HEAD
MLA_QCAT = QK_NOPE + LANE
N_IN = DN_CONV_CH + DN_VW + 2 * DN_HEADS + Q_LORA + KV_LORA + QK_ROPE
P_QKV = 0
P_Z = DN_CONV_CH
P_CQ = P_Z + DN_VW
P_CKV = P_CQ + Q_LORA
P_BA = P_CKV + KV_LORA
P_KR = P_BA + LANE
N_INP = P_KR + LANE

ADAM_LR = 0.001
ADAM_B1 = 0.9
ADAM_B2 = 0.999
ADAM_EPS = 1e-08
ADAM_WD = 0.01
ADAM_STEP = 10

NN = (((1,), (0,)), ((), ()))
NT = (((1,), (1,)), ((), ()))
TN = (((0,), (0,)), ((), ()))


def _pick(n, prefs):
    for p in prefs:
        if n % p == 0:
            return p
    return n


def _full(shape):
    return pl.BlockSpec(shape, lambda *_: (0,) * len(shape))


def _dot(a, b, dims=NN):
    return lax.dot_general(a, b, dims, preferred_element_type=F32)


def _doth(a, b, dims=NN):
    return lax.dot_general(a, b, dims, precision=HI, preferred_element_type=F32)


@jax.custom_vjp
def _mmb(a, b):
    return _dot(a.astype(BF), b.astype(BF), NN)


def _mmb_fwd(a, b):
    return _mmb(a, b), (a, b)


def _mmb_bwd(res, g):
    a, b = res
    gb = g.astype(BF)
    return (_dot(gb, b.astype(BF), NT).astype(a.dtype), _dot(a.astype(BF), gb, TN).astype(b.dtype))


_mmb.defvjp(_mmb_fwd, _mmb_bwd)


@jax.custom_vjp
def _mmb_nt(a, b):
    return _dot(a.astype(BF), b.astype(BF), NT)


def _mmb_nt_fwd(a, b):
    return _mmb_nt(a, b), (a, b)


def _mmb_nt_bwd(res, g):
    a, b = res
    gb = g.astype(BF)
    return (_dot(gb, b.astype(BF), NN).astype(a.dtype), _dot(gb, a.astype(BF), TN).astype(b.dtype))


_mmb_nt.defvjp(_mmb_nt_fwd, _mmb_nt_bwd)


@jax.custom_vjp
def _mmb_tn(a, b):
    return _dot(a.astype(BF), b.astype(BF), TN)


def _mmb_tn_fwd(a, b):
    return _mmb_tn(a, b), (a, b)


def _mmb_tn_bwd(res, g):
    a, b = res
    gb = g.astype(BF)
    return (_dot(b.astype(BF), gb, NT).astype(a.dtype), _dot(a.astype(BF), gb, NN).astype(b.dtype))


_mmb_tn.defvjp(_mmb_tn_fwd, _mmb_tn_bwd)


def _sigmoid(x):
    return 0.5 * (jnp.tanh(0.5 * x) + 1.0)


def _silu(x):
    return x * _sigmoid(x)


def _softplus(x):
    return jnp.maximum(x, 0.0) + jnp.log(1.0 + jnp.exp(-jnp.abs(x)))


def _layernorm(x, g, b, eps=1e-5):
    mu = jnp.mean(x, axis=-1, keepdims=True)
    xc = x - mu
    var = jnp.mean(xc * xc, axis=-1, keepdims=True)
    return xc * lax.rsqrt(var + eps) * g + b


def _rmsnorm(x, g, eps=1e-6):
    return x * lax.rsqrt(jnp.mean(x * x, axis=-1, keepdims=True) + eps) * g


def _l2norm(x, eps=1e-6):
    return x * lax.rsqrt(jnp.sum(x * x, axis=-1, keepdims=True) + eps)


def _rowwise(name, fn, rows, vecs, out_rows, out_accs, tm):
    rows = [r if isinstance(r, tuple) else (r, r.shape[1], 0) for r in rows]
    t = rows[0][0].shape[0]
    tm = min(tm, t)
    assert t % tm == 0
    nr, nv, no = len(rows), len(vecs), len(out_rows)

    def body(*refs):
        ins = [r[...] for r in refs[:nr + nv]]
        outs = fn(*ins)
        outs = outs if isinstance(outs, (tuple, list)) else (outs,)
        o_rows = refs[nr + nv:nr + nv + no]
        o_accs = refs[nr + nv + no:]
        for o, val in zip(o_rows, outs[:no]):
            o[...] = val.astype(o.dtype)
        if o_accs:
            @pl.when(pl.program_id(0) == 0)
            def _():
                for o in o_accs:
                    o[...] = jnp.zeros_like(o)
            for o, val in zip(o_accs, outs[no:]):
                o[...] += val

    in_specs = [pl.BlockSpec((tm, w), functools.partial(lambda i, j: (i, j), j=j)) for (_, w, j) in rows]
    in_specs += [_full(v.shape) for v in vecs]
    out_specs = [pl.BlockSpec((tm, w), lambda i: (i, 0)) for (w, _) in out_rows]
    out_specs += [_full(s) for s in out_accs]
    out_shape = [jax.ShapeDtypeStruct((t, w), d) for (w, d) in out_rows]
    out_shape += [jax.ShapeDtypeStruct(s, F32) for s in out_accs]
    res = pl.pallas_call(
        body, grid=(t // tm,), in_specs=in_specs, out_specs=out_specs, out_shape=out_shape, name=name,
        compiler_params=pltpu.CompilerParams(dimension_semantics=("arbitrary",)),
    )(*[r[0] for r in rows], *vecs)
    return res


def _matmul(a, b, mode, name, out_dtype=F32):
    if mode == "nn":
        (m, k), n = a.shape, b.shape[1]
    elif mode == "nt":
        (m, k), n = a.shape, b.shape[0]
    else:
        (k, m), n = a.shape, b.shape[1]
    tm = _pick(m, (1024, 512, 256, 128))
    tn = _pick(n, (1024, 512, 256, 128))
    tk = _pick(k, (512, 256, 128))
    nk = k // tk
    dims = {"nn": NN, "nt": NT, "tn": TN}[mode]

    def body(a_ref, b_ref, o_ref, acc_ref):
        kk = pl.program_id(2)

        @pl.when(kk == 0)
        def _():
            acc_ref[...] = jnp.zeros_like(acc_ref)

        acc_ref[...] += _dot(a_ref[...].astype(BF), b_ref[...].astype(BF), dims)

        @pl.when(kk == nk - 1)
        def _():
            o_ref[...] = acc_ref[...].astype(o_ref.dtype)

    a_spec = pl.BlockSpec((tk, tm), lambda i, j, kk: (kk, i)) if mode == "tn" else pl.BlockSpec((tm, tk), lambda i, j, kk: (i, kk))
    b_spec = pl.BlockSpec((tn, tk), lambda i, j, kk: (j, kk)) if mode == "nt" else pl.BlockSpec((tk, tn), lambda i, j, kk: (kk, j))
    return pl.pallas_call(
        body, grid=(m // tm, n // tn, nk), in_specs=[a_spec, b_spec],
        out_specs=pl.BlockSpec((tm, tn), lambda i, j, kk: (i, j)),
        out_shape=jax.ShapeDtypeStruct((m, n), out_dtype),
        scratch_shapes=[pltpu.VMEM((tm, tn), F32)], name=name,
        compiler_params=pltpu.CompilerParams(dimension_semantics=("parallel", "parallel", "arbitrary")),
    )(a, b)


def _exchange(x, name, scatter):
    r, c = x.shape[-2:]

    def body(x_ref, o_ref, send_sems, recv_sems, local_sem):
        mx, my, mc = lax.axis_index("x"), lax.axis_index("y"), lax.axis_index("c")
        me = 4 * mx + 2 * my + mc
        src_me = x_ref.at[me] if scatter else x_ref
        mine = pltpu.make_async_copy(src_me, o_ref.at[me], local_sem)
        mine.start()
        copies = []
        for k in range(1, N_DEV):
            px, py, pc = mx ^ (k >> 2), my ^ ((k >> 1) & 1), mc ^ (k & 1)
            peer = 4 * px + 2 * py + pc
            cp = pltpu.make_async_remote_copy(
                src_ref=x_ref.at[peer] if scatter else x_ref, dst_ref=o_ref.at[me],
                send_sem=send_sems.at[k - 1], recv_sem=recv_sems.at[k - 1],
                device_id=(px, py, pc), device_id_type=pl.DeviceIdType.MESH)
            cp.start()
            copies.append((cp, peer))
        for k, (cp, peer) in enumerate(copies):
            pltpu.make_async_remote_copy(
                src_ref=src_me, dst_ref=o_ref.at[peer], send_sem=send_sems.at[k], recv_sem=recv_sems.at[k],
                device_id=(mx, my, mc), device_id_type=pl.DeviceIdType.MESH).wait_recv()
        for cp, _ in copies:
            cp.wait_send()
        mine.wait()

    return pl.pallas_call(
        body, out_shape=jax.ShapeDtypeStruct((N_DEV, r, c), x.dtype),
        in_specs=[pl.BlockSpec(memory_space=pl.ANY)], out_specs=pl.BlockSpec(memory_space=pl.ANY),
        scratch_shapes=[pltpu.SemaphoreType.DMA((N_DEV - 1,)), pltpu.SemaphoreType.DMA((N_DEV - 1,)),
                        pltpu.SemaphoreType.DMA],
        name=name,
    )(x)


def _sum_slots(x, name):
    _, r, c = x.shape
    tr = _pick(r, (512, 256, 128, 64, 32, 16))

    def body(x_ref, o_ref):
        acc = x_ref[0].astype(F32)
        for s in range(1, N_DEV):
            acc = acc + x_ref[s].astype(F32)
        o_ref[...] = acc

    return pl.pallas_call(
        body, grid=(r // tr,), in_specs=[pl.BlockSpec((N_DEV, tr, c), lambda i: (0, i, 0))],
        out_specs=pl.BlockSpec((tr, c), lambda i: (i, 0)), out_shape=jax.ShapeDtypeStruct((r, c), F32), name=name,
        compiler_params=pltpu.CompilerParams(dimension_semantics=("arbitrary",)),
    )(x)


def _mod_fwd(c_all, w_ada, b_ada_mine):
    def body(c_ref, w_ref, b_ref, o_ref):
        o_ref[...] = _doth(_silu(c_ref[...]), w_ref[...]) + b_ref[...]

    return pl.pallas_call(body, out_shape=jax.ShapeDtypeStruct((c_all.shape[0], w_ada.shape[1]), F32), name="mod_fwd")(c_all, w_ada, b_ada_mine)


def _mod_bwd(c_all, dmod_mine):
    def body(c_ref, d_ref, o_ref):
        o_ref[...] = _doth(_silu(c_ref[...]), d_ref[...], TN)

    return pl.pallas_call(body, out_shape=jax.ShapeDtypeStruct((c_all.shape[1], dmod_mine.shape[1]), F32), name="mod_bwd")(c_all, dmod_mine)


def _conv_fwd(proj, conv_w8, tm):
    t = proj.shape[0]
    ch = DN_CONV_CH

    def body(x_ref, w_ref, o_ref, buf):
        @pl.when(pl.program_id(0) == 0)
        def _():
            buf[pl.ds(0, CONV_HALO), :] = jnp.zeros((CONV_HALO, ch), F32)

        buf[pl.ds(CONV_HALO, tm), :] = x_ref[...]
        acc = jnp.zeros((tm, ch), F32)
        for j in range(CONV_K):
            acc = acc + buf[pl.ds(CONV_HALO - (CONV_K - 1) + j, tm), :] * w_ref[pl.ds(j, 1), :]
        o_ref[...] = _silu(acc)
        buf[pl.ds(0, CONV_HALO), :] = buf[pl.ds(tm, CONV_HALO), :]

    return pl.pallas_call(
        body, grid=(t // tm,), in_specs=[pl.BlockSpec((tm, ch), lambda i: (i, 0)), _full(conv_w8.shape)],
        out_specs=pl.BlockSpec((tm, ch), lambda i: (i, 0)), out_shape=jax.ShapeDtypeStruct((t, ch), F32),
        scratch_shapes=[pltpu.VMEM((tm + CONV_HALO, ch), F32)], name="conv_fwd",
        compiler_params=pltpu.CompilerParams(dimension_semantics=("arbitrary",)),
    )(proj, conv_w8)


def _conv_bwd(proj, conv_w8, dact, tm):
    t = proj.shape[0]
    ch = DN_CONV_CH
    nt = t // tm
    hb = tm // CONV_HALO

    def body(x_ref, xp_ref, w_ref, dy_ref, dx_ref, dw_ref, xbuf, dbuf):
        step = pl.program_id(0)

        @pl.when(step == 0)
        def _():
            dbuf[pl.ds(tm, CONV_HALO), :] = jnp.zeros((CONV_HALO, ch), F32)
            dw_ref[...] = jnp.zeros_like(dw_ref)

        first = step == nt - 1
        xbuf[pl.ds(0, CONV_HALO), :] = jnp.where(first, 0.0, xp_ref[...])
        xbuf[pl.ds(CONV_HALO, tm), :] = x_ref[...]
        pre = jnp.zeros((tm, ch), F32)
        for j in range(CONV_K):
            pre = pre + xbuf[pl.ds(CONV_HALO - (CONV_K - 1) + j, tm), :] * w_ref[pl.ds(j, 1), :]
        sg = _sigmoid(pre)
        dpre = dy_ref[...] * (sg * (1.0 + pre * (1.0 - sg)))
        dbuf[pl.ds(0, tm), :] = dpre
        dx = jnp.zeros((tm, ch), F32)
        for j in range(CONV_K):
            dx = dx + dbuf[pl.ds(CONV_K - 1 - j, tm), :] * w_ref[pl.ds(j, 1), :]
            dw_ref[pl.ds(j, 1), :] += jnp.sum(dpre * xbuf[pl.ds(CONV_HALO - (CONV_K - 1) + j, tm), :], axis=0, keepdims=True)
        dx_ref[...] = dx.astype(dx_ref.dtype)
        dbuf[pl.ds(tm, CONV_HALO), :] = dbuf[pl.ds(0, CONV_HALO), :]

    rev = lambda i: (nt - 1 - i, 0)
    prev = lambda i: (jnp.maximum((nt - 1 - i) * hb - 1, 0), 0)
    return pl.pallas_call(
        body, grid=(nt,),
        in_specs=[pl.BlockSpec((tm, ch), rev), pl.BlockSpec((CONV_HALO, ch), prev), _full(conv_w8.shape),
                  pl.BlockSpec((tm, ch), rev)],
        out_specs=[pl.BlockSpec((tm, ch), rev), _full(conv_w8.shape)],
        out_shape=[jax.ShapeDtypeStruct((t, ch), BF), jax.ShapeDtypeStruct(conv_w8.shape, F32)],
        scratch_shapes=[pltpu.VMEM((tm + CONV_HALO, ch), F32), pltpu.VMEM((tm + CONV_HALO, ch), F32)], name="conv_bwd",
        compiler_params=pltpu.CompilerParams(dimension_semantics=("arbitrary",)),
    )(proj, proj, conv_w8, dact)


def _gdn_chunk(qa, ka, va, ba, al8, dt8, s_prev, e_b, e_a):
    c = qa.shape[0]
    ri = lax.broadcasted_iota(jnp.int32, (c, c), 0)
    ci = lax.broadcasted_iota(jnp.int32, (c, c), 1)
    incl = ri >= ci
    strict = ri > ci
    eye = (ri == ci).astype(F32)
    lane0 = (lax.broadcasted_iota(jnp.int32, (c, LANE), 1) == 0).astype(F32)
    row0 = (lax.broadcasted_iota(jnp.int32, (LANE, c), 0) == 0).astype(F32)

    q = _l2norm(qa) * (DN_DK ** -0.5)
    k = _l2norm(ka)
    beta = _sigmoid(_doth(ba, e_b))
    a_log = jnp.sum(_doth(al8, e_b), axis=0, keepdims=True)
    dt_bias = jnp.sum(_doth(dt8, e_b), axis=0, keepdims=True)
    g = -jnp.exp(a_log) * _softplus(_doth(ba, e_a) + dt_bias)
    gc = _doth(incl.astype(F32), g)
    g_last = jnp.sum(g, axis=0, keepdims=True)
    diff = _doth(gc, row0) - _doth(lane0, gc, NT)
    decay = jnp.where(incl, jnp.exp(jnp.where(incl, diff, 0.0)), 0.0)
    kb = k * beta
    a_mat = jnp.where(strict, _mmb_nt(kb, k) * decay, 0.0)
    xm = -a_mat
    inv = eye + xm
    for _ in range(int(math.log2(c)) - 1):
        xm = _doth(xm, xm)
        inv = inv + _doth(inv, xm)
    egc = jnp.exp(gc)
    w_c = _doth(inv, kb * egc)
    u_c = _doth(inv, va * beta)
    attn = jnp.where(incl, _mmb_nt(q, k) * decay, 0.0)
    qg = q * egc
    kd = k * jnp.exp(g_last - gc)
    v_new = u_c - _mmb(w_c, s_prev)
    o = _mmb(qg, s_prev) + _mmb(attn, v_new)
    s_new = s_prev * jnp.exp(g_last) + _mmb_tn(kd, v_new)
    return o, s_new


def _head_spreaders(h):
    ri = lax.broadcasted_iota(jnp.int32, (LANE, LANE), 0)
    return (ri == h).astype(F32), (ri == DN_HEADS + h).astype(F32)


def _gdn_fwd(qkv, proj, al8, dt8, cb):
    t = qkv.shape[0]
    nc = t // CHUNK
    tm = cb * CHUNK

    def body(qkv_ref, ba_ref, al_ref, dt_ref, o_ref, ss_ref, s_scr):
        @pl.when(pl.program_id(0) == 0)
        def _():
            s_scr[...] = jnp.zeros_like(s_scr)

        for cc in range(cb):
            rows = pl.ds(cc * CHUNK, CHUNK)
            ba = ba_ref[rows, :]
            for h in range(DN_HEADS):
                e_b, e_a = _head_spreaders(h)
                s_prev = s_scr[h]
                ss_ref[cc, h] = s_prev
                o, s_new = _gdn_chunk(
                    qkv_ref[rows, pl.ds(h * DN_DK, DN_DK)], qkv_ref[rows, pl.ds(DN_QK + h * DN_DK, DN_DK)],
                    qkv_ref[rows, pl.ds(2 * DN_QK + h * DN_DV, DN_DV)], ba, al_ref[...], dt_ref[...], s_prev, e_b, e_a)
                o_ref[rows, pl.ds(h * DN_DV, DN_DV)] = o
                s_scr[h] = s_new

    return pl.pallas_call(
        body, grid=(nc // cb,),
        in_specs=[pl.BlockSpec((tm, DN_CONV_CH), lambda i: (i, 0)), pl.BlockSpec((tm, LANE), lambda i: (i, P_BA // LANE)),
                  _full(al8.shape), _full(dt8.shape)],
        out_specs=[pl.BlockSpec((tm, DN_VW), lambda i: (i, 0)),
                   pl.BlockSpec((cb, DN_HEADS, DN_DK, DN_DV), lambda i: (i, 0, 0, 0))],
        out_shape=[jax.ShapeDtypeStruct((t, DN_VW), F32), jax.ShapeDtypeStruct((nc, DN_HEADS, DN_DK, DN_DV), F32)],
        scratch_shapes=[pltpu.VMEM((DN_HEADS, DN_DK, DN_DV), F32)], name="gdn_fwd",
        compiler_params=pltpu.CompilerParams(dimension_semantics=("arbitrary",)),
    )(qkv, proj, al8, dt8)


def _gdn_bwd(qkv, proj, al8, dt8, states, do, cb):
    t = qkv.shape[0]
    nc = t // CHUNK
    tm = cb * CHUNK
    ng = nc // cb

    def body(qkv_ref, ba_ref, al_ref, dt_ref, ss_ref, do_ref, dqkv_ref, dba_ref, dal_ref, ddt_ref, ds_scr):
        @pl.when(pl.program_id(0) == 0)
        def _():
            ds_scr[...] = jnp.zeros_like(ds_scr)
            dal_ref[...] = jnp.zeros_like(dal_ref)
            ddt_ref[...] = jnp.zeros_like(ddt_ref)

        for cc in reversed(range(cb)):
            rows = pl.ds(cc * CHUNK, CHUNK)
            ba = ba_ref[rows, :]
            dba = jnp.zeros((CHUNK, LANE), F32)
            for h in range(DN_HEADS):
                e_b, e_a = _head_spreaders(h)
                f = functools.partial(_gdn_chunk, e_b=e_b, e_a=e_a)
                _, vjp = jax.vjp(
                    f, qkv_ref[rows, pl.ds(h * DN_DK, DN_DK)], qkv_ref[rows, pl.ds(DN_QK + h * DN_DK, DN_DK)],
                    qkv_ref[rows, pl.ds(2 * DN_QK + h * DN_DV, DN_DV)], ba, al_ref[...], dt_ref[...], ss_ref[cc, h])
                dq, dk, dv, dba_h, dal, ddt, ds_prev = vjp((do_ref[rows, pl.ds(h * DN_DV, DN_DV)], ds_scr[h]))
                dqkv_ref[rows, pl.ds(h * DN_DK, DN_DK)] = dq
                dqkv_ref[rows, pl.ds(DN_QK + h * DN_DK, DN_DK)] = dk
                dqkv_ref[rows, pl.ds(2 * DN_QK + h * DN_DV, DN_DV)] = dv
                dba = dba + dba_h
                dal_ref[...] += dal
                ddt_ref[...] += ddt
                ds_scr[h] = ds_prev
            dba_ref[rows, :] = dba.astype(dba_ref.dtype)

    rev = lambda i: (ng - 1 - i, 0)
    return pl.pallas_call(
        body, grid=(ng,),
        in_specs=[pl.BlockSpec((tm, DN_CONV_CH), rev), pl.BlockSpec((tm, LANE), lambda i: (ng - 1 - i, P_BA // LANE)),
                  _full(al8.shape), _full(dt8.shape),
                  pl.BlockSpec((cb, DN_HEADS, DN_DK, DN_DV), lambda i: (ng - 1 - i, 0, 0, 0)),
                  pl.BlockSpec((tm, DN_VW), rev)],
        out_specs=[pl.BlockSpec((tm, DN_CONV_CH), rev), pl.BlockSpec((tm, LANE), rev), _full(al8.shape), _full(dt8.shape)],
        out_shape=[jax.ShapeDtypeStruct((t, DN_CONV_CH), F32), jax.ShapeDtypeStruct((t, LANE), BF),
                   jax.ShapeDtypeStruct(al8.shape, F32), jax.ShapeDtypeStruct(dt8.shape, F32)],
        scratch_shapes=[pltpu.VMEM((DN_HEADS, DN_DK, DN_DV), F32)], name="gdn_bwd",
        compiler_params=pltpu.CompilerParams(dimension_semantics=("arbitrary",)),
    )(qkv, proj, al8, dt8, states, do)


def _gdn_out(o, z, g):
    parts = []
    for h in range(DN_HEADS):
        sl = slice(h * DN_DV, (h + 1) * DN_DV)
        parts.append(_rmsnorm(o[:, sl], g) * _silu(z[:, sl]))
    return parts


def _rope_tables(pos, inv_freq2):
    lane = lax.broadcasted_iota(jnp.int32, (1, LANE), 1)
    ang = pos * inv_freq2
    cos = jnp.where(lane < QK_ROPE, jnp.cos(ang), 0.0)
    sin = jnp.where(lane < QK_ROPE // 2, -jnp.sin(ang), jnp.where(lane < QK_ROPE, jnp.sin(ang), 0.0))
    return cos, sin


def _rope_swap():
    ri = lax.broadcasted_iota(jnp.int32, (LANE, LANE), 0)
    ci = lax.broadcasted_iota(jnp.int32, (LANE, LANE), 1)
    half = QK_ROPE // 2
    return (((ci < half) & (ri == ci + half)) | ((ci >= half) & (ci < QK_ROPE) & (ri == ci - half))).astype(F32)


def _mla_prep(cq, ckv, kr, gq, gkv, w_uq, w_ukv, cos, sin, swap):
    rope = lambda u: u * cos + _doth(u, swap) * sin
    q_lin = _mmb(_rmsnorm(cq, gq), w_uq)
    kv_lin = _mmb(_rmsnorm(ckv, gkv), w_ukv)
    k_rope = rope(kr)
    qs, ks, vs = [], [], []
    for h in range(MLA_HEADS):
        qs += [q_lin[:, h * LANE:(h + 1) * LANE], rope(q_lin[:, (MLA_HEADS + h) * LANE:(MLA_HEADS + h + 1) * LANE])]
        ks += [kv_lin[:, 2 * h * LANE:(2 * h + 1) * LANE], k_rope]
        vs += [kv_lin[:, (2 * h + 1) * LANE:(2 * h + 2) * LANE]]
    return qs + ks + vs


def _mla_prep_fwd(proj, pos_col, inv_freq2, gq, gkv, w_uq, w_ukv, tm):
    t = proj.shape[0]
    nq = 2 * MLA_HEADS

    def body(cq_ref, ckv_ref, kr_ref, pos_ref, f_ref, gq_ref, gkv_ref, wq_ref, wkv_ref, q_ref, k_ref, v_ref):
        cos, sin = _rope_tables(pos_ref[...], f_ref[...])
        outs = _mla_prep(cq_ref[...], ckv_ref[...], kr_ref[...], gq_ref[...], gkv_ref[...], wq_ref[...], wkv_ref[...],
                         cos, sin, _rope_swap())
        for i in range(nq):
            q_ref[:, pl.ds(i * LANE, LANE)] = outs[i].astype(q_ref.dtype)
            k_ref[:, pl.ds(i * LANE, LANE)] = outs[nq + i].astype(k_ref.dtype)
        for h in range(MLA_HEADS):
            v_ref[:, pl.ds(h * LANE, LANE)] = outs[2 * nq + h].astype(v_ref.dtype)

    row = lambda w, j: pl.BlockSpec((tm, w), functools.partial(lambda i, j: (i, j), j=j))
    return pl.pallas_call(
        body, grid=(t // tm,),
        in_specs=[row(Q_LORA, P_CQ // Q_LORA), row(KV_LORA, P_CKV // KV_LORA), row(LANE, P_KR // LANE),
                  pl.BlockSpec((tm, 1), lambda i: (i, 0)), _full(inv_freq2.shape), _full(gq.shape), _full(gkv.shape),
                  _full(w_uq.shape), _full(w_ukv.shape)],
        out_specs=[row(nq * LANE, 0), row(nq * LANE, 0), row(MLA_VW, 0)],
        out_shape=[jax.ShapeDtypeStruct((t, nq * LANE), BF), jax.ShapeDtypeStruct((t, nq * LANE), BF),
                   jax.ShapeDtypeStruct((t, MLA_VW), BF)],
        name="mla_prep_fwd", compiler_params=pltpu.CompilerParams(dimension_semantics=("arbitrary",)),
    )(proj, proj, proj, pos_col, inv_freq2, gq, gkv, w_uq, w_ukv)


def _mla_prep_bwd(proj, pos_col, inv_freq2, gq, gkv, w_uq, w_ukv, dq, dk, dv, tm):
    t = proj.shape[0]
    nq = 2 * MLA_HEADS

    def body(cq_ref, ckv_ref, kr_ref, pos_ref, f_ref, gq_ref, gkv_ref, wq_ref, wkv_ref, dq_ref, dk_ref, dv_ref,
             dcq_ref, dckv_ref, dkr_ref, dgq_ref, dgkv_ref, dwq_ref, dwkv_ref):
        @pl.when(pl.program_id(0) == 0)
        def _():
            for o in (dgq_ref, dgkv_ref, dwq_ref, dwkv_ref):
                o[...] = jnp.zeros_like(o)

        cos, sin = _rope_tables(pos_ref[...], f_ref[...])
        f = functools.partial(_mla_prep, cos=cos, sin=sin, swap=_rope_swap())
        _, vjp = jax.vjp(f, cq_ref[...], ckv_ref[...], kr_ref[...], gq_ref[...], gkv_ref[...], wq_ref[...], wkv_ref[...])
        cts = [dq_ref[:, pl.ds(i * LANE, LANE)] for i in range(nq)]
        cts += [dk_ref[:, pl.ds(i * LANE, LANE)] for i in range(nq)]
        cts += [dv_ref[:, pl.ds(h * LANE, LANE)] for h in range(MLA_HEADS)]
        dcq, dckv, dkr, dgq, dgkv, dwq, dwkv = vjp(cts)
        dcq_ref[...] = dcq.astype(dcq_ref.dtype)
        dckv_ref[...] = dckv.astype(dckv_ref.dtype)
        dkr_ref[...] = dkr.astype(dkr_ref.dtype)
        dgq_ref[...] += dgq
        dgkv_ref[...] += dgkv
        dwq_ref[...] += dwq
        dwkv_ref[...] += dwkv

    row = lambda w, j: pl.BlockSpec((tm, w), functools.partial(lambda i, j: (i, j), j=j))
    return pl.pallas_call(
        body, grid=(t // tm,),
        in_specs=[row(Q_LORA, P_CQ // Q_LORA), row(KV_LORA, P_CKV // KV_LORA), row(LANE, P_KR // LANE),
                  pl.BlockSpec((tm, 1), lambda i: (i, 0)), _full(inv_freq2.shape), _full(gq.shape), _full(gkv.shape),
                  _full(w_uq.shape), _full(w_ukv.shape), row(nq * LANE, 0), row(nq * LANE, 0), row(MLA_VW, 0)],
        out_specs=[row(Q_LORA, 0), row(KV_LORA, 0), row(LANE, 0), _full(gq.shape), _full(gkv.shape),
                   _full(w_uq.shape), _full(w_ukv.shape)],
        out_shape=[jax.ShapeDtypeStruct((t, Q_LORA), BF), jax.ShapeDtypeStruct((t, KV_LORA), BF),
                   jax.ShapeDtypeStruct((t, LANE), BF), jax.ShapeDtypeStruct(gq.shape, F32),
                   jax.ShapeDtypeStruct(gkv.shape, F32), jax.ShapeDtypeStruct(w_uq.shape, F32),
                   jax.ShapeDtypeStruct(w_ukv.shape, F32)],
        name="mla_prep_bwd", compiler_params=pltpu.CompilerParams(dimension_semantics=("arbitrary",)),
    )(proj, proj, proj, pos_col, inv_freq2, gq, gkv, w_uq, w_ukv, dq, dk, dv)


_ATT_SCALE = 1.0 / math.sqrt(QK_NOPE + QK_ROPE)
_NEG = -1e30


def _causal(tq, tk, q0, k0):
    row = q0 + lax.broadcasted_iota(jnp.int32, (tq, tk), 0)
    col = k0 + lax.broadcasted_iota(jnp.int32, (tq, tk), 1)
    return col <= row


def _attn_fwd(q, k, v, tq, tk):
    t = q.shape[0]

    def body(q_ref, k_ref, v_ref, o_ref, lse_ref):
        i = pl.program_id(1)
        qt = q_ref[...]
        nk = ((i + 1) * tq + tk - 1) // tk

        def step(j, carry):
            m, l, acc = carry
            k0 = pl.multiple_of(j * tk, tk)
            s = _dot(qt, k_ref[pl.ds(k0, tk), :], NT) * _ATT_SCALE
            s = jnp.where(_causal(tq, tk, i * tq, k0), s, _NEG)
            m_new = jnp.maximum(m, jnp.max(s, axis=-1, keepdims=True))
            p = jnp.exp(s - m_new)
            alpha = jnp.exp(m - m_new)
            l = alpha * l + jnp.sum(p, axis=-1, keepdims=True)
            acc = alpha * acc + _dot(p.astype(BF), v_ref[pl.ds(k0, tk), :])
            return m_new, l, acc

        m, l, acc = lax.fori_loop(0, nk, step, (jnp.full((tq, 1), _NEG, F32), jnp.zeros((tq, 1), F32),
                                                jnp.zeros((tq, V_HEAD), F32)))
        o_ref[...] = acc / l
        lse_ref[...] = jnp.broadcast_to(m + jnp.log(l), (tq, LANE))

    return pl.pallas_call(
        body, grid=(MLA_HEADS, t // tq),
        in_specs=[pl.BlockSpec((tq, 2 * LANE), lambda h, i: (i, h)), pl.BlockSpec((t, 2 * LANE), lambda h, i: (0, h)),
                  pl.BlockSpec((t, V_HEAD), lambda h, i: (0, h))],
        out_specs=[pl.BlockSpec((tq, V_HEAD), lambda h, i: (i, h)), pl.BlockSpec((tq, LANE), lambda h, i: (i, h))],
        out_shape=[jax.ShapeDtypeStruct((t, MLA_VW), F32), jax.ShapeDtypeStruct((t, MLA_HEADS * LANE), F32)],
        name="attn_fwd", compiler_params=pltpu.CompilerParams(dimension_semantics=("parallel", "arbitrary")),
    )(q, k, v)


def _attn_bwd_dq(q, k, v, do, lse, delta, tq, tk):
    t = q.shape[0]

    def body(q_ref, k_ref, v_ref, do_ref, lse_ref, dl_ref, dq_ref):
        i = pl.program_id(1)
        qt = q_ref[...]
        dot_ = do_ref[...]
        lse_c = lse_ref[:, pl.ds(0, 1)]
        dl_c = dl_ref[:, pl.ds(0, 1)]
        nk = ((i + 1) * tq + tk - 1) // tk

        def step(j, acc):
            k0 = pl.multiple_of(j * tk, tk)
            kt = k_ref[pl.ds(k0, tk), :]
            s = _dot(qt, kt, NT) * _ATT_SCALE
            p = jnp.where(_causal(tq, tk, i * tq, k0), jnp.exp(s - lse_c), 0.0)
            dp = _dot(dot_, v_ref[pl.ds(k0, tk), :], NT)
            ds = (p * (dp - dl_c) * _ATT_SCALE).astype(BF)
            return acc + _dot(ds, kt)

        dq_ref[...] = lax.fori_loop(0, nk, step, jnp.zeros((tq, 2 * LANE), F32))

    return pl.pallas_call(
        body, grid=(MLA_HEADS, t // tq),
        in_specs=[pl.BlockSpec((tq, 2 * LANE), lambda h, i: (i, h)), pl.BlockSpec((t, 2 * LANE), lambda h, i: (0, h)),
                  pl.BlockSpec((t, V_HEAD), lambda h, i: (0, h)), pl.BlockSpec((tq, V_HEAD), lambda h, i: (i, h)),
                  pl.BlockSpec((tq, LANE), lambda h, i: (i, h)), pl.BlockSpec((tq, LANE), lambda h, i: (i, h))],
        out_specs=pl.BlockSpec((tq, 2 * LANE), lambda h, i: (i, h)),
        out_shape=jax.ShapeDtypeStruct((t, MLA_HEADS * 2 * LANE), F32),
        name="attn_bwd_dq", compiler_params=pltpu.CompilerParams(dimension_semantics=("parallel", "arbitrary")),
    )(q, k, v, do, lse, delta)


def _attn_bwd_dkv(q, k, v, do, lse, delta, tq, tk):
    t = q.shape[0]
    nq = t // tq

    def body(q_ref, k_ref, v_ref, do_ref, lse_ref, dl_ref, dk_ref, dv_ref):
        j = pl.program_id(1)
        kt = k_ref[...]
        vt = v_ref[...]

        def step(i, carry):
            dk, dv = carry
            q0 = pl.multiple_of(i * tq, tq)
            qt = q_ref[pl.ds(q0, tq), :]
            dot_ = do_ref[pl.ds(q0, tq), :]
            s = _dot(qt, kt, NT) * _ATT_SCALE
            p = jnp.where(_causal(tq, tk, q0, j * tk), jnp.exp(s - lse_ref[pl.ds(q0, tq), pl.ds(0, 1)]), 0.0)
            dv = dv + _dot(p.astype(BF), dot_, TN)
            dp = _dot(dot_, vt, NT)
            ds = (p * (dp - dl_ref[pl.ds(q0, tq), pl.ds(0, 1)]) * _ATT_SCALE).astype(BF)
            dk = dk + _dot(ds, qt, TN)
            return dk, dv

        dk, dv = lax.fori_loop((j * tk) // tq, nq, step, (jnp.zeros((tk, 2 * LANE), F32), jnp.zeros((tk, V_HEAD), F32)))
        dk_ref[...] = dk
        dv_ref[...] = dv

    return pl.pallas_call(
        body, grid=(MLA_HEADS, t // tk),
        in_specs=[pl.BlockSpec((t, 2 * LANE), lambda h, j: (0, h)), pl.BlockSpec((tk, 2 * LANE), lambda h, j: (j, h)),
                  pl.BlockSpec((tk, V_HEAD), lambda h, j: (j, h)), pl.BlockSpec((t, V_HEAD), lambda h, j: (0, h)),
                  pl.BlockSpec((t, LANE), lambda h, j: (0, h)), pl.BlockSpec((t, LANE), lambda h, j: (0, h))],
        out_specs=[pl.BlockSpec((tk, 2 * LANE), lambda h, j: (j, h)), pl.BlockSpec((tk, V_HEAD), lambda h, j: (j, h))],
        out_shape=[jax.ShapeDtypeStruct((t, MLA_HEADS * 2 * LANE), F32), jax.ShapeDtypeStruct((t, MLA_VW), F32)],
        name="attn_bwd_dkv", compiler_params=pltpu.CompilerParams(dimension_semantics=("parallel", "arbitrary")),
    )(q, k, v, do, lse, delta)


def _adamw(w, g, m, v, name):
    r, c = w.shape
    tr = _pick(r, (256, 128, 64, 32, 16, 8))

    def body(w_ref, g_ref, m_ref, v_ref, d_ref, nm_ref, nv_ref):
        gg = g_ref[...]
        mm = ADAM_B1 * m_ref[...] + (1.0 - ADAM_B1) * gg
        vv = ADAM_B2 * v_ref[...] + (1.0 - ADAM_B2) * jnp.square(gg)
        m_hat = mm / (1.0 - ADAM_B1 ** ADAM_STEP)
        v_hat = vv / (1.0 - ADAM_B2 ** ADAM_STEP)
        d_ref[...] = -ADAM_LR * (m_hat / (jnp.sqrt(v_hat) + ADAM_EPS) + ADAM_WD * w_ref[...])
        nm_ref[...] = mm
        nv_ref[...] = vv

    spec = pl.BlockSpec((tr, c), lambda i: (i, 0))
    return pl.pallas_call(
        body, grid=(r // tr,), in_specs=[spec] * 4, out_specs=[spec] * 3,
        out_shape=[jax.ShapeDtypeStruct((r, c), F32)] * 3, name=name,
        compiler_params=pltpu.CompilerParams(dimension_semantics=("arbitrary",)),
    )(w, g, m, v)


def _pad_cols(a, n):
    return jnp.pad(a, ((0, 0), (0, n - a.shape[1])))


def _w_in_to_padded(w):
    s_ba = P_CQ
    s_cq = s_ba + 2 * DN_HEADS
    s_kr = s_cq + Q_LORA + KV_LORA
    return jnp.concatenate([w[:, :s_ba], w[:, s_cq:s_kr], _pad_cols(w[:, s_ba:s_cq], LANE), _pad_cols(w[:, s_kr:], LANE)], axis=1)


def _w_in_from_padded(w):
    return jnp.concatenate([w[:, :P_CQ], w[:, P_BA:P_BA + 2 * DN_HEADS], w[:, P_CQ:P_BA], w[:, P_KR:P_KR + QK_ROPE]], axis=1)


def _w_uq_to_padded(w):
    w3 = w.reshape(Q_LORA, MLA_HEADS, QK_NOPE + QK_ROPE)
    nope = w3[:, :, :QK_NOPE].reshape(Q_LORA, MLA_HEADS * QK_NOPE)
    rope = jnp.pad(w3[:, :, QK_NOPE:], ((0, 0), (0, 0), (0, LANE - QK_ROPE))).reshape(Q_LORA, MLA_HEADS * LANE)
    return jnp.concatenate([nope, rope], axis=1)


def _w_uq_from_padded(w):
    nope = w[:, :MLA_HEADS * QK_NOPE].reshape(Q_LORA, MLA_HEADS, QK_NOPE)
    rope = w[:, MLA_HEADS * QK_NOPE:].reshape(Q_LORA, MLA_HEADS, LANE)[:, :, :QK_ROPE]
    return jnp.concatenate([nope, rope], axis=2).reshape(Q_LORA, MLA_HEADS * (QK_NOPE + QK_ROPE))


def _pack(pieces, width, row_mult):
    flat = jnp.concatenate([p.reshape(-1) for p in pieces])
    n = flat.shape[0]
    rows = -(-n // (width * row_mult)) * row_mult
    return jnp.pad(flat, (0, rows * width - n)).reshape(rows, width)


def _unpack(flat, shapes):
    out, o = [], 0
    for s in shapes:
        n = math.prod(s)
        out.append(flat[o:o + n].reshape(s))
        o += n
    return out


def kernel(x, c, positions, w_ada, b_ada, w_in, conv_w, a_log, dt_bias, dn_norm_g, q_norm_g, w_uq, kv_norm_g, w_ukv, w_o, ln1_g, ln1_b, w_gate, w_up, w_down, ln2_g, ln2_b, loss_target, m_w_ada, m_b_ada, m_w_in, m_conv_w, m_a_log, m_dt_bias, m_dn_norm_g, m_q_norm_g, m_w_uq, m_kv_norm_g, m_w_ukv, m_w_o, m_ln1_g, m_ln1_b, m_w_gate, m_w_up, m_w_down, m_ln2_g, m_ln2_b, v_w_ada, v_b_ada, v_w_in, v_conv_w, v_a_log, v_dt_bias, v_dn_norm_g, v_q_norm_g, v_w_uq, v_kv_norm_g, v_w_ukv, v_w_o, v_ln1_g, v_ln1_b, v_w_gate, v_w_up, v_w_down, v_ln2_g, v_ln2_b):
    me = 4 * lax.axis_index("x") + 2 * lax.axis_index("y") + lax.axis_index("c")
    t, d = x.shape[1], x.shape[2]
    ff_n = w_gate.shape[2] * N_DEV
    ada_n = w_ada.shape[2]

    small = _pack([c[0], conv_w[0, :, 0, :]], LANE, 8)
    small_all = _exchange(small, "gather_small", scatter=False)
    sm = small_all.reshape(N_DEV, -1)
    c_all = sm[:, :d]
    cw = conv_w.shape[3]
    conv_full = sm[:, d:d + CONV_K * cw].reshape(N_DEV, CONV_K, cw).transpose(1, 0, 2).reshape(CONV_K, N_DEV * cw)
    conv_w8 = jnp.pad(conv_full, ((0, 8 - CONV_K), (0, 0)))

    b_ada_mine = lax.dynamic_slice(b_ada, (0, me * ada_n), (1, ada_n))
    mod_cols = _mod_fwd(c_all, w_ada[0], b_ada_mine)
    mod_all = _exchange(mod_cols.reshape(N_DEV, 1, ada_n), "scatter_mod", scatter=True)
    mod = mod_all.reshape(1, N_DEV * ada_n)

    shard_shapes = [w_in.shape[1:], w_uq.shape[1:], w_ukv.shape[1:], w_o.shape[1:], w_gate.shape[1:], w_up.shape[1:], w_down.shape[1:]]
    wpack = _pack([w.astype(BF) for w in (w_in[0], w_uq[0], w_ukv[0], w_o[0], w_gate[0], w_up[0], w_down[0])], 512, 16)
    wall = _exchange(wpack, "gather_weights", scatter=False).reshape(N_DEV, -1)
    per_dev = [_unpack(wall[s], shard_shapes) for s in range(N_DEV)]
    cols = lambda i: jnp.concatenate([per_dev[s][i] for s in range(N_DEV)], axis=1)
    rows_ = lambda i: jnp.concatenate([per_dev[s][i] for s in range(N_DEV)], axis=0)
    w_in_p = _w_in_to_padded(cols(0))
    w_uq_p = _w_uq_to_padded(cols(1))
    w_ukv_f = cols(2)
    w_o_f = rows_(3)
    w_gu = jnp.concatenate([cols(4), cols(5)], axis=1)
    w_down_f = rows_(6)

    loc = _local_step(x[0], loss_target[0], positions[0], mod, w_in_p, w_uq_p, w_ukv_f, w_o_f, w_gu, w_down_f, conv_w8,
                      a_log, dt_bias, dn_norm_g, q_norm_g, kv_norm_g, ln1_g, ln1_b, ln2_g, ln2_b)
    (grad_x, loss_acc, dmod, d_conv8, d_al8, d_dt8, d_dn_g, d_q_g, d_kv_g, d_ln1_g, d_ln1_b, d_ln2_g, d_ln2_b,
     g_w_in_p, g_w_uq_p, g_w_ukv, g_w_o, g_w_gu, g_w_down) = loc

    def col_pieces(g, n):
        return [g[:, s * n:(s + 1) * n] for s in range(N_DEV)]

    def row_pieces(g, n):
        return [g[s * n:(s + 1) * n] for s in range(N_DEV)]

    g_w_in = _w_in_from_padded(g_w_in_p)
    g_w_uq = _w_uq_from_padded(g_w_uq_p).astype(BF)
    pieces = [col_pieces(g_w_in, w_in.shape[2]), col_pieces(g_w_uq, w_uq.shape[2]), col_pieces(g_w_ukv.astype(BF), w_ukv.shape[2]),
              row_pieces(g_w_o, w_o.shape[1]), col_pieces(g_w_gu[:, :ff_n], w_gate.shape[2]),
              col_pieces(g_w_gu[:, ff_n:], w_up.shape[2]), row_pieces(g_w_down, w_down.shape[1])]
    gpack = jnp.stack([_pack([p[s] for p in pieces], 512, 16) for s in range(N_DEV)])
    g_sum = _sum_slots(_exchange(gpack, "scatter_grads", scatter=True), "sum_grads").reshape(-1)
    gs_w_in, gs_w_uq, gs_w_ukv, gs_w_o, gs_w_gate, gs_w_up, gs_w_down = _unpack(g_sum, shard_shapes)

    small_shapes = [(6 * d,), (CONV_K, N_DEV * cw), (DN_HEADS,), (DN_HEADS,), (DN_DV,), (Q_LORA,), (KV_LORA,), (d,), (d,), (d,), (d,), (1,)]
    gsmall = _pack([dmod, d_conv8[:CONV_K], d_al8[0, :DN_HEADS], d_dt8[0, :DN_HEADS], d_dn_g, d_q_g, d_kv_g,
                    d_ln1_g, d_ln1_b, d_ln2_g, d_ln2_b, loss_acc[0, :1]], LANE, 8)
    gsmall_all = _exchange(gsmall, "gather_small_grads", scatter=False)
    dmod_all = gsmall_all.reshape(N_DEV, -1)[:, :6 * d]
    tot = _unpack(_sum_slots(gsmall_all, "sum_small_grads").reshape(-1), small_shapes)
    g_b_ada, g_conv_full, g_a_log, g_dt_bias, g_dn_g, g_q_g, g_kv_g, g_ln1_g, g_ln1_b, g_ln2_g, g_ln2_b, loss1 = tot
    loss = loss1.reshape(())
    g_conv_w = lax.dynamic_slice(g_conv_full, (0, me * cw), (CONV_K, cw))
    g_w_ada = _mod_bwd(c_all, lax.dynamic_slice(dmod_all, (0, me * ada_n), (N_DEV, ada_n)))

    grads = {"w_ada": g_w_ada[None], "b_ada": g_b_ada[None], "w_in": gs_w_in[None], "conv_w": g_conv_w[None, :, None, :],
             "a_log": g_a_log[None], "dt_bias": g_dt_bias[None], "dn_norm_g": g_dn_g[None], "q_norm_g": g_q_g[None],
             "w_uq": gs_w_uq[None], "kv_norm_g": g_kv_g[None], "w_ukv": gs_w_ukv[None], "w_o": gs_w_o[None],
             "ln1_g": g_ln1_g[None], "ln1_b": g_ln1_b[None], "w_gate": gs_w_gate[None], "w_up": gs_w_up[None],
             "w_down": gs_w_down[None], "ln2_g": g_ln2_g[None], "ln2_b": g_ln2_b[None]}
    weights = dict(w_ada=w_ada, b_ada=b_ada, w_in=w_in, conv_w=conv_w, a_log=a_log, dt_bias=dt_bias, dn_norm_g=dn_norm_g,
                   q_norm_g=q_norm_g, w_uq=w_uq, kv_norm_g=kv_norm_g, w_ukv=w_ukv, w_o=w_o, ln1_g=ln1_g, ln1_b=ln1_b,
                   w_gate=w_gate, w_up=w_up, w_down=w_down, ln2_g=ln2_g, ln2_b=ln2_b)
    ms = dict(w_ada=m_w_ada, b_ada=m_b_ada, w_in=m_w_in, conv_w=m_conv_w, a_log=m_a_log, dt_bias=m_dt_bias,
              dn_norm_g=m_dn_norm_g, q_norm_g=m_q_norm_g, w_uq=m_w_uq, kv_norm_g=m_kv_norm_g, w_ukv=m_w_ukv, w_o=m_w_o,
              ln1_g=m_ln1_g, ln1_b=m_ln1_b, w_gate=m_w_gate, w_up=m_w_up, w_down=m_w_down, ln2_g=m_ln2_g, ln2_b=m_ln2_b)
    vs = dict(w_ada=v_w_ada, b_ada=v_b_ada, w_in=v_w_in, conv_w=v_conv_w, a_log=v_a_log, dt_bias=v_dt_bias,
              dn_norm_g=v_dn_norm_g, q_norm_g=v_q_norm_g, w_uq=v_w_uq, kv_norm_g=v_kv_norm_g, w_ukv=v_w_ukv, w_o=v_w_o,
              ln1_g=v_ln1_g, ln1_b=v_ln1_b, w_gate=v_w_gate, w_up=v_w_up, w_down=v_w_down, ln2_g=v_ln2_g, ln2_b=v_ln2_b)
    names = list(weights)
    big = ("w_ada", "w_in", "w_uq", "w_ukv", "w_o", "w_gate", "w_up", "w_down")
    delta_w, new_m, new_v = {}, {}, {}
    for n in big:
        shp = weights[n].shape
        two = lambda a: a.reshape(shp[-2], shp[-1])
        dlt, nm, nv = _adamw(two(weights[n]), two(grads[n]), two(ms[n]), two(vs[n]), "adamw_" + n)
        delta_w[n], new_m[n], new_v[n] = dlt.reshape(shp), nm.reshape(shp), nv.reshape(shp)
    rest = [n for n in names if n not in big]
    rest_shapes = [weights[n].shape for n in rest]
    packed = [_pack([src[n] for n in rest], LANE, 8) for src in (weights, grads, ms, vs)]
    outs = _adamw(*packed, "adamw_small")
    for dst, o in zip((delta_w, new_m, new_v), outs):
        for n, a in zip(rest, _unpack(o.reshape(-1), rest_shapes)):
            dst[n] = a

    return (loss, grad_x[None], *[grads[n] for n in names], *[delta_w[n] for n in names],
            *[new_m[n] for n in names], *[new_v[n] for n in names])


def _local_step(xs, tgt, pos, mod, w_in_p, w_uq_p, w_ukv_f, w_o_f, w_gu, w_down_f, conv_w8,
                a_log, dt_bias, dn_norm_g, q_norm_g, kv_norm_g, ln1_g, ln1_b, ln2_g, ln2_b):
    t, d = xs.shape
    ff_n = w_down_f.shape[0]
    sh_m, sc_m, gt_m, sh_f, sc_f, gt_f = [mod[:, i * d:(i + 1) * d] for i in range(6)]
    pos_col = pos.astype(F32).reshape(t, 1)
    inv_freq = 1.0 / (ROPE_THETA ** (jnp.arange(0, QK_ROPE, 2, dtype=F32) / QK_ROPE))
    inv_freq2 = jnp.pad(jnp.concatenate([inv_freq, inv_freq]), (0, LANE - QK_ROPE)).reshape(1, LANE)
    al8 = jnp.pad(a_log, ((0, 7), (0, LANE - DN_HEADS)))
    dt8 = jnp.pad(dt_bias, ((0, 7), (0, LANE - DN_HEADS)))

    tm = min(512, t)
    tq = min(256, t)
    tk = min(512, t)

    (h1,) = _rowwise("modulate_in", lambda xx, sc, sh: xx * (1.0 + sc) + sh, [xs], [sc_m, sh_m], [(d, BF)], [], tm)
    proj = _matmul(h1, w_in_p, "nn", "in_proj")
    qkv = _conv_fwd(proj, conv_w8, min(256, t))
    o_dn, states = _gdn_fwd(qkv, proj, al8, dt8, 2)
    qc, kc, vc = _mla_prep_fwd(proj, pos_col, inv_freq2, q_norm_g, kv_norm_g, w_uq_p, w_ukv_f, tm)
    o_mla, lse = _attn_fwd(qc, kc, vc, tq, tk)

    def mix_in(o, z, om, g):
        return jnp.concatenate(_gdn_out(o, z, g) + [om], axis=1)

    (mixin,) = _rowwise("mixer_out", mix_in, [o_dn, (proj, DN_VW, P_Z // DN_VW), o_mla], [dn_norm_g], [(2 * DN_VW, BF)], [], tm)
    mix = _matmul(mixin, w_o_f, "nn", "out_proj")

    def block1(xx, mx, gt, g1, b1, sc, sh):
        x1 = _layernorm(DEEPNORM_ALPHA * xx + gt * mx, g1, b1)
        return x1, x1 * (1.0 + sc) + sh

    x1, h2 = _rowwise("norm1_modulate", block1, [xs, mix], [gt_m, ln1_g, ln1_b, sc_f, sh_f], [(d, F32), (d, BF)], [], tm)
    gu = _matmul(h2, w_gu, "nn", "ffn_in")
    (act,) = _rowwise("swiglu", lambda gg, uu: _silu(gg) * uu, [(gu, ff_n, 0), (gu, ff_n, 1)], [], [(ff_n, BF)], [], min(256, t))
    ff = _matmul(act, w_down_f, "nn", "ffn_out")

    def tail_loss(x1_, ff_, gt, g2, b2, tg):
        y = _layernorm(DEEPNORM_ALPHA * x1_ + gt * ff_, g2, b2)
        return 0.5 * jnp.sum(jnp.mean(jnp.square(y - tg), axis=-1))

    def tail(x1_, ff_, tg, gt, g2, b2):
        loss, (dx1, dff, dgt, dg2, db2) = jax.value_and_grad(tail_loss, argnums=(0, 1, 2, 3, 4))(x1_, ff_, gt, g2, b2, tg)
        return dx1, dff, jnp.full((1, LANE), loss, F32), dgt, dg2, db2

    dx1_a, dff, loss_acc, d_gt_f, d_ln2_g, d_ln2_b = _rowwise(
        "norm2_loss", tail, [x1, ff, tgt], [gt_f, ln2_g, ln2_b], [(d, F32), (d, BF)], [(1, LANE), (1, d), (1, d), (1, d)], tm)

    dact = _matmul(dff, w_down_f, "nt", "d_ffn_act")
    g_w_down = _matmul(act, dff, "tn", "d_w_down", BF)

    def swiglu_bwd(gg, uu, da):
        _, vjp = jax.vjp(lambda a, b: _silu(a) * b, gg, uu)
        dg, du = vjp(da)
        return jnp.concatenate([dg, du], axis=1)

    (dgu,) = _rowwise("swiglu_bwd", swiglu_bwd, [(gu, ff_n, 0), (gu, ff_n, 1), dact], [], [(2 * ff_n, BF)], [], min(256, t))
    dh2 = _matmul(dgu, w_gu, "nt", "d_ffn_in")
    g_w_gu = _matmul(h2, dgu, "tn", "d_w_gate_up", BF)

    def block1_bwd(xx, mx, dx1_, dh2_, gt, g1, b1, sc, sh):
        _, vjp = jax.vjp(block1, xx, mx, gt, g1, b1, sc, sh)
        dxx, dmx, dgt, dg1, db1, dsc, dsh = vjp((dx1_, dh2_))
        return dxx, dmx, dgt, dg1, db1, dsc, dsh

    dx_a, dmix, d_gt_m, d_ln1_g, d_ln1_b, d_sc_f, d_sh_f = _rowwise(
        "norm1_modulate_bwd", block1_bwd, [xs, mix, dx1_a, dh2], [gt_m, ln1_g, ln1_b, sc_f, sh_f],
        [(d, F32), (d, BF)], [(1, d)] * 5, min(256, t))

    dmixin = _matmul(dmix, w_o_f, "nt", "d_mixer_out")
    g_w_o = _matmul(mixin, dmix, "tn", "d_w_o", BF)

    def mixer_bwd(o, z, om, dmi, g):
        _, vjp = jax.vjp(lambda o_, z_, g_: jnp.concatenate(_gdn_out(o_, z_, g_), axis=1), o, z, g)
        do_, dz_, dg_ = vjp(dmi[:, :DN_VW])
        dom = dmi[:, DN_VW:]
        delta = [jnp.broadcast_to(jnp.sum(dom[:, h * V_HEAD:(h + 1) * V_HEAD] * om[:, h * V_HEAD:(h + 1) * V_HEAD], axis=-1, keepdims=True), (o.shape[0], LANE))
                 for h in range(MLA_HEADS)]
        return do_, dz_, dom, jnp.concatenate(delta, axis=1), dg_

    do_dn, dz, do_mla, delta, d_dn_g = _rowwise(
        "mixer_out_bwd", mixer_bwd, [o_dn, (proj, DN_VW, P_Z // DN_VW), o_mla, dmixin], [dn_norm_g],
        [(DN_VW, F32), (DN_VW, BF), (MLA_VW, BF), (MLA_HEADS * LANE, F32)], [(1, DN_DV)], tm)

    dqc = _attn_bwd_dq(qc, kc, vc, do_mla, lse, delta, tq, tk)
    dkc, dvc = _attn_bwd_dkv(qc, kc, vc, do_mla, lse, delta, tq, tk)
    dcq, dckv, dkr, d_q_g, d_kv_g, g_w_uq_p, g_w_ukv = _mla_prep_bwd(
        proj, pos_col, inv_freq2, q_norm_g, kv_norm_g, w_uq_p, w_ukv_f, dqc, dkc, dvc, min(256, t))

    dqkv_act, dba, d_al8, d_dt8 = _gdn_bwd(qkv, proj, al8, dt8, states, do_dn, 1)
    dqkv_pre, d_conv8 = _conv_bwd(proj, conv_w8, dqkv_act, min(256, t))

    dproj = jnp.concatenate([dqkv_pre, dz, dcq, dckv, dba, dkr], axis=1)
    dh1 = _matmul(dproj, w_in_p, "nt", "d_in_proj")
    g_w_in_p = _matmul(h1, dproj, "tn", "d_w_in", BF)

    def modulate_bwd(xx, dh, dxa, sc):
        return dh * (1.0 + sc) + dxa, jnp.sum(dh * xx, axis=0, keepdims=True), jnp.sum(dh, axis=0, keepdims=True)

    grad_x, d_sc_m, d_sh_m = _rowwise("modulate_in_bwd", modulate_bwd, [xs, dh1, dx_a], [sc_m], [(d, F32)], [(1, d), (1, d)], tm)
    dmod = jnp.concatenate([d_sh_m, d_sc_m, d_gt_m, d_sh_f, d_sc_f, d_gt_f], axis=1)
    return (grad_x, loss_acc, dmod, d_conv8, d_al8, d_dt8, d_dn_g, d_q_g, d_kv_g, d_ln1_g, d_ln1_b, d_ln2_g, d_ln2_b,
            g_w_in_p, g_w_uq_p, g_w_ukv, g_w_o, g_w_gu, g_w_down)
```

```python
import functools
import math

import jax
import jax.numpy as jnp
from jax import lax
from jax.experimental import pallas as pl
from jax.experimental.pallas import tpu as pltpu

F32 = jnp.float32
BF = jnp.bfloat16
HI = lax.Precision.HIGHEST

N_DEV = 8
DN_HEADS = 4
DN_DK = 128
DN_DV = 128
CONV_K = 4
CHUNK = 64
MLA_HEADS = 4
QK_NOPE = 128
QK_ROPE = 64
V_HEAD = 128
Q_LORA = 512
KV_LORA = 256
ROPE_THETA = 10000.0
DEPTH = 1
DEEPNORM_ALPHA = (2.0 * DEPTH) ** 0.25
LANE = 128
CONV_HALO = 8

DN_QK = DN_HEADS * DN_DK
DN_VW = DN_HEADS * DN_DV
DN_CONV_CH = 2 * DN_QK + DN_VW
MLA_VW = MLA_HEADS * V_HEAD
MLA_QCAT = QK_NOPE + LANE
N_IN = DN_CONV_CH + DN_VW + 2 * DN_HEADS + Q_LORA + KV_LORA + QK_ROPE
P_QKV = 0
P_Z = DN_CONV_CH
P_CQ = P_Z + DN_VW
P_CKV = P_CQ + Q_LORA
P_BA = P_CKV + KV_LORA
P_KR = P_BA + LANE
N_INP = P_KR + LANE

ADAM_LR = 0.001
ADAM_B1 = 0.9
ADAM_B2 = 0.999
ADAM_EPS = 1e-08
ADAM_WD = 0.01
ADAM_STEP = 10

NN = (((1,), (0,)), ((), ()))
NT = (((1,), (1,)), ((), ()))
TN = (((0,), (0,)), ((), ()))


def _pick(n, prefs):
    for p in prefs:
        if n % p == 0:
            return p
    return n


def _full(shape):
    return pl.BlockSpec(shape, lambda *_: (0,) * len(shape))


def _dot(a, b, dims=NN):
    return lax.dot_general(a, b, dims, preferred_element_type=F32)


def _doth(a, b, dims=NN):
    return lax.dot_general(a, b, dims, precision=HI, preferred_element_type=F32)


@jax.custom_vjp
def _mmb(a, b):
    return _dot(a.astype(BF), b.astype(BF), NN)


def _mmb_fwd(a, b):
    return _mmb(a, b), (a, b)


def _mmb_bwd(res, g):
    a, b = res
    gb = g.astype(BF)
    return (_dot(gb, b.astype(BF), NT).astype(a.dtype), _dot(a.astype(BF), gb, TN).astype(b.dtype))


_mmb.defvjp(_mmb_fwd, _mmb_bwd)


@jax.custom_vjp
def _mmb_nt(a, b):
    return _dot(a.astype(BF), b.astype(BF), NT)


def _mmb_nt_fwd(a, b):
    return _mmb_nt(a, b), (a, b)


def _mmb_nt_bwd(res, g):
    a, b = res
    gb = g.astype(BF)
    return (_dot(gb, b.astype(BF), NN).astype(a.dtype), _dot(gb, a.astype(BF), TN).astype(b.dtype))


_mmb_nt.defvjp(_mmb_nt_fwd, _mmb_nt_bwd)


@jax.custom_vjp
def _mmb_tn(a, b):
    return _dot(a.astype(BF), b.astype(BF), TN)


def _mmb_tn_fwd(a, b):
    return _mmb_tn(a, b), (a, b)


def _mmb_tn_bwd(res, g):
    a, b = res
    gb = g.astype(BF)
    return (_dot(b.astype(BF), gb, NT).astype(a.dtype), _dot(a.astype(BF), gb, NN).astype(b.dtype))


_mmb_tn.defvjp(_mmb_tn_fwd, _mmb_tn_bwd)


def _sigmoid(x):
    return 0.5 * (jnp.tanh(0.5 * x) + 1.0)


def _silu(x):
    return x * _sigmoid(x)


def _softplus(x):
    return jnp.maximum(x, 0.0) + jnp.log(1.0 + jnp.exp(-jnp.abs(x)))


def _layernorm(x, g, b, eps=1e-5):
    mu = jnp.mean(x, axis=-1, keepdims=True)
    xc = x - mu
    var = jnp.mean(xc * xc, axis=-1, keepdims=True)
    return xc * lax.rsqrt(var + eps) * g + b


def _rmsnorm(x, g, eps=1e-6):
    return x * lax.rsqrt(jnp.mean(x * x, axis=-1, keepdims=True) + eps) * g


def _l2norm(x, eps=1e-6):
    return x * lax.rsqrt(jnp.sum(x * x, axis=-1, keepdims=True) + eps)


def _rowwise(name, fn, rows, vecs, out_rows, out_accs, tm):
    rows = [r if isinstance(r, tuple) else (r, r.shape[1], 0) for r in rows]
    t = rows[0][0].shape[0]
    tm = min(tm, t)
    assert t % tm == 0
    nr, nv, no = len(rows), len(vecs), len(out_rows)

    def body(*refs):
        ins = [r[...] for r in refs[:nr + nv]]
        outs = fn(*ins)
        outs = outs if isinstance(outs, (tuple, list)) else (outs,)
        o_rows = refs[nr + nv:nr + nv + no]
        o_accs = refs[nr + nv + no:]
        for o, val in zip(o_rows, outs[:no]):
            o[...] = val.astype(o.dtype)
        if o_accs:
            @pl.when(pl.program_id(0) == 0)
            def _():
                for o in o_accs:
                    o[...] = jnp.zeros_like(o)
            for o, val in zip(o_accs, outs[no:]):
                o[...] += val

    in_specs = [pl.BlockSpec((tm, w), functools.partial(lambda i, j: (i, j), j=j)) for (_, w, j) in rows]
    in_specs += [_full(v.shape) for v in vecs]
    out_specs = [pl.BlockSpec((tm, w), lambda i: (i, 0)) for (w, _) in out_rows]
    out_specs += [_full(s) for s in out_accs]
    out_shape = [jax.ShapeDtypeStruct((t, w), d) for (w, d) in out_rows]
    out_shape += [jax.ShapeDtypeStruct(s, F32) for s in out_accs]
    res = pl.pallas_call(
        body, grid=(t // tm,), in_specs=in_specs, out_specs=out_specs, out_shape=out_shape, name=name,
        compiler_params=pltpu.CompilerParams(dimension_semantics=("arbitrary",)),
    )(*[r[0] for r in rows], *vecs)
    return res


def _matmul(a, b, mode, name, out_dtype=F32):
    if mode == "nn":
        (m, k), n = a.shape, b.shape[1]
    elif mode == "nt":
        (m, k), n = a.shape, b.shape[0]
    else:
        (k, m), n = a.shape, b.shape[1]
    tm = _pick(m, (1024, 512, 256, 128))
    tn = _pick(n, (1024, 512, 256, 128))
    tk = _pick(k, (512, 256, 128))
    nk = k // tk
    dims = {"nn": NN, "nt": NT, "tn": TN}[mode]

    def body(a_ref, b_ref, o_ref, acc_ref):
        kk = pl.program_id(2)

        @pl.when(kk == 0)
        def _():
            acc_ref[...] = jnp.zeros_like(acc_ref)

        acc_ref[...] += _dot(a_ref[...].astype(BF), b_ref[...].astype(BF), dims)

        @pl.when(kk == nk - 1)
        def _():
            o_ref[...] = acc_ref[...].astype(o_ref.dtype)

    a_spec = pl.BlockSpec((tk, tm), lambda i, j, kk: (kk, i)) if mode == "tn" else pl.BlockSpec((tm, tk), lambda i, j, kk: (i, kk))
    b_spec = pl.BlockSpec((tn, tk), lambda i, j, kk: (j, kk)) if mode == "nt" else pl.BlockSpec((tk, tn), lambda i, j, kk: (kk, j))
    return pl.pallas_call(
        body, grid=(m // tm, n // tn, nk), in_specs=[a_spec, b_spec],
        out_specs=pl.BlockSpec((tm, tn), lambda i, j, kk: (i, j)),
        out_shape=jax.ShapeDtypeStruct((m, n), out_dtype),
        scratch_shapes=[pltpu.VMEM((tm, tn), F32)], name=name,
        compiler_params=pltpu.CompilerParams(dimension_semantics=("parallel", "parallel", "arbitrary")),
    )(a, b)


def _exchange(x, name, scatter):
    r, c = x.shape[-2:]

    def body(x_ref, o_ref, send_sems, recv_sems, local_sem):
        mx, my, mc = lax.axis_index("x"), lax.axis_index("y"), lax.axis_index("c")
        me = 4 * mx + 2 * my + mc
        src_me = x_ref.at[me] if scatter else x_ref
        mine = pltpu.make_async_copy(src_me, o_ref.at[me], local_sem)
        mine.start()
        copies = []
        for k in range(1, N_DEV):
            px, py, pc = mx ^ (k >> 2), my ^ ((k >> 1) & 1), mc ^ (k & 1)
            peer = 4 * px + 2 * py + pc
            cp = pltpu.make_async_remote_copy(
                src_ref=x_ref.at[peer] if scatter else x_ref, dst_ref=o_ref.at[me],
                send_sem=send_sems.at[k - 1], recv_sem=recv_sems.at[k - 1],
                device_id=(px, py, pc), device_id_type=pl.DeviceIdType.MESH)
            cp.start()
            copies.append((cp, peer))
        for k, (cp, peer) in enumerate(copies):
            pltpu.make_async_remote_copy(
                src_ref=src_me, dst_ref=o_ref.at[peer], send_sem=send_sems.at[k], recv_sem=recv_sems.at[k],
                device_id=(mx, my, mc), device_id_type=pl.DeviceIdType.MESH).wait_recv()
        for cp, _ in copies:
            cp.wait_send()
        mine.wait()

    return pl.pallas_call(
        body, out_shape=jax.ShapeDtypeStruct((N_DEV, r, c), x.dtype),
        in_specs=[pl.BlockSpec(memory_space=pl.ANY)], out_specs=pl.BlockSpec(memory_space=pl.ANY),
        scratch_shapes=[pltpu.SemaphoreType.DMA((N_DEV - 1,)), pltpu.SemaphoreType.DMA((N_DEV - 1,)),
                        pltpu.SemaphoreType.DMA],
        name=name,
    )(x)


def _sum_slots(x, name):
    _, r, c = x.shape
    tr = _pick(r, (512, 256, 128, 64, 32, 16))

    def body(x_ref, o_ref):
        acc = x_ref[0].astype(F32)
        for s in range(1, N_DEV):
            acc = acc + x_ref[s].astype(F32)
        o_ref[...] = acc

    return pl.pallas_call(
        body, grid=(r // tr,), in_specs=[pl.BlockSpec((N_DEV, tr, c), lambda i: (0, i, 0))],
        out_specs=pl.BlockSpec((tr, c), lambda i: (i, 0)), out_shape=jax.ShapeDtypeStruct((r, c), F32), name=name,
        compiler_params=pltpu.CompilerParams(dimension_semantics=("arbitrary",)),
    )(x)


def _mod_fwd(c_all, w_ada, b_ada_mine):
    def body(c_ref, w_ref, b_ref, o_ref):
        o_ref[...] = _doth(_silu(c_ref[...]), w_ref[...]) + b_ref[...]

    return pl.pallas_call(body, out_shape=jax.ShapeDtypeStruct((c_all.shape[0], w_ada.shape[1]), F32), name="mod_fwd")(c_all, w_ada, b_ada_mine)


def _mod_bwd(c_all, dmod_mine):
    def body(c_ref, d_ref, o_ref):
        o_ref[...] = _doth(_silu(c_ref[...]), d_ref[...], TN)

    return pl.pallas_call(body, out_shape=jax.ShapeDtypeStruct((c_all.shape[1], dmod_mine.shape[1]), F32), name="mod_bwd")(c_all, dmod_mine)


def _conv_fwd(proj, conv_w8, tm):
    t = proj.shape[0]
    ch = DN_CONV_CH

    def body(x_ref, w_ref, o_ref, buf):
        @pl.when(pl.program_id(0) == 0)
        def _():
            buf[pl.ds(0, CONV_HALO), :] = jnp.zeros((CONV_HALO, ch), F32)

        buf[pl.ds(CONV_HALO, tm), :] = x_ref[...]
        acc = jnp.zeros((tm, ch), F32)
        for j in range(CONV_K):
            acc = acc + buf[pl.ds(CONV_HALO - (CONV_K - 1) + j, tm), :] * w_ref[pl.ds(j, 1), :]
        o_ref[...] = _silu(acc)
        buf[pl.ds(0, CONV_HALO), :] = buf[pl.ds(tm, CONV_HALO), :]

    return pl.pallas_call(
        body, grid=(t // tm,), in_specs=[pl.BlockSpec((tm, ch), lambda i: (i, 0)), _full(conv_w8.shape)],
        out_specs=pl.BlockSpec((tm, ch), lambda i: (i, 0)), out_shape=jax.ShapeDtypeStruct((t, ch), F32),
        scratch_shapes=[pltpu.VMEM((tm + CONV_HALO, ch), F32)], name="conv_fwd",
        compiler_params=pltpu.CompilerParams(dimension_semantics=("arbitrary",)),
    )(proj, conv_w8)


def _conv_bwd(proj, conv_w8, dact, tm):
    t = proj.shape[0]
    ch = DN_CONV_CH
    nt = t // tm
    hb = tm // CONV_HALO

    def body(x_ref, xp_ref, w_ref, dy_ref, dx_ref, dw_ref, xbuf, dbuf):
        step = pl.program_id(0)

        @pl.when(step == 0)
        def _():
            dbuf[pl.ds(tm, CONV_HALO), :] = jnp.zeros((CONV_HALO, ch), F32)
            dw_ref[...] = jnp.zeros_like(dw_ref)

        first = step == nt - 1
        xbuf[pl.ds(0, CONV_HALO), :] = jnp.where(first, 0.0, xp_ref[...])
        xbuf[pl.ds(CONV_HALO, tm), :] = x_ref[...]
        pre = jnp.zeros((tm, ch), F32)
        for j in range(CONV_K):
            pre = pre + xbuf[pl.ds(CONV_HALO - (CONV_K - 1) + j, tm), :] * w_ref[pl.ds(j, 1), :]
        sg = _sigmoid(pre)
        dpre = dy_ref[...] * (sg * (1.0 + pre * (1.0 - sg)))
        dbuf[pl.ds(0, tm), :] = dpre
        dx = jnp.zeros((tm, ch), F32)
        for j in range(CONV_K):
            dx = dx + dbuf[pl.ds(CONV_K - 1 - j, tm), :] * w_ref[pl.ds(j, 1), :]
            dw_ref[pl.ds(j, 1), :] += jnp.sum(dpre * xbuf[pl.ds(CONV_HALO - (CONV_K - 1) + j, tm), :], axis=0, keepdims=True)
        dx_ref[...] = dx.astype(dx_ref.dtype)
        dbuf[pl.ds(tm, CONV_HALO), :] = dbuf[pl.ds(0, CONV_HALO), :]

    rev = lambda i: (nt - 1 - i, 0)
    prev = lambda i: (jnp.maximum((nt - 1 - i) * hb - 1, 0), 0)
    return pl.pallas_call(
        body, grid=(nt,),
        in_specs=[pl.BlockSpec((tm, ch), rev), pl.BlockSpec((CONV_HALO, ch), prev), _full(conv_w8.shape),
                  pl.BlockSpec((tm, ch), rev)],
        out_specs=[pl.BlockSpec((tm, ch), rev), _full(conv_w8.shape)],
        out_shape=[jax.ShapeDtypeStruct((t, ch), BF), jax.ShapeDtypeStruct(conv_w8.shape, F32)],
        scratch_shapes=[pltpu.VMEM((tm + CONV_HALO, ch), F32), pltpu.VMEM((tm + CONV_HALO, ch), F32)], name="conv_bwd",
        compiler_params=pltpu.CompilerParams(dimension_semantics=("arbitrary",)),
    )(proj, proj, conv_w8, dact)


BNN = (((2,), (1,)), ((0,), (0,)))
BNT = (((2,), (2,)), ((0,), (0,)))
BTN = (((1,), (1,)), ((0,), (0,)))


def _bdot(a, b, dims, precision=None):
    return lax.dot_general(a, b, dims, precision=precision, preferred_element_type=F32)


@jax.custom_vjp
def _bmmb_nt(a, b):
    return _bdot(a.astype(BF), b.astype(BF), BNT)


def _bmmb_nt_fwd(a, b):
    return _bmmb_nt(a, b), (a, b)


def _bmmb_nt_bwd(res, g):
    a, b = res
    gb = g.astype(BF)
    return _bdot(gb, b.astype(BF), BNN), _bdot(gb, a.astype(BF), BTN)


_bmmb_nt.defvjp(_bmmb_nt_fwd, _bmmb_nt_bwd)


def _gdn_intra(qkv, ba, al8, dt8):
    tm = qkv.shape[0]
    nb = tm // CHUNK
    bsz = DN_HEADS * nb

    def heads(x0):
        return jnp.concatenate([qkv[:, x0 + h * LANE:x0 + (h + 1) * LANE].reshape(nb, CHUNK, LANE) for h in range(DN_HEADS)], axis=0)

    def spread(c0):
        return jnp.concatenate([jnp.broadcast_to(ba[:, c0 + h:c0 + h + 1], (tm, LANE)).reshape(nb, CHUNK, LANE)
                                for h in range(DN_HEADS)], axis=0)

    def per_head(v8):
        return jnp.concatenate([jnp.broadcast_to(v8[0:1, h:h + 1].reshape(1, 1, 1), (nb, 1, LANE)) for h in range(DN_HEADS)], axis=0)

    ri = lax.broadcasted_iota(jnp.int32, (bsz, CHUNK, CHUNK), 1)
    ci = lax.broadcasted_iota(jnp.int32, (bsz, CHUNK, CHUNK), 2)
    incl = ri >= ci
    strict = ri > ci

    q = _l2norm(heads(0)) * (DN_DK ** -0.5)
    k = _l2norm(heads(DN_QK))
    va = heads(2 * DN_QK)
    beta = _sigmoid(spread(0))
    g = -jnp.exp(per_head(al8)) * _softplus(spread(DN_HEADS) + per_head(dt8))
    gc = _bdot(incl.astype(F32), g, BNN, HI)
    g_last = jnp.sum(g, axis=1, keepdims=True)
    gcol = gc[:, :, :CHUNK]
    diff = gcol - jnp.swapaxes(gcol, 1, 2)
    decay = jnp.where(incl, jnp.exp(jnp.where(incl, diff, 0.0)), 0.0)
    kb = k * beta
    xm = -jnp.where(strict, _bmmb_nt(kb, k) * decay, 0.0)
    inv = (ri == ci).astype(F32) + xm
    for _ in range(int(math.log2(CHUNK)) - 1):
        xm = _bdot(xm, xm, BNN, lax.Precision.HIGH)
        inv = inv + _bdot(inv, xm, BNN, lax.Precision.HIGH)
    egc = jnp.exp(gc)
    wu = _bdot(inv, jnp.concatenate([kb * egc, va * beta], axis=2), BNN, lax.Precision.HIGH)
    attn = jnp.where(incl, _bmmb_nt(q, k) * decay, 0.0)

    def unheads(x):
        return jnp.concatenate([x[h * nb:(h + 1) * nb].reshape(tm, LANE) for h in range(DN_HEADS)], axis=1)

    return (unheads(wu[:, :, :DN_DK]), unheads(wu[:, :, DN_DK:]), unheads(q * egc), unheads(k * jnp.exp(g_last - gc)),
            attn.reshape(DN_HEADS, tm, CHUNK), unheads(jnp.broadcast_to(g_last, (bsz, CHUNK, LANE))))


def _gdn_scan_step(w, u, qg, kd, att, gl, s):
    v_new = u - _mmb(w, s)
    o = _mmb(qg, s) + _mmb(att, v_new)
    return o, s * jnp.exp(gl) + _mmb_tn(kd, v_new)


def _gdn_intra_specs(t, tm, dts):
    nb = tm // CHUNK
    specs = [pl.BlockSpec((tm, DN_VW), lambda i: (i, 0))] * 4
    specs += [pl.BlockSpec((DN_HEADS, tm, CHUNK), lambda i: (0, i, 0)), pl.BlockSpec((tm, DN_VW), lambda i: (i, 0))]
    shapes = [jax.ShapeDtypeStruct((t, DN_VW), dts[i]) for i in range(4)]
    shapes += [jax.ShapeDtypeStruct((DN_HEADS, t, CHUNK), dts[4]), jax.ShapeDtypeStruct((t, DN_VW), dts[5])]
    return specs, shapes


def _gdn_intra_fwd(qkv, proj, al8, dt8, tm):
    t = qkv.shape[0]

    def body(qkv_ref, ba_ref, al_ref, dt_ref, *outs):
        for o, val in zip(outs, _gdn_intra(qkv_ref[...], ba_ref[...], al_ref[...], dt_ref[...])):
            o[...] = val.astype(o.dtype)

    specs, shapes = _gdn_intra_specs(t, tm, (BF, F32, BF, BF, BF, F32))
    return pl.pallas_call(
        body, grid=(t // tm,),
        in_specs=[pl.BlockSpec((tm, DN_CONV_CH), lambda i: (i, 0)), pl.BlockSpec((tm, LANE), lambda i: (i, P_BA // LANE)),
                  _full(al8.shape), _full(dt8.shape)],
        out_specs=specs, out_shape=shapes, name="gdn_intra_fwd",
        compiler_params=pltpu.CompilerParams(dimension_semantics=("parallel",)),
    )(qkv, proj, al8, dt8)


def _gdn_intra_bwd(qkv, proj, al8, dt8, cts, tm):
    t = qkv.shape[0]

    def body(qkv_ref, ba_ref, al_ref, dt_ref, *refs):
        ct_refs, (dqkv_ref, dba_ref, dal_ref, ddt_ref) = refs[:6], refs[6:]

        @pl.when(pl.program_id(0) == 0)
        def _():
            dal_ref[...] = jnp.zeros_like(dal_ref)
            ddt_ref[...] = jnp.zeros_like(ddt_ref)

        _, vjp = jax.vjp(_gdn_intra, qkv_ref[...], ba_ref[...], al_ref[...], dt_ref[...])
        dqkv, dba, dal, ddt = vjp(tuple(r[...] for r in ct_refs))
        dqkv_ref[...] = dqkv
        dba_ref[...] = dba.astype(dba_ref.dtype)
        dal_ref[...] += dal
        ddt_ref[...] += ddt

    specs, _ = _gdn_intra_specs(t, tm, (F32,) * 6)
    return pl.pallas_call(
        body, grid=(t // tm,),
        in_specs=[pl.BlockSpec((tm, DN_CONV_CH), lambda i: (i, 0)), pl.BlockSpec((tm, LANE), lambda i: (i, P_BA // LANE)),
                  _full(al8.shape), _full(dt8.shape)] + specs,
        out_specs=[pl.BlockSpec((tm, DN_CONV_CH), lambda i: (i, 0)), pl.BlockSpec((tm, LANE), lambda i: (i, 0)),
                   _full(al8.shape), _full(dt8.shape)],
        out_shape=[jax.ShapeDtypeStruct((t, DN_CONV_CH), F32), jax.ShapeDtypeStruct((t, LANE), BF),
                   jax.ShapeDtypeStruct(al8.shape, F32), jax.ShapeDtypeStruct(dt8.shape, F32)],
        name="gdn_intra_bwd", compiler_params=pltpu.CompilerParams(dimension_semantics=("arbitrary",)),
    )(qkv, proj, al8, dt8, *cts)


def _gdn_scan_fwd(intra, tm):
    t = intra[0].shape[0]
    nb = tm // CHUNK
    nc = t // CHUNK

    def body(w_ref, u_ref, qg_ref, kd_ref, att_ref, gl_ref, o_ref, ss_ref, s_scr):
        @pl.when(pl.program_id(0) == 0)
        def _():
            s_scr[...] = jnp.zeros_like(s_scr)

        for cc in range(nb):
            rows = pl.ds(cc * CHUNK, CHUNK)
            for h in range(DN_HEADS):
                cols = pl.ds(h * DN_DV, DN_DV)
                s_prev = s_scr[h]
                ss_ref[cc, h] = s_prev
                o, s_new = _gdn_scan_step(w_ref[rows, cols], u_ref[rows, cols], qg_ref[rows, cols], kd_ref[rows, cols],
                                          att_ref[h, rows, :], gl_ref[pl.ds(cc * CHUNK, 1), cols], s_prev)
                o_ref[rows, cols] = o
                s_scr[h] = s_new

    specs, _ = _gdn_intra_specs(t, tm, (F32,) * 6)
    return pl.pallas_call(
        body, grid=(t // tm,), in_specs=specs,
        out_specs=[pl.BlockSpec((tm, DN_VW), lambda i: (i, 0)),
                   pl.BlockSpec((nb, DN_HEADS, DN_DK, DN_DV), lambda i: (i, 0, 0, 0))],
        out_shape=[jax.ShapeDtypeStruct((t, DN_VW), F32), jax.ShapeDtypeStruct((nc, DN_HEADS, DN_DK, DN_DV), F32)],
        scratch_shapes=[pltpu.VMEM((DN_HEADS, DN_DK, DN_DV), F32)], name="gdn_scan_fwd",
        compiler_params=pltpu.CompilerParams(dimension_semantics=("arbitrary",)),
    )(*intra)


def _gdn_scan_bwd(intra, states, do, tm):
    t = intra[0].shape[0]
    nb = tm // CHUNK
    ng = t // tm

    def body(w_ref, u_ref, qg_ref, kd_ref, att_ref, gl_ref, ss_ref, do_ref,
             dw_ref, du_ref, dqg_ref, dkd_ref, datt_ref, dgl_ref, ds_scr):
        @pl.when(pl.program_id(0) == 0)
        def _():
            ds_scr[...] = jnp.zeros_like(ds_scr)

        for cc in reversed(range(nb)):
            rows = pl.ds(cc * CHUNK, CHUNK)
            for h in range(DN_HEADS):
                cols = pl.ds(h * DN_DV, DN_DV)
                f32 = lambda r: r[rows, cols].astype(F32)
                _, vjp = jax.vjp(_gdn_scan_step, f32(w_ref), u_ref[rows, cols], f32(qg_ref), f32(kd_ref),
                                 att_ref[h, rows, :].astype(F32), gl_ref[pl.ds(cc * CHUNK, 1), cols], ss_ref[cc, h])
                dw, du, dqg, dkd, datt, dgl, ds_prev = vjp((do_ref[rows, cols], ds_scr[h]))
                dw_ref[rows, cols] = dw
                du_ref[rows, cols] = du
                dqg_ref[rows, cols] = dqg
                dkd_ref[rows, cols] = dkd
                datt_ref[h, rows, :] = datt
                first_row = lax.broadcasted_iota(jnp.int32, (CHUNK, DN_DV), 0) == 0
                dgl_ref[rows, cols] = jnp.where(first_row, dgl, 0.0)
                ds_scr[h] = ds_prev

    rev = lambda i: (ng - 1 - i, 0)
    rev3 = lambda i: (0, ng - 1 - i, 0)
    row = pl.BlockSpec((tm, DN_VW), rev)
    six = [row] * 4 + [pl.BlockSpec((DN_HEADS, tm, CHUNK), rev3), row]
    _, shapes = _gdn_intra_specs(t, tm, (F32,) * 6)
    return pl.pallas_call(
        body, grid=(ng,),
        in_specs=six + [pl.BlockSpec((nb, DN_HEADS, DN_DK, DN_DV), lambda i: (ng - 1 - i, 0, 0, 0)), row],
        out_specs=six, out_shape=shapes,
        scratch_shapes=[pltpu.VMEM((DN_HEADS, DN_DK, DN_DV), F32)], name="gdn_scan_bwd",
        compiler_params=pltpu.CompilerParams(dimension_semantics=("arbitrary",)),
    )(*intra, states, do)


def _gdn_out(o, z, g):
    parts = []
    for h in range(DN_HEADS):
        sl = slice(h * DN_DV, (h + 1) * DN_DV)
        parts.append(_rmsnorm(o[:, sl], g) * _silu(z[:, sl]))
    return parts


def _rope_tables(pos, inv_freq2):
    lane = lax.broadcasted_iota(jnp.int32, (1, LANE), 1)
    ang = pos * inv_freq2
    cos = jnp.where(lane < QK_ROPE, jnp.cos(ang), 0.0)
    sin = jnp.where(lane < QK_ROPE // 2, -jnp.sin(ang), jnp.where(lane < QK_ROPE, jnp.sin(ang), 0.0))
    return cos, sin


def _rope_swap():
    ri = lax.broadcasted_iota(jnp.int32, (LANE, LANE), 0)
    ci = lax.broadcasted_iota(jnp.int32, (LANE, LANE), 1)
    half = QK_ROPE // 2
    return (((ci < half) & (ri == ci + half)) | ((ci >= half) & (ci < QK_ROPE) & (ri == ci - half))).astype(F32)


def _mla_prep(cq, ckv, kr, gq, gkv, w_uq, w_ukv, cos, sin, swap):
    rope = lambda u: u * cos + _doth(u, swap) * sin
    q_lin = _mmb(_rmsnorm(cq, gq), w_uq)
    kv_lin = _mmb(_rmsnorm(ckv, gkv), w_ukv)
    k_rope = rope(kr)
    qs, ks, vs = [], [], []
    for h in range(MLA_HEADS):
        qs += [q_lin[:, h * LANE:(h + 1) * LANE], rope(q_lin[:, (MLA_HEADS + h) * LANE:(MLA_HEADS + h + 1) * LANE])]
        ks += [kv_lin[:, 2 * h * LANE:(2 * h + 1) * LANE], k_rope]
        vs += [kv_lin[:, (2 * h + 1) * LANE:(2 * h + 2) * LANE]]
    return qs + ks + vs


def _mla_prep_fwd(proj, pos_col, inv_freq2, gq, gkv, w_uq, w_ukv, tm):
    t = proj.shape[0]
    nq = 2 * MLA_HEADS

    def body(cq_ref, ckv_ref, kr_ref, pos_ref, f_ref, gq_ref, gkv_ref, wq_ref, wkv_ref, q_ref, k_ref, v_ref):
        cos, sin = _rope_tables(pos_ref[...], f_ref[...])
        outs = _mla_prep(cq_ref[...], ckv_ref[...], kr_ref[...], gq_ref[...], gkv_ref[...], wq_ref[...], wkv_ref[...],
                         cos, sin, _rope_swap())
        for i in range(nq):
            q_ref[:, pl.ds(i * LANE, LANE)] = outs[i].astype(q_ref.dtype)
            k_ref[:, pl.ds(i * LANE, LANE)] = outs[nq + i].astype(k_ref.dtype)
        for h in range(MLA_HEADS):
            v_ref[:, pl.ds(h * LANE, LANE)] = outs[2 * nq + h].astype(v_ref.dtype)

    row = lambda w, j: pl.BlockSpec((tm, w), functools.partial(lambda i, j: (i, j), j=j))
    return pl.pallas_call(
        body, grid=(t // tm,),
        in_specs=[row(Q_LORA, P_CQ // Q_LORA), row(KV_LORA, P_CKV // KV_LORA), row(LANE, P_KR // LANE),
                  pl.BlockSpec((tm, 1), lambda i: (i, 0)), _full(inv_freq2.shape), _full(gq.shape), _full(gkv.shape),
                  _full(w_uq.shape), _full(w_ukv.shape)],
        out_specs=[row(nq * LANE, 0), row(nq * LANE, 0), row(MLA_VW, 0)],
        out_shape=[jax.ShapeDtypeStruct((t, nq * LANE), BF), jax.ShapeDtypeStruct((t, nq * LANE), BF),
                   jax.ShapeDtypeStruct((t, MLA_VW), BF)],
        name="mla_prep_fwd", compiler_params=pltpu.CompilerParams(dimension_semantics=("arbitrary",)),
    )(proj, proj, proj, pos_col, inv_freq2, gq, gkv, w_uq, w_ukv)


def _mla_prep_bwd(proj, pos_col, inv_freq2, gq, gkv, w_uq, w_ukv, dq, dk, dv, tm):
    t = proj.shape[0]
    nq = 2 * MLA_HEADS

    def body(cq_ref, ckv_ref, kr_ref, pos_ref, f_ref, gq_ref, gkv_ref, wq_ref, wkv_ref, dq_ref, dk_ref, dv_ref,
             dcq_ref, dckv_ref, dkr_ref, dgq_ref, dgkv_ref, dwq_ref, dwkv_ref):
        @pl.when(pl.program_id(0) == 0)
        def _():
            for o in (dgq_ref, dgkv_ref, dwq_ref, dwkv_ref):
                o[...] = jnp.zeros_like(o)

        cos, sin = _rope_tables(pos_ref[...], f_ref[...])
        f = functools.partial(_mla_prep, cos=cos, sin=sin, swap=_rope_swap())
        _, vjp = jax.vjp(f, cq_ref[...], ckv_ref[...], kr_ref[...], gq_ref[...], gkv_ref[...], wq_ref[...], wkv_ref[...])
        cts = [dq_ref[:, pl.ds(i * LANE, LANE)] for i in range(nq)]
        cts += [dk_ref[:, pl.ds(i * LANE, LANE)] for i in range(nq)]
        cts += [dv_ref[:, pl.ds(h * LANE, LANE)] for h in range(MLA_HEADS)]
        dcq, dckv, dkr, dgq, dgkv, dwq, dwkv = vjp(cts)
        dcq_ref[...] = dcq.astype(dcq_ref.dtype)
        dckv_ref[...] = dckv.astype(dckv_ref.dtype)
        dkr_ref[...] = dkr.astype(dkr_ref.dtype)
        dgq_ref[...] += dgq
        dgkv_ref[...] += dgkv
        dwq_ref[...] += dwq
        dwkv_ref[...] += dwkv

    row = lambda w, j: pl.BlockSpec((tm, w), functools.partial(lambda i, j: (i, j), j=j))
    return pl.pallas_call(
        body, grid=(t // tm,),
        in_specs=[row(Q_LORA, P_CQ // Q_LORA), row(KV_LORA, P_CKV // KV_LORA), row(LANE, P_KR // LANE),
                  pl.BlockSpec((tm, 1), lambda i: (i, 0)), _full(inv_freq2.shape), _full(gq.shape), _full(gkv.shape),
                  _full(w_uq.shape), _full(w_ukv.shape), row(nq * LANE, 0), row(nq * LANE, 0), row(MLA_VW, 0)],
        out_specs=[row(Q_LORA, 0), row(KV_LORA, 0), row(LANE, 0), _full(gq.shape), _full(gkv.shape),
                   _full(w_uq.shape), _full(w_ukv.shape)],
        out_shape=[jax.ShapeDtypeStruct((t, Q_LORA), BF), jax.ShapeDtypeStruct((t, KV_LORA), BF),
                   jax.ShapeDtypeStruct((t, LANE), BF), jax.ShapeDtypeStruct(gq.shape, F32),
                   jax.ShapeDtypeStruct(gkv.shape, F32), jax.ShapeDtypeStruct(w_uq.shape, F32),
                   jax.ShapeDtypeStruct(w_ukv.shape, F32)],
        name="mla_prep_bwd", compiler_params=pltpu.CompilerParams(dimension_semantics=("arbitrary",)),
    )(proj, proj, proj, pos_col, inv_freq2, gq, gkv, w_uq, w_ukv, dq, dk, dv)


_ATT_SCALE = 1.0 / math.sqrt(QK_NOPE + QK_ROPE)
_NEG = -1e30


def _causal(tq, tk, q0, k0):
    row = q0 + lax.broadcasted_iota(jnp.int32, (tq, tk), 0)
    col = k0 + lax.broadcasted_iota(jnp.int32, (tq, tk), 1)
    return col <= row


def _attn_fwd(q, k, v, tq, tk):
    t = q.shape[0]

    def body(q_ref, k_ref, v_ref, o_ref, lse_ref):
        i = pl.program_id(1)
        qt = q_ref[...]
        nk = ((i + 1) * tq + tk - 1) // tk

        def step(j, carry):
            m, l, acc = carry
            k0 = pl.multiple_of(j * tk, tk)
            s = _dot(qt, k_ref[pl.ds(k0, tk), :], NT) * _ATT_SCALE
            s = jnp.where(_causal(tq, tk, i * tq, k0), s, _NEG)
            m_new = jnp.maximum(m, jnp.max(s, axis=-1, keepdims=True))
            p = jnp.exp(s - m_new)
            alpha = jnp.exp(m - m_new)
            l = alpha * l + jnp.sum(p, axis=-1, keepdims=True)
            acc = alpha * acc + _dot(p.astype(BF), v_ref[pl.ds(k0, tk), :])
            return m_new, l, acc

        m, l, acc = lax.fori_loop(0, nk, step, (jnp.full((tq, 1), _NEG, F32), jnp.zeros((tq, 1), F32),
                                                jnp.zeros((tq, V_HEAD), F32)))
        o_ref[...] = acc / l
        lse_ref[...] = jnp.broadcast_to(m + jnp.log(l), (tq, LANE))

    return pl.pallas_call(
        body, grid=(MLA_HEADS, t // tq),
        in_specs=[pl.BlockSpec((tq, 2 * LANE), lambda h, i: (i, h)), pl.BlockSpec((t, 2 * LANE), lambda h, i: (0, h)),
                  pl.BlockSpec((t, V_HEAD), lambda h, i: (0, h))],
        out_specs=[pl.BlockSpec((tq, V_HEAD), lambda h, i: (i, h)), pl.BlockSpec((tq, LANE), lambda h, i: (i, h))],
        out_shape=[jax.ShapeDtypeStruct((t, MLA_VW), F32), jax.ShapeDtypeStruct((t, MLA_HEADS * LANE), F32)],
        name="attn_fwd", compiler_params=pltpu.CompilerParams(dimension_semantics=("parallel", "arbitrary")),
    )(q, k, v)


def _attn_bwd_dq(q, k, v, do, lse, delta, tq, tk):
    t = q.shape[0]

    def body(q_ref, k_ref, v_ref, do_ref, lse_ref, dl_ref, dq_ref):
        i = pl.program_id(1)
        qt = q_ref[...]
        dot_ = do_ref[...]
        lse_c = lse_ref[:, pl.ds(0, 1)]
        dl_c = dl_ref[:, pl.ds(0, 1)]
        nk = ((i + 1) * tq + tk - 1) // tk

        def step(j, acc):
            k0 = pl.multiple_of(j * tk, tk)
            kt = k_ref[pl.ds(k0, tk), :]
            s = _dot(qt, kt, NT) * _ATT_SCALE
            p = jnp.where(_causal(tq, tk, i * tq, k0), jnp.exp(s - lse_c), 0.0)
            dp = _dot(dot_, v_ref[pl.ds(k0, tk), :], NT)
            ds = (p * (dp - dl_c) * _ATT_SCALE).astype(BF)
            return acc + _dot(ds, kt)

        dq_ref[...] = lax.fori_loop(0, nk, step, jnp.zeros((tq, 2 * LANE), F32))

    return pl.pallas_call(
        body, grid=(MLA_HEADS, t // tq),
        in_specs=[pl.BlockSpec((tq, 2 * LANE), lambda h, i: (i, h)), pl.BlockSpec((t, 2 * LANE), lambda h, i: (0, h)),
                  pl.BlockSpec((t, V_HEAD), lambda h, i: (0, h)), pl.BlockSpec((tq, V_HEAD), lambda h, i: (i, h)),
                  pl.BlockSpec((tq, LANE), lambda h, i: (i, h)), pl.BlockSpec((tq, LANE), lambda h, i: (i, h))],
        out_specs=pl.BlockSpec((tq, 2 * LANE), lambda h, i: (i, h)),
        out_shape=jax.ShapeDtypeStruct((t, MLA_HEADS * 2 * LANE), F32),
        name="attn_bwd_dq", compiler_params=pltpu.CompilerParams(dimension_semantics=("parallel", "arbitrary")),
    )(q, k, v, do, lse, delta)


def _attn_bwd_dkv(q, k, v, do, lse, delta, tq, tk):
    t = q.shape[0]
    nq = t // tq

    def body(q_ref, k_ref, v_ref, do_ref, lse_ref, dl_ref, dk_ref, dv_ref):
        j = pl.program_id(1)
        kt = k_ref[...]
        vt = v_ref[...]

        def step(i, carry):
            dk, dv = carry
            q0 = pl.multiple_of(i * tq, tq)
            qt = q_ref[pl.ds(q0, tq), :]
            dot_ = do_ref[pl.ds(q0, tq), :]
            s = _dot(qt, kt, NT) * _ATT_SCALE
            p = jnp.where(_causal(tq, tk, q0, j * tk), jnp.exp(s - lse_ref[pl.ds(q0, tq), pl.ds(0, 1)]), 0.0)
            dv = dv + _dot(p.astype(BF), dot_, TN)
            dp = _dot(dot_, vt, NT)
            ds = (p * (dp - dl_ref[pl.ds(q0, tq), pl.ds(0, 1)]) * _ATT_SCALE).astype(BF)
            dk = dk + _dot(ds, qt, TN)
            return dk, dv

        dk, dv = lax.fori_loop((j * tk) // tq, nq, step, (jnp.zeros((tk, 2 * LANE), F32), jnp.zeros((tk, V_HEAD), F32)))
        dk_ref[...] = dk
        dv_ref[...] = dv

    return pl.pallas_call(
        body, grid=(MLA_HEADS, t // tk),
        in_specs=[pl.BlockSpec((t, 2 * LANE), lambda h, j: (0, h)), pl.BlockSpec((tk, 2 * LANE), lambda h, j: (j, h)),
                  pl.BlockSpec((tk, V_HEAD), lambda h, j: (j, h)), pl.BlockSpec((t, V_HEAD), lambda h, j: (0, h)),
                  pl.BlockSpec((t, LANE), lambda h, j: (0, h)), pl.BlockSpec((t, LANE), lambda h, j: (0, h))],
        out_specs=[pl.BlockSpec((tk, 2 * LANE), lambda h, j: (j, h)), pl.BlockSpec((tk, V_HEAD), lambda h, j: (j, h))],
        out_shape=[jax.ShapeDtypeStruct((t, MLA_HEADS * 2 * LANE), F32), jax.ShapeDtypeStruct((t, MLA_VW), F32)],
        name="attn_bwd_dkv", compiler_params=pltpu.CompilerParams(dimension_semantics=("parallel", "arbitrary")),
    )(q, k, v, do, lse, delta)


def _adamw(w, g, m, v, name):
    r, c = w.shape
    tr = _pick(r, (256, 128, 64, 32, 16, 8))

    def body(w_ref, g_ref, m_ref, v_ref, d_ref, nm_ref, nv_ref):
        gg = g_ref[...]
        mm = ADAM_B1 * m_ref[...] + (1.0 - ADAM_B1) * gg
        vv = ADAM_B2 * v_ref[...] + (1.0 - ADAM_B2) * jnp.square(gg)
        m_hat = mm / (1.0 - ADAM_B1 ** ADAM_STEP)
        v_hat = vv / (1.0 - ADAM_B2 ** ADAM_STEP)
        d_ref[...] = -ADAM_LR * (m_hat / (jnp.sqrt(v_hat) + ADAM_EPS) + ADAM_WD * w_ref[...])
        nm_ref[...] = mm
        nv_ref[...] = vv

    spec = pl.BlockSpec((tr, c), lambda i: (i, 0))
    return pl.pallas_call(
        body, grid=(r // tr,), in_specs=[spec] * 4, out_specs=[spec] * 3,
        out_shape=[jax.ShapeDtypeStruct((r, c), F32)] * 3, name=name,
        compiler_params=pltpu.CompilerParams(dimension_semantics=("arbitrary",)),
    )(w, g, m, v)


def _pad_cols(a, n):
    return jnp.pad(a, ((0, 0), (0, n - a.shape[1])))


def _w_in_to_padded(w):
    s_ba = P_CQ
    s_cq = s_ba + 2 * DN_HEADS
    s_kr = s_cq + Q_LORA + KV_LORA
    return jnp.concatenate([w[:, :s_ba], w[:, s_cq:s_kr], _pad_cols(w[:, s_ba:s_cq], LANE), _pad_cols(w[:, s_kr:], LANE)], axis=1)


def _w_in_from_padded(w):
    return jnp.concatenate([w[:, :P_CQ], w[:, P_BA:P_BA + 2 * DN_HEADS], w[:, P_CQ:P_BA], w[:, P_KR:P_KR + QK_ROPE]], axis=1)


def _w_uq_to_padded(w):
    w3 = w.reshape(Q_LORA, MLA_HEADS, QK_NOPE + QK_ROPE)
    nope = w3[:, :, :QK_NOPE].reshape(Q_LORA, MLA_HEADS * QK_NOPE)
    rope = jnp.pad(w3[:, :, QK_NOPE:], ((0, 0), (0, 0), (0, LANE - QK_ROPE))).reshape(Q_LORA, MLA_HEADS * LANE)
    return jnp.concatenate([nope, rope], axis=1)


def _w_uq_from_padded(w):
    nope = w[:, :MLA_HEADS * QK_NOPE].reshape(Q_LORA, MLA_HEADS, QK_NOPE)
    rope = w[:, MLA_HEADS * QK_NOPE:].reshape(Q_LORA, MLA_HEADS, LANE)[:, :, :QK_ROPE]
    return jnp.concatenate([nope, rope], axis=2).reshape(Q_LORA, MLA_HEADS * (QK_NOPE + QK_ROPE))


def _pack(pieces, width, row_mult):
    flat = jnp.concatenate([p.reshape(-1) for p in pieces])
    n = flat.shape[0]
    rows = -(-n // (width * row_mult)) * row_mult
    return jnp.pad(flat, (0, rows * width - n)).reshape(rows, width)


def _unpack(flat, shapes):
    out, o = [], 0
    for s in shapes:
        n = math.prod(s)
        out.append(flat[o:o + n].reshape(s))
        o += n
    return out


def kernel(x, c, positions, w_ada, b_ada, w_in, conv_w, a_log, dt_bias, dn_norm_g, q_norm_g, w_uq, kv_norm_g, w_ukv, w_o, ln1_g, ln1_b, w_gate, w_up, w_down, ln2_g, ln2_b, loss_target, m_w_ada, m_b_ada, m_w_in, m_conv_w, m_a_log, m_dt_bias, m_dn_norm_g, m_q_norm_g, m_w_uq, m_kv_norm_g, m_w_ukv, m_w_o, m_ln1_g, m_ln1_b, m_w_gate, m_w_up, m_w_down, m_ln2_g, m_ln2_b, v_w_ada, v_b_ada, v_w_in, v_conv_w, v_a_log, v_dt_bias, v_dn_norm_g, v_q_norm_g, v_w_uq, v_kv_norm_g, v_w_ukv, v_w_o, v_ln1_g, v_ln1_b, v_w_gate, v_w_up, v_w_down, v_ln2_g, v_ln2_b):
    me = 4 * lax.axis_index("x") + 2 * lax.axis_index("y") + lax.axis_index("c")
    t, d = x.shape[1], x.shape[2]
    ff_n = w_gate.shape[2] * N_DEV
    ada_n = w_ada.shape[2]

    small = _pack([c[0], conv_w[0, :, 0, :]], LANE, 8)
    small_all = _exchange(small, "gather_small", scatter=False)
    sm = small_all.reshape(N_DEV, -1)
    c_all = sm[:, :d]
    cw = conv_w.shape[3]
    conv_full = sm[:, d:d + CONV_K * cw].reshape(N_DEV, CONV_K, cw).transpose(1, 0, 2).reshape(CONV_K, N_DEV * cw)
    conv_w8 = jnp.pad(conv_full, ((0, 8 - CONV_K), (0, 0)))

    b_ada_mine = lax.dynamic_slice(b_ada, (0, me * ada_n), (1, ada_n))
    mod_cols = _mod_fwd(c_all, w_ada[0], b_ada_mine)
    mod_all = _exchange(mod_cols.reshape(N_DEV, 1, ada_n), "scatter_mod", scatter=True)
    mod = mod_all.reshape(1, N_DEV * ada_n)

    shard_shapes = [w_in.shape[1:], w_uq.shape[1:], w_ukv.shape[1:], w_o.shape[1:], w_gate.shape[1:], w_up.shape[1:], w_down.shape[1:]]
    wpack = _pack([w.astype(BF) for w in (w_in[0], w_uq[0], w_ukv[0], w_o[0], w_gate[0], w_up[0], w_down[0])], 512, 16)
    wall = _exchange(wpack, "gather_weights", scatter=False).reshape(N_DEV, -1)
    per_dev = [_unpack(wall[s], shard_shapes) for s in range(N_DEV)]
    cols = lambda i: jnp.concatenate([per_dev[s][i] for s in range(N_DEV)], axis=1)
    rows_ = lambda i: jnp.concatenate([per_dev[s][i] for s in range(N_DEV)], axis=0)
    w_in_p = _w_in_to_padded(cols(0))
    w_uq_p = _w_uq_to_padded(cols(1))
    w_ukv_f = cols(2)
    w_o_f = rows_(3)
    w_gu = jnp.concatenate([cols(4), cols(5)], axis=1)
    w_down_f = rows_(6)

    loc = _local_step(x[0], loss_target[0], positions[0], mod, w_in_p, w_uq_p, w_ukv_f, w_o_f, w_gu, w_down_f, conv_w8,
                      a_log, dt_bias, dn_norm_g, q_norm_g, kv_norm_g, ln1_g, ln1_b, ln2_g, ln2_b)
    (grad_x, loss_acc, dmod, d_conv8, d_al8, d_dt8, d_dn_g, d_q_g, d_kv_g, d_ln1_g, d_ln1_b, d_ln2_g, d_ln2_b,
     g_w_in_p, g_w_uq_p, g_w_ukv, g_w_o, g_w_gu, g_w_down) = loc

    def col_pieces(g, n):
        return [g[:, s * n:(s + 1) * n] for s in range(N_DEV)]

    def row_pieces(g, n):
        return [g[s * n:(s + 1) * n] for s in range(N_DEV)]

    g_w_in = _w_in_from_padded(g_w_in_p)
    g_w_uq = _w_uq_from_padded(g_w_uq_p).astype(BF)
    pieces = [col_pieces(g_w_in, w_in.shape[2]), col_pieces(g_w_uq, w_uq.shape[2]), col_pieces(g_w_ukv.astype(BF), w_ukv.shape[2]),
              row_pieces(g_w_o, w_o.shape[1]), col_pieces(g_w_gu[:, :ff_n], w_gate.shape[2]),
              col_pieces(g_w_gu[:, ff_n:], w_up.shape[2]), row_pieces(g_w_down, w_down.shape[1])]
    gpack = jnp.stack([_pack([p[s] for p in pieces], 512, 16) for s in range(N_DEV)])
    g_sum = _sum_slots(_exchange(gpack, "scatter_grads", scatter=True), "sum_grads").reshape(-1)
    gs_w_in, gs_w_uq, gs_w_ukv, gs_w_o, gs_w_gate, gs_w_up, gs_w_down = _unpack(g_sum, shard_shapes)

    small_shapes = [(6 * d,), (CONV_K, N_DEV * cw), (DN_HEADS,), (DN_HEADS,), (DN_DV,), (Q_LORA,), (KV_LORA,), (d,), (d,), (d,), (d,), (1,)]
    gsmall = _pack([dmod, d_conv8[:CONV_K], d_al8[0, :DN_HEADS], d_dt8[0, :DN_HEADS], d_dn_g, d_q_g, d_kv_g,
                    d_ln1_g, d_ln1_b, d_ln2_g, d_ln2_b, loss_acc[0, :1]], LANE, 8)
    gsmall_all = _exchange(gsmall, "gather_small_grads", scatter=False)
    dmod_all = gsmall_all.reshape(N_DEV, -1)[:, :6 * d]
    tot = _unpack(_sum_slots(gsmall_all, "sum_small_grads").reshape(-1), small_shapes)
    g_b_ada, g_conv_full, g_a_log, g_dt_bias, g_dn_g, g_q_g, g_kv_g, g_ln1_g, g_ln1_b, g_ln2_g, g_ln2_b, loss1 = tot
    loss = loss1.reshape(())
    g_conv_w = lax.dynamic_slice(g_conv_full, (0, me * cw), (CONV_K, cw))
    g_w_ada = _mod_bwd(c_all, lax.dynamic_slice(dmod_all, (0, me * ada_n), (N_DEV, ada_n)))

    grads = {"w_ada": g_w_ada[None], "b_ada": g_b_ada[None], "w_in": gs_w_in[None], "conv_w": g_conv_w[None, :, None, :],
             "a_log": g_a_log[None], "dt_bias": g_dt_bias[None], "dn_norm_g": g_dn_g[None], "q_norm_g": g_q_g[None],
             "w_uq": gs_w_uq[None], "kv_norm_g": g_kv_g[None], "w_ukv": gs_w_ukv[None], "w_o": gs_w_o[None],
             "ln1_g": g_ln1_g[None], "ln1_b": g_ln1_b[None], "w_gate": gs_w_gate[None], "w_up": gs_w_up[None],
             "w_down": gs_w_down[None], "ln2_g": g_ln2_g[None], "ln2_b": g_ln2_b[None]}
    weights = dict(w_ada=w_ada, b_ada=b_ada, w_in=w_in, conv_w=conv_w, a_log=a_log, dt_bias=dt_bias, dn_norm_g=dn_norm_g,
                   q_norm_g=q_norm_g, w_uq=w_uq, kv_norm_g=kv_norm_g, w_ukv=w_ukv, w_o=w_o, ln1_g=ln1_g, ln1_b=ln1_b,
                   w_gate=w_gate, w_up=w_up, w_down=w_down, ln2_g=ln2_g, ln2_b=ln2_b)
    ms = dict(w_ada=m_w_ada, b_ada=m_b_ada, w_in=m_w_in, conv_w=m_conv_w, a_log=m_a_log, dt_bias=m_dt_bias,
              dn_norm_g=m_dn_norm_g, q_norm_g=m_q_norm_g, w_uq=m_w_uq, kv_norm_g=m_kv_norm_g, w_ukv=m_w_ukv, w_o=m_w_o,
              ln1_g=m_ln1_g, ln1_b=m_ln1_b, w_gate=m_w_gate, w_up=m_w_up, w_down=m_w_down, ln2_g=m_ln2_g, ln2_b=m_ln2_b)
    vs = dict(w_ada=v_w_ada, b_ada=v_b_ada, w_in=v_w_in, conv_w=v_conv_w, a_log=v_a_log, dt_bias=v_dt_bias,
              dn_norm_g=v_dn_norm_g, q_norm_g=v_q_norm_g, w_uq=v_w_uq, kv_norm_g=v_kv_norm_g, w_ukv=v_w_ukv, w_o=v_w_o,
              ln1_g=v_ln1_g, ln1_b=v_ln1_b, w_gate=v_w_gate, w_up=v_w_up, w_down=v_w_down, ln2_g=v_ln2_g, ln2_b=v_ln2_b)
    names = list(weights)
    big = ("w_ada", "w_in", "w_uq", "w_ukv", "w_o", "w_gate", "w_up", "w_down")
    delta_w, new_m, new_v = {}, {}, {}
    for n in big:
        shp = weights[n].shape
        two = lambda a: a.reshape(shp[-2], shp[-1])
        dlt, nm, nv = _adamw(two(weights[n]), two(grads[n]), two(ms[n]), two(vs[n]), "adamw_" + n)
        delta_w[n], new_m[n], new_v[n] = dlt.reshape(shp), nm.reshape(shp), nv.reshape(shp)
    rest = [n for n in names if n not in big]
    rest_shapes = [weights[n].shape for n in rest]
    packed = [_pack([src[n] for n in rest], LANE, 8) for src in (weights, grads, ms, vs)]
    outs = _adamw(*packed, "adamw_small")
    for dst, o in zip((delta_w, new_m, new_v), outs):
        for n, a in zip(rest, _unpack(o.reshape(-1), rest_shapes)):
            dst[n] = a

    return (loss, grad_x[None], *[grads[n] for n in names], *[delta_w[n] for n in names],
            *[new_m[n] for n in names], *[new_v[n] for n in names])


def _local_step(xs, tgt, pos, mod, w_in_p, w_uq_p, w_ukv_f, w_o_f, w_gu, w_down_f, conv_w8,
                a_log, dt_bias, dn_norm_g, q_norm_g, kv_norm_g, ln1_g, ln1_b, ln2_g, ln2_b):
    t, d = xs.shape
    ff_n = w_down_f.shape[0]
    sh_m, sc_m, gt_m, sh_f, sc_f, gt_f = [mod[:, i * d:(i + 1) * d] for i in range(6)]
    pos_col = pos.astype(F32).reshape(t, 1)
    inv_freq = 1.0 / (ROPE_THETA ** (jnp.arange(0, QK_ROPE, 2, dtype=F32) / QK_ROPE))
    inv_freq2 = jnp.pad(jnp.concatenate([inv_freq, inv_freq]), (0, LANE - QK_ROPE)).reshape(1, LANE)
    al8 = jnp.pad(a_log, ((0, 7), (0, LANE - DN_HEADS)))
    dt8 = jnp.pad(dt_bias, ((0, 7), (0, LANE - DN_HEADS)))

    tm = min(512, t)
    tq = min(256, t)
    tk = min(512, t)

    (h1,) = _rowwise("modulate_in", lambda xx, sc, sh: xx * (1.0 + sc) + sh, [xs], [sc_m, sh_m], [(d, BF)], [], tm)
    proj = _matmul(h1, w_in_p, "nn", "in_proj")
    qkv = _conv_fwd(proj, conv_w8, min(256, t))
    gdn_tm = min(512, t)
    intra = _gdn_intra_fwd(qkv, proj, al8, dt8, gdn_tm)
    o_dn, states = _gdn_scan_fwd(intra, gdn_tm)
    qc, kc, vc = _mla_prep_fwd(proj, pos_col, inv_freq2, q_norm_g, kv_norm_g, w_uq_p, w_ukv_f, tm)
    o_mla, lse = _attn_fwd(qc, kc, vc, tq, tk)

    def mix_in(o, z, om, g):
        return jnp.concatenate(_gdn_out(o, z, g) + [om], axis=1)

    (mixin,) = _rowwise("mixer_out", mix_in, [o_dn, (proj, DN_VW, P_Z // DN_VW), o_mla], [dn_norm_g], [(2 * DN_VW, BF)], [], tm)
    mix = _matmul(mixin, w_o_f, "nn", "out_proj")

    def block1(xx, mx, gt, g1, b1, sc, sh):
        x1 = _layernorm(DEEPNORM_ALPHA * xx + gt * mx, g1, b1)
        return x1, x1 * (1.0 + sc) + sh

    x1, h2 = _rowwise("norm1_modulate", block1, [xs, mix], [gt_m, ln1_g, ln1_b, sc_f, sh_f], [(d, F32), (d, BF)], [], tm)
    gu = _matmul(h2, w_gu, "nn", "ffn_in")
    (act,) = _rowwise("swiglu", lambda gg, uu: _silu(gg) * uu, [(gu, ff_n, 0), (gu, ff_n, 1)], [], [(ff_n, BF)], [], min(256, t))
    ff = _matmul(act, w_down_f, "nn", "ffn_out")

    def tail_loss(x1_, ff_, gt, g2, b2, tg):
        y = _layernorm(DEEPNORM_ALPHA * x1_ + gt * ff_, g2, b2)
        return 0.5 * jnp.sum(jnp.mean(jnp.square(y - tg), axis=-1))

    def tail(x1_, ff_, tg, gt, g2, b2):
        loss, (dx1, dff, dgt, dg2, db2) = jax.value_and_grad(tail_loss, argnums=(0, 1, 2, 3, 4))(x1_, ff_, gt, g2, b2, tg)
        return dx1, dff, jnp.full((1, LANE), loss, F32), dgt, dg2, db2

    dx1_a, dff, loss_acc, d_gt_f, d_ln2_g, d_ln2_b = _rowwise(
        "norm2_loss", tail, [x1, ff, tgt], [gt_f, ln2_g, ln2_b], [(d, F32), (d, BF)], [(1, LANE), (1, d), (1, d), (1, d)], tm)

    dact = _matmul(dff, w_down_f, "nt", "d_ffn_act")
    g_w_down = _matmul(act, dff, "tn", "d_w_down", BF)

    def swiglu_bwd(gg, uu, da):
        _, vjp = jax.vjp(lambda a, b: _silu(a) * b, gg, uu)
        dg, du = vjp(da)
        return jnp.concatenate([dg, du], axis=1)

    (dgu,) = _rowwise("swiglu_bwd", swiglu_bwd, [(gu, ff_n, 0), (gu, ff_n, 1), dact], [], [(2 * ff_n, BF)], [], min(256, t))
    dh2 = _matmul(dgu, w_gu, "nt", "d_ffn_in")
    g_w_gu = _matmul(h2, dgu, "tn", "d_w_gate_up", BF)

    def block1_bwd(xx, mx, dx1_, dh2_, gt, g1, b1, sc, sh):
        _, vjp = jax.vjp(block1, xx, mx, gt, g1, b1, sc, sh)
        dxx, dmx, dgt, dg1, db1, dsc, dsh = vjp((dx1_, dh2_))
        return dxx, dmx, dgt, dg1, db1, dsc, dsh

    dx_a, dmix, d_gt_m, d_ln1_g, d_ln1_b, d_sc_f, d_sh_f = _rowwise(
        "norm1_modulate_bwd", block1_bwd, [xs, mix, dx1_a, dh2], [gt_m, ln1_g, ln1_b, sc_f, sh_f],
        [(d, F32), (d, BF)], [(1, d)] * 5, min(256, t))

    dmixin = _matmul(dmix, w_o_f, "nt", "d_mixer_out")
    g_w_o = _matmul(mixin, dmix, "tn", "d_w_o", BF)

    def mixer_bwd(o, z, om, dmi, g):
        _, vjp = jax.vjp(lambda o_, z_, g_: jnp.concatenate(_gdn_out(o_, z_, g_), axis=1), o, z, g)
        do_, dz_, dg_ = vjp(dmi[:, :DN_VW])
        dom = dmi[:, DN_VW:]
        delta = [jnp.broadcast_to(jnp.sum(dom[:, h * V_HEAD:(h + 1) * V_HEAD] * om[:, h * V_HEAD:(h + 1) * V_HEAD], axis=-1, keepdims=True), (o.shape[0], LANE))
                 for h in range(MLA_HEADS)]
        return do_, dz_, dom, jnp.concatenate(delta, axis=1), dg_

    do_dn, dz, do_mla, delta, d_dn_g = _rowwise(
        "mixer_out_bwd", mixer_bwd, [o_dn, (proj, DN_VW, P_Z // DN_VW), o_mla, dmixin], [dn_norm_g],
        [(DN_VW, F32), (DN_VW, BF), (MLA_VW, BF), (MLA_HEADS * LANE, F32)], [(1, DN_DV)], tm)

    dqc = _attn_bwd_dq(qc, kc, vc, do_mla, lse, delta, tq, tk)
    dkc, dvc = _attn_bwd_dkv(qc, kc, vc, do_mla, lse, delta, tq, tk)
    dcq, dckv, dkr, d_q_g, d_kv_g, g_w_uq_p, g_w_ukv = _mla_prep_bwd(
        proj, pos_col, inv_freq2, q_norm_g, kv_norm_g, w_uq_p, w_ukv_f, dqc, dkc, dvc, min(256, t))

    d_intra = _gdn_scan_bwd(intra, states, do_dn, gdn_tm)
    dqkv_act, dba, d_al8, d_dt8 = _gdn_intra_bwd(qkv, proj, al8, dt8, d_intra, min(256, t))
    dqkv_pre, d_conv8 = _conv_bwd(proj, conv_w8, dqkv_act, min(256, t))

    dproj = jnp.concatenate([dqkv_pre, dz, dcq, dckv, dba, dkr], axis=1)
    dh1 = _matmul(dproj, w_in_p, "nt", "d_in_proj")
    g_w_in_p = _matmul(h1, dproj, "tn", "d_w_in", BF)

    def modulate_bwd(xx, dh, dxa, sc):
        return dh * (1.0 + sc) + dxa, jnp.sum(dh * xx, axis=0, keepdims=True), jnp.sum(dh, axis=0, keepdims=True)

    grad_x, d_sc_m, d_sh_m = _rowwise("modulate_in_bwd", modulate_bwd, [xs, dh1, dx_a], [sc_m], [(d, F32)], [(1, d), (1, d)], tm)
    dmod = jnp.concatenate([d_sh_m, d_sc_m, d_gt_m, d_sh_f, d_sc_f, d_gt_f], axis=1)
    return (grad_x, loss_acc, dmod, d_conv8, d_al8, d_dt8, d_dn_g, d_q_g, d_kv_g, d_ln1_g, d_ln1_b, d_ln2_g, d_ln2_b,
            g_w_in_p, g_w_uq_p, g_w_ukv, g_w_o, g_w_gu, g_w_down)
```

```python
import functools
import math

import jax
import jax.numpy as jnp
from jax import lax
from jax.experimental import pallas as pl
from jax.experimental.pallas import tpu as pltpu

F32 = jnp.float32
BF = jnp.bfloat16
HI = lax.Precision.HIGHEST

N_DEV = 8
DN_HEADS = 4
DN_DK = 128
DN_DV = 128
CONV_K = 4
CHUNK = 64
MLA_HEADS = 4
QK_NOPE = 128
QK_ROPE = 64
V_HEAD = 128
Q_LORA = 512
KV_LORA = 256
ROPE_THETA = 10000.0
DEPTH = 1
DEEPNORM_ALPHA = (2.0 * DEPTH) ** 0.25
LANE = 128
CONV_HALO = 8

DN_QK = DN_HEADS * DN_DK
DN_VW = DN_HEADS * DN_DV
DN_CONV_CH = 2 * DN_QK + DN_VW
MLA_VW = MLA_HEADS * V_HEAD
MLA_QCAT = QK_NOPE + LANE
N_IN = DN_CONV_CH + DN_VW + 2 * DN_HEADS + Q_LORA + KV_LORA + QK_ROPE
P_QKV = 0
P_Z = DN_CONV_CH
P_CQ = P_Z + DN_VW
P_CKV = P_CQ + Q_LORA
P_BA = P_CKV + KV_LORA
P_KR = P_BA + LANE
N_INP = P_KR + LANE

ADAM_LR = 0.001
ADAM_B1 = 0.9
ADAM_B2 = 0.999
ADAM_EPS = 1e-08
ADAM_WD = 0.01
ADAM_STEP = 10

NN = (((1,), (0,)), ((), ()))
NT = (((1,), (1,)), ((), ()))
TN = (((0,), (0,)), ((), ()))


def _pick(n, prefs):
    for p in prefs:
        if n % p == 0:
            return p
    return n


def _full(shape):
    return pl.BlockSpec(shape, lambda *_: (0,) * len(shape))


def _dot(a, b, dims=NN):
    return lax.dot_general(a, b, dims, preferred_element_type=F32)


def _doth(a, b, dims=NN):
    return lax.dot_general(a, b, dims, precision=HI, preferred_element_type=F32)


@jax.custom_vjp
def _mmb(a, b):
    return _dot(a.astype(BF), b.astype(BF), NN)


def _mmb_fwd(a, b):
    return _mmb(a, b), (a, b)


def _mmb_bwd(res, g):
    a, b = res
    gb = g.astype(BF)
    return (_dot(gb, b.astype(BF), NT).astype(a.dtype), _dot(a.astype(BF), gb, TN).astype(b.dtype))


_mmb.defvjp(_mmb_fwd, _mmb_bwd)


@jax.custom_vjp
def _mmb_nt(a, b):
    return _dot(a.astype(BF), b.astype(BF), NT)


def _mmb_nt_fwd(a, b):
    return _mmb_nt(a, b), (a, b)


def _mmb_nt_bwd(res, g):
    a, b = res
    gb = g.astype(BF)
    return (_dot(gb, b.astype(BF), NN).astype(a.dtype), _dot(gb, a.astype(BF), TN).astype(b.dtype))


_mmb_nt.defvjp(_mmb_nt_fwd, _mmb_nt_bwd)


@jax.custom_vjp
def _mmb_tn(a, b):
    return _dot(a.astype(BF), b.astype(BF), TN)


def _mmb_tn_fwd(a, b):
    return _mmb_tn(a, b), (a, b)


def _mmb_tn_bwd(res, g):
    a, b = res
    gb = g.astype(BF)
    return (_dot(b.astype(BF), gb, NT).astype(a.dtype), _dot(a.astype(BF), gb, NN).astype(b.dtype))


_mmb_tn.defvjp(_mmb_tn_fwd, _mmb_tn_bwd)


def _sigmoid(x):
    return 0.5 * (jnp.tanh(0.5 * x) + 1.0)


def _silu(x):
    return x * _sigmoid(x)


def _softplus(x):
    return jnp.maximum(x, 0.0) + jnp.log(1.0 + jnp.exp(-jnp.abs(x)))


def _layernorm(x, g, b, eps=1e-5):
    mu = jnp.mean(x, axis=-1, keepdims=True)
    xc = x - mu
    var = jnp.mean(xc * xc, axis=-1, keepdims=True)
    return xc * lax.rsqrt(var + eps) * g + b


def _rmsnorm(x, g, eps=1e-6):
    return x * lax.rsqrt(jnp.mean(x * x, axis=-1, keepdims=True) + eps) * g


def _l2norm(x, eps=1e-6):
    return x * lax.rsqrt(jnp.sum(x * x, axis=-1, keepdims=True) + eps)


def _rowwise(name, fn, rows, vecs, out_rows, out_accs, tm):
    rows = [r if isinstance(r, tuple) else (r, r.shape[1], 0) for r in rows]
    t = rows[0][0].shape[0]
    tm = min(tm, t)
    assert t % tm == 0
    nr, nv, no = len(rows), len(vecs), len(out_rows)

    def body(*refs):
        ins = [r[...] for r in refs[:nr + nv]]
        outs = fn(*ins)
        outs = outs if isinstance(outs, (tuple, list)) else (outs,)
        o_rows = refs[nr + nv:nr + nv + no]
        o_accs = refs[nr + nv + no:]
        for o, val in zip(o_rows, outs[:no]):
            o[...] = val.astype(o.dtype)
        if o_accs:
            @pl.when(pl.program_id(0) == 0)
            def _():
                for o in o_accs:
                    o[...] = jnp.zeros_like(o)
            for o, val in zip(o_accs, outs[no:]):
                o[...] += val

    in_specs = [pl.BlockSpec((tm, w), functools.partial(lambda i, j: (i, j), j=j)) for (_, w, j) in rows]
    in_specs += [_full(v.shape) for v in vecs]
    out_specs = [pl.BlockSpec((tm, w), lambda i: (i, 0)) for (w, _) in out_rows]
    out_specs += [_full(s) for s in out_accs]
    out_shape = [jax.ShapeDtypeStruct((t, w), d) for (w, d) in out_rows]
    out_shape += [jax.ShapeDtypeStruct(s, F32) for s in out_accs]
    res = pl.pallas_call(
        body, grid=(t // tm,), in_specs=in_specs, out_specs=out_specs, out_shape=out_shape, name=name,
        compiler_params=pltpu.CompilerParams(dimension_semantics=("arbitrary",)),
    )(*[r[0] for r in rows], *vecs)
    return res


def _matmul(a, b, mode, name, out_dtype=F32):
    if mode == "nn":
        (m, k), n = a.shape, b.shape[1]
    elif mode == "nt":
        (m, k), n = a.shape, b.shape[0]
    else:
        (k, m), n = a.shape, b.shape[1]
    tm, tn, tk = _matmul_tiles(m, n, k, a.dtype.itemsize, b.dtype.itemsize, jnp.dtype(out_dtype).itemsize)
    nk = k // tk
    dims = {"nn": NN, "nt": NT, "tn": TN}[mode]

    def body(a_ref, b_ref, o_ref, *acc):
        part = _dot(a_ref[...].astype(BF), b_ref[...].astype(BF), dims)
        if nk == 1:
            o_ref[...] = part.astype(o_ref.dtype)
            return
        (acc_ref,) = acc
        kk = pl.program_id(2)

        @pl.when(kk == 0)
        def _():
            acc_ref[...] = part

        @pl.when(kk > 0)
        def _():
            acc_ref[...] += part

        @pl.when(kk == nk - 1)
        def _():
            o_ref[...] = acc_ref[...].astype(o_ref.dtype)

    a_spec = pl.BlockSpec((tk, tm), lambda i, j, kk: (kk, i)) if mode == "tn" else pl.BlockSpec((tm, tk), lambda i, j, kk: (i, kk))
    b_spec = pl.BlockSpec((tn, tk), lambda i, j, kk: (j, kk)) if mode == "nt" else pl.BlockSpec((tk, tn), lambda i, j, kk: (kk, j))
    return pl.pallas_call(
        body, grid=(m // tm, n // tn, nk), in_specs=[a_spec, b_spec],
        out_specs=pl.BlockSpec((tm, tn), lambda i, j, kk: (i, j)),
        out_shape=jax.ShapeDtypeStruct((m, n), out_dtype),
        scratch_shapes=[pltpu.VMEM((tm, tn), F32)] if nk > 1 else [], name=name,
        compiler_params=pltpu.CompilerParams(dimension_semantics=("parallel", "parallel", "arbitrary")),
    )(a, b)


MATMUL_VMEM_BUDGET = 28 * 1024 * 1024


def _matmul_tiles(m, n, k, a_bytes, b_bytes, o_bytes):
    def divisors(x, cap):
        return sorted({x // s for s in range(1, 65) if x % s == 0 and (x // s) % LANE == 0 and x // s <= cap}, reverse=True) or [x]

    for tk in divisors(k, k):
        best = None
        for tm in divisors(m, 1024):
            for tn in divisors(n, 2048):
                need = 2 * (tm * tk * a_bytes + tk * tn * b_bytes + tm * tn * o_bytes) + (tm * tn * 4 if tk < k else 0)
                if need <= MATMUL_VMEM_BUDGET and tm * tn >= 512 * 512 and (best is None or tm * tn > best[0] * best[1]):
                    best = (tm, tn)
        if best:
            return best[0], best[1], tk
    return _pick(m, (512, 256, 128)), _pick(n, (512, 256, 128)), _pick(k, (512, 256, 128))


def _exchange(xs, name, scatter):
    n = len(xs)
    npeer = N_DEV - 1

    def body(*refs):
        x_refs, o_refs = refs[:n], refs[n:2 * n]
        send_sems, recv_sems, local_sems = refs[2 * n:]
        mx, my, mc = lax.axis_index("x"), lax.axis_index("y"), lax.axis_index("c")
        me = 4 * mx + 2 * my + mc
        src_me = [x.at[me] if scatter else x for x in x_refs]
        mine = [pltpu.make_async_copy(src_me[a], o_refs[a].at[me], local_sems.at[a]) for a in range(n)]
        for cp in mine:
            cp.start()
        copies = []
        for k in range(1, N_DEV):
            px, py, pc = mx ^ (k >> 2), my ^ ((k >> 1) & 1), mc ^ (k & 1)
            peer = 4 * px + 2 * py + pc
            for a in range(n):
                cp = pltpu.make_async_remote_copy(
                    src_ref=x_refs[a].at[peer] if scatter else x_refs[a], dst_ref=o_refs[a].at[me],
                    send_sem=send_sems.at[a * npeer + k - 1], recv_sem=recv_sems.at[a * npeer + k - 1],
                    device_id=(px, py, pc), device_id_type=pl.DeviceIdType.MESH)
                cp.start()
                copies.append((cp, a, k, peer))
        for cp, a, k, peer in copies:
            pltpu.make_async_remote_copy(
                src_ref=src_me[a], dst_ref=o_refs[a].at[peer], send_sem=send_sems.at[a * npeer + k - 1],
                recv_sem=recv_sems.at[a * npeer + k - 1], device_id=(mx, my, mc),
                device_id_type=pl.DeviceIdType.MESH).wait_recv()
        for cp, _, _, _ in copies:
            cp.wait_send()
        for cp in mine:
            cp.wait()

    return pl.pallas_call(
        body, out_shape=[jax.ShapeDtypeStruct((N_DEV,) + x.shape[-2:], x.dtype) for x in xs],
        in_specs=[pl.BlockSpec(memory_space=pl.ANY)] * n, out_specs=[pl.BlockSpec(memory_space=pl.ANY)] * n,
        scratch_shapes=[pltpu.SemaphoreType.DMA((n * npeer,)), pltpu.SemaphoreType.DMA((n * npeer,)),
                        pltpu.SemaphoreType.DMA((n,))],
        name=name,
    )(*xs)


def _peer_of(k):
    mx, my, mc = lax.axis_index("x"), lax.axis_index("y"), lax.axis_index("c")
    px, py, pc = mx ^ (k >> 2), my ^ ((k >> 1) & 1), mc ^ (k & 1)
    return (px, py, pc), 4 * px + 2 * py + pc


def _exchange_start(xs, name, scatter):
    n = len(xs)
    npeer = N_DEV - 1

    def body(*refs):
        x_refs, land_refs = refs[:n], refs[n:2 * n]
        send_sems, recv_sems = refs[2 * n], refs[2 * n + 1]
        me = 4 * lax.axis_index("x") + 2 * lax.axis_index("y") + lax.axis_index("c")
        for k in range(1, N_DEV):
            dev, peer = _peer_of(k)
            for a in range(n):
                pltpu.make_async_remote_copy(
                    src_ref=x_refs[a].at[peer] if scatter else x_refs[a], dst_ref=land_refs[a].at[me],
                    send_sem=send_sems.at[a * npeer + k - 1], recv_sem=recv_sems.at[a * npeer + k - 1],
                    device_id=dev, device_id_type=pl.DeviceIdType.MESH).start()

    hbm = pl.BlockSpec(memory_space=pltpu.HBM)
    sem = pl.BlockSpec(memory_space=pltpu.SEMAPHORE)
    lands = [pltpu.with_memory_space_constraint(lax.empty((N_DEV,) + x.shape[-2:], x.dtype), pltpu.HBM) for x in xs]
    srcs = [pltpu.with_memory_space_constraint(x, pltpu.HBM) for x in xs]
    outs = pl.pallas_call(
        body, name=name,
        out_shape=(pltpu.SemaphoreType.DMA((n * npeer,)), pltpu.SemaphoreType.DMA((n * npeer,)),
                   *[pltpu.HBM(x.shape, x.dtype) for x in srcs], *[pltpu.HBM(z.shape, z.dtype) for z in lands]),
        in_specs=[hbm] * (2 * n), out_specs=(sem, sem, *[hbm] * (2 * n)),
        input_output_aliases={i: 2 + i for i in range(2 * n)},
        compiler_params=pltpu.CompilerParams(has_side_effects=pltpu.SideEffectType.DATAFLOW_SIDE_EFFECTING),
    )(*srcs, *lands)
    return outs[0], outs[1], list(outs[2:2 + n]), list(outs[2 + n:])


def _exchange_wait(started, after, name, scatter):
    send_sems, recv_sems, srcs, lands = started
    n = len(srcs)
    npeer = N_DEV - 1

    def body(*refs):
        x_refs, land_refs = refs[:n], refs[n:2 * n]
        send_sems, recv_sems = refs[2 * n], refs[2 * n + 1]
        mx, my, mc = lax.axis_index("x"), lax.axis_index("y"), lax.axis_index("c")
        me = 4 * mx + 2 * my + mc
        for k in range(1, N_DEV):
            _, peer = _peer_of(k)
            for a in range(n):
                src = x_refs[a].at[me] if scatter else x_refs[a]
                cp = pltpu.make_async_remote_copy(
                    src_ref=src, dst_ref=land_refs[a].at[peer], send_sem=send_sems.at[a * npeer + k - 1],
                    recv_sem=recv_sems.at[a * npeer + k - 1], device_id=(mx, my, mc), device_id_type=pl.DeviceIdType.MESH)
                cp.wait_send()
                cp.wait_recv()

    hbm = pl.BlockSpec(memory_space=pltpu.HBM)
    sem = pl.BlockSpec(memory_space=pltpu.SEMAPHORE)
    outs = pl.pallas_call(
        body, name=name,
        out_shape=(*[pltpu.HBM(x.shape, x.dtype) for x in srcs], *[pltpu.HBM(z.shape, z.dtype) for z in lands]),
        in_specs=[hbm] * (2 * n) + [sem, sem, pl.BlockSpec(memory_space=pl.ANY)], out_specs=tuple([hbm] * (2 * n)),
        input_output_aliases={i: i for i in range(2 * n)},
        compiler_params=pltpu.CompilerParams(has_side_effects=pltpu.SideEffectType.DATAFLOW_SIDE_EFFECTING),
    )(*srcs, *lands, send_sems, recv_sems, after)
    return _fill_own(list(outs[:n]), list(outs[n:]), name + "_own", scatter)


def _fill_own(srcs, lands, name, scatter):
    n = len(srcs)

    def body(*refs):
        x_refs, out_lands, sems = refs[:n], refs[2 * n:3 * n], refs[-1]
        me = 4 * lax.axis_index("x") + 2 * lax.axis_index("y") + lax.axis_index("c")
        mine = [pltpu.make_async_copy(x_refs[a].at[me] if scatter else x_refs[a], out_lands[a].at[me], sems.at[a]) for a in range(n)]
        for cp in mine:
            cp.start()
        for cp in mine:
            cp.wait()

    anyspace = pl.BlockSpec(memory_space=pl.ANY)
    return list(pl.pallas_call(
        body, name=name, out_shape=[jax.ShapeDtypeStruct(z.shape, z.dtype) for z in lands],
        in_specs=[anyspace] * (2 * n), out_specs=[anyspace] * n, input_output_aliases={n + i: i for i in range(n)},
        scratch_shapes=[pltpu.SemaphoreType.DMA((n,))],
    )(*srcs, *lands))


def _sum_slots(x, name):
    _, r, c = x.shape
    tr = _pick(r, (512, 256, 128, 64, 32, 16))

    def body(x_ref, o_ref):
        acc = x_ref[0].astype(F32)
        for s in range(1, N_DEV):
            acc = acc + x_ref[s].astype(F32)
        o_ref[...] = acc

    return pl.pallas_call(
        body, grid=(r // tr,), in_specs=[pl.BlockSpec((N_DEV, tr, c), lambda i: (0, i, 0))],
        out_specs=pl.BlockSpec((tr, c), lambda i: (i, 0)), out_shape=jax.ShapeDtypeStruct((r, c), F32), name=name,
        compiler_params=pltpu.CompilerParams(dimension_semantics=("arbitrary",)),
    )(x)


def _mod_fwd(c_all, w_ada, b_ada_mine):
    def body(c_ref, w_ref, b_ref, o_ref):
        o_ref[...] = _doth(_silu(c_ref[...]), w_ref[...]) + b_ref[...]

    return pl.pallas_call(body, out_shape=jax.ShapeDtypeStruct((c_all.shape[0], w_ada.shape[1]), F32), name="mod_fwd")(c_all, w_ada, b_ada_mine)


def _mod_bwd(c_all, dmod_mine):
    def body(c_ref, d_ref, o_ref):
        o_ref[...] = _doth(_silu(c_ref[...]), d_ref[...], TN)

    return pl.pallas_call(body, out_shape=jax.ShapeDtypeStruct((c_all.shape[1], dmod_mine.shape[1]), F32), name="mod_bwd")(c_all, dmod_mine)


def _conv_fwd(proj, conv_w8, tm):
    t = proj.shape[0]
    ch = DN_CONV_CH

    def body(x_ref, w_ref, o_ref, buf):
        @pl.when(pl.program_id(0) == 0)
        def _():
            buf[pl.ds(0, CONV_HALO), :] = jnp.zeros((CONV_HALO, ch), F32)

        buf[pl.ds(CONV_HALO, tm), :] = x_ref[...]
        acc = jnp.zeros((tm, ch), F32)
        for j in range(CONV_K):
            acc = acc + buf[pl.ds(CONV_HALO - (CONV_K - 1) + j, tm), :] * w_ref[pl.ds(j, 1), :]
        o_ref[...] = _silu(acc)
        buf[pl.ds(0, CONV_HALO), :] = buf[pl.ds(tm, CONV_HALO), :]

    return pl.pallas_call(
        body, grid=(t // tm,), in_specs=[pl.BlockSpec((tm, ch), lambda i: (i, 0)), _full(conv_w8.shape)],
        out_specs=pl.BlockSpec((tm, ch), lambda i: (i, 0)), out_shape=jax.ShapeDtypeStruct((t, ch), F32),
        scratch_shapes=[pltpu.VMEM((tm + CONV_HALO, ch), F32)], name="conv_fwd",
        compiler_params=pltpu.CompilerParams(dimension_semantics=("arbitrary",)),
    )(proj, conv_w8)


def _conv_bwd(proj, conv_w8, dact, tm):
    t = proj.shape[0]
    ch = DN_CONV_CH
    nt = t // tm
    hb = tm // CONV_HALO

    def body(x_ref, xp_ref, w_ref, dy_ref, dx_ref, dw_ref, xbuf, dbuf):
        step = pl.program_id(0)

        @pl.when(step == 0)
        def _():
            dbuf[pl.ds(tm, CONV_HALO), :] = jnp.zeros((CONV_HALO, ch), F32)
            dw_ref[...] = jnp.zeros_like(dw_ref)

        first = step == nt - 1
        xbuf[pl.ds(0, CONV_HALO), :] = jnp.where(first, 0.0, xp_ref[...])
        xbuf[pl.ds(CONV_HALO, tm), :] = x_ref[...]
        pre = jnp.zeros((tm, ch), F32)
        for j in range(CONV_K):
            pre = pre + xbuf[pl.ds(CONV_HALO - (CONV_K - 1) + j, tm), :] * w_ref[pl.ds(j, 1), :]
        sg = _sigmoid(pre)
        dpre = dy_ref[...] * (sg * (1.0 + pre * (1.0 - sg)))
        dbuf[pl.ds(0, tm), :] = dpre
        dx = jnp.zeros((tm, ch), F32)
        for j in range(CONV_K):
            dx = dx + dbuf[pl.ds(CONV_K - 1 - j, tm), :] * w_ref[pl.ds(j, 1), :]
            dw_ref[pl.ds(j, 1), :] += jnp.sum(dpre * xbuf[pl.ds(CONV_HALO - (CONV_K - 1) + j, tm), :], axis=0, keepdims=True)
        dx_ref[...] = dx.astype(dx_ref.dtype)
        dbuf[pl.ds(tm, CONV_HALO), :] = dbuf[pl.ds(0, CONV_HALO), :]

    rev = lambda i: (nt - 1 - i, 0)
    prev = lambda i: (jnp.maximum((nt - 1 - i) * hb - 1, 0), 0)
    return pl.pallas_call(
        body, grid=(nt,),
        in_specs=[pl.BlockSpec((tm, ch), rev), pl.BlockSpec((CONV_HALO, ch), prev), _full(conv_w8.shape),
                  pl.BlockSpec((tm, ch), rev)],
        out_specs=[pl.BlockSpec((tm, ch), rev), _full(conv_w8.shape)],
        out_shape=[jax.ShapeDtypeStruct((t, ch), BF), jax.ShapeDtypeStruct(conv_w8.shape, F32)],
        scratch_shapes=[pltpu.VMEM((tm + CONV_HALO, ch), F32), pltpu.VMEM((tm + CONV_HALO, ch), F32)], name="conv_bwd",
        compiler_params=pltpu.CompilerParams(dimension_semantics=("arbitrary",)),
    )(proj, proj, conv_w8, dact)


BNN = (((2,), (1,)), ((0,), (0,)))
BNT = (((2,), (2,)), ((0,), (0,)))
BTN = (((1,), (1,)), ((0,), (0,)))


def _bdot(a, b, dims, precision=None):
    return lax.dot_general(a, b, dims, precision=precision, preferred_element_type=F32)


@jax.custom_vjp
def _bmmb_nt(a, b):
    return _bdot(a.astype(BF), b.astype(BF), BNT)


def _bmmb_nt_fwd(a, b):
    return _bmmb_nt(a, b), (a, b)


def _bmmb_nt_bwd(res, g):
    a, b = res
    gb = g.astype(BF)
    return _bdot(gb, b.astype(BF), BNN), _bdot(gb, a.astype(BF), BTN)


_bmmb_nt.defvjp(_bmmb_nt_fwd, _bmmb_nt_bwd)


def _gdn_intra(qkv, ba, al8, dt8):
    tm = qkv.shape[0]
    nb = tm // CHUNK
    bsz = DN_HEADS * nb

    def heads(x0):
        return jnp.concatenate([qkv[:, x0 + h * LANE:x0 + (h + 1) * LANE].reshape(nb, CHUNK, LANE) for h in range(DN_HEADS)], axis=0)

    def spread(c0):
        return jnp.concatenate([jnp.broadcast_to(ba[:, c0 + h:c0 + h + 1], (tm, LANE)).reshape(nb, CHUNK, LANE)
                                for h in range(DN_HEADS)], axis=0)

    def per_head(v8):
        return jnp.concatenate([jnp.broadcast_to(v8[0:1, h:h + 1].reshape(1, 1, 1), (nb, 1, LANE)) for h in range(DN_HEADS)], axis=0)

    ri = lax.broadcasted_iota(jnp.int32, (bsz, CHUNK, CHUNK), 1)
    ci = lax.broadcasted_iota(jnp.int32, (bsz, CHUNK, CHUNK), 2)
    incl = ri >= ci
    strict = ri > ci

    q = _l2norm(heads(0)) * (DN_DK ** -0.5)
    k = _l2norm(heads(DN_QK))
    va = heads(2 * DN_QK)
    beta = _sigmoid(spread(0))
    g = -jnp.exp(per_head(al8)) * _softplus(spread(DN_HEADS) + per_head(dt8))
    gc = _bdot(incl.astype(F32), g, BNN, HI)
    g_last = jnp.sum(g, axis=1, keepdims=True)
    gcol = gc[:, :, :CHUNK]
    diff = gcol - jnp.swapaxes(gcol, 1, 2)
    decay = jnp.where(incl, jnp.exp(jnp.where(incl, diff, 0.0)), 0.0)
    kb = k * beta
    xm = -jnp.where(strict, _bmmb_nt(kb, k) * decay, 0.0)
    inv = (ri == ci).astype(F32) + xm
    for _ in range(int(math.log2(CHUNK)) - 1):
        xm = _bdot(xm, xm, BNN, lax.Precision.HIGH)
        inv = inv + _bdot(inv, xm, BNN, lax.Precision.HIGH)
    egc = jnp.exp(gc)
    wu = _bdot(inv, jnp.concatenate([kb * egc, va * beta], axis=2), BNN, lax.Precision.HIGH)
    attn = jnp.where(incl, _bmmb_nt(q, k) * decay, 0.0)

    def unheads(x):
        return jnp.concatenate([x[h * nb:(h + 1) * nb].reshape(tm, LANE) for h in range(DN_HEADS)], axis=1)

    return (unheads(wu[:, :, :DN_DK]), unheads(wu[:, :, DN_DK:]), unheads(q * egc), unheads(k * jnp.exp(g_last - gc)),
            attn.reshape(DN_HEADS, tm, CHUNK), unheads(jnp.broadcast_to(g_last, (bsz, CHUNK, LANE))))


def _gdn_scan_step(w, u, qg, kd, att, gl, s):
    v_new = u - _mmb(w, s)
    o = _mmb(qg, s) + _mmb(att, v_new)
    return o, s * jnp.exp(gl) + _mmb_tn(kd, v_new)


def _gdn_intra_specs(t, tm, dts):
    nb = tm // CHUNK
    specs = [pl.BlockSpec((tm, DN_VW), lambda i: (i, 0))] * 4
    specs += [pl.BlockSpec((DN_HEADS, tm, CHUNK), lambda i: (0, i, 0)), pl.BlockSpec((tm, DN_VW), lambda i: (i, 0))]
    shapes = [jax.ShapeDtypeStruct((t, DN_VW), dts[i]) for i in range(4)]
    shapes += [jax.ShapeDtypeStruct((DN_HEADS, t, CHUNK), dts[4]), jax.ShapeDtypeStruct((t, DN_VW), dts[5])]
    return specs, shapes


def _gdn_intra_fwd(qkv, proj, al8, dt8, tm):
    t = qkv.shape[0]

    def body(qkv_ref, ba_ref, al_ref, dt_ref, *outs):
        for o, val in zip(outs, _gdn_intra(qkv_ref[...], ba_ref[...], al_ref[...], dt_ref[...])):
            o[...] = val.astype(o.dtype)

    specs, shapes = _gdn_intra_specs(t, tm, (BF, F32, BF, BF, BF, F32))
    return pl.pallas_call(
        body, grid=(t // tm,),
        in_specs=[pl.BlockSpec((tm, DN_CONV_CH), lambda i: (i, 0)), pl.BlockSpec((tm, LANE), lambda i: (i, P_BA // LANE)),
                  _full(al8.shape), _full(dt8.shape)],
        out_specs=specs, out_shape=shapes, name="gdn_intra_fwd",
        compiler_params=pltpu.CompilerParams(dimension_semantics=("parallel",)),
    )(qkv, proj, al8, dt8)


def _gdn_intra_bwd(qkv, proj, al8, dt8, cts, tm):
    t = qkv.shape[0]

    def body(qkv_ref, ba_ref, al_ref, dt_ref, *refs):
        ct_refs, (dqkv_ref, dba_ref, dal_ref, ddt_ref) = refs[:6], refs[6:]

        @pl.when(pl.program_id(0) == 0)
        def _():
            dal_ref[...] = jnp.zeros_like(dal_ref)
            ddt_ref[...] = jnp.zeros_like(ddt_ref)

        _, vjp = jax.vjp(_gdn_intra, qkv_ref[...], ba_ref[...], al_ref[...], dt_ref[...])
        dqkv, dba, dal, ddt = vjp(tuple(r[...] for r in ct_refs))
        dqkv_ref[...] = dqkv
        dba_ref[...] = dba.astype(dba_ref.dtype)
        dal_ref[...] += dal
        ddt_ref[...] += ddt

    specs, _ = _gdn_intra_specs(t, tm, (F32,) * 6)
    return pl.pallas_call(
        body, grid=(t // tm,),
        in_specs=[pl.BlockSpec((tm, DN_CONV_CH), lambda i: (i, 0)), pl.BlockSpec((tm, LANE), lambda i: (i, P_BA // LANE)),
                  _full(al8.shape), _full(dt8.shape)] + specs,
        out_specs=[pl.BlockSpec((tm, DN_CONV_CH), lambda i: (i, 0)), pl.BlockSpec((tm, LANE), lambda i: (i, 0)),
                   _full(al8.shape), _full(dt8.shape)],
        out_shape=[jax.ShapeDtypeStruct((t, DN_CONV_CH), F32), jax.ShapeDtypeStruct((t, LANE), BF),
                   jax.ShapeDtypeStruct(al8.shape, F32), jax.ShapeDtypeStruct(dt8.shape, F32)],
        name="gdn_intra_bwd", compiler_params=pltpu.CompilerParams(dimension_semantics=("arbitrary",)),
    )(qkv, proj, al8, dt8, *cts)


def _gdn_scan_fwd(intra, tm):
    t = intra[0].shape[0]
    nb = tm // CHUNK
    nc = t // CHUNK

    def body(w_ref, u_ref, qg_ref, kd_ref, att_ref, gl_ref, o_ref, ss_ref, s_scr):
        @pl.when(pl.program_id(0) == 0)
        def _():
            s_scr[...] = jnp.zeros_like(s_scr)

        for cc in range(nb):
            rows = pl.ds(cc * CHUNK, CHUNK)
            for h in range(DN_HEADS):
                cols = pl.ds(h * DN_DV, DN_DV)
                s_prev = s_scr[h]
                ss_ref[cc, h] = s_prev
                o, s_new = _gdn_scan_step(w_ref[rows, cols], u_ref[rows, cols], qg_ref[rows, cols], kd_ref[rows, cols],
                                          att_ref[h, rows, :], gl_ref[pl.ds(cc * CHUNK, 1), cols], s_prev)
                o_ref[rows, cols] = o
                s_scr[h] = s_new

    specs, _ = _gdn_intra_specs(t, tm, (F32,) * 6)
    return pl.pallas_call(
        body, grid=(t // tm,), in_specs=specs,
        out_specs=[pl.BlockSpec((tm, DN_VW), lambda i: (i, 0)),
                   pl.BlockSpec((nb, DN_HEADS, DN_DK, DN_DV), lambda i: (i, 0, 0, 0))],
        out_shape=[jax.ShapeDtypeStruct((t, DN_VW), F32), jax.ShapeDtypeStruct((nc, DN_HEADS, DN_DK, DN_DV), F32)],
        scratch_shapes=[pltpu.VMEM((DN_HEADS, DN_DK, DN_DV), F32)], name="gdn_scan_fwd",
        compiler_params=pltpu.CompilerParams(dimension_semantics=("arbitrary",)),
    )(*intra)


def _gdn_scan_bwd(intra, states, do, tm):
    t = intra[0].shape[0]
    nb = tm // CHUNK
    ng = t // tm

    def body(w_ref, u_ref, qg_ref, kd_ref, att_ref, gl_ref, ss_ref, do_ref,
             dw_ref, du_ref, dqg_ref, dkd_ref, datt_ref, dgl_ref, ds_scr):
        @pl.when(pl.program_id(0) == 0)
        def _():
            ds_scr[...] = jnp.zeros_like(ds_scr)

        for cc in reversed(range(nb)):
            rows = pl.ds(cc * CHUNK, CHUNK)
            for h in range(DN_HEADS):
                cols = pl.ds(h * DN_DV, DN_DV)
                f32 = lambda r: r[rows, cols].astype(F32)
                _, vjp = jax.vjp(_gdn_scan_step, f32(w_ref), u_ref[rows, cols], f32(qg_ref), f32(kd_ref),
                                 att_ref[h, rows, :].astype(F32), gl_ref[pl.ds(cc * CHUNK, 1), cols], ss_ref[cc, h])
                dw, du, dqg, dkd, datt, dgl, ds_prev = vjp((do_ref[rows, cols], ds_scr[h]))
                dw_ref[rows, cols] = dw
                du_ref[rows, cols] = du
                dqg_ref[rows, cols] = dqg
                dkd_ref[rows, cols] = dkd
                datt_ref[h, rows, :] = datt
                first_row = lax.broadcasted_iota(jnp.int32, (CHUNK, DN_DV), 0) == 0
                dgl_ref[rows, cols] = jnp.where(first_row, dgl, 0.0)
                ds_scr[h] = ds_prev

    rev = lambda i: (ng - 1 - i, 0)
    rev3 = lambda i: (0, ng - 1 - i, 0)
    row = pl.BlockSpec((tm, DN_VW), rev)
    six = [row] * 4 + [pl.BlockSpec((DN_HEADS, tm, CHUNK), rev3), row]
    _, shapes = _gdn_intra_specs(t, tm, (F32,) * 6)
    return pl.pallas_call(
        body, grid=(ng,),
        in_specs=six + [pl.BlockSpec((nb, DN_HEADS, DN_DK, DN_DV), lambda i: (ng - 1 - i, 0, 0, 0)), row],
        out_specs=six, out_shape=shapes,
        scratch_shapes=[pltpu.VMEM((DN_HEADS, DN_DK, DN_DV), F32)], name="gdn_scan_bwd",
        compiler_params=pltpu.CompilerParams(dimension_semantics=("arbitrary",)),
    )(*intra, states, do)


def _gdn_out(o, z, g):
    parts = []
    for h in range(DN_HEADS):
        sl = slice(h * DN_DV, (h + 1) * DN_DV)
        parts.append(_rmsnorm(o[:, sl], g) * _silu(z[:, sl]))
    return parts


_Q_SCALE = math.log2(math.e) / math.sqrt(QK_NOPE + QK_ROPE)


def _rope_tables(pos, inv_freq2):
    lane = lax.broadcasted_iota(jnp.int32, (1, LANE), 1)
    ang = pos * inv_freq2
    cos = jnp.where(lane < QK_ROPE, jnp.cos(ang), 0.0)
    sin = jnp.where(lane < QK_ROPE // 2, -jnp.sin(ang), jnp.where(lane < QK_ROPE, jnp.sin(ang), 0.0))
    return cos, sin


def _rope_swap():
    ri = lax.broadcasted_iota(jnp.int32, (LANE, LANE), 0)
    ci = lax.broadcasted_iota(jnp.int32, (LANE, LANE), 1)
    half = QK_ROPE // 2
    return (((ci < half) & (ri == ci + half)) | ((ci >= half) & (ci < QK_ROPE) & (ri == ci - half))).astype(F32)


def _mla_prep(cq, ckv, kr, gq, gkv, w_uq, w_ukv, cos, sin, swap):
    rope = lambda u: u * cos + _doth(u, swap) * sin
    q_lin = _mmb(_rmsnorm(cq, gq), w_uq) * _Q_SCALE
    kv_lin = _mmb(_rmsnorm(ckv, gkv), w_ukv)
    k_rope = rope(kr)
    qs, ks, vs = [], [], []
    for h in range(MLA_HEADS):
        qs += [q_lin[:, h * LANE:(h + 1) * LANE], rope(q_lin[:, (MLA_HEADS + h) * LANE:(MLA_HEADS + h + 1) * LANE])]
        ks += [kv_lin[:, 2 * h * LANE:(2 * h + 1) * LANE], k_rope]
        vs += [kv_lin[:, (2 * h + 1) * LANE:(2 * h + 2) * LANE]]
    return qs + ks + vs


def _mla_prep_fwd(proj, pos_col, inv_freq2, gq, gkv, w_uq, w_ukv, tm):
    t = proj.shape[0]
    nq = 2 * MLA_HEADS

    def body(cq_ref, ckv_ref, kr_ref, pos_ref, f_ref, gq_ref, gkv_ref, wq_ref, wkv_ref, q_ref, k_ref, v_ref):
        cos, sin = _rope_tables(pos_ref[...], f_ref[...])
        outs = _mla_prep(cq_ref[...], ckv_ref[...], kr_ref[...], gq_ref[...], gkv_ref[...], wq_ref[...], wkv_ref[...],
                         cos, sin, _rope_swap())
        for i in range(nq):
            q_ref[:, pl.ds(i * LANE, LANE)] = outs[i].astype(q_ref.dtype)
            k_ref[:, pl.ds(i * LANE, LANE)] = outs[nq + i].astype(k_ref.dtype)
        for h in range(MLA_HEADS):
            v_ref[:, pl.ds(h * LANE, LANE)] = outs[2 * nq + h].astype(v_ref.dtype)

    row = lambda w, j: pl.BlockSpec((tm, w), functools.partial(lambda i, j: (i, j), j=j))
    return pl.pallas_call(
        body, grid=(t // tm,),
        in_specs=[row(Q_LORA, P_CQ // Q_LORA), row(KV_LORA, P_CKV // KV_LORA), row(LANE, P_KR // LANE),
                  pl.BlockSpec((tm, 1), lambda i: (i, 0)), _full(inv_freq2.shape), _full(gq.shape), _full(gkv.shape),
                  _full(w_uq.shape), _full(w_ukv.shape)],
        out_specs=[row(nq * LANE, 0), row(nq * LANE, 0), row(MLA_VW, 0)],
        out_shape=[jax.ShapeDtypeStruct((t, nq * LANE), BF), jax.ShapeDtypeStruct((t, nq * LANE), BF),
                   jax.ShapeDtypeStruct((t, MLA_VW), BF)],
        name="mla_prep_fwd", compiler_params=pltpu.CompilerParams(dimension_semantics=("arbitrary",)),
    )(proj, proj, proj, pos_col, inv_freq2, gq, gkv, w_uq, w_ukv)


def _mla_prep_bwd(proj, pos_col, inv_freq2, gq, gkv, w_uq, w_ukv, dq, dk, dv, tm):
    t = proj.shape[0]
    nq = 2 * MLA_HEADS

    def body(cq_ref, ckv_ref, kr_ref, pos_ref, f_ref, gq_ref, gkv_ref, wq_ref, wkv_ref, dq_ref, dk_ref, dv_ref,
             dcq_ref, dckv_ref, dkr_ref, dgq_ref, dgkv_ref, dwq_ref, dwkv_ref):
        @pl.when(pl.program_id(0) == 0)
        def _():
            for o in (dgq_ref, dgkv_ref, dwq_ref, dwkv_ref):
                o[...] = jnp.zeros_like(o)

        cos, sin = _rope_tables(pos_ref[...], f_ref[...])
        f = functools.partial(_mla_prep, cos=cos, sin=sin, swap=_rope_swap())
        _, vjp = jax.vjp(f, cq_ref[...], ckv_ref[...], kr_ref[...], gq_ref[...], gkv_ref[...], wq_ref[...], wkv_ref[...])
        cts = [dq_ref[:, pl.ds(i * LANE, LANE)] for i in range(nq)]
        cts += [dk_ref[:, pl.ds(i * LANE, LANE)] for i in range(nq)]
        cts += [dv_ref[:, pl.ds(h * LANE, LANE)] for h in range(MLA_HEADS)]
        dcq, dckv, dkr, dgq, dgkv, dwq, dwkv = vjp(cts)
        dcq_ref[...] = dcq.astype(dcq_ref.dtype)
        dckv_ref[...] = dckv.astype(dckv_ref.dtype)
        dkr_ref[...] = dkr.astype(dkr_ref.dtype)
        dgq_ref[...] += dgq
        dgkv_ref[...] += dgkv
        dwq_ref[...] += dwq
        dwkv_ref[...] += dwkv

    row = lambda w, j: pl.BlockSpec((tm, w), functools.partial(lambda i, j: (i, j), j=j))
    return pl.pallas_call(
        body, grid=(t // tm,),
        in_specs=[row(Q_LORA, P_CQ // Q_LORA), row(KV_LORA, P_CKV // KV_LORA), row(LANE, P_KR // LANE),
                  pl.BlockSpec((tm, 1), lambda i: (i, 0)), _full(inv_freq2.shape), _full(gq.shape), _full(gkv.shape),
                  _full(w_uq.shape), _full(w_ukv.shape), row(nq * LANE, 0), row(nq * LANE, 0), row(MLA_VW, 0)],
        out_specs=[row(Q_LORA, 0), row(KV_LORA, 0), row(LANE, 0), _full(gq.shape), _full(gkv.shape),
                   _full(w_uq.shape), _full(w_ukv.shape)],
        out_shape=[jax.ShapeDtypeStruct((t, Q_LORA), BF), jax.ShapeDtypeStruct((t, KV_LORA), BF),
                   jax.ShapeDtypeStruct((t, LANE), BF), jax.ShapeDtypeStruct(gq.shape, F32),
                   jax.ShapeDtypeStruct(gkv.shape, F32), jax.ShapeDtypeStruct(w_uq.shape, F32),
                   jax.ShapeDtypeStruct(w_ukv.shape, F32)],
        name="mla_prep_bwd", compiler_params=pltpu.CompilerParams(dimension_semantics=("arbitrary",)),
    )(proj, proj, proj, pos_col, inv_freq2, gq, gkv, w_uq, w_ukv, dq, dk, dv)


_NEG = -1e30
_LN2 = math.log(2.0)


def _causal(tq, tk, q0, k0):
    row = q0 + lax.broadcasted_iota(jnp.int32, (tq, tk), 0)
    col = k0 + lax.broadcasted_iota(jnp.int32, (tq, tk), 1)
    return col <= row


def _attn_fwd(q, k, v, tq, tk):
    t = q.shape[0]

    assert tk % tq == 0

    def body(q_ref, k_ref, v_ref, o_ref, lse_ref):
        i = pl.program_id(1)
        qt = q_ref[...]
        n_full = (i * tq) // tk

        def step(k0, carry, masked):
            m, l, acc = carry
            s = _dot(qt, k_ref[pl.ds(k0, tk), :], NT)
            if masked:
                s = jnp.where(_causal(tq, tk, i * tq, k0), s, _NEG)
            m_new = jnp.maximum(m, jnp.max(s, axis=-1, keepdims=True))
            p = jnp.exp2(s - m_new)
            alpha = jnp.exp2(m - m_new)
            l = alpha * l + jnp.sum(p, axis=-1, keepdims=True)
            acc = alpha * acc + _dot(p.astype(BF), v_ref[pl.ds(k0, tk), :])
            return m_new, l, acc

        init = (jnp.full((tq, 1), _NEG, F32), jnp.zeros((tq, 1), F32), jnp.zeros((tq, V_HEAD), F32))
        carry = lax.fori_loop(0, n_full, lambda j, c: step(pl.multiple_of(j * tk, tk), c, False), init)
        m, l, acc = step(pl.multiple_of(n_full * tk, tk), carry, True)
        o_ref[...] = acc / l
        lse_ref[...] = jnp.broadcast_to(m + jnp.log2(l), (tq, LANE))

    return pl.pallas_call(
        body, grid=(MLA_HEADS, t // tq),
        in_specs=[pl.BlockSpec((tq, 2 * LANE), lambda h, i: (i, h)), pl.BlockSpec((t, 2 * LANE), lambda h, i: (0, h)),
                  pl.BlockSpec((t, V_HEAD), lambda h, i: (0, h))],
        out_specs=[pl.BlockSpec((tq, V_HEAD), lambda h, i: (i, h)), pl.BlockSpec((tq, LANE), lambda h, i: (i, h))],
        out_shape=[jax.ShapeDtypeStruct((t, MLA_VW), F32), jax.ShapeDtypeStruct((t, MLA_HEADS * LANE), F32)],
        name="attn_fwd", compiler_params=pltpu.CompilerParams(dimension_semantics=("parallel", "arbitrary")),
    )(q, k, v)


def _attn_bwd(q, k, v, do, lse, delta, tq, tk):
    t = q.shape[0]
    nq = t // tq
    nkt = t // tk
    assert tk % tq == 0

    def body(q_ref, k_ref, v_ref, do_ref, lse_ref, dl_ref, dq_ref, dk_ref, dv_ref):
        j = pl.program_id(1)

        @pl.when(j == 0)
        def _():
            dq_ref[...] = jnp.zeros_like(dq_ref)

        kt = k_ref[...]
        vt = v_ref[...]

        def step(q0, carry, masked):
            dk, dv = carry
            rows = pl.ds(q0, tq)
            qt = q_ref[rows, :]
            dot_ = do_ref[rows, :]
            p = jnp.exp2(_dot(qt, kt, NT) - lse_ref[rows, pl.ds(0, 1)])
            if masked:
                p = jnp.where(_causal(tq, tk, q0, j * tk), p, 0.0)
            dv = dv + _dot(p.astype(BF), dot_, TN)
            ds = (p * (_dot(dot_, vt, NT) - dl_ref[rows, pl.ds(0, 1)])).astype(BF)
            dk = dk + _dot(ds, qt, TN)
            dq_ref[rows, :] += _dot(ds, kt)
            return dk, dv

        carry = (jnp.zeros((tk, 2 * LANE), F32), jnp.zeros((tk, V_HEAD), F32))
        for dd in range(tk // tq):
            carry = step(pl.multiple_of(j * tk + dd * tq, tq), carry, True)
        dk, dv = lax.fori_loop((j + 1) * (tk // tq), nq, lambda i, c: step(pl.multiple_of(i * tq, tq), c, False), carry)
        dk_ref[...] = dk * _LN2
        dv_ref[...] = dv

        @pl.when(j == nkt - 1)
        def _():
            dq_ref[...] = dq_ref[...] * _LN2

    return pl.pallas_call(
        body, grid=(MLA_HEADS, nkt),
        in_specs=[pl.BlockSpec((t, 2 * LANE), lambda h, j: (0, h)), pl.BlockSpec((tk, 2 * LANE), lambda h, j: (j, h)),
                  pl.BlockSpec((tk, V_HEAD), lambda h, j: (j, h)), pl.BlockSpec((t, V_HEAD), lambda h, j: (0, h)),
                  pl.BlockSpec((t, LANE), lambda h, j: (0, h)), pl.BlockSpec((t, LANE), lambda h, j: (0, h))],
        out_specs=[pl.BlockSpec((t, 2 * LANE), lambda h, j: (0, h)), pl.BlockSpec((tk, 2 * LANE), lambda h, j: (j, h)),
                   pl.BlockSpec((tk, V_HEAD), lambda h, j: (j, h))],
        out_shape=[jax.ShapeDtypeStruct((t, MLA_HEADS * 2 * LANE), F32), jax.ShapeDtypeStruct((t, MLA_HEADS * 2 * LANE), F32),
                   jax.ShapeDtypeStruct((t, MLA_VW), F32)],
        name="attn_bwd", compiler_params=pltpu.CompilerParams(dimension_semantics=("parallel", "arbitrary")),
    )(q, k, v, do, lse, delta)


def _adam_update(w, g, m, v):
    mm = ADAM_B1 * m + (1.0 - ADAM_B1) * g
    vv = ADAM_B2 * v + (1.0 - ADAM_B2) * jnp.square(g)
    m_hat = mm / (1.0 - ADAM_B1 ** ADAM_STEP)
    v_hat = vv / (1.0 - ADAM_B2 ** ADAM_STEP)
    return -ADAM_LR * (m_hat / (jnp.sqrt(v_hat) + ADAM_EPS) + ADAM_WD * w), mm, vv


def _adamw(w, g, m, v, name):
    r, c = w.shape
    tr = _pick(r, (256, 128, 64, 32, 16, 8))
    slots = g.ndim == 3

    def body(w_ref, g_ref, m_ref, v_ref, g_out, d_ref, nm_ref, nv_ref):
        if slots:
            gg = g_ref[0].astype(F32)
            for s in range(1, N_DEV):
                gg = gg + g_ref[s].astype(F32)
        else:
            gg = g_ref[...]
        g_out[...] = gg
        d_ref[...], nm_ref[...], nv_ref[...] = _adam_update(w_ref[...], gg, m_ref[...], v_ref[...])

    spec = pl.BlockSpec((tr, c), lambda i: (i, 0))
    g_spec = pl.BlockSpec((N_DEV, tr, c), lambda i: (0, i, 0)) if slots else spec
    return pl.pallas_call(
        body, grid=(r // tr,), in_specs=[spec, g_spec, spec, spec], out_specs=[spec] * 4,
        out_shape=[jax.ShapeDtypeStruct((r, c), F32)] * 4, name=name,
        compiler_params=pltpu.CompilerParams(dimension_semantics=("arbitrary",)),
    )(w, g, m, v)


def _adamw_many(ws, gs, ms, vs, name):
    n = len(ws)

    def body(*refs):
        for i in range(n):
            w_ref, g_ref, m_ref, v_ref = (refs[j * n + i] for j in range(4))
            d_ref, nm_ref, nv_ref = (refs[(4 + j) * n + i] for j in range(3))
            d_ref[...], nm_ref[...], nv_ref[...] = _adam_update(w_ref[...], g_ref[...], m_ref[...], v_ref[...])

    shapes = [jax.ShapeDtypeStruct(w.shape, F32) for w in ws]
    outs = pl.pallas_call(body, out_shape=shapes * 3, name=name)(*ws, *gs, *ms, *vs)
    return outs[:n], outs[n:2 * n], outs[2 * n:]


def _cast_bf16(xs, name, after=None):
    n = len(xs)
    extra = [] if after is None else [after]

    def body(*refs):
        outs = refs[n + len(extra):]
        for i in range(n):
            outs[i][...] = refs[i][...].astype(BF)

    vmem = pl.BlockSpec(memory_space=pltpu.VMEM)
    return pl.pallas_call(
        body, out_shape=[jax.ShapeDtypeStruct(x.shape, BF) for x in xs], name=name,
        in_specs=[vmem] * n + [pl.BlockSpec(memory_space=pl.ANY)] * len(extra), out_specs=[vmem] * n)(*xs, *extra)


def _pad_cols(a, n):
    return jnp.pad(a, ((0, 0), (0, n - a.shape[1])))


def _w_in_to_padded(w):
    s_ba = P_CQ
    s_cq = s_ba + 2 * DN_HEADS
    s_kr = s_cq + Q_LORA + KV_LORA
    return jnp.concatenate([w[:, :s_ba], w[:, s_cq:s_kr], _pad_cols(w[:, s_ba:s_cq], LANE), _pad_cols(w[:, s_kr:], LANE)], axis=1)


def _w_in_from_padded(w):
    return jnp.concatenate([w[:, :P_CQ], w[:, P_BA:P_BA + 2 * DN_HEADS], w[:, P_CQ:P_BA], w[:, P_KR:P_KR + QK_ROPE]], axis=1)


def _w_uq_to_padded(w):
    w3 = w.reshape(Q_LORA, MLA_HEADS, QK_NOPE + QK_ROPE)
    nope = w3[:, :, :QK_NOPE].reshape(Q_LORA, MLA_HEADS * QK_NOPE)
    rope = jnp.pad(w3[:, :, QK_NOPE:], ((0, 0), (0, 0), (0, LANE - QK_ROPE))).reshape(Q_LORA, MLA_HEADS * LANE)
    return jnp.concatenate([nope, rope], axis=1)


def _w_uq_from_padded(w):
    nope = w[:, :MLA_HEADS * QK_NOPE].reshape(Q_LORA, MLA_HEADS, QK_NOPE)
    rope = w[:, MLA_HEADS * QK_NOPE:].reshape(Q_LORA, MLA_HEADS, LANE)[:, :, :QK_ROPE]
    return jnp.concatenate([nope, rope], axis=2).reshape(Q_LORA, MLA_HEADS * (QK_NOPE + QK_ROPE))


def _pack(pieces, width, row_mult):
    flat = jnp.concatenate([p.reshape(-1) for p in pieces])
    n = flat.shape[0]
    rows = -(-n // (width * row_mult)) * row_mult
    return jnp.pad(flat, (0, rows * width - n)).reshape(rows, width)


def _unpack(flat, shapes):
    out, o = [], 0
    for s in shapes:
        n = math.prod(s)
        out.append(flat[o:o + n].reshape(s))
        o += n
    return out


def kernel(x, c, positions, w_ada, b_ada, w_in, conv_w, a_log, dt_bias, dn_norm_g, q_norm_g, w_uq, kv_norm_g, w_ukv, w_o, ln1_g, ln1_b, w_gate, w_up, w_down, ln2_g, ln2_b, loss_target, m_w_ada, m_b_ada, m_w_in, m_conv_w, m_a_log, m_dt_bias, m_dn_norm_g, m_q_norm_g, m_w_uq, m_kv_norm_g, m_w_ukv, m_w_o, m_ln1_g, m_ln1_b, m_w_gate, m_w_up, m_w_down, m_ln2_g, m_ln2_b, v_w_ada, v_b_ada, v_w_in, v_conv_w, v_a_log, v_dt_bias, v_dn_norm_g, v_q_norm_g, v_w_uq, v_kv_norm_g, v_w_ukv, v_w_o, v_ln1_g, v_ln1_b, v_w_gate, v_w_up, v_w_down, v_ln2_g, v_ln2_b):
    me = 4 * lax.axis_index("x") + 2 * lax.axis_index("y") + lax.axis_index("c")
    t, d = x.shape[1], x.shape[2]
    ff_n = w_gate.shape[2] * N_DEV
    ada_n = w_ada.shape[2]

    cw = conv_w.shape[3]
    c_all, conv_all = _exchange([c, conv_w[0, :, 0, :]], "gather_small", scatter=False)
    c_all = c_all.reshape(N_DEV, d)
    conv_full = conv_all.transpose(1, 0, 2).reshape(CONV_K, N_DEV * cw)
    conv_w8 = jnp.pad(conv_full, ((0, 8 - CONV_K), (0, 0)))

    b_ada_mine = lax.dynamic_slice(b_ada, (0, me * ada_n), (1, ada_n))
    mod_cols = _mod_fwd(c_all, w_ada[0], b_ada_mine)
    (mod_all,) = _exchange([mod_cols.reshape(N_DEV, 1, ada_n)], "scatter_mod", scatter=True)
    mod = mod_all.reshape(1, N_DEV * ada_n)

    shards = _cast_bf16([w_in[0], w_uq[0], w_ukv[0], w_o[0]], "cast_mixer_weights")
    a_in, a_uq, a_ukv, a_o = _exchange(shards, "gather_mixer_weights", scatter=False)
    ffn_shards = _cast_bf16([w_gate[0], w_up[0], w_down[0]], "cast_ffn_weights", after=a_in)
    ffn_gather = _exchange_start(ffn_shards, "gather_ffn_weights_start", scatter=False)
    cols = lambda a: a.transpose(1, 0, 2).reshape(a.shape[1], N_DEV * a.shape[2])
    w_in_p = _w_in_to_padded(cols(a_in))
    w_uq_p = _w_uq_to_padded(cols(a_uq))
    w_ukv_f = cols(a_ukv)
    w_o_f = a_o.reshape(-1, d)

    def ffn_weights(after):
        a_gate, a_up, a_down = _exchange_wait(ffn_gather, after, "gather_ffn_weights_wait", scatter=False)
        return jnp.concatenate([cols(a_gate), cols(a_up)], axis=1), a_down.reshape(-1, d)

    def by_dest_cols(g, n):
        return g.reshape(g.shape[0], N_DEV, n).transpose(1, 0, 2)

    def by_dest_rows(g, r):
        return g.reshape(N_DEV, r, g.shape[1])

    ffn_scatter = []

    def ffn_grads_ready(g_w_gu, g_w_down):
        pieces = [by_dest_cols(g_w_gu[:, :ff_n], w_gate.shape[2]), by_dest_cols(g_w_gu[:, ff_n:], w_up.shape[2]),
                  by_dest_rows(g_w_down, w_down.shape[1])]
        ffn_scatter.append(_exchange_start(pieces, "scatter_ffn_grads_start", scatter=True))

    loc = _local_step(x[0], loss_target[0], positions[0], mod, w_in_p, w_uq_p, w_ukv_f, w_o_f, ffn_weights, ffn_grads_ready,
                      conv_w8, a_log, dt_bias, dn_norm_g, q_norm_g, kv_norm_g, ln1_g, ln1_b, ln2_g, ln2_b)
    (grad_x, loss_acc, dmod, d_conv8, d_al8, d_dt8, d_dn_g, d_q_g, d_kv_g, d_ln1_g, d_ln1_b, d_ln2_g, d_ln2_b,
     g_w_in_p, g_w_uq_p, g_w_ukv, g_w_o) = loc

    g_w_in = _w_in_from_padded(g_w_in_p)
    g_w_uq = _w_uq_from_padded(g_w_uq_p).astype(BF)
    pieces = [by_dest_cols(g_w_in, w_in.shape[2]), by_dest_cols(g_w_uq, w_uq.shape[2]),
              by_dest_cols(g_w_ukv.astype(BF), w_ukv.shape[2]), by_dest_rows(g_w_o, w_o.shape[1])]
    slots = dict(zip(("w_in", "w_uq", "w_ukv", "w_o"), _exchange(pieces, "scatter_mixer_grads", scatter=True)))
    slots.update(zip(("w_gate", "w_up", "w_down"), _exchange_wait(ffn_scatter[0], slots["w_in"], "scatter_ffn_grads_wait", scatter=True)))

    small_shapes = [(6 * d,), (CONV_K, N_DEV * cw), (DN_HEADS,), (DN_HEADS,), (DN_DV,), (Q_LORA,), (KV_LORA,), (d,), (d,), (d,), (d,), (1,)]
    gsmall = _pack([dmod, d_conv8[:CONV_K], d_al8[0, :DN_HEADS], d_dt8[0, :DN_HEADS], d_dn_g, d_q_g, d_kv_g,
                    d_ln1_g, d_ln1_b, d_ln2_g, d_ln2_b, loss_acc[0, :1]], LANE, 8)
    (gsmall_all,) = _exchange([gsmall], "gather_small_grads", scatter=False)
    dmod_all = gsmall_all.reshape(N_DEV, -1)[:, :6 * d]
    tot = _unpack(_sum_slots(gsmall_all, "sum_small_grads").reshape(-1), small_shapes)
    g_b_ada, g_conv_full, g_a_log, g_dt_bias, g_dn_g, g_q_g, g_kv_g, g_ln1_g, g_ln1_b, g_ln2_g, g_ln2_b, loss1 = tot
    loss = loss1.reshape(())
    g_conv_w = lax.dynamic_slice(g_conv_full, (0, me * cw), (CONV_K, cw))
    g_w_ada = _mod_bwd(c_all, lax.dynamic_slice(dmod_all, (0, me * ada_n), (N_DEV, ada_n)))

    grads = {"w_ada": g_w_ada[None], "b_ada": g_b_ada[None], "conv_w": g_conv_w[None, :, None, :],
             "a_log": g_a_log[None], "dt_bias": g_dt_bias[None], "dn_norm_g": g_dn_g[None], "q_norm_g": g_q_g[None],
             "kv_norm_g": g_kv_g[None], "ln1_g": g_ln1_g[None], "ln1_b": g_ln1_b[None], "ln2_g": g_ln2_g[None], "ln2_b": g_ln2_b[None]}
    weights = dict(w_ada=w_ada, b_ada=b_ada, w_in=w_in, conv_w=conv_w, a_log=a_log, dt_bias=dt_bias, dn_norm_g=dn_norm_g,
                   q_norm_g=q_norm_g, w_uq=w_uq, kv_norm_g=kv_norm_g, w_ukv=w_ukv, w_o=w_o, ln1_g=ln1_g, ln1_b=ln1_b,
                   w_gate=w_gate, w_up=w_up, w_down=w_down, ln2_g=ln2_g, ln2_b=ln2_b)
    ms = dict(w_ada=m_w_ada, b_ada=m_b_ada, w_in=m_w_in, conv_w=m_conv_w, a_log=m_a_log, dt_bias=m_dt_bias,
              dn_norm_g=m_dn_norm_g, q_norm_g=m_q_norm_g, w_uq=m_w_uq, kv_norm_g=m_kv_norm_g, w_ukv=m_w_ukv, w_o=m_w_o,
              ln1_g=m_ln1_g, ln1_b=m_ln1_b, w_gate=m_w_gate, w_up=m_w_up, w_down=m_w_down, ln2_g=m_ln2_g, ln2_b=m_ln2_b)
    vs = dict(w_ada=v_w_ada, b_ada=v_b_ada, w_in=v_w_in, conv_w=v_conv_w, a_log=v_a_log, dt_bias=v_dt_bias,
              dn_norm_g=v_dn_norm_g, q_norm_g=v_q_norm_g, w_uq=v_w_uq, kv_norm_g=v_kv_norm_g, w_ukv=v_w_ukv, w_o=v_w_o,
              ln1_g=v_ln1_g, ln1_b=v_ln1_b, w_gate=v_w_gate, w_up=v_w_up, w_down=v_w_down, ln2_g=v_ln2_g, ln2_b=v_ln2_b)
    names = list(weights)
    big = ("w_ada", "w_in", "w_uq", "w_ukv", "w_o", "w_gate", "w_up", "w_down")
    delta_w, new_m, new_v = {}, {}, {}
    for n in big:
        shp = weights[n].shape
        two = lambda a: a.reshape(shp[-2], shp[-1])
        g_in = slots[n] if n in slots else two(grads[n])
        gr, dlt, nm, nv = _adamw(two(weights[n]), g_in, two(ms[n]), two(vs[n]), "adamw_" + n)
        grads[n], delta_w[n], new_m[n], new_v[n] = gr.reshape(shp), dlt.reshape(shp), nm.reshape(shp), nv.reshape(shp)
    rest = [n for n in names if n not in big]
    flat2 = lambda a: a.reshape(-1, a.shape[-1])
    outs = _adamw_many(*[[flat2(src[n]) for n in rest] for src in (weights, grads, ms, vs)], "adamw_small")
    for dst, o in zip((delta_w, new_m, new_v), outs):
        for n, a in zip(rest, o):
            dst[n] = a.reshape(weights[n].shape)

    return (loss, grad_x[None], *[grads[n] for n in names], *[delta_w[n] for n in names],
            *[new_m[n] for n in names], *[new_v[n] for n in names])


def _local_step(xs, tgt, pos, mod, w_in_p, w_uq_p, w_ukv_f, w_o_f, ffn_weights, ffn_grads_ready, conv_w8,
                a_log, dt_bias, dn_norm_g, q_norm_g, kv_norm_g, ln1_g, ln1_b, ln2_g, ln2_b):
    t, d = xs.shape
    sh_m, sc_m, gt_m, sh_f, sc_f, gt_f = [mod[:, i * d:(i + 1) * d] for i in range(6)]
    pos_col = pos.astype(F32).reshape(t, 1)
    inv_freq = 1.0 / (ROPE_THETA ** (jnp.arange(0, QK_ROPE, 2, dtype=F32) / QK_ROPE))
    inv_freq2 = jnp.pad(jnp.concatenate([inv_freq, inv_freq]), (0, LANE - QK_ROPE)).reshape(1, LANE)
    al8 = jnp.pad(a_log, ((0, 7), (0, LANE - DN_HEADS)))
    dt8 = jnp.pad(dt_bias, ((0, 7), (0, LANE - DN_HEADS)))

    tm = min(512, t)
    tq = min(256, t)
    tk = min(512, t)

    (h1,) = _rowwise("modulate_in", lambda xx, sc, sh: xx * (1.0 + sc) + sh, [xs], [sc_m, sh_m], [(d, BF)], [], tm)
    proj = _matmul(h1, w_in_p, "nn", "in_proj")
    qkv = _conv_fwd(proj, conv_w8, min(256, t))
    gdn_tm = min(512, t)
    intra = _gdn_intra_fwd(qkv, proj, al8, dt8, gdn_tm)
    o_dn, states = _gdn_scan_fwd(intra, gdn_tm)
    qc, kc, vc = _mla_prep_fwd(proj, pos_col, inv_freq2, q_norm_g, kv_norm_g, w_uq_p, w_ukv_f, tm)
    o_mla, lse = _attn_fwd(qc, kc, vc, tq, tk)

    def mix_in(o, z, om, g):
        return jnp.concatenate(_gdn_out(o, z, g) + [om], axis=1)

    (mixin,) = _rowwise("mixer_out", mix_in, [o_dn, (proj, DN_VW, P_Z // DN_VW), o_mla], [dn_norm_g], [(2 * DN_VW, BF)], [], tm)
    mix = _matmul(mixin, w_o_f, "nn", "out_proj")

    def block1(xx, mx, gt, g1, b1, sc, sh):
        x1 = _layernorm(DEEPNORM_ALPHA * xx + gt * mx, g1, b1)
        return x1, x1 * (1.0 + sc) + sh

    x1, h2 = _rowwise("norm1_modulate", block1, [xs, mix], [gt_m, ln1_g, ln1_b, sc_f, sh_f], [(d, F32), (d, BF)], [], tm)
    w_gu, w_down_f = ffn_weights(h2)
    ff_n = w_down_f.shape[0]
    gu = _matmul(h2, w_gu, "nn", "ffn_in")
    (act,) = _rowwise("swiglu", lambda gg, uu: _silu(gg) * uu, [(gu, ff_n, 0), (gu, ff_n, 1)], [], [(ff_n, BF)], [], min(256, t))
    ff = _matmul(act, w_down_f, "nn", "ffn_out")

    def tail_loss(x1_, ff_, gt, g2, b2, tg):
        y = _layernorm(DEEPNORM_ALPHA * x1_ + gt * ff_, g2, b2)
        return 0.5 * jnp.sum(jnp.mean(jnp.square(y - tg), axis=-1))

    def tail(x1_, ff_, tg, gt, g2, b2):
        loss, (dx1, dff, dgt, dg2, db2) = jax.value_and_grad(tail_loss, argnums=(0, 1, 2, 3, 4))(x1_, ff_, gt, g2, b2, tg)
        return dx1, dff, jnp.full((1, LANE), loss, F32), dgt, dg2, db2

    dx1_a, dff, loss_acc, d_gt_f, d_ln2_g, d_ln2_b = _rowwise(
        "norm2_loss", tail, [x1, ff, tgt], [gt_f, ln2_g, ln2_b], [(d, F32), (d, BF)], [(1, LANE), (1, d), (1, d), (1, d)], tm)

    dact = _matmul(dff, w_down_f, "nt", "d_ffn_act")
    g_w_down = _matmul(act, dff, "tn", "d_w_down", BF)

    def swiglu_bwd(gg, uu, da):
        _, vjp = jax.vjp(lambda a, b: _silu(a) * b, gg, uu)
        dg, du = vjp(da)
        return jnp.concatenate([dg, du], axis=1)

    (dgu,) = _rowwise("swiglu_bwd", swiglu_bwd, [(gu, ff_n, 0), (gu, ff_n, 1), dact], [], [(2 * ff_n, BF)], [], min(256, t))
    dh2 = _matmul(dgu, w_gu, "nt", "d_ffn_in")
    g_w_gu = _matmul(h2, dgu, "tn", "d_w_gate_up", BF)
    ffn_grads_ready(g_w_gu, g_w_down)

    def block1_bwd(xx, mx, dx1_, dh2_, gt, g1, b1, sc, sh):
        _, vjp = jax.vjp(block1, xx, mx, gt, g1, b1, sc, sh)
        dxx, dmx, dgt, dg1, db1, dsc, dsh = vjp((dx1_, dh2_))
        return dxx, dmx, dgt, dg1, db1, dsc, dsh

    dx_a, dmix, d_gt_m, d_ln1_g, d_ln1_b, d_sc_f, d_sh_f = _rowwise(
        "norm1_modulate_bwd", block1_bwd, [xs, mix, dx1_a, dh2], [gt_m, ln1_g, ln1_b, sc_f, sh_f],
        [(d, F32), (d, BF)], [(1, d)] * 5, min(256, t))

    dmixin = _matmul(dmix, w_o_f, "nt", "d_mixer_out")
    g_w_o = _matmul(mixin, dmix, "tn", "d_w_o", BF)

    def mixer_bwd(o, z, om, dmi, g):
        _, vjp = jax.vjp(lambda o_, z_, g_: jnp.concatenate(_gdn_out(o_, z_, g_), axis=1), o, z, g)
        do_, dz_, dg_ = vjp(dmi[:, :DN_VW])
        dom = dmi[:, DN_VW:]
        delta = [jnp.broadcast_to(jnp.sum(dom[:, h * V_HEAD:(h + 1) * V_HEAD] * om[:, h * V_HEAD:(h + 1) * V_HEAD], axis=-1, keepdims=True), (o.shape[0], LANE))
                 for h in range(MLA_HEADS)]
        return do_, dz_, dom, jnp.concatenate(delta, axis=1), dg_

    do_dn, dz, do_mla, delta, d_dn_g = _rowwise(
        "mixer_out_bwd", mixer_bwd, [o_dn, (proj, DN_VW, P_Z // DN_VW), o_mla, dmixin], [dn_norm_g],
        [(DN_VW, F32), (DN_VW, BF), (MLA_VW, BF), (MLA_HEADS * LANE, F32)], [(1, DN_DV)], tm)

    dqc, dkc, dvc = _attn_bwd(qc, kc, vc, do_mla, lse, delta, tq, tk)
    dcq, dckv, dkr, d_q_g, d_kv_g, g_w_uq_p, g_w_ukv = _mla_prep_bwd(
        proj, pos_col, inv_freq2, q_norm_g, kv_norm_g, w_uq_p, w_ukv_f, dqc, dkc, dvc, min(256, t))

    d_intra = _gdn_scan_bwd(intra, states, do_dn, gdn_tm)
    dqkv_act, dba, d_al8, d_dt8 = _gdn_intra_bwd(qkv, proj, al8, dt8, d_intra, min(256, t))
    dqkv_pre, d_conv8 = _conv_bwd(proj, conv_w8, dqkv_act, min(256, t))

    dproj = jnp.concatenate([dqkv_pre, dz, dcq, dckv, dba, dkr], axis=1)
    dh1 = _matmul(dproj, w_in_p, "nt", "d_in_proj")
    g_w_in_p = _matmul(h1, dproj, "tn", "d_w_in", BF)

    def modulate_bwd(xx, dh, dxa, sc):
        return dh * (1.0 + sc) + dxa, jnp.sum(dh * xx, axis=0, keepdims=True), jnp.sum(dh, axis=0, keepdims=True)

    grad_x, d_sc_m, d_sh_m = _rowwise("modulate_in_bwd", modulate_bwd, [xs, dh1, dx_a], [sc_m], [(d, F32)], [(1, d), (1, d)], tm)
    dmod = jnp.concatenate([d_sh_m, d_sc_m, d_gt_m, d_sh_f, d_sc_f, d_gt_f], axis=1)
    return (grad_x, loss_acc, dmod, d_conv8, d_al8, d_dt8, d_dn_g, d_q_g, d_kv_g, d_ln1_g, d_ln1_b, d_ln2_g, d_ln2_b,
            g_w_in_p, g_w_uq_p, g_w_ukv, g_w_o)
```

```python
import functools
import math

import jax
import jax.numpy as jnp
from jax import lax
from jax.experimental import pallas as pl
from jax.experimental.pallas import tpu as pltpu

F32 = jnp.float32
BF = jnp.bfloat16
HI = lax.Precision.HIGHEST

N_DEV = 8
DN_HEADS = 4
DN_DK = 128
DN_DV = 128
CONV_K = 4
CHUNK = 64
MLA_HEADS = 4
QK_NOPE = 128
QK_ROPE = 64
V_HEAD = 128
Q_LORA = 512
KV_LORA = 256
ROPE_THETA = 10000.0
DEPTH = 1
DEEPNORM_ALPHA = (2.0 * DEPTH) ** 0.25
LANE = 128
CONV_HALO = 8

DN_QK = DN_HEADS * DN_DK
DN_VW = DN_HEADS * DN_DV
DN_CONV_CH = 2 * DN_QK + DN_VW
MLA_VW = MLA_HEADS * V_HEAD
MLA_QCAT = QK_NOPE + LANE
N_IN = DN_CONV_CH + DN_VW + 2 * DN_HEADS + Q_LORA + KV_LORA + QK_ROPE
P_QKV = 0
P_Z = DN_CONV_CH
P_CQ = P_Z + DN_VW
P_CKV = P_CQ + Q_LORA
P_BA = P_CKV + KV_LORA
P_KR = P_BA + LANE
N_INP = P_KR + LANE

ADAM_LR = 0.001
ADAM_B1 = 0.9
ADAM_B2 = 0.999
ADAM_EPS = 1e-08
ADAM_WD = 0.01
ADAM_STEP = 10

NN = (((1,), (0,)), ((), ()))
NT = (((1,), (1,)), ((), ()))
TN = (((0,), (0,)), ((), ()))


def _pick(n, prefs):
    for p in prefs:
        if n % p == 0:
            return p
    return n


def _full(shape):
    return pl.BlockSpec(shape, lambda *_: (0,) * len(shape))


def _dot(a, b, dims=NN):
    return lax.dot_general(a, b, dims, preferred_element_type=F32)


def _doth(a, b, dims=NN):
    return lax.dot_general(a, b, dims, precision=HI, preferred_element_type=F32)


@jax.custom_vjp
def _mmb(a, b):
    return _dot(a.astype(BF), b.astype(BF), NN)


def _mmb_fwd(a, b):
    return _mmb(a, b), (a, b)


def _mmb_bwd(res, g):
    a, b = res
    gb = g.astype(BF)
    return (_dot(gb, b.astype(BF), NT).astype(a.dtype), _dot(a.astype(BF), gb, TN).astype(b.dtype))


_mmb.defvjp(_mmb_fwd, _mmb_bwd)


@jax.custom_vjp
def _mmb_nt(a, b):
    return _dot(a.astype(BF), b.astype(BF), NT)


def _mmb_nt_fwd(a, b):
    return _mmb_nt(a, b), (a, b)


def _mmb_nt_bwd(res, g):
    a, b = res
    gb = g.astype(BF)
    return (_dot(gb, b.astype(BF), NN).astype(a.dtype), _dot(gb, a.astype(BF), TN).astype(b.dtype))


_mmb_nt.defvjp(_mmb_nt_fwd, _mmb_nt_bwd)


@jax.custom_vjp
def _mmb_tn(a, b):
    return _dot(a.astype(BF), b.astype(BF), TN)


def _mmb_tn_fwd(a, b):
    return _mmb_tn(a, b), (a, b)


def _mmb_tn_bwd(res, g):
    a, b = res
    gb = g.astype(BF)
    return (_dot(b.astype(BF), gb, NT).astype(a.dtype), _dot(a.astype(BF), gb, NN).astype(b.dtype))


_mmb_tn.defvjp(_mmb_tn_fwd, _mmb_tn_bwd)


def _sigmoid(x):
    return 0.5 * (jnp.tanh(0.5 * x) + 1.0)


def _silu(x):
    return x * _sigmoid(x)


def _softplus(x):
    return jnp.maximum(x, 0.0) + jnp.log(1.0 + jnp.exp(-jnp.abs(x)))


def _layernorm(x, g, b, eps=1e-5):
    mu = jnp.mean(x, axis=-1, keepdims=True)
    xc = x - mu
    var = jnp.mean(xc * xc, axis=-1, keepdims=True)
    return xc * lax.rsqrt(var + eps) * g + b


def _rmsnorm(x, g, eps=1e-6):
    return x * lax.rsqrt(jnp.mean(x * x, axis=-1, keepdims=True) + eps) * g


def _l2norm(x, eps=1e-6):
    return x * lax.rsqrt(jnp.sum(x * x, axis=-1, keepdims=True) + eps)


def _rowwise(name, fn, rows, vecs, out_rows, out_accs, tm):
    rows = [r if isinstance(r, tuple) else (r, r.shape[1], 0) for r in rows]
    t = rows[0][0].shape[0]
    tm = min(tm, t)
    assert t % tm == 0
    nr, nv, no = len(rows), len(vecs), len(out_rows)

    def body(*refs):
        ins = [r[...] for r in refs[:nr + nv]]
        outs = fn(*ins)
        outs = outs if isinstance(outs, (tuple, list)) else (outs,)
        o_rows = refs[nr + nv:nr + nv + no]
        o_accs = refs[nr + nv + no:]
        for o, val in zip(o_rows, outs[:no]):
            o[...] = val.astype(o.dtype)
        if o_accs:
            @pl.when(pl.program_id(0) == 0)
            def _():
                for o in o_accs:
                    o[...] = jnp.zeros_like(o)
            for o, val in zip(o_accs, outs[no:]):
                o[...] += val

    in_specs = [pl.BlockSpec((tm, w), functools.partial(lambda i, j: (i, j), j=j)) for (_, w, j) in rows]
    in_specs += [_full(v.shape) for v in vecs]
    out_specs = [pl.BlockSpec((tm, w), lambda i: (i, 0)) for (w, _) in out_rows]
    out_specs += [_full(s) for s in out_accs]
    out_shape = [jax.ShapeDtypeStruct((t, w), d) for (w, d) in out_rows]
    out_shape += [jax.ShapeDtypeStruct(s, F32) for s in out_accs]
    res = pl.pallas_call(
        body, grid=(t // tm,), in_specs=in_specs, out_specs=out_specs, out_shape=out_shape, name=name,
        compiler_params=pltpu.CompilerParams(dimension_semantics=("arbitrary",)),
    )(*[r[0] for r in rows], *vecs)
    return res


def _matmul(a, b, mode, name, out_dtype=F32):
    if mode == "nn":
        (m, k), n = a.shape, b.shape[1]
    elif mode == "nt":
        (m, k), n = a.shape, b.shape[0]
    else:
        (k, m), n = a.shape, b.shape[1]
    tm, tn, tk = _matmul_tiles(m, n, k, a.dtype.itemsize, b.dtype.itemsize, jnp.dtype(out_dtype).itemsize)
    nk = k // tk
    dims = {"nn": NN, "nt": NT, "tn": TN}[mode]

    def body(a_ref, b_ref, o_ref, *acc):
        part = _dot(a_ref[...].astype(BF), b_ref[...].astype(BF), dims)
        if nk == 1:
            o_ref[...] = part.astype(o_ref.dtype)
            return
        (acc_ref,) = acc
        kk = pl.program_id(2)

        @pl.when(kk == 0)
        def _():
            acc_ref[...] = part

        @pl.when(kk > 0)
        def _():
            acc_ref[...] += part

        @pl.when(kk == nk - 1)
        def _():
            o_ref[...] = acc_ref[...].astype(o_ref.dtype)

    a_spec = pl.BlockSpec((tk, tm), lambda i, j, kk: (kk, i)) if mode == "tn" else pl.BlockSpec((tm, tk), lambda i, j, kk: (i, kk))
    b_spec = pl.BlockSpec((tn, tk), lambda i, j, kk: (j, kk)) if mode == "nt" else pl.BlockSpec((tk, tn), lambda i, j, kk: (kk, j))
    return pl.pallas_call(
        body, grid=(m // tm, n // tn, nk), in_specs=[a_spec, b_spec],
        out_specs=pl.BlockSpec((tm, tn), lambda i, j, kk: (i, j)),
        out_shape=jax.ShapeDtypeStruct((m, n), out_dtype),
        scratch_shapes=[pltpu.VMEM((tm, tn), F32)] if nk > 1 else [], name=name,
        compiler_params=pltpu.CompilerParams(dimension_semantics=("parallel", "parallel", "arbitrary")),
    )(a, b)


MATMUL_VMEM_BUDGET = 28 * 1024 * 1024


def _matmul_tiles(m, n, k, a_bytes, b_bytes, o_bytes):
    def divisors(x, cap):
        return sorted({x // s for s in range(1, 65) if x % s == 0 and (x // s) % LANE == 0 and x // s <= cap}, reverse=True) or [x]

    for tk in divisors(k, k):
        best = None
        for tm in divisors(m, 1024):
            for tn in divisors(n, 2048):
                need = 2 * (tm * tk * a_bytes + tk * tn * b_bytes + tm * tn * o_bytes) + (tm * tn * 4 if tk < k else 0)
                if need <= MATMUL_VMEM_BUDGET and tm * tn >= 512 * 512 and (best is None or tm * tn > best[0] * best[1]):
                    best = (tm, tn)
        if best:
            return best[0], best[1], tk
    return _pick(m, (512, 256, 128)), _pick(n, (512, 256, 128)), _pick(k, (512, 256, 128))


def _exchange(xs, name, scatter):
    n = len(xs)
    npeer = N_DEV - 1

    def body(*refs):
        x_refs, o_refs = refs[:n], refs[n:2 * n]
        send_sems, recv_sems, local_sems = refs[2 * n:]
        mx, my, mc = lax.axis_index("x"), lax.axis_index("y"), lax.axis_index("c")
        me = 4 * mx + 2 * my + mc
        src_me = [x.at[me] if scatter else x for x in x_refs]
        mine = [pltpu.make_async_copy(src_me[a], o_refs[a].at[me], local_sems.at[a]) for a in range(n)]
        for cp in mine:
            cp.start()
        copies = []
        for k in range(1, N_DEV):
            px, py, pc = mx ^ (k >> 2), my ^ ((k >> 1) & 1), mc ^ (k & 1)
            peer = 4 * px + 2 * py + pc
            for a in range(n):
                cp = pltpu.make_async_remote_copy(
                    src_ref=x_refs[a].at[peer] if scatter else x_refs[a], dst_ref=o_refs[a].at[me],
                    send_sem=send_sems.at[a * npeer + k - 1], recv_sem=recv_sems.at[a * npeer + k - 1],
                    device_id=(px, py, pc), device_id_type=pl.DeviceIdType.MESH)
                cp.start()
                copies.append((cp, a, k, peer))
        for cp, a, k, peer in copies:
            pltpu.make_async_remote_copy(
                src_ref=src_me[a], dst_ref=o_refs[a].at[peer], send_sem=send_sems.at[a * npeer + k - 1],
                recv_sem=recv_sems.at[a * npeer + k - 1], device_id=(mx, my, mc),
                device_id_type=pl.DeviceIdType.MESH).wait_recv()
        for cp, _, _, _ in copies:
            cp.wait_send()
        for cp in mine:
            cp.wait()

    return pl.pallas_call(
        body, out_shape=[jax.ShapeDtypeStruct((N_DEV,) + x.shape[-2:], x.dtype) for x in xs],
        in_specs=[pl.BlockSpec(memory_space=pl.ANY)] * n, out_specs=[pl.BlockSpec(memory_space=pl.ANY)] * n,
        scratch_shapes=[pltpu.SemaphoreType.DMA((n * npeer,)), pltpu.SemaphoreType.DMA((n * npeer,)),
                        pltpu.SemaphoreType.DMA((n,))],
        name=name,
    )(*xs)


def _peer_of(k):
    mx, my, mc = lax.axis_index("x"), lax.axis_index("y"), lax.axis_index("c")
    px, py, pc = mx ^ (k >> 2), my ^ ((k >> 1) & 1), mc ^ (k & 1)
    return (px, py, pc), 4 * px + 2 * py + pc


def _exchange_start(xs, name, scatter):
    n = len(xs)
    npeer = N_DEV - 1

    def body(*refs):
        x_refs, land_refs = refs[:n], refs[n:2 * n]
        send_sems, recv_sems, token = refs[2 * n], refs[2 * n + 1], refs[-1]
        me = 4 * lax.axis_index("x") + 2 * lax.axis_index("y") + lax.axis_index("c")
        for k in range(1, N_DEV):
            dev, peer = _peer_of(k)
            for a in range(n):
                pltpu.make_async_remote_copy(
                    src_ref=x_refs[a].at[peer] if scatter else x_refs[a], dst_ref=land_refs[a].at[me],
                    send_sem=send_sems.at[a * npeer + k - 1], recv_sem=recv_sems.at[a * npeer + k - 1],
                    device_id=dev, device_id_type=pl.DeviceIdType.MESH).start()
        token[...] = jnp.zeros_like(token)

    hbm = pl.BlockSpec(memory_space=pltpu.HBM)
    sem = pl.BlockSpec(memory_space=pltpu.SEMAPHORE)
    lands = [pltpu.with_memory_space_constraint(lax.empty((N_DEV,) + x.shape[-2:], x.dtype), pltpu.HBM) for x in xs]
    srcs = [pltpu.with_memory_space_constraint(x, pltpu.HBM) for x in xs]
    outs = pl.pallas_call(
        body, name=name,
        out_shape=(pltpu.SemaphoreType.DMA((n * npeer,)), pltpu.SemaphoreType.DMA((n * npeer,)),
                   *[pltpu.HBM(x.shape, x.dtype) for x in srcs], *[pltpu.HBM(z.shape, z.dtype) for z in lands],
                   jax.ShapeDtypeStruct((8, LANE), F32)),
        in_specs=[hbm] * (2 * n), out_specs=(sem, sem, *[hbm] * (2 * n), pl.BlockSpec(memory_space=pltpu.VMEM)),
        input_output_aliases={i: 2 + i for i in range(2 * n)},
        compiler_params=pltpu.CompilerParams(has_side_effects=pltpu.SideEffectType.DATAFLOW_SIDE_EFFECTING),
    )(*srcs, *lands)
    return (outs[0], outs[1], list(outs[2:2 + n]), list(outs[2 + n:2 + 2 * n])), outs[-1][0:1, 0:1]


def _exchange_wait(started, after, name, scatter):
    send_sems, recv_sems, srcs, lands = started
    n = len(srcs)
    npeer = N_DEV - 1

    def body(*refs):
        x_refs, land_refs = refs[:n], refs[n:2 * n]
        send_sems, recv_sems = refs[2 * n], refs[2 * n + 1]
        mx, my, mc = lax.axis_index("x"), lax.axis_index("y"), lax.axis_index("c")
        me = 4 * mx + 2 * my + mc
        for k in range(1, N_DEV):
            _, peer = _peer_of(k)
            for a in range(n):
                src = x_refs[a].at[me] if scatter else x_refs[a]
                cp = pltpu.make_async_remote_copy(
                    src_ref=src, dst_ref=land_refs[a].at[peer], send_sem=send_sems.at[a * npeer + k - 1],
                    recv_sem=recv_sems.at[a * npeer + k - 1], device_id=(mx, my, mc), device_id_type=pl.DeviceIdType.MESH)
                cp.wait_send()
                cp.wait_recv()

    hbm = pl.BlockSpec(memory_space=pltpu.HBM)
    sem = pl.BlockSpec(memory_space=pltpu.SEMAPHORE)
    outs = pl.pallas_call(
        body, name=name,
        out_shape=(*[pltpu.HBM(x.shape, x.dtype) for x in srcs], *[pltpu.HBM(z.shape, z.dtype) for z in lands]),
        in_specs=[hbm] * (2 * n) + [sem, sem, pl.BlockSpec(memory_space=pl.ANY)], out_specs=tuple([hbm] * (2 * n)),
        input_output_aliases={i: i for i in range(2 * n)},
        compiler_params=pltpu.CompilerParams(has_side_effects=pltpu.SideEffectType.DATAFLOW_SIDE_EFFECTING),
    )(*srcs, *lands, send_sems, recv_sems, after)
    me = 4 * lax.axis_index("x") + 2 * lax.axis_index("y") + lax.axis_index("c")
    full = []
    for x, land in zip(outs[:n], outs[n:]):
        own = lax.dynamic_slice(x, (me, 0, 0), (1,) + x.shape[1:]) if scatter else x[None]
        full.append(lax.dynamic_update_slice(land, own, (me, 0, 0)))
    return full


def _sum_slots(x, name):
    _, r, c = x.shape
    tr = _pick(r, (512, 256, 128, 64, 32, 16))

    def body(x_ref, o_ref):
        acc = x_ref[0].astype(F32)
        for s in range(1, N_DEV):
            acc = acc + x_ref[s].astype(F32)
        o_ref[...] = acc

    return pl.pallas_call(
        body, grid=(r // tr,), in_specs=[pl.BlockSpec((N_DEV, tr, c), lambda i: (0, i, 0))],
        out_specs=pl.BlockSpec((tr, c), lambda i: (i, 0)), out_shape=jax.ShapeDtypeStruct((r, c), F32), name=name,
        compiler_params=pltpu.CompilerParams(dimension_semantics=("arbitrary",)),
    )(x)


def _mod_fwd(c_all, w_ada, b_ada_mine):
    def body(c_ref, w_ref, b_ref, o_ref):
        o_ref[...] = _doth(_silu(c_ref[...]), w_ref[...]) + b_ref[...]

    return pl.pallas_call(body, out_shape=jax.ShapeDtypeStruct((c_all.shape[0], w_ada.shape[1]), F32), name="mod_fwd")(c_all, w_ada, b_ada_mine)


def _mod_bwd(c_all_t, dmod_mine):
    def body(ct_ref, d_ref, o_ref):
        s = _silu(ct_ref[...])
        acc = s[:, 0:1] * d_ref[pl.ds(0, 1), :]
        for b in range(1, N_DEV):
            acc = acc + s[:, b:b + 1] * d_ref[pl.ds(b, 1), :]
        o_ref[...] = acc

    return pl.pallas_call(body, out_shape=jax.ShapeDtypeStruct((c_all_t.shape[0], dmod_mine.shape[1]), F32), name="mod_bwd")(c_all_t, dmod_mine)


def _conv_fwd(proj, conv_w8, tm):
    t = proj.shape[0]
    ch = DN_CONV_CH

    def body(x_ref, w_ref, o_ref, buf):
        @pl.when(pl.program_id(0) == 0)
        def _():
            buf[pl.ds(0, CONV_HALO), :] = jnp.zeros((CONV_HALO, ch), F32)

        buf[pl.ds(CONV_HALO, tm), :] = x_ref[...]
        acc = jnp.zeros((tm, ch), F32)
        for j in range(CONV_K):
            acc = acc + buf[pl.ds(CONV_HALO - (CONV_K - 1) + j, tm), :] * w_ref[pl.ds(j, 1), :]
        o_ref[...] = _silu(acc)
        buf[pl.ds(0, CONV_HALO), :] = buf[pl.ds(tm, CONV_HALO), :]

    return pl.pallas_call(
        body, grid=(t // tm,), in_specs=[pl.BlockSpec((tm, ch), lambda i: (i, 0)), _full(conv_w8.shape)],
        out_specs=pl.BlockSpec((tm, ch), lambda i: (i, 0)), out_shape=jax.ShapeDtypeStruct((t, ch), F32),
        scratch_shapes=[pltpu.VMEM((tm + CONV_HALO, ch), F32)], name="conv_fwd",
        compiler_params=pltpu.CompilerParams(dimension_semantics=("arbitrary",)),
    )(proj, conv_w8)


def _conv_bwd(proj, conv_w8, dact, tm):
    t = proj.shape[0]
    ch = DN_CONV_CH
    nt = t // tm
    hb = tm // CONV_HALO

    def body(x_ref, xp_ref, w_ref, dy_ref, dx_ref, dw_ref, xbuf, dbuf):
        step = pl.program_id(0)

        @pl.when(step == 0)
        def _():
            dbuf[pl.ds(tm, CONV_HALO), :] = jnp.zeros((CONV_HALO, ch), F32)
            dw_ref[...] = jnp.zeros_like(dw_ref)

        first = step == nt - 1
        xbuf[pl.ds(0, CONV_HALO), :] = jnp.where(first, 0.0, xp_ref[...])
        xbuf[pl.ds(CONV_HALO, tm), :] = x_ref[...]
        pre = jnp.zeros((tm, ch), F32)
        for j in range(CONV_K):
            pre = pre + xbuf[pl.ds(CONV_HALO - (CONV_K - 1) + j, tm), :] * w_ref[pl.ds(j, 1), :]
        sg = _sigmoid(pre)
        dpre = dy_ref[...] * (sg * (1.0 + pre * (1.0 - sg)))
        dbuf[pl.ds(0, tm), :] = dpre
        dx = jnp.zeros((tm, ch), F32)
        for j in range(CONV_K):
            dx = dx + dbuf[pl.ds(CONV_K - 1 - j, tm), :] * w_ref[pl.ds(j, 1), :]
            dw_ref[pl.ds(j, 1), :] += jnp.sum(dpre * xbuf[pl.ds(CONV_HALO - (CONV_K - 1) + j, tm), :], axis=0, keepdims=True)
        dx_ref[...] = dx.astype(dx_ref.dtype)
        dbuf[pl.ds(tm, CONV_HALO), :] = dbuf[pl.ds(0, CONV_HALO), :]

    rev = lambda i: (nt - 1 - i, 0)
    prev = lambda i: (jnp.maximum((nt - 1 - i) * hb - 1, 0), 0)
    return pl.pallas_call(
        body, grid=(nt,),
        in_specs=[pl.BlockSpec((tm, ch), rev), pl.BlockSpec((CONV_HALO, ch), prev), _full(conv_w8.shape),
                  pl.BlockSpec((tm, ch), rev)],
        out_specs=[pl.BlockSpec((tm, ch), rev), _full(conv_w8.shape)],
        out_shape=[jax.ShapeDtypeStruct((t, ch), BF), jax.ShapeDtypeStruct(conv_w8.shape, F32)],
        scratch_shapes=[pltpu.VMEM((tm + CONV_HALO, ch), F32), pltpu.VMEM((tm + CONV_HALO, ch), F32)], name="conv_bwd",
        compiler_params=pltpu.CompilerParams(dimension_semantics=("arbitrary",)),
    )(proj, proj, conv_w8, dact)


BNN = (((2,), (1,)), ((0,), (0,)))
BNT = (((2,), (2,)), ((0,), (0,)))
BTN = (((1,), (1,)), ((0,), (0,)))


def _bdot(a, b, dims, precision=None):
    return lax.dot_general(a, b, dims, precision=precision, preferred_element_type=F32)


@jax.custom_vjp
def _bmmb_nt(a, b):
    return _bdot(a.astype(BF), b.astype(BF), BNT)


def _bmmb_nt_fwd(a, b):
    return _bmmb_nt(a, b), (a, b)


def _bmmb_nt_bwd(res, g):
    a, b = res
    gb = g.astype(BF)
    return _bdot(gb, b.astype(BF), BNN), _bdot(gb, a.astype(BF), BTN)


_bmmb_nt.defvjp(_bmmb_nt_fwd, _bmmb_nt_bwd)


def _gdn_intra(qkv, ba, al8, dt8):
    tm = qkv.shape[0]
    nb = tm // CHUNK
    bsz = DN_HEADS * nb

    def heads(x0):
        return jnp.concatenate([qkv[:, x0 + h * LANE:x0 + (h + 1) * LANE].reshape(nb, CHUNK, LANE) for h in range(DN_HEADS)], axis=0)

    def spread(c0):
        return jnp.concatenate([jnp.broadcast_to(ba[:, c0 + h:c0 + h + 1], (tm, LANE)).reshape(nb, CHUNK, LANE)
                                for h in range(DN_HEADS)], axis=0)

    def per_head(v8):
        return jnp.concatenate([jnp.broadcast_to(v8[0:1, h:h + 1].reshape(1, 1, 1), (nb, 1, LANE)) for h in range(DN_HEADS)], axis=0)

    ri = lax.broadcasted_iota(jnp.int32, (bsz, CHUNK, CHUNK), 1)
    ci = lax.broadcasted_iota(jnp.int32, (bsz, CHUNK, CHUNK), 2)
    incl = ri >= ci
    strict = ri > ci

    q = _l2norm(heads(0)) * (DN_DK ** -0.5)
    k = _l2norm(heads(DN_QK))
    va = heads(2 * DN_QK)
    beta = _sigmoid(spread(0))
    g = -jnp.exp(per_head(al8)) * _softplus(spread(DN_HEADS) + per_head(dt8))
    gc = _bdot(incl.astype(F32), g, BNN, HI)
    g_last = jnp.sum(g, axis=1, keepdims=True)
    gcol = gc[:, :, :CHUNK]
    diff = gcol - jnp.swapaxes(gcol, 1, 2)
    decay = jnp.where(incl, jnp.exp(jnp.where(incl, diff, 0.0)), 0.0)
    kb = k * beta
    xm = -jnp.where(strict, _bmmb_nt(kb, k) * decay, 0.0)
    inv = (ri == ci).astype(F32) + xm
    for _ in range(int(math.log2(CHUNK)) - 1):
        xm = _bdot(xm, xm, BNN, HI)
        inv = inv + _bdot(inv, xm, BNN, HI)
    egc = jnp.exp(gc)
    wu = _bdot(inv, jnp.concatenate([kb * egc, va * beta], axis=2), BNN, HI)
    attn = jnp.where(incl, _bmmb_nt(q, k) * decay, 0.0)

    def unheads(x):
        return jnp.concatenate([x[h * nb:(h + 1) * nb].reshape(tm, LANE) for h in range(DN_HEADS)], axis=1)

    return (unheads(wu[:, :, :DN_DK]), unheads(wu[:, :, DN_DK:]), unheads(q * egc), unheads(k * jnp.exp(g_last - gc)),
            attn.reshape(DN_HEADS, tm, CHUNK), unheads(jnp.broadcast_to(g_last, (bsz, CHUNK, LANE))))


def _gdn_scan_step(w, u, qg, kd, att, gl, s):
    v_new = u - _mmb(w, s)
    o = _mmb(qg, s) + _mmb(att, v_new)
    return o, s * jnp.exp(gl) + _mmb_tn(kd, v_new)


def _gdn_intra_specs(t, tm, dts):
    nb = tm // CHUNK
    specs = [pl.BlockSpec((tm, DN_VW), lambda i: (i, 0))] * 4
    specs += [pl.BlockSpec((DN_HEADS, tm, CHUNK), lambda i: (0, i, 0)), pl.BlockSpec((tm, DN_VW), lambda i: (i, 0))]
    shapes = [jax.ShapeDtypeStruct((t, DN_VW), dts[i]) for i in range(4)]
    shapes += [jax.ShapeDtypeStruct((DN_HEADS, t, CHUNK), dts[4]), jax.ShapeDtypeStruct((t, DN_VW), dts[5])]
    return specs, shapes


def _gdn_intra_fwd(qkv, proj, al8, dt8, tm):
    t = qkv.shape[0]

    def body(qkv_ref, ba_ref, al_ref, dt_ref, *outs):
        for o, val in zip(outs, _gdn_intra(qkv_ref[...], ba_ref[...], al_ref[...], dt_ref[...])):
            o[...] = val.astype(o.dtype)

    specs, shapes = _gdn_intra_specs(t, tm, (BF, F32, BF, BF, BF, F32))
    return pl.pallas_call(
        body, grid=(t // tm,),
        in_specs=[pl.BlockSpec((tm, DN_CONV_CH), lambda i: (i, 0)), pl.BlockSpec((tm, LANE), lambda i: (i, P_BA // LANE)),
                  _full(al8.shape), _full(dt8.shape)],
        out_specs=specs, out_shape=shapes, name="gdn_intra_fwd",
        compiler_params=pltpu.CompilerParams(dimension_semantics=("parallel",)),
    )(qkv, proj, al8, dt8)


def _gdn_intra_bwd(qkv, proj, al8, dt8, cts, tm):
    t = qkv.shape[0]

    def body(qkv_ref, ba_ref, al_ref, dt_ref, *refs):
        ct_refs, (dqkv_ref, dba_ref, dal_ref, ddt_ref) = refs[:6], refs[6:]

        @pl.when(pl.program_id(0) == 0)
        def _():
            dal_ref[...] = jnp.zeros_like(dal_ref)
            ddt_ref[...] = jnp.zeros_like(ddt_ref)

        _, vjp = jax.vjp(_gdn_intra, qkv_ref[...], ba_ref[...], al_ref[...], dt_ref[...])
        dqkv, dba, dal, ddt = vjp(tuple(r[...] for r in ct_refs))
        dqkv_ref[...] = dqkv
        dba_ref[...] = dba.astype(dba_ref.dtype)
        dal_ref[...] += dal
        ddt_ref[...] += ddt

    specs, _ = _gdn_intra_specs(t, tm, (F32,) * 6)
    return pl.pallas_call(
        body, grid=(t // tm,),
        in_specs=[pl.BlockSpec((tm, DN_CONV_CH), lambda i: (i, 0)), pl.BlockSpec((tm, LANE), lambda i: (i, P_BA // LANE)),
                  _full(al8.shape), _full(dt8.shape)] + specs,
        out_specs=[pl.BlockSpec((tm, DN_CONV_CH), lambda i: (i, 0)), pl.BlockSpec((tm, LANE), lambda i: (i, 0)),
                   _full(al8.shape), _full(dt8.shape)],
        out_shape=[jax.ShapeDtypeStruct((t, DN_CONV_CH), F32), jax.ShapeDtypeStruct((t, LANE), BF),
                   jax.ShapeDtypeStruct(al8.shape, F32), jax.ShapeDtypeStruct(dt8.shape, F32)],
        name="gdn_intra_bwd", compiler_params=pltpu.CompilerParams(dimension_semantics=("arbitrary",)),
    )(qkv, proj, al8, dt8, *cts)


def _gdn_scan_fwd(intra, tm):
    t = intra[0].shape[0]
    nb = tm // CHUNK
    nc = t // CHUNK

    def body(w_ref, u_ref, qg_ref, kd_ref, att_ref, gl_ref, o_ref, ss_ref, s_scr):
        @pl.when(pl.program_id(0) == 0)
        def _():
            s_scr[...] = jnp.zeros_like(s_scr)

        for cc in range(nb):
            rows = pl.ds(cc * CHUNK, CHUNK)
            for h in range(DN_HEADS):
                cols = pl.ds(h * DN_DV, DN_DV)
                s_prev = s_scr[h]
                ss_ref[cc, h] = s_prev
                o, s_new = _gdn_scan_step(w_ref[rows, cols], u_ref[rows, cols], qg_ref[rows, cols], kd_ref[rows, cols],
                                          att_ref[h, rows, :], gl_ref[pl.ds(cc * CHUNK, 1), cols], s_prev)
                o_ref[rows, cols] = o
                s_scr[h] = s_new

    specs, _ = _gdn_intra_specs(t, tm, (F32,) * 6)
    return pl.pallas_call(
        body, grid=(t // tm,), in_specs=specs,
        out_specs=[pl.BlockSpec((tm, DN_VW), lambda i: (i, 0)),
                   pl.BlockSpec((nb, DN_HEADS, DN_DK, DN_DV), lambda i: (i, 0, 0, 0))],
        out_shape=[jax.ShapeDtypeStruct((t, DN_VW), F32), jax.ShapeDtypeStruct((nc, DN_HEADS, DN_DK, DN_DV), F32)],
        scratch_shapes=[pltpu.VMEM((DN_HEADS, DN_DK, DN_DV), F32)], name="gdn_scan_fwd",
        compiler_params=pltpu.CompilerParams(dimension_semantics=("arbitrary",)),
    )(*intra)


def _gdn_scan_bwd(intra, states, do, tm):
    t = intra[0].shape[0]
    nb = tm // CHUNK
    ng = t // tm

    def body(w_ref, u_ref, qg_ref, kd_ref, att_ref, gl_ref, ss_ref, do_ref,
             dw_ref, du_ref, dqg_ref, dkd_ref, datt_ref, dgl_ref, ds_scr):
        @pl.when(pl.program_id(0) == 0)
        def _():
            ds_scr[...] = jnp.zeros_like(ds_scr)

        for cc in reversed(range(nb)):
            rows = pl.ds(cc * CHUNK, CHUNK)
            for h in range(DN_HEADS):
                cols = pl.ds(h * DN_DV, DN_DV)
                f32 = lambda r: r[rows, cols].astype(F32)
                _, vjp = jax.vjp(_gdn_scan_step, f32(w_ref), u_ref[rows, cols], f32(qg_ref), f32(kd_ref),
                                 att_ref[h, rows, :].astype(F32), gl_ref[pl.ds(cc * CHUNK, 1), cols], ss_ref[cc, h])
                dw, du, dqg, dkd, datt, dgl, ds_prev = vjp((do_ref[rows, cols], ds_scr[h]))
                dw_ref[rows, cols] = dw
                du_ref[rows, cols] = du
                dqg_ref[rows, cols] = dqg
                dkd_ref[rows, cols] = dkd
                datt_ref[h, rows, :] = datt
                first_row = lax.broadcasted_iota(jnp.int32, (CHUNK, DN_DV), 0) == 0
                dgl_ref[rows, cols] = jnp.where(first_row, dgl, 0.0)
                ds_scr[h] = ds_prev

    rev = lambda i: (ng - 1 - i, 0)
    rev3 = lambda i: (0, ng - 1 - i, 0)
    row = pl.BlockSpec((tm, DN_VW), rev)
    six = [row] * 4 + [pl.BlockSpec((DN_HEADS, tm, CHUNK), rev3), row]
    _, shapes = _gdn_intra_specs(t, tm, (F32,) * 6)
    return pl.pallas_call(
        body, grid=(ng,),
        in_specs=six + [pl.BlockSpec((nb, DN_HEADS, DN_DK, DN_DV), lambda i: (ng - 1 - i, 0, 0, 0)), row],
        out_specs=six, out_shape=shapes,
        scratch_shapes=[pltpu.VMEM((DN_HEADS, DN_DK, DN_DV), F32)], name="gdn_scan_bwd",
        compiler_params=pltpu.CompilerParams(dimension_semantics=("arbitrary",)),
    )(*intra, states, do)


def _gdn_out(o, z, g):
    parts = []
    for h in range(DN_HEADS):
        sl = slice(h * DN_DV, (h + 1) * DN_DV)
        parts.append(_rmsnorm(o[:, sl], g) * _silu(z[:, sl]))
    return parts


_Q_SCALE = math.log2(math.e) / math.sqrt(QK_NOPE + QK_ROPE)


def _rope_tables(pos, inv_freq2):
    lane = lax.broadcasted_iota(jnp.int32, (1, LANE), 1)
    ang = pos * inv_freq2
    cos = jnp.where(lane < QK_ROPE, jnp.cos(ang), 0.0)
    sin = jnp.where(lane < QK_ROPE // 2, -jnp.sin(ang), jnp.where(lane < QK_ROPE, jnp.sin(ang), 0.0))
    return cos, sin


def _rope_swap():
    ri = lax.broadcasted_iota(jnp.int32, (LANE, LANE), 0)
    ci = lax.broadcasted_iota(jnp.int32, (LANE, LANE), 1)
    half = QK_ROPE // 2
    return (((ci < half) & (ri == ci + half)) | ((ci >= half) & (ci < QK_ROPE) & (ri == ci - half))).astype(F32)


def _mla_prep(cq, ckv, kr, gq, gkv, w_uq, w_ukv, cos, sin, swap):
    rope = lambda u: u * cos + _doth(u, swap) * sin
    q_lin = _mmb(_rmsnorm(cq, gq), w_uq) * _Q_SCALE
    kv_lin = _mmb(_rmsnorm(ckv, gkv), w_ukv)
    k_rope = rope(kr)
    qs, ks, vs = [], [], []
    for h in range(MLA_HEADS):
        qs += [q_lin[:, h * LANE:(h + 1) * LANE], rope(q_lin[:, (MLA_HEADS + h) * LANE:(MLA_HEADS + h + 1) * LANE])]
        ks += [kv_lin[:, 2 * h * LANE:(2 * h + 1) * LANE], k_rope]
        vs += [kv_lin[:, (2 * h + 1) * LANE:(2 * h + 2) * LANE]]
    return qs + ks + vs


def _mla_prep_fwd(proj, pos_col, inv_freq2, gq, gkv, w_uq, w_ukv, tm):
    t = proj.shape[0]
    nq = 2 * MLA_HEADS

    def body(cq_ref, ckv_ref, kr_ref, pos_ref, f_ref, gq_ref, gkv_ref, wq_ref, wkv_ref, q_ref, k_ref, v_ref):
        cos, sin = _rope_tables(pos_ref[...], f_ref[...])
        outs = _mla_prep(cq_ref[...], ckv_ref[...], kr_ref[...], gq_ref[...], gkv_ref[...], wq_ref[...], wkv_ref[...],
                         cos, sin, _rope_swap())
        for i in range(nq):
            q_ref[:, pl.ds(i * LANE, LANE)] = outs[i].astype(q_ref.dtype)
            k_ref[:, pl.ds(i * LANE, LANE)] = outs[nq + i].astype(k_ref.dtype)
        for h in range(MLA_HEADS):
            v_ref[:, pl.ds(h * LANE, LANE)] = outs[2 * nq + h].astype(v_ref.dtype)

    row = lambda w, j: pl.BlockSpec((tm, w), functools.partial(lambda i, j: (i, j), j=j))
    return pl.pallas_call(
        body, grid=(t // tm,),
        in_specs=[row(Q_LORA, P_CQ // Q_LORA), row(KV_LORA, P_CKV // KV_LORA), row(LANE, P_KR // LANE),
                  pl.BlockSpec((tm, 1), lambda i: (i, 0)), _full(inv_freq2.shape), _full(gq.shape), _full(gkv.shape),
                  _full(w_uq.shape), _full(w_ukv.shape)],
        out_specs=[row(nq * LANE, 0), row(nq * LANE, 0), row(MLA_VW, 0)],
        out_shape=[jax.ShapeDtypeStruct((t, nq * LANE), BF), jax.ShapeDtypeStruct((t, nq * LANE), BF),
                   jax.ShapeDtypeStruct((t, MLA_VW), BF)],
        name="mla_prep_fwd", compiler_params=pltpu.CompilerParams(dimension_semantics=("arbitrary",)),
    )(proj, proj, proj, pos_col, inv_freq2, gq, gkv, w_uq, w_ukv)


def _mla_prep_bwd(proj, pos_col, inv_freq2, gq, gkv, w_uq, w_ukv, dq, dk, dv, tm):
    t = proj.shape[0]
    nq = 2 * MLA_HEADS

    def body(cq_ref, ckv_ref, kr_ref, pos_ref, f_ref, gq_ref, gkv_ref, wq_ref, wkv_ref, dq_ref, dk_ref, dv_ref,
             dcq_ref, dckv_ref, dkr_ref, dgq_ref, dgkv_ref, dwq_ref, dwkv_ref):
        @pl.when(pl.program_id(0) == 0)
        def _():
            for o in (dgq_ref, dgkv_ref, dwq_ref, dwkv_ref):
                o[...] = jnp.zeros_like(o)

        cos, sin = _rope_tables(pos_ref[...], f_ref[...])
        f = functools.partial(_mla_prep, cos=cos, sin=sin, swap=_rope_swap())
        _, vjp = jax.vjp(f, cq_ref[...], ckv_ref[...], kr_ref[...], gq_ref[...], gkv_ref[...], wq_ref[...], wkv_ref[...])
        cts = [dq_ref[:, pl.ds(i * LANE, LANE)] for i in range(nq)]
        cts += [dk_ref[:, pl.ds(i * LANE, LANE)] for i in range(nq)]
        cts += [dv_ref[:, pl.ds(h * LANE, LANE)] for h in range(MLA_HEADS)]
        dcq, dckv, dkr, dgq, dgkv, dwq, dwkv = vjp(cts)
        dcq_ref[...] = dcq.astype(dcq_ref.dtype)
        dckv_ref[...] = dckv.astype(dckv_ref.dtype)
        dkr_ref[...] = dkr.astype(dkr_ref.dtype)
        dgq_ref[...] += dgq
        dgkv_ref[...] += dgkv
        dwq_ref[...] += dwq
        dwkv_ref[...] += dwkv

    row = lambda w, j: pl.BlockSpec((tm, w), functools.partial(lambda i, j: (i, j), j=j))
    return pl.pallas_call(
        body, grid=(t // tm,),
        in_specs=[row(Q_LORA, P_CQ // Q_LORA), row(KV_LORA, P_CKV // KV_LORA), row(LANE, P_KR // LANE),
                  pl.BlockSpec((tm, 1), lambda i: (i, 0)), _full(inv_freq2.shape), _full(gq.shape), _full(gkv.shape),
                  _full(w_uq.shape), _full(w_ukv.shape), row(nq * LANE, 0), row(nq * LANE, 0), row(MLA_VW, 0)],
        out_specs=[row(Q_LORA, 0), row(KV_LORA, 0), row(LANE, 0), _full(gq.shape), _full(gkv.shape),
                   _full(w_uq.shape), _full(w_ukv.shape)],
        out_shape=[jax.ShapeDtypeStruct((t, Q_LORA), BF), jax.ShapeDtypeStruct((t, KV_LORA), BF),
                   jax.ShapeDtypeStruct((t, LANE), BF), jax.ShapeDtypeStruct(gq.shape, F32),
                   jax.ShapeDtypeStruct(gkv.shape, F32), jax.ShapeDtypeStruct(w_uq.shape, F32),
                   jax.ShapeDtypeStruct(w_ukv.shape, F32)],
        name="mla_prep_bwd", compiler_params=pltpu.CompilerParams(dimension_semantics=("arbitrary",)),
    )(proj, proj, proj, pos_col, inv_freq2, gq, gkv, w_uq, w_ukv, dq, dk, dv)


_NEG = -1e30
_LN2 = math.log(2.0)
ATT_CHAINS = 2


def _causal(tq, tk, q0, k0):
    row = q0 + lax.broadcasted_iota(jnp.int32, (tq, tk), 0)
    col = k0 + lax.broadcasted_iota(jnp.int32, (tq, tk), 1)
    return col <= row


def _attn_fwd(q, k, v, tq, tk):
    t = q.shape[0]

    assert tk % tq == 0

    th = tq // ATT_CHAINS

    def body(q_ref, k_ref, v_ref, o_ref, lse_ref):
        i = pl.program_id(1)
        n_full = (i * tq) // tk

        def step(k0, carry, masked):
            kt = k_ref[pl.ds(k0, tk), :]
            vt = v_ref[pl.ds(k0, tk), :]
            out = []
            for c, (m, l, acc) in enumerate(carry):
                s = _dot(q_ref[pl.ds(c * th, th), :], kt, NT)
                if masked:
                    s = jnp.where(_causal(th, tk, i * tq + c * th, k0), s, _NEG)
                m_new = jnp.maximum(m, jnp.max(s, axis=-1, keepdims=True))
                p = jnp.exp2(s - m_new)
                alpha = jnp.exp2(m - m_new)
                out.append((m_new, alpha * l + jnp.sum(p, axis=-1, keepdims=True), alpha * acc + _dot(p.astype(BF), vt)))
            return tuple(out)

        init = tuple((jnp.full((th, 1), _NEG, F32), jnp.zeros((th, 1), F32), jnp.zeros((th, V_HEAD), F32)) for _ in range(ATT_CHAINS))
        carry = lax.fori_loop(0, n_full, lambda j, c: step(pl.multiple_of(j * tk, tk), c, False), init)
        for c, (m, l, acc) in enumerate(step(pl.multiple_of(n_full * tk, tk), carry, True)):
            o_ref[pl.ds(c * th, th), :] = acc / l
            lse_ref[pl.ds(c * th, th), :] = jnp.broadcast_to(m + jnp.log2(l), (th, LANE))

    return pl.pallas_call(
        body, grid=(MLA_HEADS, t // tq),
        in_specs=[pl.BlockSpec((tq, 2 * LANE), lambda h, i: (i, h)), pl.BlockSpec((t, 2 * LANE), lambda h, i: (0, h)),
                  pl.BlockSpec((t, V_HEAD), lambda h, i: (0, h))],
        out_specs=[pl.BlockSpec((tq, V_HEAD), lambda h, i: (i, h)), pl.BlockSpec((tq, LANE), lambda h, i: (i, h))],
        out_shape=[jax.ShapeDtypeStruct((t, MLA_VW), F32), jax.ShapeDtypeStruct((t, MLA_HEADS * LANE), F32)],
        name="attn_fwd", compiler_params=pltpu.CompilerParams(dimension_semantics=("parallel", "arbitrary")),
    )(q, k, v)


def _attn_bwd(q, k, v, do, lse, delta, tq, tk):
    t = q.shape[0]
    nq = t // tq
    nkt = t // tk
    assert tk % tq == 0

    def body(q_ref, k_ref, v_ref, do_ref, lse_ref, dl_ref, dq_ref, dk_ref, dv_ref):
        j = pl.program_id(1)

        @pl.when(j == 0)
        def _():
            dq_ref[...] = jnp.zeros_like(dq_ref)

        kt = k_ref[...]
        vt = v_ref[...]

        def step(q0, carry, masked):
            dk, dv = carry
            rows = pl.ds(q0, tq)
            qt = q_ref[rows, :]
            dot_ = do_ref[rows, :]
            p = jnp.exp2(_dot(qt, kt, NT) - lse_ref[rows, pl.ds(0, 1)])
            if masked:
                p = jnp.where(_causal(tq, tk, q0, j * tk), p, 0.0)
            dv = dv + _dot(p.astype(BF), dot_, TN)
            ds = (p * (_dot(dot_, vt, NT) - dl_ref[rows, pl.ds(0, 1)])).astype(BF)
            dk = dk + _dot(ds, qt, TN)
            dq_ref[rows, :] += _dot(ds, kt)
            return dk, dv

        per = tk // tq
        carry = (jnp.zeros((tk, 2 * LANE), F32), jnp.zeros((tk, V_HEAD), F32))
        for dd in range(per):
            carry = step(pl.multiple_of(j * tk + dd * tq, tq), carry, True)

        def group(g, c):
            for dd in range(per):
                c = step(pl.multiple_of(g * tk + dd * tq, tq), c, False)
            return c

        dk, dv = lax.fori_loop(j + 1, nkt, group, carry)
        dk_ref[...] = dk * _LN2
        dv_ref[...] = dv

        @pl.when(j == nkt - 1)
        def _():
            dq_ref[...] = dq_ref[...] * _LN2

    return pl.pallas_call(
        body, grid=(MLA_HEADS, nkt),
        in_specs=[pl.BlockSpec((t, 2 * LANE), lambda h, j: (0, h)), pl.BlockSpec((tk, 2 * LANE), lambda h, j: (j, h)),
                  pl.BlockSpec((tk, V_HEAD), lambda h, j: (j, h)), pl.BlockSpec((t, V_HEAD), lambda h, j: (0, h)),
                  pl.BlockSpec((t, LANE), lambda h, j: (0, h)), pl.BlockSpec((t, LANE), lambda h, j: (0, h))],
        out_specs=[pl.BlockSpec((t, 2 * LANE), lambda h, j: (0, h)), pl.BlockSpec((tk, 2 * LANE), lambda h, j: (j, h)),
                   pl.BlockSpec((tk, V_HEAD), lambda h, j: (j, h))],
        out_shape=[jax.ShapeDtypeStruct((t, MLA_HEADS * 2 * LANE), F32), jax.ShapeDtypeStruct((t, MLA_HEADS * 2 * LANE), F32),
                   jax.ShapeDtypeStruct((t, MLA_VW), F32)],
        name="attn_bwd", compiler_params=pltpu.CompilerParams(dimension_semantics=("parallel", "arbitrary")),
    )(q, k, v, do, lse, delta)


def _adam_update(w, g, m, v):
    mm = ADAM_B1 * m + (1.0 - ADAM_B1) * g
    vv = ADAM_B2 * v + (1.0 - ADAM_B2) * jnp.square(g)
    m_hat = mm / (1.0 - ADAM_B1 ** ADAM_STEP)
    v_hat = vv / (1.0 - ADAM_B2 ** ADAM_STEP)
    return -ADAM_LR * (m_hat / (jnp.sqrt(v_hat) + ADAM_EPS) + ADAM_WD * w), mm, vv


def _adamw(w, g, m, v, name):
    r, c = w.shape
    tr = _pick(r, (256, 128, 64, 32, 16, 8))
    slots = g.ndim == 3

    def body(w_ref, g_ref, m_ref, v_ref, g_out, d_ref, nm_ref, nv_ref):
        if slots:
            gg = g_ref[0].astype(F32)
            for s in range(1, N_DEV):
                gg = gg + g_ref[s].astype(F32)
        else:
            gg = g_ref[...]
        g_out[...] = gg
        d_ref[...], nm_ref[...], nv_ref[...] = _adam_update(w_ref[...], gg, m_ref[...], v_ref[...])

    spec = pl.BlockSpec((tr, c), lambda i: (i, 0))
    g_spec = pl.BlockSpec((N_DEV, tr, c), lambda i: (0, i, 0)) if slots else spec
    return pl.pallas_call(
        body, grid=(r // tr,), in_specs=[spec, g_spec, spec, spec], out_specs=[spec] * 4,
        out_shape=[jax.ShapeDtypeStruct((r, c), F32)] * 4, name=name,
        compiler_params=pltpu.CompilerParams(dimension_semantics=("arbitrary",)),
    )(w, g, m, v)


def _adamw_many(ws, gs, ms, vs, name):
    n = len(ws)

    def body(*refs):
        for i in range(n):
            w_ref, g_ref, m_ref, v_ref = (refs[j * n + i] for j in range(4))
            d_ref, nm_ref, nv_ref = (refs[(4 + j) * n + i] for j in range(3))
            d_ref[...], nm_ref[...], nv_ref[...] = _adam_update(w_ref[...], g_ref[...], m_ref[...], v_ref[...])

    shapes = [jax.ShapeDtypeStruct(w.shape, F32) for w in ws]
    outs = pl.pallas_call(body, out_shape=shapes * 3, name=name)(*ws, *gs, *ms, *vs)
    return outs[:n], outs[n:2 * n], outs[2 * n:]


def _cast_bf16(xs, name, after=None):
    n = len(xs)
    extra = [] if after is None else [after]

    def body(*refs):
        outs = refs[n + len(extra):]
        for i in range(n):
            outs[i][...] = refs[i][...].astype(BF)

    vmem = pl.BlockSpec(memory_space=pltpu.VMEM)
    return pl.pallas_call(
        body, out_shape=[jax.ShapeDtypeStruct(x.shape, BF) for x in xs], name=name,
        in_specs=[vmem] * n + [pl.BlockSpec(memory_space=pl.ANY)] * len(extra), out_specs=[vmem] * n)(*xs, *extra)


def _pad_cols(a, n):
    return jnp.pad(a, ((0, 0), (0, n - a.shape[1])))


def _w_in_to_padded(w):
    s_ba = P_CQ
    s_cq = s_ba + 2 * DN_HEADS
    s_kr = s_cq + Q_LORA + KV_LORA
    return jnp.concatenate([w[:, :s_ba], w[:, s_cq:s_kr], _pad_cols(w[:, s_ba:s_cq], LANE), _pad_cols(w[:, s_kr:], LANE)], axis=1)


def _w_in_from_padded(w):
    return jnp.concatenate([w[:, :P_CQ], w[:, P_BA:P_BA + 2 * DN_HEADS], w[:, P_CQ:P_BA], w[:, P_KR:P_KR + QK_ROPE]], axis=1)


def _w_uq_to_padded(w):
    w3 = w.reshape(Q_LORA, MLA_HEADS, QK_NOPE + QK_ROPE)
    nope = w3[:, :, :QK_NOPE].reshape(Q_LORA, MLA_HEADS * QK_NOPE)
    rope = jnp.pad(w3[:, :, QK_NOPE:], ((0, 0), (0, 0), (0, LANE - QK_ROPE))).reshape(Q_LORA, MLA_HEADS * LANE)
    return jnp.concatenate([nope, rope], axis=1)


def _w_uq_from_padded(w):
    nope = w[:, :MLA_HEADS * QK_NOPE].reshape(Q_LORA, MLA_HEADS, QK_NOPE)
    rope = w[:, MLA_HEADS * QK_NOPE:].reshape(Q_LORA, MLA_HEADS, LANE)[:, :, :QK_ROPE]
    return jnp.concatenate([nope, rope], axis=2).reshape(Q_LORA, MLA_HEADS * (QK_NOPE + QK_ROPE))


def _pack(pieces, width, row_mult):
    flat = jnp.concatenate([p.reshape(-1) for p in pieces])
    n = flat.shape[0]
    rows = -(-n // (width * row_mult)) * row_mult
    return jnp.pad(flat, (0, rows * width - n)).reshape(rows, width)


def _unpack(flat, shapes):
    out, o = [], 0
    for s in shapes:
        n = math.prod(s)
        out.append(flat[o:o + n].reshape(s))
        o += n
    return out


def kernel(x, c, positions, w_ada, b_ada, w_in, conv_w, a_log, dt_bias, dn_norm_g, q_norm_g, w_uq, kv_norm_g, w_ukv, w_o, ln1_g, ln1_b, w_gate, w_up, w_down, ln2_g, ln2_b, loss_target, m_w_ada, m_b_ada, m_w_in, m_conv_w, m_a_log, m_dt_bias, m_dn_norm_g, m_q_norm_g, m_w_uq, m_kv_norm_g, m_w_ukv, m_w_o, m_ln1_g, m_ln1_b, m_w_gate, m_w_up, m_w_down, m_ln2_g, m_ln2_b, v_w_ada, v_b_ada, v_w_in, v_conv_w, v_a_log, v_dt_bias, v_dn_norm_g, v_q_norm_g, v_w_uq, v_kv_norm_g, v_w_ukv, v_w_o, v_ln1_g, v_ln1_b, v_w_gate, v_w_up, v_w_down, v_ln2_g, v_ln2_b):
    me = 4 * lax.axis_index("x") + 2 * lax.axis_index("y") + lax.axis_index("c")
    t, d = x.shape[1], x.shape[2]
    ff_n = w_gate.shape[2] * N_DEV
    ada_n = w_ada.shape[2]

    cw = conv_w.shape[3]
    c_all, conv_all = _exchange([c, conv_w[0, :, 0, :]], "gather_small", scatter=False)
    c_all = c_all.reshape(N_DEV, d)
    conv_full = conv_all.transpose(1, 0, 2).reshape(CONV_K, N_DEV * cw)
    conv_w8 = jnp.pad(conv_full, ((0, 8 - CONV_K), (0, 0)))

    b_ada_mine = lax.dynamic_slice(b_ada, (0, me * ada_n), (1, ada_n))
    mod_cols = _mod_fwd(c_all, w_ada[0], b_ada_mine)
    (mod_all,) = _exchange([mod_cols.reshape(N_DEV, 1, ada_n)], "scatter_mod", scatter=True)
    mod = mod_all.reshape(1, N_DEV * ada_n)

    (in_shard,) = _cast_bf16([w_in[0]], "cast_w_in")
    (a_in,) = _exchange([in_shard], "gather_w_in", scatter=False)
    later = _cast_bf16([w_uq[0], w_ukv[0], w_o[0], w_gate[0], w_up[0], w_down[0]], "cast_weights", after=a_in)
    mixer_gather, _ = _exchange_start(later[:3], "gather_mixer_weights_start", scatter=False)
    ffn_gather, _ = _exchange_start(later[3:], "gather_ffn_weights_start", scatter=False)
    cols = lambda a: a.transpose(1, 0, 2).reshape(a.shape[1], N_DEV * a.shape[2])
    w_in_p = _w_in_to_padded(cols(a_in))

    def mixer_weights(after):
        a_uq, a_ukv, a_o = _exchange_wait(mixer_gather, after, "gather_mixer_weights_wait", scatter=False)
        return _w_uq_to_padded(cols(a_uq)), cols(a_ukv), a_o.reshape(-1, d)

    def ffn_weights(after):
        a_gate, a_up, a_down = _exchange_wait(ffn_gather, after, "gather_ffn_weights_wait", scatter=False)
        return jnp.concatenate([cols(a_gate), cols(a_up)], axis=1), a_down.reshape(-1, d)

    def by_dest_cols(g, n):
        return g.reshape(g.shape[0], N_DEV, n).transpose(1, 0, 2)

    def by_dest_rows(g, r):
        return g.reshape(N_DEV, r, g.shape[1])

    scatters = {}

    def grads_ready(tag, *g):
        if tag == "ffn":
            g_w_gu, g_w_down = g
            pieces = [by_dest_cols(g_w_gu[:, :ff_n], w_gate.shape[2]), by_dest_cols(g_w_gu[:, ff_n:], w_up.shape[2]),
                      by_dest_rows(g_w_down, w_down.shape[1])]
        elif tag == "mixer":
            g_w_o, g_w_uq_p, g_w_ukv = g
            pieces = [by_dest_rows(g_w_o, w_o.shape[1]), by_dest_cols(_w_uq_from_padded(g_w_uq_p).astype(BF), w_uq.shape[2]),
                      by_dest_cols(g_w_ukv.astype(BF), w_ukv.shape[2])]
        else:
            pieces = [by_dest_cols(_w_in_from_padded(g[0]), w_in.shape[2])]
        scatters[tag], token = _exchange_start(pieces, "scatter_%s_grads_start" % tag, scatter=True)
        return token

    loc = _local_step(x[0], loss_target[0], positions[0], mod, w_in_p, mixer_weights, ffn_weights, grads_ready,
                      conv_w8, a_log, dt_bias, dn_norm_g, q_norm_g, kv_norm_g, ln1_g, ln1_b, ln2_g, ln2_b)
    grad_x, loss_acc, dmod, d_conv8, d_al8, d_dt8, d_dn_g, d_q_g, d_kv_g, d_ln1_g, d_ln1_b, d_ln2_g, d_ln2_b = loc

    small_shapes = [(6 * d,), (CONV_K, N_DEV * cw), (DN_HEADS,), (DN_HEADS,), (DN_DV,), (Q_LORA,), (KV_LORA,), (d,), (d,), (d,), (d,), (1,)]
    gsmall = _pack([dmod, d_conv8[:CONV_K], d_al8[0, :DN_HEADS], d_dt8[0, :DN_HEADS], d_dn_g, d_q_g, d_kv_g,
                    d_ln1_g, d_ln1_b, d_ln2_g, d_ln2_b, loss_acc[0, :1]], LANE, 8)
    (gsmall_all,) = _exchange([gsmall], "gather_small_grads", scatter=False)
    dmod_all = gsmall_all.reshape(N_DEV, -1)[:, :6 * d]
    tot = _unpack(_sum_slots(gsmall_all, "sum_small_grads").reshape(-1), small_shapes)
    g_b_ada, g_conv_full, g_a_log, g_dt_bias, g_dn_g, g_q_g, g_kv_g, g_ln1_g, g_ln1_b, g_ln2_g, g_ln2_b, loss1 = tot
    loss = loss1.reshape(())
    g_conv_w = lax.dynamic_slice(g_conv_full, (0, me * cw), (CONV_K, cw))
    g_w_ada = _mod_bwd(c_all.T, lax.dynamic_slice(dmod_all, (0, me * ada_n), (N_DEV, ada_n)))

    grads = {"w_ada": g_w_ada[None], "b_ada": g_b_ada[None], "conv_w": g_conv_w[None, :, None, :],
             "a_log": g_a_log[None], "dt_bias": g_dt_bias[None], "dn_norm_g": g_dn_g[None], "q_norm_g": g_q_g[None],
             "kv_norm_g": g_kv_g[None], "ln1_g": g_ln1_g[None], "ln1_b": g_ln1_b[None], "ln2_g": g_ln2_g[None], "ln2_b": g_ln2_b[None]}
    weights = dict(w_ada=w_ada, b_ada=b_ada, w_in=w_in, conv_w=conv_w, a_log=a_log, dt_bias=dt_bias, dn_norm_g=dn_norm_g,
                   q_norm_g=q_norm_g, w_uq=w_uq, kv_norm_g=kv_norm_g, w_ukv=w_ukv, w_o=w_o, ln1_g=ln1_g, ln1_b=ln1_b,
                   w_gate=w_gate, w_up=w_up, w_down=w_down, ln2_g=ln2_g, ln2_b=ln2_b)
    ms = dict(w_ada=m_w_ada, b_ada=m_b_ada, w_in=m_w_in, conv_w=m_conv_w, a_log=m_a_log, dt_bias=m_dt_bias,
              dn_norm_g=m_dn_norm_g, q_norm_g=m_q_norm_g, w_uq=m_w_uq, kv_norm_g=m_kv_norm_g, w_ukv=m_w_ukv, w_o=m_w_o,
              ln1_g=m_ln1_g, ln1_b=m_ln1_b, w_gate=m_w_gate, w_up=m_w_up, w_down=m_w_down, ln2_g=m_ln2_g, ln2_b=m_ln2_b)
    vs = dict(w_ada=v_w_ada, b_ada=v_b_ada, w_in=v_w_in, conv_w=v_conv_w, a_log=v_a_log, dt_bias=v_dt_bias,
              dn_norm_g=v_dn_norm_g, q_norm_g=v_q_norm_g, w_uq=v_w_uq, kv_norm_g=v_kv_norm_g, w_ukv=v_w_ukv, w_o=v_w_o,
              ln1_g=v_ln1_g, ln1_b=v_ln1_b, w_gate=v_w_gate, w_up=v_w_up, w_down=v_w_down, ln2_g=v_ln2_g, ln2_b=v_ln2_b)
    names = list(weights)
    big = ("w_ada", "w_gate", "w_up", "w_down", "w_o", "w_uq", "w_ukv", "w_in")
    waits = {"w_gate": ("ffn", ("w_gate", "w_up", "w_down")), "w_o": ("mixer", ("w_o", "w_uq", "w_ukv")), "w_in": ("in", ("w_in",))}
    delta_w, new_m, new_v, slots = {}, {}, {}, {}
    last = g_w_ada
    for n in big:
        if n == "w_in":
            rest = [r for r in names if r not in big]
            flat2 = lambda a: a.reshape(-1, a.shape[-1])
            outs = _adamw_many(*[[flat2(src[r]) for r in rest] for src in (weights, grads, ms, vs)], "adamw_small")
            for dst, o in zip((delta_w, new_m, new_v), outs):
                for r, a in zip(rest, o):
                    dst[r] = a.reshape(weights[r].shape)
            last = outs[0][0]
        shp = weights[n].shape
        two = lambda a: a.reshape(shp[-2], shp[-1])
        if n in waits:
            tag, members = waits[n]
            slots.update(zip(members, _exchange_wait(scatters[tag], last, "scatter_%s_grads_wait" % tag, scatter=True)))
        g_in = slots[n] if n in slots else two(grads[n])
        gr, dlt, nm, nv = _adamw(two(weights[n]), g_in, two(ms[n]), two(vs[n]), "adamw_" + n)
        grads[n], delta_w[n], new_m[n], new_v[n] = gr.reshape(shp), dlt.reshape(shp), nm.reshape(shp), nv.reshape(shp)
        last = nv

    return (loss, grad_x[None], *[grads[n] for n in names], *[delta_w[n] for n in names],
            *[new_m[n] for n in names], *[new_v[n] for n in names])


def _local_step(xs, tgt, pos, mod, w_in_p, mixer_weights, ffn_weights, grads_ready, conv_w8,
                a_log, dt_bias, dn_norm_g, q_norm_g, kv_norm_g, ln1_g, ln1_b, ln2_g, ln2_b):
    t, d = xs.shape
    sh_m, sc_m, gt_m, sh_f, sc_f, gt_f = [mod[:, i * d:(i + 1) * d] for i in range(6)]
    pos_col = pos.astype(F32).reshape(t, 1)
    inv_freq = 1.0 / (ROPE_THETA ** (jnp.arange(0, QK_ROPE, 2, dtype=F32) / QK_ROPE))
    inv_freq2 = jnp.pad(jnp.concatenate([inv_freq, inv_freq]), (0, LANE - QK_ROPE)).reshape(1, LANE)
    al8 = jnp.pad(a_log, ((0, 7), (0, LANE - DN_HEADS)))
    dt8 = jnp.pad(dt_bias, ((0, 7), (0, LANE - DN_HEADS)))

    tm = min(512, t)
    tq = min(256, t)
    tk = min(512, t)

    (h1,) = _rowwise("modulate_in", lambda xx, sc, sh: xx * (1.0 + sc) + sh, [xs], [sc_m, sh_m], [(d, BF)], [], tm)
    proj = _matmul(h1, w_in_p, "nn", "in_proj")
    qkv = _conv_fwd(proj, conv_w8, min(256, t))
    gdn_tm = min(512, t)
    intra = _gdn_intra_fwd(qkv, proj, al8, dt8, gdn_tm)
    o_dn, states = _gdn_scan_fwd(intra, gdn_tm)
    w_uq_p, w_ukv_f, w_o_f = mixer_weights(states)
    qc, kc, vc = _mla_prep_fwd(proj, pos_col, inv_freq2, q_norm_g, kv_norm_g, w_uq_p, w_ukv_f, tm)
    o_mla, lse = _attn_fwd(qc, kc, vc, tk, tk)

    def mix_in(o, z, om, g):
        return jnp.concatenate(_gdn_out(o, z, g) + [om], axis=1)

    (mixin,) = _rowwise("mixer_out", mix_in, [o_dn, (proj, DN_VW, P_Z // DN_VW), o_mla], [dn_norm_g], [(2 * DN_VW, BF)], [], tm)
    mix = _matmul(mixin, w_o_f, "nn", "out_proj")

    def block1(xx, mx, gt, g1, b1, sc, sh):
        x1 = _layernorm(DEEPNORM_ALPHA * xx + gt * mx, g1, b1)
        return x1, x1 * (1.0 + sc) + sh

    x1, h2 = _rowwise("norm1_modulate", block1, [xs, mix], [gt_m, ln1_g, ln1_b, sc_f, sh_f], [(d, F32), (d, BF)], [], tm)
    w_gu, w_down_f = ffn_weights(h2)
    ff_n = w_down_f.shape[0]
    gu = _matmul(h2, w_gu, "nn", "ffn_in")
    (act,) = _rowwise("swiglu", lambda gg, uu: _silu(gg) * uu, [(gu, ff_n, 0), (gu, ff_n, 1)], [], [(ff_n, BF)], [], min(256, t))
    ff = _matmul(act, w_down_f, "nn", "ffn_out")

    def tail_loss(x1_, ff_, gt, g2, b2, tg):
        y = _layernorm(DEEPNORM_ALPHA * x1_ + gt * ff_, g2, b2)
        return 0.5 * jnp.sum(jnp.mean(jnp.square(y - tg), axis=-1))

    def tail(x1_, ff_, tg, gt, g2, b2):
        loss, (dx1, dff, dgt, dg2, db2) = jax.value_and_grad(tail_loss, argnums=(0, 1, 2, 3, 4))(x1_, ff_, gt, g2, b2, tg)
        return dx1, dff, jnp.full((1, LANE), loss, F32), dgt, dg2, db2

    dx1_a, dff, loss_acc, d_gt_f, d_ln2_g, d_ln2_b = _rowwise(
        "norm2_loss", tail, [x1, ff, tgt], [gt_f, ln2_g, ln2_b], [(d, F32), (d, BF)], [(1, LANE), (1, d), (1, d), (1, d)], tm)

    dact = _matmul(dff, w_down_f, "nt", "d_ffn_act")
    g_w_down = _matmul(act, dff, "tn", "d_w_down", BF)

    def swiglu_bwd(gg, uu, da):
        _, vjp = jax.vjp(lambda a, b: _silu(a) * b, gg, uu)
        dg, du = vjp(da)
        return jnp.concatenate([dg, du], axis=1)

    (dgu,) = _rowwise("swiglu_bwd", swiglu_bwd, [(gu, ff_n, 0), (gu, ff_n, 1), dact], [], [(2 * ff_n, BF)], [], min(256, t))
    dh2 = _matmul(dgu, w_gu, "nt", "d_ffn_in")
    g_w_gu = _matmul(h2, dgu, "tn", "d_w_gate_up", BF)
    token = grads_ready("ffn", g_w_gu, g_w_down)

    def block1_bwd(xx, mx, dx1_, dh2_, gt, g1, b1, sc, sh):
        _, vjp = jax.vjp(block1, xx, mx, gt, g1, b1, sc, sh)
        dxx, dmx, dgt, dg1, db1, dsc, dsh = vjp((dx1_, dh2_))
        return dxx, dmx, dgt, dg1, db1, dsc, dsh

    dx_a, dmix, d_gt_m, d_ln1_g, d_ln1_b, d_sc_f, d_sh_f = _rowwise(
        "norm1_modulate_bwd", block1_bwd, [xs, mix, dx1_a, dh2], [gt_m + token, ln1_g, ln1_b, sc_f, sh_f],
        [(d, F32), (d, BF)], [(1, d)] * 5, min(256, t))

    dmixin = _matmul(dmix, w_o_f, "nt", "d_mixer_out")
    g_w_o = _matmul(mixin, dmix, "tn", "d_w_o", BF)

    def mixer_bwd(o, z, om, dmi, g):
        _, vjp = jax.vjp(lambda o_, z_, g_: jnp.concatenate(_gdn_out(o_, z_, g_), axis=1), o, z, g)
        do_, dz_, dg_ = vjp(dmi[:, :DN_VW])
        dom = dmi[:, DN_VW:]
        delta = [jnp.broadcast_to(jnp.sum(dom[:, h * V_HEAD:(h + 1) * V_HEAD] * om[:, h * V_HEAD:(h + 1) * V_HEAD], axis=-1, keepdims=True), (o.shape[0], LANE))
                 for h in range(MLA_HEADS)]
        return do_, dz_, dom, jnp.concatenate(delta, axis=1), dg_

    do_dn, dz, do_mla, delta, d_dn_g = _rowwise(
        "mixer_out_bwd", mixer_bwd, [o_dn, (proj, DN_VW, P_Z // DN_VW), o_mla, dmixin], [dn_norm_g],
        [(DN_VW, F32), (DN_VW, BF), (MLA_VW, BF), (MLA_HEADS * LANE, F32)], [(1, DN_DV)], tm)

    dqc, dkc, dvc = _attn_bwd(qc, kc, vc, do_mla, lse, delta, tq, tk)
    dcq, dckv, dkr, d_q_g, d_kv_g, g_w_uq_p, g_w_ukv = _mla_prep_bwd(
        proj, pos_col, inv_freq2, q_norm_g, kv_norm_g, w_uq_p, w_ukv_f, dqc, dkc, dvc, min(256, t))

    token = grads_ready("mixer", g_w_o, g_w_uq_p, g_w_ukv)

    d_intra = _gdn_scan_bwd(intra, states, do_dn, gdn_tm)
    dqkv_act, dba, d_al8, d_dt8 = _gdn_intra_bwd(qkv, proj, al8 + token, dt8, d_intra, min(256, t))
    dqkv_pre, d_conv8 = _conv_bwd(proj, conv_w8, dqkv_act, min(256, t))

    dproj = jnp.concatenate([dqkv_pre, dz, dcq, dckv, dba, dkr], axis=1)
    dh1 = _matmul(dproj, w_in_p, "nt", "d_in_proj")
    g_w_in_p = _matmul(h1, dproj, "tn", "d_w_in", BF)
    token = grads_ready("in", g_w_in_p)

    def modulate_bwd(xx, dh, dxa, sc):
        return dh * (1.0 + sc) + dxa, jnp.sum(dh * xx, axis=0, keepdims=True), jnp.sum(dh, axis=0, keepdims=True)

    grad_x, d_sc_m, d_sh_m = _rowwise("modulate_in_bwd", modulate_bwd, [xs, dh1, dx_a], [sc_m + token], [(d, F32)], [(1, d), (1, d)], tm)
    dmod = jnp.concatenate([d_sh_m, d_sc_m, d_gt_m, d_sh_f, d_sc_f, d_gt_f], axis=1)
    return grad_x, loss_acc, dmod, d_conv8, d_al8, d_dt8, d_dn_g, d_q_g, d_kv_g, d_ln1_g, d_ln1_b, d_ln2_g, d_ln2_b
```

```python
import functools
import math

import jax
import jax.numpy as jnp
from jax import lax
from jax.experimental import pallas as pl
from jax.experimental.pallas import tpu as pltpu

F32 = jnp.float32
BF = jnp.bfloat16
HI = lax.Precision.HIGHEST

N_DEV = 8
DN_HEADS = 4
DN_DK = 128
DN_DV = 128
CONV_K = 4
CHUNK = 64
MLA_HEADS = 4
QK_NOPE = 128
QK_ROPE = 64
V_HEAD = 128
Q_LORA = 512
KV_LORA = 256
ROPE_THETA = 10000.0
DEPTH = 1
DEEPNORM_ALPHA = (2.0 * DEPTH) ** 0.25
LANE = 128
CONV_HALO = 8

DN_QK = DN_HEADS * DN_DK
DN_VW = DN_HEADS * DN_DV
DN_CONV_CH = 2 * DN_QK + DN_VW
MLA_VW = MLA_HEADS * V_HEAD
MLA_QCAT = QK_NOPE + LANE
N_IN = DN_CONV_CH + DN_VW + 2 * DN_HEADS + Q_LORA + KV_LORA + QK_ROPE
P_QKV = 0
P_Z = DN_CONV_CH
P_CQ = P_Z + DN_VW
P_CKV = P_CQ + Q_LORA
P_BA = P_CKV + KV_LORA
P_KR = P_BA + LANE
N_INP = P_KR + LANE

ADAM_LR = 0.001
ADAM_B1 = 0.9
ADAM_B2 = 0.999
ADAM_EPS = 1e-08
ADAM_WD = 0.01
ADAM_STEP = 10

NN = (((1,), (0,)), ((), ()))
NT = (((1,), (1,)), ((), ()))
TN = (((0,), (0,)), ((), ()))


def _pick(n, prefs):
    for p in prefs:
        if n % p == 0:
            return p
    return n


def _full(shape):
    return pl.BlockSpec(shape, lambda *_: (0,) * len(shape))


def _dot(a, b, dims=NN):
    return lax.dot_general(a, b, dims, preferred_element_type=F32)


def _doth(a, b, dims=NN):
    return lax.dot_general(a, b, dims, precision=HI, preferred_element_type=F32)


@jax.custom_vjp
def _mmb(a, b):
    return _dot(a.astype(BF), b.astype(BF), NN)


def _mmb_fwd(a, b):
    return _mmb(a, b), (a, b)


def _mmb_bwd(res, g):
    a, b = res
    gb = g.astype(BF)
    return (_dot(gb, b.astype(BF), NT).astype(a.dtype), _dot(a.astype(BF), gb, TN).astype(b.dtype))


_mmb.defvjp(_mmb_fwd, _mmb_bwd)


@jax.custom_vjp
def _mmb_nt(a, b):
    return _dot(a.astype(BF), b.astype(BF), NT)


def _mmb_nt_fwd(a, b):
    return _mmb_nt(a, b), (a, b)


def _mmb_nt_bwd(res, g):
    a, b = res
    gb = g.astype(BF)
    return (_dot(gb, b.astype(BF), NN).astype(a.dtype), _dot(gb, a.astype(BF), TN).astype(b.dtype))


_mmb_nt.defvjp(_mmb_nt_fwd, _mmb_nt_bwd)


@jax.custom_vjp
def _mmb_tn(a, b):
    return _dot(a.astype(BF), b.astype(BF), TN)


def _mmb_tn_fwd(a, b):
    return _mmb_tn(a, b), (a, b)


def _mmb_tn_bwd(res, g):
    a, b = res
    gb = g.astype(BF)
    return (_dot(b.astype(BF), gb, NT).astype(a.dtype), _dot(a.astype(BF), gb, NN).astype(b.dtype))


_mmb_tn.defvjp(_mmb_tn_fwd, _mmb_tn_bwd)


def _sigmoid(x):
    return 0.5 * (jnp.tanh(0.5 * x) + 1.0)


def _silu(x):
    return x * _sigmoid(x)


def _softplus(x):
    return jnp.maximum(x, 0.0) + jnp.log(1.0 + jnp.exp(-jnp.abs(x)))


def _layernorm(x, g, b, eps=1e-5):
    mu = jnp.mean(x, axis=-1, keepdims=True)
    xc = x - mu
    var = jnp.mean(xc * xc, axis=-1, keepdims=True)
    return xc * lax.rsqrt(var + eps) * g + b


def _rmsnorm(x, g, eps=1e-6):
    return x * lax.rsqrt(jnp.mean(x * x, axis=-1, keepdims=True) + eps) * g


def _l2norm(x, eps=1e-6):
    return x * lax.rsqrt(jnp.sum(x * x, axis=-1, keepdims=True) + eps)


def _rowwise(name, fn, rows, vecs, out_rows, out_accs, tm):
    rows = [r if isinstance(r, tuple) else (r, r.shape[1], 0) for r in rows]
    t = rows[0][0].shape[0]
    tm = min(tm, t)
    assert t % tm == 0
    nr, nv, no = len(rows), len(vecs), len(out_rows)

    def body(*refs):
        ins = [r[...] for r in refs[:nr + nv]]
        outs = fn(*ins)
        outs = outs if isinstance(outs, (tuple, list)) else (outs,)
        o_rows = refs[nr + nv:nr + nv + no]
        o_accs = refs[nr + nv + no:]
        for o, val in zip(o_rows, outs[:no]):
            o[...] = val.astype(o.dtype)
        if o_accs:
            @pl.when(pl.program_id(0) == 0)
            def _():
                for o in o_accs:
                    o[...] = jnp.zeros_like(o)
            for o, val in zip(o_accs, outs[no:]):
                o[...] += val

    in_specs = [pl.BlockSpec((tm, w), functools.partial(lambda i, j: (i, j), j=j)) for (_, w, j) in rows]
    in_specs += [_full(v.shape) for v in vecs]
    out_specs = [pl.BlockSpec((tm, w), lambda i: (i, 0)) for (w, _) in out_rows]
    out_specs += [_full(s) for s in out_accs]
    out_shape = [jax.ShapeDtypeStruct((t, w), d) for (w, d) in out_rows]
    out_shape += [jax.ShapeDtypeStruct(s, F32) for s in out_accs]
    res = pl.pallas_call(
        body, grid=(t // tm,), in_specs=in_specs, out_specs=out_specs, out_shape=out_shape, name=name,
        compiler_params=pltpu.CompilerParams(dimension_semantics=("arbitrary",)),
    )(*[r[0] for r in rows], *vecs)
    return res


def _matmul(a, b, mode, name, out_dtype=F32):
    if mode == "nn":
        (m, k), n = a.shape, b.shape[1]
    elif mode == "nt":
        (m, k), n = a.shape, b.shape[0]
    else:
        (k, m), n = a.shape, b.shape[1]
    tm, tn, tk = _matmul_tiles(m, n, k, a.dtype.itemsize, b.dtype.itemsize, jnp.dtype(out_dtype).itemsize)
    nk = k // tk
    dims = {"nn": NN, "nt": NT, "tn": TN}[mode]

    def body(a_ref, b_ref, o_ref, *acc):
        part = _dot(a_ref[...].astype(BF), b_ref[...].astype(BF), dims)
        if nk == 1:
            o_ref[...] = part.astype(o_ref.dtype)
            return
        (acc_ref,) = acc
        kk = pl.program_id(2)

        @pl.when(kk == 0)
        def _():
            acc_ref[...] = part

        @pl.when(kk > 0)
        def _():
            acc_ref[...] += part

        @pl.when(kk == nk - 1)
        def _():
            o_ref[...] = acc_ref[...].astype(o_ref.dtype)

    a_spec = pl.BlockSpec((tk, tm), lambda i, j, kk: (kk, i)) if mode == "tn" else pl.BlockSpec((tm, tk), lambda i, j, kk: (i, kk))
    b_spec = pl.BlockSpec((tn, tk), lambda i, j, kk: (j, kk)) if mode == "nt" else pl.BlockSpec((tk, tn), lambda i, j, kk: (kk, j))
    return pl.pallas_call(
        body, grid=(m // tm, n // tn, nk), in_specs=[a_spec, b_spec],
        out_specs=pl.BlockSpec((tm, tn), lambda i, j, kk: (i, j)),
        out_shape=jax.ShapeDtypeStruct((m, n), out_dtype),
        scratch_shapes=[pltpu.VMEM((tm, tn), F32)] if nk > 1 else [], name=name,
        compiler_params=pltpu.CompilerParams(dimension_semantics=("parallel", "parallel", "arbitrary")),
    )(a, b)


def _lane_tile(n, cap):
    return max([n // s for s in range(1, n // LANE + 1) if n % s == 0 and (n // s) % LANE == 0 and n // s <= cap] or [n])


def _ffn_in(h, w_gate, w_up):
    m, k = h.shape
    f = w_gate.shape[1]
    tm, tn = _pick(m, (512, 256, 128)), _lane_tile(f, 1408)

    def body(h_ref, wg_ref, wu_ref, act_ref, g_ref, u_ref):
        hh = h_ref[...]
        g = _dot(hh, wg_ref[...])
        u = _dot(hh, wu_ref[...])
        act_ref[...] = (_silu(g) * u).astype(act_ref.dtype)
        g_ref[...] = g.astype(g_ref.dtype)
        u_ref[...] = u.astype(u_ref.dtype)

    w_spec = pl.BlockSpec((k, tn), lambda i, j: (0, j))
    o_spec = pl.BlockSpec((tm, tn), lambda i, j: (i, j))
    return pl.pallas_call(
        body, grid=(m // tm, f // tn), in_specs=[pl.BlockSpec((tm, k), lambda i, j: (i, 0)), w_spec, w_spec],
        out_specs=[o_spec] * 3, out_shape=[jax.ShapeDtypeStruct((m, f), BF)] * 3, name="ffn_in",
        compiler_params=pltpu.CompilerParams(dimension_semantics=("parallel", "parallel")),
    )(h, w_gate, w_up)


def _ffn_act_bwd(dff, w_down, gate, up):
    m, k = dff.shape
    f = w_down.shape[0]
    tm, tn = _pick(m, (512, 256, 128)), _lane_tile(f, 1408)

    def body(d_ref, w_ref, g_ref, u_ref, dg_ref, du_ref):
        da = _dot(d_ref[...], w_ref[...], NT)
        g = g_ref[...].astype(F32)
        sg = _sigmoid(g)
        dg_ref[...] = (da * u_ref[...].astype(F32) * (sg * (1.0 + g * (1.0 - sg)))).astype(dg_ref.dtype)
        du_ref[...] = (da * (g * sg)).astype(du_ref.dtype)

    o_spec = pl.BlockSpec((tm, tn), lambda i, j: (i, j))
    return pl.pallas_call(
        body, grid=(m // tm, f // tn),
        in_specs=[pl.BlockSpec((tm, k), lambda i, j: (i, 0)), pl.BlockSpec((tn, k), lambda i, j: (j, 0)), o_spec, o_spec],
        out_specs=[o_spec] * 2, out_shape=[jax.ShapeDtypeStruct((m, f), BF)] * 2, name="d_ffn_act",
        compiler_params=pltpu.CompilerParams(dimension_semantics=("parallel", "parallel")),
    )(dff, w_down, gate, up)


def _matmul2_nt(a1, b1, a2, b2, name):
    m, k = a1.shape
    n = b1.shape[0]
    tm, tn = _pick(m, (512, 256, 128)), _pick(n, (512, 256, 128))

    def body(a1_ref, b1_ref, a2_ref, b2_ref, o_ref):
        o_ref[...] = _dot(a1_ref[...], b1_ref[...], NT) + _dot(a2_ref[...], b2_ref[...], NT)

    a_spec = pl.BlockSpec((tm, k), lambda i, j: (i, 0))
    b_spec = pl.BlockSpec((tn, k), lambda i, j: (j, 0))
    return pl.pallas_call(
        body, grid=(m // tm, n // tn), in_specs=[a_spec, b_spec, a_spec, b_spec],
        out_specs=pl.BlockSpec((tm, tn), lambda i, j: (i, j)), out_shape=jax.ShapeDtypeStruct((m, n), F32), name=name,
        compiler_params=pltpu.CompilerParams(dimension_semantics=("parallel", "parallel")),
    )(a1, b1, a2, b2)


MATMUL_VMEM_BUDGET = 28 * 1024 * 1024


def _matmul_tiles(m, n, k, a_bytes, b_bytes, o_bytes):
    def divisors(x, cap):
        return sorted({x // s for s in range(1, 65) if x % s == 0 and (x // s) % LANE == 0 and x // s <= cap}, reverse=True) or [x]

    for tk in divisors(k, k):
        best = None
        for tm in divisors(m, 1024):
            for tn in divisors(n, 2048):
                need = 2 * (tm * tk * a_bytes + tk * tn * b_bytes + tm * tn * o_bytes) + (tm * tn * 4 if tk < k else 0)
                if need <= MATMUL_VMEM_BUDGET and tm * tn >= 512 * 512 and (best is None or tm * tn > best[0] * best[1]):
                    best = (tm, tn)
        if best:
            return best[0], best[1], tk
    return _pick(m, (512, 256, 128)), _pick(n, (512, 256, 128)), _pick(k, (512, 256, 128))


def _exchange(xs, name, scatter):
    n = len(xs)
    npeer = N_DEV - 1

    def body(*refs):
        x_refs, o_refs = refs[:n], refs[n:2 * n]
        send_sems, recv_sems, local_sems = refs[2 * n:]
        mx, my, mc = lax.axis_index("x"), lax.axis_index("y"), lax.axis_index("c")
        me = 4 * mx + 2 * my + mc
        src_me = [x.at[me] if scatter else x for x in x_refs]
        mine = [pltpu.make_async_copy(src_me[a], o_refs[a].at[me], local_sems.at[a]) for a in range(n)]
        for cp in mine:
            cp.start()
        copies = []
        for k in range(1, N_DEV):
            px, py, pc = mx ^ (k >> 2), my ^ ((k >> 1) & 1), mc ^ (k & 1)
            peer = 4 * px + 2 * py + pc
            for a in range(n):
                cp = pltpu.make_async_remote_copy(
                    src_ref=x_refs[a].at[peer] if scatter else x_refs[a], dst_ref=o_refs[a].at[me],
                    send_sem=send_sems.at[a * npeer + k - 1], recv_sem=recv_sems.at[a * npeer + k - 1],
                    device_id=(px, py, pc), device_id_type=pl.DeviceIdType.MESH)
                cp.start()
                copies.append((cp, a, k, peer))
        for cp, a, k, peer in copies:
            pltpu.make_async_remote_copy(
                src_ref=src_me[a], dst_ref=o_refs[a].at[peer], send_sem=send_sems.at[a * npeer + k - 1],
                recv_sem=recv_sems.at[a * npeer + k - 1], device_id=(mx, my, mc),
                device_id_type=pl.DeviceIdType.MESH).wait_recv()
        for cp, _, _, _ in copies:
            cp.wait_send()
        for cp in mine:
            cp.wait()

    return pl.pallas_call(
        body, out_shape=[jax.ShapeDtypeStruct((N_DEV,) + x.shape[-2:], x.dtype) for x in xs],
        in_specs=[pl.BlockSpec(memory_space=pl.ANY)] * n, out_specs=[pl.BlockSpec(memory_space=pl.ANY)] * n,
        scratch_shapes=[pltpu.SemaphoreType.DMA((n * npeer,)), pltpu.SemaphoreType.DMA((n * npeer,)),
                        pltpu.SemaphoreType.DMA((n,))],
        name=name,
    )(*xs)


def _peer_of(k):
    mx, my, mc = lax.axis_index("x"), lax.axis_index("y"), lax.axis_index("c")
    px, py, pc = mx ^ (k >> 2), my ^ ((k >> 1) & 1), mc ^ (k & 1)
    return (px, py, pc), 4 * px + 2 * py + pc


def _exchange_start(xs, name, scatter):
    n = len(xs)
    npeer = N_DEV - 1

    def body(*refs):
        x_refs, land_refs = refs[:n], refs[n:2 * n]
        send_sems, recv_sems, token = refs[2 * n], refs[2 * n + 1], refs[-1]
        me = 4 * lax.axis_index("x") + 2 * lax.axis_index("y") + lax.axis_index("c")
        for k in range(1, N_DEV):
            dev, peer = _peer_of(k)
            for a in range(n):
                pltpu.make_async_remote_copy(
                    src_ref=x_refs[a].at[peer] if scatter else x_refs[a], dst_ref=land_refs[a].at[me],
                    send_sem=send_sems.at[a * npeer + k - 1], recv_sem=recv_sems.at[a * npeer + k - 1],
                    device_id=dev, device_id_type=pl.DeviceIdType.MESH).start()
        token[...] = jnp.zeros_like(token)

    hbm = pl.BlockSpec(memory_space=pltpu.HBM)
    sem = pl.BlockSpec(memory_space=pltpu.SEMAPHORE)
    lands = [pltpu.with_memory_space_constraint(lax.empty((N_DEV,) + x.shape[-2:], x.dtype), pltpu.HBM) for x in xs]
    srcs = [pltpu.with_memory_space_constraint(x, pltpu.HBM) for x in xs]
    outs = pl.pallas_call(
        body, name=name,
        out_shape=(pltpu.SemaphoreType.DMA((n * npeer,)), pltpu.SemaphoreType.DMA((n * npeer,)),
                   *[pltpu.HBM(x.shape, x.dtype) for x in srcs], *[pltpu.HBM(z.shape, z.dtype) for z in lands],
                   jax.ShapeDtypeStruct((8, LANE), F32)),
        in_specs=[hbm] * (2 * n), out_specs=(sem, sem, *[hbm] * (2 * n), pl.BlockSpec(memory_space=pltpu.VMEM)),
        input_output_aliases={i: 2 + i for i in range(2 * n)},
        compiler_params=pltpu.CompilerParams(has_side_effects=pltpu.SideEffectType.DATAFLOW_SIDE_EFFECTING),
    )(*srcs, *lands)
    return (outs[0], outs[1], list(outs[2:2 + n]), list(outs[2 + n:2 + 2 * n])), outs[-1][0:1, 0:1]


def _exchange_wait(started, after, name, scatter):
    send_sems, recv_sems, srcs, lands = started
    n = len(srcs)
    npeer = N_DEV - 1

    def body(*refs):
        x_refs, land_refs = refs[:n], refs[n:2 * n]
        send_sems, recv_sems = refs[2 * n], refs[2 * n + 1]
        mx, my, mc = lax.axis_index("x"), lax.axis_index("y"), lax.axis_index("c")
        me = 4 * mx + 2 * my + mc
        for k in range(1, N_DEV):
            _, peer = _peer_of(k)
            for a in range(n):
                src = x_refs[a].at[me] if scatter else x_refs[a]
                cp = pltpu.make_async_remote_copy(
                    src_ref=src, dst_ref=land_refs[a].at[peer], send_sem=send_sems.at[a * npeer + k - 1],
                    recv_sem=recv_sems.at[a * npeer + k - 1], device_id=(mx, my, mc), device_id_type=pl.DeviceIdType.MESH)
                cp.wait_send()
                cp.wait_recv()

    hbm = pl.BlockSpec(memory_space=pltpu.HBM)
    sem = pl.BlockSpec(memory_space=pltpu.SEMAPHORE)
    outs = pl.pallas_call(
        body, name=name,
        out_shape=(*[pltpu.HBM(x.shape, x.dtype) for x in srcs], *[pltpu.HBM(z.shape, z.dtype) for z in lands]),
        in_specs=[hbm] * (2 * n) + [sem, sem, pl.BlockSpec(memory_space=pl.ANY)], out_specs=tuple([hbm] * (2 * n)),
        input_output_aliases={i: i for i in range(2 * n)},
        compiler_params=pltpu.CompilerParams(has_side_effects=pltpu.SideEffectType.DATAFLOW_SIDE_EFFECTING),
    )(*srcs, *lands, send_sems, recv_sems, after)
    me = 4 * lax.axis_index("x") + 2 * lax.axis_index("y") + lax.axis_index("c")
    full = []
    for x, land in zip(outs[:n], outs[n:]):
        own = lax.dynamic_slice(x, (me, 0, 0), (1,) + x.shape[1:]) if scatter else x[None]
        full.append(lax.dynamic_update_slice(land, own, (me, 0, 0)))
    return full


def _sum_slots(x, name):
    _, r, c = x.shape
    tr = _pick(r, (512, 256, 128, 64, 32, 16))

    def body(x_ref, o_ref):
        acc = x_ref[0].astype(F32)
        for s in range(1, N_DEV):
            acc = acc + x_ref[s].astype(F32)
        o_ref[...] = acc

    return pl.pallas_call(
        body, grid=(r // tr,), in_specs=[pl.BlockSpec((N_DEV, tr, c), lambda i: (0, i, 0))],
        out_specs=pl.BlockSpec((tr, c), lambda i: (i, 0)), out_shape=jax.ShapeDtypeStruct((r, c), F32), name=name,
        compiler_params=pltpu.CompilerParams(dimension_semantics=("arbitrary",)),
    )(x)


def _mod_fwd(c_all, w_ada, b_ada_mine):
    def body(c_ref, w_ref, b_ref, o_ref):
        o_ref[...] = _doth(_silu(c_ref[...]), w_ref[...]) + b_ref[...]

    return pl.pallas_call(body, out_shape=jax.ShapeDtypeStruct((c_all.shape[0], w_ada.shape[1]), F32), name="mod_fwd")(c_all, w_ada, b_ada_mine)


def _mod_bwd(c_all_t, dmod_mine):
    def body(ct_ref, d_ref, o_ref):
        s = _silu(ct_ref[...])
        acc = s[:, 0:1] * d_ref[pl.ds(0, 1), :]
        for b in range(1, N_DEV):
            acc = acc + s[:, b:b + 1] * d_ref[pl.ds(b, 1), :]
        o_ref[...] = acc

    return pl.pallas_call(body, out_shape=jax.ShapeDtypeStruct((c_all_t.shape[0], dmod_mine.shape[1]), F32), name="mod_bwd")(c_all_t, dmod_mine)


def _conv_fwd(proj, conv_w8, tm):
    t = proj.shape[0]
    ch = DN_CONV_CH

    def body(x_ref, w_ref, o_ref, buf):
        @pl.when(pl.program_id(0) == 0)
        def _():
            buf[pl.ds(0, CONV_HALO), :] = jnp.zeros((CONV_HALO, ch), F32)

        buf[pl.ds(CONV_HALO, tm), :] = x_ref[...]
        acc = jnp.zeros((tm, ch), F32)
        for j in range(CONV_K):
            acc = acc + buf[pl.ds(CONV_HALO - (CONV_K - 1) + j, tm), :] * w_ref[pl.ds(j, 1), :]
        o_ref[...] = _silu(acc)
        buf[pl.ds(0, CONV_HALO), :] = buf[pl.ds(tm, CONV_HALO), :]

    return pl.pallas_call(
        body, grid=(t // tm,), in_specs=[pl.BlockSpec((tm, ch), lambda i: (i, 0)), _full(conv_w8.shape)],
        out_specs=pl.BlockSpec((tm, ch), lambda i: (i, 0)), out_shape=jax.ShapeDtypeStruct((t, ch), F32),
        scratch_shapes=[pltpu.VMEM((tm + CONV_HALO, ch), F32)], name="conv_fwd",
        compiler_params=pltpu.CompilerParams(dimension_semantics=("arbitrary",)),
    )(proj, conv_w8)


def _conv_bwd(proj, conv_w8, dact, tm):
    t = proj.shape[0]
    ch = DN_CONV_CH
    nt = t // tm
    hb = tm // CONV_HALO

    def body(x_ref, xp_ref, w_ref, dy_ref, dx_ref, dw_ref, xbuf, dbuf):
        step = pl.program_id(0)

        @pl.when(step == 0)
        def _():
            dbuf[pl.ds(tm, CONV_HALO), :] = jnp.zeros((CONV_HALO, ch), F32)
            dw_ref[...] = jnp.zeros_like(dw_ref)

        first = step == nt - 1
        xbuf[pl.ds(0, CONV_HALO), :] = jnp.where(first, 0.0, xp_ref[...])
        xbuf[pl.ds(CONV_HALO, tm), :] = x_ref[...]
        pre = jnp.zeros((tm, ch), F32)
        for j in range(CONV_K):
            pre = pre + xbuf[pl.ds(CONV_HALO - (CONV_K - 1) + j, tm), :] * w_ref[pl.ds(j, 1), :]
        sg = _sigmoid(pre)
        dpre = dy_ref[...] * (sg * (1.0 + pre * (1.0 - sg)))
        dbuf[pl.ds(0, tm), :] = dpre
        dx = jnp.zeros((tm, ch), F32)
        for j in range(CONV_K):
            dx = dx + dbuf[pl.ds(CONV_K - 1 - j, tm), :] * w_ref[pl.ds(j, 1), :]
            dw_ref[pl.ds(j, 1), :] += jnp.sum(dpre * xbuf[pl.ds(CONV_HALO - (CONV_K - 1) + j, tm), :], axis=0, keepdims=True)
        dx_ref[...] = dx.astype(dx_ref.dtype)
        dbuf[pl.ds(tm, CONV_HALO), :] = dbuf[pl.ds(0, CONV_HALO), :]

    rev = lambda i: (nt - 1 - i, 0)
    prev = lambda i: (jnp.maximum((nt - 1 - i) * hb - 1, 0), 0)
    return pl.pallas_call(
        body, grid=(nt,),
        in_specs=[pl.BlockSpec((tm, ch), rev), pl.BlockSpec((CONV_HALO, ch), prev), _full(conv_w8.shape),
                  pl.BlockSpec((tm, ch), rev)],
        out_specs=[pl.BlockSpec((tm, ch), rev), _full(conv_w8.shape)],
        out_shape=[jax.ShapeDtypeStruct((t, ch), BF), jax.ShapeDtypeStruct(conv_w8.shape, F32)],
        scratch_shapes=[pltpu.VMEM((tm + CONV_HALO, ch), F32), pltpu.VMEM((tm + CONV_HALO, ch), F32)], name="conv_bwd",
        compiler_params=pltpu.CompilerParams(dimension_semantics=("arbitrary",)),
    )(proj, proj, conv_w8, dact)


BNN = (((2,), (1,)), ((0,), (0,)))
BNT = (((2,), (2,)), ((0,), (0,)))
BTN = (((1,), (1,)), ((0,), (0,)))


def _bdot(a, b, dims, precision=None):
    return lax.dot_general(a, b, dims, precision=precision, preferred_element_type=F32)


@jax.custom_vjp
def _bmmb_nt(a, b):
    return _bdot(a.astype(BF), b.astype(BF), BNT)


def _bmmb_nt_fwd(a, b):
    return _bmmb_nt(a, b), (a, b)


def _bmmb_nt_bwd(res, g):
    a, b = res
    gb = g.astype(BF)
    return _bdot(gb, b.astype(BF), BNN), _bdot(gb, a.astype(BF), BTN)


_bmmb_nt.defvjp(_bmmb_nt_fwd, _bmmb_nt_bwd)


@jax.custom_vjp
def _unit_lower_solve(a, r):
    return _unit_lower_solve_fwd(a, r)[0]


def _unit_lower_solve_fwd(a, r):
    c = a.shape[-1]
    ri = lax.broadcasted_iota(jnp.int32, a.shape, 1)
    ci = lax.broadcasted_iota(jnp.int32, a.shape, 2)
    xm = -a
    inv = (ri == ci).astype(F32) + xm
    for _ in range(int(math.log2(c)) - 1):
        xm = _bdot(xm, xm, BNN, HI)
        inv = inv + _bdot(inv, xm, BNN, HI)
    x = _bdot(inv, r, BNN, HI)
    return x, (inv, x)


def _unit_lower_solve_bwd(res, g):
    inv, x = res
    dr = _bdot(inv, g, BTN, HI)
    return -_bdot(dr, x, BNT, HI), dr


_unit_lower_solve.defvjp(_unit_lower_solve_fwd, _unit_lower_solve_bwd)


def _gdn_intra(qkv, ba, al8, dt8):
    tm = qkv.shape[0]
    nb = tm // CHUNK
    bsz = DN_HEADS * nb

    def heads(x0):
        return jnp.concatenate([qkv[:, x0 + h * LANE:x0 + (h + 1) * LANE].reshape(nb, CHUNK, LANE) for h in range(DN_HEADS)], axis=0)

    def spread(c0):
        return jnp.concatenate([jnp.broadcast_to(ba[:, c0 + h:c0 + h + 1], (tm, LANE)).reshape(nb, CHUNK, LANE)
                                for h in range(DN_HEADS)], axis=0)

    def per_head(v8):
        return jnp.concatenate([jnp.broadcast_to(v8[0:1, h:h + 1].reshape(1, 1, 1), (nb, 1, LANE)) for h in range(DN_HEADS)], axis=0)

    ri = lax.broadcasted_iota(jnp.int32, (bsz, CHUNK, CHUNK), 1)
    ci = lax.broadcasted_iota(jnp.int32, (bsz, CHUNK, CHUNK), 2)
    incl = ri >= ci
    strict = ri > ci

    q = _l2norm(heads(0)) * (DN_DK ** -0.5)
    k = _l2norm(heads(DN_QK))
    va = heads(2 * DN_QK)
    beta = _sigmoid(spread(0))
    g = -jnp.exp(per_head(al8)) * _softplus(spread(DN_HEADS) + per_head(dt8))
    gc = _bdot(incl.astype(F32), g, BNN, HI)
    g_last = jnp.sum(g, axis=1, keepdims=True)
    gcol = gc[:, :, :CHUNK]
    diff = gcol - jnp.swapaxes(gcol, 1, 2)
    decay = jnp.where(incl, jnp.exp(jnp.where(incl, diff, 0.0)), 0.0)
    kb = k * beta
    a_mat = jnp.where(strict, _bmmb_nt(kb, k) * decay, 0.0)
    egc = jnp.exp(gc)
    wu = _unit_lower_solve(a_mat, jnp.concatenate([kb * egc, va * beta], axis=2))
    attn = jnp.where(incl, _bmmb_nt(q, k) * decay, 0.0)

    def unheads(x):
        return jnp.concatenate([x[h * nb:(h + 1) * nb].reshape(tm, LANE) for h in range(DN_HEADS)], axis=1)

    return (unheads(wu[:, :, :DN_DK]), unheads(wu[:, :, DN_DK:]), unheads(q * egc), unheads(k * jnp.exp(g_last - gc)),
            attn.reshape(DN_HEADS, tm, CHUNK), unheads(jnp.broadcast_to(g_last, (bsz, CHUNK, LANE))))


def _gdn_scan_step(w, u, qg, kd, att, gl, s):
    v_new = u - _mmb(w, s)
    o = _mmb(qg, s) + _mmb(att, v_new)
    return o, s * jnp.exp(gl) + _mmb_tn(kd, v_new)


def _gdn_intra_specs(t, tm, dts):
    nb = tm // CHUNK
    specs = [pl.BlockSpec((tm, DN_VW), lambda i: (i, 0))] * 4
    specs += [pl.BlockSpec((DN_HEADS, tm, CHUNK), lambda i: (0, i, 0)), pl.BlockSpec((tm, DN_VW), lambda i: (i, 0))]
    shapes = [jax.ShapeDtypeStruct((t, DN_VW), dts[i]) for i in range(4)]
    shapes += [jax.ShapeDtypeStruct((DN_HEADS, t, CHUNK), dts[4]), jax.ShapeDtypeStruct((t, DN_VW), dts[5])]
    return specs, shapes


def _gdn_intra_fwd(qkv, proj, al8, dt8, tm):
    t = qkv.shape[0]

    def body(qkv_ref, ba_ref, al_ref, dt_ref, *outs):
        for o, val in zip(outs, _gdn_intra(qkv_ref[...], ba_ref[...], al_ref[...], dt_ref[...])):
            o[...] = val.astype(o.dtype)

    specs, shapes = _gdn_intra_specs(t, tm, (BF, F32, BF, BF, BF, F32))
    return pl.pallas_call(
        body, grid=(t // tm,),
        in_specs=[pl.BlockSpec((tm, DN_CONV_CH), lambda i: (i, 0)), pl.BlockSpec((tm, LANE), lambda i: (i, P_BA // LANE)),
                  _full(al8.shape), _full(dt8.shape)],
        out_specs=specs, out_shape=shapes, name="gdn_intra_fwd",
        compiler_params=pltpu.CompilerParams(dimension_semantics=("parallel",)),
    )(qkv, proj, al8, dt8)


def _gdn_intra_bwd(qkv, proj, al8, dt8, cts, tm):
    t = qkv.shape[0]

    def body(qkv_ref, ba_ref, al_ref, dt_ref, *refs):
        ct_refs, (dqkv_ref, dba_ref, dal_ref, ddt_ref) = refs[:6], refs[6:]

        @pl.when(pl.program_id(0) == 0)
        def _():
            dal_ref[...] = jnp.zeros_like(dal_ref)
            ddt_ref[...] = jnp.zeros_like(ddt_ref)

        _, vjp = jax.vjp(_gdn_intra, qkv_ref[...], ba_ref[...], al_ref[...], dt_ref[...])
        dqkv, dba, dal, ddt = vjp(tuple(r[...] for r in ct_refs))
        dqkv_ref[...] = dqkv
        dba_ref[...] = dba.astype(dba_ref.dtype)
        dal_ref[...] += dal
        ddt_ref[...] += ddt

    specs, _ = _gdn_intra_specs(t, tm, (F32,) * 6)
    return pl.pallas_call(
        body, grid=(t // tm,),
        in_specs=[pl.BlockSpec((tm, DN_CONV_CH), lambda i: (i, 0)), pl.BlockSpec((tm, LANE), lambda i: (i, P_BA // LANE)),
                  _full(al8.shape), _full(dt8.shape)] + specs,
        out_specs=[pl.BlockSpec((tm, DN_CONV_CH), lambda i: (i, 0)), pl.BlockSpec((tm, LANE), lambda i: (i, 0)),
                   _full(al8.shape), _full(dt8.shape)],
        out_shape=[jax.ShapeDtypeStruct((t, DN_CONV_CH), F32), jax.ShapeDtypeStruct((t, LANE), BF),
                   jax.ShapeDtypeStruct(al8.shape, F32), jax.ShapeDtypeStruct(dt8.shape, F32)],
        name="gdn_intra_bwd", compiler_params=pltpu.CompilerParams(dimension_semantics=("arbitrary",)),
    )(qkv, proj, al8, dt8, *cts)


def _gdn_scan_fwd(intra, tm):
    t = intra[0].shape[0]
    nb = tm // CHUNK
    nc = t // CHUNK

    def body(w_ref, u_ref, qg_ref, kd_ref, att_ref, gl_ref, o_ref, ss_ref, s_scr):
        @pl.when(pl.program_id(0) == 0)
        def _():
            s_scr[...] = jnp.zeros_like(s_scr)

        for cc in range(nb):
            rows = pl.ds(cc * CHUNK, CHUNK)
            for h in range(DN_HEADS):
                cols = pl.ds(h * DN_DV, DN_DV)
                s_prev = s_scr[h]
                ss_ref[cc, h] = s_prev
                o, s_new = _gdn_scan_step(w_ref[rows, cols], u_ref[rows, cols], qg_ref[rows, cols], kd_ref[rows, cols],
                                          att_ref[h, rows, :], gl_ref[pl.ds(cc * CHUNK, 1), cols], s_prev)
                o_ref[rows, cols] = o
                s_scr[h] = s_new

    specs, _ = _gdn_intra_specs(t, tm, (F32,) * 6)
    return pl.pallas_call(
        body, grid=(t // tm,), in_specs=specs,
        out_specs=[pl.BlockSpec((tm, DN_VW), lambda i: (i, 0)),
                   pl.BlockSpec((nb, DN_HEADS, DN_DK, DN_DV), lambda i: (i, 0, 0, 0))],
        out_shape=[jax.ShapeDtypeStruct((t, DN_VW), F32), jax.ShapeDtypeStruct((nc, DN_HEADS, DN_DK, DN_DV), F32)],
        scratch_shapes=[pltpu.VMEM((DN_HEADS, DN_DK, DN_DV), F32)], name="gdn_scan_fwd",
        compiler_params=pltpu.CompilerParams(dimension_semantics=("arbitrary",)),
    )(*intra)


def _gdn_scan_bwd(intra, states, do, tm):
    t = intra[0].shape[0]
    nb = tm // CHUNK
    ng = t // tm

    def body(w_ref, u_ref, qg_ref, kd_ref, att_ref, gl_ref, ss_ref, do_ref,
             dw_ref, du_ref, dqg_ref, dkd_ref, datt_ref, dgl_ref, ds_scr):
        @pl.when(pl.program_id(0) == 0)
        def _():
            ds_scr[...] = jnp.zeros_like(ds_scr)

        for cc in reversed(range(nb)):
            rows = pl.ds(cc * CHUNK, CHUNK)
            for h in range(DN_HEADS):
                cols = pl.ds(h * DN_DV, DN_DV)
                f32 = lambda r: r[rows, cols].astype(F32)
                _, vjp = jax.vjp(_gdn_scan_step, f32(w_ref), u_ref[rows, cols], f32(qg_ref), f32(kd_ref),
                                 att_ref[h, rows, :].astype(F32), gl_ref[pl.ds(cc * CHUNK, 1), cols], ss_ref[cc, h])
                dw, du, dqg, dkd, datt, dgl, ds_prev = vjp((do_ref[rows, cols], ds_scr[h]))
                dw_ref[rows, cols] = dw
                du_ref[rows, cols] = du
                dqg_ref[rows, cols] = dqg
                dkd_ref[rows, cols] = dkd
                datt_ref[h, rows, :] = datt
                first_row = lax.broadcasted_iota(jnp.int32, (CHUNK, DN_DV), 0) == 0
                dgl_ref[rows, cols] = jnp.where(first_row, dgl, 0.0)
                ds_scr[h] = ds_prev

    rev = lambda i: (ng - 1 - i, 0)
    rev3 = lambda i: (0, ng - 1 - i, 0)
    row = pl.BlockSpec((tm, DN_VW), rev)
    six = [row] * 4 + [pl.BlockSpec((DN_HEADS, tm, CHUNK), rev3), row]
    _, shapes = _gdn_intra_specs(t, tm, (F32,) * 6)
    return pl.pallas_call(
        body, grid=(ng,),
        in_specs=six + [pl.BlockSpec((nb, DN_HEADS, DN_DK, DN_DV), lambda i: (ng - 1 - i, 0, 0, 0)), row],
        out_specs=six, out_shape=shapes,
        scratch_shapes=[pltpu.VMEM((DN_HEADS, DN_DK, DN_DV), F32)], name="gdn_scan_bwd",
        compiler_params=pltpu.CompilerParams(dimension_semantics=("arbitrary",)),
    )(*intra, states, do)


def _gdn_out(o, z, g):
    parts = []
    for h in range(DN_HEADS):
        sl = slice(h * DN_DV, (h + 1) * DN_DV)
        parts.append(_rmsnorm(o[:, sl], g) * _silu(z[:, sl]))
    return parts


_Q_SCALE = math.log2(math.e) / math.sqrt(QK_NOPE + QK_ROPE)


def _rope_tables(pos, inv_freq2):
    lane = lax.broadcasted_iota(jnp.int32, (1, LANE), 1)
    ang = pos * inv_freq2
    cos = jnp.where(lane < QK_ROPE, jnp.cos(ang), 0.0)
    sin = jnp.where(lane < QK_ROPE // 2, -jnp.sin(ang), jnp.where(lane < QK_ROPE, jnp.sin(ang), 0.0))
    return cos, sin


def _rope_swap():
    ri = lax.broadcasted_iota(jnp.int32, (LANE, LANE), 0)
    ci = lax.broadcasted_iota(jnp.int32, (LANE, LANE), 1)
    half = QK_ROPE // 2
    return (((ci < half) & (ri == ci + half)) | ((ci >= half) & (ci < QK_ROPE) & (ri == ci - half))).astype(F32)


def _mla_prep(cq, ckv, kr, gq, gkv, w_uq, w_ukv, cos, sin, swap):
    rope = lambda u: u * cos + _doth(u, swap) * sin
    q_lin = _mmb(_rmsnorm(cq, gq), w_uq) * _Q_SCALE
    kv_lin = _mmb(_rmsnorm(ckv, gkv), w_ukv)
    k_rope = rope(kr)
    qs, ks, vs = [], [], []
    for h in range(MLA_HEADS):
        qs += [q_lin[:, h * LANE:(h + 1) * LANE], rope(q_lin[:, (MLA_HEADS + h) * LANE:(MLA_HEADS + h + 1) * LANE])]
        ks += [kv_lin[:, 2 * h * LANE:(2 * h + 1) * LANE], k_rope]
        vs += [kv_lin[:, (2 * h + 1) * LANE:(2 * h + 2) * LANE]]
    return qs + ks + vs


def _mla_prep_fwd(proj, pos_col, inv_freq2, gq, gkv, w_uq, w_ukv, tm):
    t = proj.shape[0]
    nq = 2 * MLA_HEADS

    def body(cq_ref, ckv_ref, kr_ref, pos_ref, f_ref, gq_ref, gkv_ref, wq_ref, wkv_ref, q_ref, k_ref, v_ref):
        cos, sin = _rope_tables(pos_ref[...], f_ref[...])
        outs = _mla_prep(cq_ref[...], ckv_ref[...], kr_ref[...], gq_ref[...], gkv_ref[...], wq_ref[...], wkv_ref[...],
                         cos, sin, _rope_swap())
        for i in range(nq):
            q_ref[:, pl.ds(i * LANE, LANE)] = outs[i].astype(q_ref.dtype)
            k_ref[:, pl.ds(i * LANE, LANE)] = outs[nq + i].astype(k_ref.dtype)
        for h in range(MLA_HEADS):
            v_ref[:, pl.ds(h * LANE, LANE)] = outs[2 * nq + h].astype(v_ref.dtype)

    row = lambda w, j: pl.BlockSpec((tm, w), functools.partial(lambda i, j: (i, j), j=j))
    return pl.pallas_call(
        body, grid=(t // tm,),
        in_specs=[row(Q_LORA, P_CQ // Q_LORA), row(KV_LORA, P_CKV // KV_LORA), row(LANE, P_KR // LANE),
                  pl.BlockSpec((tm, 1), lambda i: (i, 0)), _full(inv_freq2.shape), _full(gq.shape), _full(gkv.shape),
                  _full(w_uq.shape), _full(w_ukv.shape)],
        out_specs=[row(nq * LANE, 0), row(nq * LANE, 0), row(MLA_VW, 0)],
        out_shape=[jax.ShapeDtypeStruct((t, nq * LANE), BF), jax.ShapeDtypeStruct((t, nq * LANE), BF),
                   jax.ShapeDtypeStruct((t, MLA_VW), BF)],
        name="mla_prep_fwd", compiler_params=pltpu.CompilerParams(dimension_semantics=("arbitrary",)),
    )(proj, proj, proj, pos_col, inv_freq2, gq, gkv, w_uq, w_ukv)


def _mla_prep_bwd(proj, pos_col, inv_freq2, gq, gkv, w_uq, w_ukv, dq, dk, dv, tm):
    t = proj.shape[0]
    nq = 2 * MLA_HEADS

    def body(cq_ref, ckv_ref, kr_ref, pos_ref, f_ref, gq_ref, gkv_ref, wq_ref, wkv_ref, dq_ref, dk_ref, dv_ref,
             dcq_ref, dckv_ref, dkr_ref, dgq_ref, dgkv_ref, dwq_ref, dwkv_ref):
        @pl.when(pl.program_id(0) == 0)
        def _():
            for o in (dgq_ref, dgkv_ref, dwq_ref, dwkv_ref):
                o[...] = jnp.zeros_like(o)

        cos, sin = _rope_tables(pos_ref[...], f_ref[...])
        f = functools.partial(_mla_prep, cos=cos, sin=sin, swap=_rope_swap())
        _, vjp = jax.vjp(f, cq_ref[...], ckv_ref[...], kr_ref[...], gq_ref[...], gkv_ref[...], wq_ref[...], wkv_ref[...])
        cts = [dq_ref[:, pl.ds(i * LANE, LANE)] for i in range(nq)]
        cts += [dk_ref[:, pl.ds(i * LANE, LANE)] for i in range(nq)]
        cts += [dv_ref[:, pl.ds(h * LANE, LANE)] for h in range(MLA_HEADS)]
        dcq, dckv, dkr, dgq, dgkv, dwq, dwkv = vjp(cts)
        dcq_ref[...] = dcq.astype(dcq_ref.dtype)
        dckv_ref[...] = dckv.astype(dckv_ref.dtype)
        dkr_ref[...] = dkr.astype(dkr_ref.dtype)
        dgq_ref[...] += dgq
        dgkv_ref[...] += dgkv
        dwq_ref[...] += dwq
        dwkv_ref[...] += dwkv

    row = lambda w, j: pl.BlockSpec((tm, w), functools.partial(lambda i, j: (i, j), j=j))
    return pl.pallas_call(
        body, grid=(t // tm,),
        in_specs=[row(Q_LORA, P_CQ // Q_LORA), row(KV_LORA, P_CKV // KV_LORA), row(LANE, P_KR // LANE),
                  pl.BlockSpec((tm, 1), lambda i: (i, 0)), _full(inv_freq2.shape), _full(gq.shape), _full(gkv.shape),
                  _full(w_uq.shape), _full(w_ukv.shape), row(nq * LANE, 0), row(nq * LANE, 0), row(MLA_VW, 0)],
        out_specs=[row(Q_LORA, 0), row(KV_LORA, 0), row(LANE, 0), _full(gq.shape), _full(gkv.shape),
                   _full(w_uq.shape), _full(w_ukv.shape)],
        out_shape=[jax.ShapeDtypeStruct((t, Q_LORA), BF), jax.ShapeDtypeStruct((t, KV_LORA), BF),
                   jax.ShapeDtypeStruct((t, LANE), BF), jax.ShapeDtypeStruct(gq.shape, F32),
                   jax.ShapeDtypeStruct(gkv.shape, F32), jax.ShapeDtypeStruct(w_uq.shape, F32),
                   jax.ShapeDtypeStruct(w_ukv.shape, F32)],
        name="mla_prep_bwd", compiler_params=pltpu.CompilerParams(dimension_semantics=("arbitrary",)),
    )(proj, proj, proj, pos_col, inv_freq2, gq, gkv, w_uq, w_ukv, dq, dk, dv)


_NEG = -1e30
_LN2 = math.log(2.0)
ATT_CHAINS = 2


def _causal(tq, tk, q0, k0):
    row = q0 + lax.broadcasted_iota(jnp.int32, (tq, tk), 0)
    col = k0 + lax.broadcasted_iota(jnp.int32, (tq, tk), 1)
    return col <= row


def _attn_fwd(q, k, v, tq, tk):
    t = q.shape[0]

    assert tk % tq == 0

    th = tq // ATT_CHAINS

    def body(q_ref, k_ref, v_ref, o_ref, lse_ref):
        i = pl.program_id(1)
        n_full = (i * tq) // tk

        def step(k0, carry, masked):
            kt = k_ref[pl.ds(k0, tk), :]
            vt = v_ref[pl.ds(k0, tk), :]
            out = []
            for c, (m, l, acc) in enumerate(carry):
                s = _dot(q_ref[pl.ds(c * th, th), :], kt, NT)
                if masked:
                    s = jnp.where(_causal(th, tk, i * tq + c * th, k0), s, _NEG)
                m_new = jnp.maximum(m, jnp.max(s, axis=-1, keepdims=True))
                p = jnp.exp2(s - m_new)
                alpha = jnp.exp2(m - m_new)
                out.append((m_new, alpha * l + jnp.sum(p, axis=-1, keepdims=True), alpha * acc + _dot(p.astype(BF), vt)))
            return tuple(out)

        init = tuple((jnp.full((th, 1), _NEG, F32), jnp.zeros((th, 1), F32), jnp.zeros((th, V_HEAD), F32)) for _ in range(ATT_CHAINS))
        carry = lax.fori_loop(0, n_full, lambda j, c: step(pl.multiple_of(j * tk, tk), c, False), init)
        for c, (m, l, acc) in enumerate(step(pl.multiple_of(n_full * tk, tk), carry, True)):
            o_ref[pl.ds(c * th, th), :] = acc / l
            lse_ref[pl.ds(c * th, th), :] = jnp.broadcast_to(m + jnp.log2(l), (th, LANE))

    return pl.pallas_call(
        body, grid=(MLA_HEADS, t // tq),
        in_specs=[pl.BlockSpec((tq, 2 * LANE), lambda h, i: (i, h)), pl.BlockSpec((t, 2 * LANE), lambda h, i: (0, h)),
                  pl.BlockSpec((t, V_HEAD), lambda h, i: (0, h))],
        out_specs=[pl.BlockSpec((tq, V_HEAD), lambda h, i: (i, h)), pl.BlockSpec((tq, LANE), lambda h, i: (i, h))],
        out_shape=[jax.ShapeDtypeStruct((t, MLA_VW), F32), jax.ShapeDtypeStruct((t, MLA_HEADS * LANE), F32)],
        name="attn_fwd", compiler_params=pltpu.CompilerParams(dimension_semantics=("parallel", "arbitrary")),
    )(q, k, v)


def _attn_bwd(q, k, v, do, lse, delta, tq, tk):
    t = q.shape[0]
    nq = t // tq
    nkt = t // tk
    assert tk % tq == 0

    def body(q_ref, k_ref, v_ref, do_ref, lse_ref, dl_ref, dq_ref, dk_ref, dv_ref):
        j = pl.program_id(1)

        @pl.when(j == 0)
        def _():
            dq_ref[...] = jnp.zeros_like(dq_ref)

        kt = k_ref[...]
        vt = v_ref[...]

        def step(q0, carry, masked):
            dk, dv = carry
            rows = pl.ds(q0, tq)
            qt = q_ref[rows, :]
            dot_ = do_ref[rows, :]
            p = jnp.exp2(_dot(qt, kt, NT) - lse_ref[rows, pl.ds(0, 1)])
            if masked:
                p = jnp.where(_causal(tq, tk, q0, j * tk), p, 0.0)
            dv = dv + _dot(p.astype(BF), dot_, TN)
            ds = (p * (_dot(dot_, vt, NT) - dl_ref[rows, pl.ds(0, 1)])).astype(BF)
            dk = dk + _dot(ds, qt, TN)
            dq_ref[rows, :] += _dot(ds, kt)
            return dk, dv

        per = tk // tq
        carry = (jnp.zeros((tk, 2 * LANE), F32), jnp.zeros((tk, V_HEAD), F32))
        for dd in range(per):
            carry = step(pl.multiple_of(j * tk + dd * tq, tq), carry, True)

        def group(g, c):
            for dd in range(per):
                c = step(pl.multiple_of(g * tk + dd * tq, tq), c, False)
            return c

        dk, dv = lax.fori_loop(j + 1, nkt, group, carry)
        dk_ref[...] = dk * _LN2
        dv_ref[...] = dv

        @pl.when(j == nkt - 1)
        def _():
            dq_ref[...] = dq_ref[...] * _LN2

    return pl.pallas_call(
        body, grid=(MLA_HEADS, nkt),
        in_specs=[pl.BlockSpec((t, 2 * LANE), lambda h, j: (0, h)), pl.BlockSpec((tk, 2 * LANE), lambda h, j: (j, h)),
                  pl.BlockSpec((tk, V_HEAD), lambda h, j: (j, h)), pl.BlockSpec((t, V_HEAD), lambda h, j: (0, h)),
                  pl.BlockSpec((t, LANE), lambda h, j: (0, h)), pl.BlockSpec((t, LANE), lambda h, j: (0, h))],
        out_specs=[pl.BlockSpec((t, 2 * LANE), lambda h, j: (0, h)), pl.BlockSpec((tk, 2 * LANE), lambda h, j: (j, h)),
                   pl.BlockSpec((tk, V_HEAD), lambda h, j: (j, h))],
        out_shape=[jax.ShapeDtypeStruct((t, MLA_HEADS * 2 * LANE), F32), jax.ShapeDtypeStruct((t, MLA_HEADS * 2 * LANE), F32),
                   jax.ShapeDtypeStruct((t, MLA_VW), F32)],
        name="attn_bwd", compiler_params=pltpu.CompilerParams(dimension_semantics=("parallel", "arbitrary")),
    )(q, k, v, do, lse, delta)


def _adam_update(w, g, m, v):
    mm = ADAM_B1 * m + (1.0 - ADAM_B1) * g
    vv = ADAM_B2 * v + (1.0 - ADAM_B2) * jnp.square(g)
    m_hat = mm / (1.0 - ADAM_B1 ** ADAM_STEP)
    v_hat = vv / (1.0 - ADAM_B2 ** ADAM_STEP)
    return -ADAM_LR * (m_hat / (jnp.sqrt(v_hat) + ADAM_EPS) + ADAM_WD * w), mm, vv


def _adamw(w, g, m, v, name):
    r, c = w.shape
    tr = _pick(r, (256, 128, 64, 32, 16, 8))
    slots = g.ndim == 3

    def body(w_ref, g_ref, m_ref, v_ref, g_out, d_ref, nm_ref, nv_ref):
        if slots:
            gg = g_ref[0].astype(F32)
            for s in range(1, N_DEV):
                gg = gg + g_ref[s].astype(F32)
        else:
            gg = g_ref[...]
        g_out[...] = gg
        d_ref[...], nm_ref[...], nv_ref[...] = _adam_update(w_ref[...], gg, m_ref[...], v_ref[...])

    spec = pl.BlockSpec((tr, c), lambda i: (i, 0))
    g_spec = pl.BlockSpec((N_DEV, tr, c), lambda i: (0, i, 0)) if slots else spec
    return pl.pallas_call(
        body, grid=(r // tr,), in_specs=[spec, g_spec, spec, spec], out_specs=[spec] * 4,
        out_shape=[jax.ShapeDtypeStruct((r, c), F32)] * 4, name=name,
        compiler_params=pltpu.CompilerParams(dimension_semantics=("arbitrary",)),
    )(w, g, m, v)


def _adamw_many(ws, gs, ms, vs, name):
    n = len(ws)

    def body(*refs):
        for i in range(n):
            w_ref, g_ref, m_ref, v_ref = (refs[j * n + i] for j in range(4))
            d_ref, nm_ref, nv_ref = (refs[(4 + j) * n + i] for j in range(3))
            d_ref[...], nm_ref[...], nv_ref[...] = _adam_update(w_ref[...], g_ref[...], m_ref[...], v_ref[...])

    shapes = [jax.ShapeDtypeStruct(w.shape, F32) for w in ws]
    outs = pl.pallas_call(body, out_shape=shapes * 3, name=name)(*ws, *gs, *ms, *vs)
    return outs[:n], outs[n:2 * n], outs[2 * n:]


def _cast_bf16(xs, name, after=None):
    n = len(xs)
    extra = [] if after is None else [after]

    def body(*refs):
        outs = refs[n + len(extra):]
        for i in range(n):
            outs[i][...] = refs[i][...].astype(BF)

    vmem = pl.BlockSpec(memory_space=pltpu.VMEM)
    return pl.pallas_call(
        body, out_shape=[jax.ShapeDtypeStruct(x.shape, BF) for x in xs], name=name,
        in_specs=[vmem] * n + [pl.BlockSpec(memory_space=pl.ANY)] * len(extra), out_specs=[vmem] * n)(*xs, *extra)


def _pad_cols(a, n):
    return jnp.pad(a, ((0, 0), (0, n - a.shape[1])))


def _w_in_to_padded(w):
    s_ba = P_CQ
    s_cq = s_ba + 2 * DN_HEADS
    s_kr = s_cq + Q_LORA + KV_LORA
    return jnp.concatenate([w[:, :s_ba], w[:, s_cq:s_kr], _pad_cols(w[:, s_ba:s_cq], LANE), _pad_cols(w[:, s_kr:], LANE)], axis=1)


def _w_in_from_padded(w):
    return jnp.concatenate([w[:, :P_CQ], w[:, P_BA:P_BA + 2 * DN_HEADS], w[:, P_CQ:P_BA], w[:, P_KR:P_KR + QK_ROPE]], axis=1)


def _w_uq_to_padded(w):
    w3 = w.reshape(Q_LORA, MLA_HEADS, QK_NOPE + QK_ROPE)
    nope = w3[:, :, :QK_NOPE].reshape(Q_LORA, MLA_HEADS * QK_NOPE)
    rope = jnp.pad(w3[:, :, QK_NOPE:], ((0, 0), (0, 0), (0, LANE - QK_ROPE))).reshape(Q_LORA, MLA_HEADS * LANE)
    return jnp.concatenate([nope, rope], axis=1)


def _w_uq_from_padded(w):
    nope = w[:, :MLA_HEADS * QK_NOPE].reshape(Q_LORA, MLA_HEADS, QK_NOPE)
    rope = w[:, MLA_HEADS * QK_NOPE:].reshape(Q_LORA, MLA_HEADS, LANE)[:, :, :QK_ROPE]
    return jnp.concatenate([nope, rope], axis=2).reshape(Q_LORA, MLA_HEADS * (QK_NOPE + QK_ROPE))


def _pack(pieces, width, row_mult):
    flat = jnp.concatenate([p.reshape(-1) for p in pieces])
    n = flat.shape[0]
    rows = -(-n // (width * row_mult)) * row_mult
    return jnp.pad(flat, (0, rows * width - n)).reshape(rows, width)


def _unpack(flat, shapes):
    out, o = [], 0
    for s in shapes:
        n = math.prod(s)
        out.append(flat[o:o + n].reshape(s))
        o += n
    return out


def kernel(x, c, positions, w_ada, b_ada, w_in, conv_w, a_log, dt_bias, dn_norm_g, q_norm_g, w_uq, kv_norm_g, w_ukv, w_o, ln1_g, ln1_b, w_gate, w_up, w_down, ln2_g, ln2_b, loss_target, m_w_ada, m_b_ada, m_w_in, m_conv_w, m_a_log, m_dt_bias, m_dn_norm_g, m_q_norm_g, m_w_uq, m_kv_norm_g, m_w_ukv, m_w_o, m_ln1_g, m_ln1_b, m_w_gate, m_w_up, m_w_down, m_ln2_g, m_ln2_b, v_w_ada, v_b_ada, v_w_in, v_conv_w, v_a_log, v_dt_bias, v_dn_norm_g, v_q_norm_g, v_w_uq, v_kv_norm_g, v_w_ukv, v_w_o, v_ln1_g, v_ln1_b, v_w_gate, v_w_up, v_w_down, v_ln2_g, v_ln2_b):
    me = 4 * lax.axis_index("x") + 2 * lax.axis_index("y") + lax.axis_index("c")
    t, d = x.shape[1], x.shape[2]
    ada_n = w_ada.shape[2]

    cw = conv_w.shape[3]
    c_all, conv_all = _exchange([c, conv_w[0, :, 0, :]], "gather_small", scatter=False)
    c_all = c_all.reshape(N_DEV, d)
    conv_full = conv_all.transpose(1, 0, 2).reshape(CONV_K, N_DEV * cw)
    conv_w8 = jnp.pad(conv_full, ((0, 8 - CONV_K), (0, 0)))

    b_ada_mine = lax.dynamic_slice(b_ada, (0, me * ada_n), (1, ada_n))
    mod_cols = _mod_fwd(c_all, w_ada[0], b_ada_mine)
    (mod_all,) = _exchange([mod_cols.reshape(N_DEV, 1, ada_n)], "scatter_mod", scatter=True)
    mod = mod_all.reshape(1, N_DEV * ada_n)

    (in_shard,) = _cast_bf16([w_in[0]], "cast_w_in")
    (a_in,) = _exchange([in_shard], "gather_w_in", scatter=False)
    later = _cast_bf16([w_uq[0], w_ukv[0], w_o[0], w_gate[0], w_up[0], w_down[0]], "cast_weights", after=a_in)
    mixer_gather, _ = _exchange_start(later[:3], "gather_mixer_weights_start", scatter=False)
    ffn_gather, _ = _exchange_start(later[3:], "gather_ffn_weights_start", scatter=False)
    cols = lambda a: a.transpose(1, 0, 2).reshape(a.shape[1], N_DEV * a.shape[2])
    w_in_p = _w_in_to_padded(cols(a_in))

    def mixer_weights(after):
        a_uq, a_ukv, a_o = _exchange_wait(mixer_gather, after, "gather_mixer_weights_wait", scatter=False)
        return _w_uq_to_padded(cols(a_uq)), cols(a_ukv), a_o.reshape(-1, d)

    def ffn_weights(after):
        a_gate, a_up, a_down = _exchange_wait(ffn_gather, after, "gather_ffn_weights_wait", scatter=False)
        return cols(a_gate), cols(a_up), a_down.reshape(-1, d)

    def by_dest_cols(g, n):
        return g.reshape(g.shape[0], N_DEV, n).transpose(1, 0, 2)

    def by_dest_rows(g, r):
        return g.reshape(N_DEV, r, g.shape[1])

    scatters = {}

    def grads_ready(tag, *g):
        if tag == "ffn":
            g_w_gate, g_w_up, g_w_down = g
            pieces = [by_dest_cols(g_w_gate, w_gate.shape[2]), by_dest_cols(g_w_up, w_up.shape[2]), by_dest_rows(g_w_down, w_down.shape[1])]
        elif tag == "mixer":
            g_w_o, g_w_uq_p, g_w_ukv = g
            pieces = [by_dest_rows(g_w_o, w_o.shape[1]), by_dest_cols(_w_uq_from_padded(g_w_uq_p).astype(BF), w_uq.shape[2]),
                      by_dest_cols(g_w_ukv.astype(BF), w_ukv.shape[2])]
        else:
            pieces = [by_dest_cols(_w_in_from_padded(g[0]), w_in.shape[2])]
        scatters[tag], token = _exchange_start(pieces, "scatter_%s_grads_start" % tag, scatter=True)
        return token

    loc = _local_step(x[0], loss_target[0], positions[0], mod, w_in_p, mixer_weights, ffn_weights, grads_ready,
                      conv_w8, a_log, dt_bias, dn_norm_g, q_norm_g, kv_norm_g, ln1_g, ln1_b, ln2_g, ln2_b)
    grad_x, loss_acc, dmod, d_conv8, d_al8, d_dt8, d_dn_g, d_q_g, d_kv_g, d_ln1_g, d_ln1_b, d_ln2_g, d_ln2_b = loc

    small_shapes = [(6 * d,), (CONV_K, N_DEV * cw), (DN_HEADS,), (DN_HEADS,), (DN_DV,), (Q_LORA,), (KV_LORA,), (d,), (d,), (d,), (d,), (1,)]
    gsmall = _pack([dmod, d_conv8[:CONV_K], d_al8[0, :DN_HEADS], d_dt8[0, :DN_HEADS], d_dn_g, d_q_g, d_kv_g,
                    d_ln1_g, d_ln1_b, d_ln2_g, d_ln2_b, loss_acc[0, :1]], LANE, 8)
    (gsmall_all,) = _exchange([gsmall], "gather_small_grads", scatter=False)
    dmod_all = gsmall_all.reshape(N_DEV, -1)[:, :6 * d]
    tot = _unpack(_sum_slots(gsmall_all, "sum_small_grads").reshape(-1), small_shapes)
    g_b_ada, g_conv_full, g_a_log, g_dt_bias, g_dn_g, g_q_g, g_kv_g, g_ln1_g, g_ln1_b, g_ln2_g, g_ln2_b, loss1 = tot
    loss = loss1.reshape(())
    g_conv_w = lax.dynamic_slice(g_conv_full, (0, me * cw), (CONV_K, cw))
    g_w_ada = _mod_bwd(c_all.T, lax.dynamic_slice(dmod_all, (0, me * ada_n), (N_DEV, ada_n)))

    grads = {"w_ada": g_w_ada[None], "b_ada": g_b_ada[None], "conv_w": g_conv_w[None, :, None, :],
             "a_log": g_a_log[None], "dt_bias": g_dt_bias[None], "dn_norm_g": g_dn_g[None], "q_norm_g": g_q_g[None],
             "kv_norm_g": g_kv_g[None], "ln1_g": g_ln1_g[None], "ln1_b": g_ln1_b[None], "ln2_g": g_ln2_g[None], "ln2_b": g_ln2_b[None]}
    weights = dict(w_ada=w_ada, b_ada=b_ada, w_in=w_in, conv_w=conv_w, a_log=a_log, dt_bias=dt_bias, dn_norm_g=dn_norm_g,
                   q_norm_g=q_norm_g, w_uq=w_uq, kv_norm_g=kv_norm_g, w_ukv=w_ukv, w_o=w_o, ln1_g=ln1_g, ln1_b=ln1_b,
                   w_gate=w_gate, w_up=w_up, w_down=w_down, ln2_g=ln2_g, ln2_b=ln2_b)
    ms = dict(w_ada=m_w_ada, b_ada=m_b_ada, w_in=m_w_in, conv_w=m_conv_w, a_log=m_a_log, dt_bias=m_dt_bias,
              dn_norm_g=m_dn_norm_g, q_norm_g=m_q_norm_g, w_uq=m_w_uq, kv_norm_g=m_kv_norm_g, w_ukv=m_w_ukv, w_o=m_w_o,
              ln1_g=m_ln1_g, ln1_b=m_ln1_b, w_gate=m_w_gate, w_up=m_w_up, w_down=m_w_down, ln2_g=m_ln2_g, ln2_b=m_ln2_b)
    vs = dict(w_ada=v_w_ada, b_ada=v_b_ada, w_in=v_w_in, conv_w=v_conv_w, a_log=v_a_log, dt_bias=v_dt_bias,
              dn_norm_g=v_dn_norm_g, q_norm_g=v_q_norm_g, w_uq=v_w_uq, kv_norm_g=v_kv_norm_g, w_ukv=v_w_ukv, w_o=v_w_o,
              ln1_g=v_ln1_g, ln1_b=v_ln1_b, w_gate=v_w_gate, w_up=v_w_up, w_down=v_w_down, ln2_g=v_ln2_g, ln2_b=v_ln2_b)
    names = list(weights)
    big = ("w_ada", "w_gate", "w_up", "w_down", "w_o", "w_uq", "w_ukv", "w_in")
    waits = {"w_gate": ("ffn", ("w_gate", "w_up", "w_down")), "w_o": ("mixer", ("w_o", "w_uq", "w_ukv")), "w_in": ("in", ("w_in",))}
    delta_w, new_m, new_v, slots = {}, {}, {}, {}
    last = g_w_ada
    for n in big:
        if n == "w_in":
            rest = [r for r in names if r not in big]
            flat2 = lambda a: a.reshape(-1, a.shape[-1])
            outs = _adamw_many(*[[flat2(src[r]) for r in rest] for src in (weights, grads, ms, vs)], "adamw_small")
            for dst, o in zip((delta_w, new_m, new_v), outs):
                for r, a in zip(rest, o):
                    dst[r] = a.reshape(weights[r].shape)
            last = outs[0][0]
        shp = weights[n].shape
        two = lambda a: a.reshape(shp[-2], shp[-1])
        if n in waits:
            tag, members = waits[n]
            slots.update(zip(members, _exchange_wait(scatters[tag], last, "scatter_%s_grads_wait" % tag, scatter=True)))
        g_in = slots[n] if n in slots else two(grads[n])
        gr, dlt, nm, nv = _adamw(two(weights[n]), g_in, two(ms[n]), two(vs[n]), "adamw_" + n)
        grads[n], delta_w[n], new_m[n], new_v[n] = gr.reshape(shp), dlt.reshape(shp), nm.reshape(shp), nv.reshape(shp)
        last = nv

    return (loss, grad_x[None], *[grads[n] for n in names], *[delta_w[n] for n in names],
            *[new_m[n] for n in names], *[new_v[n] for n in names])


def _local_step(xs, tgt, pos, mod, w_in_p, mixer_weights, ffn_weights, grads_ready, conv_w8,
                a_log, dt_bias, dn_norm_g, q_norm_g, kv_norm_g, ln1_g, ln1_b, ln2_g, ln2_b):
    t, d = xs.shape
    sh_m, sc_m, gt_m, sh_f, sc_f, gt_f = [mod[:, i * d:(i + 1) * d] for i in range(6)]
    pos_col = pos.astype(F32).reshape(t, 1)
    inv_freq = 1.0 / (ROPE_THETA ** (jnp.arange(0, QK_ROPE, 2, dtype=F32) / QK_ROPE))
    inv_freq2 = jnp.pad(jnp.concatenate([inv_freq, inv_freq]), (0, LANE - QK_ROPE)).reshape(1, LANE)
    al8 = jnp.pad(a_log, ((0, 7), (0, LANE - DN_HEADS)))
    dt8 = jnp.pad(dt_bias, ((0, 7), (0, LANE - DN_HEADS)))

    tm = min(512, t)
    tq = min(256, t)
    tk = min(512, t)

    (h1,) = _rowwise("modulate_in", lambda xx, sc, sh: xx * (1.0 + sc) + sh, [xs], [sc_m, sh_m], [(d, BF)], [], tm)
    proj = _matmul(h1, w_in_p, "nn", "in_proj")
    qkv = _conv_fwd(proj, conv_w8, min(256, t))
    gdn_tm = min(512, t)
    intra = _gdn_intra_fwd(qkv, proj, al8, dt8, gdn_tm)
    o_dn, states = _gdn_scan_fwd(intra, gdn_tm)
    w_uq_p, w_ukv_f, w_o_f = mixer_weights(states)
    qc, kc, vc = _mla_prep_fwd(proj, pos_col, inv_freq2, q_norm_g, kv_norm_g, w_uq_p, w_ukv_f, tm)
    o_mla, lse = _attn_fwd(qc, kc, vc, tk, tk)

    def mix_in(o, z, om, g):
        return jnp.concatenate(_gdn_out(o, z, g) + [om], axis=1)

    (mixin,) = _rowwise("mixer_out", mix_in, [o_dn, (proj, DN_VW, P_Z // DN_VW), o_mla], [dn_norm_g], [(2 * DN_VW, BF)], [], tm)
    mix = _matmul(mixin, w_o_f, "nn", "out_proj")

    def block1(xx, mx, gt, g1, b1, sc, sh):
        x1 = _layernorm(DEEPNORM_ALPHA * xx + gt * mx, g1, b1)
        return x1, x1 * (1.0 + sc) + sh

    x1, h2 = _rowwise("norm1_modulate", block1, [xs, mix], [gt_m, ln1_g, ln1_b, sc_f, sh_f], [(d, F32), (d, BF)], [], tm)
    w_gate_f, w_up_f, w_down_f = ffn_weights(h2)
    act, gate, up = _ffn_in(h2, w_gate_f, w_up_f)
    ff = _matmul(act, w_down_f, "nn", "ffn_out")

    def tail_loss(x1_, ff_, gt, g2, b2, tg):
        y = _layernorm(DEEPNORM_ALPHA * x1_ + gt * ff_, g2, b2)
        return 0.5 * jnp.sum(jnp.mean(jnp.square(y - tg), axis=-1))

    def tail(x1_, ff_, tg, gt, g2, b2):
        loss, (dx1, dff, dgt, dg2, db2) = jax.value_and_grad(tail_loss, argnums=(0, 1, 2, 3, 4))(x1_, ff_, gt, g2, b2, tg)
        return dx1, dff, jnp.full((1, LANE), loss, F32), dgt, dg2, db2

    dx1_a, dff, loss_acc, d_gt_f, d_ln2_g, d_ln2_b = _rowwise(
        "norm2_loss", tail, [x1, ff, tgt], [gt_f, ln2_g, ln2_b], [(d, F32), (d, BF)], [(1, LANE), (1, d), (1, d), (1, d)], tm)

    g_w_down = _matmul(act, dff, "tn", "d_w_down", BF)
    dgate, dup = _ffn_act_bwd(dff, w_down_f, gate, up)
    g_w_gate = _matmul(h2, dgate, "tn", "d_w_gate", BF)
    g_w_up = _matmul(h2, dup, "tn", "d_w_up", BF)
    token = grads_ready("ffn", g_w_gate, g_w_up, g_w_down)
    dh2 = _matmul2_nt(dgate, w_gate_f, dup, w_up_f, "d_ffn_in")

    def block1_bwd(xx, mx, dx1_, dh2_, gt, g1, b1, sc, sh):
        _, vjp = jax.vjp(block1, xx, mx, gt, g1, b1, sc, sh)
        dxx, dmx, dgt, dg1, db1, dsc, dsh = vjp((dx1_, dh2_))
        return dxx, dmx, dgt, dg1, db1, dsc, dsh

    dx_a, dmix, d_gt_m, d_ln1_g, d_ln1_b, d_sc_f, d_sh_f = _rowwise(
        "norm1_modulate_bwd", block1_bwd, [xs, mix, dx1_a, dh2], [gt_m + token, ln1_g, ln1_b, sc_f, sh_f],
        [(d, F32), (d, BF)], [(1, d)] * 5, min(256, t))

    dmixin = _matmul(dmix, w_o_f, "nt", "d_mixer_out")
    g_w_o = _matmul(mixin, dmix, "tn", "d_w_o", BF)

    def mixer_bwd(o, z, om, dmi, g):
        _, vjp = jax.vjp(lambda o_, z_, g_: jnp.concatenate(_gdn_out(o_, z_, g_), axis=1), o, z, g)
        do_, dz_, dg_ = vjp(dmi[:, :DN_VW])
        dom = dmi[:, DN_VW:]
        delta = [jnp.broadcast_to(jnp.sum(dom[:, h * V_HEAD:(h + 1) * V_HEAD] * om[:, h * V_HEAD:(h + 1) * V_HEAD], axis=-1, keepdims=True), (o.shape[0], LANE))
                 for h in range(MLA_HEADS)]
        return do_, dz_, dom, jnp.concatenate(delta, axis=1), dg_

    do_dn, dz, do_mla, delta, d_dn_g = _rowwise(
        "mixer_out_bwd", mixer_bwd, [o_dn, (proj, DN_VW, P_Z // DN_VW), o_mla, dmixin], [dn_norm_g],
        [(DN_VW, F32), (DN_VW, BF), (MLA_VW, BF), (MLA_HEADS * LANE, F32)], [(1, DN_DV)], tm)

    dqc, dkc, dvc = _attn_bwd(qc, kc, vc, do_mla, lse, delta, tq, tk)
    dcq, dckv, dkr, d_q_g, d_kv_g, g_w_uq_p, g_w_ukv = _mla_prep_bwd(
        proj, pos_col, inv_freq2, q_norm_g, kv_norm_g, w_uq_p, w_ukv_f, dqc, dkc, dvc, min(256, t))

    token = grads_ready("mixer", g_w_o, g_w_uq_p, g_w_ukv)

    d_intra = _gdn_scan_bwd(intra, states, do_dn, gdn_tm)
    dqkv_act, dba, d_al8, d_dt8 = _gdn_intra_bwd(qkv, proj, al8 + token, dt8, d_intra, min(256, t))
    dqkv_pre, d_conv8 = _conv_bwd(proj, conv_w8, dqkv_act, min(256, t))

    dproj = jnp.concatenate([dqkv_pre, dz, dcq, dckv, dba, dkr], axis=1)
    dh1 = _matmul(dproj, w_in_p, "nt", "d_in_proj")
    g_w_in_p = _matmul(h1, dproj, "tn", "d_w_in", BF)
    token = grads_ready("in", g_w_in_p)

    def modulate_bwd(xx, dh, dxa, sc):
        return dh * (1.0 + sc) + dxa, jnp.sum(dh * xx, axis=0, keepdims=True), jnp.sum(dh, axis=0, keepdims=True)

    grad_x, d_sc_m, d_sh_m = _rowwise("modulate_in_bwd", modulate_bwd, [xs, dh1, dx_a], [sc_m + token], [(d, F32)], [(1, d), (1, d)], tm)
    dmod = jnp.concatenate([d_sh_m, d_sc_m, d_gt_m, d_sh_f, d_sc_f, d_gt_f], axis=1)
    return grad_x, loss_acc, dmod, d_conv8, d_al8, d_dt8, d_dn_g, d_q_g, d_kv_g, d_ln1_g, d_ln1_b, d_ln2_g, d_ln2_b
```

```python
import functools
import math

import jax
import jax.numpy as jnp
from jax import lax
from jax.experimental import pallas as pl
from jax.experimental.pallas import tpu as pltpu

F32 = jnp.float32
BF = jnp.bfloat16
HI = lax.Precision.HIGHEST

N_DEV = 8
DN_HEADS = 4
DN_DK = 128
DN_DV = 128
CONV_K = 4
CHUNK = 64
MLA_HEADS = 4
QK_NOPE = 128
QK_ROPE = 64
V_HEAD = 128
Q_LORA = 512
KV_LORA = 256
ROPE_THETA = 10000.0
DEPTH = 1
DEEPNORM_ALPHA = (2.0 * DEPTH) ** 0.25
LANE = 128
CONV_HALO = 8

DN_QK = DN_HEADS * DN_DK
DN_VW = DN_HEADS * DN_DV
DN_CONV_CH = 2 * DN_QK + DN_VW
MLA_VW = MLA_HEADS * V_HEAD
MLA_QCAT = QK_NOPE + LANE
N_IN = DN_CONV_CH + DN_VW + 2 * DN_HEADS + Q_LORA + KV_LORA + QK_ROPE
P_QKV = 0
P_Z = DN_CONV_CH
P_CQ = P_Z + DN_VW
P_CKV = P_CQ + Q_LORA
P_BA = P_CKV + KV_LORA
P_KR = P_BA + LANE
N_INP = P_KR + LANE

ADAM_LR = 0.001
ADAM_B1 = 0.9
ADAM_B2 = 0.999
ADAM_EPS = 1e-08
ADAM_WD = 0.01
ADAM_STEP = 10

NN = (((1,), (0,)), ((), ()))
NT = (((1,), (1,)), ((), ()))
TN = (((0,), (0,)), ((), ()))


def _pick(n, prefs):
    for p in prefs:
        if n % p == 0:
            return p
    return n


def _full(shape):
    return pl.BlockSpec(shape, lambda *_: (0,) * len(shape))


def _dot(a, b, dims=NN):
    return lax.dot_general(a, b, dims, preferred_element_type=F32)


def _doth(a, b, dims=NN):
    return lax.dot_general(a, b, dims, precision=HI, preferred_element_type=F32)


@jax.custom_vjp
def _mmb(a, b):
    return _dot(a.astype(BF), b.astype(BF), NN)


def _mmb_fwd(a, b):
    return _mmb(a, b), (a, b)


def _mmb_bwd(res, g):
    a, b = res
    gb = g.astype(BF)
    return (_dot(gb, b.astype(BF), NT).astype(a.dtype), _dot(a.astype(BF), gb, TN).astype(b.dtype))


_mmb.defvjp(_mmb_fwd, _mmb_bwd)


@jax.custom_vjp
def _mmb_nt(a, b):
    return _dot(a.astype(BF), b.astype(BF), NT)


def _mmb_nt_fwd(a, b):
    return _mmb_nt(a, b), (a, b)


def _mmb_nt_bwd(res, g):
    a, b = res
    gb = g.astype(BF)
    return (_dot(gb, b.astype(BF), NN).astype(a.dtype), _dot(gb, a.astype(BF), TN).astype(b.dtype))


_mmb_nt.defvjp(_mmb_nt_fwd, _mmb_nt_bwd)


@jax.custom_vjp
def _mmb_tn(a, b):
    return _dot(a.astype(BF), b.astype(BF), TN)


def _mmb_tn_fwd(a, b):
    return _mmb_tn(a, b), (a, b)


def _mmb_tn_bwd(res, g):
    a, b = res
    gb = g.astype(BF)
    return (_dot(b.astype(BF), gb, NT).astype(a.dtype), _dot(a.astype(BF), gb, NN).astype(b.dtype))


_mmb_tn.defvjp(_mmb_tn_fwd, _mmb_tn_bwd)


def _sigmoid(x):
    return 0.5 * (jnp.tanh(0.5 * x) + 1.0)


def _silu(x):
    return x * _sigmoid(x)


def _softplus(x):
    return jnp.maximum(x, 0.0) + jnp.log(1.0 + jnp.exp(-jnp.abs(x)))


def _layernorm(x, g, b, eps=1e-5):
    mu = jnp.mean(x, axis=-1, keepdims=True)
    xc = x - mu
    var = jnp.mean(xc * xc, axis=-1, keepdims=True)
    return xc * lax.rsqrt(var + eps) * g + b


def _rmsnorm(x, g, eps=1e-6):
    return x * lax.rsqrt(jnp.mean(x * x, axis=-1, keepdims=True) + eps) * g


def _l2norm(x, eps=1e-6):
    return x * lax.rsqrt(jnp.sum(x * x, axis=-1, keepdims=True) + eps)


def _rowwise(name, fn, rows, vecs, out_rows, out_accs, tm):
    rows = [r if isinstance(r, tuple) else (r, r.shape[1], 0) for r in rows]
    t = rows[0][0].shape[0]
    tm = min(tm, t)
    assert t % tm == 0
    nr, nv, no = len(rows), len(vecs), len(out_rows)

    def body(*refs):
        ins = [r[...] for r in refs[:nr + nv]]
        outs = fn(*ins)
        outs = outs if isinstance(outs, (tuple, list)) else (outs,)
        o_rows = refs[nr + nv:nr + nv + no]
        o_accs = refs[nr + nv + no:]
        for o, val in zip(o_rows, outs[:no]):
            o[...] = val.astype(o.dtype)
        if o_accs:
            @pl.when(pl.program_id(0) == 0)
            def _():
                for o in o_accs:
                    o[...] = jnp.zeros_like(o)
            for o, val in zip(o_accs, outs[no:]):
                o[...] += val

    in_specs = [pl.BlockSpec((tm, w), functools.partial(lambda i, j: (i, j), j=j)) for (_, w, j) in rows]
    in_specs += [_full(v.shape) for v in vecs]
    out_specs = [pl.BlockSpec((tm, w), lambda i: (i, 0)) for (w, _) in out_rows]
    out_specs += [_full(s) for s in out_accs]
    out_shape = [jax.ShapeDtypeStruct((t, w), d) for (w, d) in out_rows]
    out_shape += [jax.ShapeDtypeStruct(s, F32) for s in out_accs]
    res = pl.pallas_call(
        body, grid=(t // tm,), in_specs=in_specs, out_specs=out_specs, out_shape=out_shape, name=name,
        compiler_params=pltpu.CompilerParams(dimension_semantics=("arbitrary",)),
    )(*[r[0] for r in rows], *vecs)
    return res


def _matmul(a, b, mode, name, out_dtype=F32):
    if mode == "nn":
        (m, k), n = a.shape, b.shape[1]
    elif mode == "nt":
        (m, k), n = a.shape, b.shape[0]
    else:
        (k, m), n = a.shape, b.shape[1]
    tm, tn, tk = _matmul_tiles(m, n, k, a.dtype.itemsize, b.dtype.itemsize, jnp.dtype(out_dtype).itemsize)
    nk = k // tk
    dims = {"nn": NN, "nt": NT, "tn": TN}[mode]

    def body(a_ref, b_ref, o_ref, *acc):
        part = _dot(a_ref[...].astype(BF), b_ref[...].astype(BF), dims)
        if nk == 1:
            o_ref[...] = part.astype(o_ref.dtype)
            return
        (acc_ref,) = acc
        kk = pl.program_id(2)

        @pl.when(kk == 0)
        def _():
            acc_ref[...] = part

        @pl.when(kk > 0)
        def _():
            acc_ref[...] += part

        @pl.when(kk == nk - 1)
        def _():
            o_ref[...] = acc_ref[...].astype(o_ref.dtype)

    a_spec = pl.BlockSpec((tk, tm), lambda i, j, kk: (kk, i)) if mode == "tn" else pl.BlockSpec((tm, tk), lambda i, j, kk: (i, kk))
    b_spec = pl.BlockSpec((tn, tk), lambda i, j, kk: (j, kk)) if mode == "nt" else pl.BlockSpec((tk, tn), lambda i, j, kk: (kk, j))
    return pl.pallas_call(
        body, grid=(m // tm, n // tn, nk), in_specs=[a_spec, b_spec],
        out_specs=pl.BlockSpec((tm, tn), lambda i, j, kk: (i, j)),
        out_shape=jax.ShapeDtypeStruct((m, n), out_dtype),
        scratch_shapes=[pltpu.VMEM((tm, tn), F32)] if nk > 1 else [], name=name,
        compiler_params=pltpu.CompilerParams(dimension_semantics=("parallel", "parallel", "arbitrary")),
    )(a, b)


def _lane_tile(n, cap):
    return max([n // s for s in range(1, n // LANE + 1) if n % s == 0 and (n // s) % LANE == 0 and n // s <= cap] or [n])


def _ffn_in(h, w_gate, w_up):
    m, k = h.shape
    f = w_gate.shape[0]
    tm, tn = _pick(m, (512, 256, 128)), _lane_tile(f, 1408)

    def body(h_ref, wg_ref, wu_ref, act_ref, g_ref, u_ref):
        hh = h_ref[...]
        g = _dot(hh, wg_ref[...], NT)
        u = _dot(hh, wu_ref[...], NT)
        act_ref[...] = (_silu(g) * u).astype(act_ref.dtype)
        g_ref[...] = g.astype(g_ref.dtype)
        u_ref[...] = u.astype(u_ref.dtype)

    w_spec = pl.BlockSpec((tn, k), lambda i, j: (j, 0))
    o_spec = pl.BlockSpec((tm, tn), lambda i, j: (i, j))
    return pl.pallas_call(
        body, grid=(m // tm, f // tn), in_specs=[pl.BlockSpec((tm, k), lambda i, j: (i, 0)), w_spec, w_spec],
        out_specs=[o_spec] * 3, out_shape=[jax.ShapeDtypeStruct((m, f), BF)] * 3, name="ffn_in",
        compiler_params=pltpu.CompilerParams(dimension_semantics=("parallel", "parallel")),
    )(h, w_gate, w_up)


def _ffn_act_bwd(dff, w_down, gate, up):
    m, k = dff.shape
    f = w_down.shape[0]
    tm, tn = _pick(m, (512, 256, 128)), _lane_tile(f, 1408)

    def body(d_ref, w_ref, g_ref, u_ref, dg_ref, du_ref):
        da = _dot(d_ref[...], w_ref[...], NT)
        g = g_ref[...].astype(F32)
        sg = _sigmoid(g)
        dg_ref[...] = (da * u_ref[...].astype(F32) * (sg * (1.0 + g * (1.0 - sg)))).astype(dg_ref.dtype)
        du_ref[...] = (da * (g * sg)).astype(du_ref.dtype)

    o_spec = pl.BlockSpec((tm, tn), lambda i, j: (i, j))
    return pl.pallas_call(
        body, grid=(m // tm, f // tn),
        in_specs=[pl.BlockSpec((tm, k), lambda i, j: (i, 0)), pl.BlockSpec((tn, k), lambda i, j: (j, 0)), o_spec, o_spec],
        out_specs=[o_spec] * 2, out_shape=[jax.ShapeDtypeStruct((m, f), BF)] * 2, name="d_ffn_act",
        compiler_params=pltpu.CompilerParams(dimension_semantics=("parallel", "parallel")),
    )(dff, w_down, gate, up)


def _matmul2_nn(a1, b1, a2, b2, name):
    m, k = a1.shape
    n = b1.shape[1]
    tm, tn = _pick(m, (512, 256, 128)), _pick(n, (512, 256, 128))

    def body(a1_ref, b1_ref, a2_ref, b2_ref, o_ref):
        o_ref[...] = _dot(a1_ref[...], b1_ref[...]) + _dot(a2_ref[...], b2_ref[...])

    a_spec = pl.BlockSpec((tm, k), lambda i, j: (i, 0))
    b_spec = pl.BlockSpec((k, tn), lambda i, j: (0, j))
    return pl.pallas_call(
        body, grid=(m // tm, n // tn), in_specs=[a_spec, b_spec, a_spec, b_spec],
        out_specs=pl.BlockSpec((tm, tn), lambda i, j: (i, j)), out_shape=jax.ShapeDtypeStruct((m, n), F32), name=name,
        compiler_params=pltpu.CompilerParams(dimension_semantics=("parallel", "parallel")),
    )(a1, b1, a2, b2)


MATMUL_VMEM_BUDGET = 28 * 1024 * 1024


def _matmul_tiles(m, n, k, a_bytes, b_bytes, o_bytes):
    def divisors(x, cap):
        return sorted({x // s for s in range(1, 65) if x % s == 0 and (x // s) % LANE == 0 and x // s <= cap}, reverse=True) or [x]

    for tk in divisors(k, k):
        best = None
        for tm in divisors(m, 1024):
            for tn in divisors(n, 2048):
                need = 2 * (tm * tk * a_bytes + tk * tn * b_bytes + tm * tn * o_bytes) + (tm * tn * 4 if tk < k else 0)
                if need <= MATMUL_VMEM_BUDGET and tm * tn >= 512 * 512 and (best is None or tm * tn > best[0] * best[1]):
                    best = (tm, tn)
        if best:
            return best[0], best[1], tk
    return _pick(m, (512, 256, 128)), _pick(n, (512, 256, 128)), _pick(k, (512, 256, 128))


def _exchange(xs, name, scatter):
    n = len(xs)
    npeer = N_DEV - 1

    def body(*refs):
        x_refs, o_refs = refs[:n], refs[n:2 * n]
        send_sems, recv_sems, local_sems = refs[2 * n:]
        mx, my, mc = lax.axis_index("x"), lax.axis_index("y"), lax.axis_index("c")
        me = 4 * mx + 2 * my + mc
        src_me = [x.at[me] if scatter else x for x in x_refs]
        mine = [pltpu.make_async_copy(src_me[a], o_refs[a].at[me], local_sems.at[a]) for a in range(n)]
        for cp in mine:
            cp.start()
        copies = []
        for k in range(1, N_DEV):
            px, py, pc = mx ^ (k >> 2), my ^ ((k >> 1) & 1), mc ^ (k & 1)
            peer = 4 * px + 2 * py + pc
            for a in range(n):
                cp = pltpu.make_async_remote_copy(
                    src_ref=x_refs[a].at[peer] if scatter else x_refs[a], dst_ref=o_refs[a].at[me],
                    send_sem=send_sems.at[a * npeer + k - 1], recv_sem=recv_sems.at[a * npeer + k - 1],
                    device_id=(px, py, pc), device_id_type=pl.DeviceIdType.MESH)
                cp.start()
                copies.append((cp, a, k, peer))
        for cp, a, k, peer in copies:
            pltpu.make_async_remote_copy(
                src_ref=src_me[a], dst_ref=o_refs[a].at[peer], send_sem=send_sems.at[a * npeer + k - 1],
                recv_sem=recv_sems.at[a * npeer + k - 1], device_id=(mx, my, mc),
                device_id_type=pl.DeviceIdType.MESH).wait_recv()
        for cp, _, _, _ in copies:
            cp.wait_send()
        for cp in mine:
            cp.wait()

    return pl.pallas_call(
        body, out_shape=[jax.ShapeDtypeStruct((N_DEV,) + x.shape[-2:], x.dtype) for x in xs],
        in_specs=[pl.BlockSpec(memory_space=pl.ANY)] * n, out_specs=[pl.BlockSpec(memory_space=pl.ANY)] * n,
        scratch_shapes=[pltpu.SemaphoreType.DMA((n * npeer,)), pltpu.SemaphoreType.DMA((n * npeer,)),
                        pltpu.SemaphoreType.DMA((n,))],
        name=name,
    )(*xs)


def _peer_of(k):
    mx, my, mc = lax.axis_index("x"), lax.axis_index("y"), lax.axis_index("c")
    px, py, pc = mx ^ (k >> 2), my ^ ((k >> 1) & 1), mc ^ (k & 1)
    return (px, py, pc), 4 * px + 2 * py + pc


def _exchange_start(xs, name, scatter):
    n = len(xs)
    npeer = N_DEV - 1

    def body(*refs):
        x_refs, land_refs = refs[:n], refs[n:2 * n]
        send_sems, recv_sems, token = refs[2 * n], refs[2 * n + 1], refs[-1]
        me = 4 * lax.axis_index("x") + 2 * lax.axis_index("y") + lax.axis_index("c")
        for k in range(1, N_DEV):
            dev, peer = _peer_of(k)
            for a in range(n):
                pltpu.make_async_remote_copy(
                    src_ref=x_refs[a].at[peer] if scatter else x_refs[a], dst_ref=land_refs[a].at[me],
                    send_sem=send_sems.at[a * npeer + k - 1], recv_sem=recv_sems.at[a * npeer + k - 1],
                    device_id=dev, device_id_type=pl.DeviceIdType.MESH).start()
        token[...] = jnp.zeros_like(token)

    hbm = pl.BlockSpec(memory_space=pltpu.HBM)
    sem = pl.BlockSpec(memory_space=pltpu.SEMAPHORE)
    lands = [pltpu.with_memory_space_constraint(lax.empty((N_DEV,) + x.shape[-2:], x.dtype), pltpu.HBM) for x in xs]
    srcs = [pltpu.with_memory_space_constraint(x, pltpu.HBM) for x in xs]
    outs = pl.pallas_call(
        body, name=name,
        out_shape=(pltpu.SemaphoreType.DMA((n * npeer,)), pltpu.SemaphoreType.DMA((n * npeer,)),
                   *[pltpu.HBM(x.shape, x.dtype) for x in srcs], *[pltpu.HBM(z.shape, z.dtype) for z in lands],
                   jax.ShapeDtypeStruct((8, LANE), F32)),
        in_specs=[hbm] * (2 * n), out_specs=(sem, sem, *[hbm] * (2 * n), pl.BlockSpec(memory_space=pltpu.VMEM)),
        input_output_aliases={i: 2 + i for i in range(2 * n)},
        compiler_params=pltpu.CompilerParams(has_side_effects=pltpu.SideEffectType.DATAFLOW_SIDE_EFFECTING),
    )(*srcs, *lands)
    return (outs[0], outs[1], list(outs[2:2 + n]), list(outs[2 + n:2 + 2 * n])), outs[-1][0:1, 0:1]


def _exchange_wait(started, after, name, scatter):
    send_sems, recv_sems, srcs, lands = started
    n = len(srcs)
    npeer = N_DEV - 1

    def body(*refs):
        x_refs, land_refs = refs[:n], refs[n:2 * n]
        send_sems, recv_sems = refs[2 * n], refs[2 * n + 1]
        mx, my, mc = lax.axis_index("x"), lax.axis_index("y"), lax.axis_index("c")
        me = 4 * mx + 2 * my + mc
        for k in range(1, N_DEV):
            _, peer = _peer_of(k)
            for a in range(n):
                src = x_refs[a].at[me] if scatter else x_refs[a]
                cp = pltpu.make_async_remote_copy(
                    src_ref=src, dst_ref=land_refs[a].at[peer], send_sem=send_sems.at[a * npeer + k - 1],
                    recv_sem=recv_sems.at[a * npeer + k - 1], device_id=(mx, my, mc), device_id_type=pl.DeviceIdType.MESH)
                cp.wait_send()
                cp.wait_recv()

    hbm = pl.BlockSpec(memory_space=pltpu.HBM)
    sem = pl.BlockSpec(memory_space=pltpu.SEMAPHORE)
    outs = pl.pallas_call(
        body, name=name,
        out_shape=(*[pltpu.HBM(x.shape, x.dtype) for x in srcs], *[pltpu.HBM(z.shape, z.dtype) for z in lands]),
        in_specs=[hbm] * (2 * n) + [sem, sem, pl.BlockSpec(memory_space=pl.ANY)], out_specs=tuple([hbm] * (2 * n)),
        input_output_aliases={i: i for i in range(2 * n)},
        compiler_params=pltpu.CompilerParams(has_side_effects=pltpu.SideEffectType.DATAFLOW_SIDE_EFFECTING),
    )(*srcs, *lands, send_sems, recv_sems, after)
    me = 4 * lax.axis_index("x") + 2 * lax.axis_index("y") + lax.axis_index("c")
    full = []
    for x, land in zip(outs[:n], outs[n:]):
        own = lax.dynamic_slice(x, (me, 0, 0), (1,) + x.shape[1:]) if scatter else x[None]
        full.append(lax.dynamic_update_slice(land, own, (me, 0, 0)))
    return full


def _sum_slots(x, name):
    _, r, c = x.shape
    tr = _pick(r, (512, 256, 128, 64, 32, 16))

    def body(x_ref, o_ref):
        acc = x_ref[0].astype(F32)
        for s in range(1, N_DEV):
            acc = acc + x_ref[s].astype(F32)
        o_ref[...] = acc

    return pl.pallas_call(
        body, grid=(r // tr,), in_specs=[pl.BlockSpec((N_DEV, tr, c), lambda i: (0, i, 0))],
        out_specs=pl.BlockSpec((tr, c), lambda i: (i, 0)), out_shape=jax.ShapeDtypeStruct((r, c), F32), name=name,
        compiler_params=pltpu.CompilerParams(dimension_semantics=("arbitrary",)),
    )(x)


def _mod_fwd(c_all, w_ada, b_ada_mine):
    def body(c_ref, w_ref, b_ref, o_ref):
        o_ref[...] = _doth(_silu(c_ref[...]), w_ref[...]) + b_ref[...]

    return pl.pallas_call(body, out_shape=jax.ShapeDtypeStruct((c_all.shape[0], w_ada.shape[1]), F32), name="mod_fwd")(c_all, w_ada, b_ada_mine)


def _mod_bwd(c_all_t, dmod_mine):
    def body(ct_ref, d_ref, o_ref):
        s = _silu(ct_ref[...])
        acc = s[:, 0:1] * d_ref[pl.ds(0, 1), :]
        for b in range(1, N_DEV):
            acc = acc + s[:, b:b + 1] * d_ref[pl.ds(b, 1), :]
        o_ref[...] = acc

    return pl.pallas_call(body, out_shape=jax.ShapeDtypeStruct((c_all_t.shape[0], dmod_mine.shape[1]), F32), name="mod_bwd")(c_all_t, dmod_mine)


def _conv_fwd(proj, conv_w8, tm):
    t = proj.shape[0]
    ch = DN_CONV_CH

    def body(x_ref, w_ref, o_ref, buf):
        @pl.when(pl.program_id(0) == 0)
        def _():
            buf[pl.ds(0, CONV_HALO), :] = jnp.zeros((CONV_HALO, ch), F32)

        buf[pl.ds(CONV_HALO, tm), :] = x_ref[...]
        acc = jnp.zeros((tm, ch), F32)
        for j in range(CONV_K):
            acc = acc + buf[pl.ds(CONV_HALO - (CONV_K - 1) + j, tm), :] * w_ref[pl.ds(j, 1), :]
        o_ref[...] = _silu(acc)
        buf[pl.ds(0, CONV_HALO), :] = buf[pl.ds(tm, CONV_HALO), :]

    return pl.pallas_call(
        body, grid=(t // tm,), in_specs=[pl.BlockSpec((tm, ch), lambda i: (i, 0)), _full(conv_w8.shape)],
        out_specs=pl.BlockSpec((tm, ch), lambda i: (i, 0)), out_shape=jax.ShapeDtypeStruct((t, ch), F32),
        scratch_shapes=[pltpu.VMEM((tm + CONV_HALO, ch), F32)], name="conv_fwd",
        compiler_params=pltpu.CompilerParams(dimension_semantics=("arbitrary",)),
    )(proj, conv_w8)


def _conv_bwd(proj, conv_w8, dact, tm):
    t = proj.shape[0]
    ch = DN_CONV_CH
    nt = t // tm
    hb = tm // CONV_HALO

    def body(x_ref, xp_ref, w_ref, dy_ref, dx_ref, dw_ref, xbuf, dbuf):
        step = pl.program_id(0)

        @pl.when(step == 0)
        def _():
            dbuf[pl.ds(tm, CONV_HALO), :] = jnp.zeros((CONV_HALO, ch), F32)
            dw_ref[...] = jnp.zeros_like(dw_ref)

        first = step == nt - 1
        xbuf[pl.ds(0, CONV_HALO), :] = jnp.where(first, 0.0, xp_ref[...])
        xbuf[pl.ds(CONV_HALO, tm), :] = x_ref[...]
        pre = jnp.zeros((tm, ch), F32)
        for j in range(CONV_K):
            pre = pre + xbuf[pl.ds(CONV_HALO - (CONV_K - 1) + j, tm), :] * w_ref[pl.ds(j, 1), :]
        sg = _sigmoid(pre)
        dpre = dy_ref[...] * (sg * (1.0 + pre * (1.0 - sg)))
        dbuf[pl.ds(0, tm), :] = dpre
        dx = jnp.zeros((tm, ch), F32)
        for j in range(CONV_K):
            dx = dx + dbuf[pl.ds(CONV_K - 1 - j, tm), :] * w_ref[pl.ds(j, 1), :]
            dw_ref[pl.ds(j, 1), :] += jnp.sum(dpre * xbuf[pl.ds(CONV_HALO - (CONV_K - 1) + j, tm), :], axis=0, keepdims=True)
        dx_ref[...] = dx.astype(dx_ref.dtype)
        dbuf[pl.ds(tm, CONV_HALO), :] = dbuf[pl.ds(0, CONV_HALO), :]

    rev = lambda i: (nt - 1 - i, 0)
    prev = lambda i: (jnp.maximum((nt - 1 - i) * hb - 1, 0), 0)
    return pl.pallas_call(
        body, grid=(nt,),
        in_specs=[pl.BlockSpec((tm, ch), rev), pl.BlockSpec((CONV_HALO, ch), prev), _full(conv_w8.shape),
                  pl.BlockSpec((tm, ch), rev)],
        out_specs=[pl.BlockSpec((tm, ch), rev), _full(conv_w8.shape)],
        out_shape=[jax.ShapeDtypeStruct((t, ch), BF), jax.ShapeDtypeStruct(conv_w8.shape, F32)],
        scratch_shapes=[pltpu.VMEM((tm + CONV_HALO, ch), F32), pltpu.VMEM((tm + CONV_HALO, ch), F32)], name="conv_bwd",
        compiler_params=pltpu.CompilerParams(dimension_semantics=("arbitrary",)),
    )(proj, proj, conv_w8, dact)


BNN = (((2,), (1,)), ((0,), (0,)))
BNT = (((2,), (2,)), ((0,), (0,)))
BTN = (((1,), (1,)), ((0,), (0,)))


def _bdot(a, b, dims, precision=None):
    return lax.dot_general(a, b, dims, precision=precision, preferred_element_type=F32)


@jax.custom_vjp
def _bmmb_nt(a, b):
    return _bdot(a.astype(BF), b.astype(BF), BNT)


def _bmmb_nt_fwd(a, b):
    return _bmmb_nt(a, b), (a, b)


def _bmmb_nt_bwd(res, g):
    a, b = res
    gb = g.astype(BF)
    return _bdot(gb, b.astype(BF), BNN), _bdot(gb, a.astype(BF), BTN)


_bmmb_nt.defvjp(_bmmb_nt_fwd, _bmmb_nt_bwd)


@jax.custom_vjp
def _unit_lower_solve(a, r):
    return _unit_lower_solve_fwd(a, r)[0]


def _unit_lower_solve_fwd(a, r):
    c = a.shape[-1]
    ri = lax.broadcasted_iota(jnp.int32, a.shape, 1)
    ci = lax.broadcasted_iota(jnp.int32, a.shape, 2)
    xm = -a
    inv = (ri == ci).astype(F32) + xm
    for _ in range(int(math.log2(c)) - 1):
        xm = _bdot(xm, xm, BNN, HI)
        inv = inv + _bdot(inv, xm, BNN, HI)
    x = _bdot(inv, r, BNN, HI)
    return x, (inv, x)


def _unit_lower_solve_bwd(res, g):
    inv, x = res
    dr = _bdot(inv, g, BTN, HI)
    return -_bdot(dr, x, BNT, HI), dr


_unit_lower_solve.defvjp(_unit_lower_solve_fwd, _unit_lower_solve_bwd)


def _gdn_intra(qkv, ba, al8, dt8):
    tm = qkv.shape[0]
    nb = tm // CHUNK
    bsz = DN_HEADS * nb

    def heads(x0):
        return jnp.concatenate([qkv[:, x0 + h * LANE:x0 + (h + 1) * LANE].reshape(nb, CHUNK, LANE) for h in range(DN_HEADS)], axis=0)

    def spread(c0):
        return jnp.concatenate([jnp.broadcast_to(ba[:, c0 + h:c0 + h + 1], (tm, LANE)).reshape(nb, CHUNK, LANE)
                                for h in range(DN_HEADS)], axis=0)

    def per_head(v8):
        return jnp.concatenate([jnp.broadcast_to(v8[0:1, h:h + 1].reshape(1, 1, 1), (nb, 1, LANE)) for h in range(DN_HEADS)], axis=0)

    ri = lax.broadcasted_iota(jnp.int32, (bsz, CHUNK, CHUNK), 1)
    ci = lax.broadcasted_iota(jnp.int32, (bsz, CHUNK, CHUNK), 2)
    incl = ri >= ci
    strict = ri > ci

    q = _l2norm(heads(0)) * (DN_DK ** -0.5)
    k = _l2norm(heads(DN_QK))
    va = heads(2 * DN_QK)
    beta = _sigmoid(spread(0))
    g = -jnp.exp(per_head(al8)) * _softplus(spread(DN_HEADS) + per_head(dt8))
    gc = _bdot(incl.astype(F32), g, BNN, HI)
    g_last = jnp.sum(g, axis=1, keepdims=True)
    gcol = gc[:, :, :CHUNK]
    diff = gcol - jnp.swapaxes(gcol, 1, 2)
    decay = jnp.where(incl, jnp.exp(jnp.where(incl, diff, 0.0)), 0.0)
    kb = k * beta
    a_mat = jnp.where(strict, _bmmb_nt(kb, k) * decay, 0.0)
    egc = jnp.exp(gc)
    wu = _unit_lower_solve(a_mat, jnp.concatenate([kb * egc, va * beta], axis=2))
    attn = jnp.where(incl, _bmmb_nt(q, k) * decay, 0.0)

    def unheads(x):
        return jnp.concatenate([x[h * nb:(h + 1) * nb].reshape(tm, LANE) for h in range(DN_HEADS)], axis=1)

    return (unheads(wu[:, :, :DN_DK]), unheads(wu[:, :, DN_DK:]), unheads(q * egc), unheads(k * jnp.exp(g_last - gc)),
            attn.reshape(DN_HEADS, tm, CHUNK), unheads(jnp.broadcast_to(g_last, (bsz, CHUNK, LANE))))


def _gdn_scan_step(w, u, qg, kd, att, gl, s):
    v_new = u - _mmb(w, s)
    o = _mmb(qg, s) + _mmb(att, v_new)
    return o, s * jnp.exp(gl) + _mmb_tn(kd, v_new)


def _gdn_intra_specs(t, tm, dts):
    nb = tm // CHUNK
    specs = [pl.BlockSpec((tm, DN_VW), lambda i: (i, 0))] * 4
    specs += [pl.BlockSpec((DN_HEADS, tm, CHUNK), lambda i: (0, i, 0)), pl.BlockSpec((tm, DN_VW), lambda i: (i, 0))]
    shapes = [jax.ShapeDtypeStruct((t, DN_VW), dts[i]) for i in range(4)]
    shapes += [jax.ShapeDtypeStruct((DN_HEADS, t, CHUNK), dts[4]), jax.ShapeDtypeStruct((t, DN_VW), dts[5])]
    return specs, shapes


def _gdn_intra_fwd(qkv, proj, al8, dt8, tm):
    t = qkv.shape[0]

    def body(qkv_ref, ba_ref, al_ref, dt_ref, *outs):
        for o, val in zip(outs, _gdn_intra(qkv_ref[...], ba_ref[...], al_ref[...], dt_ref[...])):
            o[...] = val.astype(o.dtype)

    specs, shapes = _gdn_intra_specs(t, tm, (BF, F32, BF, BF, BF, F32))
    return pl.pallas_call(
        body, grid=(t // tm,),
        in_specs=[pl.BlockSpec((tm, DN_CONV_CH), lambda i: (i, 0)), pl.BlockSpec((tm, LANE), lambda i: (i, P_BA // LANE)),
                  _full(al8.shape), _full(dt8.shape)],
        out_specs=specs, out_shape=shapes, name="gdn_intra_fwd",
        compiler_params=pltpu.CompilerParams(dimension_semantics=("parallel",)),
    )(qkv, proj, al8, dt8)


def _gdn_intra_bwd(qkv, proj, al8, dt8, cts, tm):
    t = qkv.shape[0]

    def body(qkv_ref, ba_ref, al_ref, dt_ref, *refs):
        ct_refs, (dqkv_ref, dba_ref, dal_ref, ddt_ref) = refs[:6], refs[6:]

        @pl.when(pl.program_id(0) == 0)
        def _():
            dal_ref[...] = jnp.zeros_like(dal_ref)
            ddt_ref[...] = jnp.zeros_like(ddt_ref)

        _, vjp = jax.vjp(_gdn_intra, qkv_ref[...], ba_ref[...], al_ref[...], dt_ref[...])
        dqkv, dba, dal, ddt = vjp(tuple(r[...] for r in ct_refs))
        dqkv_ref[...] = dqkv
        dba_ref[...] = dba.astype(dba_ref.dtype)
        dal_ref[...] += dal
        ddt_ref[...] += ddt

    specs, _ = _gdn_intra_specs(t, tm, (F32,) * 6)
    return pl.pallas_call(
        body, grid=(t // tm,),
        in_specs=[pl.BlockSpec((tm, DN_CONV_CH), lambda i: (i, 0)), pl.BlockSpec((tm, LANE), lambda i: (i, P_BA // LANE)),
                  _full(al8.shape), _full(dt8.shape)] + specs,
        out_specs=[pl.BlockSpec((tm, DN_CONV_CH), lambda i: (i, 0)), pl.BlockSpec((tm, LANE), lambda i: (i, 0)),
                   _full(al8.shape), _full(dt8.shape)],
        out_shape=[jax.ShapeDtypeStruct((t, DN_CONV_CH), F32), jax.ShapeDtypeStruct((t, LANE), BF),
                   jax.ShapeDtypeStruct(al8.shape, F32), jax.ShapeDtypeStruct(dt8.shape, F32)],
        name="gdn_intra_bwd", compiler_params=pltpu.CompilerParams(dimension_semantics=("arbitrary",)),
    )(qkv, proj, al8, dt8, *cts)


def _gdn_scan_fwd(intra, tm):
    t = intra[0].shape[0]
    nb = tm // CHUNK
    nc = t // CHUNK

    def body(w_ref, u_ref, qg_ref, kd_ref, att_ref, gl_ref, o_ref, ss_ref, s_scr):
        @pl.when(pl.program_id(0) == 0)
        def _():
            s_scr[...] = jnp.zeros_like(s_scr)

        for cc in range(nb):
            rows = pl.ds(cc * CHUNK, CHUNK)
            for h in range(DN_HEADS):
                cols = pl.ds(h * DN_DV, DN_DV)
                s_prev = s_scr[h]
                ss_ref[cc, h] = s_prev
                o, s_new = _gdn_scan_step(w_ref[rows, cols], u_ref[rows, cols], qg_ref[rows, cols], kd_ref[rows, cols],
                                          att_ref[h, rows, :], gl_ref[pl.ds(cc * CHUNK, 1), cols], s_prev)
                o_ref[rows, cols] = o
                s_scr[h] = s_new

    specs, _ = _gdn_intra_specs(t, tm, (F32,) * 6)
    return pl.pallas_call(
        body, grid=(t // tm,), in_specs=specs,
        out_specs=[pl.BlockSpec((tm, DN_VW), lambda i: (i, 0)),
                   pl.BlockSpec((nb, DN_HEADS, DN_DK, DN_DV), lambda i: (i, 0, 0, 0))],
        out_shape=[jax.ShapeDtypeStruct((t, DN_VW), F32), jax.ShapeDtypeStruct((nc, DN_HEADS, DN_DK, DN_DV), F32)],
        scratch_shapes=[pltpu.VMEM((DN_HEADS, DN_DK, DN_DV), F32)], name="gdn_scan_fwd",
        compiler_params=pltpu.CompilerParams(dimension_semantics=("arbitrary",)),
    )(*intra)


def _gdn_scan_bwd(intra, states, do, tm):
    t = intra[0].shape[0]
    nb = tm // CHUNK
    ng = t // tm

    def body(w_ref, u_ref, qg_ref, kd_ref, att_ref, gl_ref, ss_ref, do_ref,
             dw_ref, du_ref, dqg_ref, dkd_ref, datt_ref, dgl_ref, ds_scr):
        @pl.when(pl.program_id(0) == 0)
        def _():
            ds_scr[...] = jnp.zeros_like(ds_scr)

        for cc in reversed(range(nb)):
            rows = pl.ds(cc * CHUNK, CHUNK)
            for h in range(DN_HEADS):
                cols = pl.ds(h * DN_DV, DN_DV)
                f32 = lambda r: r[rows, cols].astype(F32)
                _, vjp = jax.vjp(_gdn_scan_step, f32(w_ref), u_ref[rows, cols], f32(qg_ref), f32(kd_ref),
                                 att_ref[h, rows, :].astype(F32), gl_ref[pl.ds(cc * CHUNK, 1), cols], ss_ref[cc, h])
                dw, du, dqg, dkd, datt, dgl, ds_prev = vjp((do_ref[rows, cols], ds_scr[h]))
                dw_ref[rows, cols] = dw
                du_ref[rows, cols] = du
                dqg_ref[rows, cols] = dqg
                dkd_ref[rows, cols] = dkd
                datt_ref[h, rows, :] = datt
                first_row = lax.broadcasted_iota(jnp.int32, (CHUNK, DN_DV), 0) == 0
                dgl_ref[rows, cols] = jnp.where(first_row, dgl, 0.0)
                ds_scr[h] = ds_prev

    rev = lambda i: (ng - 1 - i, 0)
    rev3 = lambda i: (0, ng - 1 - i, 0)
    row = pl.BlockSpec((tm, DN_VW), rev)
    six = [row] * 4 + [pl.BlockSpec((DN_HEADS, tm, CHUNK), rev3), row]
    _, shapes = _gdn_intra_specs(t, tm, (F32,) * 6)
    return pl.pallas_call(
        body, grid=(ng,),
        in_specs=six + [pl.BlockSpec((nb, DN_HEADS, DN_DK, DN_DV), lambda i: (ng - 1 - i, 0, 0, 0)), row],
        out_specs=six, out_shape=shapes,
        scratch_shapes=[pltpu.VMEM((DN_HEADS, DN_DK, DN_DV), F32)], name="gdn_scan_bwd",
        compiler_params=pltpu.CompilerParams(dimension_semantics=("arbitrary",)),
    )(*intra, states, do)


def _gdn_out(o, z, g):
    parts = []
    for h in range(DN_HEADS):
        sl = slice(h * DN_DV, (h + 1) * DN_DV)
        parts.append(_rmsnorm(o[:, sl], g) * _silu(z[:, sl]))
    return parts


_Q_SCALE = math.log2(math.e) / math.sqrt(QK_NOPE + QK_ROPE)


def _rope_tables(pos, inv_freq2):
    lane = lax.broadcasted_iota(jnp.int32, (1, LANE), 1)
    ang = pos * inv_freq2
    cos = jnp.where(lane < QK_ROPE, jnp.cos(ang), 0.0)
    sin = jnp.where(lane < QK_ROPE // 2, -jnp.sin(ang), jnp.where(lane < QK_ROPE, jnp.sin(ang), 0.0))
    return cos, sin


def _rope_swap():
    ri = lax.broadcasted_iota(jnp.int32, (LANE, LANE), 0)
    ci = lax.broadcasted_iota(jnp.int32, (LANE, LANE), 1)
    half = QK_ROPE // 2
    return (((ci < half) & (ri == ci + half)) | ((ci >= half) & (ci < QK_ROPE) & (ri == ci - half))).astype(F32)


def _mla_prep(cq, ckv, kr, gq, gkv, w_uq, w_ukv, cos, sin, swap):
    rope = lambda u: u * cos + _doth(u, swap) * sin
    q_lin = _mmb_nt(_rmsnorm(cq, gq), w_uq) * _Q_SCALE
    kv_lin = _mmb_nt(_rmsnorm(ckv, gkv), w_ukv)
    k_rope = rope(kr)
    qs, ks, vs = [], [], []
    for h in range(MLA_HEADS):
        qs += [q_lin[:, h * LANE:(h + 1) * LANE], rope(q_lin[:, (MLA_HEADS + h) * LANE:(MLA_HEADS + h + 1) * LANE])]
        ks += [kv_lin[:, 2 * h * LANE:(2 * h + 1) * LANE], k_rope]
        vs += [kv_lin[:, (2 * h + 1) * LANE:(2 * h + 2) * LANE]]
    return qs + ks + vs


def _mla_prep_fwd(proj, pos_col, inv_freq2, gq, gkv, w_uq, w_ukv, tm):
    t = proj.shape[0]
    nq = 2 * MLA_HEADS

    def body(cq_ref, ckv_ref, kr_ref, pos_ref, f_ref, gq_ref, gkv_ref, wq_ref, wkv_ref, q_ref, k_ref, v_ref):
        cos, sin = _rope_tables(pos_ref[...], f_ref[...])
        outs = _mla_prep(cq_ref[...], ckv_ref[...], kr_ref[...], gq_ref[...], gkv_ref[...], wq_ref[...], wkv_ref[...],
                         cos, sin, _rope_swap())
        for i in range(nq):
            q_ref[:, pl.ds(i * LANE, LANE)] = outs[i].astype(q_ref.dtype)
            k_ref[:, pl.ds(i * LANE, LANE)] = outs[nq + i].astype(k_ref.dtype)
        for h in range(MLA_HEADS):
            v_ref[:, pl.ds(h * LANE, LANE)] = outs[2 * nq + h].astype(v_ref.dtype)

    row = lambda w, j: pl.BlockSpec((tm, w), functools.partial(lambda i, j: (i, j), j=j))
    return pl.pallas_call(
        body, grid=(t // tm,),
        in_specs=[row(Q_LORA, P_CQ // Q_LORA), row(KV_LORA, P_CKV // KV_LORA), row(LANE, P_KR // LANE),
                  pl.BlockSpec((tm, 1), lambda i: (i, 0)), _full(inv_freq2.shape), _full(gq.shape), _full(gkv.shape),
                  _full(w_uq.shape), _full(w_ukv.shape)],
        out_specs=[row(nq * LANE, 0), row(nq * LANE, 0), row(MLA_VW, 0)],
        out_shape=[jax.ShapeDtypeStruct((t, nq * LANE), BF), jax.ShapeDtypeStruct((t, nq * LANE), BF),
                   jax.ShapeDtypeStruct((t, MLA_VW), BF)],
        name="mla_prep_fwd", compiler_params=pltpu.CompilerParams(dimension_semantics=("arbitrary",)),
    )(proj, proj, proj, pos_col, inv_freq2, gq, gkv, w_uq, w_ukv)


def _mla_prep_bwd(proj, pos_col, inv_freq2, gq, gkv, w_uq, w_ukv, dq, dk, dv, tm):
    t = proj.shape[0]
    nq = 2 * MLA_HEADS

    def body(cq_ref, ckv_ref, kr_ref, pos_ref, f_ref, gq_ref, gkv_ref, wq_ref, wkv_ref, dq_ref, dk_ref, dv_ref,
             dcq_ref, dckv_ref, dkr_ref, dgq_ref, dgkv_ref, dwq_ref, dwkv_ref):
        @pl.when(pl.program_id(0) == 0)
        def _():
            for o in (dgq_ref, dgkv_ref, dwq_ref, dwkv_ref):
                o[...] = jnp.zeros_like(o)

        cos, sin = _rope_tables(pos_ref[...], f_ref[...])
        f = functools.partial(_mla_prep, cos=cos, sin=sin, swap=_rope_swap())
        _, vjp = jax.vjp(f, cq_ref[...], ckv_ref[...], kr_ref[...], gq_ref[...], gkv_ref[...], wq_ref[...], wkv_ref[...])
        cts = [dq_ref[:, pl.ds(i * LANE, LANE)] for i in range(nq)]
        cts += [dk_ref[:, pl.ds(i * LANE, LANE)] for i in range(nq)]
        cts += [dv_ref[:, pl.ds(h * LANE, LANE)] for h in range(MLA_HEADS)]
        dcq, dckv, dkr, dgq, dgkv, dwq, dwkv = vjp(cts)
        dcq_ref[...] = dcq.astype(dcq_ref.dtype)
        dckv_ref[...] = dckv.astype(dckv_ref.dtype)
        dkr_ref[...] = dkr.astype(dkr_ref.dtype)
        dgq_ref[...] += dgq
        dgkv_ref[...] += dgkv
        dwq_ref[...] += dwq
        dwkv_ref[...] += dwkv

    row = lambda w, j: pl.BlockSpec((tm, w), functools.partial(lambda i, j: (i, j), j=j))
    return pl.pallas_call(
        body, grid=(t // tm,),
        in_specs=[row(Q_LORA, P_CQ // Q_LORA), row(KV_LORA, P_CKV // KV_LORA), row(LANE, P_KR // LANE),
                  pl.BlockSpec((tm, 1), lambda i: (i, 0)), _full(inv_freq2.shape), _full(gq.shape), _full(gkv.shape),
                  _full(w_uq.shape), _full(w_ukv.shape), row(nq * LANE, 0), row(nq * LANE, 0), row(MLA_VW, 0)],
        out_specs=[row(Q_LORA, 0), row(KV_LORA, 0), row(LANE, 0), _full(gq.shape), _full(gkv.shape),
                   _full(w_uq.shape), _full(w_ukv.shape)],
        out_shape=[jax.ShapeDtypeStruct((t, Q_LORA), BF), jax.ShapeDtypeStruct((t, KV_LORA), BF),
                   jax.ShapeDtypeStruct((t, LANE), BF), jax.ShapeDtypeStruct(gq.shape, F32),
                   jax.ShapeDtypeStruct(gkv.shape, F32), jax.ShapeDtypeStruct(w_uq.shape, F32),
                   jax.ShapeDtypeStruct(w_ukv.shape, F32)],
        name="mla_prep_bwd", compiler_params=pltpu.CompilerParams(dimension_semantics=("arbitrary",)),
    )(proj, proj, proj, pos_col, inv_freq2, gq, gkv, w_uq, w_ukv, dq, dk, dv)


_NEG = -1e30
_LN2 = math.log(2.0)
ATT_CHAINS = 2


def _causal(tq, tk, q0, k0):
    row = q0 + lax.broadcasted_iota(jnp.int32, (tq, tk), 0)
    col = k0 + lax.broadcasted_iota(jnp.int32, (tq, tk), 1)
    return col <= row


def _attn_fwd(q, k, v, tq, tk):
    t = q.shape[0]

    assert tk % tq == 0

    th = tq // ATT_CHAINS

    def body(q_ref, k_ref, v_ref, o_ref, lse_ref):
        i = pl.program_id(1)
        n_full = (i * tq) // tk

        def step(k0, carry, masked):
            kt = k_ref[pl.ds(k0, tk), :]
            vt = v_ref[pl.ds(k0, tk), :]
            out = []
            for c, (m, l, acc) in enumerate(carry):
                s = _dot(q_ref[pl.ds(c * th, th), :], kt, NT)
                if masked:
                    s = jnp.where(_causal(th, tk, i * tq + c * th, k0), s, _NEG)
                m_new = jnp.maximum(m, jnp.max(s, axis=-1, keepdims=True))
                p = jnp.exp2(s - m_new)
                alpha = jnp.exp2(m - m_new)
                out.append((m_new, alpha * l + jnp.sum(p, axis=-1, keepdims=True), alpha * acc + _dot(p.astype(BF), vt)))
            return tuple(out)

        init = tuple((jnp.full((th, 1), _NEG, F32), jnp.zeros((th, 1), F32), jnp.zeros((th, V_HEAD), F32)) for _ in range(ATT_CHAINS))
        carry = lax.fori_loop(0, n_full, lambda j, c: step(pl.multiple_of(j * tk, tk), c, False), init)
        for c, (m, l, acc) in enumerate(step(pl.multiple_of(n_full * tk, tk), carry, True)):
            o_ref[pl.ds(c * th, th), :] = acc / l
            lse_ref[pl.ds(c * th, th), :] = jnp.broadcast_to(m + jnp.log2(l), (th, LANE))

    return pl.pallas_call(
        body, grid=(MLA_HEADS, t // tq),
        in_specs=[pl.BlockSpec((tq, 2 * LANE), lambda h, i: (i, h)), pl.BlockSpec((t, 2 * LANE), lambda h, i: (0, h)),
                  pl.BlockSpec((t, V_HEAD), lambda h, i: (0, h))],
        out_specs=[pl.BlockSpec((tq, V_HEAD), lambda h, i: (i, h)), pl.BlockSpec((tq, LANE), lambda h, i: (i, h))],
        out_shape=[jax.ShapeDtypeStruct((t, MLA_VW), F32), jax.ShapeDtypeStruct((t, MLA_HEADS * LANE), F32)],
        name="attn_fwd", compiler_params=pltpu.CompilerParams(dimension_semantics=("parallel", "arbitrary")),
    )(q, k, v)


def _attn_bwd(q, k, v, do, lse, delta, tq, tk):
    t = q.shape[0]
    nq = t // tq
    nkt = t // tk
    assert tk % tq == 0

    def body(q_ref, k_ref, v_ref, do_ref, lse_ref, dl_ref, dq_ref, dk_ref, dv_ref):
        j = pl.program_id(1)

        @pl.when(j == 0)
        def _():
            dq_ref[...] = jnp.zeros_like(dq_ref)

        kt = k_ref[...]
        vt = v_ref[...]

        def step(q0, carry, masked):
            dk, dv = carry
            rows = pl.ds(q0, tq)
            qt = q_ref[rows, :]
            dot_ = do_ref[rows, :]
            p = jnp.exp2(_dot(qt, kt, NT) - lse_ref[rows, pl.ds(0, 1)])
            if masked:
                p = jnp.where(_causal(tq, tk, q0, j * tk), p, 0.0)
            dv = dv + _dot(p.astype(BF), dot_, TN)
            ds = (p * (_dot(dot_, vt, NT) - dl_ref[rows, pl.ds(0, 1)])).astype(BF)
            dk = dk + _dot(ds, qt, TN)
            dq_ref[rows, :] += _dot(ds, kt)
            return dk, dv

        per = tk // tq
        carry = (jnp.zeros((tk, 2 * LANE), F32), jnp.zeros((tk, V_HEAD), F32))
        for dd in range(per):
            carry = step(pl.multiple_of(j * tk + dd * tq, tq), carry, True)

        def group(g, c):
            for dd in range(per):
                c = step(pl.multiple_of(g * tk + dd * tq, tq), c, False)
            return c

        dk, dv = lax.fori_loop(j + 1, nkt, group, carry)
        dk_ref[...] = dk * _LN2
        dv_ref[...] = dv

        @pl.when(j == nkt - 1)
        def _():
            dq_ref[...] = dq_ref[...] * _LN2

    return pl.pallas_call(
        body, grid=(MLA_HEADS, nkt),
        in_specs=[pl.BlockSpec((t, 2 * LANE), lambda h, j: (0, h)), pl.BlockSpec((tk, 2 * LANE), lambda h, j: (j, h)),
                  pl.BlockSpec((tk, V_HEAD), lambda h, j: (j, h)), pl.BlockSpec((t, V_HEAD), lambda h, j: (0, h)),
                  pl.BlockSpec((t, LANE), lambda h, j: (0, h)), pl.BlockSpec((t, LANE), lambda h, j: (0, h))],
        out_specs=[pl.BlockSpec((t, 2 * LANE), lambda h, j: (0, h)), pl.BlockSpec((tk, 2 * LANE), lambda h, j: (j, h)),
                   pl.BlockSpec((tk, V_HEAD), lambda h, j: (j, h))],
        out_shape=[jax.ShapeDtypeStruct((t, MLA_HEADS * 2 * LANE), F32), jax.ShapeDtypeStruct((t, MLA_HEADS * 2 * LANE), F32),
                   jax.ShapeDtypeStruct((t, MLA_VW), F32)],
        name="attn_bwd", compiler_params=pltpu.CompilerParams(dimension_semantics=("parallel", "arbitrary")),
    )(q, k, v, do, lse, delta)


def _adam_update(w, g, m, v):
    mm = ADAM_B1 * m + (1.0 - ADAM_B1) * g
    vv = ADAM_B2 * v + (1.0 - ADAM_B2) * jnp.square(g)
    m_hat = mm / (1.0 - ADAM_B1 ** ADAM_STEP)
    v_hat = vv / (1.0 - ADAM_B2 ** ADAM_STEP)
    return -ADAM_LR * (m_hat / (jnp.sqrt(v_hat) + ADAM_EPS) + ADAM_WD * w), mm, vv


def _adamw(w, g, m, v, name):
    r, c = w.shape
    tr = _pick(r, (256, 128, 64, 32, 16, 8))
    slots = g.ndim == 3

    def body(w_ref, g_ref, m_ref, v_ref, g_out, d_ref, nm_ref, nv_ref):
        if slots:
            gg = g_ref[0].astype(F32)
            for s in range(1, N_DEV):
                gg = gg + g_ref[s].astype(F32)
        else:
            gg = g_ref[...]
        g_out[...] = gg
        d_ref[...], nm_ref[...], nv_ref[...] = _adam_update(w_ref[...], gg, m_ref[...], v_ref[...])

    spec = pl.BlockSpec((tr, c), lambda i: (i, 0))
    g_spec = pl.BlockSpec((N_DEV, tr, c), lambda i: (0, i, 0)) if slots else spec
    return pl.pallas_call(
        body, grid=(r // tr,), in_specs=[spec, g_spec, spec, spec], out_specs=[spec] * 4,
        out_shape=[jax.ShapeDtypeStruct((r, c), F32)] * 4, name=name,
        compiler_params=pltpu.CompilerParams(dimension_semantics=("arbitrary",)),
    )(w, g, m, v)


def _adamw_many(ws, gs, ms, vs, name):
    n = len(ws)

    def body(*refs):
        for i in range(n):
            w_ref, g_ref, m_ref, v_ref = (refs[j * n + i] for j in range(4))
            d_ref, nm_ref, nv_ref = (refs[(4 + j) * n + i] for j in range(3))
            d_ref[...], nm_ref[...], nv_ref[...] = _adam_update(w_ref[...], g_ref[...], m_ref[...], v_ref[...])

    shapes = [jax.ShapeDtypeStruct(w.shape, F32) for w in ws]
    outs = pl.pallas_call(body, out_shape=shapes * 3, name=name)(*ws, *gs, *ms, *vs)
    return outs[:n], outs[n:2 * n], outs[2 * n:]


def _cast_bf16(xs, name, after=None):
    n = len(xs)
    extra = [] if after is None else [after]

    def body(*refs):
        outs = refs[n + len(extra):]
        for i in range(n):
            outs[i][...] = refs[i][...].astype(BF)

    vmem = pl.BlockSpec(memory_space=pltpu.VMEM)
    return pl.pallas_call(
        body, out_shape=[jax.ShapeDtypeStruct(x.shape, BF) for x in xs], name=name,
        in_specs=[vmem] * n + [pl.BlockSpec(memory_space=pl.ANY)] * len(extra), out_specs=[vmem] * n)(*xs, *extra)


def _pad_rows(a, n):
    return jnp.pad(a, ((0, n - a.shape[0]), (0, 0)))


def _w_in_to_padded(wt):
    s_ba = P_CQ
    s_cq = s_ba + 2 * DN_HEADS
    s_kr = s_cq + Q_LORA + KV_LORA
    return jnp.concatenate([wt[:s_ba], wt[s_cq:s_kr], _pad_rows(wt[s_ba:s_cq], LANE), _pad_rows(wt[s_kr:], LANE)], axis=0)


def _w_in_from_padded(wt):
    return jnp.concatenate([wt[:P_CQ], wt[P_BA:P_BA + 2 * DN_HEADS], wt[P_CQ:P_BA], wt[P_KR:P_KR + QK_ROPE]], axis=0)


def _w_uq_to_padded(wt):
    w3 = wt.reshape(MLA_HEADS, QK_NOPE + QK_ROPE, Q_LORA)
    nope = w3[:, :QK_NOPE].reshape(MLA_HEADS * QK_NOPE, Q_LORA)
    rope = jnp.pad(w3[:, QK_NOPE:], ((0, 0), (0, LANE - QK_ROPE), (0, 0))).reshape(MLA_HEADS * LANE, Q_LORA)
    return jnp.concatenate([nope, rope], axis=0)


def _w_uq_from_padded(wt):
    nope = wt[:MLA_HEADS * QK_NOPE].reshape(MLA_HEADS, QK_NOPE, Q_LORA)
    rope = wt[MLA_HEADS * QK_NOPE:].reshape(MLA_HEADS, LANE, Q_LORA)[:, :QK_ROPE]
    return jnp.concatenate([nope, rope], axis=1).reshape(MLA_HEADS * (QK_NOPE + QK_ROPE), Q_LORA)


def _pack(pieces, width, row_mult):
    flat = jnp.concatenate([p.reshape(-1) for p in pieces])
    n = flat.shape[0]
    rows = -(-n // (width * row_mult)) * row_mult
    return jnp.pad(flat, (0, rows * width - n)).reshape(rows, width)


def _unpack(flat, shapes):
    out, o = [], 0
    for s in shapes:
        n = math.prod(s)
        out.append(flat[o:o + n].reshape(s))
        o += n
    return out


def kernel(x, c, positions, w_ada, b_ada, w_in, conv_w, a_log, dt_bias, dn_norm_g, q_norm_g, w_uq, kv_norm_g, w_ukv, w_o, ln1_g, ln1_b, w_gate, w_up, w_down, ln2_g, ln2_b, loss_target, m_w_ada, m_b_ada, m_w_in, m_conv_w, m_a_log, m_dt_bias, m_dn_norm_g, m_q_norm_g, m_w_uq, m_kv_norm_g, m_w_ukv, m_w_o, m_ln1_g, m_ln1_b, m_w_gate, m_w_up, m_w_down, m_ln2_g, m_ln2_b, v_w_ada, v_b_ada, v_w_in, v_conv_w, v_a_log, v_dt_bias, v_dn_norm_g, v_q_norm_g, v_w_uq, v_kv_norm_g, v_w_ukv, v_w_o, v_ln1_g, v_ln1_b, v_w_gate, v_w_up, v_w_down, v_ln2_g, v_ln2_b):
    me = 4 * lax.axis_index("x") + 2 * lax.axis_index("y") + lax.axis_index("c")
    t, d = x.shape[1], x.shape[2]
    ada_n = w_ada.shape[2]

    cw = conv_w.shape[3]
    c_all, conv_all = _exchange([c, conv_w[0, :, 0, :]], "gather_small", scatter=False)
    c_all = c_all.reshape(N_DEV, d)
    conv_full = conv_all.transpose(1, 0, 2).reshape(CONV_K, N_DEV * cw)
    conv_w8 = jnp.pad(conv_full, ((0, 8 - CONV_K), (0, 0)))

    b_ada_mine = lax.dynamic_slice(b_ada, (0, me * ada_n), (1, ada_n))
    mod_cols = _mod_fwd(c_all, w_ada[0], b_ada_mine)
    (mod_all,) = _exchange([mod_cols.reshape(N_DEV, 1, ada_n)], "scatter_mod", scatter=True)
    mod = mod_all.reshape(1, N_DEV * ada_n)

    tr = lambda w: w[0].T
    (in_shard,) = _cast_bf16([tr(w_in)], "cast_w_in")
    (a_in,) = _exchange([in_shard], "gather_w_in", scatter=False)
    later = _cast_bf16([tr(w_uq), tr(w_ukv), w_o[0], tr(w_gate), tr(w_up), w_down[0]], "cast_weights", after=a_in)
    mixer_gather, _ = _exchange_start(later[:3], "gather_mixer_weights_start", scatter=False)
    ffn_gather, _ = _exchange_start(later[3:], "gather_ffn_weights_start", scatter=False)
    rows = lambda a: a.reshape(-1, a.shape[2])
    w_in_t = _w_in_to_padded(rows(a_in))

    def mixer_weights(after):
        a_uq, a_ukv, a_o = _exchange_wait(mixer_gather, after, "gather_mixer_weights_wait", scatter=False)
        return _w_uq_to_padded(rows(a_uq)), rows(a_ukv), rows(a_o)

    def ffn_weights(after):
        a_gate, a_up, a_down = _exchange_wait(ffn_gather, after, "gather_ffn_weights_wait", scatter=False)
        return rows(a_gate), rows(a_up), rows(a_down)

    def by_dest(g):
        return g.reshape(N_DEV, -1, g.shape[1])

    scatters = {}

    def grads_ready(tag, *g):
        if tag == "ffn":
            pieces = [by_dest(a) for a in g]
        elif tag == "mixer":
            g_w_o, g_w_uq_t, g_w_ukv_t = g
            pieces = [by_dest(g_w_o), by_dest(_w_uq_from_padded(g_w_uq_t).astype(BF)), by_dest(g_w_ukv_t.astype(BF))]
        else:
            pieces = [by_dest(_w_in_from_padded(g[0]))]
        scatters[tag], token = _exchange_start(pieces, "scatter_%s_grads_start" % tag, scatter=True)
        return token

    loc = _local_step(x[0], loss_target[0], positions[0], mod, w_in_t, mixer_weights, ffn_weights, grads_ready,
                      conv_w8, a_log, dt_bias, dn_norm_g, q_norm_g, kv_norm_g, ln1_g, ln1_b, ln2_g, ln2_b)
    grad_x, loss_acc, dmod, d_conv8, d_al8, d_dt8, d_dn_g, d_q_g, d_kv_g, d_ln1_g, d_ln1_b, d_ln2_g, d_ln2_b = loc

    small_shapes = [(6 * d,), (CONV_K, N_DEV * cw), (DN_HEADS,), (DN_HEADS,), (DN_DV,), (Q_LORA,), (KV_LORA,), (d,), (d,), (d,), (d,), (1,)]
    gsmall = _pack([dmod, d_conv8[:CONV_K], d_al8[0, :DN_HEADS], d_dt8[0, :DN_HEADS], d_dn_g, d_q_g, d_kv_g,
                    d_ln1_g, d_ln1_b, d_ln2_g, d_ln2_b, loss_acc[0, :1]], LANE, 8)
    (gsmall_all,) = _exchange([gsmall], "gather_small_grads", scatter=False)
    dmod_all = gsmall_all.reshape(N_DEV, -1)[:, :6 * d]
    tot = _unpack(_sum_slots(gsmall_all, "sum_small_grads").reshape(-1), small_shapes)
    g_b_ada, g_conv_full, g_a_log, g_dt_bias, g_dn_g, g_q_g, g_kv_g, g_ln1_g, g_ln1_b, g_ln2_g, g_ln2_b, loss1 = tot
    loss = loss1.reshape(())
    g_conv_w = lax.dynamic_slice(g_conv_full, (0, me * cw), (CONV_K, cw))
    g_w_ada = _mod_bwd(c_all.T, lax.dynamic_slice(dmod_all, (0, me * ada_n), (N_DEV, ada_n)))

    grads = {"w_ada": g_w_ada[None], "b_ada": g_b_ada[None], "conv_w": g_conv_w[None, :, None, :],
             "a_log": g_a_log[None], "dt_bias": g_dt_bias[None], "dn_norm_g": g_dn_g[None], "q_norm_g": g_q_g[None],
             "kv_norm_g": g_kv_g[None], "ln1_g": g_ln1_g[None], "ln1_b": g_ln1_b[None], "ln2_g": g_ln2_g[None], "ln2_b": g_ln2_b[None]}
    weights = dict(w_ada=w_ada, b_ada=b_ada, w_in=w_in, conv_w=conv_w, a_log=a_log, dt_bias=dt_bias, dn_norm_g=dn_norm_g,
                   q_norm_g=q_norm_g, w_uq=w_uq, kv_norm_g=kv_norm_g, w_ukv=w_ukv, w_o=w_o, ln1_g=ln1_g, ln1_b=ln1_b,
                   w_gate=w_gate, w_up=w_up, w_down=w_down, ln2_g=ln2_g, ln2_b=ln2_b)
    ms = dict(w_ada=m_w_ada, b_ada=m_b_ada, w_in=m_w_in, conv_w=m_conv_w, a_log=m_a_log, dt_bias=m_dt_bias,
              dn_norm_g=m_dn_norm_g, q_norm_g=m_q_norm_g, w_uq=m_w_uq, kv_norm_g=m_kv_norm_g, w_ukv=m_w_ukv, w_o=m_w_o,
              ln1_g=m_ln1_g, ln1_b=m_ln1_b, w_gate=m_w_gate, w_up=m_w_up, w_down=m_w_down, ln2_g=m_ln2_g, ln2_b=m_ln2_b)
    vs = dict(w_ada=v_w_ada, b_ada=v_b_ada, w_in=v_w_in, conv_w=v_conv_w, a_log=v_a_log, dt_bias=v_dt_bias,
              dn_norm_g=v_dn_norm_g, q_norm_g=v_q_norm_g, w_uq=v_w_uq, kv_norm_g=v_kv_norm_g, w_ukv=v_w_ukv, w_o=v_w_o,
              ln1_g=v_ln1_g, ln1_b=v_ln1_b, w_gate=v_w_gate, w_up=v_w_up, w_down=v_w_down, ln2_g=v_ln2_g, ln2_b=v_ln2_b)
    names = list(weights)
    big = ("w_ada", "w_gate", "w_up", "w_down", "w_o", "w_uq", "w_ukv", "w_in")
    waits = {"w_gate": ("ffn", ("w_gate", "w_up", "w_down")), "w_o": ("mixer", ("w_o", "w_uq", "w_ukv")), "w_in": ("in", ("w_in",))}
    delta_w, new_m, new_v, slots = {}, {}, {}, {}
    last = g_w_ada
    for n in big:
        if n == "w_in":
            rest = [r for r in names if r not in big]
            flat2 = lambda a: a.reshape(-1, a.shape[-1])
            outs = _adamw_many(*[[flat2(src[r]) for r in rest] for src in (weights, grads, ms, vs)], "adamw_small")
            for dst, o in zip((delta_w, new_m, new_v), outs):
                for r, a in zip(rest, o):
                    dst[r] = a.reshape(weights[r].shape)
            last = outs[0][0]
        transposed = n in ("w_in", "w_uq", "w_ukv", "w_gate", "w_up")
        two = (lambda a: a[0].T) if transposed else (lambda a: a[0])
        back = (lambda a: a.T[None]) if transposed else (lambda a: a[None])
        if n in waits:
            tag, members = waits[n]
            slots.update(zip(members, _exchange_wait(scatters[tag], last, "scatter_%s_grads_wait" % tag, scatter=True)))
        g_in = slots[n] if n in slots else two(grads[n])
        gr, dlt, nm, nv = _adamw(two(weights[n]), g_in, two(ms[n]), two(vs[n]), "adamw_" + n)
        grads[n], delta_w[n], new_m[n], new_v[n] = back(gr), back(dlt), back(nm), back(nv)
        last = nv

    return (loss, grad_x[None], *[grads[n] for n in names], *[delta_w[n] for n in names],
            *[new_m[n] for n in names], *[new_v[n] for n in names])


def _local_step(xs, tgt, pos, mod, w_in_t, mixer_weights, ffn_weights, grads_ready, conv_w8,
                a_log, dt_bias, dn_norm_g, q_norm_g, kv_norm_g, ln1_g, ln1_b, ln2_g, ln2_b):
    t, d = xs.shape
    sh_m, sc_m, gt_m, sh_f, sc_f, gt_f = [mod[:, i * d:(i + 1) * d] for i in range(6)]
    pos_col = pos.astype(F32).reshape(t, 1)
    inv_freq = 1.0 / (ROPE_THETA ** (jnp.arange(0, QK_ROPE, 2, dtype=F32) / QK_ROPE))
    inv_freq2 = jnp.pad(jnp.concatenate([inv_freq, inv_freq]), (0, LANE - QK_ROPE)).reshape(1, LANE)
    al8 = jnp.pad(a_log, ((0, 7), (0, LANE - DN_HEADS)))
    dt8 = jnp.pad(dt_bias, ((0, 7), (0, LANE - DN_HEADS)))

    tm = min(512, t)
    tq = min(256, t)
    tk = min(512, t)

    (h1,) = _rowwise("modulate_in", lambda xx, sc, sh: xx * (1.0 + sc) + sh, [xs], [sc_m, sh_m], [(d, BF)], [], tm)
    proj = _matmul(h1, w_in_t, "nt", "in_proj")
    qkv = _conv_fwd(proj, conv_w8, min(256, t))
    gdn_tm = min(512, t)
    intra = _gdn_intra_fwd(qkv, proj, al8, dt8, gdn_tm)
    o_dn, states = _gdn_scan_fwd(intra, gdn_tm)
    w_uq_t, w_ukv_t, w_o_f = mixer_weights(states)
    qc, kc, vc = _mla_prep_fwd(proj, pos_col, inv_freq2, q_norm_g, kv_norm_g, w_uq_t, w_ukv_t, tm)
    o_mla, lse = _attn_fwd(qc, kc, vc, tk, tk)

    def mix_in(o, z, om, g):
        return jnp.concatenate(_gdn_out(o, z, g) + [om], axis=1)

    (mixin,) = _rowwise("mixer_out", mix_in, [o_dn, (proj, DN_VW, P_Z // DN_VW), o_mla], [dn_norm_g], [(2 * DN_VW, BF)], [], tm)
    mix = _matmul(mixin, w_o_f, "nn", "out_proj")

    def block1(xx, mx, gt, g1, b1, sc, sh):
        x1 = _layernorm(DEEPNORM_ALPHA * xx + gt * mx, g1, b1)
        return x1, x1 * (1.0 + sc) + sh

    x1, h2 = _rowwise("norm1_modulate", block1, [xs, mix], [gt_m, ln1_g, ln1_b, sc_f, sh_f], [(d, F32), (d, BF)], [], tm)
    w_gate_f, w_up_f, w_down_f = ffn_weights(h2)
    act, gate, up = _ffn_in(h2, w_gate_f, w_up_f)
    ff = _matmul(act, w_down_f, "nn", "ffn_out")

    def tail_loss(x1_, ff_, gt, g2, b2, tg):
        y = _layernorm(DEEPNORM_ALPHA * x1_ + gt * ff_, g2, b2)
        return 0.5 * jnp.sum(jnp.mean(jnp.square(y - tg), axis=-1))

    def tail(x1_, ff_, tg, gt, g2, b2):
        loss, (dx1, dff, dgt, dg2, db2) = jax.value_and_grad(tail_loss, argnums=(0, 1, 2, 3, 4))(x1_, ff_, gt, g2, b2, tg)
        return dx1, dff, jnp.full((1, LANE), loss, F32), dgt, dg2, db2

    dx1_a, dff, loss_acc, d_gt_f, d_ln2_g, d_ln2_b = _rowwise(
        "norm2_loss", tail, [x1, ff, tgt], [gt_f, ln2_g, ln2_b], [(d, F32), (d, BF)], [(1, LANE), (1, d), (1, d), (1, d)], tm)

    g_w_down = _matmul(act, dff, "tn", "d_w_down", BF)
    dgate, dup = _ffn_act_bwd(dff, w_down_f, gate, up)
    g_w_gate = _matmul(dgate, h2, "tn", "d_w_gate", BF)
    g_w_up = _matmul(dup, h2, "tn", "d_w_up", BF)
    token = grads_ready("ffn", g_w_gate, g_w_up, g_w_down)
    dh2 = _matmul2_nn(dgate, w_gate_f, dup, w_up_f, "d_ffn_in")

    def block1_bwd(xx, mx, dx1_, dh2_, gt, g1, b1, sc, sh):
        _, vjp = jax.vjp(block1, xx, mx, gt, g1, b1, sc, sh)
        dxx, dmx, dgt, dg1, db1, dsc, dsh = vjp((dx1_, dh2_))
        return dxx, dmx, dgt, dg1, db1, dsc, dsh

    dx_a, dmix, d_gt_m, d_ln1_g, d_ln1_b, d_sc_f, d_sh_f = _rowwise(
        "norm1_modulate_bwd", block1_bwd, [xs, mix, dx1_a, dh2], [gt_m + token, ln1_g, ln1_b, sc_f, sh_f],
        [(d, F32), (d, BF)], [(1, d)] * 5, min(256, t))

    dmixin = _matmul(dmix, w_o_f, "nt", "d_mixer_out")
    g_w_o = _matmul(mixin, dmix, "tn", "d_w_o", BF)

    def mixer_bwd(o, z, om, dmi, g):
        _, vjp = jax.vjp(lambda o_, z_, g_: jnp.concatenate(_gdn_out(o_, z_, g_), axis=1), o, z, g)
        do_, dz_, dg_ = vjp(dmi[:, :DN_VW])
        dom = dmi[:, DN_VW:]
        delta = [jnp.broadcast_to(jnp.sum(dom[:, h * V_HEAD:(h + 1) * V_HEAD] * om[:, h * V_HEAD:(h + 1) * V_HEAD], axis=-1, keepdims=True), (o.shape[0], LANE))
                 for h in range(MLA_HEADS)]
        return do_, dz_, dom, jnp.concatenate(delta, axis=1), dg_

    do_dn, dz, do_mla, delta, d_dn_g = _rowwise(
        "mixer_out_bwd", mixer_bwd, [o_dn, (proj, DN_VW, P_Z // DN_VW), o_mla, dmixin], [dn_norm_g],
        [(DN_VW, F32), (DN_VW, BF), (MLA_VW, BF), (MLA_HEADS * LANE, F32)], [(1, DN_DV)], tm)

    dqc, dkc, dvc = _attn_bwd(qc, kc, vc, do_mla, lse, delta, tq, tk)
    dcq, dckv, dkr, d_q_g, d_kv_g, g_w_uq_t, g_w_ukv_t = _mla_prep_bwd(
        proj, pos_col, inv_freq2, q_norm_g, kv_norm_g, w_uq_t, w_ukv_t, dqc, dkc, dvc, min(256, t))

    token = grads_ready("mixer", g_w_o, g_w_uq_t, g_w_ukv_t)

    d_intra = _gdn_scan_bwd(intra, states, do_dn, gdn_tm)
    dqkv_act, dba, d_al8, d_dt8 = _gdn_intra_bwd(qkv, proj, al8 + token, dt8, d_intra, min(256, t))
    dqkv_pre, d_conv8 = _conv_bwd(proj, conv_w8, dqkv_act, min(256, t))

    dproj = jnp.concatenate([dqkv_pre, dz, dcq, dckv, dba, dkr], axis=1)
    dh1 = _matmul(dproj, w_in_t, "nn", "d_in_proj")
    g_w_in_t = _matmul(dproj, h1, "tn", "d_w_in", BF)
    token = grads_ready("in", g_w_in_t)

    def modulate_bwd(xx, dh, dxa, sc):
        return dh * (1.0 + sc) + dxa, jnp.sum(dh * xx, axis=0, keepdims=True), jnp.sum(dh, axis=0, keepdims=True)

    grad_x, d_sc_m, d_sh_m = _rowwise("modulate_in_bwd", modulate_bwd, [xs, dh1, dx_a], [sc_m + token], [(d, F32)], [(1, d), (1, d)], tm)
    dmod = jnp.concatenate([d_sh_m, d_sc_m, d_gt_m, d_sh_f, d_sc_f, d_gt_f], axis=1)
    return grad_x, loss_acc, dmod, d_conv8, d_al8, d_dt8, d_dn_g, d_q_g, d_kv_g, d_ln1_g, d_ln1_b, d_ln2_g, d_ln2_b
```

```python
import functools
import math

import jax
import jax.numpy as jnp
from jax import lax
from jax.experimental import pallas as pl
from jax.experimental.pallas import tpu as pltpu

F32 = jnp.float32
BF = jnp.bfloat16
HI = lax.Precision.HIGHEST

N_DEV = 8
DN_HEADS = 4
DN_DK = 128
DN_DV = 128
CONV_K = 4
CHUNK = 64
MLA_HEADS = 4
QK_NOPE = 128
QK_ROPE = 64
V_HEAD = 128
Q_LORA = 512
KV_LORA = 256
ROPE_THETA = 10000.0
DEPTH = 1
DEEPNORM_ALPHA = (2.0 * DEPTH) ** 0.25
LANE = 128
CONV_HALO = 8

DN_QK = DN_HEADS * DN_DK
DN_VW = DN_HEADS * DN_DV
DN_CONV_CH = 2 * DN_QK + DN_VW
MLA_VW = MLA_HEADS * V_HEAD
MLA_QCAT = QK_NOPE + LANE
N_IN = DN_CONV_CH + DN_VW + 2 * DN_HEADS + Q_LORA + KV_LORA + QK_ROPE
P_QKV = 0
P_Z = DN_CONV_CH
P_CQ = P_Z + DN_VW
P_CKV = P_CQ + Q_LORA
P_BA = P_CKV + KV_LORA
P_KR = P_BA + LANE
N_INP = P_KR + LANE

ADAM_LR = 0.001
ADAM_B1 = 0.9
ADAM_B2 = 0.999
ADAM_EPS = 1e-08
ADAM_WD = 0.01
ADAM_STEP = 10

NN = (((1,), (0,)), ((), ()))
NT = (((1,), (1,)), ((), ()))
TN = (((0,), (0,)), ((), ()))


def _pick(n, prefs):
    for p in prefs:
        if n % p == 0:
            return p
    return n


def _full(shape):
    return pl.BlockSpec(shape, lambda *_: (0,) * len(shape))


def _dot(a, b, dims=NN):
    return lax.dot_general(a, b, dims, preferred_element_type=F32)


def _doth(a, b, dims=NN):
    return lax.dot_general(a, b, dims, precision=HI, preferred_element_type=F32)


@jax.custom_vjp
def _mmb(a, b):
    return _dot(a.astype(BF), b.astype(BF), NN)


def _mmb_fwd(a, b):
    return _mmb(a, b), (a, b)


def _mmb_bwd(res, g):
    a, b = res
    gb = g.astype(BF)
    return (_dot(gb, b.astype(BF), NT).astype(a.dtype), _dot(a.astype(BF), gb, TN).astype(b.dtype))


_mmb.defvjp(_mmb_fwd, _mmb_bwd)


@jax.custom_vjp
def _mmb_nt(a, b):
    return _dot(a.astype(BF), b.astype(BF), NT)


def _mmb_nt_fwd(a, b):
    return _mmb_nt(a, b), (a, b)


def _mmb_nt_bwd(res, g):
    a, b = res
    gb = g.astype(BF)
    return (_dot(gb, b.astype(BF), NN).astype(a.dtype), _dot(gb, a.astype(BF), TN).astype(b.dtype))


_mmb_nt.defvjp(_mmb_nt_fwd, _mmb_nt_bwd)


@jax.custom_vjp
def _mmb_tn(a, b):
    return _dot(a.astype(BF), b.astype(BF), TN)


def _mmb_tn_fwd(a, b):
    return _mmb_tn(a, b), (a, b)


def _mmb_tn_bwd(res, g):
    a, b = res
    gb = g.astype(BF)
    return (_dot(b.astype(BF), gb, NT).astype(a.dtype), _dot(a.astype(BF), gb, NN).astype(b.dtype))


_mmb_tn.defvjp(_mmb_tn_fwd, _mmb_tn_bwd)


def _sigmoid(x):
    return 0.5 * (jnp.tanh(0.5 * x) + 1.0)


def _silu(x):
    return x * _sigmoid(x)


def _softplus(x):
    return jnp.maximum(x, 0.0) + jnp.log(1.0 + jnp.exp(-jnp.abs(x)))


def _layernorm(x, g, b, eps=1e-5):
    mu = jnp.mean(x, axis=-1, keepdims=True)
    xc = x - mu
    var = jnp.mean(xc * xc, axis=-1, keepdims=True)
    return xc * lax.rsqrt(var + eps) * g + b


def _rmsnorm(x, g, eps=1e-6):
    return x * lax.rsqrt(jnp.mean(x * x, axis=-1, keepdims=True) + eps) * g


def _l2norm(x, eps=1e-6):
    return x * lax.rsqrt(jnp.sum(x * x, axis=-1, keepdims=True) + eps)


def _rowwise(name, fn, rows, vecs, out_rows, out_accs, tm):
    rows = [r if isinstance(r, tuple) else (r, r.shape[1], 0) for r in rows]
    t = rows[0][0].shape[0]
    tm = min(tm, t)
    assert t % tm == 0
    nr, nv, no = len(rows), len(vecs), len(out_rows)

    def body(*refs):
        ins = [r[...] for r in refs[:nr + nv]]
        outs = fn(*ins)
        outs = outs if isinstance(outs, (tuple, list)) else (outs,)
        o_rows = refs[nr + nv:nr + nv + no]
        o_accs = refs[nr + nv + no:]
        for o, val in zip(o_rows, outs[:no]):
            o[...] = val.astype(o.dtype)
        if o_accs:
            @pl.when(pl.program_id(0) == 0)
            def _():
                for o in o_accs:
                    o[...] = jnp.zeros_like(o)
            for o, val in zip(o_accs, outs[no:]):
                o[...] += val

    in_specs = [pl.BlockSpec((tm, w), functools.partial(lambda i, j: (i, j), j=j)) for (_, w, j) in rows]
    in_specs += [_full(v.shape) for v in vecs]
    out_specs = [pl.BlockSpec((tm, w), lambda i: (i, 0)) for (w, _) in out_rows]
    out_specs += [_full(s) for s in out_accs]
    out_shape = [jax.ShapeDtypeStruct((t, w), d) for (w, d) in out_rows]
    out_shape += [jax.ShapeDtypeStruct(s, F32) for s in out_accs]
    res = pl.pallas_call(
        body, grid=(t // tm,), in_specs=in_specs, out_specs=out_specs, out_shape=out_shape, name=name,
        compiler_params=pltpu.CompilerParams(dimension_semantics=("arbitrary",)),
    )(*[r[0] for r in rows], *vecs)
    return res


def _matmul(a, b, mode, name, out_dtype=F32):
    if mode == "nn":
        (m, k), n = a.shape, b.shape[1]
    elif mode == "nt":
        (m, k), n = a.shape, b.shape[0]
    else:
        (k, m), n = a.shape, b.shape[1]
    tm, tn, tk = _matmul_tiles(m, n, k, a.dtype.itemsize, b.dtype.itemsize, jnp.dtype(out_dtype).itemsize)
    nk = k // tk
    dims = {"nn": NN, "nt": NT, "tn": TN}[mode]

    def body(a_ref, b_ref, o_ref, *acc):
        part = _dot(a_ref[...].astype(BF), b_ref[...].astype(BF), dims)
        if nk == 1:
            o_ref[...] = part.astype(o_ref.dtype)
            return
        (acc_ref,) = acc
        kk = pl.program_id(2)

        @pl.when(kk == 0)
        def _():
            acc_ref[...] = part

        @pl.when(kk > 0)
        def _():
            acc_ref[...] += part

        @pl.when(kk == nk - 1)
        def _():
            o_ref[...] = acc_ref[...].astype(o_ref.dtype)

    a_spec = pl.BlockSpec((tk, tm), lambda i, j, kk: (kk, i)) if mode == "tn" else pl.BlockSpec((tm, tk), lambda i, j, kk: (i, kk))
    b_spec = pl.BlockSpec((tn, tk), lambda i, j, kk: (j, kk)) if mode == "nt" else pl.BlockSpec((tk, tn), lambda i, j, kk: (kk, j))
    return pl.pallas_call(
        body, grid=(m // tm, n // tn, nk), in_specs=[a_spec, b_spec],
        out_specs=pl.BlockSpec((tm, tn), lambda i, j, kk: (i, j)),
        out_shape=jax.ShapeDtypeStruct((m, n), out_dtype),
        scratch_shapes=[pltpu.VMEM((tm, tn), F32)] if nk > 1 else [], name=name,
        compiler_params=pltpu.CompilerParams(dimension_semantics=("parallel", "parallel", "arbitrary")),
    )(a, b)


def _lane_tile(n, cap):
    return max([n // s for s in range(1, n // LANE + 1) if n % s == 0 and (n // s) % LANE == 0 and n // s <= cap] or [n])


def _ffn_in(h, w_gate, w_up):
    m, k = h.shape
    f = w_gate.shape[0]
    tm, tn = _pick(m, (512, 256, 128)), _lane_tile(f, 1408)

    def body(h_ref, wg_ref, wu_ref, act_ref, g_ref, u_ref):
        hh = h_ref[...]
        g = _dot(hh, wg_ref[...], NT)
        u = _dot(hh, wu_ref[...], NT)
        act_ref[...] = (_silu(g) * u).astype(act_ref.dtype)
        g_ref[...] = g.astype(g_ref.dtype)
        u_ref[...] = u.astype(u_ref.dtype)

    w_spec = pl.BlockSpec((tn, k), lambda i, j: (j, 0))
    o_spec = pl.BlockSpec((tm, tn), lambda i, j: (i, j))
    return pl.pallas_call(
        body, grid=(m // tm, f // tn), in_specs=[pl.BlockSpec((tm, k), lambda i, j: (i, 0)), w_spec, w_spec],
        out_specs=[o_spec] * 3, out_shape=[jax.ShapeDtypeStruct((m, f), BF)] * 3, name="ffn_in",
        compiler_params=pltpu.CompilerParams(dimension_semantics=("parallel", "parallel")),
    )(h, w_gate, w_up)


def _ffn_act_bwd(dff, w_down, gate, up):
    m, k = dff.shape
    f = w_down.shape[0]
    tm, tn = _pick(m, (512, 256, 128)), _lane_tile(f, 1408)

    def body(d_ref, w_ref, g_ref, u_ref, dg_ref, du_ref):
        da = _dot(d_ref[...], w_ref[...], NT)
        g = g_ref[...].astype(F32)
        sg = _sigmoid(g)
        dg_ref[...] = (da * u_ref[...].astype(F32) * (sg * (1.0 + g * (1.0 - sg)))).astype(dg_ref.dtype)
        du_ref[...] = (da * (g * sg)).astype(du_ref.dtype)

    o_spec = pl.BlockSpec((tm, tn), lambda i, j: (i, j))
    return pl.pallas_call(
        body, grid=(m // tm, f // tn),
        in_specs=[pl.BlockSpec((tm, k), lambda i, j: (i, 0)), pl.BlockSpec((tn, k), lambda i, j: (j, 0)), o_spec, o_spec],
        out_specs=[o_spec] * 2, out_shape=[jax.ShapeDtypeStruct((m, f), BF)] * 2, name="d_ffn_act",
        compiler_params=pltpu.CompilerParams(dimension_semantics=("parallel", "parallel")),
    )(dff, w_down, gate, up)


def _matmul2_nn(a1, b1, a2, b2, name):
    m, k = a1.shape
    n = b1.shape[1]
    tm, tn = _pick(m, (512, 256, 128)), _pick(n, (512, 256, 128))

    def body(a1_ref, b1_ref, a2_ref, b2_ref, o_ref):
        o_ref[...] = _dot(a1_ref[...], b1_ref[...]) + _dot(a2_ref[...], b2_ref[...])

    a_spec = pl.BlockSpec((tm, k), lambda i, j: (i, 0))
    b_spec = pl.BlockSpec((k, tn), lambda i, j: (0, j))
    return pl.pallas_call(
        body, grid=(m // tm, n // tn), in_specs=[a_spec, b_spec, a_spec, b_spec],
        out_specs=pl.BlockSpec((tm, tn), lambda i, j: (i, j)), out_shape=jax.ShapeDtypeStruct((m, n), F32), name=name,
        compiler_params=pltpu.CompilerParams(dimension_semantics=("parallel", "parallel")),
    )(a1, b1, a2, b2)


MATMUL_VMEM_BUDGET = 28 * 1024 * 1024


def _matmul_tiles(m, n, k, a_bytes, b_bytes, o_bytes):
    def divisors(x, cap):
        return sorted({x // s for s in range(1, 65) if x % s == 0 and (x // s) % LANE == 0 and x // s <= cap}, reverse=True) or [x]

    for tk in divisors(k, k):
        best = None
        for tm in divisors(m, 1024):
            for tn in divisors(n, 2048):
                need = 2 * (tm * tk * a_bytes + tk * tn * b_bytes + tm * tn * o_bytes) + (tm * tn * 4 if tk < k else 0)
                if need <= MATMUL_VMEM_BUDGET and tm * tn >= 512 * 512 and (best is None or tm * tn > best[0] * best[1]):
                    best = (tm, tn)
        if best:
            return best[0], best[1], tk
    return _pick(m, (512, 256, 128)), _pick(n, (512, 256, 128)), _pick(k, (512, 256, 128))


def _exchange(xs, name, scatter):
    n = len(xs)
    npeer = N_DEV - 1

    def body(*refs):
        x_refs, o_refs = refs[:n], refs[n:2 * n]
        send_sems, recv_sems, local_sems = refs[2 * n:]
        mx, my, mc = lax.axis_index("x"), lax.axis_index("y"), lax.axis_index("c")
        me = 4 * mx + 2 * my + mc
        src_me = [x.at[me] if scatter else x for x in x_refs]
        mine = [pltpu.make_async_copy(src_me[a], o_refs[a].at[me], local_sems.at[a]) for a in range(n)]
        for cp in mine:
            cp.start()
        copies = []
        for k in range(1, N_DEV):
            px, py, pc = mx ^ (k >> 2), my ^ ((k >> 1) & 1), mc ^ (k & 1)
            peer = 4 * px + 2 * py + pc
            for a in range(n):
                cp = pltpu.make_async_remote_copy(
                    src_ref=x_refs[a].at[peer] if scatter else x_refs[a], dst_ref=o_refs[a].at[me],
                    send_sem=send_sems.at[a * npeer + k - 1], recv_sem=recv_sems.at[a * npeer + k - 1],
                    device_id=(px, py, pc), device_id_type=pl.DeviceIdType.MESH)
                cp.start()
                copies.append((cp, a, k, peer))
        for cp, a, k, peer in copies:
            pltpu.make_async_remote_copy(
                src_ref=src_me[a], dst_ref=o_refs[a].at[peer], send_sem=send_sems.at[a * npeer + k - 1],
                recv_sem=recv_sems.at[a * npeer + k - 1], device_id=(mx, my, mc),
                device_id_type=pl.DeviceIdType.MESH).wait_recv()
        for cp, _, _, _ in copies:
            cp.wait_send()
        for cp in mine:
            cp.wait()

    return pl.pallas_call(
        body, out_shape=[jax.ShapeDtypeStruct((N_DEV,) + x.shape[-2:], x.dtype) for x in xs],
        in_specs=[pl.BlockSpec(memory_space=pl.ANY)] * n, out_specs=[pl.BlockSpec(memory_space=pl.ANY)] * n,
        scratch_shapes=[pltpu.SemaphoreType.DMA((n * npeer,)), pltpu.SemaphoreType.DMA((n * npeer,)),
                        pltpu.SemaphoreType.DMA((n,))],
        name=name,
    )(*xs)


def _peer_of(k):
    mx, my, mc = lax.axis_index("x"), lax.axis_index("y"), lax.axis_index("c")
    px, py, pc = mx ^ (k >> 2), my ^ ((k >> 1) & 1), mc ^ (k & 1)
    return (px, py, pc), 4 * px + 2 * py + pc


def _exchange_start(xs, name, scatter):
    n = len(xs)
    npeer = N_DEV - 1

    def body(*refs):
        x_refs, land_refs = refs[:n], refs[n:2 * n]
        send_sems, recv_sems, token = refs[2 * n], refs[2 * n + 1], refs[-1]
        me = 4 * lax.axis_index("x") + 2 * lax.axis_index("y") + lax.axis_index("c")
        for k in range(1, N_DEV):
            dev, peer = _peer_of(k)
            for a in range(n):
                pltpu.make_async_remote_copy(
                    src_ref=x_refs[a].at[peer] if scatter else x_refs[a], dst_ref=land_refs[a].at[me],
                    send_sem=send_sems.at[a * npeer + k - 1], recv_sem=recv_sems.at[a * npeer + k - 1],
                    device_id=dev, device_id_type=pl.DeviceIdType.MESH).start()
        token[...] = jnp.zeros_like(token)

    hbm = pl.BlockSpec(memory_space=pltpu.HBM)
    sem = pl.BlockSpec(memory_space=pltpu.SEMAPHORE)
    lands = [pltpu.with_memory_space_constraint(lax.empty((N_DEV,) + x.shape[-2:], x.dtype), pltpu.HBM) for x in xs]
    srcs = [pltpu.with_memory_space_constraint(x, pltpu.HBM) for x in xs]
    outs = pl.pallas_call(
        body, name=name,
        out_shape=(pltpu.SemaphoreType.DMA((n * npeer,)), pltpu.SemaphoreType.DMA((n * npeer,)),
                   *[pltpu.HBM(x.shape, x.dtype) for x in srcs], *[pltpu.HBM(z.shape, z.dtype) for z in lands],
                   jax.ShapeDtypeStruct((8, LANE), F32)),
        in_specs=[hbm] * (2 * n), out_specs=(sem, sem, *[hbm] * (2 * n), pl.BlockSpec(memory_space=pltpu.VMEM)),
        input_output_aliases={i: 2 + i for i in range(2 * n)},
        compiler_params=pltpu.CompilerParams(has_side_effects=pltpu.SideEffectType.DATAFLOW_SIDE_EFFECTING),
    )(*srcs, *lands)
    return (outs[0], outs[1], list(outs[2:2 + n]), list(outs[2 + n:2 + 2 * n])), outs[-1][0:1, 0:1]


def _exchange_wait(started, after, name, scatter):
    send_sems, recv_sems, srcs, lands = started
    n = len(srcs)
    npeer = N_DEV - 1

    def body(*refs):
        x_refs, land_refs = refs[:n], refs[n:2 * n]
        send_sems, recv_sems = refs[2 * n], refs[2 * n + 1]
        mx, my, mc = lax.axis_index("x"), lax.axis_index("y"), lax.axis_index("c")
        me = 4 * mx + 2 * my + mc
        for k in range(1, N_DEV):
            _, peer = _peer_of(k)
            for a in range(n):
                src = x_refs[a].at[me] if scatter else x_refs[a]
                cp = pltpu.make_async_remote_copy(
                    src_ref=src, dst_ref=land_refs[a].at[peer], send_sem=send_sems.at[a * npeer + k - 1],
                    recv_sem=recv_sems.at[a * npeer + k - 1], device_id=(mx, my, mc), device_id_type=pl.DeviceIdType.MESH)
                cp.wait_send()
                cp.wait_recv()

    hbm = pl.BlockSpec(memory_space=pltpu.HBM)
    sem = pl.BlockSpec(memory_space=pltpu.SEMAPHORE)
    outs = pl.pallas_call(
        body, name=name,
        out_shape=(*[pltpu.HBM(x.shape, x.dtype) for x in srcs], *[pltpu.HBM(z.shape, z.dtype) for z in lands]),
        in_specs=[hbm] * (2 * n) + [sem, sem, pl.BlockSpec(memory_space=pl.ANY)], out_specs=tuple([hbm] * (2 * n)),
        input_output_aliases={i: i for i in range(2 * n)},
        compiler_params=pltpu.CompilerParams(has_side_effects=pltpu.SideEffectType.DATAFLOW_SIDE_EFFECTING),
    )(*srcs, *lands, send_sems, recv_sems, after)
    me = 4 * lax.axis_index("x") + 2 * lax.axis_index("y") + lax.axis_index("c")
    full = []
    for x, land in zip(outs[:n], outs[n:]):
        own = lax.dynamic_slice(x, (me, 0, 0), (1,) + x.shape[1:]) if scatter else x[None]
        full.append(lax.dynamic_update_slice(land, own, (me, 0, 0)))
    return full


def _sum_slots(x, name):
    _, r, c = x.shape
    tr = _pick(r, (512, 256, 128, 64, 32, 16))

    def body(x_ref, o_ref):
        acc = x_ref[0].astype(F32)
        for s in range(1, N_DEV):
            acc = acc + x_ref[s].astype(F32)
        o_ref[...] = acc

    return pl.pallas_call(
        body, grid=(r // tr,), in_specs=[pl.BlockSpec((N_DEV, tr, c), lambda i: (0, i, 0))],
        out_specs=pl.BlockSpec((tr, c), lambda i: (i, 0)), out_shape=jax.ShapeDtypeStruct((r, c), F32), name=name,
        compiler_params=pltpu.CompilerParams(dimension_semantics=("arbitrary",)),
    )(x)


def _mod_fwd(c_all, w_ada, b_ada_mine):
    def body(c_ref, w_ref, b_ref, o_ref):
        o_ref[...] = _doth(_silu(c_ref[...]), w_ref[...]) + b_ref[...]

    return pl.pallas_call(body, out_shape=jax.ShapeDtypeStruct((c_all.shape[0], w_ada.shape[1]), F32), name="mod_fwd")(c_all, w_ada, b_ada_mine)


def _mod_bwd(c_all_t, dmod_mine):
    def body(ct_ref, d_ref, o_ref):
        s = _silu(ct_ref[...])
        acc = s[:, 0:1] * d_ref[pl.ds(0, 1), :]
        for b in range(1, N_DEV):
            acc = acc + s[:, b:b + 1] * d_ref[pl.ds(b, 1), :]
        o_ref[...] = acc

    return pl.pallas_call(body, out_shape=jax.ShapeDtypeStruct((c_all_t.shape[0], dmod_mine.shape[1]), F32), name="mod_bwd")(c_all_t, dmod_mine)


def _conv_fwd(proj, conv_w8, tm):
    t = proj.shape[0]
    ch = DN_CONV_CH

    def body(x_ref, w_ref, o_ref, buf):
        @pl.when(pl.program_id(0) == 0)
        def _():
            buf[pl.ds(0, CONV_HALO), :] = jnp.zeros((CONV_HALO, ch), F32)

        buf[pl.ds(CONV_HALO, tm), :] = x_ref[...]
        acc = jnp.zeros((tm, ch), F32)
        for j in range(CONV_K):
            acc = acc + buf[pl.ds(CONV_HALO - (CONV_K - 1) + j, tm), :] * w_ref[pl.ds(j, 1), :]
        o_ref[...] = _silu(acc)
        buf[pl.ds(0, CONV_HALO), :] = buf[pl.ds(tm, CONV_HALO), :]

    return pl.pallas_call(
        body, grid=(t // tm,), in_specs=[pl.BlockSpec((tm, ch), lambda i: (i, 0)), _full(conv_w8.shape)],
        out_specs=pl.BlockSpec((tm, ch), lambda i: (i, 0)), out_shape=jax.ShapeDtypeStruct((t, ch), F32),
        scratch_shapes=[pltpu.VMEM((tm + CONV_HALO, ch), F32)], name="conv_fwd",
        compiler_params=pltpu.CompilerParams(dimension_semantics=("arbitrary",)),
    )(proj, conv_w8)


def _conv_bwd(proj, conv_w8, dact, tm):
    t = proj.shape[0]
    ch = DN_CONV_CH
    nt = t // tm
    hb = tm // CONV_HALO

    def body(x_ref, xp_ref, w_ref, dy_ref, dx_ref, dw_ref, xbuf, dbuf):
        step = pl.program_id(0)

        @pl.when(step == 0)
        def _():
            dbuf[pl.ds(tm, CONV_HALO), :] = jnp.zeros((CONV_HALO, ch), F32)
            dw_ref[...] = jnp.zeros_like(dw_ref)

        first = step == nt - 1
        xbuf[pl.ds(0, CONV_HALO), :] = jnp.where(first, 0.0, xp_ref[...])
        xbuf[pl.ds(CONV_HALO, tm), :] = x_ref[...]
        pre = jnp.zeros((tm, ch), F32)
        for j in range(CONV_K):
            pre = pre + xbuf[pl.ds(CONV_HALO - (CONV_K - 1) + j, tm), :] * w_ref[pl.ds(j, 1), :]
        sg = _sigmoid(pre)
        dpre = dy_ref[...] * (sg * (1.0 + pre * (1.0 - sg)))
        dbuf[pl.ds(0, tm), :] = dpre
        dx = jnp.zeros((tm, ch), F32)
        for j in range(CONV_K):
            dx = dx + dbuf[pl.ds(CONV_K - 1 - j, tm), :] * w_ref[pl.ds(j, 1), :]
            dw_ref[pl.ds(j, 1), :] += jnp.sum(dpre * xbuf[pl.ds(CONV_HALO - (CONV_K - 1) + j, tm), :], axis=0, keepdims=True)
        dx_ref[...] = dx.astype(dx_ref.dtype)
        dbuf[pl.ds(tm, CONV_HALO), :] = dbuf[pl.ds(0, CONV_HALO), :]

    rev = lambda i: (nt - 1 - i, 0)
    prev = lambda i: (jnp.maximum((nt - 1 - i) * hb - 1, 0), 0)
    return pl.pallas_call(
        body, grid=(nt,),
        in_specs=[pl.BlockSpec((tm, ch), rev), pl.BlockSpec((CONV_HALO, ch), prev), _full(conv_w8.shape),
                  pl.BlockSpec((tm, ch), rev)],
        out_specs=[pl.BlockSpec((tm, ch), rev), _full(conv_w8.shape)],
        out_shape=[jax.ShapeDtypeStruct((t, ch), BF), jax.ShapeDtypeStruct(conv_w8.shape, F32)],
        scratch_shapes=[pltpu.VMEM((tm + CONV_HALO, ch), F32), pltpu.VMEM((tm + CONV_HALO, ch), F32)], name="conv_bwd",
        compiler_params=pltpu.CompilerParams(dimension_semantics=("arbitrary",)),
    )(proj, proj, conv_w8, dact)


BNN = (((2,), (1,)), ((0,), (0,)))
BNT = (((2,), (2,)), ((0,), (0,)))
BTN = (((1,), (1,)), ((0,), (0,)))


def _bdot(a, b, dims, precision=None):
    return lax.dot_general(a, b, dims, precision=precision, preferred_element_type=F32)


@jax.custom_vjp
def _bmmb_nt(a, b):
    return _bdot(a.astype(BF), b.astype(BF), BNT)


def _bmmb_nt_fwd(a, b):
    return _bmmb_nt(a, b), (a, b)


def _bmmb_nt_bwd(res, g):
    a, b = res
    gb = g.astype(BF)
    return _bdot(gb, b.astype(BF), BNN), _bdot(gb, a.astype(BF), BTN)


_bmmb_nt.defvjp(_bmmb_nt_fwd, _bmmb_nt_bwd)


@jax.custom_vjp
def _unit_lower_solve(a, r):
    return _unit_lower_solve_fwd(a, r)[0]


def _unit_lower_solve_fwd(a, r):
    c = a.shape[-1]
    ri = lax.broadcasted_iota(jnp.int32, a.shape, 1)
    ci = lax.broadcasted_iota(jnp.int32, a.shape, 2)
    xm = -a
    inv = (ri == ci).astype(F32) + xm
    for _ in range(int(math.log2(c)) - 1):
        xm = _bdot(xm, xm, BNN, HI)
        inv = inv + _bdot(inv, xm, BNN, HI)
    x = _bdot(inv, r, BNN, HI)
    return x, (inv, x)


def _unit_lower_solve_bwd(res, g):
    inv, x = res
    dr = _bdot(inv, g, BTN, HI)
    return -_bdot(dr, x, BNT, HI), dr


_unit_lower_solve.defvjp(_unit_lower_solve_fwd, _unit_lower_solve_bwd)


def _gdn_intra(qkv, ba, al8, dt8):
    tm = qkv.shape[0]
    nb = tm // CHUNK
    bsz = DN_HEADS * nb

    def heads(x0):
        return jnp.concatenate([qkv[:, x0 + h * LANE:x0 + (h + 1) * LANE].reshape(nb, CHUNK, LANE) for h in range(DN_HEADS)], axis=0)

    def spread(c0):
        return jnp.concatenate([jnp.broadcast_to(ba[:, c0 + h:c0 + h + 1], (tm, LANE)).reshape(nb, CHUNK, LANE)
                                for h in range(DN_HEADS)], axis=0)

    def per_head(v8):
        return jnp.concatenate([jnp.broadcast_to(v8[0:1, h:h + 1].reshape(1, 1, 1), (nb, 1, LANE)) for h in range(DN_HEADS)], axis=0)

    ri = lax.broadcasted_iota(jnp.int32, (bsz, CHUNK, CHUNK), 1)
    ci = lax.broadcasted_iota(jnp.int32, (bsz, CHUNK, CHUNK), 2)
    incl = ri >= ci
    strict = ri > ci

    q = _l2norm(heads(0)) * (DN_DK ** -0.5)
    k = _l2norm(heads(DN_QK))
    va = heads(2 * DN_QK)
    beta = _sigmoid(spread(0))
    g = -jnp.exp(per_head(al8)) * _softplus(spread(DN_HEADS) + per_head(dt8))
    gc = _bdot(incl.astype(F32), g, BNN, HI)
    g_last = jnp.sum(g, axis=1, keepdims=True)
    gcol = gc[:, :, :CHUNK]
    diff = gcol - jnp.swapaxes(gcol, 1, 2)
    decay = jnp.where(incl, jnp.exp(jnp.where(incl, diff, 0.0)), 0.0)
    kb = k * beta
    a_mat = jnp.where(strict, _bmmb_nt(kb, k) * decay, 0.0)
    egc = jnp.exp(gc)
    wu = _unit_lower_solve(a_mat, jnp.concatenate([kb * egc, va * beta], axis=2))
    attn = jnp.where(incl, _bmmb_nt(q, k) * decay, 0.0)

    def unheads(x):
        return jnp.concatenate([x[h * nb:(h + 1) * nb].reshape(tm, LANE) for h in range(DN_HEADS)], axis=1)

    return (unheads(wu[:, :, :DN_DK]), unheads(wu[:, :, DN_DK:]), unheads(q * egc), unheads(k * jnp.exp(g_last - gc)),
            attn.reshape(DN_HEADS, tm, CHUNK), unheads(jnp.broadcast_to(g_last, (bsz, CHUNK, LANE))))


def _gdn_scan_step(w, u, qg, kd, att, gl, s):
    v_new = u - _mmb(w, s)
    o = _mmb(qg, s) + _mmb(att, v_new)
    return o, s * jnp.exp(gl) + _mmb_tn(kd, v_new)


def _gdn_intra_specs(t, tm, dts):
    nb = tm // CHUNK
    specs = [pl.BlockSpec((tm, DN_VW), lambda i: (i, 0))] * 4
    specs += [pl.BlockSpec((DN_HEADS, tm, CHUNK), lambda i: (0, i, 0)), pl.BlockSpec((tm, DN_VW), lambda i: (i, 0))]
    shapes = [jax.ShapeDtypeStruct((t, DN_VW), dts[i]) for i in range(4)]
    shapes += [jax.ShapeDtypeStruct((DN_HEADS, t, CHUNK), dts[4]), jax.ShapeDtypeStruct((t, DN_VW), dts[5])]
    return specs, shapes


def _gdn_intra_fwd(qkv, proj, al8, dt8, tm):
    t = qkv.shape[0]

    def body(qkv_ref, ba_ref, al_ref, dt_ref, *outs):
        for o, val in zip(outs, _gdn_intra(qkv_ref[...], ba_ref[...], al_ref[...], dt_ref[...])):
            o[...] = val.astype(o.dtype)

    specs, shapes = _gdn_intra_specs(t, tm, (BF, F32, BF, BF, BF, F32))
    return pl.pallas_call(
        body, grid=(t // tm,),
        in_specs=[pl.BlockSpec((tm, DN_CONV_CH), lambda i: (i, 0)), pl.BlockSpec((tm, LANE), lambda i: (i, P_BA // LANE)),
                  _full(al8.shape), _full(dt8.shape)],
        out_specs=specs, out_shape=shapes, name="gdn_intra_fwd",
        compiler_params=pltpu.CompilerParams(dimension_semantics=("parallel",)),
    )(qkv, proj, al8, dt8)


def _gdn_intra_bwd(qkv, proj, al8, dt8, cts, tm):
    t = qkv.shape[0]

    def body(qkv_ref, ba_ref, al_ref, dt_ref, *refs):
        ct_refs, (dqkv_ref, dba_ref, dal_ref, ddt_ref) = refs[:6], refs[6:]

        @pl.when(pl.program_id(0) == 0)
        def _():
            dal_ref[...] = jnp.zeros_like(dal_ref)
            ddt_ref[...] = jnp.zeros_like(ddt_ref)

        _, vjp = jax.vjp(_gdn_intra, qkv_ref[...], ba_ref[...], al_ref[...], dt_ref[...])
        dqkv, dba, dal, ddt = vjp(tuple(r[...] for r in ct_refs))
        dqkv_ref[...] = dqkv
        dba_ref[...] = dba.astype(dba_ref.dtype)
        dal_ref[...] += dal
        ddt_ref[...] += ddt

    specs, _ = _gdn_intra_specs(t, tm, (F32,) * 6)
    return pl.pallas_call(
        body, grid=(t // tm,),
        in_specs=[pl.BlockSpec((tm, DN_CONV_CH), lambda i: (i, 0)), pl.BlockSpec((tm, LANE), lambda i: (i, P_BA // LANE)),
                  _full(al8.shape), _full(dt8.shape)] + specs,
        out_specs=[pl.BlockSpec((tm, DN_CONV_CH), lambda i: (i, 0)), pl.BlockSpec((tm, LANE), lambda i: (i, 0)),
                   _full(al8.shape), _full(dt8.shape)],
        out_shape=[jax.ShapeDtypeStruct((t, DN_CONV_CH), F32), jax.ShapeDtypeStruct((t, LANE), BF),
                   jax.ShapeDtypeStruct(al8.shape, F32), jax.ShapeDtypeStruct(dt8.shape, F32)],
        name="gdn_intra_bwd", compiler_params=pltpu.CompilerParams(dimension_semantics=("arbitrary",)),
    )(qkv, proj, al8, dt8, *cts)


def _gdn_scan_fwd(intra, tm):
    t = intra[0].shape[0]
    nb = tm // CHUNK
    nc = t // CHUNK

    def body(w_ref, u_ref, qg_ref, kd_ref, att_ref, gl_ref, o_ref, ss_ref, s_scr):
        @pl.when(pl.program_id(0) == 0)
        def _():
            s_scr[...] = jnp.zeros_like(s_scr)

        for cc in range(nb):
            rows = pl.ds(cc * CHUNK, CHUNK)
            for h in range(DN_HEADS):
                cols = pl.ds(h * DN_DV, DN_DV)
                s_prev = s_scr[h]
                ss_ref[cc, h] = s_prev
                o, s_new = _gdn_scan_step(w_ref[rows, cols], u_ref[rows, cols], qg_ref[rows, cols], kd_ref[rows, cols],
                                          att_ref[h, rows, :], gl_ref[pl.ds(cc * CHUNK, 1), cols], s_prev)
                o_ref[rows, cols] = o
                s_scr[h] = s_new

    specs, _ = _gdn_intra_specs(t, tm, (F32,) * 6)
    return pl.pallas_call(
        body, grid=(t // tm,), in_specs=specs,
        out_specs=[pl.BlockSpec((tm, DN_VW), lambda i: (i, 0)),
                   pl.BlockSpec((nb, DN_HEADS, DN_DK, DN_DV), lambda i: (i, 0, 0, 0))],
        out_shape=[jax.ShapeDtypeStruct((t, DN_VW), F32), jax.ShapeDtypeStruct((nc, DN_HEADS, DN_DK, DN_DV), F32)],
        scratch_shapes=[pltpu.VMEM((DN_HEADS, DN_DK, DN_DV), F32)], name="gdn_scan_fwd",
        compiler_params=pltpu.CompilerParams(dimension_semantics=("arbitrary",)),
    )(*intra)


def _gdn_scan_bwd(intra, states, do, tm):
    t = intra[0].shape[0]
    nb = tm // CHUNK
    ng = t // tm

    def body(w_ref, u_ref, qg_ref, kd_ref, att_ref, gl_ref, ss_ref, do_ref,
             dw_ref, du_ref, dqg_ref, dkd_ref, datt_ref, dgl_ref, ds_scr):
        @pl.when(pl.program_id(0) == 0)
        def _():
            ds_scr[...] = jnp.zeros_like(ds_scr)

        for cc in reversed(range(nb)):
            rows = pl.ds(cc * CHUNK, CHUNK)
            for h in range(DN_HEADS):
                cols = pl.ds(h * DN_DV, DN_DV)
                f32 = lambda r: r[rows, cols].astype(F32)
                _, vjp = jax.vjp(_gdn_scan_step, f32(w_ref), u_ref[rows, cols], f32(qg_ref), f32(kd_ref),
                                 att_ref[h, rows, :].astype(F32), gl_ref[pl.ds(cc * CHUNK, 1), cols], ss_ref[cc, h])
                dw, du, dqg, dkd, datt, dgl, ds_prev = vjp((do_ref[rows, cols], ds_scr[h]))
                dw_ref[rows, cols] = dw
                du_ref[rows, cols] = du
                dqg_ref[rows, cols] = dqg
                dkd_ref[rows, cols] = dkd
                datt_ref[h, rows, :] = datt
                first_row = lax.broadcasted_iota(jnp.int32, (CHUNK, DN_DV), 0) == 0
                dgl_ref[rows, cols] = jnp.where(first_row, dgl, 0.0)
                ds_scr[h] = ds_prev

    rev = lambda i: (ng - 1 - i, 0)
    rev3 = lambda i: (0, ng - 1 - i, 0)
    row = pl.BlockSpec((tm, DN_VW), rev)
    six = [row] * 4 + [pl.BlockSpec((DN_HEADS, tm, CHUNK), rev3), row]
    _, shapes = _gdn_intra_specs(t, tm, (F32,) * 6)
    return pl.pallas_call(
        body, grid=(ng,),
        in_specs=six + [pl.BlockSpec((nb, DN_HEADS, DN_DK, DN_DV), lambda i: (ng - 1 - i, 0, 0, 0)), row],
        out_specs=six, out_shape=shapes,
        scratch_shapes=[pltpu.VMEM((DN_HEADS, DN_DK, DN_DV), F32)], name="gdn_scan_bwd",
        compiler_params=pltpu.CompilerParams(dimension_semantics=("arbitrary",)),
    )(*intra, states, do)


def _gdn_out(o, z, g):
    parts = []
    for h in range(DN_HEADS):
        sl = slice(h * DN_DV, (h + 1) * DN_DV)
        parts.append(_rmsnorm(o[:, sl], g) * _silu(z[:, sl]))
    return parts


_Q_SCALE = math.log2(math.e) / math.sqrt(QK_NOPE + QK_ROPE)


def _rope_tables(pos, inv_freq2):
    lane = lax.broadcasted_iota(jnp.int32, (1, LANE), 1)
    ang = pos * inv_freq2
    cos = jnp.where(lane < QK_ROPE, jnp.cos(ang), 0.0)
    sin = jnp.where(lane < QK_ROPE // 2, -jnp.sin(ang), jnp.where(lane < QK_ROPE, jnp.sin(ang), 0.0))
    return cos, sin


def _rope_swap():
    ri = lax.broadcasted_iota(jnp.int32, (LANE, LANE), 0)
    ci = lax.broadcasted_iota(jnp.int32, (LANE, LANE), 1)
    half = QK_ROPE // 2
    return (((ci < half) & (ri == ci + half)) | ((ci >= half) & (ci < QK_ROPE) & (ri == ci - half))).astype(F32)


def _mla_prep(cq, ckv, kr, gq, gkv, w_uq, w_ukv, cos, sin, swap):
    rope = lambda u: u * cos + _doth(u, swap) * sin
    q_lin = _mmb_nt(_rmsnorm(cq, gq), w_uq) * _Q_SCALE
    kv_lin = _mmb_nt(_rmsnorm(ckv, gkv), w_ukv)
    k_rope = rope(kr)
    qs, ks, vs = [], [], []
    for h in range(MLA_HEADS):
        qs += [q_lin[:, h * LANE:(h + 1) * LANE], rope(q_lin[:, (MLA_HEADS + h) * LANE:(MLA_HEADS + h + 1) * LANE])]
        ks += [kv_lin[:, 2 * h * LANE:(2 * h + 1) * LANE], k_rope]
        vs += [kv_lin[:, (2 * h + 1) * LANE:(2 * h + 2) * LANE]]
    return qs + ks + vs


def _mla_prep_fwd(proj, pos_col, inv_freq2, gq, gkv, w_uq, w_ukv, tm):
    t = proj.shape[0]
    nq = 2 * MLA_HEADS

    def body(cq_ref, ckv_ref, kr_ref, pos_ref, f_ref, gq_ref, gkv_ref, wq_ref, wkv_ref, q_ref, k_ref, v_ref):
        cos, sin = _rope_tables(pos_ref[...], f_ref[...])
        outs = _mla_prep(cq_ref[...], ckv_ref[...], kr_ref[...], gq_ref[...], gkv_ref[...], wq_ref[...], wkv_ref[...],
                         cos, sin, _rope_swap())
        for i in range(nq):
            q_ref[:, pl.ds(i * LANE, LANE)] = outs[i].astype(q_ref.dtype)
            k_ref[:, pl.ds(i * LANE, LANE)] = outs[nq + i].astype(k_ref.dtype)
        for h in range(MLA_HEADS):
            v_ref[:, pl.ds(h * LANE, LANE)] = outs[2 * nq + h].astype(v_ref.dtype)

    row = lambda w, j: pl.BlockSpec((tm, w), functools.partial(lambda i, j: (i, j), j=j))
    return pl.pallas_call(
        body, grid=(t // tm,),
        in_specs=[row(Q_LORA, P_CQ // Q_LORA), row(KV_LORA, P_CKV // KV_LORA), row(LANE, P_KR // LANE),
                  pl.BlockSpec((tm, 1), lambda i: (i, 0)), _full(inv_freq2.shape), _full(gq.shape), _full(gkv.shape),
                  _full(w_uq.shape), _full(w_ukv.shape)],
        out_specs=[row(nq * LANE, 0), row(nq * LANE, 0), row(MLA_VW, 0)],
        out_shape=[jax.ShapeDtypeStruct((t, nq * LANE), BF), jax.ShapeDtypeStruct((t, nq * LANE), BF),
                   jax.ShapeDtypeStruct((t, MLA_VW), BF)],
        name="mla_prep_fwd", compiler_params=pltpu.CompilerParams(dimension_semantics=("arbitrary",)),
    )(proj, proj, proj, pos_col, inv_freq2, gq, gkv, w_uq, w_ukv)


def _mla_prep_bwd(proj, pos_col, inv_freq2, gq, gkv, w_uq, w_ukv, dq, dk, dv, tm):
    t = proj.shape[0]
    nq = 2 * MLA_HEADS

    def body(cq_ref, ckv_ref, kr_ref, pos_ref, f_ref, gq_ref, gkv_ref, wq_ref, wkv_ref, dq_ref, dk_ref, dv_ref,
             dcq_ref, dckv_ref, dkr_ref, dgq_ref, dgkv_ref, dwq_ref, dwkv_ref):
        @pl.when(pl.program_id(0) == 0)
        def _():
            for o in (dgq_ref, dgkv_ref, dwq_ref, dwkv_ref):
                o[...] = jnp.zeros_like(o)

        cos, sin = _rope_tables(pos_ref[...], f_ref[...])
        f = functools.partial(_mla_prep, cos=cos, sin=sin, swap=_rope_swap())
        _, vjp = jax.vjp(f, cq_ref[...], ckv_ref[...], kr_ref[...], gq_ref[...], gkv_ref[...], wq_ref[...], wkv_ref[...])
        cts = [dq_ref[:, pl.ds(i * LANE, LANE)] for i in range(nq)]
        cts += [dk_ref[:, pl.ds(i * LANE, LANE)] for i in range(nq)]
        cts += [dv_ref[:, pl.ds(h * LANE, LANE)] for h in range(MLA_HEADS)]
        dcq, dckv, dkr, dgq, dgkv, dwq, dwkv = vjp(cts)
        dcq_ref[...] = dcq.astype(dcq_ref.dtype)
        dckv_ref[...] = dckv.astype(dckv_ref.dtype)
        dkr_ref[...] = dkr.astype(dkr_ref.dtype)
        dgq_ref[...] += dgq
        dgkv_ref[...] += dgkv
        dwq_ref[...] += dwq
        dwkv_ref[...] += dwkv

    row = lambda w, j: pl.BlockSpec((tm, w), functools.partial(lambda i, j: (i, j), j=j))
    return pl.pallas_call(
        body, grid=(t // tm,),
        in_specs=[row(Q_LORA, P_CQ // Q_LORA), row(KV_LORA, P_CKV // KV_LORA), row(LANE, P_KR // LANE),
                  pl.BlockSpec((tm, 1), lambda i: (i, 0)), _full(inv_freq2.shape), _full(gq.shape), _full(gkv.shape),
                  _full(w_uq.shape), _full(w_ukv.shape), row(nq * LANE, 0), row(nq * LANE, 0), row(MLA_VW, 0)],
        out_specs=[row(Q_LORA, 0), row(KV_LORA, 0), row(LANE, 0), _full(gq.shape), _full(gkv.shape),
                   _full(w_uq.shape), _full(w_ukv.shape)],
        out_shape=[jax.ShapeDtypeStruct((t, Q_LORA), BF), jax.ShapeDtypeStruct((t, KV_LORA), BF),
                   jax.ShapeDtypeStruct((t, LANE), BF), jax.ShapeDtypeStruct(gq.shape, F32),
                   jax.ShapeDtypeStruct(gkv.shape, F32), jax.ShapeDtypeStruct(w_uq.shape, F32),
                   jax.ShapeDtypeStruct(w_ukv.shape, F32)],
        name="mla_prep_bwd", compiler_params=pltpu.CompilerParams(dimension_semantics=("arbitrary",)),
    )(proj, proj, proj, pos_col, inv_freq2, gq, gkv, w_uq, w_ukv, dq, dk, dv)


_NEG = -1e30
_LN2 = math.log(2.0)
ATT_CHAINS = 2


def _causal(tq, tk, q0, k0):
    row = q0 + lax.broadcasted_iota(jnp.int32, (tq, tk), 0)
    col = k0 + lax.broadcasted_iota(jnp.int32, (tq, tk), 1)
    return col <= row


def _attn_fwd(q, k, v, tq, tk):
    t = q.shape[0]

    assert tk % tq == 0

    th = tq // ATT_CHAINS

    def body(q_ref, k_ref, v_ref, o_ref, lse_ref):
        i = pl.program_id(1)
        n_full = (i * tq) // tk

        def step(k0, carry, masked):
            kt = k_ref[pl.ds(k0, tk), :]
            vt = v_ref[pl.ds(k0, tk), :]
            out = []
            for c, (m, l, acc) in enumerate(carry):
                s = _dot(q_ref[pl.ds(c * th, th), :], kt, NT)
                if masked:
                    s = jnp.where(_causal(th, tk, i * tq + c * th, k0), s, _NEG)
                m_new = jnp.maximum(m, jnp.max(s, axis=-1, keepdims=True))
                p = jnp.exp2(s - m_new)
                alpha = jnp.exp2(m - m_new)
                out.append((m_new, alpha * l + jnp.sum(p, axis=-1, keepdims=True), alpha * acc + _dot(p.astype(BF), vt)))
            return tuple(out)

        init = tuple((jnp.full((th, 1), _NEG, F32), jnp.zeros((th, 1), F32), jnp.zeros((th, V_HEAD), F32)) for _ in range(ATT_CHAINS))
        carry = lax.fori_loop(0, n_full, lambda j, c: step(pl.multiple_of(j * tk, tk), c, False), init)
        for c, (m, l, acc) in enumerate(step(pl.multiple_of(n_full * tk, tk), carry, True)):
            o_ref[pl.ds(c * th, th), :] = acc / l
            lse_ref[pl.ds(c * th, th), :] = jnp.broadcast_to(m + jnp.log2(l), (th, LANE))

    return pl.pallas_call(
        body, grid=(MLA_HEADS, t // tq),
        in_specs=[pl.BlockSpec((tq, 2 * LANE), lambda h, i: (i, h)), pl.BlockSpec((t, 2 * LANE), lambda h, i: (0, h)),
                  pl.BlockSpec((t, V_HEAD), lambda h, i: (0, h))],
        out_specs=[pl.BlockSpec((tq, V_HEAD), lambda h, i: (i, h)), pl.BlockSpec((tq, LANE), lambda h, i: (i, h))],
        out_shape=[jax.ShapeDtypeStruct((t, MLA_VW), F32), jax.ShapeDtypeStruct((t, MLA_HEADS * LANE), F32)],
        name="attn_fwd", compiler_params=pltpu.CompilerParams(dimension_semantics=("parallel", "arbitrary")),
    )(q, k, v)


def _attn_bwd(q, k, v, do, lse, delta, tq, tk):
    t = q.shape[0]
    nq = t // tq
    nkt = t // tk
    assert tk % tq == 0

    def body(q_ref, k_ref, v_ref, do_ref, lse_ref, dl_ref, dq_ref, dk_ref, dv_ref):
        j = pl.program_id(1)

        @pl.when(j == 0)
        def _():
            dq_ref[...] = jnp.zeros_like(dq_ref)

        kt = k_ref[...]
        vt = v_ref[...]

        def step(q0, carry, masked):
            dk, dv = carry
            rows = pl.ds(q0, tq)
            qt = q_ref[rows, :]
            dot_ = do_ref[rows, :]
            p = jnp.exp2(_dot(qt, kt, NT) - lse_ref[rows, pl.ds(0, 1)])
            if masked:
                p = jnp.where(_causal(tq, tk, q0, j * tk), p, 0.0)
            dv = dv + _dot(p.astype(BF), dot_, TN)
            ds = (p * (_dot(dot_, vt, NT) - dl_ref[rows, pl.ds(0, 1)])).astype(BF)
            dk = dk + _dot(ds, qt, TN)
            dq_ref[rows, :] += _dot(ds, kt)
            return dk, dv

        per = tk // tq
        carry = (jnp.zeros((tk, 2 * LANE), F32), jnp.zeros((tk, V_HEAD), F32))
        for dd in range(per):
            carry = step(pl.multiple_of(j * tk + dd * tq, tq), carry, True)

        def group(g, c):
            for dd in range(per):
                c = step(pl.multiple_of(g * tk + dd * tq, tq), c, False)
            return c

        dk, dv = lax.fori_loop(j + 1, nkt, group, carry)
        dk_ref[...] = dk * _LN2
        dv_ref[...] = dv

        @pl.when(j == nkt - 1)
        def _():
            dq_ref[...] = dq_ref[...] * _LN2

    return pl.pallas_call(
        body, grid=(MLA_HEADS, nkt),
        in_specs=[pl.BlockSpec((t, 2 * LANE), lambda h, j: (0, h)), pl.BlockSpec((tk, 2 * LANE), lambda h, j: (j, h)),
                  pl.BlockSpec((tk, V_HEAD), lambda h, j: (j, h)), pl.BlockSpec((t, V_HEAD), lambda h, j: (0, h)),
                  pl.BlockSpec((t, LANE), lambda h, j: (0, h)), pl.BlockSpec((t, LANE), lambda h, j: (0, h))],
        out_specs=[pl.BlockSpec((t, 2 * LANE), lambda h, j: (0, h)), pl.BlockSpec((tk, 2 * LANE), lambda h, j: (j, h)),
                   pl.BlockSpec((tk, V_HEAD), lambda h, j: (j, h))],
        out_shape=[jax.ShapeDtypeStruct((t, MLA_HEADS * 2 * LANE), F32), jax.ShapeDtypeStruct((t, MLA_HEADS * 2 * LANE), F32),
                   jax.ShapeDtypeStruct((t, MLA_VW), F32)],
        name="attn_bwd", compiler_params=pltpu.CompilerParams(dimension_semantics=("parallel", "arbitrary")),
    )(q, k, v, do, lse, delta)


def _adam_update(w, g, m, v):
    mm = ADAM_B1 * m + (1.0 - ADAM_B1) * g
    vv = ADAM_B2 * v + (1.0 - ADAM_B2) * jnp.square(g)
    m_hat = mm / (1.0 - ADAM_B1 ** ADAM_STEP)
    v_hat = vv / (1.0 - ADAM_B2 ** ADAM_STEP)
    return -ADAM_LR * (m_hat / (jnp.sqrt(v_hat) + ADAM_EPS) + ADAM_WD * w), mm, vv


def _adamw(w, g, m, v, name):
    r, c = w.shape
    tr = _pick(r, (256, 128, 64, 32, 16, 8))
    slots = g.ndim == 3

    def body(w_ref, g_ref, m_ref, v_ref, g_out, d_ref, nm_ref, nv_ref):
        if slots:
            gg = g_ref[0].astype(F32)
            for s in range(1, N_DEV):
                gg = gg + g_ref[s].astype(F32)
        else:
            gg = g_ref[...]
        g_out[...] = gg
        d_ref[...], nm_ref[...], nv_ref[...] = _adam_update(w_ref[...], gg, m_ref[...], v_ref[...])

    spec = pl.BlockSpec((tr, c), lambda i: (i, 0))
    g_spec = pl.BlockSpec((N_DEV, tr, c), lambda i: (0, i, 0)) if slots else spec
    return pl.pallas_call(
        body, grid=(r // tr,), in_specs=[spec, g_spec, spec, spec], out_specs=[spec] * 4,
        out_shape=[jax.ShapeDtypeStruct((r, c), F32)] * 4, name=name,
        compiler_params=pltpu.CompilerParams(dimension_semantics=("arbitrary",)),
    )(w, g, m, v)


def _adamw_many(ws, gs, ms, vs, name):
    n = len(ws)

    def body(*refs):
        for i in range(n):
            w_ref, g_ref, m_ref, v_ref = (refs[j * n + i] for j in range(4))
            d_ref, nm_ref, nv_ref = (refs[(4 + j) * n + i] for j in range(3))
            d_ref[...], nm_ref[...], nv_ref[...] = _adam_update(w_ref[...], g_ref[...], m_ref[...], v_ref[...])

    shapes = [jax.ShapeDtypeStruct(w.shape, F32) for w in ws]
    outs = pl.pallas_call(body, out_shape=shapes * 3, name=name)(*ws, *gs, *ms, *vs)
    return outs[:n], outs[n:2 * n], outs[2 * n:]


def _cast_bf16(xs, name, after=None):
    n = len(xs)
    extra = [] if after is None else [after]

    def body(*refs):
        outs = refs[n + len(extra):]
        for i in range(n):
            outs[i][...] = refs[i][...].astype(BF)

    vmem = pl.BlockSpec(memory_space=pltpu.VMEM)
    return pl.pallas_call(
        body, out_shape=[jax.ShapeDtypeStruct(x.shape, BF) for x in xs], name=name,
        in_specs=[vmem] * n + [pl.BlockSpec(memory_space=pl.ANY)] * len(extra), out_specs=[vmem] * n)(*xs, *extra)


def _pad_rows(a, n):
    return jnp.pad(a, ((0, n - a.shape[0]), (0, 0)))


def _w_in_to_padded(wt):
    s_ba = P_CQ
    s_cq = s_ba + 2 * DN_HEADS
    s_kr = s_cq + Q_LORA + KV_LORA
    return jnp.concatenate([wt[:s_ba], wt[s_cq:s_kr], _pad_rows(wt[s_ba:s_cq], LANE), _pad_rows(wt[s_kr:], LANE)], axis=0)


def _w_in_from_padded(wt):
    return jnp.concatenate([wt[:P_CQ], wt[P_BA:P_BA + 2 * DN_HEADS], wt[P_CQ:P_BA], wt[P_KR:P_KR + QK_ROPE]], axis=0)


def _w_uq_to_padded(wt):
    w3 = wt.reshape(MLA_HEADS, QK_NOPE + QK_ROPE, Q_LORA)
    nope = w3[:, :QK_NOPE].reshape(MLA_HEADS * QK_NOPE, Q_LORA)
    rope = jnp.pad(w3[:, QK_NOPE:], ((0, 0), (0, LANE - QK_ROPE), (0, 0))).reshape(MLA_HEADS * LANE, Q_LORA)
    return jnp.concatenate([nope, rope], axis=0)


def _w_uq_from_padded(wt):
    nope = wt[:MLA_HEADS * QK_NOPE].reshape(MLA_HEADS, QK_NOPE, Q_LORA)
    rope = wt[MLA_HEADS * QK_NOPE:].reshape(MLA_HEADS, LANE, Q_LORA)[:, :QK_ROPE]
    return jnp.concatenate([nope, rope], axis=1).reshape(MLA_HEADS * (QK_NOPE + QK_ROPE), Q_LORA)


def _pack(pieces, width, row_mult):
    flat = jnp.concatenate([p.reshape(-1) for p in pieces])
    n = flat.shape[0]
    rows = -(-n // (width * row_mult)) * row_mult
    return jnp.pad(flat, (0, rows * width - n)).reshape(rows, width)


def _unpack(flat, shapes):
    out, o = [], 0
    for s in shapes:
        n = math.prod(s)
        out.append(flat[o:o + n].reshape(s))
        o += n
    return out


def kernel(x, c, positions, w_ada, b_ada, w_in, conv_w, a_log, dt_bias, dn_norm_g, q_norm_g, w_uq, kv_norm_g, w_ukv, w_o, ln1_g, ln1_b, w_gate, w_up, w_down, ln2_g, ln2_b, loss_target, m_w_ada, m_b_ada, m_w_in, m_conv_w, m_a_log, m_dt_bias, m_dn_norm_g, m_q_norm_g, m_w_uq, m_kv_norm_g, m_w_ukv, m_w_o, m_ln1_g, m_ln1_b, m_w_gate, m_w_up, m_w_down, m_ln2_g, m_ln2_b, v_w_ada, v_b_ada, v_w_in, v_conv_w, v_a_log, v_dt_bias, v_dn_norm_g, v_q_norm_g, v_w_uq, v_kv_norm_g, v_w_ukv, v_w_o, v_ln1_g, v_ln1_b, v_w_gate, v_w_up, v_w_down, v_ln2_g, v_ln2_b):
    me = 4 * lax.axis_index("x") + 2 * lax.axis_index("y") + lax.axis_index("c")
    t, d = x.shape[1], x.shape[2]
    ada_n = w_ada.shape[2]

    tr = lambda w: w[0].T
    rows = lambda a: a.reshape(-1, a.shape[2])
    (in_shard,) = _cast_bf16([tr(w_in)], "cast_w_in")
    in_gather, token = _exchange_start([in_shard], "gather_w_in_start", scatter=False)
    cw = conv_w.shape[3]
    c_all, conv_all = _exchange([c + token, conv_w[0, :, 0, :]], "gather_small", scatter=False)
    c_all = c_all.reshape(N_DEV, d)
    conv_full = conv_all.transpose(1, 0, 2).reshape(CONV_K, N_DEV * cw)
    conv_w8 = jnp.pad(conv_full, ((0, 8 - CONV_K), (0, 0)))

    b_ada_mine = lax.dynamic_slice(b_ada, (0, me * ada_n), (1, ada_n))
    mod_cols = _mod_fwd(c_all, w_ada[0], b_ada_mine)
    (mod_all,) = _exchange([mod_cols.reshape(N_DEV, 1, ada_n)], "scatter_mod", scatter=True)
    mod = mod_all.reshape(1, N_DEV * ada_n)

    (a_in,) = _exchange_wait(in_gather, mod, "gather_w_in_wait", scatter=False)
    later = _cast_bf16([tr(w_uq), tr(w_ukv), w_o[0], tr(w_gate), tr(w_up), w_down[0]], "cast_weights", after=a_in)
    mixer_gather, token_a = _exchange_start(later[:3], "gather_mixer_weights_start", scatter=False)
    ffn_gather, token_b = _exchange_start(later[3:], "gather_ffn_weights_start", scatter=False)
    w_in_t = _w_in_to_padded(rows(a_in) + (token_a + token_b).astype(BF))

    def mixer_weights(after):
        a_uq, a_ukv, a_o = _exchange_wait(mixer_gather, after, "gather_mixer_weights_wait", scatter=False)
        return _w_uq_to_padded(rows(a_uq)), rows(a_ukv), rows(a_o)

    def ffn_weights(after):
        a_gate, a_up, a_down = _exchange_wait(ffn_gather, after, "gather_ffn_weights_wait", scatter=False)
        return rows(a_gate), rows(a_up), rows(a_down)

    def by_dest(g):
        return g.reshape(N_DEV, -1, g.shape[1])

    scatters = {}

    def grads_ready(tag, *g):
        if tag == "ffn":
            pieces = [by_dest(a) for a in g]
        elif tag == "mixer":
            g_w_o, g_w_uq_t, g_w_ukv_t = g
            pieces = [by_dest(g_w_o), by_dest(_w_uq_from_padded(g_w_uq_t).astype(BF)), by_dest(g_w_ukv_t.astype(BF))]
        else:
            pieces = [by_dest(_w_in_from_padded(g[0]))]
        scatters[tag], token = _exchange_start(pieces, "scatter_%s_grads_start" % tag, scatter=True)
        return token

    loc = _local_step(x[0], loss_target[0], positions[0], mod, w_in_t, mixer_weights, ffn_weights, grads_ready,
                      conv_w8, a_log, dt_bias, dn_norm_g, q_norm_g, kv_norm_g, ln1_g, ln1_b, ln2_g, ln2_b)
    grad_x, loss_acc, dmod, d_conv8, d_al8, d_dt8, d_dn_g, d_q_g, d_kv_g, d_ln1_g, d_ln1_b, d_ln2_g, d_ln2_b = loc

    small_shapes = [(6 * d,), (CONV_K, N_DEV * cw), (DN_HEADS,), (DN_HEADS,), (DN_DV,), (Q_LORA,), (KV_LORA,), (d,), (d,), (d,), (d,), (1,)]
    gsmall = _pack([dmod, d_conv8[:CONV_K], d_al8[0, :DN_HEADS], d_dt8[0, :DN_HEADS], d_dn_g, d_q_g, d_kv_g,
                    d_ln1_g, d_ln1_b, d_ln2_g, d_ln2_b, loss_acc[0, :1]], LANE, 8)
    (gsmall_all,) = _exchange([gsmall], "gather_small_grads", scatter=False)
    dmod_all = gsmall_all.reshape(N_DEV, -1)[:, :6 * d]
    tot = _unpack(_sum_slots(gsmall_all, "sum_small_grads").reshape(-1), small_shapes)
    g_b_ada, g_conv_full, g_a_log, g_dt_bias, g_dn_g, g_q_g, g_kv_g, g_ln1_g, g_ln1_b, g_ln2_g, g_ln2_b, loss1 = tot
    loss = loss1.reshape(())
    g_conv_w = lax.dynamic_slice(g_conv_full, (0, me * cw), (CONV_K, cw))
    g_w_ada = _mod_bwd(c_all.T, lax.dynamic_slice(dmod_all, (0, me * ada_n), (N_DEV, ada_n)))

    grads = {"w_ada": g_w_ada[None], "b_ada": g_b_ada[None], "conv_w": g_conv_w[None, :, None, :],
             "a_log": g_a_log[None], "dt_bias": g_dt_bias[None], "dn_norm_g": g_dn_g[None], "q_norm_g": g_q_g[None],
             "kv_norm_g": g_kv_g[None], "ln1_g": g_ln1_g[None], "ln1_b": g_ln1_b[None], "ln2_g": g_ln2_g[None], "ln2_b": g_ln2_b[None]}
    weights = dict(w_ada=w_ada, b_ada=b_ada, w_in=w_in, conv_w=conv_w, a_log=a_log, dt_bias=dt_bias, dn_norm_g=dn_norm_g,
                   q_norm_g=q_norm_g, w_uq=w_uq, kv_norm_g=kv_norm_g, w_ukv=w_ukv, w_o=w_o, ln1_g=ln1_g, ln1_b=ln1_b,
                   w_gate=w_gate, w_up=w_up, w_down=w_down, ln2_g=ln2_g, ln2_b=ln2_b)
    ms = dict(w_ada=m_w_ada, b_ada=m_b_ada, w_in=m_w_in, conv_w=m_conv_w, a_log=m_a_log, dt_bias=m_dt_bias,
              dn_norm_g=m_dn_norm_g, q_norm_g=m_q_norm_g, w_uq=m_w_uq, kv_norm_g=m_kv_norm_g, w_ukv=m_w_ukv, w_o=m_w_o,
              ln1_g=m_ln1_g, ln1_b=m_ln1_b, w_gate=m_w_gate, w_up=m_w_up, w_down=m_w_down, ln2_g=m_ln2_g, ln2_b=m_ln2_b)
    vs = dict(w_ada=v_w_ada, b_ada=v_b_ada, w_in=v_w_in, conv_w=v_conv_w, a_log=v_a_log, dt_bias=v_dt_bias,
              dn_norm_g=v_dn_norm_g, q_norm_g=v_q_norm_g, w_uq=v_w_uq, kv_norm_g=v_kv_norm_g, w_ukv=v_w_ukv, w_o=v_w_o,
              ln1_g=v_ln1_g, ln1_b=v_ln1_b, w_gate=v_w_gate, w_up=v_w_up, w_down=v_w_down, ln2_g=v_ln2_g, ln2_b=v_ln2_b)
    names = list(weights)
    big = ("w_ada", "w_gate", "w_up", "w_down", "w_o", "w_uq", "w_ukv", "w_in")
    waits = {"w_gate": ("ffn", ("w_gate", "w_up", "w_down")), "w_o": ("mixer", ("w_o", "w_uq", "w_ukv")), "w_in": ("in", ("w_in",))}
    delta_w, new_m, new_v, slots = {}, {}, {}, {}
    last = g_w_ada
    for n in big:
        if n == "w_in":
            rest = [r for r in names if r not in big]
            flat2 = lambda a: a.reshape(-1, a.shape[-1])
            outs = _adamw_many(*[[flat2(src[r]) for r in rest] for src in (weights, grads, ms, vs)], "adamw_small")
            for dst, o in zip((delta_w, new_m, new_v), outs):
                for r, a in zip(rest, o):
                    dst[r] = a.reshape(weights[r].shape)
            last = outs[0][0]
        transposed = n in ("w_in", "w_uq", "w_ukv", "w_gate", "w_up")
        two = (lambda a: a[0].T) if transposed else (lambda a: a[0])
        back = (lambda a: a.T[None]) if transposed else (lambda a: a[None])
        if n in waits:
            tag, members = waits[n]
            slots.update(zip(members, _exchange_wait(scatters[tag], last, "scatter_%s_grads_wait" % tag, scatter=True)))
        g_in = slots[n] if n in slots else two(grads[n])
        gr, dlt, nm, nv = _adamw(two(weights[n]), g_in, two(ms[n]), two(vs[n]), "adamw_" + n)
        grads[n], delta_w[n], new_m[n], new_v[n] = back(gr), back(dlt), back(nm), back(nv)
        last = nv

    return (loss, grad_x[None], *[grads[n] for n in names], *[delta_w[n] for n in names],
            *[new_m[n] for n in names], *[new_v[n] for n in names])


def _local_step(xs, tgt, pos, mod, w_in_t, mixer_weights, ffn_weights, grads_ready, conv_w8,
                a_log, dt_bias, dn_norm_g, q_norm_g, kv_norm_g, ln1_g, ln1_b, ln2_g, ln2_b):
    t, d = xs.shape
    sh_m, sc_m, gt_m, sh_f, sc_f, gt_f = [mod[:, i * d:(i + 1) * d] for i in range(6)]
    pos_col = pos.astype(F32).reshape(t, 1)
    inv_freq = 1.0 / (ROPE_THETA ** (jnp.arange(0, QK_ROPE, 2, dtype=F32) / QK_ROPE))
    inv_freq2 = jnp.pad(jnp.concatenate([inv_freq, inv_freq]), (0, LANE - QK_ROPE)).reshape(1, LANE)
    al8 = jnp.pad(a_log, ((0, 7), (0, LANE - DN_HEADS)))
    dt8 = jnp.pad(dt_bias, ((0, 7), (0, LANE - DN_HEADS)))

    tm = min(512, t)
    tq = min(256, t)
    tk = min(512, t)

    (h1,) = _rowwise("modulate_in", lambda xx, sc, sh: xx * (1.0 + sc) + sh, [xs], [sc_m, sh_m], [(d, BF)], [], tm)
    proj = _matmul(h1, w_in_t, "nt", "in_proj")
    qkv = _conv_fwd(proj, conv_w8, min(256, t))
    gdn_tm = min(512, t)
    intra = _gdn_intra_fwd(qkv, proj, al8, dt8, gdn_tm)
    o_dn, states = _gdn_scan_fwd(intra, gdn_tm)
    w_uq_t, w_ukv_t, w_o_f = mixer_weights(states)
    qc, kc, vc = _mla_prep_fwd(proj, pos_col, inv_freq2, q_norm_g, kv_norm_g, w_uq_t, w_ukv_t, tm)
    o_mla, lse = _attn_fwd(qc, kc, vc, tk, tk)

    def mix_in(o, z, om, g):
        return jnp.concatenate(_gdn_out(o, z, g) + [om], axis=1)

    (mixin,) = _rowwise("mixer_out", mix_in, [o_dn, (proj, DN_VW, P_Z // DN_VW), o_mla], [dn_norm_g], [(2 * DN_VW, BF)], [], tm)
    mix = _matmul(mixin, w_o_f, "nn", "out_proj")

    def block1(xx, mx, gt, g1, b1, sc, sh):
        x1 = _layernorm(DEEPNORM_ALPHA * xx + gt * mx, g1, b1)
        return x1, x1 * (1.0 + sc) + sh

    x1, h2 = _rowwise("norm1_modulate", block1, [xs, mix], [gt_m, ln1_g, ln1_b, sc_f, sh_f], [(d, F32), (d, BF)], [], tm)
    w_gate_f, w_up_f, w_down_f = ffn_weights(h2)
    act, gate, up = _ffn_in(h2, w_gate_f, w_up_f)
    ff = _matmul(act, w_down_f, "nn", "ffn_out")

    def tail_loss(x1_, ff_, gt, g2, b2, tg):
        y = _layernorm(DEEPNORM_ALPHA * x1_ + gt * ff_, g2, b2)
        return 0.5 * jnp.sum(jnp.mean(jnp.square(y - tg), axis=-1))

    def tail(x1_, ff_, tg, gt, g2, b2):
        loss, (dx1, dff, dgt, dg2, db2) = jax.value_and_grad(tail_loss, argnums=(0, 1, 2, 3, 4))(x1_, ff_, gt, g2, b2, tg)
        return dx1, dff, jnp.full((1, LANE), loss, F32), dgt, dg2, db2

    dx1_a, dff, loss_acc, d_gt_f, d_ln2_g, d_ln2_b = _rowwise(
        "norm2_loss", tail, [x1, ff, tgt], [gt_f, ln2_g, ln2_b], [(d, F32), (d, BF)], [(1, LANE), (1, d), (1, d), (1, d)], tm)

    g_w_down = _matmul(act, dff, "tn", "d_w_down", BF)
    dgate, dup = _ffn_act_bwd(dff, w_down_f, gate, up)
    g_w_gate = _matmul(dgate, h2, "tn", "d_w_gate", BF)
    g_w_up = _matmul(dup, h2, "tn", "d_w_up", BF)
    token = grads_ready("ffn", g_w_gate, g_w_up, g_w_down)
    dh2 = _matmul2_nn(dgate, w_gate_f, dup, w_up_f, "d_ffn_in")

    def block1_bwd(xx, mx, dx1_, dh2_, gt, g1, b1, sc, sh):
        _, vjp = jax.vjp(block1, xx, mx, gt, g1, b1, sc, sh)
        dxx, dmx, dgt, dg1, db1, dsc, dsh = vjp((dx1_, dh2_))
        return dxx, dmx, dgt, dg1, db1, dsc, dsh

    dx_a, dmix, d_gt_m, d_ln1_g, d_ln1_b, d_sc_f, d_sh_f = _rowwise(
        "norm1_modulate_bwd", block1_bwd, [xs, mix, dx1_a, dh2], [gt_m + token, ln1_g, ln1_b, sc_f, sh_f],
        [(d, F32), (d, BF)], [(1, d)] * 5, min(256, t))

    dmixin = _matmul(dmix, w_o_f, "nt", "d_mixer_out")
    g_w_o = _matmul(mixin, dmix, "tn", "d_w_o", BF)

    def mixer_bwd(o, z, om, dmi, g):
        _, vjp = jax.vjp(lambda o_, z_, g_: jnp.concatenate(_gdn_out(o_, z_, g_), axis=1), o, z, g)
        do_, dz_, dg_ = vjp(dmi[:, :DN_VW])
        dom = dmi[:, DN_VW:]
        delta = [jnp.broadcast_to(jnp.sum(dom[:, h * V_HEAD:(h + 1) * V_HEAD] * om[:, h * V_HEAD:(h + 1) * V_HEAD], axis=-1, keepdims=True), (o.shape[0], LANE))
                 for h in range(MLA_HEADS)]
        return do_, dz_, dom, jnp.concatenate(delta, axis=1), dg_

    do_dn, dz, do_mla, delta, d_dn_g = _rowwise(
        "mixer_out_bwd", mixer_bwd, [o_dn, (proj, DN_VW, P_Z // DN_VW), o_mla, dmixin], [dn_norm_g],
        [(DN_VW, F32), (DN_VW, BF), (MLA_VW, BF), (MLA_HEADS * LANE, F32)], [(1, DN_DV)], tm)

    dqc, dkc, dvc = _attn_bwd(qc, kc, vc, do_mla, lse, delta, tq, tk)
    dcq, dckv, dkr, d_q_g, d_kv_g, g_w_uq_t, g_w_ukv_t = _mla_prep_bwd(
        proj, pos_col, inv_freq2, q_norm_g, kv_norm_g, w_uq_t, w_ukv_t, dqc, dkc, dvc, min(256, t))

    token = grads_ready("mixer", g_w_o, g_w_uq_t, g_w_ukv_t)

    d_intra = _gdn_scan_bwd(intra, states, do_dn, gdn_tm)
    dqkv_act, dba, d_al8, d_dt8 = _gdn_intra_bwd(qkv, proj, al8 + token, dt8, d_intra, min(256, t))
    dqkv_pre, d_conv8 = _conv_bwd(proj, conv_w8, dqkv_act, min(256, t))

    dproj = jnp.concatenate([dqkv_pre, dz, dcq, dckv, dba, dkr], axis=1)
    dh1 = _matmul(dproj, w_in_t, "nn", "d_in_proj")
    g_w_in_t = _matmul(dproj, h1, "tn", "d_w_in", BF)
    token = grads_ready("in", g_w_in_t)

    def modulate_bwd(xx, dh, dxa, sc):
        return dh * (1.0 + sc) + dxa, jnp.sum(dh * xx, axis=0, keepdims=True), jnp.sum(dh, axis=0, keepdims=True)

    grad_x, d_sc_m, d_sh_m = _rowwise("modulate_in_bwd", modulate_bwd, [xs, dh1, dx_a], [sc_m + token], [(d, F32)], [(1, d), (1, d)], tm)
    dmod = jnp.concatenate([d_sh_m, d_sc_m, d_gt_m, d_sh_f, d_sc_f, d_gt_f], axis=1)
    return grad_x, loss_acc, dmod, d_conv8, d_al8, d_dt8, d_dn_g, d_q_g, d_kv_g, d_ln1_g, d_ln1_b, d_ln2_g, d_ln2_b
```

```python
import functools
import math

import jax
import jax.numpy as jnp
from jax import lax
from jax.experimental import pallas as pl
from jax.experimental.pallas import tpu as pltpu

F32 = jnp.float32
BF = jnp.bfloat16
HI = lax.Precision.HIGHEST

N_DEV = 8
DN_HEADS = 4
DN_DK = 128
DN_DV = 128
CONV_K = 4
CHUNK = 64
MLA_HEADS = 4
QK_NOPE = 128
QK_ROPE = 64
V_HEAD = 128
Q_LORA = 512
KV_LORA = 256
ROPE_THETA = 10000.0
DEPTH = 1
DEEPNORM_ALPHA = (2.0 * DEPTH) ** 0.25
LANE = 128
CONV_HALO = 8
CONV_ROWS, CONV_COLS = 64, 256

DN_QK = DN_HEADS * DN_DK
DN_VW = DN_HEADS * DN_DV
DN_CONV_CH = 2 * DN_QK + DN_VW
MLA_VW = MLA_HEADS * V_HEAD
MLA_QCAT = QK_NOPE + LANE
N_IN = DN_CONV_CH + DN_VW + 2 * DN_HEADS + Q_LORA + KV_LORA + QK_ROPE
P_QKV = 0
P_Z = DN_CONV_CH
P_CQ = P_Z + DN_VW
P_CKV = P_CQ + Q_LORA
P_BA = P_CKV + KV_LORA
P_KR = P_BA + LANE
N_INP = P_KR + LANE

ADAM_LR = 0.001
ADAM_B1 = 0.9
ADAM_B2 = 0.999
ADAM_EPS = 1e-08
ADAM_WD = 0.01
ADAM_STEP = 10

NN = (((1,), (0,)), ((), ()))
NT = (((1,), (1,)), ((), ()))
TN = (((0,), (0,)), ((), ()))


def _pick(n, prefs):
    for p in prefs:
        if n % p == 0:
            return p
    return n


def _full(shape):
    return pl.BlockSpec(shape, lambda *_: (0,) * len(shape))


def _dot(a, b, dims=NN):
    return lax.dot_general(a, b, dims, preferred_element_type=F32)


def _doth(a, b, dims=NN):
    return lax.dot_general(a, b, dims, precision=HI, preferred_element_type=F32)


@jax.custom_vjp
def _mmb(a, b):
    return _dot(a.astype(BF), b.astype(BF), NN)


def _mmb_fwd(a, b):
    return _mmb(a, b), (a, b)


def _mmb_bwd(res, g):
    a, b = res
    gb = g.astype(BF)
    return (_dot(gb, b.astype(BF), NT).astype(a.dtype), _dot(a.astype(BF), gb, TN).astype(b.dtype))


_mmb.defvjp(_mmb_fwd, _mmb_bwd)


@jax.custom_vjp
def _mmb_nt(a, b):
    return _dot(a.astype(BF), b.astype(BF), NT)


def _mmb_nt_fwd(a, b):
    return _mmb_nt(a, b), (a, b)


def _mmb_nt_bwd(res, g):
    a, b = res
    gb = g.astype(BF)
    return (_dot(gb, b.astype(BF), NN).astype(a.dtype), _dot(gb, a.astype(BF), TN).astype(b.dtype))


_mmb_nt.defvjp(_mmb_nt_fwd, _mmb_nt_bwd)


@jax.custom_vjp
def _mmb_tn(a, b):
    return _dot(a.astype(BF), b.astype(BF), TN)


def _mmb_tn_fwd(a, b):
    return _mmb_tn(a, b), (a, b)


def _mmb_tn_bwd(res, g):
    a, b = res
    gb = g.astype(BF)
    return (_dot(b.astype(BF), gb, NT).astype(a.dtype), _dot(a.astype(BF), gb, NN).astype(b.dtype))


_mmb_tn.defvjp(_mmb_tn_fwd, _mmb_tn_bwd)


def _sigmoid(x):
    return 0.5 * (jnp.tanh(0.5 * x) + 1.0)


def _silu(x):
    return x * _sigmoid(x)


def _softplus(x):
    return jnp.maximum(x, 0.0) + jnp.log(1.0 + jnp.exp(-jnp.abs(x)))


def _layernorm(x, g, b, eps=1e-5):
    mu = jnp.mean(x, axis=-1, keepdims=True)
    xc = x - mu
    var = jnp.mean(xc * xc, axis=-1, keepdims=True)
    return xc * lax.rsqrt(var + eps) * g + b


def _rmsnorm(x, g, eps=1e-6):
    return x * lax.rsqrt(jnp.mean(x * x, axis=-1, keepdims=True) + eps) * g


def _l2norm(x, eps=1e-6):
    return x * lax.rsqrt(jnp.sum(x * x, axis=-1, keepdims=True) + eps)


def _rowwise(name, fn, rows, vecs, out_rows, out_accs, tm):
    rows = [r if isinstance(r, tuple) else (r, r.shape[1], 0) for r in rows]
    t = rows[0][0].shape[0]
    tm = min(tm, t)
    assert t % tm == 0
    nr, nv, no = len(rows), len(vecs), len(out_rows)

    def body(*refs):
        ins = [r[...] for r in refs[:nr + nv]]
        outs = fn(*ins)
        outs = outs if isinstance(outs, (tuple, list)) else (outs,)
        o_rows = refs[nr + nv:nr + nv + no]
        o_accs = refs[nr + nv + no:]
        for o, val in zip(o_rows, outs[:no]):
            o[...] = val.astype(o.dtype)
        if o_accs:
            @pl.when(pl.program_id(0) == 0)
            def _():
                for o in o_accs:
                    o[...] = jnp.zeros_like(o)
            for o, val in zip(o_accs, outs[no:]):
                o[...] += val

    in_specs = [pl.BlockSpec((tm, w), functools.partial(lambda i, j: (i, j), j=j)) for (_, w, j) in rows]
    in_specs += [_full(v.shape) for v in vecs]
    out_specs = [pl.BlockSpec((tm, w), lambda i: (i, 0)) for (w, _) in out_rows]
    out_specs += [_full(s) for s in out_accs]
    out_shape = [jax.ShapeDtypeStruct((t, w), d) for (w, d) in out_rows]
    out_shape += [jax.ShapeDtypeStruct(s, F32) for s in out_accs]
    res = pl.pallas_call(
        body, grid=(t // tm,), in_specs=in_specs, out_specs=out_specs, out_shape=out_shape, name=name,
        compiler_params=pltpu.CompilerParams(dimension_semantics=("arbitrary",)),
    )(*[r[0] for r in rows], *vecs)
    return res


def _matmul(a, b, mode, name, out_dtype=F32):
    if mode == "nn":
        (m, k), n = a.shape, b.shape[1]
    elif mode == "nt":
        (m, k), n = a.shape, b.shape[0]
    else:
        (k, m), n = a.shape, b.shape[1]
    tm, tn, tk = _matmul_tiles(m, n, k, a.dtype.itemsize, b.dtype.itemsize, jnp.dtype(out_dtype).itemsize)
    nk = k // tk
    dims = {"nn": NN, "nt": NT, "tn": TN}[mode]

    def body(a_ref, b_ref, o_ref, *acc):
        part = _dot(a_ref[...].astype(BF), b_ref[...].astype(BF), dims)
        if nk == 1:
            o_ref[...] = part.astype(o_ref.dtype)
            return
        (acc_ref,) = acc
        kk = pl.program_id(2)

        @pl.when(kk == 0)
        def _():
            acc_ref[...] = part

        @pl.when(kk > 0)
        def _():
            acc_ref[...] += part

        @pl.when(kk == nk - 1)
        def _():
            o_ref[...] = acc_ref[...].astype(o_ref.dtype)

    a_spec = pl.BlockSpec((tk, tm), lambda i, j, kk: (kk, i)) if mode == "tn" else pl.BlockSpec((tm, tk), lambda i, j, kk: (i, kk))
    b_spec = pl.BlockSpec((tn, tk), lambda i, j, kk: (j, kk)) if mode == "nt" else pl.BlockSpec((tk, tn), lambda i, j, kk: (kk, j))
    return pl.pallas_call(
        body, grid=(m // tm, n // tn, nk), in_specs=[a_spec, b_spec],
        out_specs=pl.BlockSpec((tm, tn), lambda i, j, kk: (i, j)),
        out_shape=jax.ShapeDtypeStruct((m, n), out_dtype),
        scratch_shapes=[pltpu.VMEM((tm, tn), F32)] if nk > 1 else [], name=name,
        compiler_params=pltpu.CompilerParams(dimension_semantics=("parallel", "parallel", "arbitrary")),
    )(a, b)


def _lane_tile(n, cap):
    return max([n // s for s in range(1, n // LANE + 1) if n % s == 0 and (n // s) % LANE == 0 and n // s <= cap] or [n])


def _ffn_in(h, w_gate, w_up):
    m, k = h.shape
    f = w_gate.shape[0]
    tm, tn = _pick(m, (512, 256, 128)), _lane_tile(f, 1408)

    def body(h_ref, wg_ref, wu_ref, act_ref, g_ref, u_ref):
        hh = h_ref[...]
        g = _dot(hh, wg_ref[...], NT)
        u = _dot(hh, wu_ref[...], NT)
        act_ref[...] = (_silu(g) * u).astype(act_ref.dtype)
        g_ref[...] = g.astype(g_ref.dtype)
        u_ref[...] = u.astype(u_ref.dtype)

    w_spec = pl.BlockSpec((tn, k), lambda i, j: (j, 0))
    o_spec = pl.BlockSpec((tm, tn), lambda i, j: (i, j))
    return pl.pallas_call(
        body, grid=(m // tm, f // tn), in_specs=[pl.BlockSpec((tm, k), lambda i, j: (i, 0)), w_spec, w_spec],
        out_specs=[o_spec] * 3, out_shape=[jax.ShapeDtypeStruct((m, f), BF)] * 3, name="ffn_in",
        compiler_params=pltpu.CompilerParams(dimension_semantics=("parallel", "parallel")),
    )(h, w_gate, w_up)


def _ffn_act_bwd(dff, w_down, gate, up):
    m, k = dff.shape
    f = w_down.shape[0]
    tm, tn = _pick(m, (512, 256, 128)), _lane_tile(f, 1408)

    def body(d_ref, w_ref, g_ref, u_ref, dg_ref, du_ref):
        da = _dot(d_ref[...], w_ref[...], NT)
        g = g_ref[...].astype(F32)
        sg = _sigmoid(g)
        dg_ref[...] = (da * u_ref[...].astype(F32) * (sg * (1.0 + g * (1.0 - sg)))).astype(dg_ref.dtype)
        du_ref[...] = (da * (g * sg)).astype(du_ref.dtype)

    o_spec = pl.BlockSpec((tm, tn), lambda i, j: (i, j))
    return pl.pallas_call(
        body, grid=(m // tm, f // tn),
        in_specs=[pl.BlockSpec((tm, k), lambda i, j: (i, 0)), pl.BlockSpec((tn, k), lambda i, j: (j, 0)), o_spec, o_spec],
        out_specs=[o_spec] * 2, out_shape=[jax.ShapeDtypeStruct((m, f), BF)] * 2, name="d_ffn_act",
        compiler_params=pltpu.CompilerParams(dimension_semantics=("parallel", "parallel")),
    )(dff, w_down, gate, up)


def _matmul2_nn(a1, b1, a2, b2, name):
    m, k = a1.shape
    n = b1.shape[1]
    tm, tn = _pick(m, (512, 256, 128)), _pick(n, (512, 256, 128))

    def body(a1_ref, b1_ref, a2_ref, b2_ref, o_ref):
        o_ref[...] = _dot(a1_ref[...], b1_ref[...]) + _dot(a2_ref[...], b2_ref[...])

    a_spec = pl.BlockSpec((tm, k), lambda i, j: (i, 0))
    b_spec = pl.BlockSpec((k, tn), lambda i, j: (0, j))
    return pl.pallas_call(
        body, grid=(m // tm, n // tn), in_specs=[a_spec, b_spec, a_spec, b_spec],
        out_specs=pl.BlockSpec((tm, tn), lambda i, j: (i, j)), out_shape=jax.ShapeDtypeStruct((m, n), F32), name=name,
        compiler_params=pltpu.CompilerParams(dimension_semantics=("parallel", "parallel")),
    )(a1, b1, a2, b2)


MATMUL_VMEM_BUDGET = 28 * 1024 * 1024


def _matmul_tiles(m, n, k, a_bytes, b_bytes, o_bytes):
    def divisors(x, cap):
        return sorted({x // s for s in range(1, 65) if x % s == 0 and (x // s) % LANE == 0 and x // s <= cap}, reverse=True) or [x]

    for tk in divisors(k, k):
        best = None
        for tm in divisors(m, 1024):
            for tn in divisors(n, 2048):
                need = 2 * (tm * tk * a_bytes + tk * tn * b_bytes + tm * tn * o_bytes) + (tm * tn * 4 if tk < k else 0)
                if need <= MATMUL_VMEM_BUDGET and tm * tn >= 512 * 512 and (best is None or tm * tn > best[0] * best[1]):
                    best = (tm, tn)
        if best:
            return best[0], best[1], tk
    return _pick(m, (512, 256, 128)), _pick(n, (512, 256, 128)), _pick(k, (512, 256, 128))


def _exchange(xs, name, scatter):
    n = len(xs)
    npeer = N_DEV - 1

    def body(*refs):
        x_refs, o_refs = refs[:n], refs[n:2 * n]
        send_sems, recv_sems, local_sems = refs[2 * n:]
        mx, my, mc = lax.axis_index("x"), lax.axis_index("y"), lax.axis_index("c")
        me = 4 * mx + 2 * my + mc
        src_me = [x.at[me] if scatter else x for x in x_refs]
        mine = [pltpu.make_async_copy(src_me[a], o_refs[a].at[me], local_sems.at[a]) for a in range(n)]
        for cp in mine:
            cp.start()
        copies = []
        for k in range(1, N_DEV):
            px, py, pc = mx ^ (k >> 2), my ^ ((k >> 1) & 1), mc ^ (k & 1)
            peer = 4 * px + 2 * py + pc
            for a in range(n):
                cp = pltpu.make_async_remote_copy(
                    src_ref=x_refs[a].at[peer] if scatter else x_refs[a], dst_ref=o_refs[a].at[me],
                    send_sem=send_sems.at[a * npeer + k - 1], recv_sem=recv_sems.at[a * npeer + k - 1],
                    device_id=(px, py, pc), device_id_type=pl.DeviceIdType.MESH)
                cp.start()
                copies.append((cp, a, k, peer))
        for cp, a, k, peer in copies:
            pltpu.make_async_remote_copy(
                src_ref=src_me[a], dst_ref=o_refs[a].at[peer], send_sem=send_sems.at[a * npeer + k - 1],
                recv_sem=recv_sems.at[a * npeer + k - 1], device_id=(mx, my, mc),
                device_id_type=pl.DeviceIdType.MESH).wait_recv()
        for cp, _, _, _ in copies:
            cp.wait_send()
        for cp in mine:
            cp.wait()

    return pl.pallas_call(
        body, out_shape=[jax.ShapeDtypeStruct((N_DEV,) + x.shape[-2:], x.dtype) for x in xs],
        in_specs=[pl.BlockSpec(memory_space=pl.ANY)] * n, out_specs=[pl.BlockSpec(memory_space=pl.ANY)] * n,
        scratch_shapes=[pltpu.SemaphoreType.DMA((n * npeer,)), pltpu.SemaphoreType.DMA((n * npeer,)),
                        pltpu.SemaphoreType.DMA((n,))],
        name=name,
    )(*xs)


def _peer_of(k):
    mx, my, mc = lax.axis_index("x"), lax.axis_index("y"), lax.axis_index("c")
    px, py, pc = mx ^ (k >> 2), my ^ ((k >> 1) & 1), mc ^ (k & 1)
    return (px, py, pc), 4 * px + 2 * py + pc


def _exchange_start(xs, name, scatter):
    n = len(xs)
    npeer = N_DEV - 1

    def body(*refs):
        x_refs, land_refs = refs[:n], refs[n:2 * n]
        send_sems, recv_sems, token = refs[2 * n], refs[2 * n + 1], refs[-1]
        me = 4 * lax.axis_index("x") + 2 * lax.axis_index("y") + lax.axis_index("c")
        for k in range(1, N_DEV):
            dev, peer = _peer_of(k)
            for a in range(n):
                pltpu.make_async_remote_copy(
                    src_ref=x_refs[a].at[peer] if scatter else x_refs[a], dst_ref=land_refs[a].at[me],
                    send_sem=send_sems.at[a * npeer + k - 1], recv_sem=recv_sems.at[a * npeer + k - 1],
                    device_id=dev, device_id_type=pl.DeviceIdType.MESH).start()
        token[...] = jnp.zeros_like(token)

    hbm = pl.BlockSpec(memory_space=pltpu.HBM)
    sem = pl.BlockSpec(memory_space=pltpu.SEMAPHORE)
    lands = [pltpu.with_memory_space_constraint(lax.empty((N_DEV,) + x.shape[-2:], x.dtype), pltpu.HBM) for x in xs]
    srcs = [pltpu.with_memory_space_constraint(x, pltpu.HBM) for x in xs]
    outs = pl.pallas_call(
        body, name=name,
        out_shape=(pltpu.SemaphoreType.DMA((n * npeer,)), pltpu.SemaphoreType.DMA((n * npeer,)),
                   *[pltpu.HBM(x.shape, x.dtype) for x in srcs], *[pltpu.HBM(z.shape, z.dtype) for z in lands],
                   jax.ShapeDtypeStruct((8, LANE), F32)),
        in_specs=[hbm] * (2 * n), out_specs=(sem, sem, *[hbm] * (2 * n), pl.BlockSpec(memory_space=pltpu.VMEM)),
        input_output_aliases={i: 2 + i for i in range(2 * n)},
        compiler_params=pltpu.CompilerParams(has_side_effects=pltpu.SideEffectType.DATAFLOW_SIDE_EFFECTING),
    )(*srcs, *lands)
    return (outs[0], outs[1], list(outs[2:2 + n]), list(outs[2 + n:2 + 2 * n])), outs[-1][0:1, 0:1]


def _exchange_wait(started, after, name, scatter):
    send_sems, recv_sems, srcs, lands = started
    n = len(srcs)
    npeer = N_DEV - 1

    def body(*refs):
        x_refs, land_refs = refs[:n], refs[n:2 * n]
        send_sems, recv_sems = refs[2 * n], refs[2 * n + 1]
        mx, my, mc = lax.axis_index("x"), lax.axis_index("y"), lax.axis_index("c")
        me = 4 * mx + 2 * my + mc
        for k in range(1, N_DEV):
            _, peer = _peer_of(k)
            for a in range(n):
                src = x_refs[a].at[me] if scatter else x_refs[a]
                cp = pltpu.make_async_remote_copy(
                    src_ref=src, dst_ref=land_refs[a].at[peer], send_sem=send_sems.at[a * npeer + k - 1],
                    recv_sem=recv_sems.at[a * npeer + k - 1], device_id=(mx, my, mc), device_id_type=pl.DeviceIdType.MESH)
                cp.wait_send()
                cp.wait_recv()

    hbm = pl.BlockSpec(memory_space=pltpu.HBM)
    sem = pl.BlockSpec(memory_space=pltpu.SEMAPHORE)
    outs = pl.pallas_call(
        body, name=name,
        out_shape=(*[pltpu.HBM(x.shape, x.dtype) for x in srcs], *[pltpu.HBM(z.shape, z.dtype) for z in lands]),
        in_specs=[hbm] * (2 * n) + [sem, sem, pl.BlockSpec(memory_space=pl.ANY)], out_specs=tuple([hbm] * (2 * n)),
        input_output_aliases={i: i for i in range(2 * n)},
        compiler_params=pltpu.CompilerParams(has_side_effects=pltpu.SideEffectType.DATAFLOW_SIDE_EFFECTING),
    )(*srcs, *lands, send_sems, recv_sems, after)
    me = 4 * lax.axis_index("x") + 2 * lax.axis_index("y") + lax.axis_index("c")
    full = []
    for x, land in zip(outs[:n], outs[n:]):
        own = lax.dynamic_slice(x, (me, 0, 0), (1,) + x.shape[1:]) if scatter else x[None]
        full.append(lax.dynamic_update_slice(land, own, (me, 0, 0)))
    return full


def _sum_slots(x, name):
    _, r, c = x.shape
    tr = _pick(r, (512, 256, 128, 64, 32, 16))

    def body(x_ref, o_ref):
        acc = x_ref[0].astype(F32)
        for s in range(1, N_DEV):
            acc = acc + x_ref[s].astype(F32)
        o_ref[...] = acc

    return pl.pallas_call(
        body, grid=(r // tr,), in_specs=[pl.BlockSpec((N_DEV, tr, c), lambda i: (0, i, 0))],
        out_specs=pl.BlockSpec((tr, c), lambda i: (i, 0)), out_shape=jax.ShapeDtypeStruct((r, c), F32), name=name,
        compiler_params=pltpu.CompilerParams(dimension_semantics=("arbitrary",)),
    )(x)


def _mod_fwd(c_all, w_ada, b_ada_mine):
    def body(c_ref, w_ref, b_ref, o_ref):
        o_ref[...] = _doth(_silu(c_ref[...]), w_ref[...]) + b_ref[...]

    return pl.pallas_call(body, out_shape=jax.ShapeDtypeStruct((c_all.shape[0], w_ada.shape[1]), F32), name="mod_fwd")(c_all, w_ada, b_ada_mine)


def _mod_bwd(c_all_t, dmod_mine):
    def body(ct_ref, d_ref, o_ref):
        s = _silu(ct_ref[...])
        acc = s[:, 0:1] * d_ref[pl.ds(0, 1), :]
        for b in range(1, N_DEV):
            acc = acc + s[:, b:b + 1] * d_ref[pl.ds(b, 1), :]
        o_ref[...] = acc

    return pl.pallas_call(body, out_shape=jax.ShapeDtypeStruct((c_all_t.shape[0], dmod_mine.shape[1]), F32), name="mod_bwd")(c_all_t, dmod_mine)


def _conv_fwd(proj, conv_w8, tm):
    t = proj.shape[0]
    ch = DN_CONV_CH

    def body(x_ref, w_ref, o_ref, buf):
        @pl.when(pl.program_id(0) == 0)
        def _():
            buf[pl.ds(0, CONV_HALO), :] = jnp.zeros((CONV_HALO, ch), F32)

        buf[pl.ds(CONV_HALO, tm), :] = x_ref[...]
        for c0 in range(0, ch, CONV_COLS):
            cols = pl.ds(c0, CONV_COLS)
            w = [w_ref[pl.ds(j, 1), cols] for j in range(CONV_K)]
            for r0 in range(0, tm, CONV_ROWS):
                acc = buf[pl.ds(r0 + CONV_HALO - (CONV_K - 1), CONV_ROWS), cols] * w[0]
                for j in range(1, CONV_K):
                    acc = acc + buf[pl.ds(r0 + CONV_HALO - (CONV_K - 1) + j, CONV_ROWS), cols] * w[j]
                o_ref[pl.ds(r0, CONV_ROWS), cols] = _silu(acc)
        buf[pl.ds(0, CONV_HALO), :] = buf[pl.ds(tm, CONV_HALO), :]

    return pl.pallas_call(
        body, grid=(t // tm,), in_specs=[pl.BlockSpec((tm, ch), lambda i: (i, 0)), _full(conv_w8.shape)],
        out_specs=pl.BlockSpec((tm, ch), lambda i: (i, 0)), out_shape=jax.ShapeDtypeStruct((t, ch), F32),
        scratch_shapes=[pltpu.VMEM((tm + CONV_HALO, ch), F32)], name="conv_fwd",
        compiler_params=pltpu.CompilerParams(dimension_semantics=("arbitrary",)),
    )(proj, conv_w8)


def _conv_bwd(proj, conv_w8, dact, tm):
    t = proj.shape[0]
    ch = DN_CONV_CH
    nt = t // tm
    hb = tm // CONV_HALO

    def body(x_ref, xp_ref, w_ref, dy_ref, dx_ref, dw_ref, xbuf, dbuf):
        step = pl.program_id(0)

        @pl.when(step == 0)
        def _():
            dbuf[pl.ds(tm, CONV_HALO), :] = jnp.zeros((CONV_HALO, ch), F32)
            dw_ref[...] = jnp.zeros_like(dw_ref)

        first = step == nt - 1
        xbuf[pl.ds(0, CONV_HALO), :] = jnp.where(first, 0.0, xp_ref[...])
        xbuf[pl.ds(CONV_HALO, tm), :] = x_ref[...]
        for c0 in range(0, ch, CONV_COLS):
            cols = pl.ds(c0, CONV_COLS)
            w = [w_ref[pl.ds(j, 1), cols] for j in range(CONV_K)]
            dw = [jnp.zeros((1, CONV_COLS), F32) for _ in range(CONV_K)]
            for r0 in range(0, tm, CONV_ROWS):
                xs = [xbuf[pl.ds(r0 + CONV_HALO - (CONV_K - 1) + j, CONV_ROWS), cols] for j in range(CONV_K)]
                pre = xs[0] * w[0]
                for j in range(1, CONV_K):
                    pre = pre + xs[j] * w[j]
                sg = _sigmoid(pre)
                dpre = dy_ref[pl.ds(r0, CONV_ROWS), cols] * (sg * (1.0 + pre * (1.0 - sg)))
                dbuf[pl.ds(r0, CONV_ROWS), cols] = dpre
                dw = [dw[j] + jnp.sum(dpre * xs[j], axis=0, keepdims=True) for j in range(CONV_K)]
            for j in range(CONV_K):
                dw_ref[pl.ds(j, 1), cols] += dw[j]
            for r0 in range(0, tm, CONV_ROWS):
                dx = dbuf[pl.ds(r0 + CONV_K - 1, CONV_ROWS), cols] * w[0]
                for j in range(1, CONV_K):
                    dx = dx + dbuf[pl.ds(r0 + CONV_K - 1 - j, CONV_ROWS), cols] * w[j]
                dx_ref[pl.ds(r0, CONV_ROWS), cols] = dx.astype(dx_ref.dtype)
        dbuf[pl.ds(tm, CONV_HALO), :] = dbuf[pl.ds(0, CONV_HALO), :]

    rev = lambda i: (nt - 1 - i, 0)
    prev = lambda i: (jnp.maximum((nt - 1 - i) * hb - 1, 0), 0)
    return pl.pallas_call(
        body, grid=(nt,),
        in_specs=[pl.BlockSpec((tm, ch), rev), pl.BlockSpec((CONV_HALO, ch), prev), _full(conv_w8.shape),
                  pl.BlockSpec((tm, ch), rev)],
        out_specs=[pl.BlockSpec((tm, ch), rev), _full(conv_w8.shape)],
        out_shape=[jax.ShapeDtypeStruct((t, ch), BF), jax.ShapeDtypeStruct(conv_w8.shape, F32)],
        scratch_shapes=[pltpu.VMEM((tm + CONV_HALO, ch), F32), pltpu.VMEM((tm + CONV_HALO, ch), F32)], name="conv_bwd",
        compiler_params=pltpu.CompilerParams(dimension_semantics=("arbitrary",)),
    )(proj, proj, conv_w8, dact)


BNN = (((2,), (1,)), ((0,), (0,)))
BNT = (((2,), (2,)), ((0,), (0,)))
BTN = (((1,), (1,)), ((0,), (0,)))


def _bdot(a, b, dims, precision=None):
    return lax.dot_general(a, b, dims, precision=precision, preferred_element_type=F32)


@jax.custom_vjp
def _bmmb_nt(a, b):
    return _bdot(a.astype(BF), b.astype(BF), BNT)


def _bmmb_nt_fwd(a, b):
    return _bmmb_nt(a, b), (a, b)


def _bmmb_nt_bwd(res, g):
    a, b = res
    gb = g.astype(BF)
    return _bdot(gb, b.astype(BF), BNN), _bdot(gb, a.astype(BF), BTN)


_bmmb_nt.defvjp(_bmmb_nt_fwd, _bmmb_nt_bwd)


@jax.custom_vjp
def _unit_lower_solve(a, r):
    return _unit_lower_solve_fwd(a, r)[0]


def _unit_lower_solve_fwd(a, r):
    c = a.shape[-1]
    ri = lax.broadcasted_iota(jnp.int32, a.shape, 1)
    ci = lax.broadcasted_iota(jnp.int32, a.shape, 2)
    xm = -a
    inv = (ri == ci).astype(F32) + xm
    for _ in range(int(math.log2(c)) - 1):
        xm = _bdot(xm, xm, BNN, HI)
        inv = inv + _bdot(inv, xm, BNN, HI)
    x = _bdot(inv, r, BNN, HI)
    return x, (inv, x)


def _unit_lower_solve_bwd(res, g):
    inv, x = res
    dr = _bdot(inv, g, BTN, HI)
    return -_bdot(dr, x, BNT, HI), dr


_unit_lower_solve.defvjp(_unit_lower_solve_fwd, _unit_lower_solve_bwd)


def _gdn_intra(qkv, ba, al8, dt8):
    tm = qkv.shape[0]
    nb = tm // CHUNK
    bsz = DN_HEADS * nb

    def heads(x0):
        return jnp.concatenate([qkv[:, x0 + h * LANE:x0 + (h + 1) * LANE].reshape(nb, CHUNK, LANE) for h in range(DN_HEADS)], axis=0)

    def spread(c0):
        return jnp.concatenate([jnp.broadcast_to(ba[:, c0 + h:c0 + h + 1], (tm, LANE)).reshape(nb, CHUNK, LANE)
                                for h in range(DN_HEADS)], axis=0)

    def per_head(v8):
        return jnp.concatenate([jnp.broadcast_to(v8[0:1, h:h + 1].reshape(1, 1, 1), (nb, 1, LANE)) for h in range(DN_HEADS)], axis=0)

    ri = lax.broadcasted_iota(jnp.int32, (bsz, CHUNK, CHUNK), 1)
    ci = lax.broadcasted_iota(jnp.int32, (bsz, CHUNK, CHUNK), 2)
    incl = ri >= ci
    strict = ri > ci

    q = _l2norm(heads(0)) * (DN_DK ** -0.5)
    k = _l2norm(heads(DN_QK))
    va = heads(2 * DN_QK)
    beta = _sigmoid(spread(0))
    g = -jnp.exp(per_head(al8)) * _softplus(spread(DN_HEADS) + per_head(dt8))
    gc = _bdot(incl.astype(F32), g, BNN, HI)
    g_last = jnp.sum(g, axis=1, keepdims=True)
    gcol = gc[:, :, :CHUNK]
    diff = gcol - jnp.swapaxes(gcol, 1, 2)
    decay = jnp.where(incl, jnp.exp(jnp.where(incl, diff, 0.0)), 0.0)
    kb = k * beta
    a_mat = jnp.where(strict, _bmmb_nt(kb, k) * decay, 0.0)
    egc = jnp.exp(gc)
    wu = _unit_lower_solve(a_mat, jnp.concatenate([kb * egc, va * beta], axis=2))
    attn = jnp.where(incl, _bmmb_nt(q, k) * decay, 0.0)

    def unheads(x):
        return jnp.concatenate([x[h * nb:(h + 1) * nb].reshape(tm, LANE) for h in range(DN_HEADS)], axis=1)

    return (unheads(wu[:, :, :DN_DK]), unheads(wu[:, :, DN_DK:]), unheads(q * egc), unheads(k * jnp.exp(g_last - gc)),
            attn.reshape(DN_HEADS, tm, CHUNK), unheads(jnp.broadcast_to(g_last, (bsz, CHUNK, LANE))))


def _gdn_scan_step(w, u, qg, kd, att, gl, s):
    v_new = u - _mmb(w, s)
    o = _mmb(qg, s) + _mmb(att, v_new)
    return o, s * jnp.exp(gl) + _mmb_tn(kd, v_new)


def _gdn_intra_specs(t, tm, dts):
    nb = tm // CHUNK
    specs = [pl.BlockSpec((tm, DN_VW), lambda i: (i, 0))] * 4
    specs += [pl.BlockSpec((DN_HEADS, tm, CHUNK), lambda i: (0, i, 0)), pl.BlockSpec((tm, DN_VW), lambda i: (i, 0))]
    shapes = [jax.ShapeDtypeStruct((t, DN_VW), dts[i]) for i in range(4)]
    shapes += [jax.ShapeDtypeStruct((DN_HEADS, t, CHUNK), dts[4]), jax.ShapeDtypeStruct((t, DN_VW), dts[5])]
    return specs, shapes


def _gdn_intra_fwd(qkv, proj, al8, dt8, tm):
    t = qkv.shape[0]

    def body(qkv_ref, ba_ref, al_ref, dt_ref, *outs):
        for o, val in zip(outs, _gdn_intra(qkv_ref[...], ba_ref[...], al_ref[...], dt_ref[...])):
            o[...] = val.astype(o.dtype)

    specs, shapes = _gdn_intra_specs(t, tm, (BF, F32, BF, BF, BF, F32))
    return pl.pallas_call(
        body, grid=(t // tm,),
        in_specs=[pl.BlockSpec((tm, DN_CONV_CH), lambda i: (i, 0)), pl.BlockSpec((tm, LANE), lambda i: (i, P_BA // LANE)),
                  _full(al8.shape), _full(dt8.shape)],
        out_specs=specs, out_shape=shapes, name="gdn_intra_fwd",
        compiler_params=pltpu.CompilerParams(dimension_semantics=("parallel",)),
    )(qkv, proj, al8, dt8)


def _gdn_intra_bwd(qkv, proj, al8, dt8, cts, tm):
    t = qkv.shape[0]

    def body(qkv_ref, ba_ref, al_ref, dt_ref, *refs):
        ct_refs, (dqkv_ref, dba_ref, dal_ref, ddt_ref) = refs[:6], refs[6:]

        @pl.when(pl.program_id(0) == 0)
        def _():
            dal_ref[...] = jnp.zeros_like(dal_ref)
            ddt_ref[...] = jnp.zeros_like(ddt_ref)

        _, vjp = jax.vjp(_gdn_intra, qkv_ref[...], ba_ref[...], al_ref[...], dt_ref[...])
        dqkv, dba, dal, ddt = vjp(tuple(r[...] for r in ct_refs))
        dqkv_ref[...] = dqkv
        dba_ref[...] = dba.astype(dba_ref.dtype)
        dal_ref[...] += dal
        ddt_ref[...] += ddt

    specs, _ = _gdn_intra_specs(t, tm, (F32,) * 6)
    return pl.pallas_call(
        body, grid=(t // tm,),
        in_specs=[pl.BlockSpec((tm, DN_CONV_CH), lambda i: (i, 0)), pl.BlockSpec((tm, LANE), lambda i: (i, P_BA // LANE)),
                  _full(al8.shape), _full(dt8.shape)] + specs,
        out_specs=[pl.BlockSpec((tm, DN_CONV_CH), lambda i: (i, 0)), pl.BlockSpec((tm, LANE), lambda i: (i, 0)),
                   _full(al8.shape), _full(dt8.shape)],
        out_shape=[jax.ShapeDtypeStruct((t, DN_CONV_CH), F32), jax.ShapeDtypeStruct((t, LANE), BF),
                   jax.ShapeDtypeStruct(al8.shape, F32), jax.ShapeDtypeStruct(dt8.shape, F32)],
        name="gdn_intra_bwd", compiler_params=pltpu.CompilerParams(dimension_semantics=("arbitrary",)),
    )(qkv, proj, al8, dt8, *cts)


def _gdn_scan_fwd(intra, tm):
    t = intra[0].shape[0]
    nb = tm // CHUNK
    nc = t // CHUNK

    def body(w_ref, u_ref, qg_ref, kd_ref, att_ref, gl_ref, o_ref, ss_ref, s_scr):
        @pl.when(pl.program_id(0) == 0)
        def _():
            s_scr[...] = jnp.zeros_like(s_scr)

        for cc in range(nb):
            rows = pl.ds(cc * CHUNK, CHUNK)
            for h in range(DN_HEADS):
                cols = pl.ds(h * DN_DV, DN_DV)
                s_prev = s_scr[h]
                ss_ref[cc, h] = s_prev
                o, s_new = _gdn_scan_step(w_ref[rows, cols], u_ref[rows, cols], qg_ref[rows, cols], kd_ref[rows, cols],
                                          att_ref[h, rows, :], gl_ref[pl.ds(cc * CHUNK, 1), cols], s_prev)
                o_ref[rows, cols] = o
                s_scr[h] = s_new

    specs, _ = _gdn_intra_specs(t, tm, (F32,) * 6)
    return pl.pallas_call(
        body, grid=(t // tm,), in_specs=specs,
        out_specs=[pl.BlockSpec((tm, DN_VW), lambda i: (i, 0)),
                   pl.BlockSpec((nb, DN_HEADS, DN_DK, DN_DV), lambda i: (i, 0, 0, 0))],
        out_shape=[jax.ShapeDtypeStruct((t, DN_VW), F32), jax.ShapeDtypeStruct((nc, DN_HEADS, DN_DK, DN_DV), F32)],
        scratch_shapes=[pltpu.VMEM((DN_HEADS, DN_DK, DN_DV), F32)], name="gdn_scan_fwd",
        compiler_params=pltpu.CompilerParams(dimension_semantics=("arbitrary",)),
    )(*intra)


def _gdn_scan_bwd(intra, states, do, tm):
    t = intra[0].shape[0]
    nb = tm // CHUNK
    ng = t // tm

    def body(w_ref, u_ref, qg_ref, kd_ref, att_ref, gl_ref, ss_ref, do_ref,
             dw_ref, du_ref, dqg_ref, dkd_ref, datt_ref, dgl_ref, ds_scr):
        @pl.when(pl.program_id(0) == 0)
        def _():
            ds_scr[...] = jnp.zeros_like(ds_scr)

        for cc in reversed(range(nb)):
            rows = pl.ds(cc * CHUNK, CHUNK)
            for h in range(DN_HEADS):
                cols = pl.ds(h * DN_DV, DN_DV)
                f32 = lambda r: r[rows, cols].astype(F32)
                _, vjp = jax.vjp(_gdn_scan_step, f32(w_ref), u_ref[rows, cols], f32(qg_ref), f32(kd_ref),
                                 att_ref[h, rows, :].astype(F32), gl_ref[pl.ds(cc * CHUNK, 1), cols], ss_ref[cc, h])
                dw, du, dqg, dkd, datt, dgl, ds_prev = vjp((do_ref[rows, cols], ds_scr[h]))
                dw_ref[rows, cols] = dw
                du_ref[rows, cols] = du
                dqg_ref[rows, cols] = dqg
                dkd_ref[rows, cols] = dkd
                datt_ref[h, rows, :] = datt
                first_row = lax.broadcasted_iota(jnp.int32, (CHUNK, DN_DV), 0) == 0
                dgl_ref[rows, cols] = jnp.where(first_row, dgl, 0.0)
                ds_scr[h] = ds_prev

    rev = lambda i: (ng - 1 - i, 0)
    rev3 = lambda i: (0, ng - 1 - i, 0)
    row = pl.BlockSpec((tm, DN_VW), rev)
    six = [row] * 4 + [pl.BlockSpec((DN_HEADS, tm, CHUNK), rev3), row]
    _, shapes = _gdn_intra_specs(t, tm, (F32,) * 6)
    return pl.pallas_call(
        body, grid=(ng,),
        in_specs=six + [pl.BlockSpec((nb, DN_HEADS, DN_DK, DN_DV), lambda i: (ng - 1 - i, 0, 0, 0)), row],
        out_specs=six, out_shape=shapes,
        scratch_shapes=[pltpu.VMEM((DN_HEADS, DN_DK, DN_DV), F32)], name="gdn_scan_bwd",
        compiler_params=pltpu.CompilerParams(dimension_semantics=("arbitrary",)),
    )(*intra, states, do)


def _gdn_out(o, z, g):
    parts = []
    for h in range(DN_HEADS):
        sl = slice(h * DN_DV, (h + 1) * DN_DV)
        parts.append(_rmsnorm(o[:, sl], g) * _silu(z[:, sl]))
    return parts


_Q_SCALE = math.log2(math.e) / math.sqrt(QK_NOPE + QK_ROPE)


def _rope_tables(pos, inv_freq2):
    lane = lax.broadcasted_iota(jnp.int32, (1, LANE), 1)
    ang = pos * inv_freq2
    cos = jnp.where(lane < QK_ROPE, jnp.cos(ang), 0.0)
    sin = jnp.where(lane < QK_ROPE // 2, -jnp.sin(ang), jnp.where(lane < QK_ROPE, jnp.sin(ang), 0.0))
    return cos, sin


def _rope_swap():
    ri = lax.broadcasted_iota(jnp.int32, (LANE, LANE), 0)
    ci = lax.broadcasted_iota(jnp.int32, (LANE, LANE), 1)
    half = QK_ROPE // 2
    return (((ci < half) & (ri == ci + half)) | ((ci >= half) & (ci < QK_ROPE) & (ri == ci - half))).astype(F32)


def _mla_prep(cq, ckv, kr, gq, gkv, w_uq, w_ukv, cos, sin, swap):
    rope = lambda u: u * cos + _doth(u, swap) * sin
    q_lin = _mmb_nt(_rmsnorm(cq, gq), w_uq) * _Q_SCALE
    kv_lin = _mmb_nt(_rmsnorm(ckv, gkv), w_ukv)
    k_rope = rope(kr)
    qs, ks, vs = [], [], []
    for h in range(MLA_HEADS):
        qs += [q_lin[:, h * LANE:(h + 1) * LANE], rope(q_lin[:, (MLA_HEADS + h) * LANE:(MLA_HEADS + h + 1) * LANE])]
        ks += [kv_lin[:, 2 * h * LANE:(2 * h + 1) * LANE], k_rope]
        vs += [kv_lin[:, (2 * h + 1) * LANE:(2 * h + 2) * LANE]]
    return qs + ks + vs


def _mla_prep_fwd(proj, pos_col, inv_freq2, gq, gkv, w_uq, w_ukv, tm):
    t = proj.shape[0]
    nq = 2 * MLA_HEADS

    def body(cq_ref, ckv_ref, kr_ref, pos_ref, f_ref, gq_ref, gkv_ref, wq_ref, wkv_ref, q_ref, k_ref, v_ref):
        cos, sin = _rope_tables(pos_ref[...], f_ref[...])
        outs = _mla_prep(cq_ref[...], ckv_ref[...], kr_ref[...], gq_ref[...], gkv_ref[...], wq_ref[...], wkv_ref[...],
                         cos, sin, _rope_swap())
        for i in range(nq):
            q_ref[:, pl.ds(i * LANE, LANE)] = outs[i].astype(q_ref.dtype)
            k_ref[:, pl.ds(i * LANE, LANE)] = outs[nq + i].astype(k_ref.dtype)
        for h in range(MLA_HEADS):
            v_ref[:, pl.ds(h * LANE, LANE)] = outs[2 * nq + h].astype(v_ref.dtype)

    row = lambda w, j: pl.BlockSpec((tm, w), functools.partial(lambda i, j: (i, j), j=j))
    return pl.pallas_call(
        body, grid=(t // tm,),
        in_specs=[row(Q_LORA, P_CQ // Q_LORA), row(KV_LORA, P_CKV // KV_LORA), row(LANE, P_KR // LANE),
                  pl.BlockSpec((tm, 1), lambda i: (i, 0)), _full(inv_freq2.shape), _full(gq.shape), _full(gkv.shape),
                  _full(w_uq.shape), _full(w_ukv.shape)],
        out_specs=[row(nq * LANE, 0), row(nq * LANE, 0), row(MLA_VW, 0)],
        out_shape=[jax.ShapeDtypeStruct((t, nq * LANE), BF), jax.ShapeDtypeStruct((t, nq * LANE), BF),
                   jax.ShapeDtypeStruct((t, MLA_VW), BF)],
        name="mla_prep_fwd", compiler_params=pltpu.CompilerParams(dimension_semantics=("arbitrary",)),
    )(proj, proj, proj, pos_col, inv_freq2, gq, gkv, w_uq, w_ukv)


def _mla_prep_bwd(proj, pos_col, inv_freq2, gq, gkv, w_uq, w_ukv, dq, dk, dv, tm):
    t = proj.shape[0]
    nq = 2 * MLA_HEADS

    def body(cq_ref, ckv_ref, kr_ref, pos_ref, f_ref, gq_ref, gkv_ref, wq_ref, wkv_ref, dq_ref, dk_ref, dv_ref,
             dcq_ref, dckv_ref, dkr_ref, dgq_ref, dgkv_ref, dwq_ref, dwkv_ref):
        @pl.when(pl.program_id(0) == 0)
        def _():
            for o in (dgq_ref, dgkv_ref, dwq_ref, dwkv_ref):
                o[...] = jnp.zeros_like(o)

        cos, sin = _rope_tables(pos_ref[...], f_ref[...])
        f = functools.partial(_mla_prep, cos=cos, sin=sin, swap=_rope_swap())
        _, vjp = jax.vjp(f, cq_ref[...], ckv_ref[...], kr_ref[...], gq_ref[...], gkv_ref[...], wq_ref[...], wkv_ref[...])
        cts = [dq_ref[:, pl.ds(i * LANE, LANE)] for i in range(nq)]
        cts += [dk_ref[:, pl.ds(i * LANE, LANE)] for i in range(nq)]
        cts += [dv_ref[:, pl.ds(h * LANE, LANE)] for h in range(MLA_HEADS)]
        dcq, dckv, dkr, dgq, dgkv, dwq, dwkv = vjp(cts)
        dcq_ref[...] = dcq.astype(dcq_ref.dtype)
        dckv_ref[...] = dckv.astype(dckv_ref.dtype)
        dkr_ref[...] = dkr.astype(dkr_ref.dtype)
        dgq_ref[...] += dgq
        dgkv_ref[...] += dgkv
        dwq_ref[...] += dwq
        dwkv_ref[...] += dwkv

    row = lambda w, j: pl.BlockSpec((tm, w), functools.partial(lambda i, j: (i, j), j=j))
    return pl.pallas_call(
        body, grid=(t // tm,),
        in_specs=[row(Q_LORA, P_CQ // Q_LORA), row(KV_LORA, P_CKV // KV_LORA), row(LANE, P_KR // LANE),
                  pl.BlockSpec((tm, 1), lambda i: (i, 0)), _full(inv_freq2.shape), _full(gq.shape), _full(gkv.shape),
                  _full(w_uq.shape), _full(w_ukv.shape), row(nq * LANE, 0), row(nq * LANE, 0), row(MLA_VW, 0)],
        out_specs=[row(Q_LORA, 0), row(KV_LORA, 0), row(LANE, 0), _full(gq.shape), _full(gkv.shape),
                   _full(w_uq.shape), _full(w_ukv.shape)],
        out_shape=[jax.ShapeDtypeStruct((t, Q_LORA), BF), jax.ShapeDtypeStruct((t, KV_LORA), BF),
                   jax.ShapeDtypeStruct((t, LANE), BF), jax.ShapeDtypeStruct(gq.shape, F32),
                   jax.ShapeDtypeStruct(gkv.shape, F32), jax.ShapeDtypeStruct(w_uq.shape, F32),
                   jax.ShapeDtypeStruct(w_ukv.shape, F32)],
        name="mla_prep_bwd", compiler_params=pltpu.CompilerParams(dimension_semantics=("arbitrary",)),
    )(proj, proj, proj, pos_col, inv_freq2, gq, gkv, w_uq, w_ukv, dq, dk, dv)


_NEG = -1e30
_LN2 = math.log(2.0)
ATT_CHAINS = 2


def _causal(tq, tk, q0, k0):
    row = q0 + lax.broadcasted_iota(jnp.int32, (tq, tk), 0)
    col = k0 + lax.broadcasted_iota(jnp.int32, (tq, tk), 1)
    return col <= row


def _attn_fwd(q, k, v, tq, tk):
    t = q.shape[0]

    assert tk % tq == 0

    th = tq // ATT_CHAINS

    def body(q_ref, k_ref, v_ref, o_ref, lse_ref):
        i = pl.program_id(1)
        n_full = (i * tq) // tk

        def step(k0, carry, masked):
            kt = k_ref[pl.ds(k0, tk), :]
            vt = v_ref[pl.ds(k0, tk), :]
            out = []
            for c, (m, l, acc) in enumerate(carry):
                s = _dot(q_ref[pl.ds(c * th, th), :], kt, NT)
                if masked:
                    s = jnp.where(_causal(th, tk, i * tq + c * th, k0), s, _NEG)
                m_new = jnp.maximum(m, jnp.max(s, axis=-1, keepdims=True))
                p = jnp.exp2(s - m_new)
                alpha = jnp.exp2(m - m_new)
                out.append((m_new, alpha * l + jnp.sum(p, axis=-1, keepdims=True), alpha * acc + _dot(p.astype(BF), vt)))
            return tuple(out)

        init = tuple((jnp.full((th, 1), _NEG, F32), jnp.zeros((th, 1), F32), jnp.zeros((th, V_HEAD), F32)) for _ in range(ATT_CHAINS))
        carry = lax.fori_loop(0, n_full, lambda j, c: step(pl.multiple_of(j * tk, tk), c, False), init)
        for c, (m, l, acc) in enumerate(step(pl.multiple_of(n_full * tk, tk), carry, True)):
            o_ref[pl.ds(c * th, th), :] = acc / l
            lse_ref[pl.ds(c * th, th), :] = jnp.broadcast_to(m + jnp.log2(l), (th, LANE))

    return pl.pallas_call(
        body, grid=(MLA_HEADS, t // tq),
        in_specs=[pl.BlockSpec((tq, 2 * LANE), lambda h, i: (i, h)), pl.BlockSpec((t, 2 * LANE), lambda h, i: (0, h)),
                  pl.BlockSpec((t, V_HEAD), lambda h, i: (0, h))],
        out_specs=[pl.BlockSpec((tq, V_HEAD), lambda h, i: (i, h)), pl.BlockSpec((tq, LANE), lambda h, i: (i, h))],
        out_shape=[jax.ShapeDtypeStruct((t, MLA_VW), F32), jax.ShapeDtypeStruct((t, MLA_HEADS * LANE), F32)],
        name="attn_fwd", compiler_params=pltpu.CompilerParams(dimension_semantics=("parallel", "arbitrary")),
    )(q, k, v)


def _attn_bwd(q, k, v, do, lse, delta, tq, tk):
    t = q.shape[0]
    nkt = t // tk
    assert tk % tq == 0

    def body(q_ref, k_ref, v_ref, do_ref, lse_ref, dl_ref, dq_ref, dk_ref, dv_ref):
        j = pl.program_id(1)

        @pl.when(j == 0)
        def _():
            dq_ref[...] = jnp.zeros_like(dq_ref)

        kt = k_ref[...]
        vt = v_ref[...]

        def step(q0, carry, masked):
            dk, dv = carry
            rows = pl.ds(q0, tq)
            qt = q_ref[rows, :]
            dot_ = do_ref[rows, :]
            p = jnp.exp2(_dot(qt, kt, NT) - lse_ref[rows, pl.ds(0, 1)])
            if masked:
                p = jnp.where(_causal(tq, tk, q0, j * tk), p, 0.0)
            dv = dv + _dot(p.astype(BF), dot_, TN)
            ds = (p * (_dot(dot_, vt, NT) - dl_ref[rows, pl.ds(0, 1)])).astype(BF)
            dk = dk + _dot(ds, qt, TN)
            dq_ref[rows, :] += _dot(ds, kt)
            return dk, dv

        per = tk // tq
        carry = (jnp.zeros((tk, 2 * LANE), F32), jnp.zeros((tk, V_HEAD), F32))
        for dd in range(per):
            carry = step(pl.multiple_of(j * tk + dd * tq, tq), carry, True)

        def group(g, c):
            for dd in range(per):
                c = step(pl.multiple_of(g * tk + dd * tq, tq), c, False)
            return c

        dk, dv = lax.fori_loop(j + 1, nkt, group, carry)
        dk_ref[...] = dk * _LN2
        dv_ref[...] = dv

        @pl.when(j == nkt - 1)
        def _():
            dq_ref[...] = dq_ref[...] * _LN2

    return pl.pallas_call(
        body, grid=(MLA_HEADS, nkt),
        in_specs=[pl.BlockSpec((t, 2 * LANE), lambda h, j: (0, h)), pl.BlockSpec((tk, 2 * LANE), lambda h, j: (j, h)),
                  pl.BlockSpec((tk, V_HEAD), lambda h, j: (j, h)), pl.BlockSpec((t, V_HEAD), lambda h, j: (0, h)),
                  pl.BlockSpec((t, LANE), lambda h, j: (0, h)), pl.BlockSpec((t, LANE), lambda h, j: (0, h))],
        out_specs=[pl.BlockSpec((t, 2 * LANE), lambda h, j: (0, h)), pl.BlockSpec((tk, 2 * LANE), lambda h, j: (j, h)),
                   pl.BlockSpec((tk, V_HEAD), lambda h, j: (j, h))],
        out_shape=[jax.ShapeDtypeStruct((t, MLA_HEADS * 2 * LANE), F32), jax.ShapeDtypeStruct((t, MLA_HEADS * 2 * LANE), F32),
                   jax.ShapeDtypeStruct((t, MLA_VW), F32)],
        name="attn_bwd", compiler_params=pltpu.CompilerParams(dimension_semantics=("parallel", "arbitrary")),
    )(q, k, v, do, lse, delta)


def _adam_update(w, g, m, v):
    mm = ADAM_B1 * m + (1.0 - ADAM_B1) * g
    vv = ADAM_B2 * v + (1.0 - ADAM_B2) * jnp.square(g)
    m_hat = mm / (1.0 - ADAM_B1 ** ADAM_STEP)
    v_hat = vv / (1.0 - ADAM_B2 ** ADAM_STEP)
    return -ADAM_LR * (m_hat / (jnp.sqrt(v_hat) + ADAM_EPS) + ADAM_WD * w), mm, vv


def _adamw(w, g, m, v, name):
    r, c = w.shape
    tr = _pick(r, (256, 128, 64, 32, 16, 8))
    slots = g.ndim == 3

    def body(w_ref, g_ref, m_ref, v_ref, g_out, d_ref, nm_ref, nv_ref):
        if slots:
            gg = g_ref[0].astype(F32)
            for s in range(1, N_DEV):
                gg = gg + g_ref[s].astype(F32)
        else:
            gg = g_ref[...]
        g_out[...] = gg
        d_ref[...], nm_ref[...], nv_ref[...] = _adam_update(w_ref[...], gg, m_ref[...], v_ref[...])

    spec = pl.BlockSpec((tr, c), lambda i: (i, 0))
    g_spec = pl.BlockSpec((N_DEV, tr, c), lambda i: (0, i, 0)) if slots else spec
    return pl.pallas_call(
        body, grid=(r // tr,), in_specs=[spec, g_spec, spec, spec], out_specs=[spec] * 4,
        out_shape=[jax.ShapeDtypeStruct((r, c), F32)] * 4, name=name,
        compiler_params=pltpu.CompilerParams(dimension_semantics=("arbitrary",)),
    )(w, g, m, v)


def _adamw_many(ws, gs, ms, vs, name):
    n = len(ws)

    def body(*refs):
        for i in range(n):
            w_ref, g_ref, m_ref, v_ref = (refs[j * n + i] for j in range(4))
            d_ref, nm_ref, nv_ref = (refs[(4 + j) * n + i] for j in range(3))
            d_ref[...], nm_ref[...], nv_ref[...] = _adam_update(w_ref[...], g_ref[...], m_ref[...], v_ref[...])

    shapes = [jax.ShapeDtypeStruct(w.shape, F32) for w in ws]
    outs = pl.pallas_call(body, out_shape=shapes * 3, name=name)(*ws, *gs, *ms, *vs)
    return outs[:n], outs[n:2 * n], outs[2 * n:]


def _cast_bf16(xs, name, after=None):
    n = len(xs)
    extra = [] if after is None else [after]

    def body(*refs):
        outs = refs[n + len(extra):]
        for i in range(n):
            outs[i][...] = refs[i][...].astype(BF)

    vmem = pl.BlockSpec(memory_space=pltpu.VMEM)
    return pl.pallas_call(
        body, out_shape=[jax.ShapeDtypeStruct(x.shape, BF) for x in xs], name=name,
        in_specs=[vmem] * n + [pl.BlockSpec(memory_space=pl.ANY)] * len(extra), out_specs=[vmem] * n)(*xs, *extra)


def _pad_rows(a, n):
    return jnp.pad(a, ((0, n - a.shape[0]), (0, 0)))


def _w_in_to_padded(wt):
    s_ba = P_CQ
    s_cq = s_ba + 2 * DN_HEADS
    s_kr = s_cq + Q_LORA + KV_LORA
    return jnp.concatenate([wt[:s_ba], wt[s_cq:s_kr], _pad_rows(wt[s_ba:s_cq], LANE), _pad_rows(wt[s_kr:], LANE)], axis=0)


def _w_in_from_padded(wt):
    return jnp.concatenate([wt[:P_CQ], wt[P_BA:P_BA + 2 * DN_HEADS], wt[P_CQ:P_BA], wt[P_KR:P_KR + QK_ROPE]], axis=0)


def _w_uq_to_padded(wt):
    w3 = wt.reshape(MLA_HEADS, QK_NOPE + QK_ROPE, Q_LORA)
    nope = w3[:, :QK_NOPE].reshape(MLA_HEADS * QK_NOPE, Q_LORA)
    rope = jnp.pad(w3[:, QK_NOPE:], ((0, 0), (0, LANE - QK_ROPE), (0, 0))).reshape(MLA_HEADS * LANE, Q_LORA)
    return jnp.concatenate([nope, rope], axis=0)


def _w_uq_from_padded(wt):
    nope = wt[:MLA_HEADS * QK_NOPE].reshape(MLA_HEADS, QK_NOPE, Q_LORA)
    rope = wt[MLA_HEADS * QK_NOPE:].reshape(MLA_HEADS, LANE, Q_LORA)[:, :QK_ROPE]
    return jnp.concatenate([nope, rope], axis=1).reshape(MLA_HEADS * (QK_NOPE + QK_ROPE), Q_LORA)


def _pack(pieces, width, row_mult):
    flat = jnp.concatenate([p.reshape(-1) for p in pieces])
    n = flat.shape[0]
    rows = -(-n // (width * row_mult)) * row_mult
    return jnp.pad(flat, (0, rows * width - n)).reshape(rows, width)


def _unpack(flat, shapes):
    out, o = [], 0
    for s in shapes:
        n = math.prod(s)
        out.append(flat[o:o + n].reshape(s))
        o += n
    return out


def kernel(x, c, positions, w_ada, b_ada, w_in, conv_w, a_log, dt_bias, dn_norm_g, q_norm_g, w_uq, kv_norm_g, w_ukv, w_o, ln1_g, ln1_b, w_gate, w_up, w_down, ln2_g, ln2_b, loss_target, m_w_ada, m_b_ada, m_w_in, m_conv_w, m_a_log, m_dt_bias, m_dn_norm_g, m_q_norm_g, m_w_uq, m_kv_norm_g, m_w_ukv, m_w_o, m_ln1_g, m_ln1_b, m_w_gate, m_w_up, m_w_down, m_ln2_g, m_ln2_b, v_w_ada, v_b_ada, v_w_in, v_conv_w, v_a_log, v_dt_bias, v_dn_norm_g, v_q_norm_g, v_w_uq, v_kv_norm_g, v_w_ukv, v_w_o, v_ln1_g, v_ln1_b, v_w_gate, v_w_up, v_w_down, v_ln2_g, v_ln2_b):
    me = 4 * lax.axis_index("x") + 2 * lax.axis_index("y") + lax.axis_index("c")
    t, d = x.shape[1], x.shape[2]
    ada_n = w_ada.shape[2]

    tr = lambda w: w[0].T
    rows = lambda a: a.reshape(-1, a.shape[2])
    (in_shard,) = _cast_bf16([tr(w_in)], "cast_w_in")
    in_gather, token = _exchange_start([in_shard], "gather_w_in_start", scatter=False)
    cw = conv_w.shape[3]
    c_all, conv_all = _exchange([c + token, conv_w[0, :, 0, :]], "gather_small", scatter=False)
    c_all = c_all.reshape(N_DEV, d)
    conv_full = conv_all.transpose(1, 0, 2).reshape(CONV_K, N_DEV * cw)
    conv_w8 = jnp.pad(conv_full, ((0, 8 - CONV_K), (0, 0)))

    b_ada_mine = lax.dynamic_slice(b_ada, (0, me * ada_n), (1, ada_n))
    mod_cols = _mod_fwd(c_all, w_ada[0], b_ada_mine)
    (mod_all,) = _exchange([mod_cols.reshape(N_DEV, 1, ada_n)], "scatter_mod", scatter=True)
    mod = mod_all.reshape(1, N_DEV * ada_n)

    (a_in,) = _exchange_wait(in_gather, mod, "gather_w_in_wait", scatter=False)
    later = _cast_bf16([tr(w_uq), tr(w_ukv), w_o[0], tr(w_gate), tr(w_up), w_down[0]], "cast_weights", after=a_in)
    mixer_gather, token_a = _exchange_start(later[:3], "gather_mixer_weights_start", scatter=False)
    ffn_gather, token_b = _exchange_start(later[3:], "gather_ffn_weights_start", scatter=False)
    mod = mod + (token_a + token_b)
    w_in_t = _w_in_to_padded(rows(a_in))

    def mixer_weights(after):
        a_uq, a_ukv, a_o = _exchange_wait(mixer_gather, after, "gather_mixer_weights_wait", scatter=False)
        return _w_uq_to_padded(rows(a_uq)), rows(a_ukv), rows(a_o)

    def ffn_weights(after):
        a_gate, a_up, a_down = _exchange_wait(ffn_gather, after, "gather_ffn_weights_wait", scatter=False)
        return rows(a_gate), rows(a_up), rows(a_down)

    def by_dest(g):
        return g.reshape(N_DEV, -1, g.shape[1])

    scatters = {}

    def grads_ready(tag, *g):
        if tag == "ffn":
            pieces = [by_dest(a) for a in g]
        elif tag == "mixer":
            g_w_o, g_w_uq_t, g_w_ukv_t = g
            pieces = [by_dest(g_w_o), by_dest(_w_uq_from_padded(g_w_uq_t).astype(BF)), by_dest(g_w_ukv_t.astype(BF))]
        else:
            pieces = [by_dest(_w_in_from_padded(g[0]))]
        scatters[tag], token = _exchange_start(pieces, "scatter_%s_grads_start" % tag, scatter=True)
        return token

    loc = _local_step(x[0], loss_target[0], positions[0], mod, w_in_t, mixer_weights, ffn_weights, grads_ready,
                      conv_w8, a_log, dt_bias, dn_norm_g, q_norm_g, kv_norm_g, ln1_g, ln1_b, ln2_g, ln2_b)
    grad_x, loss_acc, dmod, d_conv8, d_al8, d_dt8, d_dn_g, d_q_g, d_kv_g, d_ln1_g, d_ln1_b, d_ln2_g, d_ln2_b = loc

    small_shapes = [(6 * d,), (CONV_K, N_DEV * cw), (DN_HEADS,), (DN_HEADS,), (DN_DV,), (Q_LORA,), (KV_LORA,), (d,), (d,), (d,), (d,), (1,)]
    gsmall = _pack([dmod, d_conv8[:CONV_K], d_al8[0, :DN_HEADS], d_dt8[0, :DN_HEADS], d_dn_g, d_q_g, d_kv_g,
                    d_ln1_g, d_ln1_b, d_ln2_g, d_ln2_b, loss_acc[0, :1]], LANE, 8)
    (gsmall_all,) = _exchange([gsmall], "gather_small_grads", scatter=False)
    dmod_all = gsmall_all.reshape(N_DEV, -1)[:, :6 * d]
    tot = _unpack(_sum_slots(gsmall_all, "sum_small_grads").reshape(-1), small_shapes)
    g_b_ada, g_conv_full, g_a_log, g_dt_bias, g_dn_g, g_q_g, g_kv_g, g_ln1_g, g_ln1_b, g_ln2_g, g_ln2_b, loss1 = tot
    loss = loss1.reshape(())
    g_conv_w = lax.dynamic_slice(g_conv_full, (0, me * cw), (CONV_K, cw))
    g_w_ada = _mod_bwd(c_all.T, lax.dynamic_slice(dmod_all, (0, me * ada_n), (N_DEV, ada_n)))

    grads = {"w_ada": g_w_ada[None], "b_ada": g_b_ada[None], "conv_w": g_conv_w[None, :, None, :],
             "a_log": g_a_log[None], "dt_bias": g_dt_bias[None], "dn_norm_g": g_dn_g[None], "q_norm_g": g_q_g[None],
             "kv_norm_g": g_kv_g[None], "ln1_g": g_ln1_g[None], "ln1_b": g_ln1_b[None], "ln2_g": g_ln2_g[None], "ln2_b": g_ln2_b[None]}
    weights = dict(w_ada=w_ada, b_ada=b_ada, w_in=w_in, conv_w=conv_w, a_log=a_log, dt_bias=dt_bias, dn_norm_g=dn_norm_g,
                   q_norm_g=q_norm_g, w_uq=w_uq, kv_norm_g=kv_norm_g, w_ukv=w_ukv, w_o=w_o, ln1_g=ln1_g, ln1_b=ln1_b,
                   w_gate=w_gate, w_up=w_up, w_down=w_down, ln2_g=ln2_g, ln2_b=ln2_b)
    ms = dict(w_ada=m_w_ada, b_ada=m_b_ada, w_in=m_w_in, conv_w=m_conv_w, a_log=m_a_log, dt_bias=m_dt_bias,
              dn_norm_g=m_dn_norm_g, q_norm_g=m_q_norm_g, w_uq=m_w_uq, kv_norm_g=m_kv_norm_g, w_ukv=m_w_ukv, w_o=m_w_o,
              ln1_g=m_ln1_g, ln1_b=m_ln1_b, w_gate=m_w_gate, w_up=m_w_up, w_down=m_w_down, ln2_g=m_ln2_g, ln2_b=m_ln2_b)
    vs = dict(w_ada=v_w_ada, b_ada=v_b_ada, w_in=v_w_in, conv_w=v_conv_w, a_log=v_a_log, dt_bias=v_dt_bias,
              dn_norm_g=v_dn_norm_g, q_norm_g=v_q_norm_g, w_uq=v_w_uq, kv_norm_g=v_kv_norm_g, w_ukv=v_w_ukv, w_o=v_w_o,
              ln1_g=v_ln1_g, ln1_b=v_ln1_b, w_gate=v_w_gate, w_up=v_w_up, w_down=v_w_down, ln2_g=v_ln2_g, ln2_b=v_ln2_b)
    names = list(weights)
    big = ("w_ada", "w_gate", "w_up", "w_down", "w_o", "w_uq", "w_ukv", "w_in")
    waits = {"w_gate": ("ffn", ("w_gate", "w_up", "w_down")), "w_o": ("mixer", ("w_o", "w_uq", "w_ukv")), "w_in": ("in", ("w_in",))}
    delta_w, new_m, new_v, slots = {}, {}, {}, {}
    last = g_w_ada
    for n in big:
        if n == "w_in":
            rest = [r for r in names if r not in big]
            flat2 = lambda a: a.reshape(-1, a.shape[-1])
            outs = _adamw_many(*[[flat2(src[r]) for r in rest] for src in (weights, grads, ms, vs)], "adamw_small")
            for dst, o in zip((delta_w, new_m, new_v), outs):
                for r, a in zip(rest, o):
                    dst[r] = a.reshape(weights[r].shape)
            last = outs[0][0]
        transposed = n in ("w_in", "w_uq", "w_ukv", "w_gate", "w_up")
        two = (lambda a: a[0].T) if transposed else (lambda a: a[0])
        back = (lambda a: a.T[None]) if transposed else (lambda a: a[None])
        if n in waits:
            tag, members = waits[n]
            slots.update(zip(members, _exchange_wait(scatters[tag], last, "scatter_%s_grads_wait" % tag, scatter=True)))
        g_in = slots[n] if n in slots else two(grads[n])
        gr, dlt, nm, nv = _adamw(two(weights[n]), g_in, two(ms[n]), two(vs[n]), "adamw_" + n)
        grads[n], delta_w[n], new_m[n], new_v[n] = back(gr), back(dlt), back(nm), back(nv)
        last = nv

    return (loss, grad_x[None], *[grads[n] for n in names], *[delta_w[n] for n in names],
            *[new_m[n] for n in names], *[new_v[n] for n in names])


def _local_step(xs, tgt, pos, mod, w_in_t, mixer_weights, ffn_weights, grads_ready, conv_w8,
                a_log, dt_bias, dn_norm_g, q_norm_g, kv_norm_g, ln1_g, ln1_b, ln2_g, ln2_b):
    t, d = xs.shape
    sh_m, sc_m, gt_m, sh_f, sc_f, gt_f = [mod[:, i * d:(i + 1) * d] for i in range(6)]
    pos_col = pos.astype(F32).reshape(t, 1)
    inv_freq = 1.0 / (ROPE_THETA ** (jnp.arange(0, QK_ROPE, 2, dtype=F32) / QK_ROPE))
    inv_freq2 = jnp.pad(jnp.concatenate([inv_freq, inv_freq]), (0, LANE - QK_ROPE)).reshape(1, LANE)
    al8 = jnp.pad(a_log, ((0, 7), (0, LANE - DN_HEADS)))
    dt8 = jnp.pad(dt_bias, ((0, 7), (0, LANE - DN_HEADS)))

    tm = min(512, t)
    tq = min(256, t)
    tk = min(512, t)

    (h1,) = _rowwise("modulate_in", lambda xx, sc, sh: xx * (1.0 + sc) + sh, [xs], [sc_m, sh_m], [(d, BF)], [], tm)
    proj = _matmul(h1, w_in_t, "nt", "in_proj")
    qkv = _conv_fwd(proj, conv_w8, min(256, t))
    gdn_tm = min(512, t)
    intra = _gdn_intra_fwd(qkv, proj, al8, dt8, gdn_tm)
    o_dn, states = _gdn_scan_fwd(intra, gdn_tm)
    w_uq_t, w_ukv_t, w_o_f = mixer_weights(states)
    qc, kc, vc = _mla_prep_fwd(proj, pos_col, inv_freq2, q_norm_g, kv_norm_g, w_uq_t, w_ukv_t, tm)
    o_mla, lse = _attn_fwd(qc, kc, vc, tk, tk)

    def mix_in(o, z, om, g):
        return jnp.concatenate(_gdn_out(o, z, g) + [om], axis=1)

    (mixin,) = _rowwise("mixer_out", mix_in, [o_dn, (proj, DN_VW, P_Z // DN_VW), o_mla], [dn_norm_g], [(2 * DN_VW, BF)], [], tm)
    mix = _matmul(mixin, w_o_f, "nn", "out_proj")

    def block1(xx, mx, gt, g1, b1, sc, sh):
        x1 = _layernorm(DEEPNORM_ALPHA * xx + gt * mx, g1, b1)
        return x1, x1 * (1.0 + sc) + sh

    x1, h2 = _rowwise("norm1_modulate", block1, [xs, mix], [gt_m, ln1_g, ln1_b, sc_f, sh_f], [(d, F32), (d, BF)], [], tm)
    w_gate_f, w_up_f, w_down_f = ffn_weights(h2)
    act, gate, up = _ffn_in(h2, w_gate_f, w_up_f)
    ff = _matmul(act, w_down_f, "nn", "ffn_out")

    def tail_loss(x1_, ff_, gt, g2, b2, tg):
        y = _layernorm(DEEPNORM_ALPHA * x1_ + gt * ff_, g2, b2)
        return 0.5 * jnp.sum(jnp.mean(jnp.square(y - tg), axis=-1))

    def tail(x1_, ff_, tg, gt, g2, b2):
        loss, (dx1, dff, dgt, dg2, db2) = jax.value_and_grad(tail_loss, argnums=(0, 1, 2, 3, 4))(x1_, ff_, gt, g2, b2, tg)
        return dx1, dff, jnp.full((1, LANE), loss, F32), dgt, dg2, db2

    dx1_a, dff, loss_acc, d_gt_f, d_ln2_g, d_ln2_b = _rowwise(
        "norm2_loss", tail, [x1, ff, tgt], [gt_f, ln2_g, ln2_b], [(d, F32), (d, BF)], [(1, LANE), (1, d), (1, d), (1, d)], tm)

    g_w_down = _matmul(act, dff, "tn", "d_w_down", BF)
    dgate, dup = _ffn_act_bwd(dff, w_down_f, gate, up)
    g_w_gate = _matmul(dgate, h2, "tn", "d_w_gate", BF)
    g_w_up = _matmul(dup, h2, "tn", "d_w_up", BF)
    token = grads_ready("ffn", g_w_gate, g_w_up, g_w_down)
    dh2 = _matmul2_nn(dgate, w_gate_f, dup, w_up_f, "d_ffn_in")

    def block1_bwd(xx, mx, dx1_, dh2_, gt, g1, b1, sc, sh):
        _, vjp = jax.vjp(block1, xx, mx, gt, g1, b1, sc, sh)
        dxx, dmx, dgt, dg1, db1, dsc, dsh = vjp((dx1_, dh2_))
        return dxx, dmx, dgt, dg1, db1, dsc, dsh

    dx_a, dmix, d_gt_m, d_ln1_g, d_ln1_b, d_sc_f, d_sh_f = _rowwise(
        "norm1_modulate_bwd", block1_bwd, [xs, mix, dx1_a, dh2], [gt_m + token, ln1_g, ln1_b, sc_f, sh_f],
        [(d, F32), (d, BF)], [(1, d)] * 5, min(256, t))

    dmixin = _matmul(dmix, w_o_f, "nt", "d_mixer_out")
    g_w_o = _matmul(mixin, dmix, "tn", "d_w_o", BF)

    def mixer_bwd(o, z, om, dmi, g):
        _, vjp = jax.vjp(lambda o_, z_, g_: jnp.concatenate(_gdn_out(o_, z_, g_), axis=1), o, z, g)
        do_, dz_, dg_ = vjp(dmi[:, :DN_VW])
        dom = dmi[:, DN_VW:]
        delta = [jnp.broadcast_to(jnp.sum(dom[:, h * V_HEAD:(h + 1) * V_HEAD] * om[:, h * V_HEAD:(h + 1) * V_HEAD], axis=-1, keepdims=True), (o.shape[0], LANE))
                 for h in range(MLA_HEADS)]
        return do_, dz_, dom, jnp.concatenate(delta, axis=1), dg_

    do_dn, dz, do_mla, delta, d_dn_g = _rowwise(
        "mixer_out_bwd", mixer_bwd, [o_dn, (proj, DN_VW, P_Z // DN_VW), o_mla, dmixin], [dn_norm_g],
        [(DN_VW, F32), (DN_VW, BF), (MLA_VW, BF), (MLA_HEADS * LANE, F32)], [(1, DN_DV)], tm)

    dqc, dkc, dvc = _attn_bwd(qc, kc, vc, do_mla, lse, delta, tq, tk)
    dcq, dckv, dkr, d_q_g, d_kv_g, g_w_uq_t, g_w_ukv_t = _mla_prep_bwd(
        proj, pos_col, inv_freq2, q_norm_g, kv_norm_g, w_uq_t, w_ukv_t, dqc, dkc, dvc, min(256, t))

    token = grads_ready("mixer", g_w_o, g_w_uq_t, g_w_ukv_t)

    d_intra = _gdn_scan_bwd(intra, states, do_dn, gdn_tm)
    dqkv_act, dba, d_al8, d_dt8 = _gdn_intra_bwd(qkv, proj, al8 + token, dt8, d_intra, min(256, t))
    dqkv_pre, d_conv8 = _conv_bwd(proj, conv_w8, dqkv_act, min(256, t))

    dproj = jnp.concatenate([dqkv_pre, dz, dcq, dckv, dba, dkr], axis=1)
    dh1 = _matmul(dproj, w_in_t, "nn", "d_in_proj")
    g_w_in_t = _matmul(dproj, h1, "tn", "d_w_in", BF)
    token = grads_ready("in", g_w_in_t)

    def modulate_bwd(xx, dh, dxa, sc):
        return dh * (1.0 + sc) + dxa, jnp.sum(dh * xx, axis=0, keepdims=True), jnp.sum(dh, axis=0, keepdims=True)

    grad_x, d_sc_m, d_sh_m = _rowwise("modulate_in_bwd", modulate_bwd, [xs, dh1, dx_a], [sc_m + token], [(d, F32)], [(1, d), (1, d)], tm)
    dmod = jnp.concatenate([d_sh_m, d_sc_m, d_gt_m, d_sh_f, d_sc_f, d_gt_f], axis=1)
    return grad_x, loss_acc, dmod, d_conv8, d_al8, d_dt8, d_dn_g, d_q_g, d_kv_g, d_ln1_g, d_ln1_b, d_ln2_g, d_ln2_b
```

```python
import functools
import math

import jax
import jax.numpy as jnp
from jax import lax
from jax.experimental import pallas as pl
from jax.experimental.pallas import tpu as pltpu

F32 = jnp.float32
BF = jnp.bfloat16
HI = lax.Precision.HIGHEST

N_DEV = 8
DN_HEADS = 4
DN_DK = 128
DN_DV = 128
CONV_K = 4
CHUNK = 64
MLA_HEADS = 4
QK_NOPE = 128
QK_ROPE = 64
V_HEAD = 128
Q_LORA = 512
KV_LORA = 256
ROPE_THETA = 10000.0
DEPTH = 1
DEEPNORM_ALPHA = (2.0 * DEPTH) ** 0.25
LANE = 128
CONV_HALO = 8
CONV_ROWS, CONV_COLS = 64, 256

DN_QK = DN_HEADS * DN_DK
DN_VW = DN_HEADS * DN_DV
DN_CONV_CH = 2 * DN_QK + DN_VW
MLA_VW = MLA_HEADS * V_HEAD
MLA_QCAT = QK_NOPE + LANE
N_IN = DN_CONV_CH + DN_VW + 2 * DN_HEADS + Q_LORA + KV_LORA + QK_ROPE
P_QKV = 0
P_Z = DN_CONV_CH
P_CQ = P_Z + DN_VW
P_CKV = P_CQ + Q_LORA
P_BA = P_CKV + KV_LORA
P_KR = P_BA + LANE
N_INP = P_KR + LANE

ADAM_LR = 0.001
ADAM_B1 = 0.9
ADAM_B2 = 0.999
ADAM_EPS = 1e-08
ADAM_WD = 0.01
ADAM_STEP = 10

NN = (((1,), (0,)), ((), ()))
NT = (((1,), (1,)), ((), ()))
TN = (((0,), (0,)), ((), ()))


def _pick(n, prefs):
    for p in prefs:
        if n % p == 0:
            return p
    return n


def _full(shape):
    return pl.BlockSpec(shape, lambda *_: (0,) * len(shape))


def _dot(a, b, dims=NN):
    return lax.dot_general(a, b, dims, preferred_element_type=F32)


def _doth(a, b, dims=NN):
    return lax.dot_general(a, b, dims, precision=HI, preferred_element_type=F32)


@jax.custom_vjp
def _mmb(a, b):
    return _dot(a.astype(BF), b.astype(BF), NN)


def _mmb_fwd(a, b):
    return _mmb(a, b), (a, b)


def _mmb_bwd(res, g):
    a, b = res
    gb = g.astype(BF)
    return (_dot(gb, b.astype(BF), NT).astype(a.dtype), _dot(a.astype(BF), gb, TN).astype(b.dtype))


_mmb.defvjp(_mmb_fwd, _mmb_bwd)


@jax.custom_vjp
def _mmb_nt(a, b):
    return _dot(a.astype(BF), b.astype(BF), NT)


def _mmb_nt_fwd(a, b):
    return _mmb_nt(a, b), (a, b)


def _mmb_nt_bwd(res, g):
    a, b = res
    gb = g.astype(BF)
    return (_dot(gb, b.astype(BF), NN).astype(a.dtype), _dot(gb, a.astype(BF), TN).astype(b.dtype))


_mmb_nt.defvjp(_mmb_nt_fwd, _mmb_nt_bwd)


@jax.custom_vjp
def _mmb_tn(a, b):
    return _dot(a.astype(BF), b.astype(BF), TN)


def _mmb_tn_fwd(a, b):
    return _mmb_tn(a, b), (a, b)


def _mmb_tn_bwd(res, g):
    a, b = res
    gb = g.astype(BF)
    return (_dot(b.astype(BF), gb, NT).astype(a.dtype), _dot(a.astype(BF), gb, NN).astype(b.dtype))


_mmb_tn.defvjp(_mmb_tn_fwd, _mmb_tn_bwd)


def _sigmoid(x):
    return 0.5 * (jnp.tanh(0.5 * x) + 1.0)


def _silu(x):
    return x * _sigmoid(x)


def _softplus(x):
    return jnp.maximum(x, 0.0) + jnp.log(1.0 + jnp.exp(-jnp.abs(x)))


def _layernorm(x, g, b, eps=1e-5):
    mu = jnp.mean(x, axis=-1, keepdims=True)
    xc = x - mu
    var = jnp.mean(xc * xc, axis=-1, keepdims=True)
    return xc * lax.rsqrt(var + eps) * g + b


def _rmsnorm(x, g, eps=1e-6):
    return x * lax.rsqrt(jnp.mean(x * x, axis=-1, keepdims=True) + eps) * g


def _l2norm(x, eps=1e-6):
    return x * lax.rsqrt(jnp.sum(x * x, axis=-1, keepdims=True) + eps)


def _rowwise(name, fn, rows, vecs, out_rows, out_accs, tm):
    rows = [r if isinstance(r, tuple) else (r, r.shape[1], 0) for r in rows]
    t = rows[0][0].shape[0]
    tm = min(tm, t)
    assert t % tm == 0
    nr, nv, no = len(rows), len(vecs), len(out_rows)

    def body(*refs):
        ins = [r[...] for r in refs[:nr + nv]]
        outs = fn(*ins)
        outs = outs if isinstance(outs, (tuple, list)) else (outs,)
        o_rows = refs[nr + nv:nr + nv + no]
        o_accs = refs[nr + nv + no:]
        for o, val in zip(o_rows, outs[:no]):
            o[...] = val.astype(o.dtype)
        if o_accs:
            @pl.when(pl.program_id(0) == 0)
            def _():
                for o in o_accs:
                    o[...] = jnp.zeros_like(o)
            for o, val in zip(o_accs, outs[no:]):
                o[...] += val

    in_specs = [pl.BlockSpec((tm, w), functools.partial(lambda i, j: (i, j), j=j)) for (_, w, j) in rows]
    in_specs += [_full(v.shape) for v in vecs]
    out_specs = [pl.BlockSpec((tm, w), lambda i: (i, 0)) for (w, _) in out_rows]
    out_specs += [_full(s) for s in out_accs]
    out_shape = [jax.ShapeDtypeStruct((t, w), d) for (w, d) in out_rows]
    out_shape += [jax.ShapeDtypeStruct(s, F32) for s in out_accs]
    res = pl.pallas_call(
        body, grid=(t // tm,), in_specs=in_specs, out_specs=out_specs, out_shape=out_shape, name=name,
        compiler_params=pltpu.CompilerParams(dimension_semantics=("arbitrary",)),
    )(*[r[0] for r in rows], *vecs)
    return res


def _matmul(a, b, mode, name, out_dtype=F32):
    if mode == "nn":
        (m, k), n = a.shape, b.shape[1]
    elif mode == "nt":
        (m, k), n = a.shape, b.shape[0]
    else:
        (k, m), n = a.shape, b.shape[1]
    tm, tn, tk = _matmul_tiles(m, n, k, a.dtype.itemsize, b.dtype.itemsize, jnp.dtype(out_dtype).itemsize)
    nk = k // tk
    dims = {"nn": NN, "nt": NT, "tn": TN}[mode]

    def body(a_ref, b_ref, o_ref, *acc):
        part = _dot(a_ref[...].astype(BF), b_ref[...].astype(BF), dims)
        if nk == 1:
            o_ref[...] = part.astype(o_ref.dtype)
            return
        (acc_ref,) = acc
        kk = pl.program_id(2)

        @pl.when(kk == 0)
        def _():
            acc_ref[...] = part

        @pl.when(kk > 0)
        def _():
            acc_ref[...] += part

        @pl.when(kk == nk - 1)
        def _():
            o_ref[...] = acc_ref[...].astype(o_ref.dtype)

    a_spec = pl.BlockSpec((tk, tm), lambda i, j, kk: (kk, i)) if mode == "tn" else pl.BlockSpec((tm, tk), lambda i, j, kk: (i, kk))
    b_spec = pl.BlockSpec((tn, tk), lambda i, j, kk: (j, kk)) if mode == "nt" else pl.BlockSpec((tk, tn), lambda i, j, kk: (kk, j))
    return pl.pallas_call(
        body, grid=(m // tm, n // tn, nk), in_specs=[a_spec, b_spec],
        out_specs=pl.BlockSpec((tm, tn), lambda i, j, kk: (i, j)),
        out_shape=jax.ShapeDtypeStruct((m, n), out_dtype),
        scratch_shapes=[pltpu.VMEM((tm, tn), F32)] if nk > 1 else [], name=name,
        compiler_params=pltpu.CompilerParams(dimension_semantics=("parallel", "parallel", "arbitrary")),
    )(a, b)


def _lane_tile(n, cap):
    return max([n // s for s in range(1, n // LANE + 1) if n % s == 0 and (n // s) % LANE == 0 and n // s <= cap] or [n])


def _ffn_in(h, w_gate, w_up):
    m, k = h.shape
    f = w_gate.shape[0]
    tm, tn = _pick(m, (512, 256, 128)), _lane_tile(f, 1408)

    def body(h_ref, wg_ref, wu_ref, act_ref, g_ref, u_ref):
        hh = h_ref[...]
        g = _dot(hh, wg_ref[...], NT)
        u = _dot(hh, wu_ref[...], NT)
        act_ref[...] = (_silu(g) * u).astype(act_ref.dtype)
        g_ref[...] = g.astype(g_ref.dtype)
        u_ref[...] = u.astype(u_ref.dtype)

    w_spec = pl.BlockSpec((tn, k), lambda i, j: (j, 0))
    o_spec = pl.BlockSpec((tm, tn), lambda i, j: (i, j))
    return pl.pallas_call(
        body, grid=(m // tm, f // tn), in_specs=[pl.BlockSpec((tm, k), lambda i, j: (i, 0)), w_spec, w_spec],
        out_specs=[o_spec] * 3, out_shape=[jax.ShapeDtypeStruct((m, f), BF)] * 3, name="ffn_in",
        compiler_params=pltpu.CompilerParams(dimension_semantics=("parallel", "parallel")),
    )(h, w_gate, w_up)


def _ffn_act_bwd(dff, w_down, gate, up):
    m, k = dff.shape
    f = w_down.shape[0]
    tm, tn = _pick(m, (512, 256, 128)), _lane_tile(f, 1408)

    def body(d_ref, w_ref, g_ref, u_ref, dg_ref, du_ref):
        da = _dot(d_ref[...], w_ref[...], NT)
        g = g_ref[...].astype(F32)
        sg = _sigmoid(g)
        dg_ref[...] = (da * u_ref[...].astype(F32) * (sg * (1.0 + g * (1.0 - sg)))).astype(dg_ref.dtype)
        du_ref[...] = (da * (g * sg)).astype(du_ref.dtype)

    o_spec = pl.BlockSpec((tm, tn), lambda i, j: (i, j))
    return pl.pallas_call(
        body, grid=(m // tm, f // tn),
        in_specs=[pl.BlockSpec((tm, k), lambda i, j: (i, 0)), pl.BlockSpec((tn, k), lambda i, j: (j, 0)), o_spec, o_spec],
        out_specs=[o_spec] * 2, out_shape=[jax.ShapeDtypeStruct((m, f), BF)] * 2, name="d_ffn_act",
        compiler_params=pltpu.CompilerParams(dimension_semantics=("parallel", "parallel")),
    )(dff, w_down, gate, up)


def _matmul2_nn(a1, b1, a2, b2, name):
    m, k = a1.shape
    n = b1.shape[1]
    tm, tn = _pick(m, (512, 256, 128)), _pick(n, (512, 256, 128))

    def body(a1_ref, b1_ref, a2_ref, b2_ref, o_ref):
        o_ref[...] = _dot(a1_ref[...], b1_ref[...]) + _dot(a2_ref[...], b2_ref[...])

    a_spec = pl.BlockSpec((tm, k), lambda i, j: (i, 0))
    b_spec = pl.BlockSpec((k, tn), lambda i, j: (0, j))
    return pl.pallas_call(
        body, grid=(m // tm, n // tn), in_specs=[a_spec, b_spec, a_spec, b_spec],
        out_specs=pl.BlockSpec((tm, tn), lambda i, j: (i, j)), out_shape=jax.ShapeDtypeStruct((m, n), F32), name=name,
        compiler_params=pltpu.CompilerParams(dimension_semantics=("parallel", "parallel")),
    )(a1, b1, a2, b2)


MATMUL_VMEM_BUDGET = 28 * 1024 * 1024


def _matmul_tiles(m, n, k, a_bytes, b_bytes, o_bytes):
    def divisors(x, cap):
        return sorted({x // s for s in range(1, 65) if x % s == 0 and (x // s) % LANE == 0 and x // s <= cap}, reverse=True) or [x]

    for tk in divisors(k, k):
        best = None
        for tm in divisors(m, 1024):
            for tn in divisors(n, 2048):
                need = 2 * (tm * tk * a_bytes + tk * tn * b_bytes + tm * tn * o_bytes) + (tm * tn * 4 if tk < k else 0)
                if need <= MATMUL_VMEM_BUDGET and tm * tn >= 512 * 512 and (best is None or tm * tn > best[0] * best[1]):
                    best = (tm, tn)
        if best:
            return best[0], best[1], tk
    return _pick(m, (512, 256, 128)), _pick(n, (512, 256, 128)), _pick(k, (512, 256, 128))


def _exchange(xs, name, scatter):
    n = len(xs)
    npeer = N_DEV - 1

    def body(*refs):
        x_refs, o_refs = refs[:n], refs[n:2 * n]
        send_sems, recv_sems, local_sems = refs[2 * n:]
        mx, my, mc = lax.axis_index("x"), lax.axis_index("y"), lax.axis_index("c")
        me = 4 * mx + 2 * my + mc
        src_me = [x.at[me] if scatter else x for x in x_refs]
        mine = [pltpu.make_async_copy(src_me[a], o_refs[a].at[me], local_sems.at[a]) for a in range(n)]
        for cp in mine:
            cp.start()
        copies = []
        for k in range(1, N_DEV):
            px, py, pc = mx ^ (k >> 2), my ^ ((k >> 1) & 1), mc ^ (k & 1)
            peer = 4 * px + 2 * py + pc
            for a in range(n):
                cp = pltpu.make_async_remote_copy(
                    src_ref=x_refs[a].at[peer] if scatter else x_refs[a], dst_ref=o_refs[a].at[me],
                    send_sem=send_sems.at[a * npeer + k - 1], recv_sem=recv_sems.at[a * npeer + k - 1],
                    device_id=(px, py, pc), device_id_type=pl.DeviceIdType.MESH)
                cp.start()
                copies.append((cp, a, k, peer))
        for cp, a, k, peer in copies:
            pltpu.make_async_remote_copy(
                src_ref=src_me[a], dst_ref=o_refs[a].at[peer], send_sem=send_sems.at[a * npeer + k - 1],
                recv_sem=recv_sems.at[a * npeer + k - 1], device_id=(mx, my, mc),
                device_id_type=pl.DeviceIdType.MESH).wait_recv()
        for cp, _, _, _ in copies:
            cp.wait_send()
        for cp in mine:
            cp.wait()

    return pl.pallas_call(
        body, out_shape=[jax.ShapeDtypeStruct((N_DEV,) + x.shape[-2:], x.dtype) for x in xs],
        in_specs=[pl.BlockSpec(memory_space=pl.ANY)] * n, out_specs=[pl.BlockSpec(memory_space=pl.ANY)] * n,
        scratch_shapes=[pltpu.SemaphoreType.DMA((n * npeer,)), pltpu.SemaphoreType.DMA((n * npeer,)),
                        pltpu.SemaphoreType.DMA((n,))],
        name=name,
    )(*xs)


def _peer_of(k):
    mx, my, mc = lax.axis_index("x"), lax.axis_index("y"), lax.axis_index("c")
    px, py, pc = mx ^ (k >> 2), my ^ ((k >> 1) & 1), mc ^ (k & 1)
    return (px, py, pc), 4 * px + 2 * py + pc


def _exchange_start(xs, name, scatter):
    n = len(xs)
    npeer = N_DEV - 1

    def body(*refs):
        x_refs, land_refs = refs[:n], refs[n:2 * n]
        send_sems, recv_sems, token = refs[2 * n], refs[2 * n + 1], refs[-1]
        me = 4 * lax.axis_index("x") + 2 * lax.axis_index("y") + lax.axis_index("c")
        for k in range(1, N_DEV):
            dev, peer = _peer_of(k)
            for a in range(n):
                pltpu.make_async_remote_copy(
                    src_ref=x_refs[a].at[peer] if scatter else x_refs[a], dst_ref=land_refs[a].at[me],
                    send_sem=send_sems.at[a * npeer + k - 1], recv_sem=recv_sems.at[a * npeer + k - 1],
                    device_id=dev, device_id_type=pl.DeviceIdType.MESH).start()
        token[...] = jnp.zeros_like(token)

    hbm = pl.BlockSpec(memory_space=pltpu.HBM)
    sem = pl.BlockSpec(memory_space=pltpu.SEMAPHORE)
    lands = [pltpu.with_memory_space_constraint(lax.empty((N_DEV,) + x.shape[-2:], x.dtype), pltpu.HBM) for x in xs]
    srcs = [pltpu.with_memory_space_constraint(x, pltpu.HBM) for x in xs]
    outs = pl.pallas_call(
        body, name=name,
        out_shape=(pltpu.SemaphoreType.DMA((n * npeer,)), pltpu.SemaphoreType.DMA((n * npeer,)),
                   *[pltpu.HBM(x.shape, x.dtype) for x in srcs], *[pltpu.HBM(z.shape, z.dtype) for z in lands],
                   jax.ShapeDtypeStruct((8, LANE), F32)),
        in_specs=[hbm] * (2 * n), out_specs=(sem, sem, *[hbm] * (2 * n), pl.BlockSpec(memory_space=pltpu.VMEM)),
        input_output_aliases={i: 2 + i for i in range(2 * n)},
        compiler_params=pltpu.CompilerParams(has_side_effects=pltpu.SideEffectType.DATAFLOW_SIDE_EFFECTING),
    )(*srcs, *lands)
    return (outs[0], outs[1], list(outs[2:2 + n]), list(outs[2 + n:2 + 2 * n])), outs[-1][0:1, 0:1]


def _exchange_wait(started, after, name, scatter):
    send_sems, recv_sems, srcs, lands = started
    n = len(srcs)
    npeer = N_DEV - 1

    def body(*refs):
        x_refs, land_refs = refs[:n], refs[n:2 * n]
        send_sems, recv_sems = refs[2 * n], refs[2 * n + 1]
        mx, my, mc = lax.axis_index("x"), lax.axis_index("y"), lax.axis_index("c")
        me = 4 * mx + 2 * my + mc
        for k in range(1, N_DEV):
            _, peer = _peer_of(k)
            for a in range(n):
                src = x_refs[a].at[me] if scatter else x_refs[a]
                cp = pltpu.make_async_remote_copy(
                    src_ref=src, dst_ref=land_refs[a].at[peer], send_sem=send_sems.at[a * npeer + k - 1],
                    recv_sem=recv_sems.at[a * npeer + k - 1], device_id=(mx, my, mc), device_id_type=pl.DeviceIdType.MESH)
                cp.wait_send()
                cp.wait_recv()

    hbm = pl.BlockSpec(memory_space=pltpu.HBM)
    sem = pl.BlockSpec(memory_space=pltpu.SEMAPHORE)
    outs = pl.pallas_call(
        body, name=name,
        out_shape=(*[pltpu.HBM(x.shape, x.dtype) for x in srcs], *[pltpu.HBM(z.shape, z.dtype) for z in lands]),
        in_specs=[hbm] * (2 * n) + [sem, sem, pl.BlockSpec(memory_space=pl.ANY)], out_specs=tuple([hbm] * (2 * n)),
        input_output_aliases={i: i for i in range(2 * n)},
        compiler_params=pltpu.CompilerParams(has_side_effects=pltpu.SideEffectType.DATAFLOW_SIDE_EFFECTING),
    )(*srcs, *lands, send_sems, recv_sems, after)
    me = 4 * lax.axis_index("x") + 2 * lax.axis_index("y") + lax.axis_index("c")
    full = []
    for x, land in zip(outs[:n], outs[n:]):
        own = lax.dynamic_slice(x, (me, 0, 0), (1,) + x.shape[1:]) if scatter else x[None]
        full.append(lax.dynamic_update_slice(land, own, (me, 0, 0)))
    return full


def _sum_slots(x, name):
    _, r, c = x.shape
    tr = _pick(r, (512, 256, 128, 64, 32, 16))

    def body(x_ref, o_ref):
        acc = x_ref[0].astype(F32)
        for s in range(1, N_DEV):
            acc = acc + x_ref[s].astype(F32)
        o_ref[...] = acc

    return pl.pallas_call(
        body, grid=(r // tr,), in_specs=[pl.BlockSpec((N_DEV, tr, c), lambda i: (0, i, 0))],
        out_specs=pl.BlockSpec((tr, c), lambda i: (i, 0)), out_shape=jax.ShapeDtypeStruct((r, c), F32), name=name,
        compiler_params=pltpu.CompilerParams(dimension_semantics=("arbitrary",)),
    )(x)


def _mod_fwd(c_all, w_ada, b_ada_mine):
    def body(c_ref, w_ref, b_ref, o_ref):
        o_ref[...] = _doth(_silu(c_ref[...]), w_ref[...]) + b_ref[...]

    return pl.pallas_call(body, out_shape=jax.ShapeDtypeStruct((c_all.shape[0], w_ada.shape[1]), F32), name="mod_fwd")(c_all, w_ada, b_ada_mine)


def _mod_bwd(c_all_t, dmod_mine):
    def body(ct_ref, d_ref, o_ref):
        s = _silu(ct_ref[...])
        acc = s[:, 0:1] * d_ref[pl.ds(0, 1), :]
        for b in range(1, N_DEV):
            acc = acc + s[:, b:b + 1] * d_ref[pl.ds(b, 1), :]
        o_ref[...] = acc

    return pl.pallas_call(body, out_shape=jax.ShapeDtypeStruct((c_all_t.shape[0], dmod_mine.shape[1]), F32), name="mod_bwd")(c_all_t, dmod_mine)


def _conv_fwd(proj, conv_w8, tm):
    t = proj.shape[0]
    ch = DN_CONV_CH

    def body(x_ref, w_ref, o_ref, buf):
        @pl.when(pl.program_id(0) == 0)
        def _():
            buf[pl.ds(0, CONV_HALO), :] = jnp.zeros((CONV_HALO, ch), F32)

        buf[pl.ds(CONV_HALO, tm), :] = x_ref[...]
        for c0 in range(0, ch, CONV_COLS):
            cols = pl.ds(c0, CONV_COLS)
            w = [w_ref[pl.ds(j, 1), cols] for j in range(CONV_K)]
            for r0 in range(0, tm, CONV_ROWS):
                acc = buf[pl.ds(r0 + CONV_HALO - (CONV_K - 1), CONV_ROWS), cols] * w[0]
                for j in range(1, CONV_K):
                    acc = acc + buf[pl.ds(r0 + CONV_HALO - (CONV_K - 1) + j, CONV_ROWS), cols] * w[j]
                o_ref[pl.ds(r0, CONV_ROWS), cols] = _silu(acc)
        buf[pl.ds(0, CONV_HALO), :] = buf[pl.ds(tm, CONV_HALO), :]

    return pl.pallas_call(
        body, grid=(t // tm,), in_specs=[pl.BlockSpec((tm, ch), lambda i: (i, 0)), _full(conv_w8.shape)],
        out_specs=pl.BlockSpec((tm, ch), lambda i: (i, 0)), out_shape=jax.ShapeDtypeStruct((t, ch), F32),
        scratch_shapes=[pltpu.VMEM((tm + CONV_HALO, ch), F32)], name="conv_fwd",
        compiler_params=pltpu.CompilerParams(dimension_semantics=("arbitrary",)),
    )(proj, conv_w8)


def _conv_bwd(proj, conv_w8, dact, tm):
    t = proj.shape[0]
    ch = DN_CONV_CH
    nt = t // tm
    hb = tm // CONV_HALO

    def body(x_ref, xp_ref, w_ref, dy_ref, dx_ref, dw_ref, xbuf, dbuf):
        step = pl.program_id(0)

        @pl.when(step == 0)
        def _():
            dbuf[pl.ds(tm, CONV_HALO), :] = jnp.zeros((CONV_HALO, ch), F32)
            dw_ref[...] = jnp.zeros_like(dw_ref)

        first = step == nt - 1
        xbuf[pl.ds(0, CONV_HALO), :] = jnp.where(first, 0.0, xp_ref[...])
        xbuf[pl.ds(CONV_HALO, tm), :] = x_ref[...]
        for c0 in range(0, ch, CONV_COLS):
            cols = pl.ds(c0, CONV_COLS)
            w = [w_ref[pl.ds(j, 1), cols] for j in range(CONV_K)]
            dw = [jnp.zeros((1, CONV_COLS), F32) for _ in range(CONV_K)]
            for r0 in range(0, tm, CONV_ROWS):
                xs = [xbuf[pl.ds(r0 + CONV_HALO - (CONV_K - 1) + j, CONV_ROWS), cols] for j in range(CONV_K)]
                pre = xs[0] * w[0]
                for j in range(1, CONV_K):
                    pre = pre + xs[j] * w[j]
                sg = _sigmoid(pre)
                dpre = dy_ref[pl.ds(r0, CONV_ROWS), cols] * (sg * (1.0 + pre * (1.0 - sg)))
                dbuf[pl.ds(r0, CONV_ROWS), cols] = dpre
                dw = [dw[j] + jnp.sum(dpre * xs[j], axis=0, keepdims=True) for j in range(CONV_K)]
            for j in range(CONV_K):
                dw_ref[pl.ds(j, 1), cols] += dw[j]
            for r0 in range(0, tm, CONV_ROWS):
                dx = dbuf[pl.ds(r0 + CONV_K - 1, CONV_ROWS), cols] * w[0]
                for j in range(1, CONV_K):
                    dx = dx + dbuf[pl.ds(r0 + CONV_K - 1 - j, CONV_ROWS), cols] * w[j]
                dx_ref[pl.ds(r0, CONV_ROWS), cols] = dx.astype(dx_ref.dtype)
        dbuf[pl.ds(tm, CONV_HALO), :] = dbuf[pl.ds(0, CONV_HALO), :]

    rev = lambda i: (nt - 1 - i, 0)
    prev = lambda i: (jnp.maximum((nt - 1 - i) * hb - 1, 0), 0)
    return pl.pallas_call(
        body, grid=(nt,),
        in_specs=[pl.BlockSpec((tm, ch), rev), pl.BlockSpec((CONV_HALO, ch), prev), _full(conv_w8.shape),
                  pl.BlockSpec((tm, ch), rev)],
        out_specs=[pl.BlockSpec((tm, ch), rev), _full(conv_w8.shape)],
        out_shape=[jax.ShapeDtypeStruct((t, ch), BF), jax.ShapeDtypeStruct(conv_w8.shape, F32)],
        scratch_shapes=[pltpu.VMEM((tm + CONV_HALO, ch), F32), pltpu.VMEM((tm + CONV_HALO, ch), F32)], name="conv_bwd",
        compiler_params=pltpu.CompilerParams(dimension_semantics=("arbitrary",)),
    )(proj, proj, conv_w8, dact)


BNN = (((2,), (1,)), ((0,), (0,)))
BNT = (((2,), (2,)), ((0,), (0,)))
BTN = (((1,), (1,)), ((0,), (0,)))


def _bdot(a, b, dims, precision=None):
    return lax.dot_general(a, b, dims, precision=precision, preferred_element_type=F32)


@jax.custom_vjp
def _bmmb_nt(a, b):
    return _bdot(a.astype(BF), b.astype(BF), BNT)


def _bmmb_nt_fwd(a, b):
    return _bmmb_nt(a, b), (a, b)


def _bmmb_nt_bwd(res, g):
    a, b = res
    gb = g.astype(BF)
    return _bdot(gb, b.astype(BF), BNN), _bdot(gb, a.astype(BF), BTN)


_bmmb_nt.defvjp(_bmmb_nt_fwd, _bmmb_nt_bwd)


def _unit_lower_solve_fwd(a, r):
    c = a.shape[-1]
    ri = lax.broadcasted_iota(jnp.int32, a.shape, 1)
    ci = lax.broadcasted_iota(jnp.int32, a.shape, 2)
    xm = -a
    inv = (ri == ci).astype(F32) + xm
    for _ in range(int(math.log2(c)) - 1):
        xm = _bdot(xm, xm, BNN, HI)
        inv = inv + _bdot(inv, xm, BNN, HI)
    x = _bdot(inv, r, BNN, HI)
    return x, (inv, x)


def _unit_lower_solve_bwd(res, g):
    inv, x = res
    dr = _bdot(inv, g, BTN, HI)
    return -_bdot(dr, x, BNT, HI), dr


@jax.custom_vjp
def _unit_lower_solve_given(a, r, inv):
    return _bdot(inv, r, BNN, HI)


def _unit_lower_solve_given_fwd(a, r, inv):
    x = _bdot(inv, r, BNN, HI)
    return x, (inv, x)


def _unit_lower_solve_given_bwd(res, g):
    da, dr = _unit_lower_solve_bwd(res, g)
    return da, dr, jnp.zeros_like(res[0])


_unit_lower_solve_given.defvjp(_unit_lower_solve_given_fwd, _unit_lower_solve_given_bwd)


def _gdn_intra(qkv, ba, al8, dt8, inv4=None):
    tm = qkv.shape[0]
    nb = tm // CHUNK
    bsz = DN_HEADS * nb

    def heads(x0):
        return jnp.concatenate([qkv[:, x0 + h * LANE:x0 + (h + 1) * LANE].reshape(nb, CHUNK, LANE) for h in range(DN_HEADS)], axis=0)

    def spread(c0):
        return jnp.concatenate([jnp.broadcast_to(ba[:, c0 + h:c0 + h + 1], (tm, LANE)).reshape(nb, CHUNK, LANE)
                                for h in range(DN_HEADS)], axis=0)

    def per_head(v8):
        return jnp.concatenate([jnp.broadcast_to(v8[0:1, h:h + 1].reshape(1, 1, 1), (nb, 1, LANE)) for h in range(DN_HEADS)], axis=0)

    ri = lax.broadcasted_iota(jnp.int32, (bsz, CHUNK, CHUNK), 1)
    ci = lax.broadcasted_iota(jnp.int32, (bsz, CHUNK, CHUNK), 2)
    incl = ri >= ci
    strict = ri > ci

    q = _l2norm(heads(0)) * (DN_DK ** -0.5)
    k = _l2norm(heads(DN_QK))
    va = heads(2 * DN_QK)
    beta = _sigmoid(spread(0))
    g = -jnp.exp(per_head(al8)) * _softplus(spread(DN_HEADS) + per_head(dt8))
    gc = _bdot(incl.astype(F32), g, BNN, HI)
    g_last = jnp.sum(g, axis=1, keepdims=True)
    gcol = gc[:, :, :CHUNK]
    diff = gcol - jnp.swapaxes(gcol, 1, 2)
    decay = jnp.where(incl, jnp.exp(jnp.where(incl, diff, 0.0)), 0.0)
    kb = k * beta
    a_mat = jnp.where(strict, _bmmb_nt(kb, k) * decay, 0.0)
    egc = jnp.exp(gc)
    rhs = jnp.concatenate([kb * egc, va * beta], axis=2)
    if inv4 is None:
        wu, (inv, _) = _unit_lower_solve_fwd(a_mat, rhs)
    else:
        wu = _unit_lower_solve_given(a_mat, rhs, inv4.reshape(bsz, CHUNK, CHUNK))
    attn = jnp.where(incl, _bmmb_nt(q, k) * decay, 0.0)

    def unheads(x):
        return jnp.concatenate([x[h * nb:(h + 1) * nb].reshape(tm, LANE) for h in range(DN_HEADS)], axis=1)

    out = (unheads(wu[:, :, :DN_DK]), unheads(wu[:, :, DN_DK:]), unheads(q * egc), unheads(k * jnp.exp(g_last - gc)),
           attn.reshape(DN_HEADS, tm, CHUNK), unheads(jnp.broadcast_to(g_last, (bsz, CHUNK, LANE))))
    return out if inv4 is not None else out + (inv.reshape(DN_HEADS, nb, CHUNK, CHUNK),)


def _gdn_scan_step(w, u, qg, kd, att, gl, s):
    v_new = u - _mmb(w, s)
    o = _mmb(qg, s) + _mmb(att, v_new)
    return o, s * jnp.exp(gl) + _mmb_tn(kd, v_new)


def _gdn_intra_specs(t, tm, dts):
    nb = tm // CHUNK
    specs = [pl.BlockSpec((tm, DN_VW), lambda i: (i, 0))] * 4
    specs += [pl.BlockSpec((DN_HEADS, tm, CHUNK), lambda i: (0, i, 0)), pl.BlockSpec((tm, DN_VW), lambda i: (i, 0))]
    shapes = [jax.ShapeDtypeStruct((t, DN_VW), dts[i]) for i in range(4)]
    shapes += [jax.ShapeDtypeStruct((DN_HEADS, t, CHUNK), dts[4]), jax.ShapeDtypeStruct((t, DN_VW), dts[5])]
    return specs, shapes


def _gdn_intra_fwd(qkv, proj, al8, dt8, tm):
    t = qkv.shape[0]

    def body(qkv_ref, ba_ref, al_ref, dt_ref, *outs):
        for o, val in zip(outs, _gdn_intra(qkv_ref[...], ba_ref[...], al_ref[...], dt_ref[...])):
            o[...] = val.astype(o.dtype)

    specs, shapes = _gdn_intra_specs(t, tm, (BF, F32, BF, BF, BF, F32))
    specs.append(_gdn_inverse_spec(tm))
    shapes.append(jax.ShapeDtypeStruct((DN_HEADS, t // CHUNK, CHUNK, CHUNK), F32))
    res = pl.pallas_call(
        body, grid=(t // tm,),
        in_specs=[pl.BlockSpec((tm, DN_CONV_CH), lambda i: (i, 0)), pl.BlockSpec((tm, LANE), lambda i: (i, P_BA // LANE)),
                  _full(al8.shape), _full(dt8.shape)],
        out_specs=specs, out_shape=shapes, name="gdn_intra_fwd",
        compiler_params=pltpu.CompilerParams(dimension_semantics=("parallel",)),
    )(qkv, proj, al8, dt8)
    return res[:6], res[6]


def _gdn_inverse_spec(tm):
    return pl.BlockSpec((DN_HEADS, tm // CHUNK, CHUNK, CHUNK), lambda i: (0, i, 0, 0))


def _gdn_intra_bwd(qkv, proj, al8, dt8, inverses, cts, tm):
    t = qkv.shape[0]

    def body(qkv_ref, ba_ref, al_ref, dt_ref, inv_ref, *refs):
        ct_refs, (dqkv_ref, dba_ref, dal_ref, ddt_ref) = refs[:6], refs[6:]

        @pl.when(pl.program_id(0) == 0)
        def _():
            dal_ref[...] = jnp.zeros_like(dal_ref)
            ddt_ref[...] = jnp.zeros_like(ddt_ref)

        _, vjp = jax.vjp(functools.partial(_gdn_intra, inv4=inv_ref[...]), qkv_ref[...], ba_ref[...], al_ref[...], dt_ref[...])
        dqkv, dba, dal, ddt = vjp(tuple(r[...] for r in ct_refs))
        dqkv_ref[...] = dqkv
        dba_ref[...] = dba.astype(dba_ref.dtype)
        dal_ref[...] += dal
        ddt_ref[...] += ddt

    specs, _ = _gdn_intra_specs(t, tm, (F32,) * 6)
    return pl.pallas_call(
        body, grid=(t // tm,),
        in_specs=[pl.BlockSpec((tm, DN_CONV_CH), lambda i: (i, 0)), pl.BlockSpec((tm, LANE), lambda i: (i, P_BA // LANE)),
                  _full(al8.shape), _full(dt8.shape), _gdn_inverse_spec(tm)] + specs,
        out_specs=[pl.BlockSpec((tm, DN_CONV_CH), lambda i: (i, 0)), pl.BlockSpec((tm, LANE), lambda i: (i, 0)),
                   _full(al8.shape), _full(dt8.shape)],
        out_shape=[jax.ShapeDtypeStruct((t, DN_CONV_CH), F32), jax.ShapeDtypeStruct((t, LANE), BF),
                   jax.ShapeDtypeStruct(al8.shape, F32), jax.ShapeDtypeStruct(dt8.shape, F32)],
        name="gdn_intra_bwd", compiler_params=pltpu.CompilerParams(dimension_semantics=("arbitrary",)),
    )(qkv, proj, al8, dt8, inverses, *cts)


def _gdn_scan_fwd(intra, tm):
    t = intra[0].shape[0]
    nb = tm // CHUNK
    nc = t // CHUNK

    def body(w_ref, u_ref, qg_ref, kd_ref, att_ref, gl_ref, o_ref, ss_ref, s_scr):
        @pl.when(pl.program_id(0) == 0)
        def _():
            s_scr[...] = jnp.zeros_like(s_scr)

        for cc in range(nb):
            rows = pl.ds(cc * CHUNK, CHUNK)
            for h in range(DN_HEADS):
                cols = pl.ds(h * DN_DV, DN_DV)
                s_prev = s_scr[h]
                ss_ref[cc, h] = s_prev
                o, s_new = _gdn_scan_step(w_ref[rows, cols], u_ref[rows, cols], qg_ref[rows, cols], kd_ref[rows, cols],
                                          att_ref[h, rows, :], gl_ref[pl.ds(cc * CHUNK, 1), cols], s_prev)
                o_ref[rows, cols] = o
                s_scr[h] = s_new

    specs, _ = _gdn_intra_specs(t, tm, (F32,) * 6)
    return pl.pallas_call(
        body, grid=(t // tm,), in_specs=specs,
        out_specs=[pl.BlockSpec((tm, DN_VW), lambda i: (i, 0)),
                   pl.BlockSpec((nb, DN_HEADS, DN_DK, DN_DV), lambda i: (i, 0, 0, 0))],
        out_shape=[jax.ShapeDtypeStruct((t, DN_VW), F32), jax.ShapeDtypeStruct((nc, DN_HEADS, DN_DK, DN_DV), F32)],
        scratch_shapes=[pltpu.VMEM((DN_HEADS, DN_DK, DN_DV), F32)], name="gdn_scan_fwd",
        compiler_params=pltpu.CompilerParams(dimension_semantics=("arbitrary",)),
    )(*intra)


def _gdn_scan_bwd(intra, states, do, tm):
    t = intra[0].shape[0]
    nb = tm // CHUNK
    ng = t // tm

    def body(w_ref, u_ref, qg_ref, kd_ref, att_ref, gl_ref, ss_ref, do_ref,
             dw_ref, du_ref, dqg_ref, dkd_ref, datt_ref, dgl_ref, ds_scr):
        @pl.when(pl.program_id(0) == 0)
        def _():
            ds_scr[...] = jnp.zeros_like(ds_scr)

        for cc in reversed(range(nb)):
            rows = pl.ds(cc * CHUNK, CHUNK)
            for h in range(DN_HEADS):
                cols = pl.ds(h * DN_DV, DN_DV)
                f32 = lambda r: r[rows, cols].astype(F32)
                _, vjp = jax.vjp(_gdn_scan_step, f32(w_ref), u_ref[rows, cols], f32(qg_ref), f32(kd_ref),
                                 att_ref[h, rows, :].astype(F32), gl_ref[pl.ds(cc * CHUNK, 1), cols], ss_ref[cc, h])
                dw, du, dqg, dkd, datt, dgl, ds_prev = vjp((do_ref[rows, cols], ds_scr[h]))
                dw_ref[rows, cols] = dw
                du_ref[rows, cols] = du
                dqg_ref[rows, cols] = dqg
                dkd_ref[rows, cols] = dkd
                datt_ref[h, rows, :] = datt
                first_row = lax.broadcasted_iota(jnp.int32, (CHUNK, DN_DV), 0) == 0
                dgl_ref[rows, cols] = jnp.where(first_row, dgl, 0.0)
                ds_scr[h] = ds_prev

    rev = lambda i: (ng - 1 - i, 0)
    rev3 = lambda i: (0, ng - 1 - i, 0)
    row = pl.BlockSpec((tm, DN_VW), rev)
    six = [row] * 4 + [pl.BlockSpec((DN_HEADS, tm, CHUNK), rev3), row]
    _, shapes = _gdn_intra_specs(t, tm, (F32,) * 6)
    return pl.pallas_call(
        body, grid=(ng,),
        in_specs=six + [pl.BlockSpec((nb, DN_HEADS, DN_DK, DN_DV), lambda i: (ng - 1 - i, 0, 0, 0)), row],
        out_specs=six, out_shape=shapes,
        scratch_shapes=[pltpu.VMEM((DN_HEADS, DN_DK, DN_DV), F32)], name="gdn_scan_bwd",
        compiler_params=pltpu.CompilerParams(dimension_semantics=("arbitrary",)),
    )(*intra, states, do)


def _gdn_out(o, z, g):
    parts = []
    for h in range(DN_HEADS):
        sl = slice(h * DN_DV, (h + 1) * DN_DV)
        parts.append(_rmsnorm(o[:, sl], g) * _silu(z[:, sl]))
    return parts


_Q_SCALE = math.log2(math.e) / math.sqrt(QK_NOPE + QK_ROPE)


def _rope_tables(pos, inv_freq2):
    lane = lax.broadcasted_iota(jnp.int32, (1, LANE), 1)
    ang = pos * inv_freq2
    cos = jnp.where(lane < QK_ROPE, jnp.cos(ang), 0.0)
    sin = jnp.where(lane < QK_ROPE // 2, -jnp.sin(ang), jnp.where(lane < QK_ROPE, jnp.sin(ang), 0.0))
    return cos, sin


def _rope_swap():
    ri = lax.broadcasted_iota(jnp.int32, (LANE, LANE), 0)
    ci = lax.broadcasted_iota(jnp.int32, (LANE, LANE), 1)
    half = QK_ROPE // 2
    return (((ci < half) & (ri == ci + half)) | ((ci >= half) & (ci < QK_ROPE) & (ri == ci - half))).astype(F32)


def _mla_prep(cq, ckv, kr, gq, gkv, w_uq, w_ukv, cos, sin, swap):
    rope = lambda u: u * cos + _doth(u, swap) * sin
    q_lin = _mmb_nt(_rmsnorm(cq, gq), w_uq) * _Q_SCALE
    kv_lin = _mmb_nt(_rmsnorm(ckv, gkv), w_ukv)
    k_rope = rope(kr)
    qs, ks, vs = [], [], []
    for h in range(MLA_HEADS):
        qs += [q_lin[:, h * LANE:(h + 1) * LANE], rope(q_lin[:, (MLA_HEADS + h) * LANE:(MLA_HEADS + h + 1) * LANE])]
        ks += [kv_lin[:, 2 * h * LANE:(2 * h + 1) * LANE], k_rope]
        vs += [kv_lin[:, (2 * h + 1) * LANE:(2 * h + 2) * LANE]]
    return qs + ks + vs


def _mla_prep_fwd(proj, pos_col, inv_freq2, gq, gkv, w_uq, w_ukv, tm):
    t = proj.shape[0]
    nq = 2 * MLA_HEADS

    def body(cq_ref, ckv_ref, kr_ref, pos_ref, f_ref, gq_ref, gkv_ref, wq_ref, wkv_ref, q_ref, k_ref, v_ref):
        cos, sin = _rope_tables(pos_ref[...], f_ref[...])
        outs = _mla_prep(cq_ref[...], ckv_ref[...], kr_ref[...], gq_ref[...], gkv_ref[...], wq_ref[...], wkv_ref[...],
                         cos, sin, _rope_swap())
        for i in range(nq):
            q_ref[:, pl.ds(i * LANE, LANE)] = outs[i].astype(q_ref.dtype)
            k_ref[:, pl.ds(i * LANE, LANE)] = outs[nq + i].astype(k_ref.dtype)
        for h in range(MLA_HEADS):
            v_ref[:, pl.ds(h * LANE, LANE)] = outs[2 * nq + h].astype(v_ref.dtype)

    row = lambda w, j: pl.BlockSpec((tm, w), functools.partial(lambda i, j: (i, j), j=j))
    return pl.pallas_call(
        body, grid=(t // tm,),
        in_specs=[row(Q_LORA, P_CQ // Q_LORA), row(KV_LORA, P_CKV // KV_LORA), row(LANE, P_KR // LANE),
                  pl.BlockSpec((tm, 1), lambda i: (i, 0)), _full(inv_freq2.shape), _full(gq.shape), _full(gkv.shape),
                  _full(w_uq.shape), _full(w_ukv.shape)],
        out_specs=[row(nq * LANE, 0), row(nq * LANE, 0), row(MLA_VW, 0)],
        out_shape=[jax.ShapeDtypeStruct((t, nq * LANE), BF), jax.ShapeDtypeStruct((t, nq * LANE), BF),
                   jax.ShapeDtypeStruct((t, MLA_VW), BF)],
        name="mla_prep_fwd", compiler_params=pltpu.CompilerParams(dimension_semantics=("arbitrary",)),
    )(proj, proj, proj, pos_col, inv_freq2, gq, gkv, w_uq, w_ukv)


def _mla_prep_bwd(proj, pos_col, inv_freq2, gq, gkv, w_uq, w_ukv, dq, dk, dv, tm):
    t = proj.shape[0]
    nq = 2 * MLA_HEADS

    def body(cq_ref, ckv_ref, kr_ref, pos_ref, f_ref, gq_ref, gkv_ref, wq_ref, wkv_ref, dq_ref, dk_ref, dv_ref,
             dcq_ref, dckv_ref, dkr_ref, dgq_ref, dgkv_ref, dwq_ref, dwkv_ref):
        @pl.when(pl.program_id(0) == 0)
        def _():
            for o in (dgq_ref, dgkv_ref, dwq_ref, dwkv_ref):
                o[...] = jnp.zeros_like(o)

        cos, sin = _rope_tables(pos_ref[...], f_ref[...])
        f = functools.partial(_mla_prep, cos=cos, sin=sin, swap=_rope_swap())
        _, vjp = jax.vjp(f, cq_ref[...], ckv_ref[...], kr_ref[...], gq_ref[...], gkv_ref[...], wq_ref[...], wkv_ref[...])
        cts = [dq_ref[:, pl.ds(i * LANE, LANE)] for i in range(nq)]
        cts += [dk_ref[:, pl.ds(i * LANE, LANE)] for i in range(nq)]
        cts += [dv_ref[:, pl.ds(h * LANE, LANE)] for h in range(MLA_HEADS)]
        dcq, dckv, dkr, dgq, dgkv, dwq, dwkv = vjp(cts)
        dcq_ref[...] = dcq.astype(dcq_ref.dtype)
        dckv_ref[...] = dckv.astype(dckv_ref.dtype)
        dkr_ref[...] = dkr.astype(dkr_ref.dtype)
        dgq_ref[...] += dgq
        dgkv_ref[...] += dgkv
        dwq_ref[...] += dwq
        dwkv_ref[...] += dwkv

    row = lambda w, j: pl.BlockSpec((tm, w), functools.partial(lambda i, j: (i, j), j=j))
    return pl.pallas_call(
        body, grid=(t // tm,),
        in_specs=[row(Q_LORA, P_CQ // Q_LORA), row(KV_LORA, P_CKV // KV_LORA), row(LANE, P_KR // LANE),
                  pl.BlockSpec((tm, 1), lambda i: (i, 0)), _full(inv_freq2.shape), _full(gq.shape), _full(gkv.shape),
                  _full(w_uq.shape), _full(w_ukv.shape), row(nq * LANE, 0), row(nq * LANE, 0), row(MLA_VW, 0)],
        out_specs=[row(Q_LORA, 0), row(KV_LORA, 0), row(LANE, 0), _full(gq.shape), _full(gkv.shape),
                   _full(w_uq.shape), _full(w_ukv.shape)],
        out_shape=[jax.ShapeDtypeStruct((t, Q_LORA), BF), jax.ShapeDtypeStruct((t, KV_LORA), BF),
                   jax.ShapeDtypeStruct((t, LANE), BF), jax.ShapeDtypeStruct(gq.shape, F32),
                   jax.ShapeDtypeStruct(gkv.shape, F32), jax.ShapeDtypeStruct(w_uq.shape, F32),
                   jax.ShapeDtypeStruct(w_ukv.shape, F32)],
        name="mla_prep_bwd", compiler_params=pltpu.CompilerParams(dimension_semantics=("arbitrary",)),
    )(proj, proj, proj, pos_col, inv_freq2, gq, gkv, w_uq, w_ukv, dq, dk, dv)


_NEG = -1e30
_LN2 = math.log(2.0)
ATT_CHAINS = 2


def _causal(tq, tk, q0, k0):
    row = q0 + lax.broadcasted_iota(jnp.int32, (tq, tk), 0)
    col = k0 + lax.broadcasted_iota(jnp.int32, (tq, tk), 1)
    return col <= row


def _attn_fwd(q, k, v, tq, tk):
    t = q.shape[0]

    assert tk % tq == 0

    th = tq // ATT_CHAINS

    def body(q_ref, k_ref, v_ref, o_ref, lse_ref):
        i = pl.program_id(1)
        n_full = (i * tq) // tk

        def step(k0, carry, masked):
            kt = k_ref[pl.ds(k0, tk), :]
            vt = v_ref[pl.ds(k0, tk), :]
            out = []
            for c, (m, l, acc) in enumerate(carry):
                s = _dot(q_ref[pl.ds(c * th, th), :], kt, NT)
                if masked:
                    s = jnp.where(_causal(th, tk, i * tq + c * th, k0), s, _NEG)
                m_new = jnp.maximum(m, jnp.max(s, axis=-1, keepdims=True))
                p = jnp.exp2(s - m_new)
                alpha = jnp.exp2(m - m_new)
                out.append((m_new, alpha * l + jnp.sum(p, axis=-1, keepdims=True), alpha * acc + _dot(p.astype(BF), vt)))
            return tuple(out)

        init = tuple((jnp.full((th, 1), _NEG, F32), jnp.zeros((th, 1), F32), jnp.zeros((th, V_HEAD), F32)) for _ in range(ATT_CHAINS))
        carry = lax.fori_loop(0, n_full, lambda j, c: step(pl.multiple_of(j * tk, tk), c, False), init)
        for c, (m, l, acc) in enumerate(step(pl.multiple_of(n_full * tk, tk), carry, True)):
            o_ref[pl.ds(c * th, th), :] = acc / l
            lse_ref[pl.ds(c * th, th), :] = jnp.broadcast_to(m + jnp.log2(l), (th, LANE))

    return pl.pallas_call(
        body, grid=(MLA_HEADS, t // tq),
        in_specs=[pl.BlockSpec((tq, 2 * LANE), lambda h, i: (i, h)), pl.BlockSpec((t, 2 * LANE), lambda h, i: (0, h)),
                  pl.BlockSpec((t, V_HEAD), lambda h, i: (0, h))],
        out_specs=[pl.BlockSpec((tq, V_HEAD), lambda h, i: (i, h)), pl.BlockSpec((tq, LANE), lambda h, i: (i, h))],
        out_shape=[jax.ShapeDtypeStruct((t, MLA_VW), F32), jax.ShapeDtypeStruct((t, MLA_HEADS * LANE), F32)],
        name="attn_fwd", compiler_params=pltpu.CompilerParams(dimension_semantics=("parallel", "arbitrary")),
    )(q, k, v)


def _attn_bwd(q, k, v, do, lse, delta, tq, tk):
    t = q.shape[0]
    nkt = t // tk
    assert tk % tq == 0

    def body(q_ref, k_ref, v_ref, do_ref, lse_ref, dl_ref, dq_ref, dk_ref, dv_ref):
        j = pl.program_id(1)

        @pl.when(j == 0)
        def _():
            dq_ref[...] = jnp.zeros_like(dq_ref)

        kt = k_ref[...]
        vt = v_ref[...]

        def step(q0, carry, masked):
            dk, dv = carry
            rows = pl.ds(q0, tq)
            qt = q_ref[rows, :]
            dot_ = do_ref[rows, :]
            p = jnp.exp2(_dot(qt, kt, NT) - lse_ref[rows, pl.ds(0, 1)])
            if masked:
                p = jnp.where(_causal(tq, tk, q0, j * tk), p, 0.0)
            dv = dv + _dot(p.astype(BF), dot_, TN)
            ds = (p * (_dot(dot_, vt, NT) - dl_ref[rows, pl.ds(0, 1)])).astype(BF)
            dk = dk + _dot(ds, qt, TN)
            dq_ref[rows, :] += _dot(ds, kt)
            return dk, dv

        per = tk // tq
        carry = (jnp.zeros((tk, 2 * LANE), F32), jnp.zeros((tk, V_HEAD), F32))
        for dd in range(per):
            carry = step(pl.multiple_of(j * tk + dd * tq, tq), carry, True)

        def group(g, c):
            for dd in range(per):
                c = step(pl.multiple_of(g * tk + dd * tq, tq), c, False)
            return c

        dk, dv = lax.fori_loop(j + 1, nkt, group, carry)
        dk_ref[...] = dk * _LN2
        dv_ref[...] = dv

        @pl.when(j == nkt - 1)
        def _():
            dq_ref[...] = dq_ref[...] * _LN2

    return pl.pallas_call(
        body, grid=(MLA_HEADS, nkt),
        in_specs=[pl.BlockSpec((t, 2 * LANE), lambda h, j: (0, h)), pl.BlockSpec((tk, 2 * LANE), lambda h, j: (j, h)),
                  pl.BlockSpec((tk, V_HEAD), lambda h, j: (j, h)), pl.BlockSpec((t, V_HEAD), lambda h, j: (0, h)),
                  pl.BlockSpec((t, LANE), lambda h, j: (0, h)), pl.BlockSpec((t, LANE), lambda h, j: (0, h))],
        out_specs=[pl.BlockSpec((t, 2 * LANE), lambda h, j: (0, h)), pl.BlockSpec((tk, 2 * LANE), lambda h, j: (j, h)),
                   pl.BlockSpec((tk, V_HEAD), lambda h, j: (j, h))],
        out_shape=[jax.ShapeDtypeStruct((t, MLA_HEADS * 2 * LANE), F32), jax.ShapeDtypeStruct((t, MLA_HEADS * 2 * LANE), F32),
                   jax.ShapeDtypeStruct((t, MLA_VW), F32)],
        name="attn_bwd", compiler_params=pltpu.CompilerParams(dimension_semantics=("parallel", "arbitrary")),
    )(q, k, v, do, lse, delta)


def _adam_update(w, g, m, v):
    mm = ADAM_B1 * m + (1.0 - ADAM_B1) * g
    vv = ADAM_B2 * v + (1.0 - ADAM_B2) * jnp.square(g)
    m_hat = mm / (1.0 - ADAM_B1 ** ADAM_STEP)
    v_hat = vv / (1.0 - ADAM_B2 ** ADAM_STEP)
    return -ADAM_LR * (m_hat / (jnp.sqrt(v_hat) + ADAM_EPS) + ADAM_WD * w), mm, vv


def _adamw(w, g, m, v, name):
    r, c = w.shape
    tr = max([r // s for s in range(1, r // 8 + 1) if r % s == 0 and (r // s) % 8 == 0 and r // s <= 256] or [r])
    slots = g.ndim == 3

    def body(w_ref, g_ref, m_ref, v_ref, g_out, d_ref, nm_ref, nv_ref):
        if slots:
            gg = g_ref[0].astype(F32)
            for s in range(1, N_DEV):
                gg = gg + g_ref[s].astype(F32)
        else:
            gg = g_ref[...]
        g_out[...] = gg
        d_ref[...], nm_ref[...], nv_ref[...] = _adam_update(w_ref[...], gg, m_ref[...], v_ref[...])

    spec = pl.BlockSpec((tr, c), lambda i: (i, 0))
    g_spec = pl.BlockSpec((N_DEV, tr, c), lambda i: (0, i, 0)) if slots else spec
    return pl.pallas_call(
        body, grid=(r // tr,), in_specs=[spec, g_spec, spec, spec], out_specs=[spec] * 4,
        out_shape=[jax.ShapeDtypeStruct((r, c), F32)] * 4, name=name,
        compiler_params=pltpu.CompilerParams(dimension_semantics=("arbitrary",)),
    )(w, g, m, v)


def _adamw_many(ws, gs, ms, vs, name):
    n = len(ws)

    def body(*refs):
        for i in range(n):
            w_ref, g_ref, m_ref, v_ref = (refs[j * n + i] for j in range(4))
            d_ref, nm_ref, nv_ref = (refs[(4 + j) * n + i] for j in range(3))
            d_ref[...], nm_ref[...], nv_ref[...] = _adam_update(w_ref[...], g_ref[...], m_ref[...], v_ref[...])

    shapes = [jax.ShapeDtypeStruct(w.shape, F32) for w in ws]
    outs = pl.pallas_call(body, out_shape=shapes * 3, name=name)(*ws, *gs, *ms, *vs)
    return outs[:n], outs[n:2 * n], outs[2 * n:]


def _cast_bf16(xs, name, after=None):
    n = len(xs)
    extra = [] if after is None else [after]

    def body(*refs):
        outs = refs[n + len(extra):]
        for i in range(n):
            outs[i][...] = refs[i][...].astype(BF)

    vmem = pl.BlockSpec(memory_space=pltpu.VMEM)
    return pl.pallas_call(
        body, out_shape=[jax.ShapeDtypeStruct(x.shape, BF) for x in xs], name=name,
        in_specs=[vmem] * n + [pl.BlockSpec(memory_space=pl.ANY)] * len(extra), out_specs=[vmem] * n)(*xs, *extra)


def _pad_rows(a, n):
    return jnp.pad(a, ((0, n - a.shape[0]), (0, 0)))


def _w_in_to_padded(wt):
    s_ba = P_CQ
    s_cq = s_ba + 2 * DN_HEADS
    s_kr = s_cq + Q_LORA + KV_LORA
    return jnp.concatenate([wt[:s_ba], wt[s_cq:s_kr], _pad_rows(wt[s_ba:s_cq], LANE), _pad_rows(wt[s_kr:], LANE)], axis=0)


def _w_in_from_padded(wt):
    return jnp.concatenate([wt[:P_CQ], wt[P_BA:P_BA + 2 * DN_HEADS], wt[P_CQ:P_BA], wt[P_KR:P_KR + QK_ROPE]], axis=0)


def _w_uq_to_padded(wt):
    w3 = wt.reshape(MLA_HEADS, QK_NOPE + QK_ROPE, Q_LORA)
    nope = w3[:, :QK_NOPE].reshape(MLA_HEADS * QK_NOPE, Q_LORA)
    rope = jnp.pad(w3[:, QK_NOPE:], ((0, 0), (0, LANE - QK_ROPE), (0, 0))).reshape(MLA_HEADS * LANE, Q_LORA)
    return jnp.concatenate([nope, rope], axis=0)


def _w_uq_from_padded(wt):
    nope = wt[:MLA_HEADS * QK_NOPE].reshape(MLA_HEADS, QK_NOPE, Q_LORA)
    rope = wt[MLA_HEADS * QK_NOPE:].reshape(MLA_HEADS, LANE, Q_LORA)[:, :QK_ROPE]
    return jnp.concatenate([nope, rope], axis=1).reshape(MLA_HEADS * (QK_NOPE + QK_ROPE), Q_LORA)


def _pack(pieces, width, row_mult):
    flat = jnp.concatenate([p.reshape(-1) for p in pieces])
    n = flat.shape[0]
    rows = -(-n // (width * row_mult)) * row_mult
    return jnp.pad(flat, (0, rows * width - n)).reshape(rows, width)


def _unpack(flat, shapes):
    out, o = [], 0
    for s in shapes:
        n = math.prod(s)
        out.append(flat[o:o + n].reshape(s))
        o += n
    return out


def kernel(x, c, positions, w_ada, b_ada, w_in, conv_w, a_log, dt_bias, dn_norm_g, q_norm_g, w_uq, kv_norm_g, w_ukv, w_o, ln1_g, ln1_b, w_gate, w_up, w_down, ln2_g, ln2_b, loss_target, m_w_ada, m_b_ada, m_w_in, m_conv_w, m_a_log, m_dt_bias, m_dn_norm_g, m_q_norm_g, m_w_uq, m_kv_norm_g, m_w_ukv, m_w_o, m_ln1_g, m_ln1_b, m_w_gate, m_w_up, m_w_down, m_ln2_g, m_ln2_b, v_w_ada, v_b_ada, v_w_in, v_conv_w, v_a_log, v_dt_bias, v_dn_norm_g, v_q_norm_g, v_w_uq, v_kv_norm_g, v_w_ukv, v_w_o, v_ln1_g, v_ln1_b, v_w_gate, v_w_up, v_w_down, v_ln2_g, v_ln2_b):
    me = 4 * lax.axis_index("x") + 2 * lax.axis_index("y") + lax.axis_index("c")
    t, d = x.shape[1], x.shape[2]
    ada_n = w_ada.shape[2]

    tr = lambda w: w[0].T
    rows = lambda a: a.reshape(-1, a.shape[2])
    (in_shard,) = _cast_bf16([tr(w_in)], "cast_w_in")
    in_gather, token = _exchange_start([in_shard], "gather_w_in_start", scatter=False)
    cw = conv_w.shape[3]
    c_all, conv_all = _exchange([c + token, conv_w[0, :, 0, :]], "gather_small", scatter=False)
    c_all = c_all.reshape(N_DEV, d)
    conv_full = conv_all.transpose(1, 0, 2).reshape(CONV_K, N_DEV * cw)
    conv_w8 = jnp.pad(conv_full, ((0, 8 - CONV_K), (0, 0)))

    b_ada_mine = lax.dynamic_slice(b_ada, (0, me * ada_n), (1, ada_n))
    mod_cols = _mod_fwd(c_all, w_ada[0], b_ada_mine)
    (mod_all,) = _exchange([mod_cols.reshape(N_DEV, 1, ada_n)], "scatter_mod", scatter=True)
    mod = mod_all.reshape(1, N_DEV * ada_n)

    (a_in,) = _exchange_wait(in_gather, mod, "gather_w_in_wait", scatter=False)
    later = _cast_bf16([tr(w_uq), tr(w_ukv), w_o[0], tr(w_gate), tr(w_up), w_down[0]], "cast_weights", after=a_in)
    mixer_gather, token_a = _exchange_start(later[:3], "gather_mixer_weights_start", scatter=False)
    ffn_gather, token_b = _exchange_start(later[3:], "gather_ffn_weights_start", scatter=False)
    mod = mod + (token_a + token_b)
    w_in_t = _w_in_to_padded(rows(a_in))

    def mixer_weights(after):
        a_uq, a_ukv, a_o = _exchange_wait(mixer_gather, after, "gather_mixer_weights_wait", scatter=False)
        return _w_uq_to_padded(rows(a_uq)), rows(a_ukv), rows(a_o)

    def ffn_weights(after):
        a_gate, a_up, a_down = _exchange_wait(ffn_gather, after, "gather_ffn_weights_wait", scatter=False)
        return rows(a_gate), rows(a_up), rows(a_down)

    def by_dest(g):
        return g.reshape(N_DEV, -1, g.shape[1])

    scatters = {}

    def grads_ready(tag, *g):
        if tag == "ffn":
            pieces = [by_dest(a) for a in g]
        elif tag == "mixer":
            g_w_o, g_w_uq_t, g_w_ukv_t = g
            pieces = [by_dest(g_w_o), by_dest(_w_uq_from_padded(g_w_uq_t).astype(BF)), by_dest(g_w_ukv_t.astype(BF))]
        else:
            pieces = [by_dest(_w_in_from_padded(g[0]))]
        scatters[tag], token = _exchange_start(pieces, "scatter_%s_grads_start" % tag, scatter=True)
        return token

    loc = _local_step(x[0], loss_target[0], positions[0], mod, w_in_t, mixer_weights, ffn_weights, grads_ready,
                      conv_w8, a_log, dt_bias, dn_norm_g, q_norm_g, kv_norm_g, ln1_g, ln1_b, ln2_g, ln2_b)
    grad_x, loss_acc, dmod, d_conv8, d_al8, d_dt8, d_dn_g, d_q_g, d_kv_g, d_ln1_g, d_ln1_b, d_ln2_g, d_ln2_b = loc

    small_shapes = [(6 * d,), (CONV_K, N_DEV * cw), (DN_HEADS,), (DN_HEADS,), (DN_DV,), (Q_LORA,), (KV_LORA,), (d,), (d,), (d,), (d,), (1,)]
    gsmall = _pack([dmod, d_conv8[:CONV_K], d_al8[0, :DN_HEADS], d_dt8[0, :DN_HEADS], d_dn_g, d_q_g, d_kv_g,
                    d_ln1_g, d_ln1_b, d_ln2_g, d_ln2_b, loss_acc[0, :1]], LANE, 8)
    (gsmall_all,) = _exchange([gsmall], "gather_small_grads", scatter=False)
    dmod_all = gsmall_all.reshape(N_DEV, -1)[:, :6 * d]
    tot = _unpack(_sum_slots(gsmall_all, "sum_small_grads").reshape(-1), small_shapes)
    g_b_ada, g_conv_full, g_a_log, g_dt_bias, g_dn_g, g_q_g, g_kv_g, g_ln1_g, g_ln1_b, g_ln2_g, g_ln2_b, loss1 = tot
    loss = loss1.reshape(())
    g_conv_w = lax.dynamic_slice(g_conv_full, (0, me * cw), (CONV_K, cw))
    g_w_ada = _mod_bwd(c_all.T, lax.dynamic_slice(dmod_all, (0, me * ada_n), (N_DEV, ada_n)))

    grads = {"w_ada": g_w_ada[None], "b_ada": g_b_ada[None], "conv_w": g_conv_w[None, :, None, :],
             "a_log": g_a_log[None], "dt_bias": g_dt_bias[None], "dn_norm_g": g_dn_g[None], "q_norm_g": g_q_g[None],
             "kv_norm_g": g_kv_g[None], "ln1_g": g_ln1_g[None], "ln1_b": g_ln1_b[None], "ln2_g": g_ln2_g[None], "ln2_b": g_ln2_b[None]}
    weights = dict(w_ada=w_ada, b_ada=b_ada, w_in=w_in, conv_w=conv_w, a_log=a_log, dt_bias=dt_bias, dn_norm_g=dn_norm_g,
                   q_norm_g=q_norm_g, w_uq=w_uq, kv_norm_g=kv_norm_g, w_ukv=w_ukv, w_o=w_o, ln1_g=ln1_g, ln1_b=ln1_b,
                   w_gate=w_gate, w_up=w_up, w_down=w_down, ln2_g=ln2_g, ln2_b=ln2_b)
    ms = dict(w_ada=m_w_ada, b_ada=m_b_ada, w_in=m_w_in, conv_w=m_conv_w, a_log=m_a_log, dt_bias=m_dt_bias,
              dn_norm_g=m_dn_norm_g, q_norm_g=m_q_norm_g, w_uq=m_w_uq, kv_norm_g=m_kv_norm_g, w_ukv=m_w_ukv, w_o=m_w_o,
              ln1_g=m_ln1_g, ln1_b=m_ln1_b, w_gate=m_w_gate, w_up=m_w_up, w_down=m_w_down, ln2_g=m_ln2_g, ln2_b=m_ln2_b)
    vs = dict(w_ada=v_w_ada, b_ada=v_b_ada, w_in=v_w_in, conv_w=v_conv_w, a_log=v_a_log, dt_bias=v_dt_bias,
              dn_norm_g=v_dn_norm_g, q_norm_g=v_q_norm_g, w_uq=v_w_uq, kv_norm_g=v_kv_norm_g, w_ukv=v_w_ukv, w_o=v_w_o,
              ln1_g=v_ln1_g, ln1_b=v_ln1_b, w_gate=v_w_gate, w_up=v_w_up, w_down=v_w_down, ln2_g=v_ln2_g, ln2_b=v_ln2_b)
    names = list(weights)
    big = ("w_ada", "w_gate", "w_up", "w_down", "w_o", "w_uq", "w_ukv", "w_in")
    waits = {"w_gate": ("ffn", ("w_gate", "w_up", "w_down")), "w_o": ("mixer", ("w_o", "w_uq", "w_ukv")), "w_in": ("in", ("w_in",))}
    delta_w, new_m, new_v, slots = {}, {}, {}, {}
    last = g_w_ada
    for n in big:
        if n == "w_in":
            rest = [r for r in names if r not in big]
            flat2 = lambda a: a.reshape(-1, a.shape[-1])
            outs = _adamw_many(*[[flat2(src[r]) for r in rest] for src in (weights, grads, ms, vs)], "adamw_small")
            for dst, o in zip((delta_w, new_m, new_v), outs):
                for r, a in zip(rest, o):
                    dst[r] = a.reshape(weights[r].shape)
            last = outs[0][0]
        transposed = n in ("w_in", "w_uq", "w_ukv", "w_gate", "w_up")
        two = (lambda a: a[0].T) if transposed else (lambda a: a[0])
        back = (lambda a: a.T[None]) if transposed else (lambda a: a[None])
        if n in waits:
            tag, members = waits[n]
            slots.update(zip(members, _exchange_wait(scatters[tag], last, "scatter_%s_grads_wait" % tag, scatter=True)))
        g_in = slots[n] if n in slots else two(grads[n])
        gr, dlt, nm, nv = _adamw(two(weights[n]), g_in, two(ms[n]), two(vs[n]), "adamw_" + n)
        grads[n], delta_w[n], new_m[n], new_v[n] = back(gr), back(dlt), back(nm), back(nv)
        last = nv

    return (loss, grad_x[None], *[grads[n] for n in names], *[delta_w[n] for n in names],
            *[new_m[n] for n in names], *[new_v[n] for n in names])


def _local_step(xs, tgt, pos, mod, w_in_t, mixer_weights, ffn_weights, grads_ready, conv_w8,
                a_log, dt_bias, dn_norm_g, q_norm_g, kv_norm_g, ln1_g, ln1_b, ln2_g, ln2_b):
    t, d = xs.shape
    sh_m, sc_m, gt_m, sh_f, sc_f, gt_f = [mod[:, i * d:(i + 1) * d] for i in range(6)]
    pos_col = pos.astype(F32).reshape(t, 1)
    inv_freq = 1.0 / (ROPE_THETA ** (jnp.arange(0, QK_ROPE, 2, dtype=F32) / QK_ROPE))
    inv_freq2 = jnp.pad(jnp.concatenate([inv_freq, inv_freq]), (0, LANE - QK_ROPE)).reshape(1, LANE)
    al8 = jnp.pad(a_log, ((0, 7), (0, LANE - DN_HEADS)))
    dt8 = jnp.pad(dt_bias, ((0, 7), (0, LANE - DN_HEADS)))

    tm = min(512, t)
    tq = min(256, t)
    tk = min(512, t)

    (h1,) = _rowwise("modulate_in", lambda xx, sc, sh: xx * (1.0 + sc) + sh, [xs], [sc_m, sh_m], [(d, BF)], [], tm)
    proj = _matmul(h1, w_in_t, "nt", "in_proj")
    qkv = _conv_fwd(proj, conv_w8, min(256, t))
    gdn_tm = min(512, t)
    intra, inverses = _gdn_intra_fwd(qkv, proj, al8, dt8, gdn_tm)
    o_dn, states = _gdn_scan_fwd(intra, gdn_tm)
    w_uq_t, w_ukv_t, w_o_f = mixer_weights(states)
    qc, kc, vc = _mla_prep_fwd(proj, pos_col, inv_freq2, q_norm_g, kv_norm_g, w_uq_t, w_ukv_t, tm)
    o_mla, lse = _attn_fwd(qc, kc, vc, tk, tk)

    def mix_in(o, z, om, g):
        return jnp.concatenate(_gdn_out(o, z, g) + [om], axis=1)

    (mixin,) = _rowwise("mixer_out", mix_in, [o_dn, (proj, DN_VW, P_Z // DN_VW), o_mla], [dn_norm_g], [(2 * DN_VW, BF)], [], tm)
    mix = _matmul(mixin, w_o_f, "nn", "out_proj")

    def block1(xx, mx, gt, g1, b1, sc, sh):
        x1 = _layernorm(DEEPNORM_ALPHA * xx + gt * mx, g1, b1)
        return x1, x1 * (1.0 + sc) + sh

    x1, h2 = _rowwise("norm1_modulate", block1, [xs, mix], [gt_m, ln1_g, ln1_b, sc_f, sh_f], [(d, F32), (d, BF)], [], tm)
    w_gate_f, w_up_f, w_down_f = ffn_weights(h2)
    act, gate, up = _ffn_in(h2, w_gate_f, w_up_f)
    ff = _matmul(act, w_down_f, "nn", "ffn_out")

    def tail_loss(x1_, ff_, gt, g2, b2, tg):
        y = _layernorm(DEEPNORM_ALPHA * x1_ + gt * ff_, g2, b2)
        return 0.5 * jnp.sum(jnp.mean(jnp.square(y - tg), axis=-1))

    def tail(x1_, ff_, tg, gt, g2, b2):
        loss, (dx1, dff, dgt, dg2, db2) = jax.value_and_grad(tail_loss, argnums=(0, 1, 2, 3, 4))(x1_, ff_, gt, g2, b2, tg)
        return dx1, dff, jnp.full((1, LANE), loss, F32), dgt, dg2, db2

    dx1_a, dff, loss_acc, d_gt_f, d_ln2_g, d_ln2_b = _rowwise(
        "norm2_loss", tail, [x1, ff, tgt], [gt_f, ln2_g, ln2_b], [(d, F32), (d, BF)], [(1, LANE), (1, d), (1, d), (1, d)], tm)

    g_w_down = _matmul(act, dff, "tn", "d_w_down", BF)
    dgate, dup = _ffn_act_bwd(dff, w_down_f, gate, up)
    g_w_gate = _matmul(dgate, h2, "tn", "d_w_gate", BF)
    g_w_up = _matmul(dup, h2, "tn", "d_w_up", BF)
    token = grads_ready("ffn", g_w_gate, g_w_up, g_w_down)
    dh2 = _matmul2_nn(dgate, w_gate_f, dup, w_up_f, "d_ffn_in")

    def block1_bwd(xx, mx, dx1_, dh2_, gt, g1, b1, sc, sh):
        _, vjp = jax.vjp(block1, xx, mx, gt, g1, b1, sc, sh)
        dxx, dmx, dgt, dg1, db1, dsc, dsh = vjp((dx1_, dh2_))
        return dxx, dmx, dgt, dg1, db1, dsc, dsh

    dx_a, dmix, d_gt_m, d_ln1_g, d_ln1_b, d_sc_f, d_sh_f = _rowwise(
        "norm1_modulate_bwd", block1_bwd, [xs, mix, dx1_a, dh2], [gt_m + token, ln1_g, ln1_b, sc_f, sh_f],
        [(d, F32), (d, BF)], [(1, d)] * 5, min(256, t))

    dmixin = _matmul(dmix, w_o_f, "nt", "d_mixer_out")
    g_w_o = _matmul(mixin, dmix, "tn", "d_w_o", BF)

    def mixer_bwd(o, z, om, dmi, g):
        _, vjp = jax.vjp(lambda o_, z_, g_: jnp.concatenate(_gdn_out(o_, z_, g_), axis=1), o, z, g)
        do_, dz_, dg_ = vjp(dmi[:, :DN_VW])
        dom = dmi[:, DN_VW:]
        delta = [jnp.broadcast_to(jnp.sum(dom[:, h * V_HEAD:(h + 1) * V_HEAD] * om[:, h * V_HEAD:(h + 1) * V_HEAD], axis=-1, keepdims=True), (o.shape[0], LANE))
                 for h in range(MLA_HEADS)]
        return do_, dz_, dom, jnp.concatenate(delta, axis=1), dg_

    do_dn, dz, do_mla, delta, d_dn_g = _rowwise(
        "mixer_out_bwd", mixer_bwd, [o_dn, (proj, DN_VW, P_Z // DN_VW), o_mla, dmixin], [dn_norm_g],
        [(DN_VW, F32), (DN_VW, BF), (MLA_VW, BF), (MLA_HEADS * LANE, F32)], [(1, DN_DV)], tm)

    dqc, dkc, dvc = _attn_bwd(qc, kc, vc, do_mla, lse, delta, tq, tk)
    dcq, dckv, dkr, d_q_g, d_kv_g, g_w_uq_t, g_w_ukv_t = _mla_prep_bwd(
        proj, pos_col, inv_freq2, q_norm_g, kv_norm_g, w_uq_t, w_ukv_t, dqc, dkc, dvc, min(256, t))

    token = grads_ready("mixer", g_w_o, g_w_uq_t, g_w_ukv_t)

    d_intra = _gdn_scan_bwd(intra, states, do_dn, gdn_tm)
    dqkv_act, dba, d_al8, d_dt8 = _gdn_intra_bwd(qkv, proj, al8 + token, dt8, inverses, d_intra, min(256, t))
    dqkv_pre, d_conv8 = _conv_bwd(proj, conv_w8, dqkv_act, min(256, t))

    dproj = jnp.concatenate([dqkv_pre, dz, dcq, dckv, dba, dkr], axis=1)
    dh1 = _matmul(dproj, w_in_t, "nn", "d_in_proj")
    g_w_in_t = _matmul(dproj, h1, "tn", "d_w_in", BF)
    token = grads_ready("in", g_w_in_t)

    def modulate_bwd(xx, dh, dxa, sc):
        return dh * (1.0 + sc) + dxa, jnp.sum(dh * xx, axis=0, keepdims=True), jnp.sum(dh, axis=0, keepdims=True)

    grad_x, d_sc_m, d_sh_m = _rowwise("modulate_in_bwd", modulate_bwd, [xs, dh1, dx_a], [sc_m + token], [(d, F32)], [(1, d), (1, d)], tm)
    dmod = jnp.concatenate([d_sh_m, d_sc_m, d_gt_m, d_sh_f, d_sc_f, d_gt_f], axis=1)
    return grad_x, loss_acc, dmod, d_conv8, d_al8, d_dt8, d_dn_g, d_q_g, d_kv_g, d_ln1_g, d_ln1_b, d_ln2_g, d_ln2_b
```

```python
import functools
import math

import jax
import jax.numpy as jnp
from jax import lax
from jax.experimental import pallas as pl
from jax.experimental.pallas import tpu as pltpu

F32 = jnp.float32
BF = jnp.bfloat16
HI = lax.Precision.HIGHEST

N_DEV = 8
DN_HEADS = 4
DN_DK = 128
DN_DV = 128
CONV_K = 4
CHUNK = 64
MLA_HEADS = 4
QK_NOPE = 128
QK_ROPE = 64
V_HEAD = 128
Q_LORA = 512
KV_LORA = 256
ROPE_THETA = 10000.0
DEPTH = 1
DEEPNORM_ALPHA = (2.0 * DEPTH) ** 0.25
LANE = 128
CONV_HALO = 8
CONV_ROWS, CONV_COLS = 64, 256

DN_QK = DN_HEADS * DN_DK
DN_VW = DN_HEADS * DN_DV
DN_CONV_CH = 2 * DN_QK + DN_VW
MLA_VW = MLA_HEADS * V_HEAD
MLA_QCAT = QK_NOPE + LANE
N_IN = DN_CONV_CH + DN_VW + 2 * DN_HEADS + Q_LORA + KV_LORA + QK_ROPE
P_QKV = 0
P_Z = DN_CONV_CH
P_CQ = P_Z + DN_VW
P_CKV = P_CQ + Q_LORA
P_BA = P_CKV + KV_LORA
P_KR = P_BA + LANE
N_INP = P_KR + LANE

ADAM_LR = 0.001
ADAM_B1 = 0.9
ADAM_B2 = 0.999
ADAM_EPS = 1e-08
ADAM_WD = 0.01
ADAM_STEP = 10

NN = (((1,), (0,)), ((), ()))
NT = (((1,), (1,)), ((), ()))
TN = (((0,), (0,)), ((), ()))


def _pick(n, prefs):
    for p in prefs:
        if n % p == 0:
            return p
    return n


def _full(shape):
    return pl.BlockSpec(shape, lambda *_: (0,) * len(shape))


def _dot(a, b, dims=NN):
    return lax.dot_general(a, b, dims, preferred_element_type=F32)


def _doth(a, b, dims=NN):
    return lax.dot_general(a, b, dims, precision=HI, preferred_element_type=F32)


@jax.custom_vjp
def _mmb(a, b):
    return _dot(a.astype(BF), b.astype(BF), NN)


def _mmb_fwd(a, b):
    return _mmb(a, b), (a, b)


def _mmb_bwd(res, g):
    a, b = res
    gb = g.astype(BF)
    return (_dot(gb, b.astype(BF), NT).astype(a.dtype), _dot(a.astype(BF), gb, TN).astype(b.dtype))


_mmb.defvjp(_mmb_fwd, _mmb_bwd)


@jax.custom_vjp
def _mmb_nt(a, b):
    return _dot(a.astype(BF), b.astype(BF), NT)


def _mmb_nt_fwd(a, b):
    return _mmb_nt(a, b), (a, b)


def _mmb_nt_bwd(res, g):
    a, b = res
    gb = g.astype(BF)
    return (_dot(gb, b.astype(BF), NN).astype(a.dtype), _dot(gb, a.astype(BF), TN).astype(b.dtype))


_mmb_nt.defvjp(_mmb_nt_fwd, _mmb_nt_bwd)


@jax.custom_vjp
def _mmb_tn(a, b):
    return _dot(a.astype(BF), b.astype(BF), TN)


def _mmb_tn_fwd(a, b):
    return _mmb_tn(a, b), (a, b)


def _mmb_tn_bwd(res, g):
    a, b = res
    gb = g.astype(BF)
    return (_dot(b.astype(BF), gb, NT).astype(a.dtype), _dot(a.astype(BF), gb, NN).astype(b.dtype))


_mmb_tn.defvjp(_mmb_tn_fwd, _mmb_tn_bwd)


def _sigmoid(x):
    return 0.5 * (jnp.tanh(0.5 * x) + 1.0)


def _silu(x):
    return x * _sigmoid(x)


def _softplus(x):
    return jnp.maximum(x, 0.0) + jnp.log(1.0 + jnp.exp(-jnp.abs(x)))


def _layernorm(x, g, b, eps=1e-5):
    mu = jnp.mean(x, axis=-1, keepdims=True)
    xc = x - mu
    var = jnp.mean(xc * xc, axis=-1, keepdims=True)
    return xc * lax.rsqrt(var + eps) * g + b


def _rmsnorm(x, g, eps=1e-6):
    return x * lax.rsqrt(jnp.mean(x * x, axis=-1, keepdims=True) + eps) * g


def _l2norm(x, eps=1e-6):
    return x * lax.rsqrt(jnp.sum(x * x, axis=-1, keepdims=True) + eps)


def _rowwise(name, fn, rows, vecs, out_rows, out_accs, tm):
    rows = [r if isinstance(r, tuple) else (r, r.shape[1], 0) for r in rows]
    t = rows[0][0].shape[0]
    tm = min(tm, t)
    assert t % tm == 0
    nr, nv, no = len(rows), len(vecs), len(out_rows)

    def body(*refs):
        ins = [r[...] for r in refs[:nr + nv]]
        outs = fn(*ins)
        outs = outs if isinstance(outs, (tuple, list)) else (outs,)
        o_rows = refs[nr + nv:nr + nv + no]
        o_accs = refs[nr + nv + no:]
        for o, val in zip(o_rows, outs[:no]):
            o[...] = val.astype(o.dtype)
        if o_accs:
            @pl.when(pl.program_id(0) == 0)
            def _():
                for o in o_accs:
                    o[...] = jnp.zeros_like(o)
            for o, val in zip(o_accs, outs[no:]):
                o[...] += val

    in_specs = [pl.BlockSpec((tm, w), functools.partial(lambda i, j: (i, j), j=j)) for (_, w, j) in rows]
    in_specs += [_full(v.shape) for v in vecs]
    out_specs = [pl.BlockSpec((tm, w), lambda i: (i, 0)) for (w, _) in out_rows]
    out_specs += [_full(s) for s in out_accs]
    out_shape = [jax.ShapeDtypeStruct((t, w), d) for (w, d) in out_rows]
    out_shape += [jax.ShapeDtypeStruct(s, F32) for s in out_accs]
    res = pl.pallas_call(
        body, grid=(t // tm,), in_specs=in_specs, out_specs=out_specs, out_shape=out_shape, name=name,
        compiler_params=pltpu.CompilerParams(dimension_semantics=("arbitrary",)),
    )(*[r[0] for r in rows], *vecs)
    return res


def _matmul(a, b, mode, name, out_dtype=F32):
    if mode == "nn":
        (m, k), n = a.shape, b.shape[1]
    elif mode == "nt":
        (m, k), n = a.shape, b.shape[0]
    else:
        (k, m), n = a.shape, b.shape[1]
    tm, tn, tk = _matmul_tiles(m, n, k, a.dtype.itemsize, b.dtype.itemsize, jnp.dtype(out_dtype).itemsize)
    nk = k // tk
    dims = {"nn": NN, "nt": NT, "tn": TN}[mode]

    def body(a_ref, b_ref, o_ref, *acc):
        part = _dot(a_ref[...].astype(BF), b_ref[...].astype(BF), dims)
        if nk == 1:
            o_ref[...] = part.astype(o_ref.dtype)
            return
        (acc_ref,) = acc
        kk = pl.program_id(2)

        @pl.when(kk == 0)
        def _():
            acc_ref[...] = part

        @pl.when(kk > 0)
        def _():
            acc_ref[...] += part

        @pl.when(kk == nk - 1)
        def _():
            o_ref[...] = acc_ref[...].astype(o_ref.dtype)

    a_spec = pl.BlockSpec((tk, tm), lambda i, j, kk: (kk, i)) if mode == "tn" else pl.BlockSpec((tm, tk), lambda i, j, kk: (i, kk))
    b_spec = pl.BlockSpec((tn, tk), lambda i, j, kk: (j, kk)) if mode == "nt" else pl.BlockSpec((tk, tn), lambda i, j, kk: (kk, j))
    return pl.pallas_call(
        body, grid=(m // tm, n // tn, nk), in_specs=[a_spec, b_spec],
        out_specs=pl.BlockSpec((tm, tn), lambda i, j, kk: (i, j)),
        out_shape=jax.ShapeDtypeStruct((m, n), out_dtype),
        scratch_shapes=[pltpu.VMEM((tm, tn), F32)] if nk > 1 else [], name=name,
        compiler_params=pltpu.CompilerParams(dimension_semantics=("parallel", "parallel", "arbitrary")),
    )(a, b)


def _lane_tile(n, cap):
    return max([n // s for s in range(1, n // LANE + 1) if n % s == 0 and (n // s) % LANE == 0 and n // s <= cap] or [n])


def _ffn_in(h, w_gate, w_up):
    m, k = h.shape
    f = w_gate.shape[0]
    tm, tn = _pick(m, (512, 256, 128)), _lane_tile(f, 1408)

    def body(h_ref, wg_ref, wu_ref, act_ref, g_ref, u_ref):
        hh = h_ref[...]
        g = _dot(hh, wg_ref[...], NT)
        u = _dot(hh, wu_ref[...], NT)
        act_ref[...] = (_silu(g) * u).astype(act_ref.dtype)
        g_ref[...] = g.astype(g_ref.dtype)
        u_ref[...] = u.astype(u_ref.dtype)

    w_spec = pl.BlockSpec((tn, k), lambda i, j: (j, 0))
    o_spec = pl.BlockSpec((tm, tn), lambda i, j: (i, j))
    return pl.pallas_call(
        body, grid=(m // tm, f // tn), in_specs=[pl.BlockSpec((tm, k), lambda i, j: (i, 0)), w_spec, w_spec],
        out_specs=[o_spec] * 3, out_shape=[jax.ShapeDtypeStruct((m, f), BF)] * 3, name="ffn_in",
        compiler_params=pltpu.CompilerParams(dimension_semantics=("parallel", "parallel")),
    )(h, w_gate, w_up)


def _ffn_act_bwd(dff, w_down, gate, up):
    m, k = dff.shape
    f = w_down.shape[0]
    tm, tn = _pick(m, (512, 256, 128)), _lane_tile(f, 1408)

    def body(d_ref, w_ref, g_ref, u_ref, dg_ref, du_ref):
        da = _dot(d_ref[...], w_ref[...], NT)
        g = g_ref[...].astype(F32)
        sg = _sigmoid(g)
        dg_ref[...] = (da * u_ref[...].astype(F32) * (sg * (1.0 + g * (1.0 - sg)))).astype(dg_ref.dtype)
        du_ref[...] = (da * (g * sg)).astype(du_ref.dtype)

    o_spec = pl.BlockSpec((tm, tn), lambda i, j: (i, j))
    return pl.pallas_call(
        body, grid=(m // tm, f // tn),
        in_specs=[pl.BlockSpec((tm, k), lambda i, j: (i, 0)), pl.BlockSpec((tn, k), lambda i, j: (j, 0)), o_spec, o_spec],
        out_specs=[o_spec] * 2, out_shape=[jax.ShapeDtypeStruct((m, f), BF)] * 2, name="d_ffn_act",
        compiler_params=pltpu.CompilerParams(dimension_semantics=("parallel", "parallel")),
    )(dff, w_down, gate, up)


def _matmul2_nn(a1, b1, a2, b2, name):
    m, k = a1.shape
    n = b1.shape[1]
    tm, tn = _pick(m, (512, 256, 128)), _pick(n, (512, 256, 128))

    def body(a1_ref, b1_ref, a2_ref, b2_ref, o_ref):
        o_ref[...] = _dot(a1_ref[...], b1_ref[...]) + _dot(a2_ref[...], b2_ref[...])

    a_spec = pl.BlockSpec((tm, k), lambda i, j: (i, 0))
    b_spec = pl.BlockSpec((k, tn), lambda i, j: (0, j))
    return pl.pallas_call(
        body, grid=(m // tm, n // tn), in_specs=[a_spec, b_spec, a_spec, b_spec],
        out_specs=pl.BlockSpec((tm, tn), lambda i, j: (i, j)), out_shape=jax.ShapeDtypeStruct((m, n), F32), name=name,
        compiler_params=pltpu.CompilerParams(dimension_semantics=("parallel", "parallel")),
    )(a1, b1, a2, b2)


MATMUL_VMEM_BUDGET = 28 * 1024 * 1024


def _matmul_tiles(m, n, k, a_bytes, b_bytes, o_bytes):
    def divisors(x, cap):
        return sorted({x // s for s in range(1, 65) if x % s == 0 and (x // s) % LANE == 0 and x // s <= cap}, reverse=True) or [x]

    for tk in divisors(k, k):
        best = None
        for tm in divisors(m, 1024):
            for tn in divisors(n, 2048):
                need = 2 * (tm * tk * a_bytes + tk * tn * b_bytes + tm * tn * o_bytes) + (tm * tn * 4 if tk < k else 0)
                if need <= MATMUL_VMEM_BUDGET and tm * tn >= 512 * 512 and (best is None or tm * tn > best[0] * best[1]):
                    best = (tm, tn)
        if best:
            return best[0], best[1], tk
    return _pick(m, (512, 256, 128)), _pick(n, (512, 256, 128)), _pick(k, (512, 256, 128))


def _exchange(xs, name, scatter):
    n = len(xs)
    npeer = N_DEV - 1

    def body(*refs):
        x_refs, o_refs = refs[:n], refs[n:2 * n]
        send_sems, recv_sems, local_sems = refs[2 * n:]
        mx, my, mc = lax.axis_index("x"), lax.axis_index("y"), lax.axis_index("c")
        me = 4 * mx + 2 * my + mc
        src_me = [x.at[me] if scatter else x for x in x_refs]
        mine = [pltpu.make_async_copy(src_me[a], o_refs[a].at[me], local_sems.at[a]) for a in range(n)]
        for cp in mine:
            cp.start()
        copies = []
        for k in range(1, N_DEV):
            px, py, pc = mx ^ (k >> 2), my ^ ((k >> 1) & 1), mc ^ (k & 1)
            peer = 4 * px + 2 * py + pc
            for a in range(n):
                cp = pltpu.make_async_remote_copy(
                    src_ref=x_refs[a].at[peer] if scatter else x_refs[a], dst_ref=o_refs[a].at[me],
                    send_sem=send_sems.at[a * npeer + k - 1], recv_sem=recv_sems.at[a * npeer + k - 1],
                    device_id=(px, py, pc), device_id_type=pl.DeviceIdType.MESH)
                cp.start()
                copies.append((cp, a, k, peer))
        for cp, a, k, peer in copies:
            pltpu.make_async_remote_copy(
                src_ref=src_me[a], dst_ref=o_refs[a].at[peer], send_sem=send_sems.at[a * npeer + k - 1],
                recv_sem=recv_sems.at[a * npeer + k - 1], device_id=(mx, my, mc),
                device_id_type=pl.DeviceIdType.MESH).wait_recv()
        for cp, _, _, _ in copies:
            cp.wait_send()
        for cp in mine:
            cp.wait()

    return pl.pallas_call(
        body, out_shape=[jax.ShapeDtypeStruct((N_DEV,) + x.shape[-2:], x.dtype) for x in xs],
        in_specs=[pl.BlockSpec(memory_space=pl.ANY)] * n, out_specs=[pl.BlockSpec(memory_space=pl.ANY)] * n,
        scratch_shapes=[pltpu.SemaphoreType.DMA((n * npeer,)), pltpu.SemaphoreType.DMA((n * npeer,)),
                        pltpu.SemaphoreType.DMA((n,))],
        name=name,
    )(*xs)


def _peer_of(k):
    mx, my, mc = lax.axis_index("x"), lax.axis_index("y"), lax.axis_index("c")
    px, py, pc = mx ^ (k >> 2), my ^ ((k >> 1) & 1), mc ^ (k & 1)
    return (px, py, pc), 4 * px + 2 * py + pc


def _exchange_start(xs, name, scatter):
    n = len(xs)
    npeer = N_DEV - 1

    def body(*refs):
        x_refs, land_refs = refs[:n], refs[n:2 * n]
        send_sems, recv_sems, token = refs[2 * n], refs[2 * n + 1], refs[-1]
        me = 4 * lax.axis_index("x") + 2 * lax.axis_index("y") + lax.axis_index("c")
        for k in range(1, N_DEV):
            dev, peer = _peer_of(k)
            for a in range(n):
                pltpu.make_async_remote_copy(
                    src_ref=x_refs[a].at[peer] if scatter else x_refs[a], dst_ref=land_refs[a].at[me],
                    send_sem=send_sems.at[a * npeer + k - 1], recv_sem=recv_sems.at[a * npeer + k - 1],
                    device_id=dev, device_id_type=pl.DeviceIdType.MESH).start()
        token[...] = jnp.zeros_like(token)

    hbm = pl.BlockSpec(memory_space=pltpu.HBM)
    sem = pl.BlockSpec(memory_space=pltpu.SEMAPHORE)
    lands = [pltpu.with_memory_space_constraint(lax.empty((N_DEV,) + x.shape[-2:], x.dtype), pltpu.HBM) for x in xs]
    srcs = [pltpu.with_memory_space_constraint(x, pltpu.HBM) for x in xs]
    outs = pl.pallas_call(
        body, name=name,
        out_shape=(pltpu.SemaphoreType.DMA((n * npeer,)), pltpu.SemaphoreType.DMA((n * npeer,)),
                   *[pltpu.HBM(x.shape, x.dtype) for x in srcs], *[pltpu.HBM(z.shape, z.dtype) for z in lands],
                   jax.ShapeDtypeStruct((8, LANE), F32)),
        in_specs=[hbm] * (2 * n), out_specs=(sem, sem, *[hbm] * (2 * n), pl.BlockSpec(memory_space=pltpu.VMEM)),
        input_output_aliases={i: 2 + i for i in range(2 * n)},
        compiler_params=pltpu.CompilerParams(has_side_effects=pltpu.SideEffectType.DATAFLOW_SIDE_EFFECTING),
    )(*srcs, *lands)
    return (outs[0], outs[1], list(outs[2:2 + n]), list(outs[2 + n:2 + 2 * n])), outs[-1][0:1, 0:1]


def _exchange_wait(started, after, name, scatter):
    send_sems, recv_sems, srcs, lands = started
    n = len(srcs)
    npeer = N_DEV - 1

    def body(*refs):
        x_refs, land_refs = refs[:n], refs[n:2 * n]
        send_sems, recv_sems = refs[2 * n], refs[2 * n + 1]
        mx, my, mc = lax.axis_index("x"), lax.axis_index("y"), lax.axis_index("c")
        me = 4 * mx + 2 * my + mc
        for k in range(1, N_DEV):
            _, peer = _peer_of(k)
            for a in range(n):
                src = x_refs[a].at[me] if scatter else x_refs[a]
                cp = pltpu.make_async_remote_copy(
                    src_ref=src, dst_ref=land_refs[a].at[peer], send_sem=send_sems.at[a * npeer + k - 1],
                    recv_sem=recv_sems.at[a * npeer + k - 1], device_id=(mx, my, mc), device_id_type=pl.DeviceIdType.MESH)
                cp.wait_send()
                cp.wait_recv()

    hbm = pl.BlockSpec(memory_space=pltpu.HBM)
    sem = pl.BlockSpec(memory_space=pltpu.SEMAPHORE)
    outs = pl.pallas_call(
        body, name=name,
        out_shape=(*[pltpu.HBM(x.shape, x.dtype) for x in srcs], *[pltpu.HBM(z.shape, z.dtype) for z in lands]),
        in_specs=[hbm] * (2 * n) + [sem, sem, pl.BlockSpec(memory_space=pl.ANY)], out_specs=tuple([hbm] * (2 * n)),
        input_output_aliases={i: i for i in range(2 * n)},
        compiler_params=pltpu.CompilerParams(has_side_effects=pltpu.SideEffectType.DATAFLOW_SIDE_EFFECTING),
    )(*srcs, *lands, send_sems, recv_sems, after)
    me = 4 * lax.axis_index("x") + 2 * lax.axis_index("y") + lax.axis_index("c")
    full = []
    for x, land in zip(outs[:n], outs[n:]):
        own = lax.dynamic_slice(x, (me, 0, 0), (1,) + x.shape[1:]) if scatter else x[None]
        full.append(lax.dynamic_update_slice(land, own, (me, 0, 0)))
    return full


def _sum_slots(x, name):
    _, r, c = x.shape
    tr = _pick(r, (512, 256, 128, 64, 32, 16))

    def body(x_ref, o_ref):
        acc = x_ref[0].astype(F32)
        for s in range(1, N_DEV):
            acc = acc + x_ref[s].astype(F32)
        o_ref[...] = acc

    return pl.pallas_call(
        body, grid=(r // tr,), in_specs=[pl.BlockSpec((N_DEV, tr, c), lambda i: (0, i, 0))],
        out_specs=pl.BlockSpec((tr, c), lambda i: (i, 0)), out_shape=jax.ShapeDtypeStruct((r, c), F32), name=name,
        compiler_params=pltpu.CompilerParams(dimension_semantics=("arbitrary",)),
    )(x)


def _mod_fwd(c_all, w_ada, b_ada_mine):
    def body(c_ref, w_ref, b_ref, o_ref):
        o_ref[...] = _doth(_silu(c_ref[...]), w_ref[...]) + b_ref[...]

    return pl.pallas_call(body, out_shape=jax.ShapeDtypeStruct((c_all.shape[0], w_ada.shape[1]), F32), name="mod_fwd")(c_all, w_ada, b_ada_mine)


def _mod_bwd(c_all_t, dmod_mine):
    def body(ct_ref, d_ref, o_ref):
        s = _silu(ct_ref[...])
        acc = s[:, 0:1] * d_ref[pl.ds(0, 1), :]
        for b in range(1, N_DEV):
            acc = acc + s[:, b:b + 1] * d_ref[pl.ds(b, 1), :]
        o_ref[...] = acc

    return pl.pallas_call(body, out_shape=jax.ShapeDtypeStruct((c_all_t.shape[0], dmod_mine.shape[1]), F32), name="mod_bwd")(c_all_t, dmod_mine)


def _conv_fwd(proj, conv_w8, tm):
    t = proj.shape[0]
    ch = DN_CONV_CH

    def body(x_ref, w_ref, o_ref, buf):
        @pl.when(pl.program_id(0) == 0)
        def _():
            buf[pl.ds(0, CONV_HALO), :] = jnp.zeros((CONV_HALO, ch), F32)

        buf[pl.ds(CONV_HALO, tm), :] = x_ref[...]
        for c0 in range(0, ch, CONV_COLS):
            cols = pl.ds(c0, CONV_COLS)
            w = [w_ref[pl.ds(j, 1), cols] for j in range(CONV_K)]
            for r0 in range(0, tm, CONV_ROWS):
                acc = buf[pl.ds(r0 + CONV_HALO - (CONV_K - 1), CONV_ROWS), cols] * w[0]
                for j in range(1, CONV_K):
                    acc = acc + buf[pl.ds(r0 + CONV_HALO - (CONV_K - 1) + j, CONV_ROWS), cols] * w[j]
                o_ref[pl.ds(r0, CONV_ROWS), cols] = _silu(acc)
        buf[pl.ds(0, CONV_HALO), :] = buf[pl.ds(tm, CONV_HALO), :]

    return pl.pallas_call(
        body, grid=(t // tm,), in_specs=[pl.BlockSpec((tm, ch), lambda i: (i, 0)), _full(conv_w8.shape)],
        out_specs=pl.BlockSpec((tm, ch), lambda i: (i, 0)), out_shape=jax.ShapeDtypeStruct((t, ch), F32),
        scratch_shapes=[pltpu.VMEM((tm + CONV_HALO, ch), F32)], name="conv_fwd",
        compiler_params=pltpu.CompilerParams(dimension_semantics=("arbitrary",)),
    )(proj, conv_w8)


def _conv_bwd(proj, conv_w8, dact, tm):
    t = proj.shape[0]
    ch = DN_CONV_CH
    nt = t // tm
    hb = tm // CONV_HALO

    def body(x_ref, xp_ref, w_ref, dy_ref, dx_ref, dw_ref, xbuf, dbuf):
        step = pl.program_id(0)

        @pl.when(step == 0)
        def _():
            dbuf[pl.ds(tm, CONV_HALO), :] = jnp.zeros((CONV_HALO, ch), F32)
            dw_ref[...] = jnp.zeros_like(dw_ref)

        first = step == nt - 1
        xbuf[pl.ds(0, CONV_HALO), :] = jnp.where(first, 0.0, xp_ref[...])
        xbuf[pl.ds(CONV_HALO, tm), :] = x_ref[...]
        for c0 in range(0, ch, CONV_COLS):
            cols = pl.ds(c0, CONV_COLS)
            w = [w_ref[pl.ds(j, 1), cols] for j in range(CONV_K)]
            dw = [jnp.zeros((1, CONV_COLS), F32) for _ in range(CONV_K)]
            for r0 in range(0, tm, CONV_ROWS):
                xs = [xbuf[pl.ds(r0 + CONV_HALO - (CONV_K - 1) + j, CONV_ROWS), cols] for j in range(CONV_K)]
                pre = xs[0] * w[0]
                for j in range(1, CONV_K):
                    pre = pre + xs[j] * w[j]
                sg = _sigmoid(pre)
                dpre = dy_ref[pl.ds(r0, CONV_ROWS), cols] * (sg * (1.0 + pre * (1.0 - sg)))
                dbuf[pl.ds(r0, CONV_ROWS), cols] = dpre
                dw = [dw[j] + jnp.sum(dpre * xs[j], axis=0, keepdims=True) for j in range(CONV_K)]
            for j in range(CONV_K):
                dw_ref[pl.ds(j, 1), cols] += dw[j]
            for r0 in range(0, tm, CONV_ROWS):
                dx = dbuf[pl.ds(r0 + CONV_K - 1, CONV_ROWS), cols] * w[0]
                for j in range(1, CONV_K):
                    dx = dx + dbuf[pl.ds(r0 + CONV_K - 1 - j, CONV_ROWS), cols] * w[j]
                dx_ref[pl.ds(r0, CONV_ROWS), cols] = dx.astype(dx_ref.dtype)
        dbuf[pl.ds(tm, CONV_HALO), :] = dbuf[pl.ds(0, CONV_HALO), :]

    rev = lambda i: (nt - 1 - i, 0)
    prev = lambda i: (jnp.maximum((nt - 1 - i) * hb - 1, 0), 0)
    return pl.pallas_call(
        body, grid=(nt,),
        in_specs=[pl.BlockSpec((tm, ch), rev), pl.BlockSpec((CONV_HALO, ch), prev), _full(conv_w8.shape),
                  pl.BlockSpec((tm, ch), rev)],
        out_specs=[pl.BlockSpec((tm, ch), rev), _full(conv_w8.shape)],
        out_shape=[jax.ShapeDtypeStruct((t, ch), BF), jax.ShapeDtypeStruct(conv_w8.shape, F32)],
        scratch_shapes=[pltpu.VMEM((tm + CONV_HALO, ch), F32), pltpu.VMEM((tm + CONV_HALO, ch), F32)], name="conv_bwd",
        compiler_params=pltpu.CompilerParams(dimension_semantics=("arbitrary",)),
    )(proj, proj, conv_w8, dact)


BNN = (((2,), (1,)), ((0,), (0,)))
BNT = (((2,), (2,)), ((0,), (0,)))
BTN = (((1,), (1,)), ((0,), (0,)))


def _bdot(a, b, dims, precision=None):
    return lax.dot_general(a, b, dims, precision=precision, preferred_element_type=F32)


@jax.custom_vjp
def _bmmb_nt(a, b):
    return _bdot(a.astype(BF), b.astype(BF), BNT)


def _bmmb_nt_fwd(a, b):
    return _bmmb_nt(a, b), (a, b)


def _bmmb_nt_bwd(res, g):
    a, b = res
    gb = g.astype(BF)
    return _bdot(gb, b.astype(BF), BNN), _bdot(gb, a.astype(BF), BTN)


_bmmb_nt.defvjp(_bmmb_nt_fwd, _bmmb_nt_bwd)


def _unit_lower_solve_fwd(a, r):
    c = a.shape[-1]
    ri = lax.broadcasted_iota(jnp.int32, a.shape, 1)
    ci = lax.broadcasted_iota(jnp.int32, a.shape, 2)
    xm = -a
    inv = (ri == ci).astype(F32) + xm
    for _ in range(int(math.log2(c)) - 1):
        xm = _bdot(xm, xm, BNN, HI)
        inv = inv + _bdot(inv, xm, BNN, HI)
    x = _bdot(inv, r, BNN, HI)
    return x, (inv, x)


def _unit_lower_solve_bwd(res, g):
    inv, x = res
    dr = _bdot(inv, g, BTN, HI)
    return -_bdot(dr, x, BNT, HI), dr


@jax.custom_vjp
def _unit_lower_solve_given(a, r, inv):
    return _bdot(inv, r, BNN, HI)


def _unit_lower_solve_given_fwd(a, r, inv):
    x = _bdot(inv, r, BNN, HI)
    return x, (inv, x)


def _unit_lower_solve_given_bwd(res, g):
    da, dr = _unit_lower_solve_bwd(res, g)
    return da, dr, jnp.zeros_like(res[0])


_unit_lower_solve_given.defvjp(_unit_lower_solve_given_fwd, _unit_lower_solve_given_bwd)


def _gdn_intra(qkv, ba, al8, dt8, inv4=None):
    tm = qkv.shape[0]
    nb = tm // CHUNK
    bsz = DN_HEADS * nb

    def heads(x0):
        return jnp.concatenate([qkv[:, x0 + h * LANE:x0 + (h + 1) * LANE].reshape(nb, CHUNK, LANE) for h in range(DN_HEADS)], axis=0)

    def spread(c0):
        return jnp.concatenate([jnp.broadcast_to(ba[:, c0 + h:c0 + h + 1], (tm, LANE)).reshape(nb, CHUNK, LANE)
                                for h in range(DN_HEADS)], axis=0)

    def per_head(v8):
        return jnp.concatenate([jnp.broadcast_to(v8[0:1, h:h + 1].reshape(1, 1, 1), (nb, 1, LANE)) for h in range(DN_HEADS)], axis=0)

    ri = lax.broadcasted_iota(jnp.int32, (bsz, CHUNK, CHUNK), 1)
    ci = lax.broadcasted_iota(jnp.int32, (bsz, CHUNK, CHUNK), 2)
    incl = ri >= ci
    strict = ri > ci

    q = _l2norm(heads(0)) * (DN_DK ** -0.5)
    k = _l2norm(heads(DN_QK))
    va = heads(2 * DN_QK)
    beta = _sigmoid(spread(0))
    g = -jnp.exp(per_head(al8)) * _softplus(spread(DN_HEADS) + per_head(dt8))
    gc = _bdot(incl.astype(F32), g, BNN, HI)
    g_last = jnp.sum(g, axis=1, keepdims=True)
    gcol = gc[:, :, :CHUNK]
    diff = gcol - jnp.swapaxes(gcol, 1, 2)
    decay = jnp.where(incl, jnp.exp(jnp.where(incl, diff, 0.0)), 0.0)
    kb = k * beta
    a_mat = jnp.where(strict, _bmmb_nt(kb, k) * decay, 0.0)
    egc = jnp.exp(gc)
    rhs = jnp.concatenate([kb * egc, va * beta], axis=2)
    if inv4 is None:
        wu, (inv, _) = _unit_lower_solve_fwd(a_mat, rhs)
    else:
        wu = _unit_lower_solve_given(a_mat, rhs, inv4.reshape(bsz, CHUNK, CHUNK))
    attn = jnp.where(incl, _bmmb_nt(q, k) * decay, 0.0)

    def unheads(x):
        return jnp.concatenate([x[h * nb:(h + 1) * nb].reshape(tm, LANE) for h in range(DN_HEADS)], axis=1)

    out = (unheads(wu[:, :, :DN_DK]), unheads(wu[:, :, DN_DK:]), unheads(q * egc), unheads(k * jnp.exp(g_last - gc)),
           attn.reshape(DN_HEADS, tm, CHUNK), unheads(jnp.broadcast_to(g_last, (bsz, CHUNK, LANE))))
    return out if inv4 is not None else out + (inv.reshape(DN_HEADS, nb, CHUNK, CHUNK),)


def _gdn_scan_step(w, u, qg, kd, att, gl, s):
    v_new = u - _mmb(w, s)
    o = _mmb(qg, s) + _mmb(att, v_new)
    return o, s * jnp.exp(gl) + _mmb_tn(kd, v_new)


def _gdn_intra_specs(t, tm, dts):
    nb = tm // CHUNK
    specs = [pl.BlockSpec((tm, DN_VW), lambda i: (i, 0))] * 4
    specs += [pl.BlockSpec((DN_HEADS, tm, CHUNK), lambda i: (0, i, 0)), pl.BlockSpec((tm, DN_VW), lambda i: (i, 0))]
    shapes = [jax.ShapeDtypeStruct((t, DN_VW), dts[i]) for i in range(4)]
    shapes += [jax.ShapeDtypeStruct((DN_HEADS, t, CHUNK), dts[4]), jax.ShapeDtypeStruct((t, DN_VW), dts[5])]
    return specs, shapes


def _gdn_intra_fwd(qkv, proj, al8, dt8, tm):
    t = qkv.shape[0]

    def body(qkv_ref, ba_ref, al_ref, dt_ref, *outs):
        for o, val in zip(outs, _gdn_intra(qkv_ref[...], ba_ref[...], al_ref[...], dt_ref[...])):
            o[...] = val.astype(o.dtype)

    specs, shapes = _gdn_intra_specs(t, tm, (BF, F32, BF, BF, BF, F32))
    specs.append(_gdn_inverse_spec(tm))
    shapes.append(jax.ShapeDtypeStruct((DN_HEADS, t // CHUNK, CHUNK, CHUNK), F32))
    res = pl.pallas_call(
        body, grid=(t // tm,),
        in_specs=[pl.BlockSpec((tm, DN_CONV_CH), lambda i: (i, 0)), pl.BlockSpec((tm, LANE), lambda i: (i, P_BA // LANE)),
                  _full(al8.shape), _full(dt8.shape)],
        out_specs=specs, out_shape=shapes, name="gdn_intra_fwd",
        compiler_params=pltpu.CompilerParams(dimension_semantics=("parallel",)),
    )(qkv, proj, al8, dt8)
    return res[:6], res[6]


def _gdn_inverse_spec(tm):
    return pl.BlockSpec((DN_HEADS, tm // CHUNK, CHUNK, CHUNK), lambda i: (0, i, 0, 0))


def _gdn_intra_bwd(qkv, proj, al8, dt8, inverses, cts, tm):
    t = qkv.shape[0]

    def body(qkv_ref, ba_ref, al_ref, dt_ref, inv_ref, *refs):
        ct_refs, (dqkv_ref, dba_ref, dal_ref, ddt_ref) = refs[:6], refs[6:]

        @pl.when(pl.program_id(0) == 0)
        def _():
            dal_ref[...] = jnp.zeros_like(dal_ref)
            ddt_ref[...] = jnp.zeros_like(ddt_ref)

        _, vjp = jax.vjp(functools.partial(_gdn_intra, inv4=inv_ref[...]), qkv_ref[...], ba_ref[...], al_ref[...], dt_ref[...])
        dqkv, dba, dal, ddt = vjp(tuple(r[...] for r in ct_refs))
        dqkv_ref[...] = dqkv
        dba_ref[...] = dba.astype(dba_ref.dtype)
        dal_ref[...] += dal
        ddt_ref[...] += ddt

    specs, _ = _gdn_intra_specs(t, tm, (F32,) * 6)
    return pl.pallas_call(
        body, grid=(t // tm,),
        in_specs=[pl.BlockSpec((tm, DN_CONV_CH), lambda i: (i, 0)), pl.BlockSpec((tm, LANE), lambda i: (i, P_BA // LANE)),
                  _full(al8.shape), _full(dt8.shape), _gdn_inverse_spec(tm)] + specs,
        out_specs=[pl.BlockSpec((tm, DN_CONV_CH), lambda i: (i, 0)), pl.BlockSpec((tm, LANE), lambda i: (i, 0)),
                   _full(al8.shape), _full(dt8.shape)],
        out_shape=[jax.ShapeDtypeStruct((t, DN_CONV_CH), F32), jax.ShapeDtypeStruct((t, LANE), BF),
                   jax.ShapeDtypeStruct(al8.shape, F32), jax.ShapeDtypeStruct(dt8.shape, F32)],
        name="gdn_intra_bwd", compiler_params=pltpu.CompilerParams(dimension_semantics=("arbitrary",)),
    )(qkv, proj, al8, dt8, inverses, *cts)


def _gdn_scan_fwd(intra, tm):
    t = intra[0].shape[0]
    nb = tm // CHUNK
    nc = t // CHUNK

    def body(w_ref, u_ref, qg_ref, kd_ref, att_ref, gl_ref, o_ref, ss_ref, s_scr):
        @pl.when(pl.program_id(0) == 0)
        def _():
            s_scr[...] = jnp.zeros_like(s_scr)

        for cc in range(nb):
            rows = pl.ds(cc * CHUNK, CHUNK)
            for h in range(DN_HEADS):
                cols = pl.ds(h * DN_DV, DN_DV)
                s_prev = s_scr[h]
                ss_ref[cc, h] = s_prev
                o, s_new = _gdn_scan_step(w_ref[rows, cols], u_ref[rows, cols], qg_ref[rows, cols], kd_ref[rows, cols],
                                          att_ref[h, rows, :], gl_ref[pl.ds(cc * CHUNK, 1), cols], s_prev)
                o_ref[rows, cols] = o
                s_scr[h] = s_new

    specs, _ = _gdn_intra_specs(t, tm, (F32,) * 6)
    return pl.pallas_call(
        body, grid=(t // tm,), in_specs=specs,
        out_specs=[pl.BlockSpec((tm, DN_VW), lambda i: (i, 0)),
                   pl.BlockSpec((nb, DN_HEADS, DN_DK, DN_DV), lambda i: (i, 0, 0, 0))],
        out_shape=[jax.ShapeDtypeStruct((t, DN_VW), F32), jax.ShapeDtypeStruct((nc, DN_HEADS, DN_DK, DN_DV), F32)],
        scratch_shapes=[pltpu.VMEM((DN_HEADS, DN_DK, DN_DV), F32)], name="gdn_scan_fwd",
        compiler_params=pltpu.CompilerParams(dimension_semantics=("arbitrary",)),
    )(*intra)


def _gdn_scan_bwd(intra, states, do, tm):
    t = intra[0].shape[0]
    nb = tm // CHUNK
    ng = t // tm

    def body(w_ref, u_ref, qg_ref, kd_ref, att_ref, gl_ref, ss_ref, do_ref,
             dw_ref, du_ref, dqg_ref, dkd_ref, datt_ref, dgl_ref, ds_scr):
        @pl.when(pl.program_id(0) == 0)
        def _():
            ds_scr[...] = jnp.zeros_like(ds_scr)

        for cc in reversed(range(nb)):
            rows = pl.ds(cc * CHUNK, CHUNK)
            for h in range(DN_HEADS):
                cols = pl.ds(h * DN_DV, DN_DV)
                f32 = lambda r: r[rows, cols].astype(F32)
                _, vjp = jax.vjp(_gdn_scan_step, f32(w_ref), u_ref[rows, cols], f32(qg_ref), f32(kd_ref),
                                 att_ref[h, rows, :].astype(F32), gl_ref[pl.ds(cc * CHUNK, 1), cols], ss_ref[cc, h])
                dw, du, dqg, dkd, datt, dgl, ds_prev = vjp((do_ref[rows, cols], ds_scr[h]))
                dw_ref[rows, cols] = dw
                du_ref[rows, cols] = du
                dqg_ref[rows, cols] = dqg
                dkd_ref[rows, cols] = dkd
                datt_ref[h, rows, :] = datt
                first_row = lax.broadcasted_iota(jnp.int32, (CHUNK, DN_DV), 0) == 0
                dgl_ref[rows, cols] = jnp.where(first_row, dgl, 0.0)
                ds_scr[h] = ds_prev

    rev = lambda i: (ng - 1 - i, 0)
    rev3 = lambda i: (0, ng - 1 - i, 0)
    row = pl.BlockSpec((tm, DN_VW), rev)
    six = [row] * 4 + [pl.BlockSpec((DN_HEADS, tm, CHUNK), rev3), row]
    _, shapes = _gdn_intra_specs(t, tm, (F32,) * 6)
    return pl.pallas_call(
        body, grid=(ng,),
        in_specs=six + [pl.BlockSpec((nb, DN_HEADS, DN_DK, DN_DV), lambda i: (ng - 1 - i, 0, 0, 0)), row],
        out_specs=six, out_shape=shapes,
        scratch_shapes=[pltpu.VMEM((DN_HEADS, DN_DK, DN_DV), F32)], name="gdn_scan_bwd",
        compiler_params=pltpu.CompilerParams(dimension_semantics=("arbitrary",)),
    )(*intra, states, do)


def _gdn_out(o, z, g):
    parts = []
    for h in range(DN_HEADS):
        sl = slice(h * DN_DV, (h + 1) * DN_DV)
        parts.append(_rmsnorm(o[:, sl], g) * _silu(z[:, sl]))
    return parts


_Q_SCALE = math.log2(math.e) / math.sqrt(QK_NOPE + QK_ROPE)


def _rope_tables(pos, inv_freq2):
    lane = lax.broadcasted_iota(jnp.int32, (1, LANE), 1)
    ang = pos * inv_freq2
    cos = jnp.where(lane < QK_ROPE, jnp.cos(ang), 0.0)
    sin = jnp.where(lane < QK_ROPE // 2, -jnp.sin(ang), jnp.where(lane < QK_ROPE, jnp.sin(ang), 0.0))
    return cos, sin


def _rope_swap():
    ri = lax.broadcasted_iota(jnp.int32, (LANE, LANE), 0)
    ci = lax.broadcasted_iota(jnp.int32, (LANE, LANE), 1)
    half = QK_ROPE // 2
    return (((ci < half) & (ri == ci + half)) | ((ci >= half) & (ci < QK_ROPE) & (ri == ci - half))).astype(F32)


def _mla_prep(cq, ckv, kr, gq, gkv, w_uq, w_ukv, cos, sin, swap):
    rope = lambda u: u * cos + _doth(u, swap) * sin
    q_lin = _mmb_nt(_rmsnorm(cq, gq), w_uq) * _Q_SCALE
    kv_lin = _mmb_nt(_rmsnorm(ckv, gkv), w_ukv)
    k_rope = rope(kr)
    qs, ks, vs = [], [], []
    for h in range(MLA_HEADS):
        qs += [q_lin[:, h * LANE:(h + 1) * LANE], rope(q_lin[:, (MLA_HEADS + h) * LANE:(MLA_HEADS + h + 1) * LANE])]
        ks += [kv_lin[:, 2 * h * LANE:(2 * h + 1) * LANE], k_rope]
        vs += [kv_lin[:, (2 * h + 1) * LANE:(2 * h + 2) * LANE]]
    return qs + ks + vs


def _mla_prep_fwd(proj, pos_col, inv_freq2, gq, gkv, w_uq, w_ukv, tm):
    t = proj.shape[0]
    nq = 2 * MLA_HEADS

    def body(cq_ref, ckv_ref, kr_ref, pos_ref, f_ref, gq_ref, gkv_ref, wq_ref, wkv_ref, q_ref, k_ref, v_ref):
        cos, sin = _rope_tables(pos_ref[...], f_ref[...])
        outs = _mla_prep(cq_ref[...], ckv_ref[...], kr_ref[...], gq_ref[...], gkv_ref[...], wq_ref[...], wkv_ref[...],
                         cos, sin, _rope_swap())
        for i in range(nq):
            q_ref[:, pl.ds(i * LANE, LANE)] = outs[i].astype(q_ref.dtype)
            k_ref[:, pl.ds(i * LANE, LANE)] = outs[nq + i].astype(k_ref.dtype)
        for h in range(MLA_HEADS):
            v_ref[:, pl.ds(h * LANE, LANE)] = outs[2 * nq + h].astype(v_ref.dtype)

    row = lambda w, j: pl.BlockSpec((tm, w), functools.partial(lambda i, j: (i, j), j=j))
    return pl.pallas_call(
        body, grid=(t // tm,),
        in_specs=[row(Q_LORA, P_CQ // Q_LORA), row(KV_LORA, P_CKV // KV_LORA), row(LANE, P_KR // LANE),
                  pl.BlockSpec((tm, 1), lambda i: (i, 0)), _full(inv_freq2.shape), _full(gq.shape), _full(gkv.shape),
                  _full(w_uq.shape), _full(w_ukv.shape)],
        out_specs=[row(nq * LANE, 0), row(nq * LANE, 0), row(MLA_VW, 0)],
        out_shape=[jax.ShapeDtypeStruct((t, nq * LANE), BF), jax.ShapeDtypeStruct((t, nq * LANE), BF),
                   jax.ShapeDtypeStruct((t, MLA_VW), BF)],
        name="mla_prep_fwd", compiler_params=pltpu.CompilerParams(dimension_semantics=("arbitrary",)),
    )(proj, proj, proj, pos_col, inv_freq2, gq, gkv, w_uq, w_ukv)


def _mla_prep_bwd(proj, pos_col, inv_freq2, gq, gkv, w_uq, w_ukv, dq, dk, dv, tm):
    t = proj.shape[0]
    nq = 2 * MLA_HEADS

    def body(cq_ref, ckv_ref, kr_ref, pos_ref, f_ref, gq_ref, gkv_ref, wq_ref, wkv_ref, dq_ref, dk_ref, dv_ref,
             dcq_ref, dckv_ref, dkr_ref, dgq_ref, dgkv_ref, dwq_ref, dwkv_ref):
        @pl.when(pl.program_id(0) == 0)
        def _():
            for o in (dgq_ref, dgkv_ref, dwq_ref, dwkv_ref):
                o[...] = jnp.zeros_like(o)

        cos, sin = _rope_tables(pos_ref[...], f_ref[...])
        f = functools.partial(_mla_prep, cos=cos, sin=sin, swap=_rope_swap())
        _, vjp = jax.vjp(f, cq_ref[...], ckv_ref[...], kr_ref[...], gq_ref[...], gkv_ref[...], wq_ref[...], wkv_ref[...])
        cts = [dq_ref[:, pl.ds(i * LANE, LANE)] for i in range(nq)]
        cts += [dk_ref[:, pl.ds(i * LANE, LANE)] for i in range(nq)]
        cts += [dv_ref[:, pl.ds(h * LANE, LANE)] for h in range(MLA_HEADS)]
        dcq, dckv, dkr, dgq, dgkv, dwq, dwkv = vjp(cts)
        dcq_ref[...] = dcq.astype(dcq_ref.dtype)
        dckv_ref[...] = dckv.astype(dckv_ref.dtype)
        dkr_ref[...] = dkr.astype(dkr_ref.dtype)
        dgq_ref[...] += dgq
        dgkv_ref[...] += dgkv
        dwq_ref[...] += dwq
        dwkv_ref[...] += dwkv

    row = lambda w, j: pl.BlockSpec((tm, w), functools.partial(lambda i, j: (i, j), j=j))
    return pl.pallas_call(
        body, grid=(t // tm,),
        in_specs=[row(Q_LORA, P_CQ // Q_LORA), row(KV_LORA, P_CKV // KV_LORA), row(LANE, P_KR // LANE),
                  pl.BlockSpec((tm, 1), lambda i: (i, 0)), _full(inv_freq2.shape), _full(gq.shape), _full(gkv.shape),
                  _full(w_uq.shape), _full(w_ukv.shape), row(nq * LANE, 0), row(nq * LANE, 0), row(MLA_VW, 0)],
        out_specs=[row(Q_LORA, 0), row(KV_LORA, 0), row(LANE, 0), _full(gq.shape), _full(gkv.shape),
                   _full(w_uq.shape), _full(w_ukv.shape)],
        out_shape=[jax.ShapeDtypeStruct((t, Q_LORA), BF), jax.ShapeDtypeStruct((t, KV_LORA), BF),
                   jax.ShapeDtypeStruct((t, LANE), BF), jax.ShapeDtypeStruct(gq.shape, F32),
                   jax.ShapeDtypeStruct(gkv.shape, F32), jax.ShapeDtypeStruct(w_uq.shape, F32),
                   jax.ShapeDtypeStruct(w_ukv.shape, F32)],
        name="mla_prep_bwd", compiler_params=pltpu.CompilerParams(dimension_semantics=("arbitrary",)),
    )(proj, proj, proj, pos_col, inv_freq2, gq, gkv, w_uq, w_ukv, dq, dk, dv)


_NEG = -1e30
_LN2 = math.log(2.0)
ATT_CHAINS = 2


def _causal(tq, tk, q0, k0):
    row = q0 + lax.broadcasted_iota(jnp.int32, (tq, tk), 0)
    col = k0 + lax.broadcasted_iota(jnp.int32, (tq, tk), 1)
    return col <= row


def _attn_fwd(q, k, v, tq, tk):
    t = q.shape[0]

    assert tk % tq == 0 or tq % tk == 0
    n_diag = max(1, tq // tk)

    th = tq // ATT_CHAINS

    def body(q_ref, k_ref, v_ref, o_ref, lse_ref):
        i = pl.program_id(1)
        n_full = (i * tq) // tk

        def step(k0, carry, masked):
            kt = k_ref[pl.ds(k0, tk), :]
            vt = v_ref[pl.ds(k0, tk), :]
            out = []
            for c, (m, l, acc) in enumerate(carry):
                s = _dot(q_ref[pl.ds(c * th, th), :], kt, NT)
                if masked:
                    s = jnp.where(_causal(th, tk, i * tq + c * th, k0), s, _NEG)
                m_new = jnp.maximum(m, jnp.max(s, axis=-1, keepdims=True))
                p = jnp.exp2(s - m_new)
                alpha = jnp.exp2(m - m_new)
                out.append((m_new, alpha * l + jnp.sum(p, axis=-1, keepdims=True), alpha * acc + _dot(p.astype(BF), vt)))
            return tuple(out)

        init = tuple((jnp.full((th, 1), _NEG, F32), jnp.zeros((th, 1), F32), jnp.zeros((th, V_HEAD), F32)) for _ in range(ATT_CHAINS))
        carry = lax.fori_loop(0, n_full, lambda j, c: step(pl.multiple_of(j * tk, tk), c, False), init)
        for dd in range(n_diag):
            carry = step(pl.multiple_of((n_full + dd) * tk, tk), carry, True)
        for c, (m, l, acc) in enumerate(carry):
            o_ref[pl.ds(c * th, th), :] = acc / l
            lse_ref[pl.ds(c * th, th), :] = jnp.broadcast_to(m + jnp.log2(l), (th, LANE))

    return pl.pallas_call(
        body, grid=(MLA_HEADS, t // tq),
        in_specs=[pl.BlockSpec((tq, 2 * LANE), lambda h, i: (i, h)), pl.BlockSpec((t, 2 * LANE), lambda h, i: (0, h)),
                  pl.BlockSpec((t, V_HEAD), lambda h, i: (0, h))],
        out_specs=[pl.BlockSpec((tq, V_HEAD), lambda h, i: (i, h)), pl.BlockSpec((tq, LANE), lambda h, i: (i, h))],
        out_shape=[jax.ShapeDtypeStruct((t, MLA_VW), F32), jax.ShapeDtypeStruct((t, MLA_HEADS * LANE), F32)],
        name="attn_fwd", compiler_params=pltpu.CompilerParams(dimension_semantics=("parallel", "arbitrary")),
    )(q, k, v)


def _attn_bwd(q, k, v, do, lse, delta, tq, tk):
    t = q.shape[0]
    nkt = t // tk
    assert tk % tq == 0

    def body(q_ref, k_ref, v_ref, do_ref, lse_ref, dl_ref, dq_ref, dk_ref, dv_ref):
        j = pl.program_id(1)

        @pl.when(j == 0)
        def _():
            dq_ref[...] = jnp.zeros_like(dq_ref)

        kt = k_ref[...]
        vt = v_ref[...]

        def step(q0, carry, masked):
            dk, dv = carry
            rows = pl.ds(q0, tq)
            qt = q_ref[rows, :]
            dot_ = do_ref[rows, :]
            p = jnp.exp2(_dot(qt, kt, NT) - lse_ref[rows, pl.ds(0, 1)])
            if masked:
                p = jnp.where(_causal(tq, tk, q0, j * tk), p, 0.0)
            dv = dv + _dot(p.astype(BF), dot_, TN)
            ds = (p * (_dot(dot_, vt, NT) - dl_ref[rows, pl.ds(0, 1)])).astype(BF)
            dk = dk + _dot(ds, qt, TN)
            dq_ref[rows, :] += _dot(ds, kt)
            return dk, dv

        per = tk // tq
        carry = (jnp.zeros((tk, 2 * LANE), F32), jnp.zeros((tk, V_HEAD), F32))
        for dd in range(per):
            carry = step(pl.multiple_of(j * tk + dd * tq, tq), carry, True)

        def group(g, c):
            for dd in range(per):
                c = step(pl.multiple_of(g * tk + dd * tq, tq), c, False)
            return c

        dk, dv = lax.fori_loop(j + 1, nkt, group, carry)
        dk_ref[...] = dk * _LN2
        dv_ref[...] = dv

        @pl.when(j == nkt - 1)
        def _():
            dq_ref[...] = dq_ref[...] * _LN2

    return pl.pallas_call(
        body, grid=(MLA_HEADS, nkt),
        in_specs=[pl.BlockSpec((t, 2 * LANE), lambda h, j: (0, h)), pl.BlockSpec((tk, 2 * LANE), lambda h, j: (j, h)),
                  pl.BlockSpec((tk, V_HEAD), lambda h, j: (j, h)), pl.BlockSpec((t, V_HEAD), lambda h, j: (0, h)),
                  pl.BlockSpec((t, LANE), lambda h, j: (0, h)), pl.BlockSpec((t, LANE), lambda h, j: (0, h))],
        out_specs=[pl.BlockSpec((t, 2 * LANE), lambda h, j: (0, h)), pl.BlockSpec((tk, 2 * LANE), lambda h, j: (j, h)),
                   pl.BlockSpec((tk, V_HEAD), lambda h, j: (j, h))],
        out_shape=[jax.ShapeDtypeStruct((t, MLA_HEADS * 2 * LANE), F32), jax.ShapeDtypeStruct((t, MLA_HEADS * 2 * LANE), F32),
                   jax.ShapeDtypeStruct((t, MLA_VW), F32)],
        name="attn_bwd", compiler_params=pltpu.CompilerParams(dimension_semantics=("parallel", "arbitrary")),
    )(q, k, v, do, lse, delta)


def _adam_update(w, g, m, v):
    mm = ADAM_B1 * m + (1.0 - ADAM_B1) * g
    vv = ADAM_B2 * v + (1.0 - ADAM_B2) * jnp.square(g)
    m_hat = mm / (1.0 - ADAM_B1 ** ADAM_STEP)
    v_hat = vv / (1.0 - ADAM_B2 ** ADAM_STEP)
    return -ADAM_LR * (m_hat / (jnp.sqrt(v_hat) + ADAM_EPS) + ADAM_WD * w), mm, vv


def _adamw(w, g, m, v, name):
    r, c = w.shape
    tr = max([r // s for s in range(1, r // 8 + 1) if r % s == 0 and (r // s) % 8 == 0 and r // s <= 256] or [r])
    slots = g.ndim == 3

    def body(w_ref, g_ref, m_ref, v_ref, g_out, d_ref, nm_ref, nv_ref):
        if slots:
            gg = g_ref[0].astype(F32)
            for s in range(1, N_DEV):
                gg = gg + g_ref[s].astype(F32)
        else:
            gg = g_ref[...]
        g_out[...] = gg
        d_ref[...], nm_ref[...], nv_ref[...] = _adam_update(w_ref[...], gg, m_ref[...], v_ref[...])

    spec = pl.BlockSpec((tr, c), lambda i: (i, 0))
    g_spec = pl.BlockSpec((N_DEV, tr, c), lambda i: (0, i, 0)) if slots else spec
    return pl.pallas_call(
        body, grid=(r // tr,), in_specs=[spec, g_spec, spec, spec], out_specs=[spec] * 4,
        out_shape=[jax.ShapeDtypeStruct((r, c), F32)] * 4, name=name,
        compiler_params=pltpu.CompilerParams(dimension_semantics=("arbitrary",)),
    )(w, g, m, v)


def _adamw_many(ws, gs, ms, vs, name):
    n = len(ws)

    def body(*refs):
        for i in range(n):
            w_ref, g_ref, m_ref, v_ref = (refs[j * n + i] for j in range(4))
            d_ref, nm_ref, nv_ref = (refs[(4 + j) * n + i] for j in range(3))
            d_ref[...], nm_ref[...], nv_ref[...] = _adam_update(w_ref[...], g_ref[...], m_ref[...], v_ref[...])

    shapes = [jax.ShapeDtypeStruct(w.shape, F32) for w in ws]
    outs = pl.pallas_call(body, out_shape=shapes * 3, name=name)(*ws, *gs, *ms, *vs)
    return outs[:n], outs[n:2 * n], outs[2 * n:]


def _cast_bf16(xs, name, after=None):
    n = len(xs)
    extra = [] if after is None else [after]

    def body(*refs):
        outs = refs[n + len(extra):]
        for i in range(n):
            outs[i][...] = refs[i][...].astype(BF)

    vmem = pl.BlockSpec(memory_space=pltpu.VMEM)
    return pl.pallas_call(
        body, out_shape=[jax.ShapeDtypeStruct(x.shape, BF) for x in xs], name=name,
        in_specs=[vmem] * n + [pl.BlockSpec(memory_space=pl.ANY)] * len(extra), out_specs=[vmem] * n)(*xs, *extra)


def _pad_rows(a, n):
    return jnp.pad(a, ((0, n - a.shape[0]), (0, 0)))


def _w_in_to_padded(wt):
    s_ba = P_CQ
    s_cq = s_ba + 2 * DN_HEADS
    s_kr = s_cq + Q_LORA + KV_LORA
    return jnp.concatenate([wt[:s_ba], wt[s_cq:s_kr], _pad_rows(wt[s_ba:s_cq], LANE), _pad_rows(wt[s_kr:], LANE)], axis=0)


def _w_in_from_padded(wt):
    return jnp.concatenate([wt[:P_CQ], wt[P_BA:P_BA + 2 * DN_HEADS], wt[P_CQ:P_BA], wt[P_KR:P_KR + QK_ROPE]], axis=0)


def _w_uq_to_padded(wt):
    w3 = wt.reshape(MLA_HEADS, QK_NOPE + QK_ROPE, Q_LORA)
    nope = w3[:, :QK_NOPE].reshape(MLA_HEADS * QK_NOPE, Q_LORA)
    rope = jnp.pad(w3[:, QK_NOPE:], ((0, 0), (0, LANE - QK_ROPE), (0, 0))).reshape(MLA_HEADS * LANE, Q_LORA)
    return jnp.concatenate([nope, rope], axis=0)


def _w_uq_from_padded(wt):
    nope = wt[:MLA_HEADS * QK_NOPE].reshape(MLA_HEADS, QK_NOPE, Q_LORA)
    rope = wt[MLA_HEADS * QK_NOPE:].reshape(MLA_HEADS, LANE, Q_LORA)[:, :QK_ROPE]
    return jnp.concatenate([nope, rope], axis=1).reshape(MLA_HEADS * (QK_NOPE + QK_ROPE), Q_LORA)


def _pack(pieces, width, row_mult):
    flat = jnp.concatenate([p.reshape(-1) for p in pieces])
    n = flat.shape[0]
    rows = -(-n // (width * row_mult)) * row_mult
    return jnp.pad(flat, (0, rows * width - n)).reshape(rows, width)


def _unpack(flat, shapes):
    out, o = [], 0
    for s in shapes:
        n = math.prod(s)
        out.append(flat[o:o + n].reshape(s))
        o += n
    return out


def kernel(x, c, positions, w_ada, b_ada, w_in, conv_w, a_log, dt_bias, dn_norm_g, q_norm_g, w_uq, kv_norm_g, w_ukv, w_o, ln1_g, ln1_b, w_gate, w_up, w_down, ln2_g, ln2_b, loss_target, m_w_ada, m_b_ada, m_w_in, m_conv_w, m_a_log, m_dt_bias, m_dn_norm_g, m_q_norm_g, m_w_uq, m_kv_norm_g, m_w_ukv, m_w_o, m_ln1_g, m_ln1_b, m_w_gate, m_w_up, m_w_down, m_ln2_g, m_ln2_b, v_w_ada, v_b_ada, v_w_in, v_conv_w, v_a_log, v_dt_bias, v_dn_norm_g, v_q_norm_g, v_w_uq, v_kv_norm_g, v_w_ukv, v_w_o, v_ln1_g, v_ln1_b, v_w_gate, v_w_up, v_w_down, v_ln2_g, v_ln2_b):
    me = 4 * lax.axis_index("x") + 2 * lax.axis_index("y") + lax.axis_index("c")
    t, d = x.shape[1], x.shape[2]
    ada_n = w_ada.shape[2]

    tr = lambda w: w[0].T
    rows = lambda a: a.reshape(-1, a.shape[2])
    (in_shard,) = _cast_bf16([tr(w_in)], "cast_w_in")
    in_gather, token = _exchange_start([in_shard], "gather_w_in_start", scatter=False)
    cw = conv_w.shape[3]
    c_all, conv_all = _exchange([c + token, conv_w[0, :, 0, :]], "gather_small", scatter=False)
    c_all = c_all.reshape(N_DEV, d)
    conv_full = conv_all.transpose(1, 0, 2).reshape(CONV_K, N_DEV * cw)
    conv_w8 = jnp.pad(conv_full, ((0, 8 - CONV_K), (0, 0)))

    b_ada_mine = lax.dynamic_slice(b_ada, (0, me * ada_n), (1, ada_n))
    mod_cols = _mod_fwd(c_all, w_ada[0], b_ada_mine)
    (mod_all,) = _exchange([mod_cols.reshape(N_DEV, 1, ada_n)], "scatter_mod", scatter=True)
    mod = mod_all.reshape(1, N_DEV * ada_n)

    (a_in,) = _exchange_wait(in_gather, mod, "gather_w_in_wait", scatter=False)
    later = _cast_bf16([tr(w_uq), tr(w_ukv), w_o[0], tr(w_gate), tr(w_up), w_down[0]], "cast_weights", after=a_in)
    mixer_gather, token_a = _exchange_start(later[:3], "gather_mixer_weights_start", scatter=False)
    ffn_gather, token_b = _exchange_start(later[3:], "gather_ffn_weights_start", scatter=False)
    mod = mod + (token_a + token_b)
    w_in_t = _w_in_to_padded(rows(a_in))

    def mixer_weights(after):
        a_uq, a_ukv, a_o = _exchange_wait(mixer_gather, after, "gather_mixer_weights_wait", scatter=False)
        return _w_uq_to_padded(rows(a_uq)), rows(a_ukv), rows(a_o)

    def ffn_weights(after):
        a_gate, a_up, a_down = _exchange_wait(ffn_gather, after, "gather_ffn_weights_wait", scatter=False)
        return rows(a_gate), rows(a_up), rows(a_down)

    def by_dest(g):
        return g.reshape(N_DEV, -1, g.shape[1])

    scatters = {}

    def grads_ready(tag, *g):
        if tag == "ffn":
            pieces = [by_dest(a) for a in g]
        elif tag == "mixer":
            g_w_o, g_w_uq_t, g_w_ukv_t = g
            pieces = [by_dest(g_w_o), by_dest(_w_uq_from_padded(g_w_uq_t).astype(BF)), by_dest(g_w_ukv_t.astype(BF))]
        else:
            pieces = [by_dest(_w_in_from_padded(g[0]))]
        scatters[tag], token = _exchange_start(pieces, "scatter_%s_grads_start" % tag, scatter=True)
        return token

    loc = _local_step(x[0], loss_target[0], positions[0], mod, w_in_t, mixer_weights, ffn_weights, grads_ready,
                      conv_w8, a_log, dt_bias, dn_norm_g, q_norm_g, kv_norm_g, ln1_g, ln1_b, ln2_g, ln2_b)
    grad_x, loss_acc, dmod, d_conv8, d_al8, d_dt8, d_dn_g, d_q_g, d_kv_g, d_ln1_g, d_ln1_b, d_ln2_g, d_ln2_b = loc

    small_shapes = [(6 * d,), (CONV_K, N_DEV * cw), (DN_HEADS,), (DN_HEADS,), (DN_DV,), (Q_LORA,), (KV_LORA,), (d,), (d,), (d,), (d,), (1,)]
    gsmall = _pack([dmod, d_conv8[:CONV_K], d_al8[0, :DN_HEADS], d_dt8[0, :DN_HEADS], d_dn_g, d_q_g, d_kv_g,
                    d_ln1_g, d_ln1_b, d_ln2_g, d_ln2_b, loss_acc[0, :1]], LANE, 8)
    (gsmall_all,) = _exchange([gsmall], "gather_small_grads", scatter=False)
    dmod_all = gsmall_all.reshape(N_DEV, -1)[:, :6 * d]
    tot = _unpack(_sum_slots(gsmall_all, "sum_small_grads").reshape(-1), small_shapes)
    g_b_ada, g_conv_full, g_a_log, g_dt_bias, g_dn_g, g_q_g, g_kv_g, g_ln1_g, g_ln1_b, g_ln2_g, g_ln2_b, loss1 = tot
    loss = loss1.reshape(())
    g_conv_w = lax.dynamic_slice(g_conv_full, (0, me * cw), (CONV_K, cw))
    g_w_ada = _mod_bwd(c_all.T, lax.dynamic_slice(dmod_all, (0, me * ada_n), (N_DEV, ada_n)))

    grads = {"w_ada": g_w_ada[None], "b_ada": g_b_ada[None], "conv_w": g_conv_w[None, :, None, :],
             "a_log": g_a_log[None], "dt_bias": g_dt_bias[None], "dn_norm_g": g_dn_g[None], "q_norm_g": g_q_g[None],
             "kv_norm_g": g_kv_g[None], "ln1_g": g_ln1_g[None], "ln1_b": g_ln1_b[None], "ln2_g": g_ln2_g[None], "ln2_b": g_ln2_b[None]}
    weights = dict(w_ada=w_ada, b_ada=b_ada, w_in=w_in, conv_w=conv_w, a_log=a_log, dt_bias=dt_bias, dn_norm_g=dn_norm_g,
                   q_norm_g=q_norm_g, w_uq=w_uq, kv_norm_g=kv_norm_g, w_ukv=w_ukv, w_o=w_o, ln1_g=ln1_g, ln1_b=ln1_b,
                   w_gate=w_gate, w_up=w_up, w_down=w_down, ln2_g=ln2_g, ln2_b=ln2_b)
    ms = dict(w_ada=m_w_ada, b_ada=m_b_ada, w_in=m_w_in, conv_w=m_conv_w, a_log=m_a_log, dt_bias=m_dt_bias,
              dn_norm_g=m_dn_norm_g, q_norm_g=m_q_norm_g, w_uq=m_w_uq, kv_norm_g=m_kv_norm_g, w_ukv=m_w_ukv, w_o=m_w_o,
              ln1_g=m_ln1_g, ln1_b=m_ln1_b, w_gate=m_w_gate, w_up=m_w_up, w_down=m_w_down, ln2_g=m_ln2_g, ln2_b=m_ln2_b)
    vs = dict(w_ada=v_w_ada, b_ada=v_b_ada, w_in=v_w_in, conv_w=v_conv_w, a_log=v_a_log, dt_bias=v_dt_bias,
              dn_norm_g=v_dn_norm_g, q_norm_g=v_q_norm_g, w_uq=v_w_uq, kv_norm_g=v_kv_norm_g, w_ukv=v_w_ukv, w_o=v_w_o,
              ln1_g=v_ln1_g, ln1_b=v_ln1_b, w_gate=v_w_gate, w_up=v_w_up, w_down=v_w_down, ln2_g=v_ln2_g, ln2_b=v_ln2_b)
    names = list(weights)
    big = ("w_ada", "w_gate", "w_up", "w_down", "w_o", "w_uq", "w_ukv", "w_in")
    waits = {"w_gate": ("ffn", ("w_gate", "w_up", "w_down")), "w_o": ("mixer", ("w_o", "w_uq", "w_ukv")), "w_in": ("in", ("w_in",))}
    delta_w, new_m, new_v, slots = {}, {}, {}, {}
    last = g_w_ada
    for n in big:
        if n == "w_in":
            rest = [r for r in names if r not in big]
            flat2 = lambda a: a.reshape(-1, a.shape[-1])
            outs = _adamw_many(*[[flat2(src[r]) for r in rest] for src in (weights, grads, ms, vs)], "adamw_small")
            for dst, o in zip((delta_w, new_m, new_v), outs):
                for r, a in zip(rest, o):
                    dst[r] = a.reshape(weights[r].shape)
            last = outs[0][0]
        transposed = n in ("w_in", "w_uq", "w_ukv", "w_gate", "w_up")
        two = (lambda a: a[0].T) if transposed else (lambda a: a[0])
        back = (lambda a: a.T[None]) if transposed else (lambda a: a[None])
        if n in waits:
            tag, members = waits[n]
            slots.update(zip(members, _exchange_wait(scatters[tag], last, "scatter_%s_grads_wait" % tag, scatter=True)))
        g_in = slots[n] if n in slots else two(grads[n])
        gr, dlt, nm, nv = _adamw(two(weights[n]), g_in, two(ms[n]), two(vs[n]), "adamw_" + n)
        grads[n], delta_w[n], new_m[n], new_v[n] = back(gr), back(dlt), back(nm), back(nv)
        last = nv

    return (loss, grad_x[None], *[grads[n] for n in names], *[delta_w[n] for n in names],
            *[new_m[n] for n in names], *[new_v[n] for n in names])


def _local_step(xs, tgt, pos, mod, w_in_t, mixer_weights, ffn_weights, grads_ready, conv_w8,
                a_log, dt_bias, dn_norm_g, q_norm_g, kv_norm_g, ln1_g, ln1_b, ln2_g, ln2_b):
    t, d = xs.shape
    sh_m, sc_m, gt_m, sh_f, sc_f, gt_f = [mod[:, i * d:(i + 1) * d] for i in range(6)]
    pos_col = pos.astype(F32).reshape(t, 1)
    inv_freq = 1.0 / (ROPE_THETA ** (jnp.arange(0, QK_ROPE, 2, dtype=F32) / QK_ROPE))
    inv_freq2 = jnp.pad(jnp.concatenate([inv_freq, inv_freq]), (0, LANE - QK_ROPE)).reshape(1, LANE)
    al8 = jnp.pad(a_log, ((0, 7), (0, LANE - DN_HEADS)))
    dt8 = jnp.pad(dt_bias, ((0, 7), (0, LANE - DN_HEADS)))

    tm = min(512, t)
    tq = min(256, t)
    tk = min(512, t)

    (h1,) = _rowwise("modulate_in", lambda xx, sc, sh: xx * (1.0 + sc) + sh, [xs], [sc_m, sh_m], [(d, BF)], [], tm)
    proj = _matmul(h1, w_in_t, "nt", "in_proj")
    qkv = _conv_fwd(proj, conv_w8, min(256, t))
    gdn_tm = min(512, t)
    intra, inverses = _gdn_intra_fwd(qkv, proj, al8, dt8, gdn_tm)
    o_dn, states = _gdn_scan_fwd(intra, gdn_tm)
    w_uq_t, w_ukv_t, w_o_f = mixer_weights(states)
    qc, kc, vc = _mla_prep_fwd(proj, pos_col, inv_freq2, q_norm_g, kv_norm_g, w_uq_t, w_ukv_t, tm)
    o_mla, lse = _attn_fwd(qc, kc, vc, min(1024, t), min(1024, t))

    def mix_in(o, z, om, g):
        return jnp.concatenate(_gdn_out(o, z, g) + [om], axis=1)

    (mixin,) = _rowwise("mixer_out", mix_in, [o_dn, (proj, DN_VW, P_Z // DN_VW), o_mla], [dn_norm_g], [(2 * DN_VW, BF)], [], tm)
    mix = _matmul(mixin, w_o_f, "nn", "out_proj")

    def block1(xx, mx, gt, g1, b1, sc, sh):
        x1 = _layernorm(DEEPNORM_ALPHA * xx + gt * mx, g1, b1)
        return x1, x1 * (1.0 + sc) + sh

    x1, h2 = _rowwise("norm1_modulate", block1, [xs, mix], [gt_m, ln1_g, ln1_b, sc_f, sh_f], [(d, F32), (d, BF)], [], tm)
    w_gate_f, w_up_f, w_down_f = ffn_weights(h2)
    act, gate, up = _ffn_in(h2, w_gate_f, w_up_f)
    ff = _matmul(act, w_down_f, "nn", "ffn_out")

    def tail_loss(x1_, ff_, gt, g2, b2, tg):
        y = _layernorm(DEEPNORM_ALPHA * x1_ + gt * ff_, g2, b2)
        return 0.5 * jnp.sum(jnp.mean(jnp.square(y - tg), axis=-1))

    def tail(x1_, ff_, tg, gt, g2, b2):
        loss, (dx1, dff, dgt, dg2, db2) = jax.value_and_grad(tail_loss, argnums=(0, 1, 2, 3, 4))(x1_, ff_, gt, g2, b2, tg)
        return dx1, dff, jnp.full((1, LANE), loss, F32), dgt, dg2, db2

    dx1_a, dff, loss_acc, d_gt_f, d_ln2_g, d_ln2_b = _rowwise(
        "norm2_loss", tail, [x1, ff, tgt], [gt_f, ln2_g, ln2_b], [(d, F32), (d, BF)], [(1, LANE), (1, d), (1, d), (1, d)], tm)

    g_w_down = _matmul(act, dff, "tn", "d_w_down", BF)
    dgate, dup = _ffn_act_bwd(dff, w_down_f, gate, up)
    g_w_gate = _matmul(dgate, h2, "tn", "d_w_gate", BF)
    g_w_up = _matmul(dup, h2, "tn", "d_w_up", BF)
    token = grads_ready("ffn", g_w_gate, g_w_up, g_w_down)
    dh2 = _matmul2_nn(dgate, w_gate_f, dup, w_up_f, "d_ffn_in")

    def block1_bwd(xx, mx, dx1_, dh2_, gt, g1, b1, sc, sh):
        _, vjp = jax.vjp(block1, xx, mx, gt, g1, b1, sc, sh)
        dxx, dmx, dgt, dg1, db1, dsc, dsh = vjp((dx1_, dh2_))
        return dxx, dmx, dgt, dg1, db1, dsc, dsh

    dx_a, dmix, d_gt_m, d_ln1_g, d_ln1_b, d_sc_f, d_sh_f = _rowwise(
        "norm1_modulate_bwd", block1_bwd, [xs, mix, dx1_a, dh2], [gt_m + token, ln1_g, ln1_b, sc_f, sh_f],
        [(d, F32), (d, BF)], [(1, d)] * 5, min(256, t))

    dmixin = _matmul(dmix, w_o_f, "nt", "d_mixer_out")
    g_w_o = _matmul(mixin, dmix, "tn", "d_w_o", BF)

    def mixer_bwd(o, z, om, dmi, g):
        _, vjp = jax.vjp(lambda o_, z_, g_: jnp.concatenate(_gdn_out(o_, z_, g_), axis=1), o, z, g)
        do_, dz_, dg_ = vjp(dmi[:, :DN_VW])
        dom = dmi[:, DN_VW:]
        delta = [jnp.broadcast_to(jnp.sum(dom[:, h * V_HEAD:(h + 1) * V_HEAD] * om[:, h * V_HEAD:(h + 1) * V_HEAD], axis=-1, keepdims=True), (o.shape[0], LANE))
                 for h in range(MLA_HEADS)]
        return do_, dz_, dom, jnp.concatenate(delta, axis=1), dg_

    do_dn, dz, do_mla, delta, d_dn_g = _rowwise(
        "mixer_out_bwd", mixer_bwd, [o_dn, (proj, DN_VW, P_Z // DN_VW), o_mla, dmixin], [dn_norm_g],
        [(DN_VW, F32), (DN_VW, BF), (MLA_VW, BF), (MLA_HEADS * LANE, F32)], [(1, DN_DV)], tm)

    dqc, dkc, dvc = _attn_bwd(qc, kc, vc, do_mla, lse, delta, min(512, t), min(1024, t))
    dcq, dckv, dkr, d_q_g, d_kv_g, g_w_uq_t, g_w_ukv_t = _mla_prep_bwd(
        proj, pos_col, inv_freq2, q_norm_g, kv_norm_g, w_uq_t, w_ukv_t, dqc, dkc, dvc, min(256, t))

    token = grads_ready("mixer", g_w_o, g_w_uq_t, g_w_ukv_t)

    d_intra = _gdn_scan_bwd(intra, states, do_dn, gdn_tm)
    dqkv_act, dba, d_al8, d_dt8 = _gdn_intra_bwd(qkv, proj, al8 + token, dt8, inverses, d_intra, min(256, t))
    dqkv_pre, d_conv8 = _conv_bwd(proj, conv_w8, dqkv_act, min(256, t))

    dproj = jnp.concatenate([dqkv_pre, dz, dcq, dckv, dba, dkr], axis=1)
    dh1 = _matmul(dproj, w_in_t, "nn", "d_in_proj")
    g_w_in_t = _matmul(dproj, h1, "tn", "d_w_in", BF)
    token = grads_ready("in", g_w_in_t)

    def modulate_bwd(xx, dh, dxa, sc):
        return dh * (1.0 + sc) + dxa, jnp.sum(dh * xx, axis=0, keepdims=True), jnp.sum(dh, axis=0, keepdims=True)

    grad_x, d_sc_m, d_sh_m = _rowwise("modulate_in_bwd", modulate_bwd, [xs, dh1, dx_a], [sc_m + token], [(d, F32)], [(1, d), (1, d)], tm)
    dmod = jnp.concatenate([d_sh_m, d_sc_m, d_gt_m, d_sh_f, d_sc_f, d_gt_f], axis=1)
    return grad_x, loss_acc, dmod, d_conv8, d_al8, d_dt8, d_dn_g, d_q_g, d_kv_g, d_ln1_g, d_ln1_b, d_ln2_g, d_ln2_b
```

```python
import functools
import math

import jax
import jax.numpy as jnp
from jax import lax
from jax.experimental import pallas as pl
from jax.experimental.pallas import tpu as pltpu

F32 = jnp.float32
BF = jnp.bfloat16
HI = lax.Precision.HIGHEST

N_DEV = 8
DN_HEADS = 4
DN_DK = 128
DN_DV = 128
CONV_K = 4
CHUNK = 64
MLA_HEADS = 4
QK_NOPE = 128
QK_ROPE = 64
V_HEAD = 128
Q_LORA = 512
KV_LORA = 256
ROPE_THETA = 10000.0
DEPTH = 1
DEEPNORM_ALPHA = (2.0 * DEPTH) ** 0.25
LANE = 128
CONV_HALO = 8
CONV_ROWS, CONV_COLS = 64, 256

DN_QK = DN_HEADS * DN_DK
DN_VW = DN_HEADS * DN_DV
DN_CONV_CH = 2 * DN_QK + DN_VW
MLA_VW = MLA_HEADS * V_HEAD
MLA_QCAT = QK_NOPE + LANE
N_IN = DN_CONV_CH + DN_VW + 2 * DN_HEADS + Q_LORA + KV_LORA + QK_ROPE
P_QKV = 0
P_Z = DN_CONV_CH
P_CQ = P_Z + DN_VW
P_CKV = P_CQ + Q_LORA
P_BA = P_CKV + KV_LORA
P_KR = P_BA + LANE
N_INP = P_KR + LANE

ADAM_LR = 0.001
ADAM_B1 = 0.9
ADAM_B2 = 0.999
ADAM_EPS = 1e-08
ADAM_WD = 0.01
ADAM_STEP = 10

NN = (((1,), (0,)), ((), ()))
NT = (((1,), (1,)), ((), ()))
TN = (((0,), (0,)), ((), ()))


def _pick(n, prefs):
    for p in prefs:
        if n % p == 0:
            return p
    return n


def _full(shape):
    return pl.BlockSpec(shape, lambda *_: (0,) * len(shape))


def _dot(a, b, dims=NN):
    return lax.dot_general(a, b, dims, preferred_element_type=F32)


def _doth(a, b, dims=NN):
    return lax.dot_general(a, b, dims, precision=HI, preferred_element_type=F32)


@jax.custom_vjp
def _mmb(a, b):
    return _dot(a.astype(BF), b.astype(BF), NN)


def _mmb_fwd(a, b):
    return _mmb(a, b), (a, b)


def _mmb_bwd(res, g):
    a, b = res
    gb = g.astype(BF)
    return (_dot(gb, b.astype(BF), NT).astype(a.dtype), _dot(a.astype(BF), gb, TN).astype(b.dtype))


_mmb.defvjp(_mmb_fwd, _mmb_bwd)


@jax.custom_vjp
def _mmb_nt(a, b):
    return _dot(a.astype(BF), b.astype(BF), NT)


def _mmb_nt_fwd(a, b):
    return _mmb_nt(a, b), (a, b)


def _mmb_nt_bwd(res, g):
    a, b = res
    gb = g.astype(BF)
    return (_dot(gb, b.astype(BF), NN).astype(a.dtype), _dot(gb, a.astype(BF), TN).astype(b.dtype))


_mmb_nt.defvjp(_mmb_nt_fwd, _mmb_nt_bwd)


@jax.custom_vjp
def _mmb_tn(a, b):
    return _dot(a.astype(BF), b.astype(BF), TN)


def _mmb_tn_fwd(a, b):
    return _mmb_tn(a, b), (a, b)


def _mmb_tn_bwd(res, g):
    a, b = res
    gb = g.astype(BF)
    return (_dot(b.astype(BF), gb, NT).astype(a.dtype), _dot(a.astype(BF), gb, NN).astype(b.dtype))


_mmb_tn.defvjp(_mmb_tn_fwd, _mmb_tn_bwd)


def _sigmoid(x):
    return 0.5 * (jnp.tanh(0.5 * x) + 1.0)


def _silu(x):
    return x * _sigmoid(x)


def _softplus(x):
    return jnp.maximum(x, 0.0) + jnp.log(1.0 + jnp.exp(-jnp.abs(x)))


def _layernorm(x, g, b, eps=1e-5):
    mu = jnp.mean(x, axis=-1, keepdims=True)
    xc = x - mu
    var = jnp.mean(xc * xc, axis=-1, keepdims=True)
    return xc * lax.rsqrt(var + eps) * g + b


def _rmsnorm(x, g, eps=1e-6):
    return x * lax.rsqrt(jnp.mean(x * x, axis=-1, keepdims=True) + eps) * g


def _l2norm(x, eps=1e-6):
    return x * lax.rsqrt(jnp.sum(x * x, axis=-1, keepdims=True) + eps)


def _rowwise(name, fn, rows, vecs, out_rows, out_accs, tm):
    rows = [r if isinstance(r, tuple) else (r, r.shape[1], 0) for r in rows]
    t = rows[0][0].shape[0]
    tm = min(tm, t)
    assert t % tm == 0
    nr, nv, no = len(rows), len(vecs), len(out_rows)

    def body(*refs):
        ins = [r[...] for r in refs[:nr + nv]]
        outs = fn(*ins)
        outs = outs if isinstance(outs, (tuple, list)) else (outs,)
        o_rows = refs[nr + nv:nr + nv + no]
        o_accs = refs[nr + nv + no:]
        for o, val in zip(o_rows, outs[:no]):
            o[...] = val.astype(o.dtype)
        if o_accs:
            @pl.when(pl.program_id(0) == 0)
            def _():
                for o in o_accs:
                    o[...] = jnp.zeros_like(o)
            for o, val in zip(o_accs, outs[no:]):
                o[...] += val

    in_specs = [pl.BlockSpec((tm, w), functools.partial(lambda i, j: (i, j), j=j)) for (_, w, j) in rows]
    in_specs += [_full(v.shape) for v in vecs]
    out_specs = [pl.BlockSpec((tm, w), lambda i: (i, 0)) for (w, _) in out_rows]
    out_specs += [_full(s) for s in out_accs]
    out_shape = [jax.ShapeDtypeStruct((t, w), d) for (w, d) in out_rows]
    out_shape += [jax.ShapeDtypeStruct(s, F32) for s in out_accs]
    res = pl.pallas_call(
        body, grid=(t // tm,), in_specs=in_specs, out_specs=out_specs, out_shape=out_shape, name=name,
        compiler_params=pltpu.CompilerParams(dimension_semantics=("arbitrary",)),
    )(*[r[0] for r in rows], *vecs)
    return res


def _matmul(a, b, mode, name, out_dtype=F32):
    if mode == "nn":
        (m, k), n = a.shape, b.shape[1]
    elif mode == "nt":
        (m, k), n = a.shape, b.shape[0]
    else:
        (k, m), n = a.shape, b.shape[1]
    tm, tn, tk = _matmul_tiles(m, n, k, a.dtype.itemsize, b.dtype.itemsize, jnp.dtype(out_dtype).itemsize)
    nk = k // tk
    dims = {"nn": NN, "nt": NT, "tn": TN}[mode]

    def body(a_ref, b_ref, o_ref, *acc):
        part = _dot(a_ref[...].astype(BF), b_ref[...].astype(BF), dims)
        if nk == 1:
            o_ref[...] = part.astype(o_ref.dtype)
            return
        (acc_ref,) = acc
        kk = pl.program_id(2)

        @pl.when(kk == 0)
        def _():
            acc_ref[...] = part

        @pl.when(kk > 0)
        def _():
            acc_ref[...] += part

        @pl.when(kk == nk - 1)
        def _():
            o_ref[...] = acc_ref[...].astype(o_ref.dtype)

    a_spec = pl.BlockSpec((tk, tm), lambda i, j, kk: (kk, i)) if mode == "tn" else pl.BlockSpec((tm, tk), lambda i, j, kk: (i, kk))
    b_spec = pl.BlockSpec((tn, tk), lambda i, j, kk: (j, kk)) if mode == "nt" else pl.BlockSpec((tk, tn), lambda i, j, kk: (kk, j))
    return pl.pallas_call(
        body, grid=(m // tm, n // tn, nk), in_specs=[a_spec, b_spec],
        out_specs=pl.BlockSpec((tm, tn), lambda i, j, kk: (i, j)),
        out_shape=jax.ShapeDtypeStruct((m, n), out_dtype),
        scratch_shapes=[pltpu.VMEM((tm, tn), F32)] if nk > 1 else [], name=name,
        compiler_params=pltpu.CompilerParams(dimension_semantics=("parallel", "parallel", "arbitrary")),
    )(a, b)


def _lane_tile(n, cap):
    return max([n // s for s in range(1, n // LANE + 1) if n % s == 0 and (n // s) % LANE == 0 and n // s <= cap] or [n])


def _ffn_in(h, w_gate, w_up):
    m, k = h.shape
    f = w_gate.shape[0]
    tm, tn = _pick(m, (512, 256, 128)), _lane_tile(f, 1408)

    def body(h_ref, wg_ref, wu_ref, act_ref, g_ref, u_ref):
        hh = h_ref[...]
        g = _dot(hh, wg_ref[...], NT)
        u = _dot(hh, wu_ref[...], NT)
        act_ref[...] = (_silu(g) * u).astype(act_ref.dtype)
        g_ref[...] = g.astype(g_ref.dtype)
        u_ref[...] = u.astype(u_ref.dtype)

    w_spec = pl.BlockSpec((tn, k), lambda i, j: (j, 0))
    o_spec = pl.BlockSpec((tm, tn), lambda i, j: (i, j))
    return pl.pallas_call(
        body, grid=(m // tm, f // tn), in_specs=[pl.BlockSpec((tm, k), lambda i, j: (i, 0)), w_spec, w_spec],
        out_specs=[o_spec] * 3, out_shape=[jax.ShapeDtypeStruct((m, f), BF)] * 3, name="ffn_in",
        compiler_params=pltpu.CompilerParams(dimension_semantics=("parallel", "parallel")),
    )(h, w_gate, w_up)


def _ffn_act_bwd(dff, w_down, gate, up):
    m, k = dff.shape
    f = w_down.shape[0]
    tm, tn = _pick(m, (512, 256, 128)), _lane_tile(f, 1408)

    def body(d_ref, w_ref, g_ref, u_ref, dg_ref, du_ref):
        da = _dot(d_ref[...], w_ref[...], NT)
        g = g_ref[...].astype(F32)
        sg = _sigmoid(g)
        dg_ref[...] = (da * u_ref[...].astype(F32) * (sg * (1.0 + g * (1.0 - sg)))).astype(dg_ref.dtype)
        du_ref[...] = (da * (g * sg)).astype(du_ref.dtype)

    o_spec = pl.BlockSpec((tm, tn), lambda i, j: (i, j))
    return pl.pallas_call(
        body, grid=(m // tm, f // tn),
        in_specs=[pl.BlockSpec((tm, k), lambda i, j: (i, 0)), pl.BlockSpec((tn, k), lambda i, j: (j, 0)), o_spec, o_spec],
        out_specs=[o_spec] * 2, out_shape=[jax.ShapeDtypeStruct((m, f), BF)] * 2, name="d_ffn_act",
        compiler_params=pltpu.CompilerParams(dimension_semantics=("parallel", "parallel")),
    )(dff, w_down, gate, up)


def _matmul2_nn(a1, b1, a2, b2, name):
    m, k = a1.shape
    n = b1.shape[1]
    tm, tn = _pick(m, (512, 256, 128)), _pick(n, (512, 256, 128))

    def body(a1_ref, b1_ref, a2_ref, b2_ref, o_ref):
        o_ref[...] = _dot(a1_ref[...], b1_ref[...]) + _dot(a2_ref[...], b2_ref[...])

    a_spec = pl.BlockSpec((tm, k), lambda i, j: (i, 0))
    b_spec = pl.BlockSpec((k, tn), lambda i, j: (0, j))
    return pl.pallas_call(
        body, grid=(m // tm, n // tn), in_specs=[a_spec, b_spec, a_spec, b_spec],
        out_specs=pl.BlockSpec((tm, tn), lambda i, j: (i, j)), out_shape=jax.ShapeDtypeStruct((m, n), F32), name=name,
        compiler_params=pltpu.CompilerParams(dimension_semantics=("parallel", "parallel")),
    )(a1, b1, a2, b2)


MATMUL_VMEM_BUDGET = 28 * 1024 * 1024


def _matmul_tiles(m, n, k, a_bytes, b_bytes, o_bytes):
    def divisors(x, cap):
        return sorted({x // s for s in range(1, 65) if x % s == 0 and (x // s) % LANE == 0 and x // s <= cap}, reverse=True) or [x]

    for tk in divisors(k, k):
        best = None
        for tm in divisors(m, 1024):
            for tn in divisors(n, 2048):
                need = 2 * (tm * tk * a_bytes + tk * tn * b_bytes + tm * tn * o_bytes) + (tm * tn * 4 if tk < k else 0)
                if need <= MATMUL_VMEM_BUDGET and tm * tn >= 512 * 512 and (best is None or tm * tn > best[0] * best[1]):
                    best = (tm, tn)
        if best:
            return best[0], best[1], tk
    return _pick(m, (512, 256, 128)), _pick(n, (512, 256, 128)), _pick(k, (512, 256, 128))


def _exchange(xs, name, scatter):
    n = len(xs)
    npeer = N_DEV - 1

    def body(*refs):
        x_refs, o_refs = refs[:n], refs[n:2 * n]
        send_sems, recv_sems, local_sems = refs[2 * n:]
        mx, my, mc = lax.axis_index("x"), lax.axis_index("y"), lax.axis_index("c")
        me = 4 * mx + 2 * my + mc
        src_me = [x.at[me] if scatter else x for x in x_refs]
        mine = [pltpu.make_async_copy(src_me[a], o_refs[a].at[me], local_sems.at[a]) for a in range(n)]
        for cp in mine:
            cp.start()
        copies = []
        for k in range(1, N_DEV):
            px, py, pc = mx ^ (k >> 2), my ^ ((k >> 1) & 1), mc ^ (k & 1)
            peer = 4 * px + 2 * py + pc
            for a in range(n):
                cp = pltpu.make_async_remote_copy(
                    src_ref=x_refs[a].at[peer] if scatter else x_refs[a], dst_ref=o_refs[a].at[me],
                    send_sem=send_sems.at[a * npeer + k - 1], recv_sem=recv_sems.at[a * npeer + k - 1],
                    device_id=(px, py, pc), device_id_type=pl.DeviceIdType.MESH)
                cp.start()
                copies.append((cp, a, k, peer))
        for cp, a, k, peer in copies:
            pltpu.make_async_remote_copy(
                src_ref=src_me[a], dst_ref=o_refs[a].at[peer], send_sem=send_sems.at[a * npeer + k - 1],
                recv_sem=recv_sems.at[a * npeer + k - 1], device_id=(mx, my, mc),
                device_id_type=pl.DeviceIdType.MESH).wait_recv()
        for cp, _, _, _ in copies:
            cp.wait_send()
        for cp in mine:
            cp.wait()

    return pl.pallas_call(
        body, out_shape=[jax.ShapeDtypeStruct((N_DEV,) + x.shape[-2:], x.dtype) for x in xs],
        in_specs=[pl.BlockSpec(memory_space=pl.ANY)] * n, out_specs=[pl.BlockSpec(memory_space=pl.ANY)] * n,
        scratch_shapes=[pltpu.SemaphoreType.DMA((n * npeer,)), pltpu.SemaphoreType.DMA((n * npeer,)),
                        pltpu.SemaphoreType.DMA((n,))],
        name=name,
    )(*xs)


def _gather_by_chip(xs, name):
    n = len(xs)
    per = N_DEV - 1

    def body(*refs):
        x_refs, o_refs = refs[:n], refs[n:2 * n]
        send_sems, recv_sems, local_sems = refs[2 * n:]
        mx, my, mc = lax.axis_index("x"), lax.axis_index("y"), lax.axis_index("c")
        me, sibling = (mx, my, mc), (mx, my, 1 - mc)
        chips = [(1 - mx, my), (mx, 1 - my), (1 - mx, 1 - my)]
        slot = lambda d: 4 * d[0] + 2 * d[1] + d[2]

        def copy(a, k, block, to, src=None):
            dst = o_refs[a].at[slot(block)]
            return pltpu.make_async_remote_copy(
                src_ref=dst if src is None else src, dst_ref=dst, send_sem=send_sems.at[a * per + k],
                recv_sem=recv_sems.at[a * per + k], device_id=to, device_id_type=pl.DeviceIdType.MESH)

        mine = [pltpu.make_async_copy(x_refs[a], o_refs[a].at[slot(me)], local_sems.at[a]) for a in range(n)]
        for cp in mine:
            cp.start()
        first = []
        for a in range(n):
            first.append(copy(a, 0, me, sibling, src=x_refs[a]))
            first += [copy(a, 1 + j, me, (*chip, mc), src=x_refs[a]) for j, chip in enumerate(chips)]
        for cp in first:
            cp.start()
        passed = []
        for j, chip in enumerate(chips):
            for a in range(n):
                copy(a, 1 + j, (*chip, mc), me).wait_recv()
                cp = copy(a, 4 + j, (*chip, mc), sibling)
                cp.start()
                passed.append(cp)
        for a in range(n):
            copy(a, 0, sibling, me).wait_recv()
            for j, chip in enumerate(chips):
                copy(a, 4 + j, (*chip, 1 - mc), me).wait_recv()
        for cp in first + passed:
            cp.wait_send()
        for cp in mine:
            cp.wait()

    return pl.pallas_call(
        body, out_shape=[jax.ShapeDtypeStruct((N_DEV,) + x.shape, x.dtype) for x in xs],
        in_specs=[pl.BlockSpec(memory_space=pl.ANY)] * n, out_specs=[pl.BlockSpec(memory_space=pl.ANY)] * n,
        scratch_shapes=[pltpu.SemaphoreType.DMA((n * per,)), pltpu.SemaphoreType.DMA((n * per,)),
                        pltpu.SemaphoreType.DMA((n,))],
        name=name,
    )(*xs)


def _peer_of(k):
    mx, my, mc = lax.axis_index("x"), lax.axis_index("y"), lax.axis_index("c")
    px, py, pc = mx ^ (k >> 2), my ^ ((k >> 1) & 1), mc ^ (k & 1)
    return (px, py, pc), 4 * px + 2 * py + pc


def _exchange_start(xs, name, scatter):
    n = len(xs)
    npeer = N_DEV - 1

    def body(*refs):
        x_refs, land_refs = refs[:n], refs[n:2 * n]
        send_sems, recv_sems, token = refs[2 * n], refs[2 * n + 1], refs[-1]
        me = 4 * lax.axis_index("x") + 2 * lax.axis_index("y") + lax.axis_index("c")
        for k in range(1, N_DEV):
            dev, peer = _peer_of(k)
            for a in range(n):
                pltpu.make_async_remote_copy(
                    src_ref=x_refs[a].at[peer] if scatter else x_refs[a], dst_ref=land_refs[a].at[me],
                    send_sem=send_sems.at[a * npeer + k - 1], recv_sem=recv_sems.at[a * npeer + k - 1],
                    device_id=dev, device_id_type=pl.DeviceIdType.MESH).start()
        token[...] = jnp.zeros_like(token)

    hbm = pl.BlockSpec(memory_space=pltpu.HBM)
    sem = pl.BlockSpec(memory_space=pltpu.SEMAPHORE)
    lands = [pltpu.with_memory_space_constraint(lax.empty((N_DEV,) + x.shape[-2:], x.dtype), pltpu.HBM) for x in xs]
    srcs = [pltpu.with_memory_space_constraint(x, pltpu.HBM) for x in xs]
    outs = pl.pallas_call(
        body, name=name,
        out_shape=(pltpu.SemaphoreType.DMA((n * npeer,)), pltpu.SemaphoreType.DMA((n * npeer,)),
                   *[pltpu.HBM(x.shape, x.dtype) for x in srcs], *[pltpu.HBM(z.shape, z.dtype) for z in lands],
                   jax.ShapeDtypeStruct((8, LANE), F32)),
        in_specs=[hbm] * (2 * n), out_specs=(sem, sem, *[hbm] * (2 * n), pl.BlockSpec(memory_space=pltpu.VMEM)),
        input_output_aliases={i: 2 + i for i in range(2 * n)},
        compiler_params=pltpu.CompilerParams(has_side_effects=pltpu.SideEffectType.DATAFLOW_SIDE_EFFECTING),
    )(*srcs, *lands)
    return (outs[0], outs[1], list(outs[2:2 + n]), list(outs[2 + n:2 + 2 * n])), outs[-1][0:1, 0:1]


def _exchange_wait(started, after, name, scatter):
    send_sems, recv_sems, srcs, lands = started
    n = len(srcs)
    npeer = N_DEV - 1

    def body(*refs):
        x_refs, land_refs = refs[:n], refs[n:2 * n]
        send_sems, recv_sems = refs[2 * n], refs[2 * n + 1]
        mx, my, mc = lax.axis_index("x"), lax.axis_index("y"), lax.axis_index("c")
        me = 4 * mx + 2 * my + mc
        for k in range(1, N_DEV):
            _, peer = _peer_of(k)
            for a in range(n):
                src = x_refs[a].at[me] if scatter else x_refs[a]
                cp = pltpu.make_async_remote_copy(
                    src_ref=src, dst_ref=land_refs[a].at[peer], send_sem=send_sems.at[a * npeer + k - 1],
                    recv_sem=recv_sems.at[a * npeer + k - 1], device_id=(mx, my, mc), device_id_type=pl.DeviceIdType.MESH)
                cp.wait_send()
                cp.wait_recv()

    hbm = pl.BlockSpec(memory_space=pltpu.HBM)
    sem = pl.BlockSpec(memory_space=pltpu.SEMAPHORE)
    outs = pl.pallas_call(
        body, name=name,
        out_shape=(*[pltpu.HBM(x.shape, x.dtype) for x in srcs], *[pltpu.HBM(z.shape, z.dtype) for z in lands]),
        in_specs=[hbm] * (2 * n) + [sem, sem, pl.BlockSpec(memory_space=pl.ANY)], out_specs=tuple([hbm] * (2 * n)),
        input_output_aliases={i: i for i in range(2 * n)},
        compiler_params=pltpu.CompilerParams(has_side_effects=pltpu.SideEffectType.DATAFLOW_SIDE_EFFECTING),
    )(*srcs, *lands, send_sems, recv_sems, after)
    me = 4 * lax.axis_index("x") + 2 * lax.axis_index("y") + lax.axis_index("c")
    full = []
    for x, land in zip(outs[:n], outs[n:]):
        own = lax.dynamic_slice(x, (me, 0, 0), (1,) + x.shape[1:]) if scatter else x[None]
        full.append(lax.dynamic_update_slice(land, own, (me, 0, 0)))
    return full


def _sum_slots(x, name):
    _, r, c = x.shape
    tr = _pick(r, (512, 256, 128, 64, 32, 16))

    def body(x_ref, o_ref):
        acc = x_ref[0].astype(F32)
        for s in range(1, N_DEV):
            acc = acc + x_ref[s].astype(F32)
        o_ref[...] = acc

    return pl.pallas_call(
        body, grid=(r // tr,), in_specs=[pl.BlockSpec((N_DEV, tr, c), lambda i: (0, i, 0))],
        out_specs=pl.BlockSpec((tr, c), lambda i: (i, 0)), out_shape=jax.ShapeDtypeStruct((r, c), F32), name=name,
        compiler_params=pltpu.CompilerParams(dimension_semantics=("arbitrary",)),
    )(x)


def _mod_fwd(c_all, w_ada, b_ada_mine):
    def body(c_ref, w_ref, b_ref, o_ref):
        o_ref[...] = _doth(_silu(c_ref[...]), w_ref[...]) + b_ref[...]

    return pl.pallas_call(body, out_shape=jax.ShapeDtypeStruct((c_all.shape[0], w_ada.shape[1]), F32), name="mod_fwd")(c_all, w_ada, b_ada_mine)


def _mod_bwd(c_all_t, dmod_mine):
    def body(ct_ref, d_ref, o_ref):
        s = _silu(ct_ref[...])
        acc = s[:, 0:1] * d_ref[pl.ds(0, 1), :]
        for b in range(1, N_DEV):
            acc = acc + s[:, b:b + 1] * d_ref[pl.ds(b, 1), :]
        o_ref[...] = acc

    return pl.pallas_call(body, out_shape=jax.ShapeDtypeStruct((c_all_t.shape[0], dmod_mine.shape[1]), F32), name="mod_bwd")(c_all_t, dmod_mine)


def _conv_fwd(proj, conv_w8, tm):
    t = proj.shape[0]
    ch = DN_CONV_CH

    def body(x_ref, w_ref, o_ref, buf):
        @pl.when(pl.program_id(0) == 0)
        def _():
            buf[pl.ds(0, CONV_HALO), :] = jnp.zeros((CONV_HALO, ch), F32)

        buf[pl.ds(CONV_HALO, tm), :] = x_ref[...]
        for c0 in range(0, ch, CONV_COLS):
            cols = pl.ds(c0, CONV_COLS)
            w = [w_ref[pl.ds(j, 1), cols] for j in range(CONV_K)]
            for r0 in range(0, tm, CONV_ROWS):
                acc = buf[pl.ds(r0 + CONV_HALO - (CONV_K - 1), CONV_ROWS), cols] * w[0]
                for j in range(1, CONV_K):
                    acc = acc + buf[pl.ds(r0 + CONV_HALO - (CONV_K - 1) + j, CONV_ROWS), cols] * w[j]
                o_ref[pl.ds(r0, CONV_ROWS), cols] = _silu(acc)
        buf[pl.ds(0, CONV_HALO), :] = buf[pl.ds(tm, CONV_HALO), :]

    return pl.pallas_call(
        body, grid=(t // tm,), in_specs=[pl.BlockSpec((tm, ch), lambda i: (i, 0)), _full(conv_w8.shape)],
        out_specs=pl.BlockSpec((tm, ch), lambda i: (i, 0)), out_shape=jax.ShapeDtypeStruct((t, ch), F32),
        scratch_shapes=[pltpu.VMEM((tm + CONV_HALO, ch), F32)], name="conv_fwd",
        compiler_params=pltpu.CompilerParams(dimension_semantics=("arbitrary",)),
    )(proj, conv_w8)


def _conv_bwd(proj, conv_w8, dact, tm):
    t = proj.shape[0]
    ch = DN_CONV_CH
    nt = t // tm
    hb = tm // CONV_HALO

    def body(x_ref, xp_ref, w_ref, dy_ref, dx_ref, dw_ref, xbuf, dbuf):
        step = pl.program_id(0)

        @pl.when(step == 0)
        def _():
            dbuf[pl.ds(tm, CONV_HALO), :] = jnp.zeros((CONV_HALO, ch), F32)
            dw_ref[...] = jnp.zeros_like(dw_ref)

        first = step == nt - 1
        xbuf[pl.ds(0, CONV_HALO), :] = jnp.where(first, 0.0, xp_ref[...])
        xbuf[pl.ds(CONV_HALO, tm), :] = x_ref[...]
        for c0 in range(0, ch, CONV_COLS):
            cols = pl.ds(c0, CONV_COLS)
            w = [w_ref[pl.ds(j, 1), cols] for j in range(CONV_K)]
            dw = [jnp.zeros((1, CONV_COLS), F32) for _ in range(CONV_K)]
            for r0 in range(0, tm, CONV_ROWS):
                xs = [xbuf[pl.ds(r0 + CONV_HALO - (CONV_K - 1) + j, CONV_ROWS), cols] for j in range(CONV_K)]
                pre = xs[0] * w[0]
                for j in range(1, CONV_K):
                    pre = pre + xs[j] * w[j]
                sg = _sigmoid(pre)
                dpre = dy_ref[pl.ds(r0, CONV_ROWS), cols] * (sg * (1.0 + pre * (1.0 - sg)))
                dbuf[pl.ds(r0, CONV_ROWS), cols] = dpre
                dw = [dw[j] + jnp.sum(dpre * xs[j], axis=0, keepdims=True) for j in range(CONV_K)]
            for j in range(CONV_K):
                dw_ref[pl.ds(j, 1), cols] += dw[j]
            for r0 in range(0, tm, CONV_ROWS):
                dx = dbuf[pl.ds(r0 + CONV_K - 1, CONV_ROWS), cols] * w[0]
                for j in range(1, CONV_K):
                    dx = dx + dbuf[pl.ds(r0 + CONV_K - 1 - j, CONV_ROWS), cols] * w[j]
                dx_ref[pl.ds(r0, CONV_ROWS), cols] = dx.astype(dx_ref.dtype)
        dbuf[pl.ds(tm, CONV_HALO), :] = dbuf[pl.ds(0, CONV_HALO), :]

    rev = lambda i: (nt - 1 - i, 0)
    prev = lambda i: (jnp.maximum((nt - 1 - i) * hb - 1, 0), 0)
    return pl.pallas_call(
        body, grid=(nt,),
        in_specs=[pl.BlockSpec((tm, ch), rev), pl.BlockSpec((CONV_HALO, ch), prev), _full(conv_w8.shape),
                  pl.BlockSpec((tm, ch), rev)],
        out_specs=[pl.BlockSpec((tm, ch), rev), _full(conv_w8.shape)],
        out_shape=[jax.ShapeDtypeStruct((t, ch), BF), jax.ShapeDtypeStruct(conv_w8.shape, F32)],
        scratch_shapes=[pltpu.VMEM((tm + CONV_HALO, ch), F32), pltpu.VMEM((tm + CONV_HALO, ch), F32)], name="conv_bwd",
        compiler_params=pltpu.CompilerParams(dimension_semantics=("arbitrary",)),
    )(proj, proj, conv_w8, dact)


BNN = (((2,), (1,)), ((0,), (0,)))
BNT = (((2,), (2,)), ((0,), (0,)))
BTN = (((1,), (1,)), ((0,), (0,)))


def _bdot(a, b, dims, precision=None):
    return lax.dot_general(a, b, dims, precision=precision, preferred_element_type=F32)


@jax.custom_vjp
def _bmmb_nt(a, b):
    return _bdot(a.astype(BF), b.astype(BF), BNT)


def _bmmb_nt_fwd(a, b):
    return _bmmb_nt(a, b), (a, b)


def _bmmb_nt_bwd(res, g):
    a, b = res
    gb = g.astype(BF)
    return _bdot(gb, b.astype(BF), BNN), _bdot(gb, a.astype(BF), BTN)


_bmmb_nt.defvjp(_bmmb_nt_fwd, _bmmb_nt_bwd)


def _unit_lower_solve_fwd(a, r):
    c = a.shape[-1]
    ri = lax.broadcasted_iota(jnp.int32, a.shape, 1)
    ci = lax.broadcasted_iota(jnp.int32, a.shape, 2)
    xm = -a
    inv = (ri == ci).astype(F32) + xm
    for _ in range(int(math.log2(c)) - 1):
        xm = _bdot(xm, xm, BNN, HI)
        inv = inv + _bdot(inv, xm, BNN, HI)
    x = _bdot(inv, r, BNN, HI)
    return x, (inv, x)


def _unit_lower_solve_bwd(res, g):
    inv, x = res
    dr = _bdot(inv, g, BTN, HI)
    return -_bdot(dr, x, BNT, HI), dr


@jax.custom_vjp
def _unit_lower_solve_given(a, r, inv):
    return _bdot(inv, r, BNN, HI)


def _unit_lower_solve_given_fwd(a, r, inv):
    x = _bdot(inv, r, BNN, HI)
    return x, (inv, x)


def _unit_lower_solve_given_bwd(res, g):
    da, dr = _unit_lower_solve_bwd(res, g)
    return da, dr, jnp.zeros_like(res[0])


_unit_lower_solve_given.defvjp(_unit_lower_solve_given_fwd, _unit_lower_solve_given_bwd)


def _gdn_intra(qkv, ba, al8, dt8, inv4=None):
    tm = qkv.shape[0]
    nb = tm // CHUNK
    bsz = DN_HEADS * nb

    def heads(x0):
        return jnp.concatenate([qkv[:, x0 + h * LANE:x0 + (h + 1) * LANE].reshape(nb, CHUNK, LANE) for h in range(DN_HEADS)], axis=0)

    def spread(c0):
        return jnp.concatenate([jnp.broadcast_to(ba[:, c0 + h:c0 + h + 1], (tm, LANE)).reshape(nb, CHUNK, LANE)
                                for h in range(DN_HEADS)], axis=0)

    def per_head(v8):
        return jnp.concatenate([jnp.broadcast_to(v8[0:1, h:h + 1].reshape(1, 1, 1), (nb, 1, LANE)) for h in range(DN_HEADS)], axis=0)

    ri = lax.broadcasted_iota(jnp.int32, (bsz, CHUNK, CHUNK), 1)
    ci = lax.broadcasted_iota(jnp.int32, (bsz, CHUNK, CHUNK), 2)
    incl = ri >= ci
    strict = ri > ci

    q = _l2norm(heads(0)) * (DN_DK ** -0.5)
    k = _l2norm(heads(DN_QK))
    va = heads(2 * DN_QK)
    beta = _sigmoid(spread(0))
    g = -jnp.exp(per_head(al8)) * _softplus(spread(DN_HEADS) + per_head(dt8))
    gc = _bdot(incl.astype(F32), g, BNN, HI)
    g_last = jnp.sum(g, axis=1, keepdims=True)
    gcol = gc[:, :, :CHUNK]
    diff = gcol - jnp.swapaxes(gcol, 1, 2)
    decay = jnp.where(incl, jnp.exp(jnp.where(incl, diff, 0.0)), 0.0)
    kb = k * beta
    a_mat = jnp.where(strict, _bmmb_nt(kb, k) * decay, 0.0)
    egc = jnp.exp(gc)
    rhs = jnp.concatenate([kb * egc, va * beta], axis=2)
    if inv4 is None:
        wu, (inv, _) = _unit_lower_solve_fwd(a_mat, rhs)
    else:
        wu = _unit_lower_solve_given(a_mat, rhs, inv4.reshape(bsz, CHUNK, CHUNK))
    attn = jnp.where(incl, _bmmb_nt(q, k) * decay, 0.0)

    def unheads(x):
        return jnp.concatenate([x[h * nb:(h + 1) * nb].reshape(tm, LANE) for h in range(DN_HEADS)], axis=1)

    out = (unheads(wu[:, :, :DN_DK]), unheads(wu[:, :, DN_DK:]), unheads(q * egc), unheads(k * jnp.exp(g_last - gc)),
           attn.reshape(DN_HEADS, tm, CHUNK), unheads(jnp.broadcast_to(g_last, (bsz, CHUNK, LANE))))
    return out if inv4 is not None else out + (inv.reshape(DN_HEADS, nb, CHUNK, CHUNK),)


def _gdn_scan_step(w, u, qg, kd, att, gl, s):
    v_new = u - _mmb(w, s)
    o = _mmb(qg, s) + _mmb(att, v_new)
    return o, s * jnp.exp(gl) + _mmb_tn(kd, v_new)


def _gdn_intra_specs(t, tm, dts):
    nb = tm // CHUNK
    specs = [pl.BlockSpec((tm, DN_VW), lambda i: (i, 0))] * 4
    specs += [pl.BlockSpec((DN_HEADS, tm, CHUNK), lambda i: (0, i, 0)), pl.BlockSpec((tm, DN_VW), lambda i: (i, 0))]
    shapes = [jax.ShapeDtypeStruct((t, DN_VW), dts[i]) for i in range(4)]
    shapes += [jax.ShapeDtypeStruct((DN_HEADS, t, CHUNK), dts[4]), jax.ShapeDtypeStruct((t, DN_VW), dts[5])]
    return specs, shapes


def _gdn_intra_fwd(qkv, proj, al8, dt8, tm):
    t = qkv.shape[0]

    def body(qkv_ref, ba_ref, al_ref, dt_ref, *outs):
        for o, val in zip(outs, _gdn_intra(qkv_ref[...], ba_ref[...], al_ref[...], dt_ref[...])):
            o[...] = val.astype(o.dtype)

    specs, shapes = _gdn_intra_specs(t, tm, (BF, F32, BF, BF, BF, F32))
    specs.append(_gdn_inverse_spec(tm))
    shapes.append(jax.ShapeDtypeStruct((DN_HEADS, t // CHUNK, CHUNK, CHUNK), F32))
    res = pl.pallas_call(
        body, grid=(t // tm,),
        in_specs=[pl.BlockSpec((tm, DN_CONV_CH), lambda i: (i, 0)), pl.BlockSpec((tm, LANE), lambda i: (i, P_BA // LANE)),
                  _full(al8.shape), _full(dt8.shape)],
        out_specs=specs, out_shape=shapes, name="gdn_intra_fwd",
        compiler_params=pltpu.CompilerParams(dimension_semantics=("parallel",)),
    )(qkv, proj, al8, dt8)
    return res[:6], res[6]


def _gdn_inverse_spec(tm):
    return pl.BlockSpec((DN_HEADS, tm // CHUNK, CHUNK, CHUNK), lambda i: (0, i, 0, 0))


def _gdn_intra_bwd(qkv, proj, al8, dt8, inverses, cts, tm):
    t = qkv.shape[0]

    def body(qkv_ref, ba_ref, al_ref, dt_ref, inv_ref, *refs):
        ct_refs, (dqkv_ref, dba_ref, dal_ref, ddt_ref) = refs[:6], refs[6:]

        @pl.when(pl.program_id(0) == 0)
        def _():
            dal_ref[...] = jnp.zeros_like(dal_ref)
            ddt_ref[...] = jnp.zeros_like(ddt_ref)

        _, vjp = jax.vjp(functools.partial(_gdn_intra, inv4=inv_ref[...]), qkv_ref[...], ba_ref[...], al_ref[...], dt_ref[...])
        dqkv, dba, dal, ddt = vjp(tuple(r[...] for r in ct_refs))
        dqkv_ref[...] = dqkv
        dba_ref[...] = dba.astype(dba_ref.dtype)
        dal_ref[...] += dal
        ddt_ref[...] += ddt

    specs, _ = _gdn_intra_specs(t, tm, (F32,) * 6)
    return pl.pallas_call(
        body, grid=(t // tm,),
        in_specs=[pl.BlockSpec((tm, DN_CONV_CH), lambda i: (i, 0)), pl.BlockSpec((tm, LANE), lambda i: (i, P_BA // LANE)),
                  _full(al8.shape), _full(dt8.shape), _gdn_inverse_spec(tm)] + specs,
        out_specs=[pl.BlockSpec((tm, DN_CONV_CH), lambda i: (i, 0)), pl.BlockSpec((tm, LANE), lambda i: (i, 0)),
                   _full(al8.shape), _full(dt8.shape)],
        out_shape=[jax.ShapeDtypeStruct((t, DN_CONV_CH), F32), jax.ShapeDtypeStruct((t, LANE), BF),
                   jax.ShapeDtypeStruct(al8.shape, F32), jax.ShapeDtypeStruct(dt8.shape, F32)],
        name="gdn_intra_bwd", compiler_params=pltpu.CompilerParams(dimension_semantics=("arbitrary",)),
    )(qkv, proj, al8, dt8, inverses, *cts)


def _gdn_scan_fwd(intra, tm):
    t = intra[0].shape[0]
    nb = tm // CHUNK
    nc = t // CHUNK

    def body(w_ref, u_ref, qg_ref, kd_ref, att_ref, gl_ref, o_ref, ss_ref, s_scr):
        @pl.when(pl.program_id(0) == 0)
        def _():
            s_scr[...] = jnp.zeros_like(s_scr)

        for cc in range(nb):
            rows = pl.ds(cc * CHUNK, CHUNK)
            for h in range(DN_HEADS):
                cols = pl.ds(h * DN_DV, DN_DV)
                s_prev = s_scr[h]
                ss_ref[cc, h] = s_prev
                o, s_new = _gdn_scan_step(w_ref[rows, cols], u_ref[rows, cols], qg_ref[rows, cols], kd_ref[rows, cols],
                                          att_ref[h, rows, :], gl_ref[pl.ds(cc * CHUNK, 1), cols], s_prev)
                o_ref[rows, cols] = o
                s_scr[h] = s_new

    specs, _ = _gdn_intra_specs(t, tm, (F32,) * 6)
    return pl.pallas_call(
        body, grid=(t // tm,), in_specs=specs,
        out_specs=[pl.BlockSpec((tm, DN_VW), lambda i: (i, 0)),
                   pl.BlockSpec((nb, DN_HEADS, DN_DK, DN_DV), lambda i: (i, 0, 0, 0))],
        out_shape=[jax.ShapeDtypeStruct((t, DN_VW), F32), jax.ShapeDtypeStruct((nc, DN_HEADS, DN_DK, DN_DV), F32)],
        scratch_shapes=[pltpu.VMEM((DN_HEADS, DN_DK, DN_DV), F32)], name="gdn_scan_fwd",
        compiler_params=pltpu.CompilerParams(dimension_semantics=("arbitrary",)),
    )(*intra)


def _gdn_scan_bwd(intra, states, do, tm):
    t = intra[0].shape[0]
    nb = tm // CHUNK
    ng = t // tm

    def body(w_ref, u_ref, qg_ref, kd_ref, att_ref, gl_ref, ss_ref, do_ref,
             dw_ref, du_ref, dqg_ref, dkd_ref, datt_ref, dgl_ref, ds_scr):
        @pl.when(pl.program_id(0) == 0)
        def _():
            ds_scr[...] = jnp.zeros_like(ds_scr)

        for cc in reversed(range(nb)):
            rows = pl.ds(cc * CHUNK, CHUNK)
            for h in range(DN_HEADS):
                cols = pl.ds(h * DN_DV, DN_DV)
                f32 = lambda r: r[rows, cols].astype(F32)
                _, vjp = jax.vjp(_gdn_scan_step, f32(w_ref), u_ref[rows, cols], f32(qg_ref), f32(kd_ref),
                                 att_ref[h, rows, :].astype(F32), gl_ref[pl.ds(cc * CHUNK, 1), cols], ss_ref[cc, h])
                dw, du, dqg, dkd, datt, dgl, ds_prev = vjp((do_ref[rows, cols], ds_scr[h]))
                dw_ref[rows, cols] = dw
                du_ref[rows, cols] = du
                dqg_ref[rows, cols] = dqg
                dkd_ref[rows, cols] = dkd
                datt_ref[h, rows, :] = datt
                first_row = lax.broadcasted_iota(jnp.int32, (CHUNK, DN_DV), 0) == 0
                dgl_ref[rows, cols] = jnp.where(first_row, dgl, 0.0)
                ds_scr[h] = ds_prev

    rev = lambda i: (ng - 1 - i, 0)
    rev3 = lambda i: (0, ng - 1 - i, 0)
    row = pl.BlockSpec((tm, DN_VW), rev)
    six = [row] * 4 + [pl.BlockSpec((DN_HEADS, tm, CHUNK), rev3), row]
    _, shapes = _gdn_intra_specs(t, tm, (F32,) * 6)
    return pl.pallas_call(
        body, grid=(ng,),
        in_specs=six + [pl.BlockSpec((nb, DN_HEADS, DN_DK, DN_DV), lambda i: (ng - 1 - i, 0, 0, 0)), row],
        out_specs=six, out_shape=shapes,
        scratch_shapes=[pltpu.VMEM((DN_HEADS, DN_DK, DN_DV), F32)], name="gdn_scan_bwd",
        compiler_params=pltpu.CompilerParams(dimension_semantics=("arbitrary",)),
    )(*intra, states, do)


def _gdn_out(o, z, g):
    parts = []
    for h in range(DN_HEADS):
        sl = slice(h * DN_DV, (h + 1) * DN_DV)
        parts.append(_rmsnorm(o[:, sl], g) * _silu(z[:, sl]))
    return parts


_Q_SCALE = math.log2(math.e) / math.sqrt(QK_NOPE + QK_ROPE)


def _rope_tables(pos, inv_freq2):
    lane = lax.broadcasted_iota(jnp.int32, (1, LANE), 1)
    ang = pos * inv_freq2
    cos = jnp.where(lane < QK_ROPE, jnp.cos(ang), 0.0)
    sin = jnp.where(lane < QK_ROPE // 2, -jnp.sin(ang), jnp.where(lane < QK_ROPE, jnp.sin(ang), 0.0))
    return cos, sin


def _rope_swap():
    ri = lax.broadcasted_iota(jnp.int32, (LANE, LANE), 0)
    ci = lax.broadcasted_iota(jnp.int32, (LANE, LANE), 1)
    half = QK_ROPE // 2
    return (((ci < half) & (ri == ci + half)) | ((ci >= half) & (ci < QK_ROPE) & (ri == ci - half))).astype(F32)


def _mla_prep(cq, ckv, kr, gq, gkv, w_uq, w_ukv, cos, sin, swap):
    rope = lambda u: u * cos + _doth(u, swap) * sin
    q_lin = _mmb_nt(_rmsnorm(cq, gq), w_uq) * _Q_SCALE
    kv_lin = _mmb_nt(_rmsnorm(ckv, gkv), w_ukv)
    k_rope = rope(kr)
    qs, ks, vs = [], [], []
    for h in range(MLA_HEADS):
        qs += [q_lin[:, h * LANE:(h + 1) * LANE], rope(q_lin[:, (MLA_HEADS + h) * LANE:(MLA_HEADS + h + 1) * LANE])]
        ks += [kv_lin[:, 2 * h * LANE:(2 * h + 1) * LANE], k_rope]
        vs += [kv_lin[:, (2 * h + 1) * LANE:(2 * h + 2) * LANE]]
    return qs + ks + vs


def _mla_prep_fwd(proj, pos_col, inv_freq2, gq, gkv, w_uq, w_ukv, tm):
    t = proj.shape[0]
    nq = 2 * MLA_HEADS

    def body(cq_ref, ckv_ref, kr_ref, pos_ref, f_ref, gq_ref, gkv_ref, wq_ref, wkv_ref, q_ref, k_ref, v_ref):
        cos, sin = _rope_tables(pos_ref[...], f_ref[...])
        outs = _mla_prep(cq_ref[...], ckv_ref[...], kr_ref[...], gq_ref[...], gkv_ref[...], wq_ref[...], wkv_ref[...],
                         cos, sin, _rope_swap())
        for i in range(nq):
            q_ref[:, pl.ds(i * LANE, LANE)] = outs[i].astype(q_ref.dtype)
            k_ref[:, pl.ds(i * LANE, LANE)] = outs[nq + i].astype(k_ref.dtype)
        for h in range(MLA_HEADS):
            v_ref[:, pl.ds(h * LANE, LANE)] = outs[2 * nq + h].astype(v_ref.dtype)

    row = lambda w, j: pl.BlockSpec((tm, w), functools.partial(lambda i, j: (i, j), j=j))
    return pl.pallas_call(
        body, grid=(t // tm,),
        in_specs=[row(Q_LORA, P_CQ // Q_LORA), row(KV_LORA, P_CKV // KV_LORA), row(LANE, P_KR // LANE),
                  pl.BlockSpec((tm, 1), lambda i: (i, 0)), _full(inv_freq2.shape), _full(gq.shape), _full(gkv.shape),
                  _full(w_uq.shape), _full(w_ukv.shape)],
        out_specs=[row(nq * LANE, 0), row(nq * LANE, 0), row(MLA_VW, 0)],
        out_shape=[jax.ShapeDtypeStruct((t, nq * LANE), BF), jax.ShapeDtypeStruct((t, nq * LANE), BF),
                   jax.ShapeDtypeStruct((t, MLA_VW), BF)],
        name="mla_prep_fwd", compiler_params=pltpu.CompilerParams(dimension_semantics=("arbitrary",)),
    )(proj, proj, proj, pos_col, inv_freq2, gq, gkv, w_uq, w_ukv)


def _mla_prep_bwd(proj, pos_col, inv_freq2, gq, gkv, w_uq, w_ukv, dq, dk, dv, tm):
    t = proj.shape[0]
    nq = 2 * MLA_HEADS

    def body(cq_ref, ckv_ref, kr_ref, pos_ref, f_ref, gq_ref, gkv_ref, wq_ref, wkv_ref, dq_ref, dk_ref, dv_ref,
             dcq_ref, dckv_ref, dkr_ref, dgq_ref, dgkv_ref, dwq_ref, dwkv_ref):
        @pl.when(pl.program_id(0) == 0)
        def _():
            for o in (dgq_ref, dgkv_ref, dwq_ref, dwkv_ref):
                o[...] = jnp.zeros_like(o)

        cos, sin = _rope_tables(pos_ref[...], f_ref[...])
        f = functools.partial(_mla_prep, cos=cos, sin=sin, swap=_rope_swap())
        _, vjp = jax.vjp(f, cq_ref[...], ckv_ref[...], kr_ref[...], gq_ref[...], gkv_ref[...], wq_ref[...], wkv_ref[...])
        cts = [dq_ref[:, pl.ds(i * LANE, LANE)] for i in range(nq)]
        cts += [dk_ref[:, pl.ds(i * LANE, LANE)] for i in range(nq)]
        cts += [dv_ref[:, pl.ds(h * LANE, LANE)] for h in range(MLA_HEADS)]
        dcq, dckv, dkr, dgq, dgkv, dwq, dwkv = vjp(cts)
        dcq_ref[...] = dcq.astype(dcq_ref.dtype)
        dckv_ref[...] = dckv.astype(dckv_ref.dtype)
        dkr_ref[...] = dkr.astype(dkr_ref.dtype)
        dgq_ref[...] += dgq
        dgkv_ref[...] += dgkv
        dwq_ref[...] += dwq
        dwkv_ref[...] += dwkv

    row = lambda w, j: pl.BlockSpec((tm, w), functools.partial(lambda i, j: (i, j), j=j))
    return pl.pallas_call(
        body, grid=(t // tm,),
        in_specs=[row(Q_LORA, P_CQ // Q_LORA), row(KV_LORA, P_CKV // KV_LORA), row(LANE, P_KR // LANE),
                  pl.BlockSpec((tm, 1), lambda i: (i, 0)), _full(inv_freq2.shape), _full(gq.shape), _full(gkv.shape),
                  _full(w_uq.shape), _full(w_ukv.shape), row(nq * LANE, 0), row(nq * LANE, 0), row(MLA_VW, 0)],
        out_specs=[row(Q_LORA, 0), row(KV_LORA, 0), row(LANE, 0), _full(gq.shape), _full(gkv.shape),
                   _full(w_uq.shape), _full(w_ukv.shape)],
        out_shape=[jax.ShapeDtypeStruct((t, Q_LORA), BF), jax.ShapeDtypeStruct((t, KV_LORA), BF),
                   jax.ShapeDtypeStruct((t, LANE), BF), jax.ShapeDtypeStruct(gq.shape, F32),
                   jax.ShapeDtypeStruct(gkv.shape, F32), jax.ShapeDtypeStruct(w_uq.shape, F32),
                   jax.ShapeDtypeStruct(w_ukv.shape, F32)],
        name="mla_prep_bwd", compiler_params=pltpu.CompilerParams(dimension_semantics=("arbitrary",)),
    )(proj, proj, proj, pos_col, inv_freq2, gq, gkv, w_uq, w_ukv, dq, dk, dv)


_NEG = -1e30
_LN2 = math.log(2.0)
ATT_CHAINS = 2


def _causal(tq, tk, q0, k0):
    row = q0 + lax.broadcasted_iota(jnp.int32, (tq, tk), 0)
    col = k0 + lax.broadcasted_iota(jnp.int32, (tq, tk), 1)
    return col <= row


def _attn_fwd(q, k, v, tq, tk):
    t = q.shape[0]

    assert tk % tq == 0 or tq % tk == 0
    n_diag = max(1, tq // tk)

    th = tq // ATT_CHAINS

    def body(q_ref, k_ref, v_ref, o_ref, lse_ref):
        i = pl.program_id(1)
        n_full = (i * tq) // tk

        def step(k0, carry, masked):
            kt = k_ref[pl.ds(k0, tk), :]
            vt = v_ref[pl.ds(k0, tk), :]
            out = []
            for c, (m, l, acc) in enumerate(carry):
                s = _dot(q_ref[pl.ds(c * th, th), :], kt, NT)
                if masked:
                    s = jnp.where(_causal(th, tk, i * tq + c * th, k0), s, _NEG)
                m_new = jnp.maximum(m, jnp.max(s, axis=-1, keepdims=True))
                p = jnp.exp2(s - m_new)
                alpha = jnp.exp2(m - m_new)
                out.append((m_new, alpha * l + jnp.sum(p, axis=-1, keepdims=True), alpha * acc + _dot(p.astype(BF), vt)))
            return tuple(out)

        init = tuple((jnp.full((th, 1), _NEG, F32), jnp.zeros((th, 1), F32), jnp.zeros((th, V_HEAD), F32)) for _ in range(ATT_CHAINS))
        carry = lax.fori_loop(0, n_full, lambda j, c: step(pl.multiple_of(j * tk, tk), c, False), init)
        for dd in range(n_diag):
            carry = step(pl.multiple_of((n_full + dd) * tk, tk), carry, True)
        for c, (m, l, acc) in enumerate(carry):
            o_ref[pl.ds(c * th, th), :] = acc / l
            lse_ref[pl.ds(c * th, th), :] = jnp.broadcast_to(m + jnp.log2(l), (th, LANE))

    return pl.pallas_call(
        body, grid=(MLA_HEADS, t // tq),
        in_specs=[pl.BlockSpec((tq, 2 * LANE), lambda h, i: (i, h)), pl.BlockSpec((t, 2 * LANE), lambda h, i: (0, h)),
                  pl.BlockSpec((t, V_HEAD), lambda h, i: (0, h))],
        out_specs=[pl.BlockSpec((tq, V_HEAD), lambda h, i: (i, h)), pl.BlockSpec((tq, LANE), lambda h, i: (i, h))],
        out_shape=[jax.ShapeDtypeStruct((t, MLA_VW), F32), jax.ShapeDtypeStruct((t, MLA_HEADS * LANE), F32)],
        name="attn_fwd", compiler_params=pltpu.CompilerParams(dimension_semantics=("parallel", "arbitrary")),
    )(q, k, v)


def _attn_bwd(q, k, v, do, lse, delta, tq, tk):
    t = q.shape[0]
    nkt = t // tk
    assert tk % tq == 0

    def body(q_ref, k_ref, v_ref, do_ref, lse_ref, dl_ref, dq_ref, dk_ref, dv_ref):
        j = pl.program_id(1)

        @pl.when(j == 0)
        def _():
            dq_ref[...] = jnp.zeros_like(dq_ref)

        kt = k_ref[...]
        vt = v_ref[...]

        def step(q0, carry, masked):
            dk, dv = carry
            rows = pl.ds(q0, tq)
            qt = q_ref[rows, :]
            dot_ = do_ref[rows, :]
            p = jnp.exp2(_dot(qt, kt, NT) - lse_ref[rows, pl.ds(0, 1)])
            if masked:
                p = jnp.where(_causal(tq, tk, q0, j * tk), p, 0.0)
            dv = dv + _dot(p.astype(BF), dot_, TN)
            ds = (p * (_dot(dot_, vt, NT) - dl_ref[rows, pl.ds(0, 1)])).astype(BF)
            dk = dk + _dot(ds, qt, TN)
            dq_ref[rows, :] += _dot(ds, kt)
            return dk, dv

        per = tk // tq
        carry = (jnp.zeros((tk, 2 * LANE), F32), jnp.zeros((tk, V_HEAD), F32))
        for dd in range(per):
            carry = step(pl.multiple_of(j * tk + dd * tq, tq), carry, True)

        def group(g, c):
            for dd in range(per):
                c = step(pl.multiple_of(g * tk + dd * tq, tq), c, False)
            return c

        dk, dv = lax.fori_loop(j + 1, nkt, group, carry)
        dk_ref[...] = dk * _LN2
        dv_ref[...] = dv

        @pl.when(j == nkt - 1)
        def _():
            dq_ref[...] = dq_ref[...] * _LN2

    return pl.pallas_call(
        body, grid=(MLA_HEADS, nkt),
        in_specs=[pl.BlockSpec((t, 2 * LANE), lambda h, j: (0, h)), pl.BlockSpec((tk, 2 * LANE), lambda h, j: (j, h)),
                  pl.BlockSpec((tk, V_HEAD), lambda h, j: (j, h)), pl.BlockSpec((t, V_HEAD), lambda h, j: (0, h)),
                  pl.BlockSpec((t, LANE), lambda h, j: (0, h)), pl.BlockSpec((t, LANE), lambda h, j: (0, h))],
        out_specs=[pl.BlockSpec((t, 2 * LANE), lambda h, j: (0, h)), pl.BlockSpec((tk, 2 * LANE), lambda h, j: (j, h)),
                   pl.BlockSpec((tk, V_HEAD), lambda h, j: (j, h))],
        out_shape=[jax.ShapeDtypeStruct((t, MLA_HEADS * 2 * LANE), F32), jax.ShapeDtypeStruct((t, MLA_HEADS * 2 * LANE), F32),
                   jax.ShapeDtypeStruct((t, MLA_VW), F32)],
        name="attn_bwd", compiler_params=pltpu.CompilerParams(dimension_semantics=("parallel", "arbitrary")),
    )(q, k, v, do, lse, delta)


def _adam_update(w, g, m, v):
    mm = ADAM_B1 * m + (1.0 - ADAM_B1) * g
    vv = ADAM_B2 * v + (1.0 - ADAM_B2) * jnp.square(g)
    m_hat = mm / (1.0 - ADAM_B1 ** ADAM_STEP)
    v_hat = vv / (1.0 - ADAM_B2 ** ADAM_STEP)
    return -ADAM_LR * (m_hat / (jnp.sqrt(v_hat) + ADAM_EPS) + ADAM_WD * w), mm, vv


def _adamw(w, g, m, v, name):
    r, c = w.shape
    tr = max([r // s for s in range(1, r // 8 + 1) if r % s == 0 and (r // s) % 8 == 0 and r // s <= 256] or [r])
    slots = g.ndim == 3

    def body(w_ref, g_ref, m_ref, v_ref, g_out, d_ref, nm_ref, nv_ref):
        if slots:
            gg = g_ref[0].astype(F32)
            for s in range(1, N_DEV):
                gg = gg + g_ref[s].astype(F32)
        else:
            gg = g_ref[...]
        g_out[...] = gg
        d_ref[...], nm_ref[...], nv_ref[...] = _adam_update(w_ref[...], gg, m_ref[...], v_ref[...])

    spec = pl.BlockSpec((tr, c), lambda i: (i, 0))
    g_spec = pl.BlockSpec((N_DEV, tr, c), lambda i: (0, i, 0)) if slots else spec
    return pl.pallas_call(
        body, grid=(r // tr,), in_specs=[spec, g_spec, spec, spec], out_specs=[spec] * 4,
        out_shape=[jax.ShapeDtypeStruct((r, c), F32)] * 4, name=name,
        compiler_params=pltpu.CompilerParams(dimension_semantics=("arbitrary",)),
    )(w, g, m, v)


def _adamw_many(ws, gs, ms, vs, name):
    n = len(ws)

    def body(*refs):
        for i in range(n):
            w_ref, g_ref, m_ref, v_ref = (refs[j * n + i] for j in range(4))
            d_ref, nm_ref, nv_ref = (refs[(4 + j) * n + i] for j in range(3))
            d_ref[...], nm_ref[...], nv_ref[...] = _adam_update(w_ref[...], g_ref[...], m_ref[...], v_ref[...])

    shapes = [jax.ShapeDtypeStruct(w.shape, F32) for w in ws]
    outs = pl.pallas_call(body, out_shape=shapes * 3, name=name)(*ws, *gs, *ms, *vs)
    return outs[:n], outs[n:2 * n], outs[2 * n:]


def _cast_bf16(xs, name, after=None):
    n = len(xs)
    extra = [] if after is None else [after]

    def body(*refs):
        outs = refs[n + len(extra):]
        for i in range(n):
            outs[i][...] = refs[i][...].astype(BF)

    vmem = pl.BlockSpec(memory_space=pltpu.VMEM)
    return pl.pallas_call(
        body, out_shape=[jax.ShapeDtypeStruct(x.shape, BF) for x in xs], name=name,
        in_specs=[vmem] * n + [pl.BlockSpec(memory_space=pl.ANY)] * len(extra), out_specs=[vmem] * n)(*xs, *extra)


def _pad_rows(a, n):
    return jnp.pad(a, ((0, n - a.shape[0]), (0, 0)))


def _w_in_to_padded(wt):
    s_ba = P_CQ
    s_cq = s_ba + 2 * DN_HEADS
    s_kr = s_cq + Q_LORA + KV_LORA
    return jnp.concatenate([wt[:s_ba], wt[s_cq:s_kr], _pad_rows(wt[s_ba:s_cq], LANE), _pad_rows(wt[s_kr:], LANE)], axis=0)


def _w_in_from_padded(wt):
    return jnp.concatenate([wt[:P_CQ], wt[P_BA:P_BA + 2 * DN_HEADS], wt[P_CQ:P_BA], wt[P_KR:P_KR + QK_ROPE]], axis=0)


def _w_uq_to_padded(wt):
    w3 = wt.reshape(MLA_HEADS, QK_NOPE + QK_ROPE, Q_LORA)
    nope = w3[:, :QK_NOPE].reshape(MLA_HEADS * QK_NOPE, Q_LORA)
    rope = jnp.pad(w3[:, QK_NOPE:], ((0, 0), (0, LANE - QK_ROPE), (0, 0))).reshape(MLA_HEADS * LANE, Q_LORA)
    return jnp.concatenate([nope, rope], axis=0)


def _w_uq_from_padded(wt):
    nope = wt[:MLA_HEADS * QK_NOPE].reshape(MLA_HEADS, QK_NOPE, Q_LORA)
    rope = wt[MLA_HEADS * QK_NOPE:].reshape(MLA_HEADS, LANE, Q_LORA)[:, :QK_ROPE]
    return jnp.concatenate([nope, rope], axis=1).reshape(MLA_HEADS * (QK_NOPE + QK_ROPE), Q_LORA)


def _pack(pieces, width, row_mult):
    flat = jnp.concatenate([p.reshape(-1) for p in pieces])
    n = flat.shape[0]
    rows = -(-n // (width * row_mult)) * row_mult
    return jnp.pad(flat, (0, rows * width - n)).reshape(rows, width)


def _unpack(flat, shapes):
    out, o = [], 0
    for s in shapes:
        n = math.prod(s)
        out.append(flat[o:o + n].reshape(s))
        o += n
    return out


def kernel(x, c, positions, w_ada, b_ada, w_in, conv_w, a_log, dt_bias, dn_norm_g, q_norm_g, w_uq, kv_norm_g, w_ukv, w_o, ln1_g, ln1_b, w_gate, w_up, w_down, ln2_g, ln2_b, loss_target, m_w_ada, m_b_ada, m_w_in, m_conv_w, m_a_log, m_dt_bias, m_dn_norm_g, m_q_norm_g, m_w_uq, m_kv_norm_g, m_w_ukv, m_w_o, m_ln1_g, m_ln1_b, m_w_gate, m_w_up, m_w_down, m_ln2_g, m_ln2_b, v_w_ada, v_b_ada, v_w_in, v_conv_w, v_a_log, v_dt_bias, v_dn_norm_g, v_q_norm_g, v_w_uq, v_kv_norm_g, v_w_ukv, v_w_o, v_ln1_g, v_ln1_b, v_w_gate, v_w_up, v_w_down, v_ln2_g, v_ln2_b):
    me = 4 * lax.axis_index("x") + 2 * lax.axis_index("y") + lax.axis_index("c")
    t, d = x.shape[1], x.shape[2]
    ada_n = w_ada.shape[2]

    tr = lambda w: w[0].T
    rows = lambda a: a.reshape(-1, a.shape[2])
    (in_shard,) = _cast_bf16([tr(w_in)], "cast_w_in")
    cw = conv_w.shape[3]
    a_in, c_all, conv_all = _gather_by_chip([in_shard, c, conv_w[0, :, 0, :]], "gather_w_in_and_small")
    c_all = c_all.reshape(N_DEV, d)
    conv_full = conv_all.transpose(1, 0, 2).reshape(CONV_K, N_DEV * cw)
    conv_w8 = jnp.pad(conv_full, ((0, 8 - CONV_K), (0, 0)))

    b_ada_mine = lax.dynamic_slice(b_ada, (0, me * ada_n), (1, ada_n))
    mod_cols = _mod_fwd(c_all, w_ada[0], b_ada_mine)
    (mod_all,) = _exchange([mod_cols.reshape(N_DEV, 1, ada_n)], "scatter_mod", scatter=True)
    mod = mod_all.reshape(1, N_DEV * ada_n)

    later = _cast_bf16([tr(w_uq), tr(w_ukv), w_o[0], tr(w_gate), tr(w_up), w_down[0]], "cast_weights", after=a_in)
    mixer_gather, token_a = _exchange_start(later[:3], "gather_mixer_weights_start", scatter=False)
    ffn_gather, token_b = _exchange_start(later[3:], "gather_ffn_weights_start", scatter=False)
    mod = mod + (token_a + token_b)
    w_in_t = _w_in_to_padded(rows(a_in))

    def mixer_weights(after):
        a_uq, a_ukv, a_o = _exchange_wait(mixer_gather, after, "gather_mixer_weights_wait", scatter=False)
        return _w_uq_to_padded(rows(a_uq)), rows(a_ukv), rows(a_o)

    def ffn_weights(after):
        a_gate, a_up, a_down = _exchange_wait(ffn_gather, after, "gather_ffn_weights_wait", scatter=False)
        return rows(a_gate), rows(a_up), rows(a_down)

    def by_dest(g):
        return g.reshape(N_DEV, -1, g.shape[1])

    scatters = {}

    def grads_ready(tag, *g):
        if tag == "ffn":
            pieces = [by_dest(a) for a in g]
        elif tag == "mixer":
            g_w_o, g_w_uq_t, g_w_ukv_t = g
            pieces = [by_dest(g_w_o), by_dest(_w_uq_from_padded(g_w_uq_t).astype(BF)), by_dest(g_w_ukv_t.astype(BF))]
        else:
            pieces = [by_dest(_w_in_from_padded(g[0]))]
        scatters[tag], token = _exchange_start(pieces, "scatter_%s_grads_start" % tag, scatter=True)
        return token

    loc = _local_step(x[0], loss_target[0], positions[0], mod, w_in_t, mixer_weights, ffn_weights, grads_ready,
                      conv_w8, a_log, dt_bias, dn_norm_g, q_norm_g, kv_norm_g, ln1_g, ln1_b, ln2_g, ln2_b)
    grad_x, loss_acc, dmod, d_conv8, d_al8, d_dt8, d_dn_g, d_q_g, d_kv_g, d_ln1_g, d_ln1_b, d_ln2_g, d_ln2_b = loc

    small_shapes = [(6 * d,), (CONV_K, N_DEV * cw), (DN_HEADS,), (DN_HEADS,), (DN_DV,), (Q_LORA,), (KV_LORA,), (d,), (d,), (d,), (d,), (1,)]
    gsmall = _pack([dmod, d_conv8[:CONV_K], d_al8[0, :DN_HEADS], d_dt8[0, :DN_HEADS], d_dn_g, d_q_g, d_kv_g,
                    d_ln1_g, d_ln1_b, d_ln2_g, d_ln2_b, loss_acc[0, :1]], LANE, 8)
    (gsmall_all,) = _exchange([gsmall], "gather_small_grads", scatter=False)
    dmod_all = gsmall_all.reshape(N_DEV, -1)[:, :6 * d]
    tot = _unpack(_sum_slots(gsmall_all, "sum_small_grads").reshape(-1), small_shapes)
    g_b_ada, g_conv_full, g_a_log, g_dt_bias, g_dn_g, g_q_g, g_kv_g, g_ln1_g, g_ln1_b, g_ln2_g, g_ln2_b, loss1 = tot
    loss = loss1.reshape(())
    g_conv_w = lax.dynamic_slice(g_conv_full, (0, me * cw), (CONV_K, cw))
    g_w_ada = _mod_bwd(c_all.T, lax.dynamic_slice(dmod_all, (0, me * ada_n), (N_DEV, ada_n)))

    grads = {"w_ada": g_w_ada[None], "b_ada": g_b_ada[None], "conv_w": g_conv_w[None, :, None, :],
             "a_log": g_a_log[None], "dt_bias": g_dt_bias[None], "dn_norm_g": g_dn_g[None], "q_norm_g": g_q_g[None],
             "kv_norm_g": g_kv_g[None], "ln1_g": g_ln1_g[None], "ln1_b": g_ln1_b[None], "ln2_g": g_ln2_g[None], "ln2_b": g_ln2_b[None]}
    weights = dict(w_ada=w_ada, b_ada=b_ada, w_in=w_in, conv_w=conv_w, a_log=a_log, dt_bias=dt_bias, dn_norm_g=dn_norm_g,
                   q_norm_g=q_norm_g, w_uq=w_uq, kv_norm_g=kv_norm_g, w_ukv=w_ukv, w_o=w_o, ln1_g=ln1_g, ln1_b=ln1_b,
                   w_gate=w_gate, w_up=w_up, w_down=w_down, ln2_g=ln2_g, ln2_b=ln2_b)
    ms = dict(w_ada=m_w_ada, b_ada=m_b_ada, w_in=m_w_in, conv_w=m_conv_w, a_log=m_a_log, dt_bias=m_dt_bias,
              dn_norm_g=m_dn_norm_g, q_norm_g=m_q_norm_g, w_uq=m_w_uq, kv_norm_g=m_kv_norm_g, w_ukv=m_w_ukv, w_o=m_w_o,
              ln1_g=m_ln1_g, ln1_b=m_ln1_b, w_gate=m_w_gate, w_up=m_w_up, w_down=m_w_down, ln2_g=m_ln2_g, ln2_b=m_ln2_b)
    vs = dict(w_ada=v_w_ada, b_ada=v_b_ada, w_in=v_w_in, conv_w=v_conv_w, a_log=v_a_log, dt_bias=v_dt_bias,
              dn_norm_g=v_dn_norm_g, q_norm_g=v_q_norm_g, w_uq=v_w_uq, kv_norm_g=v_kv_norm_g, w_ukv=v_w_ukv, w_o=v_w_o,
              ln1_g=v_ln1_g, ln1_b=v_ln1_b, w_gate=v_w_gate, w_up=v_w_up, w_down=v_w_down, ln2_g=v_ln2_g, ln2_b=v_ln2_b)
    names = list(weights)
    big = ("w_ada", "w_gate", "w_up", "w_down", "w_o", "w_uq", "w_ukv", "w_in")
    waits = {"w_gate": ("ffn", ("w_gate", "w_up", "w_down")), "w_o": ("mixer", ("w_o", "w_uq", "w_ukv")), "w_in": ("in", ("w_in",))}
    delta_w, new_m, new_v, slots = {}, {}, {}, {}
    last = g_w_ada
    for n in big:
        if n == "w_in":
            rest = [r for r in names if r not in big]
            flat2 = lambda a: a.reshape(-1, a.shape[-1])
            outs = _adamw_many(*[[flat2(src[r]) for r in rest] for src in (weights, grads, ms, vs)], "adamw_small")
            for dst, o in zip((delta_w, new_m, new_v), outs):
                for r, a in zip(rest, o):
                    dst[r] = a.reshape(weights[r].shape)
            last = outs[0][0]
        transposed = n in ("w_in", "w_uq", "w_ukv", "w_gate", "w_up")
        two = (lambda a: a[0].T) if transposed else (lambda a: a[0])
        back = (lambda a: a.T[None]) if transposed else (lambda a: a[None])
        if n in waits:
            tag, members = waits[n]
            slots.update(zip(members, _exchange_wait(scatters[tag], last, "scatter_%s_grads_wait" % tag, scatter=True)))
        g_in = slots[n] if n in slots else two(grads[n])
        gr, dlt, nm, nv = _adamw(two(weights[n]), g_in, two(ms[n]), two(vs[n]), "adamw_" + n)
        grads[n], delta_w[n], new_m[n], new_v[n] = back(gr), back(dlt), back(nm), back(nv)
        last = nv

    return (loss, grad_x[None], *[grads[n] for n in names], *[delta_w[n] for n in names],
            *[new_m[n] for n in names], *[new_v[n] for n in names])


def _local_step(xs, tgt, pos, mod, w_in_t, mixer_weights, ffn_weights, grads_ready, conv_w8,
                a_log, dt_bias, dn_norm_g, q_norm_g, kv_norm_g, ln1_g, ln1_b, ln2_g, ln2_b):
    t, d = xs.shape
    sh_m, sc_m, gt_m, sh_f, sc_f, gt_f = [mod[:, i * d:(i + 1) * d] for i in range(6)]
    pos_col = pos.astype(F32).reshape(t, 1)
    inv_freq = 1.0 / (ROPE_THETA ** (jnp.arange(0, QK_ROPE, 2, dtype=F32) / QK_ROPE))
    inv_freq2 = jnp.pad(jnp.concatenate([inv_freq, inv_freq]), (0, LANE - QK_ROPE)).reshape(1, LANE)
    al8 = jnp.pad(a_log, ((0, 7), (0, LANE - DN_HEADS)))
    dt8 = jnp.pad(dt_bias, ((0, 7), (0, LANE - DN_HEADS)))

    tm = min(512, t)
    tq = min(256, t)
    tk = min(512, t)

    (h1,) = _rowwise("modulate_in", lambda xx, sc, sh: xx * (1.0 + sc) + sh, [xs], [sc_m, sh_m], [(d, BF)], [], tm)
    proj = _matmul(h1, w_in_t, "nt", "in_proj")
    qkv = _conv_fwd(proj, conv_w8, min(256, t))
    gdn_tm = min(512, t)
    intra, inverses = _gdn_intra_fwd(qkv, proj, al8, dt8, gdn_tm)
    o_dn, states = _gdn_scan_fwd(intra, gdn_tm)
    w_uq_t, w_ukv_t, w_o_f = mixer_weights(states)
    qc, kc, vc = _mla_prep_fwd(proj, pos_col, inv_freq2, q_norm_g, kv_norm_g, w_uq_t, w_ukv_t, tm)
    o_mla, lse = _attn_fwd(qc, kc, vc, min(1024, t), min(1024, t))

    def mix_in(o, z, om, g):
        return jnp.concatenate(_gdn_out(o, z, g) + [om], axis=1)

    (mixin,) = _rowwise("mixer_out", mix_in, [o_dn, (proj, DN_VW, P_Z // DN_VW), o_mla], [dn_norm_g], [(2 * DN_VW, BF)], [], tm)
    mix = _matmul(mixin, w_o_f, "nn", "out_proj")

    def block1(xx, mx, gt, g1, b1, sc, sh):
        x1 = _layernorm(DEEPNORM_ALPHA * xx + gt * mx, g1, b1)
        return x1, x1 * (1.0 + sc) + sh

    x1, h2 = _rowwise("norm1_modulate", block1, [xs, mix], [gt_m, ln1_g, ln1_b, sc_f, sh_f], [(d, F32), (d, BF)], [], tm)
    w_gate_f, w_up_f, w_down_f = ffn_weights(h2)
    act, gate, up = _ffn_in(h2, w_gate_f, w_up_f)
    ff = _matmul(act, w_down_f, "nn", "ffn_out")

    def tail_loss(x1_, ff_, gt, g2, b2, tg):
        y = _layernorm(DEEPNORM_ALPHA * x1_ + gt * ff_, g2, b2)
        return 0.5 * jnp.sum(jnp.mean(jnp.square(y - tg), axis=-1))

    def tail(x1_, ff_, tg, gt, g2, b2):
        loss, (dx1, dff, dgt, dg2, db2) = jax.value_and_grad(tail_loss, argnums=(0, 1, 2, 3, 4))(x1_, ff_, gt, g2, b2, tg)
        return dx1, dff, jnp.full((1, LANE), loss, F32), dgt, dg2, db2

    dx1_a, dff, loss_acc, d_gt_f, d_ln2_g, d_ln2_b = _rowwise(
        "norm2_loss", tail, [x1, ff, tgt], [gt_f, ln2_g, ln2_b], [(d, F32), (d, BF)], [(1, LANE), (1, d), (1, d), (1, d)], tm)

    g_w_down = _matmul(act, dff, "tn", "d_w_down", BF)
    dgate, dup = _ffn_act_bwd(dff, w_down_f, gate, up)
    g_w_gate = _matmul(dgate, h2, "tn", "d_w_gate", BF)
    g_w_up = _matmul(dup, h2, "tn", "d_w_up", BF)
    token = grads_ready("ffn", g_w_gate, g_w_up, g_w_down)
    dh2 = _matmul2_nn(dgate, w_gate_f, dup, w_up_f, "d_ffn_in")

    def block1_bwd(xx, mx, dx1_, dh2_, gt, g1, b1, sc, sh):
        _, vjp = jax.vjp(block1, xx, mx, gt, g1, b1, sc, sh)
        dxx, dmx, dgt, dg1, db1, dsc, dsh = vjp((dx1_, dh2_))
        return dxx, dmx, dgt, dg1, db1, dsc, dsh

    dx_a, dmix, d_gt_m, d_ln1_g, d_ln1_b, d_sc_f, d_sh_f = _rowwise(
        "norm1_modulate_bwd", block1_bwd, [xs, mix, dx1_a, dh2], [gt_m + token, ln1_g, ln1_b, sc_f, sh_f],
        [(d, F32), (d, BF)], [(1, d)] * 5, min(256, t))

    dmixin = _matmul(dmix, w_o_f, "nt", "d_mixer_out")
    g_w_o = _matmul(mixin, dmix, "tn", "d_w_o", BF)

    def mixer_bwd(o, z, om, dmi, g):
        _, vjp = jax.vjp(lambda o_, z_, g_: jnp.concatenate(_gdn_out(o_, z_, g_), axis=1), o, z, g)
        do_, dz_, dg_ = vjp(dmi[:, :DN_VW])
        dom = dmi[:, DN_VW:]
        delta = [jnp.broadcast_to(jnp.sum(dom[:, h * V_HEAD:(h + 1) * V_HEAD] * om[:, h * V_HEAD:(h + 1) * V_HEAD], axis=-1, keepdims=True), (o.shape[0], LANE))
                 for h in range(MLA_HEADS)]
        return do_, dz_, dom, jnp.concatenate(delta, axis=1), dg_

    do_dn, dz, do_mla, delta, d_dn_g = _rowwise(
        "mixer_out_bwd", mixer_bwd, [o_dn, (proj, DN_VW, P_Z // DN_VW), o_mla, dmixin], [dn_norm_g],
        [(DN_VW, F32), (DN_VW, BF), (MLA_VW, BF), (MLA_HEADS * LANE, F32)], [(1, DN_DV)], tm)

    dqc, dkc, dvc = _attn_bwd(qc, kc, vc, do_mla, lse, delta, min(512, t), min(1024, t))
    dcq, dckv, dkr, d_q_g, d_kv_g, g_w_uq_t, g_w_ukv_t = _mla_prep_bwd(
        proj, pos_col, inv_freq2, q_norm_g, kv_norm_g, w_uq_t, w_ukv_t, dqc, dkc, dvc, min(256, t))

    token = grads_ready("mixer", g_w_o, g_w_uq_t, g_w_ukv_t)

    d_intra = _gdn_scan_bwd(intra, states, do_dn, gdn_tm)
    dqkv_act, dba, d_al8, d_dt8 = _gdn_intra_bwd(qkv, proj, al8 + token, dt8, inverses, d_intra, min(256, t))
    dqkv_pre, d_conv8 = _conv_bwd(proj, conv_w8, dqkv_act, min(256, t))

    dproj = jnp.concatenate([dqkv_pre, dz, dcq, dckv, dba, dkr], axis=1)
    dh1 = _matmul(dproj, w_in_t, "nn", "d_in_proj")
    g_w_in_t = _matmul(dproj, h1, "tn", "d_w_in", BF)
    token = grads_ready("in", g_w_in_t)

    def modulate_bwd(xx, dh, dxa, sc):
        return dh * (1.0 + sc) + dxa, jnp.sum(dh * xx, axis=0, keepdims=True), jnp.sum(dh, axis=0, keepdims=True)

    grad_x, d_sc_m, d_sh_m = _rowwise("modulate_in_bwd", modulate_bwd, [xs, dh1, dx_a], [sc_m + token], [(d, F32)], [(1, d), (1, d)], tm)
    dmod = jnp.concatenate([d_sh_m, d_sc_m, d_gt_m, d_sh_f, d_sc_f, d_gt_f], axis=1)
    return grad_x, loss_acc, dmod, d_conv8, d_al8, d_dt8, d_dn_g, d_q_g, d_kv_g, d_ln1_g, d_ln1_b, d_ln2_g, d_ln2_b
```

```python
import functools
import math

import jax
import jax.numpy as jnp
from jax import lax
from jax.experimental import pallas as pl
from jax.experimental.pallas import tpu as pltpu

F32 = jnp.float32
BF = jnp.bfloat16
HI = lax.Precision.HIGHEST

N_DEV = 8
DN_HEADS = 4
DN_DK = 128
DN_DV = 128
CONV_K = 4
CHUNK = 64
MLA_HEADS = 4
QK_NOPE = 128
QK_ROPE = 64
V_HEAD = 128
Q_LORA = 512
KV_LORA = 256
ROPE_THETA = 10000.0
DEPTH = 1
DEEPNORM_ALPHA = (2.0 * DEPTH) ** 0.25
LANE = 128
CONV_HALO = 8
CONV_ROWS, CONV_COLS = 64, 256

DN_QK = DN_HEADS * DN_DK
DN_VW = DN_HEADS * DN_DV
DN_CONV_CH = 2 * DN_QK + DN_VW
MLA_VW = MLA_HEADS * V_HEAD
MLA_QCAT = QK_NOPE + LANE
N_IN = DN_CONV_CH + DN_VW + 2 * DN_HEADS + Q_LORA + KV_LORA + QK_ROPE
P_QKV = 0
P_Z = DN_CONV_CH
P_CQ = P_Z + DN_VW
P_CKV = P_CQ + Q_LORA
P_BA = P_CKV + KV_LORA
P_KR = P_BA + LANE
N_INP = P_KR + LANE

ADAM_LR = 0.001
ADAM_B1 = 0.9
ADAM_B2 = 0.999
ADAM_EPS = 1e-08
ADAM_WD = 0.01
ADAM_STEP = 10

NN = (((1,), (0,)), ((), ()))
NT = (((1,), (1,)), ((), ()))
TN = (((0,), (0,)), ((), ()))


def _pick(n, prefs):
    for p in prefs:
        if n % p == 0:
            return p
    return n


def _full(shape):
    return pl.BlockSpec(shape, lambda *_: (0,) * len(shape))


def _dot(a, b, dims=NN):
    return lax.dot_general(a, b, dims, preferred_element_type=F32)


def _doth(a, b, dims=NN):
    return lax.dot_general(a, b, dims, precision=HI, preferred_element_type=F32)


@jax.custom_vjp
def _mmb(a, b):
    return _dot(a.astype(BF), b.astype(BF), NN)


def _mmb_fwd(a, b):
    return _mmb(a, b), (a, b)


def _mmb_bwd(res, g):
    a, b = res
    gb = g.astype(BF)
    return (_dot(gb, b.astype(BF), NT).astype(a.dtype), _dot(a.astype(BF), gb, TN).astype(b.dtype))


_mmb.defvjp(_mmb_fwd, _mmb_bwd)


@jax.custom_vjp
def _mmb_nt(a, b):
    return _dot(a.astype(BF), b.astype(BF), NT)


def _mmb_nt_fwd(a, b):
    return _mmb_nt(a, b), (a, b)


def _mmb_nt_bwd(res, g):
    a, b = res
    gb = g.astype(BF)
    return (_dot(gb, b.astype(BF), NN).astype(a.dtype), _dot(gb, a.astype(BF), TN).astype(b.dtype))


_mmb_nt.defvjp(_mmb_nt_fwd, _mmb_nt_bwd)


@jax.custom_vjp
def _mmb_tn(a, b):
    return _dot(a.astype(BF), b.astype(BF), TN)


def _mmb_tn_fwd(a, b):
    return _mmb_tn(a, b), (a, b)


def _mmb_tn_bwd(res, g):
    a, b = res
    gb = g.astype(BF)
    return (_dot(b.astype(BF), gb, NT).astype(a.dtype), _dot(a.astype(BF), gb, NN).astype(b.dtype))


_mmb_tn.defvjp(_mmb_tn_fwd, _mmb_tn_bwd)


def _sigmoid(x):
    return 0.5 * (jnp.tanh(0.5 * x) + 1.0)


def _silu(x):
    return x * _sigmoid(x)


def _softplus(x):
    return jnp.maximum(x, 0.0) + jnp.log(1.0 + jnp.exp(-jnp.abs(x)))


def _layernorm(x, g, b, eps=1e-5):
    mu = jnp.mean(x, axis=-1, keepdims=True)
    xc = x - mu
    var = jnp.mean(xc * xc, axis=-1, keepdims=True)
    return xc * lax.rsqrt(var + eps) * g + b


def _rmsnorm(x, g, eps=1e-6):
    return x * lax.rsqrt(jnp.mean(x * x, axis=-1, keepdims=True) + eps) * g


def _l2norm(x, eps=1e-6):
    return x * lax.rsqrt(jnp.sum(x * x, axis=-1, keepdims=True) + eps)


def _rowwise(name, fn, rows, vecs, out_rows, out_accs, tm):
    rows = [r if isinstance(r, tuple) else (r, r.shape[1], 0) for r in rows]
    t = rows[0][0].shape[0]
    tm = min(tm, t)
    assert t % tm == 0
    nr, nv, no = len(rows), len(vecs), len(out_rows)

    def body(*refs):
        ins = [r[...] for r in refs[:nr + nv]]
        outs = fn(*ins)
        outs = outs if isinstance(outs, (tuple, list)) else (outs,)
        o_rows = refs[nr + nv:nr + nv + no]
        o_accs = refs[nr + nv + no:]
        for o, val in zip(o_rows, outs[:no]):
            o[...] = val.astype(o.dtype)
        if o_accs:
            @pl.when(pl.program_id(0) == 0)
            def _():
                for o in o_accs:
                    o[...] = jnp.zeros_like(o)
            for o, val in zip(o_accs, outs[no:]):
                o[...] += val

    in_specs = [pl.BlockSpec((tm, w), functools.partial(lambda i, j: (i, j), j=j)) for (_, w, j) in rows]
    in_specs += [_full(v.shape) for v in vecs]
    out_specs = [pl.BlockSpec((tm, w), lambda i: (i, 0)) for (w, _) in out_rows]
    out_specs += [_full(s) for s in out_accs]
    out_shape = [jax.ShapeDtypeStruct((t, w), d) for (w, d) in out_rows]
    out_shape += [jax.ShapeDtypeStruct(s, F32) for s in out_accs]
    res = pl.pallas_call(
        body, grid=(t // tm,), in_specs=in_specs, out_specs=out_specs, out_shape=out_shape, name=name,
        compiler_params=pltpu.CompilerParams(dimension_semantics=("arbitrary",)),
    )(*[r[0] for r in rows], *vecs)
    return res


def _matmul(a, b, mode, name, out_dtype=F32):
    if mode == "nn":
        (m, k), n = a.shape, b.shape[1]
    elif mode == "nt":
        (m, k), n = a.shape, b.shape[0]
    else:
        (k, m), n = a.shape, b.shape[1]
    tm, tn, tk = _matmul_tiles(m, n, k, a.dtype.itemsize, b.dtype.itemsize, jnp.dtype(out_dtype).itemsize)
    nk = k // tk
    dims = {"nn": NN, "nt": NT, "tn": TN}[mode]

    def body(a_ref, b_ref, o_ref, *acc):
        part = _dot(a_ref[...].astype(BF), b_ref[...].astype(BF), dims)
        if nk == 1:
            o_ref[...] = part.astype(o_ref.dtype)
            return
        (acc_ref,) = acc
        kk = pl.program_id(2)

        @pl.when(kk == 0)
        def _():
            acc_ref[...] = part

        @pl.when(kk > 0)
        def _():
            acc_ref[...] += part

        @pl.when(kk == nk - 1)
        def _():
            o_ref[...] = acc_ref[...].astype(o_ref.dtype)

    a_spec = pl.BlockSpec((tk, tm), lambda i, j, kk: (kk, i)) if mode == "tn" else pl.BlockSpec((tm, tk), lambda i, j, kk: (i, kk))
    b_spec = pl.BlockSpec((tn, tk), lambda i, j, kk: (j, kk)) if mode == "nt" else pl.BlockSpec((tk, tn), lambda i, j, kk: (kk, j))
    return pl.pallas_call(
        body, grid=(m // tm, n // tn, nk), in_specs=[a_spec, b_spec],
        out_specs=pl.BlockSpec((tm, tn), lambda i, j, kk: (i, j)),
        out_shape=jax.ShapeDtypeStruct((m, n), out_dtype),
        scratch_shapes=[pltpu.VMEM((tm, tn), F32)] if nk > 1 else [], name=name,
        compiler_params=pltpu.CompilerParams(dimension_semantics=("parallel", "parallel", "arbitrary")),
    )(a, b)


def _lane_tile(n, cap):
    return max([n // s for s in range(1, n // LANE + 1) if n % s == 0 and (n // s) % LANE == 0 and n // s <= cap] or [n])


def _ffn_in(h, w_gate, w_up):
    m, k = h.shape
    f = w_gate.shape[0]
    tm, tn = _pick(m, (512, 256, 128)), _lane_tile(f, 1408)

    def body(h_ref, wg_ref, wu_ref, act_ref, g_ref, u_ref):
        hh = h_ref[...]
        g = _dot(hh, wg_ref[...], NT)
        u = _dot(hh, wu_ref[...], NT)
        act_ref[...] = (_silu(g) * u).astype(act_ref.dtype)
        g_ref[...] = g.astype(g_ref.dtype)
        u_ref[...] = u.astype(u_ref.dtype)

    w_spec = pl.BlockSpec((tn, k), lambda i, j: (j, 0))
    o_spec = pl.BlockSpec((tm, tn), lambda i, j: (i, j))
    return pl.pallas_call(
        body, grid=(m // tm, f // tn), in_specs=[pl.BlockSpec((tm, k), lambda i, j: (i, 0)), w_spec, w_spec],
        out_specs=[o_spec] * 3, out_shape=[jax.ShapeDtypeStruct((m, f), BF)] * 3, name="ffn_in",
        compiler_params=pltpu.CompilerParams(dimension_semantics=("parallel", "parallel")),
    )(h, w_gate, w_up)


def _ffn_act_bwd(dff, w_down, gate, up):
    m, k = dff.shape
    f = w_down.shape[0]
    tm, tn = _pick(m, (512, 256, 128)), _lane_tile(f, 1408)

    def body(d_ref, w_ref, g_ref, u_ref, dg_ref, du_ref):
        da = _dot(d_ref[...], w_ref[...], NT)
        g = g_ref[...].astype(F32)
        sg = _sigmoid(g)
        dg_ref[...] = (da * u_ref[...].astype(F32) * (sg * (1.0 + g * (1.0 - sg)))).astype(dg_ref.dtype)
        du_ref[...] = (da * (g * sg)).astype(du_ref.dtype)

    o_spec = pl.BlockSpec((tm, tn), lambda i, j: (i, j))
    return pl.pallas_call(
        body, grid=(m // tm, f // tn),
        in_specs=[pl.BlockSpec((tm, k), lambda i, j: (i, 0)), pl.BlockSpec((tn, k), lambda i, j: (j, 0)), o_spec, o_spec],
        out_specs=[o_spec] * 2, out_shape=[jax.ShapeDtypeStruct((m, f), BF)] * 2, name="d_ffn_act",
        compiler_params=pltpu.CompilerParams(dimension_semantics=("parallel", "parallel")),
    )(dff, w_down, gate, up)


def _matmul2_nn(a1, b1, a2, b2, name):
    m, k = a1.shape
    n = b1.shape[1]
    tm, tn = _pick(m, (512, 256, 128)), _pick(n, (512, 256, 128))

    def body(a1_ref, b1_ref, a2_ref, b2_ref, o_ref):
        o_ref[...] = _dot(a1_ref[...], b1_ref[...]) + _dot(a2_ref[...], b2_ref[...])

    a_spec = pl.BlockSpec((tm, k), lambda i, j: (i, 0))
    b_spec = pl.BlockSpec((k, tn), lambda i, j: (0, j))
    return pl.pallas_call(
        body, grid=(m // tm, n // tn), in_specs=[a_spec, b_spec, a_spec, b_spec],
        out_specs=pl.BlockSpec((tm, tn), lambda i, j: (i, j)), out_shape=jax.ShapeDtypeStruct((m, n), F32), name=name,
        compiler_params=pltpu.CompilerParams(dimension_semantics=("parallel", "parallel")),
    )(a1, b1, a2, b2)


MATMUL_VMEM_BUDGET = 28 * 1024 * 1024


def _matmul_tiles(m, n, k, a_bytes, b_bytes, o_bytes):
    def divisors(x, cap):
        return sorted({x // s for s in range(1, 65) if x % s == 0 and (x // s) % LANE == 0 and x // s <= cap}, reverse=True) or [x]

    for tk in divisors(k, k):
        best = None
        for tm in divisors(m, 1024):
            for tn in divisors(n, 2048):
                need = 2 * (tm * tk * a_bytes + tk * tn * b_bytes + tm * tn * o_bytes) + (tm * tn * 4 if tk < k else 0)
                if need <= MATMUL_VMEM_BUDGET and tm * tn >= 512 * 512 and (best is None or tm * tn > best[0] * best[1]):
                    best = (tm, tn)
        if best:
            return best[0], best[1], tk
    return _pick(m, (512, 256, 128)), _pick(n, (512, 256, 128)), _pick(k, (512, 256, 128))


def _exchange(xs, name, scatter):
    n = len(xs)
    npeer = N_DEV - 1

    def body(*refs):
        x_refs, o_refs = refs[:n], refs[n:2 * n]
        send_sems, recv_sems, local_sems = refs[2 * n:]
        mx, my, mc = lax.axis_index("x"), lax.axis_index("y"), lax.axis_index("c")
        me = 4 * mx + 2 * my + mc
        src_me = [x.at[me] if scatter else x for x in x_refs]
        mine = [pltpu.make_async_copy(src_me[a], o_refs[a].at[me], local_sems.at[a]) for a in range(n)]
        for cp in mine:
            cp.start()
        copies = []
        for k in range(1, N_DEV):
            px, py, pc = mx ^ (k >> 2), my ^ ((k >> 1) & 1), mc ^ (k & 1)
            peer = 4 * px + 2 * py + pc
            for a in range(n):
                cp = pltpu.make_async_remote_copy(
                    src_ref=x_refs[a].at[peer] if scatter else x_refs[a], dst_ref=o_refs[a].at[me],
                    send_sem=send_sems.at[a * npeer + k - 1], recv_sem=recv_sems.at[a * npeer + k - 1],
                    device_id=(px, py, pc), device_id_type=pl.DeviceIdType.MESH)
                cp.start()
                copies.append((cp, a, k, peer))
        for cp, a, k, peer in copies:
            pltpu.make_async_remote_copy(
                src_ref=src_me[a], dst_ref=o_refs[a].at[peer], send_sem=send_sems.at[a * npeer + k - 1],
                recv_sem=recv_sems.at[a * npeer + k - 1], device_id=(mx, my, mc),
                device_id_type=pl.DeviceIdType.MESH).wait_recv()
        for cp, _, _, _ in copies:
            cp.wait_send()
        for cp in mine:
            cp.wait()

    return pl.pallas_call(
        body, out_shape=[jax.ShapeDtypeStruct((N_DEV,) + x.shape[-2:], x.dtype) for x in xs],
        in_specs=[pl.BlockSpec(memory_space=pl.ANY)] * n, out_specs=[pl.BlockSpec(memory_space=pl.ANY)] * n,
        scratch_shapes=[pltpu.SemaphoreType.DMA((n * npeer,)), pltpu.SemaphoreType.DMA((n * npeer,)),
                        pltpu.SemaphoreType.DMA((n,))],
        name=name,
    )(*xs)


def _gather_by_chip(xs, name):
    n = len(xs)
    per = N_DEV - 1

    def body(*refs):
        x_refs, o_refs = refs[:n], refs[n:2 * n]
        send_sems, recv_sems, local_sems = refs[2 * n:]
        mx, my, mc = lax.axis_index("x"), lax.axis_index("y"), lax.axis_index("c")
        me, sibling = (mx, my, mc), (mx, my, 1 - mc)
        chips = [(1 - mx, my), (mx, 1 - my), (1 - mx, 1 - my)]
        slot = lambda d: 4 * d[0] + 2 * d[1] + d[2]

        def copy(a, k, block, to, src=None):
            dst = o_refs[a].at[slot(block)]
            return pltpu.make_async_remote_copy(
                src_ref=dst if src is None else src, dst_ref=dst, send_sem=send_sems.at[a * per + k],
                recv_sem=recv_sems.at[a * per + k], device_id=to, device_id_type=pl.DeviceIdType.MESH)

        mine = [pltpu.make_async_copy(x_refs[a], o_refs[a].at[slot(me)], local_sems.at[a]) for a in range(n)]
        for cp in mine:
            cp.start()
        first = []
        for a in range(n):
            first.append(copy(a, 0, me, sibling, src=x_refs[a]))
            first += [copy(a, 1 + j, me, (*chip, mc), src=x_refs[a]) for j, chip in enumerate(chips)]
        for cp in first:
            cp.start()
        passed = []
        for j, chip in enumerate(chips):
            for a in range(n):
                copy(a, 1 + j, (*chip, mc), me).wait_recv()
                cp = copy(a, 4 + j, (*chip, mc), sibling)
                cp.start()
                passed.append(cp)
        for a in range(n):
            copy(a, 0, sibling, me).wait_recv()
            for j, chip in enumerate(chips):
                copy(a, 4 + j, (*chip, 1 - mc), me).wait_recv()
        for cp in first + passed:
            cp.wait_send()
        for cp in mine:
            cp.wait()

    return pl.pallas_call(
        body, out_shape=[jax.ShapeDtypeStruct((N_DEV,) + x.shape, x.dtype) for x in xs],
        in_specs=[pl.BlockSpec(memory_space=pl.ANY)] * n, out_specs=[pl.BlockSpec(memory_space=pl.ANY)] * n,
        scratch_shapes=[pltpu.SemaphoreType.DMA((n * per,)), pltpu.SemaphoreType.DMA((n * per,)),
                        pltpu.SemaphoreType.DMA((n,))],
        name=name,
    )(*xs)


def _peer_of(k):
    mx, my, mc = lax.axis_index("x"), lax.axis_index("y"), lax.axis_index("c")
    px, py, pc = mx ^ (k >> 2), my ^ ((k >> 1) & 1), mc ^ (k & 1)
    return (px, py, pc), 4 * px + 2 * py + pc


def _exchange_start(xs, name, scatter):
    n = len(xs)
    npeer = N_DEV - 1

    def body(*refs):
        x_refs, land_refs = refs[:n], refs[n:2 * n]
        send_sems, recv_sems, token = refs[2 * n], refs[2 * n + 1], refs[-1]
        me = 4 * lax.axis_index("x") + 2 * lax.axis_index("y") + lax.axis_index("c")
        for k in range(1, N_DEV):
            dev, peer = _peer_of(k)
            for a in range(n):
                pltpu.make_async_remote_copy(
                    src_ref=x_refs[a].at[peer] if scatter else x_refs[a], dst_ref=land_refs[a].at[me],
                    send_sem=send_sems.at[a * npeer + k - 1], recv_sem=recv_sems.at[a * npeer + k - 1],
                    device_id=dev, device_id_type=pl.DeviceIdType.MESH).start()
        token[...] = jnp.zeros_like(token)

    hbm = pl.BlockSpec(memory_space=pltpu.HBM)
    sem = pl.BlockSpec(memory_space=pltpu.SEMAPHORE)
    lands = [pltpu.with_memory_space_constraint(lax.empty((N_DEV,) + x.shape[-2:], x.dtype), pltpu.HBM) for x in xs]
    srcs = [pltpu.with_memory_space_constraint(x, pltpu.HBM) for x in xs]
    outs = pl.pallas_call(
        body, name=name,
        out_shape=(pltpu.SemaphoreType.DMA((n * npeer,)), pltpu.SemaphoreType.DMA((n * npeer,)),
                   *[pltpu.HBM(x.shape, x.dtype) for x in srcs], *[pltpu.HBM(z.shape, z.dtype) for z in lands],
                   jax.ShapeDtypeStruct((8, LANE), F32)),
        in_specs=[hbm] * (2 * n), out_specs=(sem, sem, *[hbm] * (2 * n), pl.BlockSpec(memory_space=pltpu.VMEM)),
        input_output_aliases={i: 2 + i for i in range(2 * n)},
        compiler_params=pltpu.CompilerParams(has_side_effects=pltpu.SideEffectType.DATAFLOW_SIDE_EFFECTING),
    )(*srcs, *lands)
    return (outs[0], outs[1], list(outs[2:2 + n]), list(outs[2 + n:2 + 2 * n])), outs[-1][0:1, 0:1]


def _exchange_wait(started, after, name, scatter):
    send_sems, recv_sems, srcs, lands = started
    n = len(srcs)
    npeer = N_DEV - 1

    def body(*refs):
        x_refs, land_refs = refs[:n], refs[n:2 * n]
        send_sems, recv_sems = refs[2 * n], refs[2 * n + 1]
        mx, my, mc = lax.axis_index("x"), lax.axis_index("y"), lax.axis_index("c")
        me = 4 * mx + 2 * my + mc
        for k in range(1, N_DEV):
            _, peer = _peer_of(k)
            for a in range(n):
                src = x_refs[a].at[me] if scatter else x_refs[a]
                cp = pltpu.make_async_remote_copy(
                    src_ref=src, dst_ref=land_refs[a].at[peer], send_sem=send_sems.at[a * npeer + k - 1],
                    recv_sem=recv_sems.at[a * npeer + k - 1], device_id=(mx, my, mc), device_id_type=pl.DeviceIdType.MESH)
                cp.wait_send()
                cp.wait_recv()

    hbm = pl.BlockSpec(memory_space=pltpu.HBM)
    sem = pl.BlockSpec(memory_space=pltpu.SEMAPHORE)
    outs = pl.pallas_call(
        body, name=name,
        out_shape=(*[pltpu.HBM(x.shape, x.dtype) for x in srcs], *[pltpu.HBM(z.shape, z.dtype) for z in lands]),
        in_specs=[hbm] * (2 * n) + [sem, sem, pl.BlockSpec(memory_space=pl.ANY)], out_specs=tuple([hbm] * (2 * n)),
        input_output_aliases={i: i for i in range(2 * n)},
        compiler_params=pltpu.CompilerParams(has_side_effects=pltpu.SideEffectType.DATAFLOW_SIDE_EFFECTING),
    )(*srcs, *lands, send_sems, recv_sems, after)
    me = 4 * lax.axis_index("x") + 2 * lax.axis_index("y") + lax.axis_index("c")
    full = []
    for x, land in zip(outs[:n], outs[n:]):
        own = lax.dynamic_slice(x, (me, 0, 0), (1,) + x.shape[1:]) if scatter else x[None]
        full.append(lax.dynamic_update_slice(land, own, (me, 0, 0)))
    return full


def _sum_slots(x, name):
    _, r, c = x.shape
    tr = _pick(r, (512, 256, 128, 64, 32, 16))

    def body(x_ref, o_ref):
        acc = x_ref[0].astype(F32)
        for s in range(1, N_DEV):
            acc = acc + x_ref[s].astype(F32)
        o_ref[...] = acc

    return pl.pallas_call(
        body, grid=(r // tr,), in_specs=[pl.BlockSpec((N_DEV, tr, c), lambda i: (0, i, 0))],
        out_specs=pl.BlockSpec((tr, c), lambda i: (i, 0)), out_shape=jax.ShapeDtypeStruct((r, c), F32), name=name,
        compiler_params=pltpu.CompilerParams(dimension_semantics=("arbitrary",)),
    )(x)


def _mod_fwd(c_all, w_ada, b_ada_mine):
    def body(c_ref, w_ref, b_ref, o_ref):
        o_ref[...] = _doth(_silu(c_ref[...]), w_ref[...]) + b_ref[...]

    return pl.pallas_call(body, out_shape=jax.ShapeDtypeStruct((c_all.shape[0], w_ada.shape[1]), F32), name="mod_fwd")(c_all, w_ada, b_ada_mine)


def _mod_bwd(c_all_t, dmod_mine):
    def body(ct_ref, d_ref, o_ref):
        s = _silu(ct_ref[...])
        acc = s[:, 0:1] * d_ref[pl.ds(0, 1), :]
        for b in range(1, N_DEV):
            acc = acc + s[:, b:b + 1] * d_ref[pl.ds(b, 1), :]
        o_ref[...] = acc

    return pl.pallas_call(body, out_shape=jax.ShapeDtypeStruct((c_all_t.shape[0], dmod_mine.shape[1]), F32), name="mod_bwd")(c_all_t, dmod_mine)


def _conv_fwd(proj, conv_w8, tm):
    t = proj.shape[0]
    ch = DN_CONV_CH

    def body(x_ref, w_ref, o_ref, buf):
        @pl.when(pl.program_id(0) == 0)
        def _():
            buf[pl.ds(0, CONV_HALO), :] = jnp.zeros((CONV_HALO, ch), F32)

        buf[pl.ds(CONV_HALO, tm), :] = x_ref[...]
        for c0 in range(0, ch, CONV_COLS):
            cols = pl.ds(c0, CONV_COLS)
            w = [w_ref[pl.ds(j, 1), cols] for j in range(CONV_K)]
            for r0 in range(0, tm, CONV_ROWS):
                acc = buf[pl.ds(r0 + CONV_HALO - (CONV_K - 1), CONV_ROWS), cols] * w[0]
                for j in range(1, CONV_K):
                    acc = acc + buf[pl.ds(r0 + CONV_HALO - (CONV_K - 1) + j, CONV_ROWS), cols] * w[j]
                o_ref[pl.ds(r0, CONV_ROWS), cols] = _silu(acc)
        buf[pl.ds(0, CONV_HALO), :] = buf[pl.ds(tm, CONV_HALO), :]

    return pl.pallas_call(
        body, grid=(t // tm,), in_specs=[pl.BlockSpec((tm, ch), lambda i: (i, 0)), _full(conv_w8.shape)],
        out_specs=pl.BlockSpec((tm, ch), lambda i: (i, 0)), out_shape=jax.ShapeDtypeStruct((t, ch), F32),
        scratch_shapes=[pltpu.VMEM((tm + CONV_HALO, ch), F32)], name="conv_fwd",
        compiler_params=pltpu.CompilerParams(dimension_semantics=("arbitrary",)),
    )(proj, conv_w8)


def _conv_bwd(proj, conv_w8, dact, tm):
    t = proj.shape[0]
    ch = DN_CONV_CH
    nt = t // tm
    hb = tm // CONV_HALO

    def body(x_ref, xp_ref, w_ref, dy_ref, dx_ref, dw_ref, xbuf, dbuf):
        step = pl.program_id(0)

        @pl.when(step == 0)
        def _():
            dbuf[pl.ds(tm, CONV_HALO), :] = jnp.zeros((CONV_HALO, ch), F32)
            dw_ref[...] = jnp.zeros_like(dw_ref)

        first = step == nt - 1
        xbuf[pl.ds(0, CONV_HALO), :] = jnp.where(first, 0.0, xp_ref[...])
        xbuf[pl.ds(CONV_HALO, tm), :] = x_ref[...]
        for c0 in range(0, ch, CONV_COLS):
            cols = pl.ds(c0, CONV_COLS)
            w = [w_ref[pl.ds(j, 1), cols] for j in range(CONV_K)]
            dw = [jnp.zeros((1, CONV_COLS), F32) for _ in range(CONV_K)]
            for r0 in range(0, tm, CONV_ROWS):
                xs = [xbuf[pl.ds(r0 + CONV_HALO - (CONV_K - 1) + j, CONV_ROWS), cols] for j in range(CONV_K)]
                pre = xs[0] * w[0]
                for j in range(1, CONV_K):
                    pre = pre + xs[j] * w[j]
                sg = _sigmoid(pre)
                dpre = dy_ref[pl.ds(r0, CONV_ROWS), cols] * (sg * (1.0 + pre * (1.0 - sg)))
                dbuf[pl.ds(r0, CONV_ROWS), cols] = dpre
                dw = [dw[j] + jnp.sum(dpre * xs[j], axis=0, keepdims=True) for j in range(CONV_K)]
            for j in range(CONV_K):
                dw_ref[pl.ds(j, 1), cols] += dw[j]
            for r0 in range(0, tm, CONV_ROWS):
                dx = dbuf[pl.ds(r0 + CONV_K - 1, CONV_ROWS), cols] * w[0]
                for j in range(1, CONV_K):
                    dx = dx + dbuf[pl.ds(r0 + CONV_K - 1 - j, CONV_ROWS), cols] * w[j]
                dx_ref[pl.ds(r0, CONV_ROWS), cols] = dx.astype(dx_ref.dtype)
        dbuf[pl.ds(tm, CONV_HALO), :] = dbuf[pl.ds(0, CONV_HALO), :]

    rev = lambda i: (nt - 1 - i, 0)
    prev = lambda i: (jnp.maximum((nt - 1 - i) * hb - 1, 0), 0)
    return pl.pallas_call(
        body, grid=(nt,),
        in_specs=[pl.BlockSpec((tm, ch), rev), pl.BlockSpec((CONV_HALO, ch), prev), _full(conv_w8.shape),
                  pl.BlockSpec((tm, ch), rev)],
        out_specs=[pl.BlockSpec((tm, ch), rev), _full(conv_w8.shape)],
        out_shape=[jax.ShapeDtypeStruct((t, ch), BF), jax.ShapeDtypeStruct(conv_w8.shape, F32)],
        scratch_shapes=[pltpu.VMEM((tm + CONV_HALO, ch), F32), pltpu.VMEM((tm + CONV_HALO, ch), F32)], name="conv_bwd",
        compiler_params=pltpu.CompilerParams(dimension_semantics=("arbitrary",)),
    )(proj, proj, conv_w8, dact)


BNN = (((2,), (1,)), ((0,), (0,)))
BNT = (((2,), (2,)), ((0,), (0,)))
BTN = (((1,), (1,)), ((0,), (0,)))


def _bdot(a, b, dims, precision=None):
    return lax.dot_general(a, b, dims, precision=precision, preferred_element_type=F32)


@jax.custom_vjp
def _bmmb_nt(a, b):
    return _bdot(a.astype(BF), b.astype(BF), BNT)


def _bmmb_nt_fwd(a, b):
    return _bmmb_nt(a, b), (a, b)


def _bmmb_nt_bwd(res, g):
    a, b = res
    gb = g.astype(BF)
    return _bdot(gb, b.astype(BF), BNN), _bdot(gb, a.astype(BF), BTN)


_bmmb_nt.defvjp(_bmmb_nt_fwd, _bmmb_nt_bwd)


def _unit_lower_solve_fwd(a, r):
    c = a.shape[-1]
    ri = lax.broadcasted_iota(jnp.int32, a.shape, 1)
    ci = lax.broadcasted_iota(jnp.int32, a.shape, 2)
    xm = -a
    inv = (ri == ci).astype(F32) + xm
    for _ in range(int(math.log2(c)) - 1):
        xm = _bdot(xm, xm, BNN, HI)
        inv = inv + _bdot(inv, xm, BNN, HI)
    x = _bdot(inv, r, BNN, HI)
    return x, (inv, x)


def _unit_lower_solve_bwd(res, g):
    inv, x = res
    dr = _bdot(inv, g, BTN, HI)
    return -_bdot(dr, x, BNT, HI), dr


@jax.custom_vjp
def _unit_lower_solve_given(a, r, inv):
    return _bdot(inv, r, BNN, HI)


def _unit_lower_solve_given_fwd(a, r, inv):
    x = _bdot(inv, r, BNN, HI)
    return x, (inv, x)


def _unit_lower_solve_given_bwd(res, g):
    da, dr = _unit_lower_solve_bwd(res, g)
    return da, dr, jnp.zeros_like(res[0])


_unit_lower_solve_given.defvjp(_unit_lower_solve_given_fwd, _unit_lower_solve_given_bwd)


def _gdn_intra(qkv, ba, al8, dt8, inv4=None):
    tm = qkv.shape[0]
    nb = tm // CHUNK
    bsz = DN_HEADS * nb

    def heads(x0):
        return jnp.concatenate([qkv[:, x0 + h * LANE:x0 + (h + 1) * LANE].reshape(nb, CHUNK, LANE) for h in range(DN_HEADS)], axis=0)

    def spread(c0):
        return jnp.concatenate([jnp.broadcast_to(ba[:, c0 + h:c0 + h + 1], (tm, LANE)).reshape(nb, CHUNK, LANE)
                                for h in range(DN_HEADS)], axis=0)

    def per_head(v8):
        return jnp.concatenate([jnp.broadcast_to(v8[0:1, h:h + 1].reshape(1, 1, 1), (nb, 1, LANE)) for h in range(DN_HEADS)], axis=0)

    ri = lax.broadcasted_iota(jnp.int32, (bsz, CHUNK, CHUNK), 1)
    ci = lax.broadcasted_iota(jnp.int32, (bsz, CHUNK, CHUNK), 2)
    incl = ri >= ci
    strict = ri > ci

    q = _l2norm(heads(0)) * (DN_DK ** -0.5)
    k = _l2norm(heads(DN_QK))
    va = heads(2 * DN_QK)
    beta = _sigmoid(spread(0))
    g = -jnp.exp(per_head(al8)) * _softplus(spread(DN_HEADS) + per_head(dt8))
    gc = _bdot(incl.astype(F32), g, BNN, HI)
    g_last = jnp.sum(g, axis=1, keepdims=True)
    gcol = gc[:, :, :CHUNK]
    diff = gcol - jnp.swapaxes(gcol, 1, 2)
    decay = jnp.where(incl, jnp.exp(jnp.where(incl, diff, 0.0)), 0.0)
    kb = k * beta
    a_mat = jnp.where(strict, _bmmb_nt(kb, k) * decay, 0.0)
    egc = jnp.exp(gc)
    rhs = jnp.concatenate([kb * egc, va * beta], axis=2)
    if inv4 is None:
        wu, (inv, _) = _unit_lower_solve_fwd(a_mat, rhs)
    else:
        wu = _unit_lower_solve_given(a_mat, rhs, inv4.reshape(bsz, CHUNK, CHUNK))
    attn = jnp.where(incl, _bmmb_nt(q, k) * decay, 0.0)

    def unheads(x):
        return jnp.concatenate([x[h * nb:(h + 1) * nb].reshape(tm, LANE) for h in range(DN_HEADS)], axis=1)

    out = (unheads(wu[:, :, :DN_DK]), unheads(wu[:, :, DN_DK:]), unheads(q * egc), unheads(k * jnp.exp(g_last - gc)),
           attn.reshape(DN_HEADS, tm, CHUNK), unheads(jnp.broadcast_to(g_last, (bsz, CHUNK, LANE))))
    return out if inv4 is not None else out + (inv.reshape(DN_HEADS, nb, CHUNK, CHUNK),)


def _gdn_scan_step(w, u, qg, kd, att, gl, s):
    v_new = u - _mmb(w, s)
    o = _mmb(qg, s) + _mmb(att, v_new)
    return o, s * jnp.exp(gl) + _mmb_tn(kd, v_new)


def _gdn_intra_specs(t, tm, dts):
    nb = tm // CHUNK
    specs = [pl.BlockSpec((tm, DN_VW), lambda i: (i, 0))] * 4
    specs += [pl.BlockSpec((DN_HEADS, tm, CHUNK), lambda i: (0, i, 0)), pl.BlockSpec((tm, DN_VW), lambda i: (i, 0))]
    shapes = [jax.ShapeDtypeStruct((t, DN_VW), dts[i]) for i in range(4)]
    shapes += [jax.ShapeDtypeStruct((DN_HEADS, t, CHUNK), dts[4]), jax.ShapeDtypeStruct((t, DN_VW), dts[5])]
    return specs, shapes


def _gdn_intra_fwd(qkv, proj, al8, dt8, tm):
    t = qkv.shape[0]

    def body(qkv_ref, ba_ref, al_ref, dt_ref, *outs):
        for o, val in zip(outs, _gdn_intra(qkv_ref[...], ba_ref[...], al_ref[...], dt_ref[...])):
            o[...] = val.astype(o.dtype)

    specs, shapes = _gdn_intra_specs(t, tm, (BF, F32, BF, BF, BF, F32))
    specs.append(_gdn_inverse_spec(tm))
    shapes.append(jax.ShapeDtypeStruct((DN_HEADS, t // CHUNK, CHUNK, CHUNK), F32))
    res = pl.pallas_call(
        body, grid=(t // tm,),
        in_specs=[pl.BlockSpec((tm, DN_CONV_CH), lambda i: (i, 0)), pl.BlockSpec((tm, LANE), lambda i: (i, P_BA // LANE)),
                  _full(al8.shape), _full(dt8.shape)],
        out_specs=specs, out_shape=shapes, name="gdn_intra_fwd",
        compiler_params=pltpu.CompilerParams(dimension_semantics=("parallel",)),
    )(qkv, proj, al8, dt8)
    return res[:6], res[6]


def _gdn_inverse_spec(tm):
    return pl.BlockSpec((DN_HEADS, tm // CHUNK, CHUNK, CHUNK), lambda i: (0, i, 0, 0))


def _gdn_intra_bwd(qkv, proj, al8, dt8, inverses, cts, tm):
    t = qkv.shape[0]

    def body(qkv_ref, ba_ref, al_ref, dt_ref, inv_ref, *refs):
        ct_refs, (dqkv_ref, dba_ref, dal_ref, ddt_ref) = refs[:6], refs[6:]

        @pl.when(pl.program_id(0) == 0)
        def _():
            dal_ref[...] = jnp.zeros_like(dal_ref)
            ddt_ref[...] = jnp.zeros_like(ddt_ref)

        _, vjp = jax.vjp(functools.partial(_gdn_intra, inv4=inv_ref[...]), qkv_ref[...], ba_ref[...], al_ref[...], dt_ref[...])
        dqkv, dba, dal, ddt = vjp(tuple(r[...] for r in ct_refs))
        dqkv_ref[...] = dqkv
        dba_ref[...] = dba.astype(dba_ref.dtype)
        dal_ref[...] += dal
        ddt_ref[...] += ddt

    specs, _ = _gdn_intra_specs(t, tm, (F32,) * 6)
    return pl.pallas_call(
        body, grid=(t // tm,),
        in_specs=[pl.BlockSpec((tm, DN_CONV_CH), lambda i: (i, 0)), pl.BlockSpec((tm, LANE), lambda i: (i, P_BA // LANE)),
                  _full(al8.shape), _full(dt8.shape), _gdn_inverse_spec(tm)] + specs,
        out_specs=[pl.BlockSpec((tm, DN_CONV_CH), lambda i: (i, 0)), pl.BlockSpec((tm, LANE), lambda i: (i, 0)),
                   _full(al8.shape), _full(dt8.shape)],
        out_shape=[jax.ShapeDtypeStruct((t, DN_CONV_CH), F32), jax.ShapeDtypeStruct((t, LANE), BF),
                   jax.ShapeDtypeStruct(al8.shape, F32), jax.ShapeDtypeStruct(dt8.shape, F32)],
        name="gdn_intra_bwd", compiler_params=pltpu.CompilerParams(dimension_semantics=("arbitrary",)),
    )(qkv, proj, al8, dt8, inverses, *cts)


def _gdn_scan_fwd(intra, tm):
    t = intra[0].shape[0]
    nb = tm // CHUNK
    nc = t // CHUNK

    def body(w_ref, u_ref, qg_ref, kd_ref, att_ref, gl_ref, o_ref, ss_ref, s_scr):
        @pl.when(pl.program_id(0) == 0)
        def _():
            s_scr[...] = jnp.zeros_like(s_scr)

        for cc in range(nb):
            rows = pl.ds(cc * CHUNK, CHUNK)
            for h in range(DN_HEADS):
                cols = pl.ds(h * DN_DV, DN_DV)
                s_prev = s_scr[h]
                ss_ref[cc, h] = s_prev
                o, s_new = _gdn_scan_step(w_ref[rows, cols], u_ref[rows, cols], qg_ref[rows, cols], kd_ref[rows, cols],
                                          att_ref[h, rows, :], gl_ref[pl.ds(cc * CHUNK, 1), cols], s_prev)
                o_ref[rows, cols] = o
                s_scr[h] = s_new

    specs, _ = _gdn_intra_specs(t, tm, (F32,) * 6)
    return pl.pallas_call(
        body, grid=(t // tm,), in_specs=specs,
        out_specs=[pl.BlockSpec((tm, DN_VW), lambda i: (i, 0)),
                   pl.BlockSpec((nb, DN_HEADS, DN_DK, DN_DV), lambda i: (i, 0, 0, 0))],
        out_shape=[jax.ShapeDtypeStruct((t, DN_VW), F32), jax.ShapeDtypeStruct((nc, DN_HEADS, DN_DK, DN_DV), F32)],
        scratch_shapes=[pltpu.VMEM((DN_HEADS, DN_DK, DN_DV), F32)], name="gdn_scan_fwd",
        compiler_params=pltpu.CompilerParams(dimension_semantics=("arbitrary",)),
    )(*intra)


def _gdn_scan_bwd(intra, states, do, tm):
    t = intra[0].shape[0]
    nb = tm // CHUNK
    ng = t // tm

    def body(w_ref, u_ref, qg_ref, kd_ref, att_ref, gl_ref, ss_ref, do_ref,
             dw_ref, du_ref, dqg_ref, dkd_ref, datt_ref, dgl_ref, ds_scr):
        @pl.when(pl.program_id(0) == 0)
        def _():
            ds_scr[...] = jnp.zeros_like(ds_scr)

        for cc in reversed(range(nb)):
            rows = pl.ds(cc * CHUNK, CHUNK)
            for h in range(DN_HEADS):
                cols = pl.ds(h * DN_DV, DN_DV)
                f32 = lambda r: r[rows, cols].astype(F32)
                _, vjp = jax.vjp(_gdn_scan_step, f32(w_ref), u_ref[rows, cols], f32(qg_ref), f32(kd_ref),
                                 att_ref[h, rows, :].astype(F32), gl_ref[pl.ds(cc * CHUNK, 1), cols], ss_ref[cc, h])
                dw, du, dqg, dkd, datt, dgl, ds_prev = vjp((do_ref[rows, cols], ds_scr[h]))
                dw_ref[rows, cols] = dw
                du_ref[rows, cols] = du
                dqg_ref[rows, cols] = dqg
                dkd_ref[rows, cols] = dkd
                datt_ref[h, rows, :] = datt
                first_row = lax.broadcasted_iota(jnp.int32, (CHUNK, DN_DV), 0) == 0
                dgl_ref[rows, cols] = jnp.where(first_row, dgl, 0.0)
                ds_scr[h] = ds_prev

    rev = lambda i: (ng - 1 - i, 0)
    rev3 = lambda i: (0, ng - 1 - i, 0)
    row = pl.BlockSpec((tm, DN_VW), rev)
    six = [row] * 4 + [pl.BlockSpec((DN_HEADS, tm, CHUNK), rev3), row]
    _, shapes = _gdn_intra_specs(t, tm, (F32,) * 6)
    return pl.pallas_call(
        body, grid=(ng,),
        in_specs=six + [pl.BlockSpec((nb, DN_HEADS, DN_DK, DN_DV), lambda i: (ng - 1 - i, 0, 0, 0)), row],
        out_specs=six, out_shape=shapes,
        scratch_shapes=[pltpu.VMEM((DN_HEADS, DN_DK, DN_DV), F32)], name="gdn_scan_bwd",
        compiler_params=pltpu.CompilerParams(dimension_semantics=("arbitrary",)),
    )(*intra, states, do)


def _gdn_out(o, z, g):
    parts = []
    for h in range(DN_HEADS):
        sl = slice(h * DN_DV, (h + 1) * DN_DV)
        parts.append(_rmsnorm(o[:, sl], g) * _silu(z[:, sl]))
    return parts


_Q_SCALE = math.log2(math.e) / math.sqrt(QK_NOPE + QK_ROPE)


def _rope_tables(pos, inv_freq2):
    lane = lax.broadcasted_iota(jnp.int32, (1, LANE), 1)
    ang = pos * inv_freq2
    cos = jnp.where(lane < QK_ROPE, jnp.cos(ang), 0.0)
    sin = jnp.where(lane < QK_ROPE // 2, -jnp.sin(ang), jnp.where(lane < QK_ROPE, jnp.sin(ang), 0.0))
    return cos, sin


def _rope_swap():
    ri = lax.broadcasted_iota(jnp.int32, (LANE, LANE), 0)
    ci = lax.broadcasted_iota(jnp.int32, (LANE, LANE), 1)
    half = QK_ROPE // 2
    return (((ci < half) & (ri == ci + half)) | ((ci >= half) & (ci < QK_ROPE) & (ri == ci - half))).astype(F32)


def _mla_prep(cq, ckv, kr, gq, gkv, w_uq, w_ukv, cos, sin, swap):
    rope = lambda u: u * cos + _doth(u, swap) * sin
    q_lin = _mmb_nt(_rmsnorm(cq, gq), w_uq) * _Q_SCALE
    kv_lin = _mmb_nt(_rmsnorm(ckv, gkv), w_ukv)
    k_rope = rope(kr)
    qs, ks, vs = [], [], []
    for h in range(MLA_HEADS):
        qs += [q_lin[:, h * LANE:(h + 1) * LANE], rope(q_lin[:, (MLA_HEADS + h) * LANE:(MLA_HEADS + h + 1) * LANE])]
        ks += [kv_lin[:, 2 * h * LANE:(2 * h + 1) * LANE], k_rope]
        vs += [kv_lin[:, (2 * h + 1) * LANE:(2 * h + 2) * LANE]]
    return qs + ks + vs


def _mla_prep_fwd(proj, pos_col, inv_freq2, gq, gkv, w_uq, w_ukv, tm):
    t = proj.shape[0]
    nq = 2 * MLA_HEADS

    def body(cq_ref, ckv_ref, kr_ref, pos_ref, f_ref, gq_ref, gkv_ref, wq_ref, wkv_ref, q_ref, k_ref, v_ref):
        cos, sin = _rope_tables(pos_ref[...], f_ref[...])
        outs = _mla_prep(cq_ref[...], ckv_ref[...], kr_ref[...], gq_ref[...], gkv_ref[...], wq_ref[...], wkv_ref[...],
                         cos, sin, _rope_swap())
        for i in range(nq):
            q_ref[:, pl.ds(i * LANE, LANE)] = outs[i].astype(q_ref.dtype)
            k_ref[:, pl.ds(i * LANE, LANE)] = outs[nq + i].astype(k_ref.dtype)
        for h in range(MLA_HEADS):
            v_ref[:, pl.ds(h * LANE, LANE)] = outs[2 * nq + h].astype(v_ref.dtype)

    row = lambda w, j: pl.BlockSpec((tm, w), functools.partial(lambda i, j: (i, j), j=j))
    return pl.pallas_call(
        body, grid=(t // tm,),
        in_specs=[row(Q_LORA, P_CQ // Q_LORA), row(KV_LORA, P_CKV // KV_LORA), row(LANE, P_KR // LANE),
                  pl.BlockSpec((tm, 1), lambda i: (i, 0)), _full(inv_freq2.shape), _full(gq.shape), _full(gkv.shape),
                  _full(w_uq.shape), _full(w_ukv.shape)],
        out_specs=[row(nq * LANE, 0), row(nq * LANE, 0), row(MLA_VW, 0)],
        out_shape=[jax.ShapeDtypeStruct((t, nq * LANE), BF), jax.ShapeDtypeStruct((t, nq * LANE), BF),
                   jax.ShapeDtypeStruct((t, MLA_VW), BF)],
        name="mla_prep_fwd", compiler_params=pltpu.CompilerParams(dimension_semantics=("arbitrary",)),
    )(proj, proj, proj, pos_col, inv_freq2, gq, gkv, w_uq, w_ukv)


def _mla_prep_bwd(proj, pos_col, inv_freq2, gq, gkv, w_uq, w_ukv, dq, dk, dv, tm):
    t = proj.shape[0]
    nq = 2 * MLA_HEADS

    def body(cq_ref, ckv_ref, kr_ref, pos_ref, f_ref, gq_ref, gkv_ref, wq_ref, wkv_ref, dq_ref, dk_ref, dv_ref,
             dcq_ref, dckv_ref, dkr_ref, dgq_ref, dgkv_ref, dwq_ref, dwkv_ref):
        @pl.when(pl.program_id(0) == 0)
        def _():
            for o in (dgq_ref, dgkv_ref, dwq_ref, dwkv_ref):
                o[...] = jnp.zeros_like(o)

        cos, sin = _rope_tables(pos_ref[...], f_ref[...])
        f = functools.partial(_mla_prep, cos=cos, sin=sin, swap=_rope_swap())
        _, vjp = jax.vjp(f, cq_ref[...], ckv_ref[...], kr_ref[...], gq_ref[...], gkv_ref[...], wq_ref[...], wkv_ref[...])
        cts = [dq_ref[:, pl.ds(i * LANE, LANE)] for i in range(nq)]
        cts += [dk_ref[:, pl.ds(i * LANE, LANE)] for i in range(nq)]
        cts += [dv_ref[:, pl.ds(h * LANE, LANE)] for h in range(MLA_HEADS)]
        dcq, dckv, dkr, dgq, dgkv, dwq, dwkv = vjp(cts)
        dcq_ref[...] = dcq.astype(dcq_ref.dtype)
        dckv_ref[...] = dckv.astype(dckv_ref.dtype)
        dkr_ref[...] = dkr.astype(dkr_ref.dtype)
        dgq_ref[...] += dgq
        dgkv_ref[...] += dgkv
        dwq_ref[...] += dwq
        dwkv_ref[...] += dwkv

    row = lambda w, j: pl.BlockSpec((tm, w), functools.partial(lambda i, j: (i, j), j=j))
    return pl.pallas_call(
        body, grid=(t // tm,),
        in_specs=[row(Q_LORA, P_CQ // Q_LORA), row(KV_LORA, P_CKV // KV_LORA), row(LANE, P_KR // LANE),
                  pl.BlockSpec((tm, 1), lambda i: (i, 0)), _full(inv_freq2.shape), _full(gq.shape), _full(gkv.shape),
                  _full(w_uq.shape), _full(w_ukv.shape), row(nq * LANE, 0), row(nq * LANE, 0), row(MLA_VW, 0)],
        out_specs=[row(Q_LORA, 0), row(KV_LORA, 0), row(LANE, 0), _full(gq.shape), _full(gkv.shape),
                   _full(w_uq.shape), _full(w_ukv.shape)],
        out_shape=[jax.ShapeDtypeStruct((t, Q_LORA), BF), jax.ShapeDtypeStruct((t, KV_LORA), BF),
                   jax.ShapeDtypeStruct((t, LANE), BF), jax.ShapeDtypeStruct(gq.shape, F32),
                   jax.ShapeDtypeStruct(gkv.shape, F32), jax.ShapeDtypeStruct(w_uq.shape, F32),
                   jax.ShapeDtypeStruct(w_ukv.shape, F32)],
        name="mla_prep_bwd", compiler_params=pltpu.CompilerParams(dimension_semantics=("arbitrary",)),
    )(proj, proj, proj, pos_col, inv_freq2, gq, gkv, w_uq, w_ukv, dq, dk, dv)


_NEG = -1e30
_LN2 = math.log(2.0)
ATT_CHAINS = 2


def _causal(tq, tk, q0, k0):
    row = q0 + lax.broadcasted_iota(jnp.int32, (tq, tk), 0)
    col = k0 + lax.broadcasted_iota(jnp.int32, (tq, tk), 1)
    return col <= row


def _attn_fwd(q, k, v, tq, tk):
    t = q.shape[0]

    assert tk % tq == 0 or tq % tk == 0
    n_diag = max(1, tq // tk)

    th = tq // ATT_CHAINS

    def body(q_ref, k_ref, v_ref, o_ref, lse_ref):
        i = pl.program_id(1)
        n_full = (i * tq) // tk

        def step(k0, carry, masked):
            kt = k_ref[pl.ds(k0, tk), :]
            vt = v_ref[pl.ds(k0, tk), :]
            out = []
            for c, (m, l, acc) in enumerate(carry):
                s = _dot(q_ref[pl.ds(c * th, th), :], kt, NT)
                if masked:
                    s = jnp.where(_causal(th, tk, i * tq + c * th, k0), s, _NEG)
                m_new = jnp.maximum(m, jnp.max(s, axis=-1, keepdims=True))
                p = jnp.exp2(s - m_new)
                alpha = jnp.exp2(m - m_new)
                out.append((m_new, alpha * l + jnp.sum(p, axis=-1, keepdims=True), alpha * acc + _dot(p.astype(BF), vt)))
            return tuple(out)

        init = tuple((jnp.full((th, 1), _NEG, F32), jnp.zeros((th, 1), F32), jnp.zeros((th, V_HEAD), F32)) for _ in range(ATT_CHAINS))
        carry = lax.fori_loop(0, n_full, lambda j, c: step(pl.multiple_of(j * tk, tk), c, False), init)
        for dd in range(n_diag):
            carry = step(pl.multiple_of((n_full + dd) * tk, tk), carry, True)
        for c, (m, l, acc) in enumerate(carry):
            o_ref[pl.ds(c * th, th), :] = acc / l
            lse_ref[pl.ds(c * th, th), :] = jnp.broadcast_to(m + jnp.log2(l), (th, LANE))

    return pl.pallas_call(
        body, grid=(MLA_HEADS, t // tq),
        in_specs=[pl.BlockSpec((tq, 2 * LANE), lambda h, i: (i, h)), pl.BlockSpec((t, 2 * LANE), lambda h, i: (0, h)),
                  pl.BlockSpec((t, V_HEAD), lambda h, i: (0, h))],
        out_specs=[pl.BlockSpec((tq, V_HEAD), lambda h, i: (i, h)), pl.BlockSpec((tq, LANE), lambda h, i: (i, h))],
        out_shape=[jax.ShapeDtypeStruct((t, MLA_VW), F32), jax.ShapeDtypeStruct((t, MLA_HEADS * LANE), F32)],
        name="attn_fwd", compiler_params=pltpu.CompilerParams(dimension_semantics=("parallel", "arbitrary")),
    )(q, k, v)


def _attn_bwd(q, k, v, do, lse, delta, tq, tk):
    t = q.shape[0]
    nkt = t // tk
    assert tk % tq == 0

    def body(q_ref, k_ref, v_ref, do_ref, lse_ref, dl_ref, dq_ref, dk_ref, dv_ref):
        j = pl.program_id(1)

        @pl.when(j == 0)
        def _():
            dq_ref[...] = jnp.zeros_like(dq_ref)

        kt = k_ref[...]
        vt = v_ref[...]

        def step(q0, carry, masked):
            dk, dv = carry
            rows = pl.ds(q0, tq)
            qt = q_ref[rows, :]
            dot_ = do_ref[rows, :]
            p = jnp.exp2(_dot(qt, kt, NT) - lse_ref[rows, pl.ds(0, 1)])
            if masked:
                p = jnp.where(_causal(tq, tk, q0, j * tk), p, 0.0)
            dv = dv + _dot(p.astype(BF), dot_, TN)
            ds = (p * (_dot(dot_, vt, NT) - dl_ref[rows, pl.ds(0, 1)])).astype(BF)
            dk = dk + _dot(ds, qt, TN)
            dq_ref[rows, :] += _dot(ds, kt)
            return dk, dv

        per = tk // tq
        carry = (jnp.zeros((tk, 2 * LANE), F32), jnp.zeros((tk, V_HEAD), F32))
        for dd in range(per):
            carry = step(pl.multiple_of(j * tk + dd * tq, tq), carry, True)

        def group(g, c):
            for dd in range(per):
                c = step(pl.multiple_of(g * tk + dd * tq, tq), c, False)
            return c

        dk, dv = lax.fori_loop(j + 1, nkt, group, carry)
        dk_ref[...] = dk * _LN2
        dv_ref[...] = dv

        @pl.when(j == nkt - 1)
        def _():
            dq_ref[...] = dq_ref[...] * _LN2

    return pl.pallas_call(
        body, grid=(MLA_HEADS, nkt),
        in_specs=[pl.BlockSpec((t, 2 * LANE), lambda h, j: (0, h)), pl.BlockSpec((tk, 2 * LANE), lambda h, j: (j, h)),
                  pl.BlockSpec((tk, V_HEAD), lambda h, j: (j, h)), pl.BlockSpec((t, V_HEAD), lambda h, j: (0, h)),
                  pl.BlockSpec((t, LANE), lambda h, j: (0, h)), pl.BlockSpec((t, LANE), lambda h, j: (0, h))],
        out_specs=[pl.BlockSpec((t, 2 * LANE), lambda h, j: (0, h)), pl.BlockSpec((tk, 2 * LANE), lambda h, j: (j, h)),
                   pl.BlockSpec((tk, V_HEAD), lambda h, j: (j, h))],
        out_shape=[jax.ShapeDtypeStruct((t, MLA_HEADS * 2 * LANE), F32), jax.ShapeDtypeStruct((t, MLA_HEADS * 2 * LANE), F32),
                   jax.ShapeDtypeStruct((t, MLA_VW), F32)],
        name="attn_bwd", compiler_params=pltpu.CompilerParams(dimension_semantics=("parallel", "arbitrary")),
    )(q, k, v, do, lse, delta)


def _adam_update(w, g, m, v):
    mm = ADAM_B1 * m + (1.0 - ADAM_B1) * g
    vv = ADAM_B2 * v + (1.0 - ADAM_B2) * jnp.square(g)
    m_hat = mm / (1.0 - ADAM_B1 ** ADAM_STEP)
    v_hat = vv / (1.0 - ADAM_B2 ** ADAM_STEP)
    return -ADAM_LR * (m_hat / (jnp.sqrt(v_hat) + ADAM_EPS) + ADAM_WD * w), mm, vv


def _adamw(w, g, m, v, name):
    r, c = w.shape
    tr = max([r // s for s in range(1, r // 8 + 1) if r % s == 0 and (r // s) % 8 == 0 and r // s <= 256] or [r])
    slots = g.ndim == 3

    def body(w_ref, g_ref, m_ref, v_ref, g_out, d_ref, nm_ref, nv_ref):
        if slots:
            gg = g_ref[0].astype(F32)
            for s in range(1, N_DEV):
                gg = gg + g_ref[s].astype(F32)
        else:
            gg = g_ref[...]
        g_out[...] = gg
        d_ref[...], nm_ref[...], nv_ref[...] = _adam_update(w_ref[...], gg, m_ref[...], v_ref[...])

    spec = pl.BlockSpec((tr, c), lambda i: (i, 0))
    g_spec = pl.BlockSpec((N_DEV, tr, c), lambda i: (0, i, 0)) if slots else spec
    return pl.pallas_call(
        body, grid=(r // tr,), in_specs=[spec, g_spec, spec, spec], out_specs=[spec] * 4,
        out_shape=[jax.ShapeDtypeStruct((r, c), F32)] * 4, name=name,
        compiler_params=pltpu.CompilerParams(dimension_semantics=("arbitrary",)),
    )(w, g, m, v)


def _adamw_many(ws, gs, ms, vs, name):
    n = len(ws)

    def body(*refs):
        for i in range(n):
            w_ref, g_ref, m_ref, v_ref = (refs[j * n + i] for j in range(4))
            d_ref, nm_ref, nv_ref = (refs[(4 + j) * n + i] for j in range(3))
            d_ref[...], nm_ref[...], nv_ref[...] = _adam_update(w_ref[...], g_ref[...], m_ref[...], v_ref[...])

    shapes = [jax.ShapeDtypeStruct(w.shape, F32) for w in ws]
    outs = pl.pallas_call(body, out_shape=shapes * 3, name=name)(*ws, *gs, *ms, *vs)
    return outs[:n], outs[n:2 * n], outs[2 * n:]


def _cast_bf16(xs, name, after=None):
    n = len(xs)
    extra = [] if after is None else [after]

    def body(*refs):
        outs = refs[n + len(extra):]
        for i in range(n):
            outs[i][...] = refs[i][...].astype(BF)

    vmem = pl.BlockSpec(memory_space=pltpu.VMEM)
    return pl.pallas_call(
        body, out_shape=[jax.ShapeDtypeStruct(x.shape, BF) for x in xs], name=name,
        in_specs=[vmem] * n + [pl.BlockSpec(memory_space=pl.ANY)] * len(extra), out_specs=[vmem] * n)(*xs, *extra)


def _pad_rows(a, n):
    return jnp.pad(a, ((0, n - a.shape[0]), (0, 0)))


def _w_in_to_padded(wt):
    s_ba = P_CQ
    s_cq = s_ba + 2 * DN_HEADS
    s_kr = s_cq + Q_LORA + KV_LORA
    return jnp.concatenate([wt[:s_ba], wt[s_cq:s_kr], _pad_rows(wt[s_ba:s_cq], LANE), _pad_rows(wt[s_kr:], LANE)], axis=0)


def _w_in_from_padded(wt):
    return jnp.concatenate([wt[:P_CQ], wt[P_BA:P_BA + 2 * DN_HEADS], wt[P_CQ:P_BA], wt[P_KR:P_KR + QK_ROPE]], axis=0)


def _w_uq_to_padded(wt):
    w3 = wt.reshape(MLA_HEADS, QK_NOPE + QK_ROPE, Q_LORA)
    nope = w3[:, :QK_NOPE].reshape(MLA_HEADS * QK_NOPE, Q_LORA)
    rope = jnp.pad(w3[:, QK_NOPE:], ((0, 0), (0, LANE - QK_ROPE), (0, 0))).reshape(MLA_HEADS * LANE, Q_LORA)
    return jnp.concatenate([nope, rope], axis=0)


def _w_uq_from_padded(wt):
    nope = wt[:MLA_HEADS * QK_NOPE].reshape(MLA_HEADS, QK_NOPE, Q_LORA)
    rope = wt[MLA_HEADS * QK_NOPE:].reshape(MLA_HEADS, LANE, Q_LORA)[:, :QK_ROPE]
    return jnp.concatenate([nope, rope], axis=1).reshape(MLA_HEADS * (QK_NOPE + QK_ROPE), Q_LORA)


def _pack(pieces, width, row_mult):
    flat = jnp.concatenate([p.reshape(-1) for p in pieces])
    n = flat.shape[0]
    rows = -(-n // (width * row_mult)) * row_mult
    return jnp.pad(flat, (0, rows * width - n)).reshape(rows, width)


def _unpack(flat, shapes):
    out, o = [], 0
    for s in shapes:
        n = math.prod(s)
        out.append(flat[o:o + n].reshape(s))
        o += n
    return out


def kernel(x, c, positions, w_ada, b_ada, w_in, conv_w, a_log, dt_bias, dn_norm_g, q_norm_g, w_uq, kv_norm_g, w_ukv, w_o, ln1_g, ln1_b, w_gate, w_up, w_down, ln2_g, ln2_b, loss_target, m_w_ada, m_b_ada, m_w_in, m_conv_w, m_a_log, m_dt_bias, m_dn_norm_g, m_q_norm_g, m_w_uq, m_kv_norm_g, m_w_ukv, m_w_o, m_ln1_g, m_ln1_b, m_w_gate, m_w_up, m_w_down, m_ln2_g, m_ln2_b, v_w_ada, v_b_ada, v_w_in, v_conv_w, v_a_log, v_dt_bias, v_dn_norm_g, v_q_norm_g, v_w_uq, v_kv_norm_g, v_w_ukv, v_w_o, v_ln1_g, v_ln1_b, v_w_gate, v_w_up, v_w_down, v_ln2_g, v_ln2_b):
    me = 4 * lax.axis_index("x") + 2 * lax.axis_index("y") + lax.axis_index("c")
    t, d = x.shape[1], x.shape[2]
    ada_n = w_ada.shape[2]

    tr = lambda w: w[0].T
    rows = lambda a: a.reshape(-1, a.shape[2])
    (in_shard,) = _cast_bf16([tr(w_in)], "cast_w_in")
    cw = conv_w.shape[3]
    a_in, c_all, conv_all = _gather_by_chip([in_shard, c, conv_w[0, :, 0, :]], "gather_w_in_and_small")
    c_all = c_all.reshape(N_DEV, d)
    conv_full = conv_all.transpose(1, 0, 2).reshape(CONV_K, N_DEV * cw)
    conv_w8 = jnp.pad(conv_full, ((0, 8 - CONV_K), (0, 0)))

    b_ada_mine = lax.dynamic_slice(b_ada, (0, me * ada_n), (1, ada_n))
    mod_cols = _mod_fwd(c_all, w_ada[0], b_ada_mine)
    (mod_all,) = _exchange([mod_cols.reshape(N_DEV, 1, ada_n)], "scatter_mod", scatter=True)
    mod = mod_all.reshape(1, N_DEV * ada_n)

    later = _cast_bf16([tr(w_uq), tr(w_ukv), w_o[0], tr(w_gate), tr(w_up), w_down[0]], "cast_weights", after=mod)
    mixer_gather, token_a = _exchange_start(later[:3], "gather_mixer_weights_start", scatter=False)
    ffn_gather, token_b = _exchange_start(later[3:], "gather_ffn_weights_start", scatter=False)
    mod = mod + (token_a + token_b)
    w_in_t = _w_in_to_padded(rows(a_in))

    def mixer_weights(after):
        a_uq, a_ukv, a_o = _exchange_wait(mixer_gather, after, "gather_mixer_weights_wait", scatter=False)
        return _w_uq_to_padded(rows(a_uq)), rows(a_ukv), rows(a_o)

    def ffn_weights(after):
        a_gate, a_up, a_down = _exchange_wait(ffn_gather, after, "gather_ffn_weights_wait", scatter=False)
        return rows(a_gate), rows(a_up), rows(a_down)

    def by_dest(g):
        return g.reshape(N_DEV, -1, g.shape[1])

    scatters = {}

    def grads_ready(tag, *g):
        if tag == "ffn":
            pieces = [by_dest(a) for a in g]
        elif tag == "mixer":
            g_w_o, g_w_uq_t, g_w_ukv_t = g
            pieces = [by_dest(g_w_o), by_dest(_w_uq_from_padded(g_w_uq_t).astype(BF)), by_dest(g_w_ukv_t.astype(BF))]
        else:
            pieces = [by_dest(_w_in_from_padded(g[0]))]
        scatters[tag], token = _exchange_start(pieces, "scatter_%s_grads_start" % tag, scatter=True)
        return token

    loc = _local_step(x[0], loss_target[0], positions[0], mod, w_in_t, mixer_weights, ffn_weights, grads_ready,
                      conv_w8, a_log, dt_bias, dn_norm_g, q_norm_g, kv_norm_g, ln1_g, ln1_b, ln2_g, ln2_b)
    grad_x, loss_acc, dmod, d_conv8, d_al8, d_dt8, d_dn_g, d_q_g, d_kv_g, d_ln1_g, d_ln1_b, d_ln2_g, d_ln2_b = loc

    small_shapes = [(6 * d,), (CONV_K, N_DEV * cw), (DN_HEADS,), (DN_HEADS,), (DN_DV,), (Q_LORA,), (KV_LORA,), (d,), (d,), (d,), (d,), (1,)]
    gsmall = _pack([dmod, d_conv8[:CONV_K], d_al8[0, :DN_HEADS], d_dt8[0, :DN_HEADS], d_dn_g, d_q_g, d_kv_g,
                    d_ln1_g, d_ln1_b, d_ln2_g, d_ln2_b, loss_acc[0, :1]], LANE, 8)
    (gsmall_all,) = _exchange([gsmall], "gather_small_grads", scatter=False)
    dmod_all = gsmall_all.reshape(N_DEV, -1)[:, :6 * d]
    tot = _unpack(_sum_slots(gsmall_all, "sum_small_grads").reshape(-1), small_shapes)
    g_b_ada, g_conv_full, g_a_log, g_dt_bias, g_dn_g, g_q_g, g_kv_g, g_ln1_g, g_ln1_b, g_ln2_g, g_ln2_b, loss1 = tot
    loss = loss1.reshape(())
    g_conv_w = lax.dynamic_slice(g_conv_full, (0, me * cw), (CONV_K, cw))
    g_w_ada = _mod_bwd(c_all.T, lax.dynamic_slice(dmod_all, (0, me * ada_n), (N_DEV, ada_n)))

    grads = {"w_ada": g_w_ada[None], "b_ada": g_b_ada[None], "conv_w": g_conv_w[None, :, None, :],
             "a_log": g_a_log[None], "dt_bias": g_dt_bias[None], "dn_norm_g": g_dn_g[None], "q_norm_g": g_q_g[None],
             "kv_norm_g": g_kv_g[None], "ln1_g": g_ln1_g[None], "ln1_b": g_ln1_b[None], "ln2_g": g_ln2_g[None], "ln2_b": g_ln2_b[None]}
    weights = dict(w_ada=w_ada, b_ada=b_ada, w_in=w_in, conv_w=conv_w, a_log=a_log, dt_bias=dt_bias, dn_norm_g=dn_norm_g,
                   q_norm_g=q_norm_g, w_uq=w_uq, kv_norm_g=kv_norm_g, w_ukv=w_ukv, w_o=w_o, ln1_g=ln1_g, ln1_b=ln1_b,
                   w_gate=w_gate, w_up=w_up, w_down=w_down, ln2_g=ln2_g, ln2_b=ln2_b)
    ms = dict(w_ada=m_w_ada, b_ada=m_b_ada, w_in=m_w_in, conv_w=m_conv_w, a_log=m_a_log, dt_bias=m_dt_bias,
              dn_norm_g=m_dn_norm_g, q_norm_g=m_q_norm_g, w_uq=m_w_uq, kv_norm_g=m_kv_norm_g, w_ukv=m_w_ukv, w_o=m_w_o,
              ln1_g=m_ln1_g, ln1_b=m_ln1_b, w_gate=m_w_gate, w_up=m_w_up, w_down=m_w_down, ln2_g=m_ln2_g, ln2_b=m_ln2_b)
    vs = dict(w_ada=v_w_ada, b_ada=v_b_ada, w_in=v_w_in, conv_w=v_conv_w, a_log=v_a_log, dt_bias=v_dt_bias,
              dn_norm_g=v_dn_norm_g, q_norm_g=v_q_norm_g, w_uq=v_w_uq, kv_norm_g=v_kv_norm_g, w_ukv=v_w_ukv, w_o=v_w_o,
              ln1_g=v_ln1_g, ln1_b=v_ln1_b, w_gate=v_w_gate, w_up=v_w_up, w_down=v_w_down, ln2_g=v_ln2_g, ln2_b=v_ln2_b)
    names = list(weights)
    big = ("w_ada", "w_gate", "w_up", "w_down", "w_o", "w_uq", "w_ukv", "w_in")
    waits = {"w_gate": ("ffn", ("w_gate", "w_up", "w_down")), "w_o": ("mixer", ("w_o", "w_uq", "w_ukv")), "w_in": ("in", ("w_in",))}
    delta_w, new_m, new_v, slots = {}, {}, {}, {}
    last = g_w_ada
    for n in big:
        if n == "w_in":
            rest = [r for r in names if r not in big]
            flat2 = lambda a: a.reshape(-1, a.shape[-1])
            outs = _adamw_many(*[[flat2(src[r]) for r in rest] for src in (weights, grads, ms, vs)], "adamw_small")
            for dst, o in zip((delta_w, new_m, new_v), outs):
                for r, a in zip(rest, o):
                    dst[r] = a.reshape(weights[r].shape)
            last = outs[0][0]
        transposed = n in ("w_in", "w_uq", "w_ukv", "w_gate", "w_up")
        two = (lambda a: a[0].T) if transposed else (lambda a: a[0])
        back = (lambda a: a.T[None]) if transposed else (lambda a: a[None])
        if n in waits:
            tag, members = waits[n]
            slots.update(zip(members, _exchange_wait(scatters[tag], last, "scatter_%s_grads_wait" % tag, scatter=True)))
        g_in = slots[n] if n in slots else two(grads[n])
        gr, dlt, nm, nv = _adamw(two(weights[n]), g_in, two(ms[n]), two(vs[n]), "adamw_" + n)
        grads[n], delta_w[n], new_m[n], new_v[n] = back(gr), back(dlt), back(nm), back(nv)
        last = nv

    return (loss, grad_x[None], *[grads[n] for n in names], *[delta_w[n] for n in names],
            *[new_m[n] for n in names], *[new_v[n] for n in names])


def _local_step(xs, tgt, pos, mod, w_in_t, mixer_weights, ffn_weights, grads_ready, conv_w8,
                a_log, dt_bias, dn_norm_g, q_norm_g, kv_norm_g, ln1_g, ln1_b, ln2_g, ln2_b):
    t, d = xs.shape
    sh_m, sc_m, gt_m, sh_f, sc_f, gt_f = [mod[:, i * d:(i + 1) * d] for i in range(6)]
    pos_col = pos.astype(F32).reshape(t, 1)
    inv_freq = 1.0 / (ROPE_THETA ** (jnp.arange(0, QK_ROPE, 2, dtype=F32) / QK_ROPE))
    inv_freq2 = jnp.pad(jnp.concatenate([inv_freq, inv_freq]), (0, LANE - QK_ROPE)).reshape(1, LANE)
    al8 = jnp.pad(a_log, ((0, 7), (0, LANE - DN_HEADS)))
    dt8 = jnp.pad(dt_bias, ((0, 7), (0, LANE - DN_HEADS)))

    tm = min(512, t)
    tq = min(256, t)
    tk = min(512, t)

    (h1,) = _rowwise("modulate_in", lambda xx, sc, sh: xx * (1.0 + sc) + sh, [xs], [sc_m, sh_m], [(d, BF)], [], tm)
    proj = _matmul(h1, w_in_t, "nt", "in_proj")
    qkv = _conv_fwd(proj, conv_w8, min(256, t))
    gdn_tm = min(512, t)
    intra, inverses = _gdn_intra_fwd(qkv, proj, al8, dt8, gdn_tm)
    o_dn, states = _gdn_scan_fwd(intra, gdn_tm)
    w_uq_t, w_ukv_t, w_o_f = mixer_weights(states)
    qc, kc, vc = _mla_prep_fwd(proj, pos_col, inv_freq2, q_norm_g, kv_norm_g, w_uq_t, w_ukv_t, tm)
    o_mla, lse = _attn_fwd(qc, kc, vc, min(1024, t), min(1024, t))

    def mix_in(o, z, om, g):
        return jnp.concatenate(_gdn_out(o, z, g) + [om], axis=1)

    (mixin,) = _rowwise("mixer_out", mix_in, [o_dn, (proj, DN_VW, P_Z // DN_VW), o_mla], [dn_norm_g], [(2 * DN_VW, BF)], [], tm)
    mix = _matmul(mixin, w_o_f, "nn", "out_proj")

    def block1(xx, mx, gt, g1, b1, sc, sh):
        x1 = _layernorm(DEEPNORM_ALPHA * xx + gt * mx, g1, b1)
        return x1, x1 * (1.0 + sc) + sh

    x1, h2 = _rowwise("norm1_modulate", block1, [xs, mix], [gt_m, ln1_g, ln1_b, sc_f, sh_f], [(d, F32), (d, BF)], [], tm)
    w_gate_f, w_up_f, w_down_f = ffn_weights(h2)
    act, gate, up = _ffn_in(h2, w_gate_f, w_up_f)
    ff = _matmul(act, w_down_f, "nn", "ffn_out")

    def tail_loss(x1_, ff_, gt, g2, b2, tg):
        y = _layernorm(DEEPNORM_ALPHA * x1_ + gt * ff_, g2, b2)
        return 0.5 * jnp.sum(jnp.mean(jnp.square(y - tg), axis=-1))

    def tail(x1_, ff_, tg, gt, g2, b2):
        loss, (dx1, dff, dgt, dg2, db2) = jax.value_and_grad(tail_loss, argnums=(0, 1, 2, 3, 4))(x1_, ff_, gt, g2, b2, tg)
        return dx1, dff, jnp.full((1, LANE), loss, F32), dgt, dg2, db2

    dx1_a, dff, loss_acc, d_gt_f, d_ln2_g, d_ln2_b = _rowwise(
        "norm2_loss", tail, [x1, ff, tgt], [gt_f, ln2_g, ln2_b], [(d, F32), (d, BF)], [(1, LANE), (1, d), (1, d), (1, d)], tm)

    g_w_down = _matmul(act, dff, "tn", "d_w_down", BF)
    dgate, dup = _ffn_act_bwd(dff, w_down_f, gate, up)
    g_w_gate = _matmul(dgate, h2, "tn", "d_w_gate", BF)
    g_w_up = _matmul(dup, h2, "tn", "d_w_up", BF)
    token = grads_ready("ffn", g_w_gate, g_w_up, g_w_down)
    dh2 = _matmul2_nn(dgate, w_gate_f, dup, w_up_f, "d_ffn_in")

    def block1_bwd(xx, mx, dx1_, dh2_, gt, g1, b1, sc, sh):
        _, vjp = jax.vjp(block1, xx, mx, gt, g1, b1, sc, sh)
        dxx, dmx, dgt, dg1, db1, dsc, dsh = vjp((dx1_, dh2_))
        return dxx, dmx, dgt, dg1, db1, dsc, dsh

    dx_a, dmix, d_gt_m, d_ln1_g, d_ln1_b, d_sc_f, d_sh_f = _rowwise(
        "norm1_modulate_bwd", block1_bwd, [xs, mix, dx1_a, dh2], [gt_m + token, ln1_g, ln1_b, sc_f, sh_f],
        [(d, F32), (d, BF)], [(1, d)] * 5, min(256, t))

    dmixin = _matmul(dmix, w_o_f, "nt", "d_mixer_out")
    g_w_o = _matmul(mixin, dmix, "tn", "d_w_o", BF)

    def mixer_bwd(o, z, om, dmi, g):
        _, vjp = jax.vjp(lambda o_, z_, g_: jnp.concatenate(_gdn_out(o_, z_, g_), axis=1), o, z, g)
        do_, dz_, dg_ = vjp(dmi[:, :DN_VW])
        dom = dmi[:, DN_VW:]
        delta = [jnp.broadcast_to(jnp.sum(dom[:, h * V_HEAD:(h + 1) * V_HEAD] * om[:, h * V_HEAD:(h + 1) * V_HEAD], axis=-1, keepdims=True), (o.shape[0], LANE))
                 for h in range(MLA_HEADS)]
        return do_, dz_, dom, jnp.concatenate(delta, axis=1), dg_

    do_dn, dz, do_mla, delta, d_dn_g = _rowwise(
        "mixer_out_bwd", mixer_bwd, [o_dn, (proj, DN_VW, P_Z // DN_VW), o_mla, dmixin], [dn_norm_g],
        [(DN_VW, F32), (DN_VW, BF), (MLA_VW, BF), (MLA_HEADS * LANE, F32)], [(1, DN_DV)], tm)

    dqc, dkc, dvc = _attn_bwd(qc, kc, vc, do_mla, lse, delta, min(512, t), min(1024, t))
    dcq, dckv, dkr, d_q_g, d_kv_g, g_w_uq_t, g_w_ukv_t = _mla_prep_bwd(
        proj, pos_col, inv_freq2, q_norm_g, kv_norm_g, w_uq_t, w_ukv_t, dqc, dkc, dvc, min(256, t))

    token = grads_ready("mixer", g_w_o, g_w_uq_t, g_w_ukv_t)

    d_intra = _gdn_scan_bwd(intra, states, do_dn, gdn_tm)
    dqkv_act, dba, d_al8, d_dt8 = _gdn_intra_bwd(qkv, proj, al8 + token, dt8, inverses, d_intra, min(256, t))
    dqkv_pre, d_conv8 = _conv_bwd(proj, conv_w8, dqkv_act, min(256, t))

    dproj = jnp.concatenate([dqkv_pre, dz, dcq, dckv, dba, dkr], axis=1)
    dh1 = _matmul(dproj, w_in_t, "nn", "d_in_proj")
    g_w_in_t = _matmul(dproj, h1, "tn", "d_w_in", BF)
    token = grads_ready("in", g_w_in_t)

    def modulate_bwd(xx, dh, dxa, sc):
        return dh * (1.0 + sc) + dxa, jnp.sum(dh * xx, axis=0, keepdims=True), jnp.sum(dh, axis=0, keepdims=True)

    grad_x, d_sc_m, d_sh_m = _rowwise("modulate_in_bwd", modulate_bwd, [xs, dh1, dx_a], [sc_m + token], [(d, F32)], [(1, d), (1, d)], tm)
    dmod = jnp.concatenate([d_sh_m, d_sc_m, d_gt_m, d_sh_f, d_sc_f, d_gt_f], axis=1)
    return grad_x, loss_acc, dmod, d_conv8, d_al8, d_dt8, d_dn_g, d_q_g, d_kv_g, d_ln1_g, d_ln1_b, d_ln2_g, d_ln2_b
```

```python
import functools
import math

import jax
import jax.numpy as jnp
from jax import lax
from jax.experimental import pallas as pl
from jax.experimental.pallas import tpu as pltpu

F32 = jnp.float32
BF = jnp.bfloat16
HI = lax.Precision.HIGHEST

N_DEV = 8
DN_HEADS = 4
DN_DK = 128
DN_DV = 128
CONV_K = 4
CHUNK = 64
MLA_HEADS = 4
QK_NOPE = 128
QK_ROPE = 64
V_HEAD = 128
Q_LORA = 512
KV_LORA = 256
ROPE_THETA = 10000.0
DEPTH = 1
DEEPNORM_ALPHA = (2.0 * DEPTH) ** 0.25
LANE = 128
CONV_HALO = 8
CONV_ROWS, CONV_COLS = 64, 256

DN_QK = DN_HEADS * DN_DK
DN_VW = DN_HEADS * DN_DV
DN_CONV_CH = 2 * DN_QK + DN_VW
MLA_VW = MLA_HEADS * V_HEAD
P_Z = DN_CONV_CH
P_CQ = P_Z + DN_VW
P_CKV = P_CQ + Q_LORA
P_BA = P_CKV + KV_LORA
P_KR = P_BA + LANE
N_INP = P_KR + LANE

ADAM_LR = 0.001
ADAM_B1 = 0.9
ADAM_B2 = 0.999
ADAM_EPS = 1e-08
ADAM_WD = 0.01
ADAM_STEP = 10

NN = (((1,), (0,)), ((), ()))
NT = (((1,), (1,)), ((), ()))
TN = (((0,), (0,)), ((), ()))


def _pick(n, prefs):
    for p in prefs:
        if n % p == 0:
            return p
    return n


def _full(shape):
    return pl.BlockSpec(shape, lambda *_: (0,) * len(shape))


def _dot(a, b, dims=NN):
    return lax.dot_general(a, b, dims, preferred_element_type=F32)


def _doth(a, b, dims=NN):
    return lax.dot_general(a, b, dims, precision=HI, preferred_element_type=F32)


@jax.custom_vjp
def _mmb(a, b):
    return _dot(a.astype(BF), b.astype(BF), NN)


def _mmb_fwd(a, b):
    return _mmb(a, b), (a, b)


def _mmb_bwd(res, g):
    a, b = res
    gb = g.astype(BF)
    return (_dot(gb, b.astype(BF), NT).astype(a.dtype), _dot(a.astype(BF), gb, TN).astype(b.dtype))


_mmb.defvjp(_mmb_fwd, _mmb_bwd)


@jax.custom_vjp
def _mmb_nt(a, b):
    return _dot(a.astype(BF), b.astype(BF), NT)


def _mmb_nt_fwd(a, b):
    return _mmb_nt(a, b), (a, b)


def _mmb_nt_bwd(res, g):
    a, b = res
    gb = g.astype(BF)
    return (_dot(gb, b.astype(BF), NN).astype(a.dtype), _dot(gb, a.astype(BF), TN).astype(b.dtype))


_mmb_nt.defvjp(_mmb_nt_fwd, _mmb_nt_bwd)


@jax.custom_vjp
def _mmb_tn(a, b):
    return _dot(a.astype(BF), b.astype(BF), TN)


def _mmb_tn_fwd(a, b):
    return _mmb_tn(a, b), (a, b)


def _mmb_tn_bwd(res, g):
    a, b = res
    gb = g.astype(BF)
    return (_dot(b.astype(BF), gb, NT).astype(a.dtype), _dot(a.astype(BF), gb, NN).astype(b.dtype))


_mmb_tn.defvjp(_mmb_tn_fwd, _mmb_tn_bwd)


def _sigmoid(x):
    return 0.5 * (jnp.tanh(0.5 * x) + 1.0)


def _silu(x):
    return x * _sigmoid(x)


def _softplus(x):
    return jnp.maximum(x, 0.0) + jnp.log(1.0 + jnp.exp(-jnp.abs(x)))


def _layernorm(x, g, b, eps=1e-5):
    mu = jnp.mean(x, axis=-1, keepdims=True)
    xc = x - mu
    var = jnp.mean(xc * xc, axis=-1, keepdims=True)
    return xc * lax.rsqrt(var + eps) * g + b


def _rmsnorm(x, g, eps=1e-6):
    return x * lax.rsqrt(jnp.mean(x * x, axis=-1, keepdims=True) + eps) * g


def _l2norm(x, eps=1e-6):
    return x * lax.rsqrt(jnp.sum(x * x, axis=-1, keepdims=True) + eps)


def _rowwise(name, fn, rows, vecs, out_rows, out_accs, tm):
    rows = [r if isinstance(r, tuple) else (r, r.shape[1], 0) for r in rows]
    t = rows[0][0].shape[0]
    tm = min(tm, t)
    assert t % tm == 0
    nr, nv, no = len(rows), len(vecs), len(out_rows)

    def body(*refs):
        ins = [r[...] for r in refs[:nr + nv]]
        outs = fn(*ins)
        outs = outs if isinstance(outs, (tuple, list)) else (outs,)
        o_rows = refs[nr + nv:nr + nv + no]
        o_accs = refs[nr + nv + no:]
        for o, val in zip(o_rows, outs[:no]):
            o[...] = val.astype(o.dtype)
        if o_accs:
            @pl.when(pl.program_id(0) == 0)
            def _():
                for o in o_accs:
                    o[...] = jnp.zeros_like(o)
            for o, val in zip(o_accs, outs[no:]):
                o[...] += val

    in_specs = [pl.BlockSpec((tm, w), functools.partial(lambda i, j: (i, j), j=j)) for (_, w, j) in rows]
    in_specs += [_full(v.shape) for v in vecs]
    out_specs = [pl.BlockSpec((tm, w), lambda i: (i, 0)) for (w, _) in out_rows]
    out_specs += [_full(s) for s in out_accs]
    out_shape = [jax.ShapeDtypeStruct((t, w), d) for (w, d) in out_rows]
    out_shape += [jax.ShapeDtypeStruct(s, F32) for s in out_accs]
    res = pl.pallas_call(
        body, grid=(t // tm,), in_specs=in_specs, out_specs=out_specs, out_shape=out_shape, name=name,
        compiler_params=pltpu.CompilerParams(dimension_semantics=("arbitrary",)),
    )(*[r[0] for r in rows], *vecs)
    return res


def _matmul(a, b, mode, name, out_dtype=F32):
    if mode == "nn":
        (m, k), n = a.shape, b.shape[1]
    elif mode == "nt":
        (m, k), n = a.shape, b.shape[0]
    else:
        (k, m), n = a.shape, b.shape[1]
    tm, tn, tk = _matmul_tiles(m, n, k, a.dtype.itemsize, b.dtype.itemsize, jnp.dtype(out_dtype).itemsize)
    nk = k // tk
    dims = {"nn": NN, "nt": NT, "tn": TN}[mode]

    def body(a_ref, b_ref, o_ref, *acc):
        part = _dot(a_ref[...].astype(BF), b_ref[...].astype(BF), dims)
        if nk == 1:
            o_ref[...] = part.astype(o_ref.dtype)
            return
        (acc_ref,) = acc
        kk = pl.program_id(2)

        @pl.when(kk == 0)
        def _():
            acc_ref[...] = part

        @pl.when(kk > 0)
        def _():
            acc_ref[...] += part

        @pl.when(kk == nk - 1)
        def _():
            o_ref[...] = acc_ref[...].astype(o_ref.dtype)

    a_spec = pl.BlockSpec((tk, tm), lambda i, j, kk: (kk, i)) if mode == "tn" else pl.BlockSpec((tm, tk), lambda i, j, kk: (i, kk))
    b_spec = pl.BlockSpec((tn, tk), lambda i, j, kk: (j, kk)) if mode == "nt" else pl.BlockSpec((tk, tn), lambda i, j, kk: (kk, j))
    return pl.pallas_call(
        body, grid=(m // tm, n // tn, nk), in_specs=[a_spec, b_spec],
        out_specs=pl.BlockSpec((tm, tn), lambda i, j, kk: (i, j)),
        out_shape=jax.ShapeDtypeStruct((m, n), out_dtype),
        scratch_shapes=[pltpu.VMEM((tm, tn), F32)] if nk > 1 else [], name=name,
        compiler_params=pltpu.CompilerParams(dimension_semantics=("parallel", "parallel", "arbitrary")),
    )(a, b)


def _lane_tile(n, cap):
    return max([n // s for s in range(1, n // LANE + 1) if n % s == 0 and (n // s) % LANE == 0 and n // s <= cap] or [n])


def _ffn_in(h, w_gate, w_up):
    m, k = h.shape
    f = w_gate.shape[0]
    tm, tn = _pick(m, (1024, 512, 256, 128)), _lane_tile(f, 256)

    def body(h_ref, wg_ref, wu_ref, act_ref, g_ref, u_ref):
        hh = h_ref[...]
        g = _dot(hh, wg_ref[...], NT)
        u = _dot(hh, wu_ref[...], NT)
        act_ref[...] = (_silu(g) * u).astype(act_ref.dtype)
        g_ref[...] = g.astype(g_ref.dtype)
        u_ref[...] = u.astype(u_ref.dtype)

    w_spec = pl.BlockSpec((tn, k), lambda i, j: (j, 0))
    o_spec = pl.BlockSpec((tm, tn), lambda i, j: (i, j))
    return pl.pallas_call(
        body, grid=(m // tm, f // tn), in_specs=[pl.BlockSpec((tm, k), lambda i, j: (i, 0)), w_spec, w_spec],
        out_specs=[o_spec] * 3, out_shape=[jax.ShapeDtypeStruct((m, f), BF)] * 3, name="ffn_in",
        compiler_params=pltpu.CompilerParams(dimension_semantics=("parallel", "parallel")),
    )(h, w_gate, w_up)


def _ffn_act_bwd(dff, w_down, gate, up):
    m, k = dff.shape
    f = w_down.shape[0]
    tm, tn = _pick(m, (512, 256, 128)), _lane_tile(f, 1408)

    def body(d_ref, w_ref, g_ref, u_ref, dg_ref, du_ref):
        da = _dot(d_ref[...], w_ref[...], NT)
        g = g_ref[...].astype(F32)
        sg = _sigmoid(g)
        dg_ref[...] = (da * u_ref[...].astype(F32) * (sg * (1.0 + g * (1.0 - sg)))).astype(dg_ref.dtype)
        du_ref[...] = (da * (g * sg)).astype(du_ref.dtype)

    o_spec = pl.BlockSpec((tm, tn), lambda i, j: (i, j))
    return pl.pallas_call(
        body, grid=(m // tm, f // tn),
        in_specs=[pl.BlockSpec((tm, k), lambda i, j: (i, 0)), pl.BlockSpec((tn, k), lambda i, j: (j, 0)), o_spec, o_spec],
        out_specs=[o_spec] * 2, out_shape=[jax.ShapeDtypeStruct((m, f), BF)] * 2, name="d_ffn_act",
        compiler_params=pltpu.CompilerParams(dimension_semantics=("parallel", "parallel")),
    )(dff, w_down, gate, up)


def _matmul2_nn(a1, b1, a2, b2, name):
    m, k = a1.shape
    n = b1.shape[1]
    tm, tn = _pick(m, (1024, 512, 256, 128)), _pick(n, (512, 256, 128))

    def body(a1_ref, b1_ref, a2_ref, b2_ref, o_ref):
        o_ref[...] = _dot(a1_ref[...], b1_ref[...]) + _dot(a2_ref[...], b2_ref[...])

    a_spec = pl.BlockSpec((tm, k), lambda i, j: (i, 0))
    b_spec = pl.BlockSpec((k, tn), lambda i, j: (0, j))
    return pl.pallas_call(
        body, grid=(m // tm, n // tn), in_specs=[a_spec, b_spec, a_spec, b_spec],
        out_specs=pl.BlockSpec((tm, tn), lambda i, j: (i, j)), out_shape=jax.ShapeDtypeStruct((m, n), F32), name=name,
        compiler_params=pltpu.CompilerParams(dimension_semantics=("parallel", "parallel")),
    )(a1, b1, a2, b2)


MATMUL_VMEM_BUDGET = 28 * 1024 * 1024


def _matmul_tiles(m, n, k, a_bytes, b_bytes, o_bytes):
    def divisors(x, cap):
        return sorted({x // s for s in range(1, 65) if x % s == 0 and (x // s) % LANE == 0 and x // s <= cap}, reverse=True) or [x]

    for tk in divisors(k, k):
        best = None
        for tm in divisors(m, 1024):
            for tn in divisors(n, 2048):
                need = 2 * (tm * tk * a_bytes + tk * tn * b_bytes + tm * tn * o_bytes) + (tm * tn * 4 if tk < k else 0)
                if need <= MATMUL_VMEM_BUDGET and tm * tn >= 512 * 512 and (best is None or tm * tn > best[0] * best[1]):
                    best = (tm, tn)
        if best:
            return best[0], best[1], tk
    return _pick(m, (512, 256, 128)), _pick(n, (512, 256, 128)), _pick(k, (512, 256, 128))


def _exchange(xs, name, scatter):
    n = len(xs)
    npeer = N_DEV - 1

    def body(*refs):
        x_refs, o_refs = refs[:n], refs[n:2 * n]
        send_sems, recv_sems, local_sems = refs[2 * n:]
        mx, my, mc = lax.axis_index("x"), lax.axis_index("y"), lax.axis_index("c")
        me = 4 * mx + 2 * my + mc
        src_me = [x.at[me] if scatter else x for x in x_refs]
        mine = [pltpu.make_async_copy(src_me[a], o_refs[a].at[me], local_sems.at[a]) for a in range(n)]
        for cp in mine:
            cp.start()
        copies = []
        for k in range(1, N_DEV):
            px, py, pc = mx ^ (k >> 2), my ^ ((k >> 1) & 1), mc ^ (k & 1)
            peer = 4 * px + 2 * py + pc
            for a in range(n):
                cp = pltpu.make_async_remote_copy(
                    src_ref=x_refs[a].at[peer] if scatter else x_refs[a], dst_ref=o_refs[a].at[me],
                    send_sem=send_sems.at[a * npeer + k - 1], recv_sem=recv_sems.at[a * npeer + k - 1],
                    device_id=(px, py, pc), device_id_type=pl.DeviceIdType.MESH)
                cp.start()
                copies.append((cp, a, k, peer))
        for cp, a, k, peer in copies:
            pltpu.make_async_remote_copy(
                src_ref=src_me[a], dst_ref=o_refs[a].at[peer], send_sem=send_sems.at[a * npeer + k - 1],
                recv_sem=recv_sems.at[a * npeer + k - 1], device_id=(mx, my, mc),
                device_id_type=pl.DeviceIdType.MESH).wait_recv()
        for cp, _, _, _ in copies:
            cp.wait_send()
        for cp in mine:
            cp.wait()

    return pl.pallas_call(
        body, out_shape=[jax.ShapeDtypeStruct((N_DEV,) + x.shape[-2:], x.dtype) for x in xs],
        in_specs=[pl.BlockSpec(memory_space=pl.ANY)] * n, out_specs=[pl.BlockSpec(memory_space=pl.ANY)] * n,
        scratch_shapes=[pltpu.SemaphoreType.DMA((n * npeer,)), pltpu.SemaphoreType.DMA((n * npeer,)),
                        pltpu.SemaphoreType.DMA((n,))],
        name=name,
    )(*xs)


def _gather_by_chip(xs, name):
    n = len(xs)
    per = N_DEV - 1

    def body(*refs):
        x_refs, o_refs = refs[:n], refs[n:2 * n]
        send_sems, recv_sems, local_sems = refs[2 * n:]
        mx, my, mc = lax.axis_index("x"), lax.axis_index("y"), lax.axis_index("c")
        me, sibling = (mx, my, mc), (mx, my, 1 - mc)
        chips = [(1 - mx, my), (mx, 1 - my), (1 - mx, 1 - my)]
        slot = lambda d: 4 * d[0] + 2 * d[1] + d[2]

        def copy(a, k, block, to, src=None):
            dst = o_refs[a].at[slot(block)]
            return pltpu.make_async_remote_copy(
                src_ref=dst if src is None else src, dst_ref=dst, send_sem=send_sems.at[a * per + k],
                recv_sem=recv_sems.at[a * per + k], device_id=to, device_id_type=pl.DeviceIdType.MESH)

        mine = [pltpu.make_async_copy(x_refs[a], o_refs[a].at[slot(me)], local_sems.at[a]) for a in range(n)]
        for cp in mine:
            cp.start()
        first = []
        for a in range(n):
            first.append(copy(a, 0, me, sibling, src=x_refs[a]))
            first += [copy(a, 1 + j, me, (*chip, mc), src=x_refs[a]) for j, chip in enumerate(chips)]
        for cp in first:
            cp.start()
        passed = []
        for j, chip in enumerate(chips):
            for a in range(n):
                copy(a, 1 + j, (*chip, mc), me).wait_recv()
                cp = copy(a, 4 + j, (*chip, mc), sibling)
                cp.start()
                passed.append(cp)
        for a in range(n):
            copy(a, 0, sibling, me).wait_recv()
            for j, chip in enumerate(chips):
                copy(a, 4 + j, (*chip, 1 - mc), me).wait_recv()
        for cp in first + passed:
            cp.wait_send()
        for cp in mine:
            cp.wait()

    return pl.pallas_call(
        body, out_shape=[jax.ShapeDtypeStruct((N_DEV,) + x.shape, x.dtype) for x in xs],
        in_specs=[pl.BlockSpec(memory_space=pl.ANY)] * n, out_specs=[pl.BlockSpec(memory_space=pl.ANY)] * n,
        scratch_shapes=[pltpu.SemaphoreType.DMA((n * per,)), pltpu.SemaphoreType.DMA((n * per,)),
                        pltpu.SemaphoreType.DMA((n,))],
        name=name,
    )(*xs)


def _peer_of(k):
    mx, my, mc = lax.axis_index("x"), lax.axis_index("y"), lax.axis_index("c")
    px, py, pc = mx ^ (k >> 2), my ^ ((k >> 1) & 1), mc ^ (k & 1)
    return (px, py, pc), 4 * px + 2 * py + pc


def _exchange_start(xs, name, scatter):
    n = len(xs)
    npeer = N_DEV - 1

    def body(*refs):
        x_refs, land_refs = refs[:n], refs[n:2 * n]
        send_sems, recv_sems, token = refs[2 * n], refs[2 * n + 1], refs[-1]
        me = 4 * lax.axis_index("x") + 2 * lax.axis_index("y") + lax.axis_index("c")
        for k in range(1, N_DEV):
            dev, peer = _peer_of(k)
            for a in range(n):
                pltpu.make_async_remote_copy(
                    src_ref=x_refs[a].at[peer] if scatter else x_refs[a], dst_ref=land_refs[a].at[me],
                    send_sem=send_sems.at[a * npeer + k - 1], recv_sem=recv_sems.at[a * npeer + k - 1],
                    device_id=dev, device_id_type=pl.DeviceIdType.MESH).start()
        token[...] = jnp.zeros_like(token)

    hbm = pl.BlockSpec(memory_space=pltpu.HBM)
    sem = pl.BlockSpec(memory_space=pltpu.SEMAPHORE)
    lands = [pltpu.with_memory_space_constraint(lax.empty((N_DEV,) + x.shape[-2:], x.dtype), pltpu.HBM) for x in xs]
    srcs = [pltpu.with_memory_space_constraint(x, pltpu.HBM) for x in xs]
    outs = pl.pallas_call(
        body, name=name,
        out_shape=(pltpu.SemaphoreType.DMA((n * npeer,)), pltpu.SemaphoreType.DMA((n * npeer,)),
                   *[pltpu.HBM(x.shape, x.dtype) for x in srcs], *[pltpu.HBM(z.shape, z.dtype) for z in lands],
                   jax.ShapeDtypeStruct((8, LANE), F32)),
        in_specs=[hbm] * (2 * n), out_specs=(sem, sem, *[hbm] * (2 * n), pl.BlockSpec(memory_space=pltpu.VMEM)),
        input_output_aliases={i: 2 + i for i in range(2 * n)},
        compiler_params=pltpu.CompilerParams(has_side_effects=pltpu.SideEffectType.DATAFLOW_SIDE_EFFECTING),
    )(*srcs, *lands)
    return (outs[0], outs[1], list(outs[2:2 + n]), list(outs[2 + n:2 + 2 * n])), outs[-1][0:1, 0:1]


def _exchange_wait(started, after, name, scatter):
    send_sems, recv_sems, srcs, lands = started
    n = len(srcs)
    npeer = N_DEV - 1

    def body(*refs):
        x_refs, land_refs = refs[:n], refs[n:2 * n]
        send_sems, recv_sems = refs[2 * n], refs[2 * n + 1]
        mx, my, mc = lax.axis_index("x"), lax.axis_index("y"), lax.axis_index("c")
        me = 4 * mx + 2 * my + mc
        for k in range(1, N_DEV):
            _, peer = _peer_of(k)
            for a in range(n):
                src = x_refs[a].at[me] if scatter else x_refs[a]
                cp = pltpu.make_async_remote_copy(
                    src_ref=src, dst_ref=land_refs[a].at[peer], send_sem=send_sems.at[a * npeer + k - 1],
                    recv_sem=recv_sems.at[a * npeer + k - 1], device_id=(mx, my, mc), device_id_type=pl.DeviceIdType.MESH)
                cp.wait_send()
                cp.wait_recv()

    hbm = pl.BlockSpec(memory_space=pltpu.HBM)
    sem = pl.BlockSpec(memory_space=pltpu.SEMAPHORE)
    outs = pl.pallas_call(
        body, name=name,
        out_shape=(*[pltpu.HBM(x.shape, x.dtype) for x in srcs], *[pltpu.HBM(z.shape, z.dtype) for z in lands]),
        in_specs=[hbm] * (2 * n) + [sem, sem, pl.BlockSpec(memory_space=pl.ANY)], out_specs=tuple([hbm] * (2 * n)),
        input_output_aliases={i: i for i in range(2 * n)},
        compiler_params=pltpu.CompilerParams(has_side_effects=pltpu.SideEffectType.DATAFLOW_SIDE_EFFECTING),
    )(*srcs, *lands, send_sems, recv_sems, after)
    me = 4 * lax.axis_index("x") + 2 * lax.axis_index("y") + lax.axis_index("c")
    full = []
    for x, land in zip(outs[:n], outs[n:]):
        own = lax.dynamic_slice(x, (me, 0, 0), (1,) + x.shape[1:]) if scatter else x[None]
        full.append(lax.dynamic_update_slice(land, own, (me, 0, 0)))
    return full


def _sum_slots(x, name):
    _, r, c = x.shape
    tr = _pick(r, (512, 256, 128, 64, 32, 16))

    def body(x_ref, o_ref):
        acc = x_ref[0].astype(F32)
        for s in range(1, N_DEV):
            acc = acc + x_ref[s].astype(F32)
        o_ref[...] = acc

    return pl.pallas_call(
        body, grid=(r // tr,), in_specs=[pl.BlockSpec((N_DEV, tr, c), lambda i: (0, i, 0))],
        out_specs=pl.BlockSpec((tr, c), lambda i: (i, 0)), out_shape=jax.ShapeDtypeStruct((r, c), F32), name=name,
        compiler_params=pltpu.CompilerParams(dimension_semantics=("arbitrary",)),
    )(x)


def _mod_fwd(c_all, w_ada, b_ada_mine):
    def body(c_ref, w_ref, b_ref, o_ref):
        o_ref[...] = _doth(_silu(c_ref[...]), w_ref[...]) + b_ref[...]

    return pl.pallas_call(body, out_shape=jax.ShapeDtypeStruct((c_all.shape[0], w_ada.shape[1]), F32), name="mod_fwd")(c_all, w_ada, b_ada_mine)


def _mod_bwd(c_all_t, dmod_mine):
    def body(ct_ref, d_ref, o_ref):
        s = _silu(ct_ref[...])
        acc = s[:, 0:1] * d_ref[pl.ds(0, 1), :]
        for b in range(1, N_DEV):
            acc = acc + s[:, b:b + 1] * d_ref[pl.ds(b, 1), :]
        o_ref[...] = acc

    return pl.pallas_call(body, out_shape=jax.ShapeDtypeStruct((c_all_t.shape[0], dmod_mine.shape[1]), F32), name="mod_bwd")(c_all_t, dmod_mine)


def _conv_fwd(proj, conv_w8, tm):
    t = proj.shape[0]
    ch = DN_CONV_CH

    def body(x_ref, w_ref, o_ref, buf):
        @pl.when(pl.program_id(0) == 0)
        def _():
            buf[pl.ds(0, CONV_HALO), :] = jnp.zeros((CONV_HALO, ch), F32)

        buf[pl.ds(CONV_HALO, tm), :] = x_ref[...]
        for c0 in range(0, ch, CONV_COLS):
            cols = pl.ds(c0, CONV_COLS)
            w = [w_ref[pl.ds(j, 1), cols] for j in range(CONV_K)]
            for r0 in range(0, tm, CONV_ROWS):
                acc = buf[pl.ds(r0 + CONV_HALO - (CONV_K - 1), CONV_ROWS), cols] * w[0]
                for j in range(1, CONV_K):
                    acc = acc + buf[pl.ds(r0 + CONV_HALO - (CONV_K - 1) + j, CONV_ROWS), cols] * w[j]
                o_ref[pl.ds(r0, CONV_ROWS), cols] = _silu(acc)
        buf[pl.ds(0, CONV_HALO), :] = buf[pl.ds(tm, CONV_HALO), :]

    return pl.pallas_call(
        body, grid=(t // tm,), in_specs=[pl.BlockSpec((tm, ch), lambda i: (i, 0)), _full(conv_w8.shape)],
        out_specs=pl.BlockSpec((tm, ch), lambda i: (i, 0)), out_shape=jax.ShapeDtypeStruct((t, ch), F32),
        scratch_shapes=[pltpu.VMEM((tm + CONV_HALO, ch), F32)], name="conv_fwd",
        compiler_params=pltpu.CompilerParams(dimension_semantics=("arbitrary",)),
    )(proj, conv_w8)


def _conv_bwd(proj, conv_w8, dact, tm):
    t = proj.shape[0]
    ch = DN_CONV_CH
    nt = t // tm
    hb = tm // CONV_HALO

    def body(x_ref, xp_ref, w_ref, dy_ref, dx_ref, dw_ref, xbuf, dbuf):
        step = pl.program_id(0)

        @pl.when(step == 0)
        def _():
            dbuf[pl.ds(tm, CONV_HALO), :] = jnp.zeros((CONV_HALO, ch), F32)
            dw_ref[...] = jnp.zeros_like(dw_ref)

        first = step == nt - 1
        xbuf[pl.ds(0, CONV_HALO), :] = jnp.where(first, 0.0, xp_ref[...])
        xbuf[pl.ds(CONV_HALO, tm), :] = x_ref[...]
        for c0 in range(0, ch, CONV_COLS):
            cols = pl.ds(c0, CONV_COLS)
            w = [w_ref[pl.ds(j, 1), cols] for j in range(CONV_K)]
            dw = [jnp.zeros((1, CONV_COLS), F32) for _ in range(CONV_K)]
            for r0 in range(0, tm, CONV_ROWS):
                xs = [xbuf[pl.ds(r0 + CONV_HALO - (CONV_K - 1) + j, CONV_ROWS), cols] for j in range(CONV_K)]
                pre = xs[0] * w[0]
                for j in range(1, CONV_K):
                    pre = pre + xs[j] * w[j]
                sg = _sigmoid(pre)
                dpre = dy_ref[pl.ds(r0, CONV_ROWS), cols] * (sg * (1.0 + pre * (1.0 - sg)))
                dbuf[pl.ds(r0, CONV_ROWS), cols] = dpre
                dw = [dw[j] + jnp.sum(dpre * xs[j], axis=0, keepdims=True) for j in range(CONV_K)]
            for j in range(CONV_K):
                dw_ref[pl.ds(j, 1), cols] += dw[j]
            for r0 in range(0, tm, CONV_ROWS):
                dx = dbuf[pl.ds(r0 + CONV_K - 1, CONV_ROWS), cols] * w[0]
                for j in range(1, CONV_K):
                    dx = dx + dbuf[pl.ds(r0 + CONV_K - 1 - j, CONV_ROWS), cols] * w[j]
                dx_ref[pl.ds(r0, CONV_ROWS), cols] = dx.astype(dx_ref.dtype)
        dbuf[pl.ds(tm, CONV_HALO), :] = dbuf[pl.ds(0, CONV_HALO), :]

    rev = lambda i: (nt - 1 - i, 0)
    prev = lambda i: (jnp.maximum((nt - 1 - i) * hb - 1, 0), 0)
    return pl.pallas_call(
        body, grid=(nt,),
        in_specs=[pl.BlockSpec((tm, ch), rev), pl.BlockSpec((CONV_HALO, ch), prev), _full(conv_w8.shape),
                  pl.BlockSpec((tm, ch), rev)],
        out_specs=[pl.BlockSpec((tm, ch), rev), _full(conv_w8.shape)],
        out_shape=[jax.ShapeDtypeStruct((t, ch), BF), jax.ShapeDtypeStruct(conv_w8.shape, F32)],
        scratch_shapes=[pltpu.VMEM((tm + CONV_HALO, ch), F32), pltpu.VMEM((tm + CONV_HALO, ch), F32)], name="conv_bwd",
        compiler_params=pltpu.CompilerParams(dimension_semantics=("arbitrary",)),
    )(proj, proj, conv_w8, dact)


BNN = (((2,), (1,)), ((0,), (0,)))
BNT = (((2,), (2,)), ((0,), (0,)))
BTN = (((1,), (1,)), ((0,), (0,)))


def _bdot(a, b, dims, precision=None):
    return lax.dot_general(a, b, dims, precision=precision, preferred_element_type=F32)


@jax.custom_vjp
def _bmmb_nt(a, b):
    return _bdot(a.astype(BF), b.astype(BF), BNT)


def _bmmb_nt_fwd(a, b):
    return _bmmb_nt(a, b), (a, b)


def _bmmb_nt_bwd(res, g):
    a, b = res
    gb = g.astype(BF)
    return _bdot(gb, b.astype(BF), BNN), _bdot(gb, a.astype(BF), BTN)


_bmmb_nt.defvjp(_bmmb_nt_fwd, _bmmb_nt_bwd)


@jax.custom_vjp
def _bmmb(a, b):
    return _bdot(a.astype(BF), b.astype(BF), BNN)


def _bmmb_fwd(a, b):
    return _bmmb(a, b), (a, b)


def _bmmb_bwd(res, g):
    a, b = res
    gb = g.astype(BF)
    return _bdot(gb, b.astype(BF), BNT), _bdot(a.astype(BF), gb, BTN)


_bmmb.defvjp(_bmmb_fwd, _bmmb_bwd)


@jax.custom_vjp
def _bmmb_tn(a, b):
    return _bdot(a.astype(BF), b.astype(BF), BTN)


def _bmmb_tn_fwd(a, b):
    return _bmmb_tn(a, b), (a, b)


def _bmmb_tn_bwd(res, g):
    a, b = res
    gb = g.astype(BF)
    return _bdot(b.astype(BF), gb, BNT), _bdot(a.astype(BF), gb, BNN)


_bmmb_tn.defvjp(_bmmb_tn_fwd, _bmmb_tn_bwd)


def _unit_lower_solve_fwd(a, r):
    c = a.shape[-1]
    ri = lax.broadcasted_iota(jnp.int32, a.shape, 1)
    ci = lax.broadcasted_iota(jnp.int32, a.shape, 2)
    xm = -a
    inv = (ri == ci).astype(F32) + xm
    for _ in range(int(math.log2(c)) - 1):
        xm = _bdot(xm, xm, BNN, HI)
        inv = inv + _bdot(inv, xm, BNN, HI)
    x = _bdot(inv, r, BNN, HI)
    return x, (inv, x)


def _unit_lower_solve_bwd(res, g):
    inv, x = res
    dr = _bdot(inv, g, BTN, HI)
    return -_bdot(dr, x, BNT, HI), dr


@jax.custom_vjp
def _unit_lower_solve_given(a, r, inv):
    return _bdot(inv, r, BNN, HI)


def _unit_lower_solve_given_fwd(a, r, inv):
    x = _bdot(inv, r, BNN, HI)
    return x, (inv, x)


def _unit_lower_solve_given_bwd(res, g):
    da, dr = _unit_lower_solve_bwd(res, g)
    return da, dr, jnp.zeros_like(res[0])


_unit_lower_solve_given.defvjp(_unit_lower_solve_given_fwd, _unit_lower_solve_given_bwd)


def _gdn_intra(qkv, ba, al8, dt8, inv4=None):
    tm = qkv.shape[0]
    nb = tm // CHUNK
    bsz = DN_HEADS * nb

    def heads(x0):
        return jnp.concatenate([qkv[:, x0 + h * LANE:x0 + (h + 1) * LANE].reshape(nb, CHUNK, LANE) for h in range(DN_HEADS)], axis=0)

    def spread(c0):
        return jnp.concatenate([jnp.broadcast_to(ba[:, c0 + h:c0 + h + 1], (tm, LANE)).reshape(nb, CHUNK, LANE)
                                for h in range(DN_HEADS)], axis=0)

    def per_head(v8):
        return jnp.concatenate([jnp.broadcast_to(v8[0:1, h:h + 1].reshape(1, 1, 1), (nb, 1, LANE)) for h in range(DN_HEADS)], axis=0)

    ri = lax.broadcasted_iota(jnp.int32, (bsz, CHUNK, CHUNK), 1)
    ci = lax.broadcasted_iota(jnp.int32, (bsz, CHUNK, CHUNK), 2)
    incl = ri >= ci
    strict = ri > ci

    q = _l2norm(heads(0)) * (DN_DK ** -0.5)
    k = _l2norm(heads(DN_QK))
    va = heads(2 * DN_QK)
    beta = _sigmoid(spread(0))
    g = -jnp.exp(per_head(al8)) * _softplus(spread(DN_HEADS) + per_head(dt8))
    gc = _bdot(incl.astype(F32), g, BNN, HI)
    g_last = jnp.sum(g, axis=1, keepdims=True)
    gcol = gc[:, :, :CHUNK]
    diff = gcol - jnp.swapaxes(gcol, 1, 2)
    decay = jnp.where(incl, jnp.exp(jnp.where(incl, diff, 0.0)), 0.0)
    kb = k * beta
    a_mat = jnp.where(strict, _bmmb_nt(kb, k) * decay, 0.0)
    egc = jnp.exp(gc)
    rhs = jnp.concatenate([kb * egc, va * beta], axis=2)
    if inv4 is None:
        wu, (inv, _) = _unit_lower_solve_fwd(a_mat, rhs)
    else:
        wu = _unit_lower_solve_given(a_mat, rhs, inv4.reshape(bsz, CHUNK, CHUNK))
    attn = jnp.where(incl, _bmmb_nt(q, k) * decay, 0.0)

    def unheads(x):
        return jnp.concatenate([x[h * nb:(h + 1) * nb].reshape(tm, LANE) for h in range(DN_HEADS)], axis=1)

    w_c, u_c = wu[:, :, :DN_DK], wu[:, :, DN_DK:]
    kd = k * jnp.exp(g_last - gc)
    out = (unheads(q * egc - _bmmb(attn, w_c)), unheads(_bmmb(attn, u_c)),
           _bmmb_tn(kd, w_c).reshape(DN_HEADS, nb, DN_DK, DN_DK), _bmmb_tn(kd, u_c).reshape(DN_HEADS, nb, DN_DK, DN_DV),
           unheads(jnp.broadcast_to(g_last, (bsz, CHUNK, LANE))))
    return out if inv4 is not None else out + (inv.reshape(DN_HEADS, nb, CHUNK, CHUNK),)


def _gdn_scan_step(qp, op, c_mat, n_mat, gl, s):
    return _mmb(qp, s) + op, s * jnp.exp(gl) - _mmb(c_mat, s) + n_mat


def _gdn_intra_specs(t, tm, dts, order=lambda i: i):
    nb = tm // CHUNK
    row = pl.BlockSpec((tm, DN_VW), lambda i: (order(i), 0))
    mat = pl.BlockSpec((DN_HEADS, nb, DN_DK, DN_DV), lambda i: (0, order(i), 0, 0))
    row_shape = lambda d: jax.ShapeDtypeStruct((t, DN_VW), d)
    mat_shape = lambda d: jax.ShapeDtypeStruct((DN_HEADS, t // CHUNK, DN_DK, DN_DV), d)
    return [row, row, mat, mat, row], [row_shape(dts[0]), row_shape(dts[1]), mat_shape(dts[2]), mat_shape(dts[3]), row_shape(dts[4])]


def _gdn_intra_fwd(qkv, proj, al8, dt8, tm):
    t = qkv.shape[0]

    def body(qkv_ref, ba_ref, al_ref, dt_ref, *outs):
        for o, val in zip(outs, _gdn_intra(qkv_ref[...], ba_ref[...], al_ref[...], dt_ref[...])):
            o[...] = val.astype(o.dtype)

    specs, shapes = _gdn_intra_specs(t, tm, (BF, F32, BF, F32, F32))
    specs.append(_gdn_inverse_spec(tm))
    shapes.append(jax.ShapeDtypeStruct((DN_HEADS, t // CHUNK, CHUNK, CHUNK), F32))
    res = pl.pallas_call(
        body, grid=(t // tm,),
        in_specs=[pl.BlockSpec((tm, DN_CONV_CH), lambda i: (i, 0)), pl.BlockSpec((tm, LANE), lambda i: (i, P_BA // LANE)),
                  _full(al8.shape), _full(dt8.shape)],
        out_specs=specs, out_shape=shapes, name="gdn_intra_fwd",
        compiler_params=pltpu.CompilerParams(dimension_semantics=("parallel",)),
    )(qkv, proj, al8, dt8)
    return res[:5], res[5]


def _gdn_inverse_spec(tm):
    return pl.BlockSpec((DN_HEADS, tm // CHUNK, CHUNK, CHUNK), lambda i: (0, i, 0, 0))


def _gdn_intra_bwd(qkv, proj, al8, dt8, inverses, cts, tm):
    t = qkv.shape[0]

    def body(qkv_ref, ba_ref, al_ref, dt_ref, inv_ref, *refs):
        ct_refs, (dqkv_ref, dba_ref, dal_ref, ddt_ref) = refs[:5], refs[5:]

        @pl.when(pl.program_id(0) == 0)
        def _():
            dal_ref[...] = jnp.zeros_like(dal_ref)
            ddt_ref[...] = jnp.zeros_like(ddt_ref)

        _, vjp = jax.vjp(functools.partial(_gdn_intra, inv4=inv_ref[...]), qkv_ref[...], ba_ref[...], al_ref[...], dt_ref[...])
        dqkv, dba, dal, ddt = vjp(tuple(r[...] for r in ct_refs))
        dqkv_ref[...] = dqkv
        dba_ref[...] = dba.astype(dba_ref.dtype)
        dal_ref[...] += dal
        ddt_ref[...] += ddt

    specs, _ = _gdn_intra_specs(t, tm, (F32,) * 5)
    return pl.pallas_call(
        body, grid=(t // tm,),
        in_specs=[pl.BlockSpec((tm, DN_CONV_CH), lambda i: (i, 0)), pl.BlockSpec((tm, LANE), lambda i: (i, P_BA // LANE)),
                  _full(al8.shape), _full(dt8.shape), _gdn_inverse_spec(tm)] + specs,
        out_specs=[pl.BlockSpec((tm, DN_CONV_CH), lambda i: (i, 0)), pl.BlockSpec((tm, LANE), lambda i: (i, 0)),
                   _full(al8.shape), _full(dt8.shape)],
        out_shape=[jax.ShapeDtypeStruct((t, DN_CONV_CH), F32), jax.ShapeDtypeStruct((t, LANE), BF),
                   jax.ShapeDtypeStruct(al8.shape, F32), jax.ShapeDtypeStruct(dt8.shape, F32)],
        name="gdn_intra_bwd", compiler_params=pltpu.CompilerParams(dimension_semantics=("arbitrary",)),
    )(qkv, proj, al8, dt8, inverses, *cts)


def _gdn_scan_fwd(intra, tm):
    t = intra[0].shape[0]
    nb = tm // CHUNK
    nc = t // CHUNK

    def body(qp_ref, op_ref, c_ref, n_ref, gl_ref, o_ref, ss_ref, s_scr):
        @pl.when(pl.program_id(0) == 0)
        def _():
            s_scr[...] = jnp.zeros_like(s_scr)

        state = [s_scr[h] for h in range(DN_HEADS)]
        for cc in range(nb):
            rows = pl.ds(cc * CHUNK, CHUNK)
            for h in range(DN_HEADS):
                cols = pl.ds(h * DN_DV, DN_DV)
                ss_ref[cc, h] = state[h]
                o_ref[rows, cols], state[h] = _gdn_scan_step(
                    qp_ref[rows, cols], op_ref[rows, cols], c_ref[h, cc], n_ref[h, cc], gl_ref[pl.ds(cc * CHUNK, 1), cols], state[h])
        for h in range(DN_HEADS):
            s_scr[h] = state[h]

    specs, _ = _gdn_intra_specs(t, tm, (F32,) * 5)
    return pl.pallas_call(
        body, grid=(t // tm,), in_specs=specs,
        out_specs=[pl.BlockSpec((tm, DN_VW), lambda i: (i, 0)),
                   pl.BlockSpec((nb, DN_HEADS, DN_DK, DN_DV), lambda i: (i, 0, 0, 0))],
        out_shape=[jax.ShapeDtypeStruct((t, DN_VW), F32), jax.ShapeDtypeStruct((nc, DN_HEADS, DN_DK, DN_DV), F32)],
        scratch_shapes=[pltpu.VMEM((DN_HEADS, DN_DK, DN_DV), F32)], name="gdn_scan_fwd",
        compiler_params=pltpu.CompilerParams(dimension_semantics=("arbitrary",)),
    )(*intra)


def _gdn_scan_bwd(intra, states, do, tm):
    t = intra[0].shape[0]
    nb = tm // CHUNK
    ng = t // tm

    def body(qp_ref, op_ref, c_ref, n_ref, gl_ref, ss_ref, do_ref, dqp_ref, dop_ref, dc_ref, dn_ref, dgl_ref, ds_scr):
        @pl.when(pl.program_id(0) == 0)
        def _():
            ds_scr[...] = jnp.zeros_like(ds_scr)

        d_state = [ds_scr[h] for h in range(DN_HEADS)]
        for cc in reversed(range(nb)):
            rows = pl.ds(cc * CHUNK, CHUNK)
            for h in range(DN_HEADS):
                cols = pl.ds(h * DN_DV, DN_DV)
                _, vjp = jax.vjp(_gdn_scan_step, qp_ref[rows, cols].astype(F32), op_ref[rows, cols], c_ref[h, cc].astype(F32),
                                 n_ref[h, cc], gl_ref[pl.ds(cc * CHUNK, 1), cols], ss_ref[cc, h])
                dqp_ref[rows, cols], dop_ref[rows, cols], dc_ref[h, cc], dn_ref[h, cc], dgl, d_state[h] = vjp(
                    (do_ref[rows, cols], d_state[h]))
                first_row = lax.broadcasted_iota(jnp.int32, (CHUNK, DN_DV), 0) == 0
                dgl_ref[rows, cols] = jnp.where(first_row, dgl, 0.0)
        for h in range(DN_HEADS):
            ds_scr[h] = d_state[h]

    five, shapes = _gdn_intra_specs(t, tm, (F32,) * 5, order=lambda i: ng - 1 - i)
    row = five[0]
    return pl.pallas_call(
        body, grid=(ng,),
        in_specs=five + [pl.BlockSpec((nb, DN_HEADS, DN_DK, DN_DV), lambda i: (ng - 1 - i, 0, 0, 0)), row],
        out_specs=five, out_shape=shapes,
        scratch_shapes=[pltpu.VMEM((DN_HEADS, DN_DK, DN_DV), F32)], name="gdn_scan_bwd",
        compiler_params=pltpu.CompilerParams(dimension_semantics=("arbitrary",)),
    )(*intra, states, do)


def _gdn_out(o, z, g):
    parts = []
    for h in range(DN_HEADS):
        sl = slice(h * DN_DV, (h + 1) * DN_DV)
        parts.append(_rmsnorm(o[:, sl], g) * _silu(z[:, sl]))
    return parts


_Q_SCALE = math.log2(math.e) / math.sqrt(QK_NOPE + QK_ROPE)


def _rope_tables(pos, inv_freq2):
    lane = lax.broadcasted_iota(jnp.int32, (1, LANE), 1)
    ang = pos * inv_freq2
    cos = jnp.where(lane < QK_ROPE, jnp.cos(ang), 0.0)
    sin = jnp.where(lane < QK_ROPE // 2, -jnp.sin(ang), jnp.where(lane < QK_ROPE, jnp.sin(ang), 0.0))
    return cos, sin


def _rope_swap():
    ri = lax.broadcasted_iota(jnp.int32, (LANE, LANE), 0)
    ci = lax.broadcasted_iota(jnp.int32, (LANE, LANE), 1)
    half = QK_ROPE // 2
    return (((ci < half) & (ri == ci + half)) | ((ci >= half) & (ci < QK_ROPE) & (ri == ci - half))).astype(F32)


def _mla_prep(cq, ckv, kr, gq, gkv, w_uq, w_ukv, cos, sin, swap):
    rope = lambda u: u * cos + _doth(u, swap) * sin
    q_lin = _mmb_nt(_rmsnorm(cq, gq), w_uq) * _Q_SCALE
    kv_lin = _mmb_nt(_rmsnorm(ckv, gkv), w_ukv)
    k_rope = rope(kr)
    qs, ks, vs = [], [], []
    for h in range(MLA_HEADS):
        qs += [q_lin[:, h * LANE:(h + 1) * LANE], rope(q_lin[:, (MLA_HEADS + h) * LANE:(MLA_HEADS + h + 1) * LANE])]
        ks += [kv_lin[:, 2 * h * LANE:(2 * h + 1) * LANE], k_rope]
        vs += [kv_lin[:, (2 * h + 1) * LANE:(2 * h + 2) * LANE]]
    return qs + ks + vs


def _mla_prep_fwd(proj, pos_col, inv_freq2, gq, gkv, w_uq, w_ukv, tm):
    t = proj.shape[0]
    nq = 2 * MLA_HEADS

    def body(cq_ref, ckv_ref, kr_ref, pos_ref, f_ref, gq_ref, gkv_ref, wq_ref, wkv_ref, q_ref, k_ref, v_ref):
        cos, sin = _rope_tables(pos_ref[...], f_ref[...])
        outs = _mla_prep(cq_ref[...], ckv_ref[...], kr_ref[...], gq_ref[...], gkv_ref[...], wq_ref[...], wkv_ref[...],
                         cos, sin, _rope_swap())
        for i in range(nq):
            q_ref[:, pl.ds(i * LANE, LANE)] = outs[i].astype(q_ref.dtype)
            k_ref[:, pl.ds(i * LANE, LANE)] = outs[nq + i].astype(k_ref.dtype)
        for h in range(MLA_HEADS):
            v_ref[:, pl.ds(h * LANE, LANE)] = outs[2 * nq + h].astype(v_ref.dtype)

    row = lambda w, j: pl.BlockSpec((tm, w), functools.partial(lambda i, j: (i, j), j=j))
    return pl.pallas_call(
        body, grid=(t // tm,),
        in_specs=[row(Q_LORA, P_CQ // Q_LORA), row(KV_LORA, P_CKV // KV_LORA), row(LANE, P_KR // LANE),
                  pl.BlockSpec((tm, 1), lambda i: (i, 0)), _full(inv_freq2.shape), _full(gq.shape), _full(gkv.shape),
                  _full(w_uq.shape), _full(w_ukv.shape)],
        out_specs=[row(nq * LANE, 0), row(nq * LANE, 0), row(MLA_VW, 0)],
        out_shape=[jax.ShapeDtypeStruct((t, nq * LANE), BF), jax.ShapeDtypeStruct((t, nq * LANE), BF),
                   jax.ShapeDtypeStruct((t, MLA_VW), BF)],
        name="mla_prep_fwd", compiler_params=pltpu.CompilerParams(dimension_semantics=("arbitrary",)),
    )(proj, proj, proj, pos_col, inv_freq2, gq, gkv, w_uq, w_ukv)


def _mla_prep_bwd(proj, pos_col, inv_freq2, gq, gkv, w_uq, w_ukv, dq, dk, dv, tm):
    t = proj.shape[0]
    nq = 2 * MLA_HEADS

    def body(cq_ref, ckv_ref, kr_ref, pos_ref, f_ref, gq_ref, gkv_ref, wq_ref, wkv_ref, dq_ref, dk_ref, dv_ref,
             dcq_ref, dckv_ref, dkr_ref, dgq_ref, dgkv_ref, dwq_ref, dwkv_ref):
        @pl.when(pl.program_id(0) == 0)
        def _():
            for o in (dgq_ref, dgkv_ref, dwq_ref, dwkv_ref):
                o[...] = jnp.zeros_like(o)

        cos, sin = _rope_tables(pos_ref[...], f_ref[...])
        f = functools.partial(_mla_prep, cos=cos, sin=sin, swap=_rope_swap())
        _, vjp = jax.vjp(f, cq_ref[...], ckv_ref[...], kr_ref[...], gq_ref[...], gkv_ref[...], wq_ref[...], wkv_ref[...])
        cts = [dq_ref[:, pl.ds(i * LANE, LANE)] for i in range(nq)]
        cts += [dk_ref[:, pl.ds(i * LANE, LANE)] for i in range(nq)]
        cts += [dv_ref[:, pl.ds(h * LANE, LANE)] for h in range(MLA_HEADS)]
        dcq, dckv, dkr, dgq, dgkv, dwq, dwkv = vjp(cts)
        dcq_ref[...] = dcq.astype(dcq_ref.dtype)
        dckv_ref[...] = dckv.astype(dckv_ref.dtype)
        dkr_ref[...] = dkr.astype(dkr_ref.dtype)
        dgq_ref[...] += dgq
        dgkv_ref[...] += dgkv
        dwq_ref[...] += dwq
        dwkv_ref[...] += dwkv

    row = lambda w, j: pl.BlockSpec((tm, w), functools.partial(lambda i, j: (i, j), j=j))
    return pl.pallas_call(
        body, grid=(t // tm,),
        in_specs=[row(Q_LORA, P_CQ // Q_LORA), row(KV_LORA, P_CKV // KV_LORA), row(LANE, P_KR // LANE),
                  pl.BlockSpec((tm, 1), lambda i: (i, 0)), _full(inv_freq2.shape), _full(gq.shape), _full(gkv.shape),
                  _full(w_uq.shape), _full(w_ukv.shape), row(nq * LANE, 0), row(nq * LANE, 0), row(MLA_VW, 0)],
        out_specs=[row(Q_LORA, 0), row(KV_LORA, 0), row(LANE, 0), _full(gq.shape), _full(gkv.shape),
                   _full(w_uq.shape), _full(w_ukv.shape)],
        out_shape=[jax.ShapeDtypeStruct((t, Q_LORA), BF), jax.ShapeDtypeStruct((t, KV_LORA), BF),
                   jax.ShapeDtypeStruct((t, LANE), BF), jax.ShapeDtypeStruct(gq.shape, F32),
                   jax.ShapeDtypeStruct(gkv.shape, F32), jax.ShapeDtypeStruct(w_uq.shape, F32),
                   jax.ShapeDtypeStruct(w_ukv.shape, F32)],
        name="mla_prep_bwd", compiler_params=pltpu.CompilerParams(dimension_semantics=("arbitrary",)),
    )(proj, proj, proj, pos_col, inv_freq2, gq, gkv, w_uq, w_ukv, dq, dk, dv)


_NEG = -1e30
_LN2 = math.log(2.0)
ATT_CHAINS = 2


def _causal(tq, tk, q0, k0):
    row = q0 + lax.broadcasted_iota(jnp.int32, (tq, tk), 0)
    col = k0 + lax.broadcasted_iota(jnp.int32, (tq, tk), 1)
    return col <= row


def _attn_fwd(q, k, v, tq, tk):
    t = q.shape[0]

    assert tk % tq == 0 or tq % tk == 0
    n_diag = max(1, tq // tk)

    th = tq // ATT_CHAINS

    def body(q_ref, k_ref, v_ref, o_ref, lse_ref):
        i = pl.program_id(1)
        n_full = (i * tq) // tk

        def step(k0, carry, masked):
            kt = k_ref[pl.ds(k0, tk), :]
            vt = v_ref[pl.ds(k0, tk), :]
            out = []
            for c, (m, l, acc) in enumerate(carry):
                s = _dot(q_ref[pl.ds(c * th, th), :], kt, NT)
                if masked:
                    s = jnp.where(_causal(th, tk, i * tq + c * th, k0), s, _NEG)
                m_new = jnp.maximum(m, jnp.max(s, axis=-1, keepdims=True))
                p = jnp.exp2(s - m_new)
                alpha = jnp.exp2(m - m_new)
                out.append((m_new, alpha * l + jnp.sum(p, axis=-1, keepdims=True), alpha * acc + _dot(p.astype(BF), vt)))
            return tuple(out)

        init = tuple((jnp.full((th, 1), _NEG, F32), jnp.zeros((th, 1), F32), jnp.zeros((th, V_HEAD), F32)) for _ in range(ATT_CHAINS))
        carry = lax.fori_loop(0, n_full, lambda j, c: step(pl.multiple_of(j * tk, tk), c, False), init)
        for dd in range(n_diag):
            carry = step(pl.multiple_of((n_full + dd) * tk, tk), carry, True)
        for c, (m, l, acc) in enumerate(carry):
            o_ref[pl.ds(c * th, th), :] = acc / l
            lse_ref[pl.ds(c * th, th), :] = jnp.broadcast_to(m + jnp.log2(l), (th, LANE))

    return pl.pallas_call(
        body, grid=(MLA_HEADS, t // tq),
        in_specs=[pl.BlockSpec((tq, 2 * LANE), lambda h, i: (i, h)), pl.BlockSpec((t, 2 * LANE), lambda h, i: (0, h)),
                  pl.BlockSpec((t, V_HEAD), lambda h, i: (0, h))],
        out_specs=[pl.BlockSpec((tq, V_HEAD), lambda h, i: (i, h)), pl.BlockSpec((tq, LANE), lambda h, i: (i, h))],
        out_shape=[jax.ShapeDtypeStruct((t, MLA_VW), F32), jax.ShapeDtypeStruct((t, MLA_HEADS * LANE), F32)],
        name="attn_fwd", compiler_params=pltpu.CompilerParams(dimension_semantics=("parallel", "arbitrary")),
    )(q, k, v)


def _attn_bwd(q, k, v, do, lse, delta, tq, tk):
    t = q.shape[0]
    nkt = t // tk
    assert tk % tq == 0

    def body(q_ref, k_ref, v_ref, do_ref, lse_ref, dl_ref, dq_ref, dk_ref, dv_ref):
        j = pl.program_id(1)

        @pl.when(j == 0)
        def _():
            dq_ref[...] = jnp.zeros_like(dq_ref)

        kt = k_ref[...]
        vt = v_ref[...]

        def step(q0, carry, masked):
            dk, dv = carry
            rows = pl.ds(q0, tq)
            qt = q_ref[rows, :]
            dot_ = do_ref[rows, :]
            p = jnp.exp2(_dot(qt, kt, NT) - lse_ref[rows, pl.ds(0, 1)])
            if masked:
                p = jnp.where(_causal(tq, tk, q0, j * tk), p, 0.0)
            dv = dv + _dot(p.astype(BF), dot_, TN)
            ds = (p * (_dot(dot_, vt, NT) - dl_ref[rows, pl.ds(0, 1)])).astype(BF)
            dk = dk + _dot(ds, qt, TN)
            dq_ref[rows, :] += _dot(ds, kt)
            return dk, dv

        per = tk // tq
        carry = (jnp.zeros((tk, 2 * LANE), F32), jnp.zeros((tk, V_HEAD), F32))
        for dd in range(per):
            carry = step(pl.multiple_of(j * tk + dd * tq, tq), carry, True)

        def group(g, c):
            for dd in range(per):
                c = step(pl.multiple_of(g * tk + dd * tq, tq), c, False)
            return c

        dk, dv = lax.fori_loop(j + 1, nkt, group, carry)
        dk_ref[...] = dk * _LN2
        dv_ref[...] = dv

        @pl.when(j == nkt - 1)
        def _():
            dq_ref[...] = dq_ref[...] * _LN2

    return pl.pallas_call(
        body, grid=(MLA_HEADS, nkt),
        in_specs=[pl.BlockSpec((t, 2 * LANE), lambda h, j: (0, h)), pl.BlockSpec((tk, 2 * LANE), lambda h, j: (j, h)),
                  pl.BlockSpec((tk, V_HEAD), lambda h, j: (j, h)), pl.BlockSpec((t, V_HEAD), lambda h, j: (0, h)),
                  pl.BlockSpec((t, LANE), lambda h, j: (0, h)), pl.BlockSpec((t, LANE), lambda h, j: (0, h))],
        out_specs=[pl.BlockSpec((t, 2 * LANE), lambda h, j: (0, h)), pl.BlockSpec((tk, 2 * LANE), lambda h, j: (j, h)),
                   pl.BlockSpec((tk, V_HEAD), lambda h, j: (j, h))],
        out_shape=[jax.ShapeDtypeStruct((t, MLA_HEADS * 2 * LANE), F32), jax.ShapeDtypeStruct((t, MLA_HEADS * 2 * LANE), F32),
                   jax.ShapeDtypeStruct((t, MLA_VW), F32)],
        name="attn_bwd", compiler_params=pltpu.CompilerParams(dimension_semantics=("parallel", "arbitrary")),
    )(q, k, v, do, lse, delta)


def _adam_update(w, g, m, v):
    mm = ADAM_B1 * m + (1.0 - ADAM_B1) * g
    vv = ADAM_B2 * v + (1.0 - ADAM_B2) * jnp.square(g)
    m_hat = mm / (1.0 - ADAM_B1 ** ADAM_STEP)
    v_hat = vv / (1.0 - ADAM_B2 ** ADAM_STEP)
    return -ADAM_LR * (m_hat / (jnp.sqrt(v_hat) + ADAM_EPS) + ADAM_WD * w), mm, vv


def _adamw(w, g, m, v, name):
    r, c = w.shape
    tr = max([r // s for s in range(1, r // 8 + 1) if r % s == 0 and (r // s) % 8 == 0 and r // s <= 256] or [r])
    slots = g.ndim == 3

    def body(w_ref, g_ref, m_ref, v_ref, g_out, d_ref, nm_ref, nv_ref):
        if slots:
            gg = g_ref[0].astype(F32)
            for s in range(1, N_DEV):
                gg = gg + g_ref[s].astype(F32)
        else:
            gg = g_ref[...]
        g_out[...] = gg
        d_ref[...], nm_ref[...], nv_ref[...] = _adam_update(w_ref[...], gg, m_ref[...], v_ref[...])

    spec = pl.BlockSpec((tr, c), lambda i: (i, 0))
    g_spec = pl.BlockSpec((N_DEV, tr, c), lambda i: (0, i, 0)) if slots else spec
    return pl.pallas_call(
        body, grid=(r // tr,), in_specs=[spec, g_spec, spec, spec], out_specs=[spec] * 4,
        out_shape=[jax.ShapeDtypeStruct((r, c), F32)] * 4, name=name,
        compiler_params=pltpu.CompilerParams(dimension_semantics=("arbitrary",)),
    )(w, g, m, v)


def _adamw_many(ws, gs, ms, vs, name):
    n = len(ws)

    def body(*refs):
        for i in range(n):
            w_ref, g_ref, m_ref, v_ref = (refs[j * n + i] for j in range(4))
            d_ref, nm_ref, nv_ref = (refs[(4 + j) * n + i] for j in range(3))
            d_ref[...], nm_ref[...], nv_ref[...] = _adam_update(w_ref[...], g_ref[...], m_ref[...], v_ref[...])

    shapes = [jax.ShapeDtypeStruct(w.shape, F32) for w in ws]
    outs = pl.pallas_call(body, out_shape=shapes * 3, name=name)(*ws, *gs, *ms, *vs)
    return outs[:n], outs[n:2 * n], outs[2 * n:]


def _cast_bf16(xs, name, after=None):
    n = len(xs)
    extra = [] if after is None else [after]

    def body(*refs):
        outs = refs[n + len(extra):]
        for i in range(n):
            outs[i][...] = refs[i][...].astype(BF)

    vmem = pl.BlockSpec(memory_space=pltpu.VMEM)
    return pl.pallas_call(
        body, out_shape=[jax.ShapeDtypeStruct(x.shape, BF) for x in xs], name=name,
        in_specs=[vmem] * n + [pl.BlockSpec(memory_space=pl.ANY)] * len(extra), out_specs=[vmem] * n)(*xs, *extra)


def _pad_rows(a, n):
    return jnp.pad(a, ((0, n - a.shape[0]), (0, 0)))


def _w_in_to_padded(wt):
    s_ba = P_CQ
    s_cq = s_ba + 2 * DN_HEADS
    s_kr = s_cq + Q_LORA + KV_LORA
    return jnp.concatenate([wt[:s_ba], wt[s_cq:s_kr], _pad_rows(wt[s_ba:s_cq], LANE), _pad_rows(wt[s_kr:], LANE)], axis=0)


def _w_in_from_padded(wt):
    return jnp.concatenate([wt[:P_CQ], wt[P_BA:P_BA + 2 * DN_HEADS], wt[P_CQ:P_BA], wt[P_KR:P_KR + QK_ROPE]], axis=0)


def _w_uq_to_padded(wt):
    w3 = wt.reshape(MLA_HEADS, QK_NOPE + QK_ROPE, Q_LORA)
    nope = w3[:, :QK_NOPE].reshape(MLA_HEADS * QK_NOPE, Q_LORA)
    rope = jnp.pad(w3[:, QK_NOPE:], ((0, 0), (0, LANE - QK_ROPE), (0, 0))).reshape(MLA_HEADS * LANE, Q_LORA)
    return jnp.concatenate([nope, rope], axis=0)


def _w_uq_from_padded(wt):
    nope = wt[:MLA_HEADS * QK_NOPE].reshape(MLA_HEADS, QK_NOPE, Q_LORA)
    rope = wt[MLA_HEADS * QK_NOPE:].reshape(MLA_HEADS, LANE, Q_LORA)[:, :QK_ROPE]
    return jnp.concatenate([nope, rope], axis=1).reshape(MLA_HEADS * (QK_NOPE + QK_ROPE), Q_LORA)


def _pack(pieces, width, row_mult):
    flat = jnp.concatenate([p.reshape(-1) for p in pieces])
    n = flat.shape[0]
    rows = -(-n // (width * row_mult)) * row_mult
    return jnp.pad(flat, (0, rows * width - n)).reshape(rows, width)


def _unpack(flat, shapes):
    out, o = [], 0
    for s in shapes:
        n = math.prod(s)
        out.append(flat[o:o + n].reshape(s))
        o += n
    return out


def kernel(x, c, positions, w_ada, b_ada, w_in, conv_w, a_log, dt_bias, dn_norm_g, q_norm_g, w_uq, kv_norm_g, w_ukv, w_o, ln1_g, ln1_b, w_gate, w_up, w_down, ln2_g, ln2_b, loss_target, m_w_ada, m_b_ada, m_w_in, m_conv_w, m_a_log, m_dt_bias, m_dn_norm_g, m_q_norm_g, m_w_uq, m_kv_norm_g, m_w_ukv, m_w_o, m_ln1_g, m_ln1_b, m_w_gate, m_w_up, m_w_down, m_ln2_g, m_ln2_b, v_w_ada, v_b_ada, v_w_in, v_conv_w, v_a_log, v_dt_bias, v_dn_norm_g, v_q_norm_g, v_w_uq, v_kv_norm_g, v_w_ukv, v_w_o, v_ln1_g, v_ln1_b, v_w_gate, v_w_up, v_w_down, v_ln2_g, v_ln2_b):
    me = 4 * lax.axis_index("x") + 2 * lax.axis_index("y") + lax.axis_index("c")
    t, d = x.shape[1], x.shape[2]
    ada_n = w_ada.shape[2]

    tr = lambda w: w[0].T
    rows = lambda a: a.reshape(-1, a.shape[2])
    (in_shard,) = _cast_bf16([tr(w_in)], "cast_w_in")
    cw = conv_w.shape[3]
    a_in, c_all, conv_all = _gather_by_chip([in_shard, c, conv_w[0, :, 0, :]], "gather_w_in_and_small")
    c_all = c_all.reshape(N_DEV, d)
    conv_full = conv_all.transpose(1, 0, 2).reshape(CONV_K, N_DEV * cw)
    conv_w8 = jnp.pad(conv_full, ((0, 8 - CONV_K), (0, 0)))

    b_ada_mine = lax.dynamic_slice(b_ada, (0, me * ada_n), (1, ada_n))
    mod_cols = _mod_fwd(c_all, w_ada[0], b_ada_mine)
    (mod_all,) = _exchange([mod_cols.reshape(N_DEV, 1, ada_n)], "scatter_mod", scatter=True)
    mod = mod_all.reshape(1, N_DEV * ada_n)

    later = _cast_bf16([tr(w_uq), tr(w_ukv), w_o[0], tr(w_gate), tr(w_up), w_down[0]], "cast_weights", after=mod)
    mixer_gather, token_a = _exchange_start(later[:3], "gather_mixer_weights_start", scatter=False)
    ffn_gather, token_b = _exchange_start(later[3:], "gather_ffn_weights_start", scatter=False)
    mod = mod + (token_a + token_b)
    w_in_t = _w_in_to_padded(rows(a_in))

    def mixer_weights(after):
        a_uq, a_ukv, a_o = _exchange_wait(mixer_gather, after, "gather_mixer_weights_wait", scatter=False)
        return _w_uq_to_padded(rows(a_uq)), rows(a_ukv), rows(a_o)

    def ffn_weights(after):
        a_gate, a_up, a_down = _exchange_wait(ffn_gather, after, "gather_ffn_weights_wait", scatter=False)
        return rows(a_gate), rows(a_up), rows(a_down)

    def by_dest(g):
        return g.reshape(N_DEV, -1, g.shape[1])

    scatters = {}

    def grads_ready(tag, *g):
        if tag == "ffn":
            pieces = [by_dest(a) for a in g]
        elif tag == "mixer":
            g_w_o, g_w_uq_t, g_w_ukv_t = g
            pieces = [by_dest(g_w_o), by_dest(_w_uq_from_padded(g_w_uq_t).astype(BF)), by_dest(g_w_ukv_t.astype(BF))]
        else:
            pieces = [by_dest(_w_in_from_padded(g[0]))]
        scatters[tag], token = _exchange_start(pieces, "scatter_%s_grads_start" % tag, scatter=True)
        return token

    loc = _local_step(x[0], loss_target[0], positions[0], mod, w_in_t, mixer_weights, ffn_weights, grads_ready,
                      conv_w8, a_log, dt_bias, dn_norm_g, q_norm_g, kv_norm_g, ln1_g, ln1_b, ln2_g, ln2_b)
    grad_x, loss_acc, dmod, d_conv8, d_al8, d_dt8, d_dn_g, d_q_g, d_kv_g, d_ln1_g, d_ln1_b, d_ln2_g, d_ln2_b = loc

    small_shapes = [(6 * d,), (CONV_K, N_DEV * cw), (DN_HEADS,), (DN_HEADS,), (DN_DV,), (Q_LORA,), (KV_LORA,), (d,), (d,), (d,), (d,), (1,)]
    gsmall = _pack([dmod, d_conv8[:CONV_K], d_al8[0, :DN_HEADS], d_dt8[0, :DN_HEADS], d_dn_g, d_q_g, d_kv_g,
                    d_ln1_g, d_ln1_b, d_ln2_g, d_ln2_b, loss_acc[0, :1]], LANE, 8)
    (gsmall_all,) = _exchange([gsmall], "gather_small_grads", scatter=False)
    dmod_all = gsmall_all.reshape(N_DEV, -1)[:, :6 * d]
    tot = _unpack(_sum_slots(gsmall_all, "sum_small_grads").reshape(-1), small_shapes)
    g_b_ada, g_conv_full, g_a_log, g_dt_bias, g_dn_g, g_q_g, g_kv_g, g_ln1_g, g_ln1_b, g_ln2_g, g_ln2_b, loss1 = tot
    loss = loss1.reshape(())
    g_conv_w = lax.dynamic_slice(g_conv_full, (0, me * cw), (CONV_K, cw))
    g_w_ada = _mod_bwd(c_all.T, lax.dynamic_slice(dmod_all, (0, me * ada_n), (N_DEV, ada_n)))

    grads = {"w_ada": g_w_ada[None], "b_ada": g_b_ada[None], "conv_w": g_conv_w[None, :, None, :],
             "a_log": g_a_log[None], "dt_bias": g_dt_bias[None], "dn_norm_g": g_dn_g[None], "q_norm_g": g_q_g[None],
             "kv_norm_g": g_kv_g[None], "ln1_g": g_ln1_g[None], "ln1_b": g_ln1_b[None], "ln2_g": g_ln2_g[None], "ln2_b": g_ln2_b[None]}
    weights = dict(w_ada=w_ada, b_ada=b_ada, w_in=w_in, conv_w=conv_w, a_log=a_log, dt_bias=dt_bias, dn_norm_g=dn_norm_g,
                   q_norm_g=q_norm_g, w_uq=w_uq, kv_norm_g=kv_norm_g, w_ukv=w_ukv, w_o=w_o, ln1_g=ln1_g, ln1_b=ln1_b,
                   w_gate=w_gate, w_up=w_up, w_down=w_down, ln2_g=ln2_g, ln2_b=ln2_b)
    ms = dict(w_ada=m_w_ada, b_ada=m_b_ada, w_in=m_w_in, conv_w=m_conv_w, a_log=m_a_log, dt_bias=m_dt_bias,
              dn_norm_g=m_dn_norm_g, q_norm_g=m_q_norm_g, w_uq=m_w_uq, kv_norm_g=m_kv_norm_g, w_ukv=m_w_ukv, w_o=m_w_o,
              ln1_g=m_ln1_g, ln1_b=m_ln1_b, w_gate=m_w_gate, w_up=m_w_up, w_down=m_w_down, ln2_g=m_ln2_g, ln2_b=m_ln2_b)
    vs = dict(w_ada=v_w_ada, b_ada=v_b_ada, w_in=v_w_in, conv_w=v_conv_w, a_log=v_a_log, dt_bias=v_dt_bias,
              dn_norm_g=v_dn_norm_g, q_norm_g=v_q_norm_g, w_uq=v_w_uq, kv_norm_g=v_kv_norm_g, w_ukv=v_w_ukv, w_o=v_w_o,
              ln1_g=v_ln1_g, ln1_b=v_ln1_b, w_gate=v_w_gate, w_up=v_w_up, w_down=v_w_down, ln2_g=v_ln2_g, ln2_b=v_ln2_b)
    names = list(weights)
    big = ("w_ada", "w_gate", "w_up", "w_down", "w_o", "w_uq", "w_ukv", "w_in")
    waits = {"w_gate": ("ffn", ("w_gate", "w_up", "w_down")), "w_o": ("mixer", ("w_o", "w_uq", "w_ukv")), "w_in": ("in", ("w_in",))}
    delta_w, new_m, new_v, slots = {}, {}, {}, {}
    last = g_w_ada
    for n in big:
        if n == "w_in":
            rest = [r for r in names if r not in big]
            flat2 = lambda a: a.reshape(-1, a.shape[-1])
            outs = _adamw_many(*[[flat2(src[r]) for r in rest] for src in (weights, grads, ms, vs)], "adamw_small")
            for dst, o in zip((delta_w, new_m, new_v), outs):
                for r, a in zip(rest, o):
                    dst[r] = a.reshape(weights[r].shape)
            last = outs[0][0]
        transposed = n in ("w_in", "w_uq", "w_ukv", "w_gate", "w_up")
        two = (lambda a: a[0].T) if transposed else (lambda a: a[0])
        back = (lambda a: a.T[None]) if transposed else (lambda a: a[None])
        if n in waits:
            tag, members = waits[n]
            slots.update(zip(members, _exchange_wait(scatters[tag], last, "scatter_%s_grads_wait" % tag, scatter=True)))
        g_in = slots[n] if n in slots else two(grads[n])
        gr, dlt, nm, nv = _adamw(two(weights[n]), g_in, two(ms[n]), two(vs[n]), "adamw_" + n)
        grads[n], delta_w[n], new_m[n], new_v[n] = back(gr), back(dlt), back(nm), back(nv)
        last = nv

    return (loss, grad_x[None], *[grads[n] for n in names], *[delta_w[n] for n in names],
            *[new_m[n] for n in names], *[new_v[n] for n in names])


def _local_step(xs, tgt, pos, mod, w_in_t, mixer_weights, ffn_weights, grads_ready, conv_w8,
                a_log, dt_bias, dn_norm_g, q_norm_g, kv_norm_g, ln1_g, ln1_b, ln2_g, ln2_b):
    t, d = xs.shape
    sh_m, sc_m, gt_m, sh_f, sc_f, gt_f = [mod[:, i * d:(i + 1) * d] for i in range(6)]
    pos_col = pos.astype(F32).reshape(t, 1)
    inv_freq = 1.0 / (ROPE_THETA ** (jnp.arange(0, QK_ROPE, 2, dtype=F32) / QK_ROPE))
    inv_freq2 = jnp.pad(jnp.concatenate([inv_freq, inv_freq]), (0, LANE - QK_ROPE)).reshape(1, LANE)
    al8 = jnp.pad(a_log, ((0, 7), (0, LANE - DN_HEADS)))
    dt8 = jnp.pad(dt_bias, ((0, 7), (0, LANE - DN_HEADS)))

    tm = min(512, t)
    tq = min(256, t)
    tk = min(512, t)

    (h1,) = _rowwise("modulate_in", lambda xx, sc, sh: xx * (1.0 + sc) + sh, [xs], [sc_m, sh_m], [(d, BF)], [], tm)
    proj = _matmul(h1, w_in_t, "nt", "in_proj")
    qkv = _conv_fwd(proj, conv_w8, min(256, t))
    gdn_tm = min(512, t)
    intra, inverses = _gdn_intra_fwd(qkv, proj, al8, dt8, gdn_tm)
    o_dn, states = _gdn_scan_fwd(intra, gdn_tm)
    w_uq_t, w_ukv_t, w_o_f = mixer_weights(states)
    qc, kc, vc = _mla_prep_fwd(proj, pos_col, inv_freq2, q_norm_g, kv_norm_g, w_uq_t, w_ukv_t, tm)
    o_mla, lse = _attn_fwd(qc, kc, vc, min(1024, t), min(1024, t))

    def mix_in(o, z, om, g):
        return jnp.concatenate(_gdn_out(o, z, g) + [om], axis=1)

    (mixin,) = _rowwise("mixer_out", mix_in, [o_dn, (proj, DN_VW, P_Z // DN_VW), o_mla], [dn_norm_g], [(2 * DN_VW, BF)], [], tm)
    mix = _matmul(mixin, w_o_f, "nn", "out_proj")

    def block1(xx, mx, gt, g1, b1, sc, sh):
        x1 = _layernorm(DEEPNORM_ALPHA * xx + gt * mx, g1, b1)
        return x1, x1 * (1.0 + sc) + sh

    x1, h2 = _rowwise("norm1_modulate", block1, [xs, mix], [gt_m, ln1_g, ln1_b, sc_f, sh_f], [(d, F32), (d, BF)], [], tm)
    w_gate_f, w_up_f, w_down_f = ffn_weights(h2)
    act, gate, up = _ffn_in(h2, w_gate_f, w_up_f)
    ff = _matmul(act, w_down_f, "nn", "ffn_out")

    def tail_loss(x1_, ff_, gt, g2, b2, tg):
        y = _layernorm(DEEPNORM_ALPHA * x1_ + gt * ff_, g2, b2)
        return 0.5 * jnp.sum(jnp.mean(jnp.square(y - tg), axis=-1))

    def tail(x1_, ff_, tg, gt, g2, b2):
        loss, (dx1, dff, dgt, dg2, db2) = jax.value_and_grad(tail_loss, argnums=(0, 1, 2, 3, 4))(x1_, ff_, gt, g2, b2, tg)
        return dx1, dff, jnp.full((1, LANE), loss, F32), dgt, dg2, db2

    dx1_a, dff, loss_acc, d_gt_f, d_ln2_g, d_ln2_b = _rowwise(
        "norm2_loss", tail, [x1, ff, tgt], [gt_f, ln2_g, ln2_b], [(d, F32), (d, BF)], [(1, LANE), (1, d), (1, d), (1, d)], tm)

    g_w_down = _matmul(act, dff, "tn", "d_w_down", BF)
    dgate, dup = _ffn_act_bwd(dff, w_down_f, gate, up)
    g_w_gate = _matmul(dgate, h2, "tn", "d_w_gate", BF)
    g_w_up = _matmul(dup, h2, "tn", "d_w_up", BF)
    token = grads_ready("ffn", g_w_gate, g_w_up, g_w_down)
    dh2 = _matmul2_nn(dgate, w_gate_f, dup, w_up_f, "d_ffn_in")

    def block1_bwd(xx, mx, dx1_, dh2_, gt, g1, b1, sc, sh):
        _, vjp = jax.vjp(block1, xx, mx, gt, g1, b1, sc, sh)
        dxx, dmx, dgt, dg1, db1, dsc, dsh = vjp((dx1_, dh2_))
        return dxx, dmx, dgt, dg1, db1, dsc, dsh

    dx_a, dmix, d_gt_m, d_ln1_g, d_ln1_b, d_sc_f, d_sh_f = _rowwise(
        "norm1_modulate_bwd", block1_bwd, [xs, mix, dx1_a, dh2], [gt_m + token, ln1_g, ln1_b, sc_f, sh_f],
        [(d, F32), (d, BF)], [(1, d)] * 5, min(256, t))

    dmixin = _matmul(dmix, w_o_f, "nt", "d_mixer_out")
    g_w_o = _matmul(mixin, dmix, "tn", "d_w_o", BF)

    def mixer_bwd(o, z, om, dmi, g):
        _, vjp = jax.vjp(lambda o_, z_, g_: jnp.concatenate(_gdn_out(o_, z_, g_), axis=1), o, z, g)
        do_, dz_, dg_ = vjp(dmi[:, :DN_VW])
        dom = dmi[:, DN_VW:]
        delta = [jnp.broadcast_to(jnp.sum(dom[:, h * V_HEAD:(h + 1) * V_HEAD] * om[:, h * V_HEAD:(h + 1) * V_HEAD], axis=-1, keepdims=True), (o.shape[0], LANE))
                 for h in range(MLA_HEADS)]
        return do_, dz_, dom, jnp.concatenate(delta, axis=1), dg_

    do_dn, dz, do_mla, delta, d_dn_g = _rowwise(
        "mixer_out_bwd", mixer_bwd, [o_dn, (proj, DN_VW, P_Z // DN_VW), o_mla, dmixin], [dn_norm_g],
        [(DN_VW, F32), (DN_VW, BF), (MLA_VW, BF), (MLA_HEADS * LANE, F32)], [(1, DN_DV)], tm)

    dqc, dkc, dvc = _attn_bwd(qc, kc, vc, do_mla, lse, delta, min(512, t), min(1024, t))
    dcq, dckv, dkr, d_q_g, d_kv_g, g_w_uq_t, g_w_ukv_t = _mla_prep_bwd(
        proj, pos_col, inv_freq2, q_norm_g, kv_norm_g, w_uq_t, w_ukv_t, dqc, dkc, dvc, min(256, t))

    token = grads_ready("mixer", g_w_o, g_w_uq_t, g_w_ukv_t)

    d_intra = _gdn_scan_bwd(intra, states, do_dn, gdn_tm)
    dqkv_act, dba, d_al8, d_dt8 = _gdn_intra_bwd(qkv, proj, al8 + token, dt8, inverses, d_intra, min(256, t))
    dqkv_pre, d_conv8 = _conv_bwd(proj, conv_w8, dqkv_act, min(256, t))

    dproj = jnp.concatenate([dqkv_pre, dz, dcq, dckv, dba, dkr], axis=1)
    dh1 = _matmul(dproj, w_in_t, "nn", "d_in_proj")
    g_w_in_t = _matmul(dproj, h1, "tn", "d_w_in", BF)
    token = grads_ready("in", g_w_in_t)

    def modulate_bwd(xx, dh, dxa, sc):
        return dh * (1.0 + sc) + dxa, jnp.sum(dh * xx, axis=0, keepdims=True), jnp.sum(dh, axis=0, keepdims=True)

    grad_x, d_sc_m, d_sh_m = _rowwise("modulate_in_bwd", modulate_bwd, [xs, dh1, dx_a], [sc_m + token], [(d, F32)], [(1, d), (1, d)], tm)
    dmod = jnp.concatenate([d_sh_m, d_sc_m, d_gt_m, d_sh_f, d_sc_f, d_gt_f], axis=1)
    return grad_x, loss_acc, dmod, d_conv8, d_al8, d_dt8, d_dn_g, d_q_g, d_kv_g, d_ln1_g, d_ln1_b, d_ln2_g, d_ln2_b
```

```python
import functools
import math

import jax
import jax.numpy as jnp
from jax import lax
from jax.experimental import pallas as pl
from jax.experimental.pallas import tpu as pltpu

F32 = jnp.float32
BF = jnp.bfloat16
HI = lax.Precision.HIGHEST

N_DEV = 8
DN_HEADS = 4
DN_DK = 128
DN_DV = 128
CONV_K = 4
CHUNK = 64
MLA_HEADS = 4
QK_NOPE = 128
QK_ROPE = 64
V_HEAD = 128
Q_LORA = 512
KV_LORA = 256
ROPE_THETA = 10000.0
DEPTH = 1
DEEPNORM_ALPHA = (2.0 * DEPTH) ** 0.25
LANE = 128
CONV_HALO = 8
GL_ROWS = 8
CONV_ROWS, CONV_COLS = 64, 256

DN_QK = DN_HEADS * DN_DK
DN_VW = DN_HEADS * DN_DV
DN_CONV_CH = 2 * DN_QK + DN_VW
MLA_VW = MLA_HEADS * V_HEAD
P_Z = DN_CONV_CH
P_CQ = P_Z + DN_VW
P_CKV = P_CQ + Q_LORA
P_BA = P_CKV + KV_LORA
P_KR = P_BA + LANE
N_INP = P_KR + LANE

ADAM_LR = 0.001
ADAM_B1 = 0.9
ADAM_B2 = 0.999
ADAM_EPS = 1e-08
ADAM_WD = 0.01
ADAM_STEP = 10

NN = (((1,), (0,)), ((), ()))
NT = (((1,), (1,)), ((), ()))
TN = (((0,), (0,)), ((), ()))


def _pick(n, prefs):
    for p in prefs:
        if n % p == 0:
            return p
    return n


def _full(shape):
    return pl.BlockSpec(shape, lambda *_: (0,) * len(shape))


def _dot(a, b, dims=NN):
    return lax.dot_general(a, b, dims, preferred_element_type=F32)


def _doth(a, b, dims=NN):
    return lax.dot_general(a, b, dims, precision=HI, preferred_element_type=F32)


@jax.custom_vjp
def _mmb(a, b):
    return _dot(a.astype(BF), b.astype(BF), NN)


def _mmb_fwd(a, b):
    return _mmb(a, b), (a, b)


def _mmb_bwd(res, g):
    a, b = res
    gb = g.astype(BF)
    return (_dot(gb, b.astype(BF), NT).astype(a.dtype), _dot(a.astype(BF), gb, TN).astype(b.dtype))


_mmb.defvjp(_mmb_fwd, _mmb_bwd)


@jax.custom_vjp
def _mmb_nt(a, b):
    return _dot(a.astype(BF), b.astype(BF), NT)


def _mmb_nt_fwd(a, b):
    return _mmb_nt(a, b), (a, b)


def _mmb_nt_bwd(res, g):
    a, b = res
    gb = g.astype(BF)
    return (_dot(gb, b.astype(BF), NN).astype(a.dtype), _dot(gb, a.astype(BF), TN).astype(b.dtype))


_mmb_nt.defvjp(_mmb_nt_fwd, _mmb_nt_bwd)


def _sigmoid(x):
    return 0.5 * (jnp.tanh(0.5 * x) + 1.0)


def _silu(x):
    return x * _sigmoid(x)


def _softplus(x):
    return jnp.maximum(x, 0.0) + jnp.log(1.0 + jnp.exp(-jnp.abs(x)))


def _layernorm(x, g, b, eps=1e-5):
    mu = jnp.mean(x, axis=-1, keepdims=True)
    xc = x - mu
    var = jnp.mean(xc * xc, axis=-1, keepdims=True)
    return xc * lax.rsqrt(var + eps) * g + b


def _rmsnorm(x, g, eps=1e-6):
    return x * lax.rsqrt(jnp.mean(x * x, axis=-1, keepdims=True) + eps) * g


def _l2norm(x, eps=1e-6):
    return x * lax.rsqrt(jnp.sum(x * x, axis=-1, keepdims=True) + eps)


def _rowwise(name, fn, rows, vecs, out_rows, out_accs, tm):
    rows = [r if isinstance(r, tuple) else (r, r.shape[1], 0) for r in rows]
    t = rows[0][0].shape[0]
    tm = min(tm, t)
    assert t % tm == 0
    nr, nv, no = len(rows), len(vecs), len(out_rows)

    def body(*refs):
        ins = [r[...] for r in refs[:nr + nv]]
        outs = fn(*ins)
        outs = outs if isinstance(outs, (tuple, list)) else (outs,)
        o_rows = refs[nr + nv:nr + nv + no]
        o_accs = refs[nr + nv + no:]
        for o, val in zip(o_rows, outs[:no]):
            o[...] = val.astype(o.dtype)
        if o_accs:
            @pl.when(pl.program_id(0) == 0)
            def _():
                for o in o_accs:
                    o[...] = jnp.zeros_like(o)
            for o, val in zip(o_accs, outs[no:]):
                o[...] += val

    in_specs = [pl.BlockSpec((tm, w), functools.partial(lambda i, j: (i, j), j=j)) for (_, w, j) in rows]
    in_specs += [_full(v.shape) for v in vecs]
    out_specs = [pl.BlockSpec((tm, w), lambda i: (i, 0)) for (w, _) in out_rows]
    out_specs += [_full(s) for s in out_accs]
    out_shape = [jax.ShapeDtypeStruct((t, w), d) for (w, d) in out_rows]
    out_shape += [jax.ShapeDtypeStruct(s, F32) for s in out_accs]
    res = pl.pallas_call(
        body, grid=(t // tm,), in_specs=in_specs, out_specs=out_specs, out_shape=out_shape, name=name,
        compiler_params=pltpu.CompilerParams(dimension_semantics=("arbitrary",)),
    )(*[r[0] for r in rows], *vecs)
    return res


def _matmul(a, b, mode, name, out_dtype=F32):
    if mode == "nn":
        (m, k), n = a.shape, b.shape[1]
    elif mode == "nt":
        (m, k), n = a.shape, b.shape[0]
    else:
        (k, m), n = a.shape, b.shape[1]
    tm, tn, tk = _matmul_tiles(m, n, k, a.dtype.itemsize, b.dtype.itemsize, jnp.dtype(out_dtype).itemsize)
    nk = k // tk
    dims = {"nn": NN, "nt": NT, "tn": TN}[mode]

    def body(a_ref, b_ref, o_ref, *acc):
        part = _dot(a_ref[...].astype(BF), b_ref[...].astype(BF), dims)
        if nk == 1:
            o_ref[...] = part.astype(o_ref.dtype)
            return
        (acc_ref,) = acc
        kk = pl.program_id(2)

        @pl.when(kk == 0)
        def _():
            acc_ref[...] = part

        @pl.when(kk > 0)
        def _():
            acc_ref[...] += part

        @pl.when(kk == nk - 1)
        def _():
            o_ref[...] = acc_ref[...].astype(o_ref.dtype)

    a_spec = pl.BlockSpec((tk, tm), lambda i, j, kk: (kk, i)) if mode == "tn" else pl.BlockSpec((tm, tk), lambda i, j, kk: (i, kk))
    b_spec = pl.BlockSpec((tn, tk), lambda i, j, kk: (j, kk)) if mode == "nt" else pl.BlockSpec((tk, tn), lambda i, j, kk: (kk, j))
    return pl.pallas_call(
        body, grid=(m // tm, n // tn, nk), in_specs=[a_spec, b_spec],
        out_specs=pl.BlockSpec((tm, tn), lambda i, j, kk: (i, j)),
        out_shape=jax.ShapeDtypeStruct((m, n), out_dtype),
        scratch_shapes=[pltpu.VMEM((tm, tn), F32)] if nk > 1 else [], name=name,
        compiler_params=pltpu.CompilerParams(dimension_semantics=("parallel", "parallel", "arbitrary")),
    )(a, b)


def _lane_tile(n, cap):
    return max([n // s for s in range(1, n // LANE + 1) if n % s == 0 and (n // s) % LANE == 0 and n // s <= cap] or [n])


def _ffn_in(h, w_gate, w_up):
    m, k = h.shape
    f = w_gate.shape[0]
    tm, tn = _pick(m, (512, 256, 128)), _lane_tile(f, 1408)

    def body(h_ref, wg_ref, wu_ref, act_ref, g_ref, u_ref):
        hh = h_ref[...]
        g = _dot(hh, wg_ref[...], NT)
        u = _dot(hh, wu_ref[...], NT)
        act_ref[...] = (_silu(g) * u).astype(act_ref.dtype)
        g_ref[...] = g.astype(g_ref.dtype)
        u_ref[...] = u.astype(u_ref.dtype)

    w_spec = pl.BlockSpec((tn, k), lambda i, j: (j, 0))
    o_spec = pl.BlockSpec((tm, tn), lambda i, j: (i, j))
    return pl.pallas_call(
        body, grid=(m // tm, f // tn), in_specs=[pl.BlockSpec((tm, k), lambda i, j: (i, 0)), w_spec, w_spec],
        out_specs=[o_spec] * 3, out_shape=[jax.ShapeDtypeStruct((m, f), BF)] * 3, name="ffn_in",
        compiler_params=pltpu.CompilerParams(dimension_semantics=("parallel", "parallel")),
    )(h, w_gate, w_up)


def _ffn_act_bwd(dff, w_down, gate, up):
    m, k = dff.shape
    f = w_down.shape[0]
    tm, tn = _pick(m, (512, 256, 128)), _lane_tile(f, 1408)

    def body(d_ref, w_ref, g_ref, u_ref, dg_ref, du_ref):
        da = _dot(d_ref[...], w_ref[...], NT)
        g = g_ref[...].astype(F32)
        sg = _sigmoid(g)
        dg_ref[...] = (da * u_ref[...].astype(F32) * (sg * (1.0 + g * (1.0 - sg)))).astype(dg_ref.dtype)
        du_ref[...] = (da * (g * sg)).astype(du_ref.dtype)

    o_spec = pl.BlockSpec((tm, tn), lambda i, j: (i, j))
    return pl.pallas_call(
        body, grid=(m // tm, f // tn),
        in_specs=[pl.BlockSpec((tm, k), lambda i, j: (i, 0)), pl.BlockSpec((tn, k), lambda i, j: (j, 0)), o_spec, o_spec],
        out_specs=[o_spec] * 2, out_shape=[jax.ShapeDtypeStruct((m, f), BF)] * 2, name="d_ffn_act",
        compiler_params=pltpu.CompilerParams(dimension_semantics=("parallel", "parallel")),
    )(dff, w_down, gate, up)


def _matmul2_nn(a1, b1, a2, b2, name):
    m, k = a1.shape
    n = b1.shape[1]
    tm, tn = _pick(m, (1024, 512, 256, 128)), _pick(n, (512, 256, 128))

    def body(a1_ref, b1_ref, a2_ref, b2_ref, o_ref):
        o_ref[...] = _dot(a1_ref[...], b1_ref[...]) + _dot(a2_ref[...], b2_ref[...])

    a_spec = pl.BlockSpec((tm, k), lambda i, j: (i, 0))
    b_spec = pl.BlockSpec((k, tn), lambda i, j: (0, j))
    return pl.pallas_call(
        body, grid=(m // tm, n // tn), in_specs=[a_spec, b_spec, a_spec, b_spec],
        out_specs=pl.BlockSpec((tm, tn), lambda i, j: (i, j)), out_shape=jax.ShapeDtypeStruct((m, n), F32), name=name,
        compiler_params=pltpu.CompilerParams(dimension_semantics=("parallel", "parallel")),
    )(a1, b1, a2, b2)


MATMUL_VMEM_BUDGET = 28 * 1024 * 1024


def _matmul_tiles(m, n, k, a_bytes, b_bytes, o_bytes):
    def divisors(x, cap):
        return sorted({x // s for s in range(1, 65) if x % s == 0 and (x // s) % LANE == 0 and x // s <= cap}, reverse=True) or [x]

    for tk in divisors(k, k):
        best = None
        for tm in divisors(m, 1024):
            for tn in divisors(n, 2048):
                need = 2 * (tm * tk * a_bytes + tk * tn * b_bytes + tm * tn * o_bytes) + (tm * tn * 4 if tk < k else 0)
                if need <= MATMUL_VMEM_BUDGET and tm * tn >= 512 * 512 and (best is None or tm * tn > best[0] * best[1]):
                    best = (tm, tn)
        if best:
            return best[0], best[1], tk
    return _pick(m, (512, 256, 128)), _pick(n, (512, 256, 128)), _pick(k, (512, 256, 128))


def _exchange(xs, name, scatter):
    n = len(xs)
    npeer = N_DEV - 1

    def body(*refs):
        x_refs, o_refs = refs[:n], refs[n:2 * n]
        send_sems, recv_sems, local_sems = refs[2 * n:]
        mx, my, mc = lax.axis_index("x"), lax.axis_index("y"), lax.axis_index("c")
        me = 4 * mx + 2 * my + mc
        src_me = [x.at[me] if scatter else x for x in x_refs]
        mine = [pltpu.make_async_copy(src_me[a], o_refs[a].at[me], local_sems.at[a]) for a in range(n)]
        for cp in mine:
            cp.start()
        copies = []
        for k in range(1, N_DEV):
            px, py, pc = mx ^ (k >> 2), my ^ ((k >> 1) & 1), mc ^ (k & 1)
            peer = 4 * px + 2 * py + pc
            for a in range(n):
                cp = pltpu.make_async_remote_copy(
                    src_ref=x_refs[a].at[peer] if scatter else x_refs[a], dst_ref=o_refs[a].at[me],
                    send_sem=send_sems.at[a * npeer + k - 1], recv_sem=recv_sems.at[a * npeer + k - 1],
                    device_id=(px, py, pc), device_id_type=pl.DeviceIdType.MESH)
                cp.start()
                copies.append((cp, a, k, peer))
        for cp, a, k, peer in copies:
            pltpu.make_async_remote_copy(
                src_ref=src_me[a], dst_ref=o_refs[a].at[peer], send_sem=send_sems.at[a * npeer + k - 1],
                recv_sem=recv_sems.at[a * npeer + k - 1], device_id=(mx, my, mc),
                device_id_type=pl.DeviceIdType.MESH).wait_recv()
        for cp, _, _, _ in copies:
            cp.wait_send()
        for cp in mine:
            cp.wait()

    return pl.pallas_call(
        body, out_shape=[jax.ShapeDtypeStruct((N_DEV,) + x.shape[-2:], x.dtype) for x in xs],
        in_specs=[pl.BlockSpec(memory_space=pl.ANY)] * n, out_specs=[pl.BlockSpec(memory_space=pl.ANY)] * n,
        scratch_shapes=[pltpu.SemaphoreType.DMA((n * npeer,)), pltpu.SemaphoreType.DMA((n * npeer,)),
                        pltpu.SemaphoreType.DMA((n,))],
        name=name,
    )(*xs)


def _gather_by_chip(xs, name):
    n = len(xs)
    per = N_DEV - 1

    def body(*refs):
        x_refs, o_refs = refs[:n], refs[n:2 * n]
        send_sems, recv_sems, local_sems = refs[2 * n:]
        mx, my, mc = lax.axis_index("x"), lax.axis_index("y"), lax.axis_index("c")
        me, sibling = (mx, my, mc), (mx, my, 1 - mc)
        chips = [(1 - mx, my), (mx, 1 - my), (1 - mx, 1 - my)]
        slot = lambda d: 4 * d[0] + 2 * d[1] + d[2]

        def copy(a, k, block, to, src=None):
            dst = o_refs[a].at[slot(block)]
            return pltpu.make_async_remote_copy(
                src_ref=dst if src is None else src, dst_ref=dst, send_sem=send_sems.at[a * per + k],
                recv_sem=recv_sems.at[a * per + k], device_id=to, device_id_type=pl.DeviceIdType.MESH)

        mine = [pltpu.make_async_copy(x_refs[a], o_refs[a].at[slot(me)], local_sems.at[a]) for a in range(n)]
        for cp in mine:
            cp.start()
        first = []
        for a in range(n):
            first.append(copy(a, 0, me, sibling, src=x_refs[a]))
            first += [copy(a, 1 + j, me, (*chip, mc), src=x_refs[a]) for j, chip in enumerate(chips)]
        for cp in first:
            cp.start()
        passed = []
        for j, chip in enumerate(chips):
            for a in range(n):
                copy(a, 1 + j, (*chip, mc), me).wait_recv()
                cp = copy(a, 4 + j, (*chip, mc), sibling)
                cp.start()
                passed.append(cp)
        for a in range(n):
            copy(a, 0, sibling, me).wait_recv()
            for j, chip in enumerate(chips):
                copy(a, 4 + j, (*chip, 1 - mc), me).wait_recv()
        for cp in first + passed:
            cp.wait_send()
        for cp in mine:
            cp.wait()

    return pl.pallas_call(
        body, out_shape=[jax.ShapeDtypeStruct((N_DEV,) + x.shape, x.dtype) for x in xs],
        in_specs=[pl.BlockSpec(memory_space=pl.ANY)] * n, out_specs=[pl.BlockSpec(memory_space=pl.ANY)] * n,
        scratch_shapes=[pltpu.SemaphoreType.DMA((n * per,)), pltpu.SemaphoreType.DMA((n * per,)),
                        pltpu.SemaphoreType.DMA((n,))],
        name=name,
    )(*xs)


def _peer_of(k):
    mx, my, mc = lax.axis_index("x"), lax.axis_index("y"), lax.axis_index("c")
    px, py, pc = mx ^ (k >> 2), my ^ ((k >> 1) & 1), mc ^ (k & 1)
    return (px, py, pc), 4 * px + 2 * py + pc


def _exchange_start(xs, name, scatter):
    n = len(xs)
    npeer = N_DEV - 1

    def body(*refs):
        x_refs, land_refs = refs[:n], refs[n:2 * n]
        send_sems, recv_sems, token = refs[2 * n], refs[2 * n + 1], refs[-1]
        me = 4 * lax.axis_index("x") + 2 * lax.axis_index("y") + lax.axis_index("c")
        for k in range(1, N_DEV):
            dev, peer = _peer_of(k)
            for a in range(n):
                pltpu.make_async_remote_copy(
                    src_ref=x_refs[a].at[peer] if scatter else x_refs[a], dst_ref=land_refs[a].at[me],
                    send_sem=send_sems.at[a * npeer + k - 1], recv_sem=recv_sems.at[a * npeer + k - 1],
                    device_id=dev, device_id_type=pl.DeviceIdType.MESH).start()
        token[...] = jnp.zeros_like(token)

    hbm = pl.BlockSpec(memory_space=pltpu.HBM)
    sem = pl.BlockSpec(memory_space=pltpu.SEMAPHORE)
    lands = [pltpu.with_memory_space_constraint(lax.empty((N_DEV,) + x.shape[-2:], x.dtype), pltpu.HBM) for x in xs]
    srcs = [pltpu.with_memory_space_constraint(x, pltpu.HBM) for x in xs]
    outs = pl.pallas_call(
        body, name=name,
        out_shape=(pltpu.SemaphoreType.DMA((n * npeer,)), pltpu.SemaphoreType.DMA((n * npeer,)),
                   *[pltpu.HBM(x.shape, x.dtype) for x in srcs], *[pltpu.HBM(z.shape, z.dtype) for z in lands],
                   jax.ShapeDtypeStruct((8, LANE), F32)),
        in_specs=[hbm] * (2 * n), out_specs=(sem, sem, *[hbm] * (2 * n), pl.BlockSpec(memory_space=pltpu.VMEM)),
        input_output_aliases={i: 2 + i for i in range(2 * n)},
        compiler_params=pltpu.CompilerParams(has_side_effects=pltpu.SideEffectType.DATAFLOW_SIDE_EFFECTING),
    )(*srcs, *lands)
    return (outs[0], outs[1], list(outs[2:2 + n]), list(outs[2 + n:2 + 2 * n])), outs[-1][0:1, 0:1]


def _exchange_wait(started, after, name, scatter):
    send_sems, recv_sems, srcs, lands = started
    n = len(srcs)
    npeer = N_DEV - 1

    def body(*refs):
        x_refs, land_refs = refs[:n], refs[n:2 * n]
        send_sems, recv_sems = refs[2 * n], refs[2 * n + 1]
        mx, my, mc = lax.axis_index("x"), lax.axis_index("y"), lax.axis_index("c")
        me = 4 * mx + 2 * my + mc
        for k in range(1, N_DEV):
            _, peer = _peer_of(k)
            for a in range(n):
                src = x_refs[a].at[me] if scatter else x_refs[a]
                cp = pltpu.make_async_remote_copy(
                    src_ref=src, dst_ref=land_refs[a].at[peer], send_sem=send_sems.at[a * npeer + k - 1],
                    recv_sem=recv_sems.at[a * npeer + k - 1], device_id=(mx, my, mc), device_id_type=pl.DeviceIdType.MESH)
                cp.wait_send()
                cp.wait_recv()

    hbm = pl.BlockSpec(memory_space=pltpu.HBM)
    sem = pl.BlockSpec(memory_space=pltpu.SEMAPHORE)
    outs = pl.pallas_call(
        body, name=name,
        out_shape=(*[pltpu.HBM(x.shape, x.dtype) for x in srcs], *[pltpu.HBM(z.shape, z.dtype) for z in lands]),
        in_specs=[hbm] * (2 * n) + [sem, sem, pl.BlockSpec(memory_space=pl.ANY)], out_specs=tuple([hbm] * (2 * n)),
        input_output_aliases={i: i for i in range(2 * n)},
        compiler_params=pltpu.CompilerParams(has_side_effects=pltpu.SideEffectType.DATAFLOW_SIDE_EFFECTING),
    )(*srcs, *lands, send_sems, recv_sems, after)
    me = 4 * lax.axis_index("x") + 2 * lax.axis_index("y") + lax.axis_index("c")
    full = []
    for x, land in zip(outs[:n], outs[n:]):
        own = lax.dynamic_slice(x, (me, 0, 0), (1,) + x.shape[1:]) if scatter else x[None]
        full.append(lax.dynamic_update_slice(land, own, (me, 0, 0)))
    return full


def _sum_slots(x, name):
    _, r, c = x.shape
    tr = _pick(r, (512, 256, 128, 64, 32, 16))

    def body(x_ref, o_ref):
        acc = x_ref[0].astype(F32)
        for s in range(1, N_DEV):
            acc = acc + x_ref[s].astype(F32)
        o_ref[...] = acc

    return pl.pallas_call(
        body, grid=(r // tr,), in_specs=[pl.BlockSpec((N_DEV, tr, c), lambda i: (0, i, 0))],
        out_specs=pl.BlockSpec((tr, c), lambda i: (i, 0)), out_shape=jax.ShapeDtypeStruct((r, c), F32), name=name,
        compiler_params=pltpu.CompilerParams(dimension_semantics=("arbitrary",)),
    )(x)


def _mod_fwd(c_all, w_ada, b_ada_mine):
    def body(c_ref, w_ref, b_ref, o_ref):
        o_ref[...] = _doth(_silu(c_ref[...]), w_ref[...]) + b_ref[...]

    return pl.pallas_call(body, out_shape=jax.ShapeDtypeStruct((c_all.shape[0], w_ada.shape[1]), F32), name="mod_fwd")(c_all, w_ada, b_ada_mine)


def _mod_bwd(c_all_t, dmod_mine):
    def body(ct_ref, d_ref, o_ref):
        s = _silu(ct_ref[...])
        acc = s[:, 0:1] * d_ref[pl.ds(0, 1), :]
        for b in range(1, N_DEV):
            acc = acc + s[:, b:b + 1] * d_ref[pl.ds(b, 1), :]
        o_ref[...] = acc

    return pl.pallas_call(body, out_shape=jax.ShapeDtypeStruct((c_all_t.shape[0], dmod_mine.shape[1]), F32), name="mod_bwd")(c_all_t, dmod_mine)


def _conv_fwd(proj, conv_w8, tm):
    t = proj.shape[0]
    ch = DN_CONV_CH

    def body(x_ref, w_ref, o_ref, buf):
        @pl.when(pl.program_id(0) == 0)
        def _():
            buf[pl.ds(0, CONV_HALO), :] = jnp.zeros((CONV_HALO, ch), F32)

        buf[pl.ds(CONV_HALO, tm), :] = x_ref[...]
        for c0 in range(0, ch, CONV_COLS):
            cols = pl.ds(c0, CONV_COLS)
            w = [w_ref[pl.ds(j, 1), cols] for j in range(CONV_K)]
            for r0 in range(0, tm, CONV_ROWS):
                acc = buf[pl.ds(r0 + CONV_HALO - (CONV_K - 1), CONV_ROWS), cols] * w[0]
                for j in range(1, CONV_K):
                    acc = acc + buf[pl.ds(r0 + CONV_HALO - (CONV_K - 1) + j, CONV_ROWS), cols] * w[j]
                o_ref[pl.ds(r0, CONV_ROWS), cols] = _silu(acc)
        buf[pl.ds(0, CONV_HALO), :] = buf[pl.ds(tm, CONV_HALO), :]

    return pl.pallas_call(
        body, grid=(t // tm,), in_specs=[pl.BlockSpec((tm, ch), lambda i: (i, 0)), _full(conv_w8.shape)],
        out_specs=pl.BlockSpec((tm, ch), lambda i: (i, 0)), out_shape=jax.ShapeDtypeStruct((t, ch), F32),
        scratch_shapes=[pltpu.VMEM((tm + CONV_HALO, ch), F32)], name="conv_fwd",
        compiler_params=pltpu.CompilerParams(dimension_semantics=("arbitrary",)),
    )(proj, conv_w8)


def _conv_bwd(proj, conv_w8, dact, tm):
    t = proj.shape[0]
    ch = DN_CONV_CH
    nt = t // tm
    hb = tm // CONV_HALO

    def body(x_ref, xp_ref, w_ref, dy_ref, dx_ref, dw_ref, xbuf, dbuf):
        step = pl.program_id(0)

        @pl.when(step == 0)
        def _():
            dbuf[pl.ds(tm, CONV_HALO), :] = jnp.zeros((CONV_HALO, ch), F32)
            dw_ref[...] = jnp.zeros_like(dw_ref)

        first = step == nt - 1
        xbuf[pl.ds(0, CONV_HALO), :] = jnp.where(first, 0.0, xp_ref[...])
        xbuf[pl.ds(CONV_HALO, tm), :] = x_ref[...]
        for c0 in range(0, ch, CONV_COLS):
            cols = pl.ds(c0, CONV_COLS)
            w = [w_ref[pl.ds(j, 1), cols] for j in range(CONV_K)]
            dw = [jnp.zeros((1, CONV_COLS), F32) for _ in range(CONV_K)]
            for r0 in range(0, tm, CONV_ROWS):
                xs = [xbuf[pl.ds(r0 + CONV_HALO - (CONV_K - 1) + j, CONV_ROWS), cols] for j in range(CONV_K)]
                pre = xs[0] * w[0]
                for j in range(1, CONV_K):
                    pre = pre + xs[j] * w[j]
                sg = _sigmoid(pre)
                dpre = dy_ref[pl.ds(r0, CONV_ROWS), cols] * (sg * (1.0 + pre * (1.0 - sg)))
                dbuf[pl.ds(r0, CONV_ROWS), cols] = dpre
                dw = [dw[j] + jnp.sum(dpre * xs[j], axis=0, keepdims=True) for j in range(CONV_K)]
            for j in range(CONV_K):
                dw_ref[pl.ds(j, 1), cols] += dw[j]
            for r0 in range(0, tm, CONV_ROWS):
                dx = dbuf[pl.ds(r0 + CONV_K - 1, CONV_ROWS), cols] * w[0]
                for j in range(1, CONV_K):
                    dx = dx + dbuf[pl.ds(r0 + CONV_K - 1 - j, CONV_ROWS), cols] * w[j]
                dx_ref[pl.ds(r0, CONV_ROWS), cols] = dx.astype(dx_ref.dtype)
        dbuf[pl.ds(tm, CONV_HALO), :] = dbuf[pl.ds(0, CONV_HALO), :]

    rev = lambda i: (nt - 1 - i, 0)
    prev = lambda i: (jnp.maximum((nt - 1 - i) * hb - 1, 0), 0)
    return pl.pallas_call(
        body, grid=(nt,),
        in_specs=[pl.BlockSpec((tm, ch), rev), pl.BlockSpec((CONV_HALO, ch), prev), _full(conv_w8.shape),
                  pl.BlockSpec((tm, ch), rev)],
        out_specs=[pl.BlockSpec((tm, ch), rev), _full(conv_w8.shape)],
        out_shape=[jax.ShapeDtypeStruct((t, ch), BF), jax.ShapeDtypeStruct(conv_w8.shape, F32)],
        scratch_shapes=[pltpu.VMEM((tm + CONV_HALO, ch), F32), pltpu.VMEM((tm + CONV_HALO, ch), F32)], name="conv_bwd",
        compiler_params=pltpu.CompilerParams(dimension_semantics=("arbitrary",)),
    )(proj, proj, conv_w8, dact)


BNN = (((2,), (1,)), ((0,), (0,)))
BNT = (((2,), (2,)), ((0,), (0,)))
BTN = (((1,), (1,)), ((0,), (0,)))


def _bdot(a, b, dims, precision=None):
    return lax.dot_general(a, b, dims, precision=precision, preferred_element_type=F32)


@jax.custom_vjp
def _bmmb_nt(a, b):
    return _bdot(a.astype(BF), b.astype(BF), BNT)


def _bmmb_nt_fwd(a, b):
    return _bmmb_nt(a, b), (a, b)


def _bmmb_nt_bwd(res, g):
    a, b = res
    gb = g.astype(BF)
    return _bdot(gb, b.astype(BF), BNN), _bdot(gb, a.astype(BF), BTN)


_bmmb_nt.defvjp(_bmmb_nt_fwd, _bmmb_nt_bwd)


@jax.custom_vjp
def _bmmb(a, b):
    return _bdot(a.astype(BF), b.astype(BF), BNN)


def _bmmb_fwd(a, b):
    return _bmmb(a, b), (a, b)


def _bmmb_bwd(res, g):
    a, b = res
    gb = g.astype(BF)
    return _bdot(gb, b.astype(BF), BNT), _bdot(a.astype(BF), gb, BTN)


_bmmb.defvjp(_bmmb_fwd, _bmmb_bwd)


@jax.custom_vjp
def _bmmb_tn(a, b):
    return _bdot(a.astype(BF), b.astype(BF), BTN)


def _bmmb_tn_fwd(a, b):
    return _bmmb_tn(a, b), (a, b)


def _bmmb_tn_bwd(res, g):
    a, b = res
    gb = g.astype(BF)
    return _bdot(b.astype(BF), gb, BNT), _bdot(a.astype(BF), gb, BNN)


_bmmb_tn.defvjp(_bmmb_tn_fwd, _bmmb_tn_bwd)


def _unit_lower_solve_fwd(a, r):
    c = a.shape[-1]
    ri = lax.broadcasted_iota(jnp.int32, a.shape, 1)
    ci = lax.broadcasted_iota(jnp.int32, a.shape, 2)
    xm = -a
    inv = (ri == ci).astype(F32) + xm
    for _ in range(int(math.log2(c)) - 1):
        xm = _bdot(xm, xm, BNN, HI)
        inv = inv + _bdot(inv, xm, BNN, HI)
    x = _bdot(inv, r, BNN, HI)
    return x, (inv, x)


def _unit_lower_solve_bwd(res, g):
    inv, x = res
    dr = _bdot(inv, g, BTN, HI)
    return -_bdot(dr, x, BNT, HI), dr


@jax.custom_vjp
def _unit_lower_solve_given(a, r, inv):
    return _bdot(inv, r, BNN, HI)


def _unit_lower_solve_given_fwd(a, r, inv):
    x = _bdot(inv, r, BNN, HI)
    return x, (inv, x)


def _unit_lower_solve_given_bwd(res, g):
    da, dr = _unit_lower_solve_bwd(res, g)
    return da, dr, jnp.zeros_like(res[0])


_unit_lower_solve_given.defvjp(_unit_lower_solve_given_fwd, _unit_lower_solve_given_bwd)


def _gdn_intra(qkv, ba, al8, dt8, inv4=None):
    tm = qkv.shape[0]
    nb = tm // CHUNK
    bsz = DN_HEADS * nb

    def heads(x0):
        return jnp.concatenate([qkv[:, x0 + h * LANE:x0 + (h + 1) * LANE].reshape(nb, CHUNK, LANE) for h in range(DN_HEADS)], axis=0)

    def spread(c0):
        return jnp.concatenate([jnp.broadcast_to(ba[:, c0 + h:c0 + h + 1], (tm, LANE)).reshape(nb, CHUNK, LANE)
                                for h in range(DN_HEADS)], axis=0)

    def per_head(v8):
        return jnp.concatenate([jnp.broadcast_to(v8[0:1, h:h + 1].reshape(1, 1, 1), (nb, 1, LANE)) for h in range(DN_HEADS)], axis=0)

    ri = lax.broadcasted_iota(jnp.int32, (bsz, CHUNK, CHUNK), 1)
    ci = lax.broadcasted_iota(jnp.int32, (bsz, CHUNK, CHUNK), 2)
    incl = ri >= ci
    strict = ri > ci

    q = _l2norm(heads(0)) * (DN_DK ** -0.5)
    k = _l2norm(heads(DN_QK))
    va = heads(2 * DN_QK)
    beta = _sigmoid(spread(0))
    g = -jnp.exp(per_head(al8)) * _softplus(spread(DN_HEADS) + per_head(dt8))
    gc = _bdot(incl.astype(F32), g, BNN, HI)
    g_last = jnp.sum(g, axis=1, keepdims=True)
    gcol = gc[:, :, :CHUNK]
    diff = gcol - jnp.swapaxes(gcol, 1, 2)
    decay = jnp.where(incl, jnp.exp(jnp.where(incl, diff, 0.0)), 0.0)
    kb = k * beta
    a_mat = jnp.where(strict, _bmmb_nt(kb, k) * decay, 0.0)
    egc = jnp.exp(gc)
    rhs = jnp.concatenate([kb * egc, va * beta], axis=2)
    if inv4 is None:
        wu, (inv, _) = _unit_lower_solve_fwd(a_mat, rhs)
    else:
        wu = _unit_lower_solve_given(a_mat, rhs, inv4.reshape(bsz, CHUNK, CHUNK))
    attn = jnp.where(incl, _bmmb_nt(q, k) * decay, 0.0)

    def unheads(x):
        return jnp.concatenate([x[h * nb:(h + 1) * nb].reshape(tm, LANE) for h in range(DN_HEADS)], axis=1)

    w_c, u_c = wu[:, :, :DN_DK], wu[:, :, DN_DK:]
    kd = k * jnp.exp(g_last - gc)
    out = (unheads(q * egc - _bmmb(attn, w_c)), unheads(_bmmb(attn, u_c)),
           _bmmb_tn(kd, w_c).reshape(DN_HEADS, nb, DN_DK, DN_DK), _bmmb_tn(kd, u_c).reshape(DN_HEADS, nb, DN_DK, DN_DV),
           jnp.broadcast_to(g_last, (bsz, GL_ROWS, LANE)).reshape(DN_HEADS, nb, GL_ROWS, LANE))
    return out if inv4 is not None else out + (inv.reshape(DN_HEADS, nb, CHUNK, CHUNK),)


def _gdn_scan_step(qp, op, c_mat, n_mat, gl, s):
    return _mmb(qp, s) + op, s * jnp.exp(gl) - _mmb(c_mat, s) + n_mat


def _gdn_intra_specs(t, tm, dts, order=lambda i: i):
    nb = tm // CHUNK
    row = pl.BlockSpec((tm, DN_VW), lambda i: (order(i), 0))
    mat = pl.BlockSpec((DN_HEADS, nb, DN_DK, DN_DV), lambda i: (0, order(i), 0, 0))
    row_shape = lambda d: jax.ShapeDtypeStruct((t, DN_VW), d)
    mat_shape = lambda d: jax.ShapeDtypeStruct((DN_HEADS, t // CHUNK, DN_DK, DN_DV), d)
    gl = pl.BlockSpec((DN_HEADS, nb, GL_ROWS, LANE), lambda i: (0, order(i), 0, 0))
    gl_shape = jax.ShapeDtypeStruct((DN_HEADS, t // CHUNK, GL_ROWS, LANE), dts[4])
    return [row, row, mat, mat, gl], [row_shape(dts[0]), row_shape(dts[1]), mat_shape(dts[2]), mat_shape(dts[3]), gl_shape]


def _gdn_intra_fwd(qkv, proj, al8, dt8, tm):
    t = qkv.shape[0]

    def body(qkv_ref, ba_ref, al_ref, dt_ref, *outs):
        for o, val in zip(outs, _gdn_intra(qkv_ref[...], ba_ref[...], al_ref[...], dt_ref[...])):
            o[...] = val.astype(o.dtype)

    specs, shapes = _gdn_intra_specs(t, tm, (BF, F32, BF, BF, F32))
    specs.append(_gdn_inverse_spec(tm))
    shapes.append(jax.ShapeDtypeStruct((DN_HEADS, t // CHUNK, CHUNK, CHUNK), F32))
    res = pl.pallas_call(
        body, grid=(t // tm,),
        in_specs=[pl.BlockSpec((tm, DN_CONV_CH), lambda i: (i, 0)), pl.BlockSpec((tm, LANE), lambda i: (i, P_BA // LANE)),
                  _full(al8.shape), _full(dt8.shape)],
        out_specs=specs, out_shape=shapes, name="gdn_intra_fwd",
        compiler_params=pltpu.CompilerParams(dimension_semantics=("parallel",)),
    )(qkv, proj, al8, dt8)
    return res[:5], res[5]


def _gdn_inverse_spec(tm):
    return pl.BlockSpec((DN_HEADS, tm // CHUNK, CHUNK, CHUNK), lambda i: (0, i, 0, 0))


def _gdn_intra_bwd(qkv, proj, al8, dt8, inverses, cts, tm):
    t = qkv.shape[0]

    def body(qkv_ref, ba_ref, al_ref, dt_ref, inv_ref, *refs):
        ct_refs, (dqkv_ref, dba_ref, dal_ref, ddt_ref) = refs[:5], refs[5:]

        @pl.when(pl.program_id(0) == 0)
        def _():
            dal_ref[...] = jnp.zeros_like(dal_ref)
            ddt_ref[...] = jnp.zeros_like(ddt_ref)

        _, vjp = jax.vjp(functools.partial(_gdn_intra, inv4=inv_ref[...]), qkv_ref[...], ba_ref[...], al_ref[...], dt_ref[...])
        dqkv, dba, dal, ddt = vjp(tuple(r[...].astype(F32) for r in ct_refs))
        dqkv_ref[...] = dqkv
        dba_ref[...] = dba.astype(dba_ref.dtype)
        dal_ref[...] += dal
        ddt_ref[...] += ddt

    specs, _ = _gdn_intra_specs(t, tm, (F32,) * 5)
    return pl.pallas_call(
        body, grid=(t // tm,),
        in_specs=[pl.BlockSpec((tm, DN_CONV_CH), lambda i: (i, 0)), pl.BlockSpec((tm, LANE), lambda i: (i, P_BA // LANE)),
                  _full(al8.shape), _full(dt8.shape), _gdn_inverse_spec(tm)] + specs,
        out_specs=[pl.BlockSpec((tm, DN_CONV_CH), lambda i: (i, 0)), pl.BlockSpec((tm, LANE), lambda i: (i, 0)),
                   _full(al8.shape), _full(dt8.shape)],
        out_shape=[jax.ShapeDtypeStruct((t, DN_CONV_CH), F32), jax.ShapeDtypeStruct((t, LANE), BF),
                   jax.ShapeDtypeStruct(al8.shape, F32), jax.ShapeDtypeStruct(dt8.shape, F32)],
        name="gdn_intra_bwd", compiler_params=pltpu.CompilerParams(dimension_semantics=("arbitrary",)),
    )(qkv, proj, al8, dt8, inverses, *cts)


def _gdn_scan_fwd(intra, tm):
    t = intra[0].shape[0]
    nb = tm // CHUNK
    nc = t // CHUNK

    def body(qp_ref, op_ref, c_ref, n_ref, gl_ref, o_ref, ss_ref, s_scr):
        @pl.when(pl.program_id(0) == 0)
        def _():
            s_scr[...] = jnp.zeros_like(s_scr)

        state = [s_scr[h] for h in range(DN_HEADS)]
        for cc in range(nb):
            rows = pl.ds(cc * CHUNK, CHUNK)
            for h in range(DN_HEADS):
                cols = pl.ds(h * DN_DV, DN_DV)
                ss_ref[cc, h] = state[h]
                o_ref[rows, cols], state[h] = _gdn_scan_step(
                    qp_ref[rows, cols], op_ref[rows, cols], c_ref[h, cc], n_ref[h, cc], gl_ref[h, cc, pl.ds(0, 1), :], state[h])
        for h in range(DN_HEADS):
            s_scr[h] = state[h]

    specs, _ = _gdn_intra_specs(t, tm, (F32,) * 5)
    return pl.pallas_call(
        body, grid=(t // tm,), in_specs=specs,
        out_specs=[pl.BlockSpec((tm, DN_VW), lambda i: (i, 0)),
                   pl.BlockSpec((nb, DN_HEADS, DN_DK, DN_DV), lambda i: (i, 0, 0, 0))],
        out_shape=[jax.ShapeDtypeStruct((t, DN_VW), F32), jax.ShapeDtypeStruct((nc, DN_HEADS, DN_DK, DN_DV), F32)],
        scratch_shapes=[pltpu.VMEM((DN_HEADS, DN_DK, DN_DV), F32)], name="gdn_scan_fwd",
        compiler_params=pltpu.CompilerParams(dimension_semantics=("arbitrary",)),
    )(*intra)


def _gdn_scan_bwd(intra, states, do, tm):
    t = intra[0].shape[0]
    nb = tm // CHUNK
    ng = t // tm

    def body(qp_ref, op_ref, c_ref, n_ref, gl_ref, ss_ref, do_ref, dqp_ref, dop_ref, dc_ref, dn_ref, dgl_ref, ds_scr):
        @pl.when(pl.program_id(0) == 0)
        def _():
            ds_scr[...] = jnp.zeros_like(ds_scr)

        d_state = [ds_scr[h] for h in range(DN_HEADS)]
        for cc in reversed(range(nb)):
            rows = pl.ds(cc * CHUNK, CHUNK)
            for h in range(DN_HEADS):
                cols = pl.ds(h * DN_DV, DN_DV)
                _, vjp = jax.vjp(_gdn_scan_step, qp_ref[rows, cols].astype(F32), op_ref[rows, cols], c_ref[h, cc].astype(F32),
                                 n_ref[h, cc].astype(F32), gl_ref[h, cc, pl.ds(0, 1), :], ss_ref[cc, h])
                dqp_ref[rows, cols], dop_ref[rows, cols], dc, dn, dgl, d_state[h] = vjp((do_ref[rows, cols], d_state[h]))
                dc_ref[h, cc] = dc.astype(dc_ref.dtype)
                dn_ref[h, cc] = dn.astype(dn_ref.dtype)
                first_row = lax.broadcasted_iota(jnp.int32, (GL_ROWS, LANE), 0) == 0
                dgl_ref[h, cc] = jnp.where(first_row, dgl, 0.0)
        for h in range(DN_HEADS):
            ds_scr[h] = d_state[h]

    five, shapes = _gdn_intra_specs(t, tm, (F32, F32, BF, BF, F32), order=lambda i: ng - 1 - i)
    row = five[0]
    return pl.pallas_call(
        body, grid=(ng,),
        in_specs=five + [pl.BlockSpec((nb, DN_HEADS, DN_DK, DN_DV), lambda i: (ng - 1 - i, 0, 0, 0)), row],
        out_specs=five, out_shape=shapes,
        scratch_shapes=[pltpu.VMEM((DN_HEADS, DN_DK, DN_DV), F32)], name="gdn_scan_bwd",
        compiler_params=pltpu.CompilerParams(dimension_semantics=("arbitrary",)),
    )(*intra, states, do)


def _gdn_out(o, z, g):
    parts = []
    for h in range(DN_HEADS):
        sl = slice(h * DN_DV, (h + 1) * DN_DV)
        parts.append(_rmsnorm(o[:, sl], g) * _silu(z[:, sl]))
    return parts


_Q_SCALE = math.log2(math.e) / math.sqrt(QK_NOPE + QK_ROPE)


def _rope_tables(pos, inv_freq2):
    lane = lax.broadcasted_iota(jnp.int32, (1, LANE), 1)
    ang = pos * inv_freq2
    cos = jnp.where(lane < QK_ROPE, jnp.cos(ang), 0.0)
    sin = jnp.where(lane < QK_ROPE // 2, -jnp.sin(ang), jnp.where(lane < QK_ROPE, jnp.sin(ang), 0.0))
    return cos, sin


def _rope_swap():
    ri = lax.broadcasted_iota(jnp.int32, (LANE, LANE), 0)
    ci = lax.broadcasted_iota(jnp.int32, (LANE, LANE), 1)
    half = QK_ROPE // 2
    return (((ci < half) & (ri == ci + half)) | ((ci >= half) & (ci < QK_ROPE) & (ri == ci - half))).astype(F32)


def _mla_prep(cq, ckv, kr, gq, gkv, w_uq, w_ukv, cos, sin, swap):
    rope = lambda u: u * cos + _doth(u, swap) * sin
    q_lin = _mmb_nt(_rmsnorm(cq, gq), w_uq) * _Q_SCALE
    kv_lin = _mmb_nt(_rmsnorm(ckv, gkv), w_ukv)
    k_rope = rope(kr)
    qs, ks, vs = [], [], []
    for h in range(MLA_HEADS):
        qs += [q_lin[:, h * LANE:(h + 1) * LANE], rope(q_lin[:, (MLA_HEADS + h) * LANE:(MLA_HEADS + h + 1) * LANE])]
        ks += [kv_lin[:, 2 * h * LANE:(2 * h + 1) * LANE], k_rope]
        vs += [kv_lin[:, (2 * h + 1) * LANE:(2 * h + 2) * LANE]]
    return qs + ks + vs


def _mla_prep_fwd(proj, pos_col, inv_freq2, gq, gkv, w_uq, w_ukv, tm):
    t = proj.shape[0]
    nq = 2 * MLA_HEADS

    def body(cq_ref, ckv_ref, kr_ref, pos_ref, f_ref, gq_ref, gkv_ref, wq_ref, wkv_ref, q_ref, k_ref, v_ref):
        cos, sin = _rope_tables(pos_ref[...], f_ref[...])
        outs = _mla_prep(cq_ref[...], ckv_ref[...], kr_ref[...], gq_ref[...], gkv_ref[...], wq_ref[...], wkv_ref[...],
                         cos, sin, _rope_swap())
        for i in range(nq):
            q_ref[:, pl.ds(i * LANE, LANE)] = outs[i].astype(q_ref.dtype)
            k_ref[:, pl.ds(i * LANE, LANE)] = outs[nq + i].astype(k_ref.dtype)
        for h in range(MLA_HEADS):
            v_ref[:, pl.ds(h * LANE, LANE)] = outs[2 * nq + h].astype(v_ref.dtype)

    row = lambda w, j: pl.BlockSpec((tm, w), functools.partial(lambda i, j: (i, j), j=j))
    return pl.pallas_call(
        body, grid=(t // tm,),
        in_specs=[row(Q_LORA, P_CQ // Q_LORA), row(KV_LORA, P_CKV // KV_LORA), row(LANE, P_KR // LANE),
                  pl.BlockSpec((tm, 1), lambda i: (i, 0)), _full(inv_freq2.shape), _full(gq.shape), _full(gkv.shape),
                  _full(w_uq.shape), _full(w_ukv.shape)],
        out_specs=[row(nq * LANE, 0), row(nq * LANE, 0), row(MLA_VW, 0)],
        out_shape=[jax.ShapeDtypeStruct((t, nq * LANE), BF), jax.ShapeDtypeStruct((t, nq * LANE), BF),
                   jax.ShapeDtypeStruct((t, MLA_VW), BF)],
        name="mla_prep_fwd", compiler_params=pltpu.CompilerParams(dimension_semantics=("arbitrary",)),
    )(proj, proj, proj, pos_col, inv_freq2, gq, gkv, w_uq, w_ukv)


def _mla_prep_bwd(proj, pos_col, inv_freq2, gq, gkv, w_uq, w_ukv, dq, dk, dv, tm):
    t = proj.shape[0]
    nq = 2 * MLA_HEADS

    def body(cq_ref, ckv_ref, kr_ref, pos_ref, f_ref, gq_ref, gkv_ref, wq_ref, wkv_ref, dq_ref, dk_ref, dv_ref,
             dcq_ref, dckv_ref, dkr_ref, dgq_ref, dgkv_ref, dwq_ref, dwkv_ref):
        @pl.when(pl.program_id(0) == 0)
        def _():
            for o in (dgq_ref, dgkv_ref, dwq_ref, dwkv_ref):
                o[...] = jnp.zeros_like(o)

        cos, sin = _rope_tables(pos_ref[...], f_ref[...])
        f = functools.partial(_mla_prep, cos=cos, sin=sin, swap=_rope_swap())
        _, vjp = jax.vjp(f, cq_ref[...], ckv_ref[...], kr_ref[...], gq_ref[...], gkv_ref[...], wq_ref[...], wkv_ref[...])
        cts = [dq_ref[:, pl.ds(i * LANE, LANE)] for i in range(nq)]
        cts += [dk_ref[:, pl.ds(i * LANE, LANE)] for i in range(nq)]
        cts += [dv_ref[:, pl.ds(h * LANE, LANE)] for h in range(MLA_HEADS)]
        dcq, dckv, dkr, dgq, dgkv, dwq, dwkv = vjp(cts)
        dcq_ref[...] = dcq.astype(dcq_ref.dtype)
        dckv_ref[...] = dckv.astype(dckv_ref.dtype)
        dkr_ref[...] = dkr.astype(dkr_ref.dtype)
        dgq_ref[...] += dgq
        dgkv_ref[...] += dgkv
        dwq_ref[...] += dwq
        dwkv_ref[...] += dwkv

    row = lambda w, j: pl.BlockSpec((tm, w), functools.partial(lambda i, j: (i, j), j=j))
    return pl.pallas_call(
        body, grid=(t // tm,),
        in_specs=[row(Q_LORA, P_CQ // Q_LORA), row(KV_LORA, P_CKV // KV_LORA), row(LANE, P_KR // LANE),
                  pl.BlockSpec((tm, 1), lambda i: (i, 0)), _full(inv_freq2.shape), _full(gq.shape), _full(gkv.shape),
                  _full(w_uq.shape), _full(w_ukv.shape), row(nq * LANE, 0), row(nq * LANE, 0), row(MLA_VW, 0)],
        out_specs=[row(Q_LORA, 0), row(KV_LORA, 0), row(LANE, 0), _full(gq.shape), _full(gkv.shape),
                   _full(w_uq.shape), _full(w_ukv.shape)],
        out_shape=[jax.ShapeDtypeStruct((t, Q_LORA), BF), jax.ShapeDtypeStruct((t, KV_LORA), BF),
                   jax.ShapeDtypeStruct((t, LANE), BF), jax.ShapeDtypeStruct(gq.shape, F32),
                   jax.ShapeDtypeStruct(gkv.shape, F32), jax.ShapeDtypeStruct(w_uq.shape, F32),
                   jax.ShapeDtypeStruct(w_ukv.shape, F32)],
        name="mla_prep_bwd", compiler_params=pltpu.CompilerParams(dimension_semantics=("arbitrary",)),
    )(proj, proj, proj, pos_col, inv_freq2, gq, gkv, w_uq, w_ukv, dq, dk, dv)


_NEG = -1e30
_LN2 = math.log(2.0)
ATT_CHAINS = 2


def _causal(tq, tk, q0, k0):
    row = q0 + lax.broadcasted_iota(jnp.int32, (tq, tk), 0)
    col = k0 + lax.broadcasted_iota(jnp.int32, (tq, tk), 1)
    return col <= row


def _attn_fwd(q, k, v, tq, tk):
    t = q.shape[0]

    assert tk % tq == 0 or tq % tk == 0
    n_diag = max(1, tq // tk)

    th = tq // ATT_CHAINS

    def body(q_ref, k_ref, v_ref, o_ref, lse_ref):
        i = pl.program_id(1)
        n_full = (i * tq) // tk

        def step(k0, carry, masked):
            kt = k_ref[pl.ds(k0, tk), :]
            vt = v_ref[pl.ds(k0, tk), :]
            out = []
            for c, (m, l, acc) in enumerate(carry):
                s = _dot(q_ref[pl.ds(c * th, th), :], kt, NT)
                if masked:
                    s = jnp.where(_causal(th, tk, i * tq + c * th, k0), s, _NEG)
                m_new = jnp.maximum(m, jnp.max(s, axis=-1, keepdims=True))
                p = jnp.exp2(s - m_new)
                alpha = jnp.exp2(m - m_new)
                out.append((m_new, alpha * l + jnp.sum(p, axis=-1, keepdims=True), alpha * acc + _dot(p.astype(BF), vt)))
            return tuple(out)

        init = tuple((jnp.full((th, 1), _NEG, F32), jnp.zeros((th, 1), F32), jnp.zeros((th, V_HEAD), F32)) for _ in range(ATT_CHAINS))
        carry = lax.fori_loop(0, n_full, lambda j, c: step(pl.multiple_of(j * tk, tk), c, False), init)
        for dd in range(n_diag):
            carry = step(pl.multiple_of((n_full + dd) * tk, tk), carry, True)
        for c, (m, l, acc) in enumerate(carry):
            o_ref[pl.ds(c * th, th), :] = acc / l
            lse_ref[pl.ds(c * th, th), :] = jnp.broadcast_to(m + jnp.log2(l), (th, LANE))

    return pl.pallas_call(
        body, grid=(MLA_HEADS, t // tq),
        in_specs=[pl.BlockSpec((tq, 2 * LANE), lambda h, i: (i, h)), pl.BlockSpec((t, 2 * LANE), lambda h, i: (0, h)),
                  pl.BlockSpec((t, V_HEAD), lambda h, i: (0, h))],
        out_specs=[pl.BlockSpec((tq, V_HEAD), lambda h, i: (i, h)), pl.BlockSpec((tq, LANE), lambda h, i: (i, h))],
        out_shape=[jax.ShapeDtypeStruct((t, MLA_VW), F32), jax.ShapeDtypeStruct((t, MLA_HEADS * LANE), F32)],
        name="attn_fwd", compiler_params=pltpu.CompilerParams(dimension_semantics=("parallel", "arbitrary")),
    )(q, k, v)


def _attn_bwd(q, k, v, do, lse, delta, tq, tk):
    t = q.shape[0]
    nkt = t // tk
    assert tk % tq == 0

    def body(q_ref, k_ref, v_ref, do_ref, lse_ref, dl_ref, dq_ref, dk_ref, dv_ref):
        j = pl.program_id(1)

        @pl.when(j == 0)
        def _():
            dq_ref[...] = jnp.zeros_like(dq_ref)

        kt = k_ref[...]
        vt = v_ref[...]

        def step(q0, carry, masked):
            dk, dv = carry
            rows = pl.ds(q0, tq)
            qt = q_ref[rows, :]
            dot_ = do_ref[rows, :]
            p = jnp.exp2(_dot(qt, kt, NT) - lse_ref[rows, pl.ds(0, 1)])
            if masked:
                p = jnp.where(_causal(tq, tk, q0, j * tk), p, 0.0)
            dv = dv + _dot(p.astype(BF), dot_, TN)
            ds = (p * (_dot(dot_, vt, NT) - dl_ref[rows, pl.ds(0, 1)])).astype(BF)
            dk = dk + _dot(ds, qt, TN)
            dq_ref[rows, :] += _dot(ds, kt)
            return dk, dv

        per = tk // tq
        carry = (jnp.zeros((tk, 2 * LANE), F32), jnp.zeros((tk, V_HEAD), F32))
        for dd in range(per):
            carry = step(pl.multiple_of(j * tk + dd * tq, tq), carry, True)

        def group(g, c):
            for dd in range(per):
                c = step(pl.multiple_of(g * tk + dd * tq, tq), c, False)
            return c

        dk, dv = lax.fori_loop(j + 1, nkt, group, carry)
        dk_ref[...] = dk * _LN2
        dv_ref[...] = dv

        @pl.when(j == nkt - 1)
        def _():
            dq_ref[...] = dq_ref[...] * _LN2

    return pl.pallas_call(
        body, grid=(MLA_HEADS, nkt),
        in_specs=[pl.BlockSpec((t, 2 * LANE), lambda h, j: (0, h)), pl.BlockSpec((tk, 2 * LANE), lambda h, j: (j, h)),
                  pl.BlockSpec((tk, V_HEAD), lambda h, j: (j, h)), pl.BlockSpec((t, V_HEAD), lambda h, j: (0, h)),
                  pl.BlockSpec((t, LANE), lambda h, j: (0, h)), pl.BlockSpec((t, LANE), lambda h, j: (0, h))],
        out_specs=[pl.BlockSpec((t, 2 * LANE), lambda h, j: (0, h)), pl.BlockSpec((tk, 2 * LANE), lambda h, j: (j, h)),
                   pl.BlockSpec((tk, V_HEAD), lambda h, j: (j, h))],
        out_shape=[jax.ShapeDtypeStruct((t, MLA_HEADS * 2 * LANE), F32), jax.ShapeDtypeStruct((t, MLA_HEADS * 2 * LANE), F32),
                   jax.ShapeDtypeStruct((t, MLA_VW), F32)],
        name="attn_bwd", compiler_params=pltpu.CompilerParams(dimension_semantics=("parallel", "arbitrary")),
    )(q, k, v, do, lse, delta)


def _adam_update(w, g, m, v):
    mm = ADAM_B1 * m + (1.0 - ADAM_B1) * g
    vv = ADAM_B2 * v + (1.0 - ADAM_B2) * jnp.square(g)
    m_hat = mm / (1.0 - ADAM_B1 ** ADAM_STEP)
    v_hat = vv / (1.0 - ADAM_B2 ** ADAM_STEP)
    return -ADAM_LR * (m_hat / (jnp.sqrt(v_hat) + ADAM_EPS) + ADAM_WD * w), mm, vv


def _adamw(w, g, m, v, name):
    r, c = w.shape
    tr = max([r // s for s in range(1, r // 8 + 1) if r % s == 0 and (r // s) % 8 == 0 and r // s <= 256] or [r])
    slots = g.ndim == 3

    def body(w_ref, g_ref, m_ref, v_ref, g_out, d_ref, nm_ref, nv_ref):
        if slots:
            gg = g_ref[0].astype(F32)
            for s in range(1, N_DEV):
                gg = gg + g_ref[s].astype(F32)
        else:
            gg = g_ref[...]
        g_out[...] = gg
        d_ref[...], nm_ref[...], nv_ref[...] = _adam_update(w_ref[...], gg, m_ref[...], v_ref[...])

    spec = pl.BlockSpec((tr, c), lambda i: (i, 0))
    g_spec = pl.BlockSpec((N_DEV, tr, c), lambda i: (0, i, 0)) if slots else spec
    return pl.pallas_call(
        body, grid=(r // tr,), in_specs=[spec, g_spec, spec, spec], out_specs=[spec] * 4,
        out_shape=[jax.ShapeDtypeStruct((r, c), F32)] * 4, name=name,
        compiler_params=pltpu.CompilerParams(dimension_semantics=("arbitrary",)),
    )(w, g, m, v)


def _adamw_many(ws, gs, ms, vs, name):
    n = len(ws)

    def body(*refs):
        for i in range(n):
            w_ref, g_ref, m_ref, v_ref = (refs[j * n + i] for j in range(4))
            d_ref, nm_ref, nv_ref = (refs[(4 + j) * n + i] for j in range(3))
            d_ref[...], nm_ref[...], nv_ref[...] = _adam_update(w_ref[...], g_ref[...], m_ref[...], v_ref[...])

    shapes = [jax.ShapeDtypeStruct(w.shape, F32) for w in ws]
    outs = pl.pallas_call(body, out_shape=shapes * 3, name=name)(*ws, *gs, *ms, *vs)
    return outs[:n], outs[n:2 * n], outs[2 * n:]


def _cast_bf16(xs, name, after=None):
    n = len(xs)
    extra = [] if after is None else [after]

    def body(*refs):
        outs = refs[n + len(extra):]
        for i in range(n):
            outs[i][...] = refs[i][...].astype(BF)

    vmem = pl.BlockSpec(memory_space=pltpu.VMEM)
    return pl.pallas_call(
        body, out_shape=[jax.ShapeDtypeStruct(x.shape, BF) for x in xs], name=name,
        in_specs=[vmem] * n + [pl.BlockSpec(memory_space=pl.ANY)] * len(extra), out_specs=[vmem] * n)(*xs, *extra)


def _pad_rows(a, n):
    return jnp.pad(a, ((0, n - a.shape[0]), (0, 0)))


def _w_in_to_padded(wt):
    s_ba = P_CQ
    s_cq = s_ba + 2 * DN_HEADS
    s_kr = s_cq + Q_LORA + KV_LORA
    return jnp.concatenate([wt[:s_ba], wt[s_cq:s_kr], _pad_rows(wt[s_ba:s_cq], LANE), _pad_rows(wt[s_kr:], LANE)], axis=0)


def _w_in_from_padded(wt):
    return jnp.concatenate([wt[:P_CQ], wt[P_BA:P_BA + 2 * DN_HEADS], wt[P_CQ:P_BA], wt[P_KR:P_KR + QK_ROPE]], axis=0)


def _w_uq_to_padded(wt):
    w3 = wt.reshape(MLA_HEADS, QK_NOPE + QK_ROPE, Q_LORA)
    nope = w3[:, :QK_NOPE].reshape(MLA_HEADS * QK_NOPE, Q_LORA)
    rope = jnp.pad(w3[:, QK_NOPE:], ((0, 0), (0, LANE - QK_ROPE), (0, 0))).reshape(MLA_HEADS * LANE, Q_LORA)
    return jnp.concatenate([nope, rope], axis=0)


def _w_uq_from_padded(wt):
    nope = wt[:MLA_HEADS * QK_NOPE].reshape(MLA_HEADS, QK_NOPE, Q_LORA)
    rope = wt[MLA_HEADS * QK_NOPE:].reshape(MLA_HEADS, LANE, Q_LORA)[:, :QK_ROPE]
    return jnp.concatenate([nope, rope], axis=1).reshape(MLA_HEADS * (QK_NOPE + QK_ROPE), Q_LORA)


def _pack(pieces, width, row_mult):
    flat = jnp.concatenate([p.reshape(-1) for p in pieces])
    n = flat.shape[0]
    rows = -(-n // (width * row_mult)) * row_mult
    return jnp.pad(flat, (0, rows * width - n)).reshape(rows, width)


def _unpack(flat, shapes):
    out, o = [], 0
    for s in shapes:
        n = math.prod(s)
        out.append(flat[o:o + n].reshape(s))
        o += n
    return out


def kernel(x, c, positions, w_ada, b_ada, w_in, conv_w, a_log, dt_bias, dn_norm_g, q_norm_g, w_uq, kv_norm_g, w_ukv, w_o, ln1_g, ln1_b, w_gate, w_up, w_down, ln2_g, ln2_b, loss_target, m_w_ada, m_b_ada, m_w_in, m_conv_w, m_a_log, m_dt_bias, m_dn_norm_g, m_q_norm_g, m_w_uq, m_kv_norm_g, m_w_ukv, m_w_o, m_ln1_g, m_ln1_b, m_w_gate, m_w_up, m_w_down, m_ln2_g, m_ln2_b, v_w_ada, v_b_ada, v_w_in, v_conv_w, v_a_log, v_dt_bias, v_dn_norm_g, v_q_norm_g, v_w_uq, v_kv_norm_g, v_w_ukv, v_w_o, v_ln1_g, v_ln1_b, v_w_gate, v_w_up, v_w_down, v_ln2_g, v_ln2_b):
    me = 4 * lax.axis_index("x") + 2 * lax.axis_index("y") + lax.axis_index("c")
    t, d = x.shape[1], x.shape[2]
    ada_n = w_ada.shape[2]

    tr = lambda w: w[0].T
    rows = lambda a: a.reshape(-1, a.shape[2])
    (in_shard,) = _cast_bf16([tr(w_in)], "cast_w_in")
    cw = conv_w.shape[3]
    a_in, c_all, conv_all = _gather_by_chip([in_shard, c, conv_w[0, :, 0, :]], "gather_w_in_and_small")
    c_all = c_all.reshape(N_DEV, d)
    conv_full = conv_all.transpose(1, 0, 2).reshape(CONV_K, N_DEV * cw)
    conv_w8 = jnp.pad(conv_full, ((0, 8 - CONV_K), (0, 0)))

    b_ada_mine = lax.dynamic_slice(b_ada, (0, me * ada_n), (1, ada_n))
    mod_cols = _mod_fwd(c_all, w_ada[0], b_ada_mine)
    (mod_all,) = _exchange([mod_cols.reshape(N_DEV, 1, ada_n)], "scatter_mod", scatter=True)
    mod = mod_all.reshape(1, N_DEV * ada_n)

    later = _cast_bf16([tr(w_uq), tr(w_ukv), w_o[0], tr(w_gate), tr(w_up), w_down[0]], "cast_weights", after=mod)
    mixer_gather, token_a = _exchange_start(later[:3], "gather_mixer_weights_start", scatter=False)
    ffn_gather, token_b = _exchange_start(later[3:], "gather_ffn_weights_start", scatter=False)
    mod = mod + (token_a + token_b)
    w_in_t = _w_in_to_padded(rows(a_in))

    def mixer_weights(after):
        a_uq, a_ukv, a_o = _exchange_wait(mixer_gather, after, "gather_mixer_weights_wait", scatter=False)
        return _w_uq_to_padded(rows(a_uq)), rows(a_ukv), rows(a_o)

    def ffn_weights(after):
        a_gate, a_up, a_down = _exchange_wait(ffn_gather, after, "gather_ffn_weights_wait", scatter=False)
        return rows(a_gate), rows(a_up), rows(a_down)

    def by_dest(g):
        return g.reshape(N_DEV, -1, g.shape[1])

    scatters = {}

    def grads_ready(tag, *g):
        if tag == "ffn":
            pieces = [by_dest(a) for a in g]
        elif tag == "mixer":
            g_w_o, g_w_uq_t, g_w_ukv_t = g
            pieces = [by_dest(g_w_o), by_dest(_w_uq_from_padded(g_w_uq_t).astype(BF)), by_dest(g_w_ukv_t.astype(BF))]
        else:
            pieces = [by_dest(_w_in_from_padded(g[0]))]
        scatters[tag], token = _exchange_start(pieces, "scatter_%s_grads_start" % tag, scatter=True)
        return token

    loc = _local_step(x[0], loss_target[0], positions[0], mod, w_in_t, mixer_weights, ffn_weights, grads_ready,
                      conv_w8, a_log, dt_bias, dn_norm_g, q_norm_g, kv_norm_g, ln1_g, ln1_b, ln2_g, ln2_b)
    grad_x, loss_acc, dmod, d_conv8, d_al8, d_dt8, d_dn_g, d_q_g, d_kv_g, d_ln1_g, d_ln1_b, d_ln2_g, d_ln2_b = loc

    small_shapes = [(6 * d,), (CONV_K, N_DEV * cw), (DN_HEADS,), (DN_HEADS,), (DN_DV,), (Q_LORA,), (KV_LORA,), (d,), (d,), (d,), (d,), (1,)]
    gsmall = _pack([dmod, d_conv8[:CONV_K], d_al8[0, :DN_HEADS], d_dt8[0, :DN_HEADS], d_dn_g, d_q_g, d_kv_g,
                    d_ln1_g, d_ln1_b, d_ln2_g, d_ln2_b, loss_acc[0, :1]], LANE, 8)
    (gsmall_all,) = _exchange([gsmall], "gather_small_grads", scatter=False)
    dmod_all = gsmall_all.reshape(N_DEV, -1)[:, :6 * d]
    tot = _unpack(_sum_slots(gsmall_all, "sum_small_grads").reshape(-1), small_shapes)
    g_b_ada, g_conv_full, g_a_log, g_dt_bias, g_dn_g, g_q_g, g_kv_g, g_ln1_g, g_ln1_b, g_ln2_g, g_ln2_b, loss1 = tot
    loss = loss1.reshape(())
    g_conv_w = lax.dynamic_slice(g_conv_full, (0, me * cw), (CONV_K, cw))
    g_w_ada = _mod_bwd(c_all.T, lax.dynamic_slice(dmod_all, (0, me * ada_n), (N_DEV, ada_n)))

    grads = {"w_ada": g_w_ada[None], "b_ada": g_b_ada[None], "conv_w": g_conv_w[None, :, None, :],
             "a_log": g_a_log[None], "dt_bias": g_dt_bias[None], "dn_norm_g": g_dn_g[None], "q_norm_g": g_q_g[None],
             "kv_norm_g": g_kv_g[None], "ln1_g": g_ln1_g[None], "ln1_b": g_ln1_b[None], "ln2_g": g_ln2_g[None], "ln2_b": g_ln2_b[None]}
    weights = dict(w_ada=w_ada, b_ada=b_ada, w_in=w_in, conv_w=conv_w, a_log=a_log, dt_bias=dt_bias, dn_norm_g=dn_norm_g,
                   q_norm_g=q_norm_g, w_uq=w_uq, kv_norm_g=kv_norm_g, w_ukv=w_ukv, w_o=w_o, ln1_g=ln1_g, ln1_b=ln1_b,
                   w_gate=w_gate, w_up=w_up, w_down=w_down, ln2_g=ln2_g, ln2_b=ln2_b)
    ms = dict(w_ada=m_w_ada, b_ada=m_b_ada, w_in=m_w_in, conv_w=m_conv_w, a_log=m_a_log, dt_bias=m_dt_bias,
              dn_norm_g=m_dn_norm_g, q_norm_g=m_q_norm_g, w_uq=m_w_uq, kv_norm_g=m_kv_norm_g, w_ukv=m_w_ukv, w_o=m_w_o,
              ln1_g=m_ln1_g, ln1_b=m_ln1_b, w_gate=m_w_gate, w_up=m_w_up, w_down=m_w_down, ln2_g=m_ln2_g, ln2_b=m_ln2_b)
    vs = dict(w_ada=v_w_ada, b_ada=v_b_ada, w_in=v_w_in, conv_w=v_conv_w, a_log=v_a_log, dt_bias=v_dt_bias,
              dn_norm_g=v_dn_norm_g, q_norm_g=v_q_norm_g, w_uq=v_w_uq, kv_norm_g=v_kv_norm_g, w_ukv=v_w_ukv, w_o=v_w_o,
              ln1_g=v_ln1_g, ln1_b=v_ln1_b, w_gate=v_w_gate, w_up=v_w_up, w_down=v_w_down, ln2_g=v_ln2_g, ln2_b=v_ln2_b)
    names = list(weights)
    big = ("w_ada", "w_gate", "w_up", "w_down", "w_o", "w_uq", "w_ukv", "w_in")
    waits = {"w_gate": ("ffn", ("w_gate", "w_up", "w_down")), "w_o": ("mixer", ("w_o", "w_uq", "w_ukv")), "w_in": ("in", ("w_in",))}
    delta_w, new_m, new_v, slots = {}, {}, {}, {}
    last = g_w_ada
    for n in big:
        if n == "w_in":
            rest = [r for r in names if r not in big]
            flat2 = lambda a: a.reshape(-1, a.shape[-1])
            outs = _adamw_many(*[[flat2(src[r]) for r in rest] for src in (weights, grads, ms, vs)], "adamw_small")
            for dst, o in zip((delta_w, new_m, new_v), outs):
                for r, a in zip(rest, o):
                    dst[r] = a.reshape(weights[r].shape)
            last = outs[0][0]
        transposed = n in ("w_in", "w_uq", "w_ukv", "w_gate", "w_up")
        two = (lambda a: a[0].T) if transposed else (lambda a: a[0])
        back = (lambda a: a.T[None]) if transposed else (lambda a: a[None])
        if n in waits:
            tag, members = waits[n]
            slots.update(zip(members, _exchange_wait(scatters[tag], last, "scatter_%s_grads_wait" % tag, scatter=True)))
        g_in = slots[n] if n in slots else two(grads[n])
        gr, dlt, nm, nv = _adamw(two(weights[n]), g_in, two(ms[n]), two(vs[n]), "adamw_" + n)
        grads[n], delta_w[n], new_m[n], new_v[n] = back(gr), back(dlt), back(nm), back(nv)
        last = nv

    return (loss, grad_x[None], *[grads[n] for n in names], *[delta_w[n] for n in names],
            *[new_m[n] for n in names], *[new_v[n] for n in names])


def _local_step(xs, tgt, pos, mod, w_in_t, mixer_weights, ffn_weights, grads_ready, conv_w8,
                a_log, dt_bias, dn_norm_g, q_norm_g, kv_norm_g, ln1_g, ln1_b, ln2_g, ln2_b):
    t, d = xs.shape
    sh_m, sc_m, gt_m, sh_f, sc_f, gt_f = [mod[:, i * d:(i + 1) * d] for i in range(6)]
    pos_col = pos.astype(F32).reshape(t, 1)
    inv_freq = 1.0 / (ROPE_THETA ** (jnp.arange(0, QK_ROPE, 2, dtype=F32) / QK_ROPE))
    inv_freq2 = jnp.pad(jnp.concatenate([inv_freq, inv_freq]), (0, LANE - QK_ROPE)).reshape(1, LANE)
    al8 = jnp.pad(a_log, ((0, 7), (0, LANE - DN_HEADS)))
    dt8 = jnp.pad(dt_bias, ((0, 7), (0, LANE - DN_HEADS)))

    tm = min(512, t)
    tq = min(256, t)
    tk = min(512, t)

    (h1,) = _rowwise("modulate_in", lambda xx, sc, sh: xx * (1.0 + sc) + sh, [xs], [sc_m, sh_m], [(d, BF)], [], tm)
    proj = _matmul(h1, w_in_t, "nt", "in_proj")
    qkv = _conv_fwd(proj, conv_w8, min(256, t))
    gdn_tm = min(512, t)
    intra, inverses = _gdn_intra_fwd(qkv, proj, al8, dt8, gdn_tm)
    o_dn, states = _gdn_scan_fwd(intra, gdn_tm)
    w_uq_t, w_ukv_t, w_o_f = mixer_weights(states)
    qc, kc, vc = _mla_prep_fwd(proj, pos_col, inv_freq2, q_norm_g, kv_norm_g, w_uq_t, w_ukv_t, tm)
    o_mla, lse = _attn_fwd(qc, kc, vc, min(1024, t), min(1024, t))

    def mix_in(o, z, om, g):
        return jnp.concatenate(_gdn_out(o, z, g) + [om], axis=1)

    (mixin,) = _rowwise("mixer_out", mix_in, [o_dn, (proj, DN_VW, P_Z // DN_VW), o_mla], [dn_norm_g], [(2 * DN_VW, BF)], [], tm)
    mix = _matmul(mixin, w_o_f, "nn", "out_proj")

    def block1(xx, mx, gt, g1, b1, sc, sh):
        x1 = _layernorm(DEEPNORM_ALPHA * xx + gt * mx, g1, b1)
        return x1, x1 * (1.0 + sc) + sh

    x1, h2 = _rowwise("norm1_modulate", block1, [xs, mix], [gt_m, ln1_g, ln1_b, sc_f, sh_f], [(d, F32), (d, BF)], [], tm)
    w_gate_f, w_up_f, w_down_f = ffn_weights(h2)
    act, gate, up = _ffn_in(h2, w_gate_f, w_up_f)
    ff = _matmul(act, w_down_f, "nn", "ffn_out")

    def tail_loss(x1_, ff_, gt, g2, b2, tg):
        y = _layernorm(DEEPNORM_ALPHA * x1_ + gt * ff_, g2, b2)
        return 0.5 * jnp.sum(jnp.mean(jnp.square(y - tg), axis=-1))

    def tail(x1_, ff_, tg, gt, g2, b2):
        loss, (dx1, dff, dgt, dg2, db2) = jax.value_and_grad(tail_loss, argnums=(0, 1, 2, 3, 4))(x1_, ff_, gt, g2, b2, tg)
        return dx1, dff, jnp.full((1, LANE), loss, F32), dgt, dg2, db2

    dx1_a, dff, loss_acc, d_gt_f, d_ln2_g, d_ln2_b = _rowwise(
        "norm2_loss", tail, [x1, ff, tgt], [gt_f, ln2_g, ln2_b], [(d, F32), (d, BF)], [(1, LANE), (1, d), (1, d), (1, d)], tm)

    g_w_down = _matmul(act, dff, "tn", "d_w_down", BF)
    dgate, dup = _ffn_act_bwd(dff, w_down_f, gate, up)
    g_w_gate = _matmul(dgate, h2, "tn", "d_w_gate", BF)
    g_w_up = _matmul(dup, h2, "tn", "d_w_up", BF)
    token = grads_ready("ffn", g_w_gate, g_w_up, g_w_down)
    dh2 = _matmul2_nn(dgate, w_gate_f, dup, w_up_f, "d_ffn_in")

    def block1_bwd(xx, mx, dx1_, dh2_, gt, g1, b1, sc, sh):
        _, vjp = jax.vjp(block1, xx, mx, gt, g1, b1, sc, sh)
        dxx, dmx, dgt, dg1, db1, dsc, dsh = vjp((dx1_, dh2_))
        return dxx, dmx, dgt, dg1, db1, dsc, dsh

    dx_a, dmix, d_gt_m, d_ln1_g, d_ln1_b, d_sc_f, d_sh_f = _rowwise(
        "norm1_modulate_bwd", block1_bwd, [xs, mix, dx1_a, dh2], [gt_m + token, ln1_g, ln1_b, sc_f, sh_f],
        [(d, F32), (d, BF)], [(1, d)] * 5, min(256, t))

    dmixin = _matmul(dmix, w_o_f, "nt", "d_mixer_out")
    g_w_o = _matmul(mixin, dmix, "tn", "d_w_o", BF)

    def mixer_bwd(o, z, om, dmi, g):
        _, vjp = jax.vjp(lambda o_, z_, g_: jnp.concatenate(_gdn_out(o_, z_, g_), axis=1), o, z, g)
        do_, dz_, dg_ = vjp(dmi[:, :DN_VW])
        dom = dmi[:, DN_VW:]
        delta = [jnp.broadcast_to(jnp.sum(dom[:, h * V_HEAD:(h + 1) * V_HEAD] * om[:, h * V_HEAD:(h + 1) * V_HEAD], axis=-1, keepdims=True), (o.shape[0], LANE))
                 for h in range(MLA_HEADS)]
        return do_, dz_, dom, jnp.concatenate(delta, axis=1), dg_

    do_dn, dz, do_mla, delta, d_dn_g = _rowwise(
        "mixer_out_bwd", mixer_bwd, [o_dn, (proj, DN_VW, P_Z // DN_VW), o_mla, dmixin], [dn_norm_g],
        [(DN_VW, F32), (DN_VW, BF), (MLA_VW, BF), (MLA_HEADS * LANE, F32)], [(1, DN_DV)], tm)

    dqc, dkc, dvc = _attn_bwd(qc, kc, vc, do_mla, lse, delta, min(512, t), min(1024, t))
    dcq, dckv, dkr, d_q_g, d_kv_g, g_w_uq_t, g_w_ukv_t = _mla_prep_bwd(
        proj, pos_col, inv_freq2, q_norm_g, kv_norm_g, w_uq_t, w_ukv_t, dqc, dkc, dvc, min(256, t))

    token = grads_ready("mixer", g_w_o, g_w_uq_t, g_w_ukv_t)

    d_intra = _gdn_scan_bwd(intra, states, do_dn, gdn_tm)
    dqkv_act, dba, d_al8, d_dt8 = _gdn_intra_bwd(qkv, proj, al8 + token, dt8, inverses, d_intra, min(256, t))
    dqkv_pre, d_conv8 = _conv_bwd(proj, conv_w8, dqkv_act, min(256, t))

    dproj = jnp.concatenate([dqkv_pre, dz, dcq, dckv, dba, dkr], axis=1)
    dh1 = _matmul(dproj, w_in_t, "nn", "d_in_proj")
    g_w_in_t = _matmul(dproj, h1, "tn", "d_w_in", BF)
    token = grads_ready("in", g_w_in_t)

    def modulate_bwd(xx, dh, dxa, sc):
        return dh * (1.0 + sc) + dxa, jnp.sum(dh * xx, axis=0, keepdims=True), jnp.sum(dh, axis=0, keepdims=True)

    grad_x, d_sc_m, d_sh_m = _rowwise("modulate_in_bwd", modulate_bwd, [xs, dh1, dx_a], [sc_m + token], [(d, F32)], [(1, d), (1, d)], tm)
    dmod = jnp.concatenate([d_sh_m, d_sc_m, d_gt_m, d_sh_f, d_sc_f, d_gt_f], axis=1)
    return grad_x, loss_acc, dmod, d_conv8, d_al8, d_dt8, d_dn_g, d_q_g, d_kv_g, d_ln1_g, d_ln1_b, d_ln2_g, d_ln2_b
```

```python
import functools
import math

import jax
import jax.numpy as jnp
from jax import lax
from jax.experimental import pallas as pl
from jax.experimental.pallas import tpu as pltpu

F32 = jnp.float32
BF = jnp.bfloat16
HI = lax.Precision.HIGHEST

N_DEV = 8
DN_HEADS = 4
DN_DK = 128
DN_DV = 128
CONV_K = 4
CHUNK = 64
MLA_HEADS = 4
QK_NOPE = 128
QK_ROPE = 64
V_HEAD = 128
Q_LORA = 512
KV_LORA = 256
ROPE_THETA = 10000.0
DEPTH = 1
DEEPNORM_ALPHA = (2.0 * DEPTH) ** 0.25
LANE = 128
CONV_HALO = 8
GL_ROWS = 8
CONV_ROWS, CONV_COLS = 64, 256

DN_QK = DN_HEADS * DN_DK
DN_VW = DN_HEADS * DN_DV
DN_CONV_CH = 2 * DN_QK + DN_VW
MLA_VW = MLA_HEADS * V_HEAD
P_Z = DN_CONV_CH
P_CQ = P_Z + DN_VW
P_CKV = P_CQ + Q_LORA
P_BA = P_CKV + KV_LORA
P_KR = P_BA + LANE
N_INP = P_KR + LANE

ADAM_LR = 0.001
ADAM_B1 = 0.9
ADAM_B2 = 0.999
ADAM_EPS = 1e-08
ADAM_WD = 0.01
ADAM_STEP = 10

NN = (((1,), (0,)), ((), ()))
NT = (((1,), (1,)), ((), ()))
TN = (((0,), (0,)), ((), ()))


def _pick(n, prefs):
    for p in prefs:
        if n % p == 0:
            return p
    return n


def _full(shape):
    return pl.BlockSpec(shape, lambda *_: (0,) * len(shape))


def _dot(a, b, dims=NN):
    return lax.dot_general(a, b, dims, preferred_element_type=F32)


def _doth(a, b, dims=NN):
    return lax.dot_general(a, b, dims, precision=HI, preferred_element_type=F32)


@jax.custom_vjp
def _mmb(a, b):
    return _dot(a.astype(BF), b.astype(BF), NN)


def _mmb_fwd(a, b):
    return _mmb(a, b), (a, b)


def _mmb_bwd(res, g):
    a, b = res
    gb = g.astype(BF)
    return (_dot(gb, b.astype(BF), NT).astype(a.dtype), _dot(a.astype(BF), gb, TN).astype(b.dtype))


_mmb.defvjp(_mmb_fwd, _mmb_bwd)


@jax.custom_vjp
def _mmb_nt(a, b):
    return _dot(a.astype(BF), b.astype(BF), NT)


def _mmb_nt_fwd(a, b):
    return _mmb_nt(a, b), (a, b)


def _mmb_nt_bwd(res, g):
    a, b = res
    gb = g.astype(BF)
    return (_dot(gb, b.astype(BF), NN).astype(a.dtype), _dot(gb, a.astype(BF), TN).astype(b.dtype))


_mmb_nt.defvjp(_mmb_nt_fwd, _mmb_nt_bwd)


def _sigmoid(x):
    return 0.5 * (jnp.tanh(0.5 * x) + 1.0)


def _silu(x):
    return x * _sigmoid(x)


def _softplus(x):
    return jnp.maximum(x, 0.0) + jnp.log(1.0 + jnp.exp(-jnp.abs(x)))


def _layernorm(x, g, b, eps=1e-5):
    mu = jnp.mean(x, axis=-1, keepdims=True)
    xc = x - mu
    var = jnp.mean(xc * xc, axis=-1, keepdims=True)
    return xc * lax.rsqrt(var + eps) * g + b


def _rmsnorm(x, g, eps=1e-6):
    return x * lax.rsqrt(jnp.mean(x * x, axis=-1, keepdims=True) + eps) * g


def _l2norm(x, eps=1e-6):
    return x * lax.rsqrt(jnp.sum(x * x, axis=-1, keepdims=True) + eps)


def _rowwise(name, fn, rows, vecs, out_rows, out_accs, tm):
    rows = [r if isinstance(r, tuple) else (r, r.shape[1], 0) for r in rows]
    t = rows[0][0].shape[0]
    tm = min(tm, t)
    assert t % tm == 0
    nr, nv, no = len(rows), len(vecs), len(out_rows)

    def body(*refs):
        ins = [r[...] for r in refs[:nr + nv]]
        outs = fn(*ins)
        outs = outs if isinstance(outs, (tuple, list)) else (outs,)
        o_rows = refs[nr + nv:nr + nv + no]
        o_accs = refs[nr + nv + no:]
        for o, val in zip(o_rows, outs[:no]):
            o[...] = val.astype(o.dtype)
        if o_accs:
            @pl.when(pl.program_id(0) == 0)
            def _():
                for o in o_accs:
                    o[...] = jnp.zeros_like(o)
            for o, val in zip(o_accs, outs[no:]):
                o[...] += val

    in_specs = [pl.BlockSpec((tm, w), functools.partial(lambda i, j: (i, j), j=j)) for (_, w, j) in rows]
    in_specs += [_full(v.shape) for v in vecs]
    out_specs = [pl.BlockSpec((tm, w), lambda i: (i, 0)) for (w, _) in out_rows]
    out_specs += [_full(s) for s in out_accs]
    out_shape = [jax.ShapeDtypeStruct((t, w), d) for (w, d) in out_rows]
    out_shape += [jax.ShapeDtypeStruct(s, F32) for s in out_accs]
    res = pl.pallas_call(
        body, grid=(t // tm,), in_specs=in_specs, out_specs=out_specs, out_shape=out_shape, name=name,
        compiler_params=pltpu.CompilerParams(dimension_semantics=("arbitrary",)),
    )(*[r[0] for r in rows], *vecs)
    return res


def _matmul(a, b, mode, name, out_dtype=F32):
    if mode == "nn":
        (m, k), n = a.shape, b.shape[1]
    elif mode == "nt":
        (m, k), n = a.shape, b.shape[0]
    else:
        (k, m), n = a.shape, b.shape[1]
    tm, tn, tk = _matmul_tiles(m, n, k, a.dtype.itemsize, b.dtype.itemsize, jnp.dtype(out_dtype).itemsize)
    nk = k // tk
    dims = {"nn": NN, "nt": NT, "tn": TN}[mode]

    def body(a_ref, b_ref, o_ref, *acc):
        part = _dot(a_ref[...].astype(BF), b_ref[...].astype(BF), dims)
        if nk == 1:
            o_ref[...] = part.astype(o_ref.dtype)
            return
        (acc_ref,) = acc
        kk = pl.program_id(2)

        @pl.when(kk == 0)
        def _():
            acc_ref[...] = part

        @pl.when(kk > 0)
        def _():
            acc_ref[...] += part

        @pl.when(kk == nk - 1)
        def _():
            o_ref[...] = acc_ref[...].astype(o_ref.dtype)

    a_spec = pl.BlockSpec((tk, tm), lambda i, j, kk: (kk, i)) if mode == "tn" else pl.BlockSpec((tm, tk), lambda i, j, kk: (i, kk))
    b_spec = pl.BlockSpec((tn, tk), lambda i, j, kk: (j, kk)) if mode == "nt" else pl.BlockSpec((tk, tn), lambda i, j, kk: (kk, j))
    return pl.pallas_call(
        body, grid=(m // tm, n // tn, nk), in_specs=[a_spec, b_spec],
        out_specs=pl.BlockSpec((tm, tn), lambda i, j, kk: (i, j)),
        out_shape=jax.ShapeDtypeStruct((m, n), out_dtype),
        scratch_shapes=[pltpu.VMEM((tm, tn), F32)] if nk > 1 else [], name=name,
        compiler_params=pltpu.CompilerParams(dimension_semantics=("parallel", "parallel", "arbitrary")),
    )(a, b)


def _lane_tile(n, cap):
    return max([n // s for s in range(1, n // LANE + 1) if n % s == 0 and (n // s) % LANE == 0 and n // s <= cap] or [n])


def _ffn_in(h, w_gate, w_up):
    m, k = h.shape
    f = w_gate.shape[0]
    tm, tn = _pick(m, (512, 256, 128)), _lane_tile(f, 1408)

    def body(h_ref, wg_ref, wu_ref, act_ref, g_ref, u_ref):
        hh = h_ref[...]
        g = _dot(hh, wg_ref[...], NT)
        u = _dot(hh, wu_ref[...], NT)
        act_ref[...] = (_silu(g) * u).astype(act_ref.dtype)
        g_ref[...] = g.astype(g_ref.dtype)
        u_ref[...] = u.astype(u_ref.dtype)

    w_spec = pl.BlockSpec((tn, k), lambda i, j: (j, 0))
    o_spec = pl.BlockSpec((tm, tn), lambda i, j: (i, j))
    return pl.pallas_call(
        body, grid=(m // tm, f // tn), in_specs=[pl.BlockSpec((tm, k), lambda i, j: (i, 0)), w_spec, w_spec],
        out_specs=[o_spec] * 3, out_shape=[jax.ShapeDtypeStruct((m, f), BF)] * 3, name="ffn_in",
        compiler_params=pltpu.CompilerParams(dimension_semantics=("parallel", "parallel")),
    )(h, w_gate, w_up)


def _ffn_act_bwd(dff, w_down, gate, up):
    m, k = dff.shape
    f = w_down.shape[0]
    tm, tn = _pick(m, (512, 256, 128)), _lane_tile(f, 1408)

    def body(d_ref, w_ref, g_ref, u_ref, dg_ref, du_ref):
        da = _dot(d_ref[...], w_ref[...], NT)
        g = g_ref[...].astype(F32)
        sg = _sigmoid(g)
        dg_ref[...] = (da * u_ref[...].astype(F32) * (sg * (1.0 + g * (1.0 - sg)))).astype(dg_ref.dtype)
        du_ref[...] = (da * (g * sg)).astype(du_ref.dtype)

    o_spec = pl.BlockSpec((tm, tn), lambda i, j: (i, j))
    return pl.pallas_call(
        body, grid=(m // tm, f // tn),
        in_specs=[pl.BlockSpec((tm, k), lambda i, j: (i, 0)), pl.BlockSpec((tn, k), lambda i, j: (j, 0)), o_spec, o_spec],
        out_specs=[o_spec] * 2, out_shape=[jax.ShapeDtypeStruct((m, f), BF)] * 2, name="d_ffn_act",
        compiler_params=pltpu.CompilerParams(dimension_semantics=("parallel", "parallel")),
    )(dff, w_down, gate, up)


def _matmul2_nn(a1, b1, a2, b2, name, out_dtype=F32):
    m, k = a1.shape
    n = b1.shape[1]
    tm, tn = _pick(m, (1024, 512, 256, 128)), _pick(n, (512, 256, 128))

    def body(a1_ref, b1_ref, a2_ref, b2_ref, o_ref):
        o_ref[...] = (_dot(a1_ref[...], b1_ref[...]) + _dot(a2_ref[...], b2_ref[...])).astype(o_ref.dtype)

    a_spec = pl.BlockSpec((tm, k), lambda i, j: (i, 0))
    b_spec = pl.BlockSpec((k, tn), lambda i, j: (0, j))
    return pl.pallas_call(
        body, grid=(m // tm, n // tn), in_specs=[a_spec, b_spec, a_spec, b_spec],
        out_specs=pl.BlockSpec((tm, tn), lambda i, j: (i, j)), out_shape=jax.ShapeDtypeStruct((m, n), out_dtype), name=name,
        compiler_params=pltpu.CompilerParams(dimension_semantics=("parallel", "parallel")),
    )(a1, b1, a2, b2)


MATMUL_VMEM_BUDGET = 28 * 1024 * 1024


def _matmul_tiles(m, n, k, a_bytes, b_bytes, o_bytes):
    def divisors(x, cap):
        return sorted({x // s for s in range(1, 65) if x % s == 0 and (x // s) % LANE == 0 and x // s <= cap}, reverse=True) or [x]

    for tk in divisors(k, k):
        best = None
        for tm in divisors(m, 1024):
            for tn in divisors(n, 2048):
                need = 2 * (tm * tk * a_bytes + tk * tn * b_bytes + tm * tn * o_bytes) + (tm * tn * 4 if tk < k else 0)
                if need <= MATMUL_VMEM_BUDGET and tm * tn >= 512 * 512 and (best is None or tm * tn > best[0] * best[1]):
                    best = (tm, tn)
        if best:
            return best[0], best[1], tk
    return _pick(m, (512, 256, 128)), _pick(n, (512, 256, 128)), _pick(k, (512, 256, 128))


def _exchange(xs, name, scatter):
    n = len(xs)
    npeer = N_DEV - 1

    def body(*refs):
        x_refs, o_refs = refs[:n], refs[n:2 * n]
        send_sems, recv_sems, local_sems = refs[2 * n:]
        mx, my, mc = lax.axis_index("x"), lax.axis_index("y"), lax.axis_index("c")
        me = 4 * mx + 2 * my + mc
        src_me = [x.at[me] if scatter else x for x in x_refs]
        mine = [pltpu.make_async_copy(src_me[a], o_refs[a].at[me], local_sems.at[a]) for a in range(n)]
        for cp in mine:
            cp.start()
        copies = []
        for k in range(1, N_DEV):
            px, py, pc = mx ^ (k >> 2), my ^ ((k >> 1) & 1), mc ^ (k & 1)
            peer = 4 * px + 2 * py + pc
            for a in range(n):
                cp = pltpu.make_async_remote_copy(
                    src_ref=x_refs[a].at[peer] if scatter else x_refs[a], dst_ref=o_refs[a].at[me],
                    send_sem=send_sems.at[a * npeer + k - 1], recv_sem=recv_sems.at[a * npeer + k - 1],
                    device_id=(px, py, pc), device_id_type=pl.DeviceIdType.MESH)
                cp.start()
                copies.append((cp, a, k, peer))
        for cp, a, k, peer in copies:
            pltpu.make_async_remote_copy(
                src_ref=src_me[a], dst_ref=o_refs[a].at[peer], send_sem=send_sems.at[a * npeer + k - 1],
                recv_sem=recv_sems.at[a * npeer + k - 1], device_id=(mx, my, mc),
                device_id_type=pl.DeviceIdType.MESH).wait_recv()
        for cp, _, _, _ in copies:
            cp.wait_send()
        for cp in mine:
            cp.wait()

    return pl.pallas_call(
        body, out_shape=[jax.ShapeDtypeStruct((N_DEV,) + x.shape[-2:], x.dtype) for x in xs],
        in_specs=[pl.BlockSpec(memory_space=pl.ANY)] * n, out_specs=[pl.BlockSpec(memory_space=pl.ANY)] * n,
        scratch_shapes=[pltpu.SemaphoreType.DMA((n * npeer,)), pltpu.SemaphoreType.DMA((n * npeer,)),
                        pltpu.SemaphoreType.DMA((n,))],
        name=name,
    )(*xs)


def _gather_by_chip(xs, name):
    n = len(xs)
    per = N_DEV - 1

    def body(*refs):
        x_refs, o_refs = refs[:n], refs[n:2 * n]
        send_sems, recv_sems, local_sems = refs[2 * n:]
        mx, my, mc = lax.axis_index("x"), lax.axis_index("y"), lax.axis_index("c")
        me, sibling = (mx, my, mc), (mx, my, 1 - mc)
        chips = [(1 - mx, my), (mx, 1 - my), (1 - mx, 1 - my)]
        slot = lambda d: 4 * d[0] + 2 * d[1] + d[2]

        def copy(a, k, block, to, src=None):
            dst = o_refs[a].at[slot(block)]
            return pltpu.make_async_remote_copy(
                src_ref=dst if src is None else src, dst_ref=dst, send_sem=send_sems.at[a * per + k],
                recv_sem=recv_sems.at[a * per + k], device_id=to, device_id_type=pl.DeviceIdType.MESH)

        mine = [pltpu.make_async_copy(x_refs[a], o_refs[a].at[slot(me)], local_sems.at[a]) for a in range(n)]
        for cp in mine:
            cp.start()
        first = []
        for a in range(n):
            first.append(copy(a, 0, me, sibling, src=x_refs[a]))
            first += [copy(a, 1 + j, me, (*chip, mc), src=x_refs[a]) for j, chip in enumerate(chips)]
        for cp in first:
            cp.start()
        passed = []
        for j, chip in enumerate(chips):
            for a in range(n):
                copy(a, 1 + j, (*chip, mc), me).wait_recv()
                cp = copy(a, 4 + j, (*chip, mc), sibling)
                cp.start()
                passed.append(cp)
        for a in range(n):
            copy(a, 0, sibling, me).wait_recv()
            for j, chip in enumerate(chips):
                copy(a, 4 + j, (*chip, 1 - mc), me).wait_recv()
        for cp in first + passed:
            cp.wait_send()
        for cp in mine:
            cp.wait()

    return pl.pallas_call(
        body, out_shape=[jax.ShapeDtypeStruct((N_DEV,) + x.shape, x.dtype) for x in xs],
        in_specs=[pl.BlockSpec(memory_space=pl.ANY)] * n, out_specs=[pl.BlockSpec(memory_space=pl.ANY)] * n,
        scratch_shapes=[pltpu.SemaphoreType.DMA((n * per,)), pltpu.SemaphoreType.DMA((n * per,)),
                        pltpu.SemaphoreType.DMA((n,))],
        name=name,
    )(*xs)


def _peer_of(k):
    mx, my, mc = lax.axis_index("x"), lax.axis_index("y"), lax.axis_index("c")
    px, py, pc = mx ^ (k >> 2), my ^ ((k >> 1) & 1), mc ^ (k & 1)
    return (px, py, pc), 4 * px + 2 * py + pc


def _exchange_start(xs, name, scatter):
    n = len(xs)
    npeer = N_DEV - 1

    def body(*refs):
        x_refs, land_refs = refs[:n], refs[n:2 * n]
        send_sems, recv_sems, token = refs[2 * n], refs[2 * n + 1], refs[-1]
        me = 4 * lax.axis_index("x") + 2 * lax.axis_index("y") + lax.axis_index("c")
        for k in range(1, N_DEV):
            dev, peer = _peer_of(k)
            for a in range(n):
                pltpu.make_async_remote_copy(
                    src_ref=x_refs[a].at[peer] if scatter else x_refs[a], dst_ref=land_refs[a].at[me],
                    send_sem=send_sems.at[a * npeer + k - 1], recv_sem=recv_sems.at[a * npeer + k - 1],
                    device_id=dev, device_id_type=pl.DeviceIdType.MESH).start()
        token[...] = jnp.zeros_like(token)

    hbm = pl.BlockSpec(memory_space=pltpu.HBM)
    sem = pl.BlockSpec(memory_space=pltpu.SEMAPHORE)
    lands = [pltpu.with_memory_space_constraint(lax.empty((N_DEV,) + x.shape[-2:], x.dtype), pltpu.HBM) for x in xs]
    srcs = [pltpu.with_memory_space_constraint(x, pltpu.HBM) for x in xs]
    outs = pl.pallas_call(
        body, name=name,
        out_shape=(pltpu.SemaphoreType.DMA((n * npeer,)), pltpu.SemaphoreType.DMA((n * npeer,)),
                   *[pltpu.HBM(x.shape, x.dtype) for x in srcs], *[pltpu.HBM(z.shape, z.dtype) for z in lands],
                   jax.ShapeDtypeStruct((8, LANE), F32)),
        in_specs=[hbm] * (2 * n), out_specs=(sem, sem, *[hbm] * (2 * n), pl.BlockSpec(memory_space=pltpu.VMEM)),
        input_output_aliases={i: 2 + i for i in range(2 * n)},
        compiler_params=pltpu.CompilerParams(has_side_effects=pltpu.SideEffectType.DATAFLOW_SIDE_EFFECTING),
    )(*srcs, *lands)
    return (outs[0], outs[1], list(outs[2:2 + n]), list(outs[2 + n:2 + 2 * n])), outs[-1][0:1, 0:1]


def _exchange_wait(started, after, name, scatter):
    send_sems, recv_sems, srcs, lands = started
    n = len(srcs)
    npeer = N_DEV - 1

    def body(*refs):
        x_refs, land_refs = refs[:n], refs[n:2 * n]
        send_sems, recv_sems = refs[2 * n], refs[2 * n + 1]
        mx, my, mc = lax.axis_index("x"), lax.axis_index("y"), lax.axis_index("c")
        me = 4 * mx + 2 * my + mc
        for k in range(1, N_DEV):
            _, peer = _peer_of(k)
            for a in range(n):
                src = x_refs[a].at[me] if scatter else x_refs[a]
                cp = pltpu.make_async_remote_copy(
                    src_ref=src, dst_ref=land_refs[a].at[peer], send_sem=send_sems.at[a * npeer + k - 1],
                    recv_sem=recv_sems.at[a * npeer + k - 1], device_id=(mx, my, mc), device_id_type=pl.DeviceIdType.MESH)
                cp.wait_send()
                cp.wait_recv()

    hbm = pl.BlockSpec(memory_space=pltpu.HBM)
    sem = pl.BlockSpec(memory_space=pltpu.SEMAPHORE)
    outs = pl.pallas_call(
        body, name=name,
        out_shape=(*[pltpu.HBM(x.shape, x.dtype) for x in srcs], *[pltpu.HBM(z.shape, z.dtype) for z in lands]),
        in_specs=[hbm] * (2 * n) + [sem, sem, pl.BlockSpec(memory_space=pl.ANY)], out_specs=tuple([hbm] * (2 * n)),
        input_output_aliases={i: i for i in range(2 * n)},
        compiler_params=pltpu.CompilerParams(has_side_effects=pltpu.SideEffectType.DATAFLOW_SIDE_EFFECTING),
    )(*srcs, *lands, send_sems, recv_sems, after)
    me = 4 * lax.axis_index("x") + 2 * lax.axis_index("y") + lax.axis_index("c")
    full = []
    for x, land in zip(outs[:n], outs[n:]):
        own = lax.dynamic_slice(x, (me, 0, 0), (1,) + x.shape[1:]) if scatter else x[None]
        full.append(lax.dynamic_update_slice(land, own, (me, 0, 0)))
    return full


def _sum_slots(x, name):
    _, r, c = x.shape
    tr = _pick(r, (512, 256, 128, 64, 32, 16))

    def body(x_ref, o_ref):
        acc = x_ref[0].astype(F32)
        for s in range(1, N_DEV):
            acc = acc + x_ref[s].astype(F32)
        o_ref[...] = acc

    return pl.pallas_call(
        body, grid=(r // tr,), in_specs=[pl.BlockSpec((N_DEV, tr, c), lambda i: (0, i, 0))],
        out_specs=pl.BlockSpec((tr, c), lambda i: (i, 0)), out_shape=jax.ShapeDtypeStruct((r, c), F32), name=name,
        compiler_params=pltpu.CompilerParams(dimension_semantics=("arbitrary",)),
    )(x)


def _mod_fwd(c_all, w_ada, b_ada_mine):
    def body(c_ref, w_ref, b_ref, o_ref):
        o_ref[...] = _doth(_silu(c_ref[...]), w_ref[...]) + b_ref[...]

    return pl.pallas_call(body, out_shape=jax.ShapeDtypeStruct((c_all.shape[0], w_ada.shape[1]), F32), name="mod_fwd")(c_all, w_ada, b_ada_mine)


def _mod_bwd(c_all_t, dmod_mine):
    def body(ct_ref, d_ref, o_ref):
        s = _silu(ct_ref[...])
        acc = s[:, 0:1] * d_ref[pl.ds(0, 1), :]
        for b in range(1, N_DEV):
            acc = acc + s[:, b:b + 1] * d_ref[pl.ds(b, 1), :]
        o_ref[...] = acc

    return pl.pallas_call(body, out_shape=jax.ShapeDtypeStruct((c_all_t.shape[0], dmod_mine.shape[1]), F32), name="mod_bwd")(c_all_t, dmod_mine)


def _conv_fwd(proj, conv_w8, tm):
    t = proj.shape[0]
    ch = DN_CONV_CH

    def body(x_ref, w_ref, o_ref, buf):
        @pl.when(pl.program_id(0) == 0)
        def _():
            buf[pl.ds(0, CONV_HALO), :] = jnp.zeros((CONV_HALO, ch), F32)

        buf[pl.ds(CONV_HALO, tm), :] = x_ref[...]
        for c0 in range(0, ch, CONV_COLS):
            cols = pl.ds(c0, CONV_COLS)
            w = [w_ref[pl.ds(j, 1), cols] for j in range(CONV_K)]
            for r0 in range(0, tm, CONV_ROWS):
                acc = buf[pl.ds(r0 + CONV_HALO - (CONV_K - 1), CONV_ROWS), cols] * w[0]
                for j in range(1, CONV_K):
                    acc = acc + buf[pl.ds(r0 + CONV_HALO - (CONV_K - 1) + j, CONV_ROWS), cols] * w[j]
                o_ref[pl.ds(r0, CONV_ROWS), cols] = _silu(acc)
        buf[pl.ds(0, CONV_HALO), :] = buf[pl.ds(tm, CONV_HALO), :]

    return pl.pallas_call(
        body, grid=(t // tm,), in_specs=[pl.BlockSpec((tm, ch), lambda i: (i, 0)), _full(conv_w8.shape)],
        out_specs=pl.BlockSpec((tm, ch), lambda i: (i, 0)), out_shape=jax.ShapeDtypeStruct((t, ch), F32),
        scratch_shapes=[pltpu.VMEM((tm + CONV_HALO, ch), F32)], name="conv_fwd",
        compiler_params=pltpu.CompilerParams(dimension_semantics=("arbitrary",)),
    )(proj, conv_w8)


def _conv_bwd(proj, conv_w8, dact, tm):
    t = proj.shape[0]
    ch = DN_CONV_CH
    nt = t // tm
    hb = tm // CONV_HALO

    def body(x_ref, xp_ref, w_ref, dy_ref, dx_ref, dw_ref, xbuf, dbuf):
        step = pl.program_id(0)

        @pl.when(step == 0)
        def _():
            dbuf[pl.ds(tm, CONV_HALO), :] = jnp.zeros((CONV_HALO, ch), F32)
            dw_ref[...] = jnp.zeros_like(dw_ref)

        first = step == nt - 1
        xbuf[pl.ds(0, CONV_HALO), :] = jnp.where(first, 0.0, xp_ref[...])
        xbuf[pl.ds(CONV_HALO, tm), :] = x_ref[...]
        for c0 in range(0, ch, CONV_COLS):
            cols = pl.ds(c0, CONV_COLS)
            w = [w_ref[pl.ds(j, 1), cols] for j in range(CONV_K)]
            dw = [jnp.zeros((1, CONV_COLS), F32) for _ in range(CONV_K)]
            for r0 in range(0, tm, CONV_ROWS):
                xs = [xbuf[pl.ds(r0 + CONV_HALO - (CONV_K - 1) + j, CONV_ROWS), cols] for j in range(CONV_K)]
                pre = xs[0] * w[0]
                for j in range(1, CONV_K):
                    pre = pre + xs[j] * w[j]
                sg = _sigmoid(pre)
                dpre = dy_ref[pl.ds(r0, CONV_ROWS), cols] * (sg * (1.0 + pre * (1.0 - sg)))
                dbuf[pl.ds(r0, CONV_ROWS), cols] = dpre
                dw = [dw[j] + jnp.sum(dpre * xs[j], axis=0, keepdims=True) for j in range(CONV_K)]
            for j in range(CONV_K):
                dw_ref[pl.ds(j, 1), cols] += dw[j]
            for r0 in range(0, tm, CONV_ROWS):
                dx = dbuf[pl.ds(r0 + CONV_K - 1, CONV_ROWS), cols] * w[0]
                for j in range(1, CONV_K):
                    dx = dx + dbuf[pl.ds(r0 + CONV_K - 1 - j, CONV_ROWS), cols] * w[j]
                dx_ref[pl.ds(r0, CONV_ROWS), cols] = dx.astype(dx_ref.dtype)
        dbuf[pl.ds(tm, CONV_HALO), :] = dbuf[pl.ds(0, CONV_HALO), :]

    rev = lambda i: (nt - 1 - i, 0)
    prev = lambda i: (jnp.maximum((nt - 1 - i) * hb - 1, 0), 0)
    return pl.pallas_call(
        body, grid=(nt,),
        in_specs=[pl.BlockSpec((tm, ch), rev), pl.BlockSpec((CONV_HALO, ch), prev), _full(conv_w8.shape),
                  pl.BlockSpec((tm, ch), rev)],
        out_specs=[pl.BlockSpec((tm, ch), rev), _full(conv_w8.shape)],
        out_shape=[jax.ShapeDtypeStruct((t, ch), BF), jax.ShapeDtypeStruct(conv_w8.shape, F32)],
        scratch_shapes=[pltpu.VMEM((tm + CONV_HALO, ch), F32), pltpu.VMEM((tm + CONV_HALO, ch), F32)], name="conv_bwd",
        compiler_params=pltpu.CompilerParams(dimension_semantics=("arbitrary",)),
    )(proj, proj, conv_w8, dact)


BNN = (((2,), (1,)), ((0,), (0,)))
BNT = (((2,), (2,)), ((0,), (0,)))
BTN = (((1,), (1,)), ((0,), (0,)))


def _bdot(a, b, dims, precision=None):
    return lax.dot_general(a, b, dims, precision=precision, preferred_element_type=F32)


@jax.custom_vjp
def _bmmb_nt(a, b):
    return _bdot(a.astype(BF), b.astype(BF), BNT)


def _bmmb_nt_fwd(a, b):
    return _bmmb_nt(a, b), (a, b)


def _bmmb_nt_bwd(res, g):
    a, b = res
    gb = g.astype(BF)
    return _bdot(gb, b.astype(BF), BNN), _bdot(gb, a.astype(BF), BTN)


_bmmb_nt.defvjp(_bmmb_nt_fwd, _bmmb_nt_bwd)


@jax.custom_vjp
def _bmmb(a, b):
    return _bdot(a.astype(BF), b.astype(BF), BNN)


def _bmmb_fwd(a, b):
    return _bmmb(a, b), (a, b)


def _bmmb_bwd(res, g):
    a, b = res
    gb = g.astype(BF)
    return _bdot(gb, b.astype(BF), BNT), _bdot(a.astype(BF), gb, BTN)


_bmmb.defvjp(_bmmb_fwd, _bmmb_bwd)


@jax.custom_vjp
def _bmmb_tn(a, b):
    return _bdot(a.astype(BF), b.astype(BF), BTN)


def _bmmb_tn_fwd(a, b):
    return _bmmb_tn(a, b), (a, b)


def _bmmb_tn_bwd(res, g):
    a, b = res
    gb = g.astype(BF)
    return _bdot(b.astype(BF), gb, BNT), _bdot(a.astype(BF), gb, BNN)


_bmmb_tn.defvjp(_bmmb_tn_fwd, _bmmb_tn_bwd)


def _unit_lower_solve_fwd(a, r):
    c = a.shape[-1]
    ri = lax.broadcasted_iota(jnp.int32, a.shape, 1)
    ci = lax.broadcasted_iota(jnp.int32, a.shape, 2)
    xm = -a
    inv = (ri == ci).astype(F32) + xm
    for _ in range(int(math.log2(c)) - 1):
        xm = _bdot(xm, xm, BNN, HI)
        inv = inv + _bdot(inv, xm, BNN, HI)
    x = _bdot(inv, r, BNN, HI)
    return x, (inv, x)


def _unit_lower_solve_bwd(res, g):
    inv, x = res
    dr = _bdot(inv, g, BTN, HI)
    return -_bdot(dr, x, BNT, HI), dr


@jax.custom_vjp
def _unit_lower_solve_given(a, r, inv):
    return _bdot(inv, r, BNN, HI)


def _unit_lower_solve_given_fwd(a, r, inv):
    x = _bdot(inv, r, BNN, HI)
    return x, (inv, x)


def _unit_lower_solve_given_bwd(res, g):
    da, dr = _unit_lower_solve_bwd(res, g)
    return da, dr, jnp.zeros_like(res[0])


_unit_lower_solve_given.defvjp(_unit_lower_solve_given_fwd, _unit_lower_solve_given_bwd)


def _gdn_intra(qkv, ba, al8, dt8, inv4=None):
    tm = qkv.shape[0]
    nb = tm // CHUNK
    bsz = DN_HEADS * nb

    def heads(x0):
        return jnp.concatenate([qkv[:, x0 + h * LANE:x0 + (h + 1) * LANE].reshape(nb, CHUNK, LANE) for h in range(DN_HEADS)], axis=0)

    def spread(c0):
        return jnp.concatenate([jnp.broadcast_to(ba[:, c0 + h:c0 + h + 1], (tm, LANE)).reshape(nb, CHUNK, LANE)
                                for h in range(DN_HEADS)], axis=0)

    def per_head(v8):
        return jnp.concatenate([jnp.broadcast_to(v8[0:1, h:h + 1].reshape(1, 1, 1), (nb, 1, LANE)) for h in range(DN_HEADS)], axis=0)

    ri = lax.broadcasted_iota(jnp.int32, (bsz, CHUNK, CHUNK), 1)
    ci = lax.broadcasted_iota(jnp.int32, (bsz, CHUNK, CHUNK), 2)
    incl = ri >= ci
    strict = ri > ci

    q = _l2norm(heads(0)) * (DN_DK ** -0.5)
    k = _l2norm(heads(DN_QK))
    va = heads(2 * DN_QK)
    beta = _sigmoid(spread(0))
    g = -jnp.exp(per_head(al8)) * _softplus(spread(DN_HEADS) + per_head(dt8))
    gc = _bdot(incl.astype(F32), g, BNN, HI)
    g_last = jnp.sum(g, axis=1, keepdims=True)
    gcol = gc[:, :, :CHUNK]
    diff = gcol - jnp.swapaxes(gcol, 1, 2)
    decay = jnp.where(incl, jnp.exp(jnp.where(incl, diff, 0.0)), 0.0)
    kb = k * beta
    a_mat = jnp.where(strict, _bmmb_nt(kb, k) * decay, 0.0)
    egc = jnp.exp(gc)
    rhs = jnp.concatenate([kb * egc, va * beta], axis=2)
    if inv4 is None:
        wu, (inv, _) = _unit_lower_solve_fwd(a_mat, rhs)
    else:
        wu = _unit_lower_solve_given(a_mat, rhs, inv4.reshape(bsz, CHUNK, CHUNK))
    attn = jnp.where(incl, _bmmb_nt(q, k) * decay, 0.0)

    def unheads(x):
        return jnp.concatenate([x[h * nb:(h + 1) * nb].reshape(tm, LANE) for h in range(DN_HEADS)], axis=1)

    w_c, u_c = wu[:, :, :DN_DK], wu[:, :, DN_DK:]
    kd = k * jnp.exp(g_last - gc)
    out = (unheads(q * egc - _bmmb(attn, w_c)), unheads(_bmmb(attn, u_c)),
           _bmmb_tn(kd, w_c).reshape(DN_HEADS, nb, DN_DK, DN_DK), _bmmb_tn(kd, u_c).reshape(DN_HEADS, nb, DN_DK, DN_DV),
           jnp.broadcast_to(g_last, (bsz, GL_ROWS, LANE)).reshape(DN_HEADS, nb, GL_ROWS, LANE))
    return out if inv4 is not None else out + (inv.reshape(DN_HEADS, nb, CHUNK, CHUNK),)


def _gdn_scan_step(qp, op, c_mat, n_mat, gl, s):
    return _mmb(qp, s) + op, s * jnp.exp(gl) - _mmb(c_mat, s) + n_mat


def _gdn_intra_specs(t, tm, dts, order=lambda i: i):
    nb = tm // CHUNK
    row = pl.BlockSpec((tm, DN_VW), lambda i: (order(i), 0))
    mat = pl.BlockSpec((DN_HEADS, nb, DN_DK, DN_DV), lambda i: (0, order(i), 0, 0))
    row_shape = lambda d: jax.ShapeDtypeStruct((t, DN_VW), d)
    mat_shape = lambda d: jax.ShapeDtypeStruct((DN_HEADS, t // CHUNK, DN_DK, DN_DV), d)
    gl = pl.BlockSpec((DN_HEADS, nb, GL_ROWS, LANE), lambda i: (0, order(i), 0, 0))
    gl_shape = jax.ShapeDtypeStruct((DN_HEADS, t // CHUNK, GL_ROWS, LANE), dts[4])
    return [row, row, mat, mat, gl], [row_shape(dts[0]), row_shape(dts[1]), mat_shape(dts[2]), mat_shape(dts[3]), gl_shape]


def _gdn_intra_fwd(qkv, proj, al8, dt8, tm):
    t = qkv.shape[0]

    def body(qkv_ref, ba_ref, al_ref, dt_ref, *outs):
        for o, val in zip(outs, _gdn_intra(qkv_ref[...], ba_ref[...], al_ref[...], dt_ref[...])):
            o[...] = val.astype(o.dtype)

    specs, shapes = _gdn_intra_specs(t, tm, (BF, F32, BF, BF, F32))
    specs.append(_gdn_inverse_spec(tm))
    shapes.append(jax.ShapeDtypeStruct((DN_HEADS, t // CHUNK, CHUNK, CHUNK), F32))
    res = pl.pallas_call(
        body, grid=(t // tm,),
        in_specs=[pl.BlockSpec((tm, DN_CONV_CH), lambda i: (i, 0)), pl.BlockSpec((tm, LANE), lambda i: (i, P_BA // LANE)),
                  _full(al8.shape), _full(dt8.shape)],
        out_specs=specs, out_shape=shapes, name="gdn_intra_fwd",
        compiler_params=pltpu.CompilerParams(dimension_semantics=("parallel",)),
    )(qkv, proj, al8, dt8)
    return res[:5], res[5]


def _gdn_inverse_spec(tm):
    return pl.BlockSpec((DN_HEADS, tm // CHUNK, CHUNK, CHUNK), lambda i: (0, i, 0, 0))


def _gdn_intra_bwd(qkv, proj, al8, dt8, inverses, cts, tm):
    t = qkv.shape[0]

    def body(qkv_ref, ba_ref, al_ref, dt_ref, inv_ref, *refs):
        ct_refs, (dqkv_ref, dba_ref, dal_ref, ddt_ref) = refs[:5], refs[5:]

        @pl.when(pl.program_id(0) == 0)
        def _():
            dal_ref[...] = jnp.zeros_like(dal_ref)
            ddt_ref[...] = jnp.zeros_like(ddt_ref)

        _, vjp = jax.vjp(functools.partial(_gdn_intra, inv4=inv_ref[...]), qkv_ref[...], ba_ref[...], al_ref[...], dt_ref[...])
        dqkv, dba, dal, ddt = vjp(tuple(r[...].astype(F32) for r in ct_refs))
        dqkv_ref[...] = dqkv
        dba_ref[...] = dba.astype(dba_ref.dtype)
        dal_ref[...] += dal
        ddt_ref[...] += ddt

    specs, _ = _gdn_intra_specs(t, tm, (F32,) * 5)
    return pl.pallas_call(
        body, grid=(t // tm,),
        in_specs=[pl.BlockSpec((tm, DN_CONV_CH), lambda i: (i, 0)), pl.BlockSpec((tm, LANE), lambda i: (i, P_BA // LANE)),
                  _full(al8.shape), _full(dt8.shape), _gdn_inverse_spec(tm)] + specs,
        out_specs=[pl.BlockSpec((tm, DN_CONV_CH), lambda i: (i, 0)), pl.BlockSpec((tm, LANE), lambda i: (i, 0)),
                   _full(al8.shape), _full(dt8.shape)],
        out_shape=[jax.ShapeDtypeStruct((t, DN_CONV_CH), F32), jax.ShapeDtypeStruct((t, LANE), BF),
                   jax.ShapeDtypeStruct(al8.shape, F32), jax.ShapeDtypeStruct(dt8.shape, F32)],
        name="gdn_intra_bwd", compiler_params=pltpu.CompilerParams(dimension_semantics=("arbitrary",)),
    )(qkv, proj, al8, dt8, inverses, *cts)


def _gdn_scan_fwd(intra, tm):
    t = intra[0].shape[0]
    nb = tm // CHUNK
    nc = t // CHUNK

    def body(qp_ref, op_ref, c_ref, n_ref, gl_ref, o_ref, ss_ref, s_scr):
        @pl.when(pl.program_id(0) == 0)
        def _():
            s_scr[...] = jnp.zeros_like(s_scr)

        state = [s_scr[h] for h in range(DN_HEADS)]
        for cc in range(nb):
            rows = pl.ds(cc * CHUNK, CHUNK)
            for h in range(DN_HEADS):
                cols = pl.ds(h * DN_DV, DN_DV)
                ss_ref[cc, h] = state[h]
                o_ref[rows, cols], state[h] = _gdn_scan_step(
                    qp_ref[rows, cols], op_ref[rows, cols], c_ref[h, cc], n_ref[h, cc], gl_ref[h, cc, pl.ds(0, 1), :], state[h])
        for h in range(DN_HEADS):
            s_scr[h] = state[h]

    specs, _ = _gdn_intra_specs(t, tm, (F32,) * 5)
    return pl.pallas_call(
        body, grid=(t // tm,), in_specs=specs,
        out_specs=[pl.BlockSpec((tm, DN_VW), lambda i: (i, 0)),
                   pl.BlockSpec((nb, DN_HEADS, DN_DK, DN_DV), lambda i: (i, 0, 0, 0))],
        out_shape=[jax.ShapeDtypeStruct((t, DN_VW), F32), jax.ShapeDtypeStruct((nc, DN_HEADS, DN_DK, DN_DV), F32)],
        scratch_shapes=[pltpu.VMEM((DN_HEADS, DN_DK, DN_DV), F32)], name="gdn_scan_fwd",
        compiler_params=pltpu.CompilerParams(dimension_semantics=("arbitrary",)),
    )(*intra)


def _gdn_scan_bwd(intra, states, do, tm):
    t = intra[0].shape[0]
    nb = tm // CHUNK
    ng = t // tm

    def body(qp_ref, op_ref, c_ref, n_ref, gl_ref, ss_ref, do_ref, dqp_ref, dop_ref, dc_ref, dn_ref, dgl_ref, ds_scr):
        @pl.when(pl.program_id(0) == 0)
        def _():
            ds_scr[...] = jnp.zeros_like(ds_scr)

        d_state = [ds_scr[h] for h in range(DN_HEADS)]
        for cc in reversed(range(nb)):
            rows = pl.ds(cc * CHUNK, CHUNK)
            for h in range(DN_HEADS):
                cols = pl.ds(h * DN_DV, DN_DV)
                _, vjp = jax.vjp(_gdn_scan_step, qp_ref[rows, cols].astype(F32), op_ref[rows, cols], c_ref[h, cc].astype(F32),
                                 n_ref[h, cc].astype(F32), gl_ref[h, cc, pl.ds(0, 1), :], ss_ref[cc, h])
                dqp_ref[rows, cols], dop_ref[rows, cols], dc, dn, dgl, d_state[h] = vjp((do_ref[rows, cols], d_state[h]))
                dc_ref[h, cc] = dc.astype(dc_ref.dtype)
                dn_ref[h, cc] = dn.astype(dn_ref.dtype)
                first_row = lax.broadcasted_iota(jnp.int32, (GL_ROWS, LANE), 0) == 0
                dgl_ref[h, cc] = jnp.where(first_row, dgl, 0.0)
        for h in range(DN_HEADS):
            ds_scr[h] = d_state[h]

    five, shapes = _gdn_intra_specs(t, tm, (F32, F32, BF, BF, F32), order=lambda i: ng - 1 - i)
    row = five[0]
    return pl.pallas_call(
        body, grid=(ng,),
        in_specs=five + [pl.BlockSpec((nb, DN_HEADS, DN_DK, DN_DV), lambda i: (ng - 1 - i, 0, 0, 0)), row],
        out_specs=five, out_shape=shapes,
        scratch_shapes=[pltpu.VMEM((DN_HEADS, DN_DK, DN_DV), F32)], name="gdn_scan_bwd",
        compiler_params=pltpu.CompilerParams(dimension_semantics=("arbitrary",)),
    )(*intra, states, do)


def _gdn_out(o, z, g):
    parts = []
    for h in range(DN_HEADS):
        sl = slice(h * DN_DV, (h + 1) * DN_DV)
        parts.append(_rmsnorm(o[:, sl], g) * _silu(z[:, sl]))
    return parts


_Q_SCALE = math.log2(math.e) / math.sqrt(QK_NOPE + QK_ROPE)


def _rope_tables(pos, inv_freq2):
    lane = lax.broadcasted_iota(jnp.int32, (1, LANE), 1)
    ang = pos * inv_freq2
    cos = jnp.where(lane < QK_ROPE, jnp.cos(ang), 0.0)
    sin = jnp.where(lane < QK_ROPE // 2, -jnp.sin(ang), jnp.where(lane < QK_ROPE, jnp.sin(ang), 0.0))
    return cos, sin


def _rope_swap():
    ri = lax.broadcasted_iota(jnp.int32, (LANE, LANE), 0)
    ci = lax.broadcasted_iota(jnp.int32, (LANE, LANE), 1)
    half = QK_ROPE // 2
    return (((ci < half) & (ri == ci + half)) | ((ci >= half) & (ci < QK_ROPE) & (ri == ci - half))).astype(F32)


def _mla_prep(cq, ckv, kr, gq, gkv, w_uq, w_ukv, cos, sin, swap):
    rope = lambda u: u * cos + _doth(u, swap) * sin
    q_lin = _mmb_nt(_rmsnorm(cq, gq), w_uq) * _Q_SCALE
    kv_lin = _mmb_nt(_rmsnorm(ckv, gkv), w_ukv)
    k_rope = rope(kr)
    qs, ks, vs = [], [], []
    for h in range(MLA_HEADS):
        qs += [q_lin[:, h * LANE:(h + 1) * LANE], rope(q_lin[:, (MLA_HEADS + h) * LANE:(MLA_HEADS + h + 1) * LANE])]
        ks += [kv_lin[:, 2 * h * LANE:(2 * h + 1) * LANE], k_rope]
        vs += [kv_lin[:, (2 * h + 1) * LANE:(2 * h + 2) * LANE]]
    return qs + ks + vs


def _mla_prep_fwd(proj, pos_col, inv_freq2, gq, gkv, w_uq, w_ukv, tm):
    t = proj.shape[0]
    nq = 2 * MLA_HEADS

    def body(cq_ref, ckv_ref, kr_ref, pos_ref, f_ref, gq_ref, gkv_ref, wq_ref, wkv_ref, q_ref, k_ref, v_ref):
        cos, sin = _rope_tables(pos_ref[...], f_ref[...])
        outs = _mla_prep(cq_ref[...], ckv_ref[...], kr_ref[...], gq_ref[...], gkv_ref[...], wq_ref[...], wkv_ref[...],
                         cos, sin, _rope_swap())
        for i in range(nq):
            q_ref[:, pl.ds(i * LANE, LANE)] = outs[i].astype(q_ref.dtype)
            k_ref[:, pl.ds(i * LANE, LANE)] = outs[nq + i].astype(k_ref.dtype)
        for h in range(MLA_HEADS):
            v_ref[:, pl.ds(h * LANE, LANE)] = outs[2 * nq + h].astype(v_ref.dtype)

    row = lambda w, j: pl.BlockSpec((tm, w), functools.partial(lambda i, j: (i, j), j=j))
    return pl.pallas_call(
        body, grid=(t // tm,),
        in_specs=[row(Q_LORA, P_CQ // Q_LORA), row(KV_LORA, P_CKV // KV_LORA), row(LANE, P_KR // LANE),
                  pl.BlockSpec((tm, 1), lambda i: (i, 0)), _full(inv_freq2.shape), _full(gq.shape), _full(gkv.shape),
                  _full(w_uq.shape), _full(w_ukv.shape)],
        out_specs=[row(nq * LANE, 0), row(nq * LANE, 0), row(MLA_VW, 0)],
        out_shape=[jax.ShapeDtypeStruct((t, nq * LANE), BF), jax.ShapeDtypeStruct((t, nq * LANE), BF),
                   jax.ShapeDtypeStruct((t, MLA_VW), BF)],
        name="mla_prep_fwd", compiler_params=pltpu.CompilerParams(dimension_semantics=("arbitrary",)),
    )(proj, proj, proj, pos_col, inv_freq2, gq, gkv, w_uq, w_ukv)


def _mla_prep_bwd(proj, pos_col, inv_freq2, gq, gkv, w_uq, w_ukv, dq, dk, dv, tm):
    t = proj.shape[0]
    nq = 2 * MLA_HEADS

    def body(cq_ref, ckv_ref, kr_ref, pos_ref, f_ref, gq_ref, gkv_ref, wq_ref, wkv_ref, dq_ref, dk_ref, dv_ref,
             dcq_ref, dckv_ref, dkr_ref, dgq_ref, dgkv_ref, dwq_ref, dwkv_ref):
        @pl.when(pl.program_id(0) == 0)
        def _():
            for o in (dgq_ref, dgkv_ref, dwq_ref, dwkv_ref):
                o[...] = jnp.zeros_like(o)

        cos, sin = _rope_tables(pos_ref[...], f_ref[...])
        f = functools.partial(_mla_prep, cos=cos, sin=sin, swap=_rope_swap())
        _, vjp = jax.vjp(f, cq_ref[...], ckv_ref[...], kr_ref[...], gq_ref[...], gkv_ref[...], wq_ref[...], wkv_ref[...])
        cts = [dq_ref[:, pl.ds(i * LANE, LANE)] for i in range(nq)]
        cts += [dk_ref[:, pl.ds(i * LANE, LANE)] for i in range(nq)]
        cts += [dv_ref[:, pl.ds(h * LANE, LANE)] for h in range(MLA_HEADS)]
        dcq, dckv, dkr, dgq, dgkv, dwq, dwkv = vjp(cts)
        dcq_ref[...] = dcq.astype(dcq_ref.dtype)
        dckv_ref[...] = dckv.astype(dckv_ref.dtype)
        dkr_ref[...] = dkr.astype(dkr_ref.dtype)
        dgq_ref[...] += dgq
        dgkv_ref[...] += dgkv
        dwq_ref[...] += dwq
        dwkv_ref[...] += dwkv

    row = lambda w, j: pl.BlockSpec((tm, w), functools.partial(lambda i, j: (i, j), j=j))
    return pl.pallas_call(
        body, grid=(t // tm,),
        in_specs=[row(Q_LORA, P_CQ // Q_LORA), row(KV_LORA, P_CKV // KV_LORA), row(LANE, P_KR // LANE),
                  pl.BlockSpec((tm, 1), lambda i: (i, 0)), _full(inv_freq2.shape), _full(gq.shape), _full(gkv.shape),
                  _full(w_uq.shape), _full(w_ukv.shape), row(nq * LANE, 0), row(nq * LANE, 0), row(MLA_VW, 0)],
        out_specs=[row(Q_LORA, 0), row(KV_LORA, 0), row(LANE, 0), _full(gq.shape), _full(gkv.shape),
                   _full(w_uq.shape), _full(w_ukv.shape)],
        out_shape=[jax.ShapeDtypeStruct((t, Q_LORA), BF), jax.ShapeDtypeStruct((t, KV_LORA), BF),
                   jax.ShapeDtypeStruct((t, LANE), BF), jax.ShapeDtypeStruct(gq.shape, F32),
                   jax.ShapeDtypeStruct(gkv.shape, F32), jax.ShapeDtypeStruct(w_uq.shape, F32),
                   jax.ShapeDtypeStruct(w_ukv.shape, F32)],
        name="mla_prep_bwd", compiler_params=pltpu.CompilerParams(dimension_semantics=("arbitrary",)),
    )(proj, proj, proj, pos_col, inv_freq2, gq, gkv, w_uq, w_ukv, dq, dk, dv)


_NEG = -1e30
_LN2 = math.log(2.0)
ATT_CHAINS = 2


def _causal(tq, tk, q0, k0):
    row = q0 + lax.broadcasted_iota(jnp.int32, (tq, tk), 0)
    col = k0 + lax.broadcasted_iota(jnp.int32, (tq, tk), 1)
    return col <= row


def _attn_fwd(q, k, v, tq, tk):
    t = q.shape[0]

    assert tk % tq == 0 or tq % tk == 0
    n_diag = max(1, tq // tk)

    th = tq // ATT_CHAINS

    def body(q_ref, k_ref, v_ref, o_ref, lse_ref):
        i = pl.program_id(1)
        n_full = (i * tq) // tk

        def step(k0, carry, masked):
            kt = k_ref[pl.ds(k0, tk), :]
            vt = v_ref[pl.ds(k0, tk), :]
            out = []
            for c, (m, l, acc) in enumerate(carry):
                s = _dot(q_ref[pl.ds(c * th, th), :], kt, NT)
                if masked:
                    s = jnp.where(_causal(th, tk, i * tq + c * th, k0), s, _NEG)
                m_new = jnp.maximum(m, jnp.max(s, axis=-1, keepdims=True))
                p = jnp.exp2(s - m_new)
                alpha = jnp.exp2(m - m_new)
                out.append((m_new, alpha * l + jnp.sum(p, axis=-1, keepdims=True), alpha * acc + _dot(p.astype(BF), vt)))
            return tuple(out)

        init = tuple((jnp.full((th, 1), _NEG, F32), jnp.zeros((th, 1), F32), jnp.zeros((th, V_HEAD), F32)) for _ in range(ATT_CHAINS))
        carry = lax.fori_loop(0, n_full, lambda j, c: step(pl.multiple_of(j * tk, tk), c, False), init)
        for dd in range(n_diag):
            carry = step(pl.multiple_of((n_full + dd) * tk, tk), carry, True)
        for c, (m, l, acc) in enumerate(carry):
            o_ref[pl.ds(c * th, th), :] = acc / l
            lse_ref[pl.ds(c * th, th), :] = jnp.broadcast_to(m + jnp.log2(l), (th, LANE))

    return pl.pallas_call(
        body, grid=(MLA_HEADS, t // tq),
        in_specs=[pl.BlockSpec((tq, 2 * LANE), lambda h, i: (i, h)), pl.BlockSpec((t, 2 * LANE), lambda h, i: (0, h)),
                  pl.BlockSpec((t, V_HEAD), lambda h, i: (0, h))],
        out_specs=[pl.BlockSpec((tq, V_HEAD), lambda h, i: (i, h)), pl.BlockSpec((tq, LANE), lambda h, i: (i, h))],
        out_shape=[jax.ShapeDtypeStruct((t, MLA_VW), F32), jax.ShapeDtypeStruct((t, MLA_HEADS * LANE), F32)],
        name="attn_fwd", compiler_params=pltpu.CompilerParams(dimension_semantics=("parallel", "arbitrary")),
    )(q, k, v)


def _attn_bwd(q, k, v, do, lse, delta, tq, tk):
    t = q.shape[0]
    nkt = t // tk
    assert tk % tq == 0

    def body(q_ref, k_ref, v_ref, do_ref, lse_ref, dl_ref, dq_ref, dk_ref, dv_ref):
        j = pl.program_id(1)

        @pl.when(j == 0)
        def _():
            dq_ref[...] = jnp.zeros_like(dq_ref)

        kt = k_ref[...]
        vt = v_ref[...]

        def step(q0, carry, masked):
            dk, dv = carry
            rows = pl.ds(q0, tq)
            qt = q_ref[rows, :]
            dot_ = do_ref[rows, :]
            p = jnp.exp2(_dot(qt, kt, NT) - lse_ref[rows, pl.ds(0, 1)])
            if masked:
                p = jnp.where(_causal(tq, tk, q0, j * tk), p, 0.0)
            dv = dv + _dot(p.astype(BF), dot_, TN)
            ds = (p * (_dot(dot_, vt, NT) - dl_ref[rows, pl.ds(0, 1)])).astype(BF)
            dk = dk + _dot(ds, qt, TN)
            dq_ref[rows, :] += _dot(ds, kt)
            return dk, dv

        per = tk // tq
        carry = (jnp.zeros((tk, 2 * LANE), F32), jnp.zeros((tk, V_HEAD), F32))
        for dd in range(per):
            carry = step(pl.multiple_of(j * tk + dd * tq, tq), carry, True)

        def group(g, c):
            for dd in range(per):
                c = step(pl.multiple_of(g * tk + dd * tq, tq), c, False)
            return c

        dk, dv = lax.fori_loop(j + 1, nkt, group, carry)
        dk_ref[...] = dk * _LN2
        dv_ref[...] = dv

        @pl.when(j == nkt - 1)
        def _():
            dq_ref[...] = dq_ref[...] * _LN2

    return pl.pallas_call(
        body, grid=(MLA_HEADS, nkt),
        in_specs=[pl.BlockSpec((t, 2 * LANE), lambda h, j: (0, h)), pl.BlockSpec((tk, 2 * LANE), lambda h, j: (j, h)),
                  pl.BlockSpec((tk, V_HEAD), lambda h, j: (j, h)), pl.BlockSpec((t, V_HEAD), lambda h, j: (0, h)),
                  pl.BlockSpec((t, LANE), lambda h, j: (0, h)), pl.BlockSpec((t, LANE), lambda h, j: (0, h))],
        out_specs=[pl.BlockSpec((t, 2 * LANE), lambda h, j: (0, h)), pl.BlockSpec((tk, 2 * LANE), lambda h, j: (j, h)),
                   pl.BlockSpec((tk, V_HEAD), lambda h, j: (j, h))],
        out_shape=[jax.ShapeDtypeStruct((t, MLA_HEADS * 2 * LANE), F32), jax.ShapeDtypeStruct((t, MLA_HEADS * 2 * LANE), F32),
                   jax.ShapeDtypeStruct((t, MLA_VW), F32)],
        name="attn_bwd", compiler_params=pltpu.CompilerParams(dimension_semantics=("parallel", "arbitrary")),
    )(q, k, v, do, lse, delta)


def _adam_update(w, g, m, v):
    mm = ADAM_B1 * m + (1.0 - ADAM_B1) * g
    vv = ADAM_B2 * v + (1.0 - ADAM_B2) * jnp.square(g)
    m_hat = mm / (1.0 - ADAM_B1 ** ADAM_STEP)
    v_hat = vv / (1.0 - ADAM_B2 ** ADAM_STEP)
    return -ADAM_LR * (m_hat / (jnp.sqrt(v_hat) + ADAM_EPS) + ADAM_WD * w), mm, vv


def _adamw(w, g, m, v, name):
    r, c = w.shape
    tr = max([r // s for s in range(1, r // 8 + 1) if r % s == 0 and (r // s) % 8 == 0 and r // s <= 256] or [r])
    slots = g.ndim == 3

    def body(w_ref, g_ref, m_ref, v_ref, g_out, d_ref, nm_ref, nv_ref):
        if slots:
            gg = g_ref[0].astype(F32)
            for s in range(1, N_DEV):
                gg = gg + g_ref[s].astype(F32)
        else:
            gg = g_ref[...]
        g_out[...] = gg
        d_ref[...], nm_ref[...], nv_ref[...] = _adam_update(w_ref[...], gg, m_ref[...], v_ref[...])

    spec = pl.BlockSpec((tr, c), lambda i: (i, 0))
    g_spec = pl.BlockSpec((N_DEV, tr, c), lambda i: (0, i, 0)) if slots else spec
    return pl.pallas_call(
        body, grid=(r // tr,), in_specs=[spec, g_spec, spec, spec], out_specs=[spec] * 4,
        out_shape=[jax.ShapeDtypeStruct((r, c), F32)] * 4, name=name,
        compiler_params=pltpu.CompilerParams(dimension_semantics=("arbitrary",)),
    )(w, g, m, v)


def _adamw_many(ws, gs, ms, vs, name):
    n = len(ws)

    def body(*refs):
        for i in range(n):
            w_ref, g_ref, m_ref, v_ref = (refs[j * n + i] for j in range(4))
            d_ref, nm_ref, nv_ref = (refs[(4 + j) * n + i] for j in range(3))
            d_ref[...], nm_ref[...], nv_ref[...] = _adam_update(w_ref[...], g_ref[...], m_ref[...], v_ref[...])

    shapes = [jax.ShapeDtypeStruct(w.shape, F32) for w in ws]
    outs = pl.pallas_call(body, out_shape=shapes * 3, name=name)(*ws, *gs, *ms, *vs)
    return outs[:n], outs[n:2 * n], outs[2 * n:]


def _cast_bf16(xs, name, after=None):
    n = len(xs)
    extra = [] if after is None else [after]

    def body(*refs):
        outs = refs[n + len(extra):]
        for i in range(n):
            outs[i][...] = refs[i][...].astype(BF)

    vmem = pl.BlockSpec(memory_space=pltpu.VMEM)
    return pl.pallas_call(
        body, out_shape=[jax.ShapeDtypeStruct(x.shape, BF) for x in xs], name=name,
        in_specs=[vmem] * n + [pl.BlockSpec(memory_space=pl.ANY)] * len(extra), out_specs=[vmem] * n)(*xs, *extra)


def _pad_rows(a, n):
    return jnp.pad(a, ((0, n - a.shape[0]), (0, 0)))


def _w_in_to_padded(wt):
    s_ba = P_CQ
    s_cq = s_ba + 2 * DN_HEADS
    s_kr = s_cq + Q_LORA + KV_LORA
    return jnp.concatenate([wt[:s_ba], wt[s_cq:s_kr], _pad_rows(wt[s_ba:s_cq], LANE), _pad_rows(wt[s_kr:], LANE)], axis=0)


def _w_in_from_padded(wt):
    return jnp.concatenate([wt[:P_CQ], wt[P_BA:P_BA + 2 * DN_HEADS], wt[P_CQ:P_BA], wt[P_KR:P_KR + QK_ROPE]], axis=0)


def _w_uq_to_padded(wt):
    w3 = wt.reshape(MLA_HEADS, QK_NOPE + QK_ROPE, Q_LORA)
    nope = w3[:, :QK_NOPE].reshape(MLA_HEADS * QK_NOPE, Q_LORA)
    rope = jnp.pad(w3[:, QK_NOPE:], ((0, 0), (0, LANE - QK_ROPE), (0, 0))).reshape(MLA_HEADS * LANE, Q_LORA)
    return jnp.concatenate([nope, rope], axis=0)


def _w_uq_from_padded(wt):
    nope = wt[:MLA_HEADS * QK_NOPE].reshape(MLA_HEADS, QK_NOPE, Q_LORA)
    rope = wt[MLA_HEADS * QK_NOPE:].reshape(MLA_HEADS, LANE, Q_LORA)[:, :QK_ROPE]
    return jnp.concatenate([nope, rope], axis=1).reshape(MLA_HEADS * (QK_NOPE + QK_ROPE), Q_LORA)


def _pack(pieces, width, row_mult):
    flat = jnp.concatenate([p.reshape(-1) for p in pieces])
    n = flat.shape[0]
    rows = -(-n // (width * row_mult)) * row_mult
    return jnp.pad(flat, (0, rows * width - n)).reshape(rows, width)


def _unpack(flat, shapes):
    out, o = [], 0
    for s in shapes:
        n = math.prod(s)
        out.append(flat[o:o + n].reshape(s))
        o += n
    return out


def kernel(x, c, positions, w_ada, b_ada, w_in, conv_w, a_log, dt_bias, dn_norm_g, q_norm_g, w_uq, kv_norm_g, w_ukv, w_o, ln1_g, ln1_b, w_gate, w_up, w_down, ln2_g, ln2_b, loss_target, m_w_ada, m_b_ada, m_w_in, m_conv_w, m_a_log, m_dt_bias, m_dn_norm_g, m_q_norm_g, m_w_uq, m_kv_norm_g, m_w_ukv, m_w_o, m_ln1_g, m_ln1_b, m_w_gate, m_w_up, m_w_down, m_ln2_g, m_ln2_b, v_w_ada, v_b_ada, v_w_in, v_conv_w, v_a_log, v_dt_bias, v_dn_norm_g, v_q_norm_g, v_w_uq, v_kv_norm_g, v_w_ukv, v_w_o, v_ln1_g, v_ln1_b, v_w_gate, v_w_up, v_w_down, v_ln2_g, v_ln2_b):
    me = 4 * lax.axis_index("x") + 2 * lax.axis_index("y") + lax.axis_index("c")
    t, d = x.shape[1], x.shape[2]
    ada_n = w_ada.shape[2]

    tr = lambda w: w[0].T
    rows = lambda a: a.reshape(-1, a.shape[2])
    (in_shard,) = _cast_bf16([tr(w_in)], "cast_w_in")
    cw = conv_w.shape[3]
    a_in, c_all, conv_all = _gather_by_chip([in_shard, c, conv_w[0, :, 0, :]], "gather_w_in_and_small")
    c_all = c_all.reshape(N_DEV, d)
    conv_full = conv_all.transpose(1, 0, 2).reshape(CONV_K, N_DEV * cw)
    conv_w8 = jnp.pad(conv_full, ((0, 8 - CONV_K), (0, 0)))

    b_ada_mine = lax.dynamic_slice(b_ada, (0, me * ada_n), (1, ada_n))
    mod_cols = _mod_fwd(c_all, w_ada[0], b_ada_mine)
    (mod_all,) = _exchange([mod_cols.reshape(N_DEV, 1, ada_n)], "scatter_mod", scatter=True)
    mod = mod_all.reshape(1, N_DEV * ada_n)

    later = _cast_bf16([tr(w_uq), tr(w_ukv), w_o[0], tr(w_gate), tr(w_up), w_down[0]], "cast_weights", after=mod)
    mixer_gather, token_a = _exchange_start(later[:3], "gather_mixer_weights_start", scatter=False)
    ffn_gather, token_b = _exchange_start(later[3:], "gather_ffn_weights_start", scatter=False)
    mod = mod + (token_a + token_b)
    w_in_t = _w_in_to_padded(rows(a_in))

    def mixer_weights(after):
        a_uq, a_ukv, a_o = _exchange_wait(mixer_gather, after, "gather_mixer_weights_wait", scatter=False)
        return _w_uq_to_padded(rows(a_uq)), rows(a_ukv), rows(a_o)

    def ffn_weights(after):
        a_gate, a_up, a_down = _exchange_wait(ffn_gather, after, "gather_ffn_weights_wait", scatter=False)
        return rows(a_gate), rows(a_up), rows(a_down)

    def by_dest(g):
        return g.reshape(N_DEV, -1, g.shape[1])

    scatters = {}

    def grads_ready(tag, *g):
        if tag == "ffn":
            pieces = [by_dest(a) for a in g]
        elif tag == "mixer":
            g_w_o, g_w_uq_t, g_w_ukv_t = g
            pieces = [by_dest(g_w_o), by_dest(_w_uq_from_padded(g_w_uq_t).astype(BF)), by_dest(g_w_ukv_t.astype(BF))]
        else:
            pieces = [by_dest(_w_in_from_padded(g[0]))]
        scatters[tag], token = _exchange_start(pieces, "scatter_%s_grads_start" % tag, scatter=True)
        return token

    loc = _local_step(x[0], loss_target[0], positions[0], mod, w_in_t, mixer_weights, ffn_weights, grads_ready,
                      conv_w8, a_log, dt_bias, dn_norm_g, q_norm_g, kv_norm_g, ln1_g, ln1_b, ln2_g, ln2_b)
    grad_x, loss_acc, dmod, d_conv8, d_al8, d_dt8, d_dn_g, d_q_g, d_kv_g, d_ln1_g, d_ln1_b, d_ln2_g, d_ln2_b = loc

    small_shapes = [(6 * d,), (CONV_K, N_DEV * cw), (DN_HEADS,), (DN_HEADS,), (DN_DV,), (Q_LORA,), (KV_LORA,), (d,), (d,), (d,), (d,), (1,)]
    gsmall = _pack([dmod, d_conv8[:CONV_K], d_al8[0, :DN_HEADS], d_dt8[0, :DN_HEADS], d_dn_g, d_q_g, d_kv_g,
                    d_ln1_g, d_ln1_b, d_ln2_g, d_ln2_b, loss_acc[0, :1]], LANE, 8)
    (gsmall_all,) = _exchange([gsmall], "gather_small_grads", scatter=False)
    dmod_all = gsmall_all.reshape(N_DEV, -1)[:, :6 * d]
    tot = _unpack(_sum_slots(gsmall_all, "sum_small_grads").reshape(-1), small_shapes)
    g_b_ada, g_conv_full, g_a_log, g_dt_bias, g_dn_g, g_q_g, g_kv_g, g_ln1_g, g_ln1_b, g_ln2_g, g_ln2_b, loss1 = tot
    loss = loss1.reshape(())
    g_conv_w = lax.dynamic_slice(g_conv_full, (0, me * cw), (CONV_K, cw))
    g_w_ada = _mod_bwd(c_all.T, lax.dynamic_slice(dmod_all, (0, me * ada_n), (N_DEV, ada_n)))

    grads = {"w_ada": g_w_ada[None], "b_ada": g_b_ada[None], "conv_w": g_conv_w[None, :, None, :],
             "a_log": g_a_log[None], "dt_bias": g_dt_bias[None], "dn_norm_g": g_dn_g[None], "q_norm_g": g_q_g[None],
             "kv_norm_g": g_kv_g[None], "ln1_g": g_ln1_g[None], "ln1_b": g_ln1_b[None], "ln2_g": g_ln2_g[None], "ln2_b": g_ln2_b[None]}
    weights = dict(w_ada=w_ada, b_ada=b_ada, w_in=w_in, conv_w=conv_w, a_log=a_log, dt_bias=dt_bias, dn_norm_g=dn_norm_g,
                   q_norm_g=q_norm_g, w_uq=w_uq, kv_norm_g=kv_norm_g, w_ukv=w_ukv, w_o=w_o, ln1_g=ln1_g, ln1_b=ln1_b,
                   w_gate=w_gate, w_up=w_up, w_down=w_down, ln2_g=ln2_g, ln2_b=ln2_b)
    ms = dict(w_ada=m_w_ada, b_ada=m_b_ada, w_in=m_w_in, conv_w=m_conv_w, a_log=m_a_log, dt_bias=m_dt_bias,
              dn_norm_g=m_dn_norm_g, q_norm_g=m_q_norm_g, w_uq=m_w_uq, kv_norm_g=m_kv_norm_g, w_ukv=m_w_ukv, w_o=m_w_o,
              ln1_g=m_ln1_g, ln1_b=m_ln1_b, w_gate=m_w_gate, w_up=m_w_up, w_down=m_w_down, ln2_g=m_ln2_g, ln2_b=m_ln2_b)
    vs = dict(w_ada=v_w_ada, b_ada=v_b_ada, w_in=v_w_in, conv_w=v_conv_w, a_log=v_a_log, dt_bias=v_dt_bias,
              dn_norm_g=v_dn_norm_g, q_norm_g=v_q_norm_g, w_uq=v_w_uq, kv_norm_g=v_kv_norm_g, w_ukv=v_w_ukv, w_o=v_w_o,
              ln1_g=v_ln1_g, ln1_b=v_ln1_b, w_gate=v_w_gate, w_up=v_w_up, w_down=v_w_down, ln2_g=v_ln2_g, ln2_b=v_ln2_b)
    names = list(weights)
    big = ("w_ada", "w_gate", "w_up", "w_down", "w_o", "w_uq", "w_ukv", "w_in")
    waits = {"w_gate": ("ffn", ("w_gate", "w_up", "w_down")), "w_o": ("mixer", ("w_o", "w_uq", "w_ukv")), "w_in": ("in", ("w_in",))}
    delta_w, new_m, new_v, slots = {}, {}, {}, {}
    last = g_w_ada
    for n in big:
        if n == "w_in":
            rest = [r for r in names if r not in big]
            flat2 = lambda a: a.reshape(-1, a.shape[-1])
            outs = _adamw_many(*[[flat2(src[r]) for r in rest] for src in (weights, grads, ms, vs)], "adamw_small")
            for dst, o in zip((delta_w, new_m, new_v), outs):
                for r, a in zip(rest, o):
                    dst[r] = a.reshape(weights[r].shape)
            last = outs[0][0]
        transposed = n in ("w_in", "w_uq", "w_ukv", "w_gate", "w_up")
        two = (lambda a: a[0].T) if transposed else (lambda a: a[0])
        back = (lambda a: a.T[None]) if transposed else (lambda a: a[None])
        if n in waits:
            tag, members = waits[n]
            slots.update(zip(members, _exchange_wait(scatters[tag], last, "scatter_%s_grads_wait" % tag, scatter=True)))
        g_in = slots[n] if n in slots else two(grads[n])
        gr, dlt, nm, nv = _adamw(two(weights[n]), g_in, two(ms[n]), two(vs[n]), "adamw_" + n)
        grads[n], delta_w[n], new_m[n], new_v[n] = back(gr), back(dlt), back(nm), back(nv)
        last = nv

    return (loss, grad_x[None], *[grads[n] for n in names], *[delta_w[n] for n in names],
            *[new_m[n] for n in names], *[new_v[n] for n in names])


def _local_step(xs, tgt, pos, mod, w_in_t, mixer_weights, ffn_weights, grads_ready, conv_w8,
                a_log, dt_bias, dn_norm_g, q_norm_g, kv_norm_g, ln1_g, ln1_b, ln2_g, ln2_b):
    t, d = xs.shape
    sh_m, sc_m, gt_m, sh_f, sc_f, gt_f = [mod[:, i * d:(i + 1) * d] for i in range(6)]
    pos_col = pos.astype(F32).reshape(t, 1)
    inv_freq = 1.0 / (ROPE_THETA ** (jnp.arange(0, QK_ROPE, 2, dtype=F32) / QK_ROPE))
    inv_freq2 = jnp.pad(jnp.concatenate([inv_freq, inv_freq]), (0, LANE - QK_ROPE)).reshape(1, LANE)
    al8 = jnp.pad(a_log, ((0, 7), (0, LANE - DN_HEADS)))
    dt8 = jnp.pad(dt_bias, ((0, 7), (0, LANE - DN_HEADS)))

    tm = min(512, t)
    tq = min(256, t)
    tk = min(512, t)

    (h1,) = _rowwise("modulate_in", lambda xx, sc, sh: xx * (1.0 + sc) + sh, [xs], [sc_m, sh_m], [(d, BF)], [], tm)
    proj = _matmul(h1, w_in_t, "nt", "in_proj")
    qkv = _conv_fwd(proj, conv_w8, min(256, t))
    gdn_tm = min(512, t)
    intra, inverses = _gdn_intra_fwd(qkv, proj, al8, dt8, gdn_tm)
    o_dn, states = _gdn_scan_fwd(intra, gdn_tm)
    w_uq_t, w_ukv_t, w_o_f = mixer_weights(states)
    qc, kc, vc = _mla_prep_fwd(proj, pos_col, inv_freq2, q_norm_g, kv_norm_g, w_uq_t, w_ukv_t, tm)
    o_mla, lse = _attn_fwd(qc, kc, vc, min(1024, t), min(1024, t))

    def mix_in(o, z, om, g):
        return jnp.concatenate(_gdn_out(o, z, g) + [om], axis=1)

    (mixin,) = _rowwise("mixer_out", mix_in, [o_dn, (proj, DN_VW, P_Z // DN_VW), o_mla], [dn_norm_g], [(2 * DN_VW, BF)], [], tm)
    mix = _matmul(mixin, w_o_f, "nn", "out_proj")

    def block1(xx, mx, gt, g1, b1, sc, sh):
        x1 = _layernorm(DEEPNORM_ALPHA * xx + gt * mx, g1, b1)
        return x1, x1 * (1.0 + sc) + sh

    x1, h2 = _rowwise("norm1_modulate", block1, [xs, mix], [gt_m, ln1_g, ln1_b, sc_f, sh_f], [(d, F32), (d, BF)], [], tm)
    w_gate_f, w_up_f, w_down_f = ffn_weights(h2)
    act, gate, up = _ffn_in(h2, w_gate_f, w_up_f)
    ff = _matmul(act, w_down_f, "nn", "ffn_out")

    def tail_loss(x1_, ff_, gt, g2, b2, tg):
        y = _layernorm(DEEPNORM_ALPHA * x1_ + gt * ff_, g2, b2)
        return 0.5 * jnp.sum(jnp.mean(jnp.square(y - tg), axis=-1))

    def tail(x1_, ff_, tg, gt, g2, b2):
        loss, (dx1, dff, dgt, dg2, db2) = jax.value_and_grad(tail_loss, argnums=(0, 1, 2, 3, 4))(x1_, ff_, gt, g2, b2, tg)
        return dx1, dff, jnp.full((1, LANE), loss, F32), dgt, dg2, db2

    dx1_a, dff, loss_acc, d_gt_f, d_ln2_g, d_ln2_b = _rowwise(
        "norm2_loss", tail, [x1, ff, tgt], [gt_f, ln2_g, ln2_b], [(d, BF), (d, BF)], [(1, LANE), (1, d), (1, d), (1, d)], tm)

    g_w_down = _matmul(act, dff, "tn", "d_w_down", BF)
    dgate, dup = _ffn_act_bwd(dff, w_down_f, gate, up)
    g_w_gate = _matmul(dgate, h2, "tn", "d_w_gate", BF)
    g_w_up = _matmul(dup, h2, "tn", "d_w_up", BF)
    token = grads_ready("ffn", g_w_gate, g_w_up, g_w_down)
    dh2 = _matmul2_nn(dgate, w_gate_f, dup, w_up_f, "d_ffn_in", BF)

    def block1_bwd(xx, mx, dx1_, dh2_, gt, g1, b1, sc, sh):
        _, vjp = jax.vjp(block1, xx, mx, gt, g1, b1, sc, sh)
        dxx, dmx, dgt, dg1, db1, dsc, dsh = vjp((dx1_.astype(F32), dh2_.astype(F32)))
        return dxx, dmx, dgt, dg1, db1, dsc, dsh

    dx_a, dmix, d_gt_m, d_ln1_g, d_ln1_b, d_sc_f, d_sh_f = _rowwise(
        "norm1_modulate_bwd", block1_bwd, [xs, mix, dx1_a, dh2], [gt_m + token, ln1_g, ln1_b, sc_f, sh_f],
        [(d, F32), (d, BF)], [(1, d)] * 5, min(256, t))

    dmixin = _matmul(dmix, w_o_f, "nt", "d_mixer_out", BF)
    g_w_o = _matmul(mixin, dmix, "tn", "d_w_o", BF)

    def mixer_bwd(o, z, om, dmi, g):
        _, vjp = jax.vjp(lambda o_, z_, g_: jnp.concatenate(_gdn_out(o_, z_, g_), axis=1), o, z, g)
        do_, dz_, dg_ = vjp(dmi[:, :DN_VW].astype(F32))
        dom = dmi[:, DN_VW:]
        delta = [jnp.broadcast_to(jnp.sum(dom[:, h * V_HEAD:(h + 1) * V_HEAD] * om[:, h * V_HEAD:(h + 1) * V_HEAD], axis=-1, keepdims=True), (o.shape[0], LANE))
                 for h in range(MLA_HEADS)]
        return do_, dz_, dom, jnp.concatenate(delta, axis=1), dg_

    do_dn, dz, do_mla, delta, d_dn_g = _rowwise(
        "mixer_out_bwd", mixer_bwd, [o_dn, (proj, DN_VW, P_Z // DN_VW), o_mla, dmixin], [dn_norm_g],
        [(DN_VW, F32), (DN_VW, BF), (MLA_VW, BF), (MLA_HEADS * LANE, F32)], [(1, DN_DV)], tm)

    dqc, dkc, dvc = _attn_bwd(qc, kc, vc, do_mla, lse, delta, min(512, t), min(1024, t))
    dcq, dckv, dkr, d_q_g, d_kv_g, g_w_uq_t, g_w_ukv_t = _mla_prep_bwd(
        proj, pos_col, inv_freq2, q_norm_g, kv_norm_g, w_uq_t, w_ukv_t, dqc, dkc, dvc, min(256, t))

    token = grads_ready("mixer", g_w_o, g_w_uq_t, g_w_ukv_t)

    d_intra = _gdn_scan_bwd(intra, states, do_dn, gdn_tm)
    dqkv_act, dba, d_al8, d_dt8 = _gdn_intra_bwd(qkv, proj, al8 + token, dt8, inverses, d_intra, min(256, t))
    dqkv_pre, d_conv8 = _conv_bwd(proj, conv_w8, dqkv_act, min(256, t))

    dproj = jnp.concatenate([dqkv_pre, dz, dcq, dckv, dba, dkr], axis=1)
    dh1 = _matmul(dproj, w_in_t, "nn", "d_in_proj", BF)
    g_w_in_t = _matmul(dproj, h1, "tn", "d_w_in", BF)
    token = grads_ready("in", g_w_in_t)

    def modulate_bwd(xx, dh, dxa, sc):
        dh = dh.astype(F32)
        return dh * (1.0 + sc) + dxa, jnp.sum(dh * xx, axis=0, keepdims=True), jnp.sum(dh, axis=0, keepdims=True)

    grad_x, d_sc_m, d_sh_m = _rowwise("modulate_in_bwd", modulate_bwd, [xs, dh1, dx_a], [sc_m + token], [(d, F32)], [(1, d), (1, d)], tm)
    dmod = jnp.concatenate([d_sh_m, d_sc_m, d_gt_m, d_sh_f, d_sc_f, d_gt_f], axis=1)
    return grad_x, loss_acc, dmod, d_conv8, d_al8, d_dt8, d_dn_g, d_q_g, d_kv_g, d_ln1_g, d_ln1_b, d_ln2_g, d_ln2_b
```

```python
import functools
import math

import jax
import jax.numpy as jnp
from jax import lax
from jax.experimental import pallas as pl
from jax.experimental.pallas import tpu as pltpu

F32 = jnp.float32
BF = jnp.bfloat16
HI = lax.Precision.HIGHEST

N_DEV = 8
DN_HEADS = 4
DN_DK = 128
DN_DV = 128
CONV_K = 4
CHUNK = 64
MLA_HEADS = 4
QK_NOPE = 128
QK_ROPE = 64
V_HEAD = 128
Q_LORA = 512
KV_LORA = 256
ROPE_THETA = 10000.0
DEPTH = 1
DEEPNORM_ALPHA = (2.0 * DEPTH) ** 0.25
LANE = 128
CONV_HALO = 8
GL_ROWS = 8
CONV_ROWS, CONV_COLS = 64, 256

DN_QK = DN_HEADS * DN_DK
DN_VW = DN_HEADS * DN_DV
DN_CONV_CH = 2 * DN_QK + DN_VW
MLA_VW = MLA_HEADS * V_HEAD
P_Z = DN_CONV_CH
P_CQ = P_Z + DN_VW
P_CKV = P_CQ + Q_LORA
P_BA = P_CKV + KV_LORA
P_KR = P_BA + LANE
N_INP = P_KR + LANE

ADAM_LR = 0.001
ADAM_B1 = 0.9
ADAM_B2 = 0.999
ADAM_EPS = 1e-08
ADAM_WD = 0.01
ADAM_STEP = 10

NN = (((1,), (0,)), ((), ()))
NT = (((1,), (1,)), ((), ()))
TN = (((0,), (0,)), ((), ()))


def _pick(n, prefs):
    for p in prefs:
        if n % p == 0:
            return p
    return n


def _full(shape):
    return pl.BlockSpec(shape, lambda *_: (0,) * len(shape))


def _dot(a, b, dims=NN):
    return lax.dot_general(a, b, dims, preferred_element_type=F32)


def _doth(a, b, dims=NN):
    return lax.dot_general(a, b, dims, precision=HI, preferred_element_type=F32)


@jax.custom_vjp
def _mmb(a, b):
    return _dot(a.astype(BF), b.astype(BF), NN)


def _mmb_fwd(a, b):
    return _mmb(a, b), (a, b)


def _mmb_bwd(res, g):
    a, b = res
    gb = g.astype(BF)
    return (_dot(gb, b.astype(BF), NT).astype(a.dtype), _dot(a.astype(BF), gb, TN).astype(b.dtype))


_mmb.defvjp(_mmb_fwd, _mmb_bwd)


@jax.custom_vjp
def _mmb_nt(a, b):
    return _dot(a.astype(BF), b.astype(BF), NT)


def _mmb_nt_fwd(a, b):
    return _mmb_nt(a, b), (a, b)


def _mmb_nt_bwd(res, g):
    a, b = res
    gb = g.astype(BF)
    return (_dot(gb, b.astype(BF), NN).astype(a.dtype), _dot(gb, a.astype(BF), TN).astype(b.dtype))


_mmb_nt.defvjp(_mmb_nt_fwd, _mmb_nt_bwd)


def _sigmoid(x):
    return 0.5 * (jnp.tanh(0.5 * x) + 1.0)


def _silu(x):
    return x * _sigmoid(x)


def _softplus(x):
    return jnp.maximum(x, 0.0) + jnp.log(1.0 + jnp.exp(-jnp.abs(x)))


def _layernorm(x, g, b, eps=1e-5):
    mu = jnp.mean(x, axis=-1, keepdims=True)
    xc = x - mu
    var = jnp.mean(xc * xc, axis=-1, keepdims=True)
    return xc * lax.rsqrt(var + eps) * g + b


def _rmsnorm(x, g, eps=1e-6):
    return x * lax.rsqrt(jnp.mean(x * x, axis=-1, keepdims=True) + eps) * g


def _l2norm(x, eps=1e-6):
    return x * lax.rsqrt(jnp.sum(x * x, axis=-1, keepdims=True) + eps)


def _rowwise(name, fn, rows, vecs, out_rows, out_accs, tm):
    rows = [r if isinstance(r, tuple) else (r, r.shape[1], 0) for r in rows]
    t = rows[0][0].shape[0]
    tm = min(tm, t)
    assert t % tm == 0
    nr, nv, no = len(rows), len(vecs), len(out_rows)

    def body(*refs):
        ins = [r[...] for r in refs[:nr + nv]]
        outs = fn(*ins)
        outs = outs if isinstance(outs, (tuple, list)) else (outs,)
        o_rows = refs[nr + nv:nr + nv + no]
        o_accs = refs[nr + nv + no:]
        for o, val in zip(o_rows, outs[:no]):
            o[...] = val.astype(o.dtype)
        if o_accs:
            @pl.when(pl.program_id(0) == 0)
            def _():
                for o in o_accs:
                    o[...] = jnp.zeros_like(o)
            for o, val in zip(o_accs, outs[no:]):
                o[...] += val

    in_specs = [pl.BlockSpec((tm, w), functools.partial(lambda i, j: (i, j), j=j)) for (_, w, j) in rows]
    in_specs += [_full(v.shape) for v in vecs]
    out_specs = [pl.BlockSpec((tm, w), lambda i: (i, 0)) for (w, _) in out_rows]
    out_specs += [_full(s) for s in out_accs]
    out_shape = [jax.ShapeDtypeStruct((t, w), d) for (w, d) in out_rows]
    out_shape += [jax.ShapeDtypeStruct(s, F32) for s in out_accs]
    res = pl.pallas_call(
        body, grid=(t // tm,), in_specs=in_specs, out_specs=out_specs, out_shape=out_shape, name=name,
        compiler_params=pltpu.CompilerParams(dimension_semantics=("arbitrary",)),
    )(*[r[0] for r in rows], *vecs)
    return res


def _matmul(a, b, mode, name, out_dtype=F32):
    if mode == "nn":
        (m, k), n = a.shape, b.shape[1]
    elif mode == "nt":
        (m, k), n = a.shape, b.shape[0]
    else:
        (k, m), n = a.shape, b.shape[1]
    tm, tn, tk = _matmul_tiles(m, n, k, a.dtype.itemsize, b.dtype.itemsize, jnp.dtype(out_dtype).itemsize)
    nk = k // tk
    dims = {"nn": NN, "nt": NT, "tn": TN}[mode]

    def body(a_ref, b_ref, o_ref, *acc):
        part = _dot(a_ref[...].astype(BF), b_ref[...].astype(BF), dims)
        if nk == 1:
            o_ref[...] = part.astype(o_ref.dtype)
            return
        (acc_ref,) = acc
        kk = pl.program_id(2)

        @pl.when(kk == 0)
        def _():
            acc_ref[...] = part

        @pl.when(kk > 0)
        def _():
            acc_ref[...] += part

        @pl.when(kk == nk - 1)
        def _():
            o_ref[...] = acc_ref[...].astype(o_ref.dtype)

    a_spec = pl.BlockSpec((tk, tm), lambda i, j, kk: (kk, i)) if mode == "tn" else pl.BlockSpec((tm, tk), lambda i, j, kk: (i, kk))
    b_spec = pl.BlockSpec((tn, tk), lambda i, j, kk: (j, kk)) if mode == "nt" else pl.BlockSpec((tk, tn), lambda i, j, kk: (kk, j))
    return pl.pallas_call(
        body, grid=(m // tm, n // tn, nk), in_specs=[a_spec, b_spec],
        out_specs=pl.BlockSpec((tm, tn), lambda i, j, kk: (i, j)),
        out_shape=jax.ShapeDtypeStruct((m, n), out_dtype),
        scratch_shapes=[pltpu.VMEM((tm, tn), F32)] if nk > 1 else [], name=name,
        compiler_params=pltpu.CompilerParams(dimension_semantics=("parallel", "parallel", "arbitrary")),
    )(a, b)


def _lane_tile(n, cap):
    return max([n // s for s in range(1, n // LANE + 1) if n % s == 0 and (n // s) % LANE == 0 and n // s <= cap] or [n])


def _ffn_in(h, w_gate, w_up):
    m, k = h.shape
    f = w_gate.shape[0]
    tm, tn = _pick(m, (512, 256, 128)), _lane_tile(f, 1408)

    def body(h_ref, wg_ref, wu_ref, act_ref, g_ref, u_ref):
        hh = h_ref[...]
        g = _dot(hh, wg_ref[...], NT)
        u = _dot(hh, wu_ref[...], NT)
        act_ref[...] = (_silu(g) * u).astype(act_ref.dtype)
        g_ref[...] = g.astype(g_ref.dtype)
        u_ref[...] = u.astype(u_ref.dtype)

    w_spec = pl.BlockSpec((tn, k), lambda i, j: (j, 0))
    o_spec = pl.BlockSpec((tm, tn), lambda i, j: (i, j))
    return pl.pallas_call(
        body, grid=(m // tm, f // tn), in_specs=[pl.BlockSpec((tm, k), lambda i, j: (i, 0)), w_spec, w_spec],
        out_specs=[o_spec] * 3, out_shape=[jax.ShapeDtypeStruct((m, f), BF)] * 3, name="ffn_in",
        compiler_params=pltpu.CompilerParams(dimension_semantics=("parallel", "parallel")),
    )(h, w_gate, w_up)


def _ffn_act_bwd(dff, w_down, gate, up):
    m, k = dff.shape
    f = w_down.shape[0]
    tm, tn = _pick(m, (512, 256, 128)), _lane_tile(f, 1408)

    def body(d_ref, w_ref, g_ref, u_ref, dg_ref, du_ref):
        da = _dot(d_ref[...], w_ref[...], NT)
        g = g_ref[...].astype(F32)
        sg = _sigmoid(g)
        dg_ref[...] = (da * u_ref[...].astype(F32) * (sg * (1.0 + g * (1.0 - sg)))).astype(dg_ref.dtype)
        du_ref[...] = (da * (g * sg)).astype(du_ref.dtype)

    o_spec = pl.BlockSpec((tm, tn), lambda i, j: (i, j))
    return pl.pallas_call(
        body, grid=(m // tm, f // tn),
        in_specs=[pl.BlockSpec((tm, k), lambda i, j: (i, 0)), pl.BlockSpec((tn, k), lambda i, j: (j, 0)), o_spec, o_spec],
        out_specs=[o_spec] * 2, out_shape=[jax.ShapeDtypeStruct((m, f), BF)] * 2, name="d_ffn_act",
        compiler_params=pltpu.CompilerParams(dimension_semantics=("parallel", "parallel")),
    )(dff, w_down, gate, up)


def _matmul2_nn(a1, b1, a2, b2, name, out_dtype=F32):
    m, k = a1.shape
    n = b1.shape[1]
    tm, tn = _pick(m, (1024, 512, 256, 128)), _pick(n, (512, 256, 128))

    def body(a1_ref, b1_ref, a2_ref, b2_ref, o_ref):
        o_ref[...] = (_dot(a1_ref[...], b1_ref[...]) + _dot(a2_ref[...], b2_ref[...])).astype(o_ref.dtype)

    a_spec = pl.BlockSpec((tm, k), lambda i, j: (i, 0))
    b_spec = pl.BlockSpec((k, tn), lambda i, j: (0, j))
    return pl.pallas_call(
        body, grid=(m // tm, n // tn), in_specs=[a_spec, b_spec, a_spec, b_spec],
        out_specs=pl.BlockSpec((tm, tn), lambda i, j: (i, j)), out_shape=jax.ShapeDtypeStruct((m, n), out_dtype), name=name,
        compiler_params=pltpu.CompilerParams(dimension_semantics=("parallel", "parallel")),
    )(a1, b1, a2, b2)


MATMUL_VMEM_BUDGET = 28 * 1024 * 1024


def _matmul_tiles(m, n, k, a_bytes, b_bytes, o_bytes):
    def divisors(x, cap):
        return sorted({x // s for s in range(1, 65) if x % s == 0 and (x // s) % LANE == 0 and x // s <= cap}, reverse=True) or [x]

    for tk in divisors(k, k):
        best = None
        for tm in divisors(m, 1024):
            for tn in divisors(n, 2048):
                need = 2 * (tm * tk * a_bytes + tk * tn * b_bytes + tm * tn * o_bytes) + (tm * tn * 4 if tk < k else 0)
                if need <= MATMUL_VMEM_BUDGET and tm * tn >= 512 * 512 and (best is None or tm * tn > best[0] * best[1]):
                    best = (tm, tn)
        if best:
            return best[0], best[1], tk
    return _pick(m, (512, 256, 128)), _pick(n, (512, 256, 128)), _pick(k, (512, 256, 128))


def _exchange(xs, name, scatter):
    n = len(xs)
    npeer = N_DEV - 1

    def body(*refs):
        x_refs, o_refs = refs[:n], refs[n:2 * n]
        send_sems, recv_sems, local_sems = refs[2 * n:]
        mx, my, mc = lax.axis_index("x"), lax.axis_index("y"), lax.axis_index("c")
        me = 4 * mx + 2 * my + mc
        src_me = [x.at[me] if scatter else x for x in x_refs]
        mine = [pltpu.make_async_copy(src_me[a], o_refs[a].at[me], local_sems.at[a]) for a in range(n)]
        for cp in mine:
            cp.start()
        copies = []
        for k in range(1, N_DEV):
            px, py, pc = mx ^ (k >> 2), my ^ ((k >> 1) & 1), mc ^ (k & 1)
            peer = 4 * px + 2 * py + pc
            for a in range(n):
                cp = pltpu.make_async_remote_copy(
                    src_ref=x_refs[a].at[peer] if scatter else x_refs[a], dst_ref=o_refs[a].at[me],
                    send_sem=send_sems.at[a * npeer + k - 1], recv_sem=recv_sems.at[a * npeer + k - 1],
                    device_id=(px, py, pc), device_id_type=pl.DeviceIdType.MESH)
                cp.start()
                copies.append((cp, a, k, peer))
        for cp, a, k, peer in copies:
            pltpu.make_async_remote_copy(
                src_ref=src_me[a], dst_ref=o_refs[a].at[peer], send_sem=send_sems.at[a * npeer + k - 1],
                recv_sem=recv_sems.at[a * npeer + k - 1], device_id=(mx, my, mc),
                device_id_type=pl.DeviceIdType.MESH).wait_recv()
        for cp, _, _, _ in copies:
            cp.wait_send()
        for cp in mine:
            cp.wait()

    return pl.pallas_call(
        body, out_shape=[jax.ShapeDtypeStruct((N_DEV,) + x.shape[-2:], x.dtype) for x in xs],
        in_specs=[pl.BlockSpec(memory_space=pl.ANY)] * n, out_specs=[pl.BlockSpec(memory_space=pl.ANY)] * n,
        scratch_shapes=[pltpu.SemaphoreType.DMA((n * npeer,)), pltpu.SemaphoreType.DMA((n * npeer,)),
                        pltpu.SemaphoreType.DMA((n,))],
        name=name,
    )(*xs)


def _gather_by_chip(xs, name):
    n = len(xs)
    per = N_DEV - 1

    def body(*refs):
        x_refs, o_refs = refs[:n], refs[n:2 * n]
        send_sems, recv_sems, local_sems = refs[2 * n:]
        mx, my, mc = lax.axis_index("x"), lax.axis_index("y"), lax.axis_index("c")
        me, sibling = (mx, my, mc), (mx, my, 1 - mc)
        chips = [(1 - mx, my), (mx, 1 - my), (1 - mx, 1 - my)]
        slot = lambda d: 4 * d[0] + 2 * d[1] + d[2]

        def copy(a, k, block, to, src=None):
            dst = o_refs[a].at[slot(block)]
            return pltpu.make_async_remote_copy(
                src_ref=dst if src is None else src, dst_ref=dst, send_sem=send_sems.at[a * per + k],
                recv_sem=recv_sems.at[a * per + k], device_id=to, device_id_type=pl.DeviceIdType.MESH)

        mine = [pltpu.make_async_copy(x_refs[a], o_refs[a].at[slot(me)], local_sems.at[a]) for a in range(n)]
        for cp in mine:
            cp.start()
        first = []
        for a in range(n):
            first.append(copy(a, 0, me, sibling, src=x_refs[a]))
            first += [copy(a, 1 + j, me, (*chip, mc), src=x_refs[a]) for j, chip in enumerate(chips)]
        for cp in first:
            cp.start()
        passed = []
        for j, chip in enumerate(chips):
            for a in range(n):
                copy(a, 1 + j, (*chip, mc), me).wait_recv()
                cp = copy(a, 4 + j, (*chip, mc), sibling)
                cp.start()
                passed.append(cp)
        for a in range(n):
            copy(a, 0, sibling, me).wait_recv()
            for j, chip in enumerate(chips):
                copy(a, 4 + j, (*chip, 1 - mc), me).wait_recv()
        for cp in first + passed:
            cp.wait_send()
        for cp in mine:
            cp.wait()

    return pl.pallas_call(
        body, out_shape=[jax.ShapeDtypeStruct((N_DEV,) + x.shape, x.dtype) for x in xs],
        in_specs=[pl.BlockSpec(memory_space=pl.ANY)] * n, out_specs=[pl.BlockSpec(memory_space=pl.ANY)] * n,
        scratch_shapes=[pltpu.SemaphoreType.DMA((n * per,)), pltpu.SemaphoreType.DMA((n * per,)),
                        pltpu.SemaphoreType.DMA((n,))],
        name=name,
    )(*xs)


def _peer_of(k):
    mx, my, mc = lax.axis_index("x"), lax.axis_index("y"), lax.axis_index("c")
    px, py, pc = mx ^ (k >> 2), my ^ ((k >> 1) & 1), mc ^ (k & 1)
    return (px, py, pc), 4 * px + 2 * py + pc


def _exchange_start(xs, name, scatter):
    n = len(xs)
    npeer = N_DEV - 1

    def body(*refs):
        x_refs, land_refs = refs[:n], refs[n:2 * n]
        send_sems, recv_sems, token = refs[2 * n], refs[2 * n + 1], refs[-1]
        me = 4 * lax.axis_index("x") + 2 * lax.axis_index("y") + lax.axis_index("c")
        for k in range(1, N_DEV):
            dev, peer = _peer_of(k)
            for a in range(n):
                pltpu.make_async_remote_copy(
                    src_ref=x_refs[a].at[peer] if scatter else x_refs[a], dst_ref=land_refs[a].at[me],
                    send_sem=send_sems.at[a * npeer + k - 1], recv_sem=recv_sems.at[a * npeer + k - 1],
                    device_id=dev, device_id_type=pl.DeviceIdType.MESH).start()
        token[...] = jnp.zeros_like(token)

    hbm = pl.BlockSpec(memory_space=pltpu.HBM)
    sem = pl.BlockSpec(memory_space=pltpu.SEMAPHORE)
    lands = [pltpu.with_memory_space_constraint(lax.empty((N_DEV,) + x.shape[-2:], x.dtype), pltpu.HBM) for x in xs]
    srcs = [pltpu.with_memory_space_constraint(x, pltpu.HBM) for x in xs]
    outs = pl.pallas_call(
        body, name=name,
        out_shape=(pltpu.SemaphoreType.DMA((n * npeer,)), pltpu.SemaphoreType.DMA((n * npeer,)),
                   *[pltpu.HBM(x.shape, x.dtype) for x in srcs], *[pltpu.HBM(z.shape, z.dtype) for z in lands],
                   jax.ShapeDtypeStruct((8, LANE), F32)),
        in_specs=[hbm] * (2 * n), out_specs=(sem, sem, *[hbm] * (2 * n), pl.BlockSpec(memory_space=pltpu.VMEM)),
        input_output_aliases={i: 2 + i for i in range(2 * n)},
        compiler_params=pltpu.CompilerParams(has_side_effects=pltpu.SideEffectType.DATAFLOW_SIDE_EFFECTING),
    )(*srcs, *lands)
    return (outs[0], outs[1], list(outs[2:2 + n]), list(outs[2 + n:2 + 2 * n])), outs[-1][0:1, 0:1]


def _exchange_wait(started, after, name, scatter):
    send_sems, recv_sems, srcs, lands = started
    n = len(srcs)
    npeer = N_DEV - 1

    def body(*refs):
        x_refs, land_refs = refs[:n], refs[n:2 * n]
        send_sems, recv_sems = refs[2 * n], refs[2 * n + 1]
        mx, my, mc = lax.axis_index("x"), lax.axis_index("y"), lax.axis_index("c")
        me = 4 * mx + 2 * my + mc
        for k in range(1, N_DEV):
            _, peer = _peer_of(k)
            for a in range(n):
                src = x_refs[a].at[me] if scatter else x_refs[a]
                cp = pltpu.make_async_remote_copy(
                    src_ref=src, dst_ref=land_refs[a].at[peer], send_sem=send_sems.at[a * npeer + k - 1],
                    recv_sem=recv_sems.at[a * npeer + k - 1], device_id=(mx, my, mc), device_id_type=pl.DeviceIdType.MESH)
                cp.wait_send()
                cp.wait_recv()

    hbm = pl.BlockSpec(memory_space=pltpu.HBM)
    sem = pl.BlockSpec(memory_space=pltpu.SEMAPHORE)
    outs = pl.pallas_call(
        body, name=name,
        out_shape=(*[pltpu.HBM(x.shape, x.dtype) for x in srcs], *[pltpu.HBM(z.shape, z.dtype) for z in lands]),
        in_specs=[hbm] * (2 * n) + [sem, sem, pl.BlockSpec(memory_space=pl.ANY)], out_specs=tuple([hbm] * (2 * n)),
        input_output_aliases={i: i for i in range(2 * n)},
        compiler_params=pltpu.CompilerParams(has_side_effects=pltpu.SideEffectType.DATAFLOW_SIDE_EFFECTING),
    )(*srcs, *lands, send_sems, recv_sems, after)
    me = 4 * lax.axis_index("x") + 2 * lax.axis_index("y") + lax.axis_index("c")
    full = []
    for x, land in zip(outs[:n], outs[n:]):
        own = lax.dynamic_slice(x, (me, 0, 0), (1,) + x.shape[1:]) if scatter else x[None]
        full.append(lax.dynamic_update_slice(land, own, (me, 0, 0)))
    return full


def _sum_slots(x, name):
    _, r, c = x.shape
    tr = _pick(r, (512, 256, 128, 64, 32, 16))

    def body(x_ref, o_ref):
        acc = x_ref[0].astype(F32)
        for s in range(1, N_DEV):
            acc = acc + x_ref[s].astype(F32)
        o_ref[...] = acc

    return pl.pallas_call(
        body, grid=(r // tr,), in_specs=[pl.BlockSpec((N_DEV, tr, c), lambda i: (0, i, 0))],
        out_specs=pl.BlockSpec((tr, c), lambda i: (i, 0)), out_shape=jax.ShapeDtypeStruct((r, c), F32), name=name,
        compiler_params=pltpu.CompilerParams(dimension_semantics=("arbitrary",)),
    )(x)


def _mod_fwd(c_all, w_ada, b_ada_mine):
    def body(c_ref, w_ref, b_ref, o_ref):
        o_ref[...] = _doth(_silu(c_ref[...]), w_ref[...]) + b_ref[...]

    return pl.pallas_call(body, out_shape=jax.ShapeDtypeStruct((c_all.shape[0], w_ada.shape[1]), F32), name="mod_fwd")(c_all, w_ada, b_ada_mine)


def _mod_bwd(c_all_t, dmod_mine):
    def body(ct_ref, d_ref, o_ref):
        s = _silu(ct_ref[...])
        acc = s[:, 0:1] * d_ref[pl.ds(0, 1), :]
        for b in range(1, N_DEV):
            acc = acc + s[:, b:b + 1] * d_ref[pl.ds(b, 1), :]
        o_ref[...] = acc

    return pl.pallas_call(body, out_shape=jax.ShapeDtypeStruct((c_all_t.shape[0], dmod_mine.shape[1]), F32), name="mod_bwd")(c_all_t, dmod_mine)


def _conv_fwd(proj, conv_w8, tm):
    t = proj.shape[0]
    ch = DN_CONV_CH

    def body(x_ref, w_ref, o_ref, buf):
        @pl.when(pl.program_id(0) == 0)
        def _():
            buf[pl.ds(0, CONV_HALO), :] = jnp.zeros((CONV_HALO, ch), F32)

        buf[pl.ds(CONV_HALO, tm), :] = x_ref[...]
        for c0 in range(0, ch, CONV_COLS):
            cols = pl.ds(c0, CONV_COLS)
            w = [w_ref[pl.ds(j, 1), cols] for j in range(CONV_K)]
            for r0 in range(0, tm, CONV_ROWS):
                acc = buf[pl.ds(r0 + CONV_HALO - (CONV_K - 1), CONV_ROWS), cols] * w[0]
                for j in range(1, CONV_K):
                    acc = acc + buf[pl.ds(r0 + CONV_HALO - (CONV_K - 1) + j, CONV_ROWS), cols] * w[j]
                o_ref[pl.ds(r0, CONV_ROWS), cols] = _silu(acc)
        buf[pl.ds(0, CONV_HALO), :] = buf[pl.ds(tm, CONV_HALO), :]

    return pl.pallas_call(
        body, grid=(t // tm,), in_specs=[pl.BlockSpec((tm, ch), lambda i: (i, 0)), _full(conv_w8.shape)],
        out_specs=pl.BlockSpec((tm, ch), lambda i: (i, 0)), out_shape=jax.ShapeDtypeStruct((t, ch), F32),
        scratch_shapes=[pltpu.VMEM((tm + CONV_HALO, ch), F32)], name="conv_fwd",
        compiler_params=pltpu.CompilerParams(dimension_semantics=("arbitrary",)),
    )(proj, conv_w8)


def _conv_bwd(proj, conv_w8, dact, tm):
    t = proj.shape[0]
    ch = DN_CONV_CH
    nt = t // tm
    hb = tm // CONV_HALO

    def body(x_ref, xp_ref, w_ref, dy_ref, dx_ref, dw_ref, xbuf, dbuf):
        step = pl.program_id(0)

        @pl.when(step == 0)
        def _():
            dbuf[pl.ds(tm, CONV_HALO), :] = jnp.zeros((CONV_HALO, ch), F32)
            dw_ref[...] = jnp.zeros_like(dw_ref)

        first = step == nt - 1
        xbuf[pl.ds(0, CONV_HALO), :] = jnp.where(first, 0.0, xp_ref[...])
        xbuf[pl.ds(CONV_HALO, tm), :] = x_ref[...]
        for c0 in range(0, ch, CONV_COLS):
            cols = pl.ds(c0, CONV_COLS)
            w = [w_ref[pl.ds(j, 1), cols] for j in range(CONV_K)]
            dw = [jnp.zeros((1, CONV_COLS), F32) for _ in range(CONV_K)]
            for r0 in range(0, tm, CONV_ROWS):
                xs = [xbuf[pl.ds(r0 + CONV_HALO - (CONV_K - 1) + j, CONV_ROWS), cols] for j in range(CONV_K)]
                pre = xs[0] * w[0]
                for j in range(1, CONV_K):
                    pre = pre + xs[j] * w[j]
                sg = _sigmoid(pre)
                dpre = dy_ref[pl.ds(r0, CONV_ROWS), cols] * (sg * (1.0 + pre * (1.0 - sg)))
                dbuf[pl.ds(r0, CONV_ROWS), cols] = dpre
                dw = [dw[j] + jnp.sum(dpre * xs[j], axis=0, keepdims=True) for j in range(CONV_K)]
            for j in range(CONV_K):
                dw_ref[pl.ds(j, 1), cols] += dw[j]
            for r0 in range(0, tm, CONV_ROWS):
                dx = dbuf[pl.ds(r0 + CONV_K - 1, CONV_ROWS), cols] * w[0]
                for j in range(1, CONV_K):
                    dx = dx + dbuf[pl.ds(r0 + CONV_K - 1 - j, CONV_ROWS), cols] * w[j]
                dx_ref[pl.ds(r0, CONV_ROWS), cols] = dx.astype(dx_ref.dtype)
        dbuf[pl.ds(tm, CONV_HALO), :] = dbuf[pl.ds(0, CONV_HALO), :]

    rev = lambda i: (nt - 1 - i, 0)
    prev = lambda i: (jnp.maximum((nt - 1 - i) * hb - 1, 0), 0)
    return pl.pallas_call(
        body, grid=(nt,),
        in_specs=[pl.BlockSpec((tm, ch), rev), pl.BlockSpec((CONV_HALO, ch), prev), _full(conv_w8.shape),
                  pl.BlockSpec((tm, ch), rev)],
        out_specs=[pl.BlockSpec((tm, ch), rev), _full(conv_w8.shape)],
        out_shape=[jax.ShapeDtypeStruct((t, ch), BF), jax.ShapeDtypeStruct(conv_w8.shape, F32)],
        scratch_shapes=[pltpu.VMEM((tm + CONV_HALO, ch), F32), pltpu.VMEM((tm + CONV_HALO, ch), F32)], name="conv_bwd",
        compiler_params=pltpu.CompilerParams(dimension_semantics=("arbitrary",)),
    )(proj, proj, conv_w8, dact)


BNN = (((2,), (1,)), ((0,), (0,)))
BNT = (((2,), (2,)), ((0,), (0,)))
BTN = (((1,), (1,)), ((0,), (0,)))


def _bdot(a, b, dims, precision=None):
    return lax.dot_general(a, b, dims, precision=precision, preferred_element_type=F32)


@jax.custom_vjp
def _bmmb_nt(a, b):
    return _bdot(a.astype(BF), b.astype(BF), BNT)


def _bmmb_nt_fwd(a, b):
    return _bmmb_nt(a, b), (a, b)


def _bmmb_nt_bwd(res, g):
    a, b = res
    gb = g.astype(BF)
    return _bdot(gb, b.astype(BF), BNN), _bdot(gb, a.astype(BF), BTN)


_bmmb_nt.defvjp(_bmmb_nt_fwd, _bmmb_nt_bwd)


@jax.custom_vjp
def _bmmb(a, b):
    return _bdot(a.astype(BF), b.astype(BF), BNN)


def _bmmb_fwd(a, b):
    return _bmmb(a, b), (a, b)


def _bmmb_bwd(res, g):
    a, b = res
    gb = g.astype(BF)
    return _bdot(gb, b.astype(BF), BNT), _bdot(a.astype(BF), gb, BTN)


_bmmb.defvjp(_bmmb_fwd, _bmmb_bwd)


@jax.custom_vjp
def _bmmb_tn(a, b):
    return _bdot(a.astype(BF), b.astype(BF), BTN)


def _bmmb_tn_fwd(a, b):
    return _bmmb_tn(a, b), (a, b)


def _bmmb_tn_bwd(res, g):
    a, b = res
    gb = g.astype(BF)
    return _bdot(b.astype(BF), gb, BNT), _bdot(a.astype(BF), gb, BNN)


_bmmb_tn.defvjp(_bmmb_tn_fwd, _bmmb_tn_bwd)


def _unit_lower_solve_fwd(a, r):
    c = a.shape[-1]
    ri = lax.broadcasted_iota(jnp.int32, a.shape, 1)
    ci = lax.broadcasted_iota(jnp.int32, a.shape, 2)
    xm = -a
    inv = (ri == ci).astype(F32) + xm
    for _ in range(int(math.log2(c)) - 1):
        xm = _bdot(xm, xm, BNN, HI)
        inv = inv + _bdot(inv, xm, BNN, HI)
    x = _bdot(inv, r, BNN, HI)
    return x, (inv, x)


def _unit_lower_solve_bwd(res, g):
    inv, x = res
    dr = _bdot(inv, g, BTN, HI)
    return -_bdot(dr, x, BNT, HI), dr


@jax.custom_vjp
def _unit_lower_solve_given(a, r, inv):
    return _bdot(inv, r, BNN, HI)


def _unit_lower_solve_given_fwd(a, r, inv):
    x = _bdot(inv, r, BNN, HI)
    return x, (inv, x)


def _unit_lower_solve_given_bwd(res, g):
    da, dr = _unit_lower_solve_bwd(res, g)
    return da, dr, jnp.zeros_like(res[0])


_unit_lower_solve_given.defvjp(_unit_lower_solve_given_fwd, _unit_lower_solve_given_bwd)


def _gdn_intra(qkv, ba, al8, dt8, inv4=None):
    tm = qkv.shape[0]
    nb = tm // CHUNK
    bsz = DN_HEADS * nb

    def heads(x0):
        return jnp.concatenate([qkv[:, x0 + h * LANE:x0 + (h + 1) * LANE].reshape(nb, CHUNK, LANE) for h in range(DN_HEADS)], axis=0)

    def spread(c0):
        return jnp.concatenate([jnp.broadcast_to(ba[:, c0 + h:c0 + h + 1], (tm, LANE)).reshape(nb, CHUNK, LANE)
                                for h in range(DN_HEADS)], axis=0)

    def per_head(v8):
        return jnp.concatenate([jnp.broadcast_to(v8[0:1, h:h + 1].reshape(1, 1, 1), (nb, 1, LANE)) for h in range(DN_HEADS)], axis=0)

    ri = lax.broadcasted_iota(jnp.int32, (bsz, CHUNK, CHUNK), 1)
    ci = lax.broadcasted_iota(jnp.int32, (bsz, CHUNK, CHUNK), 2)
    incl = ri >= ci
    strict = ri > ci

    q = _l2norm(heads(0)) * (DN_DK ** -0.5)
    k = _l2norm(heads(DN_QK))
    va = heads(2 * DN_QK)
    beta = _sigmoid(spread(0))
    g = -jnp.exp(per_head(al8)) * _softplus(spread(DN_HEADS) + per_head(dt8))
    gc = _bdot(incl.astype(F32), g, BNN, HI)
    g_last = jnp.sum(g, axis=1, keepdims=True)
    gcol = gc[:, :, :CHUNK]
    diff = gcol - jnp.swapaxes(gcol, 1, 2)
    decay = jnp.where(incl, jnp.exp(jnp.where(incl, diff, 0.0)), 0.0)
    kb = k * beta
    a_mat = jnp.where(strict, _bmmb_nt(kb, k) * decay, 0.0)
    egc = jnp.exp(gc)
    rhs = jnp.concatenate([kb * egc, va * beta], axis=2)
    if inv4 is None:
        wu, (inv, _) = _unit_lower_solve_fwd(a_mat, rhs)
    else:
        wu = _unit_lower_solve_given(a_mat, rhs, inv4.reshape(bsz, CHUNK, CHUNK))
    attn = jnp.where(incl, _bmmb_nt(q, k) * decay, 0.0)

    def unheads(x):
        return jnp.concatenate([x[h * nb:(h + 1) * nb].reshape(tm, LANE) for h in range(DN_HEADS)], axis=1)

    w_c, u_c = wu[:, :, :DN_DK], wu[:, :, DN_DK:]
    kd = k * jnp.exp(g_last - gc)
    out = (unheads(q * egc - _bmmb(attn, w_c)), unheads(_bmmb(attn, u_c)),
           _bmmb_tn(kd, w_c).reshape(DN_HEADS, nb, DN_DK, DN_DK), _bmmb_tn(kd, u_c).reshape(DN_HEADS, nb, DN_DK, DN_DV),
           jnp.broadcast_to(g_last, (bsz, GL_ROWS, LANE)).reshape(DN_HEADS, nb, GL_ROWS, LANE))
    return out if inv4 is not None else out + (inv.reshape(DN_HEADS, nb, CHUNK, CHUNK),)


def _gdn_scan_step(qp, op, c_mat, n_mat, gl, s):
    return _mmb(qp, s) + op, s * jnp.exp(gl) - _mmb(c_mat, s) + n_mat


def _gdn_intra_specs(t, tm, dts, order=lambda i: i):
    nb = tm // CHUNK
    row = pl.BlockSpec((tm, DN_VW), lambda i: (order(i), 0))
    mat = pl.BlockSpec((DN_HEADS, nb, DN_DK, DN_DV), lambda i: (0, order(i), 0, 0))
    row_shape = lambda d: jax.ShapeDtypeStruct((t, DN_VW), d)
    mat_shape = lambda d: jax.ShapeDtypeStruct((DN_HEADS, t // CHUNK, DN_DK, DN_DV), d)
    gl = pl.BlockSpec((DN_HEADS, nb, GL_ROWS, LANE), lambda i: (0, order(i), 0, 0))
    gl_shape = jax.ShapeDtypeStruct((DN_HEADS, t // CHUNK, GL_ROWS, LANE), dts[4])
    return [row, row, mat, mat, gl], [row_shape(dts[0]), row_shape(dts[1]), mat_shape(dts[2]), mat_shape(dts[3]), gl_shape]


def _gdn_intra_fwd(qkv, proj, al8, dt8, tm):
    t = qkv.shape[0]

    def body(qkv_ref, ba_ref, al_ref, dt_ref, *outs):
        for o, val in zip(outs, _gdn_intra(qkv_ref[...], ba_ref[...], al_ref[...], dt_ref[...])):
            o[...] = val.astype(o.dtype)

    specs, shapes = _gdn_intra_specs(t, tm, (BF, F32, BF, BF, F32))
    specs.append(_gdn_inverse_spec(tm))
    shapes.append(jax.ShapeDtypeStruct((DN_HEADS, t // CHUNK, CHUNK, CHUNK), F32))
    res = pl.pallas_call(
        body, grid=(t // tm,),
        in_specs=[pl.BlockSpec((tm, DN_CONV_CH), lambda i: (i, 0)), pl.BlockSpec((tm, LANE), lambda i: (i, P_BA // LANE)),
                  _full(al8.shape), _full(dt8.shape)],
        out_specs=specs, out_shape=shapes, name="gdn_intra_fwd",
        compiler_params=pltpu.CompilerParams(dimension_semantics=("parallel",)),
    )(qkv, proj, al8, dt8)
    return res[:5], res[5]


def _gdn_inverse_spec(tm):
    return pl.BlockSpec((DN_HEADS, tm // CHUNK, CHUNK, CHUNK), lambda i: (0, i, 0, 0))


def _gdn_intra_bwd(qkv, proj, al8, dt8, inverses, cts, tm):
    t = qkv.shape[0]

    def body(qkv_ref, ba_ref, al_ref, dt_ref, inv_ref, *refs):
        ct_refs, (dqkv_ref, dba_ref, dal_ref, ddt_ref) = refs[:5], refs[5:]

        @pl.when(pl.program_id(0) == 0)
        def _():
            dal_ref[...] = jnp.zeros_like(dal_ref)
            ddt_ref[...] = jnp.zeros_like(ddt_ref)

        _, vjp = jax.vjp(functools.partial(_gdn_intra, inv4=inv_ref[...]), qkv_ref[...], ba_ref[...], al_ref[...], dt_ref[...])
        dqkv, dba, dal, ddt = vjp(tuple(r[...].astype(F32) for r in ct_refs))
        dqkv_ref[...] = dqkv.astype(dqkv_ref.dtype)
        dba_ref[...] = dba.astype(dba_ref.dtype)
        dal_ref[...] += dal
        ddt_ref[...] += ddt

    specs, _ = _gdn_intra_specs(t, tm, (F32,) * 5)
    return pl.pallas_call(
        body, grid=(t // tm,),
        in_specs=[pl.BlockSpec((tm, DN_CONV_CH), lambda i: (i, 0)), pl.BlockSpec((tm, LANE), lambda i: (i, P_BA // LANE)),
                  _full(al8.shape), _full(dt8.shape), _gdn_inverse_spec(tm)] + specs,
        out_specs=[pl.BlockSpec((tm, DN_CONV_CH), lambda i: (i, 0)), pl.BlockSpec((tm, LANE), lambda i: (i, 0)),
                   _full(al8.shape), _full(dt8.shape)],
        out_shape=[jax.ShapeDtypeStruct((t, DN_CONV_CH), BF), jax.ShapeDtypeStruct((t, LANE), BF),
                   jax.ShapeDtypeStruct(al8.shape, F32), jax.ShapeDtypeStruct(dt8.shape, F32)],
        name="gdn_intra_bwd", compiler_params=pltpu.CompilerParams(dimension_semantics=("arbitrary",)),
    )(qkv, proj, al8, dt8, inverses, *cts)


def _gdn_scan_fwd(intra, tm):
    t = intra[0].shape[0]
    nb = tm // CHUNK
    nc = t // CHUNK

    def body(qp_ref, op_ref, c_ref, n_ref, gl_ref, o_ref, ss_ref, s_scr):
        @pl.when(pl.program_id(0) == 0)
        def _():
            s_scr[...] = jnp.zeros_like(s_scr)

        state = [s_scr[h] for h in range(DN_HEADS)]
        for cc in range(nb):
            rows = pl.ds(cc * CHUNK, CHUNK)
            for h in range(DN_HEADS):
                cols = pl.ds(h * DN_DV, DN_DV)
                ss_ref[cc, h] = state[h].astype(ss_ref.dtype)
                o_ref[rows, cols], state[h] = _gdn_scan_step(
                    qp_ref[rows, cols], op_ref[rows, cols], c_ref[h, cc], n_ref[h, cc], gl_ref[h, cc, pl.ds(0, 1), :], state[h])
        for h in range(DN_HEADS):
            s_scr[h] = state[h]

    specs, _ = _gdn_intra_specs(t, tm, (F32,) * 5)
    return pl.pallas_call(
        body, grid=(t // tm,), in_specs=specs,
        out_specs=[pl.BlockSpec((tm, DN_VW), lambda i: (i, 0)),
                   pl.BlockSpec((nb, DN_HEADS, DN_DK, DN_DV), lambda i: (i, 0, 0, 0))],
        out_shape=[jax.ShapeDtypeStruct((t, DN_VW), F32), jax.ShapeDtypeStruct((nc, DN_HEADS, DN_DK, DN_DV), BF)],
        scratch_shapes=[pltpu.VMEM((DN_HEADS, DN_DK, DN_DV), F32)], name="gdn_scan_fwd",
        compiler_params=pltpu.CompilerParams(dimension_semantics=("arbitrary",)),
    )(*intra)


def _gdn_scan_bwd(intra, states, do, tm):
    t = intra[0].shape[0]
    nb = tm // CHUNK
    ng = t // tm

    def body(qp_ref, op_ref, c_ref, n_ref, gl_ref, ss_ref, do_ref, dqp_ref, dop_ref, dc_ref, dn_ref, dgl_ref, ds_scr):
        @pl.when(pl.program_id(0) == 0)
        def _():
            ds_scr[...] = jnp.zeros_like(ds_scr)

        d_state = [ds_scr[h] for h in range(DN_HEADS)]
        for cc in reversed(range(nb)):
            rows = pl.ds(cc * CHUNK, CHUNK)
            for h in range(DN_HEADS):
                cols = pl.ds(h * DN_DV, DN_DV)
                _, vjp = jax.vjp(_gdn_scan_step, qp_ref[rows, cols].astype(F32), op_ref[rows, cols], c_ref[h, cc].astype(F32),
                                 n_ref[h, cc].astype(F32), gl_ref[h, cc, pl.ds(0, 1), :], ss_ref[cc, h].astype(F32))
                dqp_ref[rows, cols], dop_ref[rows, cols], dc, dn, dgl, d_state[h] = vjp((do_ref[rows, cols], d_state[h]))
                dc_ref[h, cc] = dc.astype(dc_ref.dtype)
                dn_ref[h, cc] = dn.astype(dn_ref.dtype)
                first_row = lax.broadcasted_iota(jnp.int32, (GL_ROWS, LANE), 0) == 0
                dgl_ref[h, cc] = jnp.where(first_row, dgl, 0.0)
        for h in range(DN_HEADS):
            ds_scr[h] = d_state[h]

    five, shapes = _gdn_intra_specs(t, tm, (F32, F32, BF, BF, F32), order=lambda i: ng - 1 - i)
    row = five[0]
    return pl.pallas_call(
        body, grid=(ng,),
        in_specs=five + [pl.BlockSpec((nb, DN_HEADS, DN_DK, DN_DV), lambda i: (ng - 1 - i, 0, 0, 0)), row],
        out_specs=five, out_shape=shapes,
        scratch_shapes=[pltpu.VMEM((DN_HEADS, DN_DK, DN_DV), F32)], name="gdn_scan_bwd",
        compiler_params=pltpu.CompilerParams(dimension_semantics=("arbitrary",)),
    )(*intra, states, do)


def _gdn_out(o, z, g):
    parts = []
    for h in range(DN_HEADS):
        sl = slice(h * DN_DV, (h + 1) * DN_DV)
        parts.append(_rmsnorm(o[:, sl], g) * _silu(z[:, sl]))
    return parts


_Q_SCALE = math.log2(math.e) / math.sqrt(QK_NOPE + QK_ROPE)


def _rope_tables(pos, inv_freq2):
    lane = lax.broadcasted_iota(jnp.int32, (1, LANE), 1)
    ang = pos * inv_freq2
    cos = jnp.where(lane < QK_ROPE, jnp.cos(ang), 0.0)
    sin = jnp.where(lane < QK_ROPE // 2, -jnp.sin(ang), jnp.where(lane < QK_ROPE, jnp.sin(ang), 0.0))
    return cos, sin


@jax.custom_vjp
def _rope_swap(u):
    lane = lax.broadcasted_iota(jnp.int32, u.shape, 1)
    half = QK_ROPE // 2
    return jnp.where(lane < half, pltpu.roll(u, LANE - half, 1), jnp.where(lane < QK_ROPE, pltpu.roll(u, half, 1), 0.0))


_rope_swap.defvjp(lambda u: (_rope_swap(u), None), lambda _, g: (_rope_swap(g),))


def _mla_prep(cq, ckv, kr, gq, gkv, w_uq, w_ukv, cos, sin):
    rope = lambda u: u * cos + _rope_swap(u) * sin
    q_lin = _mmb_nt(_rmsnorm(cq, gq), w_uq) * _Q_SCALE
    kv_lin = _mmb_nt(_rmsnorm(ckv, gkv), w_ukv)
    k_rope = rope(kr)
    qs, ks, vs = [], [], []
    for h in range(MLA_HEADS):
        qs += [q_lin[:, h * LANE:(h + 1) * LANE], rope(q_lin[:, (MLA_HEADS + h) * LANE:(MLA_HEADS + h + 1) * LANE])]
        ks += [kv_lin[:, 2 * h * LANE:(2 * h + 1) * LANE], k_rope]
        vs += [kv_lin[:, (2 * h + 1) * LANE:(2 * h + 2) * LANE]]
    return qs + ks + vs


def _mla_prep_fwd(proj, pos_col, inv_freq2, gq, gkv, w_uq, w_ukv, tm):
    t = proj.shape[0]
    nq = 2 * MLA_HEADS

    def body(cq_ref, ckv_ref, kr_ref, pos_ref, f_ref, gq_ref, gkv_ref, wq_ref, wkv_ref, q_ref, k_ref, v_ref):
        cos, sin = _rope_tables(pos_ref[...], f_ref[...])
        outs = _mla_prep(cq_ref[...], ckv_ref[...], kr_ref[...], gq_ref[...], gkv_ref[...], wq_ref[...], wkv_ref[...],
                         cos, sin)
        for i in range(nq):
            q_ref[:, pl.ds(i * LANE, LANE)] = outs[i].astype(q_ref.dtype)
            k_ref[:, pl.ds(i * LANE, LANE)] = outs[nq + i].astype(k_ref.dtype)
        for h in range(MLA_HEADS):
            v_ref[:, pl.ds(h * LANE, LANE)] = outs[2 * nq + h].astype(v_ref.dtype)

    row = lambda w, j: pl.BlockSpec((tm, w), functools.partial(lambda i, j: (i, j), j=j))
    return pl.pallas_call(
        body, grid=(t // tm,),
        in_specs=[row(Q_LORA, P_CQ // Q_LORA), row(KV_LORA, P_CKV // KV_LORA), row(LANE, P_KR // LANE),
                  pl.BlockSpec((tm, 1), lambda i: (i, 0)), _full(inv_freq2.shape), _full(gq.shape), _full(gkv.shape),
                  _full(w_uq.shape), _full(w_ukv.shape)],
        out_specs=[row(nq * LANE, 0), row(nq * LANE, 0), row(MLA_VW, 0)],
        out_shape=[jax.ShapeDtypeStruct((t, nq * LANE), BF), jax.ShapeDtypeStruct((t, nq * LANE), BF),
                   jax.ShapeDtypeStruct((t, MLA_VW), BF)],
        name="mla_prep_fwd", compiler_params=pltpu.CompilerParams(dimension_semantics=("arbitrary",)),
    )(proj, proj, proj, pos_col, inv_freq2, gq, gkv, w_uq, w_ukv)


def _mla_prep_bwd(proj, pos_col, inv_freq2, gq, gkv, w_uq, w_ukv, dq, dk, dv, tm):
    t = proj.shape[0]
    nq = 2 * MLA_HEADS

    def body(cq_ref, ckv_ref, kr_ref, pos_ref, f_ref, gq_ref, gkv_ref, wq_ref, wkv_ref, dq_ref, dk_ref, dv_ref,
             dcq_ref, dckv_ref, dkr_ref, dgq_ref, dgkv_ref, dwq_ref, dwkv_ref):
        @pl.when(pl.program_id(0) == 0)
        def _():
            for o in (dgq_ref, dgkv_ref, dwq_ref, dwkv_ref):
                o[...] = jnp.zeros_like(o)

        cos, sin = _rope_tables(pos_ref[...], f_ref[...])
        f = functools.partial(_mla_prep, cos=cos, sin=sin)
        _, vjp = jax.vjp(f, cq_ref[...], ckv_ref[...], kr_ref[...], gq_ref[...], gkv_ref[...], wq_ref[...], wkv_ref[...])
        cts = [dq_ref[:, pl.ds(i * LANE, LANE)] for i in range(nq)]
        cts += [dk_ref[:, pl.ds(i * LANE, LANE)] for i in range(nq)]
        cts += [dv_ref[:, pl.ds(h * LANE, LANE)] for h in range(MLA_HEADS)]
        dcq, dckv, dkr, dgq, dgkv, dwq, dwkv = vjp(cts)
        dcq_ref[...] = dcq.astype(dcq_ref.dtype)
        dckv_ref[...] = dckv.astype(dckv_ref.dtype)
        dkr_ref[...] = dkr.astype(dkr_ref.dtype)
        dgq_ref[...] += dgq
        dgkv_ref[...] += dgkv
        dwq_ref[...] += dwq
        dwkv_ref[...] += dwkv

    row = lambda w, j: pl.BlockSpec((tm, w), functools.partial(lambda i, j: (i, j), j=j))
    return pl.pallas_call(
        body, grid=(t // tm,),
        in_specs=[row(Q_LORA, P_CQ // Q_LORA), row(KV_LORA, P_CKV // KV_LORA), row(LANE, P_KR // LANE),
                  pl.BlockSpec((tm, 1), lambda i: (i, 0)), _full(inv_freq2.shape), _full(gq.shape), _full(gkv.shape),
                  _full(w_uq.shape), _full(w_ukv.shape), row(nq * LANE, 0), row(nq * LANE, 0), row(MLA_VW, 0)],
        out_specs=[row(Q_LORA, 0), row(KV_LORA, 0), row(LANE, 0), _full(gq.shape), _full(gkv.shape),
                   _full(w_uq.shape), _full(w_ukv.shape)],
        out_shape=[jax.ShapeDtypeStruct((t, Q_LORA), BF), jax.ShapeDtypeStruct((t, KV_LORA), BF),
                   jax.ShapeDtypeStruct((t, LANE), BF), jax.ShapeDtypeStruct(gq.shape, F32),
                   jax.ShapeDtypeStruct(gkv.shape, F32), jax.ShapeDtypeStruct(w_uq.shape, F32),
                   jax.ShapeDtypeStruct(w_ukv.shape, F32)],
        name="mla_prep_bwd", compiler_params=pltpu.CompilerParams(dimension_semantics=("arbitrary",)),
    )(proj, proj, proj, pos_col, inv_freq2, gq, gkv, w_uq, w_ukv, dq, dk, dv)


_NEG = -1e30
_LN2 = math.log(2.0)
ATT_CHAINS = 2


def _causal(tq, tk, q0, k0):
    row = q0 + lax.broadcasted_iota(jnp.int32, (tq, tk), 0)
    col = k0 + lax.broadcasted_iota(jnp.int32, (tq, tk), 1)
    return col <= row


def _attn_fwd(q, k, v, tq, tk):
    t = q.shape[0]

    assert tk % tq == 0 or tq % tk == 0
    n_diag = max(1, tq // tk)

    th = tq // ATT_CHAINS

    def body(q_ref, k_ref, v_ref, o_ref, lse_ref):
        i = pl.program_id(1)
        n_full = (i * tq) // tk

        def step(k0, carry, masked):
            kt = k_ref[pl.ds(k0, tk), :]
            vt = v_ref[pl.ds(k0, tk), :]
            out = []
            for c, (m, l, acc) in enumerate(carry):
                s = _dot(q_ref[pl.ds(c * th, th), :], kt, NT)
                if masked:
                    s = jnp.where(_causal(th, tk, i * tq + c * th, k0), s, _NEG)
                m_new = jnp.maximum(m, jnp.max(s, axis=-1, keepdims=True))
                p = jnp.exp2(s - m_new)
                alpha = jnp.exp2(m - m_new)
                out.append((m_new, alpha * l + jnp.sum(p, axis=-1, keepdims=True), alpha * acc + _dot(p.astype(BF), vt)))
            return tuple(out)

        init = tuple((jnp.full((th, 1), _NEG, F32), jnp.zeros((th, 1), F32), jnp.zeros((th, V_HEAD), F32)) for _ in range(ATT_CHAINS))
        carry = lax.fori_loop(0, n_full, lambda j, c: step(pl.multiple_of(j * tk, tk), c, False), init)
        for dd in range(n_diag):
            carry = step(pl.multiple_of((n_full + dd) * tk, tk), carry, True)
        for c, (m, l, acc) in enumerate(carry):
            o_ref[pl.ds(c * th, th), :] = acc / l
            lse_ref[pl.ds(c * th, th), :] = jnp.broadcast_to(m + jnp.log2(l), (th, LANE))

    return pl.pallas_call(
        body, grid=(MLA_HEADS, t // tq),
        in_specs=[pl.BlockSpec((tq, 2 * LANE), lambda h, i: (i, h)), pl.BlockSpec((t, 2 * LANE), lambda h, i: (0, h)),
                  pl.BlockSpec((t, V_HEAD), lambda h, i: (0, h))],
        out_specs=[pl.BlockSpec((tq, V_HEAD), lambda h, i: (i, h)), pl.BlockSpec((tq, LANE), lambda h, i: (i, h))],
        out_shape=[jax.ShapeDtypeStruct((t, MLA_VW), F32), jax.ShapeDtypeStruct((t, MLA_HEADS * LANE), F32)],
        name="attn_fwd", compiler_params=pltpu.CompilerParams(dimension_semantics=("parallel", "arbitrary")),
    )(q, k, v)


def _attn_bwd(q, k, v, do, lse, delta, tq, tk):
    t = q.shape[0]
    nkt = t // tk
    assert tk % tq == 0

    def body(q_ref, k_ref, v_ref, do_ref, lse_ref, dl_ref, dq_ref, dk_ref, dv_ref):
        j = pl.program_id(1)

        @pl.when(j == 0)
        def _():
            dq_ref[...] = jnp.zeros_like(dq_ref)

        kt = k_ref[...]
        vt = v_ref[...]

        def step(q0, carry, masked):
            dk, dv = carry
            rows = pl.ds(q0, tq)
            qt = q_ref[rows, :]
            dot_ = do_ref[rows, :]
            p = jnp.exp2(_dot(qt, kt, NT) - lse_ref[rows, pl.ds(0, 1)])
            if masked:
                p = jnp.where(_causal(tq, tk, q0, j * tk), p, 0.0)
            dv = dv + _dot(p.astype(BF), dot_, TN)
            ds = (p * (_dot(dot_, vt, NT) - dl_ref[rows, pl.ds(0, 1)])).astype(BF)
            dk = dk + _dot(ds, qt, TN)
            dq_ref[rows, :] += _dot(ds, kt)
            return dk, dv

        per = tk // tq
        carry = (jnp.zeros((tk, 2 * LANE), F32), jnp.zeros((tk, V_HEAD), F32))
        for dd in range(per):
            carry = step(pl.multiple_of(j * tk + dd * tq, tq), carry, True)

        def group(g, c):
            for dd in range(per):
                c = step(pl.multiple_of(g * tk + dd * tq, tq), c, False)
            return c

        dk, dv = lax.fori_loop(j + 1, nkt, group, carry)
        dk_ref[...] = dk * _LN2
        dv_ref[...] = dv

        @pl.when(j == nkt - 1)
        def _():
            dq_ref[...] = dq_ref[...] * _LN2

    return pl.pallas_call(
        body, grid=(MLA_HEADS, nkt),
        in_specs=[pl.BlockSpec((t, 2 * LANE), lambda h, j: (0, h)), pl.BlockSpec((tk, 2 * LANE), lambda h, j: (j, h)),
                  pl.BlockSpec((tk, V_HEAD), lambda h, j: (j, h)), pl.BlockSpec((t, V_HEAD), lambda h, j: (0, h)),
                  pl.BlockSpec((t, LANE), lambda h, j: (0, h)), pl.BlockSpec((t, LANE), lambda h, j: (0, h))],
        out_specs=[pl.BlockSpec((t, 2 * LANE), lambda h, j: (0, h)), pl.BlockSpec((tk, 2 * LANE), lambda h, j: (j, h)),
                   pl.BlockSpec((tk, V_HEAD), lambda h, j: (j, h))],
        out_shape=[jax.ShapeDtypeStruct((t, MLA_HEADS * 2 * LANE), F32), jax.ShapeDtypeStruct((t, MLA_HEADS * 2 * LANE), F32),
                   jax.ShapeDtypeStruct((t, MLA_VW), F32)],
        name="attn_bwd", compiler_params=pltpu.CompilerParams(dimension_semantics=("parallel", "arbitrary")),
    )(q, k, v, do, lse, delta)


def _adam_update(w, g, m, v):
    mm = ADAM_B1 * m + (1.0 - ADAM_B1) * g
    vv = ADAM_B2 * v + (1.0 - ADAM_B2) * jnp.square(g)
    m_hat = mm / (1.0 - ADAM_B1 ** ADAM_STEP)
    v_hat = vv / (1.0 - ADAM_B2 ** ADAM_STEP)
    return -ADAM_LR * (m_hat / (jnp.sqrt(v_hat) + ADAM_EPS) + ADAM_WD * w), mm, vv


def _adamw(w, g, m, v, name):
    r, c = w.shape
    tr = max([r // s for s in range(1, r // 8 + 1) if r % s == 0 and (r // s) % 8 == 0 and r // s <= 256] or [r])
    slots = g.ndim == 3

    def body(w_ref, g_ref, m_ref, v_ref, g_out, d_ref, nm_ref, nv_ref):
        if slots:
            gg = g_ref[0].astype(F32)
            for s in range(1, N_DEV):
                gg = gg + g_ref[s].astype(F32)
        else:
            gg = g_ref[...]
        g_out[...] = gg
        d_ref[...], nm_ref[...], nv_ref[...] = _adam_update(w_ref[...], gg, m_ref[...], v_ref[...])

    spec = pl.BlockSpec((tr, c), lambda i: (i, 0))
    g_spec = pl.BlockSpec((N_DEV, tr, c), lambda i: (0, i, 0)) if slots else spec
    return pl.pallas_call(
        body, grid=(r // tr,), in_specs=[spec, g_spec, spec, spec], out_specs=[spec] * 4,
        out_shape=[jax.ShapeDtypeStruct((r, c), F32)] * 4, name=name,
        compiler_params=pltpu.CompilerParams(dimension_semantics=("arbitrary",)),
    )(w, g, m, v)


def _adamw_many(ws, gs, ms, vs, name):
    n = len(ws)

    def body(*refs):
        for i in range(n):
            w_ref, g_ref, m_ref, v_ref = (refs[j * n + i] for j in range(4))
            d_ref, nm_ref, nv_ref = (refs[(4 + j) * n + i] for j in range(3))
            d_ref[...], nm_ref[...], nv_ref[...] = _adam_update(w_ref[...], g_ref[...], m_ref[...], v_ref[...])

    shapes = [jax.ShapeDtypeStruct(w.shape, F32) for w in ws]
    outs = pl.pallas_call(body, out_shape=shapes * 3, name=name)(*ws, *gs, *ms, *vs)
    return outs[:n], outs[n:2 * n], outs[2 * n:]


def _cast_bf16(xs, name, after=None):
    n = len(xs)
    extra = [] if after is None else [after]

    def body(*refs):
        outs = refs[n + len(extra):]
        for i in range(n):
            outs[i][...] = refs[i][...].astype(BF)

    vmem = pl.BlockSpec(memory_space=pltpu.VMEM)
    return pl.pallas_call(
        body, out_shape=[jax.ShapeDtypeStruct(x.shape, BF) for x in xs], name=name,
        in_specs=[vmem] * n + [pl.BlockSpec(memory_space=pl.ANY)] * len(extra), out_specs=[vmem] * n)(*xs, *extra)


def _pad_rows(a, n):
    return jnp.pad(a, ((0, n - a.shape[0]), (0, 0)))


def _w_in_to_padded(wt):
    s_ba = P_CQ
    s_cq = s_ba + 2 * DN_HEADS
    s_kr = s_cq + Q_LORA + KV_LORA
    return jnp.concatenate([wt[:s_ba], wt[s_cq:s_kr], _pad_rows(wt[s_ba:s_cq], LANE), _pad_rows(wt[s_kr:], LANE)], axis=0)


def _w_in_from_padded(wt):
    return jnp.concatenate([wt[:P_CQ], wt[P_BA:P_BA + 2 * DN_HEADS], wt[P_CQ:P_BA], wt[P_KR:P_KR + QK_ROPE]], axis=0)


def _w_uq_to_padded(wt):
    w3 = wt.reshape(MLA_HEADS, QK_NOPE + QK_ROPE, Q_LORA)
    nope = w3[:, :QK_NOPE].reshape(MLA_HEADS * QK_NOPE, Q_LORA)
    rope = jnp.pad(w3[:, QK_NOPE:], ((0, 0), (0, LANE - QK_ROPE), (0, 0))).reshape(MLA_HEADS * LANE, Q_LORA)
    return jnp.concatenate([nope, rope], axis=0)


def _w_uq_from_padded(wt):
    nope = wt[:MLA_HEADS * QK_NOPE].reshape(MLA_HEADS, QK_NOPE, Q_LORA)
    rope = wt[MLA_HEADS * QK_NOPE:].reshape(MLA_HEADS, LANE, Q_LORA)[:, :QK_ROPE]
    return jnp.concatenate([nope, rope], axis=1).reshape(MLA_HEADS * (QK_NOPE + QK_ROPE), Q_LORA)


def _pack(pieces, width, row_mult):
    flat = jnp.concatenate([p.reshape(-1) for p in pieces])
    n = flat.shape[0]
    rows = -(-n // (width * row_mult)) * row_mult
    return jnp.pad(flat, (0, rows * width - n)).reshape(rows, width)


def _unpack(flat, shapes):
    out, o = [], 0
    for s in shapes:
        n = math.prod(s)
        out.append(flat[o:o + n].reshape(s))
        o += n
    return out


def kernel(x, c, positions, w_ada, b_ada, w_in, conv_w, a_log, dt_bias, dn_norm_g, q_norm_g, w_uq, kv_norm_g, w_ukv, w_o, ln1_g, ln1_b, w_gate, w_up, w_down, ln2_g, ln2_b, loss_target, m_w_ada, m_b_ada, m_w_in, m_conv_w, m_a_log, m_dt_bias, m_dn_norm_g, m_q_norm_g, m_w_uq, m_kv_norm_g, m_w_ukv, m_w_o, m_ln1_g, m_ln1_b, m_w_gate, m_w_up, m_w_down, m_ln2_g, m_ln2_b, v_w_ada, v_b_ada, v_w_in, v_conv_w, v_a_log, v_dt_bias, v_dn_norm_g, v_q_norm_g, v_w_uq, v_kv_norm_g, v_w_ukv, v_w_o, v_ln1_g, v_ln1_b, v_w_gate, v_w_up, v_w_down, v_ln2_g, v_ln2_b):
    me = 4 * lax.axis_index("x") + 2 * lax.axis_index("y") + lax.axis_index("c")
    t, d = x.shape[1], x.shape[2]
    ada_n = w_ada.shape[2]

    tr = lambda w: w[0].T
    rows = lambda a: a.reshape(-1, a.shape[2])
    (in_shard,) = _cast_bf16([tr(w_in)], "cast_w_in")
    cw = conv_w.shape[3]
    a_in, c_all, conv_all = _gather_by_chip([in_shard, c, conv_w[0, :, 0, :]], "gather_w_in_and_small")
    c_all = c_all.reshape(N_DEV, d)
    conv_full = conv_all.transpose(1, 0, 2).reshape(CONV_K, N_DEV * cw)
    conv_w8 = jnp.pad(conv_full, ((0, 8 - CONV_K), (0, 0)))

    b_ada_mine = lax.dynamic_slice(b_ada, (0, me * ada_n), (1, ada_n))
    mod_cols = _mod_fwd(c_all, w_ada[0], b_ada_mine)
    (mod_all,) = _exchange([mod_cols.reshape(N_DEV, 1, ada_n)], "scatter_mod", scatter=True)
    mod = mod_all.reshape(1, N_DEV * ada_n)

    later = _cast_bf16([tr(w_uq), tr(w_ukv), w_o[0], tr(w_gate), tr(w_up), w_down[0]], "cast_weights", after=mod)
    mixer_gather, token_a = _exchange_start(later[:3], "gather_mixer_weights_start", scatter=False)
    ffn_gather, token_b = _exchange_start(later[3:], "gather_ffn_weights_start", scatter=False)
    mod = mod + (token_a + token_b)
    w_in_t = _w_in_to_padded(rows(a_in))

    def mixer_weights(after):
        a_uq, a_ukv, a_o = _exchange_wait(mixer_gather, after, "gather_mixer_weights_wait", scatter=False)
        return _w_uq_to_padded(rows(a_uq)), rows(a_ukv), rows(a_o)

    def ffn_weights(after):
        a_gate, a_up, a_down = _exchange_wait(ffn_gather, after, "gather_ffn_weights_wait", scatter=False)
        return rows(a_gate), rows(a_up), rows(a_down)

    def by_dest(g):
        return g.reshape(N_DEV, -1, g.shape[1])

    scatters = {}

    def grads_ready(tag, *g):
        if tag == "ffn":
            pieces = [by_dest(a) for a in g]
        elif tag == "mixer":
            g_w_o, g_w_uq_t, g_w_ukv_t = g
            pieces = [by_dest(g_w_o), by_dest(_w_uq_from_padded(g_w_uq_t).astype(BF)), by_dest(g_w_ukv_t.astype(BF))]
        else:
            pieces = [by_dest(_w_in_from_padded(g[0]))]
        scatters[tag], token = _exchange_start(pieces, "scatter_%s_grads_start" % tag, scatter=True)
        return token

    loc = _local_step(x[0], loss_target[0], positions[0], mod, w_in_t, mixer_weights, ffn_weights, grads_ready,
                      conv_w8, a_log, dt_bias, dn_norm_g, q_norm_g, kv_norm_g, ln1_g, ln1_b, ln2_g, ln2_b)
    grad_x, loss_acc, dmod, d_conv8, d_al8, d_dt8, d_dn_g, d_q_g, d_kv_g, d_ln1_g, d_ln1_b, d_ln2_g, d_ln2_b = loc

    small_shapes = [(6 * d,), (CONV_K, N_DEV * cw), (DN_HEADS,), (DN_HEADS,), (DN_DV,), (Q_LORA,), (KV_LORA,), (d,), (d,), (d,), (d,), (1,)]
    gsmall = _pack([dmod, d_conv8[:CONV_K], d_al8[0, :DN_HEADS], d_dt8[0, :DN_HEADS], d_dn_g, d_q_g, d_kv_g,
                    d_ln1_g, d_ln1_b, d_ln2_g, d_ln2_b, loss_acc[0, :1]], LANE, 8)
    (gsmall_all,) = _exchange([gsmall], "gather_small_grads", scatter=False)
    dmod_all = gsmall_all.reshape(N_DEV, -1)[:, :6 * d]
    tot = _unpack(_sum_slots(gsmall_all, "sum_small_grads").reshape(-1), small_shapes)
    g_b_ada, g_conv_full, g_a_log, g_dt_bias, g_dn_g, g_q_g, g_kv_g, g_ln1_g, g_ln1_b, g_ln2_g, g_ln2_b, loss1 = tot
    loss = loss1.reshape(())
    g_conv_w = lax.dynamic_slice(g_conv_full, (0, me * cw), (CONV_K, cw))
    g_w_ada = _mod_bwd(c_all.T, lax.dynamic_slice(dmod_all, (0, me * ada_n), (N_DEV, ada_n)))

    grads = {"w_ada": g_w_ada[None], "b_ada": g_b_ada[None], "conv_w": g_conv_w[None, :, None, :],
             "a_log": g_a_log[None], "dt_bias": g_dt_bias[None], "dn_norm_g": g_dn_g[None], "q_norm_g": g_q_g[None],
             "kv_norm_g": g_kv_g[None], "ln1_g": g_ln1_g[None], "ln1_b": g_ln1_b[None], "ln2_g": g_ln2_g[None], "ln2_b": g_ln2_b[None]}
    weights = dict(w_ada=w_ada, b_ada=b_ada, w_in=w_in, conv_w=conv_w, a_log=a_log, dt_bias=dt_bias, dn_norm_g=dn_norm_g,
                   q_norm_g=q_norm_g, w_uq=w_uq, kv_norm_g=kv_norm_g, w_ukv=w_ukv, w_o=w_o, ln1_g=ln1_g, ln1_b=ln1_b,
                   w_gate=w_gate, w_up=w_up, w_down=w_down, ln2_g=ln2_g, ln2_b=ln2_b)
    ms = dict(w_ada=m_w_ada, b_ada=m_b_ada, w_in=m_w_in, conv_w=m_conv_w, a_log=m_a_log, dt_bias=m_dt_bias,
              dn_norm_g=m_dn_norm_g, q_norm_g=m_q_norm_g, w_uq=m_w_uq, kv_norm_g=m_kv_norm_g, w_ukv=m_w_ukv, w_o=m_w_o,
              ln1_g=m_ln1_g, ln1_b=m_ln1_b, w_gate=m_w_gate, w_up=m_w_up, w_down=m_w_down, ln2_g=m_ln2_g, ln2_b=m_ln2_b)
    vs = dict(w_ada=v_w_ada, b_ada=v_b_ada, w_in=v_w_in, conv_w=v_conv_w, a_log=v_a_log, dt_bias=v_dt_bias,
              dn_norm_g=v_dn_norm_g, q_norm_g=v_q_norm_g, w_uq=v_w_uq, kv_norm_g=v_kv_norm_g, w_ukv=v_w_ukv, w_o=v_w_o,
              ln1_g=v_ln1_g, ln1_b=v_ln1_b, w_gate=v_w_gate, w_up=v_w_up, w_down=v_w_down, ln2_g=v_ln2_g, ln2_b=v_ln2_b)
    names = list(weights)
    big = ("w_ada", "w_gate", "w_up", "w_down", "w_o", "w_uq", "w_ukv", "w_in")
    waits = {"w_gate": ("ffn", ("w_gate", "w_up", "w_down")), "w_o": ("mixer", ("w_o", "w_uq", "w_ukv")), "w_in": ("in", ("w_in",))}
    delta_w, new_m, new_v, slots = {}, {}, {}, {}
    last = g_w_ada
    for n in big:
        if n == "w_in":
            rest = [r for r in names if r not in big]
            flat2 = lambda a: a.reshape(-1, a.shape[-1])
            outs = _adamw_many(*[[flat2(src[r]) for r in rest] for src in (weights, grads, ms, vs)], "adamw_small")
            for dst, o in zip((delta_w, new_m, new_v), outs):
                for r, a in zip(rest, o):
                    dst[r] = a.reshape(weights[r].shape)
            last = outs[0][0]
        transposed = n in ("w_in", "w_uq", "w_ukv", "w_gate", "w_up")
        two = (lambda a: a[0].T) if transposed else (lambda a: a[0])
        back = (lambda a: a.T[None]) if transposed else (lambda a: a[None])
        if n in waits:
            tag, members = waits[n]
            slots.update(zip(members, _exchange_wait(scatters[tag], last, "scatter_%s_grads_wait" % tag, scatter=True)))
        g_in = slots[n] if n in slots else two(grads[n])
        gr, dlt, nm, nv = _adamw(two(weights[n]), g_in, two(ms[n]), two(vs[n]), "adamw_" + n)
        grads[n], delta_w[n], new_m[n], new_v[n] = back(gr), back(dlt), back(nm), back(nv)
        last = nv

    return (loss, grad_x[None], *[grads[n] for n in names], *[delta_w[n] for n in names],
            *[new_m[n] for n in names], *[new_v[n] for n in names])


def _local_step(xs, tgt, pos, mod, w_in_t, mixer_weights, ffn_weights, grads_ready, conv_w8,
                a_log, dt_bias, dn_norm_g, q_norm_g, kv_norm_g, ln1_g, ln1_b, ln2_g, ln2_b):
    t, d = xs.shape
    sh_m, sc_m, gt_m, sh_f, sc_f, gt_f = [mod[:, i * d:(i + 1) * d] for i in range(6)]
    pos_col = pos.astype(F32).reshape(t, 1)
    inv_freq = 1.0 / (ROPE_THETA ** (jnp.arange(0, QK_ROPE, 2, dtype=F32) / QK_ROPE))
    inv_freq2 = jnp.pad(jnp.concatenate([inv_freq, inv_freq]), (0, LANE - QK_ROPE)).reshape(1, LANE)
    al8 = jnp.pad(a_log, ((0, 7), (0, LANE - DN_HEADS)))
    dt8 = jnp.pad(dt_bias, ((0, 7), (0, LANE - DN_HEADS)))

    tm = min(512, t)
    tq = min(256, t)
    tk = min(512, t)

    (h1,) = _rowwise("modulate_in", lambda xx, sc, sh: xx * (1.0 + sc) + sh, [xs], [sc_m, sh_m], [(d, BF)], [], tm)
    proj = _matmul(h1, w_in_t, "nt", "in_proj")
    qkv = _conv_fwd(proj, conv_w8, min(256, t))
    gdn_tm = min(512, t)
    intra, inverses = _gdn_intra_fwd(qkv, proj, al8, dt8, gdn_tm)
    o_dn, states = _gdn_scan_fwd(intra, gdn_tm)
    w_uq_t, w_ukv_t, w_o_f = mixer_weights(states)
    qc, kc, vc = _mla_prep_fwd(proj, pos_col, inv_freq2, q_norm_g, kv_norm_g, w_uq_t, w_ukv_t, tm)
    o_mla, lse = _attn_fwd(qc, kc, vc, min(1024, t), min(1024, t))

    def mix_in(o, z, om, g):
        return jnp.concatenate(_gdn_out(o, z, g) + [om], axis=1)

    (mixin,) = _rowwise("mixer_out", mix_in, [o_dn, (proj, DN_VW, P_Z // DN_VW), o_mla], [dn_norm_g], [(2 * DN_VW, BF)], [], tm)
    mix = _matmul(mixin, w_o_f, "nn", "out_proj", BF)

    def block1(xx, mx, gt, g1, b1, sc, sh):
        x1 = _layernorm(DEEPNORM_ALPHA * xx + gt * mx, g1, b1)
        return x1, x1 * (1.0 + sc) + sh

    x1, h2 = _rowwise("norm1_modulate", block1, [xs, mix], [gt_m, ln1_g, ln1_b, sc_f, sh_f], [(d, F32), (d, BF)], [], tm)
    w_gate_f, w_up_f, w_down_f = ffn_weights(h2)
    act, gate, up = _ffn_in(h2, w_gate_f, w_up_f)
    ff = _matmul(act, w_down_f, "nn", "ffn_out", BF)

    def tail_loss(x1_, ff_, gt, g2, b2, tg):
        y = _layernorm(DEEPNORM_ALPHA * x1_ + gt * ff_, g2, b2)
        return 0.5 * jnp.sum(jnp.mean(jnp.square(y - tg), axis=-1))

    def tail(x1_, ff_, tg, gt, g2, b2):
        loss, (dx1, dff, dgt, dg2, db2) = jax.value_and_grad(tail_loss, argnums=(0, 1, 2, 3, 4))(x1_, ff_, gt, g2, b2, tg)
        return dx1, dff, jnp.full((1, LANE), loss, F32), dgt, dg2, db2

    dx1_a, dff, loss_acc, d_gt_f, d_ln2_g, d_ln2_b = _rowwise(
        "norm2_loss", tail, [x1, ff, tgt], [gt_f, ln2_g, ln2_b], [(d, BF), (d, BF)], [(1, LANE), (1, d), (1, d), (1, d)], tm)

    g_w_down = _matmul(act, dff, "tn", "d_w_down", BF)
    dgate, dup = _ffn_act_bwd(dff, w_down_f, gate, up)
    g_w_gate = _matmul(dgate, h2, "tn", "d_w_gate", BF)
    g_w_up = _matmul(dup, h2, "tn", "d_w_up", BF)
    token = grads_ready("ffn", g_w_gate, g_w_up, g_w_down)
    dh2 = _matmul2_nn(dgate, w_gate_f, dup, w_up_f, "d_ffn_in", BF)

    def block1_bwd(xx, mx, dx1_, dh2_, gt, g1, b1, sc, sh):
        _, vjp = jax.vjp(block1, xx, mx, gt, g1, b1, sc, sh)
        dxx, dmx, dgt, dg1, db1, dsc, dsh = vjp((dx1_.astype(F32), dh2_.astype(F32)))
        return dxx, dmx, dgt, dg1, db1, dsc, dsh

    dx_a, dmix, d_gt_m, d_ln1_g, d_ln1_b, d_sc_f, d_sh_f = _rowwise(
        "norm1_modulate_bwd", block1_bwd, [xs, mix, dx1_a, dh2], [gt_m + token, ln1_g, ln1_b, sc_f, sh_f],
        [(d, F32), (d, BF)], [(1, d)] * 5, min(256, t))

    dmixin = _matmul(dmix, w_o_f, "nt", "d_mixer_out", BF)
    g_w_o = _matmul(mixin, dmix, "tn", "d_w_o", BF)

    def mixer_bwd(o, z, om, dmi, g):
        _, vjp = jax.vjp(lambda o_, z_, g_: jnp.concatenate(_gdn_out(o_, z_, g_), axis=1), o, z, g)
        do_, dz_, dg_ = vjp(dmi[:, :DN_VW].astype(F32))
        dom = dmi[:, DN_VW:]
        delta = [jnp.broadcast_to(jnp.sum(dom[:, h * V_HEAD:(h + 1) * V_HEAD] * om[:, h * V_HEAD:(h + 1) * V_HEAD], axis=-1, keepdims=True), (o.shape[0], LANE))
                 for h in range(MLA_HEADS)]
        return do_, dz_, dom, jnp.concatenate(delta, axis=1), dg_

    do_dn, dz, do_mla, delta, d_dn_g = _rowwise(
        "mixer_out_bwd", mixer_bwd, [o_dn, (proj, DN_VW, P_Z // DN_VW), o_mla, dmixin], [dn_norm_g],
        [(DN_VW, F32), (DN_VW, BF), (MLA_VW, BF), (MLA_HEADS * LANE, F32)], [(1, DN_DV)], tm)

    dqc, dkc, dvc = _attn_bwd(qc, kc, vc, do_mla, lse, delta, min(512, t), min(1024, t))
    dcq, dckv, dkr, d_q_g, d_kv_g, g_w_uq_t, g_w_ukv_t = _mla_prep_bwd(
        proj, pos_col, inv_freq2, q_norm_g, kv_norm_g, w_uq_t, w_ukv_t, dqc, dkc, dvc, min(256, t))

    token = grads_ready("mixer", g_w_o, g_w_uq_t, g_w_ukv_t)

    d_intra = _gdn_scan_bwd(intra, states, do_dn, gdn_tm)
    dqkv_act, dba, d_al8, d_dt8 = _gdn_intra_bwd(qkv, proj, al8 + token, dt8, inverses, d_intra, min(256, t))
    dqkv_pre, d_conv8 = _conv_bwd(proj, conv_w8, dqkv_act, min(256, t))

    dproj = jnp.concatenate([dqkv_pre, dz, dcq, dckv, dba, dkr], axis=1)
    dh1 = _matmul(dproj, w_in_t, "nn", "d_in_proj", BF)
    g_w_in_t = _matmul(dproj, h1, "tn", "d_w_in", BF)
    token = grads_ready("in", g_w_in_t)

    def modulate_bwd(xx, dh, dxa, sc):
        dh = dh.astype(F32)
        return dh * (1.0 + sc) + dxa, jnp.sum(dh * xx, axis=0, keepdims=True), jnp.sum(dh, axis=0, keepdims=True)

    grad_x, d_sc_m, d_sh_m = _rowwise("modulate_in_bwd", modulate_bwd, [xs, dh1, dx_a], [sc_m + token], [(d, F32)], [(1, d), (1, d)], tm)
    dmod = jnp.concatenate([d_sh_m, d_sc_m, d_gt_m, d_sh_f, d_sc_f, d_gt_f], axis=1)
    return grad_x, loss_acc, dmod, d_conv8, d_al8, d_dt8, d_dn_g, d_q_g, d_kv_g, d_ln1_g, d_ln1_b, d_ln2_g, d_ln2_b
```

```python
import functools
import math

import jax
import jax.numpy as jnp
from jax import lax
from jax.experimental import pallas as pl
from jax.experimental.pallas import tpu as pltpu

F32 = jnp.float32
BF = jnp.bfloat16
HI = lax.Precision.HIGHEST

N_DEV = 8
DN_HEADS = 4
DN_DK = 128
DN_DV = 128
CONV_K = 4
CHUNK = 64
MLA_HEADS = 4
QK_NOPE = 128
QK_ROPE = 64
V_HEAD = 128
Q_LORA = 512
KV_LORA = 256
ROPE_THETA = 10000.0
DEPTH = 1
DEEPNORM_ALPHA = (2.0 * DEPTH) ** 0.25
LANE = 128
CONV_HALO = 8
GL_ROWS = 8
CONV_ROWS, CONV_COLS = 64, 256

DN_QK = DN_HEADS * DN_DK
DN_VW = DN_HEADS * DN_DV
DN_CONV_CH = 2 * DN_QK + DN_VW
MLA_VW = MLA_HEADS * V_HEAD
P_Z = DN_CONV_CH
P_CQ = P_Z + DN_VW
P_CKV = P_CQ + Q_LORA
P_BA = P_CKV + KV_LORA
P_KR = P_BA + LANE
N_INP = P_KR + LANE

ADAM_LR = 0.001
ADAM_B1 = 0.9
ADAM_B2 = 0.999
ADAM_EPS = 1e-08
ADAM_WD = 0.01
ADAM_STEP = 10

NN = (((1,), (0,)), ((), ()))
NT = (((1,), (1,)), ((), ()))
TN = (((0,), (0,)), ((), ()))


def _pick(n, prefs):
    for p in prefs:
        if n % p == 0:
            return p
    return n


def _full(shape):
    return pl.BlockSpec(shape, lambda *_: (0,) * len(shape))


def _dot(a, b, dims=NN):
    return lax.dot_general(a, b, dims, preferred_element_type=F32)


def _doth(a, b, dims=NN):
    return lax.dot_general(a, b, dims, precision=HI, preferred_element_type=F32)


@jax.custom_vjp
def _mmb(a, b):
    return _dot(a.astype(BF), b.astype(BF), NN)


def _mmb_fwd(a, b):
    return _mmb(a, b), (a, b)


def _mmb_bwd(res, g):
    a, b = res
    gb = g.astype(BF)
    return (_dot(gb, b.astype(BF), NT).astype(a.dtype), _dot(a.astype(BF), gb, TN).astype(b.dtype))


_mmb.defvjp(_mmb_fwd, _mmb_bwd)


@jax.custom_vjp
def _mmb_nt(a, b):
    return _dot(a.astype(BF), b.astype(BF), NT)


def _mmb_nt_fwd(a, b):
    return _mmb_nt(a, b), (a, b)


def _mmb_nt_bwd(res, g):
    a, b = res
    gb = g.astype(BF)
    return (_dot(gb, b.astype(BF), NN).astype(a.dtype), _dot(gb, a.astype(BF), TN).astype(b.dtype))


_mmb_nt.defvjp(_mmb_nt_fwd, _mmb_nt_bwd)


def _sigmoid(x):
    return 0.5 * (jnp.tanh(0.5 * x) + 1.0)


def _silu(x):
    return x * _sigmoid(x)


def _softplus(x):
    return jnp.maximum(x, 0.0) + jnp.log(1.0 + jnp.exp(-jnp.abs(x)))


def _layernorm(x, g, b, eps=1e-5):
    mu = jnp.mean(x, axis=-1, keepdims=True)
    xc = x - mu
    var = jnp.mean(xc * xc, axis=-1, keepdims=True)
    return xc * lax.rsqrt(var + eps) * g + b


def _rmsnorm(x, g, eps=1e-6):
    return x * lax.rsqrt(jnp.mean(x * x, axis=-1, keepdims=True) + eps) * g


def _l2norm(x, eps=1e-6):
    return x * lax.rsqrt(jnp.sum(x * x, axis=-1, keepdims=True) + eps)


def _rowwise(name, fn, rows, vecs, out_rows, out_accs, tm):
    rows = [r if isinstance(r, tuple) else (r, r.shape[1], 0) for r in rows]
    t = rows[0][0].shape[0]
    tm = min(tm, t)
    assert t % tm == 0
    nr, nv, no = len(rows), len(vecs), len(out_rows)

    def body(*refs):
        ins = [r[...] for r in refs[:nr + nv]]
        outs = fn(*ins)
        outs = outs if isinstance(outs, (tuple, list)) else (outs,)
        o_rows = refs[nr + nv:nr + nv + no]
        o_accs = refs[nr + nv + no:]
        for o, val in zip(o_rows, outs[:no]):
            o[...] = val.astype(o.dtype)
        if o_accs:
            @pl.when(pl.program_id(0) == 0)
            def _():
                for o in o_accs:
                    o[...] = jnp.zeros_like(o)
            for o, val in zip(o_accs, outs[no:]):
                o[...] += val

    in_specs = [pl.BlockSpec((tm, w), functools.partial(lambda i, j: (i, j), j=j)) for (_, w, j) in rows]
    in_specs += [_full(v.shape) for v in vecs]
    out_specs = [pl.BlockSpec((tm, w), lambda i: (i, 0)) for (w, _) in out_rows]
    out_specs += [_full(s) for s in out_accs]
    out_shape = [jax.ShapeDtypeStruct((t, w), d) for (w, d) in out_rows]
    out_shape += [jax.ShapeDtypeStruct(s, F32) for s in out_accs]
    res = pl.pallas_call(
        body, grid=(t // tm,), in_specs=in_specs, out_specs=out_specs, out_shape=out_shape, name=name,
        compiler_params=pltpu.CompilerParams(dimension_semantics=("arbitrary",)),
    )(*[r[0] for r in rows], *vecs)
    return res


def _matmul(a, b, mode, name, out_dtype=F32):
    if mode == "nn":
        (m, k), n = a.shape, b.shape[1]
    elif mode == "nt":
        (m, k), n = a.shape, b.shape[0]
    else:
        (k, m), n = a.shape, b.shape[1]
    tm, tn, tk = _matmul_tiles(m, n, k, a.dtype.itemsize, b.dtype.itemsize, jnp.dtype(out_dtype).itemsize)
    nk = k // tk
    dims = {"nn": NN, "nt": NT, "tn": TN}[mode]

    def body(a_ref, b_ref, o_ref, *acc):
        part = _dot(a_ref[...].astype(BF), b_ref[...].astype(BF), dims)
        if nk == 1:
            o_ref[...] = part.astype(o_ref.dtype)
            return
        (acc_ref,) = acc
        kk = pl.program_id(2)

        @pl.when(kk == 0)
        def _():
            acc_ref[...] = part

        @pl.when(kk > 0)
        def _():
            acc_ref[...] += part

        @pl.when(kk == nk - 1)
        def _():
            o_ref[...] = acc_ref[...].astype(o_ref.dtype)

    a_spec = pl.BlockSpec((tk, tm), lambda i, j, kk: (kk, i)) if mode == "tn" else pl.BlockSpec((tm, tk), lambda i, j, kk: (i, kk))
    b_spec = pl.BlockSpec((tn, tk), lambda i, j, kk: (j, kk)) if mode == "nt" else pl.BlockSpec((tk, tn), lambda i, j, kk: (kk, j))
    return pl.pallas_call(
        body, grid=(m // tm, n // tn, nk), in_specs=[a_spec, b_spec],
        out_specs=pl.BlockSpec((tm, tn), lambda i, j, kk: (i, j)),
        out_shape=jax.ShapeDtypeStruct((m, n), out_dtype),
        scratch_shapes=[pltpu.VMEM((tm, tn), F32)] if nk > 1 else [], name=name,
        compiler_params=pltpu.CompilerParams(dimension_semantics=("parallel", "parallel", "arbitrary")),
    )(a, b)


def _lane_tile(n, cap):
    return max([n // s for s in range(1, n // LANE + 1) if n % s == 0 and (n // s) % LANE == 0 and n // s <= cap] or [n])


def _ffn_in(h, w_gate, w_up):
    m, k = h.shape
    f = w_gate.shape[0]
    tm, tn = _pick(m, (512, 256, 128)), _lane_tile(f, 1408)

    def body(h_ref, wg_ref, wu_ref, act_ref, g_ref, u_ref):
        hh = h_ref[...]
        g = _dot(hh, wg_ref[...], NT)
        u = _dot(hh, wu_ref[...], NT)
        act_ref[...] = (_silu(g) * u).astype(act_ref.dtype)
        g_ref[...] = g.astype(g_ref.dtype)
        u_ref[...] = u.astype(u_ref.dtype)

    w_spec = pl.BlockSpec((tn, k), lambda i, j: (j, 0))
    o_spec = pl.BlockSpec((tm, tn), lambda i, j: (i, j))
    return pl.pallas_call(
        body, grid=(m // tm, f // tn), in_specs=[pl.BlockSpec((tm, k), lambda i, j: (i, 0)), w_spec, w_spec],
        out_specs=[o_spec] * 3, out_shape=[jax.ShapeDtypeStruct((m, f), BF)] * 3, name="ffn_in",
        compiler_params=pltpu.CompilerParams(dimension_semantics=("parallel", "parallel")),
    )(h, w_gate, w_up)


def _ffn_act_bwd(dff, w_down, gate, up):
    m, k = dff.shape
    f = w_down.shape[0]
    tm, tn = _pick(m, (512, 256, 128)), _lane_tile(f, 1408)

    def body(d_ref, w_ref, g_ref, u_ref, dg_ref, du_ref):
        da = _dot(d_ref[...], w_ref[...], NT)
        g = g_ref[...].astype(F32)
        sg = _sigmoid(g)
        dg_ref[...] = (da * u_ref[...].astype(F32) * (sg * (1.0 + g * (1.0 - sg)))).astype(dg_ref.dtype)
        du_ref[...] = (da * (g * sg)).astype(du_ref.dtype)

    o_spec = pl.BlockSpec((tm, tn), lambda i, j: (i, j))
    return pl.pallas_call(
        body, grid=(m // tm, f // tn),
        in_specs=[pl.BlockSpec((tm, k), lambda i, j: (i, 0)), pl.BlockSpec((tn, k), lambda i, j: (j, 0)), o_spec, o_spec],
        out_specs=[o_spec] * 2, out_shape=[jax.ShapeDtypeStruct((m, f), BF)] * 2, name="d_ffn_act",
        compiler_params=pltpu.CompilerParams(dimension_semantics=("parallel", "parallel")),
    )(dff, w_down, gate, up)


def _matmul2_nn(a1, b1, a2, b2, name, out_dtype=F32):
    m, k = a1.shape
    n = b1.shape[1]
    tm, tn = _pick(m, (1024, 512, 256, 128)), _pick(n, (512, 256, 128))

    def body(a1_ref, b1_ref, a2_ref, b2_ref, o_ref):
        o_ref[...] = (_dot(a1_ref[...], b1_ref[...]) + _dot(a2_ref[...], b2_ref[...])).astype(o_ref.dtype)

    a_spec = pl.BlockSpec((tm, k), lambda i, j: (i, 0))
    b_spec = pl.BlockSpec((k, tn), lambda i, j: (0, j))
    return pl.pallas_call(
        body, grid=(m // tm, n // tn), in_specs=[a_spec, b_spec, a_spec, b_spec],
        out_specs=pl.BlockSpec((tm, tn), lambda i, j: (i, j)), out_shape=jax.ShapeDtypeStruct((m, n), out_dtype), name=name,
        compiler_params=pltpu.CompilerParams(dimension_semantics=("parallel", "parallel")),
    )(a1, b1, a2, b2)


MATMUL_VMEM_BUDGET = 28 * 1024 * 1024


def _matmul_tiles(m, n, k, a_bytes, b_bytes, o_bytes):
    def divisors(x, cap):
        return sorted({x // s for s in range(1, 65) if x % s == 0 and (x // s) % LANE == 0 and x // s <= cap}, reverse=True) or [x]

    for tk in divisors(k, k):
        best = None
        for tm in divisors(m, 1024):
            for tn in divisors(n, 2048):
                need = 2 * (tm * tk * a_bytes + tk * tn * b_bytes + tm * tn * o_bytes) + (tm * tn * 4 if tk < k else 0)
                if need <= MATMUL_VMEM_BUDGET and tm * tn >= 512 * 512 and (best is None or tm * tn > best[0] * best[1]):
                    best = (tm, tn)
        if best:
            return best[0], best[1], tk
    return _pick(m, (512, 256, 128)), _pick(n, (512, 256, 128)), _pick(k, (512, 256, 128))


def _exchange(xs, name, scatter):
    n = len(xs)
    npeer = N_DEV - 1

    def body(*refs):
        x_refs, o_refs = refs[:n], refs[n:2 * n]
        send_sems, recv_sems, local_sems = refs[2 * n:]
        mx, my, mc = lax.axis_index("x"), lax.axis_index("y"), lax.axis_index("c")
        me = 4 * mx + 2 * my + mc
        src_me = [x.at[me] if scatter else x for x in x_refs]
        mine = [pltpu.make_async_copy(src_me[a], o_refs[a].at[me], local_sems.at[a]) for a in range(n)]
        for cp in mine:
            cp.start()
        copies = []
        for k in range(1, N_DEV):
            px, py, pc = mx ^ (k >> 2), my ^ ((k >> 1) & 1), mc ^ (k & 1)
            peer = 4 * px + 2 * py + pc
            for a in range(n):
                cp = pltpu.make_async_remote_copy(
                    src_ref=x_refs[a].at[peer] if scatter else x_refs[a], dst_ref=o_refs[a].at[me],
                    send_sem=send_sems.at[a * npeer + k - 1], recv_sem=recv_sems.at[a * npeer + k - 1],
                    device_id=(px, py, pc), device_id_type=pl.DeviceIdType.MESH)
                cp.start()
                copies.append((cp, a, k, peer))
        for cp, a, k, peer in copies:
            pltpu.make_async_remote_copy(
                src_ref=src_me[a], dst_ref=o_refs[a].at[peer], send_sem=send_sems.at[a * npeer + k - 1],
                recv_sem=recv_sems.at[a * npeer + k - 1], device_id=(mx, my, mc),
                device_id_type=pl.DeviceIdType.MESH).wait_recv()
        for cp, _, _, _ in copies:
            cp.wait_send()
        for cp in mine:
            cp.wait()

    return pl.pallas_call(
        body, out_shape=[jax.ShapeDtypeStruct((N_DEV,) + x.shape[-2:], x.dtype) for x in xs],
        in_specs=[pl.BlockSpec(memory_space=pl.ANY)] * n, out_specs=[pl.BlockSpec(memory_space=pl.ANY)] * n,
        scratch_shapes=[pltpu.SemaphoreType.DMA((n * npeer,)), pltpu.SemaphoreType.DMA((n * npeer,)),
                        pltpu.SemaphoreType.DMA((n,))],
        name=name,
    )(*xs)


def _gather_by_chip(xs, name):
    n = len(xs)
    per = N_DEV - 1

    def body(*refs):
        x_refs, o_refs = refs[:n], refs[n:2 * n]
        send_sems, recv_sems, local_sems = refs[2 * n:]
        mx, my, mc = lax.axis_index("x"), lax.axis_index("y"), lax.axis_index("c")
        me, sibling = (mx, my, mc), (mx, my, 1 - mc)
        chips = [(1 - mx, my), (mx, 1 - my), (1 - mx, 1 - my)]
        slot = lambda d: 4 * d[0] + 2 * d[1] + d[2]

        def copy(a, k, block, to, src=None):
            dst = o_refs[a].at[slot(block)]
            return pltpu.make_async_remote_copy(
                src_ref=dst if src is None else src, dst_ref=dst, send_sem=send_sems.at[a * per + k],
                recv_sem=recv_sems.at[a * per + k], device_id=to, device_id_type=pl.DeviceIdType.MESH)

        mine = [pltpu.make_async_copy(x_refs[a], o_refs[a].at[slot(me)], local_sems.at[a]) for a in range(n)]
        for cp in mine:
            cp.start()
        first = []
        for a in range(n):
            first.append(copy(a, 0, me, sibling, src=x_refs[a]))
            first += [copy(a, 1 + j, me, (*chip, mc), src=x_refs[a]) for j, chip in enumerate(chips)]
        for cp in first:
            cp.start()
        passed = []
        for j, chip in enumerate(chips):
            for a in range(n):
                copy(a, 1 + j, (*chip, mc), me).wait_recv()
                cp = copy(a, 4 + j, (*chip, mc), sibling)
                cp.start()
                passed.append(cp)
        for a in range(n):
            copy(a, 0, sibling, me).wait_recv()
            for j, chip in enumerate(chips):
                copy(a, 4 + j, (*chip, 1 - mc), me).wait_recv()
        for cp in first + passed:
            cp.wait_send()
        for cp in mine:
            cp.wait()

    return pl.pallas_call(
        body, out_shape=[jax.ShapeDtypeStruct((N_DEV,) + x.shape, x.dtype) for x in xs],
        in_specs=[pl.BlockSpec(memory_space=pl.ANY)] * n, out_specs=[pl.BlockSpec(memory_space=pl.ANY)] * n,
        scratch_shapes=[pltpu.SemaphoreType.DMA((n * per,)), pltpu.SemaphoreType.DMA((n * per,)),
                        pltpu.SemaphoreType.DMA((n,))],
        name=name,
    )(*xs)


def _peer_of(k):
    mx, my, mc = lax.axis_index("x"), lax.axis_index("y"), lax.axis_index("c")
    px, py, pc = mx ^ (k >> 2), my ^ ((k >> 1) & 1), mc ^ (k & 1)
    return (px, py, pc), 4 * px + 2 * py + pc


def _exchange_start(xs, name, scatter):
    n = len(xs)
    npeer = N_DEV - 1

    def body(*refs):
        x_refs, land_refs = refs[:n], refs[n:2 * n]
        send_sems, recv_sems, token = refs[2 * n], refs[2 * n + 1], refs[-1]
        me = 4 * lax.axis_index("x") + 2 * lax.axis_index("y") + lax.axis_index("c")
        for k in range(1, N_DEV):
            dev, peer = _peer_of(k)
            for a in range(n):
                pltpu.make_async_remote_copy(
                    src_ref=x_refs[a].at[peer] if scatter else x_refs[a], dst_ref=land_refs[a].at[me],
                    send_sem=send_sems.at[a * npeer + k - 1], recv_sem=recv_sems.at[a * npeer + k - 1],
                    device_id=dev, device_id_type=pl.DeviceIdType.MESH).start()
        token[...] = jnp.zeros_like(token)

    hbm = pl.BlockSpec(memory_space=pltpu.HBM)
    sem = pl.BlockSpec(memory_space=pltpu.SEMAPHORE)
    lands = [pltpu.with_memory_space_constraint(lax.empty((N_DEV,) + x.shape[-2:], x.dtype), pltpu.HBM) for x in xs]
    srcs = [pltpu.with_memory_space_constraint(x, pltpu.HBM) for x in xs]
    outs = pl.pallas_call(
        body, name=name,
        out_shape=(pltpu.SemaphoreType.DMA((n * npeer,)), pltpu.SemaphoreType.DMA((n * npeer,)),
                   *[pltpu.HBM(x.shape, x.dtype) for x in srcs], *[pltpu.HBM(z.shape, z.dtype) for z in lands],
                   jax.ShapeDtypeStruct((8, LANE), F32)),
        in_specs=[hbm] * (2 * n), out_specs=(sem, sem, *[hbm] * (2 * n), pl.BlockSpec(memory_space=pltpu.VMEM)),
        input_output_aliases={i: 2 + i for i in range(2 * n)},
        compiler_params=pltpu.CompilerParams(has_side_effects=pltpu.SideEffectType.DATAFLOW_SIDE_EFFECTING),
    )(*srcs, *lands)
    return (outs[0], outs[1], list(outs[2:2 + n]), list(outs[2 + n:2 + 2 * n])), outs[-1][0:1, 0:1]


def _exchange_wait(started, after, name, scatter):
    send_sems, recv_sems, srcs, lands = started
    n = len(srcs)
    npeer = N_DEV - 1

    def body(*refs):
        x_refs, land_refs = refs[:n], refs[n:2 * n]
        send_sems, recv_sems = refs[2 * n], refs[2 * n + 1]
        mx, my, mc = lax.axis_index("x"), lax.axis_index("y"), lax.axis_index("c")
        me = 4 * mx + 2 * my + mc
        for k in range(1, N_DEV):
            _, peer = _peer_of(k)
            for a in range(n):
                src = x_refs[a].at[me] if scatter else x_refs[a]
                cp = pltpu.make_async_remote_copy(
                    src_ref=src, dst_ref=land_refs[a].at[peer], send_sem=send_sems.at[a * npeer + k - 1],
                    recv_sem=recv_sems.at[a * npeer + k - 1], device_id=(mx, my, mc), device_id_type=pl.DeviceIdType.MESH)
                cp.wait_send()
                cp.wait_recv()

    hbm = pl.BlockSpec(memory_space=pltpu.HBM)
    sem = pl.BlockSpec(memory_space=pltpu.SEMAPHORE)
    outs = pl.pallas_call(
        body, name=name,
        out_shape=(*[pltpu.HBM(x.shape, x.dtype) for x in srcs], *[pltpu.HBM(z.shape, z.dtype) for z in lands]),
        in_specs=[hbm] * (2 * n) + [sem, sem, pl.BlockSpec(memory_space=pl.ANY)], out_specs=tuple([hbm] * (2 * n)),
        input_output_aliases={i: i for i in range(2 * n)},
        compiler_params=pltpu.CompilerParams(has_side_effects=pltpu.SideEffectType.DATAFLOW_SIDE_EFFECTING),
    )(*srcs, *lands, send_sems, recv_sems, after)
    me = 4 * lax.axis_index("x") + 2 * lax.axis_index("y") + lax.axis_index("c")
    full = []
    for x, land in zip(outs[:n], outs[n:]):
        own = lax.dynamic_slice(x, (me, 0, 0), (1,) + x.shape[1:]) if scatter else x[None]
        full.append(lax.dynamic_update_slice(land, own, (me, 0, 0)))
    return full


def _sum_slots(x, name):
    _, r, c = x.shape
    tr = _pick(r, (512, 256, 128, 64, 32, 16))

    def body(x_ref, o_ref):
        acc = x_ref[0].astype(F32)
        for s in range(1, N_DEV):
            acc = acc + x_ref[s].astype(F32)
        o_ref[...] = acc

    return pl.pallas_call(
        body, grid=(r // tr,), in_specs=[pl.BlockSpec((N_DEV, tr, c), lambda i: (0, i, 0))],
        out_specs=pl.BlockSpec((tr, c), lambda i: (i, 0)), out_shape=jax.ShapeDtypeStruct((r, c), F32), name=name,
        compiler_params=pltpu.CompilerParams(dimension_semantics=("arbitrary",)),
    )(x)


def _mod_fwd(c_all, w_ada, b_ada_mine):
    def body(c_ref, w_ref, b_ref, o_ref):
        o_ref[...] = _doth(_silu(c_ref[...]), w_ref[...]) + b_ref[...]

    return pl.pallas_call(body, out_shape=jax.ShapeDtypeStruct((c_all.shape[0], w_ada.shape[1]), F32), name="mod_fwd")(c_all, w_ada, b_ada_mine)


def _mod_bwd(c_all_t, dmod_mine):
    def body(ct_ref, d_ref, o_ref):
        s = _silu(ct_ref[...])
        acc = s[:, 0:1] * d_ref[pl.ds(0, 1), :]
        for b in range(1, N_DEV):
            acc = acc + s[:, b:b + 1] * d_ref[pl.ds(b, 1), :]
        o_ref[...] = acc

    return pl.pallas_call(body, out_shape=jax.ShapeDtypeStruct((c_all_t.shape[0], dmod_mine.shape[1]), F32), name="mod_bwd")(c_all_t, dmod_mine)


def _conv_fwd(proj, conv_w8, tm):
    t = proj.shape[0]
    ch = DN_CONV_CH

    def body(x_ref, w_ref, o_ref, buf):
        @pl.when(pl.program_id(0) == 0)
        def _():
            buf[pl.ds(0, CONV_HALO), :] = jnp.zeros((CONV_HALO, ch), F32)

        buf[pl.ds(CONV_HALO, tm), :] = x_ref[...].astype(F32)
        for c0 in range(0, ch, CONV_COLS):
            cols = pl.ds(c0, CONV_COLS)
            w = [w_ref[pl.ds(j, 1), cols] for j in range(CONV_K)]
            for r0 in range(0, tm, CONV_ROWS):
                acc = buf[pl.ds(r0 + CONV_HALO - (CONV_K - 1), CONV_ROWS), cols] * w[0]
                for j in range(1, CONV_K):
                    acc = acc + buf[pl.ds(r0 + CONV_HALO - (CONV_K - 1) + j, CONV_ROWS), cols] * w[j]
                o_ref[pl.ds(r0, CONV_ROWS), cols] = _silu(acc)
        buf[pl.ds(0, CONV_HALO), :] = buf[pl.ds(tm, CONV_HALO), :]

    return pl.pallas_call(
        body, grid=(t // tm,), in_specs=[pl.BlockSpec((tm, ch), lambda i: (i, 0)), _full(conv_w8.shape)],
        out_specs=pl.BlockSpec((tm, ch), lambda i: (i, 0)), out_shape=jax.ShapeDtypeStruct((t, ch), F32),
        scratch_shapes=[pltpu.VMEM((tm + CONV_HALO, ch), F32)], name="conv_fwd",
        compiler_params=pltpu.CompilerParams(dimension_semantics=("arbitrary",)),
    )(proj, conv_w8)


def _conv_bwd(proj, conv_w8, dact, tm):
    t = proj.shape[0]
    ch = DN_CONV_CH
    nt = t // tm
    halo_blk = 2 * CONV_HALO
    hb = tm // halo_blk

    def body(x_ref, xp_ref, w_ref, dy_ref, dx_ref, dw_ref, xbuf, dbuf):
        step = pl.program_id(0)

        @pl.when(step == 0)
        def _():
            dbuf[pl.ds(tm, CONV_HALO), :] = jnp.zeros((CONV_HALO, ch), F32)
            dw_ref[...] = jnp.zeros_like(dw_ref)

        first = step == nt - 1
        xbuf[pl.ds(0, CONV_HALO), :] = jnp.where(first, 0.0, xp_ref[...].astype(F32)[halo_blk - CONV_HALO:])
        xbuf[pl.ds(CONV_HALO, tm), :] = x_ref[...].astype(F32)
        for c0 in range(0, ch, CONV_COLS):
            cols = pl.ds(c0, CONV_COLS)
            w = [w_ref[pl.ds(j, 1), cols] for j in range(CONV_K)]
            dw = [jnp.zeros((1, CONV_COLS), F32) for _ in range(CONV_K)]
            for r0 in range(0, tm, CONV_ROWS):
                xs = [xbuf[pl.ds(r0 + CONV_HALO - (CONV_K - 1) + j, CONV_ROWS), cols] for j in range(CONV_K)]
                pre = xs[0] * w[0]
                for j in range(1, CONV_K):
                    pre = pre + xs[j] * w[j]
                sg = _sigmoid(pre)
                dpre = dy_ref[pl.ds(r0, CONV_ROWS), cols] * (sg * (1.0 + pre * (1.0 - sg)))
                dbuf[pl.ds(r0, CONV_ROWS), cols] = dpre
                dw = [dw[j] + jnp.sum(dpre * xs[j], axis=0, keepdims=True) for j in range(CONV_K)]
            for j in range(CONV_K):
                dw_ref[pl.ds(j, 1), cols] += dw[j]
            for r0 in range(0, tm, CONV_ROWS):
                dx = dbuf[pl.ds(r0 + CONV_K - 1, CONV_ROWS), cols] * w[0]
                for j in range(1, CONV_K):
                    dx = dx + dbuf[pl.ds(r0 + CONV_K - 1 - j, CONV_ROWS), cols] * w[j]
                dx_ref[pl.ds(r0, CONV_ROWS), cols] = dx.astype(dx_ref.dtype)
        dbuf[pl.ds(tm, CONV_HALO), :] = dbuf[pl.ds(0, CONV_HALO), :]

    rev = lambda i: (nt - 1 - i, 0)
    prev = lambda i: (jnp.maximum((nt - 1 - i) * hb - 1, 0), 0)
    return pl.pallas_call(
        body, grid=(nt,),
        in_specs=[pl.BlockSpec((tm, ch), rev), pl.BlockSpec((halo_blk, ch), prev), _full(conv_w8.shape),
                  pl.BlockSpec((tm, ch), rev)],
        out_specs=[pl.BlockSpec((tm, ch), rev), _full(conv_w8.shape)],
        out_shape=[jax.ShapeDtypeStruct((t, ch), BF), jax.ShapeDtypeStruct(conv_w8.shape, F32)],
        scratch_shapes=[pltpu.VMEM((tm + CONV_HALO, ch), F32), pltpu.VMEM((tm + CONV_HALO, ch), F32)], name="conv_bwd",
        compiler_params=pltpu.CompilerParams(dimension_semantics=("arbitrary",)),
    )(proj, proj, conv_w8, dact)


BNN = (((2,), (1,)), ((0,), (0,)))
BNT = (((2,), (2,)), ((0,), (0,)))
BTN = (((1,), (1,)), ((0,), (0,)))


def _bdot(a, b, dims, precision=None):
    return lax.dot_general(a, b, dims, precision=precision, preferred_element_type=F32)


@jax.custom_vjp
def _bmmb_nt(a, b):
    return _bdot(a.astype(BF), b.astype(BF), BNT)


def _bmmb_nt_fwd(a, b):
    return _bmmb_nt(a, b), (a, b)


def _bmmb_nt_bwd(res, g):
    a, b = res
    gb = g.astype(BF)
    return _bdot(gb, b.astype(BF), BNN), _bdot(gb, a.astype(BF), BTN)


_bmmb_nt.defvjp(_bmmb_nt_fwd, _bmmb_nt_bwd)


@jax.custom_vjp
def _bmmb(a, b):
    return _bdot(a.astype(BF), b.astype(BF), BNN)


def _bmmb_fwd(a, b):
    return _bmmb(a, b), (a, b)


def _bmmb_bwd(res, g):
    a, b = res
    gb = g.astype(BF)
    return _bdot(gb, b.astype(BF), BNT), _bdot(a.astype(BF), gb, BTN)


_bmmb.defvjp(_bmmb_fwd, _bmmb_bwd)


@jax.custom_vjp
def _bmmb_tn(a, b):
    return _bdot(a.astype(BF), b.astype(BF), BTN)


def _bmmb_tn_fwd(a, b):
    return _bmmb_tn(a, b), (a, b)


def _bmmb_tn_bwd(res, g):
    a, b = res
    gb = g.astype(BF)
    return _bdot(b.astype(BF), gb, BNT), _bdot(a.astype(BF), gb, BNN)


_bmmb_tn.defvjp(_bmmb_tn_fwd, _bmmb_tn_bwd)


def _unit_lower_solve_fwd(a, r):
    c = a.shape[-1]
    ri = lax.broadcasted_iota(jnp.int32, a.shape, 1)
    ci = lax.broadcasted_iota(jnp.int32, a.shape, 2)
    xm = -a
    inv = (ri == ci).astype(F32) + xm
    for _ in range(int(math.log2(c)) - 1):
        xm = _bdot(xm, xm, BNN, HI)
        inv = inv + _bdot(inv, xm, BNN, HI)
    x = _bdot(inv, r, BNN, HI)
    return x, (inv, x)


def _unit_lower_solve_bwd(res, g):
    inv, x = res
    dr = _bdot(inv, g, BTN, HI)
    return -_bdot(dr, x, BNT, HI), dr


@jax.custom_vjp
def _unit_lower_solve_given(a, r, inv):
    return _bdot(inv, r, BNN, HI)


def _unit_lower_solve_given_fwd(a, r, inv):
    x = _bdot(inv, r, BNN, HI)
    return x, (inv, x)


def _unit_lower_solve_given_bwd(res, g):
    da, dr = _unit_lower_solve_bwd(res, g)
    return da, dr, jnp.zeros_like(res[0])


_unit_lower_solve_given.defvjp(_unit_lower_solve_given_fwd, _unit_lower_solve_given_bwd)


def _gdn_intra(qkv, ba, al8, dt8, inv4=None):
    tm = qkv.shape[0]
    nb = tm // CHUNK
    bsz = DN_HEADS * nb

    def heads(x0):
        return jnp.concatenate([qkv[:, x0 + h * LANE:x0 + (h + 1) * LANE].reshape(nb, CHUNK, LANE) for h in range(DN_HEADS)], axis=0)

    def spread(c0):
        return jnp.concatenate([jnp.broadcast_to(ba[:, c0 + h:c0 + h + 1], (tm, LANE)).reshape(nb, CHUNK, LANE)
                                for h in range(DN_HEADS)], axis=0)

    def per_head(v8):
        return jnp.concatenate([jnp.broadcast_to(v8[0:1, h:h + 1].reshape(1, 1, 1), (nb, 1, LANE)) for h in range(DN_HEADS)], axis=0)

    ri = lax.broadcasted_iota(jnp.int32, (bsz, CHUNK, CHUNK), 1)
    ci = lax.broadcasted_iota(jnp.int32, (bsz, CHUNK, CHUNK), 2)
    incl = ri >= ci
    strict = ri > ci

    q = _l2norm(heads(0)) * (DN_DK ** -0.5)
    k = _l2norm(heads(DN_QK))
    va = heads(2 * DN_QK)
    beta = _sigmoid(spread(0))
    g = -jnp.exp(per_head(al8)) * _softplus(spread(DN_HEADS) + per_head(dt8))
    gc = _bdot(incl.astype(F32), g, BNN, HI)
    g_last = jnp.sum(g, axis=1, keepdims=True)
    gcol = gc[:, :, :CHUNK]
    diff = gcol - jnp.swapaxes(gcol, 1, 2)
    decay = jnp.where(incl, jnp.exp(jnp.where(incl, diff, 0.0)), 0.0)
    kb = k * beta
    a_mat = jnp.where(strict, _bmmb_nt(kb, k) * decay, 0.0)
    egc = jnp.exp(gc)
    rhs = jnp.concatenate([kb * egc, va * beta], axis=2)
    if inv4 is None:
        wu, (inv, _) = _unit_lower_solve_fwd(a_mat, rhs)
    else:
        wu = _unit_lower_solve_given(a_mat, rhs, inv4.reshape(bsz, CHUNK, CHUNK))
    attn = jnp.where(incl, _bmmb_nt(q, k) * decay, 0.0)

    def unheads(x):
        return jnp.concatenate([x[h * nb:(h + 1) * nb].reshape(tm, LANE) for h in range(DN_HEADS)], axis=1)

    w_c, u_c = wu[:, :, :DN_DK], wu[:, :, DN_DK:]
    kd = k * jnp.exp(g_last - gc)
    out = (unheads(q * egc - _bmmb(attn, w_c)), unheads(_bmmb(attn, u_c)),
           _bmmb_tn(kd, w_c).reshape(DN_HEADS, nb, DN_DK, DN_DK), _bmmb_tn(kd, u_c).reshape(DN_HEADS, nb, DN_DK, DN_DV),
           jnp.broadcast_to(g_last, (bsz, GL_ROWS, LANE)).reshape(DN_HEADS, nb, GL_ROWS, LANE))
    return out if inv4 is not None else out + (inv.reshape(DN_HEADS, nb, CHUNK, CHUNK),)


def _gdn_scan_step(qp, op, c_mat, n_mat, gl, s):
    return _mmb(qp, s) + op, s * jnp.exp(gl) - _mmb(c_mat, s) + n_mat


def _gdn_intra_specs(t, tm, dts, order=lambda i: i):
    nb = tm // CHUNK
    row = pl.BlockSpec((tm, DN_VW), lambda i: (order(i), 0))
    mat = pl.BlockSpec((DN_HEADS, nb, DN_DK, DN_DV), lambda i: (0, order(i), 0, 0))
    row_shape = lambda d: jax.ShapeDtypeStruct((t, DN_VW), d)
    mat_shape = lambda d: jax.ShapeDtypeStruct((DN_HEADS, t // CHUNK, DN_DK, DN_DV), d)
    gl = pl.BlockSpec((DN_HEADS, nb, GL_ROWS, LANE), lambda i: (0, order(i), 0, 0))
    gl_shape = jax.ShapeDtypeStruct((DN_HEADS, t // CHUNK, GL_ROWS, LANE), dts[4])
    return [row, row, mat, mat, gl], [row_shape(dts[0]), row_shape(dts[1]), mat_shape(dts[2]), mat_shape(dts[3]), gl_shape]


def _gdn_intra_fwd(qkv, proj, al8, dt8, tm):
    t = qkv.shape[0]

    def body(qkv_ref, ba_ref, al_ref, dt_ref, *outs):
        for o, val in zip(outs, _gdn_intra(qkv_ref[...], ba_ref[...], al_ref[...], dt_ref[...])):
            o[...] = val.astype(o.dtype)

    specs, shapes = _gdn_intra_specs(t, tm, (BF, F32, BF, BF, F32))
    specs.append(_gdn_inverse_spec(tm))
    shapes.append(jax.ShapeDtypeStruct((DN_HEADS, t // CHUNK, CHUNK, CHUNK), F32))
    res = pl.pallas_call(
        body, grid=(t // tm,),
        in_specs=[pl.BlockSpec((tm, DN_CONV_CH), lambda i: (i, 0)), pl.BlockSpec((tm, LANE), lambda i: (i, 0)),
                  _full(al8.shape), _full(dt8.shape)],
        out_specs=specs, out_shape=shapes, name="gdn_intra_fwd",
        compiler_params=pltpu.CompilerParams(dimension_semantics=("parallel",)),
    )(qkv, proj, al8, dt8)
    return res[:5], res[5]


def _gdn_inverse_spec(tm):
    return pl.BlockSpec((DN_HEADS, tm // CHUNK, CHUNK, CHUNK), lambda i: (0, i, 0, 0))


def _gdn_intra_bwd(qkv, proj, al8, dt8, inverses, cts, tm):
    t = qkv.shape[0]

    def body(qkv_ref, ba_ref, al_ref, dt_ref, inv_ref, *refs):
        ct_refs, (dqkv_ref, dba_ref, dal_ref, ddt_ref) = refs[:5], refs[5:]

        @pl.when(pl.program_id(0) == 0)
        def _():
            dal_ref[...] = jnp.zeros_like(dal_ref)
            ddt_ref[...] = jnp.zeros_like(ddt_ref)

        _, vjp = jax.vjp(functools.partial(_gdn_intra, inv4=inv_ref[...]), qkv_ref[...], ba_ref[...], al_ref[...], dt_ref[...])
        dqkv, dba, dal, ddt = vjp(tuple(r[...].astype(F32) for r in ct_refs))
        dqkv_ref[...] = dqkv.astype(dqkv_ref.dtype)
        dba_ref[...] = dba.astype(dba_ref.dtype)
        dal_ref[...] += dal
        ddt_ref[...] += ddt

    specs, _ = _gdn_intra_specs(t, tm, (F32,) * 5)
    return pl.pallas_call(
        body, grid=(t // tm,),
        in_specs=[pl.BlockSpec((tm, DN_CONV_CH), lambda i: (i, 0)), pl.BlockSpec((tm, LANE), lambda i: (i, 0)),
                  _full(al8.shape), _full(dt8.shape), _gdn_inverse_spec(tm)] + specs,
        out_specs=[pl.BlockSpec((tm, DN_CONV_CH), lambda i: (i, 0)), pl.BlockSpec((tm, LANE), lambda i: (i, 0)),
                   _full(al8.shape), _full(dt8.shape)],
        out_shape=[jax.ShapeDtypeStruct((t, DN_CONV_CH), BF), jax.ShapeDtypeStruct((t, LANE), BF),
                   jax.ShapeDtypeStruct(al8.shape, F32), jax.ShapeDtypeStruct(dt8.shape, F32)],
        name="gdn_intra_bwd", compiler_params=pltpu.CompilerParams(dimension_semantics=("arbitrary",)),
    )(qkv, proj, al8, dt8, inverses, *cts)


def _gdn_scan_fwd(intra, tm):
    t = intra[0].shape[0]
    nb = tm // CHUNK
    nc = t // CHUNK

    def body(qp_ref, op_ref, c_ref, n_ref, gl_ref, o_ref, ss_ref, s_scr):
        @pl.when(pl.program_id(0) == 0)
        def _():
            s_scr[...] = jnp.zeros_like(s_scr)

        state = [s_scr[h] for h in range(DN_HEADS)]
        for cc in range(nb):
            rows = pl.ds(cc * CHUNK, CHUNK)
            for h in range(DN_HEADS):
                cols = pl.ds(h * DN_DV, DN_DV)
                ss_ref[cc, h] = state[h].astype(ss_ref.dtype)
                o_ref[rows, cols], state[h] = _gdn_scan_step(
                    qp_ref[rows, cols], op_ref[rows, cols], c_ref[h, cc], n_ref[h, cc], gl_ref[h, cc, pl.ds(0, 1), :], state[h])
        for h in range(DN_HEADS):
            s_scr[h] = state[h]

    specs, _ = _gdn_intra_specs(t, tm, (F32,) * 5)
    return pl.pallas_call(
        body, grid=(t // tm,), in_specs=specs,
        out_specs=[pl.BlockSpec((tm, DN_VW), lambda i: (i, 0)),
                   pl.BlockSpec((nb, DN_HEADS, DN_DK, DN_DV), lambda i: (i, 0, 0, 0))],
        out_shape=[jax.ShapeDtypeStruct((t, DN_VW), F32), jax.ShapeDtypeStruct((nc, DN_HEADS, DN_DK, DN_DV), BF)],
        scratch_shapes=[pltpu.VMEM((DN_HEADS, DN_DK, DN_DV), F32)], name="gdn_scan_fwd",
        compiler_params=pltpu.CompilerParams(dimension_semantics=("arbitrary",)),
    )(*intra)


def _gdn_scan_bwd(intra, states, do, tm):
    t = intra[0].shape[0]
    nb = tm // CHUNK
    ng = t // tm

    def body(qp_ref, op_ref, c_ref, n_ref, gl_ref, ss_ref, do_ref, dqp_ref, dop_ref, dc_ref, dn_ref, dgl_ref, ds_scr):
        @pl.when(pl.program_id(0) == 0)
        def _():
            ds_scr[...] = jnp.zeros_like(ds_scr)

        d_state = [ds_scr[h] for h in range(DN_HEADS)]
        for cc in reversed(range(nb)):
            rows = pl.ds(cc * CHUNK, CHUNK)
            for h in range(DN_HEADS):
                cols = pl.ds(h * DN_DV, DN_DV)
                _, vjp = jax.vjp(_gdn_scan_step, qp_ref[rows, cols].astype(F32), op_ref[rows, cols], c_ref[h, cc].astype(F32),
                                 n_ref[h, cc].astype(F32), gl_ref[h, cc, pl.ds(0, 1), :], ss_ref[cc, h].astype(F32))
                dqp_ref[rows, cols], dop_ref[rows, cols], dc, dn, dgl, d_state[h] = vjp((do_ref[rows, cols], d_state[h]))
                dc_ref[h, cc] = dc.astype(dc_ref.dtype)
                dn_ref[h, cc] = dn.astype(dn_ref.dtype)
                first_row = lax.broadcasted_iota(jnp.int32, (GL_ROWS, LANE), 0) == 0
                dgl_ref[h, cc] = jnp.where(first_row, dgl, 0.0)
        for h in range(DN_HEADS):
            ds_scr[h] = d_state[h]

    five, shapes = _gdn_intra_specs(t, tm, (F32, F32, BF, BF, F32), order=lambda i: ng - 1 - i)
    row = five[0]
    return pl.pallas_call(
        body, grid=(ng,),
        in_specs=five + [pl.BlockSpec((nb, DN_HEADS, DN_DK, DN_DV), lambda i: (ng - 1 - i, 0, 0, 0)), row],
        out_specs=five, out_shape=shapes,
        scratch_shapes=[pltpu.VMEM((DN_HEADS, DN_DK, DN_DV), F32)], name="gdn_scan_bwd",
        compiler_params=pltpu.CompilerParams(dimension_semantics=("arbitrary",)),
    )(*intra, states, do)


def _gdn_out(o, z, g):
    parts = []
    for h in range(DN_HEADS):
        sl = slice(h * DN_DV, (h + 1) * DN_DV)
        parts.append(_rmsnorm(o[:, sl], g) * _silu(z[:, sl]))
    return parts


_Q_SCALE = math.log2(math.e) / math.sqrt(QK_NOPE + QK_ROPE)


def _rope_tables(pos, inv_freq2):
    lane = lax.broadcasted_iota(jnp.int32, (1, LANE), 1)
    ang = pos * inv_freq2
    cos = jnp.where(lane < QK_ROPE, jnp.cos(ang), 0.0)
    sin = jnp.where(lane < QK_ROPE // 2, -jnp.sin(ang), jnp.where(lane < QK_ROPE, jnp.sin(ang), 0.0))
    return cos, sin


@jax.custom_vjp
def _rope_swap(u):
    lane = lax.broadcasted_iota(jnp.int32, u.shape, 1)
    half = QK_ROPE // 2
    return jnp.where(lane < half, pltpu.roll(u, LANE - half, 1), jnp.where(lane < QK_ROPE, pltpu.roll(u, half, 1), 0.0))


_rope_swap.defvjp(lambda u: (_rope_swap(u), None), lambda _, g: (_rope_swap(g),))


def _mla_prep(cq, ckv, kr, gq, gkv, w_uq, w_ukv, cos, sin):
    rope = lambda u: u * cos + _rope_swap(u) * sin
    q_lin = _mmb_nt(_rmsnorm(cq, gq), w_uq) * _Q_SCALE
    kv_lin = _mmb_nt(_rmsnorm(ckv, gkv), w_ukv)
    k_rope = rope(kr)
    qs, ks, vs = [], [], []
    for h in range(MLA_HEADS):
        qs += [q_lin[:, h * LANE:(h + 1) * LANE], rope(q_lin[:, (MLA_HEADS + h) * LANE:(MLA_HEADS + h + 1) * LANE])]
        ks += [kv_lin[:, 2 * h * LANE:(2 * h + 1) * LANE], k_rope]
        vs += [kv_lin[:, (2 * h + 1) * LANE:(2 * h + 2) * LANE]]
    return qs + ks + vs


def _mla_prep_fwd(proj, pos_col, inv_freq2, gq, gkv, w_uq, w_ukv, tm):
    t = proj.shape[0]
    nq = 2 * MLA_HEADS

    def body(cq_ref, ckv_ref, kr_ref, pos_ref, f_ref, gq_ref, gkv_ref, wq_ref, wkv_ref, q_ref, k_ref, v_ref):
        cos, sin = _rope_tables(pos_ref[...], f_ref[...])
        outs = _mla_prep(cq_ref[...].astype(F32), ckv_ref[...].astype(F32), kr_ref[...].astype(F32), gq_ref[...], gkv_ref[...], wq_ref[...], wkv_ref[...],
                         cos, sin)
        for i in range(nq):
            q_ref[:, pl.ds(i * LANE, LANE)] = outs[i].astype(q_ref.dtype)
            k_ref[:, pl.ds(i * LANE, LANE)] = outs[nq + i].astype(k_ref.dtype)
        for h in range(MLA_HEADS):
            v_ref[:, pl.ds(h * LANE, LANE)] = outs[2 * nq + h].astype(v_ref.dtype)

    row = lambda w, j: pl.BlockSpec((tm, w), functools.partial(lambda i, j: (i, j), j=j))
    return pl.pallas_call(
        body, grid=(t // tm,),
        in_specs=[row(Q_LORA, P_CQ // Q_LORA), row(KV_LORA, P_CKV // KV_LORA), row(LANE, P_KR // LANE),
                  pl.BlockSpec((tm, 1), lambda i: (i, 0)), _full(inv_freq2.shape), _full(gq.shape), _full(gkv.shape),
                  _full(w_uq.shape), _full(w_ukv.shape)],
        out_specs=[row(nq * LANE, 0), row(nq * LANE, 0), row(MLA_VW, 0)],
        out_shape=[jax.ShapeDtypeStruct((t, nq * LANE), BF), jax.ShapeDtypeStruct((t, nq * LANE), BF),
                   jax.ShapeDtypeStruct((t, MLA_VW), BF)],
        name="mla_prep_fwd", compiler_params=pltpu.CompilerParams(dimension_semantics=("arbitrary",)),
    )(proj, proj, proj, pos_col, inv_freq2, gq, gkv, w_uq, w_ukv)


def _mla_prep_bwd(proj, pos_col, inv_freq2, gq, gkv, w_uq, w_ukv, dq, dk, dv, tm):
    t = proj.shape[0]
    nq = 2 * MLA_HEADS

    def body(cq_ref, ckv_ref, kr_ref, pos_ref, f_ref, gq_ref, gkv_ref, wq_ref, wkv_ref, dq_ref, dk_ref, dv_ref,
             dcq_ref, dckv_ref, dkr_ref, dgq_ref, dgkv_ref, dwq_ref, dwkv_ref):
        @pl.when(pl.program_id(0) == 0)
        def _():
            for o in (dgq_ref, dgkv_ref, dwq_ref, dwkv_ref):
                o[...] = jnp.zeros_like(o)

        cos, sin = _rope_tables(pos_ref[...], f_ref[...])
        f = functools.partial(_mla_prep, cos=cos, sin=sin)
        _, vjp = jax.vjp(f, cq_ref[...].astype(F32), ckv_ref[...].astype(F32), kr_ref[...].astype(F32), gq_ref[...], gkv_ref[...], wq_ref[...], wkv_ref[...])
        cts = [dq_ref[:, pl.ds(i * LANE, LANE)] for i in range(nq)]
        cts += [dk_ref[:, pl.ds(i * LANE, LANE)] for i in range(nq)]
        cts += [dv_ref[:, pl.ds(h * LANE, LANE)] for h in range(MLA_HEADS)]
        dcq, dckv, dkr, dgq, dgkv, dwq, dwkv = vjp(cts)
        dcq_ref[...] = dcq.astype(dcq_ref.dtype)
        dckv_ref[...] = dckv.astype(dckv_ref.dtype)
        dkr_ref[...] = dkr.astype(dkr_ref.dtype)
        dgq_ref[...] += dgq
        dgkv_ref[...] += dgkv
        dwq_ref[...] += dwq
        dwkv_ref[...] += dwkv

    row = lambda w, j: pl.BlockSpec((tm, w), functools.partial(lambda i, j: (i, j), j=j))
    return pl.pallas_call(
        body, grid=(t // tm,),
        in_specs=[row(Q_LORA, P_CQ // Q_LORA), row(KV_LORA, P_CKV // KV_LORA), row(LANE, P_KR // LANE),
                  pl.BlockSpec((tm, 1), lambda i: (i, 0)), _full(inv_freq2.shape), _full(gq.shape), _full(gkv.shape),
                  _full(w_uq.shape), _full(w_ukv.shape), row(nq * LANE, 0), row(nq * LANE, 0), row(MLA_VW, 0)],
        out_specs=[row(Q_LORA, 0), row(KV_LORA, 0), row(LANE, 0), _full(gq.shape), _full(gkv.shape),
                   _full(w_uq.shape), _full(w_ukv.shape)],
        out_shape=[jax.ShapeDtypeStruct((t, Q_LORA), BF), jax.ShapeDtypeStruct((t, KV_LORA), BF),
                   jax.ShapeDtypeStruct((t, LANE), BF), jax.ShapeDtypeStruct(gq.shape, F32),
                   jax.ShapeDtypeStruct(gkv.shape, F32), jax.ShapeDtypeStruct(w_uq.shape, F32),
                   jax.ShapeDtypeStruct(w_ukv.shape, F32)],
        name="mla_prep_bwd", compiler_params=pltpu.CompilerParams(dimension_semantics=("arbitrary",)),
    )(proj, proj, proj, pos_col, inv_freq2, gq, gkv, w_uq, w_ukv, dq, dk, dv)


_NEG = -1e30
_LN2 = math.log(2.0)
ATT_CHAINS = 2


def _causal(tq, tk, q0, k0):
    row = q0 + lax.broadcasted_iota(jnp.int32, (tq, tk), 0)
    col = k0 + lax.broadcasted_iota(jnp.int32, (tq, tk), 1)
    return col <= row


def _attn_fwd(q, k, v, tq, tk):
    t = q.shape[0]

    assert tk % tq == 0 or tq % tk == 0
    n_diag = max(1, tq // tk)

    th = tq // ATT_CHAINS

    def body(q_ref, k_ref, v_ref, o_ref, lse_ref):
        i = pl.program_id(1)
        n_full = (i * tq) // tk

        def step(k0, carry, masked):
            out = []
            for c, (m, l, acc) in enumerate(carry):
                kw = min(tk, (c + 1) * th) if masked and tk == tq else tk
                kt = k_ref[pl.ds(k0, kw), :]
                vt = v_ref[pl.ds(k0, kw), :]
                s = _dot(q_ref[pl.ds(c * th, th), :], kt, NT)
                if masked:
                    s = jnp.where(_causal(th, kw, i * tq + c * th, k0), s, _NEG)
                m_new = jnp.maximum(m, jnp.max(s, axis=-1, keepdims=True))
                p = jnp.exp2(s - m_new)
                alpha = jnp.exp2(m - m_new)
                out.append((m_new, alpha * l + jnp.sum(p, axis=-1, keepdims=True), alpha * acc + _dot(p.astype(BF), vt)))
            return tuple(out)

        init = tuple((jnp.full((th, 1), _NEG, F32), jnp.zeros((th, 1), F32), jnp.zeros((th, V_HEAD), F32)) for _ in range(ATT_CHAINS))
        carry = lax.fori_loop(0, n_full, lambda j, c: step(pl.multiple_of(j * tk, tk), c, False), init)
        for dd in range(n_diag):
            carry = step(pl.multiple_of((n_full + dd) * tk, tk), carry, True)
        for c, (m, l, acc) in enumerate(carry):
            o_ref[pl.ds(c * th, th), :] = acc / l
            lse_ref[pl.ds(c * th, th), :] = jnp.broadcast_to(m + jnp.log2(l), (th, LANE))

    return pl.pallas_call(
        body, grid=(MLA_HEADS, t // tq),
        in_specs=[pl.BlockSpec((tq, 2 * LANE), lambda h, i: (i, h)), pl.BlockSpec((t, 2 * LANE), lambda h, i: (0, h)),
                  pl.BlockSpec((t, V_HEAD), lambda h, i: (0, h))],
        out_specs=[pl.BlockSpec((tq, V_HEAD), lambda h, i: (i, h)), pl.BlockSpec((tq, LANE), lambda h, i: (i, h))],
        out_shape=[jax.ShapeDtypeStruct((t, MLA_VW), F32), jax.ShapeDtypeStruct((t, MLA_HEADS * LANE), F32)],
        name="attn_fwd", compiler_params=pltpu.CompilerParams(dimension_semantics=("parallel", "arbitrary")),
    )(q, k, v)


def _attn_bwd(q, k, v, do, lse, delta, tq, tk):
    t = q.shape[0]
    nkt = t // tk
    assert tk % tq == 0

    def body(q_ref, k_ref, v_ref, do_ref, lse_ref, dl_ref, dq_ref, dk_ref, dv_ref):
        j = pl.program_id(1)

        @pl.when(j == 0)
        def _():
            dq_ref[...] = jnp.zeros_like(dq_ref)

        kt = k_ref[...]
        vt = v_ref[...]

        def step(q0, carry, masked, kw=tk):
            dk, dv = carry
            rows = pl.ds(q0, tq)
            qt = q_ref[rows, :]
            dot_ = do_ref[rows, :]
            ktw, vtw = kt[:kw], vt[:kw]
            p = jnp.exp2(_dot(qt, ktw, NT) - lse_ref[rows, pl.ds(0, 1)])
            if masked:
                p = jnp.where(_causal(tq, kw, q0, j * tk), p, 0.0)
            dv_w = _dot(p.astype(BF), dot_, TN)
            ds = (p * (_dot(dot_, vtw, NT) - dl_ref[rows, pl.ds(0, 1)])).astype(BF)
            dk_w = _dot(ds, qt, TN)
            dq_ref[rows, :] += _dot(ds, ktw)
            if kw == tk:
                return dk + dk_w, dv + dv_w
            return (jnp.concatenate([dk[:kw] + dk_w, dk[kw:]], axis=0), jnp.concatenate([dv[:kw] + dv_w, dv[kw:]], axis=0))

        per = tk // tq
        carry = (jnp.zeros((tk, 2 * LANE), F32), jnp.zeros((tk, V_HEAD), F32))
        for dd in range(per):
            carry = step(pl.multiple_of(j * tk + dd * tq, tq), carry, True, kw=(dd + 1) * tq)

        def group(g, c):
            for dd in range(per):
                c = step(pl.multiple_of(g * tk + dd * tq, tq), c, False)
            return c

        dk, dv = lax.fori_loop(j + 1, nkt, group, carry)
        dk_ref[...] = dk * _LN2
        dv_ref[...] = dv

        @pl.when(j == nkt - 1)
        def _():
            dq_ref[...] = dq_ref[...] * _LN2

    return pl.pallas_call(
        body, grid=(MLA_HEADS, nkt),
        in_specs=[pl.BlockSpec((t, 2 * LANE), lambda h, j: (0, h)), pl.BlockSpec((tk, 2 * LANE), lambda h, j: (j, h)),
                  pl.BlockSpec((tk, V_HEAD), lambda h, j: (j, h)), pl.BlockSpec((t, V_HEAD), lambda h, j: (0, h)),
                  pl.BlockSpec((t, LANE), lambda h, j: (0, h)), pl.BlockSpec((t, LANE), lambda h, j: (0, h))],
        out_specs=[pl.BlockSpec((t, 2 * LANE), lambda h, j: (0, h)), pl.BlockSpec((tk, 2 * LANE), lambda h, j: (j, h)),
                   pl.BlockSpec((tk, V_HEAD), lambda h, j: (j, h))],
        out_shape=[jax.ShapeDtypeStruct((t, MLA_HEADS * 2 * LANE), F32), jax.ShapeDtypeStruct((t, MLA_HEADS * 2 * LANE), F32),
                   jax.ShapeDtypeStruct((t, MLA_VW), F32)],
        name="attn_bwd", compiler_params=pltpu.CompilerParams(dimension_semantics=("parallel", "arbitrary")),
    )(q, k, v, do, lse, delta)


def _adam_update(w, g, m, v):
    mm = ADAM_B1 * m + (1.0 - ADAM_B1) * g
    vv = ADAM_B2 * v + (1.0 - ADAM_B2) * jnp.square(g)
    m_hat = mm / (1.0 - ADAM_B1 ** ADAM_STEP)
    v_hat = vv / (1.0 - ADAM_B2 ** ADAM_STEP)
    return -ADAM_LR * (m_hat / (jnp.sqrt(v_hat) + ADAM_EPS) + ADAM_WD * w), mm, vv


def _adamw(w, g, m, v, name):
    r, c = w.shape
    tr = max([r // s for s in range(1, r // 8 + 1) if r % s == 0 and (r // s) % 8 == 0 and r // s <= 256] or [r])
    slots = g.ndim == 3

    def body(w_ref, g_ref, m_ref, v_ref, g_out, d_ref, nm_ref, nv_ref):
        if slots:
            gg = g_ref[0].astype(F32)
            for s in range(1, N_DEV):
                gg = gg + g_ref[s].astype(F32)
        else:
            gg = g_ref[...]
        g_out[...] = gg
        d_ref[...], nm_ref[...], nv_ref[...] = _adam_update(w_ref[...], gg, m_ref[...], v_ref[...])

    spec = pl.BlockSpec((tr, c), lambda i: (i, 0))
    g_spec = pl.BlockSpec((N_DEV, tr, c), lambda i: (0, i, 0)) if slots else spec
    return pl.pallas_call(
        body, grid=(r // tr,), in_specs=[spec, g_spec, spec, spec], out_specs=[spec] * 4,
        out_shape=[jax.ShapeDtypeStruct((r, c), F32)] * 4, name=name,
        compiler_params=pltpu.CompilerParams(dimension_semantics=("arbitrary",)),
    )(w, g, m, v)


def _adamw_many(ws, gs, ms, vs, name):
    n = len(ws)

    def body(*refs):
        for i in range(n):
            w_ref, g_ref, m_ref, v_ref = (refs[j * n + i] for j in range(4))
            d_ref, nm_ref, nv_ref = (refs[(4 + j) * n + i] for j in range(3))
            d_ref[...], nm_ref[...], nv_ref[...] = _adam_update(w_ref[...], g_ref[...], m_ref[...], v_ref[...])

    shapes = [jax.ShapeDtypeStruct(w.shape, F32) for w in ws]
    outs = pl.pallas_call(body, out_shape=shapes * 3, name=name)(*ws, *gs, *ms, *vs)
    return outs[:n], outs[n:2 * n], outs[2 * n:]


def _cast_bf16(xs, name, after=None):
    n = len(xs)
    extra = [] if after is None else [after]

    def body(*refs):
        outs = refs[n + len(extra):]
        for i in range(n):
            outs[i][...] = refs[i][...].astype(BF)

    vmem = pl.BlockSpec(memory_space=pltpu.VMEM)
    return pl.pallas_call(
        body, out_shape=[jax.ShapeDtypeStruct(x.shape, BF) for x in xs], name=name,
        in_specs=[vmem] * n + [pl.BlockSpec(memory_space=pl.ANY)] * len(extra), out_specs=[vmem] * n)(*xs, *extra)


def _pad_rows(a, n):
    return jnp.pad(a, ((0, n - a.shape[0]), (0, 0)))


def _w_in_to_padded(wt):
    s_ba = P_CQ
    s_cq = s_ba + 2 * DN_HEADS
    s_kr = s_cq + Q_LORA + KV_LORA
    return jnp.concatenate([wt[:s_ba], wt[s_cq:s_kr], _pad_rows(wt[s_ba:s_cq], LANE), _pad_rows(wt[s_kr:], LANE)], axis=0)


def _w_in_from_padded(wt):
    return jnp.concatenate([wt[:P_CQ], wt[P_BA:P_BA + 2 * DN_HEADS], wt[P_CQ:P_BA], wt[P_KR:P_KR + QK_ROPE]], axis=0)


def _w_uq_to_padded(wt):
    w3 = wt.reshape(MLA_HEADS, QK_NOPE + QK_ROPE, Q_LORA)
    nope = w3[:, :QK_NOPE].reshape(MLA_HEADS * QK_NOPE, Q_LORA)
    rope = jnp.pad(w3[:, QK_NOPE:], ((0, 0), (0, LANE - QK_ROPE), (0, 0))).reshape(MLA_HEADS * LANE, Q_LORA)
    return jnp.concatenate([nope, rope], axis=0)


def _w_uq_from_padded(wt):
    nope = wt[:MLA_HEADS * QK_NOPE].reshape(MLA_HEADS, QK_NOPE, Q_LORA)
    rope = wt[MLA_HEADS * QK_NOPE:].reshape(MLA_HEADS, LANE, Q_LORA)[:, :QK_ROPE]
    return jnp.concatenate([nope, rope], axis=1).reshape(MLA_HEADS * (QK_NOPE + QK_ROPE), Q_LORA)


def _pack(pieces, width, row_mult):
    flat = jnp.concatenate([p.reshape(-1) for p in pieces])
    n = flat.shape[0]
    rows = -(-n // (width * row_mult)) * row_mult
    return jnp.pad(flat, (0, rows * width - n)).reshape(rows, width)


def _unpack(flat, shapes):
    out, o = [], 0
    for s in shapes:
        n = math.prod(s)
        out.append(flat[o:o + n].reshape(s))
        o += n
    return out


def kernel(x, c, positions, w_ada, b_ada, w_in, conv_w, a_log, dt_bias, dn_norm_g, q_norm_g, w_uq, kv_norm_g, w_ukv, w_o, ln1_g, ln1_b, w_gate, w_up, w_down, ln2_g, ln2_b, loss_target, m_w_ada, m_b_ada, m_w_in, m_conv_w, m_a_log, m_dt_bias, m_dn_norm_g, m_q_norm_g, m_w_uq, m_kv_norm_g, m_w_ukv, m_w_o, m_ln1_g, m_ln1_b, m_w_gate, m_w_up, m_w_down, m_ln2_g, m_ln2_b, v_w_ada, v_b_ada, v_w_in, v_conv_w, v_a_log, v_dt_bias, v_dn_norm_g, v_q_norm_g, v_w_uq, v_kv_norm_g, v_w_ukv, v_w_o, v_ln1_g, v_ln1_b, v_w_gate, v_w_up, v_w_down, v_ln2_g, v_ln2_b):
    me = 4 * lax.axis_index("x") + 2 * lax.axis_index("y") + lax.axis_index("c")
    t, d = x.shape[1], x.shape[2]
    ada_n = w_ada.shape[2]

    tr = lambda w: w[0].T
    rows = lambda a: a.reshape(-1, a.shape[2])
    (in_shard,) = _cast_bf16([tr(w_in)], "cast_w_in")
    cw = conv_w.shape[3]
    a_in, c_all, conv_all = _gather_by_chip([in_shard, c, conv_w[0, :, 0, :]], "gather_w_in_and_small")
    c_all = c_all.reshape(N_DEV, d)
    conv_full = conv_all.transpose(1, 0, 2).reshape(CONV_K, N_DEV * cw)
    conv_w8 = jnp.pad(conv_full, ((0, 8 - CONV_K), (0, 0)))

    b_ada_mine = lax.dynamic_slice(b_ada, (0, me * ada_n), (1, ada_n))
    mod_cols = _mod_fwd(c_all, w_ada[0], b_ada_mine)
    (mod_all,) = _exchange([mod_cols.reshape(N_DEV, 1, ada_n)], "scatter_mod", scatter=True)
    mod = mod_all.reshape(1, N_DEV * ada_n)

    later = _cast_bf16([tr(w_uq), tr(w_ukv), w_o[0], tr(w_gate), tr(w_up), w_down[0]], "cast_weights", after=mod)
    mixer_gather, token_a = _exchange_start(later[:3], "gather_mixer_weights_start", scatter=False)
    ffn_gather, token_b = _exchange_start(later[3:], "gather_ffn_weights_start", scatter=False)
    mod = mod + (token_a + token_b)
    w_in_t = _w_in_to_padded(rows(a_in))

    def mixer_weights(after):
        a_uq, a_ukv, a_o = _exchange_wait(mixer_gather, after, "gather_mixer_weights_wait", scatter=False)
        return _w_uq_to_padded(rows(a_uq)), rows(a_ukv), rows(a_o)

    def ffn_weights(after):
        a_gate, a_up, a_down = _exchange_wait(ffn_gather, after, "gather_ffn_weights_wait", scatter=False)
        return rows(a_gate), rows(a_up), rows(a_down)

    def by_dest(g):
        return g.reshape(N_DEV, -1, g.shape[1])

    scatters = {}

    def grads_ready(tag, *g):
        if tag == "ffn":
            pieces = [by_dest(a) for a in g]
        elif tag == "mixer":
            g_w_o, g_w_uq_t, g_w_ukv_t = g
            pieces = [by_dest(g_w_o), by_dest(_w_uq_from_padded(g_w_uq_t).astype(BF)), by_dest(g_w_ukv_t.astype(BF))]
        else:
            pieces = [by_dest(_w_in_from_padded(g[0]))]
        scatters[tag], token = _exchange_start(pieces, "scatter_%s_grads_start" % tag, scatter=True)
        return token

    loc = _local_step(x[0], loss_target[0], positions[0], mod, w_in_t, mixer_weights, ffn_weights, grads_ready,
                      conv_w8, a_log, dt_bias, dn_norm_g, q_norm_g, kv_norm_g, ln1_g, ln1_b, ln2_g, ln2_b)
    grad_x, loss_acc, dmod, d_conv8, d_al8, d_dt8, d_dn_g, d_q_g, d_kv_g, d_ln1_g, d_ln1_b, d_ln2_g, d_ln2_b = loc

    small_shapes = [(6 * d,), (CONV_K, N_DEV * cw), (DN_HEADS,), (DN_HEADS,), (DN_DV,), (Q_LORA,), (KV_LORA,), (d,), (d,), (d,), (d,), (1,)]
    gsmall = _pack([dmod, d_conv8[:CONV_K], d_al8[0, :DN_HEADS], d_dt8[0, :DN_HEADS], d_dn_g, d_q_g, d_kv_g,
                    d_ln1_g, d_ln1_b, d_ln2_g, d_ln2_b, loss_acc[0, :1]], LANE, 8)
    (gsmall_all,) = _exchange([gsmall], "gather_small_grads", scatter=False)
    dmod_all = gsmall_all.reshape(N_DEV, -1)[:, :6 * d]
    tot = _unpack(_sum_slots(gsmall_all, "sum_small_grads").reshape(-1), small_shapes)
    g_b_ada, g_conv_full, g_a_log, g_dt_bias, g_dn_g, g_q_g, g_kv_g, g_ln1_g, g_ln1_b, g_ln2_g, g_ln2_b, loss1 = tot
    loss = loss1.reshape(())
    g_conv_w = lax.dynamic_slice(g_conv_full, (0, me * cw), (CONV_K, cw))
    g_w_ada = _mod_bwd(c_all.T, lax.dynamic_slice(dmod_all, (0, me * ada_n), (N_DEV, ada_n)))

    grads = {"w_ada": g_w_ada[None], "b_ada": g_b_ada[None], "conv_w": g_conv_w[None, :, None, :],
             "a_log": g_a_log[None], "dt_bias": g_dt_bias[None], "dn_norm_g": g_dn_g[None], "q_norm_g": g_q_g[None],
             "kv_norm_g": g_kv_g[None], "ln1_g": g_ln1_g[None], "ln1_b": g_ln1_b[None], "ln2_g": g_ln2_g[None], "ln2_b": g_ln2_b[None]}
    weights = dict(w_ada=w_ada, b_ada=b_ada, w_in=w_in, conv_w=conv_w, a_log=a_log, dt_bias=dt_bias, dn_norm_g=dn_norm_g,
                   q_norm_g=q_norm_g, w_uq=w_uq, kv_norm_g=kv_norm_g, w_ukv=w_ukv, w_o=w_o, ln1_g=ln1_g, ln1_b=ln1_b,
                   w_gate=w_gate, w_up=w_up, w_down=w_down, ln2_g=ln2_g, ln2_b=ln2_b)
    ms = dict(w_ada=m_w_ada, b_ada=m_b_ada, w_in=m_w_in, conv_w=m_conv_w, a_log=m_a_log, dt_bias=m_dt_bias,
              dn_norm_g=m_dn_norm_g, q_norm_g=m_q_norm_g, w_uq=m_w_uq, kv_norm_g=m_kv_norm_g, w_ukv=m_w_ukv, w_o=m_w_o,
              ln1_g=m_ln1_g, ln1_b=m_ln1_b, w_gate=m_w_gate, w_up=m_w_up, w_down=m_w_down, ln2_g=m_ln2_g, ln2_b=m_ln2_b)
    vs = dict(w_ada=v_w_ada, b_ada=v_b_ada, w_in=v_w_in, conv_w=v_conv_w, a_log=v_a_log, dt_bias=v_dt_bias,
              dn_norm_g=v_dn_norm_g, q_norm_g=v_q_norm_g, w_uq=v_w_uq, kv_norm_g=v_kv_norm_g, w_ukv=v_w_ukv, w_o=v_w_o,
              ln1_g=v_ln1_g, ln1_b=v_ln1_b, w_gate=v_w_gate, w_up=v_w_up, w_down=v_w_down, ln2_g=v_ln2_g, ln2_b=v_ln2_b)
    names = list(weights)
    big = ("w_ada", "w_gate", "w_up", "w_down", "w_o", "w_uq", "w_ukv", "w_in")
    waits = {"w_gate": ("ffn", ("w_gate", "w_up", "w_down")), "w_o": ("mixer", ("w_o", "w_uq", "w_ukv")), "w_in": ("in", ("w_in",))}
    delta_w, new_m, new_v, slots = {}, {}, {}, {}
    last = g_w_ada
    for n in big:
        if n == "w_in":
            rest = [r for r in names if r not in big]
            flat2 = lambda a: a.reshape(-1, a.shape[-1])
            outs = _adamw_many(*[[flat2(src[r]) for r in rest] for src in (weights, grads, ms, vs)], "adamw_small")
            for dst, o in zip((delta_w, new_m, new_v), outs):
                for r, a in zip(rest, o):
                    dst[r] = a.reshape(weights[r].shape)
            last = outs[0][0]
        transposed = n in ("w_in", "w_uq", "w_ukv", "w_gate", "w_up")
        two = (lambda a: a[0].T) if transposed else (lambda a: a[0])
        back = (lambda a: a.T[None]) if transposed else (lambda a: a[None])
        if n in waits:
            tag, members = waits[n]
            slots.update(zip(members, _exchange_wait(scatters[tag], last, "scatter_%s_grads_wait" % tag, scatter=True)))
        g_in = slots[n] if n in slots else two(grads[n])
        gr, dlt, nm, nv = _adamw(two(weights[n]), g_in, two(ms[n]), two(vs[n]), "adamw_" + n)
        grads[n], delta_w[n], new_m[n], new_v[n] = back(gr), back(dlt), back(nm), back(nv)
        last = nv

    return (loss, grad_x[None], *[grads[n] for n in names], *[delta_w[n] for n in names],
            *[new_m[n] for n in names], *[new_v[n] for n in names])


def _local_step(xs, tgt, pos, mod, w_in_t, mixer_weights, ffn_weights, grads_ready, conv_w8,
                a_log, dt_bias, dn_norm_g, q_norm_g, kv_norm_g, ln1_g, ln1_b, ln2_g, ln2_b):
    t, d = xs.shape
    sh_m, sc_m, gt_m, sh_f, sc_f, gt_f = [mod[:, i * d:(i + 1) * d] for i in range(6)]
    pos_col = pos.astype(F32).reshape(t, 1)
    inv_freq = 1.0 / (ROPE_THETA ** (jnp.arange(0, QK_ROPE, 2, dtype=F32) / QK_ROPE))
    inv_freq2 = jnp.pad(jnp.concatenate([inv_freq, inv_freq]), (0, LANE - QK_ROPE)).reshape(1, LANE)
    al8 = jnp.pad(a_log, ((0, 7), (0, LANE - DN_HEADS)))
    dt8 = jnp.pad(dt_bias, ((0, 7), (0, LANE - DN_HEADS)))

    tm = min(512, t)
    tq = min(256, t)
    tk = min(512, t)

    (h1,) = _rowwise("modulate_in", lambda xx, sc, sh: xx * (1.0 + sc) + sh, [xs], [sc_m, sh_m], [(d, BF)], [], tm)
    proj = _matmul(h1, w_in_t, "nt", "in_proj", BF)
    ba_raw = _matmul(h1, w_in_t[P_BA:P_BA + LANE], "nt", "in_proj_decay")
    qkv = _conv_fwd(proj, conv_w8, min(256, t))
    gdn_tm = min(512, t)
    intra, inverses = _gdn_intra_fwd(qkv, ba_raw, al8, dt8, gdn_tm)
    o_dn, states = _gdn_scan_fwd(intra, gdn_tm)
    w_uq_t, w_ukv_t, w_o_f = mixer_weights(states)
    qc, kc, vc = _mla_prep_fwd(proj, pos_col, inv_freq2, q_norm_g, kv_norm_g, w_uq_t, w_ukv_t, tm)
    o_mla, lse = _attn_fwd(qc, kc, vc, min(1024, t), min(1024, t))

    def mix_in(o, z, om, g):
        return jnp.concatenate(_gdn_out(o, z.astype(F32), g) + [om], axis=1)

    (mixin,) = _rowwise("mixer_out", mix_in, [o_dn, (proj, DN_VW, P_Z // DN_VW), o_mla], [dn_norm_g], [(2 * DN_VW, BF)], [], tm)
    mix = _matmul(mixin, w_o_f, "nn", "out_proj", BF)

    def block1(xx, mx, gt, g1, b1, sc, sh):
        x1 = _layernorm(DEEPNORM_ALPHA * xx + gt * mx, g1, b1)
        return x1, x1 * (1.0 + sc) + sh

    x1, h2 = _rowwise("norm1_modulate", block1, [xs, mix], [gt_m, ln1_g, ln1_b, sc_f, sh_f], [(d, F32), (d, BF)], [], tm)
    w_gate_f, w_up_f, w_down_f = ffn_weights(h2)
    act, gate, up = _ffn_in(h2, w_gate_f, w_up_f)
    ff = _matmul(act, w_down_f, "nn", "ffn_out", BF)

    def tail_loss(x1_, ff_, gt, g2, b2, tg):
        y = _layernorm(DEEPNORM_ALPHA * x1_ + gt * ff_, g2, b2)
        return 0.5 * jnp.sum(jnp.mean(jnp.square(y - tg), axis=-1))

    def tail(x1_, ff_, tg, gt, g2, b2):
        loss, (dx1, dff, dgt, dg2, db2) = jax.value_and_grad(tail_loss, argnums=(0, 1, 2, 3, 4))(x1_, ff_, gt, g2, b2, tg)
        return dx1, dff, jnp.full((1, LANE), loss, F32), dgt, dg2, db2

    dx1_a, dff, loss_acc, d_gt_f, d_ln2_g, d_ln2_b = _rowwise(
        "norm2_loss", tail, [x1, ff, tgt], [gt_f, ln2_g, ln2_b], [(d, BF), (d, BF)], [(1, LANE), (1, d), (1, d), (1, d)], tm)

    g_w_down = _matmul(act, dff, "tn", "d_w_down", BF)
    dgate, dup = _ffn_act_bwd(dff, w_down_f, gate, up)
    g_w_gate = _matmul(dgate, h2, "tn", "d_w_gate", BF)
    g_w_up = _matmul(dup, h2, "tn", "d_w_up", BF)
    token = grads_ready("ffn", g_w_gate, g_w_up, g_w_down)
    dh2 = _matmul2_nn(dgate, w_gate_f, dup, w_up_f, "d_ffn_in", BF)

    def block1_bwd(xx, mx, dx1_, dh2_, gt, g1, b1, sc, sh):
        _, vjp = jax.vjp(block1, xx, mx, gt, g1, b1, sc, sh)
        dxx, dmx, dgt, dg1, db1, dsc, dsh = vjp((dx1_.astype(F32), dh2_.astype(F32)))
        return dxx, dmx, dgt, dg1, db1, dsc, dsh

    dx_a, dmix, d_gt_m, d_ln1_g, d_ln1_b, d_sc_f, d_sh_f = _rowwise(
        "norm1_modulate_bwd", block1_bwd, [xs, mix, dx1_a, dh2], [gt_m + token, ln1_g, ln1_b, sc_f, sh_f],
        [(d, F32), (d, BF)], [(1, d)] * 5, min(256, t))

    dmixin = _matmul(dmix, w_o_f, "nt", "d_mixer_out", BF)
    g_w_o = _matmul(mixin, dmix, "tn", "d_w_o", BF)

    def mixer_bwd(o, z, om, dmi, g):
        _, vjp = jax.vjp(lambda o_, z_, g_: jnp.concatenate(_gdn_out(o_, z_, g_), axis=1), o, z.astype(F32), g)
        do_, dz_, dg_ = vjp(dmi[:, :DN_VW].astype(F32))
        dom = dmi[:, DN_VW:]
        delta = [jnp.broadcast_to(jnp.sum(dom[:, h * V_HEAD:(h + 1) * V_HEAD] * om[:, h * V_HEAD:(h + 1) * V_HEAD], axis=-1, keepdims=True), (o.shape[0], LANE))
                 for h in range(MLA_HEADS)]
        return do_, dz_, dom, jnp.concatenate(delta, axis=1), dg_

    do_dn, dz, do_mla, delta, d_dn_g = _rowwise(
        "mixer_out_bwd", mixer_bwd, [o_dn, (proj, DN_VW, P_Z // DN_VW), o_mla, dmixin], [dn_norm_g],
        [(DN_VW, F32), (DN_VW, BF), (MLA_VW, BF), (MLA_HEADS * LANE, F32)], [(1, DN_DV)], tm)

    dqc, dkc, dvc = _attn_bwd(qc, kc, vc, do_mla, lse, delta, min(512, t), min(1024, t))
    dcq, dckv, dkr, d_q_g, d_kv_g, g_w_uq_t, g_w_ukv_t = _mla_prep_bwd(
        proj, pos_col, inv_freq2, q_norm_g, kv_norm_g, w_uq_t, w_ukv_t, dqc, dkc, dvc, min(256, t))

    token = grads_ready("mixer", g_w_o, g_w_uq_t, g_w_ukv_t)

    d_intra = _gdn_scan_bwd(intra, states, do_dn, gdn_tm)
    dqkv_act, dba, d_al8, d_dt8 = _gdn_intra_bwd(qkv, ba_raw, al8 + token, dt8, inverses, d_intra, min(256, t))
    dqkv_pre, d_conv8 = _conv_bwd(proj, conv_w8, dqkv_act, min(256, t))

    dproj = jnp.concatenate([dqkv_pre, dz, dcq, dckv, dba, dkr], axis=1)
    dh1 = _matmul(dproj, w_in_t, "nn", "d_in_proj", BF)
    g_w_in_t = _matmul(dproj, h1, "tn", "d_w_in", BF)
    token = grads_ready("in", g_w_in_t)

    def modulate_bwd(xx, dh, dxa, sc):
        dh = dh.astype(F32)
        return dh * (1.0 + sc) + dxa, jnp.sum(dh * xx, axis=0, keepdims=True), jnp.sum(dh, axis=0, keepdims=True)

    grad_x, d_sc_m, d_sh_m = _rowwise("modulate_in_bwd", modulate_bwd, [xs, dh1, dx_a], [sc_m + token], [(d, F32)], [(1, d), (1, d)], tm)
    dmod = jnp.concatenate([d_sh_m, d_sc_m, d_gt_m, d_sh_f, d_sc_f, d_gt_f], axis=1)
    return grad_x, loss_acc, dmod, d_conv8, d_al8, d_dt8, d_dn_g, d_q_g, d_kv_g, d_ln1_g, d_ln1_b, d_ln2_g, d_ln2_b
```

```python
import functools
import math

import jax
import jax.numpy as jnp
from jax import lax
from jax.experimental import pallas as pl
from jax.experimental.pallas import tpu as pltpu

F32 = jnp.float32
BF = jnp.bfloat16
HI = lax.Precision.HIGHEST

N_DEV = 8
DN_HEADS = 4
DN_DK = 128
DN_DV = 128
CONV_K = 4
CHUNK = 64
MLA_HEADS = 4
QK_NOPE = 128
QK_ROPE = 64
V_HEAD = 128
Q_LORA = 512
KV_LORA = 256
ROPE_THETA = 10000.0
DEPTH = 1
DEEPNORM_ALPHA = (2.0 * DEPTH) ** 0.25
LANE = 128
CONV_HALO = 8
GL_ROWS = 8
CONV_ROWS, CONV_COLS = 64, 256

DN_QK = DN_HEADS * DN_DK
DN_VW = DN_HEADS * DN_DV
DN_CONV_CH = 2 * DN_QK + DN_VW
MLA_VW = MLA_HEADS * V_HEAD
P_Z = DN_CONV_CH
P_CQ = P_Z + DN_VW
P_CKV = P_CQ + Q_LORA
P_BA = P_CKV + KV_LORA
P_KR = P_BA + LANE
N_INP = P_KR + LANE

ADAM_LR = 0.001
ADAM_B1 = 0.9
ADAM_B2 = 0.999
ADAM_EPS = 1e-08
ADAM_WD = 0.01
ADAM_STEP = 10

NN = (((1,), (0,)), ((), ()))
NT = (((1,), (1,)), ((), ()))
TN = (((0,), (0,)), ((), ()))


def _pick(n, prefs):
    for p in prefs:
        if n % p == 0:
            return p
    return n


def _full(shape):
    return pl.BlockSpec(shape, lambda *_: (0,) * len(shape))


def _dot(a, b, dims=NN):
    return lax.dot_general(a, b, dims, preferred_element_type=F32)


def _doth(a, b, dims=NN):
    return lax.dot_general(a, b, dims, precision=HI, preferred_element_type=F32)


@jax.custom_vjp
def _mmb(a, b):
    return _dot(a.astype(BF), b.astype(BF), NN)


def _mmb_fwd(a, b):
    return _mmb(a, b), (a, b)


def _mmb_bwd(res, g):
    a, b = res
    gb = g.astype(BF)
    return (_dot(gb, b.astype(BF), NT).astype(a.dtype), _dot(a.astype(BF), gb, TN).astype(b.dtype))


_mmb.defvjp(_mmb_fwd, _mmb_bwd)


@jax.custom_vjp
def _mmb_nt(a, b):
    return _dot(a.astype(BF), b.astype(BF), NT)


def _mmb_nt_fwd(a, b):
    return _mmb_nt(a, b), (a, b)


def _mmb_nt_bwd(res, g):
    a, b = res
    gb = g.astype(BF)
    return (_dot(gb, b.astype(BF), NN).astype(a.dtype), _dot(gb, a.astype(BF), TN).astype(b.dtype))


_mmb_nt.defvjp(_mmb_nt_fwd, _mmb_nt_bwd)


def _sigmoid(x):
    return 0.5 * (jnp.tanh(0.5 * x) + 1.0)


def _silu(x):
    return x * _sigmoid(x)


def _softplus(x):
    return jnp.maximum(x, 0.0) + jnp.log(1.0 + jnp.exp(-jnp.abs(x)))


def _layernorm(x, g, b, eps=1e-5):
    mu = jnp.mean(x, axis=-1, keepdims=True)
    xc = x - mu
    var = jnp.mean(xc * xc, axis=-1, keepdims=True)
    return xc * lax.rsqrt(var + eps) * g + b


def _rmsnorm(x, g, eps=1e-6):
    return x * lax.rsqrt(jnp.mean(x * x, axis=-1, keepdims=True) + eps) * g


def _l2norm(x, eps=1e-6):
    return x * lax.rsqrt(jnp.sum(x * x, axis=-1, keepdims=True) + eps)


def _rowwise(name, fn, rows, vecs, out_rows, out_accs, tm):
    rows = [r if isinstance(r, tuple) else (r, r.shape[1], 0) for r in rows]
    t = rows[0][0].shape[0]
    tm = min(tm, t)
    assert t % tm == 0
    nr, nv, no = len(rows), len(vecs), len(out_rows)

    def body(*refs):
        ins = [r[...] for r in refs[:nr + nv]]
        outs = fn(*ins)
        outs = outs if isinstance(outs, (tuple, list)) else (outs,)
        o_rows = refs[nr + nv:nr + nv + no]
        o_accs = refs[nr + nv + no:]
        for o, val in zip(o_rows, outs[:no]):
            o[...] = val.astype(o.dtype)
        if o_accs:
            @pl.when(pl.program_id(0) == 0)
            def _():
                for o in o_accs:
                    o[...] = jnp.zeros_like(o)
            for o, val in zip(o_accs, outs[no:]):
                o[...] += val

    in_specs = [pl.BlockSpec((tm, w), functools.partial(lambda i, j: (i, j), j=j)) for (_, w, j) in rows]
    in_specs += [_full(v.shape) for v in vecs]
    out_specs = [pl.BlockSpec((tm, w), lambda i: (i, 0)) for (w, _) in out_rows]
    out_specs += [_full(s) for s in out_accs]
    out_shape = [jax.ShapeDtypeStruct((t, w), d) for (w, d) in out_rows]
    out_shape += [jax.ShapeDtypeStruct(s, F32) for s in out_accs]
    res = pl.pallas_call(
        body, grid=(t // tm,), in_specs=in_specs, out_specs=out_specs, out_shape=out_shape, name=name,
        compiler_params=pltpu.CompilerParams(dimension_semantics=("arbitrary",)),
    )(*[r[0] for r in rows], *vecs)
    return res


def _matmul(a, b, mode, name, out_dtype=F32):
    if mode == "nn":
        (m, k), n = a.shape, b.shape[1]
    elif mode == "nt":
        (m, k), n = a.shape, b.shape[0]
    else:
        (k, m), n = a.shape, b.shape[1]
    tm, tn, tk = _matmul_tiles(m, n, k, a.dtype.itemsize, b.dtype.itemsize, jnp.dtype(out_dtype).itemsize)
    nk = k // tk
    dims = {"nn": NN, "nt": NT, "tn": TN}[mode]

    def body(a_ref, b_ref, o_ref, *acc):
        part = _dot(a_ref[...].astype(BF), b_ref[...].astype(BF), dims)
        if nk == 1:
            o_ref[...] = part.astype(o_ref.dtype)
            return
        (acc_ref,) = acc
        kk = pl.program_id(2)

        @pl.when(kk == 0)
        def _():
            acc_ref[...] = part

        @pl.when(kk > 0)
        def _():
            acc_ref[...] += part

        @pl.when(kk == nk - 1)
        def _():
            o_ref[...] = acc_ref[...].astype(o_ref.dtype)

    a_spec = pl.BlockSpec((tk, tm), lambda i, j, kk: (kk, i)) if mode == "tn" else pl.BlockSpec((tm, tk), lambda i, j, kk: (i, kk))
    b_spec = pl.BlockSpec((tn, tk), lambda i, j, kk: (j, kk)) if mode == "nt" else pl.BlockSpec((tk, tn), lambda i, j, kk: (kk, j))
    return pl.pallas_call(
        body, grid=(m // tm, n // tn, nk), in_specs=[a_spec, b_spec],
        out_specs=pl.BlockSpec((tm, tn), lambda i, j, kk: (i, j)),
        out_shape=jax.ShapeDtypeStruct((m, n), out_dtype),
        scratch_shapes=[pltpu.VMEM((tm, tn), F32)] if nk > 1 else [], name=name,
        compiler_params=pltpu.CompilerParams(dimension_semantics=("parallel", "parallel", "arbitrary")),
    )(a, b)


def _lane_tile(n, cap):
    return max([n // s for s in range(1, n // LANE + 1) if n % s == 0 and (n // s) % LANE == 0 and n // s <= cap] or [n])


def _ffn_in(h, w_gate, w_up):
    m, k = h.shape
    f = w_gate.shape[0]
    tm, tn = _pick(m, (512, 256, 128)), _lane_tile(f, 1408)

    def body(h_ref, wg_ref, wu_ref, act_ref, dg_ref, du_ref):
        hh = h_ref[...]
        g = _dot(hh, wg_ref[...], NT)
        u = _dot(hh, wu_ref[...], NT)
        sg = _sigmoid(g)
        silu_g = g * sg
        act_ref[...] = (silu_g * u).astype(act_ref.dtype)
        dg_ref[...] = (u * (sg + silu_g * (1.0 - sg))).astype(dg_ref.dtype)
        du_ref[...] = silu_g.astype(du_ref.dtype)

    w_spec = pl.BlockSpec((tn, k), lambda i, j: (j, 0))
    o_spec = pl.BlockSpec((tm, tn), lambda i, j: (i, j))
    return pl.pallas_call(
        body, grid=(m // tm, f // tn), in_specs=[pl.BlockSpec((tm, k), lambda i, j: (i, 0)), w_spec, w_spec],
        out_specs=[o_spec] * 3, out_shape=[jax.ShapeDtypeStruct((m, f), BF)] * 3, name="ffn_in",
        compiler_params=pltpu.CompilerParams(dimension_semantics=("parallel", "parallel")),
    )(h, w_gate, w_up)


def _ffn_act_bwd(dff, w_down, act_dg, act_du):
    m, k = dff.shape
    f = w_down.shape[0]
    tm, tn = _pick(m, (512, 256, 128)), _lane_tile(f, 1408)

    def body(d_ref, w_ref, fg_ref, fu_ref, dg_ref, du_ref):
        da = _dot(d_ref[...], w_ref[...], NT)
        dg_ref[...] = (da * fg_ref[...].astype(F32)).astype(dg_ref.dtype)
        du_ref[...] = (da * fu_ref[...].astype(F32)).astype(du_ref.dtype)

    o_spec = pl.BlockSpec((tm, tn), lambda i, j: (i, j))
    return pl.pallas_call(
        body, grid=(m // tm, f // tn),
        in_specs=[pl.BlockSpec((tm, k), lambda i, j: (i, 0)), pl.BlockSpec((tn, k), lambda i, j: (j, 0)), o_spec, o_spec],
        out_specs=[o_spec] * 2, out_shape=[jax.ShapeDtypeStruct((m, f), BF)] * 2, name="d_ffn_act",
        compiler_params=pltpu.CompilerParams(dimension_semantics=("parallel", "parallel")),
    )(dff, w_down, act_dg, act_du)


def _matmul2_nn(a1, b1, a2, b2, name, out_dtype=F32):
    m, k = a1.shape
    n = b1.shape[1]
    tm, tn = _pick(m, (1024, 512, 256, 128)), _pick(n, (512, 256, 128))

    def body(a1_ref, b1_ref, a2_ref, b2_ref, o_ref):
        o_ref[...] = (_dot(a1_ref[...], b1_ref[...]) + _dot(a2_ref[...], b2_ref[...])).astype(o_ref.dtype)

    a_spec = pl.BlockSpec((tm, k), lambda i, j: (i, 0))
    b_spec = pl.BlockSpec((k, tn), lambda i, j: (0, j))
    return pl.pallas_call(
        body, grid=(m // tm, n // tn), in_specs=[a_spec, b_spec, a_spec, b_spec],
        out_specs=pl.BlockSpec((tm, tn), lambda i, j: (i, j)), out_shape=jax.ShapeDtypeStruct((m, n), out_dtype), name=name,
        compiler_params=pltpu.CompilerParams(dimension_semantics=("parallel", "parallel")),
    )(a1, b1, a2, b2)


MATMUL_VMEM_BUDGET = 28 * 1024 * 1024


def _matmul_tiles(m, n, k, a_bytes, b_bytes, o_bytes):
    def divisors(x, cap):
        return sorted({x // s for s in range(1, 65) if x % s == 0 and (x // s) % LANE == 0 and x // s <= cap}, reverse=True) or [x]

    for tk in divisors(k, k):
        best = None
        for tm in divisors(m, 1024):
            for tn in divisors(n, 2048):
                need = 2 * (tm * tk * a_bytes + tk * tn * b_bytes + tm * tn * o_bytes) + (tm * tn * 4 if tk < k else 0)
                if need <= MATMUL_VMEM_BUDGET and tm * tn >= 512 * 512 and (best is None or tm * tn > best[0] * best[1]):
                    best = (tm, tn)
        if best:
            return best[0], best[1], tk
    return _pick(m, (512, 256, 128)), _pick(n, (512, 256, 128)), _pick(k, (512, 256, 128))


def _exchange(xs, name, scatter):
    n = len(xs)
    npeer = N_DEV - 1

    def body(*refs):
        x_refs, o_refs = refs[:n], refs[n:2 * n]
        send_sems, recv_sems, local_sems = refs[2 * n:]
        mx, my, mc = lax.axis_index("x"), lax.axis_index("y"), lax.axis_index("c")
        me = 4 * mx + 2 * my + mc
        src_me = [x.at[me] if scatter else x for x in x_refs]
        mine = [pltpu.make_async_copy(src_me[a], o_refs[a].at[me], local_sems.at[a]) for a in range(n)]
        for cp in mine:
            cp.start()
        copies = []
        for k in range(1, N_DEV):
            px, py, pc = mx ^ (k >> 2), my ^ ((k >> 1) & 1), mc ^ (k & 1)
            peer = 4 * px + 2 * py + pc
            for a in range(n):
                cp = pltpu.make_async_remote_copy(
                    src_ref=x_refs[a].at[peer] if scatter else x_refs[a], dst_ref=o_refs[a].at[me],
                    send_sem=send_sems.at[a * npeer + k - 1], recv_sem=recv_sems.at[a * npeer + k - 1],
                    device_id=(px, py, pc), device_id_type=pl.DeviceIdType.MESH)
                cp.start()
                copies.append((cp, a, k, peer))
        for cp, a, k, peer in copies:
            pltpu.make_async_remote_copy(
                src_ref=src_me[a], dst_ref=o_refs[a].at[peer], send_sem=send_sems.at[a * npeer + k - 1],
                recv_sem=recv_sems.at[a * npeer + k - 1], device_id=(mx, my, mc),
                device_id_type=pl.DeviceIdType.MESH).wait_recv()
        for cp, _, _, _ in copies:
            cp.wait_send()
        for cp in mine:
            cp.wait()

    return pl.pallas_call(
        body, out_shape=[jax.ShapeDtypeStruct((N_DEV,) + x.shape[-2:], x.dtype) for x in xs],
        in_specs=[pl.BlockSpec(memory_space=pl.ANY)] * n, out_specs=[pl.BlockSpec(memory_space=pl.ANY)] * n,
        scratch_shapes=[pltpu.SemaphoreType.DMA((n * npeer,)), pltpu.SemaphoreType.DMA((n * npeer,)),
                        pltpu.SemaphoreType.DMA((n,))],
        name=name,
    )(*xs)


def _gather_by_chip(xs, name):
    n = len(xs)
    per = N_DEV - 1

    def body(*refs):
        x_refs, o_refs = refs[:n], refs[n:2 * n]
        send_sems, recv_sems, local_sems = refs[2 * n:]
        mx, my, mc = lax.axis_index("x"), lax.axis_index("y"), lax.axis_index("c")
        me, sibling = (mx, my, mc), (mx, my, 1 - mc)
        chips = [(1 - mx, my), (mx, 1 - my), (1 - mx, 1 - my)]
        slot = lambda d: 4 * d[0] + 2 * d[1] + d[2]

        def copy(a, k, block, to, src=None):
            dst = o_refs[a].at[slot(block)]
            return pltpu.make_async_remote_copy(
                src_ref=dst if src is None else src, dst_ref=dst, send_sem=send_sems.at[a * per + k],
                recv_sem=recv_sems.at[a * per + k], device_id=to, device_id_type=pl.DeviceIdType.MESH)

        mine = [pltpu.make_async_copy(x_refs[a], o_refs[a].at[slot(me)], local_sems.at[a]) for a in range(n)]
        for cp in mine:
            cp.start()
        first = []
        for a in range(n):
            first.append(copy(a, 0, me, sibling, src=x_refs[a]))
            first += [copy(a, 1 + j, me, (*chip, mc), src=x_refs[a]) for j, chip in enumerate(chips)]
        for cp in first:
            cp.start()
        passed = []
        for j, chip in enumerate(chips):
            for a in range(n):
                copy(a, 1 + j, (*chip, mc), me).wait_recv()
                cp = copy(a, 4 + j, (*chip, mc), sibling)
                cp.start()
                passed.append(cp)
        for a in range(n):
            copy(a, 0, sibling, me).wait_recv()
            for j, chip in enumerate(chips):
                copy(a, 4 + j, (*chip, 1 - mc), me).wait_recv()
        for cp in first + passed:
            cp.wait_send()
        for cp in mine:
            cp.wait()

    return pl.pallas_call(
        body, out_shape=[jax.ShapeDtypeStruct((N_DEV,) + x.shape, x.dtype) for x in xs],
        in_specs=[pl.BlockSpec(memory_space=pl.ANY)] * n, out_specs=[pl.BlockSpec(memory_space=pl.ANY)] * n,
        scratch_shapes=[pltpu.SemaphoreType.DMA((n * per,)), pltpu.SemaphoreType.DMA((n * per,)),
                        pltpu.SemaphoreType.DMA((n,))],
        name=name,
    )(*xs)


def _peer_of(k):
    mx, my, mc = lax.axis_index("x"), lax.axis_index("y"), lax.axis_index("c")
    px, py, pc = mx ^ (k >> 2), my ^ ((k >> 1) & 1), mc ^ (k & 1)
    return (px, py, pc), 4 * px + 2 * py + pc


def _exchange_start(xs, name, scatter):
    n = len(xs)
    npeer = N_DEV - 1

    def body(*refs):
        x_refs, land_refs = refs[:n], refs[n:2 * n]
        send_sems, recv_sems, token = refs[2 * n], refs[2 * n + 1], refs[-1]
        me = 4 * lax.axis_index("x") + 2 * lax.axis_index("y") + lax.axis_index("c")
        for k in range(1, N_DEV):
            dev, peer = _peer_of(k)
            for a in range(n):
                pltpu.make_async_remote_copy(
                    src_ref=x_refs[a].at[peer] if scatter else x_refs[a], dst_ref=land_refs[a].at[me],
                    send_sem=send_sems.at[a * npeer + k - 1], recv_sem=recv_sems.at[a * npeer + k - 1],
                    device_id=dev, device_id_type=pl.DeviceIdType.MESH).start()
        token[...] = jnp.zeros_like(token)

    hbm = pl.BlockSpec(memory_space=pltpu.HBM)
    sem = pl.BlockSpec(memory_space=pltpu.SEMAPHORE)
    lands = [pltpu.with_memory_space_constraint(lax.empty((N_DEV,) + x.shape[-2:], x.dtype), pltpu.HBM) for x in xs]
    srcs = [pltpu.with_memory_space_constraint(x, pltpu.HBM) for x in xs]
    outs = pl.pallas_call(
        body, name=name,
        out_shape=(pltpu.SemaphoreType.DMA((n * npeer,)), pltpu.SemaphoreType.DMA((n * npeer,)),
                   *[pltpu.HBM(x.shape, x.dtype) for x in srcs], *[pltpu.HBM(z.shape, z.dtype) for z in lands],
                   jax.ShapeDtypeStruct((8, LANE), F32)),
        in_specs=[hbm] * (2 * n), out_specs=(sem, sem, *[hbm] * (2 * n), pl.BlockSpec(memory_space=pltpu.VMEM)),
        input_output_aliases={i: 2 + i for i in range(2 * n)},
        compiler_params=pltpu.CompilerParams(has_side_effects=pltpu.SideEffectType.DATAFLOW_SIDE_EFFECTING),
    )(*srcs, *lands)
    return (outs[0], outs[1], list(outs[2:2 + n]), list(outs[2 + n:2 + 2 * n])), outs[-1][0:1, 0:1]


def _exchange_wait(started, after, name, scatter):
    send_sems, recv_sems, srcs, lands = started
    n = len(srcs)
    npeer = N_DEV - 1

    def body(*refs):
        x_refs, land_refs = refs[:n], refs[n:2 * n]
        send_sems, recv_sems = refs[2 * n], refs[2 * n + 1]
        mx, my, mc = lax.axis_index("x"), lax.axis_index("y"), lax.axis_index("c")
        me = 4 * mx + 2 * my + mc
        for k in range(1, N_DEV):
            _, peer = _peer_of(k)
            for a in range(n):
                src = x_refs[a].at[me] if scatter else x_refs[a]
                cp = pltpu.make_async_remote_copy(
                    src_ref=src, dst_ref=land_refs[a].at[peer], send_sem=send_sems.at[a * npeer + k - 1],
                    recv_sem=recv_sems.at[a * npeer + k - 1], device_id=(mx, my, mc), device_id_type=pl.DeviceIdType.MESH)
                cp.wait_send()
                cp.wait_recv()

    hbm = pl.BlockSpec(memory_space=pltpu.HBM)
    sem = pl.BlockSpec(memory_space=pltpu.SEMAPHORE)
    outs = pl.pallas_call(
        body, name=name,
        out_shape=(*[pltpu.HBM(x.shape, x.dtype) for x in srcs], *[pltpu.HBM(z.shape, z.dtype) for z in lands]),
        in_specs=[hbm] * (2 * n) + [sem, sem, pl.BlockSpec(memory_space=pl.ANY)], out_specs=tuple([hbm] * (2 * n)),
        input_output_aliases={i: i for i in range(2 * n)},
        compiler_params=pltpu.CompilerParams(has_side_effects=pltpu.SideEffectType.DATAFLOW_SIDE_EFFECTING),
    )(*srcs, *lands, send_sems, recv_sems, after)
    me = 4 * lax.axis_index("x") + 2 * lax.axis_index("y") + lax.axis_index("c")
    full = []
    for x, land in zip(outs[:n], outs[n:]):
        own = lax.dynamic_slice(x, (me, 0, 0), (1,) + x.shape[1:]) if scatter else x[None]
        full.append(lax.dynamic_update_slice(land, own, (me, 0, 0)))
    return full


def _sum_slots(x, name):
    _, r, c = x.shape
    tr = _pick(r, (512, 256, 128, 64, 32, 16))

    def body(x_ref, o_ref):
        acc = x_ref[0].astype(F32)
        for s in range(1, N_DEV):
            acc = acc + x_ref[s].astype(F32)
        o_ref[...] = acc

    return pl.pallas_call(
        body, grid=(r // tr,), in_specs=[pl.BlockSpec((N_DEV, tr, c), lambda i: (0, i, 0))],
        out_specs=pl.BlockSpec((tr, c), lambda i: (i, 0)), out_shape=jax.ShapeDtypeStruct((r, c), F32), name=name,
        compiler_params=pltpu.CompilerParams(dimension_semantics=("arbitrary",)),
    )(x)


def _mod_fwd(c_all, w_ada, b_ada_mine):
    def body(c_ref, w_ref, b_ref, o_ref):
        o_ref[...] = _doth(_silu(c_ref[...]), w_ref[...]) + b_ref[...]

    return pl.pallas_call(body, out_shape=jax.ShapeDtypeStruct((c_all.shape[0], w_ada.shape[1]), F32), name="mod_fwd")(c_all, w_ada, b_ada_mine)


def _mod_bwd(c_all_t, dmod_mine):
    def body(ct_ref, d_ref, o_ref):
        s = _silu(ct_ref[...])
        acc = s[:, 0:1] * d_ref[pl.ds(0, 1), :]
        for b in range(1, N_DEV):
            acc = acc + s[:, b:b + 1] * d_ref[pl.ds(b, 1), :]
        o_ref[...] = acc

    return pl.pallas_call(body, out_shape=jax.ShapeDtypeStruct((c_all_t.shape[0], dmod_mine.shape[1]), F32), name="mod_bwd")(c_all_t, dmod_mine)


def _conv_fwd(proj, conv_w8, tm):
    t = proj.shape[0]
    ch = DN_CONV_CH

    def body(x_ref, w_ref, o_ref, buf):
        @pl.when(pl.program_id(0) == 0)
        def _():
            buf[pl.ds(0, CONV_HALO), :] = jnp.zeros((CONV_HALO, ch), F32)

        buf[pl.ds(CONV_HALO, tm), :] = x_ref[...].astype(F32)
        for c0 in range(0, ch, CONV_COLS):
            cols = pl.ds(c0, CONV_COLS)
            w = [w_ref[pl.ds(j, 1), cols] for j in range(CONV_K)]
            for r0 in range(0, tm, CONV_ROWS):
                acc = buf[pl.ds(r0 + CONV_HALO - (CONV_K - 1), CONV_ROWS), cols] * w[0]
                for j in range(1, CONV_K):
                    acc = acc + buf[pl.ds(r0 + CONV_HALO - (CONV_K - 1) + j, CONV_ROWS), cols] * w[j]
                o_ref[pl.ds(r0, CONV_ROWS), cols] = _silu(acc)
        buf[pl.ds(0, CONV_HALO), :] = buf[pl.ds(tm, CONV_HALO), :]

    return pl.pallas_call(
        body, grid=(t // tm,), in_specs=[pl.BlockSpec((tm, ch), lambda i: (i, 0)), _full(conv_w8.shape)],
        out_specs=pl.BlockSpec((tm, ch), lambda i: (i, 0)), out_shape=jax.ShapeDtypeStruct((t, ch), F32),
        scratch_shapes=[pltpu.VMEM((tm + CONV_HALO, ch), F32)], name="conv_fwd",
        compiler_params=pltpu.CompilerParams(dimension_semantics=("arbitrary",)),
    )(proj, conv_w8)


def _conv_bwd(proj, conv_w8, dact, tm):
    t = proj.shape[0]
    ch = DN_CONV_CH
    nt = t // tm
    halo_blk = 2 * CONV_HALO
    hb = tm // halo_blk

    def body(x_ref, xp_ref, w_ref, dy_ref, dx_ref, dw_ref, xbuf, dbuf):
        step = pl.program_id(0)

        @pl.when(step == 0)
        def _():
            dbuf[pl.ds(tm, CONV_HALO), :] = jnp.zeros((CONV_HALO, ch), F32)
            dw_ref[...] = jnp.zeros_like(dw_ref)

        first = step == nt - 1
        xbuf[pl.ds(0, CONV_HALO), :] = jnp.where(first, 0.0, xp_ref[...].astype(F32)[halo_blk - CONV_HALO:])
        xbuf[pl.ds(CONV_HALO, tm), :] = x_ref[...].astype(F32)
        for c0 in range(0, ch, CONV_COLS):
            cols = pl.ds(c0, CONV_COLS)
            w = [w_ref[pl.ds(j, 1), cols] for j in range(CONV_K)]
            dw = [jnp.zeros((1, CONV_COLS), F32) for _ in range(CONV_K)]
            for r0 in range(0, tm, CONV_ROWS):
                xs = [xbuf[pl.ds(r0 + CONV_HALO - (CONV_K - 1) + j, CONV_ROWS), cols] for j in range(CONV_K)]
                pre = xs[0] * w[0]
                for j in range(1, CONV_K):
                    pre = pre + xs[j] * w[j]
                sg = _sigmoid(pre)
                dpre = dy_ref[pl.ds(r0, CONV_ROWS), cols] * (sg * (1.0 + pre * (1.0 - sg)))
                dbuf[pl.ds(r0, CONV_ROWS), cols] = dpre
                dw = [dw[j] + jnp.sum(dpre * xs[j], axis=0, keepdims=True) for j in range(CONV_K)]
            for j in range(CONV_K):
                dw_ref[pl.ds(j, 1), cols] += dw[j]
            for r0 in range(0, tm, CONV_ROWS):
                dx = dbuf[pl.ds(r0 + CONV_K - 1, CONV_ROWS), cols] * w[0]
                for j in range(1, CONV_K):
                    dx = dx + dbuf[pl.ds(r0 + CONV_K - 1 - j, CONV_ROWS), cols] * w[j]
                dx_ref[pl.ds(r0, CONV_ROWS), cols] = dx.astype(dx_ref.dtype)
        dbuf[pl.ds(tm, CONV_HALO), :] = dbuf[pl.ds(0, CONV_HALO), :]

    rev = lambda i: (nt - 1 - i, 0)
    prev = lambda i: (jnp.maximum((nt - 1 - i) * hb - 1, 0), 0)
    return pl.pallas_call(
        body, grid=(nt,),
        in_specs=[pl.BlockSpec((tm, ch), rev), pl.BlockSpec((halo_blk, ch), prev), _full(conv_w8.shape),
                  pl.BlockSpec((tm, ch), rev)],
        out_specs=[pl.BlockSpec((tm, ch), rev), _full(conv_w8.shape)],
        out_shape=[jax.ShapeDtypeStruct((t, ch), BF), jax.ShapeDtypeStruct(conv_w8.shape, F32)],
        scratch_shapes=[pltpu.VMEM((tm + CONV_HALO, ch), F32), pltpu.VMEM((tm + CONV_HALO, ch), F32)], name="conv_bwd",
        compiler_params=pltpu.CompilerParams(dimension_semantics=("arbitrary",)),
    )(proj, proj, conv_w8, dact)


BNN = (((2,), (1,)), ((0,), (0,)))
BNT = (((2,), (2,)), ((0,), (0,)))
BTN = (((1,), (1,)), ((0,), (0,)))


def _bdot(a, b, dims, precision=None):
    return lax.dot_general(a, b, dims, precision=precision, preferred_element_type=F32)


@jax.custom_vjp
def _bmmb_nt(a, b):
    return _bdot(a.astype(BF), b.astype(BF), BNT)


def _bmmb_nt_fwd(a, b):
    return _bmmb_nt(a, b), (a, b)


def _bmmb_nt_bwd(res, g):
    a, b = res
    gb = g.astype(BF)
    return _bdot(gb, b.astype(BF), BNN), _bdot(gb, a.astype(BF), BTN)


_bmmb_nt.defvjp(_bmmb_nt_fwd, _bmmb_nt_bwd)


@jax.custom_vjp
def _bmmb(a, b):
    return _bdot(a.astype(BF), b.astype(BF), BNN)


def _bmmb_fwd(a, b):
    return _bmmb(a, b), (a, b)


def _bmmb_bwd(res, g):
    a, b = res
    gb = g.astype(BF)
    return _bdot(gb, b.astype(BF), BNT), _bdot(a.astype(BF), gb, BTN)


_bmmb.defvjp(_bmmb_fwd, _bmmb_bwd)


@jax.custom_vjp
def _bmmb_tn(a, b):
    return _bdot(a.astype(BF), b.astype(BF), BTN)


def _bmmb_tn_fwd(a, b):
    return _bmmb_tn(a, b), (a, b)


def _bmmb_tn_bwd(res, g):
    a, b = res
    gb = g.astype(BF)
    return _bdot(b.astype(BF), gb, BNT), _bdot(a.astype(BF), gb, BNN)


_bmmb_tn.defvjp(_bmmb_tn_fwd, _bmmb_tn_bwd)


def _unit_lower_solve_fwd(a, r):
    c = a.shape[-1]
    ri = lax.broadcasted_iota(jnp.int32, a.shape, 1)
    ci = lax.broadcasted_iota(jnp.int32, a.shape, 2)
    xm = -a
    inv = (ri == ci).astype(F32) + xm
    for _ in range(int(math.log2(c)) - 1):
        xm = _bdot(xm, xm, BNN, HI)
        inv = inv + _bdot(inv, xm, BNN, HI)
    x = _bdot(inv, r, BNN, HI)
    return x, (inv, x)


def _unit_lower_solve_bwd(res, g):
    inv, x = res
    dr = _bdot(inv, g, BTN, HI)
    return -_bdot(dr, x, BNT, HI), dr


@jax.custom_vjp
def _unit_lower_solve_given(a, r, inv):
    return _bdot(inv, r, BNN, HI)


def _unit_lower_solve_given_fwd(a, r, inv):
    x = _bdot(inv, r, BNN, HI)
    return x, (inv, x)


def _unit_lower_solve_given_bwd(res, g):
    da, dr = _unit_lower_solve_bwd(res, g)
    return da, dr, jnp.zeros_like(res[0])


_unit_lower_solve_given.defvjp(_unit_lower_solve_given_fwd, _unit_lower_solve_given_bwd)


def _gdn_intra(qkv, ba, al8, dt8, inv4=None):
    tm = qkv.shape[0]
    nb = tm // CHUNK
    bsz = DN_HEADS * nb

    def heads(x0):
        return jnp.concatenate([qkv[:, x0 + h * LANE:x0 + (h + 1) * LANE].reshape(nb, CHUNK, LANE) for h in range(DN_HEADS)], axis=0)

    def spread(c0):
        return jnp.concatenate([jnp.broadcast_to(ba[:, c0 + h:c0 + h + 1], (tm, LANE)).reshape(nb, CHUNK, LANE)
                                for h in range(DN_HEADS)], axis=0)

    def per_head(v8):
        return jnp.concatenate([jnp.broadcast_to(v8[0:1, h:h + 1].reshape(1, 1, 1), (nb, 1, LANE)) for h in range(DN_HEADS)], axis=0)

    ri = lax.broadcasted_iota(jnp.int32, (bsz, CHUNK, CHUNK), 1)
    ci = lax.broadcasted_iota(jnp.int32, (bsz, CHUNK, CHUNK), 2)
    incl = ri >= ci
    strict = ri > ci

    q = _l2norm(heads(0)) * (DN_DK ** -0.5)
    k = _l2norm(heads(DN_QK))
    va = heads(2 * DN_QK)
    beta = _sigmoid(spread(0))
    g = -jnp.exp(per_head(al8)) * _softplus(spread(DN_HEADS) + per_head(dt8))
    gc = _bdot(incl.astype(F32), g, BNN, HI)
    g_last = jnp.sum(g, axis=1, keepdims=True)
    gcol = gc[:, :, :CHUNK]
    diff = gcol - jnp.swapaxes(gcol, 1, 2)
    decay = jnp.where(incl, jnp.exp(jnp.where(incl, diff, 0.0)), 0.0)
    kb = k * beta
    a_mat = jnp.where(strict, _bmmb_nt(kb, k) * decay, 0.0)
    egc = jnp.exp(gc)
    rhs = jnp.concatenate([kb * egc, va * beta], axis=2)
    if inv4 is None:
        wu, (inv, _) = _unit_lower_solve_fwd(a_mat, rhs)
    else:
        wu = _unit_lower_solve_given(a_mat, rhs, inv4.reshape(bsz, CHUNK, CHUNK))
    attn = jnp.where(incl, _bmmb_nt(q, k) * decay, 0.0)

    def unheads(x):
        return jnp.concatenate([x[h * nb:(h + 1) * nb].reshape(tm, LANE) for h in range(DN_HEADS)], axis=1)

    w_c, u_c = wu[:, :, :DN_DK], wu[:, :, DN_DK:]
    kd = k * jnp.exp(g_last - gc)
    out = (unheads(q * egc - _bmmb(attn, w_c)), unheads(_bmmb(attn, u_c)),
           _bmmb_tn(kd, w_c).reshape(DN_HEADS, nb, DN_DK, DN_DK), _bmmb_tn(kd, u_c).reshape(DN_HEADS, nb, DN_DK, DN_DV),
           jnp.broadcast_to(g_last, (bsz, GL_ROWS, LANE)).reshape(DN_HEADS, nb, GL_ROWS, LANE))
    return out if inv4 is not None else out + (inv.reshape(DN_HEADS, nb, CHUNK, CHUNK),)


def _gdn_scan_step(qp, op, c_mat, n_mat, gl, s):
    return _mmb(qp, s) + op, s * jnp.exp(gl) - _mmb(c_mat, s) + n_mat


def _gdn_intra_specs(t, tm, dts, order=lambda i: i):
    nb = tm // CHUNK
    row = pl.BlockSpec((tm, DN_VW), lambda i: (order(i), 0))
    mat = pl.BlockSpec((DN_HEADS, nb, DN_DK, DN_DV), lambda i: (0, order(i), 0, 0))
    row_shape = lambda d: jax.ShapeDtypeStruct((t, DN_VW), d)
    mat_shape = lambda d: jax.ShapeDtypeStruct((DN_HEADS, t // CHUNK, DN_DK, DN_DV), d)
    gl = pl.BlockSpec((DN_HEADS, nb, GL_ROWS, LANE), lambda i: (0, order(i), 0, 0))
    gl_shape = jax.ShapeDtypeStruct((DN_HEADS, t // CHUNK, GL_ROWS, LANE), dts[4])
    return [row, row, mat, mat, gl], [row_shape(dts[0]), row_shape(dts[1]), mat_shape(dts[2]), mat_shape(dts[3]), gl_shape]


def _gdn_intra_fwd(qkv, proj, al8, dt8, tm):
    t = qkv.shape[0]

    def body(qkv_ref, ba_ref, al_ref, dt_ref, *outs):
        for o, val in zip(outs, _gdn_intra(qkv_ref[...], ba_ref[...], al_ref[...], dt_ref[...])):
            o[...] = val.astype(o.dtype)

    specs, shapes = _gdn_intra_specs(t, tm, (BF, F32, BF, BF, F32))
    specs.append(_gdn_inverse_spec(tm))
    shapes.append(jax.ShapeDtypeStruct((DN_HEADS, t // CHUNK, CHUNK, CHUNK), F32))
    res = pl.pallas_call(
        body, grid=(t // tm,),
        in_specs=[pl.BlockSpec((tm, DN_CONV_CH), lambda i: (i, 0)), pl.BlockSpec((tm, LANE), lambda i: (i, 0)),
                  _full(al8.shape), _full(dt8.shape)],
        out_specs=specs, out_shape=shapes, name="gdn_intra_fwd",
        compiler_params=pltpu.CompilerParams(dimension_semantics=("parallel",)),
    )(qkv, proj, al8, dt8)
    return res[:5], res[5]


def _gdn_inverse_spec(tm):
    return pl.BlockSpec((DN_HEADS, tm // CHUNK, CHUNK, CHUNK), lambda i: (0, i, 0, 0))


def _gdn_intra_bwd(qkv, proj, al8, dt8, inverses, cts, tm):
    t = qkv.shape[0]

    def body(qkv_ref, ba_ref, al_ref, dt_ref, inv_ref, *refs):
        ct_refs, (dqkv_ref, dba_ref, dal_ref, ddt_ref) = refs[:5], refs[5:]

        @pl.when(pl.program_id(0) == 0)
        def _():
            dal_ref[...] = jnp.zeros_like(dal_ref)
            ddt_ref[...] = jnp.zeros_like(ddt_ref)

        _, vjp = jax.vjp(functools.partial(_gdn_intra, inv4=inv_ref[...]), qkv_ref[...], ba_ref[...], al_ref[...], dt_ref[...])
        dqkv, dba, dal, ddt = vjp(tuple(r[...].astype(F32) for r in ct_refs))
        dqkv_ref[...] = dqkv.astype(dqkv_ref.dtype)
        dba_ref[...] = dba.astype(dba_ref.dtype)
        dal_ref[...] += dal
        ddt_ref[...] += ddt

    specs, _ = _gdn_intra_specs(t, tm, (F32,) * 5)
    return pl.pallas_call(
        body, grid=(t // tm,),
        in_specs=[pl.BlockSpec((tm, DN_CONV_CH), lambda i: (i, 0)), pl.BlockSpec((tm, LANE), lambda i: (i, 0)),
                  _full(al8.shape), _full(dt8.shape), _gdn_inverse_spec(tm)] + specs,
        out_specs=[pl.BlockSpec((tm, DN_CONV_CH), lambda i: (i, 0)), pl.BlockSpec((tm, LANE), lambda i: (i, 0)),
                   _full(al8.shape), _full(dt8.shape)],
        out_shape=[jax.ShapeDtypeStruct((t, DN_CONV_CH), BF), jax.ShapeDtypeStruct((t, LANE), BF),
                   jax.ShapeDtypeStruct(al8.shape, F32), jax.ShapeDtypeStruct(dt8.shape, F32)],
        name="gdn_intra_bwd", compiler_params=pltpu.CompilerParams(dimension_semantics=("arbitrary",)),
    )(qkv, proj, al8, dt8, inverses, *cts)


def _gdn_scan_fwd(intra, tm):
    t = intra[0].shape[0]
    nb = tm // CHUNK
    nc = t // CHUNK

    def body(qp_ref, op_ref, c_ref, n_ref, gl_ref, o_ref, ss_ref, s_scr):
        @pl.when(pl.program_id(0) == 0)
        def _():
            s_scr[...] = jnp.zeros_like(s_scr)

        state = [s_scr[h] for h in range(DN_HEADS)]
        for cc in range(nb):
            rows = pl.ds(cc * CHUNK, CHUNK)
            for h in range(DN_HEADS):
                cols = pl.ds(h * DN_DV, DN_DV)
                ss_ref[cc, h] = state[h].astype(ss_ref.dtype)
                o_ref[rows, cols], state[h] = _gdn_scan_step(
                    qp_ref[rows, cols], op_ref[rows, cols], c_ref[h, cc], n_ref[h, cc], gl_ref[h, cc, pl.ds(0, 1), :], state[h])
        for h in range(DN_HEADS):
            s_scr[h] = state[h]

    specs, _ = _gdn_intra_specs(t, tm, (F32,) * 5)
    return pl.pallas_call(
        body, grid=(t // tm,), in_specs=specs,
        out_specs=[pl.BlockSpec((tm, DN_VW), lambda i: (i, 0)),
                   pl.BlockSpec((nb, DN_HEADS, DN_DK, DN_DV), lambda i: (i, 0, 0, 0))],
        out_shape=[jax.ShapeDtypeStruct((t, DN_VW), F32), jax.ShapeDtypeStruct((nc, DN_HEADS, DN_DK, DN_DV), BF)],
        scratch_shapes=[pltpu.VMEM((DN_HEADS, DN_DK, DN_DV), F32)], name="gdn_scan_fwd",
        compiler_params=pltpu.CompilerParams(dimension_semantics=("arbitrary",)),
    )(*intra)


def _gdn_scan_bwd(intra, states, do, tm):
    t = intra[0].shape[0]
    nb = tm // CHUNK
    ng = t // tm

    def body(qp_ref, op_ref, c_ref, n_ref, gl_ref, ss_ref, do_ref, dqp_ref, dop_ref, dc_ref, dn_ref, dgl_ref, ds_scr):
        @pl.when(pl.program_id(0) == 0)
        def _():
            ds_scr[...] = jnp.zeros_like(ds_scr)

        d_state = [ds_scr[h] for h in range(DN_HEADS)]
        for cc in reversed(range(nb)):
            rows = pl.ds(cc * CHUNK, CHUNK)
            for h in range(DN_HEADS):
                cols = pl.ds(h * DN_DV, DN_DV)
                _, vjp = jax.vjp(_gdn_scan_step, qp_ref[rows, cols].astype(F32), op_ref[rows, cols], c_ref[h, cc].astype(F32),
                                 n_ref[h, cc].astype(F32), gl_ref[h, cc, pl.ds(0, 1), :], ss_ref[cc, h].astype(F32))
                dqp_ref[rows, cols], dop_ref[rows, cols], dc, dn, dgl, d_state[h] = vjp((do_ref[rows, cols], d_state[h]))
                dc_ref[h, cc] = dc.astype(dc_ref.dtype)
                dn_ref[h, cc] = dn.astype(dn_ref.dtype)
                first_row = lax.broadcasted_iota(jnp.int32, (GL_ROWS, LANE), 0) == 0
                dgl_ref[h, cc] = jnp.where(first_row, dgl, 0.0)
        for h in range(DN_HEADS):
            ds_scr[h] = d_state[h]

    five, shapes = _gdn_intra_specs(t, tm, (F32, F32, BF, BF, F32), order=lambda i: ng - 1 - i)
    row = five[0]
    return pl.pallas_call(
        body, grid=(ng,),
        in_specs=five + [pl.BlockSpec((nb, DN_HEADS, DN_DK, DN_DV), lambda i: (ng - 1 - i, 0, 0, 0)), row],
        out_specs=five, out_shape=shapes,
        scratch_shapes=[pltpu.VMEM((DN_HEADS, DN_DK, DN_DV), F32)], name="gdn_scan_bwd",
        compiler_params=pltpu.CompilerParams(dimension_semantics=("arbitrary",)),
    )(*intra, states, do)


def _gdn_out(o, z, g):
    parts = []
    for h in range(DN_HEADS):
        sl = slice(h * DN_DV, (h + 1) * DN_DV)
        parts.append(_rmsnorm(o[:, sl], g) * _silu(z[:, sl]))
    return parts


_Q_SCALE = math.log2(math.e) / math.sqrt(QK_NOPE + QK_ROPE)


def _rope_tables(pos, inv_freq2):
    lane = lax.broadcasted_iota(jnp.int32, (1, LANE), 1)
    ang = pos * inv_freq2
    cos = jnp.where(lane < QK_ROPE, jnp.cos(ang), 0.0)
    sin = jnp.where(lane < QK_ROPE // 2, -jnp.sin(ang), jnp.where(lane < QK_ROPE, jnp.sin(ang), 0.0))
    return cos, sin


@jax.custom_vjp
def _rope_swap(u):
    lane = lax.broadcasted_iota(jnp.int32, u.shape, 1)
    half = QK_ROPE // 2
    return jnp.where(lane < half, pltpu.roll(u, LANE - half, 1), jnp.where(lane < QK_ROPE, pltpu.roll(u, half, 1), 0.0))


_rope_swap.defvjp(lambda u: (_rope_swap(u), None), lambda _, g: (_rope_swap(g),))


def _mla_prep(cq, ckv, kr, gq, gkv, w_uq, w_ukv, cos, sin):
    rope = lambda u: u * cos + _rope_swap(u) * sin
    q_lin = _mmb_nt(_rmsnorm(cq, gq), w_uq) * _Q_SCALE
    kv_lin = _mmb_nt(_rmsnorm(ckv, gkv), w_ukv)
    k_rope = rope(kr)
    qs, ks, vs = [], [], []
    for h in range(MLA_HEADS):
        qs += [q_lin[:, h * LANE:(h + 1) * LANE], rope(q_lin[:, (MLA_HEADS + h) * LANE:(MLA_HEADS + h + 1) * LANE])]
        ks += [kv_lin[:, 2 * h * LANE:(2 * h + 1) * LANE], k_rope]
        vs += [kv_lin[:, (2 * h + 1) * LANE:(2 * h + 2) * LANE]]
    return qs + ks + vs


def _mla_prep_fwd(proj, pos_col, inv_freq2, gq, gkv, w_uq, w_ukv, tm):
    t = proj.shape[0]
    nq = 2 * MLA_HEADS

    def body(cq_ref, ckv_ref, kr_ref, pos_ref, f_ref, gq_ref, gkv_ref, wq_ref, wkv_ref, q_ref, k_ref, v_ref):
        cos, sin = _rope_tables(pos_ref[...], f_ref[...])
        outs = _mla_prep(cq_ref[...].astype(F32), ckv_ref[...].astype(F32), kr_ref[...].astype(F32), gq_ref[...], gkv_ref[...], wq_ref[...], wkv_ref[...],
                         cos, sin)
        for i in range(nq):
            q_ref[:, pl.ds(i * LANE, LANE)] = outs[i].astype(q_ref.dtype)
            k_ref[:, pl.ds(i * LANE, LANE)] = outs[nq + i].astype(k_ref.dtype)
        for h in range(MLA_HEADS):
            v_ref[:, pl.ds(h * LANE, LANE)] = outs[2 * nq + h].astype(v_ref.dtype)

    row = lambda w, j: pl.BlockSpec((tm, w), functools.partial(lambda i, j: (i, j), j=j))
    return pl.pallas_call(
        body, grid=(t // tm,),
        in_specs=[row(Q_LORA, P_CQ // Q_LORA), row(KV_LORA, P_CKV // KV_LORA), row(LANE, P_KR // LANE),
                  pl.BlockSpec((tm, 1), lambda i: (i, 0)), _full(inv_freq2.shape), _full(gq.shape), _full(gkv.shape),
                  _full(w_uq.shape), _full(w_ukv.shape)],
        out_specs=[row(nq * LANE, 0), row(nq * LANE, 0), row(MLA_VW, 0)],
        out_shape=[jax.ShapeDtypeStruct((t, nq * LANE), BF), jax.ShapeDtypeStruct((t, nq * LANE), BF),
                   jax.ShapeDtypeStruct((t, MLA_VW), BF)],
        name="mla_prep_fwd", compiler_params=pltpu.CompilerParams(dimension_semantics=("arbitrary",)),
    )(proj, proj, proj, pos_col, inv_freq2, gq, gkv, w_uq, w_ukv)


def _mla_prep_bwd(proj, pos_col, inv_freq2, gq, gkv, w_uq, w_ukv, dq, dk, dv, tm):
    t = proj.shape[0]
    nq = 2 * MLA_HEADS

    def body(cq_ref, ckv_ref, kr_ref, pos_ref, f_ref, gq_ref, gkv_ref, wq_ref, wkv_ref, dq_ref, dk_ref, dv_ref,
             dcq_ref, dckv_ref, dkr_ref, dgq_ref, dgkv_ref, dwq_ref, dwkv_ref):
        @pl.when(pl.program_id(0) == 0)
        def _():
            for o in (dgq_ref, dgkv_ref, dwq_ref, dwkv_ref):
                o[...] = jnp.zeros_like(o)

        cos, sin = _rope_tables(pos_ref[...], f_ref[...])
        f = functools.partial(_mla_prep, cos=cos, sin=sin)
        _, vjp = jax.vjp(f, cq_ref[...].astype(F32), ckv_ref[...].astype(F32), kr_ref[...].astype(F32), gq_ref[...], gkv_ref[...], wq_ref[...], wkv_ref[...])
        cts = [dq_ref[:, pl.ds(i * LANE, LANE)] for i in range(nq)]
        cts += [dk_ref[:, pl.ds(i * LANE, LANE)] for i in range(nq)]
        cts += [dv_ref[:, pl.ds(h * LANE, LANE)] for h in range(MLA_HEADS)]
        dcq, dckv, dkr, dgq, dgkv, dwq, dwkv = vjp(cts)
        dcq_ref[...] = dcq.astype(dcq_ref.dtype)
        dckv_ref[...] = dckv.astype(dckv_ref.dtype)
        dkr_ref[...] = dkr.astype(dkr_ref.dtype)
        dgq_ref[...] += dgq
        dgkv_ref[...] += dgkv
        dwq_ref[...] += dwq
        dwkv_ref[...] += dwkv

    row = lambda w, j: pl.BlockSpec((tm, w), functools.partial(lambda i, j: (i, j), j=j))
    return pl.pallas_call(
        body, grid=(t // tm,),
        in_specs=[row(Q_LORA, P_CQ // Q_LORA), row(KV_LORA, P_CKV // KV_LORA), row(LANE, P_KR // LANE),
                  pl.BlockSpec((tm, 1), lambda i: (i, 0)), _full(inv_freq2.shape), _full(gq.shape), _full(gkv.shape),
                  _full(w_uq.shape), _full(w_ukv.shape), row(nq * LANE, 0), row(nq * LANE, 0), row(MLA_VW, 0)],
        out_specs=[row(Q_LORA, 0), row(KV_LORA, 0), row(LANE, 0), _full(gq.shape), _full(gkv.shape),
                   _full(w_uq.shape), _full(w_ukv.shape)],
        out_shape=[jax.ShapeDtypeStruct((t, Q_LORA), BF), jax.ShapeDtypeStruct((t, KV_LORA), BF),
                   jax.ShapeDtypeStruct((t, LANE), BF), jax.ShapeDtypeStruct(gq.shape, F32),
                   jax.ShapeDtypeStruct(gkv.shape, F32), jax.ShapeDtypeStruct(w_uq.shape, F32),
                   jax.ShapeDtypeStruct(w_ukv.shape, F32)],
        name="mla_prep_bwd", compiler_params=pltpu.CompilerParams(dimension_semantics=("arbitrary",)),
    )(proj, proj, proj, pos_col, inv_freq2, gq, gkv, w_uq, w_ukv, dq, dk, dv)


_NEG = -1e30
_LN2 = math.log(2.0)
ATT_CHAINS = 2


def _causal(tq, tk, q0, k0):
    row = q0 + lax.broadcasted_iota(jnp.int32, (tq, tk), 0)
    col = k0 + lax.broadcasted_iota(jnp.int32, (tq, tk), 1)
    return col <= row


def _attn_fwd(q, k, v, tq, tk):
    t = q.shape[0]

    assert tk % tq == 0 or tq % tk == 0
    n_diag = max(1, tq // tk)

    th = tq // ATT_CHAINS

    def body(q_ref, k_ref, v_ref, o_ref, lse_ref):
        i = pl.program_id(1)
        n_full = (i * tq) // tk

        def step(k0, carry, masked):
            out = []
            for c, (m, l, acc) in enumerate(carry):
                kw = min(tk, (c + 1) * th) if masked and tk == tq else tk
                kt = k_ref[pl.ds(k0, kw), :]
                vt = v_ref[pl.ds(k0, kw), :]
                s = _dot(q_ref[pl.ds(c * th, th), :], kt, NT)
                if masked:
                    s = jnp.where(_causal(th, kw, i * tq + c * th, k0), s, _NEG)
                m_new = jnp.maximum(m, jnp.max(s, axis=-1, keepdims=True))
                p = jnp.exp2(s - m_new)
                alpha = jnp.exp2(m - m_new)
                out.append((m_new, alpha * l + jnp.sum(p, axis=-1, keepdims=True), alpha * acc + _dot(p.astype(BF), vt)))
            return tuple(out)

        init = tuple((jnp.full((th, 1), _NEG, F32), jnp.zeros((th, 1), F32), jnp.zeros((th, V_HEAD), F32)) for _ in range(ATT_CHAINS))
        carry = lax.fori_loop(0, n_full, lambda j, c: step(pl.multiple_of(j * tk, tk), c, False), init)
        for dd in range(n_diag):
            carry = step(pl.multiple_of((n_full + dd) * tk, tk), carry, True)
        for c, (m, l, acc) in enumerate(carry):
            o_ref[pl.ds(c * th, th), :] = acc / l
            lse_ref[pl.ds(c * th, th), :] = jnp.broadcast_to(m + jnp.log2(l), (th, LANE))

    return pl.pallas_call(
        body, grid=(MLA_HEADS, t // tq),
        in_specs=[pl.BlockSpec((tq, 2 * LANE), lambda h, i: (i, h)), pl.BlockSpec((t, 2 * LANE), lambda h, i: (0, h)),
                  pl.BlockSpec((t, V_HEAD), lambda h, i: (0, h))],
        out_specs=[pl.BlockSpec((tq, V_HEAD), lambda h, i: (i, h)), pl.BlockSpec((tq, LANE), lambda h, i: (i, h))],
        out_shape=[jax.ShapeDtypeStruct((t, MLA_VW), F32), jax.ShapeDtypeStruct((t, MLA_HEADS * LANE), F32)],
        name="attn_fwd", compiler_params=pltpu.CompilerParams(dimension_semantics=("parallel", "arbitrary")),
    )(q, k, v)


def _attn_bwd(q, k, v, do, lse, delta, tq, tk):
    t = q.shape[0]
    nkt = t // tk
    assert tk % tq == 0

    def body(q_ref, k_ref, v_ref, do_ref, lse_ref, dl_ref, dq_ref, dk_ref, dv_ref):
        j = pl.program_id(1)

        @pl.when(j == 0)
        def _():
            dq_ref[...] = jnp.zeros_like(dq_ref)

        kt = k_ref[...]
        vt = v_ref[...]

        def step(q0, carry, masked, kw=tk):
            dk, dv = carry
            rows = pl.ds(q0, tq)
            qt = q_ref[rows, :]
            dot_ = do_ref[rows, :]
            ktw, vtw = kt[:kw], vt[:kw]
            p = jnp.exp2(_dot(qt, ktw, NT) - lse_ref[rows, pl.ds(0, 1)])
            if masked:
                p = jnp.where(_causal(tq, kw, q0, j * tk), p, 0.0)
            dv_w = _dot(p.astype(BF), dot_, TN)
            ds = (p * (_dot(dot_, vtw, NT) - dl_ref[rows, pl.ds(0, 1)])).astype(BF)
            dk_w = _dot(ds, qt, TN)
            dq_ref[rows, :] += _dot(ds, ktw)
            if kw == tk:
                return dk + dk_w, dv + dv_w
            return (jnp.concatenate([dk[:kw] + dk_w, dk[kw:]], axis=0), jnp.concatenate([dv[:kw] + dv_w, dv[kw:]], axis=0))

        per = tk // tq
        carry = (jnp.zeros((tk, 2 * LANE), F32), jnp.zeros((tk, V_HEAD), F32))
        for dd in range(per):
            carry = step(pl.multiple_of(j * tk + dd * tq, tq), carry, True, kw=(dd + 1) * tq)

        def group(g, c):
            for dd in range(per):
                c = step(pl.multiple_of(g * tk + dd * tq, tq), c, False)
            return c

        dk, dv = lax.fori_loop(j + 1, nkt, group, carry)
        dk_ref[...] = dk * _LN2
        dv_ref[...] = dv

        @pl.when(j == nkt - 1)
        def _():
            dq_ref[...] = dq_ref[...] * _LN2

    return pl.pallas_call(
        body, grid=(MLA_HEADS, nkt),
        in_specs=[pl.BlockSpec((t, 2 * LANE), lambda h, j: (0, h)), pl.BlockSpec((tk, 2 * LANE), lambda h, j: (j, h)),
                  pl.BlockSpec((tk, V_HEAD), lambda h, j: (j, h)), pl.BlockSpec((t, V_HEAD), lambda h, j: (0, h)),
                  pl.BlockSpec((t, LANE), lambda h, j: (0, h)), pl.BlockSpec((t, LANE), lambda h, j: (0, h))],
        out_specs=[pl.BlockSpec((t, 2 * LANE), lambda h, j: (0, h)), pl.BlockSpec((tk, 2 * LANE), lambda h, j: (j, h)),
                   pl.BlockSpec((tk, V_HEAD), lambda h, j: (j, h))],
        out_shape=[jax.ShapeDtypeStruct((t, MLA_HEADS * 2 * LANE), F32), jax.ShapeDtypeStruct((t, MLA_HEADS * 2 * LANE), F32),
                   jax.ShapeDtypeStruct((t, MLA_VW), F32)],
        name="attn_bwd", compiler_params=pltpu.CompilerParams(dimension_semantics=("parallel", "arbitrary")),
    )(q, k, v, do, lse, delta)


def _adam_update(w, g, m, v):
    mm = ADAM_B1 * m + (1.0 - ADAM_B1) * g
    vv = ADAM_B2 * v + (1.0 - ADAM_B2) * jnp.square(g)
    m_hat = mm / (1.0 - ADAM_B1 ** ADAM_STEP)
    v_hat = vv / (1.0 - ADAM_B2 ** ADAM_STEP)
    return -ADAM_LR * (m_hat / (jnp.sqrt(v_hat) + ADAM_EPS) + ADAM_WD * w), mm, vv


def _adamw(w, g, m, v, name):
    r, c = w.shape
    tr = max([r // s for s in range(1, r // 8 + 1) if r % s == 0 and (r // s) % 8 == 0 and r // s <= 256] or [r])
    slots = g.ndim == 3

    def body(w_ref, g_ref, m_ref, v_ref, g_out, d_ref, nm_ref, nv_ref):
        if slots:
            gg = g_ref[0].astype(F32)
            for s in range(1, N_DEV):
                gg = gg + g_ref[s].astype(F32)
        else:
            gg = g_ref[...]
        g_out[...] = gg
        d_ref[...], nm_ref[...], nv_ref[...] = _adam_update(w_ref[...], gg, m_ref[...], v_ref[...])

    spec = pl.BlockSpec((tr, c), lambda i: (i, 0))
    g_spec = pl.BlockSpec((N_DEV, tr, c), lambda i: (0, i, 0)) if slots else spec
    return pl.pallas_call(
        body, grid=(r // tr,), in_specs=[spec, g_spec, spec, spec], out_specs=[spec] * 4,
        out_shape=[jax.ShapeDtypeStruct((r, c), F32)] * 4, name=name,
        compiler_params=pltpu.CompilerParams(dimension_semantics=("arbitrary",)),
    )(w, g, m, v)


def _adamw_many(ws, gs, ms, vs, name):
    n = len(ws)

    def body(*refs):
        for i in range(n):
            w_ref, g_ref, m_ref, v_ref = (refs[j * n + i] for j in range(4))
            d_ref, nm_ref, nv_ref = (refs[(4 + j) * n + i] for j in range(3))
            d_ref[...], nm_ref[...], nv_ref[...] = _adam_update(w_ref[...], g_ref[...], m_ref[...], v_ref[...])

    shapes = [jax.ShapeDtypeStruct(w.shape, F32) for w in ws]
    outs = pl.pallas_call(body, out_shape=shapes * 3, name=name)(*ws, *gs, *ms, *vs)
    return outs[:n], outs[n:2 * n], outs[2 * n:]


def _cast_bf16(xs, name, after=None):
    n = len(xs)
    extra = [] if after is None else [after]

    def body(*refs):
        outs = refs[n + len(extra):]
        for i in range(n):
            outs[i][...] = refs[i][...].astype(BF)

    vmem = pl.BlockSpec(memory_space=pltpu.VMEM)
    return pl.pallas_call(
        body, out_shape=[jax.ShapeDtypeStruct(x.shape, BF) for x in xs], name=name,
        in_specs=[vmem] * n + [pl.BlockSpec(memory_space=pl.ANY)] * len(extra), out_specs=[vmem] * n)(*xs, *extra)


def _pad_rows(a, n):
    return jnp.pad(a, ((0, n - a.shape[0]), (0, 0)))


def _w_in_to_padded(wt):
    s_ba = P_CQ
    s_cq = s_ba + 2 * DN_HEADS
    s_kr = s_cq + Q_LORA + KV_LORA
    return jnp.concatenate([wt[:s_ba], wt[s_cq:s_kr], _pad_rows(wt[s_ba:s_cq], LANE), _pad_rows(wt[s_kr:], LANE)], axis=0)


def _w_in_from_padded(wt):
    return jnp.concatenate([wt[:P_CQ], wt[P_BA:P_BA + 2 * DN_HEADS], wt[P_CQ:P_BA], wt[P_KR:P_KR + QK_ROPE]], axis=0)


def _w_uq_to_padded(wt):
    w3 = wt.reshape(MLA_HEADS, QK_NOPE + QK_ROPE, Q_LORA)
    nope = w3[:, :QK_NOPE].reshape(MLA_HEADS * QK_NOPE, Q_LORA)
    rope = jnp.pad(w3[:, QK_NOPE:], ((0, 0), (0, LANE - QK_ROPE), (0, 0))).reshape(MLA_HEADS * LANE, Q_LORA)
    return jnp.concatenate([nope, rope], axis=0)


def _w_uq_from_padded(wt):
    nope = wt[:MLA_HEADS * QK_NOPE].reshape(MLA_HEADS, QK_NOPE, Q_LORA)
    rope = wt[MLA_HEADS * QK_NOPE:].reshape(MLA_HEADS, LANE, Q_LORA)[:, :QK_ROPE]
    return jnp.concatenate([nope, rope], axis=1).reshape(MLA_HEADS * (QK_NOPE + QK_ROPE), Q_LORA)


def _pack(pieces, width, row_mult):
    flat = jnp.concatenate([p.reshape(-1) for p in pieces])
    n = flat.shape[0]
    rows = -(-n // (width * row_mult)) * row_mult
    return jnp.pad(flat, (0, rows * width - n)).reshape(rows, width)


def _unpack(flat, shapes):
    out, o = [], 0
    for s in shapes:
        n = math.prod(s)
        out.append(flat[o:o + n].reshape(s))
        o += n
    return out


def kernel(x, c, positions, w_ada, b_ada, w_in, conv_w, a_log, dt_bias, dn_norm_g, q_norm_g, w_uq, kv_norm_g, w_ukv, w_o, ln1_g, ln1_b, w_gate, w_up, w_down, ln2_g, ln2_b, loss_target, m_w_ada, m_b_ada, m_w_in, m_conv_w, m_a_log, m_dt_bias, m_dn_norm_g, m_q_norm_g, m_w_uq, m_kv_norm_g, m_w_ukv, m_w_o, m_ln1_g, m_ln1_b, m_w_gate, m_w_up, m_w_down, m_ln2_g, m_ln2_b, v_w_ada, v_b_ada, v_w_in, v_conv_w, v_a_log, v_dt_bias, v_dn_norm_g, v_q_norm_g, v_w_uq, v_kv_norm_g, v_w_ukv, v_w_o, v_ln1_g, v_ln1_b, v_w_gate, v_w_up, v_w_down, v_ln2_g, v_ln2_b):
    me = 4 * lax.axis_index("x") + 2 * lax.axis_index("y") + lax.axis_index("c")
    t, d = x.shape[1], x.shape[2]
    ada_n = w_ada.shape[2]

    tr = lambda w: w[0].T
    rows = lambda a: a.reshape(-1, a.shape[2])
    (in_shard,) = _cast_bf16([tr(w_in)], "cast_w_in")
    cw = conv_w.shape[3]
    a_in, c_all, conv_all = _gather_by_chip([in_shard, c, conv_w[0, :, 0, :]], "gather_w_in_and_small")
    c_all = c_all.reshape(N_DEV, d)
    conv_full = conv_all.transpose(1, 0, 2).reshape(CONV_K, N_DEV * cw)
    conv_w8 = jnp.pad(conv_full, ((0, 8 - CONV_K), (0, 0)))

    b_ada_mine = lax.dynamic_slice(b_ada, (0, me * ada_n), (1, ada_n))
    mod_cols = _mod_fwd(c_all, w_ada[0], b_ada_mine)
    (mod_all,) = _exchange([mod_cols.reshape(N_DEV, 1, ada_n)], "scatter_mod", scatter=True)
    mod = mod_all.reshape(1, N_DEV * ada_n)

    later = _cast_bf16([tr(w_uq), tr(w_ukv), w_o[0], tr(w_gate), tr(w_up), w_down[0]], "cast_weights", after=mod)
    mixer_gather, token_a = _exchange_start(later[:3], "gather_mixer_weights_start", scatter=False)
    ffn_gather, token_b = _exchange_start(later[3:], "gather_ffn_weights_start", scatter=False)
    mod = mod + (token_a + token_b)
    w_in_t = _w_in_to_padded(rows(a_in))

    def mixer_weights(after):
        a_uq, a_ukv, a_o = _exchange_wait(mixer_gather, after, "gather_mixer_weights_wait", scatter=False)
        return _w_uq_to_padded(rows(a_uq)), rows(a_ukv), rows(a_o)

    def ffn_weights(after):
        a_gate, a_up, a_down = _exchange_wait(ffn_gather, after, "gather_ffn_weights_wait", scatter=False)
        return rows(a_gate), rows(a_up), rows(a_down)

    def by_dest(g):
        return g.reshape(N_DEV, -1, g.shape[1])

    scatters = {}

    def grads_ready(tag, *g):
        if tag == "ffn":
            pieces = [by_dest(a) for a in g]
        elif tag == "mixer":
            g_w_o, g_w_uq_t, g_w_ukv_t = g
            pieces = [by_dest(g_w_o), by_dest(_w_uq_from_padded(g_w_uq_t).astype(BF)), by_dest(g_w_ukv_t.astype(BF))]
        else:
            pieces = [by_dest(_w_in_from_padded(g[0]))]
        scatters[tag], token = _exchange_start(pieces, "scatter_%s_grads_start" % tag, scatter=True)
        return token

    loc = _local_step(x[0], loss_target[0], positions[0], mod, w_in_t, mixer_weights, ffn_weights, grads_ready,
                      conv_w8, a_log, dt_bias, dn_norm_g, q_norm_g, kv_norm_g, ln1_g, ln1_b, ln2_g, ln2_b)
    grad_x, loss_acc, dmod, d_conv8, d_al8, d_dt8, d_dn_g, d_q_g, d_kv_g, d_ln1_g, d_ln1_b, d_ln2_g, d_ln2_b = loc

    small_shapes = [(6 * d,), (CONV_K, N_DEV * cw), (DN_HEADS,), (DN_HEADS,), (DN_DV,), (Q_LORA,), (KV_LORA,), (d,), (d,), (d,), (d,), (1,)]
    gsmall = _pack([dmod, d_conv8[:CONV_K], d_al8[0, :DN_HEADS], d_dt8[0, :DN_HEADS], d_dn_g, d_q_g, d_kv_g,
                    d_ln1_g, d_ln1_b, d_ln2_g, d_ln2_b, loss_acc[0, :1]], LANE, 8)
    (gsmall_all,) = _exchange([gsmall], "gather_small_grads", scatter=False)
    dmod_all = gsmall_all.reshape(N_DEV, -1)[:, :6 * d]
    tot = _unpack(_sum_slots(gsmall_all, "sum_small_grads").reshape(-1), small_shapes)
    g_b_ada, g_conv_full, g_a_log, g_dt_bias, g_dn_g, g_q_g, g_kv_g, g_ln1_g, g_ln1_b, g_ln2_g, g_ln2_b, loss1 = tot
    loss = loss1.reshape(())
    g_conv_w = lax.dynamic_slice(g_conv_full, (0, me * cw), (CONV_K, cw))
    g_w_ada = _mod_bwd(c_all.T, lax.dynamic_slice(dmod_all, (0, me * ada_n), (N_DEV, ada_n)))

    grads = {"w_ada": g_w_ada[None], "b_ada": g_b_ada[None], "conv_w": g_conv_w[None, :, None, :],
             "a_log": g_a_log[None], "dt_bias": g_dt_bias[None], "dn_norm_g": g_dn_g[None], "q_norm_g": g_q_g[None],
             "kv_norm_g": g_kv_g[None], "ln1_g": g_ln1_g[None], "ln1_b": g_ln1_b[None], "ln2_g": g_ln2_g[None], "ln2_b": g_ln2_b[None]}
    weights = dict(w_ada=w_ada, b_ada=b_ada, w_in=w_in, conv_w=conv_w, a_log=a_log, dt_bias=dt_bias, dn_norm_g=dn_norm_g,
                   q_norm_g=q_norm_g, w_uq=w_uq, kv_norm_g=kv_norm_g, w_ukv=w_ukv, w_o=w_o, ln1_g=ln1_g, ln1_b=ln1_b,
                   w_gate=w_gate, w_up=w_up, w_down=w_down, ln2_g=ln2_g, ln2_b=ln2_b)
    ms = dict(w_ada=m_w_ada, b_ada=m_b_ada, w_in=m_w_in, conv_w=m_conv_w, a_log=m_a_log, dt_bias=m_dt_bias,
              dn_norm_g=m_dn_norm_g, q_norm_g=m_q_norm_g, w_uq=m_w_uq, kv_norm_g=m_kv_norm_g, w_ukv=m_w_ukv, w_o=m_w_o,
              ln1_g=m_ln1_g, ln1_b=m_ln1_b, w_gate=m_w_gate, w_up=m_w_up, w_down=m_w_down, ln2_g=m_ln2_g, ln2_b=m_ln2_b)
    vs = dict(w_ada=v_w_ada, b_ada=v_b_ada, w_in=v_w_in, conv_w=v_conv_w, a_log=v_a_log, dt_bias=v_dt_bias,
              dn_norm_g=v_dn_norm_g, q_norm_g=v_q_norm_g, w_uq=v_w_uq, kv_norm_g=v_kv_norm_g, w_ukv=v_w_ukv, w_o=v_w_o,
              ln1_g=v_ln1_g, ln1_b=v_ln1_b, w_gate=v_w_gate, w_up=v_w_up, w_down=v_w_down, ln2_g=v_ln2_g, ln2_b=v_ln2_b)
    names = list(weights)
    big = ("w_ada", "w_gate", "w_up", "w_down", "w_o", "w_uq", "w_ukv", "w_in")
    waits = {"w_gate": ("ffn", ("w_gate", "w_up", "w_down")), "w_o": ("mixer", ("w_o", "w_uq", "w_ukv")), "w_in": ("in", ("w_in",))}
    delta_w, new_m, new_v, slots = {}, {}, {}, {}
    last = g_w_ada
    for n in big:
        if n == "w_in":
            rest = [r for r in names if r not in big]
            flat2 = lambda a: a.reshape(-1, a.shape[-1])
            outs = _adamw_many(*[[flat2(src[r]) for r in rest] for src in (weights, grads, ms, vs)], "adamw_small")
            for dst, o in zip((delta_w, new_m, new_v), outs):
                for r, a in zip(rest, o):
                    dst[r] = a.reshape(weights[r].shape)
            last = outs[0][0]
        transposed = n in ("w_in", "w_uq", "w_ukv", "w_gate", "w_up")
        two = (lambda a: a[0].T) if transposed else (lambda a: a[0])
        back = (lambda a: a.T[None]) if transposed else (lambda a: a[None])
        if n in waits:
            tag, members = waits[n]
            slots.update(zip(members, _exchange_wait(scatters[tag], last, "scatter_%s_grads_wait" % tag, scatter=True)))
        g_in = slots[n] if n in slots else two(grads[n])
        gr, dlt, nm, nv = _adamw(two(weights[n]), g_in, two(ms[n]), two(vs[n]), "adamw_" + n)
        grads[n], delta_w[n], new_m[n], new_v[n] = back(gr), back(dlt), back(nm), back(nv)
        last = nv

    return (loss, grad_x[None], *[grads[n] for n in names], *[delta_w[n] for n in names],
            *[new_m[n] for n in names], *[new_v[n] for n in names])


def _local_step(xs, tgt, pos, mod, w_in_t, mixer_weights, ffn_weights, grads_ready, conv_w8,
                a_log, dt_bias, dn_norm_g, q_norm_g, kv_norm_g, ln1_g, ln1_b, ln2_g, ln2_b):
    t, d = xs.shape
    sh_m, sc_m, gt_m, sh_f, sc_f, gt_f = [mod[:, i * d:(i + 1) * d] for i in range(6)]
    pos_col = pos.astype(F32).reshape(t, 1)
    inv_freq = 1.0 / (ROPE_THETA ** (jnp.arange(0, QK_ROPE, 2, dtype=F32) / QK_ROPE))
    inv_freq2 = jnp.pad(jnp.concatenate([inv_freq, inv_freq]), (0, LANE - QK_ROPE)).reshape(1, LANE)
    al8 = jnp.pad(a_log, ((0, 7), (0, LANE - DN_HEADS)))
    dt8 = jnp.pad(dt_bias, ((0, 7), (0, LANE - DN_HEADS)))

    tm = min(512, t)
    tq = min(256, t)
    tk = min(512, t)

    (h1,) = _rowwise("modulate_in", lambda xx, sc, sh: xx * (1.0 + sc) + sh, [xs], [sc_m, sh_m], [(d, BF)], [], tm)
    proj = _matmul(h1, w_in_t, "nt", "in_proj", BF)
    ba_raw = _matmul(h1, w_in_t[P_BA:P_BA + LANE], "nt", "in_proj_decay")
    qkv = _conv_fwd(proj, conv_w8, min(256, t))
    gdn_tm = min(512, t)
    intra, inverses = _gdn_intra_fwd(qkv, ba_raw, al8, dt8, gdn_tm)
    o_dn, states = _gdn_scan_fwd(intra, gdn_tm)
    w_uq_t, w_ukv_t, w_o_f = mixer_weights(states)
    qc, kc, vc = _mla_prep_fwd(proj, pos_col, inv_freq2, q_norm_g, kv_norm_g, w_uq_t, w_ukv_t, tm)
    o_mla, lse = _attn_fwd(qc, kc, vc, min(1024, t), min(1024, t))

    def mix_in(o, z, om, g):
        return jnp.concatenate(_gdn_out(o, z.astype(F32), g) + [om], axis=1)

    (mixin,) = _rowwise("mixer_out", mix_in, [o_dn, (proj, DN_VW, P_Z // DN_VW), o_mla], [dn_norm_g], [(2 * DN_VW, BF)], [], tm)
    mix = _matmul(mixin, w_o_f, "nn", "out_proj", BF)

    def block1(xx, mx, gt, g1, b1, sc, sh):
        x1 = _layernorm(DEEPNORM_ALPHA * xx + gt * mx, g1, b1)
        return x1, x1 * (1.0 + sc) + sh

    x1, h2 = _rowwise("norm1_modulate", block1, [xs, mix], [gt_m, ln1_g, ln1_b, sc_f, sh_f], [(d, F32), (d, BF)], [], tm)
    w_gate_f, w_up_f, w_down_f = ffn_weights(h2)
    act, act_dg, act_du = _ffn_in(h2, w_gate_f, w_up_f)
    ff = _matmul(act, w_down_f, "nn", "ffn_out", BF)

    def tail_loss(x1_, ff_, gt, g2, b2, tg):
        y = _layernorm(DEEPNORM_ALPHA * x1_ + gt * ff_, g2, b2)
        return 0.5 * jnp.sum(jnp.mean(jnp.square(y - tg), axis=-1))

    def tail(x1_, ff_, tg, gt, g2, b2):
        loss, (dx1, dff, dgt, dg2, db2) = jax.value_and_grad(tail_loss, argnums=(0, 1, 2, 3, 4))(x1_, ff_, gt, g2, b2, tg)
        return dx1, dff, jnp.full((1, LANE), loss, F32), dgt, dg2, db2

    dx1_a, dff, loss_acc, d_gt_f, d_ln2_g, d_ln2_b = _rowwise(
        "norm2_loss", tail, [x1, ff, tgt], [gt_f, ln2_g, ln2_b], [(d, BF), (d, BF)], [(1, LANE), (1, d), (1, d), (1, d)], tm)

    g_w_down = _matmul(act, dff, "tn", "d_w_down", BF)
    dgate, dup = _ffn_act_bwd(dff, w_down_f, act_dg, act_du)
    g_w_gate = _matmul(dgate, h2, "tn", "d_w_gate", BF)
    g_w_up = _matmul(dup, h2, "tn", "d_w_up", BF)
    token = grads_ready("ffn", g_w_gate, g_w_up, g_w_down)
    dh2 = _matmul2_nn(dgate, w_gate_f, dup, w_up_f, "d_ffn_in", BF)

    def block1_bwd(xx, mx, dx1_, dh2_, gt, g1, b1, sc, sh):
        _, vjp = jax.vjp(block1, xx, mx, gt, g1, b1, sc, sh)
        dxx, dmx, dgt, dg1, db1, dsc, dsh = vjp((dx1_.astype(F32), dh2_.astype(F32)))
        return dxx, dmx, dgt, dg1, db1, dsc, dsh

    dx_a, dmix, d_gt_m, d_ln1_g, d_ln1_b, d_sc_f, d_sh_f = _rowwise(
        "norm1_modulate_bwd", block1_bwd, [xs, mix, dx1_a, dh2], [gt_m + token, ln1_g, ln1_b, sc_f, sh_f],
        [(d, F32), (d, BF)], [(1, d)] * 5, min(256, t))

    dmixin = _matmul(dmix, w_o_f, "nt", "d_mixer_out", BF)
    g_w_o = _matmul(mixin, dmix, "tn", "d_w_o", BF)

    def mixer_bwd(o, z, om, dmi, g):
        _, vjp = jax.vjp(lambda o_, z_, g_: jnp.concatenate(_gdn_out(o_, z_, g_), axis=1), o, z.astype(F32), g)
        do_, dz_, dg_ = vjp(dmi[:, :DN_VW].astype(F32))
        dom = dmi[:, DN_VW:]
        delta = [jnp.broadcast_to(jnp.sum(dom[:, h * V_HEAD:(h + 1) * V_HEAD] * om[:, h * V_HEAD:(h + 1) * V_HEAD], axis=-1, keepdims=True), (o.shape[0], LANE))
                 for h in range(MLA_HEADS)]
        return do_, dz_, dom, jnp.concatenate(delta, axis=1), dg_

    do_dn, dz, do_mla, delta, d_dn_g = _rowwise(
        "mixer_out_bwd", mixer_bwd, [o_dn, (proj, DN_VW, P_Z // DN_VW), o_mla, dmixin], [dn_norm_g],
        [(DN_VW, F32), (DN_VW, BF), (MLA_VW, BF), (MLA_HEADS * LANE, F32)], [(1, DN_DV)], tm)

    dqc, dkc, dvc = _attn_bwd(qc, kc, vc, do_mla, lse, delta, min(512, t), min(1024, t))
    dcq, dckv, dkr, d_q_g, d_kv_g, g_w_uq_t, g_w_ukv_t = _mla_prep_bwd(
        proj, pos_col, inv_freq2, q_norm_g, kv_norm_g, w_uq_t, w_ukv_t, dqc, dkc, dvc, min(256, t))

    token = grads_ready("mixer", g_w_o, g_w_uq_t, g_w_ukv_t)

    d_intra = _gdn_scan_bwd(intra, states, do_dn, gdn_tm)
    dqkv_act, dba, d_al8, d_dt8 = _gdn_intra_bwd(qkv, ba_raw, al8 + token, dt8, inverses, d_intra, min(256, t))
    dqkv_pre, d_conv8 = _conv_bwd(proj, conv_w8, dqkv_act, min(256, t))

    dproj = jnp.concatenate([dqkv_pre, dz, dcq, dckv, dba, dkr], axis=1)
    dh1 = _matmul(dproj, w_in_t, "nn", "d_in_proj", BF)
    g_w_in_t = _matmul(dproj, h1, "tn", "d_w_in", BF)
    token = grads_ready("in", g_w_in_t)

    def modulate_bwd(xx, dh, dxa, sc):
        dh = dh.astype(F32)
        return dh * (1.0 + sc) + dxa, jnp.sum(dh * xx, axis=0, keepdims=True), jnp.sum(dh, axis=0, keepdims=True)

    grad_x, d_sc_m, d_sh_m = _rowwise("modulate_in_bwd", modulate_bwd, [xs, dh1, dx_a], [sc_m + token], [(d, F32)], [(1, d), (1, d)], tm)
    dmod = jnp.concatenate([d_sh_m, d_sc_m, d_gt_m, d_sh_f, d_sc_f, d_gt_f], axis=1)
    return grad_x, loss_acc, dmod, d_conv8, d_al8, d_dt8, d_dn_g, d_q_g, d_kv_g, d_ln1_g, d_ln1_b, d_ln2_g, d_ln2_b
```

```python
import functools
import math

import jax
import jax.numpy as jnp
from jax import lax
from jax.experimental import pallas as pl
from jax.experimental.pallas import tpu as pltpu

F32 = jnp.float32
BF = jnp.bfloat16
HI = lax.Precision.HIGHEST

N_DEV = 8
DN_HEADS = 4
DN_DK = 128
DN_DV = 128
CONV_K = 4
CHUNK = 64
MLA_HEADS = 4
QK_NOPE = 128
QK_ROPE = 64
V_HEAD = 128
Q_LORA = 512
KV_LORA = 256
ROPE_THETA = 10000.0
DEPTH = 1
DEEPNORM_ALPHA = (2.0 * DEPTH) ** 0.25
LANE = 128
CONV_HALO = 8
GL_ROWS = 8
CONV_ROWS, CONV_COLS = 64, 256

DN_QK = DN_HEADS * DN_DK
DN_VW = DN_HEADS * DN_DV
DN_CONV_CH = 2 * DN_QK + DN_VW
MLA_VW = MLA_HEADS * V_HEAD
P_Z = DN_CONV_CH
P_CQ = P_Z + DN_VW
P_CKV = P_CQ + Q_LORA
P_BA = P_CKV + KV_LORA
P_KR = P_BA + LANE
N_INP = P_KR + LANE

ADAM_LR = 0.001
ADAM_B1 = 0.9
ADAM_B2 = 0.999
ADAM_EPS = 1e-08
ADAM_WD = 0.01
ADAM_STEP = 10

NN = (((1,), (0,)), ((), ()))
NT = (((1,), (1,)), ((), ()))
TN = (((0,), (0,)), ((), ()))


def _pick(n, prefs):
    for p in prefs:
        if n % p == 0:
            return p
    return n


def _full(shape):
    return pl.BlockSpec(shape, lambda *_: (0,) * len(shape))


def _dot(a, b, dims=NN):
    return lax.dot_general(a, b, dims, preferred_element_type=F32)


def _doth(a, b, dims=NN):
    return lax.dot_general(a, b, dims, precision=HI, preferred_element_type=F32)


@jax.custom_vjp
def _mmb(a, b):
    return _dot(a.astype(BF), b.astype(BF), NN)


def _mmb_fwd(a, b):
    return _mmb(a, b), (a, b)


def _mmb_bwd(res, g):
    a, b = res
    gb = g.astype(BF)
    return (_dot(gb, b.astype(BF), NT).astype(a.dtype), _dot(a.astype(BF), gb, TN).astype(b.dtype))


_mmb.defvjp(_mmb_fwd, _mmb_bwd)


@jax.custom_vjp
def _mmb_nt(a, b):
    return _dot(a.astype(BF), b.astype(BF), NT)


def _mmb_nt_fwd(a, b):
    return _mmb_nt(a, b), (a, b)


def _mmb_nt_bwd(res, g):
    a, b = res
    gb = g.astype(BF)
    return (_dot(gb, b.astype(BF), NN).astype(a.dtype), _dot(gb, a.astype(BF), TN).astype(b.dtype))


_mmb_nt.defvjp(_mmb_nt_fwd, _mmb_nt_bwd)


def _sigmoid(x):
    return 0.5 * (jnp.tanh(0.5 * x) + 1.0)


def _silu(x):
    return x * _sigmoid(x)


def _softplus(x):
    return jnp.maximum(x, 0.0) + jnp.log(1.0 + jnp.exp(-jnp.abs(x)))


def _layernorm(x, g, b, eps=1e-5):
    mu = jnp.mean(x, axis=-1, keepdims=True)
    xc = x - mu
    var = jnp.mean(xc * xc, axis=-1, keepdims=True)
    return xc * lax.rsqrt(var + eps) * g + b


def _rmsnorm(x, g, eps=1e-6):
    return x * lax.rsqrt(jnp.mean(x * x, axis=-1, keepdims=True) + eps) * g


def _l2norm(x, eps=1e-6):
    return x * lax.rsqrt(jnp.sum(x * x, axis=-1, keepdims=True) + eps)


def _rowwise(name, fn, rows, vecs, out_rows, out_accs, tm):
    rows = [r if isinstance(r, tuple) else (r, r.shape[1], 0) for r in rows]
    t = rows[0][0].shape[0]
    tm = min(tm, t)
    assert t % tm == 0
    nr, nv, no = len(rows), len(vecs), len(out_rows)

    def body(*refs):
        ins = [r[...] for r in refs[:nr + nv]]
        outs = fn(*ins)
        outs = outs if isinstance(outs, (tuple, list)) else (outs,)
        o_rows = refs[nr + nv:nr + nv + no]
        o_accs = refs[nr + nv + no:]
        for o, val in zip(o_rows, outs[:no]):
            o[...] = val.astype(o.dtype)
        if o_accs:
            @pl.when(pl.program_id(0) == 0)
            def _():
                for o in o_accs:
                    o[...] = jnp.zeros_like(o)
            for o, val in zip(o_accs, outs[no:]):
                o[...] += val

    in_specs = [pl.BlockSpec((tm, w), functools.partial(lambda i, j: (i, j), j=j)) for (_, w, j) in rows]
    in_specs += [_full(v.shape) for v in vecs]
    out_specs = [pl.BlockSpec((tm, w), lambda i: (i, 0)) for (w, _) in out_rows]
    out_specs += [_full(s) for s in out_accs]
    out_shape = [jax.ShapeDtypeStruct((t, w), d) for (w, d) in out_rows]
    out_shape += [jax.ShapeDtypeStruct(s, F32) for s in out_accs]
    res = pl.pallas_call(
        body, grid=(t // tm,), in_specs=in_specs, out_specs=out_specs, out_shape=out_shape, name=name,
        compiler_params=pltpu.CompilerParams(dimension_semantics=("arbitrary",)),
    )(*[r[0] for r in rows], *vecs)
    return res


def _matmul(a, b, mode, name, out_dtype=F32):
    if mode == "nn":
        (m, k), n = a.shape, b.shape[1]
    elif mode == "nt":
        (m, k), n = a.shape, b.shape[0]
    else:
        (k, m), n = a.shape, b.shape[1]
    tm, tn, tk = _matmul_tiles(m, n, k, a.dtype.itemsize, b.dtype.itemsize, jnp.dtype(out_dtype).itemsize)
    nk = k // tk
    dims = {"nn": NN, "nt": NT, "tn": TN}[mode]

    def body(a_ref, b_ref, o_ref, *acc):
        part = _dot(a_ref[...].astype(BF), b_ref[...].astype(BF), dims)
        if nk == 1:
            o_ref[...] = part.astype(o_ref.dtype)
            return
        (acc_ref,) = acc
        kk = pl.program_id(2)

        @pl.when(kk == 0)
        def _():
            acc_ref[...] = part

        @pl.when(kk > 0)
        def _():
            acc_ref[...] += part

        @pl.when(kk == nk - 1)
        def _():
            o_ref[...] = acc_ref[...].astype(o_ref.dtype)

    a_spec = pl.BlockSpec((tk, tm), lambda i, j, kk: (kk, i)) if mode == "tn" else pl.BlockSpec((tm, tk), lambda i, j, kk: (i, kk))
    b_spec = pl.BlockSpec((tn, tk), lambda i, j, kk: (j, kk)) if mode == "nt" else pl.BlockSpec((tk, tn), lambda i, j, kk: (kk, j))
    return pl.pallas_call(
        body, grid=(m // tm, n // tn, nk), in_specs=[a_spec, b_spec],
        out_specs=pl.BlockSpec((tm, tn), lambda i, j, kk: (i, j)),
        out_shape=jax.ShapeDtypeStruct((m, n), out_dtype),
        scratch_shapes=[pltpu.VMEM((tm, tn), F32)] if nk > 1 else [], name=name,
        compiler_params=pltpu.CompilerParams(dimension_semantics=("parallel", "parallel", "arbitrary")),
    )(a, b)


def _lane_tile(n, cap):
    return max([n // s for s in range(1, n // LANE + 1) if n % s == 0 and (n // s) % LANE == 0 and n // s <= cap] or [n])


def _ffn_in(h, w_gate, w_up):
    m, k = h.shape
    f = w_gate.shape[0]
    tm, tn = _pick(m, (1024, 512, 256, 128)), _lane_tile(f, 1408)

    def body(h_ref, wg_ref, wu_ref, act_ref, dg_ref, du_ref):
        hh = h_ref[...]
        g = _dot(hh, wg_ref[...], NT)
        u = _dot(hh, wu_ref[...], NT)
        sg = _sigmoid(g)
        silu_g = g * sg
        act_ref[...] = (silu_g * u).astype(act_ref.dtype)
        dg_ref[...] = (u * (sg + silu_g * (1.0 - sg))).astype(dg_ref.dtype)
        du_ref[...] = silu_g.astype(du_ref.dtype)

    w_spec = pl.BlockSpec((tn, k), lambda i, j: (j, 0))
    o_spec = pl.BlockSpec((tm, tn), lambda i, j: (i, j))
    return pl.pallas_call(
        body, grid=(m // tm, f // tn), in_specs=[pl.BlockSpec((tm, k), lambda i, j: (i, 0)), w_spec, w_spec],
        out_specs=[o_spec] * 3, out_shape=[jax.ShapeDtypeStruct((m, f), BF)] * 3, name="ffn_in",
        compiler_params=pltpu.CompilerParams(dimension_semantics=("parallel", "parallel")),
    )(h, w_gate, w_up)


def _ffn_act_bwd(dff, w_down, act_dg, act_du):
    m, k = dff.shape
    f = w_down.shape[0]
    tm, tn = _pick(m, (1024, 512, 256, 128)), _lane_tile(f, 1408)

    def body(d_ref, w_ref, fg_ref, fu_ref, dg_ref, du_ref):
        da = _dot(d_ref[...], w_ref[...], NT)
        dg_ref[...] = (da * fg_ref[...].astype(F32)).astype(dg_ref.dtype)
        du_ref[...] = (da * fu_ref[...].astype(F32)).astype(du_ref.dtype)

    o_spec = pl.BlockSpec((tm, tn), lambda i, j: (i, j))
    return pl.pallas_call(
        body, grid=(m // tm, f // tn),
        in_specs=[pl.BlockSpec((tm, k), lambda i, j: (i, 0)), pl.BlockSpec((tn, k), lambda i, j: (j, 0)), o_spec, o_spec],
        out_specs=[o_spec] * 2, out_shape=[jax.ShapeDtypeStruct((m, f), BF)] * 2, name="d_ffn_act",
        compiler_params=pltpu.CompilerParams(dimension_semantics=("parallel", "parallel")),
    )(dff, w_down, act_dg, act_du)


def _matmul2_nn(a1, b1, a2, b2, name, out_dtype=F32):
    m, k = a1.shape
    n = b1.shape[1]
    tm, tn = _pick(m, (1024, 512, 256, 128)), _pick(n, (512, 256, 128))

    def body(a1_ref, b1_ref, a2_ref, b2_ref, o_ref):
        o_ref[...] = (_dot(a1_ref[...], b1_ref[...]) + _dot(a2_ref[...], b2_ref[...])).astype(o_ref.dtype)

    a_spec = pl.BlockSpec((tm, k), lambda i, j: (i, 0))
    b_spec = pl.BlockSpec((k, tn), lambda i, j: (0, j))
    return pl.pallas_call(
        body, grid=(m // tm, n // tn), in_specs=[a_spec, b_spec, a_spec, b_spec],
        out_specs=pl.BlockSpec((tm, tn), lambda i, j: (i, j)), out_shape=jax.ShapeDtypeStruct((m, n), out_dtype), name=name,
        compiler_params=pltpu.CompilerParams(dimension_semantics=("parallel", "parallel")),
    )(a1, b1, a2, b2)


MATMUL_VMEM_BUDGET = 28 * 1024 * 1024


def _matmul_tiles(m, n, k, a_bytes, b_bytes, o_bytes):
    def divisors(x, cap):
        return sorted({x // s for s in range(1, 65) if x % s == 0 and (x // s) % LANE == 0 and x // s <= cap}, reverse=True) or [x]

    for tk in divisors(k, k):
        best = None
        for tm in divisors(m, 1024):
            for tn in divisors(n, 2048):
                need = 2 * (tm * tk * a_bytes + tk * tn * b_bytes + tm * tn * o_bytes) + (tm * tn * 4 if tk < k else 0)
                if need <= MATMUL_VMEM_BUDGET and tm * tn >= 512 * 512 and (best is None or tm * tn > best[0] * best[1]):
                    best = (tm, tn)
        if best:
            return best[0], best[1], tk
    return _pick(m, (512, 256, 128)), _pick(n, (512, 256, 128)), _pick(k, (512, 256, 128))


def _exchange(xs, name, scatter):
    n = len(xs)
    npeer = N_DEV - 1

    def body(*refs):
        x_refs, o_refs = refs[:n], refs[n:2 * n]
        send_sems, recv_sems, local_sems = refs[2 * n:]
        mx, my, mc = lax.axis_index("x"), lax.axis_index("y"), lax.axis_index("c")
        me = 4 * mx + 2 * my + mc
        src_me = [x.at[me] if scatter else x for x in x_refs]
        mine = [pltpu.make_async_copy(src_me[a], o_refs[a].at[me], local_sems.at[a]) for a in range(n)]
        for cp in mine:
            cp.start()
        copies = []
        for k in range(1, N_DEV):
            px, py, pc = mx ^ (k >> 2), my ^ ((k >> 1) & 1), mc ^ (k & 1)
            peer = 4 * px + 2 * py + pc
            for a in range(n):
                cp = pltpu.make_async_remote_copy(
                    src_ref=x_refs[a].at[peer] if scatter else x_refs[a], dst_ref=o_refs[a].at[me],
                    send_sem=send_sems.at[a * npeer + k - 1], recv_sem=recv_sems.at[a * npeer + k - 1],
                    device_id=(px, py, pc), device_id_type=pl.DeviceIdType.MESH)
                cp.start()
                copies.append((cp, a, k, peer))
        for cp, a, k, peer in copies:
            pltpu.make_async_remote_copy(
                src_ref=src_me[a], dst_ref=o_refs[a].at[peer], send_sem=send_sems.at[a * npeer + k - 1],
                recv_sem=recv_sems.at[a * npeer + k - 1], device_id=(mx, my, mc),
                device_id_type=pl.DeviceIdType.MESH).wait_recv()
        for cp, _, _, _ in copies:
            cp.wait_send()
        for cp in mine:
            cp.wait()

    return pl.pallas_call(
        body, out_shape=[jax.ShapeDtypeStruct((N_DEV,) + x.shape[-2:], x.dtype) for x in xs],
        in_specs=[pl.BlockSpec(memory_space=pl.ANY)] * n, out_specs=[pl.BlockSpec(memory_space=pl.ANY)] * n,
        scratch_shapes=[pltpu.SemaphoreType.DMA((n * npeer,)), pltpu.SemaphoreType.DMA((n * npeer,)),
                        pltpu.SemaphoreType.DMA((n,))],
        name=name,
    )(*xs)


def _gather_by_chip(xs, name):
    n = len(xs)
    per = N_DEV - 1

    def body(*refs):
        x_refs, o_refs = refs[:n], refs[n:2 * n]
        send_sems, recv_sems, local_sems = refs[2 * n:]
        mx, my, mc = lax.axis_index("x"), lax.axis_index("y"), lax.axis_index("c")
        me, sibling = (mx, my, mc), (mx, my, 1 - mc)
        chips = [(1 - mx, my), (mx, 1 - my), (1 - mx, 1 - my)]
        slot = lambda d: 4 * d[0] + 2 * d[1] + d[2]

        def copy(a, k, block, to, src=None):
            dst = o_refs[a].at[slot(block)]
            return pltpu.make_async_remote_copy(
                src_ref=dst if src is None else src, dst_ref=dst, send_sem=send_sems.at[a * per + k],
                recv_sem=recv_sems.at[a * per + k], device_id=to, device_id_type=pl.DeviceIdType.MESH)

        mine = [pltpu.make_async_copy(x_refs[a], o_refs[a].at[slot(me)], local_sems.at[a]) for a in range(n)]
        for cp in mine:
            cp.start()
        first = []
        for a in range(n):
            first.append(copy(a, 0, me, sibling, src=x_refs[a]))
            first += [copy(a, 1 + j, me, (*chip, mc), src=x_refs[a]) for j, chip in enumerate(chips)]
        for cp in first:
            cp.start()
        passed = []
        for j, chip in enumerate(chips):
            for a in range(n):
                copy(a, 1 + j, (*chip, mc), me).wait_recv()
                cp = copy(a, 4 + j, (*chip, mc), sibling)
                cp.start()
                passed.append(cp)
        for a in range(n):
            copy(a, 0, sibling, me).wait_recv()
            for j, chip in enumerate(chips):
                copy(a, 4 + j, (*chip, 1 - mc), me).wait_recv()
        for cp in first + passed:
            cp.wait_send()
        for cp in mine:
            cp.wait()

    return pl.pallas_call(
        body, out_shape=[jax.ShapeDtypeStruct((N_DEV,) + x.shape, x.dtype) for x in xs],
        in_specs=[pl.BlockSpec(memory_space=pl.ANY)] * n, out_specs=[pl.BlockSpec(memory_space=pl.ANY)] * n,
        scratch_shapes=[pltpu.SemaphoreType.DMA((n * per,)), pltpu.SemaphoreType.DMA((n * per,)),
                        pltpu.SemaphoreType.DMA((n,))],
        name=name,
    )(*xs)


def _peer_of(k):
    mx, my, mc = lax.axis_index("x"), lax.axis_index("y"), lax.axis_index("c")
    px, py, pc = mx ^ (k >> 2), my ^ ((k >> 1) & 1), mc ^ (k & 1)
    return (px, py, pc), 4 * px + 2 * py + pc


def _exchange_start(xs, name, scatter):
    n = len(xs)
    npeer = N_DEV - 1

    def body(*refs):
        x_refs, land_refs = refs[:n], refs[n:2 * n]
        send_sems, recv_sems, token = refs[2 * n], refs[2 * n + 1], refs[-1]
        me = 4 * lax.axis_index("x") + 2 * lax.axis_index("y") + lax.axis_index("c")
        for k in range(1, N_DEV):
            dev, peer = _peer_of(k)
            for a in range(n):
                pltpu.make_async_remote_copy(
                    src_ref=x_refs[a].at[peer] if scatter else x_refs[a], dst_ref=land_refs[a].at[me],
                    send_sem=send_sems.at[a * npeer + k - 1], recv_sem=recv_sems.at[a * npeer + k - 1],
                    device_id=dev, device_id_type=pl.DeviceIdType.MESH).start()
        token[...] = jnp.zeros_like(token)

    hbm = pl.BlockSpec(memory_space=pltpu.HBM)
    sem = pl.BlockSpec(memory_space=pltpu.SEMAPHORE)
    lands = [pltpu.with_memory_space_constraint(lax.empty((N_DEV,) + x.shape[-2:], x.dtype), pltpu.HBM) for x in xs]
    srcs = [pltpu.with_memory_space_constraint(x, pltpu.HBM) for x in xs]
    outs = pl.pallas_call(
        body, name=name,
        out_shape=(pltpu.SemaphoreType.DMA((n * npeer,)), pltpu.SemaphoreType.DMA((n * npeer,)),
                   *[pltpu.HBM(x.shape, x.dtype) for x in srcs], *[pltpu.HBM(z.shape, z.dtype) for z in lands],
                   jax.ShapeDtypeStruct((8, LANE), F32)),
        in_specs=[hbm] * (2 * n), out_specs=(sem, sem, *[hbm] * (2 * n), pl.BlockSpec(memory_space=pltpu.VMEM)),
        input_output_aliases={i: 2 + i for i in range(2 * n)},
        compiler_params=pltpu.CompilerParams(has_side_effects=pltpu.SideEffectType.DATAFLOW_SIDE_EFFECTING),
    )(*srcs, *lands)
    return (outs[0], outs[1], list(outs[2:2 + n]), list(outs[2 + n:2 + 2 * n])), outs[-1][0:1, 0:1]


def _exchange_wait(started, after, name, scatter):
    send_sems, recv_sems, srcs, lands = started
    n = len(srcs)
    npeer = N_DEV - 1

    def body(*refs):
        x_refs, land_refs = refs[:n], refs[n:2 * n]
        send_sems, recv_sems = refs[2 * n], refs[2 * n + 1]
        mx, my, mc = lax.axis_index("x"), lax.axis_index("y"), lax.axis_index("c")
        me = 4 * mx + 2 * my + mc
        for k in range(1, N_DEV):
            _, peer = _peer_of(k)
            for a in range(n):
                src = x_refs[a].at[me] if scatter else x_refs[a]
                cp = pltpu.make_async_remote_copy(
                    src_ref=src, dst_ref=land_refs[a].at[peer], send_sem=send_sems.at[a * npeer + k - 1],
                    recv_sem=recv_sems.at[a * npeer + k - 1], device_id=(mx, my, mc), device_id_type=pl.DeviceIdType.MESH)
                cp.wait_send()
                cp.wait_recv()

    hbm = pl.BlockSpec(memory_space=pltpu.HBM)
    sem = pl.BlockSpec(memory_space=pltpu.SEMAPHORE)
    outs = pl.pallas_call(
        body, name=name,
        out_shape=(*[pltpu.HBM(x.shape, x.dtype) for x in srcs], *[pltpu.HBM(z.shape, z.dtype) for z in lands]),
        in_specs=[hbm] * (2 * n) + [sem, sem, pl.BlockSpec(memory_space=pl.ANY)], out_specs=tuple([hbm] * (2 * n)),
        input_output_aliases={i: i for i in range(2 * n)},
        compiler_params=pltpu.CompilerParams(has_side_effects=pltpu.SideEffectType.DATAFLOW_SIDE_EFFECTING),
    )(*srcs, *lands, send_sems, recv_sems, after)
    me = 4 * lax.axis_index("x") + 2 * lax.axis_index("y") + lax.axis_index("c")
    full = []
    for x, land in zip(outs[:n], outs[n:]):
        own = lax.dynamic_slice(x, (me, 0, 0), (1,) + x.shape[1:]) if scatter else x[None]
        full.append(lax.dynamic_update_slice(land, own, (me, 0, 0)))
    return full


def _sum_slots(x, name):
    _, r, c = x.shape
    tr = _pick(r, (512, 256, 128, 64, 32, 16))

    def body(x_ref, o_ref):
        acc = x_ref[0].astype(F32)
        for s in range(1, N_DEV):
            acc = acc + x_ref[s].astype(F32)
        o_ref[...] = acc

    return pl.pallas_call(
        body, grid=(r // tr,), in_specs=[pl.BlockSpec((N_DEV, tr, c), lambda i: (0, i, 0))],
        out_specs=pl.BlockSpec((tr, c), lambda i: (i, 0)), out_shape=jax.ShapeDtypeStruct((r, c), F32), name=name,
        compiler_params=pltpu.CompilerParams(dimension_semantics=("arbitrary",)),
    )(x)


def _mod_fwd(c_all, w_ada, b_ada_mine):
    def body(c_ref, w_ref, b_ref, o_ref):
        o_ref[...] = _doth(_silu(c_ref[...]), w_ref[...]) + b_ref[...]

    return pl.pallas_call(body, out_shape=jax.ShapeDtypeStruct((c_all.shape[0], w_ada.shape[1]), F32), name="mod_fwd")(c_all, w_ada, b_ada_mine)


def _mod_bwd(c_all_t, dmod_mine):
    def body(ct_ref, d_ref, o_ref):
        s = _silu(ct_ref[...])
        acc = s[:, 0:1] * d_ref[pl.ds(0, 1), :]
        for b in range(1, N_DEV):
            acc = acc + s[:, b:b + 1] * d_ref[pl.ds(b, 1), :]
        o_ref[...] = acc

    return pl.pallas_call(body, out_shape=jax.ShapeDtypeStruct((c_all_t.shape[0], dmod_mine.shape[1]), F32), name="mod_bwd")(c_all_t, dmod_mine)


def _conv_fwd(proj, conv_w8, tm):
    t = proj.shape[0]
    ch = DN_CONV_CH

    def body(x_ref, w_ref, o_ref, buf):
        @pl.when(pl.program_id(0) == 0)
        def _():
            buf[pl.ds(0, CONV_HALO), :] = jnp.zeros((CONV_HALO, ch), F32)

        buf[pl.ds(CONV_HALO, tm), :] = x_ref[...].astype(F32)
        for c0 in range(0, ch, CONV_COLS):
            cols = pl.ds(c0, CONV_COLS)
            w = [w_ref[pl.ds(j, 1), cols] for j in range(CONV_K)]
            for r0 in range(0, tm, CONV_ROWS):
                acc = buf[pl.ds(r0 + CONV_HALO - (CONV_K - 1), CONV_ROWS), cols] * w[0]
                for j in range(1, CONV_K):
                    acc = acc + buf[pl.ds(r0 + CONV_HALO - (CONV_K - 1) + j, CONV_ROWS), cols] * w[j]
                o_ref[pl.ds(r0, CONV_ROWS), cols] = _silu(acc)
        buf[pl.ds(0, CONV_HALO), :] = buf[pl.ds(tm, CONV_HALO), :]

    return pl.pallas_call(
        body, grid=(t // tm,), in_specs=[pl.BlockSpec((tm, ch), lambda i: (i, 0)), _full(conv_w8.shape)],
        out_specs=pl.BlockSpec((tm, ch), lambda i: (i, 0)), out_shape=jax.ShapeDtypeStruct((t, ch), F32),
        scratch_shapes=[pltpu.VMEM((tm + CONV_HALO, ch), F32)], name="conv_fwd",
        compiler_params=pltpu.CompilerParams(dimension_semantics=("arbitrary",)),
    )(proj, conv_w8)


def _conv_bwd(proj, conv_w8, dact, tm):
    t = proj.shape[0]
    ch = DN_CONV_CH
    nt = t // tm
    halo_blk = 2 * CONV_HALO
    hb = tm // halo_blk

    def body(x_ref, xp_ref, w_ref, dy_ref, dx_ref, dw_ref, xbuf, dbuf):
        step = pl.program_id(0)

        @pl.when(step == 0)
        def _():
            dbuf[pl.ds(tm, CONV_HALO), :] = jnp.zeros((CONV_HALO, ch), F32)
            dw_ref[...] = jnp.zeros_like(dw_ref)

        first = step == nt - 1
        xbuf[pl.ds(0, CONV_HALO), :] = jnp.where(first, 0.0, xp_ref[...].astype(F32)[halo_blk - CONV_HALO:])
        xbuf[pl.ds(CONV_HALO, tm), :] = x_ref[...].astype(F32)
        for c0 in range(0, ch, CONV_COLS):
            cols = pl.ds(c0, CONV_COLS)
            w = [w_ref[pl.ds(j, 1), cols] for j in range(CONV_K)]
            dw = [jnp.zeros((1, CONV_COLS), F32) for _ in range(CONV_K)]
            for r0 in range(0, tm, CONV_ROWS):
                xs = [xbuf[pl.ds(r0 + CONV_HALO - (CONV_K - 1) + j, CONV_ROWS), cols] for j in range(CONV_K)]
                pre = xs[0] * w[0]
                for j in range(1, CONV_K):
                    pre = pre + xs[j] * w[j]
                sg = _sigmoid(pre)
                dpre = dy_ref[pl.ds(r0, CONV_ROWS), cols] * (sg * (1.0 + pre * (1.0 - sg)))
                dbuf[pl.ds(r0, CONV_ROWS), cols] = dpre
                dw = [dw[j] + jnp.sum(dpre * xs[j], axis=0, keepdims=True) for j in range(CONV_K)]
            for j in range(CONV_K):
                dw_ref[pl.ds(j, 1), cols] += dw[j]
            for r0 in range(0, tm, CONV_ROWS):
                dx = dbuf[pl.ds(r0 + CONV_K - 1, CONV_ROWS), cols] * w[0]
                for j in range(1, CONV_K):
                    dx = dx + dbuf[pl.ds(r0 + CONV_K - 1 - j, CONV_ROWS), cols] * w[j]
                dx_ref[pl.ds(r0, CONV_ROWS), cols] = dx.astype(dx_ref.dtype)
        dbuf[pl.ds(tm, CONV_HALO), :] = dbuf[pl.ds(0, CONV_HALO), :]

    rev = lambda i: (nt - 1 - i, 0)
    prev = lambda i: (jnp.maximum((nt - 1 - i) * hb - 1, 0), 0)
    return pl.pallas_call(
        body, grid=(nt,),
        in_specs=[pl.BlockSpec((tm, ch), rev), pl.BlockSpec((halo_blk, ch), prev), _full(conv_w8.shape),
                  pl.BlockSpec((tm, ch), rev)],
        out_specs=[pl.BlockSpec((tm, ch), rev), _full(conv_w8.shape)],
        out_shape=[jax.ShapeDtypeStruct((t, ch), BF), jax.ShapeDtypeStruct(conv_w8.shape, F32)],
        scratch_shapes=[pltpu.VMEM((tm + CONV_HALO, ch), F32), pltpu.VMEM((tm + CONV_HALO, ch), F32)], name="conv_bwd",
        compiler_params=pltpu.CompilerParams(dimension_semantics=("arbitrary",)),
    )(proj, proj, conv_w8, dact)


BNN = (((2,), (1,)), ((0,), (0,)))
BNT = (((2,), (2,)), ((0,), (0,)))
BTN = (((1,), (1,)), ((0,), (0,)))


def _bdot(a, b, dims, precision=None):
    return lax.dot_general(a, b, dims, precision=precision, preferred_element_type=F32)


@jax.custom_vjp
def _bmmb_nt(a, b):
    return _bdot(a.astype(BF), b.astype(BF), BNT)


def _bmmb_nt_fwd(a, b):
    return _bmmb_nt(a, b), (a, b)


def _bmmb_nt_bwd(res, g):
    a, b = res
    gb = g.astype(BF)
    return _bdot(gb, b.astype(BF), BNN), _bdot(gb, a.astype(BF), BTN)


_bmmb_nt.defvjp(_bmmb_nt_fwd, _bmmb_nt_bwd)


@jax.custom_vjp
def _bmmb(a, b):
    return _bdot(a.astype(BF), b.astype(BF), BNN)


def _bmmb_fwd(a, b):
    return _bmmb(a, b), (a, b)


def _bmmb_bwd(res, g):
    a, b = res
    gb = g.astype(BF)
    return _bdot(gb, b.astype(BF), BNT), _bdot(a.astype(BF), gb, BTN)


_bmmb.defvjp(_bmmb_fwd, _bmmb_bwd)


@jax.custom_vjp
def _bmmb_tn(a, b):
    return _bdot(a.astype(BF), b.astype(BF), BTN)


def _bmmb_tn_fwd(a, b):
    return _bmmb_tn(a, b), (a, b)


def _bmmb_tn_bwd(res, g):
    a, b = res
    gb = g.astype(BF)
    return _bdot(b.astype(BF), gb, BNT), _bdot(a.astype(BF), gb, BNN)


_bmmb_tn.defvjp(_bmmb_tn_fwd, _bmmb_tn_bwd)


def _unit_lower_solve_fwd(a, r):
    c = a.shape[-1]
    ri = lax.broadcasted_iota(jnp.int32, a.shape, 1)
    ci = lax.broadcasted_iota(jnp.int32, a.shape, 2)
    xm = -a
    inv = (ri == ci).astype(F32) + xm
    for _ in range(int(math.log2(c)) - 1):
        xm = _bdot(xm, xm, BNN, HI)
        inv = inv + _bdot(inv, xm, BNN, HI)
    x = _bdot(inv, r, BNN, HI)
    return x, (inv, x)


def _unit_lower_solve_bwd(res, g):
    inv, x = res
    dr = _bdot(inv, g, BTN, HI)
    return -_bdot(dr, x, BNT, HI), dr


@jax.custom_vjp
def _unit_lower_solve_given(a, r, inv):
    return _bdot(inv, r, BNN, HI)


def _unit_lower_solve_given_fwd(a, r, inv):
    x = _bdot(inv, r, BNN, HI)
    return x, (inv, x)


def _unit_lower_solve_given_bwd(res, g):
    da, dr = _unit_lower_solve_bwd(res, g)
    return da, dr, jnp.zeros_like(res[0])


_unit_lower_solve_given.defvjp(_unit_lower_solve_given_fwd, _unit_lower_solve_given_bwd)


def _gdn_intra(qkv, ba, al8, dt8, inv4=None):
    tm = qkv.shape[0]
    nb = tm // CHUNK
    bsz = DN_HEADS * nb

    def heads(x0):
        return jnp.concatenate([qkv[:, x0 + h * LANE:x0 + (h + 1) * LANE].reshape(nb, CHUNK, LANE) for h in range(DN_HEADS)], axis=0)

    def spread(c0):
        return jnp.concatenate([jnp.broadcast_to(ba[:, c0 + h:c0 + h + 1], (tm, LANE)).reshape(nb, CHUNK, LANE)
                                for h in range(DN_HEADS)], axis=0)

    def per_head(v8):
        return jnp.concatenate([jnp.broadcast_to(v8[0:1, h:h + 1].reshape(1, 1, 1), (nb, 1, LANE)) for h in range(DN_HEADS)], axis=0)

    ri = lax.broadcasted_iota(jnp.int32, (bsz, CHUNK, CHUNK), 1)
    ci = lax.broadcasted_iota(jnp.int32, (bsz, CHUNK, CHUNK), 2)
    incl = ri >= ci
    strict = ri > ci

    q = _l2norm(heads(0)) * (DN_DK ** -0.5)
    k = _l2norm(heads(DN_QK))
    va = heads(2 * DN_QK)
    beta = _sigmoid(spread(0))
    g = -jnp.exp(per_head(al8)) * _softplus(spread(DN_HEADS) + per_head(dt8))
    gc = _bdot(incl.astype(F32), g, BNN, HI)
    g_last = jnp.sum(g, axis=1, keepdims=True)
    gcol = gc[:, :, :CHUNK]
    diff = gcol - jnp.swapaxes(gcol, 1, 2)
    decay = jnp.where(incl, jnp.exp(jnp.where(incl, diff, 0.0)), 0.0)
    kb = k * beta
    a_mat = jnp.where(strict, _bmmb_nt(kb, k) * decay, 0.0)
    egc = jnp.exp(gc)
    rhs = jnp.concatenate([kb * egc, va * beta], axis=2)
    if inv4 is None:
        wu, (inv, _) = _unit_lower_solve_fwd(a_mat, rhs)
    else:
        wu = _unit_lower_solve_given(a_mat, rhs, inv4.reshape(bsz, CHUNK, CHUNK))
    attn = jnp.where(incl, _bmmb_nt(q, k) * decay, 0.0)

    def unheads(x):
        return jnp.concatenate([x[h * nb:(h + 1) * nb].reshape(tm, LANE) for h in range(DN_HEADS)], axis=1)

    w_c, u_c = wu[:, :, :DN_DK], wu[:, :, DN_DK:]
    kd = k * jnp.exp(g_last - gc)
    out = (unheads(q * egc - _bmmb(attn, w_c)), unheads(_bmmb(attn, u_c)),
           _bmmb_tn(kd, w_c).reshape(DN_HEADS, nb, DN_DK, DN_DK), _bmmb_tn(kd, u_c).reshape(DN_HEADS, nb, DN_DK, DN_DV),
           jnp.broadcast_to(g_last, (bsz, GL_ROWS, LANE)).reshape(DN_HEADS, nb, GL_ROWS, LANE))
    return out if inv4 is not None else out + (inv.reshape(DN_HEADS, nb, CHUNK, CHUNK),)


def _gdn_scan_step(qp, op, c_mat, n_mat, gl, s):
    return _mmb(qp, s) + op, s * jnp.exp(gl) - _mmb(c_mat, s) + n_mat


def _gdn_intra_specs(t, tm, dts, order=lambda i: i):
    nb = tm // CHUNK
    row = pl.BlockSpec((tm, DN_VW), lambda i: (order(i), 0))
    mat = pl.BlockSpec((DN_HEADS, nb, DN_DK, DN_DV), lambda i: (0, order(i), 0, 0))
    row_shape = lambda d: jax.ShapeDtypeStruct((t, DN_VW), d)
    mat_shape = lambda d: jax.ShapeDtypeStruct((DN_HEADS, t // CHUNK, DN_DK, DN_DV), d)
    gl = pl.BlockSpec((DN_HEADS, nb, GL_ROWS, LANE), lambda i: (0, order(i), 0, 0))
    gl_shape = jax.ShapeDtypeStruct((DN_HEADS, t // CHUNK, GL_ROWS, LANE), dts[4])
    return [row, row, mat, mat, gl], [row_shape(dts[0]), row_shape(dts[1]), mat_shape(dts[2]), mat_shape(dts[3]), gl_shape]


def _gdn_intra_fwd(qkv, proj, al8, dt8, tm):
    t = qkv.shape[0]

    def body(qkv_ref, ba_ref, al_ref, dt_ref, *outs):
        for o, val in zip(outs, _gdn_intra(qkv_ref[...], ba_ref[...], al_ref[...], dt_ref[...])):
            o[...] = val.astype(o.dtype)

    specs, shapes = _gdn_intra_specs(t, tm, (BF, F32, BF, BF, F32))
    specs.append(_gdn_inverse_spec(tm))
    shapes.append(jax.ShapeDtypeStruct((DN_HEADS, t // CHUNK, CHUNK, CHUNK), F32))
    res = pl.pallas_call(
        body, grid=(t // tm,),
        in_specs=[pl.BlockSpec((tm, DN_CONV_CH), lambda i: (i, 0)), pl.BlockSpec((tm, LANE), lambda i: (i, 0)),
                  _full(al8.shape), _full(dt8.shape)],
        out_specs=specs, out_shape=shapes, name="gdn_intra_fwd",
        compiler_params=pltpu.CompilerParams(dimension_semantics=("parallel",)),
    )(qkv, proj, al8, dt8)
    return res[:5], res[5]


def _gdn_inverse_spec(tm):
    return pl.BlockSpec((DN_HEADS, tm // CHUNK, CHUNK, CHUNK), lambda i: (0, i, 0, 0))


def _gdn_intra_bwd(qkv, proj, al8, dt8, inverses, cts, tm):
    t = qkv.shape[0]

    def body(qkv_ref, ba_ref, al_ref, dt_ref, inv_ref, *refs):
        ct_refs, (dqkv_ref, dba_ref, dal_ref, ddt_ref) = refs[:5], refs[5:]

        @pl.when(pl.program_id(0) == 0)
        def _():
            dal_ref[...] = jnp.zeros_like(dal_ref)
            ddt_ref[...] = jnp.zeros_like(ddt_ref)

        _, vjp = jax.vjp(functools.partial(_gdn_intra, inv4=inv_ref[...]), qkv_ref[...], ba_ref[...], al_ref[...], dt_ref[...])
        dqkv, dba, dal, ddt = vjp(tuple(r[...].astype(F32) for r in ct_refs))
        dqkv_ref[...] = dqkv.astype(dqkv_ref.dtype)
        dba_ref[...] = dba.astype(dba_ref.dtype)
        dal_ref[...] += dal
        ddt_ref[...] += ddt

    specs, _ = _gdn_intra_specs(t, tm, (F32,) * 5)
    return pl.pallas_call(
        body, grid=(t // tm,),
        in_specs=[pl.BlockSpec((tm, DN_CONV_CH), lambda i: (i, 0)), pl.BlockSpec((tm, LANE), lambda i: (i, 0)),
                  _full(al8.shape), _full(dt8.shape), _gdn_inverse_spec(tm)] + specs,
        out_specs=[pl.BlockSpec((tm, DN_CONV_CH), lambda i: (i, 0)), pl.BlockSpec((tm, LANE), lambda i: (i, 0)),
                   _full(al8.shape), _full(dt8.shape)],
        out_shape=[jax.ShapeDtypeStruct((t, DN_CONV_CH), BF), jax.ShapeDtypeStruct((t, LANE), BF),
                   jax.ShapeDtypeStruct(al8.shape, F32), jax.ShapeDtypeStruct(dt8.shape, F32)],
        name="gdn_intra_bwd", compiler_params=pltpu.CompilerParams(dimension_semantics=("arbitrary",)),
    )(qkv, proj, al8, dt8, inverses, *cts)


def _gdn_scan_fwd(intra, tm):
    t = intra[0].shape[0]
    nb = tm // CHUNK
    nc = t // CHUNK

    def body(qp_ref, op_ref, c_ref, n_ref, gl_ref, o_ref, ss_ref, s_scr):
        @pl.when(pl.program_id(0) == 0)
        def _():
            s_scr[...] = jnp.zeros_like(s_scr)

        state = [s_scr[h] for h in range(DN_HEADS)]
        for cc in range(nb):
            rows = pl.ds(cc * CHUNK, CHUNK)
            for h in range(DN_HEADS):
                cols = pl.ds(h * DN_DV, DN_DV)
                ss_ref[cc, h] = state[h].astype(ss_ref.dtype)
                o_ref[rows, cols], state[h] = _gdn_scan_step(
                    qp_ref[rows, cols], op_ref[rows, cols], c_ref[h, cc], n_ref[h, cc], gl_ref[h, cc, pl.ds(0, 1), :], state[h])
        for h in range(DN_HEADS):
            s_scr[h] = state[h]

    specs, _ = _gdn_intra_specs(t, tm, (F32,) * 5)
    return pl.pallas_call(
        body, grid=(t // tm,), in_specs=specs,
        out_specs=[pl.BlockSpec((tm, DN_VW), lambda i: (i, 0)),
                   pl.BlockSpec((nb, DN_HEADS, DN_DK, DN_DV), lambda i: (i, 0, 0, 0))],
        out_shape=[jax.ShapeDtypeStruct((t, DN_VW), F32), jax.ShapeDtypeStruct((nc, DN_HEADS, DN_DK, DN_DV), BF)],
        scratch_shapes=[pltpu.VMEM((DN_HEADS, DN_DK, DN_DV), F32)], name="gdn_scan_fwd",
        compiler_params=pltpu.CompilerParams(dimension_semantics=("arbitrary",)),
    )(*intra)


def _gdn_scan_bwd(intra, states, do, tm):
    t = intra[0].shape[0]
    nb = tm // CHUNK
    ng = t // tm

    def body(qp_ref, op_ref, c_ref, n_ref, gl_ref, ss_ref, do_ref, dqp_ref, dop_ref, dc_ref, dn_ref, dgl_ref, ds_scr):
        @pl.when(pl.program_id(0) == 0)
        def _():
            ds_scr[...] = jnp.zeros_like(ds_scr)

        d_state = [ds_scr[h] for h in range(DN_HEADS)]
        for cc in reversed(range(nb)):
            rows = pl.ds(cc * CHUNK, CHUNK)
            for h in range(DN_HEADS):
                cols = pl.ds(h * DN_DV, DN_DV)
                _, vjp = jax.vjp(_gdn_scan_step, qp_ref[rows, cols].astype(F32), op_ref[rows, cols], c_ref[h, cc].astype(F32),
                                 n_ref[h, cc].astype(F32), gl_ref[h, cc, pl.ds(0, 1), :], ss_ref[cc, h].astype(F32))
                dqp_ref[rows, cols], dop_ref[rows, cols], dc, dn, dgl, d_state[h] = vjp((do_ref[rows, cols], d_state[h]))
                dc_ref[h, cc] = dc.astype(dc_ref.dtype)
                dn_ref[h, cc] = dn.astype(dn_ref.dtype)
                first_row = lax.broadcasted_iota(jnp.int32, (GL_ROWS, LANE), 0) == 0
                dgl_ref[h, cc] = jnp.where(first_row, dgl, 0.0)
        for h in range(DN_HEADS):
            ds_scr[h] = d_state[h]

    five, shapes = _gdn_intra_specs(t, tm, (F32, F32, BF, BF, F32), order=lambda i: ng - 1 - i)
    row = five[0]
    return pl.pallas_call(
        body, grid=(ng,),
        in_specs=five + [pl.BlockSpec((nb, DN_HEADS, DN_DK, DN_DV), lambda i: (ng - 1 - i, 0, 0, 0)), row],
        out_specs=five, out_shape=shapes,
        scratch_shapes=[pltpu.VMEM((DN_HEADS, DN_DK, DN_DV), F32)], name="gdn_scan_bwd",
        compiler_params=pltpu.CompilerParams(dimension_semantics=("arbitrary",)),
    )(*intra, states, do)


def _gdn_out(o, z, g):
    parts = []
    for h in range(DN_HEADS):
        sl = slice(h * DN_DV, (h + 1) * DN_DV)
        parts.append(_rmsnorm(o[:, sl], g) * _silu(z[:, sl]))
    return parts


_Q_SCALE = math.log2(math.e) / math.sqrt(QK_NOPE + QK_ROPE)


def _rope_tables(pos, inv_freq2):
    lane = lax.broadcasted_iota(jnp.int32, (1, LANE), 1)
    ang = pos * inv_freq2
    cos = jnp.where(lane < QK_ROPE, jnp.cos(ang), 0.0)
    sin = jnp.where(lane < QK_ROPE // 2, -jnp.sin(ang), jnp.where(lane < QK_ROPE, jnp.sin(ang), 0.0))
    return cos, sin


@jax.custom_vjp
def _rope_swap(u):
    lane = lax.broadcasted_iota(jnp.int32, u.shape, 1)
    half = QK_ROPE // 2
    return jnp.where(lane < half, pltpu.roll(u, LANE - half, 1), jnp.where(lane < QK_ROPE, pltpu.roll(u, half, 1), 0.0))


_rope_swap.defvjp(lambda u: (_rope_swap(u), None), lambda _, g: (_rope_swap(g),))


def _mla_prep(cq, ckv, kr, gq, gkv, w_uq, w_ukv, cos, sin):
    rope = lambda u: u * cos + _rope_swap(u) * sin
    q_lin = _mmb_nt(_rmsnorm(cq, gq), w_uq) * _Q_SCALE
    kv_lin = _mmb_nt(_rmsnorm(ckv, gkv), w_ukv)
    k_rope = rope(kr)
    qs, ks, vs = [], [], []
    for h in range(MLA_HEADS):
        qs += [q_lin[:, h * LANE:(h + 1) * LANE], rope(q_lin[:, (MLA_HEADS + h) * LANE:(MLA_HEADS + h + 1) * LANE])]
        ks += [kv_lin[:, 2 * h * LANE:(2 * h + 1) * LANE], k_rope]
        vs += [kv_lin[:, (2 * h + 1) * LANE:(2 * h + 2) * LANE]]
    return qs + ks + vs


def _mla_prep_fwd(proj, pos_col, inv_freq2, gq, gkv, w_uq, w_ukv, tm):
    t = proj.shape[0]
    nq = 2 * MLA_HEADS

    def body(cq_ref, ckv_ref, kr_ref, pos_ref, f_ref, gq_ref, gkv_ref, wq_ref, wkv_ref, q_ref, k_ref, v_ref):
        cos, sin = _rope_tables(pos_ref[...], f_ref[...])
        outs = _mla_prep(cq_ref[...].astype(F32), ckv_ref[...].astype(F32), kr_ref[...].astype(F32), gq_ref[...], gkv_ref[...], wq_ref[...], wkv_ref[...],
                         cos, sin)
        for i in range(nq):
            q_ref[:, pl.ds(i * LANE, LANE)] = outs[i].astype(q_ref.dtype)
            k_ref[:, pl.ds(i * LANE, LANE)] = outs[nq + i].astype(k_ref.dtype)
        for h in range(MLA_HEADS):
            v_ref[:, pl.ds(h * LANE, LANE)] = outs[2 * nq + h].astype(v_ref.dtype)

    row = lambda w, j: pl.BlockSpec((tm, w), functools.partial(lambda i, j: (i, j), j=j))
    return pl.pallas_call(
        body, grid=(t // tm,),
        in_specs=[row(Q_LORA, P_CQ // Q_LORA), row(KV_LORA, P_CKV // KV_LORA), row(LANE, P_KR // LANE),
                  pl.BlockSpec((tm, 1), lambda i: (i, 0)), _full(inv_freq2.shape), _full(gq.shape), _full(gkv.shape),
                  _full(w_uq.shape), _full(w_ukv.shape)],
        out_specs=[row(nq * LANE, 0), row(nq * LANE, 0), row(MLA_VW, 0)],
        out_shape=[jax.ShapeDtypeStruct((t, nq * LANE), BF), jax.ShapeDtypeStruct((t, nq * LANE), BF),
                   jax.ShapeDtypeStruct((t, MLA_VW), BF)],
        name="mla_prep_fwd", compiler_params=pltpu.CompilerParams(dimension_semantics=("arbitrary",)),
    )(proj, proj, proj, pos_col, inv_freq2, gq, gkv, w_uq, w_ukv)


def _mla_prep_bwd(proj, pos_col, inv_freq2, gq, gkv, w_uq, w_ukv, dq, dk, dv, tm):
    t = proj.shape[0]
    nq = 2 * MLA_HEADS

    def body(cq_ref, ckv_ref, kr_ref, pos_ref, f_ref, gq_ref, gkv_ref, wq_ref, wkv_ref, dq_ref, dk_ref, dv_ref,
             dcq_ref, dckv_ref, dkr_ref, dgq_ref, dgkv_ref, dwq_ref, dwkv_ref):
        @pl.when(pl.program_id(0) == 0)
        def _():
            for o in (dgq_ref, dgkv_ref, dwq_ref, dwkv_ref):
                o[...] = jnp.zeros_like(o)

        cos, sin = _rope_tables(pos_ref[...], f_ref[...])
        f = functools.partial(_mla_prep, cos=cos, sin=sin)
        _, vjp = jax.vjp(f, cq_ref[...].astype(F32), ckv_ref[...].astype(F32), kr_ref[...].astype(F32), gq_ref[...], gkv_ref[...], wq_ref[...], wkv_ref[...])
        cts = [dq_ref[:, pl.ds(i * LANE, LANE)] for i in range(nq)]
        cts += [dk_ref[:, pl.ds(i * LANE, LANE)] for i in range(nq)]
        cts += [dv_ref[:, pl.ds(h * LANE, LANE)] for h in range(MLA_HEADS)]
        dcq, dckv, dkr, dgq, dgkv, dwq, dwkv = vjp(cts)
        dcq_ref[...] = dcq.astype(dcq_ref.dtype)
        dckv_ref[...] = dckv.astype(dckv_ref.dtype)
        dkr_ref[...] = dkr.astype(dkr_ref.dtype)
        dgq_ref[...] += dgq
        dgkv_ref[...] += dgkv
        dwq_ref[...] += dwq
        dwkv_ref[...] += dwkv

    row = lambda w, j: pl.BlockSpec((tm, w), functools.partial(lambda i, j: (i, j), j=j))
    return pl.pallas_call(
        body, grid=(t // tm,),
        in_specs=[row(Q_LORA, P_CQ // Q_LORA), row(KV_LORA, P_CKV // KV_LORA), row(LANE, P_KR // LANE),
                  pl.BlockSpec((tm, 1), lambda i: (i, 0)), _full(inv_freq2.shape), _full(gq.shape), _full(gkv.shape),
                  _full(w_uq.shape), _full(w_ukv.shape), row(nq * LANE, 0), row(nq * LANE, 0), row(MLA_VW, 0)],
        out_specs=[row(Q_LORA, 0), row(KV_LORA, 0), row(LANE, 0), _full(gq.shape), _full(gkv.shape),
                   _full(w_uq.shape), _full(w_ukv.shape)],
        out_shape=[jax.ShapeDtypeStruct((t, Q_LORA), BF), jax.ShapeDtypeStruct((t, KV_LORA), BF),
                   jax.ShapeDtypeStruct((t, LANE), BF), jax.ShapeDtypeStruct(gq.shape, F32),
                   jax.ShapeDtypeStruct(gkv.shape, F32), jax.ShapeDtypeStruct(w_uq.shape, F32),
                   jax.ShapeDtypeStruct(w_ukv.shape, F32)],
        name="mla_prep_bwd", compiler_params=pltpu.CompilerParams(dimension_semantics=("arbitrary",)),
    )(proj, proj, proj, pos_col, inv_freq2, gq, gkv, w_uq, w_ukv, dq, dk, dv)


_NEG = -1e30
_LN2 = math.log(2.0)
ATT_CHAINS = 2


def _causal(tq, tk, q0, k0):
    row = q0 + lax.broadcasted_iota(jnp.int32, (tq, tk), 0)
    col = k0 + lax.broadcasted_iota(jnp.int32, (tq, tk), 1)
    return col <= row


def _attn_fwd(q, k, v, tq, tk):
    t = q.shape[0]

    assert tk % tq == 0 or tq % tk == 0
    n_diag = max(1, tq // tk)

    th = tq // ATT_CHAINS

    def body(q_ref, k_ref, v_ref, o_ref, lse_ref):
        i = pl.program_id(1)
        n_full = (i * tq) // tk

        def step(k0, carry, masked):
            out = []
            for c, (m, l, acc) in enumerate(carry):
                kw = min(tk, (c + 1) * th) if masked and tk == tq else tk
                kt = k_ref[pl.ds(k0, kw), :]
                vt = v_ref[pl.ds(k0, kw), :]
                s = _dot(q_ref[pl.ds(c * th, th), :], kt, NT)
                if masked:
                    s = jnp.where(_causal(th, kw, i * tq + c * th, k0), s, _NEG)
                m_new = jnp.maximum(m, jnp.max(s, axis=-1, keepdims=True))
                p = jnp.exp2(s - m_new)
                alpha = jnp.exp2(m - m_new)
                out.append((m_new, alpha * l + jnp.sum(p, axis=-1, keepdims=True), alpha * acc + _dot(p.astype(BF), vt)))
            return tuple(out)

        init = tuple((jnp.full((th, 1), _NEG, F32), jnp.zeros((th, 1), F32), jnp.zeros((th, V_HEAD), F32)) for _ in range(ATT_CHAINS))
        carry = lax.fori_loop(0, n_full, lambda j, c: step(pl.multiple_of(j * tk, tk), c, False), init)
        for dd in range(n_diag):
            carry = step(pl.multiple_of((n_full + dd) * tk, tk), carry, True)
        for c, (m, l, acc) in enumerate(carry):
            o_ref[pl.ds(c * th, th), :] = acc / l
            lse_ref[pl.ds(c * th, th), :] = jnp.broadcast_to(m + jnp.log2(l), (th, LANE))

    return pl.pallas_call(
        body, grid=(MLA_HEADS, t // tq),
        in_specs=[pl.BlockSpec((tq, 2 * LANE), lambda h, i: (i, h)), pl.BlockSpec((t, 2 * LANE), lambda h, i: (0, h)),
                  pl.BlockSpec((t, V_HEAD), lambda h, i: (0, h))],
        out_specs=[pl.BlockSpec((tq, V_HEAD), lambda h, i: (i, h)), pl.BlockSpec((tq, LANE), lambda h, i: (i, h))],
        out_shape=[jax.ShapeDtypeStruct((t, MLA_VW), F32), jax.ShapeDtypeStruct((t, MLA_HEADS * LANE), F32)],
        name="attn_fwd", compiler_params=pltpu.CompilerParams(dimension_semantics=("parallel", "arbitrary")),
    )(q, k, v)


def _attn_bwd(q, k, v, do, lse, delta, tq, tk):
    t = q.shape[0]
    nkt = t // tk
    assert tk % tq == 0

    def body(q_ref, k_ref, v_ref, do_ref, lse_ref, dl_ref, dq_ref, dk_ref, dv_ref):
        j = pl.program_id(1)

        @pl.when(j == 0)
        def _():
            dq_ref[...] = jnp.zeros_like(dq_ref)

        kt = k_ref[...]
        vt = v_ref[...]

        def step(q0, carry, masked, kw=tk):
            dk, dv = carry
            rows = pl.ds(q0, tq)
            qt = q_ref[rows, :]
            dot_ = do_ref[rows, :]
            ktw, vtw = kt[:kw], vt[:kw]
            p = jnp.exp2(_dot(qt, ktw, NT) - lse_ref[rows, pl.ds(0, 1)])
            if masked:
                p = jnp.where(_causal(tq, kw, q0, j * tk), p, 0.0)
            dv_w = _dot(p.astype(BF), dot_, TN)
            ds = (p * (_dot(dot_, vtw, NT) - dl_ref[rows, pl.ds(0, 1)])).astype(BF)
            dk_w = _dot(ds, qt, TN)
            dq_ref[rows, :] += _dot(ds, ktw)
            if kw == tk:
                return dk + dk_w, dv + dv_w
            return (jnp.concatenate([dk[:kw] + dk_w, dk[kw:]], axis=0), jnp.concatenate([dv[:kw] + dv_w, dv[kw:]], axis=0))

        per = tk // tq
        carry = (jnp.zeros((tk, 2 * LANE), F32), jnp.zeros((tk, V_HEAD), F32))
        for dd in range(per):
            carry = step(pl.multiple_of(j * tk + dd * tq, tq), carry, True, kw=(dd + 1) * tq)

        def group(g, c):
            for dd in range(per):
                c = step(pl.multiple_of(g * tk + dd * tq, tq), c, False)
            return c

        dk, dv = lax.fori_loop(j + 1, nkt, group, carry)
        dk_ref[...] = dk * _LN2
        dv_ref[...] = dv

        @pl.when(j == nkt - 1)
        def _():
            dq_ref[...] = dq_ref[...] * _LN2

    return pl.pallas_call(
        body, grid=(MLA_HEADS, nkt),
        in_specs=[pl.BlockSpec((t, 2 * LANE), lambda h, j: (0, h)), pl.BlockSpec((tk, 2 * LANE), lambda h, j: (j, h)),
                  pl.BlockSpec((tk, V_HEAD), lambda h, j: (j, h)), pl.BlockSpec((t, V_HEAD), lambda h, j: (0, h)),
                  pl.BlockSpec((t, LANE), lambda h, j: (0, h)), pl.BlockSpec((t, LANE), lambda h, j: (0, h))],
        out_specs=[pl.BlockSpec((t, 2 * LANE), lambda h, j: (0, h)), pl.BlockSpec((tk, 2 * LANE), lambda h, j: (j, h)),
                   pl.BlockSpec((tk, V_HEAD), lambda h, j: (j, h))],
        out_shape=[jax.ShapeDtypeStruct((t, MLA_HEADS * 2 * LANE), F32), jax.ShapeDtypeStruct((t, MLA_HEADS * 2 * LANE), F32),
                   jax.ShapeDtypeStruct((t, MLA_VW), F32)],
        name="attn_bwd", compiler_params=pltpu.CompilerParams(dimension_semantics=("parallel", "arbitrary")),
    )(q, k, v, do, lse, delta)


def _adam_update(w, g, m, v):
    mm = ADAM_B1 * m + (1.0 - ADAM_B1) * g
    vv = ADAM_B2 * v + (1.0 - ADAM_B2) * jnp.square(g)
    m_hat = mm / (1.0 - ADAM_B1 ** ADAM_STEP)
    v_hat = vv / (1.0 - ADAM_B2 ** ADAM_STEP)
    return -ADAM_LR * (m_hat / (jnp.sqrt(v_hat) + ADAM_EPS) + ADAM_WD * w), mm, vv


def _adamw(w, g, m, v, name):
    r, c = w.shape
    tr = max([r // s for s in range(1, r // 8 + 1) if r % s == 0 and (r // s) % 8 == 0 and r // s <= 256] or [r])
    slots = g.ndim == 3

    def body(w_ref, g_ref, m_ref, v_ref, g_out, d_ref, nm_ref, nv_ref):
        if slots:
            gg = g_ref[0].astype(F32)
            for s in range(1, N_DEV):
                gg = gg + g_ref[s].astype(F32)
        else:
            gg = g_ref[...]
        g_out[...] = gg
        d_ref[...], nm_ref[...], nv_ref[...] = _adam_update(w_ref[...], gg, m_ref[...], v_ref[...])

    spec = pl.BlockSpec((tr, c), lambda i: (i, 0))
    g_spec = pl.BlockSpec((N_DEV, tr, c), lambda i: (0, i, 0)) if slots else spec
    return pl.pallas_call(
        body, grid=(r // tr,), in_specs=[spec, g_spec, spec, spec], out_specs=[spec] * 4,
        out_shape=[jax.ShapeDtypeStruct((r, c), F32)] * 4, name=name,
        compiler_params=pltpu.CompilerParams(dimension_semantics=("arbitrary",)),
    )(w, g, m, v)


def _adamw_many(ws, gs, ms, vs, name):
    n = len(ws)

    def body(*refs):
        for i in range(n):
            w_ref, g_ref, m_ref, v_ref = (refs[j * n + i] for j in range(4))
            d_ref, nm_ref, nv_ref = (refs[(4 + j) * n + i] for j in range(3))
            d_ref[...], nm_ref[...], nv_ref[...] = _adam_update(w_ref[...], g_ref[...], m_ref[...], v_ref[...])

    shapes = [jax.ShapeDtypeStruct(w.shape, F32) for w in ws]
    outs = pl.pallas_call(body, out_shape=shapes * 3, name=name)(*ws, *gs, *ms, *vs)
    return outs[:n], outs[n:2 * n], outs[2 * n:]


def _cast_bf16(xs, name, after=None):
    n = len(xs)
    extra = [] if after is None else [after]

    def body(*refs):
        outs = refs[n + len(extra):]
        for i in range(n):
            outs[i][...] = refs[i][...].astype(BF)

    vmem = pl.BlockSpec(memory_space=pltpu.VMEM)
    return pl.pallas_call(
        body, out_shape=[jax.ShapeDtypeStruct(x.shape, BF) for x in xs], name=name,
        in_specs=[vmem] * n + [pl.BlockSpec(memory_space=pl.ANY)] * len(extra), out_specs=[vmem] * n)(*xs, *extra)


def _pad_rows(a, n):
    return jnp.pad(a, ((0, n - a.shape[0]), (0, 0)))


def _w_in_to_padded(wt):
    s_ba = P_CQ
    s_cq = s_ba + 2 * DN_HEADS
    s_kr = s_cq + Q_LORA + KV_LORA
    return jnp.concatenate([wt[:s_ba], wt[s_cq:s_kr], _pad_rows(wt[s_ba:s_cq], LANE), _pad_rows(wt[s_kr:], LANE)], axis=0)


def _w_in_from_padded(wt):
    return jnp.concatenate([wt[:P_CQ], wt[P_BA:P_BA + 2 * DN_HEADS], wt[P_CQ:P_BA], wt[P_KR:P_KR + QK_ROPE]], axis=0)


def _w_uq_to_padded(wt):
    w3 = wt.reshape(MLA_HEADS, QK_NOPE + QK_ROPE, Q_LORA)
    nope = w3[:, :QK_NOPE].reshape(MLA_HEADS * QK_NOPE, Q_LORA)
    rope = jnp.pad(w3[:, QK_NOPE:], ((0, 0), (0, LANE - QK_ROPE), (0, 0))).reshape(MLA_HEADS * LANE, Q_LORA)
    return jnp.concatenate([nope, rope], axis=0)


def _w_uq_from_padded(wt):
    nope = wt[:MLA_HEADS * QK_NOPE].reshape(MLA_HEADS, QK_NOPE, Q_LORA)
    rope = wt[MLA_HEADS * QK_NOPE:].reshape(MLA_HEADS, LANE, Q_LORA)[:, :QK_ROPE]
    return jnp.concatenate([nope, rope], axis=1).reshape(MLA_HEADS * (QK_NOPE + QK_ROPE), Q_LORA)


def _pack(pieces, width, row_mult):
    flat = jnp.concatenate([p.reshape(-1) for p in pieces])
    n = flat.shape[0]
    rows = -(-n // (width * row_mult)) * row_mult
    return jnp.pad(flat, (0, rows * width - n)).reshape(rows, width)


def _unpack(flat, shapes):
    out, o = [], 0
    for s in shapes:
        n = math.prod(s)
        out.append(flat[o:o + n].reshape(s))
        o += n
    return out


def kernel(x, c, positions, w_ada, b_ada, w_in, conv_w, a_log, dt_bias, dn_norm_g, q_norm_g, w_uq, kv_norm_g, w_ukv, w_o, ln1_g, ln1_b, w_gate, w_up, w_down, ln2_g, ln2_b, loss_target, m_w_ada, m_b_ada, m_w_in, m_conv_w, m_a_log, m_dt_bias, m_dn_norm_g, m_q_norm_g, m_w_uq, m_kv_norm_g, m_w_ukv, m_w_o, m_ln1_g, m_ln1_b, m_w_gate, m_w_up, m_w_down, m_ln2_g, m_ln2_b, v_w_ada, v_b_ada, v_w_in, v_conv_w, v_a_log, v_dt_bias, v_dn_norm_g, v_q_norm_g, v_w_uq, v_kv_norm_g, v_w_ukv, v_w_o, v_ln1_g, v_ln1_b, v_w_gate, v_w_up, v_w_down, v_ln2_g, v_ln2_b):
    me = 4 * lax.axis_index("x") + 2 * lax.axis_index("y") + lax.axis_index("c")
    t, d = x.shape[1], x.shape[2]
    ada_n = w_ada.shape[2]

    tr = lambda w: w[0].T
    rows = lambda a: a.reshape(-1, a.shape[2])
    (in_shard,) = _cast_bf16([tr(w_in)], "cast_w_in")
    cw = conv_w.shape[3]
    a_in, c_all, conv_all = _gather_by_chip([in_shard, c, conv_w[0, :, 0, :]], "gather_w_in_and_small")
    c_all = c_all.reshape(N_DEV, d)
    conv_full = conv_all.transpose(1, 0, 2).reshape(CONV_K, N_DEV * cw)
    conv_w8 = jnp.pad(conv_full, ((0, 8 - CONV_K), (0, 0)))

    b_ada_mine = lax.dynamic_slice(b_ada, (0, me * ada_n), (1, ada_n))
    mod_cols = _mod_fwd(c_all, w_ada[0], b_ada_mine)
    (mod_all,) = _exchange([mod_cols.reshape(N_DEV, 1, ada_n)], "scatter_mod", scatter=True)
    mod = mod_all.reshape(1, N_DEV * ada_n)

    later = _cast_bf16([tr(w_uq), tr(w_ukv), w_o[0], tr(w_gate), tr(w_up), w_down[0]], "cast_weights", after=mod)
    mixer_gather, token_a = _exchange_start(later[:3], "gather_mixer_weights_start", scatter=False)
    ffn_gather, token_b = _exchange_start(later[3:], "gather_ffn_weights_start", scatter=False)
    mod = mod + (token_a + token_b)
    w_in_t = _w_in_to_padded(rows(a_in))

    def mixer_weights(after):
        a_uq, a_ukv, a_o = _exchange_wait(mixer_gather, after, "gather_mixer_weights_wait", scatter=False)
        return _w_uq_to_padded(rows(a_uq)), rows(a_ukv), rows(a_o)

    def ffn_weights(after):
        a_gate, a_up, a_down = _exchange_wait(ffn_gather, after, "gather_ffn_weights_wait", scatter=False)
        return rows(a_gate), rows(a_up), rows(a_down)

    def by_dest(g):
        return g.reshape(N_DEV, -1, g.shape[1])

    scatters = {}

    def grads_ready(tag, *g):
        if tag == "ffn":
            pieces = [by_dest(a) for a in g]
        elif tag == "mixer":
            g_w_o, g_w_uq_t, g_w_ukv_t = g
            pieces = [by_dest(g_w_o), by_dest(_w_uq_from_padded(g_w_uq_t).astype(BF)), by_dest(g_w_ukv_t.astype(BF))]
        else:
            pieces = [by_dest(_w_in_from_padded(g[0]))]
        scatters[tag], token = _exchange_start(pieces, "scatter_%s_grads_start" % tag, scatter=True)
        return token

    loc = _local_step(x[0], loss_target[0], positions[0], mod, w_in_t, mixer_weights, ffn_weights, grads_ready,
                      conv_w8, a_log, dt_bias, dn_norm_g, q_norm_g, kv_norm_g, ln1_g, ln1_b, ln2_g, ln2_b)
    grad_x, loss_acc, dmod, d_conv8, d_al8, d_dt8, d_dn_g, d_q_g, d_kv_g, d_ln1_g, d_ln1_b, d_ln2_g, d_ln2_b = loc

    small_shapes = [(6 * d,), (CONV_K, N_DEV * cw), (DN_HEADS,), (DN_HEADS,), (DN_DV,), (Q_LORA,), (KV_LORA,), (d,), (d,), (d,), (d,), (1,)]
    gsmall = _pack([dmod, d_conv8[:CONV_K], d_al8[0, :DN_HEADS], d_dt8[0, :DN_HEADS], d_dn_g, d_q_g, d_kv_g,
                    d_ln1_g, d_ln1_b, d_ln2_g, d_ln2_b, loss_acc[0, :1]], LANE, 8)
    (gsmall_all,) = _exchange([gsmall], "gather_small_grads", scatter=False)
    dmod_all = gsmall_all.reshape(N_DEV, -1)[:, :6 * d]
    tot = _unpack(_sum_slots(gsmall_all, "sum_small_grads").reshape(-1), small_shapes)
    g_b_ada, g_conv_full, g_a_log, g_dt_bias, g_dn_g, g_q_g, g_kv_g, g_ln1_g, g_ln1_b, g_ln2_g, g_ln2_b, loss1 = tot
    loss = loss1.reshape(())
    g_conv_w = lax.dynamic_slice(g_conv_full, (0, me * cw), (CONV_K, cw))
    g_w_ada = _mod_bwd(c_all.T, lax.dynamic_slice(dmod_all, (0, me * ada_n), (N_DEV, ada_n)))

    grads = {"w_ada": g_w_ada[None], "b_ada": g_b_ada[None], "conv_w": g_conv_w[None, :, None, :],
             "a_log": g_a_log[None], "dt_bias": g_dt_bias[None], "dn_norm_g": g_dn_g[None], "q_norm_g": g_q_g[None],
             "kv_norm_g": g_kv_g[None], "ln1_g": g_ln1_g[None], "ln1_b": g_ln1_b[None], "ln2_g": g_ln2_g[None], "ln2_b": g_ln2_b[None]}
    weights = dict(w_ada=w_ada, b_ada=b_ada, w_in=w_in, conv_w=conv_w, a_log=a_log, dt_bias=dt_bias, dn_norm_g=dn_norm_g,
                   q_norm_g=q_norm_g, w_uq=w_uq, kv_norm_g=kv_norm_g, w_ukv=w_ukv, w_o=w_o, ln1_g=ln1_g, ln1_b=ln1_b,
                   w_gate=w_gate, w_up=w_up, w_down=w_down, ln2_g=ln2_g, ln2_b=ln2_b)
    ms = dict(w_ada=m_w_ada, b_ada=m_b_ada, w_in=m_w_in, conv_w=m_conv_w, a_log=m_a_log, dt_bias=m_dt_bias,
              dn_norm_g=m_dn_norm_g, q_norm_g=m_q_norm_g, w_uq=m_w_uq, kv_norm_g=m_kv_norm_g, w_ukv=m_w_ukv, w_o=m_w_o,
              ln1_g=m_ln1_g, ln1_b=m_ln1_b, w_gate=m_w_gate, w_up=m_w_up, w_down=m_w_down, ln2_g=m_ln2_g, ln2_b=m_ln2_b)
    vs = dict(w_ada=v_w_ada, b_ada=v_b_ada, w_in=v_w_in, conv_w=v_conv_w, a_log=v_a_log, dt_bias=v_dt_bias,
              dn_norm_g=v_dn_norm_g, q_norm_g=v_q_norm_g, w_uq=v_w_uq, kv_norm_g=v_kv_norm_g, w_ukv=v_w_ukv, w_o=v_w_o,
              ln1_g=v_ln1_g, ln1_b=v_ln1_b, w_gate=v_w_gate, w_up=v_w_up, w_down=v_w_down, ln2_g=v_ln2_g, ln2_b=v_ln2_b)
    names = list(weights)
    big = ("w_ada", "w_gate", "w_up", "w_down", "w_o", "w_uq", "w_ukv", "w_in")
    waits = {"w_gate": ("ffn", ("w_gate", "w_up", "w_down")), "w_o": ("mixer", ("w_o", "w_uq", "w_ukv")), "w_in": ("in", ("w_in",))}
    delta_w, new_m, new_v, slots = {}, {}, {}, {}
    last = g_w_ada
    for n in big:
        if n == "w_in":
            rest = [r for r in names if r not in big]
            flat2 = lambda a: a.reshape(-1, a.shape[-1])
            outs = _adamw_many(*[[flat2(src[r]) for r in rest] for src in (weights, grads, ms, vs)], "adamw_small")
            for dst, o in zip((delta_w, new_m, new_v), outs):
                for r, a in zip(rest, o):
                    dst[r] = a.reshape(weights[r].shape)
            last = outs[0][0]
        transposed = n in ("w_in", "w_uq", "w_ukv", "w_gate", "w_up")
        two = (lambda a: a[0].T) if transposed else (lambda a: a[0])
        back = (lambda a: a.T[None]) if transposed else (lambda a: a[None])
        if n in waits:
            tag, members = waits[n]
            slots.update(zip(members, _exchange_wait(scatters[tag], last, "scatter_%s_grads_wait" % tag, scatter=True)))
        g_in = slots[n] if n in slots else two(grads[n])
        gr, dlt, nm, nv = _adamw(two(weights[n]), g_in, two(ms[n]), two(vs[n]), "adamw_" + n)
        grads[n], delta_w[n], new_m[n], new_v[n] = back(gr), back(dlt), back(nm), back(nv)
        last = nv

    return (loss, grad_x[None], *[grads[n] for n in names], *[delta_w[n] for n in names],
            *[new_m[n] for n in names], *[new_v[n] for n in names])


def _local_step(xs, tgt, pos, mod, w_in_t, mixer_weights, ffn_weights, grads_ready, conv_w8,
                a_log, dt_bias, dn_norm_g, q_norm_g, kv_norm_g, ln1_g, ln1_b, ln2_g, ln2_b):
    t, d = xs.shape
    sh_m, sc_m, gt_m, sh_f, sc_f, gt_f = [mod[:, i * d:(i + 1) * d] for i in range(6)]
    pos_col = pos.astype(F32).reshape(t, 1)
    inv_freq = 1.0 / (ROPE_THETA ** (jnp.arange(0, QK_ROPE, 2, dtype=F32) / QK_ROPE))
    inv_freq2 = jnp.pad(jnp.concatenate([inv_freq, inv_freq]), (0, LANE - QK_ROPE)).reshape(1, LANE)
    al8 = jnp.pad(a_log, ((0, 7), (0, LANE - DN_HEADS)))
    dt8 = jnp.pad(dt_bias, ((0, 7), (0, LANE - DN_HEADS)))

    tm = min(512, t)
    tq = min(256, t)
    tk = min(512, t)

    (h1,) = _rowwise("modulate_in", lambda xx, sc, sh: xx * (1.0 + sc) + sh, [xs], [sc_m, sh_m], [(d, BF)], [], tm)
    proj = _matmul(h1, w_in_t, "nt", "in_proj", BF)
    ba_raw = _matmul(h1, w_in_t[P_BA:P_BA + LANE], "nt", "in_proj_decay")
    qkv = _conv_fwd(proj, conv_w8, min(256, t))
    gdn_tm = min(512, t)
    intra, inverses = _gdn_intra_fwd(qkv, ba_raw, al8, dt8, gdn_tm)
    o_dn, states = _gdn_scan_fwd(intra, gdn_tm)
    w_uq_t, w_ukv_t, w_o_f = mixer_weights(states)
    qc, kc, vc = _mla_prep_fwd(proj, pos_col, inv_freq2, q_norm_g, kv_norm_g, w_uq_t, w_ukv_t, tm)
    o_mla, lse = _attn_fwd(qc, kc, vc, min(1024, t), min(1024, t))

    def mix_in(o, z, om, g):
        return jnp.concatenate(_gdn_out(o, z.astype(F32), g) + [om], axis=1)

    (mixin,) = _rowwise("mixer_out", mix_in, [o_dn, (proj, DN_VW, P_Z // DN_VW), o_mla], [dn_norm_g], [(2 * DN_VW, BF)], [], tm)
    mix = _matmul(mixin, w_o_f, "nn", "out_proj", BF)

    def block1(xx, mx, gt, g1, b1, sc, sh):
        x1 = _layernorm(DEEPNORM_ALPHA * xx + gt * mx, g1, b1)
        return x1, x1 * (1.0 + sc) + sh

    x1, h2 = _rowwise("norm1_modulate", block1, [xs, mix], [gt_m, ln1_g, ln1_b, sc_f, sh_f], [(d, F32), (d, BF)], [], tm)
    w_gate_f, w_up_f, w_down_f = ffn_weights(h2)
    act, act_dg, act_du = _ffn_in(h2, w_gate_f, w_up_f)
    ff = _matmul(act, w_down_f, "nn", "ffn_out", BF)

    def tail_loss(x1_, ff_, gt, g2, b2, tg):
        y = _layernorm(DEEPNORM_ALPHA * x1_ + gt * ff_, g2, b2)
        return 0.5 * jnp.sum(jnp.mean(jnp.square(y - tg), axis=-1))

    def tail(x1_, ff_, tg, gt, g2, b2):
        loss, (dx1, dff, dgt, dg2, db2) = jax.value_and_grad(tail_loss, argnums=(0, 1, 2, 3, 4))(x1_, ff_, gt, g2, b2, tg)
        return dx1, dff, jnp.full((1, LANE), loss, F32), dgt, dg2, db2

    dx1_a, dff, loss_acc, d_gt_f, d_ln2_g, d_ln2_b = _rowwise(
        "norm2_loss", tail, [x1, ff, tgt], [gt_f, ln2_g, ln2_b], [(d, BF), (d, BF)], [(1, LANE), (1, d), (1, d), (1, d)], tm)

    g_w_down = _matmul(act, dff, "tn", "d_w_down", BF)
    dgate, dup = _ffn_act_bwd(dff, w_down_f, act_dg, act_du)
    g_w_gate = _matmul(dgate, h2, "tn", "d_w_gate", BF)
    g_w_up = _matmul(dup, h2, "tn", "d_w_up", BF)
    token = grads_ready("ffn", g_w_gate, g_w_up, g_w_down)
    dh2 = _matmul2_nn(dgate, w_gate_f, dup, w_up_f, "d_ffn_in", BF)

    def block1_bwd(xx, mx, dx1_, dh2_, gt, g1, b1, sc, sh):
        _, vjp = jax.vjp(block1, xx, mx, gt, g1, b1, sc, sh)
        dxx, dmx, dgt, dg1, db1, dsc, dsh = vjp((dx1_.astype(F32), dh2_.astype(F32)))
        return dxx, dmx, dgt, dg1, db1, dsc, dsh

    dx_a, dmix, d_gt_m, d_ln1_g, d_ln1_b, d_sc_f, d_sh_f = _rowwise(
        "norm1_modulate_bwd", block1_bwd, [xs, mix, dx1_a, dh2], [gt_m + token, ln1_g, ln1_b, sc_f, sh_f],
        [(d, F32), (d, BF)], [(1, d)] * 5, min(256, t))

    dmixin = _matmul(dmix, w_o_f, "nt", "d_mixer_out", BF)
    g_w_o = _matmul(mixin, dmix, "tn", "d_w_o", BF)

    def mixer_bwd(o, z, om, dmi, g):
        _, vjp = jax.vjp(lambda o_, z_, g_: jnp.concatenate(_gdn_out(o_, z_, g_), axis=1), o, z.astype(F32), g)
        do_, dz_, dg_ = vjp(dmi[:, :DN_VW].astype(F32))
        dom = dmi[:, DN_VW:]
        delta = [jnp.broadcast_to(jnp.sum(dom[:, h * V_HEAD:(h + 1) * V_HEAD] * om[:, h * V_HEAD:(h + 1) * V_HEAD], axis=-1, keepdims=True), (o.shape[0], LANE))
                 for h in range(MLA_HEADS)]
        return do_, dz_, dom, jnp.concatenate(delta, axis=1), dg_

    do_dn, dz, do_mla, delta, d_dn_g = _rowwise(
        "mixer_out_bwd", mixer_bwd, [o_dn, (proj, DN_VW, P_Z // DN_VW), o_mla, dmixin], [dn_norm_g],
        [(DN_VW, F32), (DN_VW, BF), (MLA_VW, BF), (MLA_HEADS * LANE, F32)], [(1, DN_DV)], tm)

    dqc, dkc, dvc = _attn_bwd(qc, kc, vc, do_mla, lse, delta, min(512, t), min(1024, t))
    dcq, dckv, dkr, d_q_g, d_kv_g, g_w_uq_t, g_w_ukv_t = _mla_prep_bwd(
        proj, pos_col, inv_freq2, q_norm_g, kv_norm_g, w_uq_t, w_ukv_t, dqc, dkc, dvc, min(256, t))

    token = grads_ready("mixer", g_w_o, g_w_uq_t, g_w_ukv_t)

    d_intra = _gdn_scan_bwd(intra, states, do_dn, gdn_tm)
    dqkv_act, dba, d_al8, d_dt8 = _gdn_intra_bwd(qkv, ba_raw, al8 + token, dt8, inverses, d_intra, min(256, t))
    dqkv_pre, d_conv8 = _conv_bwd(proj, conv_w8, dqkv_act, min(256, t))

    dproj = jnp.concatenate([dqkv_pre, dz, dcq, dckv, dba, dkr], axis=1)
    dh1 = _matmul(dproj, w_in_t, "nn", "d_in_proj", BF)
    g_w_in_t = _matmul(dproj, h1, "tn", "d_w_in", BF)
    token = grads_ready("in", g_w_in_t)

    def modulate_bwd(xx, dh, dxa, sc):
        dh = dh.astype(F32)
        return dh * (1.0 + sc) + dxa, jnp.sum(dh * xx, axis=0, keepdims=True), jnp.sum(dh, axis=0, keepdims=True)

    grad_x, d_sc_m, d_sh_m = _rowwise("modulate_in_bwd", modulate_bwd, [xs, dh1, dx_a], [sc_m + token], [(d, F32)], [(1, d), (1, d)], tm)
    dmod = jnp.concatenate([d_sh_m, d_sc_m, d_gt_m, d_sh_f, d_sc_f, d_gt_f], axis=1)
    return grad_x, loss_acc, dmod, d_conv8, d_al8, d_dt8, d_dn_g, d_q_g, d_kv_g, d_ln1_g, d_ln1_b, d_ln2_g, d_ln2_b
```

```python
import functools
import math

import jax
import jax.numpy as jnp
from jax import lax
from jax.experimental import pallas as pl
from jax.experimental.pallas import tpu as pltpu

F32 = jnp.float32
BF = jnp.bfloat16
HI = lax.Precision.HIGHEST

N_DEV = 8
DN_HEADS = 4
DN_DK = 128
DN_DV = 128
CONV_K = 4
CHUNK = 64
MLA_HEADS = 4
QK_NOPE = 128
QK_ROPE = 64
V_HEAD = 128
Q_LORA = 512
KV_LORA = 256
ROPE_THETA = 10000.0
DEPTH = 1
DEEPNORM_ALPHA = (2.0 * DEPTH) ** 0.25
LANE = 128
CONV_HALO = 8
GL_ROWS = 8
CONV_ROWS, CONV_COLS = 64, 256

DN_QK = DN_HEADS * DN_DK
DN_VW = DN_HEADS * DN_DV
DN_CONV_CH = 2 * DN_QK + DN_VW
MLA_VW = MLA_HEADS * V_HEAD
P_Z = DN_CONV_CH
P_CQ = P_Z + DN_VW
P_CKV = P_CQ + Q_LORA
P_BA = P_CKV + KV_LORA
P_KR = P_BA + LANE
N_INP = P_KR + LANE

ADAM_LR = 0.001
ADAM_B1 = 0.9
ADAM_B2 = 0.999
ADAM_EPS = 1e-08
ADAM_WD = 0.01
ADAM_STEP = 10

NN = (((1,), (0,)), ((), ()))
NT = (((1,), (1,)), ((), ()))
TN = (((0,), (0,)), ((), ()))


def _pick(n, prefs):
    for p in prefs:
        if n % p == 0:
            return p
    return n


def _full(shape):
    return pl.BlockSpec(shape, lambda *_: (0,) * len(shape))


def _dot(a, b, dims=NN):
    return lax.dot_general(a, b, dims, preferred_element_type=F32)


def _doth(a, b, dims=NN):
    return lax.dot_general(a, b, dims, precision=HI, preferred_element_type=F32)


@jax.custom_vjp
def _mmb(a, b):
    return _dot(a.astype(BF), b.astype(BF), NN)


def _mmb_fwd(a, b):
    return _mmb(a, b), (a, b)


def _mmb_bwd(res, g):
    a, b = res
    gb = g.astype(BF)
    return (_dot(gb, b.astype(BF), NT).astype(a.dtype), _dot(a.astype(BF), gb, TN).astype(b.dtype))


_mmb.defvjp(_mmb_fwd, _mmb_bwd)


@jax.custom_vjp
def _mmb_nt(a, b):
    return _dot(a.astype(BF), b.astype(BF), NT)


def _mmb_nt_fwd(a, b):
    return _mmb_nt(a, b), (a, b)


def _mmb_nt_bwd(res, g):
    a, b = res
    gb = g.astype(BF)
    return (_dot(gb, b.astype(BF), NN).astype(a.dtype), _dot(gb, a.astype(BF), TN).astype(b.dtype))


_mmb_nt.defvjp(_mmb_nt_fwd, _mmb_nt_bwd)


def _sigmoid(x):
    return 0.5 * (jnp.tanh(0.5 * x) + 1.0)


def _silu(x):
    return x * _sigmoid(x)


def _softplus(x):
    return jnp.maximum(x, 0.0) + jnp.log(1.0 + jnp.exp(-jnp.abs(x)))


def _layernorm(x, g, b, eps=1e-5):
    mu = jnp.mean(x, axis=-1, keepdims=True)
    xc = x - mu
    var = jnp.mean(xc * xc, axis=-1, keepdims=True)
    return xc * lax.rsqrt(var + eps) * g + b


def _rmsnorm(x, g, eps=1e-6):
    return x * lax.rsqrt(jnp.mean(x * x, axis=-1, keepdims=True) + eps) * g


def _l2norm(x, eps=1e-6):
    return x * lax.rsqrt(jnp.sum(x * x, axis=-1, keepdims=True) + eps)


def _rowwise(name, fn, rows, vecs, out_rows, out_accs, tm):
    rows = [r if isinstance(r, tuple) else (r, r.shape[1], 0) for r in rows]
    t = rows[0][0].shape[0]
    tm = min(tm, t)
    assert t % tm == 0
    nr, nv, no = len(rows), len(vecs), len(out_rows)

    def body(*refs):
        ins = [r[...] for r in refs[:nr + nv]]
        outs = fn(*ins)
        outs = outs if isinstance(outs, (tuple, list)) else (outs,)
        o_rows = refs[nr + nv:nr + nv + no]
        o_accs = refs[nr + nv + no:]
        for o, val in zip(o_rows, outs[:no]):
            o[...] = val.astype(o.dtype)
        if o_accs:
            @pl.when(pl.program_id(0) == 0)
            def _():
                for o in o_accs:
                    o[...] = jnp.zeros_like(o)
            for o, val in zip(o_accs, outs[no:]):
                o[...] += val

    in_specs = [pl.BlockSpec((tm, w), functools.partial(lambda i, j: (i, j), j=j)) for (_, w, j) in rows]
    in_specs += [_full(v.shape) for v in vecs]
    out_specs = [pl.BlockSpec((tm, w), lambda i: (i, 0)) for (w, _) in out_rows]
    out_specs += [_full(s) for s in out_accs]
    out_shape = [jax.ShapeDtypeStruct((t, w), d) for (w, d) in out_rows]
    out_shape += [jax.ShapeDtypeStruct(s, F32) for s in out_accs]
    res = pl.pallas_call(
        body, grid=(t // tm,), in_specs=in_specs, out_specs=out_specs, out_shape=out_shape, name=name,
        compiler_params=pltpu.CompilerParams(dimension_semantics=("arbitrary",)),
    )(*[r[0] for r in rows], *vecs)
    return res


def _matmul(a, b, mode, name, out_dtype=F32):
    if mode == "nn":
        (m, k), n = a.shape, b.shape[1]
    elif mode == "nt":
        (m, k), n = a.shape, b.shape[0]
    else:
        (k, m), n = a.shape, b.shape[1]
    tm, tn, tk = _matmul_tiles(m, n, k, a.dtype.itemsize, b.dtype.itemsize, jnp.dtype(out_dtype).itemsize)
    nk = k // tk
    dims = {"nn": NN, "nt": NT, "tn": TN}[mode]

    def body(a_ref, b_ref, o_ref, *acc):
        part = _dot(a_ref[...].astype(BF), b_ref[...].astype(BF), dims)
        if nk == 1:
            o_ref[...] = part.astype(o_ref.dtype)
            return
        (acc_ref,) = acc
        kk = pl.program_id(2)

        @pl.when(kk == 0)
        def _():
            acc_ref[...] = part

        @pl.when(kk > 0)
        def _():
            acc_ref[...] += part

        @pl.when(kk == nk - 1)
        def _():
            o_ref[...] = acc_ref[...].astype(o_ref.dtype)

    a_spec = pl.BlockSpec((tk, tm), lambda i, j, kk: (kk, i)) if mode == "tn" else pl.BlockSpec((tm, tk), lambda i, j, kk: (i, kk))
    b_spec = pl.BlockSpec((tn, tk), lambda i, j, kk: (j, kk)) if mode == "nt" else pl.BlockSpec((tk, tn), lambda i, j, kk: (kk, j))
    return pl.pallas_call(
        body, grid=(m // tm, n // tn, nk), in_specs=[a_spec, b_spec],
        out_specs=pl.BlockSpec((tm, tn), lambda i, j, kk: (i, j)),
        out_shape=jax.ShapeDtypeStruct((m, n), out_dtype),
        scratch_shapes=[pltpu.VMEM((tm, tn), F32)] if nk > 1 else [], name=name,
        compiler_params=pltpu.CompilerParams(dimension_semantics=("parallel", "parallel", "arbitrary")),
    )(a, b)


def _lane_tile(n, cap):
    return max([n // s for s in range(1, n // LANE + 1) if n % s == 0 and (n // s) % LANE == 0 and n // s <= cap] or [n])


def _ffn_in(h, w_gate, w_up):
    m, k = h.shape
    f = w_gate.shape[0]
    tm, tn = _pick(m, (1024, 512, 256, 128)), _lane_tile(f, 1408)

    def body(h_ref, wg_ref, wu_ref, act_ref, dg_ref, du_ref):
        hh = h_ref[...]
        g = _dot(hh, wg_ref[...], NT)
        u = _dot(hh, wu_ref[...], NT)
        sg = _sigmoid(g)
        silu_g = g * sg
        act_ref[...] = (silu_g * u).astype(act_ref.dtype)
        dg_ref[...] = (u * (sg + silu_g * (1.0 - sg))).astype(dg_ref.dtype)
        du_ref[...] = silu_g.astype(du_ref.dtype)

    w_spec = pl.BlockSpec((tn, k), lambda i, j: (j, 0))
    o_spec = pl.BlockSpec((tm, tn), lambda i, j: (i, j))
    return pl.pallas_call(
        body, grid=(m // tm, f // tn), in_specs=[pl.BlockSpec((tm, k), lambda i, j: (i, 0)), w_spec, w_spec],
        out_specs=[o_spec] * 3, out_shape=[jax.ShapeDtypeStruct((m, f), BF)] * 3, name="ffn_in",
        compiler_params=pltpu.CompilerParams(dimension_semantics=("parallel", "parallel")),
    )(h, w_gate, w_up)


def _ffn_act_bwd(dff, w_down, act_dg, act_du):
    m, k = dff.shape
    f = w_down.shape[0]
    tm, tn = _pick(m, (1024, 512, 256, 128)), _lane_tile(f, 1408)

    def body(d_ref, w_ref, fg_ref, fu_ref, dg_ref, du_ref):
        da = _dot(d_ref[...], w_ref[...], NT)
        dg_ref[...] = (da * fg_ref[...].astype(F32)).astype(dg_ref.dtype)
        du_ref[...] = (da * fu_ref[...].astype(F32)).astype(du_ref.dtype)

    o_spec = pl.BlockSpec((tm, tn), lambda i, j: (i, j))
    return pl.pallas_call(
        body, grid=(m // tm, f // tn),
        in_specs=[pl.BlockSpec((tm, k), lambda i, j: (i, 0)), pl.BlockSpec((tn, k), lambda i, j: (j, 0)), o_spec, o_spec],
        out_specs=[o_spec] * 2, out_shape=[jax.ShapeDtypeStruct((m, f), BF)] * 2, name="d_ffn_act",
        compiler_params=pltpu.CompilerParams(dimension_semantics=("parallel", "parallel")),
    )(dff, w_down, act_dg, act_du)


def _matmul2_nn(a1, b1, a2, b2, name, out_dtype=F32):
    m, k = a1.shape
    n = b1.shape[1]
    tm, tn = _pick(m, (1024, 512, 256, 128)), _pick(n, (512, 256, 128))

    def body(a1_ref, b1_ref, a2_ref, b2_ref, o_ref):
        o_ref[...] = (_dot(a1_ref[...], b1_ref[...]) + _dot(a2_ref[...], b2_ref[...])).astype(o_ref.dtype)

    a_spec = pl.BlockSpec((tm, k), lambda i, j: (i, 0))
    b_spec = pl.BlockSpec((k, tn), lambda i, j: (0, j))
    return pl.pallas_call(
        body, grid=(m // tm, n // tn), in_specs=[a_spec, b_spec, a_spec, b_spec],
        out_specs=pl.BlockSpec((tm, tn), lambda i, j: (i, j)), out_shape=jax.ShapeDtypeStruct((m, n), out_dtype), name=name,
        compiler_params=pltpu.CompilerParams(dimension_semantics=("parallel", "parallel")),
    )(a1, b1, a2, b2)


MATMUL_VMEM_BUDGET = 28 * 1024 * 1024


def _matmul_tiles(m, n, k, a_bytes, b_bytes, o_bytes):
    def divisors(x, cap):
        return sorted({x // s for s in range(1, 65) if x % s == 0 and (x // s) % LANE == 0 and x // s <= cap}, reverse=True) or [x]

    for tk in divisors(k, k):
        best = None
        for tm in divisors(m, 1024):
            for tn in divisors(n, 2048):
                need = 2 * (tm * tk * a_bytes + tk * tn * b_bytes + tm * tn * o_bytes) + (tm * tn * 4 if tk < k else 0)
                if need <= MATMUL_VMEM_BUDGET and tm * tn >= 512 * 512 and (best is None or tm * tn > best[0] * best[1]):
                    best = (tm, tn)
        if best:
            return best[0], best[1], tk
    return _pick(m, (512, 256, 128)), _pick(n, (512, 256, 128)), _pick(k, (512, 256, 128))


def _exchange(xs, name, scatter):
    n = len(xs)
    npeer = N_DEV - 1

    def body(*refs):
        x_refs, o_refs = refs[:n], refs[n:2 * n]
        send_sems, recv_sems, local_sems = refs[2 * n:]
        mx, my, mc = lax.axis_index("x"), lax.axis_index("y"), lax.axis_index("c")
        me = 4 * mx + 2 * my + mc
        src_me = [x.at[me] if scatter else x for x in x_refs]
        mine = [pltpu.make_async_copy(src_me[a], o_refs[a].at[me], local_sems.at[a]) for a in range(n)]
        for cp in mine:
            cp.start()
        copies = []
        for k in range(1, N_DEV):
            px, py, pc = mx ^ (k >> 2), my ^ ((k >> 1) & 1), mc ^ (k & 1)
            peer = 4 * px + 2 * py + pc
            for a in range(n):
                cp = pltpu.make_async_remote_copy(
                    src_ref=x_refs[a].at[peer] if scatter else x_refs[a], dst_ref=o_refs[a].at[me],
                    send_sem=send_sems.at[a * npeer + k - 1], recv_sem=recv_sems.at[a * npeer + k - 1],
                    device_id=(px, py, pc), device_id_type=pl.DeviceIdType.MESH)
                cp.start()
                copies.append((cp, a, k, peer))
        for cp, a, k, peer in copies:
            pltpu.make_async_remote_copy(
                src_ref=src_me[a], dst_ref=o_refs[a].at[peer], send_sem=send_sems.at[a * npeer + k - 1],
                recv_sem=recv_sems.at[a * npeer + k - 1], device_id=(mx, my, mc),
                device_id_type=pl.DeviceIdType.MESH).wait_recv()
        for cp, _, _, _ in copies:
            cp.wait_send()
        for cp in mine:
            cp.wait()

    return pl.pallas_call(
        body, out_shape=[jax.ShapeDtypeStruct((N_DEV,) + x.shape[-2:], x.dtype) for x in xs],
        in_specs=[pl.BlockSpec(memory_space=pl.ANY)] * n, out_specs=[pl.BlockSpec(memory_space=pl.ANY)] * n,
        scratch_shapes=[pltpu.SemaphoreType.DMA((n * npeer,)), pltpu.SemaphoreType.DMA((n * npeer,)),
                        pltpu.SemaphoreType.DMA((n,))],
        name=name,
    )(*xs)


def _gather_by_chip(xs, name):
    n = len(xs)
    per = N_DEV - 1

    def body(*refs):
        x_refs, o_refs = refs[:n], refs[n:2 * n]
        send_sems, recv_sems, local_sems = refs[2 * n:]
        mx, my, mc = lax.axis_index("x"), lax.axis_index("y"), lax.axis_index("c")
        me, sibling = (mx, my, mc), (mx, my, 1 - mc)
        chips = [(1 - mx, my), (mx, 1 - my), (1 - mx, 1 - my)]
        slot = lambda d: 4 * d[0] + 2 * d[1] + d[2]

        def copy(a, k, block, to, src=None):
            dst = o_refs[a].at[slot(block)]
            return pltpu.make_async_remote_copy(
                src_ref=dst if src is None else src, dst_ref=dst, send_sem=send_sems.at[a * per + k],
                recv_sem=recv_sems.at[a * per + k], device_id=to, device_id_type=pl.DeviceIdType.MESH)

        mine = [pltpu.make_async_copy(x_refs[a], o_refs[a].at[slot(me)], local_sems.at[a]) for a in range(n)]
        for cp in mine:
            cp.start()
        first = []
        for a in range(n):
            first.append(copy(a, 0, me, sibling, src=x_refs[a]))
            first += [copy(a, 1 + j, me, (*chip, mc), src=x_refs[a]) for j, chip in enumerate(chips)]
        for cp in first:
            cp.start()
        passed = []
        for j, chip in enumerate(chips):
            for a in range(n):
                copy(a, 1 + j, (*chip, mc), me).wait_recv()
                cp = copy(a, 4 + j, (*chip, mc), sibling)
                cp.start()
                passed.append(cp)
        for a in range(n):
            copy(a, 0, sibling, me).wait_recv()
            for j, chip in enumerate(chips):
                copy(a, 4 + j, (*chip, 1 - mc), me).wait_recv()
        for cp in first + passed:
            cp.wait_send()
        for cp in mine:
            cp.wait()

    return pl.pallas_call(
        body, out_shape=[jax.ShapeDtypeStruct((N_DEV,) + x.shape, x.dtype) for x in xs],
        in_specs=[pl.BlockSpec(memory_space=pl.ANY)] * n, out_specs=[pl.BlockSpec(memory_space=pl.ANY)] * n,
        scratch_shapes=[pltpu.SemaphoreType.DMA((n * per,)), pltpu.SemaphoreType.DMA((n * per,)),
                        pltpu.SemaphoreType.DMA((n,))],
        name=name,
    )(*xs)


def _peer_of(k):
    mx, my, mc = lax.axis_index("x"), lax.axis_index("y"), lax.axis_index("c")
    px, py, pc = mx ^ (k >> 2), my ^ ((k >> 1) & 1), mc ^ (k & 1)
    return (px, py, pc), 4 * px + 2 * py + pc


def _exchange_start(xs, name, scatter):
    n = len(xs)
    npeer = N_DEV - 1

    def body(*refs):
        x_refs, land_refs = refs[:n], refs[n:2 * n]
        send_sems, recv_sems, token = refs[2 * n], refs[2 * n + 1], refs[-1]
        me = 4 * lax.axis_index("x") + 2 * lax.axis_index("y") + lax.axis_index("c")
        for k in range(1, N_DEV):
            dev, peer = _peer_of(k)
            for a in range(n):
                pltpu.make_async_remote_copy(
                    src_ref=x_refs[a].at[peer] if scatter else x_refs[a], dst_ref=land_refs[a].at[me],
                    send_sem=send_sems.at[a * npeer + k - 1], recv_sem=recv_sems.at[a * npeer + k - 1],
                    device_id=dev, device_id_type=pl.DeviceIdType.MESH).start()
        token[...] = jnp.zeros_like(token)

    hbm = pl.BlockSpec(memory_space=pltpu.HBM)
    sem = pl.BlockSpec(memory_space=pltpu.SEMAPHORE)
    lands = [pltpu.with_memory_space_constraint(lax.empty((N_DEV,) + x.shape[-2:], x.dtype), pltpu.HBM) for x in xs]
    srcs = [pltpu.with_memory_space_constraint(x, pltpu.HBM) for x in xs]
    outs = pl.pallas_call(
        body, name=name,
        out_shape=(pltpu.SemaphoreType.DMA((n * npeer,)), pltpu.SemaphoreType.DMA((n * npeer,)),
                   *[pltpu.HBM(x.shape, x.dtype) for x in srcs], *[pltpu.HBM(z.shape, z.dtype) for z in lands],
                   jax.ShapeDtypeStruct((8, LANE), F32)),
        in_specs=[hbm] * (2 * n), out_specs=(sem, sem, *[hbm] * (2 * n), pl.BlockSpec(memory_space=pltpu.VMEM)),
        input_output_aliases={i: 2 + i for i in range(2 * n)},
        compiler_params=pltpu.CompilerParams(has_side_effects=pltpu.SideEffectType.DATAFLOW_SIDE_EFFECTING),
    )(*srcs, *lands)
    return (outs[0], outs[1], list(outs[2:2 + n]), list(outs[2 + n:2 + 2 * n])), outs[-1][0:1, 0:1]


def _exchange_wait(started, after, name, scatter):
    send_sems, recv_sems, srcs, lands = started
    n = len(srcs)
    npeer = N_DEV - 1

    def body(*refs):
        x_refs, land_refs = refs[:n], refs[n:2 * n]
        send_sems, recv_sems = refs[2 * n], refs[2 * n + 1]
        mx, my, mc = lax.axis_index("x"), lax.axis_index("y"), lax.axis_index("c")
        me = 4 * mx + 2 * my + mc
        for k in range(1, N_DEV):
            _, peer = _peer_of(k)
            for a in range(n):
                src = x_refs[a].at[me] if scatter else x_refs[a]
                cp = pltpu.make_async_remote_copy(
                    src_ref=src, dst_ref=land_refs[a].at[peer], send_sem=send_sems.at[a * npeer + k - 1],
                    recv_sem=recv_sems.at[a * npeer + k - 1], device_id=(mx, my, mc), device_id_type=pl.DeviceIdType.MESH)
                cp.wait_send()
                cp.wait_recv()

    hbm = pl.BlockSpec(memory_space=pltpu.HBM)
    sem = pl.BlockSpec(memory_space=pltpu.SEMAPHORE)
    outs = pl.pallas_call(
        body, name=name,
        out_shape=(*[pltpu.HBM(x.shape, x.dtype) for x in srcs], *[pltpu.HBM(z.shape, z.dtype) for z in lands]),
        in_specs=[hbm] * (2 * n) + [sem, sem, pl.BlockSpec(memory_space=pl.ANY)], out_specs=tuple([hbm] * (2 * n)),
        input_output_aliases={i: i for i in range(2 * n)},
        compiler_params=pltpu.CompilerParams(has_side_effects=pltpu.SideEffectType.DATAFLOW_SIDE_EFFECTING),
    )(*srcs, *lands, send_sems, recv_sems, after)
    me = 4 * lax.axis_index("x") + 2 * lax.axis_index("y") + lax.axis_index("c")
    full = []
    for x, land in zip(outs[:n], outs[n:]):
        own = lax.dynamic_slice(x, (me, 0, 0), (1,) + x.shape[1:]) if scatter else x[None]
        full.append(lax.dynamic_update_slice(land, own, (me, 0, 0)))
    return full


def _sum_slots(x, name):
    _, r, c = x.shape
    tr = _pick(r, (512, 256, 128, 64, 32, 16))

    def body(x_ref, o_ref):
        acc = x_ref[0].astype(F32)
        for s in range(1, N_DEV):
            acc = acc + x_ref[s].astype(F32)
        o_ref[...] = acc

    return pl.pallas_call(
        body, grid=(r // tr,), in_specs=[pl.BlockSpec((N_DEV, tr, c), lambda i: (0, i, 0))],
        out_specs=pl.BlockSpec((tr, c), lambda i: (i, 0)), out_shape=jax.ShapeDtypeStruct((r, c), F32), name=name,
        compiler_params=pltpu.CompilerParams(dimension_semantics=("arbitrary",)),
    )(x)


def _mod_fwd(c_all, w_ada, b_ada_mine):
    def body(c_ref, w_ref, b_ref, o_ref):
        o_ref[...] = _doth(_silu(c_ref[...]), w_ref[...]) + b_ref[...]

    return pl.pallas_call(body, out_shape=jax.ShapeDtypeStruct((c_all.shape[0], w_ada.shape[1]), F32), name="mod_fwd")(c_all, w_ada, b_ada_mine)


def _mod_bwd(c_all_t, dmod_mine):
    def body(ct_ref, d_ref, o_ref):
        s = _silu(ct_ref[...])
        acc = s[:, 0:1] * d_ref[pl.ds(0, 1), :]
        for b in range(1, N_DEV):
            acc = acc + s[:, b:b + 1] * d_ref[pl.ds(b, 1), :]
        o_ref[...] = acc

    return pl.pallas_call(body, out_shape=jax.ShapeDtypeStruct((c_all_t.shape[0], dmod_mine.shape[1]), F32), name="mod_bwd")(c_all_t, dmod_mine)


def _conv_fwd(proj, conv_w8, tm):
    t = proj.shape[0]
    ch = DN_CONV_CH

    def body(x_ref, w_ref, o_ref, buf):
        @pl.when(pl.program_id(0) == 0)
        def _():
            buf[pl.ds(0, CONV_HALO), :] = jnp.zeros((CONV_HALO, ch), F32)

        buf[pl.ds(CONV_HALO, tm), :] = x_ref[...].astype(F32)
        for c0 in range(0, ch, CONV_COLS):
            cols = pl.ds(c0, CONV_COLS)
            w = [w_ref[pl.ds(j, 1), cols] for j in range(CONV_K)]
            for r0 in range(0, tm, CONV_ROWS):
                acc = buf[pl.ds(r0 + CONV_HALO - (CONV_K - 1), CONV_ROWS), cols] * w[0]
                for j in range(1, CONV_K):
                    acc = acc + buf[pl.ds(r0 + CONV_HALO - (CONV_K - 1) + j, CONV_ROWS), cols] * w[j]
                o_ref[pl.ds(r0, CONV_ROWS), cols] = _silu(acc)
        buf[pl.ds(0, CONV_HALO), :] = buf[pl.ds(tm, CONV_HALO), :]

    return pl.pallas_call(
        body, grid=(t // tm,), in_specs=[pl.BlockSpec((tm, ch), lambda i: (i, 0)), _full(conv_w8.shape)],
        out_specs=pl.BlockSpec((tm, ch), lambda i: (i, 0)), out_shape=jax.ShapeDtypeStruct((t, ch), F32),
        scratch_shapes=[pltpu.VMEM((tm + CONV_HALO, ch), F32)], name="conv_fwd",
        compiler_params=pltpu.CompilerParams(dimension_semantics=("arbitrary",)),
    )(proj, conv_w8)


def _conv_bwd(proj, conv_w8, dact, others, tm):
    t = proj.shape[0]
    ch = DN_CONV_CH
    nt = t // tm
    halo_blk = 2 * CONV_HALO
    hb = tm // halo_blk
    n_others = len(others)

    def body(x_ref, xp_ref, w_ref, dy_ref, *refs):
        piece_refs, (dx_ref, dw_ref, xbuf, dbuf) = refs[:n_others], refs[n_others:]
        step = pl.program_id(0)
        for (off, arr), p_ref in zip(others, piece_refs):
            dx_ref[:, pl.ds(off, arr.shape[1])] = p_ref[...].astype(dx_ref.dtype)

        @pl.when(step == 0)
        def _():
            dbuf[pl.ds(tm, CONV_HALO), :] = jnp.zeros((CONV_HALO, ch), F32)
            dw_ref[...] = jnp.zeros_like(dw_ref)

        first = step == nt - 1
        xbuf[pl.ds(0, CONV_HALO), :] = jnp.where(first, 0.0, xp_ref[...].astype(F32)[halo_blk - CONV_HALO:])
        xbuf[pl.ds(CONV_HALO, tm), :] = x_ref[...].astype(F32)
        for c0 in range(0, ch, CONV_COLS):
            cols = pl.ds(c0, CONV_COLS)
            w = [w_ref[pl.ds(j, 1), cols] for j in range(CONV_K)]
            dw = [jnp.zeros((1, CONV_COLS), F32) for _ in range(CONV_K)]
            for r0 in range(0, tm, CONV_ROWS):
                xs = [xbuf[pl.ds(r0 + CONV_HALO - (CONV_K - 1) + j, CONV_ROWS), cols] for j in range(CONV_K)]
                pre = xs[0] * w[0]
                for j in range(1, CONV_K):
                    pre = pre + xs[j] * w[j]
                sg = _sigmoid(pre)
                dpre = dy_ref[pl.ds(r0, CONV_ROWS), cols] * (sg * (1.0 + pre * (1.0 - sg)))
                dbuf[pl.ds(r0, CONV_ROWS), cols] = dpre
                dw = [dw[j] + jnp.sum(dpre * xs[j], axis=0, keepdims=True) for j in range(CONV_K)]
            for j in range(CONV_K):
                dw_ref[pl.ds(j, 1), cols] += dw[j]
            for r0 in range(0, tm, CONV_ROWS):
                dx = dbuf[pl.ds(r0 + CONV_K - 1, CONV_ROWS), cols] * w[0]
                for j in range(1, CONV_K):
                    dx = dx + dbuf[pl.ds(r0 + CONV_K - 1 - j, CONV_ROWS), cols] * w[j]
                dx_ref[pl.ds(r0, CONV_ROWS), cols] = dx.astype(dx_ref.dtype)
        dbuf[pl.ds(tm, CONV_HALO), :] = dbuf[pl.ds(0, CONV_HALO), :]

    rev = lambda i: (nt - 1 - i, 0)
    prev = lambda i: (jnp.maximum((nt - 1 - i) * hb - 1, 0), 0)
    return pl.pallas_call(
        body, grid=(nt,),
        in_specs=[pl.BlockSpec((tm, ch), rev), pl.BlockSpec((halo_blk, ch), prev), _full(conv_w8.shape),
                  pl.BlockSpec((tm, ch), rev)] + [pl.BlockSpec((tm, arr.shape[1]), rev) for _, arr in others],
        out_specs=[pl.BlockSpec((tm, N_INP), rev), _full(conv_w8.shape)],
        out_shape=[jax.ShapeDtypeStruct((t, N_INP), BF), jax.ShapeDtypeStruct(conv_w8.shape, F32)],
        scratch_shapes=[pltpu.VMEM((tm + CONV_HALO, ch), F32), pltpu.VMEM((tm + CONV_HALO, ch), F32)], name="conv_bwd",
        compiler_params=pltpu.CompilerParams(dimension_semantics=("arbitrary",)),
    )(proj, proj, conv_w8, dact, *[arr for _, arr in others])


BNN = (((2,), (1,)), ((0,), (0,)))
BNT = (((2,), (2,)), ((0,), (0,)))
BTN = (((1,), (1,)), ((0,), (0,)))


def _bdot(a, b, dims, precision=None):
    return lax.dot_general(a, b, dims, precision=precision, preferred_element_type=F32)


@jax.custom_vjp
def _bmmb_nt(a, b):
    return _bdot(a.astype(BF), b.astype(BF), BNT)


def _bmmb_nt_fwd(a, b):
    return _bmmb_nt(a, b), (a, b)


def _bmmb_nt_bwd(res, g):
    a, b = res
    gb = g.astype(BF)
    return _bdot(gb, b.astype(BF), BNN), _bdot(gb, a.astype(BF), BTN)


_bmmb_nt.defvjp(_bmmb_nt_fwd, _bmmb_nt_bwd)


@jax.custom_vjp
def _bmmb(a, b):
    return _bdot(a.astype(BF), b.astype(BF), BNN)


def _bmmb_fwd(a, b):
    return _bmmb(a, b), (a, b)


def _bmmb_bwd(res, g):
    a, b = res
    gb = g.astype(BF)
    return _bdot(gb, b.astype(BF), BNT), _bdot(a.astype(BF), gb, BTN)


_bmmb.defvjp(_bmmb_fwd, _bmmb_bwd)


@jax.custom_vjp
def _bmmb_tn(a, b):
    return _bdot(a.astype(BF), b.astype(BF), BTN)


def _bmmb_tn_fwd(a, b):
    return _bmmb_tn(a, b), (a, b)


def _bmmb_tn_bwd(res, g):
    a, b = res
    gb = g.astype(BF)
    return _bdot(b.astype(BF), gb, BNT), _bdot(a.astype(BF), gb, BNN)


_bmmb_tn.defvjp(_bmmb_tn_fwd, _bmmb_tn_bwd)


def _unit_lower_solve_fwd(a, r):
    c = a.shape[-1]
    ri = lax.broadcasted_iota(jnp.int32, a.shape, 1)
    ci = lax.broadcasted_iota(jnp.int32, a.shape, 2)
    xm = -a
    inv = (ri == ci).astype(F32) + xm
    for _ in range(int(math.log2(c)) - 1):
        xm = _bdot(xm, xm, BNN, HI)
        inv = inv + _bdot(inv, xm, BNN, HI)
    x = _bdot(inv, r, BNN, HI)
    return x, (inv, x)


def _unit_lower_solve_bwd(res, g):
    inv, x = res
    dr = _bdot(inv, g, BTN, HI)
    return -_bdot(dr, x, BNT, HI), dr


@jax.custom_vjp
def _unit_lower_solve_given(a, r, inv):
    return _bdot(inv, r, BNN, HI)


def _unit_lower_solve_given_fwd(a, r, inv):
    x = _bdot(inv, r, BNN, HI)
    return x, (inv, x)


def _unit_lower_solve_given_bwd(res, g):
    da, dr = _unit_lower_solve_bwd(res, g)
    return da, dr, jnp.zeros_like(res[0])


_unit_lower_solve_given.defvjp(_unit_lower_solve_given_fwd, _unit_lower_solve_given_bwd)


def _gdn_intra(qkv, ba, al8, dt8, inv4=None):
    tm = qkv.shape[0]
    nb = tm // CHUNK
    bsz = DN_HEADS * nb

    def heads(x0):
        return jnp.concatenate([qkv[:, x0 + h * LANE:x0 + (h + 1) * LANE].reshape(nb, CHUNK, LANE) for h in range(DN_HEADS)], axis=0)

    def spread(c0):
        return jnp.concatenate([jnp.broadcast_to(ba[:, c0 + h:c0 + h + 1], (tm, LANE)).reshape(nb, CHUNK, LANE)
                                for h in range(DN_HEADS)], axis=0)

    def per_head(v8):
        return jnp.concatenate([jnp.broadcast_to(v8[0:1, h:h + 1].reshape(1, 1, 1), (nb, 1, LANE)) for h in range(DN_HEADS)], axis=0)

    ri = lax.broadcasted_iota(jnp.int32, (bsz, CHUNK, CHUNK), 1)
    ci = lax.broadcasted_iota(jnp.int32, (bsz, CHUNK, CHUNK), 2)
    incl = ri >= ci
    strict = ri > ci

    q = _l2norm(heads(0)) * (DN_DK ** -0.5)
    k = _l2norm(heads(DN_QK))
    va = heads(2 * DN_QK)
    beta = _sigmoid(spread(0))
    g = -jnp.exp(per_head(al8)) * _softplus(spread(DN_HEADS) + per_head(dt8))
    gc = _bdot(incl.astype(F32), g, BNN, HI)
    g_last = jnp.sum(g, axis=1, keepdims=True)
    gcol = gc[:, :, :CHUNK]
    diff = gcol - jnp.swapaxes(gcol, 1, 2)
    decay = jnp.where(incl, jnp.exp(jnp.where(incl, diff, 0.0)), 0.0)
    kb = k * beta
    a_mat = jnp.where(strict, _bmmb_nt(kb, k) * decay, 0.0)
    egc = jnp.exp(gc)
    rhs = jnp.concatenate([kb * egc, va * beta], axis=2)
    if inv4 is None:
        wu, (inv, _) = _unit_lower_solve_fwd(a_mat, rhs)
    else:
        wu = _unit_lower_solve_given(a_mat, rhs, inv4.reshape(bsz, CHUNK, CHUNK))
    attn = jnp.where(incl, _bmmb_nt(q, k) * decay, 0.0)

    def unheads(x):
        return jnp.concatenate([x[h * nb:(h + 1) * nb].reshape(tm, LANE) for h in range(DN_HEADS)], axis=1)

    w_c, u_c = wu[:, :, :DN_DK], wu[:, :, DN_DK:]
    kd = k * jnp.exp(g_last - gc)
    out = (unheads(q * egc - _bmmb(attn, w_c)), unheads(_bmmb(attn, u_c)),
           _bmmb_tn(kd, w_c).reshape(DN_HEADS, nb, DN_DK, DN_DK), _bmmb_tn(kd, u_c).reshape(DN_HEADS, nb, DN_DK, DN_DV),
           jnp.broadcast_to(g_last, (bsz, GL_ROWS, LANE)).reshape(DN_HEADS, nb, GL_ROWS, LANE))
    return out if inv4 is not None else out + (inv.reshape(DN_HEADS, nb, CHUNK, CHUNK),)


def _gdn_scan_step(qp, op, c_mat, n_mat, gl, s):
    return _mmb(qp, s) + op, s * jnp.exp(gl) - _mmb(c_mat, s) + n_mat


def _gdn_intra_specs(t, tm, dts, order=lambda i: i):
    nb = tm // CHUNK
    row = pl.BlockSpec((tm, DN_VW), lambda i: (order(i), 0))
    mat = pl.BlockSpec((DN_HEADS, nb, DN_DK, DN_DV), lambda i: (0, order(i), 0, 0))
    row_shape = lambda d: jax.ShapeDtypeStruct((t, DN_VW), d)
    mat_shape = lambda d: jax.ShapeDtypeStruct((DN_HEADS, t // CHUNK, DN_DK, DN_DV), d)
    gl = pl.BlockSpec((DN_HEADS, nb, GL_ROWS, LANE), lambda i: (0, order(i), 0, 0))
    gl_shape = jax.ShapeDtypeStruct((DN_HEADS, t // CHUNK, GL_ROWS, LANE), dts[4])
    return [row, row, mat, mat, gl], [row_shape(dts[0]), row_shape(dts[1]), mat_shape(dts[2]), mat_shape(dts[3]), gl_shape]


def _gdn_intra_fwd(qkv, proj, al8, dt8, tm):
    t = qkv.shape[0]

    def body(qkv_ref, ba_ref, al_ref, dt_ref, *outs):
        for o, val in zip(outs, _gdn_intra(qkv_ref[...], ba_ref[...], al_ref[...], dt_ref[...])):
            o[...] = val.astype(o.dtype)

    specs, shapes = _gdn_intra_specs(t, tm, (BF, F32, BF, BF, F32))
    specs.append(_gdn_inverse_spec(tm))
    shapes.append(jax.ShapeDtypeStruct((DN_HEADS, t // CHUNK, CHUNK, CHUNK), F32))
    res = pl.pallas_call(
        body, grid=(t // tm,),
        in_specs=[pl.BlockSpec((tm, DN_CONV_CH), lambda i: (i, 0)), pl.BlockSpec((tm, LANE), lambda i: (i, 0)),
                  _full(al8.shape), _full(dt8.shape)],
        out_specs=specs, out_shape=shapes, name="gdn_intra_fwd",
        compiler_params=pltpu.CompilerParams(dimension_semantics=("parallel",)),
    )(qkv, proj, al8, dt8)
    return res[:5], res[5]


def _gdn_inverse_spec(tm):
    return pl.BlockSpec((DN_HEADS, tm // CHUNK, CHUNK, CHUNK), lambda i: (0, i, 0, 0))


def _gdn_intra_bwd(qkv, proj, al8, dt8, inverses, cts, tm):
    t = qkv.shape[0]

    def body(qkv_ref, ba_ref, al_ref, dt_ref, inv_ref, *refs):
        ct_refs, (dqkv_ref, dba_ref, dal_ref, ddt_ref) = refs[:5], refs[5:]

        @pl.when(pl.program_id(0) == 0)
        def _():
            dal_ref[...] = jnp.zeros_like(dal_ref)
            ddt_ref[...] = jnp.zeros_like(ddt_ref)

        _, vjp = jax.vjp(functools.partial(_gdn_intra, inv4=inv_ref[...]), qkv_ref[...], ba_ref[...], al_ref[...], dt_ref[...])
        dqkv, dba, dal, ddt = vjp(tuple(r[...].astype(F32) for r in ct_refs))
        dqkv_ref[...] = dqkv.astype(dqkv_ref.dtype)
        dba_ref[...] = dba.astype(dba_ref.dtype)
        dal_ref[...] += dal
        ddt_ref[...] += ddt

    specs, _ = _gdn_intra_specs(t, tm, (F32,) * 5)
    return pl.pallas_call(
        body, grid=(t // tm,),
        in_specs=[pl.BlockSpec((tm, DN_CONV_CH), lambda i: (i, 0)), pl.BlockSpec((tm, LANE), lambda i: (i, 0)),
                  _full(al8.shape), _full(dt8.shape), _gdn_inverse_spec(tm)] + specs,
        out_specs=[pl.BlockSpec((tm, DN_CONV_CH), lambda i: (i, 0)), pl.BlockSpec((tm, LANE), lambda i: (i, 0)),
                   _full(al8.shape), _full(dt8.shape)],
        out_shape=[jax.ShapeDtypeStruct((t, DN_CONV_CH), BF), jax.ShapeDtypeStruct((t, LANE), BF),
                   jax.ShapeDtypeStruct(al8.shape, F32), jax.ShapeDtypeStruct(dt8.shape, F32)],
        name="gdn_intra_bwd", compiler_params=pltpu.CompilerParams(dimension_semantics=("arbitrary",)),
    )(qkv, proj, al8, dt8, inverses, *cts)


def _gdn_scan_fwd(intra, tm):
    t = intra[0].shape[0]
    nb = tm // CHUNK
    nc = t // CHUNK

    def body(qp_ref, op_ref, c_ref, n_ref, gl_ref, o_ref, ss_ref, s_scr):
        @pl.when(pl.program_id(0) == 0)
        def _():
            s_scr[...] = jnp.zeros_like(s_scr)

        state = [s_scr[h] for h in range(DN_HEADS)]
        for cc in range(nb):
            rows = pl.ds(cc * CHUNK, CHUNK)
            for h in range(DN_HEADS):
                cols = pl.ds(h * DN_DV, DN_DV)
                ss_ref[cc, h] = state[h].astype(ss_ref.dtype)
                o_ref[rows, cols], state[h] = _gdn_scan_step(
                    qp_ref[rows, cols], op_ref[rows, cols], c_ref[h, cc], n_ref[h, cc], gl_ref[h, cc, pl.ds(0, 1), :], state[h])
        for h in range(DN_HEADS):
            s_scr[h] = state[h]

    specs, _ = _gdn_intra_specs(t, tm, (F32,) * 5)
    return pl.pallas_call(
        body, grid=(t // tm,), in_specs=specs,
        out_specs=[pl.BlockSpec((tm, DN_VW), lambda i: (i, 0)),
                   pl.BlockSpec((nb, DN_HEADS, DN_DK, DN_DV), lambda i: (i, 0, 0, 0))],
        out_shape=[jax.ShapeDtypeStruct((t, DN_VW), F32), jax.ShapeDtypeStruct((nc, DN_HEADS, DN_DK, DN_DV), BF)],
        scratch_shapes=[pltpu.VMEM((DN_HEADS, DN_DK, DN_DV), F32)], name="gdn_scan_fwd",
        compiler_params=pltpu.CompilerParams(dimension_semantics=("arbitrary",)),
    )(*intra)


def _gdn_scan_bwd(intra, states, do, tm):
    t = intra[0].shape[0]
    nb = tm // CHUNK
    ng = t // tm

    def body(qp_ref, op_ref, c_ref, n_ref, gl_ref, ss_ref, do_ref, dqp_ref, dop_ref, dc_ref, dn_ref, dgl_ref, ds_scr):
        @pl.when(pl.program_id(0) == 0)
        def _():
            ds_scr[...] = jnp.zeros_like(ds_scr)

        d_state = [ds_scr[h] for h in range(DN_HEADS)]
        for cc in reversed(range(nb)):
            rows = pl.ds(cc * CHUNK, CHUNK)
            for h in range(DN_HEADS):
                cols = pl.ds(h * DN_DV, DN_DV)
                _, vjp = jax.vjp(_gdn_scan_step, qp_ref[rows, cols].astype(F32), op_ref[rows, cols], c_ref[h, cc].astype(F32),
                                 n_ref[h, cc].astype(F32), gl_ref[h, cc, pl.ds(0, 1), :], ss_ref[cc, h].astype(F32))
                dqp_ref[rows, cols], dop_ref[rows, cols], dc, dn, dgl, d_state[h] = vjp((do_ref[rows, cols], d_state[h]))
                dc_ref[h, cc] = dc.astype(dc_ref.dtype)
                dn_ref[h, cc] = dn.astype(dn_ref.dtype)
                first_row = lax.broadcasted_iota(jnp.int32, (GL_ROWS, LANE), 0) == 0
                dgl_ref[h, cc] = jnp.where(first_row, dgl, 0.0)
        for h in range(DN_HEADS):
            ds_scr[h] = d_state[h]

    five, shapes = _gdn_intra_specs(t, tm, (F32, F32, BF, BF, F32), order=lambda i: ng - 1 - i)
    row = five[0]
    return pl.pallas_call(
        body, grid=(ng,),
        in_specs=five + [pl.BlockSpec((nb, DN_HEADS, DN_DK, DN_DV), lambda i: (ng - 1 - i, 0, 0, 0)), row],
        out_specs=five, out_shape=shapes,
        scratch_shapes=[pltpu.VMEM((DN_HEADS, DN_DK, DN_DV), F32)], name="gdn_scan_bwd",
        compiler_params=pltpu.CompilerParams(dimension_semantics=("arbitrary",)),
    )(*intra, states, do)


def _gdn_out(o, z, g):
    parts = []
    for h in range(DN_HEADS):
        sl = slice(h * DN_DV, (h + 1) * DN_DV)
        parts.append(_rmsnorm(o[:, sl], g) * _silu(z[:, sl]))
    return parts


_Q_SCALE = math.log2(math.e) / math.sqrt(QK_NOPE + QK_ROPE)


def _rope_tables(pos, inv_freq2):
    lane = lax.broadcasted_iota(jnp.int32, (1, LANE), 1)
    ang = pos * inv_freq2
    cos = jnp.where(lane < QK_ROPE, jnp.cos(ang), 0.0)
    sin = jnp.where(lane < QK_ROPE // 2, -jnp.sin(ang), jnp.where(lane < QK_ROPE, jnp.sin(ang), 0.0))
    return cos, sin


@jax.custom_vjp
def _rope_swap(u):
    lane = lax.broadcasted_iota(jnp.int32, u.shape, 1)
    half = QK_ROPE // 2
    return jnp.where(lane < half, pltpu.roll(u, LANE - half, 1), jnp.where(lane < QK_ROPE, pltpu.roll(u, half, 1), 0.0))


_rope_swap.defvjp(lambda u: (_rope_swap(u), None), lambda _, g: (_rope_swap(g),))


def _mla_prep(cq, ckv, kr, gq, gkv, w_uq, w_ukv, cos, sin):
    rope = lambda u: u * cos + _rope_swap(u) * sin
    q_lin = _mmb_nt(_rmsnorm(cq, gq), w_uq) * _Q_SCALE
    kv_lin = _mmb_nt(_rmsnorm(ckv, gkv), w_ukv)
    k_rope = rope(kr)
    qs, ks, vs = [], [], []
    for h in range(MLA_HEADS):
        qs += [q_lin[:, h * LANE:(h + 1) * LANE], rope(q_lin[:, (MLA_HEADS + h) * LANE:(MLA_HEADS + h + 1) * LANE])]
        ks += [kv_lin[:, 2 * h * LANE:(2 * h + 1) * LANE], k_rope]
        vs += [kv_lin[:, (2 * h + 1) * LANE:(2 * h + 2) * LANE]]
    return qs + ks + vs


def _mla_prep_fwd(proj, pos_col, inv_freq2, gq, gkv, w_uq, w_ukv, tm):
    t = proj.shape[0]
    nq = 2 * MLA_HEADS

    def body(cq_ref, ckv_ref, kr_ref, pos_ref, f_ref, gq_ref, gkv_ref, wq_ref, wkv_ref, q_ref, k_ref, v_ref):
        cos, sin = _rope_tables(pos_ref[...], f_ref[...])
        outs = _mla_prep(cq_ref[...].astype(F32), ckv_ref[...].astype(F32), kr_ref[...].astype(F32), gq_ref[...], gkv_ref[...], wq_ref[...], wkv_ref[...],
                         cos, sin)
        for i in range(nq):
            q_ref[:, pl.ds(i * LANE, LANE)] = outs[i].astype(q_ref.dtype)
            k_ref[:, pl.ds(i * LANE, LANE)] = outs[nq + i].astype(k_ref.dtype)
        for h in range(MLA_HEADS):
            v_ref[:, pl.ds(h * LANE, LANE)] = outs[2 * nq + h].astype(v_ref.dtype)

    row = lambda w, j: pl.BlockSpec((tm, w), functools.partial(lambda i, j: (i, j), j=j))
    return pl.pallas_call(
        body, grid=(t // tm,),
        in_specs=[row(Q_LORA, P_CQ // Q_LORA), row(KV_LORA, P_CKV // KV_LORA), row(LANE, P_KR // LANE),
                  pl.BlockSpec((tm, 1), lambda i: (i, 0)), _full(inv_freq2.shape), _full(gq.shape), _full(gkv.shape),
                  _full(w_uq.shape), _full(w_ukv.shape)],
        out_specs=[row(nq * LANE, 0), row(nq * LANE, 0), row(MLA_VW, 0)],
        out_shape=[jax.ShapeDtypeStruct((t, nq * LANE), BF), jax.ShapeDtypeStruct((t, nq * LANE), BF),
                   jax.ShapeDtypeStruct((t, MLA_VW), BF)],
        name="mla_prep_fwd", compiler_params=pltpu.CompilerParams(dimension_semantics=("arbitrary",)),
    )(proj, proj, proj, pos_col, inv_freq2, gq, gkv, w_uq, w_ukv)


def _mla_prep_bwd(proj, pos_col, inv_freq2, gq, gkv, w_uq, w_ukv, dq, dk, dv, tm):
    t = proj.shape[0]
    nq = 2 * MLA_HEADS

    def body(cq_ref, ckv_ref, kr_ref, pos_ref, f_ref, gq_ref, gkv_ref, wq_ref, wkv_ref, dq_ref, dk_ref, dv_ref,
             dcq_ref, dckv_ref, dkr_ref, dgq_ref, dgkv_ref, dwq_ref, dwkv_ref):
        @pl.when(pl.program_id(0) == 0)
        def _():
            for o in (dgq_ref, dgkv_ref, dwq_ref, dwkv_ref):
                o[...] = jnp.zeros_like(o)

        cos, sin = _rope_tables(pos_ref[...], f_ref[...])
        f = functools.partial(_mla_prep, cos=cos, sin=sin)
        _, vjp = jax.vjp(f, cq_ref[...].astype(F32), ckv_ref[...].astype(F32), kr_ref[...].astype(F32), gq_ref[...], gkv_ref[...], wq_ref[...], wkv_ref[...])
        cts = [dq_ref[:, pl.ds(i * LANE, LANE)] for i in range(nq)]
        cts += [dk_ref[:, pl.ds(i * LANE, LANE)] for i in range(nq)]
        cts += [dv_ref[:, pl.ds(h * LANE, LANE)] for h in range(MLA_HEADS)]
        dcq, dckv, dkr, dgq, dgkv, dwq, dwkv = vjp(cts)
        dcq_ref[...] = dcq.astype(dcq_ref.dtype)
        dckv_ref[...] = dckv.astype(dckv_ref.dtype)
        dkr_ref[...] = dkr.astype(dkr_ref.dtype)
        dgq_ref[...] += dgq
        dgkv_ref[...] += dgkv
        dwq_ref[...] += dwq
        dwkv_ref[...] += dwkv

    row = lambda w, j: pl.BlockSpec((tm, w), functools.partial(lambda i, j: (i, j), j=j))
    return pl.pallas_call(
        body, grid=(t // tm,),
        in_specs=[row(Q_LORA, P_CQ // Q_LORA), row(KV_LORA, P_CKV // KV_LORA), row(LANE, P_KR // LANE),
                  pl.BlockSpec((tm, 1), lambda i: (i, 0)), _full(inv_freq2.shape), _full(gq.shape), _full(gkv.shape),
                  _full(w_uq.shape), _full(w_ukv.shape), row(nq * LANE, 0), row(nq * LANE, 0), row(MLA_VW, 0)],
        out_specs=[row(Q_LORA, 0), row(KV_LORA, 0), row(LANE, 0), _full(gq.shape), _full(gkv.shape),
                   _full(w_uq.shape), _full(w_ukv.shape)],
        out_shape=[jax.ShapeDtypeStruct((t, Q_LORA), BF), jax.ShapeDtypeStruct((t, KV_LORA), BF),
                   jax.ShapeDtypeStruct((t, LANE), BF), jax.ShapeDtypeStruct(gq.shape, F32),
                   jax.ShapeDtypeStruct(gkv.shape, F32), jax.ShapeDtypeStruct(w_uq.shape, F32),
                   jax.ShapeDtypeStruct(w_ukv.shape, F32)],
        name="mla_prep_bwd", compiler_params=pltpu.CompilerParams(dimension_semantics=("arbitrary",)),
    )(proj, proj, proj, pos_col, inv_freq2, gq, gkv, w_uq, w_ukv, dq, dk, dv)


_NEG = -1e30
_LN2 = math.log(2.0)
ATT_CHAINS = 2


def _causal(tq, tk, q0, k0):
    row = q0 + lax.broadcasted_iota(jnp.int32, (tq, tk), 0)
    col = k0 + lax.broadcasted_iota(jnp.int32, (tq, tk), 1)
    return col <= row


def _attn_fwd(q, k, v, tq, tk):
    t = q.shape[0]

    assert tk % tq == 0 or tq % tk == 0
    n_diag = max(1, tq // tk)

    th = tq // ATT_CHAINS

    def body(q_ref, k_ref, v_ref, o_ref, lse_ref):
        i = pl.program_id(1)
        n_full = (i * tq) // tk

        def step(k0, carry, masked):
            out = []
            for c, (m, l, acc) in enumerate(carry):
                kw = min(tk, (c + 1) * th) if masked and tk == tq else tk
                kt = k_ref[pl.ds(k0, kw), :]
                vt = v_ref[pl.ds(k0, kw), :]
                s = _dot(q_ref[pl.ds(c * th, th), :], kt, NT)
                if masked:
                    s = jnp.where(_causal(th, kw, i * tq + c * th, k0), s, _NEG)
                m_new = jnp.maximum(m, jnp.max(s, axis=-1, keepdims=True))
                p = jnp.exp2(s - m_new)
                alpha = jnp.exp2(m - m_new)
                out.append((m_new, alpha * l + jnp.sum(p, axis=-1, keepdims=True), alpha * acc + _dot(p.astype(BF), vt)))
            return tuple(out)

        init = tuple((jnp.full((th, 1), _NEG, F32), jnp.zeros((th, 1), F32), jnp.zeros((th, V_HEAD), F32)) for _ in range(ATT_CHAINS))
        carry = lax.fori_loop(0, n_full, lambda j, c: step(pl.multiple_of(j * tk, tk), c, False), init)
        for dd in range(n_diag):
            carry = step(pl.multiple_of((n_full + dd) * tk, tk), carry, True)
        for c, (m, l, acc) in enumerate(carry):
            o_ref[pl.ds(c * th, th), :] = acc / l
            lse_ref[pl.ds(c * th, th), :] = jnp.broadcast_to(m + jnp.log2(l), (th, LANE))

    return pl.pallas_call(
        body, grid=(MLA_HEADS, t // tq),
        in_specs=[pl.BlockSpec((tq, 2 * LANE), lambda h, i: (i, h)), pl.BlockSpec((t, 2 * LANE), lambda h, i: (0, h)),
                  pl.BlockSpec((t, V_HEAD), lambda h, i: (0, h))],
        out_specs=[pl.BlockSpec((tq, V_HEAD), lambda h, i: (i, h)), pl.BlockSpec((tq, LANE), lambda h, i: (i, h))],
        out_shape=[jax.ShapeDtypeStruct((t, MLA_VW), F32), jax.ShapeDtypeStruct((t, MLA_HEADS * LANE), F32)],
        name="attn_fwd", compiler_params=pltpu.CompilerParams(dimension_semantics=("parallel", "arbitrary")),
    )(q, k, v)


def _attn_bwd(q, k, v, do, lse, delta, tq, tk):
    t = q.shape[0]
    nkt = t // tk
    assert tk % tq == 0

    def body(q_ref, k_ref, v_ref, do_ref, lse_ref, dl_ref, dq_ref, dk_ref, dv_ref):
        j = pl.program_id(1)

        @pl.when(j == 0)
        def _():
            dq_ref[...] = jnp.zeros_like(dq_ref)

        kt = k_ref[...]
        vt = v_ref[...]

        def step(q0, carry, masked, kw=tk):
            dk, dv = carry
            rows = pl.ds(q0, tq)
            qt = q_ref[rows, :]
            dot_ = do_ref[rows, :]
            ktw, vtw = kt[:kw], vt[:kw]
            p = jnp.exp2(_dot(qt, ktw, NT) - lse_ref[rows, pl.ds(0, 1)])
            if masked:
                p = jnp.where(_causal(tq, kw, q0, j * tk), p, 0.0)
            dv_w = _dot(p.astype(BF), dot_, TN)
            ds = (p * (_dot(dot_, vtw, NT) - dl_ref[rows, pl.ds(0, 1)])).astype(BF)
            dk_w = _dot(ds, qt, TN)
            dq_ref[rows, :] += _dot(ds, ktw)
            if kw == tk:
                return dk + dk_w, dv + dv_w
            return (jnp.concatenate([dk[:kw] + dk_w, dk[kw:]], axis=0), jnp.concatenate([dv[:kw] + dv_w, dv[kw:]], axis=0))

        per = tk // tq
        carry = (jnp.zeros((tk, 2 * LANE), F32), jnp.zeros((tk, V_HEAD), F32))
        for dd in range(per):
            carry = step(pl.multiple_of(j * tk + dd * tq, tq), carry, True, kw=(dd + 1) * tq)

        def group(g, c):
            for dd in range(per):
                c = step(pl.multiple_of(g * tk + dd * tq, tq), c, False)
            return c

        dk, dv = lax.fori_loop(j + 1, nkt, group, carry)
        dk_ref[...] = dk * _LN2
        dv_ref[...] = dv

        @pl.when(j == nkt - 1)
        def _():
            dq_ref[...] = dq_ref[...] * _LN2

    return pl.pallas_call(
        body, grid=(MLA_HEADS, nkt),
        in_specs=[pl.BlockSpec((t, 2 * LANE), lambda h, j: (0, h)), pl.BlockSpec((tk, 2 * LANE), lambda h, j: (j, h)),
                  pl.BlockSpec((tk, V_HEAD), lambda h, j: (j, h)), pl.BlockSpec((t, V_HEAD), lambda h, j: (0, h)),
                  pl.BlockSpec((t, LANE), lambda h, j: (0, h)), pl.BlockSpec((t, LANE), lambda h, j: (0, h))],
        out_specs=[pl.BlockSpec((t, 2 * LANE), lambda h, j: (0, h)), pl.BlockSpec((tk, 2 * LANE), lambda h, j: (j, h)),
                   pl.BlockSpec((tk, V_HEAD), lambda h, j: (j, h))],
        out_shape=[jax.ShapeDtypeStruct((t, MLA_HEADS * 2 * LANE), F32), jax.ShapeDtypeStruct((t, MLA_HEADS * 2 * LANE), F32),
                   jax.ShapeDtypeStruct((t, MLA_VW), F32)],
        name="attn_bwd", compiler_params=pltpu.CompilerParams(dimension_semantics=("parallel", "arbitrary")),
    )(q, k, v, do, lse, delta)


def _adam_update(w, g, m, v):
    mm = ADAM_B1 * m + (1.0 - ADAM_B1) * g
    vv = ADAM_B2 * v + (1.0 - ADAM_B2) * jnp.square(g)
    m_hat = mm / (1.0 - ADAM_B1 ** ADAM_STEP)
    v_hat = vv / (1.0 - ADAM_B2 ** ADAM_STEP)
    return -ADAM_LR * (m_hat / (jnp.sqrt(v_hat) + ADAM_EPS) + ADAM_WD * w), mm, vv


def _adamw(w, g, m, v, name):
    r, c = w.shape
    tr = max([r // s for s in range(1, r // 8 + 1) if r % s == 0 and (r // s) % 8 == 0 and r // s <= 256] or [r])
    slots = g.ndim == 3

    def body(w_ref, g_ref, m_ref, v_ref, g_out, d_ref, nm_ref, nv_ref):
        if slots:
            gg = g_ref[0].astype(F32)
            for s in range(1, N_DEV):
                gg = gg + g_ref[s].astype(F32)
        else:
            gg = g_ref[...]
        g_out[...] = gg
        d_ref[...], nm_ref[...], nv_ref[...] = _adam_update(w_ref[...], gg, m_ref[...], v_ref[...])

    spec = pl.BlockSpec((tr, c), lambda i: (i, 0))
    g_spec = pl.BlockSpec((N_DEV, tr, c), lambda i: (0, i, 0)) if slots else spec
    return pl.pallas_call(
        body, grid=(r // tr,), in_specs=[spec, g_spec, spec, spec], out_specs=[spec] * 4,
        out_shape=[jax.ShapeDtypeStruct((r, c), F32)] * 4, name=name,
        compiler_params=pltpu.CompilerParams(dimension_semantics=("arbitrary",)),
    )(w, g, m, v)


def _adamw_many(ws, gs, ms, vs, name):
    n = len(ws)

    def body(*refs):
        for i in range(n):
            w_ref, g_ref, m_ref, v_ref = (refs[j * n + i] for j in range(4))
            d_ref, nm_ref, nv_ref = (refs[(4 + j) * n + i] for j in range(3))
            d_ref[...], nm_ref[...], nv_ref[...] = _adam_update(w_ref[...], g_ref[...], m_ref[...], v_ref[...])

    shapes = [jax.ShapeDtypeStruct(w.shape, F32) for w in ws]
    outs = pl.pallas_call(body, out_shape=shapes * 3, name=name)(*ws, *gs, *ms, *vs)
    return outs[:n], outs[n:2 * n], outs[2 * n:]


def _cast_bf16(xs, name, after=None):
    n = len(xs)
    extra = [] if after is None else [after]

    def body(*refs):
        outs = refs[n + len(extra):]
        for i in range(n):
            outs[i][...] = refs[i][...].astype(BF)

    vmem = pl.BlockSpec(memory_space=pltpu.VMEM)
    return pl.pallas_call(
        body, out_shape=[jax.ShapeDtypeStruct(x.shape, BF) for x in xs], name=name,
        in_specs=[vmem] * n + [pl.BlockSpec(memory_space=pl.ANY)] * len(extra), out_specs=[vmem] * n)(*xs, *extra)


def _pad_rows(a, n):
    return jnp.pad(a, ((0, n - a.shape[0]), (0, 0)))


def _w_in_to_padded(wt):
    s_ba = P_CQ
    s_cq = s_ba + 2 * DN_HEADS
    s_kr = s_cq + Q_LORA + KV_LORA
    return jnp.concatenate([wt[:s_ba], wt[s_cq:s_kr], _pad_rows(wt[s_ba:s_cq], LANE), _pad_rows(wt[s_kr:], LANE)], axis=0)


def _w_in_from_padded(wt):
    return jnp.concatenate([wt[:P_CQ], wt[P_BA:P_BA + 2 * DN_HEADS], wt[P_CQ:P_BA], wt[P_KR:P_KR + QK_ROPE]], axis=0)


def _w_uq_to_padded(wt):
    w3 = wt.reshape(MLA_HEADS, QK_NOPE + QK_ROPE, Q_LORA)
    nope = w3[:, :QK_NOPE].reshape(MLA_HEADS * QK_NOPE, Q_LORA)
    rope = jnp.pad(w3[:, QK_NOPE:], ((0, 0), (0, LANE - QK_ROPE), (0, 0))).reshape(MLA_HEADS * LANE, Q_LORA)
    return jnp.concatenate([nope, rope], axis=0)


def _w_uq_from_padded(wt):
    nope = wt[:MLA_HEADS * QK_NOPE].reshape(MLA_HEADS, QK_NOPE, Q_LORA)
    rope = wt[MLA_HEADS * QK_NOPE:].reshape(MLA_HEADS, LANE, Q_LORA)[:, :QK_ROPE]
    return jnp.concatenate([nope, rope], axis=1).reshape(MLA_HEADS * (QK_NOPE + QK_ROPE), Q_LORA)


def _pack(pieces, width, row_mult):
    flat = jnp.concatenate([p.reshape(-1) for p in pieces])
    n = flat.shape[0]
    rows = -(-n // (width * row_mult)) * row_mult
    return jnp.pad(flat, (0, rows * width - n)).reshape(rows, width)


def _unpack(flat, shapes):
    out, o = [], 0
    for s in shapes:
        n = math.prod(s)
        out.append(flat[o:o + n].reshape(s))
        o += n
    return out


def kernel(x, c, positions, w_ada, b_ada, w_in, conv_w, a_log, dt_bias, dn_norm_g, q_norm_g, w_uq, kv_norm_g, w_ukv, w_o, ln1_g, ln1_b, w_gate, w_up, w_down, ln2_g, ln2_b, loss_target, m_w_ada, m_b_ada, m_w_in, m_conv_w, m_a_log, m_dt_bias, m_dn_norm_g, m_q_norm_g, m_w_uq, m_kv_norm_g, m_w_ukv, m_w_o, m_ln1_g, m_ln1_b, m_w_gate, m_w_up, m_w_down, m_ln2_g, m_ln2_b, v_w_ada, v_b_ada, v_w_in, v_conv_w, v_a_log, v_dt_bias, v_dn_norm_g, v_q_norm_g, v_w_uq, v_kv_norm_g, v_w_ukv, v_w_o, v_ln1_g, v_ln1_b, v_w_gate, v_w_up, v_w_down, v_ln2_g, v_ln2_b):
    me = 4 * lax.axis_index("x") + 2 * lax.axis_index("y") + lax.axis_index("c")
    t, d = x.shape[1], x.shape[2]
    ada_n = w_ada.shape[2]

    tr = lambda w: w[0].T
    rows = lambda a: a.reshape(-1, a.shape[2])
    (in_shard,) = _cast_bf16([tr(w_in)], "cast_w_in")
    cw = conv_w.shape[3]
    a_in, c_all, conv_all = _gather_by_chip([in_shard, c, conv_w[0, :, 0, :]], "gather_w_in_and_small")
    c_all = c_all.reshape(N_DEV, d)
    conv_full = conv_all.transpose(1, 0, 2).reshape(CONV_K, N_DEV * cw)
    conv_w8 = jnp.pad(conv_full, ((0, 8 - CONV_K), (0, 0)))

    b_ada_mine = lax.dynamic_slice(b_ada, (0, me * ada_n), (1, ada_n))
    mod_cols = _mod_fwd(c_all, w_ada[0], b_ada_mine)
    (mod_all,) = _exchange([mod_cols.reshape(N_DEV, 1, ada_n)], "scatter_mod", scatter=True)
    mod = mod_all.reshape(1, N_DEV * ada_n)

    later = _cast_bf16([tr(w_uq), tr(w_ukv), w_o[0], tr(w_gate), tr(w_up), w_down[0]], "cast_weights", after=mod)
    mixer_gather, token_a = _exchange_start(later[:3], "gather_mixer_weights_start", scatter=False)
    ffn_gather, token_b = _exchange_start(later[3:], "gather_ffn_weights_start", scatter=False)
    mod = mod + (token_a + token_b)
    w_in_t = _w_in_to_padded(rows(a_in))

    def mixer_weights(after):
        a_uq, a_ukv, a_o = _exchange_wait(mixer_gather, after, "gather_mixer_weights_wait", scatter=False)
        return _w_uq_to_padded(rows(a_uq)), rows(a_ukv), rows(a_o)

    def ffn_weights(after):
        a_gate, a_up, a_down = _exchange_wait(ffn_gather, after, "gather_ffn_weights_wait", scatter=False)
        return rows(a_gate), rows(a_up), rows(a_down)

    def by_dest(g):
        return g.reshape(N_DEV, -1, g.shape[1])

    scatters = {}

    def grads_ready(tag, *g):
        if tag == "ffn":
            pieces = [by_dest(a) for a in g]
        elif tag == "mixer":
            g_w_o, g_w_uq_t, g_w_ukv_t = g
            pieces = [by_dest(g_w_o), by_dest(_w_uq_from_padded(g_w_uq_t).astype(BF)), by_dest(g_w_ukv_t.astype(BF))]
        else:
            pieces = [by_dest(_w_in_from_padded(g[0]))]
        scatters[tag], token = _exchange_start(pieces, "scatter_%s_grads_start" % tag, scatter=True)
        return token

    loc = _local_step(x[0], loss_target[0], positions[0], mod, w_in_t, mixer_weights, ffn_weights, grads_ready,
                      conv_w8, a_log, dt_bias, dn_norm_g, q_norm_g, kv_norm_g, ln1_g, ln1_b, ln2_g, ln2_b)
    grad_x, loss_acc, dmod, d_conv8, d_al8, d_dt8, d_dn_g, d_q_g, d_kv_g, d_ln1_g, d_ln1_b, d_ln2_g, d_ln2_b = loc

    small_shapes = [(6 * d,), (CONV_K, N_DEV * cw), (DN_HEADS,), (DN_HEADS,), (DN_DV,), (Q_LORA,), (KV_LORA,), (d,), (d,), (d,), (d,), (1,)]
    gsmall = _pack([dmod, d_conv8[:CONV_K], d_al8[0, :DN_HEADS], d_dt8[0, :DN_HEADS], d_dn_g, d_q_g, d_kv_g,
                    d_ln1_g, d_ln1_b, d_ln2_g, d_ln2_b, loss_acc[0, :1]], LANE, 8)
    (gsmall_all,) = _exchange([gsmall], "gather_small_grads", scatter=False)
    dmod_all = gsmall_all.reshape(N_DEV, -1)[:, :6 * d]
    tot = _unpack(_sum_slots(gsmall_all, "sum_small_grads").reshape(-1), small_shapes)
    g_b_ada, g_conv_full, g_a_log, g_dt_bias, g_dn_g, g_q_g, g_kv_g, g_ln1_g, g_ln1_b, g_ln2_g, g_ln2_b, loss1 = tot
    loss = loss1.reshape(())
    g_conv_w = lax.dynamic_slice(g_conv_full, (0, me * cw), (CONV_K, cw))
    g_w_ada = _mod_bwd(c_all.T, lax.dynamic_slice(dmod_all, (0, me * ada_n), (N_DEV, ada_n)))

    grads = {"w_ada": g_w_ada[None], "b_ada": g_b_ada[None], "conv_w": g_conv_w[None, :, None, :],
             "a_log": g_a_log[None], "dt_bias": g_dt_bias[None], "dn_norm_g": g_dn_g[None], "q_norm_g": g_q_g[None],
             "kv_norm_g": g_kv_g[None], "ln1_g": g_ln1_g[None], "ln1_b": g_ln1_b[None], "ln2_g": g_ln2_g[None], "ln2_b": g_ln2_b[None]}
    weights = dict(w_ada=w_ada, b_ada=b_ada, w_in=w_in, conv_w=conv_w, a_log=a_log, dt_bias=dt_bias, dn_norm_g=dn_norm_g,
                   q_norm_g=q_norm_g, w_uq=w_uq, kv_norm_g=kv_norm_g, w_ukv=w_ukv, w_o=w_o, ln1_g=ln1_g, ln1_b=ln1_b,
                   w_gate=w_gate, w_up=w_up, w_down=w_down, ln2_g=ln2_g, ln2_b=ln2_b)
    ms = dict(w_ada=m_w_ada, b_ada=m_b_ada, w_in=m_w_in, conv_w=m_conv_w, a_log=m_a_log, dt_bias=m_dt_bias,
              dn_norm_g=m_dn_norm_g, q_norm_g=m_q_norm_g, w_uq=m_w_uq, kv_norm_g=m_kv_norm_g, w_ukv=m_w_ukv, w_o=m_w_o,
              ln1_g=m_ln1_g, ln1_b=m_ln1_b, w_gate=m_w_gate, w_up=m_w_up, w_down=m_w_down, ln2_g=m_ln2_g, ln2_b=m_ln2_b)
    vs = dict(w_ada=v_w_ada, b_ada=v_b_ada, w_in=v_w_in, conv_w=v_conv_w, a_log=v_a_log, dt_bias=v_dt_bias,
              dn_norm_g=v_dn_norm_g, q_norm_g=v_q_norm_g, w_uq=v_w_uq, kv_norm_g=v_kv_norm_g, w_ukv=v_w_ukv, w_o=v_w_o,
              ln1_g=v_ln1_g, ln1_b=v_ln1_b, w_gate=v_w_gate, w_up=v_w_up, w_down=v_w_down, ln2_g=v_ln2_g, ln2_b=v_ln2_b)
    names = list(weights)
    big = ("w_ada", "w_gate", "w_up", "w_down", "w_o", "w_uq", "w_ukv", "w_in")
    waits = {"w_gate": ("ffn", ("w_gate", "w_up", "w_down")), "w_o": ("mixer", ("w_o", "w_uq", "w_ukv")), "w_in": ("in", ("w_in",))}
    delta_w, new_m, new_v, slots = {}, {}, {}, {}
    last = g_w_ada
    for n in big:
        if n == "w_in":
            rest = [r for r in names if r not in big]
            flat2 = lambda a: a.reshape(-1, a.shape[-1])
            outs = _adamw_many(*[[flat2(src[r]) for r in rest] for src in (weights, grads, ms, vs)], "adamw_small")
            for dst, o in zip((delta_w, new_m, new_v), outs):
                for r, a in zip(rest, o):
                    dst[r] = a.reshape(weights[r].shape)
            last = outs[0][0]
        transposed = n in ("w_in", "w_uq", "w_ukv", "w_gate", "w_up")
        two = (lambda a: a[0].T) if transposed else (lambda a: a[0])
        back = (lambda a: a.T[None]) if transposed else (lambda a: a[None])
        if n in waits:
            tag, members = waits[n]
            slots.update(zip(members, _exchange_wait(scatters[tag], last, "scatter_%s_grads_wait" % tag, scatter=True)))
        g_in = slots[n] if n in slots else two(grads[n])
        gr, dlt, nm, nv = _adamw(two(weights[n]), g_in, two(ms[n]), two(vs[n]), "adamw_" + n)
        grads[n], delta_w[n], new_m[n], new_v[n] = back(gr), back(dlt), back(nm), back(nv)
        last = nv

    return (loss, grad_x[None], *[grads[n] for n in names], *[delta_w[n] for n in names],
            *[new_m[n] for n in names], *[new_v[n] for n in names])


def _local_step(xs, tgt, pos, mod, w_in_t, mixer_weights, ffn_weights, grads_ready, conv_w8,
                a_log, dt_bias, dn_norm_g, q_norm_g, kv_norm_g, ln1_g, ln1_b, ln2_g, ln2_b):
    t, d = xs.shape
    sh_m, sc_m, gt_m, sh_f, sc_f, gt_f = [mod[:, i * d:(i + 1) * d] for i in range(6)]
    pos_col = pos.astype(F32).reshape(t, 1)
    inv_freq = 1.0 / (ROPE_THETA ** (jnp.arange(0, QK_ROPE, 2, dtype=F32) / QK_ROPE))
    inv_freq2 = jnp.pad(jnp.concatenate([inv_freq, inv_freq]), (0, LANE - QK_ROPE)).reshape(1, LANE)
    al8 = jnp.pad(a_log, ((0, 7), (0, LANE - DN_HEADS)))
    dt8 = jnp.pad(dt_bias, ((0, 7), (0, LANE - DN_HEADS)))

    tm = min(512, t)
    tq = min(256, t)
    tk = min(512, t)

    (h1,) = _rowwise("modulate_in", lambda xx, sc, sh: xx * (1.0 + sc) + sh, [xs], [sc_m, sh_m], [(d, BF)], [], tm)
    proj = _matmul(h1, w_in_t, "nt", "in_proj", BF)
    ba_raw = _matmul(h1, w_in_t[P_BA:P_BA + LANE], "nt", "in_proj_decay")
    qkv = _conv_fwd(proj, conv_w8, min(256, t))
    gdn_tm = min(512, t)
    intra, inverses = _gdn_intra_fwd(qkv, ba_raw, al8, dt8, gdn_tm)
    o_dn, states = _gdn_scan_fwd(intra, gdn_tm)
    w_uq_t, w_ukv_t, w_o_f = mixer_weights(states)
    qc, kc, vc = _mla_prep_fwd(proj, pos_col, inv_freq2, q_norm_g, kv_norm_g, w_uq_t, w_ukv_t, tm)
    o_mla, lse = _attn_fwd(qc, kc, vc, min(1024, t), min(1024, t))

    def mix_in(o, z, om, g):
        return jnp.concatenate(_gdn_out(o, z.astype(F32), g) + [om], axis=1)

    (mixin,) = _rowwise("mixer_out", mix_in, [o_dn, (proj, DN_VW, P_Z // DN_VW), o_mla], [dn_norm_g], [(2 * DN_VW, BF)], [], tm)
    mix = _matmul(mixin, w_o_f, "nn", "out_proj", BF)

    def block1(xx, mx, gt, g1, b1, sc, sh):
        x1 = _layernorm(DEEPNORM_ALPHA * xx + gt * mx, g1, b1)
        return x1, x1 * (1.0 + sc) + sh

    x1, h2 = _rowwise("norm1_modulate", block1, [xs, mix], [gt_m, ln1_g, ln1_b, sc_f, sh_f], [(d, F32), (d, BF)], [], tm)
    w_gate_f, w_up_f, w_down_f = ffn_weights(h2)
    act, act_dg, act_du = _ffn_in(h2, w_gate_f, w_up_f)
    ff = _matmul(act, w_down_f, "nn", "ffn_out", BF)

    def tail_loss(x1_, ff_, gt, g2, b2, tg):
        y = _layernorm(DEEPNORM_ALPHA * x1_ + gt * ff_, g2, b2)
        return 0.5 * jnp.sum(jnp.mean(jnp.square(y - tg), axis=-1))

    def tail(x1_, ff_, tg, gt, g2, b2):
        loss, (dx1, dff, dgt, dg2, db2) = jax.value_and_grad(tail_loss, argnums=(0, 1, 2, 3, 4))(x1_, ff_, gt, g2, b2, tg)
        return dx1, dff, jnp.full((1, LANE), loss, F32), dgt, dg2, db2

    dx1_a, dff, loss_acc, d_gt_f, d_ln2_g, d_ln2_b = _rowwise(
        "norm2_loss", tail, [x1, ff, tgt], [gt_f, ln2_g, ln2_b], [(d, BF), (d, BF)], [(1, LANE), (1, d), (1, d), (1, d)], tm)

    g_w_down = _matmul(act, dff, "tn", "d_w_down", BF)
    dgate, dup = _ffn_act_bwd(dff, w_down_f, act_dg, act_du)
    g_w_gate = _matmul(dgate, h2, "tn", "d_w_gate", BF)
    g_w_up = _matmul(dup, h2, "tn", "d_w_up", BF)
    token = grads_ready("ffn", g_w_gate, g_w_up, g_w_down)
    dh2 = _matmul2_nn(dgate, w_gate_f, dup, w_up_f, "d_ffn_in", BF)

    def block1_bwd(xx, mx, dx1_, dh2_, gt, g1, b1, sc, sh):
        _, vjp = jax.vjp(block1, xx, mx, gt, g1, b1, sc, sh)
        dxx, dmx, dgt, dg1, db1, dsc, dsh = vjp((dx1_.astype(F32), dh2_.astype(F32)))
        return dxx, dmx, dgt, dg1, db1, dsc, dsh

    dx_a, dmix, d_gt_m, d_ln1_g, d_ln1_b, d_sc_f, d_sh_f = _rowwise(
        "norm1_modulate_bwd", block1_bwd, [xs, mix, dx1_a, dh2], [gt_m + token, ln1_g, ln1_b, sc_f, sh_f],
        [(d, F32), (d, BF)], [(1, d)] * 5, min(256, t))

    dmixin = _matmul(dmix, w_o_f, "nt", "d_mixer_out", BF)
    g_w_o = _matmul(mixin, dmix, "tn", "d_w_o", BF)

    def mixer_bwd(o, z, om, dmi, g):
        _, vjp = jax.vjp(lambda o_, z_, g_: jnp.concatenate(_gdn_out(o_, z_, g_), axis=1), o, z.astype(F32), g)
        do_, dz_, dg_ = vjp(dmi[:, :DN_VW].astype(F32))
        dom = dmi[:, DN_VW:]
        delta = [jnp.broadcast_to(jnp.sum(dom[:, h * V_HEAD:(h + 1) * V_HEAD] * om[:, h * V_HEAD:(h + 1) * V_HEAD], axis=-1, keepdims=True), (o.shape[0], LANE))
                 for h in range(MLA_HEADS)]
        return do_, dz_, dom, jnp.concatenate(delta, axis=1), dg_

    do_dn, dz, do_mla, delta, d_dn_g = _rowwise(
        "mixer_out_bwd", mixer_bwd, [o_dn, (proj, DN_VW, P_Z // DN_VW), o_mla, dmixin], [dn_norm_g],
        [(DN_VW, F32), (DN_VW, BF), (MLA_VW, BF), (MLA_HEADS * LANE, F32)], [(1, DN_DV)], tm)

    dqc, dkc, dvc = _attn_bwd(qc, kc, vc, do_mla, lse, delta, min(512, t), min(1024, t))
    dcq, dckv, dkr, d_q_g, d_kv_g, g_w_uq_t, g_w_ukv_t = _mla_prep_bwd(
        proj, pos_col, inv_freq2, q_norm_g, kv_norm_g, w_uq_t, w_ukv_t, dqc, dkc, dvc, min(256, t))

    token = grads_ready("mixer", g_w_o, g_w_uq_t, g_w_ukv_t)

    d_intra = _gdn_scan_bwd(intra, states, do_dn, gdn_tm)
    dqkv_act, dba, d_al8, d_dt8 = _gdn_intra_bwd(qkv, ba_raw, al8 + token, dt8, inverses, d_intra, min(256, t))
    dproj, d_conv8 = _conv_bwd(proj, conv_w8, dqkv_act, [(P_Z, dz), (P_CQ, dcq), (P_CKV, dckv), (P_BA, dba), (P_KR, dkr)],
                               min(256, t))
    dh1 = _matmul(dproj, w_in_t, "nn", "d_in_proj", BF)
    g_w_in_t = _matmul(dproj, h1, "tn", "d_w_in", BF)
    token = grads_ready("in", g_w_in_t)

    def modulate_bwd(xx, dh, dxa, sc):
        dh = dh.astype(F32)
        return dh * (1.0 + sc) + dxa, jnp.sum(dh * xx, axis=0, keepdims=True), jnp.sum(dh, axis=0, keepdims=True)

    grad_x, d_sc_m, d_sh_m = _rowwise("modulate_in_bwd", modulate_bwd, [xs, dh1, dx_a], [sc_m + token], [(d, F32)], [(1, d), (1, d)], tm)
    dmod = jnp.concatenate([d_sh_m, d_sc_m, d_gt_m, d_sh_f, d_sc_f, d_gt_f], axis=1)
    return grad_x, loss_acc, dmod, d_conv8, d_al8, d_dt8, d_dn_g, d_q_g, d_kv_g, d_ln1_g, d_ln1_b, d_ln2_g, d_ln2_b
```

```python
import functools
import math

import jax
import jax.numpy as jnp
from jax import lax
from jax.experimental import pallas as pl
from jax.experimental.pallas import tpu as pltpu

F32 = jnp.float32
BF = jnp.bfloat16
HI = lax.Precision.HIGHEST

N_DEV = 8
DN_HEADS = 4
DN_DK = 128
DN_DV = 128
CONV_K = 4
CHUNK = 64
MLA_HEADS = 4
QK_NOPE = 128
QK_ROPE = 64
V_HEAD = 128
Q_LORA = 512
KV_LORA = 256
ROPE_THETA = 10000.0
DEPTH = 1
DEEPNORM_ALPHA = (2.0 * DEPTH) ** 0.25
LANE = 128
CONV_HALO = 8
GL_ROWS = 8
CONV_ROWS, CONV_COLS = 64, 256

DN_QK = DN_HEADS * DN_DK
DN_VW = DN_HEADS * DN_DV
DN_CONV_CH = 2 * DN_QK + DN_VW
MLA_VW = MLA_HEADS * V_HEAD
P_Z = DN_CONV_CH
P_CQ = P_Z + DN_VW
P_CKV = P_CQ + Q_LORA
P_BA = P_CKV + KV_LORA
P_KR = P_BA + LANE
N_INP = P_KR + LANE

ADAM_LR = 0.001
ADAM_B1 = 0.9
ADAM_B2 = 0.999
ADAM_EPS = 1e-08
ADAM_WD = 0.01
ADAM_STEP = 10

NN = (((1,), (0,)), ((), ()))
NT = (((1,), (1,)), ((), ()))
TN = (((0,), (0,)), ((), ()))


def _pick(n, prefs):
    for p in prefs:
        if n % p == 0:
            return p
    return n


def _full(shape):
    return pl.BlockSpec(shape, lambda *_: (0,) * len(shape))


def _dot(a, b, dims=NN):
    return lax.dot_general(a, b, dims, preferred_element_type=F32)


def _doth(a, b, dims=NN):
    return lax.dot_general(a, b, dims, precision=HI, preferred_element_type=F32)


@jax.custom_vjp
def _mmb(a, b):
    return _dot(a.astype(BF), b.astype(BF), NN)


def _mmb_fwd(a, b):
    return _mmb(a, b), (a, b)


def _mmb_bwd(res, g):
    a, b = res
    gb = g.astype(BF)
    return (_dot(gb, b.astype(BF), NT).astype(a.dtype), _dot(a.astype(BF), gb, TN).astype(b.dtype))


_mmb.defvjp(_mmb_fwd, _mmb_bwd)


@jax.custom_vjp
def _mmb_nt(a, b):
    return _dot(a.astype(BF), b.astype(BF), NT)


def _mmb_nt_fwd(a, b):
    return _mmb_nt(a, b), (a, b)


def _mmb_nt_bwd(res, g):
    a, b = res
    gb = g.astype(BF)
    return (_dot(gb, b.astype(BF), NN).astype(a.dtype), _dot(gb, a.astype(BF), TN).astype(b.dtype))


_mmb_nt.defvjp(_mmb_nt_fwd, _mmb_nt_bwd)


def _sigmoid(x):
    return 0.5 * (jnp.tanh(0.5 * x) + 1.0)


def _silu(x):
    return x * _sigmoid(x)


def _softplus(x):
    return jnp.maximum(x, 0.0) + jnp.log(1.0 + jnp.exp(-jnp.abs(x)))


def _layernorm(x, g, b, eps=1e-5):
    mu = jnp.mean(x, axis=-1, keepdims=True)
    xc = x - mu
    var = jnp.mean(xc * xc, axis=-1, keepdims=True)
    return xc * lax.rsqrt(var + eps) * g + b


def _rmsnorm(x, g, eps=1e-6):
    return x * lax.rsqrt(jnp.mean(x * x, axis=-1, keepdims=True) + eps) * g


def _l2norm(x, eps=1e-6):
    return x * lax.rsqrt(jnp.sum(x * x, axis=-1, keepdims=True) + eps)


def _rowwise(name, fn, rows, vecs, out_rows, out_accs, tm):
    rows = [r if isinstance(r, tuple) else (r, r.shape[1], 0) for r in rows]
    t = rows[0][0].shape[0]
    tm = min(tm, t)
    assert t % tm == 0
    nr, nv, no = len(rows), len(vecs), len(out_rows)

    def body(*refs):
        ins = [r[...] for r in refs[:nr + nv]]
        outs = fn(*ins)
        outs = outs if isinstance(outs, (tuple, list)) else (outs,)
        o_rows = refs[nr + nv:nr + nv + no]
        o_accs = refs[nr + nv + no:]
        for o, val in zip(o_rows, outs[:no]):
            o[...] = val.astype(o.dtype)
        if o_accs:
            @pl.when(pl.program_id(0) == 0)
            def _():
                for o in o_accs:
                    o[...] = jnp.zeros_like(o)
            for o, val in zip(o_accs, outs[no:]):
                o[...] += val

    in_specs = [pl.BlockSpec((tm, w), functools.partial(lambda i, j: (i, j), j=j)) for (_, w, j) in rows]
    in_specs += [_full(v.shape) for v in vecs]
    out_specs = [pl.BlockSpec((tm, w), lambda i: (i, 0)) for (w, _) in out_rows]
    out_specs += [_full(s) for s in out_accs]
    out_shape = [jax.ShapeDtypeStruct((t, w), d) for (w, d) in out_rows]
    out_shape += [jax.ShapeDtypeStruct(s, F32) for s in out_accs]
    res = pl.pallas_call(
        body, grid=(t // tm,), in_specs=in_specs, out_specs=out_specs, out_shape=out_shape, name=name,
        compiler_params=pltpu.CompilerParams(dimension_semantics=("arbitrary",)),
    )(*[r[0] for r in rows], *vecs)
    return res


def _matmul(a, b, mode, name, out_dtype=F32):
    if mode == "nn":
        (m, k), n = a.shape, b.shape[1]
    elif mode == "nt":
        (m, k), n = a.shape, b.shape[0]
    else:
        (k, m), n = a.shape, b.shape[1]
    tm, tn, tk = _matmul_tiles(m, n, k, a.dtype.itemsize, b.dtype.itemsize, jnp.dtype(out_dtype).itemsize)
    nk = k // tk
    dims = {"nn": NN, "nt": NT, "tn": TN}[mode]

    def body(a_ref, b_ref, o_ref, *acc):
        part = _dot(a_ref[...].astype(BF), b_ref[...].astype(BF), dims)
        if nk == 1:
            o_ref[...] = part.astype(o_ref.dtype)
            return
        (acc_ref,) = acc
        kk = pl.program_id(2)

        @pl.when(kk == 0)
        def _():
            acc_ref[...] = part

        @pl.when(kk > 0)
        def _():
            acc_ref[...] += part

        @pl.when(kk == nk - 1)
        def _():
            o_ref[...] = acc_ref[...].astype(o_ref.dtype)

    a_spec = pl.BlockSpec((tk, tm), lambda i, j, kk: (kk, i)) if mode == "tn" else pl.BlockSpec((tm, tk), lambda i, j, kk: (i, kk))
    b_spec = pl.BlockSpec((tn, tk), lambda i, j, kk: (j, kk)) if mode == "nt" else pl.BlockSpec((tk, tn), lambda i, j, kk: (kk, j))
    return pl.pallas_call(
        body, grid=(m // tm, n // tn, nk), in_specs=[a_spec, b_spec],
        out_specs=pl.BlockSpec((tm, tn), lambda i, j, kk: (i, j)),
        out_shape=jax.ShapeDtypeStruct((m, n), out_dtype),
        scratch_shapes=[pltpu.VMEM((tm, tn), F32)] if nk > 1 else [], name=name,
        compiler_params=pltpu.CompilerParams(dimension_semantics=("parallel", "parallel", "arbitrary")),
    )(a, b)


def _lane_tile(n, cap):
    return max([n // s for s in range(1, n // LANE + 1) if n % s == 0 and (n // s) % LANE == 0 and n // s <= cap] or [n])


def _ffn_in(h, w_gate, w_up):
    m, k = h.shape
    f = w_gate.shape[0]
    tm, tn = _pick(m, (1024, 512, 256, 128)), _lane_tile(f, 1408)

    def body(h_ref, wg_ref, wu_ref, act_ref, dg_ref, du_ref):
        hh = h_ref[...]
        g = _dot(hh, wg_ref[...], NT)
        u = _dot(hh, wu_ref[...], NT)
        sg = _sigmoid(g)
        silu_g = g * sg
        act_ref[...] = (silu_g * u).astype(act_ref.dtype)
        dg_ref[...] = (u * (sg + silu_g * (1.0 - sg))).astype(dg_ref.dtype)
        du_ref[...] = silu_g.astype(du_ref.dtype)

    w_spec = pl.BlockSpec((tn, k), lambda i, j: (j, 0))
    o_spec = pl.BlockSpec((tm, tn), lambda i, j: (i, j))
    return pl.pallas_call(
        body, grid=(m // tm, f // tn), in_specs=[pl.BlockSpec((tm, k), lambda i, j: (i, 0)), w_spec, w_spec],
        out_specs=[o_spec] * 3, out_shape=[jax.ShapeDtypeStruct((m, f), BF)] * 3, name="ffn_in",
        compiler_params=pltpu.CompilerParams(dimension_semantics=("parallel", "parallel")),
    )(h, w_gate, w_up)


def _ffn_act_bwd(dff, w_down, act_dg, act_du):
    m, k = dff.shape
    f = w_down.shape[0]
    tm, tn = _pick(m, (1024, 512, 256, 128)), _lane_tile(f, 1408)

    def body(d_ref, w_ref, fg_ref, fu_ref, dg_ref, du_ref):
        da = _dot(d_ref[...], w_ref[...], NT)
        dg_ref[...] = (da * fg_ref[...].astype(F32)).astype(dg_ref.dtype)
        du_ref[...] = (da * fu_ref[...].astype(F32)).astype(du_ref.dtype)

    o_spec = pl.BlockSpec((tm, tn), lambda i, j: (i, j))
    return pl.pallas_call(
        body, grid=(m // tm, f // tn),
        in_specs=[pl.BlockSpec((tm, k), lambda i, j: (i, 0)), pl.BlockSpec((tn, k), lambda i, j: (j, 0)), o_spec, o_spec],
        out_specs=[o_spec] * 2, out_shape=[jax.ShapeDtypeStruct((m, f), BF)] * 2, name="d_ffn_act",
        compiler_params=pltpu.CompilerParams(dimension_semantics=("parallel", "parallel")),
    )(dff, w_down, act_dg, act_du)


def _matmul2_nn(a1, b1, a2, b2, name, out_dtype=F32):
    m, k = a1.shape
    n = b1.shape[1]
    tm, tn = _pick(m, (1024, 512, 256, 128)), _pick(n, (512, 256, 128))

    def body(a1_ref, b1_ref, a2_ref, b2_ref, o_ref):
        o_ref[...] = (_dot(a1_ref[...], b1_ref[...]) + _dot(a2_ref[...], b2_ref[...])).astype(o_ref.dtype)

    a_spec = pl.BlockSpec((tm, k), lambda i, j: (i, 0))
    b_spec = pl.BlockSpec((k, tn), lambda i, j: (0, j))
    return pl.pallas_call(
        body, grid=(m // tm, n // tn), in_specs=[a_spec, b_spec, a_spec, b_spec],
        out_specs=pl.BlockSpec((tm, tn), lambda i, j: (i, j)), out_shape=jax.ShapeDtypeStruct((m, n), out_dtype), name=name,
        compiler_params=pltpu.CompilerParams(dimension_semantics=("parallel", "parallel")),
    )(a1, b1, a2, b2)


MATMUL_VMEM_BUDGET = 28 * 1024 * 1024


def _matmul_tiles(m, n, k, a_bytes, b_bytes, o_bytes):
    def divisors(x, cap):
        return sorted({x // s for s in range(1, 65) if x % s == 0 and (x // s) % LANE == 0 and x // s <= cap}, reverse=True) or [x]

    for tk in divisors(k, k):
        best = None
        for tm in divisors(m, 1024):
            for tn in divisors(n, 2048):
                need = 2 * (tm * tk * a_bytes + tk * tn * b_bytes + tm * tn * o_bytes) + (tm * tn * 4 if tk < k else 0)
                if need <= MATMUL_VMEM_BUDGET and tm * tn >= 512 * 512 and (best is None or tm * tn > best[0] * best[1]):
                    best = (tm, tn)
        if best:
            return best[0], best[1], tk
    return _pick(m, (512, 256, 128)), _pick(n, (512, 256, 128)), _pick(k, (512, 256, 128))


def _exchange(xs, name, scatter):
    n = len(xs)
    npeer = N_DEV - 1

    def body(*refs):
        x_refs, o_refs = refs[:n], refs[n:2 * n]
        send_sems, recv_sems, local_sems = refs[2 * n:]
        mx, my, mc = lax.axis_index("x"), lax.axis_index("y"), lax.axis_index("c")
        me = 4 * mx + 2 * my + mc
        src_me = [x.at[me] if scatter else x for x in x_refs]
        mine = [pltpu.make_async_copy(src_me[a], o_refs[a].at[me], local_sems.at[a]) for a in range(n)]
        for cp in mine:
            cp.start()
        copies = []
        for k in range(1, N_DEV):
            px, py, pc = mx ^ (k >> 2), my ^ ((k >> 1) & 1), mc ^ (k & 1)
            peer = 4 * px + 2 * py + pc
            for a in range(n):
                cp = pltpu.make_async_remote_copy(
                    src_ref=x_refs[a].at[peer] if scatter else x_refs[a], dst_ref=o_refs[a].at[me],
                    send_sem=send_sems.at[a * npeer + k - 1], recv_sem=recv_sems.at[a * npeer + k - 1],
                    device_id=(px, py, pc), device_id_type=pl.DeviceIdType.MESH)
                cp.start()
                copies.append((cp, a, k, peer))
        for cp, a, k, peer in copies:
            pltpu.make_async_remote_copy(
                src_ref=src_me[a], dst_ref=o_refs[a].at[peer], send_sem=send_sems.at[a * npeer + k - 1],
                recv_sem=recv_sems.at[a * npeer + k - 1], device_id=(mx, my, mc),
                device_id_type=pl.DeviceIdType.MESH).wait_recv()
        for cp, _, _, _ in copies:
            cp.wait_send()
        for cp in mine:
            cp.wait()

    return pl.pallas_call(
        body, out_shape=[jax.ShapeDtypeStruct((N_DEV,) + x.shape[-2:], x.dtype) for x in xs],
        in_specs=[pl.BlockSpec(memory_space=pl.ANY)] * n, out_specs=[pl.BlockSpec(memory_space=pl.ANY)] * n,
        scratch_shapes=[pltpu.SemaphoreType.DMA((n * npeer,)), pltpu.SemaphoreType.DMA((n * npeer,)),
                        pltpu.SemaphoreType.DMA((n,))],
        name=name,
    )(*xs)


def _gather_by_chip(xs, name):
    n = len(xs)
    per = N_DEV - 1

    def body(*refs):
        x_refs, o_refs = refs[:n], refs[n:2 * n]
        send_sems, recv_sems, local_sems = refs[2 * n:]
        mx, my, mc = lax.axis_index("x"), lax.axis_index("y"), lax.axis_index("c")
        me, sibling = (mx, my, mc), (mx, my, 1 - mc)
        chips = [(1 - mx, my), (mx, 1 - my), (1 - mx, 1 - my)]
        slot = lambda d: 4 * d[0] + 2 * d[1] + d[2]

        def copy(a, k, block, to, src=None):
            dst = o_refs[a].at[slot(block)]
            return pltpu.make_async_remote_copy(
                src_ref=dst if src is None else src, dst_ref=dst, send_sem=send_sems.at[a * per + k],
                recv_sem=recv_sems.at[a * per + k], device_id=to, device_id_type=pl.DeviceIdType.MESH)

        mine = [pltpu.make_async_copy(x_refs[a], o_refs[a].at[slot(me)], local_sems.at[a]) for a in range(n)]
        for cp in mine:
            cp.start()
        first = []
        for a in range(n):
            first.append(copy(a, 0, me, sibling, src=x_refs[a]))
            first += [copy(a, 1 + j, me, (*chip, mc), src=x_refs[a]) for j, chip in enumerate(chips)]
        for cp in first:
            cp.start()
        passed = []
        for j, chip in enumerate(chips):
            for a in range(n):
                copy(a, 1 + j, (*chip, mc), me).wait_recv()
                cp = copy(a, 4 + j, (*chip, mc), sibling)
                cp.start()
                passed.append(cp)
        for a in range(n):
            copy(a, 0, sibling, me).wait_recv()
            for j, chip in enumerate(chips):
                copy(a, 4 + j, (*chip, 1 - mc), me).wait_recv()
        for cp in first + passed:
            cp.wait_send()
        for cp in mine:
            cp.wait()

    return pl.pallas_call(
        body, out_shape=[jax.ShapeDtypeStruct((N_DEV,) + x.shape, x.dtype) for x in xs],
        in_specs=[pl.BlockSpec(memory_space=pl.ANY)] * n, out_specs=[pl.BlockSpec(memory_space=pl.ANY)] * n,
        scratch_shapes=[pltpu.SemaphoreType.DMA((n * per,)), pltpu.SemaphoreType.DMA((n * per,)),
                        pltpu.SemaphoreType.DMA((n,))],
        name=name,
    )(*xs)


def _peer_of(k):
    mx, my, mc = lax.axis_index("x"), lax.axis_index("y"), lax.axis_index("c")
    px, py, pc = mx ^ (k >> 2), my ^ ((k >> 1) & 1), mc ^ (k & 1)
    return (px, py, pc), 4 * px + 2 * py + pc


def _exchange_start(xs, name, scatter):
    n = len(xs)
    npeer = N_DEV - 1

    def body(*refs):
        x_refs, land_refs = refs[:n], refs[n:2 * n]
        send_sems, recv_sems, token = refs[2 * n], refs[2 * n + 1], refs[-1]
        me = 4 * lax.axis_index("x") + 2 * lax.axis_index("y") + lax.axis_index("c")
        for k in range(1, N_DEV):
            dev, peer = _peer_of(k)
            for a in range(n):
                pltpu.make_async_remote_copy(
                    src_ref=x_refs[a].at[peer] if scatter else x_refs[a], dst_ref=land_refs[a].at[me],
                    send_sem=send_sems.at[a * npeer + k - 1], recv_sem=recv_sems.at[a * npeer + k - 1],
                    device_id=dev, device_id_type=pl.DeviceIdType.MESH).start()
        token[...] = jnp.zeros_like(token)

    hbm = pl.BlockSpec(memory_space=pltpu.HBM)
    sem = pl.BlockSpec(memory_space=pltpu.SEMAPHORE)
    lands = [pltpu.with_memory_space_constraint(lax.empty((N_DEV,) + x.shape[-2:], x.dtype), pltpu.HBM) for x in xs]
    srcs = [pltpu.with_memory_space_constraint(x, pltpu.HBM) for x in xs]
    outs = pl.pallas_call(
        body, name=name,
        out_shape=(pltpu.SemaphoreType.DMA((n * npeer,)), pltpu.SemaphoreType.DMA((n * npeer,)),
                   *[pltpu.HBM(x.shape, x.dtype) for x in srcs], *[pltpu.HBM(z.shape, z.dtype) for z in lands],
                   jax.ShapeDtypeStruct((8, LANE), F32)),
        in_specs=[hbm] * (2 * n), out_specs=(sem, sem, *[hbm] * (2 * n), pl.BlockSpec(memory_space=pltpu.VMEM)),
        input_output_aliases={i: 2 + i for i in range(2 * n)},
        compiler_params=pltpu.CompilerParams(has_side_effects=pltpu.SideEffectType.DATAFLOW_SIDE_EFFECTING),
    )(*srcs, *lands)
    return (outs[0], outs[1], list(outs[2:2 + n]), list(outs[2 + n:2 + 2 * n])), outs[-1][0:1, 0:1]


def _exchange_wait(started, after, name, scatter):
    send_sems, recv_sems, srcs, lands = started
    n = len(srcs)
    npeer = N_DEV - 1

    def body(*refs):
        x_refs, land_refs = refs[:n], refs[n:2 * n]
        send_sems, recv_sems = refs[2 * n], refs[2 * n + 1]
        mx, my, mc = lax.axis_index("x"), lax.axis_index("y"), lax.axis_index("c")
        me = 4 * mx + 2 * my + mc
        for k in range(1, N_DEV):
            _, peer = _peer_of(k)
            for a in range(n):
                src = x_refs[a].at[me] if scatter else x_refs[a]
                cp = pltpu.make_async_remote_copy(
                    src_ref=src, dst_ref=land_refs[a].at[peer], send_sem=send_sems.at[a * npeer + k - 1],
                    recv_sem=recv_sems.at[a * npeer + k - 1], device_id=(mx, my, mc), device_id_type=pl.DeviceIdType.MESH)
                cp.wait_send()
                cp.wait_recv()

    hbm = pl.BlockSpec(memory_space=pltpu.HBM)
    sem = pl.BlockSpec(memory_space=pltpu.SEMAPHORE)
    outs = pl.pallas_call(
        body, name=name,
        out_shape=(*[pltpu.HBM(x.shape, x.dtype) for x in srcs], *[pltpu.HBM(z.shape, z.dtype) for z in lands]),
        in_specs=[hbm] * (2 * n) + [sem, sem, pl.BlockSpec(memory_space=pl.ANY)], out_specs=tuple([hbm] * (2 * n)),
        input_output_aliases={i: i for i in range(2 * n)},
        compiler_params=pltpu.CompilerParams(has_side_effects=pltpu.SideEffectType.DATAFLOW_SIDE_EFFECTING),
    )(*srcs, *lands, send_sems, recv_sems, after)
    me = 4 * lax.axis_index("x") + 2 * lax.axis_index("y") + lax.axis_index("c")
    full = []
    for x, land in zip(outs[:n], outs[n:]):
        own = lax.dynamic_slice(x, (me, 0, 0), (1,) + x.shape[1:]) if scatter else x[None]
        full.append(lax.dynamic_update_slice(land, own, (me, 0, 0)))
    return full


def _sum_slots(x, name):
    _, r, c = x.shape
    tr = _pick(r, (512, 256, 128, 64, 32, 16))

    def body(x_ref, o_ref):
        acc = x_ref[0].astype(F32)
        for s in range(1, N_DEV):
            acc = acc + x_ref[s].astype(F32)
        o_ref[...] = acc

    return pl.pallas_call(
        body, grid=(r // tr,), in_specs=[pl.BlockSpec((N_DEV, tr, c), lambda i: (0, i, 0))],
        out_specs=pl.BlockSpec((tr, c), lambda i: (i, 0)), out_shape=jax.ShapeDtypeStruct((r, c), F32), name=name,
        compiler_params=pltpu.CompilerParams(dimension_semantics=("arbitrary",)),
    )(x)


def _mod_fwd(c_all, w_ada, b_ada_mine):
    def body(c_ref, w_ref, b_ref, o_ref):
        o_ref[...] = _doth(_silu(c_ref[...]), w_ref[...]) + b_ref[...]

    return pl.pallas_call(body, out_shape=jax.ShapeDtypeStruct((c_all.shape[0], w_ada.shape[1]), F32), name="mod_fwd")(c_all, w_ada, b_ada_mine)


def _mod_bwd(c_all_t, dmod_mine):
    def body(ct_ref, d_ref, o_ref):
        s = _silu(ct_ref[...])
        acc = s[:, 0:1] * d_ref[pl.ds(0, 1), :]
        for b in range(1, N_DEV):
            acc = acc + s[:, b:b + 1] * d_ref[pl.ds(b, 1), :]
        o_ref[...] = acc

    return pl.pallas_call(body, out_shape=jax.ShapeDtypeStruct((c_all_t.shape[0], dmod_mine.shape[1]), F32), name="mod_bwd")(c_all_t, dmod_mine)


def _conv_fwd(proj, conv_w8, tm):
    t = proj.shape[0]
    ch = DN_CONV_CH

    def body(x_ref, w_ref, o_ref, buf):
        @pl.when(pl.program_id(0) == 0)
        def _():
            buf[pl.ds(0, CONV_HALO), :] = jnp.zeros((CONV_HALO, ch), F32)

        buf[pl.ds(CONV_HALO, tm), :] = x_ref[...].astype(F32)
        for c0 in range(0, ch, CONV_COLS):
            cols = pl.ds(c0, CONV_COLS)
            w = [w_ref[pl.ds(j, 1), cols] for j in range(CONV_K)]
            for r0 in range(0, tm, CONV_ROWS):
                acc = buf[pl.ds(r0 + CONV_HALO - (CONV_K - 1), CONV_ROWS), cols] * w[0]
                for j in range(1, CONV_K):
                    acc = acc + buf[pl.ds(r0 + CONV_HALO - (CONV_K - 1) + j, CONV_ROWS), cols] * w[j]
                o_ref[pl.ds(r0, CONV_ROWS), cols] = _silu(acc)
        buf[pl.ds(0, CONV_HALO), :] = buf[pl.ds(tm, CONV_HALO), :]

    return pl.pallas_call(
        body, grid=(t // tm,), in_specs=[pl.BlockSpec((tm, ch), lambda i: (i, 0)), _full(conv_w8.shape)],
        out_specs=pl.BlockSpec((tm, ch), lambda i: (i, 0)), out_shape=jax.ShapeDtypeStruct((t, ch), F32),
        scratch_shapes=[pltpu.VMEM((tm + CONV_HALO, ch), F32)], name="conv_fwd",
        compiler_params=pltpu.CompilerParams(dimension_semantics=("arbitrary",)),
    )(proj, conv_w8)


def _conv_bwd(proj, conv_w8, dact, others, tm):
    t = proj.shape[0]
    ch = DN_CONV_CH
    nt = t // tm
    halo_blk = 2 * CONV_HALO
    hb = tm // halo_blk
    n_others = len(others)

    def body(x_ref, xp_ref, w_ref, dy_ref, *refs):
        piece_refs, (dx_ref, dw_ref, xbuf, dbuf) = refs[:n_others], refs[n_others:]
        step = pl.program_id(0)
        for (off, arr), p_ref in zip(others, piece_refs):
            dx_ref[:, pl.ds(off, arr.shape[1])] = p_ref[...].astype(dx_ref.dtype)

        @pl.when(step == 0)
        def _():
            dbuf[pl.ds(tm, CONV_HALO), :] = jnp.zeros((CONV_HALO, ch), F32)
            dw_ref[...] = jnp.zeros_like(dw_ref)

        first = step == nt - 1
        xbuf[pl.ds(0, CONV_HALO), :] = jnp.where(first, 0.0, xp_ref[...].astype(F32)[halo_blk - CONV_HALO:])
        xbuf[pl.ds(CONV_HALO, tm), :] = x_ref[...].astype(F32)
        for c0 in range(0, ch, CONV_COLS):
            cols = pl.ds(c0, CONV_COLS)
            w = [w_ref[pl.ds(j, 1), cols] for j in range(CONV_K)]
            dw = [jnp.zeros((1, CONV_COLS), F32) for _ in range(CONV_K)]
            for r0 in range(0, tm, CONV_ROWS):
                xs = [xbuf[pl.ds(r0 + CONV_HALO - (CONV_K - 1) + j, CONV_ROWS), cols] for j in range(CONV_K)]
                pre = xs[0] * w[0]
                for j in range(1, CONV_K):
                    pre = pre + xs[j] * w[j]
                sg = _sigmoid(pre)
                dpre = dy_ref[pl.ds(r0, CONV_ROWS), cols] * (sg * (1.0 + pre * (1.0 - sg)))
                dbuf[pl.ds(r0, CONV_ROWS), cols] = dpre
                dw = [dw[j] + jnp.sum(dpre * xs[j], axis=0, keepdims=True) for j in range(CONV_K)]
            for j in range(CONV_K):
                dw_ref[pl.ds(j, 1), cols] += dw[j]
            for r0 in range(0, tm, CONV_ROWS):
                dx = dbuf[pl.ds(r0 + CONV_K - 1, CONV_ROWS), cols] * w[0]
                for j in range(1, CONV_K):
                    dx = dx + dbuf[pl.ds(r0 + CONV_K - 1 - j, CONV_ROWS), cols] * w[j]
                dx_ref[pl.ds(r0, CONV_ROWS), cols] = dx.astype(dx_ref.dtype)
        dbuf[pl.ds(tm, CONV_HALO), :] = dbuf[pl.ds(0, CONV_HALO), :]

    rev = lambda i: (nt - 1 - i, 0)
    prev = lambda i: (jnp.maximum((nt - 1 - i) * hb - 1, 0), 0)
    return pl.pallas_call(
        body, grid=(nt,),
        in_specs=[pl.BlockSpec((tm, ch), rev), pl.BlockSpec((halo_blk, ch), prev), _full(conv_w8.shape),
                  pl.BlockSpec((tm, ch), rev)] + [pl.BlockSpec((tm, arr.shape[1]), rev) for _, arr in others],
        out_specs=[pl.BlockSpec((tm, N_INP), rev), _full(conv_w8.shape)],
        out_shape=[jax.ShapeDtypeStruct((t, N_INP), BF), jax.ShapeDtypeStruct(conv_w8.shape, F32)],
        scratch_shapes=[pltpu.VMEM((tm + CONV_HALO, ch), F32), pltpu.VMEM((tm + CONV_HALO, ch), F32)], name="conv_bwd",
        compiler_params=pltpu.CompilerParams(dimension_semantics=("arbitrary",)),
    )(proj, proj, conv_w8, dact, *[arr for _, arr in others])


BNN = (((2,), (1,)), ((0,), (0,)))
BNT = (((2,), (2,)), ((0,), (0,)))
BTN = (((1,), (1,)), ((0,), (0,)))


def _bdot(a, b, dims, precision=None):
    return lax.dot_general(a, b, dims, precision=precision, preferred_element_type=F32)


@jax.custom_vjp
def _bmmb_nt(a, b):
    return _bdot(a.astype(BF), b.astype(BF), BNT)


def _bmmb_nt_fwd(a, b):
    return _bmmb_nt(a, b), (a, b)


def _bmmb_nt_bwd(res, g):
    a, b = res
    gb = g.astype(BF)
    return _bdot(gb, b.astype(BF), BNN), _bdot(gb, a.astype(BF), BTN)


_bmmb_nt.defvjp(_bmmb_nt_fwd, _bmmb_nt_bwd)


@jax.custom_vjp
def _bmmb(a, b):
    return _bdot(a.astype(BF), b.astype(BF), BNN)


def _bmmb_fwd(a, b):
    return _bmmb(a, b), (a, b)


def _bmmb_bwd(res, g):
    a, b = res
    gb = g.astype(BF)
    return _bdot(gb, b.astype(BF), BNT), _bdot(a.astype(BF), gb, BTN)


_bmmb.defvjp(_bmmb_fwd, _bmmb_bwd)


@jax.custom_vjp
def _bmmb_tn(a, b):
    return _bdot(a.astype(BF), b.astype(BF), BTN)


def _bmmb_tn_fwd(a, b):
    return _bmmb_tn(a, b), (a, b)


def _bmmb_tn_bwd(res, g):
    a, b = res
    gb = g.astype(BF)
    return _bdot(b.astype(BF), gb, BNT), _bdot(a.astype(BF), gb, BNN)


_bmmb_tn.defvjp(_bmmb_tn_fwd, _bmmb_tn_bwd)


def _unit_lower_solve_fwd(a, r):
    c = a.shape[-1]
    ri = lax.broadcasted_iota(jnp.int32, a.shape, 1)
    ci = lax.broadcasted_iota(jnp.int32, a.shape, 2)
    xm = -a
    inv = (ri == ci).astype(F32) + xm
    for _ in range(int(math.log2(c)) - 1):
        xm = _bdot(xm, xm, BNN, HI)
        inv = inv + _bdot(inv, xm, BNN, HI)
    x = _bdot(inv, r, BNN, HI)
    return x, (inv, x)


def _unit_lower_solve_bwd(res, g):
    inv, x = res
    dr = _bdot(inv, g, BTN, HI)
    return -_bdot(dr, x, BNT, HI), dr


@jax.custom_vjp
def _unit_lower_solve_given(a, r, inv):
    return _bdot(inv, r, BNN, HI)


def _unit_lower_solve_given_fwd(a, r, inv):
    x = _bdot(inv, r, BNN, HI)
    return x, (inv, x)


def _unit_lower_solve_given_bwd(res, g):
    da, dr = _unit_lower_solve_bwd(res, g)
    return da, dr, jnp.zeros_like(res[0])


_unit_lower_solve_given.defvjp(_unit_lower_solve_given_fwd, _unit_lower_solve_given_bwd)


def _gdn_intra(qkv, ba, al8, dt8, inv4=None):
    tm = qkv.shape[0]
    nb = tm // CHUNK
    bsz = DN_HEADS * nb

    def heads(x0):
        return jnp.concatenate([qkv[:, x0 + h * LANE:x0 + (h + 1) * LANE].reshape(nb, CHUNK, LANE) for h in range(DN_HEADS)], axis=0)

    def spread(c0):
        return jnp.concatenate([jnp.broadcast_to(ba[:, c0 + h:c0 + h + 1], (tm, LANE)).reshape(nb, CHUNK, LANE)
                                for h in range(DN_HEADS)], axis=0)

    def per_head(v8):
        return jnp.concatenate([jnp.broadcast_to(v8[0:1, h:h + 1].reshape(1, 1, 1), (nb, 1, LANE)) for h in range(DN_HEADS)], axis=0)

    ri = lax.broadcasted_iota(jnp.int32, (bsz, CHUNK, CHUNK), 1)
    ci = lax.broadcasted_iota(jnp.int32, (bsz, CHUNK, CHUNK), 2)
    incl = ri >= ci
    strict = ri > ci

    q = _l2norm(heads(0)) * (DN_DK ** -0.5)
    k = _l2norm(heads(DN_QK))
    va = heads(2 * DN_QK)
    beta = _sigmoid(spread(0))
    g = -jnp.exp(per_head(al8)) * _softplus(spread(DN_HEADS) + per_head(dt8))
    gc = _bdot(incl.astype(F32), g, BNN, HI)
    g_last = jnp.sum(g, axis=1, keepdims=True)
    gcol = gc[:, :, :CHUNK]
    diff = gcol - jnp.swapaxes(gcol, 1, 2)
    decay = jnp.where(incl, jnp.exp(jnp.where(incl, diff, 0.0)), 0.0)
    kb = k * beta
    a_mat = jnp.where(strict, _bmmb_nt(kb, k) * decay, 0.0)
    egc = jnp.exp(gc)
    rhs = jnp.concatenate([kb * egc, va * beta], axis=2)
    if inv4 is None:
        wu, (inv, _) = _unit_lower_solve_fwd(a_mat, rhs)
    else:
        wu = _unit_lower_solve_given(a_mat, rhs, inv4.reshape(bsz, CHUNK, CHUNK))
    attn = jnp.where(incl, _bmmb_nt(q, k) * decay, 0.0)

    def unheads(x):
        return jnp.concatenate([x[h * nb:(h + 1) * nb].reshape(tm, LANE) for h in range(DN_HEADS)], axis=1)

    w_c, u_c = wu[:, :, :DN_DK], wu[:, :, DN_DK:]
    kd = k * jnp.exp(g_last - gc)
    out = (unheads(q * egc - _bmmb(attn, w_c)), unheads(_bmmb(attn, u_c)),
           _bmmb_tn(kd, w_c).reshape(DN_HEADS, nb, DN_DK, DN_DK), _bmmb_tn(kd, u_c).reshape(DN_HEADS, nb, DN_DK, DN_DV),
           jnp.broadcast_to(g_last, (bsz, GL_ROWS, LANE)).reshape(DN_HEADS, nb, GL_ROWS, LANE))
    return out if inv4 is not None else out + (inv.reshape(DN_HEADS, nb, CHUNK, CHUNK),)


def _gdn_scan_step(qp, op, c_mat, n_mat, gl, s):
    return _mmb(qp, s) + op, s * jnp.exp(gl) - _mmb(c_mat, s) + n_mat


def _gdn_intra_specs(t, tm, dts, order=lambda i: i):
    nb = tm // CHUNK
    row = pl.BlockSpec((tm, DN_VW), lambda i: (order(i), 0))
    mat = pl.BlockSpec((DN_HEADS, nb, DN_DK, DN_DV), lambda i: (0, order(i), 0, 0))
    row_shape = lambda d: jax.ShapeDtypeStruct((t, DN_VW), d)
    mat_shape = lambda d: jax.ShapeDtypeStruct((DN_HEADS, t // CHUNK, DN_DK, DN_DV), d)
    gl = pl.BlockSpec((DN_HEADS, nb, GL_ROWS, LANE), lambda i: (0, order(i), 0, 0))
    gl_shape = jax.ShapeDtypeStruct((DN_HEADS, t // CHUNK, GL_ROWS, LANE), dts[4])
    return [row, row, mat, mat, gl], [row_shape(dts[0]), row_shape(dts[1]), mat_shape(dts[2]), mat_shape(dts[3]), gl_shape]


def _gdn_intra_fwd(qkv, proj, al8, dt8, tm):
    t = qkv.shape[0]

    def body(qkv_ref, ba_ref, al_ref, dt_ref, *outs):
        for o, val in zip(outs, _gdn_intra(qkv_ref[...], ba_ref[...], al_ref[...], dt_ref[...])):
            o[...] = val.astype(o.dtype)

    specs, shapes = _gdn_intra_specs(t, tm, (BF, F32, BF, BF, F32))
    specs.append(_gdn_inverse_spec(tm))
    shapes.append(jax.ShapeDtypeStruct((DN_HEADS, t // CHUNK, CHUNK, CHUNK), F32))
    res = pl.pallas_call(
        body, grid=(t // tm,),
        in_specs=[pl.BlockSpec((tm, DN_CONV_CH), lambda i: (i, 0)), pl.BlockSpec((tm, LANE), lambda i: (i, 0)),
                  _full(al8.shape), _full(dt8.shape)],
        out_specs=specs, out_shape=shapes, name="gdn_intra_fwd",
        compiler_params=pltpu.CompilerParams(dimension_semantics=("parallel",)),
    )(qkv, proj, al8, dt8)
    return res[:5], res[5]


def _gdn_inverse_spec(tm):
    return pl.BlockSpec((DN_HEADS, tm // CHUNK, CHUNK, CHUNK), lambda i: (0, i, 0, 0))


def _gdn_intra_bwd(qkv, proj, al8, dt8, inverses, cts, tm):
    t = qkv.shape[0]

    def body(qkv_ref, ba_ref, al_ref, dt_ref, inv_ref, *refs):
        ct_refs, (dqkv_ref, dba_ref, dal_ref, ddt_ref) = refs[:5], refs[5:]

        @pl.when(pl.program_id(0) == 0)
        def _():
            dal_ref[...] = jnp.zeros_like(dal_ref)
            ddt_ref[...] = jnp.zeros_like(ddt_ref)

        _, vjp = jax.vjp(functools.partial(_gdn_intra, inv4=inv_ref[...]), qkv_ref[...], ba_ref[...], al_ref[...], dt_ref[...])
        dqkv, dba, dal, ddt = vjp(tuple(r[...].astype(F32) for r in ct_refs))
        dqkv_ref[...] = dqkv.astype(dqkv_ref.dtype)
        dba_ref[...] = dba.astype(dba_ref.dtype)
        dal_ref[...] += dal
        ddt_ref[...] += ddt

    specs, _ = _gdn_intra_specs(t, tm, (F32,) * 5)
    return pl.pallas_call(
        body, grid=(t // tm,),
        in_specs=[pl.BlockSpec((tm, DN_CONV_CH), lambda i: (i, 0)), pl.BlockSpec((tm, LANE), lambda i: (i, 0)),
                  _full(al8.shape), _full(dt8.shape), _gdn_inverse_spec(tm)] + specs,
        out_specs=[pl.BlockSpec((tm, DN_CONV_CH), lambda i: (i, 0)), pl.BlockSpec((tm, LANE), lambda i: (i, 0)),
                   _full(al8.shape), _full(dt8.shape)],
        out_shape=[jax.ShapeDtypeStruct((t, DN_CONV_CH), BF), jax.ShapeDtypeStruct((t, LANE), BF),
                   jax.ShapeDtypeStruct(al8.shape, F32), jax.ShapeDtypeStruct(dt8.shape, F32)],
        name="gdn_intra_bwd", compiler_params=pltpu.CompilerParams(dimension_semantics=("arbitrary",)),
    )(qkv, proj, al8, dt8, inverses, *cts)


def _gdn_scan_fwd(intra, tm):
    t = intra[0].shape[0]
    nb = tm // CHUNK
    nc = t // CHUNK

    def body(qp_ref, op_ref, c_ref, n_ref, gl_ref, o_ref, ss_ref, s_scr):
        @pl.when(pl.program_id(0) == 0)
        def _():
            s_scr[...] = jnp.zeros_like(s_scr)

        state = [s_scr[h] for h in range(DN_HEADS)]
        for cc in range(nb):
            rows = pl.ds(cc * CHUNK, CHUNK)
            for h in range(DN_HEADS):
                cols = pl.ds(h * DN_DV, DN_DV)
                ss_ref[cc, h] = state[h].astype(ss_ref.dtype)
                o_ref[rows, cols], state[h] = _gdn_scan_step(
                    qp_ref[rows, cols], op_ref[rows, cols], c_ref[h, cc], n_ref[h, cc], gl_ref[h, cc, pl.ds(0, 1), :], state[h])
        for h in range(DN_HEADS):
            s_scr[h] = state[h]

    specs, _ = _gdn_intra_specs(t, tm, (F32,) * 5)
    return pl.pallas_call(
        body, grid=(t // tm,), in_specs=specs,
        out_specs=[pl.BlockSpec((tm, DN_VW), lambda i: (i, 0)),
                   pl.BlockSpec((nb, DN_HEADS, DN_DK, DN_DV), lambda i: (i, 0, 0, 0))],
        out_shape=[jax.ShapeDtypeStruct((t, DN_VW), F32), jax.ShapeDtypeStruct((nc, DN_HEADS, DN_DK, DN_DV), BF)],
        scratch_shapes=[pltpu.VMEM((DN_HEADS, DN_DK, DN_DV), F32)], name="gdn_scan_fwd",
        compiler_params=pltpu.CompilerParams(dimension_semantics=("arbitrary",)),
    )(*intra)


def _gdn_scan_bwd(intra, states, do, tm):
    t = intra[0].shape[0]
    nb = tm // CHUNK
    ng = t // tm

    def body(qp_ref, op_ref, c_ref, n_ref, gl_ref, ss_ref, do_ref, dqp_ref, dop_ref, dc_ref, dn_ref, dgl_ref, ds_scr):
        @pl.when(pl.program_id(0) == 0)
        def _():
            ds_scr[...] = jnp.zeros_like(ds_scr)

        d_state = [ds_scr[h] for h in range(DN_HEADS)]
        for cc in reversed(range(nb)):
            rows = pl.ds(cc * CHUNK, CHUNK)
            for h in range(DN_HEADS):
                cols = pl.ds(h * DN_DV, DN_DV)
                _, vjp = jax.vjp(_gdn_scan_step, qp_ref[rows, cols].astype(F32), op_ref[rows, cols], c_ref[h, cc].astype(F32),
                                 n_ref[h, cc].astype(F32), gl_ref[h, cc, pl.ds(0, 1), :], ss_ref[cc, h].astype(F32))
                dqp_ref[rows, cols], dop_ref[rows, cols], dc, dn, dgl, d_state[h] = vjp((do_ref[rows, cols], d_state[h]))
                dc_ref[h, cc] = dc.astype(dc_ref.dtype)
                dn_ref[h, cc] = dn.astype(dn_ref.dtype)
                first_row = lax.broadcasted_iota(jnp.int32, (GL_ROWS, LANE), 0) == 0
                dgl_ref[h, cc] = jnp.where(first_row, dgl, 0.0)
        for h in range(DN_HEADS):
            ds_scr[h] = d_state[h]

    five, shapes = _gdn_intra_specs(t, tm, (F32, F32, BF, BF, F32), order=lambda i: ng - 1 - i)
    row = five[0]
    return pl.pallas_call(
        body, grid=(ng,),
        in_specs=five + [pl.BlockSpec((nb, DN_HEADS, DN_DK, DN_DV), lambda i: (ng - 1 - i, 0, 0, 0)), row],
        out_specs=five, out_shape=shapes,
        scratch_shapes=[pltpu.VMEM((DN_HEADS, DN_DK, DN_DV), F32)], name="gdn_scan_bwd",
        compiler_params=pltpu.CompilerParams(dimension_semantics=("arbitrary",)),
    )(*intra, states, do)


def _gdn_out(o, z, g):
    parts = []
    for h in range(DN_HEADS):
        sl = slice(h * DN_DV, (h + 1) * DN_DV)
        parts.append(_rmsnorm(o[:, sl], g) * _silu(z[:, sl]))
    return parts


_Q_SCALE = math.log2(math.e) / math.sqrt(QK_NOPE + QK_ROPE)


def _rope_tables(pos, inv_freq2):
    lane = lax.broadcasted_iota(jnp.int32, (1, LANE), 1)
    ang = pos * inv_freq2
    cos = jnp.where(lane < QK_ROPE, jnp.cos(ang), 0.0)
    sin = jnp.where(lane < QK_ROPE // 2, -jnp.sin(ang), jnp.where(lane < QK_ROPE, jnp.sin(ang), 0.0))
    return cos, sin


@jax.custom_vjp
def _rope_swap(u):
    lane = lax.broadcasted_iota(jnp.int32, u.shape, 1)
    half = QK_ROPE // 2
    return jnp.where(lane < half, pltpu.roll(u, LANE - half, 1), jnp.where(lane < QK_ROPE, pltpu.roll(u, half, 1), 0.0))


_rope_swap.defvjp(lambda u: (_rope_swap(u), None), lambda _, g: (_rope_swap(g),))


def _mla_prep(cq, ckv, kr, gq, gkv, w_uq, w_ukv, cos, sin):
    rope = lambda u: u * cos + _rope_swap(u) * sin
    q_lin = _mmb_nt(_rmsnorm(cq, gq), w_uq) * _Q_SCALE
    kv_lin = _mmb_nt(_rmsnorm(ckv, gkv), w_ukv)
    k_rope = rope(kr)
    qs, ks, vs = [], [], []
    for h in range(MLA_HEADS):
        qs += [q_lin[:, h * LANE:(h + 1) * LANE], rope(q_lin[:, (MLA_HEADS + h) * LANE:(MLA_HEADS + h + 1) * LANE])]
        ks += [kv_lin[:, 2 * h * LANE:(2 * h + 1) * LANE], k_rope]
        vs += [kv_lin[:, (2 * h + 1) * LANE:(2 * h + 2) * LANE]]
    return qs + ks + vs


def _mla_prep_fwd(proj, pos_col, inv_freq2, gq, gkv, w_uq, w_ukv, tm):
    t = proj.shape[0]
    nq = 2 * MLA_HEADS

    def body(cq_ref, ckv_ref, kr_ref, pos_ref, f_ref, gq_ref, gkv_ref, wq_ref, wkv_ref, q_ref, k_ref, v_ref):
        cos, sin = _rope_tables(pos_ref[...], f_ref[...])
        outs = _mla_prep(cq_ref[...].astype(F32), ckv_ref[...].astype(F32), kr_ref[...].astype(F32), gq_ref[...], gkv_ref[...], wq_ref[...], wkv_ref[...],
                         cos, sin)
        for i in range(nq):
            q_ref[:, pl.ds(i * LANE, LANE)] = outs[i].astype(q_ref.dtype)
            k_ref[:, pl.ds(i * LANE, LANE)] = outs[nq + i].astype(k_ref.dtype)
        for h in range(MLA_HEADS):
            v_ref[:, pl.ds(h * LANE, LANE)] = outs[2 * nq + h].astype(v_ref.dtype)

    row = lambda w, j: pl.BlockSpec((tm, w), functools.partial(lambda i, j: (i, j), j=j))
    return pl.pallas_call(
        body, grid=(t // tm,),
        in_specs=[row(Q_LORA, P_CQ // Q_LORA), row(KV_LORA, P_CKV // KV_LORA), row(LANE, P_KR // LANE),
                  pl.BlockSpec((tm, 1), lambda i: (i, 0)), _full(inv_freq2.shape), _full(gq.shape), _full(gkv.shape),
                  _full(w_uq.shape), _full(w_ukv.shape)],
        out_specs=[row(nq * LANE, 0), row(nq * LANE, 0), row(MLA_VW, 0)],
        out_shape=[jax.ShapeDtypeStruct((t, nq * LANE), BF), jax.ShapeDtypeStruct((t, nq * LANE), BF),
                   jax.ShapeDtypeStruct((t, MLA_VW), BF)],
        name="mla_prep_fwd", compiler_params=pltpu.CompilerParams(dimension_semantics=("arbitrary",)),
    )(proj, proj, proj, pos_col, inv_freq2, gq, gkv, w_uq, w_ukv)


def _mla_prep_bwd(proj, pos_col, inv_freq2, gq, gkv, w_uq, w_ukv, dq, dk, dv, tm):
    t = proj.shape[0]
    nq = 2 * MLA_HEADS

    def body(cq_ref, ckv_ref, kr_ref, pos_ref, f_ref, gq_ref, gkv_ref, wq_ref, wkv_ref, dq_ref, dk_ref, dv_ref,
             dcq_ref, dckv_ref, dkr_ref, dgq_ref, dgkv_ref, dwq_ref, dwkv_ref):
        @pl.when(pl.program_id(0) == 0)
        def _():
            for o in (dgq_ref, dgkv_ref, dwq_ref, dwkv_ref):
                o[...] = jnp.zeros_like(o)

        cos, sin = _rope_tables(pos_ref[...], f_ref[...])
        f = functools.partial(_mla_prep, cos=cos, sin=sin)
        _, vjp = jax.vjp(f, cq_ref[...].astype(F32), ckv_ref[...].astype(F32), kr_ref[...].astype(F32), gq_ref[...], gkv_ref[...], wq_ref[...], wkv_ref[...])
        cts = [dq_ref[:, pl.ds(i * LANE, LANE)] for i in range(nq)]
        cts += [dk_ref[:, pl.ds(i * LANE, LANE)] for i in range(nq)]
        cts += [dv_ref[:, pl.ds(h * LANE, LANE)] for h in range(MLA_HEADS)]
        dcq, dckv, dkr, dgq, dgkv, dwq, dwkv = vjp(cts)
        dcq_ref[...] = dcq.astype(dcq_ref.dtype)
        dckv_ref[...] = dckv.astype(dckv_ref.dtype)
        dkr_ref[...] = dkr.astype(dkr_ref.dtype)
        dgq_ref[...] += dgq
        dgkv_ref[...] += dgkv
        dwq_ref[...] += dwq
        dwkv_ref[...] += dwkv

    row = lambda w, j: pl.BlockSpec((tm, w), functools.partial(lambda i, j: (i, j), j=j))
    return pl.pallas_call(
        body, grid=(t // tm,),
        in_specs=[row(Q_LORA, P_CQ // Q_LORA), row(KV_LORA, P_CKV // KV_LORA), row(LANE, P_KR // LANE),
                  pl.BlockSpec((tm, 1), lambda i: (i, 0)), _full(inv_freq2.shape), _full(gq.shape), _full(gkv.shape),
                  _full(w_uq.shape), _full(w_ukv.shape), row(nq * LANE, 0), row(nq * LANE, 0), row(MLA_VW, 0)],
        out_specs=[row(Q_LORA, 0), row(KV_LORA, 0), row(LANE, 0), _full(gq.shape), _full(gkv.shape),
                   _full(w_uq.shape), _full(w_ukv.shape)],
        out_shape=[jax.ShapeDtypeStruct((t, Q_LORA), BF), jax.ShapeDtypeStruct((t, KV_LORA), BF),
                   jax.ShapeDtypeStruct((t, LANE), BF), jax.ShapeDtypeStruct(gq.shape, F32),
                   jax.ShapeDtypeStruct(gkv.shape, F32), jax.ShapeDtypeStruct(w_uq.shape, F32),
                   jax.ShapeDtypeStruct(w_ukv.shape, F32)],
        name="mla_prep_bwd", compiler_params=pltpu.CompilerParams(dimension_semantics=("arbitrary",)),
    )(proj, proj, proj, pos_col, inv_freq2, gq, gkv, w_uq, w_ukv, dq, dk, dv)


_NEG = -1e30
_LN2 = math.log(2.0)
ATT_CHAINS = 2


def _causal(tq, tk, q0, k0):
    row = q0 + lax.broadcasted_iota(jnp.int32, (tq, tk), 0)
    col = k0 + lax.broadcasted_iota(jnp.int32, (tq, tk), 1)
    return col <= row


def _attn_fwd(q, k, v, tq, tk):
    t = q.shape[0]

    assert tk % tq == 0 or tq % tk == 0
    n_diag = max(1, tq // tk)

    th = tq // ATT_CHAINS

    def body(q_ref, k_ref, v_ref, o_ref, lse_ref):
        i = pl.program_id(1)
        n_full = (i * tq) // tk

        def step(k0, carry, masked):
            out = []
            for c, (m, l, acc) in enumerate(carry):
                kw = min(tk, (c + 1) * th) if masked and tk == tq else tk
                kt = k_ref[pl.ds(k0, kw), :]
                vt = v_ref[pl.ds(k0, kw), :]
                s = _dot(q_ref[pl.ds(c * th, th), :], kt, NT)
                if masked:
                    s = jnp.where(_causal(th, kw, i * tq + c * th, k0), s, _NEG)
                m_new = jnp.maximum(m, jnp.max(s, axis=-1, keepdims=True))
                p = jnp.exp2(s - m_new)
                alpha = jnp.exp2(m - m_new)
                out.append((m_new, alpha * l + jnp.sum(p, axis=-1, keepdims=True), alpha * acc + _dot(p.astype(BF), vt)))
            return tuple(out)

        init = tuple((jnp.full((th, 1), _NEG, F32), jnp.zeros((th, 1), F32), jnp.zeros((th, V_HEAD), F32)) for _ in range(ATT_CHAINS))
        carry = lax.fori_loop(0, n_full, lambda j, c: step(pl.multiple_of(j * tk, tk), c, False), init)
        for dd in range(n_diag):
            carry = step(pl.multiple_of((n_full + dd) * tk, tk), carry, True)
        for c, (m, l, acc) in enumerate(carry):
            o_ref[pl.ds(c * th, th), :] = acc / l
            lse_ref[pl.ds(c * th, th), :] = jnp.broadcast_to(m + jnp.log2(l), (th, LANE))

    return pl.pallas_call(
        body, grid=(MLA_HEADS, t // tq),
        in_specs=[pl.BlockSpec((tq, 2 * LANE), lambda h, i: (i, h)), pl.BlockSpec((t, 2 * LANE), lambda h, i: (0, h)),
                  pl.BlockSpec((t, V_HEAD), lambda h, i: (0, h))],
        out_specs=[pl.BlockSpec((tq, V_HEAD), lambda h, i: (i, h)), pl.BlockSpec((tq, LANE), lambda h, i: (i, h))],
        out_shape=[jax.ShapeDtypeStruct((t, MLA_VW), F32), jax.ShapeDtypeStruct((t, MLA_HEADS * LANE), F32)],
        name="attn_fwd", compiler_params=pltpu.CompilerParams(dimension_semantics=("parallel", "arbitrary")),
    )(q, k, v)


def _attn_bwd(q, k, v, do, lse, delta, tq, tk):
    t = q.shape[0]
    nkt = t // tk
    assert tk % tq == 0

    def body(q_ref, k_ref, v_ref, do_ref, lse_ref, dl_ref, dq_ref, dk_ref, dv_ref):
        j = pl.program_id(1)

        @pl.when(j == 0)
        def _():
            dq_ref[...] = jnp.zeros_like(dq_ref)

        kt = k_ref[...]
        vt = v_ref[...]

        def step(q0, carry, masked, kw=tk):
            dk, dv = carry
            rows = pl.ds(q0, tq)
            qt = q_ref[rows, :]
            dot_ = do_ref[rows, :]
            ktw, vtw = kt[:kw], vt[:kw]
            p = jnp.exp2(_dot(qt, ktw, NT) - lse_ref[rows, pl.ds(0, 1)])
            if masked:
                p = jnp.where(_causal(tq, kw, q0, j * tk), p, 0.0)
            dv_w = _dot(p.astype(BF), dot_, TN)
            ds = (p * (_dot(dot_, vtw, NT) - dl_ref[rows, pl.ds(0, 1)])).astype(BF)
            dk_w = _dot(ds, qt, TN)
            dq_ref[rows, :] += _dot(ds, ktw)
            if kw == tk:
                return dk + dk_w, dv + dv_w
            return (jnp.concatenate([dk[:kw] + dk_w, dk[kw:]], axis=0), jnp.concatenate([dv[:kw] + dv_w, dv[kw:]], axis=0))

        per = tk // tq
        carry = (jnp.zeros((tk, 2 * LANE), F32), jnp.zeros((tk, V_HEAD), F32))
        for dd in range(per):
            carry = step(pl.multiple_of(j * tk + dd * tq, tq), carry, True, kw=(dd + 1) * tq)

        def group(g, c):
            for dd in range(per):
                c = step(pl.multiple_of(g * tk + dd * tq, tq), c, False)
            return c

        dk, dv = lax.fori_loop(j + 1, nkt, group, carry)
        dk_ref[...] = dk * _LN2
        dv_ref[...] = dv

        @pl.when(j == nkt - 1)
        def _():
            dq_ref[...] = dq_ref[...] * _LN2

    return pl.pallas_call(
        body, grid=(MLA_HEADS, nkt),
        in_specs=[pl.BlockSpec((t, 2 * LANE), lambda h, j: (0, h)), pl.BlockSpec((tk, 2 * LANE), lambda h, j: (j, h)),
                  pl.BlockSpec((tk, V_HEAD), lambda h, j: (j, h)), pl.BlockSpec((t, V_HEAD), lambda h, j: (0, h)),
                  pl.BlockSpec((t, LANE), lambda h, j: (0, h)), pl.BlockSpec((t, LANE), lambda h, j: (0, h))],
        out_specs=[pl.BlockSpec((t, 2 * LANE), lambda h, j: (0, h)), pl.BlockSpec((tk, 2 * LANE), lambda h, j: (j, h)),
                   pl.BlockSpec((tk, V_HEAD), lambda h, j: (j, h))],
        out_shape=[jax.ShapeDtypeStruct((t, MLA_HEADS * 2 * LANE), F32), jax.ShapeDtypeStruct((t, MLA_HEADS * 2 * LANE), F32),
                   jax.ShapeDtypeStruct((t, MLA_VW), F32)],
        name="attn_bwd", compiler_params=pltpu.CompilerParams(dimension_semantics=("parallel", "arbitrary")),
    )(q, k, v, do, lse, delta)


def _adam_update(w, g, m, v):
    mm = ADAM_B1 * m + (1.0 - ADAM_B1) * g
    vv = ADAM_B2 * v + (1.0 - ADAM_B2) * jnp.square(g)
    m_hat = mm / (1.0 - ADAM_B1 ** ADAM_STEP)
    v_hat = vv / (1.0 - ADAM_B2 ** ADAM_STEP)
    return -ADAM_LR * (m_hat / (jnp.sqrt(v_hat) + ADAM_EPS) + ADAM_WD * w), mm, vv


def _adamw(w, g, m, v, name):
    r, c = w.shape
    tr = max([r // s for s in range(1, r // 8 + 1) if r % s == 0 and (r // s) % 8 == 0 and r // s <= 512] or [r])
    slots = g.ndim == 3

    def body(w_ref, g_ref, m_ref, v_ref, g_out, d_ref, nm_ref, nv_ref):
        if slots:
            gg = g_ref[0].astype(F32)
            for s in range(1, N_DEV):
                gg = gg + g_ref[s].astype(F32)
        else:
            gg = g_ref[...]
        g_out[...] = gg
        d_ref[...], nm_ref[...], nv_ref[...] = _adam_update(w_ref[...], gg, m_ref[...], v_ref[...])

    spec = pl.BlockSpec((tr, c), lambda i: (i, 0))
    g_spec = pl.BlockSpec((N_DEV, tr, c), lambda i: (0, i, 0)) if slots else spec
    return pl.pallas_call(
        body, grid=(r // tr,), in_specs=[spec, g_spec, spec, spec], out_specs=[spec] * 4,
        out_shape=[jax.ShapeDtypeStruct((r, c), F32)] * 4, name=name,
        compiler_params=pltpu.CompilerParams(dimension_semantics=("arbitrary",)),
    )(w, g, m, v)


def _adamw_many(ws, gs, ms, vs, name):
    n = len(ws)

    def body(*refs):
        for i in range(n):
            w_ref, g_ref, m_ref, v_ref = (refs[j * n + i] for j in range(4))
            d_ref, nm_ref, nv_ref = (refs[(4 + j) * n + i] for j in range(3))
            d_ref[...], nm_ref[...], nv_ref[...] = _adam_update(w_ref[...], g_ref[...], m_ref[...], v_ref[...])

    shapes = [jax.ShapeDtypeStruct(w.shape, F32) for w in ws]
    outs = pl.pallas_call(body, out_shape=shapes * 3, name=name)(*ws, *gs, *ms, *vs)
    return outs[:n], outs[n:2 * n], outs[2 * n:]


def _cast_bf16(xs, name, after=None):
    n = len(xs)
    extra = [] if after is None else [after]

    def body(*refs):
        outs = refs[n + len(extra):]
        for i in range(n):
            outs[i][...] = refs[i][...].astype(BF)

    vmem = pl.BlockSpec(memory_space=pltpu.VMEM)
    return pl.pallas_call(
        body, out_shape=[jax.ShapeDtypeStruct(x.shape, BF) for x in xs], name=name,
        in_specs=[vmem] * n + [pl.BlockSpec(memory_space=pl.ANY)] * len(extra), out_specs=[vmem] * n)(*xs, *extra)


def _pad_rows(a, n):
    return jnp.pad(a, ((0, n - a.shape[0]), (0, 0)))


def _w_in_to_padded(wt):
    s_ba = P_CQ
    s_cq = s_ba + 2 * DN_HEADS
    s_kr = s_cq + Q_LORA + KV_LORA
    return jnp.concatenate([wt[:s_ba], wt[s_cq:s_kr], _pad_rows(wt[s_ba:s_cq], LANE), _pad_rows(wt[s_kr:], LANE)], axis=0)


def _w_in_from_padded(wt):
    return jnp.concatenate([wt[:P_CQ], wt[P_BA:P_BA + 2 * DN_HEADS], wt[P_CQ:P_BA], wt[P_KR:P_KR + QK_ROPE]], axis=0)


def _w_uq_to_padded(wt):
    w3 = wt.reshape(MLA_HEADS, QK_NOPE + QK_ROPE, Q_LORA)
    nope = w3[:, :QK_NOPE].reshape(MLA_HEADS * QK_NOPE, Q_LORA)
    rope = jnp.pad(w3[:, QK_NOPE:], ((0, 0), (0, LANE - QK_ROPE), (0, 0))).reshape(MLA_HEADS * LANE, Q_LORA)
    return jnp.concatenate([nope, rope], axis=0)


def _w_uq_from_padded(wt):
    nope = wt[:MLA_HEADS * QK_NOPE].reshape(MLA_HEADS, QK_NOPE, Q_LORA)
    rope = wt[MLA_HEADS * QK_NOPE:].reshape(MLA_HEADS, LANE, Q_LORA)[:, :QK_ROPE]
    return jnp.concatenate([nope, rope], axis=1).reshape(MLA_HEADS * (QK_NOPE + QK_ROPE), Q_LORA)


def _pack(pieces, width, row_mult):
    flat = jnp.concatenate([p.reshape(-1) for p in pieces])
    n = flat.shape[0]
    rows = -(-n // (width * row_mult)) * row_mult
    return jnp.pad(flat, (0, rows * width - n)).reshape(rows, width)


def _unpack(flat, shapes):
    out, o = [], 0
    for s in shapes:
        n = math.prod(s)
        out.append(flat[o:o + n].reshape(s))
        o += n
    return out


def kernel(x, c, positions, w_ada, b_ada, w_in, conv_w, a_log, dt_bias, dn_norm_g, q_norm_g, w_uq, kv_norm_g, w_ukv, w_o, ln1_g, ln1_b, w_gate, w_up, w_down, ln2_g, ln2_b, loss_target, m_w_ada, m_b_ada, m_w_in, m_conv_w, m_a_log, m_dt_bias, m_dn_norm_g, m_q_norm_g, m_w_uq, m_kv_norm_g, m_w_ukv, m_w_o, m_ln1_g, m_ln1_b, m_w_gate, m_w_up, m_w_down, m_ln2_g, m_ln2_b, v_w_ada, v_b_ada, v_w_in, v_conv_w, v_a_log, v_dt_bias, v_dn_norm_g, v_q_norm_g, v_w_uq, v_kv_norm_g, v_w_ukv, v_w_o, v_ln1_g, v_ln1_b, v_w_gate, v_w_up, v_w_down, v_ln2_g, v_ln2_b):
    me = 4 * lax.axis_index("x") + 2 * lax.axis_index("y") + lax.axis_index("c")
    t, d = x.shape[1], x.shape[2]
    ada_n = w_ada.shape[2]

    tr = lambda w: w[0].T
    rows = lambda a: a.reshape(-1, a.shape[2])
    (in_shard,) = _cast_bf16([tr(w_in)], "cast_w_in")
    cw = conv_w.shape[3]
    a_in, c_all, conv_all = _gather_by_chip([in_shard, c, conv_w[0, :, 0, :]], "gather_w_in_and_small")
    c_all = c_all.reshape(N_DEV, d)
    conv_full = conv_all.transpose(1, 0, 2).reshape(CONV_K, N_DEV * cw)
    conv_w8 = jnp.pad(conv_full, ((0, 8 - CONV_K), (0, 0)))

    b_ada_mine = lax.dynamic_slice(b_ada, (0, me * ada_n), (1, ada_n))
    mod_cols = _mod_fwd(c_all, w_ada[0], b_ada_mine)
    (mod_all,) = _exchange([mod_cols.reshape(N_DEV, 1, ada_n)], "scatter_mod", scatter=True)
    mod = mod_all.reshape(1, N_DEV * ada_n)

    later = _cast_bf16([tr(w_uq), tr(w_ukv), w_o[0], tr(w_gate), tr(w_up), w_down[0]], "cast_weights", after=mod)
    mixer_gather, token_a = _exchange_start(later[:3], "gather_mixer_weights_start", scatter=False)
    ffn_gather, token_b = _exchange_start(later[3:], "gather_ffn_weights_start", scatter=False)
    mod = mod + (token_a + token_b)
    w_in_t = _w_in_to_padded(rows(a_in))

    def mixer_weights(after):
        a_uq, a_ukv, a_o = _exchange_wait(mixer_gather, after, "gather_mixer_weights_wait", scatter=False)
        return _w_uq_to_padded(rows(a_uq)), rows(a_ukv), rows(a_o)

    def ffn_weights(after):
        a_gate, a_up, a_down = _exchange_wait(ffn_gather, after, "gather_ffn_weights_wait", scatter=False)
        return rows(a_gate), rows(a_up), rows(a_down)

    def by_dest(g):
        return g.reshape(N_DEV, -1, g.shape[1])

    scatters = {}

    def grads_ready(tag, *g):
        if tag == "ffn":
            pieces = [by_dest(a) for a in g]
        elif tag == "mixer":
            g_w_o, g_w_uq_t, g_w_ukv_t = g
            pieces = [by_dest(g_w_o), by_dest(_w_uq_from_padded(g_w_uq_t).astype(BF)), by_dest(g_w_ukv_t.astype(BF))]
        else:
            pieces = [by_dest(_w_in_from_padded(g[0]))]
        scatters[tag], token = _exchange_start(pieces, "scatter_%s_grads_start" % tag, scatter=True)
        return token

    loc = _local_step(x[0], loss_target[0], positions[0], mod, w_in_t, mixer_weights, ffn_weights, grads_ready,
                      conv_w8, a_log, dt_bias, dn_norm_g, q_norm_g, kv_norm_g, ln1_g, ln1_b, ln2_g, ln2_b)
    grad_x, loss_acc, dmod, d_conv8, d_al8, d_dt8, d_dn_g, d_q_g, d_kv_g, d_ln1_g, d_ln1_b, d_ln2_g, d_ln2_b = loc

    small_shapes = [(6 * d,), (CONV_K, N_DEV * cw), (DN_HEADS,), (DN_HEADS,), (DN_DV,), (Q_LORA,), (KV_LORA,), (d,), (d,), (d,), (d,), (1,)]
    gsmall = _pack([dmod, d_conv8[:CONV_K], d_al8[0, :DN_HEADS], d_dt8[0, :DN_HEADS], d_dn_g, d_q_g, d_kv_g,
                    d_ln1_g, d_ln1_b, d_ln2_g, d_ln2_b, loss_acc[0, :1]], LANE, 8)
    (gsmall_all,) = _exchange([gsmall], "gather_small_grads", scatter=False)
    dmod_all = gsmall_all.reshape(N_DEV, -1)[:, :6 * d]
    tot = _unpack(_sum_slots(gsmall_all, "sum_small_grads").reshape(-1), small_shapes)
    g_b_ada, g_conv_full, g_a_log, g_dt_bias, g_dn_g, g_q_g, g_kv_g, g_ln1_g, g_ln1_b, g_ln2_g, g_ln2_b, loss1 = tot
    loss = loss1.reshape(())
    g_conv_w = lax.dynamic_slice(g_conv_full, (0, me * cw), (CONV_K, cw))
    g_w_ada = _mod_bwd(c_all.T, lax.dynamic_slice(dmod_all, (0, me * ada_n), (N_DEV, ada_n)))

    grads = {"w_ada": g_w_ada[None], "b_ada": g_b_ada[None], "conv_w": g_conv_w[None, :, None, :],
             "a_log": g_a_log[None], "dt_bias": g_dt_bias[None], "dn_norm_g": g_dn_g[None], "q_norm_g": g_q_g[None],
             "kv_norm_g": g_kv_g[None], "ln1_g": g_ln1_g[None], "ln1_b": g_ln1_b[None], "ln2_g": g_ln2_g[None], "ln2_b": g_ln2_b[None]}
    weights = dict(w_ada=w_ada, b_ada=b_ada, w_in=w_in, conv_w=conv_w, a_log=a_log, dt_bias=dt_bias, dn_norm_g=dn_norm_g,
                   q_norm_g=q_norm_g, w_uq=w_uq, kv_norm_g=kv_norm_g, w_ukv=w_ukv, w_o=w_o, ln1_g=ln1_g, ln1_b=ln1_b,
                   w_gate=w_gate, w_up=w_up, w_down=w_down, ln2_g=ln2_g, ln2_b=ln2_b)
    ms = dict(w_ada=m_w_ada, b_ada=m_b_ada, w_in=m_w_in, conv_w=m_conv_w, a_log=m_a_log, dt_bias=m_dt_bias,
              dn_norm_g=m_dn_norm_g, q_norm_g=m_q_norm_g, w_uq=m_w_uq, kv_norm_g=m_kv_norm_g, w_ukv=m_w_ukv, w_o=m_w_o,
              ln1_g=m_ln1_g, ln1_b=m_ln1_b, w_gate=m_w_gate, w_up=m_w_up, w_down=m_w_down, ln2_g=m_ln2_g, ln2_b=m_ln2_b)
    vs = dict(w_ada=v_w_ada, b_ada=v_b_ada, w_in=v_w_in, conv_w=v_conv_w, a_log=v_a_log, dt_bias=v_dt_bias,
              dn_norm_g=v_dn_norm_g, q_norm_g=v_q_norm_g, w_uq=v_w_uq, kv_norm_g=v_kv_norm_g, w_ukv=v_w_ukv, w_o=v_w_o,
              ln1_g=v_ln1_g, ln1_b=v_ln1_b, w_gate=v_w_gate, w_up=v_w_up, w_down=v_w_down, ln2_g=v_ln2_g, ln2_b=v_ln2_b)
    names = list(weights)
    big = ("w_ada", "w_gate", "w_up", "w_down", "w_o", "w_uq", "w_ukv", "w_in")
    waits = {"w_gate": ("ffn", ("w_gate", "w_up", "w_down")), "w_o": ("mixer", ("w_o", "w_uq", "w_ukv")), "w_in": ("in", ("w_in",))}
    delta_w, new_m, new_v, slots = {}, {}, {}, {}
    last = g_w_ada
    for n in big:
        if n == "w_in":
            rest = [r for r in names if r not in big]
            flat2 = lambda a: a.reshape(-1, a.shape[-1])
            outs = _adamw_many(*[[flat2(src[r]) for r in rest] for src in (weights, grads, ms, vs)], "adamw_small")
            for dst, o in zip((delta_w, new_m, new_v), outs):
                for r, a in zip(rest, o):
                    dst[r] = a.reshape(weights[r].shape)
            last = outs[0][0]
        transposed = n in ("w_in", "w_uq", "w_ukv", "w_gate", "w_up")
        two = (lambda a: a[0].T) if transposed else (lambda a: a[0])
        back = (lambda a: a.T[None]) if transposed else (lambda a: a[None])
        if n in waits:
            tag, members = waits[n]
            slots.update(zip(members, _exchange_wait(scatters[tag], last, "scatter_%s_grads_wait" % tag, scatter=True)))
        g_in = slots[n] if n in slots else two(grads[n])
        gr, dlt, nm, nv = _adamw(two(weights[n]), g_in, two(ms[n]), two(vs[n]), "adamw_" + n)
        grads[n], delta_w[n], new_m[n], new_v[n] = back(gr), back(dlt), back(nm), back(nv)
        last = nv

    return (loss, grad_x[None], *[grads[n] for n in names], *[delta_w[n] for n in names],
            *[new_m[n] for n in names], *[new_v[n] for n in names])


def _local_step(xs, tgt, pos, mod, w_in_t, mixer_weights, ffn_weights, grads_ready, conv_w8,
                a_log, dt_bias, dn_norm_g, q_norm_g, kv_norm_g, ln1_g, ln1_b, ln2_g, ln2_b):
    t, d = xs.shape
    sh_m, sc_m, gt_m, sh_f, sc_f, gt_f = [mod[:, i * d:(i + 1) * d] for i in range(6)]
    pos_col = pos.astype(F32).reshape(t, 1)
    inv_freq = 1.0 / (ROPE_THETA ** (jnp.arange(0, QK_ROPE, 2, dtype=F32) / QK_ROPE))
    inv_freq2 = jnp.pad(jnp.concatenate([inv_freq, inv_freq]), (0, LANE - QK_ROPE)).reshape(1, LANE)
    al8 = jnp.pad(a_log, ((0, 7), (0, LANE - DN_HEADS)))
    dt8 = jnp.pad(dt_bias, ((0, 7), (0, LANE - DN_HEADS)))

    tm = min(512, t)
    tq = min(256, t)
    tk = min(512, t)

    def modulate_in(xx, sc, sh, w_ba):
        h = (xx * (1.0 + sc) + sh).astype(BF)
        return h, _dot(h, w_ba, NT)

    h1, ba_raw = _rowwise("modulate_in", modulate_in, [xs], [sc_m, sh_m, w_in_t[P_BA:P_BA + LANE]], [(d, BF), (LANE, F32)], [], tm)
    proj = _matmul(h1, w_in_t, "nt", "in_proj", BF)
    qkv = _conv_fwd(proj, conv_w8, min(256, t))
    gdn_tm = min(512, t)
    intra, inverses = _gdn_intra_fwd(qkv, ba_raw, al8, dt8, gdn_tm)
    o_dn, states = _gdn_scan_fwd(intra, gdn_tm)
    w_uq_t, w_ukv_t, w_o_f = mixer_weights(states)
    qc, kc, vc = _mla_prep_fwd(proj, pos_col, inv_freq2, q_norm_g, kv_norm_g, w_uq_t, w_ukv_t, tm)
    o_mla, lse = _attn_fwd(qc, kc, vc, min(1024, t), min(1024, t))

    def mix_in(o, z, om, g):
        return jnp.concatenate(_gdn_out(o, z.astype(F32), g) + [om], axis=1)

    (mixin,) = _rowwise("mixer_out", mix_in, [o_dn, (proj, DN_VW, P_Z // DN_VW), o_mla], [dn_norm_g], [(2 * DN_VW, BF)], [], tm)
    mix = _matmul(mixin, w_o_f, "nn", "out_proj", BF)

    def block1(xx, mx, gt, g1, b1, sc, sh):
        x1 = _layernorm(DEEPNORM_ALPHA * xx + gt * mx, g1, b1)
        return x1, x1 * (1.0 + sc) + sh

    x1, h2 = _rowwise("norm1_modulate", block1, [xs, mix], [gt_m, ln1_g, ln1_b, sc_f, sh_f], [(d, F32), (d, BF)], [], tm)
    w_gate_f, w_up_f, w_down_f = ffn_weights(h2)
    act, act_dg, act_du = _ffn_in(h2, w_gate_f, w_up_f)
    ff = _matmul(act, w_down_f, "nn", "ffn_out", BF)

    def tail_loss(x1_, ff_, gt, g2, b2, tg):
        y = _layernorm(DEEPNORM_ALPHA * x1_ + gt * ff_, g2, b2)
        return 0.5 * jnp.sum(jnp.mean(jnp.square(y - tg), axis=-1))

    def tail(x1_, ff_, tg, gt, g2, b2):
        loss, (dx1, dff, dgt, dg2, db2) = jax.value_and_grad(tail_loss, argnums=(0, 1, 2, 3, 4))(x1_, ff_, gt, g2, b2, tg)
        return dx1, dff, jnp.full((1, LANE), loss, F32), dgt, dg2, db2

    dx1_a, dff, loss_acc, d_gt_f, d_ln2_g, d_ln2_b = _rowwise(
        "norm2_loss", tail, [x1, ff, tgt], [gt_f, ln2_g, ln2_b], [(d, BF), (d, BF)], [(1, LANE), (1, d), (1, d), (1, d)], tm)

    g_w_down = _matmul(act, dff, "tn", "d_w_down", BF)
    dgate, dup = _ffn_act_bwd(dff, w_down_f, act_dg, act_du)
    g_w_gate = _matmul(dgate, h2, "tn", "d_w_gate", BF)
    g_w_up = _matmul(dup, h2, "tn", "d_w_up", BF)
    token = grads_ready("ffn", g_w_gate, g_w_up, g_w_down)
    dh2 = _matmul2_nn(dgate, w_gate_f, dup, w_up_f, "d_ffn_in", BF)

    def block1_bwd(xx, mx, dx1_, dh2_, gt, g1, b1, sc, sh):
        _, vjp = jax.vjp(block1, xx, mx, gt, g1, b1, sc, sh)
        dxx, dmx, dgt, dg1, db1, dsc, dsh = vjp((dx1_.astype(F32), dh2_.astype(F32)))
        return dxx, dmx, dgt, dg1, db1, dsc, dsh

    dx_a, dmix, d_gt_m, d_ln1_g, d_ln1_b, d_sc_f, d_sh_f = _rowwise(
        "norm1_modulate_bwd", block1_bwd, [xs, mix, dx1_a, dh2], [gt_m + token, ln1_g, ln1_b, sc_f, sh_f],
        [(d, F32), (d, BF)], [(1, d)] * 5, min(256, t))

    dmixin = _matmul(dmix, w_o_f, "nt", "d_mixer_out", BF)
    g_w_o = _matmul(mixin, dmix, "tn", "d_w_o", BF)

    def mixer_bwd(o, z, om, dmi, g):
        _, vjp = jax.vjp(lambda o_, z_, g_: jnp.concatenate(_gdn_out(o_, z_, g_), axis=1), o, z.astype(F32), g)
        do_, dz_, dg_ = vjp(dmi[:, :DN_VW].astype(F32))
        dom = dmi[:, DN_VW:]
        delta = [jnp.broadcast_to(jnp.sum(dom[:, h * V_HEAD:(h + 1) * V_HEAD] * om[:, h * V_HEAD:(h + 1) * V_HEAD], axis=-1, keepdims=True), (o.shape[0], LANE))
                 for h in range(MLA_HEADS)]
        return do_, dz_, dom, jnp.concatenate(delta, axis=1), dg_

    do_dn, dz, do_mla, delta, d_dn_g = _rowwise(
        "mixer_out_bwd", mixer_bwd, [o_dn, (proj, DN_VW, P_Z // DN_VW), o_mla, dmixin], [dn_norm_g],
        [(DN_VW, F32), (DN_VW, BF), (MLA_VW, BF), (MLA_HEADS * LANE, F32)], [(1, DN_DV)], tm)

    dqc, dkc, dvc = _attn_bwd(qc, kc, vc, do_mla, lse, delta, min(512, t), min(1024, t))
    dcq, dckv, dkr, d_q_g, d_kv_g, g_w_uq_t, g_w_ukv_t = _mla_prep_bwd(
        proj, pos_col, inv_freq2, q_norm_g, kv_norm_g, w_uq_t, w_ukv_t, dqc, dkc, dvc, min(256, t))

    token = grads_ready("mixer", g_w_o, g_w_uq_t, g_w_ukv_t)

    d_intra = _gdn_scan_bwd(intra, states, do_dn, gdn_tm)
    dqkv_act, dba, d_al8, d_dt8 = _gdn_intra_bwd(qkv, ba_raw, al8 + token, dt8, inverses, d_intra, min(256, t))
    dproj, d_conv8 = _conv_bwd(proj, conv_w8, dqkv_act, [(P_Z, dz), (P_CQ, dcq), (P_CKV, dckv), (P_BA, dba), (P_KR, dkr)],
                               min(256, t))
    dh1 = _matmul(dproj, w_in_t, "nn", "d_in_proj", BF)
    g_w_in_t = _matmul(dproj, h1, "tn", "d_w_in", BF)
    token = grads_ready("in", g_w_in_t)

    def modulate_bwd(xx, dh, dxa, sc):
        dh = dh.astype(F32)
        return dh * (1.0 + sc) + dxa, jnp.sum(dh * xx, axis=0, keepdims=True), jnp.sum(dh, axis=0, keepdims=True)

    grad_x, d_sc_m, d_sh_m = _rowwise("modulate_in_bwd", modulate_bwd, [xs, dh1, dx_a], [sc_m + token], [(d, F32)], [(1, d), (1, d)], tm)
    dmod = jnp.concatenate([d_sh_m, d_sc_m, d_gt_m, d_sh_f, d_sc_f, d_gt_f], axis=1)
    return grad_x, loss_acc, dmod, d_conv8, d_al8, d_dt8, d_dn_g, d_q_g, d_kv_g, d_ln1_g, d_ln1_b, d_ln2_g, d_ln2_b
```

```python
import functools
import math

import jax
import jax.numpy as jnp
from jax import lax
from jax.experimental import pallas as pl
from jax.experimental.pallas import tpu as pltpu

F32 = jnp.float32
BF = jnp.bfloat16
HI = lax.Precision.HIGHEST

N_DEV = 8
DN_HEADS = 4
DN_DK = 128
DN_DV = 128
CONV_K = 4
CHUNK = 64
MLA_HEADS = 4
QK_NOPE = 128
QK_ROPE = 64
V_HEAD = 128
Q_LORA = 512
KV_LORA = 256
ROPE_THETA = 10000.0
DEPTH = 1
DEEPNORM_ALPHA = (2.0 * DEPTH) ** 0.25
LANE = 128
CONV_HALO = 8
GL_ROWS = 8
CONV_ROWS, CONV_COLS = 64, 256

DN_QK = DN_HEADS * DN_DK
DN_VW = DN_HEADS * DN_DV
DN_CONV_CH = 2 * DN_QK + DN_VW
MLA_VW = MLA_HEADS * V_HEAD
P_Z = DN_CONV_CH
P_CQ = P_Z + DN_VW
P_CKV = P_CQ + Q_LORA
P_BA = P_CKV + KV_LORA
P_KR = P_BA + LANE
N_INP = P_KR + LANE

ADAM_LR = 0.001
ADAM_B1 = 0.9
ADAM_B2 = 0.999
ADAM_EPS = 1e-08
ADAM_WD = 0.01
ADAM_STEP = 10

NN = (((1,), (0,)), ((), ()))
NT = (((1,), (1,)), ((), ()))
TN = (((0,), (0,)), ((), ()))


def _pick(n, prefs):
    for p in prefs:
        if n % p == 0:
            return p
    return n


def _full(shape):
    return pl.BlockSpec(shape, lambda *_: (0,) * len(shape))


def _dot(a, b, dims=NN):
    return lax.dot_general(a, b, dims, preferred_element_type=F32)


def _doth(a, b, dims=NN):
    return lax.dot_general(a, b, dims, precision=HI, preferred_element_type=F32)


@jax.custom_vjp
def _mmb(a, b):
    return _dot(a.astype(BF), b.astype(BF), NN)


def _mmb_fwd(a, b):
    return _mmb(a, b), (a, b)


def _mmb_bwd(res, g):
    a, b = res
    gb = g.astype(BF)
    return (_dot(gb, b.astype(BF), NT).astype(a.dtype), _dot(a.astype(BF), gb, TN).astype(b.dtype))


_mmb.defvjp(_mmb_fwd, _mmb_bwd)


@jax.custom_vjp
def _mmb_nt(a, b):
    return _dot(a.astype(BF), b.astype(BF), NT)


def _mmb_nt_fwd(a, b):
    return _mmb_nt(a, b), (a, b)


def _mmb_nt_bwd(res, g):
    a, b = res
    gb = g.astype(BF)
    return (_dot(gb, b.astype(BF), NN).astype(a.dtype), _dot(gb, a.astype(BF), TN).astype(b.dtype))


_mmb_nt.defvjp(_mmb_nt_fwd, _mmb_nt_bwd)


def _sigmoid(x):
    return 0.5 * (jnp.tanh(0.5 * x) + 1.0)


def _silu(x):
    return x * _sigmoid(x)


def _softplus(x):
    return jnp.maximum(x, 0.0) + jnp.log(1.0 + jnp.exp(-jnp.abs(x)))


def _layernorm(x, g, b, eps=1e-5):
    mu = jnp.mean(x, axis=-1, keepdims=True)
    xc = x - mu
    var = jnp.mean(xc * xc, axis=-1, keepdims=True)
    return xc * lax.rsqrt(var + eps) * g + b


def _rmsnorm(x, g, eps=1e-6):
    return x * lax.rsqrt(jnp.mean(x * x, axis=-1, keepdims=True) + eps) * g


def _l2norm(x, eps=1e-6):
    return x * lax.rsqrt(jnp.sum(x * x, axis=-1, keepdims=True) + eps)


def _rowwise(name, fn, rows, vecs, out_rows, out_accs, tm):
    rows = [r if isinstance(r, tuple) else (r, r.shape[1], 0) for r in rows]
    t = rows[0][0].shape[0]
    tm = min(tm, t)
    assert t % tm == 0
    nr, nv, no = len(rows), len(vecs), len(out_rows)

    def body(*refs):
        ins = [r[...] for r in refs[:nr + nv]]
        outs = fn(*ins)
        outs = outs if isinstance(outs, (tuple, list)) else (outs,)
        o_rows = refs[nr + nv:nr + nv + no]
        o_accs = refs[nr + nv + no:]
        for o, val in zip(o_rows, outs[:no]):
            o[...] = val.astype(o.dtype)
        if o_accs:
            @pl.when(pl.program_id(0) == 0)
            def _():
                for o in o_accs:
                    o[...] = jnp.zeros_like(o)
            for o, val in zip(o_accs, outs[no:]):
                o[...] += val

    in_specs = [pl.BlockSpec((tm, w), functools.partial(lambda i, j: (i, j), j=j)) for (_, w, j) in rows]
    in_specs += [_full(v.shape) for v in vecs]
    out_specs = [pl.BlockSpec((tm, w), lambda i: (i, 0)) for (w, _) in out_rows]
    out_specs += [_full(s) for s in out_accs]
    out_shape = [jax.ShapeDtypeStruct((t, w), d) for (w, d) in out_rows]
    out_shape += [jax.ShapeDtypeStruct(s, F32) for s in out_accs]
    res = pl.pallas_call(
        body, grid=(t // tm,), in_specs=in_specs, out_specs=out_specs, out_shape=out_shape, name=name,
        compiler_params=pltpu.CompilerParams(dimension_semantics=("arbitrary",)),
    )(*[r[0] for r in rows], *vecs)
    return res


def _matmul(a, b, mode, name, out_dtype=F32):
    if mode == "nn":
        (m, k), n = a.shape, b.shape[1]
    elif mode == "nt":
        (m, k), n = a.shape, b.shape[0]
    else:
        (k, m), n = a.shape, b.shape[1]
    tm, tn, tk = _matmul_tiles(m, n, k, a.dtype.itemsize, b.dtype.itemsize, jnp.dtype(out_dtype).itemsize)
    nk = k // tk
    dims = {"nn": NN, "nt": NT, "tn": TN}[mode]

    def body(a_ref, b_ref, o_ref, *acc):
        part = _dot(a_ref[...].astype(BF), b_ref[...].astype(BF), dims)
        if nk == 1:
            o_ref[...] = part.astype(o_ref.dtype)
            return
        (acc_ref,) = acc
        kk = pl.program_id(2)

        @pl.when(kk == 0)
        def _():
            acc_ref[...] = part

        @pl.when(kk > 0)
        def _():
            acc_ref[...] += part

        @pl.when(kk == nk - 1)
        def _():
            o_ref[...] = acc_ref[...].astype(o_ref.dtype)

    a_spec = pl.BlockSpec((tk, tm), lambda i, j, kk: (kk, i)) if mode == "tn" else pl.BlockSpec((tm, tk), lambda i, j, kk: (i, kk))
    b_spec = pl.BlockSpec((tn, tk), lambda i, j, kk: (j, kk)) if mode == "nt" else pl.BlockSpec((tk, tn), lambda i, j, kk: (kk, j))
    return pl.pallas_call(
        body, grid=(m // tm, n // tn, nk), in_specs=[a_spec, b_spec],
        out_specs=pl.BlockSpec((tm, tn), lambda i, j, kk: (i, j)),
        out_shape=jax.ShapeDtypeStruct((m, n), out_dtype),
        scratch_shapes=[pltpu.VMEM((tm, tn), F32)] if nk > 1 else [], name=name,
        compiler_params=pltpu.CompilerParams(dimension_semantics=("parallel", "parallel", "arbitrary")),
    )(a, b)


def _lane_tile(n, cap):
    return max([n // s for s in range(1, n // LANE + 1) if n % s == 0 and (n // s) % LANE == 0 and n // s <= cap] or [n])


def _ffn_in(h, w_gate, w_up):
    m, k = h.shape
    f = w_gate.shape[0]
    tm, tn = _pick(m, (1024, 512, 256, 128)), _lane_tile(f, 1408)

    def body(h_ref, wg_ref, wu_ref, act_ref, dg_ref, du_ref):
        hh = h_ref[...]
        g = _dot(hh, wg_ref[...], NT)
        u = _dot(hh, wu_ref[...], NT)
        sg = _sigmoid(g)
        silu_g = g * sg
        act_ref[...] = (silu_g * u).astype(act_ref.dtype)
        dg_ref[...] = (u * (sg + silu_g * (1.0 - sg))).astype(dg_ref.dtype)
        du_ref[...] = silu_g.astype(du_ref.dtype)

    w_spec = pl.BlockSpec((tn, k), lambda i, j: (j, 0))
    o_spec = pl.BlockSpec((tm, tn), lambda i, j: (i, j))
    return pl.pallas_call(
        body, grid=(m // tm, f // tn), in_specs=[pl.BlockSpec((tm, k), lambda i, j: (i, 0)), w_spec, w_spec],
        out_specs=[o_spec] * 3, out_shape=[jax.ShapeDtypeStruct((m, f), BF)] * 3, name="ffn_in",
        compiler_params=pltpu.CompilerParams(dimension_semantics=("parallel", "parallel")),
    )(h, w_gate, w_up)


def _ffn_act_bwd(dff, w_down, act_dg, act_du):
    m, k = dff.shape
    f = w_down.shape[0]
    tm, tn = _pick(m, (1024, 512, 256, 128)), _lane_tile(f, 1408)

    def body(d_ref, w_ref, fg_ref, fu_ref, dg_ref, du_ref):
        da = _dot(d_ref[...], w_ref[...], NT)
        dg_ref[...] = (da * fg_ref[...].astype(F32)).astype(dg_ref.dtype)
        du_ref[...] = (da * fu_ref[...].astype(F32)).astype(du_ref.dtype)

    o_spec = pl.BlockSpec((tm, tn), lambda i, j: (i, j))
    return pl.pallas_call(
        body, grid=(m // tm, f // tn),
        in_specs=[pl.BlockSpec((tm, k), lambda i, j: (i, 0)), pl.BlockSpec((tn, k), lambda i, j: (j, 0)), o_spec, o_spec],
        out_specs=[o_spec] * 2, out_shape=[jax.ShapeDtypeStruct((m, f), BF)] * 2, name="d_ffn_act",
        compiler_params=pltpu.CompilerParams(dimension_semantics=("parallel", "parallel")),
    )(dff, w_down, act_dg, act_du)


def _matmul2_nn(a1, b1, a2, b2, name, out_dtype=F32):
    m, k = a1.shape
    n = b1.shape[1]
    tm, tn = _pick(m, (1024, 512, 256, 128)), _pick(n, (512, 256, 128))

    def body(a1_ref, b1_ref, a2_ref, b2_ref, o_ref):
        o_ref[...] = (_dot(a1_ref[...], b1_ref[...]) + _dot(a2_ref[...], b2_ref[...])).astype(o_ref.dtype)

    a_spec = pl.BlockSpec((tm, k), lambda i, j: (i, 0))
    b_spec = pl.BlockSpec((k, tn), lambda i, j: (0, j))
    return pl.pallas_call(
        body, grid=(m // tm, n // tn), in_specs=[a_spec, b_spec, a_spec, b_spec],
        out_specs=pl.BlockSpec((tm, tn), lambda i, j: (i, j)), out_shape=jax.ShapeDtypeStruct((m, n), out_dtype), name=name,
        compiler_params=pltpu.CompilerParams(dimension_semantics=("parallel", "parallel")),
    )(a1, b1, a2, b2)


MATMUL_VMEM_BUDGET = 28 * 1024 * 1024


def _matmul_tiles(m, n, k, a_bytes, b_bytes, o_bytes):
    def divisors(x, cap):
        return sorted({x // s for s in range(1, 65) if x % s == 0 and (x // s) % LANE == 0 and x // s <= cap}, reverse=True) or [x]

    for tk in divisors(k, k):
        best = None
        for tm in divisors(m, 1024):
            for tn in divisors(n, 2048):
                need = 2 * (tm * tk * a_bytes + tk * tn * b_bytes + tm * tn * o_bytes) + (tm * tn * 4 if tk < k else 0)
                if need <= MATMUL_VMEM_BUDGET and tm * tn >= 512 * 512 and (best is None or tm * tn > best[0] * best[1]):
                    best = (tm, tn)
        if best:
            return best[0], best[1], tk
    return _pick(m, (512, 256, 128)), _pick(n, (512, 256, 128)), _pick(k, (512, 256, 128))


def _exchange(xs, name, scatter):
    n = len(xs)
    npeer = N_DEV - 1

    def body(*refs):
        x_refs, o_refs = refs[:n], refs[n:2 * n]
        send_sems, recv_sems, local_sems = refs[2 * n:]
        mx, my, mc = lax.axis_index("x"), lax.axis_index("y"), lax.axis_index("c")
        me = 4 * mx + 2 * my + mc
        src_me = [x.at[me] if scatter else x for x in x_refs]
        mine = [pltpu.make_async_copy(src_me[a], o_refs[a].at[me], local_sems.at[a]) for a in range(n)]
        for cp in mine:
            cp.start()
        copies = []
        for k in range(1, N_DEV):
            px, py, pc = mx ^ (k >> 2), my ^ ((k >> 1) & 1), mc ^ (k & 1)
            peer = 4 * px + 2 * py + pc
            for a in range(n):
                cp = pltpu.make_async_remote_copy(
                    src_ref=x_refs[a].at[peer] if scatter else x_refs[a], dst_ref=o_refs[a].at[me],
                    send_sem=send_sems.at[a * npeer + k - 1], recv_sem=recv_sems.at[a * npeer + k - 1],
                    device_id=(px, py, pc), device_id_type=pl.DeviceIdType.MESH)
                cp.start()
                copies.append((cp, a, k, peer))
        for cp, a, k, peer in copies:
            pltpu.make_async_remote_copy(
                src_ref=src_me[a], dst_ref=o_refs[a].at[peer], send_sem=send_sems.at[a * npeer + k - 1],
                recv_sem=recv_sems.at[a * npeer + k - 1], device_id=(mx, my, mc),
                device_id_type=pl.DeviceIdType.MESH).wait_recv()
        for cp, _, _, _ in copies:
            cp.wait_send()
        for cp in mine:
            cp.wait()

    return pl.pallas_call(
        body, out_shape=[jax.ShapeDtypeStruct((N_DEV,) + x.shape[-2:], x.dtype) for x in xs],
        in_specs=[pl.BlockSpec(memory_space=pl.ANY)] * n, out_specs=[pl.BlockSpec(memory_space=pl.ANY)] * n,
        scratch_shapes=[pltpu.SemaphoreType.DMA((n * npeer,)), pltpu.SemaphoreType.DMA((n * npeer,)),
                        pltpu.SemaphoreType.DMA((n,))],
        name=name,
    )(*xs)


def _gather_by_chip(xs, name):
    n = len(xs)
    per = N_DEV - 1

    def body(*refs):
        x_refs, o_refs = refs[:n], refs[n:2 * n]
        send_sems, recv_sems, local_sems = refs[2 * n:]
        mx, my, mc = lax.axis_index("x"), lax.axis_index("y"), lax.axis_index("c")
        me, sibling = (mx, my, mc), (mx, my, 1 - mc)
        chips = [(1 - mx, my), (mx, 1 - my), (1 - mx, 1 - my)]
        slot = lambda d: 4 * d[0] + 2 * d[1] + d[2]

        def copy(a, k, block, to, src=None):
            dst = o_refs[a].at[slot(block)]
            return pltpu.make_async_remote_copy(
                src_ref=dst if src is None else src, dst_ref=dst, send_sem=send_sems.at[a * per + k],
                recv_sem=recv_sems.at[a * per + k], device_id=to, device_id_type=pl.DeviceIdType.MESH)

        mine = [pltpu.make_async_copy(x_refs[a], o_refs[a].at[slot(me)], local_sems.at[a]) for a in range(n)]
        for cp in mine:
            cp.start()
        first = []
        for a in range(n):
            first.append(copy(a, 0, me, sibling, src=x_refs[a]))
            first += [copy(a, 1 + j, me, (*chip, mc), src=x_refs[a]) for j, chip in enumerate(chips)]
        for cp in first:
            cp.start()
        passed = []
        for j, chip in enumerate(chips):
            for a in range(n):
                copy(a, 1 + j, (*chip, mc), me).wait_recv()
                cp = copy(a, 4 + j, (*chip, mc), sibling)
                cp.start()
                passed.append(cp)
        for a in range(n):
            copy(a, 0, sibling, me).wait_recv()
            for j, chip in enumerate(chips):
                copy(a, 4 + j, (*chip, 1 - mc), me).wait_recv()
        for cp in first + passed:
            cp.wait_send()
        for cp in mine:
            cp.wait()

    return pl.pallas_call(
        body, out_shape=[jax.ShapeDtypeStruct((N_DEV,) + x.shape, x.dtype) for x in xs],
        in_specs=[pl.BlockSpec(memory_space=pl.ANY)] * n, out_specs=[pl.BlockSpec(memory_space=pl.ANY)] * n,
        scratch_shapes=[pltpu.SemaphoreType.DMA((n * per,)), pltpu.SemaphoreType.DMA((n * per,)),
                        pltpu.SemaphoreType.DMA((n,))],
        name=name,
    )(*xs)


def _peer_of(k):
    mx, my, mc = lax.axis_index("x"), lax.axis_index("y"), lax.axis_index("c")
    px, py, pc = mx ^ (k >> 2), my ^ ((k >> 1) & 1), mc ^ (k & 1)
    return (px, py, pc), 4 * px + 2 * py + pc


def _exchange_start(xs, name, scatter):
    n = len(xs)
    npeer = N_DEV - 1

    def body(*refs):
        x_refs, land_refs = refs[:n], refs[n:2 * n]
        send_sems, recv_sems, token = refs[2 * n], refs[2 * n + 1], refs[-1]
        me = 4 * lax.axis_index("x") + 2 * lax.axis_index("y") + lax.axis_index("c")
        for k in range(1, N_DEV):
            dev, peer = _peer_of(k)
            for a in range(n):
                pltpu.make_async_remote_copy(
                    src_ref=x_refs[a].at[peer] if scatter else x_refs[a], dst_ref=land_refs[a].at[me],
                    send_sem=send_sems.at[a * npeer + k - 1], recv_sem=recv_sems.at[a * npeer + k - 1],
                    device_id=dev, device_id_type=pl.DeviceIdType.MESH).start()
        token[...] = jnp.zeros_like(token)

    hbm = pl.BlockSpec(memory_space=pltpu.HBM)
    sem = pl.BlockSpec(memory_space=pltpu.SEMAPHORE)
    lands = [pltpu.with_memory_space_constraint(lax.empty((N_DEV,) + x.shape[-2:], x.dtype), pltpu.HBM) for x in xs]
    srcs = [pltpu.with_memory_space_constraint(x, pltpu.HBM) for x in xs]
    outs = pl.pallas_call(
        body, name=name,
        out_shape=(pltpu.SemaphoreType.DMA((n * npeer,)), pltpu.SemaphoreType.DMA((n * npeer,)),
                   *[pltpu.HBM(x.shape, x.dtype) for x in srcs], *[pltpu.HBM(z.shape, z.dtype) for z in lands],
                   jax.ShapeDtypeStruct((8, LANE), F32)),
        in_specs=[hbm] * (2 * n), out_specs=(sem, sem, *[hbm] * (2 * n), pl.BlockSpec(memory_space=pltpu.VMEM)),
        input_output_aliases={i: 2 + i for i in range(2 * n)},
        compiler_params=pltpu.CompilerParams(has_side_effects=pltpu.SideEffectType.DATAFLOW_SIDE_EFFECTING),
    )(*srcs, *lands)
    return (outs[0], outs[1], list(outs[2:2 + n]), list(outs[2 + n:2 + 2 * n])), outs[-1][0:1, 0:1]


def _exchange_wait(started, after, name, scatter):
    send_sems, recv_sems, srcs, lands = started
    n = len(srcs)
    npeer = N_DEV - 1

    def body(*refs):
        x_refs, land_refs = refs[:n], refs[n:2 * n]
        send_sems, recv_sems = refs[2 * n], refs[2 * n + 1]
        mx, my, mc = lax.axis_index("x"), lax.axis_index("y"), lax.axis_index("c")
        me = 4 * mx + 2 * my + mc
        for k in range(1, N_DEV):
            _, peer = _peer_of(k)
            for a in range(n):
                src = x_refs[a].at[me] if scatter else x_refs[a]
                cp = pltpu.make_async_remote_copy(
                    src_ref=src, dst_ref=land_refs[a].at[peer], send_sem=send_sems.at[a * npeer + k - 1],
                    recv_sem=recv_sems.at[a * npeer + k - 1], device_id=(mx, my, mc), device_id_type=pl.DeviceIdType.MESH)
                cp.wait_send()
                cp.wait_recv()

    hbm = pl.BlockSpec(memory_space=pltpu.HBM)
    sem = pl.BlockSpec(memory_space=pltpu.SEMAPHORE)
    outs = pl.pallas_call(
        body, name=name,
        out_shape=(*[pltpu.HBM(x.shape, x.dtype) for x in srcs], *[pltpu.HBM(z.shape, z.dtype) for z in lands]),
        in_specs=[hbm] * (2 * n) + [sem, sem, pl.BlockSpec(memory_space=pl.ANY)], out_specs=tuple([hbm] * (2 * n)),
        input_output_aliases={i: i for i in range(2 * n)},
        compiler_params=pltpu.CompilerParams(has_side_effects=pltpu.SideEffectType.DATAFLOW_SIDE_EFFECTING),
    )(*srcs, *lands, send_sems, recv_sems, after)
    me = 4 * lax.axis_index("x") + 2 * lax.axis_index("y") + lax.axis_index("c")
    full = []
    for x, land in zip(outs[:n], outs[n:]):
        own = lax.dynamic_slice(x, (me, 0, 0), (1,) + x.shape[1:]) if scatter else x[None]
        full.append(lax.dynamic_update_slice(land, own, (me, 0, 0)))
    return full


def _sum_slots(x, name):
    _, r, c = x.shape
    tr = _pick(r, (512, 256, 128, 64, 32, 16))

    def body(x_ref, o_ref):
        acc = x_ref[0].astype(F32)
        for s in range(1, N_DEV):
            acc = acc + x_ref[s].astype(F32)
        o_ref[...] = acc

    return pl.pallas_call(
        body, grid=(r // tr,), in_specs=[pl.BlockSpec((N_DEV, tr, c), lambda i: (0, i, 0))],
        out_specs=pl.BlockSpec((tr, c), lambda i: (i, 0)), out_shape=jax.ShapeDtypeStruct((r, c), F32), name=name,
        compiler_params=pltpu.CompilerParams(dimension_semantics=("arbitrary",)),
    )(x)


def _mod_fwd(c_all, w_ada, b_ada_mine):
    def body(c_ref, w_ref, b_ref, o_ref):
        o_ref[...] = _doth(_silu(c_ref[...]), w_ref[...]) + b_ref[...]

    return pl.pallas_call(body, out_shape=jax.ShapeDtypeStruct((c_all.shape[0], w_ada.shape[1]), F32), name="mod_fwd")(c_all, w_ada, b_ada_mine)


def _mod_bwd(c_all_t, dmod_mine):
    def body(ct_ref, d_ref, o_ref):
        s = _silu(ct_ref[...])
        acc = s[:, 0:1] * d_ref[pl.ds(0, 1), :]
        for b in range(1, N_DEV):
            acc = acc + s[:, b:b + 1] * d_ref[pl.ds(b, 1), :]
        o_ref[...] = acc

    return pl.pallas_call(body, out_shape=jax.ShapeDtypeStruct((c_all_t.shape[0], dmod_mine.shape[1]), F32), name="mod_bwd")(c_all_t, dmod_mine)


def _conv_fwd(proj, conv_w8, tm):
    t = proj.shape[0]
    ch = DN_CONV_CH

    def body(x_ref, w_ref, o_ref, buf):
        @pl.when(pl.program_id(0) == 0)
        def _():
            buf[pl.ds(0, CONV_HALO), :] = jnp.zeros((CONV_HALO, ch), F32)

        buf[pl.ds(CONV_HALO, tm), :] = x_ref[...].astype(F32)
        for c0 in range(0, ch, CONV_COLS):
            cols = pl.ds(c0, CONV_COLS)
            w = [w_ref[pl.ds(j, 1), cols] for j in range(CONV_K)]
            for r0 in range(0, tm, CONV_ROWS):
                acc = buf[pl.ds(r0 + CONV_HALO - (CONV_K - 1), CONV_ROWS), cols] * w[0]
                for j in range(1, CONV_K):
                    acc = acc + buf[pl.ds(r0 + CONV_HALO - (CONV_K - 1) + j, CONV_ROWS), cols] * w[j]
                o_ref[pl.ds(r0, CONV_ROWS), cols] = _silu(acc)
        buf[pl.ds(0, CONV_HALO), :] = buf[pl.ds(tm, CONV_HALO), :]

    return pl.pallas_call(
        body, grid=(t // tm,), in_specs=[pl.BlockSpec((tm, ch), lambda i: (i, 0)), _full(conv_w8.shape)],
        out_specs=pl.BlockSpec((tm, ch), lambda i: (i, 0)), out_shape=jax.ShapeDtypeStruct((t, ch), F32),
        scratch_shapes=[pltpu.VMEM((tm + CONV_HALO, ch), F32)], name="conv_fwd",
        compiler_params=pltpu.CompilerParams(dimension_semantics=("arbitrary",)),
    )(proj, conv_w8)


def _conv_bwd(proj, conv_w8, dact, others, tm):
    t = proj.shape[0]
    ch = DN_CONV_CH
    nt = t // tm
    halo_blk = 2 * CONV_HALO
    hb = tm // halo_blk
    n_others = len(others)

    def body(x_ref, xp_ref, w_ref, dy_ref, *refs):
        piece_refs, (dx_ref, dw_ref, xbuf, dbuf) = refs[:n_others], refs[n_others:]
        step = pl.program_id(0)
        for (off, arr), p_ref in zip(others, piece_refs):
            dx_ref[:, pl.ds(off, arr.shape[1])] = p_ref[...].astype(dx_ref.dtype)

        @pl.when(step == 0)
        def _():
            dbuf[pl.ds(tm, CONV_HALO), :] = jnp.zeros((CONV_HALO, ch), F32)
            dw_ref[...] = jnp.zeros_like(dw_ref)

        first = step == nt - 1
        xbuf[pl.ds(0, CONV_HALO), :] = jnp.where(first, 0.0, xp_ref[...].astype(F32)[halo_blk - CONV_HALO:])
        xbuf[pl.ds(CONV_HALO, tm), :] = x_ref[...].astype(F32)
        for c0 in range(0, ch, CONV_COLS):
            cols = pl.ds(c0, CONV_COLS)
            w = [w_ref[pl.ds(j, 1), cols] for j in range(CONV_K)]
            dw = [jnp.zeros((1, CONV_COLS), F32) for _ in range(CONV_K)]
            for r0 in range(0, tm, CONV_ROWS):
                xs = [xbuf[pl.ds(r0 + CONV_HALO - (CONV_K - 1) + j, CONV_ROWS), cols] for j in range(CONV_K)]
                pre = xs[0] * w[0]
                for j in range(1, CONV_K):
                    pre = pre + xs[j] * w[j]
                sg = _sigmoid(pre)
                dpre = dy_ref[pl.ds(r0, CONV_ROWS), cols] * (sg * (1.0 + pre * (1.0 - sg)))
                dbuf[pl.ds(r0, CONV_ROWS), cols] = dpre
                dw = [dw[j] + jnp.sum(dpre * xs[j], axis=0, keepdims=True) for j in range(CONV_K)]
            for j in range(CONV_K):
                dw_ref[pl.ds(j, 1), cols] += dw[j]
            for r0 in range(0, tm, CONV_ROWS):
                dx = dbuf[pl.ds(r0 + CONV_K - 1, CONV_ROWS), cols] * w[0]
                for j in range(1, CONV_K):
                    dx = dx + dbuf[pl.ds(r0 + CONV_K - 1 - j, CONV_ROWS), cols] * w[j]
                dx_ref[pl.ds(r0, CONV_ROWS), cols] = dx.astype(dx_ref.dtype)
        dbuf[pl.ds(tm, CONV_HALO), :] = dbuf[pl.ds(0, CONV_HALO), :]

    rev = lambda i: (nt - 1 - i, 0)
    prev = lambda i: (jnp.maximum((nt - 1 - i) * hb - 1, 0), 0)
    return pl.pallas_call(
        body, grid=(nt,),
        in_specs=[pl.BlockSpec((tm, ch), rev), pl.BlockSpec((halo_blk, ch), prev), _full(conv_w8.shape),
                  pl.BlockSpec((tm, ch), rev)] + [pl.BlockSpec((tm, arr.shape[1]), rev) for _, arr in others],
        out_specs=[pl.BlockSpec((tm, N_INP), rev), _full(conv_w8.shape)],
        out_shape=[jax.ShapeDtypeStruct((t, N_INP), BF), jax.ShapeDtypeStruct(conv_w8.shape, F32)],
        scratch_shapes=[pltpu.VMEM((tm + CONV_HALO, ch), F32), pltpu.VMEM((tm + CONV_HALO, ch), F32)], name="conv_bwd",
        compiler_params=pltpu.CompilerParams(dimension_semantics=("arbitrary",)),
    )(proj, proj, conv_w8, dact, *[arr for _, arr in others])


BNN = (((2,), (1,)), ((0,), (0,)))
BNT = (((2,), (2,)), ((0,), (0,)))
BTN = (((1,), (1,)), ((0,), (0,)))


def _bdot(a, b, dims, precision=None):
    return lax.dot_general(a, b, dims, precision=precision, preferred_element_type=F32)


@jax.custom_vjp
def _bmmb_nt(a, b):
    return _bdot(a.astype(BF), b.astype(BF), BNT)


def _bmmb_nt_fwd(a, b):
    return _bmmb_nt(a, b), (a, b)


def _bmmb_nt_bwd(res, g):
    a, b = res
    gb = g.astype(BF)
    return _bdot(gb, b.astype(BF), BNN), _bdot(gb, a.astype(BF), BTN)


_bmmb_nt.defvjp(_bmmb_nt_fwd, _bmmb_nt_bwd)


@jax.custom_vjp
def _bmmb(a, b):
    return _bdot(a.astype(BF), b.astype(BF), BNN)


def _bmmb_fwd(a, b):
    return _bmmb(a, b), (a, b)


def _bmmb_bwd(res, g):
    a, b = res
    gb = g.astype(BF)
    return _bdot(gb, b.astype(BF), BNT), _bdot(a.astype(BF), gb, BTN)


_bmmb.defvjp(_bmmb_fwd, _bmmb_bwd)


@jax.custom_vjp
def _bmmb_tn(a, b):
    return _bdot(a.astype(BF), b.astype(BF), BTN)


def _bmmb_tn_fwd(a, b):
    return _bmmb_tn(a, b), (a, b)


def _bmmb_tn_bwd(res, g):
    a, b = res
    gb = g.astype(BF)
    return _bdot(b.astype(BF), gb, BNT), _bdot(a.astype(BF), gb, BNN)


_bmmb_tn.defvjp(_bmmb_tn_fwd, _bmmb_tn_bwd)


def _unit_lower_solve_fwd(a, r):
    c = a.shape[-1]
    ri = lax.broadcasted_iota(jnp.int32, a.shape, 1)
    ci = lax.broadcasted_iota(jnp.int32, a.shape, 2)
    xm = -a
    inv = (ri == ci).astype(F32) + xm
    for _ in range(int(math.log2(c)) - 1):
        xm = _bdot(xm, xm, BNN, HI)
        inv = inv + _bdot(inv, xm, BNN, HI)
    x = _bdot(inv, r, BNN, HI)
    return x, (inv, x)


def _unit_lower_solve_bwd(res, g):
    inv, x = res
    dr = _bdot(inv, g, BTN, HI)
    return -_bdot(dr, x, BNT, HI), dr


@jax.custom_vjp
def _unit_lower_solve_given(a, r, inv):
    return _bdot(inv, r, BNN, HI)


def _unit_lower_solve_given_fwd(a, r, inv):
    x = _bdot(inv, r, BNN, HI)
    return x, (inv, x)


def _unit_lower_solve_given_bwd(res, g):
    da, dr = _unit_lower_solve_bwd(res, g)
    return da, dr, jnp.zeros_like(res[0])


_unit_lower_solve_given.defvjp(_unit_lower_solve_given_fwd, _unit_lower_solve_given_bwd)


def _gdn_intra(qkv, ba, al8, dt8, inv4=None):
    tm = qkv.shape[0]
    nb = tm // CHUNK
    bsz = DN_HEADS * nb

    def heads(x0):
        return jnp.concatenate([qkv[:, x0 + h * LANE:x0 + (h + 1) * LANE].reshape(nb, CHUNK, LANE) for h in range(DN_HEADS)], axis=0)

    def spread(c0):
        return jnp.concatenate([jnp.broadcast_to(ba[:, c0 + h:c0 + h + 1], (tm, LANE)).reshape(nb, CHUNK, LANE)
                                for h in range(DN_HEADS)], axis=0)

    def per_head(v8):
        return jnp.concatenate([jnp.broadcast_to(v8[0:1, h:h + 1].reshape(1, 1, 1), (nb, 1, LANE)) for h in range(DN_HEADS)], axis=0)

    ri = lax.broadcasted_iota(jnp.int32, (bsz, CHUNK, CHUNK), 1)
    ci = lax.broadcasted_iota(jnp.int32, (bsz, CHUNK, CHUNK), 2)
    incl = ri >= ci
    strict = ri > ci

    q = _l2norm(heads(0)) * (DN_DK ** -0.5)
    k = _l2norm(heads(DN_QK))
    va = heads(2 * DN_QK)
    beta = _sigmoid(spread(0))
    g = -jnp.exp(per_head(al8)) * _softplus(spread(DN_HEADS) + per_head(dt8))
    gc = _bdot(incl.astype(F32), g, BNN, HI)
    g_last = jnp.sum(g, axis=1, keepdims=True)
    gcol = gc[:, :, :CHUNK]
    diff = gcol - jnp.swapaxes(gcol, 1, 2)
    decay = jnp.where(incl, jnp.exp(jnp.where(incl, diff, 0.0)), 0.0)
    kb = k * beta
    a_mat = jnp.where(strict, _bmmb_nt(kb, k) * decay, 0.0)
    egc = jnp.exp(gc)
    rhs = jnp.concatenate([kb * egc, va * beta], axis=2)
    if inv4 is None:
        wu, (inv, _) = _unit_lower_solve_fwd(a_mat, rhs)
    else:
        wu = _unit_lower_solve_given(a_mat, rhs, inv4.reshape(bsz, CHUNK, CHUNK))
    attn = jnp.where(incl, _bmmb_nt(q, k) * decay, 0.0)

    def unheads(x):
        return jnp.concatenate([x[h * nb:(h + 1) * nb].reshape(tm, LANE) for h in range(DN_HEADS)], axis=1)

    w_c, u_c = wu[:, :, :DN_DK], wu[:, :, DN_DK:]
    kd = k * jnp.exp(g_last - gc)
    out = (unheads(q * egc - _bmmb(attn, w_c)), unheads(_bmmb(attn, u_c)),
           _bmmb_tn(kd, w_c).reshape(DN_HEADS, nb, DN_DK, DN_DK), _bmmb_tn(kd, u_c).reshape(DN_HEADS, nb, DN_DK, DN_DV),
           jnp.broadcast_to(g_last, (bsz, GL_ROWS, LANE)).reshape(DN_HEADS, nb, GL_ROWS, LANE))
    return out if inv4 is not None else out + (inv.reshape(DN_HEADS, nb, CHUNK, CHUNK),)


def _gdn_scan_step(qp, op, c_mat, n_mat, gl, s):
    return _mmb(qp, s) + op, s * jnp.exp(gl) - _mmb(c_mat, s) + n_mat


def _gdn_intra_specs(t, tm, dts, order=lambda i: i):
    nb = tm // CHUNK
    row = pl.BlockSpec((tm, DN_VW), lambda i: (order(i), 0))
    mat = pl.BlockSpec((DN_HEADS, nb, DN_DK, DN_DV), lambda i: (0, order(i), 0, 0))
    row_shape = lambda d: jax.ShapeDtypeStruct((t, DN_VW), d)
    mat_shape = lambda d: jax.ShapeDtypeStruct((DN_HEADS, t // CHUNK, DN_DK, DN_DV), d)
    gl = pl.BlockSpec((DN_HEADS, nb, GL_ROWS, LANE), lambda i: (0, order(i), 0, 0))
    gl_shape = jax.ShapeDtypeStruct((DN_HEADS, t // CHUNK, GL_ROWS, LANE), dts[4])
    return [row, row, mat, mat, gl], [row_shape(dts[0]), row_shape(dts[1]), mat_shape(dts[2]), mat_shape(dts[3]), gl_shape]


def _gdn_intra_fwd(qkv, proj, al8, dt8, tm):
    t = qkv.shape[0]

    def body(qkv_ref, ba_ref, al_ref, dt_ref, *outs):
        for o, val in zip(outs, _gdn_intra(qkv_ref[...], ba_ref[...], al_ref[...], dt_ref[...])):
            o[...] = val.astype(o.dtype)

    specs, shapes = _gdn_intra_specs(t, tm, (BF, F32, BF, BF, F32))
    specs.append(_gdn_inverse_spec(tm))
    shapes.append(jax.ShapeDtypeStruct((DN_HEADS, t // CHUNK, CHUNK, CHUNK), F32))
    res = pl.pallas_call(
        body, grid=(t // tm,),
        in_specs=[pl.BlockSpec((tm, DN_CONV_CH), lambda i: (i, 0)), pl.BlockSpec((tm, LANE), lambda i: (i, 0)),
                  _full(al8.shape), _full(dt8.shape)],
        out_specs=specs, out_shape=shapes, name="gdn_intra_fwd",
        compiler_params=pltpu.CompilerParams(dimension_semantics=("parallel",)),
    )(qkv, proj, al8, dt8)
    return res[:5], res[5]


def _gdn_inverse_spec(tm):
    return pl.BlockSpec((DN_HEADS, tm // CHUNK, CHUNK, CHUNK), lambda i: (0, i, 0, 0))


def _gdn_intra_bwd(qkv, proj, al8, dt8, inverses, cts, tm):
    t = qkv.shape[0]

    def body(qkv_ref, ba_ref, al_ref, dt_ref, inv_ref, *refs):
        ct_refs, (dqkv_ref, dba_ref, dal_ref, ddt_ref) = refs[:5], refs[5:]

        @pl.when(pl.program_id(0) == 0)
        def _():
            dal_ref[...] = jnp.zeros_like(dal_ref)
            ddt_ref[...] = jnp.zeros_like(ddt_ref)

        _, vjp = jax.vjp(functools.partial(_gdn_intra, inv4=inv_ref[...]), qkv_ref[...], ba_ref[...], al_ref[...], dt_ref[...])
        dqkv, dba, dal, ddt = vjp(tuple(r[...].astype(F32) for r in ct_refs))
        dqkv_ref[...] = dqkv.astype(dqkv_ref.dtype)
        dba_ref[...] = dba.astype(dba_ref.dtype)
        dal_ref[...] += dal
        ddt_ref[...] += ddt

    specs, _ = _gdn_intra_specs(t, tm, (F32,) * 5)
    return pl.pallas_call(
        body, grid=(t // tm,),
        in_specs=[pl.BlockSpec((tm, DN_CONV_CH), lambda i: (i, 0)), pl.BlockSpec((tm, LANE), lambda i: (i, 0)),
                  _full(al8.shape), _full(dt8.shape), _gdn_inverse_spec(tm)] + specs,
        out_specs=[pl.BlockSpec((tm, DN_CONV_CH), lambda i: (i, 0)), pl.BlockSpec((tm, LANE), lambda i: (i, 0)),
                   _full(al8.shape), _full(dt8.shape)],
        out_shape=[jax.ShapeDtypeStruct((t, DN_CONV_CH), BF), jax.ShapeDtypeStruct((t, LANE), BF),
                   jax.ShapeDtypeStruct(al8.shape, F32), jax.ShapeDtypeStruct(dt8.shape, F32)],
        name="gdn_intra_bwd", compiler_params=pltpu.CompilerParams(dimension_semantics=("arbitrary",)),
    )(qkv, proj, al8, dt8, inverses, *cts)


def _gdn_scan_fwd(intra, tm):
    t = intra[0].shape[0]
    nb = tm // CHUNK
    nc = t // CHUNK

    def body(qp_ref, op_ref, c_ref, n_ref, gl_ref, o_ref, ss_ref, s_scr):
        @pl.when(pl.program_id(0) == 0)
        def _():
            s_scr[...] = jnp.zeros_like(s_scr)

        state = [s_scr[h] for h in range(DN_HEADS)]
        for cc in range(nb):
            rows = pl.ds(cc * CHUNK, CHUNK)
            for h in range(DN_HEADS):
                cols = pl.ds(h * DN_DV, DN_DV)
                ss_ref[cc, h] = state[h].astype(ss_ref.dtype)
                o_ref[rows, cols], state[h] = _gdn_scan_step(
                    qp_ref[rows, cols], op_ref[rows, cols], c_ref[h, cc], n_ref[h, cc], gl_ref[h, cc, pl.ds(0, 1), :], state[h])
        for h in range(DN_HEADS):
            s_scr[h] = state[h]

    specs, _ = _gdn_intra_specs(t, tm, (F32,) * 5)
    return pl.pallas_call(
        body, grid=(t // tm,), in_specs=specs,
        out_specs=[pl.BlockSpec((tm, DN_VW), lambda i: (i, 0)),
                   pl.BlockSpec((nb, DN_HEADS, DN_DK, DN_DV), lambda i: (i, 0, 0, 0))],
        out_shape=[jax.ShapeDtypeStruct((t, DN_VW), F32), jax.ShapeDtypeStruct((nc, DN_HEADS, DN_DK, DN_DV), BF)],
        scratch_shapes=[pltpu.VMEM((DN_HEADS, DN_DK, DN_DV), F32)], name="gdn_scan_fwd",
        compiler_params=pltpu.CompilerParams(dimension_semantics=("arbitrary",)),
    )(*intra)


def _gdn_scan_bwd(intra, states, do, tm):
    t = intra[0].shape[0]
    nb = tm // CHUNK
    ng = t // tm

    def body(qp_ref, op_ref, c_ref, n_ref, gl_ref, ss_ref, do_ref, dqp_ref, dop_ref, dc_ref, dn_ref, dgl_ref, ds_scr):
        @pl.when(pl.program_id(0) == 0)
        def _():
            ds_scr[...] = jnp.zeros_like(ds_scr)

        d_state = [ds_scr[h] for h in range(DN_HEADS)]
        for cc in reversed(range(nb)):
            rows = pl.ds(cc * CHUNK, CHUNK)
            for h in range(DN_HEADS):
                cols = pl.ds(h * DN_DV, DN_DV)
                _, vjp = jax.vjp(_gdn_scan_step, qp_ref[rows, cols].astype(F32), op_ref[rows, cols], c_ref[h, cc].astype(F32),
                                 n_ref[h, cc].astype(F32), gl_ref[h, cc, pl.ds(0, 1), :], ss_ref[cc, h].astype(F32))
                dqp_ref[rows, cols], dop_ref[rows, cols], dc, dn, dgl, d_state[h] = vjp((do_ref[rows, cols], d_state[h]))
                dc_ref[h, cc] = dc.astype(dc_ref.dtype)
                dn_ref[h, cc] = dn.astype(dn_ref.dtype)
                first_row = lax.broadcasted_iota(jnp.int32, (GL_ROWS, LANE), 0) == 0
                dgl_ref[h, cc] = jnp.where(first_row, dgl, 0.0)
        for h in range(DN_HEADS):
            ds_scr[h] = d_state[h]

    five, shapes = _gdn_intra_specs(t, tm, (F32, F32, BF, BF, F32), order=lambda i: ng - 1 - i)
    row = five[0]
    return pl.pallas_call(
        body, grid=(ng,),
        in_specs=five + [pl.BlockSpec((nb, DN_HEADS, DN_DK, DN_DV), lambda i: (ng - 1 - i, 0, 0, 0)), row],
        out_specs=five, out_shape=shapes,
        scratch_shapes=[pltpu.VMEM((DN_HEADS, DN_DK, DN_DV), F32)], name="gdn_scan_bwd",
        compiler_params=pltpu.CompilerParams(dimension_semantics=("arbitrary",)),
    )(*intra, states, do)


def _gdn_out(o, z, g):
    parts = []
    for h in range(DN_HEADS):
        sl = slice(h * DN_DV, (h + 1) * DN_DV)
        parts.append(_rmsnorm(o[:, sl], g) * _silu(z[:, sl]))
    return parts


_Q_SCALE = math.log2(math.e) / math.sqrt(QK_NOPE + QK_ROPE)


def _rope_tables(pos, inv_freq2):
    lane = lax.broadcasted_iota(jnp.int32, (1, LANE), 1)
    ang = pos * inv_freq2
    cos = jnp.where(lane < QK_ROPE, jnp.cos(ang), 0.0)
    sin = jnp.where(lane < QK_ROPE // 2, -jnp.sin(ang), jnp.where(lane < QK_ROPE, jnp.sin(ang), 0.0))
    return cos, sin


@jax.custom_vjp
def _rope_swap(u):
    lane = lax.broadcasted_iota(jnp.int32, u.shape, 1)
    half = QK_ROPE // 2
    return jnp.where(lane < half, pltpu.roll(u, LANE - half, 1), jnp.where(lane < QK_ROPE, pltpu.roll(u, half, 1), 0.0))


_rope_swap.defvjp(lambda u: (_rope_swap(u), None), lambda _, g: (_rope_swap(g),))


def _mla_prep(cq, ckv, kr, gq, gkv, w_uq, w_ukv, cos, sin):
    rope = lambda u: u * cos + _rope_swap(u) * sin
    q_lin = _mmb_nt(_rmsnorm(cq, gq), w_uq) * _Q_SCALE
    kv_lin = _mmb_nt(_rmsnorm(ckv, gkv), w_ukv)
    k_rope = rope(kr)
    qs, ks, vs = [], [], []
    for h in range(MLA_HEADS):
        qs += [q_lin[:, h * LANE:(h + 1) * LANE], rope(q_lin[:, (MLA_HEADS + h) * LANE:(MLA_HEADS + h + 1) * LANE])]
        ks += [kv_lin[:, 2 * h * LANE:(2 * h + 1) * LANE], k_rope]
        vs += [kv_lin[:, (2 * h + 1) * LANE:(2 * h + 2) * LANE]]
    return qs + ks + vs


def _mla_prep_fwd(proj, pos_col, inv_freq2, gq, gkv, w_uq, w_ukv, tm):
    t = proj.shape[0]
    nq = 2 * MLA_HEADS

    def body(cq_ref, ckv_ref, kr_ref, pos_ref, f_ref, gq_ref, gkv_ref, wq_ref, wkv_ref, q_ref, k_ref, v_ref):
        cos, sin = _rope_tables(pos_ref[...], f_ref[...])
        outs = _mla_prep(cq_ref[...].astype(F32), ckv_ref[...].astype(F32), kr_ref[...].astype(F32), gq_ref[...], gkv_ref[...], wq_ref[...], wkv_ref[...],
                         cos, sin)
        for i in range(nq):
            q_ref[:, pl.ds(i * LANE, LANE)] = outs[i].astype(q_ref.dtype)
            k_ref[:, pl.ds(i * LANE, LANE)] = outs[nq + i].astype(k_ref.dtype)
        for h in range(MLA_HEADS):
            v_ref[:, pl.ds(h * LANE, LANE)] = outs[2 * nq + h].astype(v_ref.dtype)

    row = lambda w, j: pl.BlockSpec((tm, w), functools.partial(lambda i, j: (i, j), j=j))
    return pl.pallas_call(
        body, grid=(t // tm,),
        in_specs=[row(Q_LORA, P_CQ // Q_LORA), row(KV_LORA, P_CKV // KV_LORA), row(LANE, P_KR // LANE),
                  pl.BlockSpec((tm, 1), lambda i: (i, 0)), _full(inv_freq2.shape), _full(gq.shape), _full(gkv.shape),
                  _full(w_uq.shape), _full(w_ukv.shape)],
        out_specs=[row(nq * LANE, 0), row(nq * LANE, 0), row(MLA_VW, 0)],
        out_shape=[jax.ShapeDtypeStruct((t, nq * LANE), BF), jax.ShapeDtypeStruct((t, nq * LANE), BF),
                   jax.ShapeDtypeStruct((t, MLA_VW), BF)],
        name="mla_prep_fwd", compiler_params=pltpu.CompilerParams(dimension_semantics=("arbitrary",)),
    )(proj, proj, proj, pos_col, inv_freq2, gq, gkv, w_uq, w_ukv)


def _mla_prep_bwd(proj, pos_col, inv_freq2, gq, gkv, w_uq, w_ukv, dq, dk, dv, tm):
    t = proj.shape[0]
    nq = 2 * MLA_HEADS

    def body(cq_ref, ckv_ref, kr_ref, pos_ref, f_ref, gq_ref, gkv_ref, wq_ref, wkv_ref, dq_ref, dk_ref, dv_ref,
             dcq_ref, dckv_ref, dkr_ref, dgq_ref, dgkv_ref, dwq_ref, dwkv_ref):
        @pl.when(pl.program_id(0) == 0)
        def _():
            for o in (dgq_ref, dgkv_ref, dwq_ref, dwkv_ref):
                o[...] = jnp.zeros_like(o)

        cos, sin = _rope_tables(pos_ref[...], f_ref[...])
        f = functools.partial(_mla_prep, cos=cos, sin=sin)
        _, vjp = jax.vjp(f, cq_ref[...].astype(F32), ckv_ref[...].astype(F32), kr_ref[...].astype(F32), gq_ref[...], gkv_ref[...], wq_ref[...], wkv_ref[...])
        cts = [dq_ref[:, pl.ds(i * LANE, LANE)] for i in range(nq)]
        cts += [dk_ref[:, pl.ds(i * LANE, LANE)] for i in range(nq)]
        cts += [dv_ref[:, pl.ds(h * LANE, LANE)] for h in range(MLA_HEADS)]
        dcq, dckv, dkr, dgq, dgkv, dwq, dwkv = vjp(cts)
        dcq_ref[...] = dcq.astype(dcq_ref.dtype)
        dckv_ref[...] = dckv.astype(dckv_ref.dtype)
        dkr_ref[...] = dkr.astype(dkr_ref.dtype)
        dgq_ref[...] += dgq
        dgkv_ref[...] += dgkv
        dwq_ref[...] += dwq
        dwkv_ref[...] += dwkv

    row = lambda w, j: pl.BlockSpec((tm, w), functools.partial(lambda i, j: (i, j), j=j))
    return pl.pallas_call(
        body, grid=(t // tm,),
        in_specs=[row(Q_LORA, P_CQ // Q_LORA), row(KV_LORA, P_CKV // KV_LORA), row(LANE, P_KR // LANE),
                  pl.BlockSpec((tm, 1), lambda i: (i, 0)), _full(inv_freq2.shape), _full(gq.shape), _full(gkv.shape),
                  _full(w_uq.shape), _full(w_ukv.shape), row(nq * LANE, 0), row(nq * LANE, 0), row(MLA_VW, 0)],
        out_specs=[row(Q_LORA, 0), row(KV_LORA, 0), row(LANE, 0), _full(gq.shape), _full(gkv.shape),
                   _full(w_uq.shape), _full(w_ukv.shape)],
        out_shape=[jax.ShapeDtypeStruct((t, Q_LORA), BF), jax.ShapeDtypeStruct((t, KV_LORA), BF),
                   jax.ShapeDtypeStruct((t, LANE), BF), jax.ShapeDtypeStruct(gq.shape, F32),
                   jax.ShapeDtypeStruct(gkv.shape, F32), jax.ShapeDtypeStruct(w_uq.shape, F32),
                   jax.ShapeDtypeStruct(w_ukv.shape, F32)],
        name="mla_prep_bwd", compiler_params=pltpu.CompilerParams(dimension_semantics=("arbitrary",)),
    )(proj, proj, proj, pos_col, inv_freq2, gq, gkv, w_uq, w_ukv, dq, dk, dv)


_NEG = -1e30
_LN2 = math.log(2.0)
ATT_CHAINS = 2


def _causal(tq, tk, q0, k0):
    row = q0 + lax.broadcasted_iota(jnp.int32, (tq, tk), 0)
    col = k0 + lax.broadcasted_iota(jnp.int32, (tq, tk), 1)
    return col <= row


def _attn_fwd(q, k, v, tq, tk):
    t = q.shape[0]

    assert tk % tq == 0 or tq % tk == 0
    n_diag = max(1, tq // tk)

    th = tq // ATT_CHAINS

    def body(q_ref, k_ref, v_ref, o_ref, lse_ref):
        i = pl.program_id(1)
        n_full = (i * tq) // tk

        def step(k0, carry, masked):
            out = []
            for c, (m, l, acc) in enumerate(carry):
                kw = min(tk, (c + 1) * th) if masked and tk == tq else tk
                kt = k_ref[pl.ds(k0, kw), :]
                vt = v_ref[pl.ds(k0, kw), :]
                s = _dot(q_ref[pl.ds(c * th, th), :], kt, NT)
                if masked:
                    s = jnp.where(_causal(th, kw, i * tq + c * th, k0), s, _NEG)
                m_new = jnp.maximum(m, jnp.max(s, axis=-1, keepdims=True))
                p = jnp.exp2(s - m_new)
                alpha = jnp.exp2(m - m_new)
                out.append((m_new, alpha * l + jnp.sum(p, axis=-1, keepdims=True), alpha * acc + _dot(p.astype(BF), vt)))
            return tuple(out)

        init = tuple((jnp.full((th, 1), _NEG, F32), jnp.zeros((th, 1), F32), jnp.zeros((th, V_HEAD), F32)) for _ in range(ATT_CHAINS))
        carry = lax.fori_loop(0, n_full, lambda j, c: step(pl.multiple_of(j * tk, tk), c, False), init)
        for dd in range(n_diag):
            carry = step(pl.multiple_of((n_full + dd) * tk, tk), carry, True)
        for c, (m, l, acc) in enumerate(carry):
            o_ref[pl.ds(c * th, th), :] = acc / l
            lse_ref[pl.ds(c * th, th), :] = jnp.broadcast_to(m + jnp.log2(l), (th, LANE))

    return pl.pallas_call(
        body, grid=(MLA_HEADS, t // tq),
        in_specs=[pl.BlockSpec((tq, 2 * LANE), lambda h, i: (i, h)), pl.BlockSpec((t, 2 * LANE), lambda h, i: (0, h)),
                  pl.BlockSpec((t, V_HEAD), lambda h, i: (0, h))],
        out_specs=[pl.BlockSpec((tq, V_HEAD), lambda h, i: (i, h)), pl.BlockSpec((tq, LANE), lambda h, i: (i, h))],
        out_shape=[jax.ShapeDtypeStruct((t, MLA_VW), F32), jax.ShapeDtypeStruct((t, MLA_HEADS * LANE), F32)],
        name="attn_fwd", compiler_params=pltpu.CompilerParams(dimension_semantics=("parallel", "arbitrary")),
    )(q, k, v)


def _attn_bwd(q, k, v, do, lse, delta, tq, tk):
    t = q.shape[0]
    nkt = t // tk
    assert tk % tq == 0

    def body(q_ref, k_ref, v_ref, do_ref, lse_ref, dl_ref, dq_ref, dk_ref, dv_ref):
        j = pl.program_id(1)

        @pl.when(j == 0)
        def _():
            dq_ref[...] = jnp.zeros_like(dq_ref)

        kt = k_ref[...]
        vt = v_ref[...]

        def step(q0, carry, masked, kw=tk):
            dk, dv = carry
            rows = pl.ds(q0, tq)
            qt = q_ref[rows, :]
            dot_ = do_ref[rows, :]
            ktw, vtw = kt[:kw], vt[:kw]
            p = jnp.exp2(_dot(qt, ktw, NT) - lse_ref[rows, pl.ds(0, 1)])
            if masked:
                p = jnp.where(_causal(tq, kw, q0, j * tk), p, 0.0)
            dv_w = _dot(p.astype(BF), dot_, TN)
            ds = (p * (_dot(dot_, vtw, NT) - dl_ref[rows, pl.ds(0, 1)])).astype(BF)
            dk_w = _dot(ds, qt, TN)
            dq_ref[rows, :] += _dot(ds, ktw)
            if kw == tk:
                return dk + dk_w, dv + dv_w
            return (jnp.concatenate([dk[:kw] + dk_w, dk[kw:]], axis=0), jnp.concatenate([dv[:kw] + dv_w, dv[kw:]], axis=0))

        per = tk // tq
        carry = (jnp.zeros((tk, 2 * LANE), F32), jnp.zeros((tk, V_HEAD), F32))
        for dd in range(per):
            carry = step(pl.multiple_of(j * tk + dd * tq, tq), carry, True, kw=(dd + 1) * tq)

        def group(g, c):
            for dd in range(per):
                c = step(pl.multiple_of(g * tk + dd * tq, tq), c, False)
            return c

        dk, dv = lax.fori_loop(j + 1, nkt, group, carry)
        dk_ref[...] = dk * _LN2
        dv_ref[...] = dv

        @pl.when(j == nkt - 1)
        def _():
            dq_ref[...] = dq_ref[...] * _LN2

    return pl.pallas_call(
        body, grid=(MLA_HEADS, nkt),
        in_specs=[pl.BlockSpec((t, 2 * LANE), lambda h, j: (0, h)), pl.BlockSpec((tk, 2 * LANE), lambda h, j: (j, h)),
                  pl.BlockSpec((tk, V_HEAD), lambda h, j: (j, h)), pl.BlockSpec((t, V_HEAD), lambda h, j: (0, h)),
                  pl.BlockSpec((t, LANE), lambda h, j: (0, h)), pl.BlockSpec((t, LANE), lambda h, j: (0, h))],
        out_specs=[pl.BlockSpec((t, 2 * LANE), lambda h, j: (0, h)), pl.BlockSpec((tk, 2 * LANE), lambda h, j: (j, h)),
                   pl.BlockSpec((tk, V_HEAD), lambda h, j: (j, h))],
        out_shape=[jax.ShapeDtypeStruct((t, MLA_HEADS * 2 * LANE), F32), jax.ShapeDtypeStruct((t, MLA_HEADS * 2 * LANE), F32),
                   jax.ShapeDtypeStruct((t, MLA_VW), F32)],
        name="attn_bwd", compiler_params=pltpu.CompilerParams(dimension_semantics=("parallel", "arbitrary")),
    )(q, k, v, do, lse, delta)


def _adam_update(w, g, m, v):
    mm = ADAM_B1 * m + (1.0 - ADAM_B1) * g
    vv = ADAM_B2 * v + (1.0 - ADAM_B2) * jnp.square(g)
    m_hat = mm / (1.0 - ADAM_B1 ** ADAM_STEP)
    v_hat = vv / (1.0 - ADAM_B2 ** ADAM_STEP)
    return -ADAM_LR * (m_hat / (jnp.sqrt(v_hat) + ADAM_EPS) + ADAM_WD * w), mm, vv


def _adamw(w, g, m, v, name):
    r, c = w.shape
    tr = max([r // s for s in range(1, r // 8 + 1) if r % s == 0 and (r // s) % 8 == 0 and r // s <= 512] or [r])
    slots = g.ndim == 3

    def body(w_ref, g_ref, m_ref, v_ref, g_out, d_ref, nm_ref, nv_ref):
        if slots:
            gg = g_ref[0].astype(F32)
            for s in range(1, N_DEV):
                gg = gg + g_ref[s].astype(F32)
        else:
            gg = g_ref[...]
        g_out[...] = gg
        d_ref[...], nm_ref[...], nv_ref[...] = _adam_update(w_ref[...], gg, m_ref[...], v_ref[...])

    spec = pl.BlockSpec((tr, c), lambda i: (i, 0))
    g_spec = pl.BlockSpec((N_DEV, tr, c), lambda i: (0, i, 0)) if slots else spec
    return pl.pallas_call(
        body, grid=(r // tr,), in_specs=[spec, g_spec, spec, spec], out_specs=[spec] * 4,
        out_shape=[jax.ShapeDtypeStruct((r, c), F32)] * 4, name=name,
        compiler_params=pltpu.CompilerParams(dimension_semantics=("arbitrary",)),
    )(w, g, m, v)


def _adamw_many(ws, gs, ms, vs, name):
    n = len(ws)

    def body(*refs):
        for i in range(n):
            w_ref, g_ref, m_ref, v_ref = (refs[j * n + i] for j in range(4))
            d_ref, nm_ref, nv_ref = (refs[(4 + j) * n + i] for j in range(3))
            d_ref[...], nm_ref[...], nv_ref[...] = _adam_update(w_ref[...], g_ref[...], m_ref[...], v_ref[...])

    shapes = [jax.ShapeDtypeStruct(w.shape, F32) for w in ws]
    outs = pl.pallas_call(body, out_shape=shapes * 3, name=name)(*ws, *gs, *ms, *vs)
    return outs[:n], outs[n:2 * n], outs[2 * n:]


def _cast_bf16(xs, name, after=None):
    n = len(xs)
    extra = [] if after is None else [after]

    def body(*refs):
        outs = refs[n + len(extra):]
        for i in range(n):
            outs[i][...] = refs[i][...].astype(BF)

    vmem = pl.BlockSpec(memory_space=pltpu.VMEM)
    return pl.pallas_call(
        body, out_shape=[jax.ShapeDtypeStruct(x.shape, BF) for x in xs], name=name,
        in_specs=[vmem] * n + [pl.BlockSpec(memory_space=pl.ANY)] * len(extra), out_specs=[vmem] * n)(*xs, *extra)


def _pad_rows(a, n):
    return jnp.pad(a, ((0, n - a.shape[0]), (0, 0)))


def _w_in_to_padded(wt):
    s_ba = P_CQ
    s_cq = s_ba + 2 * DN_HEADS
    s_kr = s_cq + Q_LORA + KV_LORA
    return jnp.concatenate([wt[:s_ba], wt[s_cq:s_kr], _pad_rows(wt[s_ba:s_cq], LANE), _pad_rows(wt[s_kr:], LANE)], axis=0)


def _w_in_from_padded(wt):
    return jnp.concatenate([wt[:P_CQ], wt[P_BA:P_BA + 2 * DN_HEADS], wt[P_CQ:P_BA], wt[P_KR:P_KR + QK_ROPE]], axis=0)


def _w_uq_to_padded(wt):
    w3 = wt.reshape(MLA_HEADS, QK_NOPE + QK_ROPE, Q_LORA)
    nope = w3[:, :QK_NOPE].reshape(MLA_HEADS * QK_NOPE, Q_LORA)
    rope = jnp.pad(w3[:, QK_NOPE:], ((0, 0), (0, LANE - QK_ROPE), (0, 0))).reshape(MLA_HEADS * LANE, Q_LORA)
    return jnp.concatenate([nope, rope], axis=0)


def _w_uq_from_padded(wt):
    nope = wt[:MLA_HEADS * QK_NOPE].reshape(MLA_HEADS, QK_NOPE, Q_LORA)
    rope = wt[MLA_HEADS * QK_NOPE:].reshape(MLA_HEADS, LANE, Q_LORA)[:, :QK_ROPE]
    return jnp.concatenate([nope, rope], axis=1).reshape(MLA_HEADS * (QK_NOPE + QK_ROPE), Q_LORA)


def _pack(pieces, width, row_mult):
    flat = jnp.concatenate([p.reshape(-1) for p in pieces])
    n = flat.shape[0]
    rows = -(-n // (width * row_mult)) * row_mult
    return jnp.pad(flat, (0, rows * width - n)).reshape(rows, width)


def _unpack(flat, shapes):
    out, o = [], 0
    for s in shapes:
        n = math.prod(s)
        out.append(flat[o:o + n].reshape(s))
        o += n
    return out


def kernel(x, c, positions, w_ada, b_ada, w_in, conv_w, a_log, dt_bias, dn_norm_g, q_norm_g, w_uq, kv_norm_g, w_ukv, w_o, ln1_g, ln1_b, w_gate, w_up, w_down, ln2_g, ln2_b, loss_target, m_w_ada, m_b_ada, m_w_in, m_conv_w, m_a_log, m_dt_bias, m_dn_norm_g, m_q_norm_g, m_w_uq, m_kv_norm_g, m_w_ukv, m_w_o, m_ln1_g, m_ln1_b, m_w_gate, m_w_up, m_w_down, m_ln2_g, m_ln2_b, v_w_ada, v_b_ada, v_w_in, v_conv_w, v_a_log, v_dt_bias, v_dn_norm_g, v_q_norm_g, v_w_uq, v_kv_norm_g, v_w_ukv, v_w_o, v_ln1_g, v_ln1_b, v_w_gate, v_w_up, v_w_down, v_ln2_g, v_ln2_b):
    me = 4 * lax.axis_index("x") + 2 * lax.axis_index("y") + lax.axis_index("c")
    t, d = x.shape[1], x.shape[2]
    ada_n = w_ada.shape[2]

    tr = lambda w: w[0].T
    rows = lambda a: a.reshape(-1, a.shape[2])
    (in_shard,) = _cast_bf16([tr(w_in)], "cast_w_in")
    cw = conv_w.shape[3]
    a_in, c_all, conv_all = _gather_by_chip([in_shard, c, conv_w[0, :, 0, :]], "gather_w_in_and_small")
    c_all = c_all.reshape(N_DEV, d)
    conv_full = conv_all.transpose(1, 0, 2).reshape(CONV_K, N_DEV * cw)
    conv_w8 = jnp.pad(conv_full, ((0, 8 - CONV_K), (0, 0)))

    b_ada_mine = lax.dynamic_slice(b_ada, (0, me * ada_n), (1, ada_n))
    mod_cols = _mod_fwd(c_all, w_ada[0], b_ada_mine)
    (mod_all,) = _exchange([mod_cols.reshape(N_DEV, 1, ada_n)], "scatter_mod", scatter=True)
    mod = mod_all.reshape(1, N_DEV * ada_n)

    later = _cast_bf16([tr(w_uq), tr(w_ukv), w_o[0], tr(w_gate), tr(w_up), w_down[0]], "cast_weights", after=mod)
    mixer_gather, token_a = _exchange_start(later[:3], "gather_mixer_weights_start", scatter=False)
    ffn_gather, token_b = _exchange_start(later[3:], "gather_ffn_weights_start", scatter=False)
    mod = mod + (token_a + token_b)
    w_in_t = _w_in_to_padded(rows(a_in))

    def mixer_weights(after):
        a_uq, a_ukv, a_o = _exchange_wait(mixer_gather, after, "gather_mixer_weights_wait", scatter=False)
        return _w_uq_to_padded(rows(a_uq)), rows(a_ukv), rows(a_o)

    def ffn_weights(after):
        a_gate, a_up, a_down = _exchange_wait(ffn_gather, after, "gather_ffn_weights_wait", scatter=False)
        return rows(a_gate), rows(a_up), rows(a_down)

    def by_dest(g):
        return g.reshape(N_DEV, -1, g.shape[1])

    scatters = {}

    def grads_ready(tag, *g):
        if tag == "ffn":
            pieces = [by_dest(a) for a in g]
        elif tag == "mixer":
            g_w_o, g_w_uq_t, g_w_ukv_t = g
            pieces = [by_dest(g_w_o), by_dest(_w_uq_from_padded(g_w_uq_t).astype(BF)), by_dest(g_w_ukv_t.astype(BF))]
        else:
            pieces = [by_dest(_w_in_from_padded(g[0]))]
        scatters[tag], token = _exchange_start(pieces, "scatter_%s_grads_start" % tag, scatter=True)
        return token

    loc = _local_step(x[0], loss_target[0], positions[0], mod, w_in_t, mixer_weights, ffn_weights, grads_ready,
                      conv_w8, a_log, dt_bias, dn_norm_g, q_norm_g, kv_norm_g, ln1_g, ln1_b, ln2_g, ln2_b)
    grad_x, loss_acc, dmod, d_conv8, d_al8, d_dt8, d_dn_g, d_q_g, d_kv_g, d_ln1_g, d_ln1_b, d_ln2_g, d_ln2_b = loc

    small_shapes = [(6 * d,), (CONV_K, N_DEV * cw), (DN_HEADS,), (DN_HEADS,), (DN_DV,), (Q_LORA,), (KV_LORA,), (d,), (d,), (d,), (d,), (1,)]
    gsmall = _pack([dmod, d_conv8[:CONV_K], d_al8[0, :DN_HEADS], d_dt8[0, :DN_HEADS], d_dn_g, d_q_g, d_kv_g,
                    d_ln1_g, d_ln1_b, d_ln2_g, d_ln2_b, loss_acc[0, :1]], LANE, 8)
    (gsmall_all,) = _exchange([gsmall], "gather_small_grads", scatter=False)
    dmod_all = gsmall_all.reshape(N_DEV, -1)[:, :6 * d]
    tot = _unpack(_sum_slots(gsmall_all, "sum_small_grads").reshape(-1), small_shapes)
    g_b_ada, g_conv_full, g_a_log, g_dt_bias, g_dn_g, g_q_g, g_kv_g, g_ln1_g, g_ln1_b, g_ln2_g, g_ln2_b, loss1 = tot
    loss = loss1.reshape(())
    g_conv_w = lax.dynamic_slice(g_conv_full, (0, me * cw), (CONV_K, cw))
    g_w_ada = _mod_bwd(c_all.T, lax.dynamic_slice(dmod_all, (0, me * ada_n), (N_DEV, ada_n)))

    grads = {"w_ada": g_w_ada[None], "b_ada": g_b_ada[None], "conv_w": g_conv_w[None, :, None, :],
             "a_log": g_a_log[None], "dt_bias": g_dt_bias[None], "dn_norm_g": g_dn_g[None], "q_norm_g": g_q_g[None],
             "kv_norm_g": g_kv_g[None], "ln1_g": g_ln1_g[None], "ln1_b": g_ln1_b[None], "ln2_g": g_ln2_g[None], "ln2_b": g_ln2_b[None]}
    weights = dict(w_ada=w_ada, b_ada=b_ada, w_in=w_in, conv_w=conv_w, a_log=a_log, dt_bias=dt_bias, dn_norm_g=dn_norm_g,
                   q_norm_g=q_norm_g, w_uq=w_uq, kv_norm_g=kv_norm_g, w_ukv=w_ukv, w_o=w_o, ln1_g=ln1_g, ln1_b=ln1_b,
                   w_gate=w_gate, w_up=w_up, w_down=w_down, ln2_g=ln2_g, ln2_b=ln2_b)
    ms = dict(w_ada=m_w_ada, b_ada=m_b_ada, w_in=m_w_in, conv_w=m_conv_w, a_log=m_a_log, dt_bias=m_dt_bias,
              dn_norm_g=m_dn_norm_g, q_norm_g=m_q_norm_g, w_uq=m_w_uq, kv_norm_g=m_kv_norm_g, w_ukv=m_w_ukv, w_o=m_w_o,
              ln1_g=m_ln1_g, ln1_b=m_ln1_b, w_gate=m_w_gate, w_up=m_w_up, w_down=m_w_down, ln2_g=m_ln2_g, ln2_b=m_ln2_b)
    vs = dict(w_ada=v_w_ada, b_ada=v_b_ada, w_in=v_w_in, conv_w=v_conv_w, a_log=v_a_log, dt_bias=v_dt_bias,
              dn_norm_g=v_dn_norm_g, q_norm_g=v_q_norm_g, w_uq=v_w_uq, kv_norm_g=v_kv_norm_g, w_ukv=v_w_ukv, w_o=v_w_o,
              ln1_g=v_ln1_g, ln1_b=v_ln1_b, w_gate=v_w_gate, w_up=v_w_up, w_down=v_w_down, ln2_g=v_ln2_g, ln2_b=v_ln2_b)
    names = list(weights)
    big = ("w_ada", "w_gate", "w_up", "w_down", "w_o", "w_uq", "w_ukv", "w_in")
    waits = {"w_gate": ("ffn", ("w_gate", "w_up", "w_down")), "w_o": ("mixer", ("w_o", "w_uq", "w_ukv")), "w_in": ("in", ("w_in",))}
    delta_w, new_m, new_v, slots = {}, {}, {}, {}
    last = g_w_ada
    for n in big:
        if n == "w_in":
            rest = [r for r in names if r not in big]
            flat2 = lambda a: a.reshape(-1, a.shape[-1])
            outs = _adamw_many(*[[flat2(src[r]) for r in rest] for src in (weights, grads, ms, vs)], "adamw_small")
            for dst, o in zip((delta_w, new_m, new_v), outs):
                for r, a in zip(rest, o):
                    dst[r] = a.reshape(weights[r].shape)
            last = outs[0][0]
        transposed = n in ("w_in", "w_uq", "w_ukv", "w_gate", "w_up")
        two = (lambda a: a[0].T) if transposed else (lambda a: a[0])
        back = (lambda a: a.T[None]) if transposed else (lambda a: a[None])
        if n in waits:
            tag, members = waits[n]
            slots.update(zip(members, _exchange_wait(scatters[tag], last, "scatter_%s_grads_wait" % tag, scatter=True)))
        g_in = slots[n] if n in slots else two(grads[n])
        gr, dlt, nm, nv = _adamw(two(weights[n]), g_in, two(ms[n]), two(vs[n]), "adamw_" + n)
        grads[n], delta_w[n], new_m[n], new_v[n] = back(gr), back(dlt), back(nm), back(nv)
        last = nv

    return (loss, grad_x[None], *[grads[n] for n in names], *[delta_w[n] for n in names],
            *[new_m[n] for n in names], *[new_v[n] for n in names])


def _local_step(xs, tgt, pos, mod, w_in_t, mixer_weights, ffn_weights, grads_ready, conv_w8,
                a_log, dt_bias, dn_norm_g, q_norm_g, kv_norm_g, ln1_g, ln1_b, ln2_g, ln2_b):
    t, d = xs.shape
    sh_m, sc_m, gt_m, sh_f, sc_f, gt_f = [mod[:, i * d:(i + 1) * d] for i in range(6)]
    pos_col = pos.astype(F32).reshape(t, 1)
    inv_freq = 1.0 / (ROPE_THETA ** (jnp.arange(0, QK_ROPE, 2, dtype=F32) / QK_ROPE))
    inv_freq2 = jnp.pad(jnp.concatenate([inv_freq, inv_freq]), (0, LANE - QK_ROPE)).reshape(1, LANE)
    al8 = jnp.pad(a_log, ((0, 7), (0, LANE - DN_HEADS)))
    dt8 = jnp.pad(dt_bias, ((0, 7), (0, LANE - DN_HEADS)))

    tm = min(512, t)
    tq = min(256, t)
    tk = min(512, t)

    def modulate_in(xx, sc, sh, w_ba):
        h = (xx * (1.0 + sc) + sh).astype(BF)
        return h, _dot(h, w_ba, NT)

    h1, ba_raw = _rowwise("modulate_in", modulate_in, [xs], [sc_m, sh_m, w_in_t[P_BA:P_BA + LANE]], [(d, BF), (LANE, F32)], [], tm)
    proj = _matmul(h1, w_in_t, "nt", "in_proj", BF)
    qkv = _conv_fwd(proj, conv_w8, min(256, t))
    gdn_tm = min(512, t)
    intra, inverses = _gdn_intra_fwd(qkv, ba_raw, al8, dt8, gdn_tm)
    o_dn, states = _gdn_scan_fwd(intra, gdn_tm)
    w_uq_t, w_ukv_t, w_o_f = mixer_weights(states)
    qc, kc, vc = _mla_prep_fwd(proj, pos_col, inv_freq2, q_norm_g, kv_norm_g, w_uq_t, w_ukv_t, tm)
    o_mla, lse = _attn_fwd(qc, kc, vc, min(1024, t), min(1024, t))

    def mix_in(o, z, om, g):
        return jnp.concatenate(_gdn_out(o, z.astype(F32), g) + [om], axis=1)

    (mixin,) = _rowwise("mixer_out", mix_in, [o_dn, (proj, DN_VW, P_Z // DN_VW), o_mla], [dn_norm_g], [(2 * DN_VW, BF)], [], tm)
    mix = _matmul(mixin, w_o_f, "nn", "out_proj", BF)

    def block1(xx, mx, gt, g1, b1, sc, sh):
        x1 = _layernorm(DEEPNORM_ALPHA * xx + gt * mx, g1, b1)
        return x1, x1 * (1.0 + sc) + sh

    x1, h2 = _rowwise("norm1_modulate", block1, [xs, mix], [gt_m, ln1_g, ln1_b, sc_f, sh_f], [(d, F32), (d, BF)], [], tm)
    w_gate_f, w_up_f, w_down_f = ffn_weights(h2)
    act, act_dg, act_du = _ffn_in(h2, w_gate_f, w_up_f)
    ff = _matmul(act, w_down_f, "nn", "ffn_out", BF)

    def tail_loss(x1_, ff_, gt, g2, b2, tg):
        y = _layernorm(DEEPNORM_ALPHA * x1_ + gt * ff_, g2, b2)
        return 0.5 * jnp.sum(jnp.mean(jnp.square(y - tg), axis=-1))

    def tail(x1_, ff_, tg, gt, g2, b2):
        loss, (dx1, dff, dgt, dg2, db2) = jax.value_and_grad(tail_loss, argnums=(0, 1, 2, 3, 4))(x1_, ff_, gt, g2, b2, tg)
        return dx1, dff, jnp.full((1, LANE), loss, F32), dgt, dg2, db2

    dx1_a, dff, loss_acc, d_gt_f, d_ln2_g, d_ln2_b = _rowwise(
        "norm2_loss", tail, [x1, ff, tgt], [gt_f, ln2_g, ln2_b], [(d, BF), (d, BF)], [(1, LANE), (1, d), (1, d), (1, d)], tm)

    g_w_down = _matmul(act, dff, "tn", "d_w_down", BF)
    dgate, dup = _ffn_act_bwd(dff, w_down_f, act_dg, act_du)
    g_w_gate = _matmul(dgate, h2, "tn", "d_w_gate", BF)
    g_w_up = _matmul(dup, h2, "tn", "d_w_up", BF)
    token = grads_ready("ffn", g_w_gate, g_w_up, g_w_down)
    dh2 = _matmul2_nn(dgate, w_gate_f, dup, w_up_f, "d_ffn_in", BF)

    def block1_bwd(xx, mx, dx1_, dh2_, gt, g1, b1, sc, sh):
        _, vjp = jax.vjp(block1, xx, mx, gt, g1, b1, sc, sh)
        dxx, dmx, dgt, dg1, db1, dsc, dsh = vjp((dx1_.astype(F32), dh2_.astype(F32)))
        return dxx, dmx, dgt, dg1, db1, dsc, dsh

    dx_a, dmix, d_gt_m, d_ln1_g, d_ln1_b, d_sc_f, d_sh_f = _rowwise(
        "norm1_modulate_bwd", block1_bwd, [xs, mix, dx1_a, dh2], [gt_m + token, ln1_g, ln1_b, sc_f, sh_f],
        [(d, F32), (d, BF)], [(1, d)] * 5, tm)

    dmixin = _matmul(dmix, w_o_f, "nt", "d_mixer_out", BF)
    g_w_o = _matmul(mixin, dmix, "tn", "d_w_o", BF)

    def mixer_bwd(o, z, om, dmi, g):
        _, vjp = jax.vjp(lambda o_, z_, g_: jnp.concatenate(_gdn_out(o_, z_, g_), axis=1), o, z.astype(F32), g)
        do_, dz_, dg_ = vjp(dmi[:, :DN_VW].astype(F32))
        dom = dmi[:, DN_VW:]
        delta = [jnp.broadcast_to(jnp.sum(dom[:, h * V_HEAD:(h + 1) * V_HEAD] * om[:, h * V_HEAD:(h + 1) * V_HEAD], axis=-1, keepdims=True), (o.shape[0], LANE))
                 for h in range(MLA_HEADS)]
        return do_, dz_, dom, jnp.concatenate(delta, axis=1), dg_

    do_dn, dz, do_mla, delta, d_dn_g = _rowwise(
        "mixer_out_bwd", mixer_bwd, [o_dn, (proj, DN_VW, P_Z // DN_VW), o_mla, dmixin], [dn_norm_g],
        [(DN_VW, F32), (DN_VW, BF), (MLA_VW, BF), (MLA_HEADS * LANE, F32)], [(1, DN_DV)], tm)

    dqc, dkc, dvc = _attn_bwd(qc, kc, vc, do_mla, lse, delta, min(512, t), min(1024, t))
    dcq, dckv, dkr, d_q_g, d_kv_g, g_w_uq_t, g_w_ukv_t = _mla_prep_bwd(
        proj, pos_col, inv_freq2, q_norm_g, kv_norm_g, w_uq_t, w_ukv_t, dqc, dkc, dvc, tm)

    token = grads_ready("mixer", g_w_o, g_w_uq_t, g_w_ukv_t)

    d_intra = _gdn_scan_bwd(intra, states, do_dn, gdn_tm)
    dqkv_act, dba, d_al8, d_dt8 = _gdn_intra_bwd(qkv, ba_raw, al8 + token, dt8, inverses, d_intra, min(256, t))
    dproj, d_conv8 = _conv_bwd(proj, conv_w8, dqkv_act, [(P_Z, dz), (P_CQ, dcq), (P_CKV, dckv), (P_BA, dba), (P_KR, dkr)],
                               min(256, t))
    dh1 = _matmul(dproj, w_in_t, "nn", "d_in_proj", BF)
    g_w_in_t = _matmul(dproj, h1, "tn", "d_w_in", BF)
    token = grads_ready("in", g_w_in_t)

    def modulate_bwd(xx, dh, dxa, sc):
        dh = dh.astype(F32)
        return dh * (1.0 + sc) + dxa, jnp.sum(dh * xx, axis=0, keepdims=True), jnp.sum(dh, axis=0, keepdims=True)

    grad_x, d_sc_m, d_sh_m = _rowwise("modulate_in_bwd", modulate_bwd, [xs, dh1, dx_a], [sc_m + token], [(d, F32)], [(1, d), (1, d)], tm)
    dmod = jnp.concatenate([d_sh_m, d_sc_m, d_gt_m, d_sh_f, d_sc_f, d_gt_f], axis=1)
    return grad_x, loss_acc, dmod, d_conv8, d_al8, d_dt8, d_dn_g, d_q_g, d_kv_g, d_ln1_g, d_ln1_b, d_ln2_g, d_ln2_b
```

```python
import functools
import math

import jax
import jax.numpy as jnp
from jax import lax
from jax.experimental import pallas as pl
from jax.experimental.pallas import tpu as pltpu

F32 = jnp.float32
BF = jnp.bfloat16
HI = lax.Precision.HIGHEST

N_DEV = 8
DN_HEADS = 4
DN_DK = 128
DN_DV = 128
CONV_K = 4
CHUNK = 64
MLA_HEADS = 4
QK_NOPE = 128
QK_ROPE = 64
V_HEAD = 128
Q_LORA = 512
KV_LORA = 256
ROPE_THETA = 10000.0
DEPTH = 1
DEEPNORM_ALPHA = (2.0 * DEPTH) ** 0.25
LANE = 128
CONV_HALO = 8
GL_ROWS = 8
CONV_ROWS, CONV_COLS = 64, 256

DN_QK = DN_HEADS * DN_DK
DN_VW = DN_HEADS * DN_DV
DN_CONV_CH = 2 * DN_QK + DN_VW
MLA_VW = MLA_HEADS * V_HEAD
P_Z = DN_CONV_CH
P_CQ = P_Z + DN_VW
P_CKV = P_CQ + Q_LORA
P_BA = P_CKV + KV_LORA
P_KR = P_BA + LANE
N_INP = P_KR + LANE

ADAM_LR = 0.001
ADAM_B1 = 0.9
ADAM_B2 = 0.999
ADAM_EPS = 1e-08
ADAM_WD = 0.01
ADAM_STEP = 10

NN = (((1,), (0,)), ((), ()))
NT = (((1,), (1,)), ((), ()))
TN = (((0,), (0,)), ((), ()))


def _pick(n, prefs):
    for p in prefs:
        if n % p == 0:
            return p
    return n


def _full(shape):
    return pl.BlockSpec(shape, lambda *_: (0,) * len(shape))


def _dot(a, b, dims=NN):
    return lax.dot_general(a, b, dims, preferred_element_type=F32)


def _doth(a, b, dims=NN):
    return lax.dot_general(a, b, dims, precision=HI, preferred_element_type=F32)


@jax.custom_vjp
def _mmb(a, b):
    return _dot(a.astype(BF), b.astype(BF), NN)


def _mmb_fwd(a, b):
    return _mmb(a, b), (a, b)


def _mmb_bwd(res, g):
    a, b = res
    gb = g.astype(BF)
    return (_dot(gb, b.astype(BF), NT).astype(a.dtype), _dot(a.astype(BF), gb, TN).astype(b.dtype))


_mmb.defvjp(_mmb_fwd, _mmb_bwd)


@jax.custom_vjp
def _mmb_nt(a, b):
    return _dot(a.astype(BF), b.astype(BF), NT)


def _mmb_nt_fwd(a, b):
    return _mmb_nt(a, b), (a, b)


def _mmb_nt_bwd(res, g):
    a, b = res
    gb = g.astype(BF)
    return (_dot(gb, b.astype(BF), NN).astype(a.dtype), _dot(gb, a.astype(BF), TN).astype(b.dtype))


_mmb_nt.defvjp(_mmb_nt_fwd, _mmb_nt_bwd)


def _sigmoid(x):
    return 0.5 * (jnp.tanh(0.5 * x) + 1.0)


def _silu(x):
    return x * _sigmoid(x)


def _softplus(x):
    return jnp.maximum(x, 0.0) + jnp.log(1.0 + jnp.exp(-jnp.abs(x)))


def _layernorm(x, g, b, eps=1e-5):
    mu = jnp.mean(x, axis=-1, keepdims=True)
    xc = x - mu
    var = jnp.mean(xc * xc, axis=-1, keepdims=True)
    return xc * lax.rsqrt(var + eps) * g + b


def _rmsnorm(x, g, eps=1e-6):
    return x * lax.rsqrt(jnp.mean(x * x, axis=-1, keepdims=True) + eps) * g


def _l2norm(x, eps=1e-6):
    return x * lax.rsqrt(jnp.sum(x * x, axis=-1, keepdims=True) + eps)


def _rowwise(name, fn, rows, vecs, out_rows, out_accs, tm):
    rows = [r if isinstance(r, tuple) else (r, r.shape[1], 0) for r in rows]
    t = rows[0][0].shape[0]
    tm = min(tm, t)
    assert t % tm == 0
    nr, nv, no = len(rows), len(vecs), len(out_rows)

    def body(*refs):
        ins = [r[...] for r in refs[:nr + nv]]
        outs = fn(*ins)
        outs = outs if isinstance(outs, (tuple, list)) else (outs,)
        o_rows = refs[nr + nv:nr + nv + no]
        o_accs = refs[nr + nv + no:]
        for o, val in zip(o_rows, outs[:no]):
            o[...] = val.astype(o.dtype)
        if o_accs:
            @pl.when(pl.program_id(0) == 0)
            def _():
                for o in o_accs:
                    o[...] = jnp.zeros_like(o)
            for o, val in zip(o_accs, outs[no:]):
                o[...] += val

    in_specs = [pl.BlockSpec((tm, w), functools.partial(lambda i, j: (i, j), j=j)) for (_, w, j) in rows]
    in_specs += [_full(v.shape) for v in vecs]
    out_specs = [pl.BlockSpec((tm, w), lambda i: (i, 0)) for (w, _) in out_rows]
    out_specs += [_full(s) for s in out_accs]
    out_shape = [jax.ShapeDtypeStruct((t, w), d) for (w, d) in out_rows]
    out_shape += [jax.ShapeDtypeStruct(s, F32) for s in out_accs]
    res = pl.pallas_call(
        body, grid=(t // tm,), in_specs=in_specs, out_specs=out_specs, out_shape=out_shape, name=name,
        compiler_params=pltpu.CompilerParams(dimension_semantics=("arbitrary",)),
    )(*[r[0] for r in rows], *vecs)
    return res


def _matmul(a, b, mode, name, out_dtype=F32):
    if mode == "nn":
        (m, k), n = a.shape, b.shape[1]
    elif mode == "nt":
        (m, k), n = a.shape, b.shape[0]
    else:
        (k, m), n = a.shape, b.shape[1]
    tm, tn, tk = _matmul_tiles(m, n, k, a.dtype.itemsize, b.dtype.itemsize, jnp.dtype(out_dtype).itemsize)
    nk = k // tk
    dims = {"nn": NN, "nt": NT, "tn": TN}[mode]

    def body(a_ref, b_ref, o_ref, *acc):
        part = _dot(a_ref[...].astype(BF), b_ref[...].astype(BF), dims)
        if nk == 1:
            o_ref[...] = part.astype(o_ref.dtype)
            return
        (acc_ref,) = acc
        kk = pl.program_id(2)

        @pl.when(kk == 0)
        def _():
            acc_ref[...] = part

        @pl.when(kk > 0)
        def _():
            acc_ref[...] += part

        @pl.when(kk == nk - 1)
        def _():
            o_ref[...] = acc_ref[...].astype(o_ref.dtype)

    a_spec = pl.BlockSpec((tk, tm), lambda i, j, kk: (kk, i)) if mode == "tn" else pl.BlockSpec((tm, tk), lambda i, j, kk: (i, kk))
    b_spec = pl.BlockSpec((tn, tk), lambda i, j, kk: (j, kk)) if mode == "nt" else pl.BlockSpec((tk, tn), lambda i, j, kk: (kk, j))
    return pl.pallas_call(
        body, grid=(m // tm, n // tn, nk), in_specs=[a_spec, b_spec],
        out_specs=pl.BlockSpec((tm, tn), lambda i, j, kk: (i, j)),
        out_shape=jax.ShapeDtypeStruct((m, n), out_dtype),
        scratch_shapes=[pltpu.VMEM((tm, tn), F32)] if nk > 1 else [], name=name,
        compiler_params=pltpu.CompilerParams(dimension_semantics=("parallel", "parallel", "arbitrary")),
    )(a, b)


def _lane_tile(n, cap):
    return max([n // s for s in range(1, n // LANE + 1) if n % s == 0 and (n // s) % LANE == 0 and n // s <= cap] or [n])


def _ffn_in(h, w_gate, w_up):
    m, k = h.shape
    f = w_gate.shape[0]
    tm, tn = _pick(m, (1024, 512, 256, 128)), _lane_tile(f, 1408)

    def body(h_ref, wg_ref, wu_ref, act_ref, dg_ref, du_ref):
        hh = h_ref[...]
        g = _dot(hh, wg_ref[...], NT)
        u = _dot(hh, wu_ref[...], NT)
        sg = _sigmoid(g)
        silu_g = g * sg
        act_ref[...] = (silu_g * u).astype(act_ref.dtype)
        dg_ref[...] = (u * (sg + silu_g * (1.0 - sg))).astype(dg_ref.dtype)
        du_ref[...] = silu_g.astype(du_ref.dtype)

    w_spec = pl.BlockSpec((tn, k), lambda i, j: (j, 0))
    o_spec = pl.BlockSpec((tm, tn), lambda i, j: (i, j))
    return pl.pallas_call(
        body, grid=(m // tm, f // tn), in_specs=[pl.BlockSpec((tm, k), lambda i, j: (i, 0)), w_spec, w_spec],
        out_specs=[o_spec] * 3, out_shape=[jax.ShapeDtypeStruct((m, f), BF)] * 3, name="ffn_in",
        compiler_params=pltpu.CompilerParams(dimension_semantics=("parallel", "parallel")),
    )(h, w_gate, w_up)


def _ffn_act_bwd(dff, w_down, act_dg, act_du):
    m, k = dff.shape
    f = w_down.shape[0]
    tm, tn = _pick(m, (1024, 512, 256, 128)), _lane_tile(f, 1408)

    def body(d_ref, w_ref, fg_ref, fu_ref, dg_ref, du_ref):
        da = _dot(d_ref[...], w_ref[...], NT)
        dg_ref[...] = (da * fg_ref[...].astype(F32)).astype(dg_ref.dtype)
        du_ref[...] = (da * fu_ref[...].astype(F32)).astype(du_ref.dtype)

    o_spec = pl.BlockSpec((tm, tn), lambda i, j: (i, j))
    return pl.pallas_call(
        body, grid=(m // tm, f // tn),
        in_specs=[pl.BlockSpec((tm, k), lambda i, j: (i, 0)), pl.BlockSpec((tn, k), lambda i, j: (j, 0)), o_spec, o_spec],
        out_specs=[o_spec] * 2, out_shape=[jax.ShapeDtypeStruct((m, f), BF)] * 2, name="d_ffn_act",
        compiler_params=pltpu.CompilerParams(dimension_semantics=("parallel", "parallel")),
    )(dff, w_down, act_dg, act_du)


def _matmul2_nn(a1, b1, a2, b2, name, out_dtype=F32):
    m, k = a1.shape
    n = b1.shape[1]
    tm, tn = _pick(m, (1024, 512, 256, 128)), _pick(n, (512, 256, 128))

    def body(a1_ref, b1_ref, a2_ref, b2_ref, o_ref):
        o_ref[...] = (_dot(a1_ref[...], b1_ref[...]) + _dot(a2_ref[...], b2_ref[...])).astype(o_ref.dtype)

    a_spec = pl.BlockSpec((tm, k), lambda i, j: (i, 0))
    b_spec = pl.BlockSpec((k, tn), lambda i, j: (0, j))
    return pl.pallas_call(
        body, grid=(m // tm, n // tn), in_specs=[a_spec, b_spec, a_spec, b_spec],
        out_specs=pl.BlockSpec((tm, tn), lambda i, j: (i, j)), out_shape=jax.ShapeDtypeStruct((m, n), out_dtype), name=name,
        compiler_params=pltpu.CompilerParams(dimension_semantics=("parallel", "parallel")),
    )(a1, b1, a2, b2)


MATMUL_VMEM_BUDGET = 28 * 1024 * 1024


def _matmul_tiles(m, n, k, a_bytes, b_bytes, o_bytes):
    def divisors(x, cap):
        return sorted({x // s for s in range(1, 65) if x % s == 0 and (x // s) % LANE == 0 and x // s <= cap}, reverse=True) or [x]

    for tk in divisors(k, k):
        best = None
        for tm in divisors(m, 1024):
            for tn in divisors(n, 2048):
                need = 2 * (tm * tk * a_bytes + tk * tn * b_bytes + tm * tn * o_bytes) + (tm * tn * 4 if tk < k else 0)
                if need <= MATMUL_VMEM_BUDGET and tm * tn >= 512 * 512 and (best is None or tm * tn > best[0] * best[1]):
                    best = (tm, tn)
        if best:
            return best[0], best[1], tk
    return _pick(m, (512, 256, 128)), _pick(n, (512, 256, 128)), _pick(k, (512, 256, 128))


def _exchange(xs, name, scatter):
    n = len(xs)
    npeer = N_DEV - 1

    def body(*refs):
        x_refs, o_refs = refs[:n], refs[n:2 * n]
        send_sems, recv_sems, local_sems = refs[2 * n:]
        mx, my, mc = lax.axis_index("x"), lax.axis_index("y"), lax.axis_index("c")
        me = 4 * mx + 2 * my + mc
        src_me = [x.at[me] if scatter else x for x in x_refs]
        mine = [pltpu.make_async_copy(src_me[a], o_refs[a].at[me], local_sems.at[a]) for a in range(n)]
        for cp in mine:
            cp.start()
        copies = []
        for k in range(1, N_DEV):
            px, py, pc = mx ^ (k >> 2), my ^ ((k >> 1) & 1), mc ^ (k & 1)
            peer = 4 * px + 2 * py + pc
            for a in range(n):
                cp = pltpu.make_async_remote_copy(
                    src_ref=x_refs[a].at[peer] if scatter else x_refs[a], dst_ref=o_refs[a].at[me],
                    send_sem=send_sems.at[a * npeer + k - 1], recv_sem=recv_sems.at[a * npeer + k - 1],
                    device_id=(px, py, pc), device_id_type=pl.DeviceIdType.MESH)
                cp.start()
                copies.append((cp, a, k, peer))
        for cp, a, k, peer in copies:
            pltpu.make_async_remote_copy(
                src_ref=src_me[a], dst_ref=o_refs[a].at[peer], send_sem=send_sems.at[a * npeer + k - 1],
                recv_sem=recv_sems.at[a * npeer + k - 1], device_id=(mx, my, mc),
                device_id_type=pl.DeviceIdType.MESH).wait_recv()
        for cp, _, _, _ in copies:
            cp.wait_send()
        for cp in mine:
            cp.wait()

    return pl.pallas_call(
        body, out_shape=[jax.ShapeDtypeStruct((N_DEV,) + x.shape[-2:], x.dtype) for x in xs],
        in_specs=[pl.BlockSpec(memory_space=pl.ANY)] * n, out_specs=[pl.BlockSpec(memory_space=pl.ANY)] * n,
        scratch_shapes=[pltpu.SemaphoreType.DMA((n * npeer,)), pltpu.SemaphoreType.DMA((n * npeer,)),
                        pltpu.SemaphoreType.DMA((n,))],
        name=name,
    )(*xs)


def _gather_by_chip(xs, name):
    n = len(xs)
    per = N_DEV - 1

    def body(*refs):
        x_refs, o_refs = refs[:n], refs[n:2 * n]
        send_sems, recv_sems, local_sems = refs[2 * n:]
        mx, my, mc = lax.axis_index("x"), lax.axis_index("y"), lax.axis_index("c")
        me, sibling = (mx, my, mc), (mx, my, 1 - mc)
        chips = [(1 - mx, my), (mx, 1 - my), (1 - mx, 1 - my)]
        slot = lambda d: 4 * d[0] + 2 * d[1] + d[2]

        def copy(a, k, block, to, src=None):
            dst = o_refs[a].at[slot(block)]
            return pltpu.make_async_remote_copy(
                src_ref=dst if src is None else src, dst_ref=dst, send_sem=send_sems.at[a * per + k],
                recv_sem=recv_sems.at[a * per + k], device_id=to, device_id_type=pl.DeviceIdType.MESH)

        mine = [pltpu.make_async_copy(x_refs[a], o_refs[a].at[slot(me)], local_sems.at[a]) for a in range(n)]
        for cp in mine:
            cp.start()
        first = []
        for a in range(n):
            first.append(copy(a, 0, me, sibling, src=x_refs[a]))
            first += [copy(a, 1 + j, me, (*chip, mc), src=x_refs[a]) for j, chip in enumerate(chips)]
        for cp in first:
            cp.start()
        passed = []
        for j, chip in enumerate(chips):
            for a in range(n):
                copy(a, 1 + j, (*chip, mc), me).wait_recv()
                cp = copy(a, 4 + j, (*chip, mc), sibling)
                cp.start()
                passed.append(cp)
        for a in range(n):
            copy(a, 0, sibling, me).wait_recv()
            for j, chip in enumerate(chips):
                copy(a, 4 + j, (*chip, 1 - mc), me).wait_recv()
        for cp in first + passed:
            cp.wait_send()
        for cp in mine:
            cp.wait()

    return pl.pallas_call(
        body, out_shape=[jax.ShapeDtypeStruct((N_DEV,) + x.shape, x.dtype) for x in xs],
        in_specs=[pl.BlockSpec(memory_space=pl.ANY)] * n, out_specs=[pl.BlockSpec(memory_space=pl.ANY)] * n,
        scratch_shapes=[pltpu.SemaphoreType.DMA((n * per,)), pltpu.SemaphoreType.DMA((n * per,)),
                        pltpu.SemaphoreType.DMA((n,))],
        name=name,
    )(*xs)


def _peer_of(k):
    mx, my, mc = lax.axis_index("x"), lax.axis_index("y"), lax.axis_index("c")
    px, py, pc = mx ^ (k >> 2), my ^ ((k >> 1) & 1), mc ^ (k & 1)
    return (px, py, pc), 4 * px + 2 * py + pc


def _exchange_start(xs, name, scatter):
    n = len(xs)
    npeer = N_DEV - 1

    def body(*refs):
        x_refs, land_refs = refs[:n], refs[n:2 * n]
        send_sems, recv_sems, token = refs[2 * n], refs[2 * n + 1], refs[-1]
        me = 4 * lax.axis_index("x") + 2 * lax.axis_index("y") + lax.axis_index("c")
        for k in range(1, N_DEV):
            dev, peer = _peer_of(k)
            for a in range(n):
                pltpu.make_async_remote_copy(
                    src_ref=x_refs[a].at[peer] if scatter else x_refs[a], dst_ref=land_refs[a].at[me],
                    send_sem=send_sems.at[a * npeer + k - 1], recv_sem=recv_sems.at[a * npeer + k - 1],
                    device_id=dev, device_id_type=pl.DeviceIdType.MESH).start()
        token[...] = jnp.zeros_like(token)

    hbm = pl.BlockSpec(memory_space=pltpu.HBM)
    sem = pl.BlockSpec(memory_space=pltpu.SEMAPHORE)
    lands = [pltpu.with_memory_space_constraint(lax.empty((N_DEV,) + x.shape[-2:], x.dtype), pltpu.HBM) for x in xs]
    srcs = [pltpu.with_memory_space_constraint(x, pltpu.HBM) for x in xs]
    outs = pl.pallas_call(
        body, name=name,
        out_shape=(pltpu.SemaphoreType.DMA((n * npeer,)), pltpu.SemaphoreType.DMA((n * npeer,)),
                   *[pltpu.HBM(x.shape, x.dtype) for x in srcs], *[pltpu.HBM(z.shape, z.dtype) for z in lands],
                   jax.ShapeDtypeStruct((8, LANE), F32)),
        in_specs=[hbm] * (2 * n), out_specs=(sem, sem, *[hbm] * (2 * n), pl.BlockSpec(memory_space=pltpu.VMEM)),
        input_output_aliases={i: 2 + i for i in range(2 * n)},
        compiler_params=pltpu.CompilerParams(has_side_effects=pltpu.SideEffectType.DATAFLOW_SIDE_EFFECTING),
    )(*srcs, *lands)
    return (outs[0], outs[1], list(outs[2:2 + n]), list(outs[2 + n:2 + 2 * n])), outs[-1][0:1, 0:1]


def _exchange_wait(started, after, name, scatter):
    send_sems, recv_sems, srcs, lands = started
    n = len(srcs)
    npeer = N_DEV - 1

    def body(*refs):
        x_refs, land_refs = refs[:n], refs[n:2 * n]
        send_sems, recv_sems = refs[2 * n], refs[2 * n + 1]
        mx, my, mc = lax.axis_index("x"), lax.axis_index("y"), lax.axis_index("c")
        me = 4 * mx + 2 * my + mc
        for k in range(1, N_DEV):
            _, peer = _peer_of(k)
            for a in range(n):
                src = x_refs[a].at[me] if scatter else x_refs[a]
                cp = pltpu.make_async_remote_copy(
                    src_ref=src, dst_ref=land_refs[a].at[peer], send_sem=send_sems.at[a * npeer + k - 1],
                    recv_sem=recv_sems.at[a * npeer + k - 1], device_id=(mx, my, mc), device_id_type=pl.DeviceIdType.MESH)
                cp.wait_send()
                cp.wait_recv()

    hbm = pl.BlockSpec(memory_space=pltpu.HBM)
    sem = pl.BlockSpec(memory_space=pltpu.SEMAPHORE)
    outs = pl.pallas_call(
        body, name=name,
        out_shape=(*[pltpu.HBM(x.shape, x.dtype) for x in srcs], *[pltpu.HBM(z.shape, z.dtype) for z in lands]),
        in_specs=[hbm] * (2 * n) + [sem, sem, pl.BlockSpec(memory_space=pl.ANY)], out_specs=tuple([hbm] * (2 * n)),
        input_output_aliases={i: i for i in range(2 * n)},
        compiler_params=pltpu.CompilerParams(has_side_effects=pltpu.SideEffectType.DATAFLOW_SIDE_EFFECTING),
    )(*srcs, *lands, send_sems, recv_sems, after)
    me = 4 * lax.axis_index("x") + 2 * lax.axis_index("y") + lax.axis_index("c")
    full = []
    for x, land in zip(outs[:n], outs[n:]):
        own = lax.dynamic_slice(x, (me, 0, 0), (1,) + x.shape[1:]) if scatter else x[None]
        full.append(lax.dynamic_update_slice(land, own, (me, 0, 0)))
    return full


def _sum_slots(x, name):
    _, r, c = x.shape
    tr = _pick(r, (512, 256, 128, 64, 32, 16))

    def body(x_ref, o_ref):
        acc = x_ref[0].astype(F32)
        for s in range(1, N_DEV):
            acc = acc + x_ref[s].astype(F32)
        o_ref[...] = acc

    return pl.pallas_call(
        body, grid=(r // tr,), in_specs=[pl.BlockSpec((N_DEV, tr, c), lambda i: (0, i, 0))],
        out_specs=pl.BlockSpec((tr, c), lambda i: (i, 0)), out_shape=jax.ShapeDtypeStruct((r, c), F32), name=name,
        compiler_params=pltpu.CompilerParams(dimension_semantics=("arbitrary",)),
    )(x)


def _mod_fwd(c_all, w_ada, b_ada_mine):
    def body(c_ref, w_ref, b_ref, o_ref):
        o_ref[...] = _doth(_silu(c_ref[...]), w_ref[...]) + b_ref[...]

    return pl.pallas_call(body, out_shape=jax.ShapeDtypeStruct((c_all.shape[0], w_ada.shape[1]), F32), name="mod_fwd")(c_all, w_ada, b_ada_mine)


def _mod_bwd(c_all_t, dmod_mine):
    def body(ct_ref, d_ref, o_ref):
        s = _silu(ct_ref[...])
        acc = s[:, 0:1] * d_ref[pl.ds(0, 1), :]
        for b in range(1, N_DEV):
            acc = acc + s[:, b:b + 1] * d_ref[pl.ds(b, 1), :]
        o_ref[...] = acc

    return pl.pallas_call(body, out_shape=jax.ShapeDtypeStruct((c_all_t.shape[0], dmod_mine.shape[1]), F32), name="mod_bwd")(c_all_t, dmod_mine)


def _conv_fwd(proj, conv_w8, tm):
    t = proj.shape[0]
    ch = DN_CONV_CH

    def body(x_ref, w_ref, o_ref, buf):
        @pl.when(pl.program_id(0) == 0)
        def _():
            buf[pl.ds(0, CONV_HALO), :] = jnp.zeros((CONV_HALO, ch), F32)

        buf[pl.ds(CONV_HALO, tm), :] = x_ref[...].astype(F32)
        for c0 in range(0, ch, CONV_COLS):
            cols = pl.ds(c0, CONV_COLS)
            w = [w_ref[pl.ds(j, 1), cols] for j in range(CONV_K)]
            for r0 in range(0, tm, CONV_ROWS):
                acc = buf[pl.ds(r0 + CONV_HALO - (CONV_K - 1), CONV_ROWS), cols] * w[0]
                for j in range(1, CONV_K):
                    acc = acc + buf[pl.ds(r0 + CONV_HALO - (CONV_K - 1) + j, CONV_ROWS), cols] * w[j]
                o_ref[pl.ds(r0, CONV_ROWS), cols] = _silu(acc)
        buf[pl.ds(0, CONV_HALO), :] = buf[pl.ds(tm, CONV_HALO), :]

    return pl.pallas_call(
        body, grid=(t // tm,), in_specs=[pl.BlockSpec((tm, ch), lambda i: (i, 0)), _full(conv_w8.shape)],
        out_specs=pl.BlockSpec((tm, ch), lambda i: (i, 0)), out_shape=jax.ShapeDtypeStruct((t, ch), F32),
        scratch_shapes=[pltpu.VMEM((tm + CONV_HALO, ch), F32)], name="conv_fwd",
        compiler_params=pltpu.CompilerParams(dimension_semantics=("arbitrary",)),
    )(proj, conv_w8)


def _conv_bwd(proj, conv_w8, dact, others, tm):
    t = proj.shape[0]
    ch = DN_CONV_CH
    nt = t // tm
    halo_blk = 2 * CONV_HALO
    hb = tm // halo_blk
    n_others = len(others)

    def body(x_ref, xp_ref, w_ref, dy_ref, *refs):
        piece_refs, (dx_ref, dw_ref, xbuf, dbuf) = refs[:n_others], refs[n_others:]
        step = pl.program_id(0)
        for (off, arr), p_ref in zip(others, piece_refs):
            dx_ref[:, pl.ds(off, arr.shape[1])] = p_ref[...].astype(dx_ref.dtype)

        @pl.when(step == 0)
        def _():
            dbuf[pl.ds(tm, CONV_HALO), :] = jnp.zeros((CONV_HALO, ch), F32)
            dw_ref[...] = jnp.zeros_like(dw_ref)

        first = step == nt - 1
        xbuf[pl.ds(0, CONV_HALO), :] = jnp.where(first, 0.0, xp_ref[...].astype(F32)[halo_blk - CONV_HALO:])
        xbuf[pl.ds(CONV_HALO, tm), :] = x_ref[...].astype(F32)
        for c0 in range(0, ch, CONV_COLS):
            cols = pl.ds(c0, CONV_COLS)
            w = [w_ref[pl.ds(j, 1), cols] for j in range(CONV_K)]
            dw = [jnp.zeros((1, CONV_COLS), F32) for _ in range(CONV_K)]
            for r0 in range(0, tm, CONV_ROWS):
                xs = [xbuf[pl.ds(r0 + CONV_HALO - (CONV_K - 1) + j, CONV_ROWS), cols] for j in range(CONV_K)]
                pre = xs[0] * w[0]
                for j in range(1, CONV_K):
                    pre = pre + xs[j] * w[j]
                sg = _sigmoid(pre)
                dpre = dy_ref[pl.ds(r0, CONV_ROWS), cols] * (sg * (1.0 + pre * (1.0 - sg)))
                dbuf[pl.ds(r0, CONV_ROWS), cols] = dpre
                dw = [dw[j] + jnp.sum(dpre * xs[j], axis=0, keepdims=True) for j in range(CONV_K)]
            for j in range(CONV_K):
                dw_ref[pl.ds(j, 1), cols] += dw[j]
            for r0 in range(0, tm, CONV_ROWS):
                dx = dbuf[pl.ds(r0 + CONV_K - 1, CONV_ROWS), cols] * w[0]
                for j in range(1, CONV_K):
                    dx = dx + dbuf[pl.ds(r0 + CONV_K - 1 - j, CONV_ROWS), cols] * w[j]
                dx_ref[pl.ds(r0, CONV_ROWS), cols] = dx.astype(dx_ref.dtype)
        dbuf[pl.ds(tm, CONV_HALO), :] = dbuf[pl.ds(0, CONV_HALO), :]

    rev = lambda i: (nt - 1 - i, 0)
    prev = lambda i: (jnp.maximum((nt - 1 - i) * hb - 1, 0), 0)
    return pl.pallas_call(
        body, grid=(nt,),
        in_specs=[pl.BlockSpec((tm, ch), rev), pl.BlockSpec((halo_blk, ch), prev), _full(conv_w8.shape),
                  pl.BlockSpec((tm, ch), rev)] + [pl.BlockSpec((tm, arr.shape[1]), rev) for _, arr in others],
        out_specs=[pl.BlockSpec((tm, N_INP), rev), _full(conv_w8.shape)],
        out_shape=[jax.ShapeDtypeStruct((t, N_INP), BF), jax.ShapeDtypeStruct(conv_w8.shape, F32)],
        scratch_shapes=[pltpu.VMEM((tm + CONV_HALO, ch), F32), pltpu.VMEM((tm + CONV_HALO, ch), F32)], name="conv_bwd",
        compiler_params=pltpu.CompilerParams(dimension_semantics=("arbitrary",)),
    )(proj, proj, conv_w8, dact, *[arr for _, arr in others])


BNN = (((2,), (1,)), ((0,), (0,)))
BNT = (((2,), (2,)), ((0,), (0,)))
BTN = (((1,), (1,)), ((0,), (0,)))


def _bdot(a, b, dims, precision=None):
    return lax.dot_general(a, b, dims, precision=precision, preferred_element_type=F32)


@jax.custom_vjp
def _bmmb_nt(a, b):
    return _bdot(a.astype(BF), b.astype(BF), BNT)


def _bmmb_nt_fwd(a, b):
    return _bmmb_nt(a, b), (a, b)


def _bmmb_nt_bwd(res, g):
    a, b = res
    gb = g.astype(BF)
    return _bdot(gb, b.astype(BF), BNN), _bdot(gb, a.astype(BF), BTN)


_bmmb_nt.defvjp(_bmmb_nt_fwd, _bmmb_nt_bwd)


@jax.custom_vjp
def _bmmb(a, b):
    return _bdot(a.astype(BF), b.astype(BF), BNN)


def _bmmb_fwd(a, b):
    return _bmmb(a, b), (a, b)


def _bmmb_bwd(res, g):
    a, b = res
    gb = g.astype(BF)
    return _bdot(gb, b.astype(BF), BNT), _bdot(a.astype(BF), gb, BTN)


_bmmb.defvjp(_bmmb_fwd, _bmmb_bwd)


@jax.custom_vjp
def _bmmb_tn(a, b):
    return _bdot(a.astype(BF), b.astype(BF), BTN)


def _bmmb_tn_fwd(a, b):
    return _bmmb_tn(a, b), (a, b)


def _bmmb_tn_bwd(res, g):
    a, b = res
    gb = g.astype(BF)
    return _bdot(b.astype(BF), gb, BNT), _bdot(a.astype(BF), gb, BNN)


_bmmb_tn.defvjp(_bmmb_tn_fwd, _bmmb_tn_bwd)


def _triangle_sums(g, lower):
    c = g.shape[1]
    ri = lax.broadcasted_iota(jnp.int32, (g.shape[0], c, c), 1)
    ci = lax.broadcasted_iota(jnp.int32, (g.shape[0], c, c), 2)
    tri = (ri >= ci if lower else ri <= ci).astype(BF)
    hi = g.astype(BF)
    mid = (g - hi.astype(F32)).astype(BF)
    lo = (g - hi.astype(F32) - mid.astype(F32)).astype(BF)
    return _bdot(tri, hi, BNN) + _bdot(tri, mid, BNN) + _bdot(tri, lo, BNN)


@jax.custom_vjp
def _chunk_cumsum(g):
    return _triangle_sums(g, True)


_chunk_cumsum.defvjp(lambda g: (_triangle_sums(g, True), None), lambda _, ct: (_triangle_sums(ct, False),))


def _unit_lower_solve_fwd(a, r):
    c = a.shape[-1]
    ri = lax.broadcasted_iota(jnp.int32, a.shape, 1)
    ci = lax.broadcasted_iota(jnp.int32, a.shape, 2)
    xm = -a
    inv = (ri == ci).astype(F32) + xm
    for _ in range(int(math.log2(c)) - 1):
        xm = _bdot(xm, xm, BNN, HI)
        inv = inv + _bdot(inv, xm, BNN, HI)
    x = _bdot(inv, r, BNN, HI)
    return x, (inv, x)


def _unit_lower_solve_bwd(res, g):
    inv, x = res
    dr = _bdot(inv, g, BTN, HI)
    return -_bdot(dr, x, BNT, HI), dr


@jax.custom_vjp
def _unit_lower_solve_given(a, r, inv):
    return _bdot(inv, r, BNN, HI)


def _unit_lower_solve_given_fwd(a, r, inv):
    x = _bdot(inv, r, BNN, HI)
    return x, (inv, x)


def _unit_lower_solve_given_bwd(res, g):
    da, dr = _unit_lower_solve_bwd(res, g)
    return da, dr, jnp.zeros_like(res[0])


_unit_lower_solve_given.defvjp(_unit_lower_solve_given_fwd, _unit_lower_solve_given_bwd)


def _gdn_intra(qkv, ba, al8, dt8, inv4=None):
    tm = qkv.shape[0]
    nb = tm // CHUNK
    bsz = DN_HEADS * nb

    def heads(x0):
        return jnp.concatenate([qkv[:, x0 + h * LANE:x0 + (h + 1) * LANE].reshape(nb, CHUNK, LANE) for h in range(DN_HEADS)], axis=0)

    def spread(c0):
        return jnp.concatenate([jnp.broadcast_to(ba[:, c0 + h:c0 + h + 1], (tm, LANE)).reshape(nb, CHUNK, LANE)
                                for h in range(DN_HEADS)], axis=0)

    def per_head(v8):
        return jnp.concatenate([jnp.broadcast_to(v8[0:1, h:h + 1].reshape(1, 1, 1), (nb, 1, LANE)) for h in range(DN_HEADS)], axis=0)

    ri = lax.broadcasted_iota(jnp.int32, (bsz, CHUNK, CHUNK), 1)
    ci = lax.broadcasted_iota(jnp.int32, (bsz, CHUNK, CHUNK), 2)
    incl = ri >= ci
    strict = ri > ci

    q = _l2norm(heads(0)) * (DN_DK ** -0.5)
    k = _l2norm(heads(DN_QK))
    va = heads(2 * DN_QK)
    beta = _sigmoid(spread(0))
    g = -jnp.exp(per_head(al8)) * _softplus(spread(DN_HEADS) + per_head(dt8))
    gc = _chunk_cumsum(g)
    g_last = jnp.sum(g, axis=1, keepdims=True)
    gcol = gc[:, :, :CHUNK]
    diff = gcol - jnp.swapaxes(gcol, 1, 2)
    decay = jnp.where(incl, jnp.exp(jnp.where(incl, diff, 0.0)), 0.0)
    kb = k * beta
    a_mat = jnp.where(strict, _bmmb_nt(kb, k) * decay, 0.0)
    egc = jnp.exp(gc)
    rhs = jnp.concatenate([kb * egc, va * beta], axis=2)
    if inv4 is None:
        wu, (inv, _) = _unit_lower_solve_fwd(a_mat, rhs)
    else:
        wu = _unit_lower_solve_given(a_mat, rhs, inv4.reshape(bsz, CHUNK, CHUNK))
    attn = jnp.where(incl, _bmmb_nt(q, k) * decay, 0.0)

    def unheads(x):
        return jnp.concatenate([x[h * nb:(h + 1) * nb].reshape(tm, LANE) for h in range(DN_HEADS)], axis=1)

    w_c, u_c = wu[:, :, :DN_DK], wu[:, :, DN_DK:]
    kd = k * jnp.exp(g_last - gc)
    out = (unheads(q * egc - _bmmb(attn, w_c)), unheads(_bmmb(attn, u_c)),
           _bmmb_tn(kd, w_c).reshape(DN_HEADS, nb, DN_DK, DN_DK), _bmmb_tn(kd, u_c).reshape(DN_HEADS, nb, DN_DK, DN_DV),
           jnp.broadcast_to(g_last, (bsz, GL_ROWS, LANE)).reshape(DN_HEADS, nb, GL_ROWS, LANE))
    return out if inv4 is not None else out + (inv.reshape(DN_HEADS, nb, CHUNK, CHUNK),)


def _gdn_scan_step(qp, op, c_mat, n_mat, gl, s):
    return _mmb(qp, s) + op, s * jnp.exp(gl) - _mmb(c_mat, s) + n_mat


def _gdn_intra_specs(t, tm, dts, order=lambda i: i):
    nb = tm // CHUNK
    row = pl.BlockSpec((tm, DN_VW), lambda i: (order(i), 0))
    mat = pl.BlockSpec((DN_HEADS, nb, DN_DK, DN_DV), lambda i: (0, order(i), 0, 0))
    row_shape = lambda d: jax.ShapeDtypeStruct((t, DN_VW), d)
    mat_shape = lambda d: jax.ShapeDtypeStruct((DN_HEADS, t // CHUNK, DN_DK, DN_DV), d)
    gl = pl.BlockSpec((DN_HEADS, nb, GL_ROWS, LANE), lambda i: (0, order(i), 0, 0))
    gl_shape = jax.ShapeDtypeStruct((DN_HEADS, t // CHUNK, GL_ROWS, LANE), dts[4])
    return [row, row, mat, mat, gl], [row_shape(dts[0]), row_shape(dts[1]), mat_shape(dts[2]), mat_shape(dts[3]), gl_shape]


def _gdn_intra_fwd(qkv, ba, al8, dt8, tm):
    t = qkv.shape[0]

    def body(qkv_ref, ba_ref, al_ref, dt_ref, *outs):
        for o, val in zip(outs, _gdn_intra(qkv_ref[...], ba_ref[...], al_ref[...], dt_ref[...])):
            o[...] = val.astype(o.dtype)

    specs, shapes = _gdn_intra_specs(t, tm, (BF, F32, BF, BF, F32))
    specs.append(_gdn_inverse_spec(tm))
    shapes.append(jax.ShapeDtypeStruct((DN_HEADS, t // CHUNK, CHUNK, CHUNK), F32))
    res = pl.pallas_call(
        body, grid=(t // tm,),
        in_specs=[pl.BlockSpec((tm, DN_CONV_CH), lambda i: (i, 0)), pl.BlockSpec((tm, LANE), lambda i: (i, 0)),
                  _full(al8.shape), _full(dt8.shape)],
        out_specs=specs, out_shape=shapes, name="gdn_intra_fwd",
        compiler_params=pltpu.CompilerParams(dimension_semantics=("parallel",)),
    )(qkv, ba, al8, dt8)
    return res[:5], res[5]


def _gdn_inverse_spec(tm):
    return pl.BlockSpec((DN_HEADS, tm // CHUNK, CHUNK, CHUNK), lambda i: (0, i, 0, 0))


def _gdn_intra_bwd(qkv, ba, al8, dt8, inverses, cts, tm):
    t = qkv.shape[0]

    def body(qkv_ref, ba_ref, al_ref, dt_ref, inv_ref, *refs):
        ct_refs, (dqkv_ref, dba_ref, dal_ref, ddt_ref) = refs[:5], refs[5:]

        @pl.when(pl.program_id(0) == 0)
        def _():
            dal_ref[...] = jnp.zeros_like(dal_ref)
            ddt_ref[...] = jnp.zeros_like(ddt_ref)

        _, vjp = jax.vjp(functools.partial(_gdn_intra, inv4=inv_ref[...]), qkv_ref[...], ba_ref[...], al_ref[...], dt_ref[...])
        dqkv, dba, dal, ddt = vjp(tuple(r[...].astype(F32) for r in ct_refs))
        dqkv_ref[...] = dqkv.astype(dqkv_ref.dtype)
        dba_ref[...] = dba.astype(dba_ref.dtype)
        dal_ref[...] += dal
        ddt_ref[...] += ddt

    specs, _ = _gdn_intra_specs(t, tm, (F32,) * 5)
    return pl.pallas_call(
        body, grid=(t // tm,),
        in_specs=[pl.BlockSpec((tm, DN_CONV_CH), lambda i: (i, 0)), pl.BlockSpec((tm, LANE), lambda i: (i, 0)),
                  _full(al8.shape), _full(dt8.shape), _gdn_inverse_spec(tm)] + specs,
        out_specs=[pl.BlockSpec((tm, DN_CONV_CH), lambda i: (i, 0)), pl.BlockSpec((tm, LANE), lambda i: (i, 0)),
                   _full(al8.shape), _full(dt8.shape)],
        out_shape=[jax.ShapeDtypeStruct((t, DN_CONV_CH), BF), jax.ShapeDtypeStruct((t, LANE), BF),
                   jax.ShapeDtypeStruct(al8.shape, F32), jax.ShapeDtypeStruct(dt8.shape, F32)],
        name="gdn_intra_bwd", compiler_params=pltpu.CompilerParams(dimension_semantics=("arbitrary",)),
    )(qkv, ba, al8, dt8, inverses, *cts)


def _gdn_scan_fwd(intra, tm):
    t = intra[0].shape[0]
    nb = tm // CHUNK
    nc = t // CHUNK

    def body(qp_ref, op_ref, c_ref, n_ref, gl_ref, o_ref, ss_ref, s_scr):
        @pl.when(pl.program_id(0) == 0)
        def _():
            s_scr[...] = jnp.zeros_like(s_scr)

        state = [s_scr[h] for h in range(DN_HEADS)]
        for cc in range(nb):
            rows = pl.ds(cc * CHUNK, CHUNK)
            for h in range(DN_HEADS):
                cols = pl.ds(h * DN_DV, DN_DV)
                ss_ref[cc, h] = state[h].astype(ss_ref.dtype)
                o_ref[rows, cols], state[h] = _gdn_scan_step(
                    qp_ref[rows, cols], op_ref[rows, cols], c_ref[h, cc], n_ref[h, cc], gl_ref[h, cc, pl.ds(0, 1), :], state[h])
        for h in range(DN_HEADS):
            s_scr[h] = state[h]

    specs, _ = _gdn_intra_specs(t, tm, (F32,) * 5)
    return pl.pallas_call(
        body, grid=(t // tm,), in_specs=specs,
        out_specs=[pl.BlockSpec((tm, DN_VW), lambda i: (i, 0)),
                   pl.BlockSpec((nb, DN_HEADS, DN_DK, DN_DV), lambda i: (i, 0, 0, 0))],
        out_shape=[jax.ShapeDtypeStruct((t, DN_VW), F32), jax.ShapeDtypeStruct((nc, DN_HEADS, DN_DK, DN_DV), BF)],
        scratch_shapes=[pltpu.VMEM((DN_HEADS, DN_DK, DN_DV), F32)], name="gdn_scan_fwd",
        compiler_params=pltpu.CompilerParams(dimension_semantics=("arbitrary",)),
    )(*intra)


def _gdn_scan_bwd(intra, states, do, tm):
    t = intra[0].shape[0]
    nb = tm // CHUNK
    ng = t // tm

    def body(qp_ref, op_ref, c_ref, n_ref, gl_ref, ss_ref, do_ref, dqp_ref, dop_ref, dc_ref, dn_ref, dgl_ref, ds_scr):
        @pl.when(pl.program_id(0) == 0)
        def _():
            ds_scr[...] = jnp.zeros_like(ds_scr)

        d_state = [ds_scr[h] for h in range(DN_HEADS)]
        for cc in reversed(range(nb)):
            rows = pl.ds(cc * CHUNK, CHUNK)
            for h in range(DN_HEADS):
                cols = pl.ds(h * DN_DV, DN_DV)
                _, vjp = jax.vjp(_gdn_scan_step, qp_ref[rows, cols].astype(F32), op_ref[rows, cols], c_ref[h, cc].astype(F32),
                                 n_ref[h, cc].astype(F32), gl_ref[h, cc, pl.ds(0, 1), :], ss_ref[cc, h].astype(F32))
                dqp_ref[rows, cols], dop_ref[rows, cols], dc, dn, dgl, d_state[h] = vjp((do_ref[rows, cols], d_state[h]))
                dc_ref[h, cc] = dc.astype(dc_ref.dtype)
                dn_ref[h, cc] = dn.astype(dn_ref.dtype)
                first_row = lax.broadcasted_iota(jnp.int32, (GL_ROWS, LANE), 0) == 0
                dgl_ref[h, cc] = jnp.where(first_row, dgl, 0.0)
        for h in range(DN_HEADS):
            ds_scr[h] = d_state[h]

    five, shapes = _gdn_intra_specs(t, tm, (F32, F32, BF, BF, F32), order=lambda i: ng - 1 - i)
    row = five[0]
    return pl.pallas_call(
        body, grid=(ng,),
        in_specs=five + [pl.BlockSpec((nb, DN_HEADS, DN_DK, DN_DV), lambda i: (ng - 1 - i, 0, 0, 0)), row],
        out_specs=five, out_shape=shapes,
        scratch_shapes=[pltpu.VMEM((DN_HEADS, DN_DK, DN_DV), F32)], name="gdn_scan_bwd",
        compiler_params=pltpu.CompilerParams(dimension_semantics=("arbitrary",)),
    )(*intra, states, do)


def _gdn_out(o, z, g):
    parts = []
    for h in range(DN_HEADS):
        sl = slice(h * DN_DV, (h + 1) * DN_DV)
        parts.append(_rmsnorm(o[:, sl], g) * _silu(z[:, sl]))
    return parts


_Q_SCALE = math.log2(math.e) / math.sqrt(QK_NOPE + QK_ROPE)


def _rope_tables(pos, inv_freq2):
    lane = lax.broadcasted_iota(jnp.int32, (1, LANE), 1)
    ang = pos * inv_freq2
    cos = jnp.where(lane < QK_ROPE, jnp.cos(ang), 0.0)
    sin = jnp.where(lane < QK_ROPE // 2, -jnp.sin(ang), jnp.where(lane < QK_ROPE, jnp.sin(ang), 0.0))
    return cos, sin


@jax.custom_vjp
def _rope_swap(u):
    lane = lax.broadcasted_iota(jnp.int32, u.shape, 1)
    half = QK_ROPE // 2
    return jnp.where(lane < half, pltpu.roll(u, LANE - half, 1), jnp.where(lane < QK_ROPE, pltpu.roll(u, half, 1), 0.0))


_rope_swap.defvjp(lambda u: (_rope_swap(u), None), lambda _, g: (_rope_swap(g),))


def _mla_prep(cq, ckv, kr, gq, gkv, w_uq, w_ukv, cos, sin):
    rope = lambda u: u * cos + _rope_swap(u) * sin
    q_lin = _mmb_nt(_rmsnorm(cq, gq), w_uq) * _Q_SCALE
    kv_lin = _mmb_nt(_rmsnorm(ckv, gkv), w_ukv)
    k_rope = rope(kr)
    qs, ks, vs = [], [], []
    for h in range(MLA_HEADS):
        qs += [q_lin[:, h * LANE:(h + 1) * LANE], rope(q_lin[:, (MLA_HEADS + h) * LANE:(MLA_HEADS + h + 1) * LANE])]
        ks += [kv_lin[:, 2 * h * LANE:(2 * h + 1) * LANE], k_rope]
        vs += [kv_lin[:, (2 * h + 1) * LANE:(2 * h + 2) * LANE]]
    return qs + ks + vs


def _mla_prep_fwd(proj, pos_col, inv_freq2, gq, gkv, w_uq, w_ukv, tm):
    t = proj.shape[0]
    nq = 2 * MLA_HEADS

    def body(cq_ref, ckv_ref, kr_ref, pos_ref, f_ref, gq_ref, gkv_ref, wq_ref, wkv_ref, q_ref, k_ref, v_ref):
        cos, sin = _rope_tables(pos_ref[...], f_ref[...])
        outs = _mla_prep(cq_ref[...].astype(F32), ckv_ref[...].astype(F32), kr_ref[...].astype(F32), gq_ref[...], gkv_ref[...], wq_ref[...], wkv_ref[...],
                         cos, sin)
        for i in range(nq):
            q_ref[:, pl.ds(i * LANE, LANE)] = outs[i].astype(q_ref.dtype)
            k_ref[:, pl.ds(i * LANE, LANE)] = outs[nq + i].astype(k_ref.dtype)
        for h in range(MLA_HEADS):
            v_ref[:, pl.ds(h * LANE, LANE)] = outs[2 * nq + h].astype(v_ref.dtype)

    row = lambda w, j: pl.BlockSpec((tm, w), functools.partial(lambda i, j: (i, j), j=j))
    return pl.pallas_call(
        body, grid=(t // tm,),
        in_specs=[row(Q_LORA, P_CQ // Q_LORA), row(KV_LORA, P_CKV // KV_LORA), row(LANE, P_KR // LANE),
                  pl.BlockSpec((tm, 1), lambda i: (i, 0)), _full(inv_freq2.shape), _full(gq.shape), _full(gkv.shape),
                  _full(w_uq.shape), _full(w_ukv.shape)],
        out_specs=[row(nq * LANE, 0), row(nq * LANE, 0), row(MLA_VW, 0)],
        out_shape=[jax.ShapeDtypeStruct((t, nq * LANE), BF), jax.ShapeDtypeStruct((t, nq * LANE), BF),
                   jax.ShapeDtypeStruct((t, MLA_VW), BF)],
        name="mla_prep_fwd", compiler_params=pltpu.CompilerParams(dimension_semantics=("arbitrary",)),
    )(proj, proj, proj, pos_col, inv_freq2, gq, gkv, w_uq, w_ukv)


def _mla_prep_bwd(proj, pos_col, inv_freq2, gq, gkv, w_uq, w_ukv, dq, dk, dv, tm):
    t = proj.shape[0]
    nq = 2 * MLA_HEADS

    def body(cq_ref, ckv_ref, kr_ref, pos_ref, f_ref, gq_ref, gkv_ref, wq_ref, wkv_ref, dq_ref, dk_ref, dv_ref,
             dcq_ref, dckv_ref, dkr_ref, dgq_ref, dgkv_ref, dwq_ref, dwkv_ref):
        @pl.when(pl.program_id(0) == 0)
        def _():
            for o in (dgq_ref, dgkv_ref, dwq_ref, dwkv_ref):
                o[...] = jnp.zeros_like(o)

        cos, sin = _rope_tables(pos_ref[...], f_ref[...])
        f = functools.partial(_mla_prep, cos=cos, sin=sin)
        _, vjp = jax.vjp(f, cq_ref[...].astype(F32), ckv_ref[...].astype(F32), kr_ref[...].astype(F32), gq_ref[...], gkv_ref[...], wq_ref[...], wkv_ref[...])
        cts = [dq_ref[:, pl.ds(i * LANE, LANE)] for i in range(nq)]
        cts += [dk_ref[:, pl.ds(i * LANE, LANE)] for i in range(nq)]
        cts += [dv_ref[:, pl.ds(h * LANE, LANE)] for h in range(MLA_HEADS)]
        dcq, dckv, dkr, dgq, dgkv, dwq, dwkv = vjp(cts)
        dcq_ref[...] = dcq.astype(dcq_ref.dtype)
        dckv_ref[...] = dckv.astype(dckv_ref.dtype)
        dkr_ref[...] = dkr.astype(dkr_ref.dtype)
        dgq_ref[...] += dgq
        dgkv_ref[...] += dgkv
        dwq_ref[...] += dwq
        dwkv_ref[...] += dwkv

    row = lambda w, j: pl.BlockSpec((tm, w), functools.partial(lambda i, j: (i, j), j=j))
    return pl.pallas_call(
        body, grid=(t // tm,),
        in_specs=[row(Q_LORA, P_CQ // Q_LORA), row(KV_LORA, P_CKV // KV_LORA), row(LANE, P_KR // LANE),
                  pl.BlockSpec((tm, 1), lambda i: (i, 0)), _full(inv_freq2.shape), _full(gq.shape), _full(gkv.shape),
                  _full(w_uq.shape), _full(w_ukv.shape), row(nq * LANE, 0), row(nq * LANE, 0), row(MLA_VW, 0)],
        out_specs=[row(Q_LORA, 0), row(KV_LORA, 0), row(LANE, 0), _full(gq.shape), _full(gkv.shape),
                   _full(w_uq.shape), _full(w_ukv.shape)],
        out_shape=[jax.ShapeDtypeStruct((t, Q_LORA), BF), jax.ShapeDtypeStruct((t, KV_LORA), BF),
                   jax.ShapeDtypeStruct((t, LANE), BF), jax.ShapeDtypeStruct(gq.shape, F32),
                   jax.ShapeDtypeStruct(gkv.shape, F32), jax.ShapeDtypeStruct(w_uq.shape, F32),
                   jax.ShapeDtypeStruct(w_ukv.shape, F32)],
        name="mla_prep_bwd", compiler_params=pltpu.CompilerParams(dimension_semantics=("arbitrary",)),
    )(proj, proj, proj, pos_col, inv_freq2, gq, gkv, w_uq, w_ukv, dq, dk, dv)


_NEG = -1e30
_LN2 = math.log(2.0)
ATT_CHAINS = 2


def _causal(tq, tk, q0, k0):
    row = q0 + lax.broadcasted_iota(jnp.int32, (tq, tk), 0)
    col = k0 + lax.broadcasted_iota(jnp.int32, (tq, tk), 1)
    return col <= row


def _attn_fwd(q, k, v, tq, tk):
    t = q.shape[0]

    assert tk % tq == 0 or tq % tk == 0
    n_diag = max(1, tq // tk)

    th = tq // ATT_CHAINS

    def body(q_ref, k_ref, v_ref, o_ref, lse_ref):
        i = pl.program_id(1)
        n_full = (i * tq) // tk

        def step(k0, carry, masked):
            out = []
            for c, (m, l, acc) in enumerate(carry):
                kw = min(tk, (c + 1) * th) if masked and tk == tq else tk
                kt = k_ref[pl.ds(k0, kw), :]
                vt = v_ref[pl.ds(k0, kw), :]
                s = _dot(q_ref[pl.ds(c * th, th), :], kt, NT)
                if masked:
                    s = jnp.where(_causal(th, kw, i * tq + c * th, k0), s, _NEG)
                m_new = jnp.maximum(m, jnp.max(s, axis=-1, keepdims=True))
                p = jnp.exp2(s - m_new)
                alpha = jnp.exp2(m - m_new)
                out.append((m_new, alpha * l + jnp.sum(p, axis=-1, keepdims=True), alpha * acc + _dot(p.astype(BF), vt)))
            return tuple(out)

        init = tuple((jnp.full((th, 1), _NEG, F32), jnp.zeros((th, 1), F32), jnp.zeros((th, V_HEAD), F32)) for _ in range(ATT_CHAINS))
        carry = lax.fori_loop(0, n_full, lambda j, c: step(pl.multiple_of(j * tk, tk), c, False), init)
        for dd in range(n_diag):
            carry = step(pl.multiple_of((n_full + dd) * tk, tk), carry, True)
        for c, (m, l, acc) in enumerate(carry):
            o_ref[pl.ds(c * th, th), :] = acc / l
            lse_ref[pl.ds(c * th, th), :] = jnp.broadcast_to(m + jnp.log2(l), (th, LANE))

    return pl.pallas_call(
        body, grid=(MLA_HEADS, t // tq),
        in_specs=[pl.BlockSpec((tq, 2 * LANE), lambda h, i: (i, h)), pl.BlockSpec((t, 2 * LANE), lambda h, i: (0, h)),
                  pl.BlockSpec((t, V_HEAD), lambda h, i: (0, h))],
        out_specs=[pl.BlockSpec((tq, V_HEAD), lambda h, i: (i, h)), pl.BlockSpec((tq, LANE), lambda h, i: (i, h))],
        out_shape=[jax.ShapeDtypeStruct((t, MLA_VW), F32), jax.ShapeDtypeStruct((t, MLA_HEADS * LANE), F32)],
        name="attn_fwd", compiler_params=pltpu.CompilerParams(dimension_semantics=("parallel", "arbitrary")),
    )(q, k, v)


def _attn_bwd(q, k, v, do, lse, delta, tq, tk):
    t = q.shape[0]
    nkt = t // tk
    assert tk % tq == 0

    def body(q_ref, k_ref, v_ref, do_ref, lse_ref, dl_ref, dq_ref, dk_ref, dv_ref):
        j = pl.program_id(1)

        @pl.when(j == 0)
        def _():
            dq_ref[...] = jnp.zeros_like(dq_ref)

        kt = k_ref[...]
        vt = v_ref[...]

        def step(q0, carry, masked, kw=tk):
            dk, dv = carry
            rows = pl.ds(q0, tq)
            qt = q_ref[rows, :]
            dot_ = do_ref[rows, :]
            ktw, vtw = kt[:kw], vt[:kw]
            p = jnp.exp2(_dot(qt, ktw, NT) - lse_ref[rows, pl.ds(0, 1)])
            if masked:
                p = jnp.where(_causal(tq, kw, q0, j * tk), p, 0.0)
            dv_w = _dot(p.astype(BF), dot_, TN)
            ds = (p * (_dot(dot_, vtw, NT) - dl_ref[rows, pl.ds(0, 1)])).astype(BF)
            dk_w = _dot(ds, qt, TN)
            dq_ref[rows, :] += _dot(ds, ktw)
            if kw == tk:
                return dk + dk_w, dv + dv_w
            return (jnp.concatenate([dk[:kw] + dk_w, dk[kw:]], axis=0), jnp.concatenate([dv[:kw] + dv_w, dv[kw:]], axis=0))

        per = tk // tq
        carry = (jnp.zeros((tk, 2 * LANE), F32), jnp.zeros((tk, V_HEAD), F32))
        for dd in range(per):
            carry = step(pl.multiple_of(j * tk + dd * tq, tq), carry, True, kw=(dd + 1) * tq)

        def group(g, c):
            for dd in range(per):
                c = step(pl.multiple_of(g * tk + dd * tq, tq), c, False)
            return c

        dk, dv = lax.fori_loop(j + 1, nkt, group, carry)
        dk_ref[...] = dk * _LN2
        dv_ref[...] = dv

        @pl.when(j == nkt - 1)
        def _():
            dq_ref[...] = dq_ref[...] * _LN2

    return pl.pallas_call(
        body, grid=(MLA_HEADS, nkt),
        in_specs=[pl.BlockSpec((t, 2 * LANE), lambda h, j: (0, h)), pl.BlockSpec((tk, 2 * LANE), lambda h, j: (j, h)),
                  pl.BlockSpec((tk, V_HEAD), lambda h, j: (j, h)), pl.BlockSpec((t, V_HEAD), lambda h, j: (0, h)),
                  pl.BlockSpec((t, LANE), lambda h, j: (0, h)), pl.BlockSpec((t, LANE), lambda h, j: (0, h))],
        out_specs=[pl.BlockSpec((t, 2 * LANE), lambda h, j: (0, h)), pl.BlockSpec((tk, 2 * LANE), lambda h, j: (j, h)),
                   pl.BlockSpec((tk, V_HEAD), lambda h, j: (j, h))],
        out_shape=[jax.ShapeDtypeStruct((t, MLA_HEADS * 2 * LANE), F32), jax.ShapeDtypeStruct((t, MLA_HEADS * 2 * LANE), F32),
                   jax.ShapeDtypeStruct((t, MLA_VW), F32)],
        name="attn_bwd", compiler_params=pltpu.CompilerParams(dimension_semantics=("parallel", "arbitrary")),
    )(q, k, v, do, lse, delta)


def _adam_update(w, g, m, v):
    mm = ADAM_B1 * m + (1.0 - ADAM_B1) * g
    vv = ADAM_B2 * v + (1.0 - ADAM_B2) * jnp.square(g)
    m_hat = mm / (1.0 - ADAM_B1 ** ADAM_STEP)
    v_hat = vv / (1.0 - ADAM_B2 ** ADAM_STEP)
    return -ADAM_LR * (m_hat / (jnp.sqrt(v_hat) + ADAM_EPS) + ADAM_WD * w), mm, vv


def _adamw(w, g, m, v, name):
    r, c = w.shape
    tr = max([r // s for s in range(1, r // 8 + 1) if r % s == 0 and (r // s) % 8 == 0 and r // s <= 512] or [r])
    slots = g.ndim == 3

    def body(w_ref, g_ref, m_ref, v_ref, g_out, d_ref, nm_ref, nv_ref):
        if slots:
            gg = g_ref[0].astype(F32)
            for s in range(1, N_DEV):
                gg = gg + g_ref[s].astype(F32)
        else:
            gg = g_ref[...]
        g_out[...] = gg
        d_ref[...], nm_ref[...], nv_ref[...] = _adam_update(w_ref[...], gg, m_ref[...], v_ref[...])

    spec = pl.BlockSpec((tr, c), lambda i: (i, 0))
    g_spec = pl.BlockSpec((N_DEV, tr, c), lambda i: (0, i, 0)) if slots else spec
    return pl.pallas_call(
        body, grid=(r // tr,), in_specs=[spec, g_spec, spec, spec], out_specs=[spec] * 4,
        out_shape=[jax.ShapeDtypeStruct((r, c), F32)] * 4, name=name,
        compiler_params=pltpu.CompilerParams(dimension_semantics=("arbitrary",)),
    )(w, g, m, v)


def _adamw_many(ws, gs, ms, vs, name):
    n = len(ws)

    def body(*refs):
        for i in range(n):
            w_ref, g_ref, m_ref, v_ref = (refs[j * n + i] for j in range(4))
            d_ref, nm_ref, nv_ref = (refs[(4 + j) * n + i] for j in range(3))
            d_ref[...], nm_ref[...], nv_ref[...] = _adam_update(w_ref[...], g_ref[...], m_ref[...], v_ref[...])

    shapes = [jax.ShapeDtypeStruct(w.shape, F32) for w in ws]
    outs = pl.pallas_call(body, out_shape=shapes * 3, name=name)(*ws, *gs, *ms, *vs)
    return outs[:n], outs[n:2 * n], outs[2 * n:]


def _cast_bf16(xs, name, after=None):
    n = len(xs)
    extra = [] if after is None else [after]

    def body(*refs):
        outs = refs[n + len(extra):]
        for i in range(n):
            outs[i][...] = refs[i][...].astype(BF)

    vmem = pl.BlockSpec(memory_space=pltpu.VMEM)
    return pl.pallas_call(
        body, out_shape=[jax.ShapeDtypeStruct(x.shape, BF) for x in xs], name=name,
        in_specs=[vmem] * n + [pl.BlockSpec(memory_space=pl.ANY)] * len(extra), out_specs=[vmem] * n)(*xs, *extra)


def _pad_rows(a, n):
    return jnp.pad(a, ((0, n - a.shape[0]), (0, 0)))


def _w_in_to_padded(wt):
    s_ba = P_CQ
    s_cq = s_ba + 2 * DN_HEADS
    s_kr = s_cq + Q_LORA + KV_LORA
    return jnp.concatenate([wt[:s_ba], wt[s_cq:s_kr], _pad_rows(wt[s_ba:s_cq], LANE), _pad_rows(wt[s_kr:], LANE)], axis=0)


def _w_in_from_padded(wt):
    return jnp.concatenate([wt[:P_CQ], wt[P_BA:P_BA + 2 * DN_HEADS], wt[P_CQ:P_BA], wt[P_KR:P_KR + QK_ROPE]], axis=0)


def _w_uq_to_padded(wt):
    w3 = wt.reshape(MLA_HEADS, QK_NOPE + QK_ROPE, Q_LORA)
    nope = w3[:, :QK_NOPE].reshape(MLA_HEADS * QK_NOPE, Q_LORA)
    rope = jnp.pad(w3[:, QK_NOPE:], ((0, 0), (0, LANE - QK_ROPE), (0, 0))).reshape(MLA_HEADS * LANE, Q_LORA)
    return jnp.concatenate([nope, rope], axis=0)


def _w_uq_from_padded(wt):
    nope = wt[:MLA_HEADS * QK_NOPE].reshape(MLA_HEADS, QK_NOPE, Q_LORA)
    rope = wt[MLA_HEADS * QK_NOPE:].reshape(MLA_HEADS, LANE, Q_LORA)[:, :QK_ROPE]
    return jnp.concatenate([nope, rope], axis=1).reshape(MLA_HEADS * (QK_NOPE + QK_ROPE), Q_LORA)


def _pack(pieces, width, row_mult):
    flat = jnp.concatenate([p.reshape(-1) for p in pieces])
    n = flat.shape[0]
    rows = -(-n // (width * row_mult)) * row_mult
    return jnp.pad(flat, (0, rows * width - n)).reshape(rows, width)


def _unpack(flat, shapes):
    out, o = [], 0
    for s in shapes:
        n = math.prod(s)
        out.append(flat[o:o + n].reshape(s))
        o += n
    return out


def kernel(x, c, positions, w_ada, b_ada, w_in, conv_w, a_log, dt_bias, dn_norm_g, q_norm_g, w_uq, kv_norm_g, w_ukv, w_o, ln1_g, ln1_b, w_gate, w_up, w_down, ln2_g, ln2_b, loss_target, m_w_ada, m_b_ada, m_w_in, m_conv_w, m_a_log, m_dt_bias, m_dn_norm_g, m_q_norm_g, m_w_uq, m_kv_norm_g, m_w_ukv, m_w_o, m_ln1_g, m_ln1_b, m_w_gate, m_w_up, m_w_down, m_ln2_g, m_ln2_b, v_w_ada, v_b_ada, v_w_in, v_conv_w, v_a_log, v_dt_bias, v_dn_norm_g, v_q_norm_g, v_w_uq, v_kv_norm_g, v_w_ukv, v_w_o, v_ln1_g, v_ln1_b, v_w_gate, v_w_up, v_w_down, v_ln2_g, v_ln2_b):
    me = 4 * lax.axis_index("x") + 2 * lax.axis_index("y") + lax.axis_index("c")
    t, d = x.shape[1], x.shape[2]
    ada_n = w_ada.shape[2]

    tr = lambda w: w[0].T
    rows = lambda a: a.reshape(-1, a.shape[2])
    (in_shard,) = _cast_bf16([tr(w_in)], "cast_w_in")
    cw = conv_w.shape[3]
    a_in, c_all, conv_all = _gather_by_chip([in_shard, c, conv_w[0, :, 0, :]], "gather_w_in_and_small")
    c_all = c_all.reshape(N_DEV, d)
    conv_full = conv_all.transpose(1, 0, 2).reshape(CONV_K, N_DEV * cw)
    conv_w8 = jnp.pad(conv_full, ((0, 8 - CONV_K), (0, 0)))

    b_ada_mine = lax.dynamic_slice(b_ada, (0, me * ada_n), (1, ada_n))
    mod_cols = _mod_fwd(c_all, w_ada[0], b_ada_mine)
    (mod_all,) = _exchange([mod_cols.reshape(N_DEV, 1, ada_n)], "scatter_mod", scatter=True)
    mod = mod_all.reshape(1, N_DEV * ada_n)

    later = _cast_bf16([tr(w_uq), tr(w_ukv), w_o[0], tr(w_gate), tr(w_up), w_down[0]], "cast_weights", after=mod)
    mixer_gather, token_a = _exchange_start(later[:3], "gather_mixer_weights_start", scatter=False)
    ffn_gather, token_b = _exchange_start(later[3:], "gather_ffn_weights_start", scatter=False)
    mod = mod + (token_a + token_b)
    w_in_t = _w_in_to_padded(rows(a_in))

    def mixer_weights(after):
        a_uq, a_ukv, a_o = _exchange_wait(mixer_gather, after, "gather_mixer_weights_wait", scatter=False)
        return _w_uq_to_padded(rows(a_uq)), rows(a_ukv), rows(a_o)

    def ffn_weights(after):
        a_gate, a_up, a_down = _exchange_wait(ffn_gather, after, "gather_ffn_weights_wait", scatter=False)
        return rows(a_gate), rows(a_up), rows(a_down)

    def by_dest(g):
        return g.reshape(N_DEV, -1, g.shape[1])

    scatters = {}

    def grads_ready(tag, *g):
        if tag == "ffn":
            pieces = [by_dest(a) for a in g]
        elif tag == "mixer":
            g_w_o, g_w_uq_t, g_w_ukv_t = g
            pieces = [by_dest(g_w_o), by_dest(_w_uq_from_padded(g_w_uq_t).astype(BF)), by_dest(g_w_ukv_t.astype(BF))]
        else:
            pieces = [by_dest(_w_in_from_padded(g[0]))]
        scatters[tag], token = _exchange_start(pieces, "scatter_%s_grads_start" % tag, scatter=True)
        return token

    loc = _local_step(x[0], loss_target[0], positions[0], mod, w_in_t, mixer_weights, ffn_weights, grads_ready,
                      conv_w8, a_log, dt_bias, dn_norm_g, q_norm_g, kv_norm_g, ln1_g, ln1_b, ln2_g, ln2_b)
    grad_x, loss_acc, dmod, d_conv8, d_al8, d_dt8, d_dn_g, d_q_g, d_kv_g, d_ln1_g, d_ln1_b, d_ln2_g, d_ln2_b = loc

    small_shapes = [(6 * d,), (CONV_K, N_DEV * cw), (DN_HEADS,), (DN_HEADS,), (DN_DV,), (Q_LORA,), (KV_LORA,), (d,), (d,), (d,), (d,), (1,)]
    gsmall = _pack([dmod, d_conv8[:CONV_K], d_al8[0, :DN_HEADS], d_dt8[0, :DN_HEADS], d_dn_g, d_q_g, d_kv_g,
                    d_ln1_g, d_ln1_b, d_ln2_g, d_ln2_b, loss_acc[0, :1]], LANE, 8)
    (gsmall_all,) = _exchange([gsmall], "gather_small_grads", scatter=False)
    dmod_all = gsmall_all.reshape(N_DEV, -1)[:, :6 * d]
    tot = _unpack(_sum_slots(gsmall_all, "sum_small_grads").reshape(-1), small_shapes)
    g_b_ada, g_conv_full, g_a_log, g_dt_bias, g_dn_g, g_q_g, g_kv_g, g_ln1_g, g_ln1_b, g_ln2_g, g_ln2_b, loss1 = tot
    loss = loss1.reshape(())
    g_conv_w = lax.dynamic_slice(g_conv_full, (0, me * cw), (CONV_K, cw))
    g_w_ada = _mod_bwd(c_all.T, lax.dynamic_slice(dmod_all, (0, me * ada_n), (N_DEV, ada_n)))

    grads = {"w_ada": g_w_ada[None], "b_ada": g_b_ada[None], "conv_w": g_conv_w[None, :, None, :],
             "a_log": g_a_log[None], "dt_bias": g_dt_bias[None], "dn_norm_g": g_dn_g[None], "q_norm_g": g_q_g[None],
             "kv_norm_g": g_kv_g[None], "ln1_g": g_ln1_g[None], "ln1_b": g_ln1_b[None], "ln2_g": g_ln2_g[None], "ln2_b": g_ln2_b[None]}
    weights = dict(w_ada=w_ada, b_ada=b_ada, w_in=w_in, conv_w=conv_w, a_log=a_log, dt_bias=dt_bias, dn_norm_g=dn_norm_g,
                   q_norm_g=q_norm_g, w_uq=w_uq, kv_norm_g=kv_norm_g, w_ukv=w_ukv, w_o=w_o, ln1_g=ln1_g, ln1_b=ln1_b,
                   w_gate=w_gate, w_up=w_up, w_down=w_down, ln2_g=ln2_g, ln2_b=ln2_b)
    ms = dict(w_ada=m_w_ada, b_ada=m_b_ada, w_in=m_w_in, conv_w=m_conv_w, a_log=m_a_log, dt_bias=m_dt_bias,
              dn_norm_g=m_dn_norm_g, q_norm_g=m_q_norm_g, w_uq=m_w_uq, kv_norm_g=m_kv_norm_g, w_ukv=m_w_ukv, w_o=m_w_o,
              ln1_g=m_ln1_g, ln1_b=m_ln1_b, w_gate=m_w_gate, w_up=m_w_up, w_down=m_w_down, ln2_g=m_ln2_g, ln2_b=m_ln2_b)
    vs = dict(w_ada=v_w_ada, b_ada=v_b_ada, w_in=v_w_in, conv_w=v_conv_w, a_log=v_a_log, dt_bias=v_dt_bias,
              dn_norm_g=v_dn_norm_g, q_norm_g=v_q_norm_g, w_uq=v_w_uq, kv_norm_g=v_kv_norm_g, w_ukv=v_w_ukv, w_o=v_w_o,
              ln1_g=v_ln1_g, ln1_b=v_ln1_b, w_gate=v_w_gate, w_up=v_w_up, w_down=v_w_down, ln2_g=v_ln2_g, ln2_b=v_ln2_b)
    names = list(weights)
    big = ("w_ada", "w_gate", "w_up", "w_down", "w_o", "w_uq", "w_ukv", "w_in")
    waits = {"w_gate": ("ffn", ("w_gate", "w_up", "w_down")), "w_o": ("mixer", ("w_o", "w_uq", "w_ukv")), "w_in": ("in", ("w_in",))}
    delta_w, new_m, new_v, slots = {}, {}, {}, {}
    last = g_w_ada
    for n in big:
        if n == "w_in":
            rest = [r for r in names if r not in big]
            flat2 = lambda a: a.reshape(-1, a.shape[-1])
            outs = _adamw_many(*[[flat2(src[r]) for r in rest] for src in (weights, grads, ms, vs)], "adamw_small")
            for dst, o in zip((delta_w, new_m, new_v), outs):
                for r, a in zip(rest, o):
                    dst[r] = a.reshape(weights[r].shape)
            last = outs[0][0]
        transposed = n in ("w_in", "w_uq", "w_ukv", "w_gate", "w_up")
        two = (lambda a: a[0].T) if transposed else (lambda a: a[0])
        back = (lambda a: a.T[None]) if transposed else (lambda a: a[None])
        if n in waits:
            tag, members = waits[n]
            slots.update(zip(members, _exchange_wait(scatters[tag], last, "scatter_%s_grads_wait" % tag, scatter=True)))
        g_in = slots[n] if n in slots else two(grads[n])
        gr, dlt, nm, nv = _adamw(two(weights[n]), g_in, two(ms[n]), two(vs[n]), "adamw_" + n)
        grads[n], delta_w[n], new_m[n], new_v[n] = back(gr), back(dlt), back(nm), back(nv)
        last = nv

    return (loss, grad_x[None], *[grads[n] for n in names], *[delta_w[n] for n in names],
            *[new_m[n] for n in names], *[new_v[n] for n in names])


def _local_step(xs, tgt, pos, mod, w_in_t, mixer_weights, ffn_weights, grads_ready, conv_w8,
                a_log, dt_bias, dn_norm_g, q_norm_g, kv_norm_g, ln1_g, ln1_b, ln2_g, ln2_b):
    t, d = xs.shape
    sh_m, sc_m, gt_m, sh_f, sc_f, gt_f = [mod[:, i * d:(i + 1) * d] for i in range(6)]
    pos_col = pos.astype(F32).reshape(t, 1)
    inv_freq = 1.0 / (ROPE_THETA ** (jnp.arange(0, QK_ROPE, 2, dtype=F32) / QK_ROPE))
    inv_freq2 = jnp.pad(jnp.concatenate([inv_freq, inv_freq]), (0, LANE - QK_ROPE)).reshape(1, LANE)
    al8 = jnp.pad(a_log, ((0, 7), (0, LANE - DN_HEADS)))
    dt8 = jnp.pad(dt_bias, ((0, 7), (0, LANE - DN_HEADS)))

    tm = min(512, t)
    tq = min(256, t)
    tk = min(512, t)

    def modulate_in(xx, sc, sh, w_ba):
        h = (xx * (1.0 + sc) + sh).astype(BF)
        return h, _dot(h, w_ba, NT)

    h1, ba_raw = _rowwise("modulate_in", modulate_in, [xs], [sc_m, sh_m, w_in_t[P_BA:P_BA + LANE]], [(d, BF), (LANE, F32)], [], tm)
    proj = _matmul(h1, w_in_t, "nt", "in_proj", BF)
    qkv = _conv_fwd(proj, conv_w8, min(256, t))
    gdn_tm = min(512, t)
    intra, inverses = _gdn_intra_fwd(qkv, ba_raw, al8, dt8, gdn_tm)
    o_dn, states = _gdn_scan_fwd(intra, gdn_tm)
    w_uq_t, w_ukv_t, w_o_f = mixer_weights(states)
    qc, kc, vc = _mla_prep_fwd(proj, pos_col, inv_freq2, q_norm_g, kv_norm_g, w_uq_t, w_ukv_t, tm)
    o_mla, lse = _attn_fwd(qc, kc, vc, min(1024, t), min(1024, t))

    def mix_in(o, z, om, g):
        return jnp.concatenate(_gdn_out(o, z.astype(F32), g) + [om], axis=1)

    (mixin,) = _rowwise("mixer_out", mix_in, [o_dn, (proj, DN_VW, P_Z // DN_VW), o_mla], [dn_norm_g], [(2 * DN_VW, BF)], [], tm)
    mix = _matmul(mixin, w_o_f, "nn", "out_proj", BF)

    def block1(xx, mx, gt, g1, b1, sc, sh):
        x1 = _layernorm(DEEPNORM_ALPHA * xx + gt * mx, g1, b1)
        return x1, x1 * (1.0 + sc) + sh

    x1, h2 = _rowwise("norm1_modulate", block1, [xs, mix], [gt_m, ln1_g, ln1_b, sc_f, sh_f], [(d, F32), (d, BF)], [], tm)
    w_gate_f, w_up_f, w_down_f = ffn_weights(h2)
    act, act_dg, act_du = _ffn_in(h2, w_gate_f, w_up_f)
    ff = _matmul(act, w_down_f, "nn", "ffn_out", BF)

    def tail_loss(x1_, ff_, gt, g2, b2, tg):
        y = _layernorm(DEEPNORM_ALPHA * x1_ + gt * ff_, g2, b2)
        return 0.5 * jnp.sum(jnp.mean(jnp.square(y - tg), axis=-1))

    def tail(x1_, ff_, tg, gt, g2, b2):
        loss, (dx1, dff, dgt, dg2, db2) = jax.value_and_grad(tail_loss, argnums=(0, 1, 2, 3, 4))(x1_, ff_, gt, g2, b2, tg)
        return dx1, dff, jnp.full((1, LANE), loss, F32), dgt, dg2, db2

    dx1_a, dff, loss_acc, d_gt_f, d_ln2_g, d_ln2_b = _rowwise(
        "norm2_loss", tail, [x1, ff, tgt], [gt_f, ln2_g, ln2_b], [(d, BF), (d, BF)], [(1, LANE), (1, d), (1, d), (1, d)], tm)

    g_w_down = _matmul(act, dff, "tn", "d_w_down", BF)
    dgate, dup = _ffn_act_bwd(dff, w_down_f, act_dg, act_du)
    g_w_gate = _matmul(dgate, h2, "tn", "d_w_gate", BF)
    g_w_up = _matmul(dup, h2, "tn", "d_w_up", BF)
    token = grads_ready("ffn", g_w_gate, g_w_up, g_w_down)
    dh2 = _matmul2_nn(dgate, w_gate_f, dup, w_up_f, "d_ffn_in", BF)

    def block1_bwd(xx, mx, dx1_, dh2_, gt, g1, b1, sc, sh):
        _, vjp = jax.vjp(block1, xx, mx, gt, g1, b1, sc, sh)
        dxx, dmx, dgt, dg1, db1, dsc, dsh = vjp((dx1_.astype(F32), dh2_.astype(F32)))
        return dxx, dmx, dgt, dg1, db1, dsc, dsh

    dx_a, dmix, d_gt_m, d_ln1_g, d_ln1_b, d_sc_f, d_sh_f = _rowwise(
        "norm1_modulate_bwd", block1_bwd, [xs, mix, dx1_a, dh2], [gt_m + token, ln1_g, ln1_b, sc_f, sh_f],
        [(d, F32), (d, BF)], [(1, d)] * 5, tm)

    dmixin = _matmul(dmix, w_o_f, "nt", "d_mixer_out", BF)
    g_w_o = _matmul(mixin, dmix, "tn", "d_w_o", BF)

    def mixer_bwd(o, z, om, dmi, g):
        _, vjp = jax.vjp(lambda o_, z_, g_: jnp.concatenate(_gdn_out(o_, z_, g_), axis=1), o, z.astype(F32), g)
        do_, dz_, dg_ = vjp(dmi[:, :DN_VW].astype(F32))
        dom = dmi[:, DN_VW:]
        delta = [jnp.broadcast_to(jnp.sum(dom[:, h * V_HEAD:(h + 1) * V_HEAD] * om[:, h * V_HEAD:(h + 1) * V_HEAD], axis=-1, keepdims=True), (o.shape[0], LANE))
                 for h in range(MLA_HEADS)]
        return do_, dz_, dom, jnp.concatenate(delta, axis=1), dg_

    do_dn, dz, do_mla, delta, d_dn_g = _rowwise(
        "mixer_out_bwd", mixer_bwd, [o_dn, (proj, DN_VW, P_Z // DN_VW), o_mla, dmixin], [dn_norm_g],
        [(DN_VW, F32), (DN_VW, BF), (MLA_VW, BF), (MLA_HEADS * LANE, F32)], [(1, DN_DV)], tm)

    dqc, dkc, dvc = _attn_bwd(qc, kc, vc, do_mla, lse, delta, min(512, t), min(1024, t))
    dcq, dckv, dkr, d_q_g, d_kv_g, g_w_uq_t, g_w_ukv_t = _mla_prep_bwd(
        proj, pos_col, inv_freq2, q_norm_g, kv_norm_g, w_uq_t, w_ukv_t, dqc, dkc, dvc, tm)

    token = grads_ready("mixer", g_w_o, g_w_uq_t, g_w_ukv_t)

    d_intra = _gdn_scan_bwd(intra, states, do_dn, gdn_tm)
    dqkv_act, dba, d_al8, d_dt8 = _gdn_intra_bwd(qkv, ba_raw, al8 + token, dt8, inverses, d_intra, min(256, t))
    dproj, d_conv8 = _conv_bwd(proj, conv_w8, dqkv_act, [(P_Z, dz), (P_CQ, dcq), (P_CKV, dckv), (P_BA, dba), (P_KR, dkr)],
                               min(256, t))
    dh1 = _matmul(dproj, w_in_t, "nn", "d_in_proj", BF)
    g_w_in_t = _matmul(dproj, h1, "tn", "d_w_in", BF)
    token = grads_ready("in", g_w_in_t)

    def modulate_bwd(xx, dh, dxa, sc):
        dh = dh.astype(F32)
        return dh * (1.0 + sc) + dxa, jnp.sum(dh * xx, axis=0, keepdims=True), jnp.sum(dh, axis=0, keepdims=True)

    grad_x, d_sc_m, d_sh_m = _rowwise("modulate_in_bwd", modulate_bwd, [xs, dh1, dx_a], [sc_m + token], [(d, F32)], [(1, d), (1, d)], tm)
    dmod = jnp.concatenate([d_sh_m, d_sc_m, d_gt_m, d_sh_f, d_sc_f, d_gt_f], axis=1)
    return grad_x, loss_acc, dmod, d_conv8, d_al8, d_dt8, d_dn_g, d_q_g, d_kv_g, d_ln1_g, d_ln1_b, d_ln2_g, d_ln2_b
```

```python
import functools
import math

import jax
import jax.numpy as jnp
from jax import lax
from jax.experimental import pallas as pl
from jax.experimental.pallas import tpu as pltpu

F32 = jnp.float32
BF = jnp.bfloat16
HI = lax.Precision.HIGHEST

N_DEV = 8
DN_HEADS = 4
DN_DK = 128
DN_DV = 128
CONV_K = 4
CHUNK = 64
MLA_HEADS = 4
QK_NOPE = 128
QK_ROPE = 64
V_HEAD = 128
Q_LORA = 512
KV_LORA = 256
ROPE_THETA = 10000.0
DEPTH = 1
DEEPNORM_ALPHA = (2.0 * DEPTH) ** 0.25
LANE = 128
CONV_HALO = 8
GL_ROWS = 8
CONV_ROWS, CONV_COLS = 64, 256

DN_QK = DN_HEADS * DN_DK
DN_VW = DN_HEADS * DN_DV
DN_CONV_CH = 2 * DN_QK + DN_VW
MLA_VW = MLA_HEADS * V_HEAD
P_Z = DN_CONV_CH
P_CQ = P_Z + DN_VW
P_CKV = P_CQ + Q_LORA
P_BA = P_CKV + KV_LORA
P_KR = P_BA + LANE
N_INP = P_KR + LANE

ADAM_LR = 0.001
ADAM_B1 = 0.9
ADAM_B2 = 0.999
ADAM_EPS = 1e-08
ADAM_WD = 0.01
ADAM_STEP = 10

NN = (((1,), (0,)), ((), ()))
NT = (((1,), (1,)), ((), ()))
TN = (((0,), (0,)), ((), ()))


def _pick(n, prefs):
    for p in prefs:
        if n % p == 0:
            return p
    return n


def _full(shape):
    return pl.BlockSpec(shape, lambda *_: (0,) * len(shape))


def _dot(a, b, dims=NN):
    return lax.dot_general(a, b, dims, preferred_element_type=F32)


def _doth(a, b, dims=NN):
    return lax.dot_general(a, b, dims, precision=HI, preferred_element_type=F32)


@jax.custom_vjp
def _mmb(a, b):
    return _dot(a.astype(BF), b.astype(BF), NN)


def _mmb_fwd(a, b):
    return _mmb(a, b), (a, b)


def _mmb_bwd(res, g):
    a, b = res
    gb = g.astype(BF)
    return (_dot(gb, b.astype(BF), NT).astype(a.dtype), _dot(a.astype(BF), gb, TN).astype(b.dtype))


_mmb.defvjp(_mmb_fwd, _mmb_bwd)


@jax.custom_vjp
def _mmb_nt(a, b):
    return _dot(a.astype(BF), b.astype(BF), NT)


def _mmb_nt_fwd(a, b):
    return _mmb_nt(a, b), (a, b)


def _mmb_nt_bwd(res, g):
    a, b = res
    gb = g.astype(BF)
    return (_dot(gb, b.astype(BF), NN).astype(a.dtype), _dot(gb, a.astype(BF), TN).astype(b.dtype))


_mmb_nt.defvjp(_mmb_nt_fwd, _mmb_nt_bwd)


def _sigmoid(x):
    return 0.5 * (jnp.tanh(0.5 * x) + 1.0)


def _silu(x):
    return x * _sigmoid(x)


def _softplus(x):
    return jnp.maximum(x, 0.0) + jnp.log(1.0 + jnp.exp(-jnp.abs(x)))


def _layernorm(x, g, b, eps=1e-5):
    mu = jnp.mean(x, axis=-1, keepdims=True)
    xc = x - mu
    var = jnp.mean(xc * xc, axis=-1, keepdims=True)
    return xc * lax.rsqrt(var + eps) * g + b


def _rmsnorm(x, g, eps=1e-6):
    return x * lax.rsqrt(jnp.mean(x * x, axis=-1, keepdims=True) + eps) * g


def _l2norm(x, eps=1e-6):
    return x * lax.rsqrt(jnp.sum(x * x, axis=-1, keepdims=True) + eps)


def _rowwise(name, fn, rows, vecs, out_rows, out_accs, tm):
    rows = [r if isinstance(r, tuple) else (r, r.shape[1], 0) for r in rows]
    t = rows[0][0].shape[0]
    tm = min(tm, t)
    assert t % tm == 0
    nr, nv, no = len(rows), len(vecs), len(out_rows)

    def body(*refs):
        ins = [r[...] for r in refs[:nr + nv]]
        outs = fn(*ins)
        outs = outs if isinstance(outs, (tuple, list)) else (outs,)
        o_rows = refs[nr + nv:nr + nv + no]
        o_accs = refs[nr + nv + no:]
        for o, val in zip(o_rows, outs[:no]):
            o[...] = val.astype(o.dtype)
        if o_accs:
            @pl.when(pl.program_id(0) == 0)
            def _():
                for o in o_accs:
                    o[...] = jnp.zeros_like(o)
            for o, val in zip(o_accs, outs[no:]):
                o[...] += val

    in_specs = [pl.BlockSpec((tm, w), functools.partial(lambda i, j: (i, j), j=j)) for (_, w, j) in rows]
    in_specs += [_full(v.shape) for v in vecs]
    out_specs = [pl.BlockSpec((tm, w), lambda i: (i, 0)) for (w, _) in out_rows]
    out_specs += [_full(s) for s in out_accs]
    out_shape = [jax.ShapeDtypeStruct((t, w), d) for (w, d) in out_rows]
    out_shape += [jax.ShapeDtypeStruct(s, F32) for s in out_accs]
    res = pl.pallas_call(
        body, grid=(t // tm,), in_specs=in_specs, out_specs=out_specs, out_shape=out_shape, name=name,
        compiler_params=pltpu.CompilerParams(dimension_semantics=("arbitrary",)),
    )(*[r[0] for r in rows], *vecs)
    return res


def _matmul(a, b, mode, name, out_dtype=F32):
    if mode == "nn":
        (m, k), n = a.shape, b.shape[1]
    elif mode == "nt":
        (m, k), n = a.shape, b.shape[0]
    else:
        (k, m), n = a.shape, b.shape[1]
    tm, tn, tk = _matmul_tiles(m, n, k, a.dtype.itemsize, b.dtype.itemsize, jnp.dtype(out_dtype).itemsize)
    nk = k // tk
    dims = {"nn": NN, "nt": NT, "tn": TN}[mode]

    def body(a_ref, b_ref, o_ref, *acc):
        part = _dot(a_ref[...].astype(BF), b_ref[...].astype(BF), dims)
        if nk == 1:
            o_ref[...] = part.astype(o_ref.dtype)
            return
        (acc_ref,) = acc
        kk = pl.program_id(2)

        @pl.when(kk == 0)
        def _():
            acc_ref[...] = part

        @pl.when(kk > 0)
        def _():
            acc_ref[...] += part

        @pl.when(kk == nk - 1)
        def _():
            o_ref[...] = acc_ref[...].astype(o_ref.dtype)

    a_spec = pl.BlockSpec((tk, tm), lambda i, j, kk: (kk, i)) if mode == "tn" else pl.BlockSpec((tm, tk), lambda i, j, kk: (i, kk))
    b_spec = pl.BlockSpec((tn, tk), lambda i, j, kk: (j, kk)) if mode == "nt" else pl.BlockSpec((tk, tn), lambda i, j, kk: (kk, j))
    return pl.pallas_call(
        body, grid=(m // tm, n // tn, nk), in_specs=[a_spec, b_spec],
        out_specs=pl.BlockSpec((tm, tn), lambda i, j, kk: (i, j)),
        out_shape=jax.ShapeDtypeStruct((m, n), out_dtype),
        scratch_shapes=[pltpu.VMEM((tm, tn), F32)] if nk > 1 else [], name=name,
        compiler_params=pltpu.CompilerParams(dimension_semantics=("parallel", "parallel", "arbitrary")),
    )(a, b)


def _lane_tile(n, cap):
    return max([n // s for s in range(1, n // LANE + 1) if n % s == 0 and (n // s) % LANE == 0 and n // s <= cap] or [n])


def _ffn_in(h, w_gate, w_up):
    m, k = h.shape
    f = w_gate.shape[0]
    tm, tn = _pick(m, (1024, 512, 256, 128)), _lane_tile(f, 1408)

    def body(h_ref, wg_ref, wu_ref, act_ref, dg_ref, du_ref):
        hh = h_ref[...]
        g = _dot(hh, wg_ref[...], NT)
        u = _dot(hh, wu_ref[...], NT)
        sg = _sigmoid(g)
        silu_g = g * sg
        act_ref[...] = (silu_g * u).astype(act_ref.dtype)
        dg_ref[...] = (u * (sg + silu_g * (1.0 - sg))).astype(dg_ref.dtype)
        du_ref[...] = silu_g.astype(du_ref.dtype)

    w_spec = pl.BlockSpec((tn, k), lambda i, j: (j, 0))
    o_spec = pl.BlockSpec((tm, tn), lambda i, j: (i, j))
    return pl.pallas_call(
        body, grid=(m // tm, f // tn), in_specs=[pl.BlockSpec((tm, k), lambda i, j: (i, 0)), w_spec, w_spec],
        out_specs=[o_spec] * 3, out_shape=[jax.ShapeDtypeStruct((m, f), BF)] * 3, name="ffn_in",
        compiler_params=pltpu.CompilerParams(dimension_semantics=("parallel", "parallel")),
    )(h, w_gate, w_up)


def _ffn_act_bwd(dff, w_down, act_dg, act_du):
    m, k = dff.shape
    f = w_down.shape[0]
    tm, tn = _pick(m, (1024, 512, 256, 128)), _lane_tile(f, 1408)

    def body(d_ref, w_ref, fg_ref, fu_ref, dg_ref, du_ref):
        da = _dot(d_ref[...], w_ref[...], NT)
        dg_ref[...] = (da * fg_ref[...].astype(F32)).astype(dg_ref.dtype)
        du_ref[...] = (da * fu_ref[...].astype(F32)).astype(du_ref.dtype)

    o_spec = pl.BlockSpec((tm, tn), lambda i, j: (i, j))
    return pl.pallas_call(
        body, grid=(m // tm, f // tn),
        in_specs=[pl.BlockSpec((tm, k), lambda i, j: (i, 0)), pl.BlockSpec((tn, k), lambda i, j: (j, 0)), o_spec, o_spec],
        out_specs=[o_spec] * 2, out_shape=[jax.ShapeDtypeStruct((m, f), BF)] * 2, name="d_ffn_act",
        compiler_params=pltpu.CompilerParams(dimension_semantics=("parallel", "parallel")),
    )(dff, w_down, act_dg, act_du)


def _matmul2_nn(a1, b1, a2, b2, name, out_dtype=F32):
    m, k = a1.shape
    n = b1.shape[1]
    tm, tn = _pick(m, (1024, 512, 256, 128)), _pick(n, (512, 256, 128))

    def body(a1_ref, b1_ref, a2_ref, b2_ref, o_ref):
        o_ref[...] = (_dot(a1_ref[...], b1_ref[...]) + _dot(a2_ref[...], b2_ref[...])).astype(o_ref.dtype)

    a_spec = pl.BlockSpec((tm, k), lambda i, j: (i, 0))
    b_spec = pl.BlockSpec((k, tn), lambda i, j: (0, j))
    return pl.pallas_call(
        body, grid=(m // tm, n // tn), in_specs=[a_spec, b_spec, a_spec, b_spec],
        out_specs=pl.BlockSpec((tm, tn), lambda i, j: (i, j)), out_shape=jax.ShapeDtypeStruct((m, n), out_dtype), name=name,
        compiler_params=pltpu.CompilerParams(dimension_semantics=("parallel", "parallel")),
    )(a1, b1, a2, b2)


MATMUL_VMEM_BUDGET = 28 * 1024 * 1024


def _matmul_tiles(m, n, k, a_bytes, b_bytes, o_bytes):
    def divisors(x, cap):
        return sorted({x // s for s in range(1, 65) if x % s == 0 and (x // s) % LANE == 0 and x // s <= cap}, reverse=True) or [x]

    for tk in divisors(k, k):
        best = None
        for tm in divisors(m, 1024):
            for tn in divisors(n, 2048):
                need = 2 * (tm * tk * a_bytes + tk * tn * b_bytes + tm * tn * o_bytes) + (tm * tn * 4 if tk < k else 0)
                if need <= MATMUL_VMEM_BUDGET and tm * tn >= 512 * 512 and (best is None or tm * tn > best[0] * best[1]):
                    best = (tm, tn)
        if best:
            return best[0], best[1], tk
    return _pick(m, (512, 256, 128)), _pick(n, (512, 256, 128)), _pick(k, (512, 256, 128))


def _exchange(xs, name, scatter):
    n = len(xs)
    npeer = N_DEV - 1

    def body(*refs):
        x_refs, o_refs = refs[:n], refs[n:2 * n]
        send_sems, recv_sems, local_sems = refs[2 * n:]
        mx, my, mc = lax.axis_index("x"), lax.axis_index("y"), lax.axis_index("c")
        me = 4 * mx + 2 * my + mc
        src_me = [x.at[me] if scatter else x for x in x_refs]
        mine = [pltpu.make_async_copy(src_me[a], o_refs[a].at[me], local_sems.at[a]) for a in range(n)]
        for cp in mine:
            cp.start()
        copies = []
        for k in range(1, N_DEV):
            px, py, pc = mx ^ (k >> 2), my ^ ((k >> 1) & 1), mc ^ (k & 1)
            peer = 4 * px + 2 * py + pc
            for a in range(n):
                cp = pltpu.make_async_remote_copy(
                    src_ref=x_refs[a].at[peer] if scatter else x_refs[a], dst_ref=o_refs[a].at[me],
                    send_sem=send_sems.at[a * npeer + k - 1], recv_sem=recv_sems.at[a * npeer + k - 1],
                    device_id=(px, py, pc), device_id_type=pl.DeviceIdType.MESH)
                cp.start()
                copies.append((cp, a, k, peer))
        for cp, a, k, peer in copies:
            pltpu.make_async_remote_copy(
                src_ref=src_me[a], dst_ref=o_refs[a].at[peer], send_sem=send_sems.at[a * npeer + k - 1],
                recv_sem=recv_sems.at[a * npeer + k - 1], device_id=(mx, my, mc),
                device_id_type=pl.DeviceIdType.MESH).wait_recv()
        for cp, _, _, _ in copies:
            cp.wait_send()
        for cp in mine:
            cp.wait()

    return pl.pallas_call(
        body, out_shape=[jax.ShapeDtypeStruct((N_DEV,) + x.shape[-2:], x.dtype) for x in xs],
        in_specs=[pl.BlockSpec(memory_space=pl.ANY)] * n, out_specs=[pl.BlockSpec(memory_space=pl.ANY)] * n,
        scratch_shapes=[pltpu.SemaphoreType.DMA((n * npeer,)), pltpu.SemaphoreType.DMA((n * npeer,)),
                        pltpu.SemaphoreType.DMA((n,))],
        name=name,
    )(*xs)


def _gather_by_chip(xs, name):
    n = len(xs)
    per = N_DEV - 1

    def body(*refs):
        x_refs, o_refs = refs[:n], refs[n:2 * n]
        send_sems, recv_sems, local_sems = refs[2 * n:]
        mx, my, mc = lax.axis_index("x"), lax.axis_index("y"), lax.axis_index("c")
        me, sibling = (mx, my, mc), (mx, my, 1 - mc)
        chips = [(1 - mx, my), (mx, 1 - my), (1 - mx, 1 - my)]
        slot = lambda d: 4 * d[0] + 2 * d[1] + d[2]

        def copy(a, k, block, to, src=None):
            dst = o_refs[a].at[slot(block)]
            return pltpu.make_async_remote_copy(
                src_ref=dst if src is None else src, dst_ref=dst, send_sem=send_sems.at[a * per + k],
                recv_sem=recv_sems.at[a * per + k], device_id=to, device_id_type=pl.DeviceIdType.MESH)

        mine = [pltpu.make_async_copy(x_refs[a], o_refs[a].at[slot(me)], local_sems.at[a]) for a in range(n)]
        for cp in mine:
            cp.start()
        first = []
        for a in range(n):
            first.append(copy(a, 0, me, sibling, src=x_refs[a]))
            first += [copy(a, 1 + j, me, (*chip, mc), src=x_refs[a]) for j, chip in enumerate(chips)]
        for cp in first:
            cp.start()
        passed = []
        for j, chip in enumerate(chips):
            for a in range(n):
                copy(a, 1 + j, (*chip, mc), me).wait_recv()
                cp = copy(a, 4 + j, (*chip, mc), sibling)
                cp.start()
                passed.append(cp)
        for a in range(n):
            copy(a, 0, sibling, me).wait_recv()
            for j, chip in enumerate(chips):
                copy(a, 4 + j, (*chip, 1 - mc), me).wait_recv()
        for cp in first + passed:
            cp.wait_send()
        for cp in mine:
            cp.wait()

    return pl.pallas_call(
        body, out_shape=[jax.ShapeDtypeStruct((N_DEV,) + x.shape, x.dtype) for x in xs],
        in_specs=[pl.BlockSpec(memory_space=pl.ANY)] * n, out_specs=[pl.BlockSpec(memory_space=pl.ANY)] * n,
        scratch_shapes=[pltpu.SemaphoreType.DMA((n * per,)), pltpu.SemaphoreType.DMA((n * per,)),
                        pltpu.SemaphoreType.DMA((n,))],
        name=name,
    )(*xs)


def _peer_of(k):
    mx, my, mc = lax.axis_index("x"), lax.axis_index("y"), lax.axis_index("c")
    px, py, pc = mx ^ (k >> 2), my ^ ((k >> 1) & 1), mc ^ (k & 1)
    return (px, py, pc), 4 * px + 2 * py + pc


def _exchange_start(xs, name, scatter):
    n = len(xs)
    npeer = N_DEV - 1

    def body(*refs):
        x_refs, land_refs = refs[:n], refs[n:2 * n]
        send_sems, recv_sems, token = refs[2 * n], refs[2 * n + 1], refs[-1]
        me = 4 * lax.axis_index("x") + 2 * lax.axis_index("y") + lax.axis_index("c")
        for k in range(1, N_DEV):
            dev, peer = _peer_of(k)
            for a in range(n):
                pltpu.make_async_remote_copy(
                    src_ref=x_refs[a].at[peer] if scatter else x_refs[a], dst_ref=land_refs[a].at[me],
                    send_sem=send_sems.at[a * npeer + k - 1], recv_sem=recv_sems.at[a * npeer + k - 1],
                    device_id=dev, device_id_type=pl.DeviceIdType.MESH).start()
        token[...] = jnp.zeros_like(token)

    hbm = pl.BlockSpec(memory_space=pltpu.HBM)
    sem = pl.BlockSpec(memory_space=pltpu.SEMAPHORE)
    lands = [pltpu.with_memory_space_constraint(lax.empty((N_DEV,) + x.shape[-2:], x.dtype), pltpu.HBM) for x in xs]
    srcs = [pltpu.with_memory_space_constraint(x, pltpu.HBM) for x in xs]
    outs = pl.pallas_call(
        body, name=name,
        out_shape=(pltpu.SemaphoreType.DMA((n * npeer,)), pltpu.SemaphoreType.DMA((n * npeer,)),
                   *[pltpu.HBM(x.shape, x.dtype) for x in srcs], *[pltpu.HBM(z.shape, z.dtype) for z in lands],
                   jax.ShapeDtypeStruct((8, LANE), F32)),
        in_specs=[hbm] * (2 * n), out_specs=(sem, sem, *[hbm] * (2 * n), pl.BlockSpec(memory_space=pltpu.VMEM)),
        input_output_aliases={i: 2 + i for i in range(2 * n)},
        compiler_params=pltpu.CompilerParams(has_side_effects=pltpu.SideEffectType.DATAFLOW_SIDE_EFFECTING),
    )(*srcs, *lands)
    return (outs[0], outs[1], list(outs[2:2 + n]), list(outs[2 + n:2 + 2 * n])), outs[-1][0:1, 0:1]


def _exchange_wait(started, after, name, scatter):
    send_sems, recv_sems, srcs, lands = started
    n = len(srcs)
    npeer = N_DEV - 1

    def body(*refs):
        x_refs, land_refs = refs[:n], refs[n:2 * n]
        send_sems, recv_sems = refs[2 * n], refs[2 * n + 1]
        mx, my, mc = lax.axis_index("x"), lax.axis_index("y"), lax.axis_index("c")
        me = 4 * mx + 2 * my + mc
        for k in range(1, N_DEV):
            _, peer = _peer_of(k)
            for a in range(n):
                src = x_refs[a].at[me] if scatter else x_refs[a]
                cp = pltpu.make_async_remote_copy(
                    src_ref=src, dst_ref=land_refs[a].at[peer], send_sem=send_sems.at[a * npeer + k - 1],
                    recv_sem=recv_sems.at[a * npeer + k - 1], device_id=(mx, my, mc), device_id_type=pl.DeviceIdType.MESH)
                cp.wait_send()
                cp.wait_recv()

    hbm = pl.BlockSpec(memory_space=pltpu.HBM)
    sem = pl.BlockSpec(memory_space=pltpu.SEMAPHORE)
    outs = pl.pallas_call(
        body, name=name,
        out_shape=(*[pltpu.HBM(x.shape, x.dtype) for x in srcs], *[pltpu.HBM(z.shape, z.dtype) for z in lands]),
        in_specs=[hbm] * (2 * n) + [sem, sem, pl.BlockSpec(memory_space=pl.ANY)], out_specs=tuple([hbm] * (2 * n)),
        input_output_aliases={i: i for i in range(2 * n)},
        compiler_params=pltpu.CompilerParams(has_side_effects=pltpu.SideEffectType.DATAFLOW_SIDE_EFFECTING),
    )(*srcs, *lands, send_sems, recv_sems, after)
    me = 4 * lax.axis_index("x") + 2 * lax.axis_index("y") + lax.axis_index("c")
    full = []
    for x, land in zip(outs[:n], outs[n:]):
        own = lax.dynamic_slice(x, (me, 0, 0), (1,) + x.shape[1:]) if scatter else x[None]
        full.append(lax.dynamic_update_slice(land, own, (me, 0, 0)))
    return full


def _sum_slots(x, name):
    _, r, c = x.shape
    tr = _pick(r, (512, 256, 128, 64, 32, 16))

    def body(x_ref, o_ref):
        acc = x_ref[0].astype(F32)
        for s in range(1, N_DEV):
            acc = acc + x_ref[s].astype(F32)
        o_ref[...] = acc

    return pl.pallas_call(
        body, grid=(r // tr,), in_specs=[pl.BlockSpec((N_DEV, tr, c), lambda i: (0, i, 0))],
        out_specs=pl.BlockSpec((tr, c), lambda i: (i, 0)), out_shape=jax.ShapeDtypeStruct((r, c), F32), name=name,
        compiler_params=pltpu.CompilerParams(dimension_semantics=("arbitrary",)),
    )(x)


def _mod_fwd(c_all, w_ada, b_ada_mine):
    def body(c_ref, w_ref, b_ref, o_ref):
        o_ref[...] = _doth(_silu(c_ref[...]), w_ref[...]) + b_ref[...]

    return pl.pallas_call(body, out_shape=jax.ShapeDtypeStruct((c_all.shape[0], w_ada.shape[1]), F32), name="mod_fwd")(c_all, w_ada, b_ada_mine)


def _mod_bwd(c_all_t, dmod_mine):
    def body(ct_ref, d_ref, o_ref):
        s = _silu(ct_ref[...])
        acc = s[:, 0:1] * d_ref[pl.ds(0, 1), :]
        for b in range(1, N_DEV):
            acc = acc + s[:, b:b + 1] * d_ref[pl.ds(b, 1), :]
        o_ref[...] = acc

    return pl.pallas_call(body, out_shape=jax.ShapeDtypeStruct((c_all_t.shape[0], dmod_mine.shape[1]), F32), name="mod_bwd")(c_all_t, dmod_mine)


def _conv_fwd(proj, conv_w8, tm):
    t = proj.shape[0]
    ch = DN_CONV_CH

    def body(x_ref, w_ref, o_ref, buf):
        @pl.when(pl.program_id(0) == 0)
        def _():
            buf[pl.ds(0, CONV_HALO), :] = jnp.zeros((CONV_HALO, ch), F32)

        buf[pl.ds(CONV_HALO, tm), :] = x_ref[...].astype(F32)
        for c0 in range(0, ch, CONV_COLS):
            cols = pl.ds(c0, CONV_COLS)
            w = [w_ref[pl.ds(j, 1), cols] for j in range(CONV_K)]
            for r0 in range(0, tm, CONV_ROWS):
                acc = buf[pl.ds(r0 + CONV_HALO - (CONV_K - 1), CONV_ROWS), cols] * w[0]
                for j in range(1, CONV_K):
                    acc = acc + buf[pl.ds(r0 + CONV_HALO - (CONV_K - 1) + j, CONV_ROWS), cols] * w[j]
                o_ref[pl.ds(r0, CONV_ROWS), cols] = _silu(acc)
        buf[pl.ds(0, CONV_HALO), :] = buf[pl.ds(tm, CONV_HALO), :]

    return pl.pallas_call(
        body, grid=(t // tm,), in_specs=[pl.BlockSpec((tm, ch), lambda i: (i, 0)), _full(conv_w8.shape)],
        out_specs=pl.BlockSpec((tm, ch), lambda i: (i, 0)), out_shape=jax.ShapeDtypeStruct((t, ch), F32),
        scratch_shapes=[pltpu.VMEM((tm + CONV_HALO, ch), F32)], name="conv_fwd",
        compiler_params=pltpu.CompilerParams(dimension_semantics=("arbitrary",)),
    )(proj, conv_w8)


def _conv_bwd(proj, conv_w8, dact, others, tm):
    t = proj.shape[0]
    ch = DN_CONV_CH
    nt = t // tm
    halo_blk = 2 * CONV_HALO
    hb = tm // halo_blk
    n_others = len(others)

    def body(x_ref, xp_ref, w_ref, dy_ref, *refs):
        piece_refs, (dx_ref, dw_ref, xbuf, dbuf) = refs[:n_others], refs[n_others:]
        step = pl.program_id(0)
        for (off, arr), p_ref in zip(others, piece_refs):
            dx_ref[:, pl.ds(off, arr.shape[1])] = p_ref[...].astype(dx_ref.dtype)

        @pl.when(step == 0)
        def _():
            dbuf[pl.ds(tm, CONV_HALO), :] = jnp.zeros((CONV_HALO, ch), F32)
            dw_ref[...] = jnp.zeros_like(dw_ref)

        first = step == nt - 1
        xbuf[pl.ds(0, CONV_HALO), :] = jnp.where(first, 0.0, xp_ref[...].astype(F32)[halo_blk - CONV_HALO:])
        xbuf[pl.ds(CONV_HALO, tm), :] = x_ref[...].astype(F32)
        for c0 in range(0, ch, CONV_COLS):
            cols = pl.ds(c0, CONV_COLS)
            w = [w_ref[pl.ds(j, 1), cols] for j in range(CONV_K)]
            dw = [jnp.zeros((1, CONV_COLS), F32) for _ in range(CONV_K)]
            for r0 in range(0, tm, CONV_ROWS):
                xs = [xbuf[pl.ds(r0 + CONV_HALO - (CONV_K - 1) + j, CONV_ROWS), cols] for j in range(CONV_K)]
                pre = xs[0] * w[0]
                for j in range(1, CONV_K):
                    pre = pre + xs[j] * w[j]
                sg = _sigmoid(pre)
                dpre = dy_ref[pl.ds(r0, CONV_ROWS), cols] * (sg * (1.0 + pre * (1.0 - sg)))
                dbuf[pl.ds(r0, CONV_ROWS), cols] = dpre
                dw = [dw[j] + jnp.sum(dpre * xs[j], axis=0, keepdims=True) for j in range(CONV_K)]
            for j in range(CONV_K):
                dw_ref[pl.ds(j, 1), cols] += dw[j]
            for r0 in range(0, tm, CONV_ROWS):
                dx = dbuf[pl.ds(r0 + CONV_K - 1, CONV_ROWS), cols] * w[0]
                for j in range(1, CONV_K):
                    dx = dx + dbuf[pl.ds(r0 + CONV_K - 1 - j, CONV_ROWS), cols] * w[j]
                dx_ref[pl.ds(r0, CONV_ROWS), cols] = dx.astype(dx_ref.dtype)
        dbuf[pl.ds(tm, CONV_HALO), :] = dbuf[pl.ds(0, CONV_HALO), :]

    rev = lambda i: (nt - 1 - i, 0)
    prev = lambda i: (jnp.maximum((nt - 1 - i) * hb - 1, 0), 0)
    return pl.pallas_call(
        body, grid=(nt,),
        in_specs=[pl.BlockSpec((tm, ch), rev), pl.BlockSpec((halo_blk, ch), prev), _full(conv_w8.shape),
                  pl.BlockSpec((tm, ch), rev)] + [pl.BlockSpec((tm, arr.shape[1]), rev) for _, arr in others],
        out_specs=[pl.BlockSpec((tm, N_INP), rev), _full(conv_w8.shape)],
        out_shape=[jax.ShapeDtypeStruct((t, N_INP), BF), jax.ShapeDtypeStruct(conv_w8.shape, F32)],
        scratch_shapes=[pltpu.VMEM((tm + CONV_HALO, ch), F32), pltpu.VMEM((tm + CONV_HALO, ch), F32)], name="conv_bwd",
        compiler_params=pltpu.CompilerParams(dimension_semantics=("arbitrary",)),
    )(proj, proj, conv_w8, dact, *[arr for _, arr in others])


BNN = (((2,), (1,)), ((0,), (0,)))
BNT = (((2,), (2,)), ((0,), (0,)))
BTN = (((1,), (1,)), ((0,), (0,)))


def _bdot(a, b, dims, precision=None):
    return lax.dot_general(a, b, dims, precision=precision, preferred_element_type=F32)


@jax.custom_vjp
def _bmmb_nt(a, b):
    return _bdot(a.astype(BF), b.astype(BF), BNT)


def _bmmb_nt_fwd(a, b):
    return _bmmb_nt(a, b), (a, b)


def _bmmb_nt_bwd(res, g):
    a, b = res
    gb = g.astype(BF)
    return _bdot(gb, b.astype(BF), BNN), _bdot(gb, a.astype(BF), BTN)


_bmmb_nt.defvjp(_bmmb_nt_fwd, _bmmb_nt_bwd)


@jax.custom_vjp
def _bmmb(a, b):
    return _bdot(a.astype(BF), b.astype(BF), BNN)


def _bmmb_fwd(a, b):
    return _bmmb(a, b), (a, b)


def _bmmb_bwd(res, g):
    a, b = res
    gb = g.astype(BF)
    return _bdot(gb, b.astype(BF), BNT), _bdot(a.astype(BF), gb, BTN)


_bmmb.defvjp(_bmmb_fwd, _bmmb_bwd)


@jax.custom_vjp
def _bmmb_tn(a, b):
    return _bdot(a.astype(BF), b.astype(BF), BTN)


def _bmmb_tn_fwd(a, b):
    return _bmmb_tn(a, b), (a, b)


def _bmmb_tn_bwd(res, g):
    a, b = res
    gb = g.astype(BF)
    return _bdot(b.astype(BF), gb, BNT), _bdot(a.astype(BF), gb, BNN)


_bmmb_tn.defvjp(_bmmb_tn_fwd, _bmmb_tn_bwd)


def _triangle_sums(g, lower):
    c = g.shape[1]
    ri = lax.broadcasted_iota(jnp.int32, (g.shape[0], c, c), 1)
    ci = lax.broadcasted_iota(jnp.int32, (g.shape[0], c, c), 2)
    tri = (ri >= ci if lower else ri <= ci).astype(BF)
    hi = g.astype(BF)
    mid = (g - hi.astype(F32)).astype(BF)
    lo = (g - hi.astype(F32) - mid.astype(F32)).astype(BF)
    return _bdot(tri, hi, BNN) + _bdot(tri, mid, BNN) + _bdot(tri, lo, BNN)


@jax.custom_vjp
def _chunk_cumsum(g):
    return _triangle_sums(g, True)


_chunk_cumsum.defvjp(lambda g: (_triangle_sums(g, True), None), lambda _, ct: (_triangle_sums(ct, False),))


def _unit_lower_solve_fwd(a, r):
    c = a.shape[-1]
    ri = lax.broadcasted_iota(jnp.int32, a.shape, 1)
    ci = lax.broadcasted_iota(jnp.int32, a.shape, 2)
    xm = -a
    inv = (ri == ci).astype(F32) + xm
    for _ in range(int(math.log2(c)) - 1):
        xm = _bdot(xm, xm, BNN, HI)
        inv = inv + _bdot(inv, xm, BNN, HI)
    x = _bdot(inv, r, BNN, HI)
    return x, (inv, x)


def _unit_lower_solve_bwd(res, g):
    inv, x = res
    dr = _bdot(inv, g, BTN, HI)
    return -_bdot(dr, x, BNT, HI), dr


@jax.custom_vjp
def _unit_lower_solve_given(a, r, inv):
    return _bdot(inv, r, BNN, HI)


def _unit_lower_solve_given_fwd(a, r, inv):
    x = _bdot(inv, r, BNN, HI)
    return x, (inv, x)


def _unit_lower_solve_given_bwd(res, g):
    da, dr = _unit_lower_solve_bwd(res, g)
    return da, dr, jnp.zeros_like(res[0])


_unit_lower_solve_given.defvjp(_unit_lower_solve_given_fwd, _unit_lower_solve_given_bwd)


def _gdn_intra(qkv, ba, al8, dt8, inv4=None):
    tm = qkv.shape[0]
    nb = tm // CHUNK
    bsz = DN_HEADS * nb

    def heads(x0):
        return jnp.concatenate([qkv[:, x0 + h * LANE:x0 + (h + 1) * LANE].reshape(nb, CHUNK, LANE) for h in range(DN_HEADS)], axis=0)

    def spread(c0):
        return jnp.concatenate([jnp.broadcast_to(ba[:, c0 + h:c0 + h + 1], (tm, LANE)).reshape(nb, CHUNK, LANE)
                                for h in range(DN_HEADS)], axis=0)

    def per_head(v8):
        return jnp.concatenate([jnp.broadcast_to(v8[0:1, h:h + 1].reshape(1, 1, 1), (nb, 1, LANE)) for h in range(DN_HEADS)], axis=0)

    ri = lax.broadcasted_iota(jnp.int32, (bsz, CHUNK, CHUNK), 1)
    ci = lax.broadcasted_iota(jnp.int32, (bsz, CHUNK, CHUNK), 2)
    incl = ri >= ci
    strict = ri > ci

    q = _l2norm(heads(0)) * (DN_DK ** -0.5)
    k = _l2norm(heads(DN_QK))
    va = heads(2 * DN_QK)
    beta = _sigmoid(spread(0))
    g = -jnp.exp(per_head(al8)) * _softplus(spread(DN_HEADS) + per_head(dt8))
    gc = _chunk_cumsum(g)
    g_last = jnp.sum(g, axis=1, keepdims=True)
    gcol = gc[:, :, :CHUNK]
    diff = gcol - jnp.swapaxes(gcol, 1, 2)
    decay = jnp.where(incl, jnp.exp(jnp.where(incl, diff, 0.0)), 0.0)
    kb = k * beta
    a_mat = jnp.where(strict, _bmmb_nt(kb, k) * decay, 0.0)
    egc = jnp.exp(gc)
    rhs = jnp.concatenate([kb * egc, va * beta], axis=2)
    if inv4 is None:
        wu, (inv, _) = _unit_lower_solve_fwd(a_mat, rhs)
    else:
        wu = _unit_lower_solve_given(a_mat, rhs, inv4.reshape(bsz, CHUNK, CHUNK))
    attn = jnp.where(incl, _bmmb_nt(q, k) * decay, 0.0)

    def unheads(x):
        return jnp.concatenate([x[h * nb:(h + 1) * nb].reshape(tm, LANE) for h in range(DN_HEADS)], axis=1)

    w_c, u_c = wu[:, :, :DN_DK], wu[:, :, DN_DK:]
    kd = k * jnp.exp(g_last - gc)
    out = (unheads(q * egc - _bmmb(attn, w_c)), unheads(_bmmb(attn, u_c)),
           _bmmb_tn(kd, w_c).reshape(DN_HEADS, nb, DN_DK, DN_DK), _bmmb_tn(kd, u_c).reshape(DN_HEADS, nb, DN_DK, DN_DV),
           jnp.broadcast_to(g_last, (bsz, GL_ROWS, LANE)).reshape(DN_HEADS, nb, GL_ROWS, LANE))
    return out if inv4 is not None else out + (inv.reshape(DN_HEADS, nb, CHUNK, CHUNK),)


def _gdn_scan_step(qp, op, c_mat, n_mat, gl, s):
    return _mmb(qp, s) + op, s * jnp.exp(gl) - _mmb(c_mat, s) + n_mat


def _gdn_intra_specs(t, tm, dts, order=lambda i: i):
    nb = tm // CHUNK
    row = pl.BlockSpec((tm, DN_VW), lambda i: (order(i), 0))
    mat = pl.BlockSpec((DN_HEADS, nb, DN_DK, DN_DV), lambda i: (0, order(i), 0, 0))
    row_shape = lambda d: jax.ShapeDtypeStruct((t, DN_VW), d)
    mat_shape = lambda d: jax.ShapeDtypeStruct((DN_HEADS, t // CHUNK, DN_DK, DN_DV), d)
    gl = pl.BlockSpec((DN_HEADS, nb, GL_ROWS, LANE), lambda i: (0, order(i), 0, 0))
    gl_shape = jax.ShapeDtypeStruct((DN_HEADS, t // CHUNK, GL_ROWS, LANE), dts[4])
    return [row, row, mat, mat, gl], [row_shape(dts[0]), row_shape(dts[1]), mat_shape(dts[2]), mat_shape(dts[3]), gl_shape]


def _gdn_intra_fwd(qkv, ba, al8, dt8, tm):
    t = qkv.shape[0]

    def body(qkv_ref, ba_ref, al_ref, dt_ref, *outs):
        for o, val in zip(outs, _gdn_intra(qkv_ref[...], ba_ref[...], al_ref[...], dt_ref[...])):
            o[...] = val.astype(o.dtype)

    specs, shapes = _gdn_intra_specs(t, tm, (BF, F32, BF, BF, F32))
    specs.append(_gdn_inverse_spec(tm))
    shapes.append(jax.ShapeDtypeStruct((DN_HEADS, t // CHUNK, CHUNK, CHUNK), F32))
    res = pl.pallas_call(
        body, grid=(t // tm,),
        in_specs=[pl.BlockSpec((tm, DN_CONV_CH), lambda i: (i, 0)), pl.BlockSpec((tm, LANE), lambda i: (i, 0)),
                  _full(al8.shape), _full(dt8.shape)],
        out_specs=specs, out_shape=shapes, name="gdn_intra_fwd",
        compiler_params=pltpu.CompilerParams(dimension_semantics=("parallel",)),
    )(qkv, ba, al8, dt8)
    return res[:5], res[5]


def _gdn_inverse_spec(tm):
    return pl.BlockSpec((DN_HEADS, tm // CHUNK, CHUNK, CHUNK), lambda i: (0, i, 0, 0))


def _gdn_intra_bwd(qkv, ba, al8, dt8, inverses, cts, tm):
    t = qkv.shape[0]

    def body(qkv_ref, ba_ref, al_ref, dt_ref, inv_ref, *refs):
        ct_refs, (dqkv_ref, dba_ref, dal_ref, ddt_ref) = refs[:5], refs[5:]

        @pl.when(pl.program_id(0) == 0)
        def _():
            dal_ref[...] = jnp.zeros_like(dal_ref)
            ddt_ref[...] = jnp.zeros_like(ddt_ref)

        _, vjp = jax.vjp(functools.partial(_gdn_intra, inv4=inv_ref[...]), qkv_ref[...], ba_ref[...], al_ref[...], dt_ref[...])
        dqkv, dba, dal, ddt = vjp(tuple(r[...].astype(F32) for r in ct_refs))
        dqkv_ref[...] = dqkv.astype(dqkv_ref.dtype)
        dba_ref[...] = dba.astype(dba_ref.dtype)
        dal_ref[...] += dal
        ddt_ref[...] += ddt

    specs, _ = _gdn_intra_specs(t, tm, (F32,) * 5)
    return pl.pallas_call(
        body, grid=(t // tm,),
        in_specs=[pl.BlockSpec((tm, DN_CONV_CH), lambda i: (i, 0)), pl.BlockSpec((tm, LANE), lambda i: (i, 0)),
                  _full(al8.shape), _full(dt8.shape), _gdn_inverse_spec(tm)] + specs,
        out_specs=[pl.BlockSpec((tm, DN_CONV_CH), lambda i: (i, 0)), pl.BlockSpec((tm, LANE), lambda i: (i, 0)),
                   _full(al8.shape), _full(dt8.shape)],
        out_shape=[jax.ShapeDtypeStruct((t, DN_CONV_CH), BF), jax.ShapeDtypeStruct((t, LANE), BF),
                   jax.ShapeDtypeStruct(al8.shape, F32), jax.ShapeDtypeStruct(dt8.shape, F32)],
        name="gdn_intra_bwd", compiler_params=pltpu.CompilerParams(dimension_semantics=("arbitrary",)),
    )(qkv, ba, al8, dt8, inverses, *cts)


def _gdn_scan_fwd(intra, tm):
    t = intra[0].shape[0]
    nb = tm // CHUNK
    nc = t // CHUNK

    def body(qp_ref, op_ref, c_ref, n_ref, gl_ref, o_ref, ss_ref, s_scr):
        @pl.when(pl.program_id(0) == 0)
        def _():
            s_scr[...] = jnp.zeros_like(s_scr)

        state = [s_scr[h] for h in range(DN_HEADS)]
        for cc in range(nb):
            rows = pl.ds(cc * CHUNK, CHUNK)
            for h in range(DN_HEADS):
                cols = pl.ds(h * DN_DV, DN_DV)
                ss_ref[cc, h] = state[h].astype(ss_ref.dtype)
                o_ref[rows, cols], state[h] = _gdn_scan_step(
                    qp_ref[rows, cols], op_ref[rows, cols], c_ref[h, cc], n_ref[h, cc], gl_ref[h, cc, pl.ds(0, 1), :], state[h])
        for h in range(DN_HEADS):
            s_scr[h] = state[h]

    specs, _ = _gdn_intra_specs(t, tm, (F32,) * 5)
    return pl.pallas_call(
        body, grid=(t // tm,), in_specs=specs,
        out_specs=[pl.BlockSpec((tm, DN_VW), lambda i: (i, 0)),
                   pl.BlockSpec((nb, DN_HEADS, DN_DK, DN_DV), lambda i: (i, 0, 0, 0))],
        out_shape=[jax.ShapeDtypeStruct((t, DN_VW), F32), jax.ShapeDtypeStruct((nc, DN_HEADS, DN_DK, DN_DV), BF)],
        scratch_shapes=[pltpu.VMEM((DN_HEADS, DN_DK, DN_DV), F32)], name="gdn_scan_fwd",
        compiler_params=pltpu.CompilerParams(dimension_semantics=("arbitrary",)),
    )(*intra)


def _gdn_scan_bwd(intra, states, do, tm):
    t = intra[0].shape[0]
    nb = tm // CHUNK
    ng = t // tm

    def body(qp_ref, op_ref, c_ref, n_ref, gl_ref, ss_ref, do_ref, dqp_ref, dop_ref, dc_ref, dn_ref, dgl_ref, ds_scr):
        @pl.when(pl.program_id(0) == 0)
        def _():
            ds_scr[...] = jnp.zeros_like(ds_scr)

        d_state = [ds_scr[h] for h in range(DN_HEADS)]
        for cc in reversed(range(nb)):
            rows = pl.ds(cc * CHUNK, CHUNK)
            for h in range(DN_HEADS):
                cols = pl.ds(h * DN_DV, DN_DV)
                _, vjp = jax.vjp(_gdn_scan_step, qp_ref[rows, cols].astype(F32), op_ref[rows, cols], c_ref[h, cc].astype(F32),
                                 n_ref[h, cc].astype(F32), gl_ref[h, cc, pl.ds(0, 1), :], ss_ref[cc, h].astype(F32))
                dqp_ref[rows, cols], dop_ref[rows, cols], dc, dn, dgl, d_state[h] = vjp((do_ref[rows, cols], d_state[h]))
                dc_ref[h, cc] = dc.astype(dc_ref.dtype)
                dn_ref[h, cc] = dn.astype(dn_ref.dtype)
                first_row = lax.broadcasted_iota(jnp.int32, (GL_ROWS, LANE), 0) == 0
                dgl_ref[h, cc] = jnp.where(first_row, dgl, 0.0)
        for h in range(DN_HEADS):
            ds_scr[h] = d_state[h]

    five, shapes = _gdn_intra_specs(t, tm, (F32, F32, BF, BF, F32), order=lambda i: ng - 1 - i)
    row = five[0]
    return pl.pallas_call(
        body, grid=(ng,),
        in_specs=five + [pl.BlockSpec((nb, DN_HEADS, DN_DK, DN_DV), lambda i: (ng - 1 - i, 0, 0, 0)), row],
        out_specs=five, out_shape=shapes,
        scratch_shapes=[pltpu.VMEM((DN_HEADS, DN_DK, DN_DV), F32)], name="gdn_scan_bwd",
        compiler_params=pltpu.CompilerParams(dimension_semantics=("arbitrary",)),
    )(*intra, states, do)


def _gdn_out(o, z, g):
    parts = []
    for h in range(DN_HEADS):
        sl = slice(h * DN_DV, (h + 1) * DN_DV)
        parts.append(_rmsnorm(o[:, sl], g) * _silu(z[:, sl]))
    return parts


_Q_SCALE = math.log2(math.e) / math.sqrt(QK_NOPE + QK_ROPE)


def _rope_tables(pos, inv_freq2):
    lane = lax.broadcasted_iota(jnp.int32, (1, LANE), 1)
    ang = pos * inv_freq2
    cos = jnp.where(lane < QK_ROPE, jnp.cos(ang), 0.0)
    sin = jnp.where(lane < QK_ROPE // 2, -jnp.sin(ang), jnp.where(lane < QK_ROPE, jnp.sin(ang), 0.0))
    return cos, sin


@jax.custom_vjp
def _rope_swap(u):
    lane = lax.broadcasted_iota(jnp.int32, u.shape, 1)
    half = QK_ROPE // 2
    return jnp.where(lane < half, pltpu.roll(u, LANE - half, 1), jnp.where(lane < QK_ROPE, pltpu.roll(u, half, 1), 0.0))


_rope_swap.defvjp(lambda u: (_rope_swap(u), None), lambda _, g: (_rope_swap(g),))


def _mla_prep(cq, ckv, kr, gq, gkv, w_uq, w_ukv, cos, sin):
    rope = lambda u: u * cos + _rope_swap(u) * sin
    q_lin = _mmb_nt(_rmsnorm(cq, gq), w_uq) * _Q_SCALE
    kv_lin = _mmb_nt(_rmsnorm(ckv, gkv), w_ukv)
    k_rope = rope(kr)
    qs, ks, vs = [], [], []
    for h in range(MLA_HEADS):
        qs += [q_lin[:, h * LANE:(h + 1) * LANE], rope(q_lin[:, (MLA_HEADS + h) * LANE:(MLA_HEADS + h + 1) * LANE])]
        ks += [kv_lin[:, 2 * h * LANE:(2 * h + 1) * LANE], k_rope]
        vs += [kv_lin[:, (2 * h + 1) * LANE:(2 * h + 2) * LANE]]
    return qs + ks + vs


def _mla_prep_fwd(proj, pos_col, inv_freq2, gq, gkv, w_uq, w_ukv, tm):
    t = proj.shape[0]
    nq = 2 * MLA_HEADS

    def body(cq_ref, ckv_ref, kr_ref, pos_ref, f_ref, gq_ref, gkv_ref, wq_ref, wkv_ref, q_ref, k_ref, v_ref):
        cos, sin = _rope_tables(pos_ref[...], f_ref[...])
        outs = _mla_prep(cq_ref[...].astype(F32), ckv_ref[...].astype(F32), kr_ref[...].astype(F32), gq_ref[...], gkv_ref[...], wq_ref[...], wkv_ref[...],
                         cos, sin)
        for i in range(nq):
            q_ref[:, pl.ds(i * LANE, LANE)] = outs[i].astype(q_ref.dtype)
            k_ref[:, pl.ds(i * LANE, LANE)] = outs[nq + i].astype(k_ref.dtype)
        for h in range(MLA_HEADS):
            v_ref[:, pl.ds(h * LANE, LANE)] = outs[2 * nq + h].astype(v_ref.dtype)

    row = lambda w, j: pl.BlockSpec((tm, w), functools.partial(lambda i, j: (i, j), j=j))
    return pl.pallas_call(
        body, grid=(t // tm,),
        in_specs=[row(Q_LORA, P_CQ // Q_LORA), row(KV_LORA, P_CKV // KV_LORA), row(LANE, P_KR // LANE),
                  pl.BlockSpec((tm, 1), lambda i: (i, 0)), _full(inv_freq2.shape), _full(gq.shape), _full(gkv.shape),
                  _full(w_uq.shape), _full(w_ukv.shape)],
        out_specs=[row(nq * LANE, 0), row(nq * LANE, 0), row(MLA_VW, 0)],
        out_shape=[jax.ShapeDtypeStruct((t, nq * LANE), BF), jax.ShapeDtypeStruct((t, nq * LANE), BF),
                   jax.ShapeDtypeStruct((t, MLA_VW), BF)],
        name="mla_prep_fwd", compiler_params=pltpu.CompilerParams(dimension_semantics=("arbitrary",)),
    )(proj, proj, proj, pos_col, inv_freq2, gq, gkv, w_uq, w_ukv)


def _mla_prep_bwd(proj, pos_col, inv_freq2, gq, gkv, w_uq, w_ukv, dq, dk, dv, tm):
    t = proj.shape[0]
    nq = 2 * MLA_HEADS

    def body(cq_ref, ckv_ref, kr_ref, pos_ref, f_ref, gq_ref, gkv_ref, wq_ref, wkv_ref, dq_ref, dk_ref, dv_ref,
             dcq_ref, dckv_ref, dkr_ref, dgq_ref, dgkv_ref, dwq_ref, dwkv_ref):
        @pl.when(pl.program_id(0) == 0)
        def _():
            for o in (dgq_ref, dgkv_ref, dwq_ref, dwkv_ref):
                o[...] = jnp.zeros_like(o)

        cos, sin = _rope_tables(pos_ref[...], f_ref[...])
        f = functools.partial(_mla_prep, cos=cos, sin=sin)
        _, vjp = jax.vjp(f, cq_ref[...].astype(F32), ckv_ref[...].astype(F32), kr_ref[...].astype(F32), gq_ref[...], gkv_ref[...], wq_ref[...], wkv_ref[...])
        cts = [dq_ref[:, pl.ds(i * LANE, LANE)] for i in range(nq)]
        cts += [dk_ref[:, pl.ds(i * LANE, LANE)] for i in range(nq)]
        cts += [dv_ref[:, pl.ds(h * LANE, LANE)] for h in range(MLA_HEADS)]
        dcq, dckv, dkr, dgq, dgkv, dwq, dwkv = vjp(cts)
        dcq_ref[...] = dcq.astype(dcq_ref.dtype)
        dckv_ref[...] = dckv.astype(dckv_ref.dtype)
        dkr_ref[...] = dkr.astype(dkr_ref.dtype)
        dgq_ref[...] += dgq
        dgkv_ref[...] += dgkv
        dwq_ref[...] += dwq
        dwkv_ref[...] += dwkv

    row = lambda w, j: pl.BlockSpec((tm, w), functools.partial(lambda i, j: (i, j), j=j))
    return pl.pallas_call(
        body, grid=(t // tm,),
        in_specs=[row(Q_LORA, P_CQ // Q_LORA), row(KV_LORA, P_CKV // KV_LORA), row(LANE, P_KR // LANE),
                  pl.BlockSpec((tm, 1), lambda i: (i, 0)), _full(inv_freq2.shape), _full(gq.shape), _full(gkv.shape),
                  _full(w_uq.shape), _full(w_ukv.shape), row(nq * LANE, 0), row(nq * LANE, 0), row(MLA_VW, 0)],
        out_specs=[row(Q_LORA, 0), row(KV_LORA, 0), row(LANE, 0), _full(gq.shape), _full(gkv.shape),
                   _full(w_uq.shape), _full(w_ukv.shape)],
        out_shape=[jax.ShapeDtypeStruct((t, Q_LORA), BF), jax.ShapeDtypeStruct((t, KV_LORA), BF),
                   jax.ShapeDtypeStruct((t, LANE), BF), jax.ShapeDtypeStruct(gq.shape, F32),
                   jax.ShapeDtypeStruct(gkv.shape, F32), jax.ShapeDtypeStruct(w_uq.shape, F32),
                   jax.ShapeDtypeStruct(w_ukv.shape, F32)],
        name="mla_prep_bwd", compiler_params=pltpu.CompilerParams(dimension_semantics=("arbitrary",)),
    )(proj, proj, proj, pos_col, inv_freq2, gq, gkv, w_uq, w_ukv, dq, dk, dv)


_NEG = -1e30
_LN2 = math.log(2.0)
ATT_CHAINS = 2


def _causal(tq, tk, q0, k0):
    row = q0 + lax.broadcasted_iota(jnp.int32, (tq, tk), 0)
    col = k0 + lax.broadcasted_iota(jnp.int32, (tq, tk), 1)
    return col <= row


def _attn_fwd(q, k, v, tq, tk):
    t = q.shape[0]

    assert tk % tq == 0 or tq % tk == 0
    n_diag = max(1, tq // tk)

    th = tq // ATT_CHAINS

    def body(q_ref, k_ref, v_ref, o_ref, lse_ref):
        i = pl.program_id(1)
        n_full = (i * tq) // tk

        def step(k0, carry, masked):
            out = []
            for c, (m, l, acc) in enumerate(carry):
                kw = min(tk, (c + 1) * th) if masked and tk == tq else tk
                kt = k_ref[pl.ds(k0, kw), :]
                vt = v_ref[pl.ds(k0, kw), :]
                s = _dot(q_ref[pl.ds(c * th, th), :], kt, NT)
                if masked:
                    s = jnp.where(_causal(th, kw, i * tq + c * th, k0), s, _NEG)
                m_new = jnp.maximum(m, jnp.max(s, axis=-1, keepdims=True))
                p = jnp.exp2(s - m_new)
                alpha = jnp.exp2(m - m_new)
                out.append((m_new, alpha * l + jnp.sum(p, axis=-1, keepdims=True), alpha * acc + _dot(p.astype(BF), vt)))
            return tuple(out)

        init = tuple((jnp.full((th, 1), _NEG, F32), jnp.zeros((th, 1), F32), jnp.zeros((th, V_HEAD), F32)) for _ in range(ATT_CHAINS))
        carry = lax.fori_loop(0, n_full, lambda j, c: step(pl.multiple_of(j * tk, tk), c, False), init)
        for dd in range(n_diag):
            carry = step(pl.multiple_of((n_full + dd) * tk, tk), carry, True)
        for c, (m, l, acc) in enumerate(carry):
            o_ref[pl.ds(c * th, th), :] = acc / l
            lse_ref[pl.ds(c * th, th), :] = jnp.broadcast_to(m + jnp.log2(l), (th, LANE))

    return pl.pallas_call(
        body, grid=(MLA_HEADS, t // tq),
        in_specs=[pl.BlockSpec((tq, 2 * LANE), lambda h, i: (i, h)), pl.BlockSpec((t, 2 * LANE), lambda h, i: (0, h)),
                  pl.BlockSpec((t, V_HEAD), lambda h, i: (0, h))],
        out_specs=[pl.BlockSpec((tq, V_HEAD), lambda h, i: (i, h)), pl.BlockSpec((tq, LANE), lambda h, i: (i, h))],
        out_shape=[jax.ShapeDtypeStruct((t, MLA_VW), F32), jax.ShapeDtypeStruct((t, MLA_HEADS * LANE), F32)],
        name="attn_fwd", compiler_params=pltpu.CompilerParams(dimension_semantics=("parallel", "arbitrary")),
    )(q, k, v)


def _attn_bwd(q, k, v, do, lse, delta, tq, tk):
    t = q.shape[0]
    nkt = t // tk
    assert tk % tq == 0

    def body(q_ref, k_ref, v_ref, do_ref, lse_ref, dl_ref, dq_ref, dk_ref, dv_ref):
        j = pl.program_id(1)

        @pl.when(j == 0)
        def _():
            dq_ref[...] = jnp.zeros_like(dq_ref)

        kt = k_ref[...]
        vt = v_ref[...]

        def step(q0, carry, masked, kw=tk):
            dk, dv = carry
            rows = pl.ds(q0, tq)
            qt = q_ref[rows, :]
            dot_ = do_ref[rows, :]
            ktw, vtw = kt[:kw], vt[:kw]
            p = jnp.exp2(_dot(qt, ktw, NT) - lse_ref[rows, pl.ds(0, 1)])
            if masked:
                p = jnp.where(_causal(tq, kw, q0, j * tk), p, 0.0)
            dv_w = _dot(p.astype(BF), dot_, TN)
            ds = (p * (_dot(dot_, vtw, NT) - dl_ref[rows, pl.ds(0, 1)])).astype(BF)
            dk_w = _dot(ds, qt, TN)
            dq_ref[rows, :] += _dot(ds, ktw)
            if kw == tk:
                return dk + dk_w, dv + dv_w
            return (jnp.concatenate([dk[:kw] + dk_w, dk[kw:]], axis=0), jnp.concatenate([dv[:kw] + dv_w, dv[kw:]], axis=0))

        per = tk // tq
        carry = (jnp.zeros((tk, 2 * LANE), F32), jnp.zeros((tk, V_HEAD), F32))
        for dd in range(per):
            carry = step(pl.multiple_of(j * tk + dd * tq, tq), carry, True, kw=(dd + 1) * tq)

        def group(g, c):
            for dd in range(per):
                c = step(pl.multiple_of(g * tk + dd * tq, tq), c, False)
            return c

        dk, dv = lax.fori_loop(j + 1, nkt, group, carry)
        dk_ref[...] = dk * _LN2
        dv_ref[...] = dv

        @pl.when(j == nkt - 1)
        def _():
            dq_ref[...] = dq_ref[...] * _LN2

    return pl.pallas_call(
        body, grid=(MLA_HEADS, nkt),
        in_specs=[pl.BlockSpec((t, 2 * LANE), lambda h, j: (0, h)), pl.BlockSpec((tk, 2 * LANE), lambda h, j: (j, h)),
                  pl.BlockSpec((tk, V_HEAD), lambda h, j: (j, h)), pl.BlockSpec((t, V_HEAD), lambda h, j: (0, h)),
                  pl.BlockSpec((t, LANE), lambda h, j: (0, h)), pl.BlockSpec((t, LANE), lambda h, j: (0, h))],
        out_specs=[pl.BlockSpec((t, 2 * LANE), lambda h, j: (0, h)), pl.BlockSpec((tk, 2 * LANE), lambda h, j: (j, h)),
                   pl.BlockSpec((tk, V_HEAD), lambda h, j: (j, h))],
        out_shape=[jax.ShapeDtypeStruct((t, MLA_HEADS * 2 * LANE), F32), jax.ShapeDtypeStruct((t, MLA_HEADS * 2 * LANE), F32),
                   jax.ShapeDtypeStruct((t, MLA_VW), F32)],
        name="attn_bwd", compiler_params=pltpu.CompilerParams(dimension_semantics=("parallel", "arbitrary")),
    )(q, k, v, do, lse, delta)


def _adam_update(w, g, m, v):
    mm = ADAM_B1 * m + (1.0 - ADAM_B1) * g
    vv = ADAM_B2 * v + (1.0 - ADAM_B2) * jnp.square(g)
    m_hat = mm / (1.0 - ADAM_B1 ** ADAM_STEP)
    v_hat = vv / (1.0 - ADAM_B2 ** ADAM_STEP)
    return -ADAM_LR * (m_hat / (jnp.sqrt(v_hat) + ADAM_EPS) + ADAM_WD * w), mm, vv


def _adamw(w, g, m, v, name):
    r, c = w.shape
    tr = max([r // s for s in range(1, r // 8 + 1) if r % s == 0 and (r // s) % 8 == 0 and r // s <= 512] or [r])
    slots = g.ndim == 3

    def body(w_ref, g_ref, m_ref, v_ref, g_out, d_ref, nm_ref, nv_ref):
        if slots:
            gg = g_ref[0].astype(F32)
            for s in range(1, N_DEV):
                gg = gg + g_ref[s].astype(F32)
        else:
            gg = g_ref[...]
        g_out[...] = gg
        d_ref[...], nm_ref[...], nv_ref[...] = _adam_update(w_ref[...], gg, m_ref[...], v_ref[...])

    spec = pl.BlockSpec((tr, c), lambda i: (i, 0))
    g_spec = pl.BlockSpec((N_DEV, tr, c), lambda i: (0, i, 0)) if slots else spec
    return pl.pallas_call(
        body, grid=(r // tr,), in_specs=[spec, g_spec, spec, spec], out_specs=[spec] * 4,
        out_shape=[jax.ShapeDtypeStruct((r, c), F32)] * 4, name=name,
        compiler_params=pltpu.CompilerParams(dimension_semantics=("arbitrary",)),
    )(w, g, m, v)


def _adamw_many(ws, gs, ms, vs, name):
    n = len(ws)

    def body(*refs):
        for i in range(n):
            w_ref, g_ref, m_ref, v_ref = (refs[j * n + i] for j in range(4))
            d_ref, nm_ref, nv_ref = (refs[(4 + j) * n + i] for j in range(3))
            d_ref[...], nm_ref[...], nv_ref[...] = _adam_update(w_ref[...], g_ref[...], m_ref[...], v_ref[...])

    shapes = [jax.ShapeDtypeStruct(w.shape, F32) for w in ws]
    outs = pl.pallas_call(body, out_shape=shapes * 3, name=name)(*ws, *gs, *ms, *vs)
    return outs[:n], outs[n:2 * n], outs[2 * n:]


def _cast_bf16(xs, name, after=None):
    n = len(xs)
    extra = [] if after is None else [after]

    def body(*refs):
        outs = refs[n + len(extra):]
        for i in range(n):
            outs[i][...] = refs[i][...].astype(BF)

    vmem = pl.BlockSpec(memory_space=pltpu.VMEM)
    return pl.pallas_call(
        body, out_shape=[jax.ShapeDtypeStruct(x.shape, BF) for x in xs], name=name,
        in_specs=[vmem] * n + [pl.BlockSpec(memory_space=pl.ANY)] * len(extra), out_specs=[vmem] * n)(*xs, *extra)


def _pad_rows(a, n):
    return jnp.pad(a, ((0, n - a.shape[0]), (0, 0)))


def _w_in_to_padded(wt):
    s_ba = P_CQ
    s_cq = s_ba + 2 * DN_HEADS
    s_kr = s_cq + Q_LORA + KV_LORA
    return jnp.concatenate([wt[:s_ba], wt[s_cq:s_kr], _pad_rows(wt[s_ba:s_cq], LANE), _pad_rows(wt[s_kr:], LANE)], axis=0)


def _w_in_from_padded(wt):
    return jnp.concatenate([wt[:P_CQ], wt[P_BA:P_BA + 2 * DN_HEADS], wt[P_CQ:P_BA], wt[P_KR:P_KR + QK_ROPE]], axis=0)


def _w_uq_to_padded(wt):
    w3 = wt.reshape(MLA_HEADS, QK_NOPE + QK_ROPE, Q_LORA)
    nope = w3[:, :QK_NOPE].reshape(MLA_HEADS * QK_NOPE, Q_LORA)
    rope = jnp.pad(w3[:, QK_NOPE:], ((0, 0), (0, LANE - QK_ROPE), (0, 0))).reshape(MLA_HEADS * LANE, Q_LORA)
    return jnp.concatenate([nope, rope], axis=0)


def _w_uq_from_padded(wt):
    nope = wt[:MLA_HEADS * QK_NOPE].reshape(MLA_HEADS, QK_NOPE, Q_LORA)
    rope = wt[MLA_HEADS * QK_NOPE:].reshape(MLA_HEADS, LANE, Q_LORA)[:, :QK_ROPE]
    return jnp.concatenate([nope, rope], axis=1).reshape(MLA_HEADS * (QK_NOPE + QK_ROPE), Q_LORA)


def _pack(pieces, width, row_mult):
    flat = jnp.concatenate([p.reshape(-1) for p in pieces])
    n = flat.shape[0]
    rows = -(-n // (width * row_mult)) * row_mult
    return jnp.pad(flat, (0, rows * width - n)).reshape(rows, width)


def _unpack(flat, shapes):
    out, o = [], 0
    for s in shapes:
        n = math.prod(s)
        out.append(flat[o:o + n].reshape(s))
        o += n
    return out


def kernel(x, c, positions, w_ada, b_ada, w_in, conv_w, a_log, dt_bias, dn_norm_g, q_norm_g, w_uq, kv_norm_g, w_ukv, w_o, ln1_g, ln1_b, w_gate, w_up, w_down, ln2_g, ln2_b, loss_target, m_w_ada, m_b_ada, m_w_in, m_conv_w, m_a_log, m_dt_bias, m_dn_norm_g, m_q_norm_g, m_w_uq, m_kv_norm_g, m_w_ukv, m_w_o, m_ln1_g, m_ln1_b, m_w_gate, m_w_up, m_w_down, m_ln2_g, m_ln2_b, v_w_ada, v_b_ada, v_w_in, v_conv_w, v_a_log, v_dt_bias, v_dn_norm_g, v_q_norm_g, v_w_uq, v_kv_norm_g, v_w_ukv, v_w_o, v_ln1_g, v_ln1_b, v_w_gate, v_w_up, v_w_down, v_ln2_g, v_ln2_b):
    me = 4 * lax.axis_index("x") + 2 * lax.axis_index("y") + lax.axis_index("c")
    t, d = x.shape[1], x.shape[2]
    ada_n = w_ada.shape[2]

    tr = lambda w: w[0].T
    rows = lambda a: a.reshape(-1, a.shape[2])
    (in_shard,) = _cast_bf16([tr(w_in)], "cast_w_in")
    cw = conv_w.shape[3]
    a_in, c_all, conv_all = _gather_by_chip([in_shard, c, conv_w[0, :, 0, :]], "gather_w_in_and_small")
    c_all = c_all.reshape(N_DEV, d)
    conv_full = conv_all.transpose(1, 0, 2).reshape(CONV_K, N_DEV * cw)
    conv_w8 = jnp.pad(conv_full, ((0, 8 - CONV_K), (0, 0)))

    b_ada_mine = lax.dynamic_slice(b_ada, (0, me * ada_n), (1, ada_n))
    mod_cols = _mod_fwd(c_all, w_ada[0], b_ada_mine)
    (mod_all,) = _exchange([mod_cols.reshape(N_DEV, 1, ada_n)], "scatter_mod", scatter=True)
    mod = mod_all.reshape(1, N_DEV * ada_n)

    later = _cast_bf16([tr(w_uq), tr(w_ukv), w_o[0], tr(w_gate), tr(w_up), w_down[0]], "cast_weights", after=mod)
    mixer_gather, token_a = _exchange_start(later[:3], "gather_mixer_weights_start", scatter=False)
    ffn_gather, token_b = _exchange_start(later[3:], "gather_ffn_weights_start", scatter=False)
    mod = mod + (token_a + token_b)
    w_in_t = _w_in_to_padded(rows(a_in))

    def mixer_weights(after):
        a_uq, a_ukv, a_o = _exchange_wait(mixer_gather, after, "gather_mixer_weights_wait", scatter=False)
        return _w_uq_to_padded(rows(a_uq)), rows(a_ukv), rows(a_o)

    def ffn_weights(after):
        a_gate, a_up, a_down = _exchange_wait(ffn_gather, after, "gather_ffn_weights_wait", scatter=False)
        return rows(a_gate), rows(a_up), rows(a_down)

    def by_dest(g):
        return g.reshape(N_DEV, -1, g.shape[1])

    scatters = {}

    def grads_ready(tag, *g):
        if tag == "ffn":
            pieces = [by_dest(a) for a in g]
        elif tag == "mixer":
            g_w_o, g_w_uq_t, g_w_ukv_t = g
            pieces = [by_dest(g_w_o), by_dest(_w_uq_from_padded(g_w_uq_t).astype(BF)), by_dest(g_w_ukv_t.astype(BF))]
        else:
            pieces = [by_dest(_w_in_from_padded(g[0]))]
        scatters[tag], token = _exchange_start(pieces, "scatter_%s_grads_start" % tag, scatter=True)
        return token

    loc = _local_step(x[0], loss_target[0], positions[0], mod, w_in_t, mixer_weights, ffn_weights, grads_ready,
                      conv_w8, a_log, dt_bias, dn_norm_g, q_norm_g, kv_norm_g, ln1_g, ln1_b, ln2_g, ln2_b)
    grad_x, loss_acc, dmod, d_conv8, d_al8, d_dt8, d_dn_g, d_q_g, d_kv_g, d_ln1_g, d_ln1_b, d_ln2_g, d_ln2_b = loc

    small_shapes = [(6 * d,), (CONV_K, N_DEV * cw), (DN_HEADS,), (DN_HEADS,), (DN_DV,), (Q_LORA,), (KV_LORA,), (d,), (d,), (d,), (d,), (1,)]
    gsmall = _pack([dmod, d_conv8[:CONV_K], d_al8[0, :DN_HEADS], d_dt8[0, :DN_HEADS], d_dn_g, d_q_g, d_kv_g,
                    d_ln1_g, d_ln1_b, d_ln2_g, d_ln2_b, loss_acc[0, :1]], LANE, 8)
    small_gather, _ = _exchange_start([gsmall], "gather_small_grads_start", scatter=False)

    def small_sums(after):
        (gsmall_all,) = _exchange_wait(small_gather, after, "gather_small_grads_wait", scatter=False)
        dmod_all = gsmall_all.reshape(N_DEV, -1)[:, :6 * d]
        tot = _unpack(_sum_slots(gsmall_all, "sum_small_grads").reshape(-1), small_shapes)
        g_b_ada, g_conv_full, g_a_log, g_dt_bias, g_dn_g, g_q_g, g_kv_g, g_ln1_g, g_ln1_b, g_ln2_g, g_ln2_b, loss1 = tot
        g_conv_w = lax.dynamic_slice(g_conv_full, (0, me * cw), (CONV_K, cw))
        g_w_ada = _mod_bwd(c_all.T, lax.dynamic_slice(dmod_all, (0, me * ada_n), (N_DEV, ada_n)))
        return loss1.reshape(()), {
            "w_ada": g_w_ada[None], "b_ada": g_b_ada[None], "conv_w": g_conv_w[None, :, None, :],
            "a_log": g_a_log[None], "dt_bias": g_dt_bias[None], "dn_norm_g": g_dn_g[None], "q_norm_g": g_q_g[None],
            "kv_norm_g": g_kv_g[None], "ln1_g": g_ln1_g[None], "ln1_b": g_ln1_b[None], "ln2_g": g_ln2_g[None], "ln2_b": g_ln2_b[None]}

    grads = {}
    weights = dict(w_ada=w_ada, b_ada=b_ada, w_in=w_in, conv_w=conv_w, a_log=a_log, dt_bias=dt_bias, dn_norm_g=dn_norm_g,
                   q_norm_g=q_norm_g, w_uq=w_uq, kv_norm_g=kv_norm_g, w_ukv=w_ukv, w_o=w_o, ln1_g=ln1_g, ln1_b=ln1_b,
                   w_gate=w_gate, w_up=w_up, w_down=w_down, ln2_g=ln2_g, ln2_b=ln2_b)
    ms = dict(w_ada=m_w_ada, b_ada=m_b_ada, w_in=m_w_in, conv_w=m_conv_w, a_log=m_a_log, dt_bias=m_dt_bias,
              dn_norm_g=m_dn_norm_g, q_norm_g=m_q_norm_g, w_uq=m_w_uq, kv_norm_g=m_kv_norm_g, w_ukv=m_w_ukv, w_o=m_w_o,
              ln1_g=m_ln1_g, ln1_b=m_ln1_b, w_gate=m_w_gate, w_up=m_w_up, w_down=m_w_down, ln2_g=m_ln2_g, ln2_b=m_ln2_b)
    vs = dict(w_ada=v_w_ada, b_ada=v_b_ada, w_in=v_w_in, conv_w=v_conv_w, a_log=v_a_log, dt_bias=v_dt_bias,
              dn_norm_g=v_dn_norm_g, q_norm_g=v_q_norm_g, w_uq=v_w_uq, kv_norm_g=v_kv_norm_g, w_ukv=v_w_ukv, w_o=v_w_o,
              ln1_g=v_ln1_g, ln1_b=v_ln1_b, w_gate=v_w_gate, w_up=v_w_up, w_down=v_w_down, ln2_g=v_ln2_g, ln2_b=v_ln2_b)
    names = list(weights)
    big = ("w_gate", "w_up", "w_down", "w_o", "w_uq", "w_ukv", "w_ada", "w_in")
    waits = {"w_gate": ("ffn", ("w_gate", "w_up", "w_down")), "w_o": ("mixer", ("w_o", "w_uq", "w_ukv")), "w_in": ("in", ("w_in",))}
    delta_w, new_m, new_v, slots = {}, {}, {}, {}
    last = small_gather[2][0]
    for n in big:
        if n == "w_ada":
            loss, small_grads = small_sums(last)
            grads.update(small_grads)
        if n == "w_in":
            rest = [r for r in names if r not in big]
            flat2 = lambda a: a.reshape(-1, a.shape[-1])
            outs = _adamw_many(*[[flat2(src[r]) for r in rest] for src in (weights, grads, ms, vs)], "adamw_small")
            for dst, o in zip((delta_w, new_m, new_v), outs):
                for r, a in zip(rest, o):
                    dst[r] = a.reshape(weights[r].shape)
            last = outs[0][0]
        transposed = n in ("w_in", "w_uq", "w_ukv", "w_gate", "w_up")
        two = (lambda a: a[0].T) if transposed else (lambda a: a[0])
        back = (lambda a: a.T[None]) if transposed else (lambda a: a[None])
        if n in waits:
            tag, members = waits[n]
            slots.update(zip(members, _exchange_wait(scatters[tag], last, "scatter_%s_grads_wait" % tag, scatter=True)))
        g_in = slots[n] if n in slots else two(grads[n])
        gr, dlt, nm, nv = _adamw(two(weights[n]), g_in, two(ms[n]), two(vs[n]), "adamw_" + n)
        grads[n], delta_w[n], new_m[n], new_v[n] = back(gr), back(dlt), back(nm), back(nv)
        last = nv

    return (loss, grad_x[None], *[grads[n] for n in names], *[delta_w[n] for n in names],
            *[new_m[n] for n in names], *[new_v[n] for n in names])


def _local_step(xs, tgt, pos, mod, w_in_t, mixer_weights, ffn_weights, grads_ready, conv_w8,
                a_log, dt_bias, dn_norm_g, q_norm_g, kv_norm_g, ln1_g, ln1_b, ln2_g, ln2_b):
    t, d = xs.shape
    sh_m, sc_m, gt_m, sh_f, sc_f, gt_f = [mod[:, i * d:(i + 1) * d] for i in range(6)]
    pos_col = pos.astype(F32).reshape(t, 1)
    inv_freq = 1.0 / (ROPE_THETA ** (jnp.arange(0, QK_ROPE, 2, dtype=F32) / QK_ROPE))
    inv_freq2 = jnp.pad(jnp.concatenate([inv_freq, inv_freq]), (0, LANE - QK_ROPE)).reshape(1, LANE)
    al8 = jnp.pad(a_log, ((0, 7), (0, LANE - DN_HEADS)))
    dt8 = jnp.pad(dt_bias, ((0, 7), (0, LANE - DN_HEADS)))

    tm = min(512, t)
    tq = min(256, t)
    tk = min(512, t)

    def modulate_in(xx, sc, sh, w_ba):
        h = (xx * (1.0 + sc) + sh).astype(BF)
        return h, _dot(h, w_ba, NT)

    h1, ba_raw = _rowwise("modulate_in", modulate_in, [xs], [sc_m, sh_m, w_in_t[P_BA:P_BA + LANE]], [(d, BF), (LANE, F32)], [], tm)
    proj = _matmul(h1, w_in_t, "nt", "in_proj", BF)
    qkv = _conv_fwd(proj, conv_w8, min(256, t))
    gdn_tm = min(512, t)
    intra, inverses = _gdn_intra_fwd(qkv, ba_raw, al8, dt8, gdn_tm)
    o_dn, states = _gdn_scan_fwd(intra, gdn_tm)
    w_uq_t, w_ukv_t, w_o_f = mixer_weights(states)
    qc, kc, vc = _mla_prep_fwd(proj, pos_col, inv_freq2, q_norm_g, kv_norm_g, w_uq_t, w_ukv_t, tm)
    o_mla, lse = _attn_fwd(qc, kc, vc, min(1024, t), min(1024, t))

    def mix_in(o, z, om, g):
        return jnp.concatenate(_gdn_out(o, z.astype(F32), g) + [om], axis=1)

    (mixin,) = _rowwise("mixer_out", mix_in, [o_dn, (proj, DN_VW, P_Z // DN_VW), o_mla], [dn_norm_g], [(2 * DN_VW, BF)], [], tm)
    mix = _matmul(mixin, w_o_f, "nn", "out_proj", BF)

    def block1(xx, mx, gt, g1, b1, sc, sh):
        x1 = _layernorm(DEEPNORM_ALPHA * xx + gt * mx, g1, b1)
        return x1, x1 * (1.0 + sc) + sh

    x1, h2 = _rowwise("norm1_modulate", block1, [xs, mix], [gt_m, ln1_g, ln1_b, sc_f, sh_f], [(d, F32), (d, BF)], [], tm)
    w_gate_f, w_up_f, w_down_f = ffn_weights(h2)
    act, act_dg, act_du = _ffn_in(h2, w_gate_f, w_up_f)
    ff = _matmul(act, w_down_f, "nn", "ffn_out", BF)

    def tail_loss(x1_, ff_, gt, g2, b2, tg):
        y = _layernorm(DEEPNORM_ALPHA * x1_ + gt * ff_, g2, b2)
        return 0.5 * jnp.sum(jnp.mean(jnp.square(y - tg), axis=-1))

    def tail(x1_, ff_, tg, gt, g2, b2):
        loss, (dx1, dff, dgt, dg2, db2) = jax.value_and_grad(tail_loss, argnums=(0, 1, 2, 3, 4))(x1_, ff_, gt, g2, b2, tg)
        return dx1, dff, jnp.full((1, LANE), loss, F32), dgt, dg2, db2

    dx1_a, dff, loss_acc, d_gt_f, d_ln2_g, d_ln2_b = _rowwise(
        "norm2_loss", tail, [x1, ff, tgt], [gt_f, ln2_g, ln2_b], [(d, BF), (d, BF)], [(1, LANE), (1, d), (1, d), (1, d)], tm)

    g_w_down = _matmul(act, dff, "tn", "d_w_down", BF)
    dgate, dup = _ffn_act_bwd(dff, w_down_f, act_dg, act_du)
    g_w_gate = _matmul(dgate, h2, "tn", "d_w_gate", BF)
    g_w_up = _matmul(dup, h2, "tn", "d_w_up", BF)
    token = grads_ready("ffn", g_w_gate, g_w_up, g_w_down)
    dh2 = _matmul2_nn(dgate, w_gate_f, dup, w_up_f, "d_ffn_in", BF)

    def block1_bwd(xx, mx, dx1_, dh2_, gt, g1, b1, sc, sh):
        _, vjp = jax.vjp(block1, xx, mx, gt, g1, b1, sc, sh)
        dxx, dmx, dgt, dg1, db1, dsc, dsh = vjp((dx1_.astype(F32), dh2_.astype(F32)))
        return dxx, dmx, dgt, dg1, db1, dsc, dsh

    dx_a, dmix, d_gt_m, d_ln1_g, d_ln1_b, d_sc_f, d_sh_f = _rowwise(
        "norm1_modulate_bwd", block1_bwd, [xs, mix, dx1_a, dh2], [gt_m + token, ln1_g, ln1_b, sc_f, sh_f],
        [(d, F32), (d, BF)], [(1, d)] * 5, tm)

    dmixin = _matmul(dmix, w_o_f, "nt", "d_mixer_out", BF)
    g_w_o = _matmul(mixin, dmix, "tn", "d_w_o", BF)

    def mixer_bwd(o, z, om, dmi, g):
        _, vjp = jax.vjp(lambda o_, z_, g_: jnp.concatenate(_gdn_out(o_, z_, g_), axis=1), o, z.astype(F32), g)
        do_, dz_, dg_ = vjp(dmi[:, :DN_VW].astype(F32))
        dom = dmi[:, DN_VW:]
        delta = [jnp.broadcast_to(jnp.sum(dom[:, h * V_HEAD:(h + 1) * V_HEAD] * om[:, h * V_HEAD:(h + 1) * V_HEAD], axis=-1, keepdims=True), (o.shape[0], LANE))
                 for h in range(MLA_HEADS)]
        return do_, dz_, dom, jnp.concatenate(delta, axis=1), dg_

    do_dn, dz, do_mla, delta, d_dn_g = _rowwise(
        "mixer_out_bwd", mixer_bwd, [o_dn, (proj, DN_VW, P_Z // DN_VW), o_mla, dmixin], [dn_norm_g],
        [(DN_VW, F32), (DN_VW, BF), (MLA_VW, BF), (MLA_HEADS * LANE, F32)], [(1, DN_DV)], tm)

    dqc, dkc, dvc = _attn_bwd(qc, kc, vc, do_mla, lse, delta, min(512, t), min(1024, t))
    dcq, dckv, dkr, d_q_g, d_kv_g, g_w_uq_t, g_w_ukv_t = _mla_prep_bwd(
        proj, pos_col, inv_freq2, q_norm_g, kv_norm_g, w_uq_t, w_ukv_t, dqc, dkc, dvc, tm)

    token = grads_ready("mixer", g_w_o, g_w_uq_t, g_w_ukv_t)

    d_intra = _gdn_scan_bwd(intra, states, do_dn, gdn_tm)
    dqkv_act, dba, d_al8, d_dt8 = _gdn_intra_bwd(qkv, ba_raw, al8 + token, dt8, inverses, d_intra, min(256, t))
    dproj, d_conv8 = _conv_bwd(proj, conv_w8, dqkv_act, [(P_Z, dz), (P_CQ, dcq), (P_CKV, dckv), (P_BA, dba), (P_KR, dkr)],
                               min(256, t))
    dh1 = _matmul(dproj, w_in_t, "nn", "d_in_proj", BF)
    g_w_in_t = _matmul(dproj, h1, "tn", "d_w_in", BF)
    token = grads_ready("in", g_w_in_t)

    def modulate_bwd(xx, dh, dxa, sc):
        dh = dh.astype(F32)
        return dh * (1.0 + sc) + dxa, jnp.sum(dh * xx, axis=0, keepdims=True), jnp.sum(dh, axis=0, keepdims=True)

    grad_x, d_sc_m, d_sh_m = _rowwise("modulate_in_bwd", modulate_bwd, [xs, dh1, dx_a], [sc_m + token], [(d, F32)], [(1, d), (1, d)], tm)
    dmod = jnp.concatenate([d_sh_m, d_sc_m, d_gt_m, d_sh_f, d_sc_f, d_gt_f], axis=1)
    return grad_x, loss_acc, dmod, d_conv8, d_al8, d_dt8, d_dn_g, d_q_g, d_kv_g, d_ln1_g, d_ln1_b, d_ln2_g, d_ln2_b
```

```python
import functools
import math

import jax
import jax.numpy as jnp
from jax import lax
from jax.experimental import pallas as pl
from jax.experimental.pallas import tpu as pltpu

F32 = jnp.float32
BF = jnp.bfloat16
HI = lax.Precision.HIGHEST

N_DEV = 8
DN_HEADS = 4
DN_DK = 128
DN_DV = 128
CONV_K = 4
CHUNK = 64
MLA_HEADS = 4
QK_NOPE = 128
QK_ROPE = 64
V_HEAD = 128
Q_LORA = 512
KV_LORA = 256
ROPE_THETA = 10000.0
DEPTH = 1
DEEPNORM_ALPHA = (2.0 * DEPTH) ** 0.25
LANE = 128
CONV_HALO = 8
GL_ROWS = 8
CONV_ROWS, CONV_COLS = 64, 256

DN_QK = DN_HEADS * DN_DK
DN_VW = DN_HEADS * DN_DV
DN_CONV_CH = 2 * DN_QK + DN_VW
MLA_VW = MLA_HEADS * V_HEAD
P_Z = DN_CONV_CH
P_CQ = P_Z + DN_VW
P_CKV = P_CQ + Q_LORA
P_BA = P_CKV + KV_LORA
P_KR = P_BA + LANE
N_INP = P_KR + LANE

ADAM_LR = 0.001
ADAM_B1 = 0.9
ADAM_B2 = 0.999
ADAM_EPS = 1e-08
ADAM_WD = 0.01
ADAM_STEP = 10

NN = (((1,), (0,)), ((), ()))
NT = (((1,), (1,)), ((), ()))
TN = (((0,), (0,)), ((), ()))


def _pick(n, prefs):
    for p in prefs:
        if n % p == 0:
            return p
    return n


def _full(shape):
    return pl.BlockSpec(shape, lambda *_: (0,) * len(shape))


def _dot(a, b, dims=NN):
    return lax.dot_general(a, b, dims, preferred_element_type=F32)


def _doth(a, b, dims=NN):
    return lax.dot_general(a, b, dims, precision=HI, preferred_element_type=F32)


@jax.custom_vjp
def _mmb(a, b):
    return _dot(a.astype(BF), b.astype(BF), NN)


def _mmb_fwd(a, b):
    return _mmb(a, b), (a, b)


def _mmb_bwd(res, g):
    a, b = res
    gb = g.astype(BF)
    return (_dot(gb, b.astype(BF), NT).astype(a.dtype), _dot(a.astype(BF), gb, TN).astype(b.dtype))


_mmb.defvjp(_mmb_fwd, _mmb_bwd)


@jax.custom_vjp
def _mmb_nt(a, b):
    return _dot(a.astype(BF), b.astype(BF), NT)


def _mmb_nt_fwd(a, b):
    return _mmb_nt(a, b), (a, b)


def _mmb_nt_bwd(res, g):
    a, b = res
    gb = g.astype(BF)
    return (_dot(gb, b.astype(BF), NN).astype(a.dtype), _dot(gb, a.astype(BF), TN).astype(b.dtype))


_mmb_nt.defvjp(_mmb_nt_fwd, _mmb_nt_bwd)


def _sigmoid(x):
    return 0.5 * (jnp.tanh(0.5 * x) + 1.0)


def _silu(x):
    return x * _sigmoid(x)


def _softplus(x):
    return jnp.maximum(x, 0.0) + jnp.log(1.0 + jnp.exp(-jnp.abs(x)))


def _layernorm(x, g, b, eps=1e-5):
    mu = jnp.mean(x, axis=-1, keepdims=True)
    xc = x - mu
    var = jnp.mean(xc * xc, axis=-1, keepdims=True)
    return xc * lax.rsqrt(var + eps) * g + b


def _rmsnorm(x, g, eps=1e-6):
    return x * lax.rsqrt(jnp.mean(x * x, axis=-1, keepdims=True) + eps) * g


def _l2norm(x, eps=1e-6):
    return x * lax.rsqrt(jnp.sum(x * x, axis=-1, keepdims=True) + eps)


def _rowwise(name, fn, rows, vecs, out_rows, out_accs, tm):
    rows = [r if isinstance(r, tuple) else (r, r.shape[1], 0) for r in rows]
    t = rows[0][0].shape[0]
    tm = min(tm, t)
    assert t % tm == 0
    nr, nv, no = len(rows), len(vecs), len(out_rows)

    def body(*refs):
        ins = [r[...] for r in refs[:nr + nv]]
        outs = fn(*ins)
        outs = outs if isinstance(outs, (tuple, list)) else (outs,)
        o_rows = refs[nr + nv:nr + nv + no]
        o_accs = refs[nr + nv + no:]
        for o, val in zip(o_rows, outs[:no]):
            o[...] = val.astype(o.dtype)
        if o_accs:
            @pl.when(pl.program_id(0) == 0)
            def _():
                for o in o_accs:
                    o[...] = jnp.zeros_like(o)
            for o, val in zip(o_accs, outs[no:]):
                o[...] += val

    in_specs = [pl.BlockSpec((tm, w), functools.partial(lambda i, j: (i, j), j=j)) for (_, w, j) in rows]
    in_specs += [_full(v.shape) for v in vecs]
    out_specs = [pl.BlockSpec((tm, w), lambda i: (i, 0)) for (w, _) in out_rows]
    out_specs += [_full(s) for s in out_accs]
    out_shape = [jax.ShapeDtypeStruct((t, w), d) for (w, d) in out_rows]
    out_shape += [jax.ShapeDtypeStruct(s, F32) for s in out_accs]
    res = pl.pallas_call(
        body, grid=(t // tm,), in_specs=in_specs, out_specs=out_specs, out_shape=out_shape, name=name,
        compiler_params=pltpu.CompilerParams(dimension_semantics=("arbitrary",)),
    )(*[r[0] for r in rows], *vecs)
    return res


def _matmul(a, b, mode, name, out_dtype=F32):
    if mode == "nn":
        (m, k), n = a.shape, b.shape[1]
    elif mode == "nt":
        (m, k), n = a.shape, b.shape[0]
    else:
        (k, m), n = a.shape, b.shape[1]
    tm, tn, tk = _matmul_tiles(m, n, k, a.dtype.itemsize, b.dtype.itemsize, jnp.dtype(out_dtype).itemsize)
    nk = k // tk
    dims = {"nn": NN, "nt": NT, "tn": TN}[mode]

    def body(a_ref, b_ref, o_ref, *acc):
        part = _dot(a_ref[...].astype(BF), b_ref[...].astype(BF), dims)
        if nk == 1:
            o_ref[...] = part.astype(o_ref.dtype)
            return
        (acc_ref,) = acc
        kk = pl.program_id(2)

        @pl.when(kk == 0)
        def _():
            acc_ref[...] = part

        @pl.when(kk > 0)
        def _():
            acc_ref[...] += part

        @pl.when(kk == nk - 1)
        def _():
            o_ref[...] = acc_ref[...].astype(o_ref.dtype)

    a_spec = pl.BlockSpec((tk, tm), lambda i, j, kk: (kk, i)) if mode == "tn" else pl.BlockSpec((tm, tk), lambda i, j, kk: (i, kk))
    b_spec = pl.BlockSpec((tn, tk), lambda i, j, kk: (j, kk)) if mode == "nt" else pl.BlockSpec((tk, tn), lambda i, j, kk: (kk, j))
    return pl.pallas_call(
        body, grid=(m // tm, n // tn, nk), in_specs=[a_spec, b_spec],
        out_specs=pl.BlockSpec((tm, tn), lambda i, j, kk: (i, j)),
        out_shape=jax.ShapeDtypeStruct((m, n), out_dtype),
        scratch_shapes=[pltpu.VMEM((tm, tn), F32)] if nk > 1 else [], name=name,
        compiler_params=pltpu.CompilerParams(dimension_semantics=("parallel", "parallel", "arbitrary")),
    )(a, b)


def _lane_tile(n, cap):
    return max([n // s for s in range(1, n // LANE + 1) if n % s == 0 and (n // s) % LANE == 0 and n // s <= cap] or [n])


def _ffn_in(h, w_gate, w_up):
    m, k = h.shape
    f = w_gate.shape[0]
    tm, tn = _pick(m, (1024, 512, 256, 128)), _lane_tile(f, 1408)

    def body(h_ref, wg_ref, wu_ref, act_ref, dg_ref, du_ref):
        hh = h_ref[...]
        g = _dot(hh, wg_ref[...], NT)
        u = _dot(hh, wu_ref[...], NT)
        sg = _sigmoid(g)
        silu_g = g * sg
        act_ref[...] = (silu_g * u).astype(act_ref.dtype)
        dg_ref[...] = (u * (sg + silu_g * (1.0 - sg))).astype(dg_ref.dtype)
        du_ref[...] = silu_g.astype(du_ref.dtype)

    w_spec = pl.BlockSpec((tn, k), lambda i, j: (j, 0))
    o_spec = pl.BlockSpec((tm, tn), lambda i, j: (i, j))
    return pl.pallas_call(
        body, grid=(m // tm, f // tn), in_specs=[pl.BlockSpec((tm, k), lambda i, j: (i, 0)), w_spec, w_spec],
        out_specs=[o_spec] * 3, out_shape=[jax.ShapeDtypeStruct((m, f), BF)] * 3, name="ffn_in",
        compiler_params=pltpu.CompilerParams(dimension_semantics=("parallel", "parallel")),
    )(h, w_gate, w_up)


def _ffn_act_bwd(dff, w_down, act_dg, act_du):
    m, k = dff.shape
    f = w_down.shape[0]
    tm, tn = _pick(m, (1024, 512, 256, 128)), _lane_tile(f, 1408)

    def body(d_ref, w_ref, fg_ref, fu_ref, dg_ref, du_ref):
        da = _dot(d_ref[...], w_ref[...], NT)
        dg_ref[...] = (da * fg_ref[...].astype(F32)).astype(dg_ref.dtype)
        du_ref[...] = (da * fu_ref[...].astype(F32)).astype(du_ref.dtype)

    o_spec = pl.BlockSpec((tm, tn), lambda i, j: (i, j))
    return pl.pallas_call(
        body, grid=(m // tm, f // tn),
        in_specs=[pl.BlockSpec((tm, k), lambda i, j: (i, 0)), pl.BlockSpec((tn, k), lambda i, j: (j, 0)), o_spec, o_spec],
        out_specs=[o_spec] * 2, out_shape=[jax.ShapeDtypeStruct((m, f), BF)] * 2, name="d_ffn_act",
        compiler_params=pltpu.CompilerParams(dimension_semantics=("parallel", "parallel")),
    )(dff, w_down, act_dg, act_du)


def _matmul2_nn(a1, b1, a2, b2, name, out_dtype=F32):
    m, k = a1.shape
    n = b1.shape[1]
    tm, tn = _pick(m, (1024, 512, 256, 128)), _pick(n, (512, 256, 128))

    def body(a1_ref, b1_ref, a2_ref, b2_ref, o_ref):
        o_ref[...] = (_dot(a1_ref[...], b1_ref[...]) + _dot(a2_ref[...], b2_ref[...])).astype(o_ref.dtype)

    a_spec = pl.BlockSpec((tm, k), lambda i, j: (i, 0))
    b_spec = pl.BlockSpec((k, tn), lambda i, j: (0, j))
    return pl.pallas_call(
        body, grid=(m // tm, n // tn), in_specs=[a_spec, b_spec, a_spec, b_spec],
        out_specs=pl.BlockSpec((tm, tn), lambda i, j: (i, j)), out_shape=jax.ShapeDtypeStruct((m, n), out_dtype), name=name,
        compiler_params=pltpu.CompilerParams(dimension_semantics=("parallel", "parallel")),
    )(a1, b1, a2, b2)


def _matmul_tn_pair(a1, a2, b, name, out_dtype):
    k, m = a1.shape
    n = b.shape[1]
    tm = _lane_tile(m, 256)

    def body(a1_ref, a2_ref, b_ref, o1_ref, o2_ref):
        out = _dot(jnp.concatenate([a1_ref[...], a2_ref[...]], axis=1), b_ref[...], TN)
        o1_ref[...] = out[:tm].astype(o1_ref.dtype)
        o2_ref[...] = out[tm:].astype(o2_ref.dtype)

    a_spec = pl.BlockSpec((k, tm), lambda i: (0, i))
    o_spec = pl.BlockSpec((tm, n), lambda i: (i, 0))
    return pl.pallas_call(
        body, grid=(m // tm,), in_specs=[a_spec, a_spec, _full(b.shape)], out_specs=[o_spec] * 2,
        out_shape=[jax.ShapeDtypeStruct((m, n), out_dtype)] * 2, name=name,
        compiler_params=pltpu.CompilerParams(dimension_semantics=("parallel",)),
    )(a1, a2, b)


MATMUL_VMEM_BUDGET = 28 * 1024 * 1024


def _matmul_tiles(m, n, k, a_bytes, b_bytes, o_bytes):
    def divisors(x, cap):
        return sorted({x // s for s in range(1, 65) if x % s == 0 and (x // s) % LANE == 0 and x // s <= cap}, reverse=True) or [x]

    for tk in divisors(k, k):
        best = None
        for tm in divisors(m, 1024):
            for tn in divisors(n, 2048):
                need = 2 * (tm * tk * a_bytes + tk * tn * b_bytes + tm * tn * o_bytes) + (tm * tn * 4 if tk < k else 0)
                if need <= MATMUL_VMEM_BUDGET and tm * tn >= 512 * 512 and (best is None or tm * tn > best[0] * best[1]):
                    best = (tm, tn)
        if best:
            return best[0], best[1], tk
    return _pick(m, (512, 256, 128)), _pick(n, (512, 256, 128)), _pick(k, (512, 256, 128))


def _exchange(xs, name, scatter):
    n = len(xs)
    npeer = N_DEV - 1

    def body(*refs):
        x_refs, o_refs = refs[:n], refs[n:2 * n]
        send_sems, recv_sems, local_sems = refs[2 * n:]
        mx, my, mc = lax.axis_index("x"), lax.axis_index("y"), lax.axis_index("c")
        me = 4 * mx + 2 * my + mc
        src_me = [x.at[me] if scatter else x for x in x_refs]
        mine = [pltpu.make_async_copy(src_me[a], o_refs[a].at[me], local_sems.at[a]) for a in range(n)]
        for cp in mine:
            cp.start()
        copies = []
        for k in range(1, N_DEV):
            px, py, pc = mx ^ (k >> 2), my ^ ((k >> 1) & 1), mc ^ (k & 1)
            peer = 4 * px + 2 * py + pc
            for a in range(n):
                cp = pltpu.make_async_remote_copy(
                    src_ref=x_refs[a].at[peer] if scatter else x_refs[a], dst_ref=o_refs[a].at[me],
                    send_sem=send_sems.at[a * npeer + k - 1], recv_sem=recv_sems.at[a * npeer + k - 1],
                    device_id=(px, py, pc), device_id_type=pl.DeviceIdType.MESH)
                cp.start()
                copies.append((cp, a, k, peer))
        for cp, a, k, peer in copies:
            pltpu.make_async_remote_copy(
                src_ref=src_me[a], dst_ref=o_refs[a].at[peer], send_sem=send_sems.at[a * npeer + k - 1],
                recv_sem=recv_sems.at[a * npeer + k - 1], device_id=(mx, my, mc),
                device_id_type=pl.DeviceIdType.MESH).wait_recv()
        for cp, _, _, _ in copies:
            cp.wait_send()
        for cp in mine:
            cp.wait()

    return pl.pallas_call(
        body, out_shape=[jax.ShapeDtypeStruct((N_DEV,) + x.shape[-2:], x.dtype) for x in xs],
        in_specs=[pl.BlockSpec(memory_space=pl.ANY)] * n, out_specs=[pl.BlockSpec(memory_space=pl.ANY)] * n,
        scratch_shapes=[pltpu.SemaphoreType.DMA((n * npeer,)), pltpu.SemaphoreType.DMA((n * npeer,)),
                        pltpu.SemaphoreType.DMA((n,))],
        name=name,
    )(*xs)


def _gather_by_chip(xs, name):
    n = len(xs)
    per = N_DEV - 1

    def body(*refs):
        x_refs, o_refs = refs[:n], refs[n:2 * n]
        send_sems, recv_sems, local_sems = refs[2 * n:]
        mx, my, mc = lax.axis_index("x"), lax.axis_index("y"), lax.axis_index("c")
        me, sibling = (mx, my, mc), (mx, my, 1 - mc)
        chips = [(1 - mx, my), (mx, 1 - my), (1 - mx, 1 - my)]
        slot = lambda d: 4 * d[0] + 2 * d[1] + d[2]

        def copy(a, k, block, to, src=None):
            dst = o_refs[a].at[slot(block)]
            return pltpu.make_async_remote_copy(
                src_ref=dst if src is None else src, dst_ref=dst, send_sem=send_sems.at[a * per + k],
                recv_sem=recv_sems.at[a * per + k], device_id=to, device_id_type=pl.DeviceIdType.MESH)

        mine = [pltpu.make_async_copy(x_refs[a], o_refs[a].at[slot(me)], local_sems.at[a]) for a in range(n)]
        for cp in mine:
            cp.start()
        first = []
        for a in range(n):
            first.append(copy(a, 0, me, sibling, src=x_refs[a]))
            first += [copy(a, 1 + j, me, (*chip, mc), src=x_refs[a]) for j, chip in enumerate(chips)]
        for cp in first:
            cp.start()
        passed = []
        for j, chip in enumerate(chips):
            for a in range(n):
                copy(a, 1 + j, (*chip, mc), me).wait_recv()
                cp = copy(a, 4 + j, (*chip, mc), sibling)
                cp.start()
                passed.append(cp)
        for a in range(n):
            copy(a, 0, sibling, me).wait_recv()
            for j, chip in enumerate(chips):
                copy(a, 4 + j, (*chip, 1 - mc), me).wait_recv()
        for cp in first + passed:
            cp.wait_send()
        for cp in mine:
            cp.wait()

    return pl.pallas_call(
        body, out_shape=[jax.ShapeDtypeStruct((N_DEV,) + x.shape, x.dtype) for x in xs],
        in_specs=[pl.BlockSpec(memory_space=pl.ANY)] * n, out_specs=[pl.BlockSpec(memory_space=pl.ANY)] * n,
        scratch_shapes=[pltpu.SemaphoreType.DMA((n * per,)), pltpu.SemaphoreType.DMA((n * per,)),
                        pltpu.SemaphoreType.DMA((n,))],
        name=name,
    )(*xs)


def _peer_of(k):
    mx, my, mc = lax.axis_index("x"), lax.axis_index("y"), lax.axis_index("c")
    px, py, pc = mx ^ (k >> 2), my ^ ((k >> 1) & 1), mc ^ (k & 1)
    return (px, py, pc), 4 * px + 2 * py + pc


def _exchange_start(xs, name, scatter):
    n = len(xs)
    npeer = N_DEV - 1

    def body(*refs):
        x_refs, land_refs = refs[:n], refs[n:2 * n]
        send_sems, recv_sems, token = refs[2 * n], refs[2 * n + 1], refs[-1]
        me = 4 * lax.axis_index("x") + 2 * lax.axis_index("y") + lax.axis_index("c")
        for k in range(1, N_DEV):
            dev, peer = _peer_of(k)
            for a in range(n):
                pltpu.make_async_remote_copy(
                    src_ref=x_refs[a].at[peer] if scatter else x_refs[a], dst_ref=land_refs[a].at[me],
                    send_sem=send_sems.at[a * npeer + k - 1], recv_sem=recv_sems.at[a * npeer + k - 1],
                    device_id=dev, device_id_type=pl.DeviceIdType.MESH).start()
        token[...] = jnp.zeros_like(token)

    hbm = pl.BlockSpec(memory_space=pltpu.HBM)
    sem = pl.BlockSpec(memory_space=pltpu.SEMAPHORE)
    lands = [pltpu.with_memory_space_constraint(lax.empty((N_DEV,) + x.shape[-2:], x.dtype), pltpu.HBM) for x in xs]
    srcs = [pltpu.with_memory_space_constraint(x, pltpu.HBM) for x in xs]
    outs = pl.pallas_call(
        body, name=name,
        out_shape=(pltpu.SemaphoreType.DMA((n * npeer,)), pltpu.SemaphoreType.DMA((n * npeer,)),
                   *[pltpu.HBM(x.shape, x.dtype) for x in srcs], *[pltpu.HBM(z.shape, z.dtype) for z in lands],
                   jax.ShapeDtypeStruct((8, LANE), F32)),
        in_specs=[hbm] * (2 * n), out_specs=(sem, sem, *[hbm] * (2 * n), pl.BlockSpec(memory_space=pltpu.VMEM)),
        input_output_aliases={i: 2 + i for i in range(2 * n)},
        compiler_params=pltpu.CompilerParams(has_side_effects=pltpu.SideEffectType.DATAFLOW_SIDE_EFFECTING),
    )(*srcs, *lands)
    return (outs[0], outs[1], list(outs[2:2 + n]), list(outs[2 + n:2 + 2 * n])), outs[-1][0:1, 0:1]


def _exchange_wait(started, after, name, scatter):
    send_sems, recv_sems, srcs, lands = started
    n = len(srcs)
    npeer = N_DEV - 1

    def body(*refs):
        x_refs, land_refs = refs[:n], refs[n:2 * n]
        send_sems, recv_sems = refs[2 * n], refs[2 * n + 1]
        mx, my, mc = lax.axis_index("x"), lax.axis_index("y"), lax.axis_index("c")
        me = 4 * mx + 2 * my + mc
        for k in range(1, N_DEV):
            _, peer = _peer_of(k)
            for a in range(n):
                src = x_refs[a].at[me] if scatter else x_refs[a]
                cp = pltpu.make_async_remote_copy(
                    src_ref=src, dst_ref=land_refs[a].at[peer], send_sem=send_sems.at[a * npeer + k - 1],
                    recv_sem=recv_sems.at[a * npeer + k - 1], device_id=(mx, my, mc), device_id_type=pl.DeviceIdType.MESH)
                cp.wait_send()
                cp.wait_recv()

    hbm = pl.BlockSpec(memory_space=pltpu.HBM)
    sem = pl.BlockSpec(memory_space=pltpu.SEMAPHORE)
    outs = pl.pallas_call(
        body, name=name,
        out_shape=(*[pltpu.HBM(x.shape, x.dtype) for x in srcs], *[pltpu.HBM(z.shape, z.dtype) for z in lands]),
        in_specs=[hbm] * (2 * n) + [sem, sem, pl.BlockSpec(memory_space=pl.ANY)], out_specs=tuple([hbm] * (2 * n)),
        input_output_aliases={i: i for i in range(2 * n)},
        compiler_params=pltpu.CompilerParams(has_side_effects=pltpu.SideEffectType.DATAFLOW_SIDE_EFFECTING),
    )(*srcs, *lands, send_sems, recv_sems, after)
    me = 4 * lax.axis_index("x") + 2 * lax.axis_index("y") + lax.axis_index("c")
    full = []
    for x, land in zip(outs[:n], outs[n:]):
        own = lax.dynamic_slice(x, (me, 0, 0), (1,) + x.shape[1:]) if scatter else x[None]
        full.append(lax.dynamic_update_slice(land, own, (me, 0, 0)))
    return full


def _sum_slots(x, name):
    _, r, c = x.shape
    tr = _pick(r, (512, 256, 128, 64, 32, 16))

    def body(x_ref, o_ref):
        acc = x_ref[0].astype(F32)
        for s in range(1, N_DEV):
            acc = acc + x_ref[s].astype(F32)
        o_ref[...] = acc

    return pl.pallas_call(
        body, grid=(r // tr,), in_specs=[pl.BlockSpec((N_DEV, tr, c), lambda i: (0, i, 0))],
        out_specs=pl.BlockSpec((tr, c), lambda i: (i, 0)), out_shape=jax.ShapeDtypeStruct((r, c), F32), name=name,
        compiler_params=pltpu.CompilerParams(dimension_semantics=("arbitrary",)),
    )(x)


def _mod_fwd(c_all, w_ada, b_ada_mine):
    def body(c_ref, w_ref, b_ref, o_ref):
        o_ref[...] = _doth(_silu(c_ref[...]), w_ref[...]) + b_ref[...]

    return pl.pallas_call(body, out_shape=jax.ShapeDtypeStruct((c_all.shape[0], w_ada.shape[1]), F32), name="mod_fwd")(c_all, w_ada, b_ada_mine)


def _mod_bwd(c_all_t, dmod_mine):
    def body(ct_ref, d_ref, o_ref):
        s = _silu(ct_ref[...])
        acc = s[:, 0:1] * d_ref[pl.ds(0, 1), :]
        for b in range(1, N_DEV):
            acc = acc + s[:, b:b + 1] * d_ref[pl.ds(b, 1), :]
        o_ref[...] = acc

    return pl.pallas_call(body, out_shape=jax.ShapeDtypeStruct((c_all_t.shape[0], dmod_mine.shape[1]), F32), name="mod_bwd")(c_all_t, dmod_mine)


def _conv_fwd(proj, conv_w8, tm):
    t = proj.shape[0]
    ch = DN_CONV_CH

    def body(x_ref, w_ref, o_ref, buf):
        @pl.when(pl.program_id(0) == 0)
        def _():
            buf[pl.ds(0, CONV_HALO), :] = jnp.zeros((CONV_HALO, ch), F32)

        buf[pl.ds(CONV_HALO, tm), :] = x_ref[...].astype(F32)
        for c0 in range(0, ch, CONV_COLS):
            cols = pl.ds(c0, CONV_COLS)
            w = [w_ref[pl.ds(j, 1), cols] for j in range(CONV_K)]
            for r0 in range(0, tm, CONV_ROWS):
                acc = buf[pl.ds(r0 + CONV_HALO - (CONV_K - 1), CONV_ROWS), cols] * w[0]
                for j in range(1, CONV_K):
                    acc = acc + buf[pl.ds(r0 + CONV_HALO - (CONV_K - 1) + j, CONV_ROWS), cols] * w[j]
                o_ref[pl.ds(r0, CONV_ROWS), cols] = _silu(acc)
        buf[pl.ds(0, CONV_HALO), :] = buf[pl.ds(tm, CONV_HALO), :]

    return pl.pallas_call(
        body, grid=(t // tm,), in_specs=[pl.BlockSpec((tm, ch), lambda i: (i, 0)), _full(conv_w8.shape)],
        out_specs=pl.BlockSpec((tm, ch), lambda i: (i, 0)), out_shape=jax.ShapeDtypeStruct((t, ch), F32),
        scratch_shapes=[pltpu.VMEM((tm + CONV_HALO, ch), F32)], name="conv_fwd",
        compiler_params=pltpu.CompilerParams(dimension_semantics=("arbitrary",)),
    )(proj, conv_w8)


def _conv_bwd(proj, conv_w8, dact, others, tm):
    t = proj.shape[0]
    ch = DN_CONV_CH
    nt = t // tm
    halo_blk = 2 * CONV_HALO
    hb = tm // halo_blk
    n_others = len(others)

    def body(x_ref, xp_ref, w_ref, dy_ref, *refs):
        piece_refs, (dx_ref, dw_ref, xbuf, dbuf) = refs[:n_others], refs[n_others:]
        step = pl.program_id(0)
        for (off, arr), p_ref in zip(others, piece_refs):
            dx_ref[:, pl.ds(off, arr.shape[1])] = p_ref[...].astype(dx_ref.dtype)

        @pl.when(step == 0)
        def _():
            dbuf[pl.ds(tm, CONV_HALO), :] = jnp.zeros((CONV_HALO, ch), F32)
            dw_ref[...] = jnp.zeros_like(dw_ref)

        first = step == nt - 1
        xbuf[pl.ds(0, CONV_HALO), :] = jnp.where(first, 0.0, xp_ref[...].astype(F32)[halo_blk - CONV_HALO:])
        xbuf[pl.ds(CONV_HALO, tm), :] = x_ref[...].astype(F32)
        for c0 in range(0, ch, CONV_COLS):
            cols = pl.ds(c0, CONV_COLS)
            w = [w_ref[pl.ds(j, 1), cols] for j in range(CONV_K)]
            dw = [jnp.zeros((1, CONV_COLS), F32) for _ in range(CONV_K)]
            for r0 in range(0, tm, CONV_ROWS):
                xs = [xbuf[pl.ds(r0 + CONV_HALO - (CONV_K - 1) + j, CONV_ROWS), cols] for j in range(CONV_K)]
                pre = xs[0] * w[0]
                for j in range(1, CONV_K):
                    pre = pre + xs[j] * w[j]
                sg = _sigmoid(pre)
                dpre = dy_ref[pl.ds(r0, CONV_ROWS), cols] * (sg * (1.0 + pre * (1.0 - sg)))
                dbuf[pl.ds(r0, CONV_ROWS), cols] = dpre
                dw = [dw[j] + jnp.sum(dpre * xs[j], axis=0, keepdims=True) for j in range(CONV_K)]
            for j in range(CONV_K):
                dw_ref[pl.ds(j, 1), cols] += dw[j]
            for r0 in range(0, tm, CONV_ROWS):
                dx = dbuf[pl.ds(r0 + CONV_K - 1, CONV_ROWS), cols] * w[0]
                for j in range(1, CONV_K):
                    dx = dx + dbuf[pl.ds(r0 + CONV_K - 1 - j, CONV_ROWS), cols] * w[j]
                dx_ref[pl.ds(r0, CONV_ROWS), cols] = dx.astype(dx_ref.dtype)
        dbuf[pl.ds(tm, CONV_HALO), :] = dbuf[pl.ds(0, CONV_HALO), :]

    rev = lambda i: (nt - 1 - i, 0)
    prev = lambda i: (jnp.maximum((nt - 1 - i) * hb - 1, 0), 0)
    return pl.pallas_call(
        body, grid=(nt,),
        in_specs=[pl.BlockSpec((tm, ch), rev), pl.BlockSpec((halo_blk, ch), prev), _full(conv_w8.shape),
                  pl.BlockSpec((tm, ch), rev)] + [pl.BlockSpec((tm, arr.shape[1]), rev) for _, arr in others],
        out_specs=[pl.BlockSpec((tm, N_INP), rev), _full(conv_w8.shape)],
        out_shape=[jax.ShapeDtypeStruct((t, N_INP), BF), jax.ShapeDtypeStruct(conv_w8.shape, F32)],
        scratch_shapes=[pltpu.VMEM((tm + CONV_HALO, ch), F32), pltpu.VMEM((tm + CONV_HALO, ch), F32)], name="conv_bwd",
        compiler_params=pltpu.CompilerParams(dimension_semantics=("arbitrary",)),
    )(proj, proj, conv_w8, dact, *[arr for _, arr in others])


BNN = (((2,), (1,)), ((0,), (0,)))
BNT = (((2,), (2,)), ((0,), (0,)))
BTN = (((1,), (1,)), ((0,), (0,)))


def _bdot(a, b, dims, precision=None):
    return lax.dot_general(a, b, dims, precision=precision, preferred_element_type=F32)


@jax.custom_vjp
def _bmmb_nt(a, b):
    return _bdot(a.astype(BF), b.astype(BF), BNT)


def _bmmb_nt_fwd(a, b):
    return _bmmb_nt(a, b), (a, b)


def _bmmb_nt_bwd(res, g):
    a, b = res
    gb = g.astype(BF)
    return _bdot(gb, b.astype(BF), BNN), _bdot(gb, a.astype(BF), BTN)


_bmmb_nt.defvjp(_bmmb_nt_fwd, _bmmb_nt_bwd)


@jax.custom_vjp
def _bmmb(a, b):
    return _bdot(a.astype(BF), b.astype(BF), BNN)


def _bmmb_fwd(a, b):
    return _bmmb(a, b), (a, b)


def _bmmb_bwd(res, g):
    a, b = res
    gb = g.astype(BF)
    return _bdot(gb, b.astype(BF), BNT), _bdot(a.astype(BF), gb, BTN)


_bmmb.defvjp(_bmmb_fwd, _bmmb_bwd)


@jax.custom_vjp
def _bmmb_tn(a, b):
    return _bdot(a.astype(BF), b.astype(BF), BTN)


def _bmmb_tn_fwd(a, b):
    return _bmmb_tn(a, b), (a, b)


def _bmmb_tn_bwd(res, g):
    a, b = res
    gb = g.astype(BF)
    return _bdot(b.astype(BF), gb, BNT), _bdot(a.astype(BF), gb, BNN)


_bmmb_tn.defvjp(_bmmb_tn_fwd, _bmmb_tn_bwd)


def _triangle_sums(g, lower):
    c = g.shape[1]
    ri = lax.broadcasted_iota(jnp.int32, (g.shape[0], c, c), 1)
    ci = lax.broadcasted_iota(jnp.int32, (g.shape[0], c, c), 2)
    tri = (ri >= ci if lower else ri <= ci).astype(BF)
    hi = g.astype(BF)
    mid = (g - hi.astype(F32)).astype(BF)
    lo = (g - hi.astype(F32) - mid.astype(F32)).astype(BF)
    return _bdot(tri, hi, BNN) + _bdot(tri, mid, BNN) + _bdot(tri, lo, BNN)


@jax.custom_vjp
def _chunk_cumsum(g):
    return _triangle_sums(g, True)


_chunk_cumsum.defvjp(lambda g: (_triangle_sums(g, True), None), lambda _, ct: (_triangle_sums(ct, False),))


def _unit_lower_solve_fwd(a, r):
    c = a.shape[-1]
    ri = lax.broadcasted_iota(jnp.int32, a.shape, 1)
    ci = lax.broadcasted_iota(jnp.int32, a.shape, 2)
    xm = -a
    inv = (ri == ci).astype(F32) + xm
    for _ in range(int(math.log2(c)) - 1):
        xm = _bdot(xm, xm, BNN, HI)
        inv = inv + _bdot(inv, xm, BNN, HI)
    x = _bdot(inv, r, BNN, HI)
    return x, (inv, x)


def _unit_lower_solve_bwd(res, g):
    inv, x = res
    dr = _bdot(inv, g, BTN, HI)
    return -_bdot(dr, x, BNT, HI), dr


@jax.custom_vjp
def _unit_lower_solve_given(a, r, inv):
    return _bdot(inv, r, BNN, HI)


def _unit_lower_solve_given_fwd(a, r, inv):
    x = _bdot(inv, r, BNN, HI)
    return x, (inv, x)


def _unit_lower_solve_given_bwd(res, g):
    da, dr = _unit_lower_solve_bwd(res, g)
    return da, dr, jnp.zeros_like(res[0])


_unit_lower_solve_given.defvjp(_unit_lower_solve_given_fwd, _unit_lower_solve_given_bwd)


def _gdn_intra(qkv, ba, al8, dt8, inv4=None):
    tm = qkv.shape[0]
    nb = tm // CHUNK
    bsz = DN_HEADS * nb

    def heads(x0):
        return jnp.concatenate([qkv[:, x0 + h * LANE:x0 + (h + 1) * LANE].reshape(nb, CHUNK, LANE) for h in range(DN_HEADS)], axis=0)

    def spread(c0):
        return jnp.concatenate([jnp.broadcast_to(ba[:, c0 + h:c0 + h + 1], (tm, LANE)).reshape(nb, CHUNK, LANE)
                                for h in range(DN_HEADS)], axis=0)

    def per_head(v8):
        return jnp.concatenate([jnp.broadcast_to(v8[0:1, h:h + 1].reshape(1, 1, 1), (nb, 1, LANE)) for h in range(DN_HEADS)], axis=0)

    ri = lax.broadcasted_iota(jnp.int32, (bsz, CHUNK, CHUNK), 1)
    ci = lax.broadcasted_iota(jnp.int32, (bsz, CHUNK, CHUNK), 2)
    incl = ri >= ci
    strict = ri > ci

    q = _l2norm(heads(0)) * (DN_DK ** -0.5)
    k = _l2norm(heads(DN_QK))
    va = heads(2 * DN_QK)
    beta = _sigmoid(spread(0))
    g = -jnp.exp(per_head(al8)) * _softplus(spread(DN_HEADS) + per_head(dt8))
    gc = _chunk_cumsum(g)
    g_last = jnp.sum(g, axis=1, keepdims=True)
    gcol = gc[:, :, :CHUNK]
    diff = gcol - jnp.swapaxes(gcol, 1, 2)
    decay = jnp.where(incl, jnp.exp(jnp.where(incl, diff, 0.0)), 0.0)
    kb = k * beta
    a_mat = jnp.where(strict, _bmmb_nt(kb, k) * decay, 0.0)
    egc = jnp.exp(gc)
    rhs = jnp.concatenate([kb * egc, va * beta], axis=2)
    if inv4 is None:
        wu, (inv, _) = _unit_lower_solve_fwd(a_mat, rhs)
    else:
        wu = _unit_lower_solve_given(a_mat, rhs, inv4.reshape(bsz, CHUNK, CHUNK))
    attn = jnp.where(incl, _bmmb_nt(q, k) * decay, 0.0)

    def unheads(x):
        return jnp.concatenate([x[h * nb:(h + 1) * nb].reshape(tm, LANE) for h in range(DN_HEADS)], axis=1)

    w_c, u_c = wu[:, :, :DN_DK], wu[:, :, DN_DK:]
    kd = k * jnp.exp(g_last - gc)
    out = (unheads(q * egc - _bmmb(attn, w_c)), unheads(_bmmb(attn, u_c)),
           _bmmb_tn(kd, w_c).reshape(DN_HEADS, nb, DN_DK, DN_DK), _bmmb_tn(kd, u_c).reshape(DN_HEADS, nb, DN_DK, DN_DV),
           jnp.broadcast_to(g_last, (bsz, GL_ROWS, LANE)).reshape(DN_HEADS, nb, GL_ROWS, LANE))
    return out if inv4 is not None else out + (inv.reshape(DN_HEADS, nb, CHUNK, CHUNK),)


def _gdn_scan_step(qp, op, c_mat, n_mat, gl, s):
    return _mmb(qp, s) + op, s * jnp.exp(gl) - _mmb(c_mat, s) + n_mat


def _gdn_intra_specs(t, tm, dts, order=lambda i: i):
    nb = tm // CHUNK
    row = pl.BlockSpec((tm, DN_VW), lambda i: (order(i), 0))
    mat = pl.BlockSpec((DN_HEADS, nb, DN_DK, DN_DV), lambda i: (0, order(i), 0, 0))
    row_shape = lambda d: jax.ShapeDtypeStruct((t, DN_VW), d)
    mat_shape = lambda d: jax.ShapeDtypeStruct((DN_HEADS, t // CHUNK, DN_DK, DN_DV), d)
    gl = pl.BlockSpec((DN_HEADS, nb, GL_ROWS, LANE), lambda i: (0, order(i), 0, 0))
    gl_shape = jax.ShapeDtypeStruct((DN_HEADS, t // CHUNK, GL_ROWS, LANE), dts[4])
    return [row, row, mat, mat, gl], [row_shape(dts[0]), row_shape(dts[1]), mat_shape(dts[2]), mat_shape(dts[3]), gl_shape]


def _gdn_intra_fwd(qkv, ba, al8, dt8, tm):
    t = qkv.shape[0]

    def body(qkv_ref, ba_ref, al_ref, dt_ref, *outs):
        for o, val in zip(outs, _gdn_intra(qkv_ref[...], ba_ref[...], al_ref[...], dt_ref[...])):
            o[...] = val.astype(o.dtype)

    specs, shapes = _gdn_intra_specs(t, tm, (BF, F32, BF, BF, F32))
    specs.append(_gdn_inverse_spec(tm))
    shapes.append(jax.ShapeDtypeStruct((DN_HEADS, t // CHUNK, CHUNK, CHUNK), F32))
    res = pl.pallas_call(
        body, grid=(t // tm,),
        in_specs=[pl.BlockSpec((tm, DN_CONV_CH), lambda i: (i, 0)), pl.BlockSpec((tm, LANE), lambda i: (i, 0)),
                  _full(al8.shape), _full(dt8.shape)],
        out_specs=specs, out_shape=shapes, name="gdn_intra_fwd",
        compiler_params=pltpu.CompilerParams(dimension_semantics=("parallel",)),
    )(qkv, ba, al8, dt8)
    return res[:5], res[5]


def _gdn_inverse_spec(tm):
    return pl.BlockSpec((DN_HEADS, tm // CHUNK, CHUNK, CHUNK), lambda i: (0, i, 0, 0))


def _gdn_intra_bwd(qkv, ba, al8, dt8, inverses, cts, tm):
    t = qkv.shape[0]

    def body(qkv_ref, ba_ref, al_ref, dt_ref, inv_ref, *refs):
        ct_refs, (dqkv_ref, dba_ref, dal_ref, ddt_ref) = refs[:5], refs[5:]

        @pl.when(pl.program_id(0) == 0)
        def _():
            dal_ref[...] = jnp.zeros_like(dal_ref)
            ddt_ref[...] = jnp.zeros_like(ddt_ref)

        _, vjp = jax.vjp(functools.partial(_gdn_intra, inv4=inv_ref[...]), qkv_ref[...], ba_ref[...], al_ref[...], dt_ref[...])
        dqkv, dba, dal, ddt = vjp(tuple(r[...].astype(F32) for r in ct_refs))
        dqkv_ref[...] = dqkv.astype(dqkv_ref.dtype)
        dba_ref[...] = dba.astype(dba_ref.dtype)
        dal_ref[...] += dal
        ddt_ref[...] += ddt

    specs, _ = _gdn_intra_specs(t, tm, (F32,) * 5)
    return pl.pallas_call(
        body, grid=(t // tm,),
        in_specs=[pl.BlockSpec((tm, DN_CONV_CH), lambda i: (i, 0)), pl.BlockSpec((tm, LANE), lambda i: (i, 0)),
                  _full(al8.shape), _full(dt8.shape), _gdn_inverse_spec(tm)] + specs,
        out_specs=[pl.BlockSpec((tm, DN_CONV_CH), lambda i: (i, 0)), pl.BlockSpec((tm, LANE), lambda i: (i, 0)),
                   _full(al8.shape), _full(dt8.shape)],
        out_shape=[jax.ShapeDtypeStruct((t, DN_CONV_CH), BF), jax.ShapeDtypeStruct((t, LANE), BF),
                   jax.ShapeDtypeStruct(al8.shape, F32), jax.ShapeDtypeStruct(dt8.shape, F32)],
        name="gdn_intra_bwd", compiler_params=pltpu.CompilerParams(dimension_semantics=("arbitrary",)),
    )(qkv, ba, al8, dt8, inverses, *cts)


def _gdn_scan_fwd(intra, tm):
    t = intra[0].shape[0]
    nb = tm // CHUNK
    nc = t // CHUNK

    def body(qp_ref, op_ref, c_ref, n_ref, gl_ref, o_ref, ss_ref, s_scr):
        @pl.when(pl.program_id(0) == 0)
        def _():
            s_scr[...] = jnp.zeros_like(s_scr)

        state = [s_scr[h] for h in range(DN_HEADS)]
        for cc in range(nb):
            rows = pl.ds(cc * CHUNK, CHUNK)
            for h in range(DN_HEADS):
                cols = pl.ds(h * DN_DV, DN_DV)
                ss_ref[cc, h] = state[h].astype(ss_ref.dtype)
                o_ref[rows, cols], state[h] = _gdn_scan_step(
                    qp_ref[rows, cols], op_ref[rows, cols], c_ref[h, cc], n_ref[h, cc], gl_ref[h, cc, pl.ds(0, 1), :], state[h])
        for h in range(DN_HEADS):
            s_scr[h] = state[h]

    specs, _ = _gdn_intra_specs(t, tm, (F32,) * 5)
    return pl.pallas_call(
        body, grid=(t // tm,), in_specs=specs,
        out_specs=[pl.BlockSpec((tm, DN_VW), lambda i: (i, 0)),
                   pl.BlockSpec((nb, DN_HEADS, DN_DK, DN_DV), lambda i: (i, 0, 0, 0))],
        out_shape=[jax.ShapeDtypeStruct((t, DN_VW), F32), jax.ShapeDtypeStruct((nc, DN_HEADS, DN_DK, DN_DV), BF)],
        scratch_shapes=[pltpu.VMEM((DN_HEADS, DN_DK, DN_DV), F32)], name="gdn_scan_fwd",
        compiler_params=pltpu.CompilerParams(dimension_semantics=("arbitrary",)),
    )(*intra)


def _gdn_scan_bwd(intra, states, do, tm):
    t = intra[0].shape[0]
    nb = tm // CHUNK
    ng = t // tm

    def body(qp_ref, op_ref, c_ref, n_ref, gl_ref, ss_ref, do_ref, dqp_ref, dop_ref, dc_ref, dn_ref, dgl_ref, ds_scr):
        @pl.when(pl.program_id(0) == 0)
        def _():
            ds_scr[...] = jnp.zeros_like(ds_scr)

        d_state = [ds_scr[h] for h in range(DN_HEADS)]
        for cc in reversed(range(nb)):
            rows = pl.ds(cc * CHUNK, CHUNK)
            for h in range(DN_HEADS):
                cols = pl.ds(h * DN_DV, DN_DV)
                _, vjp = jax.vjp(_gdn_scan_step, qp_ref[rows, cols].astype(F32), op_ref[rows, cols], c_ref[h, cc].astype(F32),
                                 n_ref[h, cc].astype(F32), gl_ref[h, cc, pl.ds(0, 1), :], ss_ref[cc, h].astype(F32))
                dqp_ref[rows, cols], dop_ref[rows, cols], dc, dn, dgl, d_state[h] = vjp((do_ref[rows, cols], d_state[h]))
                dc_ref[h, cc] = dc.astype(dc_ref.dtype)
                dn_ref[h, cc] = dn.astype(dn_ref.dtype)
                first_row = lax.broadcasted_iota(jnp.int32, (GL_ROWS, LANE), 0) == 0
                dgl_ref[h, cc] = jnp.where(first_row, dgl, 0.0)
        for h in range(DN_HEADS):
            ds_scr[h] = d_state[h]

    five, shapes = _gdn_intra_specs(t, tm, (F32, F32, BF, BF, F32), order=lambda i: ng - 1 - i)
    row = five[0]
    return pl.pallas_call(
        body, grid=(ng,),
        in_specs=five + [pl.BlockSpec((nb, DN_HEADS, DN_DK, DN_DV), lambda i: (ng - 1 - i, 0, 0, 0)), row],
        out_specs=five, out_shape=shapes,
        scratch_shapes=[pltpu.VMEM((DN_HEADS, DN_DK, DN_DV), F32)], name="gdn_scan_bwd",
        compiler_params=pltpu.CompilerParams(dimension_semantics=("arbitrary",)),
    )(*intra, states, do)


def _gdn_out(o, z, g):
    parts = []
    for h in range(DN_HEADS):
        sl = slice(h * DN_DV, (h + 1) * DN_DV)
        parts.append(_rmsnorm(o[:, sl], g) * _silu(z[:, sl]))
    return parts


_Q_SCALE = math.log2(math.e) / math.sqrt(QK_NOPE + QK_ROPE)


def _rope_tables(pos, inv_freq2):
    lane = lax.broadcasted_iota(jnp.int32, (1, LANE), 1)
    ang = pos * inv_freq2
    cos = jnp.where(lane < QK_ROPE, jnp.cos(ang), 0.0)
    sin = jnp.where(lane < QK_ROPE // 2, -jnp.sin(ang), jnp.where(lane < QK_ROPE, jnp.sin(ang), 0.0))
    return cos, sin


@jax.custom_vjp
def _rope_swap(u):
    lane = lax.broadcasted_iota(jnp.int32, u.shape, 1)
    half = QK_ROPE // 2
    return jnp.where(lane < half, pltpu.roll(u, LANE - half, 1), jnp.where(lane < QK_ROPE, pltpu.roll(u, half, 1), 0.0))


_rope_swap.defvjp(lambda u: (_rope_swap(u), None), lambda _, g: (_rope_swap(g),))


def _mla_prep(cq, ckv, kr, gq, gkv, w_uq, w_ukv, cos, sin):
    rope = lambda u: u * cos + _rope_swap(u) * sin
    q_lin = _mmb_nt(_rmsnorm(cq, gq), w_uq) * _Q_SCALE
    kv_lin = _mmb_nt(_rmsnorm(ckv, gkv), w_ukv)
    k_rope = rope(kr)
    qs, ks, vs = [], [], []
    for h in range(MLA_HEADS):
        qs += [q_lin[:, h * LANE:(h + 1) * LANE], rope(q_lin[:, (MLA_HEADS + h) * LANE:(MLA_HEADS + h + 1) * LANE])]
        ks += [kv_lin[:, 2 * h * LANE:(2 * h + 1) * LANE], k_rope]
        vs += [kv_lin[:, (2 * h + 1) * LANE:(2 * h + 2) * LANE]]
    return qs + ks + vs


def _mla_prep_fwd(proj, pos_col, inv_freq2, gq, gkv, w_uq, w_ukv, tm):
    t = proj.shape[0]
    nq = 2 * MLA_HEADS

    def body(cq_ref, ckv_ref, kr_ref, pos_ref, f_ref, gq_ref, gkv_ref, wq_ref, wkv_ref, q_ref, k_ref, v_ref):
        cos, sin = _rope_tables(pos_ref[...], f_ref[...])
        outs = _mla_prep(cq_ref[...].astype(F32), ckv_ref[...].astype(F32), kr_ref[...].astype(F32), gq_ref[...], gkv_ref[...], wq_ref[...], wkv_ref[...],
                         cos, sin)
        for i in range(nq):
            q_ref[:, pl.ds(i * LANE, LANE)] = outs[i].astype(q_ref.dtype)
            k_ref[:, pl.ds(i * LANE, LANE)] = outs[nq + i].astype(k_ref.dtype)
        for h in range(MLA_HEADS):
            v_ref[:, pl.ds(h * LANE, LANE)] = outs[2 * nq + h].astype(v_ref.dtype)

    row = lambda w, j: pl.BlockSpec((tm, w), functools.partial(lambda i, j: (i, j), j=j))
    return pl.pallas_call(
        body, grid=(t // tm,),
        in_specs=[row(Q_LORA, P_CQ // Q_LORA), row(KV_LORA, P_CKV // KV_LORA), row(LANE, P_KR // LANE),
                  pl.BlockSpec((tm, 1), lambda i: (i, 0)), _full(inv_freq2.shape), _full(gq.shape), _full(gkv.shape),
                  _full(w_uq.shape), _full(w_ukv.shape)],
        out_specs=[row(nq * LANE, 0), row(nq * LANE, 0), row(MLA_VW, 0)],
        out_shape=[jax.ShapeDtypeStruct((t, nq * LANE), BF), jax.ShapeDtypeStruct((t, nq * LANE), BF),
                   jax.ShapeDtypeStruct((t, MLA_VW), BF)],
        name="mla_prep_fwd", compiler_params=pltpu.CompilerParams(dimension_semantics=("arbitrary",)),
    )(proj, proj, proj, pos_col, inv_freq2, gq, gkv, w_uq, w_ukv)


def _mla_prep_bwd(proj, pos_col, inv_freq2, gq, gkv, w_uq, w_ukv, dq, dk, dv, tm):
    t = proj.shape[0]
    nq = 2 * MLA_HEADS

    def body(cq_ref, ckv_ref, kr_ref, pos_ref, f_ref, gq_ref, gkv_ref, wq_ref, wkv_ref, dq_ref, dk_ref, dv_ref,
             dcq_ref, dckv_ref, dkr_ref, dgq_ref, dgkv_ref, dwq_ref, dwkv_ref):
        @pl.when(pl.program_id(0) == 0)
        def _():
            for o in (dgq_ref, dgkv_ref, dwq_ref, dwkv_ref):
                o[...] = jnp.zeros_like(o)

        cos, sin = _rope_tables(pos_ref[...], f_ref[...])
        f = functools.partial(_mla_prep, cos=cos, sin=sin)
        _, vjp = jax.vjp(f, cq_ref[...].astype(F32), ckv_ref[...].astype(F32), kr_ref[...].astype(F32), gq_ref[...], gkv_ref[...], wq_ref[...], wkv_ref[...])
        cts = [dq_ref[:, pl.ds(i * LANE, LANE)] for i in range(nq)]
        cts += [dk_ref[:, pl.ds(i * LANE, LANE)] for i in range(nq)]
        cts += [dv_ref[:, pl.ds(h * LANE, LANE)] for h in range(MLA_HEADS)]
        dcq, dckv, dkr, dgq, dgkv, dwq, dwkv = vjp(cts)
        dcq_ref[...] = dcq.astype(dcq_ref.dtype)
        dckv_ref[...] = dckv.astype(dckv_ref.dtype)
        dkr_ref[...] = dkr.astype(dkr_ref.dtype)
        dgq_ref[...] += dgq
        dgkv_ref[...] += dgkv
        dwq_ref[...] += dwq
        dwkv_ref[...] += dwkv

    row = lambda w, j: pl.BlockSpec((tm, w), functools.partial(lambda i, j: (i, j), j=j))
    return pl.pallas_call(
        body, grid=(t // tm,),
        in_specs=[row(Q_LORA, P_CQ // Q_LORA), row(KV_LORA, P_CKV // KV_LORA), row(LANE, P_KR // LANE),
                  pl.BlockSpec((tm, 1), lambda i: (i, 0)), _full(inv_freq2.shape), _full(gq.shape), _full(gkv.shape),
                  _full(w_uq.shape), _full(w_ukv.shape), row(nq * LANE, 0), row(nq * LANE, 0), row(MLA_VW, 0)],
        out_specs=[row(Q_LORA, 0), row(KV_LORA, 0), row(LANE, 0), _full(gq.shape), _full(gkv.shape),
                   _full(w_uq.shape), _full(w_ukv.shape)],
        out_shape=[jax.ShapeDtypeStruct((t, Q_LORA), BF), jax.ShapeDtypeStruct((t, KV_LORA), BF),
                   jax.ShapeDtypeStruct((t, LANE), BF), jax.ShapeDtypeStruct(gq.shape, F32),
                   jax.ShapeDtypeStruct(gkv.shape, F32), jax.ShapeDtypeStruct(w_uq.shape, F32),
                   jax.ShapeDtypeStruct(w_ukv.shape, F32)],
        name="mla_prep_bwd", compiler_params=pltpu.CompilerParams(dimension_semantics=("arbitrary",)),
    )(proj, proj, proj, pos_col, inv_freq2, gq, gkv, w_uq, w_ukv, dq, dk, dv)


_NEG = -1e30
_LN2 = math.log(2.0)
ATT_CHAINS = 2


def _causal(tq, tk, q0, k0):
    row = q0 + lax.broadcasted_iota(jnp.int32, (tq, tk), 0)
    col = k0 + lax.broadcasted_iota(jnp.int32, (tq, tk), 1)
    return col <= row


def _attn_fwd(q, k, v, tq, tk):
    t = q.shape[0]

    assert tk % tq == 0 or tq % tk == 0
    n_diag = max(1, tq // tk)

    th = tq // ATT_CHAINS

    def body(q_ref, k_ref, v_ref, o_ref, lse_ref):
        i = pl.program_id(1)
        n_full = (i * tq) // tk

        def step(k0, carry, masked):
            out = []
            for c, (m, l, acc) in enumerate(carry):
                kw = min(tk, (c + 1) * th) if masked and tk == tq else tk
                kt = k_ref[pl.ds(k0, kw), :]
                vt = v_ref[pl.ds(k0, kw), :]
                s = _dot(q_ref[pl.ds(c * th, th), :], kt, NT)
                if masked:
                    s = jnp.where(_causal(th, kw, i * tq + c * th, k0), s, _NEG)
                m_new = jnp.maximum(m, jnp.max(s, axis=-1, keepdims=True))
                p = jnp.exp2(s - m_new)
                alpha = jnp.exp2(m - m_new)
                out.append((m_new, alpha * l + jnp.sum(p, axis=-1, keepdims=True), alpha * acc + _dot(p.astype(BF), vt)))
            return tuple(out)

        init = tuple((jnp.full((th, 1), _NEG, F32), jnp.zeros((th, 1), F32), jnp.zeros((th, V_HEAD), F32)) for _ in range(ATT_CHAINS))
        carry = lax.fori_loop(0, n_full, lambda j, c: step(pl.multiple_of(j * tk, tk), c, False), init)
        for dd in range(n_diag):
            carry = step(pl.multiple_of((n_full + dd) * tk, tk), carry, True)
        for c, (m, l, acc) in enumerate(carry):
            o_ref[pl.ds(c * th, th), :] = acc / l
            lse_ref[pl.ds(c * th, th), :] = jnp.broadcast_to(m + jnp.log2(l), (th, LANE))

    return pl.pallas_call(
        body, grid=(MLA_HEADS, t // tq),
        in_specs=[pl.BlockSpec((tq, 2 * LANE), lambda h, i: (i, h)), pl.BlockSpec((t, 2 * LANE), lambda h, i: (0, h)),
                  pl.BlockSpec((t, V_HEAD), lambda h, i: (0, h))],
        out_specs=[pl.BlockSpec((tq, V_HEAD), lambda h, i: (i, h)), pl.BlockSpec((tq, LANE), lambda h, i: (i, h))],
        out_shape=[jax.ShapeDtypeStruct((t, MLA_VW), F32), jax.ShapeDtypeStruct((t, MLA_HEADS * LANE), F32)],
        name="attn_fwd", compiler_params=pltpu.CompilerParams(dimension_semantics=("parallel", "arbitrary")),
    )(q, k, v)


def _attn_bwd(q, k, v, do, lse, delta, tq, tk):
    t = q.shape[0]
    nkt = t // tk
    assert tk % tq == 0

    def body(q_ref, k_ref, v_ref, do_ref, lse_ref, dl_ref, dq_ref, dk_ref, dv_ref):
        j = pl.program_id(1)

        @pl.when(j == 0)
        def _():
            dq_ref[...] = jnp.zeros_like(dq_ref)

        kt = k_ref[...]
        vt = v_ref[...]

        def step(q0, carry, masked, kw=tk):
            dk, dv = carry
            rows = pl.ds(q0, tq)
            qt = q_ref[rows, :]
            dot_ = do_ref[rows, :]
            ktw, vtw = kt[:kw], vt[:kw]
            p = jnp.exp2(_dot(qt, ktw, NT) - lse_ref[rows, pl.ds(0, 1)])
            if masked:
                p = jnp.where(_causal(tq, kw, q0, j * tk), p, 0.0)
            dv_w = _dot(p.astype(BF), dot_, TN)
            ds = (p * (_dot(dot_, vtw, NT) - dl_ref[rows, pl.ds(0, 1)])).astype(BF)
            dk_w = _dot(ds, qt, TN)
            dq_ref[rows, :] += _dot(ds, ktw)
            if kw == tk:
                return dk + dk_w, dv + dv_w
            return (jnp.concatenate([dk[:kw] + dk_w, dk[kw:]], axis=0), jnp.concatenate([dv[:kw] + dv_w, dv[kw:]], axis=0))

        per = tk // tq
        carry = (jnp.zeros((tk, 2 * LANE), F32), jnp.zeros((tk, V_HEAD), F32))
        for dd in range(per):
            carry = step(pl.multiple_of(j * tk + dd * tq, tq), carry, True, kw=(dd + 1) * tq)

        def group(g, c):
            for dd in range(per):
                c = step(pl.multiple_of(g * tk + dd * tq, tq), c, False)
            return c

        dk, dv = lax.fori_loop(j + 1, nkt, group, carry)
        dk_ref[...] = dk * _LN2
        dv_ref[...] = dv

        @pl.when(j == nkt - 1)
        def _():
            dq_ref[...] = dq_ref[...] * _LN2

    return pl.pallas_call(
        body, grid=(MLA_HEADS, nkt),
        in_specs=[pl.BlockSpec((t, 2 * LANE), lambda h, j: (0, h)), pl.BlockSpec((tk, 2 * LANE), lambda h, j: (j, h)),
                  pl.BlockSpec((tk, V_HEAD), lambda h, j: (j, h)), pl.BlockSpec((t, V_HEAD), lambda h, j: (0, h)),
                  pl.BlockSpec((t, LANE), lambda h, j: (0, h)), pl.BlockSpec((t, LANE), lambda h, j: (0, h))],
        out_specs=[pl.BlockSpec((t, 2 * LANE), lambda h, j: (0, h)), pl.BlockSpec((tk, 2 * LANE), lambda h, j: (j, h)),
                   pl.BlockSpec((tk, V_HEAD), lambda h, j: (j, h))],
        out_shape=[jax.ShapeDtypeStruct((t, MLA_HEADS * 2 * LANE), F32), jax.ShapeDtypeStruct((t, MLA_HEADS * 2 * LANE), F32),
                   jax.ShapeDtypeStruct((t, MLA_VW), F32)],
        name="attn_bwd", compiler_params=pltpu.CompilerParams(dimension_semantics=("parallel", "arbitrary")),
    )(q, k, v, do, lse, delta)


def _adam_update(w, g, m, v):
    mm = ADAM_B1 * m + (1.0 - ADAM_B1) * g
    vv = ADAM_B2 * v + (1.0 - ADAM_B2) * jnp.square(g)
    m_hat = mm / (1.0 - ADAM_B1 ** ADAM_STEP)
    v_hat = vv / (1.0 - ADAM_B2 ** ADAM_STEP)
    return -ADAM_LR * (m_hat / (jnp.sqrt(v_hat) + ADAM_EPS) + ADAM_WD * w), mm, vv


def _adamw(w, g, m, v, name):
    r, c = w.shape
    tr = max([r // s for s in range(1, r // 8 + 1) if r % s == 0 and (r // s) % 8 == 0 and r // s <= 512] or [r])
    slots = g.ndim == 3

    def body(w_ref, g_ref, m_ref, v_ref, g_out, d_ref, nm_ref, nv_ref):
        if slots:
            gg = g_ref[0].astype(F32)
            for s in range(1, N_DEV):
                gg = gg + g_ref[s].astype(F32)
        else:
            gg = g_ref[...]
        g_out[...] = gg
        d_ref[...], nm_ref[...], nv_ref[...] = _adam_update(w_ref[...], gg, m_ref[...], v_ref[...])

    spec = pl.BlockSpec((tr, c), lambda i: (i, 0))
    g_spec = pl.BlockSpec((N_DEV, tr, c), lambda i: (0, i, 0)) if slots else spec
    return pl.pallas_call(
        body, grid=(r // tr,), in_specs=[spec, g_spec, spec, spec], out_specs=[spec] * 4,
        out_shape=[jax.ShapeDtypeStruct((r, c), F32)] * 4, name=name,
        compiler_params=pltpu.CompilerParams(dimension_semantics=("arbitrary",)),
    )(w, g, m, v)


def _adamw_many(ws, gs, ms, vs, name):
    n = len(ws)

    def body(*refs):
        for i in range(n):
            w_ref, g_ref, m_ref, v_ref = (refs[j * n + i] for j in range(4))
            d_ref, nm_ref, nv_ref = (refs[(4 + j) * n + i] for j in range(3))
            d_ref[...], nm_ref[...], nv_ref[...] = _adam_update(w_ref[...], g_ref[...], m_ref[...], v_ref[...])

    shapes = [jax.ShapeDtypeStruct(w.shape, F32) for w in ws]
    outs = pl.pallas_call(body, out_shape=shapes * 3, name=name)(*ws, *gs, *ms, *vs)
    return outs[:n], outs[n:2 * n], outs[2 * n:]


def _cast_bf16(xs, name, after=None):
    n = len(xs)
    extra = [] if after is None else [after]

    def body(*refs):
        outs = refs[n + len(extra):]
        for i in range(n):
            outs[i][...] = refs[i][...].astype(BF)

    vmem = pl.BlockSpec(memory_space=pltpu.VMEM)
    return pl.pallas_call(
        body, out_shape=[jax.ShapeDtypeStruct(x.shape, BF) for x in xs], name=name,
        in_specs=[vmem] * n + [pl.BlockSpec(memory_space=pl.ANY)] * len(extra), out_specs=[vmem] * n)(*xs, *extra)


def _pad_rows(a, n):
    return jnp.pad(a, ((0, n - a.shape[0]), (0, 0)))


def _w_in_to_padded(wt):
    s_ba = P_CQ
    s_cq = s_ba + 2 * DN_HEADS
    s_kr = s_cq + Q_LORA + KV_LORA
    return jnp.concatenate([wt[:s_ba], wt[s_cq:s_kr], _pad_rows(wt[s_ba:s_cq], LANE), _pad_rows(wt[s_kr:], LANE)], axis=0)


def _w_in_from_padded(wt):
    return jnp.concatenate([wt[:P_CQ], wt[P_BA:P_BA + 2 * DN_HEADS], wt[P_CQ:P_BA], wt[P_KR:P_KR + QK_ROPE]], axis=0)


def _w_uq_to_padded(wt):
    w3 = wt.reshape(MLA_HEADS, QK_NOPE + QK_ROPE, Q_LORA)
    nope = w3[:, :QK_NOPE].reshape(MLA_HEADS * QK_NOPE, Q_LORA)
    rope = jnp.pad(w3[:, QK_NOPE:], ((0, 0), (0, LANE - QK_ROPE), (0, 0))).reshape(MLA_HEADS * LANE, Q_LORA)
    return jnp.concatenate([nope, rope], axis=0)


def _w_uq_from_padded(wt):
    nope = wt[:MLA_HEADS * QK_NOPE].reshape(MLA_HEADS, QK_NOPE, Q_LORA)
    rope = wt[MLA_HEADS * QK_NOPE:].reshape(MLA_HEADS, LANE, Q_LORA)[:, :QK_ROPE]
    return jnp.concatenate([nope, rope], axis=1).reshape(MLA_HEADS * (QK_NOPE + QK_ROPE), Q_LORA)


def _pack(pieces, width, row_mult):
    flat = jnp.concatenate([p.reshape(-1) for p in pieces])
    n = flat.shape[0]
    rows = -(-n // (width * row_mult)) * row_mult
    return jnp.pad(flat, (0, rows * width - n)).reshape(rows, width)


def _unpack(flat, shapes):
    out, o = [], 0
    for s in shapes:
        n = math.prod(s)
        out.append(flat[o:o + n].reshape(s))
        o += n
    return out


def kernel(x, c, positions, w_ada, b_ada, w_in, conv_w, a_log, dt_bias, dn_norm_g, q_norm_g, w_uq, kv_norm_g, w_ukv, w_o, ln1_g, ln1_b, w_gate, w_up, w_down, ln2_g, ln2_b, loss_target, m_w_ada, m_b_ada, m_w_in, m_conv_w, m_a_log, m_dt_bias, m_dn_norm_g, m_q_norm_g, m_w_uq, m_kv_norm_g, m_w_ukv, m_w_o, m_ln1_g, m_ln1_b, m_w_gate, m_w_up, m_w_down, m_ln2_g, m_ln2_b, v_w_ada, v_b_ada, v_w_in, v_conv_w, v_a_log, v_dt_bias, v_dn_norm_g, v_q_norm_g, v_w_uq, v_kv_norm_g, v_w_ukv, v_w_o, v_ln1_g, v_ln1_b, v_w_gate, v_w_up, v_w_down, v_ln2_g, v_ln2_b):
    me = 4 * lax.axis_index("x") + 2 * lax.axis_index("y") + lax.axis_index("c")
    t, d = x.shape[1], x.shape[2]
    ada_n = w_ada.shape[2]

    tr = lambda w: w[0].T
    rows = lambda a: a.reshape(-1, a.shape[2])
    (in_shard,) = _cast_bf16([tr(w_in)], "cast_w_in")
    cw = conv_w.shape[3]
    a_in, c_all, conv_all = _gather_by_chip([in_shard, c, conv_w[0, :, 0, :]], "gather_w_in_and_small")
    c_all = c_all.reshape(N_DEV, d)
    conv_full = conv_all.transpose(1, 0, 2).reshape(CONV_K, N_DEV * cw)
    conv_w8 = jnp.pad(conv_full, ((0, 8 - CONV_K), (0, 0)))

    b_ada_mine = lax.dynamic_slice(b_ada, (0, me * ada_n), (1, ada_n))
    mod_cols = _mod_fwd(c_all, w_ada[0], b_ada_mine)
    (mod_all,) = _exchange([mod_cols.reshape(N_DEV, 1, ada_n)], "scatter_mod", scatter=True)
    mod = mod_all.reshape(1, N_DEV * ada_n)

    later = _cast_bf16([tr(w_uq), tr(w_ukv), w_o[0], tr(w_gate), tr(w_up), w_down[0]], "cast_weights", after=mod)
    mixer_gather, token_a = _exchange_start(later[:3], "gather_mixer_weights_start", scatter=False)
    ffn_gather, token_b = _exchange_start(later[3:], "gather_ffn_weights_start", scatter=False)
    mod = mod + (token_a + token_b)
    w_in_t = _w_in_to_padded(rows(a_in))

    def mixer_weights(after):
        a_uq, a_ukv, a_o = _exchange_wait(mixer_gather, after, "gather_mixer_weights_wait", scatter=False)
        return _w_uq_to_padded(rows(a_uq)), rows(a_ukv), rows(a_o)

    def ffn_weights(after):
        a_gate, a_up, a_down = _exchange_wait(ffn_gather, after, "gather_ffn_weights_wait", scatter=False)
        return rows(a_gate), rows(a_up), rows(a_down)

    def by_dest(g):
        return g.reshape(N_DEV, -1, g.shape[1])

    scatters = {}

    def grads_ready(tag, *g):
        if tag == "ffn":
            pieces = [by_dest(a) for a in g]
        elif tag == "mixer":
            g_w_o, g_w_uq_t, g_w_ukv_t = g
            pieces = [by_dest(g_w_o), by_dest(_w_uq_from_padded(g_w_uq_t).astype(BF)), by_dest(g_w_ukv_t.astype(BF))]
        else:
            pieces = [by_dest(_w_in_from_padded(g[0]))]
        scatters[tag], token = _exchange_start(pieces, "scatter_%s_grads_start" % tag, scatter=True)
        return token

    loc = _local_step(x[0], loss_target[0], positions[0], mod, w_in_t, mixer_weights, ffn_weights, grads_ready,
                      conv_w8, a_log, dt_bias, dn_norm_g, q_norm_g, kv_norm_g, ln1_g, ln1_b, ln2_g, ln2_b)
    grad_x, loss_acc, dmod, d_conv8, d_al8, d_dt8, d_dn_g, d_q_g, d_kv_g, d_ln1_g, d_ln1_b, d_ln2_g, d_ln2_b = loc

    small_shapes = [(6 * d,), (CONV_K, N_DEV * cw), (DN_HEADS,), (DN_HEADS,), (DN_DV,), (Q_LORA,), (KV_LORA,), (d,), (d,), (d,), (d,), (1,)]
    gsmall = _pack([dmod, d_conv8[:CONV_K], d_al8[0, :DN_HEADS], d_dt8[0, :DN_HEADS], d_dn_g, d_q_g, d_kv_g,
                    d_ln1_g, d_ln1_b, d_ln2_g, d_ln2_b, loss_acc[0, :1]], LANE, 8)
    small_gather, _ = _exchange_start([gsmall], "gather_small_grads_start", scatter=False)

    def small_sums(after):
        (gsmall_all,) = _exchange_wait(small_gather, after, "gather_small_grads_wait", scatter=False)
        dmod_all = gsmall_all.reshape(N_DEV, -1)[:, :6 * d]
        tot = _unpack(_sum_slots(gsmall_all, "sum_small_grads").reshape(-1), small_shapes)
        g_b_ada, g_conv_full, g_a_log, g_dt_bias, g_dn_g, g_q_g, g_kv_g, g_ln1_g, g_ln1_b, g_ln2_g, g_ln2_b, loss1 = tot
        g_conv_w = lax.dynamic_slice(g_conv_full, (0, me * cw), (CONV_K, cw))
        g_w_ada = _mod_bwd(c_all.T, lax.dynamic_slice(dmod_all, (0, me * ada_n), (N_DEV, ada_n)))
        return loss1.reshape(()), {
            "w_ada": g_w_ada[None], "b_ada": g_b_ada[None], "conv_w": g_conv_w[None, :, None, :],
            "a_log": g_a_log[None], "dt_bias": g_dt_bias[None], "dn_norm_g": g_dn_g[None], "q_norm_g": g_q_g[None],
            "kv_norm_g": g_kv_g[None], "ln1_g": g_ln1_g[None], "ln1_b": g_ln1_b[None], "ln2_g": g_ln2_g[None], "ln2_b": g_ln2_b[None]}

    grads = {}
    weights = dict(w_ada=w_ada, b_ada=b_ada, w_in=w_in, conv_w=conv_w, a_log=a_log, dt_bias=dt_bias, dn_norm_g=dn_norm_g,
                   q_norm_g=q_norm_g, w_uq=w_uq, kv_norm_g=kv_norm_g, w_ukv=w_ukv, w_o=w_o, ln1_g=ln1_g, ln1_b=ln1_b,
                   w_gate=w_gate, w_up=w_up, w_down=w_down, ln2_g=ln2_g, ln2_b=ln2_b)
    ms = dict(w_ada=m_w_ada, b_ada=m_b_ada, w_in=m_w_in, conv_w=m_conv_w, a_log=m_a_log, dt_bias=m_dt_bias,
              dn_norm_g=m_dn_norm_g, q_norm_g=m_q_norm_g, w_uq=m_w_uq, kv_norm_g=m_kv_norm_g, w_ukv=m_w_ukv, w_o=m_w_o,
              ln1_g=m_ln1_g, ln1_b=m_ln1_b, w_gate=m_w_gate, w_up=m_w_up, w_down=m_w_down, ln2_g=m_ln2_g, ln2_b=m_ln2_b)
    vs = dict(w_ada=v_w_ada, b_ada=v_b_ada, w_in=v_w_in, conv_w=v_conv_w, a_log=v_a_log, dt_bias=v_dt_bias,
              dn_norm_g=v_dn_norm_g, q_norm_g=v_q_norm_g, w_uq=v_w_uq, kv_norm_g=v_kv_norm_g, w_ukv=v_w_ukv, w_o=v_w_o,
              ln1_g=v_ln1_g, ln1_b=v_ln1_b, w_gate=v_w_gate, w_up=v_w_up, w_down=v_w_down, ln2_g=v_ln2_g, ln2_b=v_ln2_b)
    names = list(weights)
    big = ("w_gate", "w_up", "w_down", "w_o", "w_uq", "w_ukv", "w_ada", "w_in")
    waits = {"w_gate": ("ffn", ("w_gate", "w_up", "w_down")), "w_o": ("mixer", ("w_o", "w_uq", "w_ukv")), "w_in": ("in", ("w_in",))}
    delta_w, new_m, new_v, slots = {}, {}, {}, {}
    last = small_gather[2][0]
    for n in big:
        if n == "w_ada":
            loss, small_grads = small_sums(last)
            grads.update(small_grads)
        if n == "w_in":
            rest = [r for r in names if r not in big]
            flat2 = lambda a: a.reshape(-1, a.shape[-1])
            outs = _adamw_many(*[[flat2(src[r]) for r in rest] for src in (weights, grads, ms, vs)], "adamw_small")
            for dst, o in zip((delta_w, new_m, new_v), outs):
                for r, a in zip(rest, o):
                    dst[r] = a.reshape(weights[r].shape)
            last = outs[0][0]
        transposed = n in ("w_in", "w_uq", "w_ukv", "w_gate", "w_up")
        two = (lambda a: a[0].T) if transposed else (lambda a: a[0])
        back = (lambda a: a.T[None]) if transposed else (lambda a: a[None])
        if n in waits:
            tag, members = waits[n]
            slots.update(zip(members, _exchange_wait(scatters[tag], last, "scatter_%s_grads_wait" % tag, scatter=True)))
        g_in = slots[n] if n in slots else two(grads[n])
        gr, dlt, nm, nv = _adamw(two(weights[n]), g_in, two(ms[n]), two(vs[n]), "adamw_" + n)
        grads[n], delta_w[n], new_m[n], new_v[n] = back(gr), back(dlt), back(nm), back(nv)
        last = nv

    return (loss, grad_x[None], *[grads[n] for n in names], *[delta_w[n] for n in names],
            *[new_m[n] for n in names], *[new_v[n] for n in names])


def _local_step(xs, tgt, pos, mod, w_in_t, mixer_weights, ffn_weights, grads_ready, conv_w8,
                a_log, dt_bias, dn_norm_g, q_norm_g, kv_norm_g, ln1_g, ln1_b, ln2_g, ln2_b):
    t, d = xs.shape
    sh_m, sc_m, gt_m, sh_f, sc_f, gt_f = [mod[:, i * d:(i + 1) * d] for i in range(6)]
    pos_col = pos.astype(F32).reshape(t, 1)
    inv_freq = 1.0 / (ROPE_THETA ** (jnp.arange(0, QK_ROPE, 2, dtype=F32) / QK_ROPE))
    inv_freq2 = jnp.pad(jnp.concatenate([inv_freq, inv_freq]), (0, LANE - QK_ROPE)).reshape(1, LANE)
    al8 = jnp.pad(a_log, ((0, 7), (0, LANE - DN_HEADS)))
    dt8 = jnp.pad(dt_bias, ((0, 7), (0, LANE - DN_HEADS)))

    tm = min(512, t)
    tq = min(256, t)
    tk = min(512, t)

    def modulate_in(xx, sc, sh, w_ba):
        h = (xx * (1.0 + sc) + sh).astype(BF)
        return h, _dot(h, w_ba, NT)

    h1, ba_raw = _rowwise("modulate_in", modulate_in, [xs], [sc_m, sh_m, w_in_t[P_BA:P_BA + LANE]], [(d, BF), (LANE, F32)], [], tm)
    proj = _matmul(h1, w_in_t, "nt", "in_proj", BF)
    qkv = _conv_fwd(proj, conv_w8, min(256, t))
    gdn_tm = min(512, t)
    intra, inverses = _gdn_intra_fwd(qkv, ba_raw, al8, dt8, gdn_tm)
    o_dn, states = _gdn_scan_fwd(intra, gdn_tm)
    w_uq_t, w_ukv_t, w_o_f = mixer_weights(states)
    qc, kc, vc = _mla_prep_fwd(proj, pos_col, inv_freq2, q_norm_g, kv_norm_g, w_uq_t, w_ukv_t, tm)
    o_mla, lse = _attn_fwd(qc, kc, vc, min(1024, t), min(1024, t))

    def mix_in(o, z, om, g):
        return jnp.concatenate(_gdn_out(o, z.astype(F32), g) + [om], axis=1)

    (mixin,) = _rowwise("mixer_out", mix_in, [o_dn, (proj, DN_VW, P_Z // DN_VW), o_mla], [dn_norm_g], [(2 * DN_VW, BF)], [], tm)
    mix = _matmul(mixin, w_o_f, "nn", "out_proj", BF)

    def block1(xx, mx, gt, g1, b1, sc, sh):
        x1 = _layernorm(DEEPNORM_ALPHA * xx + gt * mx, g1, b1)
        return x1, x1 * (1.0 + sc) + sh

    x1, h2 = _rowwise("norm1_modulate", block1, [xs, mix], [gt_m, ln1_g, ln1_b, sc_f, sh_f], [(d, F32), (d, BF)], [], tm)
    w_gate_f, w_up_f, w_down_f = ffn_weights(h2)
    act, act_dg, act_du = _ffn_in(h2, w_gate_f, w_up_f)
    ff = _matmul(act, w_down_f, "nn", "ffn_out", BF)

    def tail_loss(x1_, ff_, gt, g2, b2, tg):
        y = _layernorm(DEEPNORM_ALPHA * x1_ + gt * ff_, g2, b2)
        return 0.5 * jnp.sum(jnp.mean(jnp.square(y - tg), axis=-1))

    def tail(x1_, ff_, tg, gt, g2, b2):
        loss, (dx1, dff, dgt, dg2, db2) = jax.value_and_grad(tail_loss, argnums=(0, 1, 2, 3, 4))(x1_, ff_, gt, g2, b2, tg)
        return dx1, dff, jnp.full((1, LANE), loss, F32), dgt, dg2, db2

    dx1_a, dff, loss_acc, d_gt_f, d_ln2_g, d_ln2_b = _rowwise(
        "norm2_loss", tail, [x1, ff, tgt], [gt_f, ln2_g, ln2_b], [(d, BF), (d, BF)], [(1, LANE), (1, d), (1, d), (1, d)], tm)

    g_w_down = _matmul(act, dff, "tn", "d_w_down", BF)
    dgate, dup = _ffn_act_bwd(dff, w_down_f, act_dg, act_du)
    g_w_gate, g_w_up = _matmul_tn_pair(dgate, dup, h2, "d_w_gate_up", BF)
    token = grads_ready("ffn", g_w_gate, g_w_up, g_w_down)
    dh2 = _matmul2_nn(dgate, w_gate_f, dup, w_up_f, "d_ffn_in", BF)

    def block1_bwd(xx, mx, dx1_, dh2_, gt, g1, b1, sc, sh):
        _, vjp = jax.vjp(block1, xx, mx, gt, g1, b1, sc, sh)
        dxx, dmx, dgt, dg1, db1, dsc, dsh = vjp((dx1_.astype(F32), dh2_.astype(F32)))
        return dxx, dmx, dgt, dg1, db1, dsc, dsh

    dx_a, dmix, d_gt_m, d_ln1_g, d_ln1_b, d_sc_f, d_sh_f = _rowwise(
        "norm1_modulate_bwd", block1_bwd, [xs, mix, dx1_a, dh2], [gt_m + token, ln1_g, ln1_b, sc_f, sh_f],
        [(d, F32), (d, BF)], [(1, d)] * 5, tm)

    dmixin = _matmul(dmix, w_o_f, "nt", "d_mixer_out", BF)
    g_w_o = _matmul(mixin, dmix, "tn", "d_w_o", BF)

    def mixer_bwd(o, z, om, dmi, g):
        _, vjp = jax.vjp(lambda o_, z_, g_: jnp.concatenate(_gdn_out(o_, z_, g_), axis=1), o, z.astype(F32), g)
        do_, dz_, dg_ = vjp(dmi[:, :DN_VW].astype(F32))
        dom = dmi[:, DN_VW:]
        delta = [jnp.broadcast_to(jnp.sum(dom[:, h * V_HEAD:(h + 1) * V_HEAD] * om[:, h * V_HEAD:(h + 1) * V_HEAD], axis=-1, keepdims=True), (o.shape[0], LANE))
                 for h in range(MLA_HEADS)]
        return do_, dz_, dom, jnp.concatenate(delta, axis=1), dg_

    do_dn, dz, do_mla, delta, d_dn_g = _rowwise(
        "mixer_out_bwd", mixer_bwd, [o_dn, (proj, DN_VW, P_Z // DN_VW), o_mla, dmixin], [dn_norm_g],
        [(DN_VW, F32), (DN_VW, BF), (MLA_VW, BF), (MLA_HEADS * LANE, F32)], [(1, DN_DV)], tm)

    dqc, dkc, dvc = _attn_bwd(qc, kc, vc, do_mla, lse, delta, min(512, t), min(1024, t))
    dcq, dckv, dkr, d_q_g, d_kv_g, g_w_uq_t, g_w_ukv_t = _mla_prep_bwd(
        proj, pos_col, inv_freq2, q_norm_g, kv_norm_g, w_uq_t, w_ukv_t, dqc, dkc, dvc, tm)

    token = grads_ready("mixer", g_w_o, g_w_uq_t, g_w_ukv_t)

    d_intra = _gdn_scan_bwd(intra, states, do_dn, gdn_tm)
    dqkv_act, dba, d_al8, d_dt8 = _gdn_intra_bwd(qkv, ba_raw, al8 + token, dt8, inverses, d_intra, min(256, t))
    dproj, d_conv8 = _conv_bwd(proj, conv_w8, dqkv_act, [(P_Z, dz), (P_CQ, dcq), (P_CKV, dckv), (P_BA, dba), (P_KR, dkr)],
                               min(256, t))
    dh1 = _matmul(dproj, w_in_t, "nn", "d_in_proj", BF)
    g_w_in_t = _matmul(dproj, h1, "tn", "d_w_in", BF)
    token = grads_ready("in", g_w_in_t)

    def modulate_bwd(xx, dh, dxa, sc):
        dh = dh.astype(F32)
        return dh * (1.0 + sc) + dxa, jnp.sum(dh * xx, axis=0, keepdims=True), jnp.sum(dh, axis=0, keepdims=True)

    grad_x, d_sc_m, d_sh_m = _rowwise("modulate_in_bwd", modulate_bwd, [xs, dh1, dx_a], [sc_m + token], [(d, F32)], [(1, d), (1, d)], tm)
    dmod = jnp.concatenate([d_sh_m, d_sc_m, d_gt_m, d_sh_f, d_sc_f, d_gt_f], axis=1)
    return grad_x, loss_acc, dmod, d_conv8, d_al8, d_dt8, d_dn_g, d_q_g, d_kv_g, d_ln1_g, d_ln1_b, d_ln2_g, d_ln2_b
```

```python
import functools
import math

import jax
import jax.numpy as jnp
from jax import lax
from jax.experimental import pallas as pl
from jax.experimental.pallas import tpu as pltpu

F32 = jnp.float32
BF = jnp.bfloat16
HI = lax.Precision.HIGHEST

N_DEV = 8
DN_HEADS = 4
DN_DK = 128
DN_DV = 128
CONV_K = 4
CHUNK = 64
MLA_HEADS = 4
QK_NOPE = 128
QK_ROPE = 64
V_HEAD = 128
Q_LORA = 512
KV_LORA = 256
ROPE_THETA = 10000.0
DEPTH = 1
DEEPNORM_ALPHA = (2.0 * DEPTH) ** 0.25
LANE = 128
CONV_HALO = 8
GL_ROWS = 8
CONV_ROWS, CONV_COLS = 64, 256

DN_QK = DN_HEADS * DN_DK
DN_VW = DN_HEADS * DN_DV
DN_CONV_CH = 2 * DN_QK + DN_VW
MLA_VW = MLA_HEADS * V_HEAD
P_Z = DN_CONV_CH
P_CQ = P_Z + DN_VW
P_CKV = P_CQ + Q_LORA
P_BA = P_CKV + KV_LORA
P_KR = P_BA + LANE
N_INP = P_KR + LANE

ADAM_LR = 0.001
ADAM_B1 = 0.9
ADAM_B2 = 0.999
ADAM_EPS = 1e-08
ADAM_WD = 0.01
ADAM_STEP = 10

NN = (((1,), (0,)), ((), ()))
NT = (((1,), (1,)), ((), ()))
TN = (((0,), (0,)), ((), ()))


def _pick(n, prefs):
    for p in prefs:
        if n % p == 0:
            return p
    return n


def _full(shape):
    return pl.BlockSpec(shape, lambda *_: (0,) * len(shape))


def _dot(a, b, dims=NN):
    return lax.dot_general(a, b, dims, preferred_element_type=F32)


def _doth(a, b, dims=NN):
    return lax.dot_general(a, b, dims, precision=HI, preferred_element_type=F32)


@jax.custom_vjp
def _mmb(a, b):
    return _dot(a.astype(BF), b.astype(BF), NN)


def _mmb_fwd(a, b):
    return _mmb(a, b), (a, b)


def _mmb_bwd(res, g):
    a, b = res
    gb = g.astype(BF)
    return (_dot(gb, b.astype(BF), NT).astype(a.dtype), _dot(a.astype(BF), gb, TN).astype(b.dtype))


_mmb.defvjp(_mmb_fwd, _mmb_bwd)


@jax.custom_vjp
def _mmb_nt(a, b):
    return _dot(a.astype(BF), b.astype(BF), NT)


def _mmb_nt_fwd(a, b):
    return _mmb_nt(a, b), (a, b)


def _mmb_nt_bwd(res, g):
    a, b = res
    gb = g.astype(BF)
    return (_dot(gb, b.astype(BF), NN).astype(a.dtype), _dot(gb, a.astype(BF), TN).astype(b.dtype))


_mmb_nt.defvjp(_mmb_nt_fwd, _mmb_nt_bwd)


def _sigmoid(x):
    return 0.5 * (jnp.tanh(0.5 * x) + 1.0)


def _silu(x):
    return x * _sigmoid(x)


def _softplus(x):
    return jnp.maximum(x, 0.0) + jnp.log(1.0 + jnp.exp(-jnp.abs(x)))


def _layernorm(x, g, b, eps=1e-5):
    mu = jnp.mean(x, axis=-1, keepdims=True)
    xc = x - mu
    var = jnp.mean(xc * xc, axis=-1, keepdims=True)
    return xc * lax.rsqrt(var + eps) * g + b


def _rmsnorm(x, g, eps=1e-6):
    return x * lax.rsqrt(jnp.mean(x * x, axis=-1, keepdims=True) + eps) * g


def _l2norm(x, eps=1e-6):
    return x * lax.rsqrt(jnp.sum(x * x, axis=-1, keepdims=True) + eps)


def _rowwise(name, fn, rows, vecs, out_rows, out_accs, tm):
    rows = [r if isinstance(r, tuple) else (r, r.shape[1], 0) for r in rows]
    t = rows[0][0].shape[0]
    tm = min(tm, t)
    assert t % tm == 0
    nr, nv, no = len(rows), len(vecs), len(out_rows)

    def body(*refs):
        ins = [r[...] for r in refs[:nr + nv]]
        outs = fn(*ins)
        outs = outs if isinstance(outs, (tuple, list)) else (outs,)
        o_rows = refs[nr + nv:nr + nv + no]
        o_accs = refs[nr + nv + no:]
        for o, val in zip(o_rows, outs[:no]):
            o[...] = val.astype(o.dtype)
        if o_accs:
            @pl.when(pl.program_id(0) == 0)
            def _():
                for o in o_accs:
                    o[...] = jnp.zeros_like(o)
            for o, val in zip(o_accs, outs[no:]):
                o[...] += val

    in_specs = [pl.BlockSpec((tm, w), functools.partial(lambda i, j: (i, j), j=j)) for (_, w, j) in rows]
    in_specs += [_full(v.shape) for v in vecs]
    out_specs = [pl.BlockSpec((tm, w), lambda i: (i, 0)) for (w, _) in out_rows]
    out_specs += [_full(s) for s in out_accs]
    out_shape = [jax.ShapeDtypeStruct((t, w), d) for (w, d) in out_rows]
    out_shape += [jax.ShapeDtypeStruct(s, F32) for s in out_accs]
    res = pl.pallas_call(
        body, grid=(t // tm,), in_specs=in_specs, out_specs=out_specs, out_shape=out_shape, name=name,
        compiler_params=pltpu.CompilerParams(dimension_semantics=("arbitrary",)),
    )(*[r[0] for r in rows], *vecs)
    return res


def _matmul(a, b, mode, name, out_dtype=F32):
    if mode == "nn":
        (m, k), n = a.shape, b.shape[1]
    elif mode == "nt":
        (m, k), n = a.shape, b.shape[0]
    else:
        (k, m), n = a.shape, b.shape[1]
    tm, tn, tk = _matmul_tiles(m, n, k, a.dtype.itemsize, b.dtype.itemsize, jnp.dtype(out_dtype).itemsize)
    nk = k // tk
    dims = {"nn": NN, "nt": NT, "tn": TN}[mode]

    def body(a_ref, b_ref, o_ref, *acc):
        part = _dot(a_ref[...].astype(BF), b_ref[...].astype(BF), dims)
        if nk == 1:
            o_ref[...] = part.astype(o_ref.dtype)
            return
        (acc_ref,) = acc
        kk = pl.program_id(2)

        @pl.when(kk == 0)
        def _():
            acc_ref[...] = part

        @pl.when(kk > 0)
        def _():
            acc_ref[...] += part

        @pl.when(kk == nk - 1)
        def _():
            o_ref[...] = acc_ref[...].astype(o_ref.dtype)

    a_spec = pl.BlockSpec((tk, tm), lambda i, j, kk: (kk, i)) if mode == "tn" else pl.BlockSpec((tm, tk), lambda i, j, kk: (i, kk))
    b_spec = pl.BlockSpec((tn, tk), lambda i, j, kk: (j, kk)) if mode == "nt" else pl.BlockSpec((tk, tn), lambda i, j, kk: (kk, j))
    return pl.pallas_call(
        body, grid=(m // tm, n // tn, nk), in_specs=[a_spec, b_spec],
        out_specs=pl.BlockSpec((tm, tn), lambda i, j, kk: (i, j)),
        out_shape=jax.ShapeDtypeStruct((m, n), out_dtype),
        scratch_shapes=[pltpu.VMEM((tm, tn), F32)] if nk > 1 else [], name=name,
        compiler_params=pltpu.CompilerParams(dimension_semantics=("parallel", "parallel", "arbitrary")),
    )(a, b)


def _lane_tile(n, cap):
    return max([n // s for s in range(1, n // LANE + 1) if n % s == 0 and (n // s) % LANE == 0 and n // s <= cap] or [n])


def _ffn_in(h, w_gate, w_up):
    m, k = h.shape
    f = w_gate.shape[0]
    tm, tn = _pick(m, (1024, 512, 256, 128)), _lane_tile(f, 1408)

    def body(h_ref, wg_ref, wu_ref, act_ref, dg_ref, du_ref):
        hh = h_ref[...]
        g = _dot(hh, wg_ref[...], NT)
        u = _dot(hh, wu_ref[...], NT)
        sg = _sigmoid(g)
        silu_g = g * sg
        act_ref[...] = (silu_g * u).astype(act_ref.dtype)
        dg_ref[...] = (u * (sg + silu_g * (1.0 - sg))).astype(dg_ref.dtype)
        du_ref[...] = silu_g.astype(du_ref.dtype)

    w_spec = pl.BlockSpec((tn, k), lambda i, j: (j, 0))
    o_spec = pl.BlockSpec((tm, tn), lambda i, j: (i, j))
    return pl.pallas_call(
        body, grid=(m // tm, f // tn), in_specs=[pl.BlockSpec((tm, k), lambda i, j: (i, 0)), w_spec, w_spec],
        out_specs=[o_spec] * 3, out_shape=[jax.ShapeDtypeStruct((m, f), BF)] * 3, name="ffn_in",
        compiler_params=pltpu.CompilerParams(dimension_semantics=("parallel", "parallel")),
    )(h, w_gate, w_up)


FFN_BWD_VMEM = 44 * 1024 * 1024


def _ffn_bwd(dff, w_down, act_dg, act_du, w_gate, w_up):
    m, k = dff.shape
    f = w_down.shape[0]
    tm, tn = _pick(m, (512, 256, 128)), _lane_tile(f, 1408)
    nj = f // tn

    def body(d_ref, wd_ref, fg_ref, fu_ref, wg_ref, wu_ref, dg_ref, du_ref, dh_ref, acc_ref):
        j = pl.program_id(1)
        da = _dot(d_ref[...], wd_ref[...], NT)
        dg = (da * fg_ref[...].astype(F32)).astype(BF)
        du = (da * fu_ref[...].astype(F32)).astype(BF)
        dg_ref[...] = dg
        du_ref[...] = du
        part = _dot(dg, wg_ref[...]) + _dot(du, wu_ref[...])

        @pl.when(j == 0)
        def _():
            acc_ref[...] = part

        @pl.when(j > 0)
        def _():
            acc_ref[...] += part

        @pl.when(j == nj - 1)
        def _():
            dh_ref[...] = acc_ref[...].astype(dh_ref.dtype)

    o_spec = pl.BlockSpec((tm, tn), lambda i, j: (i, j))
    w_spec = pl.BlockSpec((tn, k), lambda i, j: (j, 0))
    row_spec = pl.BlockSpec((tm, k), lambda i, j: (i, 0))
    return pl.pallas_call(
        body, grid=(m // tm, nj), in_specs=[row_spec, w_spec, o_spec, o_spec, w_spec, w_spec],
        out_specs=[o_spec, o_spec, row_spec],
        out_shape=[jax.ShapeDtypeStruct((m, f), BF)] * 2 + [jax.ShapeDtypeStruct((m, k), BF)],
        scratch_shapes=[pltpu.VMEM((tm, k), F32)], name="d_ffn",
        compiler_params=pltpu.CompilerParams(dimension_semantics=("parallel", "arbitrary"), vmem_limit_bytes=FFN_BWD_VMEM),
    )(dff, w_down, act_dg, act_du, w_gate, w_up)


def _matmul_tn_pair(a1, a2, b, name, out_dtype):
    k, m = a1.shape
    n = b.shape[1]
    tm = _lane_tile(m, 256)

    def body(a1_ref, a2_ref, b_ref, o1_ref, o2_ref):
        out = _dot(jnp.concatenate([a1_ref[...], a2_ref[...]], axis=1), b_ref[...], TN)
        o1_ref[...] = out[:tm].astype(o1_ref.dtype)
        o2_ref[...] = out[tm:].astype(o2_ref.dtype)

    a_spec = pl.BlockSpec((k, tm), lambda i: (0, i))
    o_spec = pl.BlockSpec((tm, n), lambda i: (i, 0))
    return pl.pallas_call(
        body, grid=(m // tm,), in_specs=[a_spec, a_spec, _full(b.shape)], out_specs=[o_spec] * 2,
        out_shape=[jax.ShapeDtypeStruct((m, n), out_dtype)] * 2, name=name,
        compiler_params=pltpu.CompilerParams(dimension_semantics=("parallel",)),
    )(a1, a2, b)


MATMUL_VMEM_BUDGET = 28 * 1024 * 1024


def _matmul_tiles(m, n, k, a_bytes, b_bytes, o_bytes):
    def divisors(x, cap):
        return sorted({x // s for s in range(1, 65) if x % s == 0 and (x // s) % LANE == 0 and x // s <= cap}, reverse=True) or [x]

    for tk in divisors(k, k):
        best = None
        for tm in divisors(m, 1024):
            for tn in divisors(n, 2048):
                need = 2 * (tm * tk * a_bytes + tk * tn * b_bytes + tm * tn * o_bytes) + (tm * tn * 4 if tk < k else 0)
                if need <= MATMUL_VMEM_BUDGET and tm * tn >= 512 * 512 and (best is None or tm * tn > best[0] * best[1]):
                    best = (tm, tn)
        if best:
            return best[0], best[1], tk
    return _pick(m, (512, 256, 128)), _pick(n, (512, 256, 128)), _pick(k, (512, 256, 128))


def _exchange(xs, name, scatter):
    n = len(xs)
    npeer = N_DEV - 1

    def body(*refs):
        x_refs, o_refs = refs[:n], refs[n:2 * n]
        send_sems, recv_sems, local_sems = refs[2 * n:]
        mx, my, mc = lax.axis_index("x"), lax.axis_index("y"), lax.axis_index("c")
        me = 4 * mx + 2 * my + mc
        src_me = [x.at[me] if scatter else x for x in x_refs]
        mine = [pltpu.make_async_copy(src_me[a], o_refs[a].at[me], local_sems.at[a]) for a in range(n)]
        for cp in mine:
            cp.start()
        copies = []
        for k in range(1, N_DEV):
            px, py, pc = mx ^ (k >> 2), my ^ ((k >> 1) & 1), mc ^ (k & 1)
            peer = 4 * px + 2 * py + pc
            for a in range(n):
                cp = pltpu.make_async_remote_copy(
                    src_ref=x_refs[a].at[peer] if scatter else x_refs[a], dst_ref=o_refs[a].at[me],
                    send_sem=send_sems.at[a * npeer + k - 1], recv_sem=recv_sems.at[a * npeer + k - 1],
                    device_id=(px, py, pc), device_id_type=pl.DeviceIdType.MESH)
                cp.start()
                copies.append((cp, a, k, peer))
        for cp, a, k, peer in copies:
            pltpu.make_async_remote_copy(
                src_ref=src_me[a], dst_ref=o_refs[a].at[peer], send_sem=send_sems.at[a * npeer + k - 1],
                recv_sem=recv_sems.at[a * npeer + k - 1], device_id=(mx, my, mc),
                device_id_type=pl.DeviceIdType.MESH).wait_recv()
        for cp, _, _, _ in copies:
            cp.wait_send()
        for cp in mine:
            cp.wait()

    return pl.pallas_call(
        body, out_shape=[jax.ShapeDtypeStruct((N_DEV,) + x.shape[-2:], x.dtype) for x in xs],
        in_specs=[pl.BlockSpec(memory_space=pl.ANY)] * n, out_specs=[pl.BlockSpec(memory_space=pl.ANY)] * n,
        scratch_shapes=[pltpu.SemaphoreType.DMA((n * npeer,)), pltpu.SemaphoreType.DMA((n * npeer,)),
                        pltpu.SemaphoreType.DMA((n,))],
        name=name,
    )(*xs)


def _gather_by_chip(xs, name):
    n = len(xs)
    per = N_DEV - 1

    def body(*refs):
        x_refs, o_refs = refs[:n], refs[n:2 * n]
        send_sems, recv_sems, local_sems = refs[2 * n:]
        mx, my, mc = lax.axis_index("x"), lax.axis_index("y"), lax.axis_index("c")
        me, sibling = (mx, my, mc), (mx, my, 1 - mc)
        chips = [(1 - mx, my), (mx, 1 - my), (1 - mx, 1 - my)]
        slot = lambda d: 4 * d[0] + 2 * d[1] + d[2]

        def copy(a, k, block, to, src=None):
            dst = o_refs[a].at[slot(block)]
            return pltpu.make_async_remote_copy(
                src_ref=dst if src is None else src, dst_ref=dst, send_sem=send_sems.at[a * per + k],
                recv_sem=recv_sems.at[a * per + k], device_id=to, device_id_type=pl.DeviceIdType.MESH)

        mine = [pltpu.make_async_copy(x_refs[a], o_refs[a].at[slot(me)], local_sems.at[a]) for a in range(n)]
        for cp in mine:
            cp.start()
        first = []
        for a in range(n):
            first.append(copy(a, 0, me, sibling, src=x_refs[a]))
            first += [copy(a, 1 + j, me, (*chip, mc), src=x_refs[a]) for j, chip in enumerate(chips)]
        for cp in first:
            cp.start()
        passed = []
        for j, chip in enumerate(chips):
            for a in range(n):
                copy(a, 1 + j, (*chip, mc), me).wait_recv()
                cp = copy(a, 4 + j, (*chip, mc), sibling)
                cp.start()
                passed.append(cp)
        for a in range(n):
            copy(a, 0, sibling, me).wait_recv()
            for j, chip in enumerate(chips):
                copy(a, 4 + j, (*chip, 1 - mc), me).wait_recv()
        for cp in first + passed:
            cp.wait_send()
        for cp in mine:
            cp.wait()

    return pl.pallas_call(
        body, out_shape=[jax.ShapeDtypeStruct((N_DEV,) + x.shape, x.dtype) for x in xs],
        in_specs=[pl.BlockSpec(memory_space=pl.ANY)] * n, out_specs=[pl.BlockSpec(memory_space=pl.ANY)] * n,
        scratch_shapes=[pltpu.SemaphoreType.DMA((n * per,)), pltpu.SemaphoreType.DMA((n * per,)),
                        pltpu.SemaphoreType.DMA((n,))],
        name=name,
    )(*xs)


def _peer_of(k):
    mx, my, mc = lax.axis_index("x"), lax.axis_index("y"), lax.axis_index("c")
    px, py, pc = mx ^ (k >> 2), my ^ ((k >> 1) & 1), mc ^ (k & 1)
    return (px, py, pc), 4 * px + 2 * py + pc


def _exchange_start(xs, name, scatter):
    n = len(xs)
    npeer = N_DEV - 1

    def body(*refs):
        x_refs, land_refs = refs[:n], refs[n:2 * n]
        send_sems, recv_sems, token = refs[2 * n], refs[2 * n + 1], refs[-1]
        me = 4 * lax.axis_index("x") + 2 * lax.axis_index("y") + lax.axis_index("c")
        for k in range(1, N_DEV):
            dev, peer = _peer_of(k)
            for a in range(n):
                pltpu.make_async_remote_copy(
                    src_ref=x_refs[a].at[peer] if scatter else x_refs[a], dst_ref=land_refs[a].at[me],
                    send_sem=send_sems.at[a * npeer + k - 1], recv_sem=recv_sems.at[a * npeer + k - 1],
                    device_id=dev, device_id_type=pl.DeviceIdType.MESH).start()
        token[...] = jnp.zeros_like(token)

    hbm = pl.BlockSpec(memory_space=pltpu.HBM)
    sem = pl.BlockSpec(memory_space=pltpu.SEMAPHORE)
    lands = [pltpu.with_memory_space_constraint(lax.empty((N_DEV,) + x.shape[-2:], x.dtype), pltpu.HBM) for x in xs]
    srcs = [pltpu.with_memory_space_constraint(x, pltpu.HBM) for x in xs]
    outs = pl.pallas_call(
        body, name=name,
        out_shape=(pltpu.SemaphoreType.DMA((n * npeer,)), pltpu.SemaphoreType.DMA((n * npeer,)),
                   *[pltpu.HBM(x.shape, x.dtype) for x in srcs], *[pltpu.HBM(z.shape, z.dtype) for z in lands],
                   jax.ShapeDtypeStruct((8, LANE), F32)),
        in_specs=[hbm] * (2 * n), out_specs=(sem, sem, *[hbm] * (2 * n), pl.BlockSpec(memory_space=pltpu.VMEM)),
        input_output_aliases={i: 2 + i for i in range(2 * n)},
        compiler_params=pltpu.CompilerParams(has_side_effects=pltpu.SideEffectType.DATAFLOW_SIDE_EFFECTING),
    )(*srcs, *lands)
    return (outs[0], outs[1], list(outs[2:2 + n]), list(outs[2 + n:2 + 2 * n])), outs[-1][0:1, 0:1]


def _exchange_wait(started, after, name, scatter):
    send_sems, recv_sems, srcs, lands = started
    n = len(srcs)
    npeer = N_DEV - 1

    def body(*refs):
        x_refs, land_refs = refs[:n], refs[n:2 * n]
        send_sems, recv_sems = refs[2 * n], refs[2 * n + 1]
        mx, my, mc = lax.axis_index("x"), lax.axis_index("y"), lax.axis_index("c")
        me = 4 * mx + 2 * my + mc
        for k in range(1, N_DEV):
            _, peer = _peer_of(k)
            for a in range(n):
                src = x_refs[a].at[me] if scatter else x_refs[a]
                cp = pltpu.make_async_remote_copy(
                    src_ref=src, dst_ref=land_refs[a].at[peer], send_sem=send_sems.at[a * npeer + k - 1],
                    recv_sem=recv_sems.at[a * npeer + k - 1], device_id=(mx, my, mc), device_id_type=pl.DeviceIdType.MESH)
                cp.wait_send()
                cp.wait_recv()

    hbm = pl.BlockSpec(memory_space=pltpu.HBM)
    sem = pl.BlockSpec(memory_space=pltpu.SEMAPHORE)
    outs = pl.pallas_call(
        body, name=name,
        out_shape=(*[pltpu.HBM(x.shape, x.dtype) for x in srcs], *[pltpu.HBM(z.shape, z.dtype) for z in lands]),
        in_specs=[hbm] * (2 * n) + [sem, sem, pl.BlockSpec(memory_space=pl.ANY)], out_specs=tuple([hbm] * (2 * n)),
        input_output_aliases={i: i for i in range(2 * n)},
        compiler_params=pltpu.CompilerParams(has_side_effects=pltpu.SideEffectType.DATAFLOW_SIDE_EFFECTING),
    )(*srcs, *lands, send_sems, recv_sems, after)
    me = 4 * lax.axis_index("x") + 2 * lax.axis_index("y") + lax.axis_index("c")
    full = []
    for x, land in zip(outs[:n], outs[n:]):
        own = lax.dynamic_slice(x, (me, 0, 0), (1,) + x.shape[1:]) if scatter else x[None]
        full.append(lax.dynamic_update_slice(land, own, (me, 0, 0)))
    return full


def _sum_slots(x, name):
    _, r, c = x.shape
    tr = _pick(r, (512, 256, 128, 64, 32, 16))

    def body(x_ref, o_ref):
        acc = x_ref[0].astype(F32)
        for s in range(1, N_DEV):
            acc = acc + x_ref[s].astype(F32)
        o_ref[...] = acc

    return pl.pallas_call(
        body, grid=(r // tr,), in_specs=[pl.BlockSpec((N_DEV, tr, c), lambda i: (0, i, 0))],
        out_specs=pl.BlockSpec((tr, c), lambda i: (i, 0)), out_shape=jax.ShapeDtypeStruct((r, c), F32), name=name,
        compiler_params=pltpu.CompilerParams(dimension_semantics=("arbitrary",)),
    )(x)


def _mod_fwd(c_all, w_ada, b_ada_mine):
    def body(c_ref, w_ref, b_ref, o_ref):
        o_ref[...] = _doth(_silu(c_ref[...]), w_ref[...]) + b_ref[...]

    return pl.pallas_call(body, out_shape=jax.ShapeDtypeStruct((c_all.shape[0], w_ada.shape[1]), F32), name="mod_fwd")(c_all, w_ada, b_ada_mine)


def _mod_bwd(c_all_t, dmod_mine):
    def body(ct_ref, d_ref, o_ref):
        s = _silu(ct_ref[...])
        acc = s[:, 0:1] * d_ref[pl.ds(0, 1), :]
        for b in range(1, N_DEV):
            acc = acc + s[:, b:b + 1] * d_ref[pl.ds(b, 1), :]
        o_ref[...] = acc

    return pl.pallas_call(body, out_shape=jax.ShapeDtypeStruct((c_all_t.shape[0], dmod_mine.shape[1]), F32), name="mod_bwd")(c_all_t, dmod_mine)


def _conv_fwd(proj, conv_w8, tm):
    t = proj.shape[0]
    ch = DN_CONV_CH

    def body(x_ref, w_ref, o_ref, buf):
        @pl.when(pl.program_id(0) == 0)
        def _():
            buf[pl.ds(0, CONV_HALO), :] = jnp.zeros((CONV_HALO, ch), F32)

        buf[pl.ds(CONV_HALO, tm), :] = x_ref[...].astype(F32)
        for c0 in range(0, ch, CONV_COLS):
            cols = pl.ds(c0, CONV_COLS)
            w = [w_ref[pl.ds(j, 1), cols] for j in range(CONV_K)]
            for r0 in range(0, tm, CONV_ROWS):
                acc = buf[pl.ds(r0 + CONV_HALO - (CONV_K - 1), CONV_ROWS), cols] * w[0]
                for j in range(1, CONV_K):
                    acc = acc + buf[pl.ds(r0 + CONV_HALO - (CONV_K - 1) + j, CONV_ROWS), cols] * w[j]
                o_ref[pl.ds(r0, CONV_ROWS), cols] = _silu(acc)
        buf[pl.ds(0, CONV_HALO), :] = buf[pl.ds(tm, CONV_HALO), :]

    return pl.pallas_call(
        body, grid=(t // tm,), in_specs=[pl.BlockSpec((tm, ch), lambda i: (i, 0)), _full(conv_w8.shape)],
        out_specs=pl.BlockSpec((tm, ch), lambda i: (i, 0)), out_shape=jax.ShapeDtypeStruct((t, ch), F32),
        scratch_shapes=[pltpu.VMEM((tm + CONV_HALO, ch), F32)], name="conv_fwd",
        compiler_params=pltpu.CompilerParams(dimension_semantics=("arbitrary",)),
    )(proj, conv_w8)


def _conv_bwd(proj, conv_w8, dact, others, tm):
    t = proj.shape[0]
    ch = DN_CONV_CH
    nt = t // tm
    halo_blk = 2 * CONV_HALO
    hb = tm // halo_blk
    n_others = len(others)

    def body(x_ref, xp_ref, w_ref, dy_ref, *refs):
        piece_refs, (dx_ref, dw_ref, xbuf, dbuf) = refs[:n_others], refs[n_others:]
        step = pl.program_id(0)
        for (off, arr), p_ref in zip(others, piece_refs):
            dx_ref[:, pl.ds(off, arr.shape[1])] = p_ref[...].astype(dx_ref.dtype)

        @pl.when(step == 0)
        def _():
            dbuf[pl.ds(tm, CONV_HALO), :] = jnp.zeros((CONV_HALO, ch), F32)
            dw_ref[...] = jnp.zeros_like(dw_ref)

        first = step == nt - 1
        xbuf[pl.ds(0, CONV_HALO), :] = jnp.where(first, 0.0, xp_ref[...].astype(F32)[halo_blk - CONV_HALO:])
        xbuf[pl.ds(CONV_HALO, tm), :] = x_ref[...].astype(F32)
        for c0 in range(0, ch, CONV_COLS):
            cols = pl.ds(c0, CONV_COLS)
            w = [w_ref[pl.ds(j, 1), cols] for j in range(CONV_K)]
            dw = [jnp.zeros((1, CONV_COLS), F32) for _ in range(CONV_K)]
            for r0 in range(0, tm, CONV_ROWS):
                xs = [xbuf[pl.ds(r0 + CONV_HALO - (CONV_K - 1) + j, CONV_ROWS), cols] for j in range(CONV_K)]
                pre = xs[0] * w[0]
                for j in range(1, CONV_K):
                    pre = pre + xs[j] * w[j]
                sg = _sigmoid(pre)
                dpre = dy_ref[pl.ds(r0, CONV_ROWS), cols] * (sg * (1.0 + pre * (1.0 - sg)))
                dbuf[pl.ds(r0, CONV_ROWS), cols] = dpre
                dw = [dw[j] + jnp.sum(dpre * xs[j], axis=0, keepdims=True) for j in range(CONV_K)]
            for j in range(CONV_K):
                dw_ref[pl.ds(j, 1), cols] += dw[j]
            for r0 in range(0, tm, CONV_ROWS):
                dx = dbuf[pl.ds(r0 + CONV_K - 1, CONV_ROWS), cols] * w[0]
                for j in range(1, CONV_K):
                    dx = dx + dbuf[pl.ds(r0 + CONV_K - 1 - j, CONV_ROWS), cols] * w[j]
                dx_ref[pl.ds(r0, CONV_ROWS), cols] = dx.astype(dx_ref.dtype)
        dbuf[pl.ds(tm, CONV_HALO), :] = dbuf[pl.ds(0, CONV_HALO), :]

    rev = lambda i: (nt - 1 - i, 0)
    prev = lambda i: (jnp.maximum((nt - 1 - i) * hb - 1, 0), 0)
    return pl.pallas_call(
        body, grid=(nt,),
        in_specs=[pl.BlockSpec((tm, ch), rev), pl.BlockSpec((halo_blk, ch), prev), _full(conv_w8.shape),
                  pl.BlockSpec((tm, ch), rev)] + [pl.BlockSpec((tm, arr.shape[1]), rev) for _, arr in others],
        out_specs=[pl.BlockSpec((tm, N_INP), rev), _full(conv_w8.shape)],
        out_shape=[jax.ShapeDtypeStruct((t, N_INP), BF), jax.ShapeDtypeStruct(conv_w8.shape, F32)],
        scratch_shapes=[pltpu.VMEM((tm + CONV_HALO, ch), F32), pltpu.VMEM((tm + CONV_HALO, ch), F32)], name="conv_bwd",
        compiler_params=pltpu.CompilerParams(dimension_semantics=("arbitrary",)),
    )(proj, proj, conv_w8, dact, *[arr for _, arr in others])


BNN = (((2,), (1,)), ((0,), (0,)))
BNT = (((2,), (2,)), ((0,), (0,)))
BTN = (((1,), (1,)), ((0,), (0,)))


def _bdot(a, b, dims, precision=None):
    return lax.dot_general(a, b, dims, precision=precision, preferred_element_type=F32)


@jax.custom_vjp
def _bmmb_nt(a, b):
    return _bdot(a.astype(BF), b.astype(BF), BNT)


def _bmmb_nt_fwd(a, b):
    return _bmmb_nt(a, b), (a, b)


def _bmmb_nt_bwd(res, g):
    a, b = res
    gb = g.astype(BF)
    return _bdot(gb, b.astype(BF), BNN), _bdot(gb, a.astype(BF), BTN)


_bmmb_nt.defvjp(_bmmb_nt_fwd, _bmmb_nt_bwd)


@jax.custom_vjp
def _bmmb(a, b):
    return _bdot(a.astype(BF), b.astype(BF), BNN)


def _bmmb_fwd(a, b):
    return _bmmb(a, b), (a, b)


def _bmmb_bwd(res, g):
    a, b = res
    gb = g.astype(BF)
    return _bdot(gb, b.astype(BF), BNT), _bdot(a.astype(BF), gb, BTN)


_bmmb.defvjp(_bmmb_fwd, _bmmb_bwd)


@jax.custom_vjp
def _bmmb_tn(a, b):
    return _bdot(a.astype(BF), b.astype(BF), BTN)


def _bmmb_tn_fwd(a, b):
    return _bmmb_tn(a, b), (a, b)


def _bmmb_tn_bwd(res, g):
    a, b = res
    gb = g.astype(BF)
    return _bdot(b.astype(BF), gb, BNT), _bdot(a.astype(BF), gb, BNN)


_bmmb_tn.defvjp(_bmmb_tn_fwd, _bmmb_tn_bwd)


def _triangle_sums(g, lower):
    c = g.shape[1]
    ri = lax.broadcasted_iota(jnp.int32, (g.shape[0], c, c), 1)
    ci = lax.broadcasted_iota(jnp.int32, (g.shape[0], c, c), 2)
    tri = (ri >= ci if lower else ri <= ci).astype(BF)
    hi = g.astype(BF)
    mid = (g - hi.astype(F32)).astype(BF)
    lo = (g - hi.astype(F32) - mid.astype(F32)).astype(BF)
    return _bdot(tri, hi, BNN) + _bdot(tri, mid, BNN) + _bdot(tri, lo, BNN)


@jax.custom_vjp
def _chunk_cumsum(g):
    return _triangle_sums(g, True)


_chunk_cumsum.defvjp(lambda g: (_triangle_sums(g, True), None), lambda _, ct: (_triangle_sums(ct, False),))


def _unit_lower_solve_fwd(a, r):
    c = a.shape[-1]
    ri = lax.broadcasted_iota(jnp.int32, a.shape, 1)
    ci = lax.broadcasted_iota(jnp.int32, a.shape, 2)
    xm = -a
    inv = (ri == ci).astype(F32) + xm
    for _ in range(int(math.log2(c)) - 1):
        xm = _bdot(xm, xm, BNN, HI)
        inv = inv + _bdot(inv, xm, BNN, HI)
    x = _bdot(inv, r, BNN, HI)
    return x, (inv, x)


def _unit_lower_solve_bwd(res, g):
    inv, x = res
    dr = _bdot(inv, g, BTN, HI)
    return -_bdot(dr, x, BNT, HI), dr


@jax.custom_vjp
def _unit_lower_solve_given(a, r, inv):
    return _bdot(inv, r, BNN, HI)


def _unit_lower_solve_given_fwd(a, r, inv):
    x = _bdot(inv, r, BNN, HI)
    return x, (inv, x)


def _unit_lower_solve_given_bwd(res, g):
    da, dr = _unit_lower_solve_bwd(res, g)
    return da, dr, jnp.zeros_like(res[0])


_unit_lower_solve_given.defvjp(_unit_lower_solve_given_fwd, _unit_lower_solve_given_bwd)


def _gdn_intra(qkv, ba, al8, dt8, inv4=None):
    tm = qkv.shape[0]
    nb = tm // CHUNK
    bsz = DN_HEADS * nb

    def heads(x0):
        return jnp.concatenate([qkv[:, x0 + h * LANE:x0 + (h + 1) * LANE].reshape(nb, CHUNK, LANE) for h in range(DN_HEADS)], axis=0)

    def spread(c0):
        return jnp.concatenate([jnp.broadcast_to(ba[:, c0 + h:c0 + h + 1], (tm, LANE)).reshape(nb, CHUNK, LANE)
                                for h in range(DN_HEADS)], axis=0)

    def per_head(v8):
        return jnp.concatenate([jnp.broadcast_to(v8[0:1, h:h + 1].reshape(1, 1, 1), (nb, 1, LANE)) for h in range(DN_HEADS)], axis=0)

    ri = lax.broadcasted_iota(jnp.int32, (bsz, CHUNK, CHUNK), 1)
    ci = lax.broadcasted_iota(jnp.int32, (bsz, CHUNK, CHUNK), 2)
    incl = ri >= ci
    strict = ri > ci

    q = _l2norm(heads(0)) * (DN_DK ** -0.5)
    k = _l2norm(heads(DN_QK))
    va = heads(2 * DN_QK)
    beta = _sigmoid(spread(0))
    g = -jnp.exp(per_head(al8)) * _softplus(spread(DN_HEADS) + per_head(dt8))
    gc = _chunk_cumsum(g)
    g_last = jnp.sum(g, axis=1, keepdims=True)
    gcol = gc[:, :, :CHUNK]
    diff = gcol - jnp.swapaxes(gcol, 1, 2)
    decay = jnp.where(incl, jnp.exp(jnp.where(incl, diff, 0.0)), 0.0)
    kb = k * beta
    a_mat = jnp.where(strict, _bmmb_nt(kb, k) * decay, 0.0)
    egc = jnp.exp(gc)
    rhs = jnp.concatenate([kb * egc, va * beta], axis=2)
    if inv4 is None:
        wu, (inv, _) = _unit_lower_solve_fwd(a_mat, rhs)
    else:
        wu = _unit_lower_solve_given(a_mat, rhs, inv4.reshape(bsz, CHUNK, CHUNK))
    attn = jnp.where(incl, _bmmb_nt(q, k) * decay, 0.0)

    def unheads(x):
        return jnp.concatenate([x[h * nb:(h + 1) * nb].reshape(tm, LANE) for h in range(DN_HEADS)], axis=1)

    w_c, u_c = wu[:, :, :DN_DK], wu[:, :, DN_DK:]
    kd = k * jnp.exp(g_last - gc)
    out = (unheads(q * egc - _bmmb(attn, w_c)), unheads(_bmmb(attn, u_c)),
           _bmmb_tn(kd, w_c).reshape(DN_HEADS, nb, DN_DK, DN_DK), _bmmb_tn(kd, u_c).reshape(DN_HEADS, nb, DN_DK, DN_DV),
           jnp.broadcast_to(g_last, (bsz, GL_ROWS, LANE)).reshape(DN_HEADS, nb, GL_ROWS, LANE))
    return out if inv4 is not None else out + (inv.reshape(DN_HEADS, nb, CHUNK, CHUNK),)


def _gdn_scan_step(qp, op, c_mat, n_mat, gl, s):
    return _mmb(qp, s) + op, s * jnp.exp(gl) - _mmb(c_mat, s) + n_mat


def _gdn_intra_specs(t, tm, dts, order=lambda i: i):
    nb = tm // CHUNK
    row = pl.BlockSpec((tm, DN_VW), lambda i: (order(i), 0))
    mat = pl.BlockSpec((DN_HEADS, nb, DN_DK, DN_DV), lambda i: (0, order(i), 0, 0))
    row_shape = lambda d: jax.ShapeDtypeStruct((t, DN_VW), d)
    mat_shape = lambda d: jax.ShapeDtypeStruct((DN_HEADS, t // CHUNK, DN_DK, DN_DV), d)
    gl = pl.BlockSpec((DN_HEADS, nb, GL_ROWS, LANE), lambda i: (0, order(i), 0, 0))
    gl_shape = jax.ShapeDtypeStruct((DN_HEADS, t // CHUNK, GL_ROWS, LANE), dts[4])
    return [row, row, mat, mat, gl], [row_shape(dts[0]), row_shape(dts[1]), mat_shape(dts[2]), mat_shape(dts[3]), gl_shape]


def _gdn_intra_fwd(qkv, ba, al8, dt8, tm):
    t = qkv.shape[0]

    def body(qkv_ref, ba_ref, al_ref, dt_ref, *outs):
        for o, val in zip(outs, _gdn_intra(qkv_ref[...], ba_ref[...], al_ref[...], dt_ref[...])):
            o[...] = val.astype(o.dtype)

    specs, shapes = _gdn_intra_specs(t, tm, (BF, F32, BF, BF, F32))
    specs.append(_gdn_inverse_spec(tm))
    shapes.append(jax.ShapeDtypeStruct((DN_HEADS, t // CHUNK, CHUNK, CHUNK), F32))
    res = pl.pallas_call(
        body, grid=(t // tm,),
        in_specs=[pl.BlockSpec((tm, DN_CONV_CH), lambda i: (i, 0)), pl.BlockSpec((tm, LANE), lambda i: (i, 0)),
                  _full(al8.shape), _full(dt8.shape)],
        out_specs=specs, out_shape=shapes, name="gdn_intra_fwd",
        compiler_params=pltpu.CompilerParams(dimension_semantics=("parallel",)),
    )(qkv, ba, al8, dt8)
    return res[:5], res[5]


def _gdn_inverse_spec(tm):
    return pl.BlockSpec((DN_HEADS, tm // CHUNK, CHUNK, CHUNK), lambda i: (0, i, 0, 0))


def _gdn_intra_bwd(qkv, ba, al8, dt8, inverses, cts, tm):
    t = qkv.shape[0]

    def body(qkv_ref, ba_ref, al_ref, dt_ref, inv_ref, *refs):
        ct_refs, (dqkv_ref, dba_ref, dal_ref, ddt_ref) = refs[:5], refs[5:]

        @pl.when(pl.program_id(0) == 0)
        def _():
            dal_ref[...] = jnp.zeros_like(dal_ref)
            ddt_ref[...] = jnp.zeros_like(ddt_ref)

        _, vjp = jax.vjp(functools.partial(_gdn_intra, inv4=inv_ref[...]), qkv_ref[...], ba_ref[...], al_ref[...], dt_ref[...])
        dqkv, dba, dal, ddt = vjp(tuple(r[...].astype(F32) for r in ct_refs))
        dqkv_ref[...] = dqkv.astype(dqkv_ref.dtype)
        dba_ref[...] = dba.astype(dba_ref.dtype)
        dal_ref[...] += dal
        ddt_ref[...] += ddt

    specs, _ = _gdn_intra_specs(t, tm, (F32,) * 5)
    return pl.pallas_call(
        body, grid=(t // tm,),
        in_specs=[pl.BlockSpec((tm, DN_CONV_CH), lambda i: (i, 0)), pl.BlockSpec((tm, LANE), lambda i: (i, 0)),
                  _full(al8.shape), _full(dt8.shape), _gdn_inverse_spec(tm)] + specs,
        out_specs=[pl.BlockSpec((tm, DN_CONV_CH), lambda i: (i, 0)), pl.BlockSpec((tm, LANE), lambda i: (i, 0)),
                   _full(al8.shape), _full(dt8.shape)],
        out_shape=[jax.ShapeDtypeStruct((t, DN_CONV_CH), BF), jax.ShapeDtypeStruct((t, LANE), BF),
                   jax.ShapeDtypeStruct(al8.shape, F32), jax.ShapeDtypeStruct(dt8.shape, F32)],
        name="gdn_intra_bwd", compiler_params=pltpu.CompilerParams(dimension_semantics=("arbitrary",)),
    )(qkv, ba, al8, dt8, inverses, *cts)


def _gdn_scan_fwd(intra, tm):
    t = intra[0].shape[0]
    nb = tm // CHUNK
    nc = t // CHUNK

    def body(qp_ref, op_ref, c_ref, n_ref, gl_ref, o_ref, ss_ref, s_scr):
        @pl.when(pl.program_id(0) == 0)
        def _():
            s_scr[...] = jnp.zeros_like(s_scr)

        state = [s_scr[h] for h in range(DN_HEADS)]
        for cc in range(nb):
            rows = pl.ds(cc * CHUNK, CHUNK)
            for h in range(DN_HEADS):
                cols = pl.ds(h * DN_DV, DN_DV)
                ss_ref[cc, h] = state[h].astype(ss_ref.dtype)
                o_ref[rows, cols], state[h] = _gdn_scan_step(
                    qp_ref[rows, cols], op_ref[rows, cols], c_ref[h, cc], n_ref[h, cc], gl_ref[h, cc, pl.ds(0, 1), :], state[h])
        for h in range(DN_HEADS):
            s_scr[h] = state[h]

    specs, _ = _gdn_intra_specs(t, tm, (F32,) * 5)
    return pl.pallas_call(
        body, grid=(t // tm,), in_specs=specs,
        out_specs=[pl.BlockSpec((tm, DN_VW), lambda i: (i, 0)),
                   pl.BlockSpec((nb, DN_HEADS, DN_DK, DN_DV), lambda i: (i, 0, 0, 0))],
        out_shape=[jax.ShapeDtypeStruct((t, DN_VW), F32), jax.ShapeDtypeStruct((nc, DN_HEADS, DN_DK, DN_DV), BF)],
        scratch_shapes=[pltpu.VMEM((DN_HEADS, DN_DK, DN_DV), F32)], name="gdn_scan_fwd",
        compiler_params=pltpu.CompilerParams(dimension_semantics=("arbitrary",)),
    )(*intra)


def _gdn_scan_bwd(intra, states, do, tm):
    t = intra[0].shape[0]
    nb = tm // CHUNK
    ng = t // tm

    def body(qp_ref, op_ref, c_ref, n_ref, gl_ref, ss_ref, do_ref, dqp_ref, dop_ref, dc_ref, dn_ref, dgl_ref, ds_scr):
        @pl.when(pl.program_id(0) == 0)
        def _():
            ds_scr[...] = jnp.zeros_like(ds_scr)

        d_state = [ds_scr[h] for h in range(DN_HEADS)]
        for cc in reversed(range(nb)):
            rows = pl.ds(cc * CHUNK, CHUNK)
            for h in range(DN_HEADS):
                cols = pl.ds(h * DN_DV, DN_DV)
                _, vjp = jax.vjp(_gdn_scan_step, qp_ref[rows, cols].astype(F32), op_ref[rows, cols], c_ref[h, cc].astype(F32),
                                 n_ref[h, cc].astype(F32), gl_ref[h, cc, pl.ds(0, 1), :], ss_ref[cc, h].astype(F32))
                dqp_ref[rows, cols], dop_ref[rows, cols], dc, dn, dgl, d_state[h] = vjp((do_ref[rows, cols], d_state[h]))
                dc_ref[h, cc] = dc.astype(dc_ref.dtype)
                dn_ref[h, cc] = dn.astype(dn_ref.dtype)
                first_row = lax.broadcasted_iota(jnp.int32, (GL_ROWS, LANE), 0) == 0
                dgl_ref[h, cc] = jnp.where(first_row, dgl, 0.0)
        for h in range(DN_HEADS):
            ds_scr[h] = d_state[h]

    five, shapes = _gdn_intra_specs(t, tm, (F32, F32, BF, BF, F32), order=lambda i: ng - 1 - i)
    row = five[0]
    return pl.pallas_call(
        body, grid=(ng,),
        in_specs=five + [pl.BlockSpec((nb, DN_HEADS, DN_DK, DN_DV), lambda i: (ng - 1 - i, 0, 0, 0)), row],
        out_specs=five, out_shape=shapes,
        scratch_shapes=[pltpu.VMEM((DN_HEADS, DN_DK, DN_DV), F32)], name="gdn_scan_bwd",
        compiler_params=pltpu.CompilerParams(dimension_semantics=("arbitrary",)),
    )(*intra, states, do)


def _gdn_out(o, z, g):
    parts = []
    for h in range(DN_HEADS):
        sl = slice(h * DN_DV, (h + 1) * DN_DV)
        parts.append(_rmsnorm(o[:, sl], g) * _silu(z[:, sl]))
    return parts


_Q_SCALE = math.log2(math.e) / math.sqrt(QK_NOPE + QK_ROPE)


def _rope_tables(pos, inv_freq2):
    lane = lax.broadcasted_iota(jnp.int32, (1, LANE), 1)
    ang = pos * inv_freq2
    cos = jnp.where(lane < QK_ROPE, jnp.cos(ang), 0.0)
    sin = jnp.where(lane < QK_ROPE // 2, -jnp.sin(ang), jnp.where(lane < QK_ROPE, jnp.sin(ang), 0.0))
    return cos, sin


@jax.custom_vjp
def _rope_swap(u):
    lane = lax.broadcasted_iota(jnp.int32, u.shape, 1)
    half = QK_ROPE // 2
    return jnp.where(lane < half, pltpu.roll(u, LANE - half, 1), jnp.where(lane < QK_ROPE, pltpu.roll(u, half, 1), 0.0))


_rope_swap.defvjp(lambda u: (_rope_swap(u), None), lambda _, g: (_rope_swap(g),))


def _mla_prep(cq, ckv, kr, gq, gkv, w_uq, w_ukv, cos, sin):
    rope = lambda u: u * cos + _rope_swap(u) * sin
    q_lin = _mmb_nt(_rmsnorm(cq, gq), w_uq) * _Q_SCALE
    kv_lin = _mmb_nt(_rmsnorm(ckv, gkv), w_ukv)
    k_rope = rope(kr)
    qs, ks, vs = [], [], []
    for h in range(MLA_HEADS):
        qs += [q_lin[:, h * LANE:(h + 1) * LANE], rope(q_lin[:, (MLA_HEADS + h) * LANE:(MLA_HEADS + h + 1) * LANE])]
        ks += [kv_lin[:, 2 * h * LANE:(2 * h + 1) * LANE], k_rope]
        vs += [kv_lin[:, (2 * h + 1) * LANE:(2 * h + 2) * LANE]]
    return qs + ks + vs


def _mla_prep_fwd(proj, pos_col, inv_freq2, gq, gkv, w_uq, w_ukv, tm):
    t = proj.shape[0]
    nq = 2 * MLA_HEADS

    def body(cq_ref, ckv_ref, kr_ref, pos_ref, f_ref, gq_ref, gkv_ref, wq_ref, wkv_ref, q_ref, k_ref, v_ref):
        cos, sin = _rope_tables(pos_ref[...], f_ref[...])
        outs = _mla_prep(cq_ref[...].astype(F32), ckv_ref[...].astype(F32), kr_ref[...].astype(F32), gq_ref[...], gkv_ref[...], wq_ref[...], wkv_ref[...],
                         cos, sin)
        for i in range(nq):
            q_ref[:, pl.ds(i * LANE, LANE)] = outs[i].astype(q_ref.dtype)
            k_ref[:, pl.ds(i * LANE, LANE)] = outs[nq + i].astype(k_ref.dtype)
        for h in range(MLA_HEADS):
            v_ref[:, pl.ds(h * LANE, LANE)] = outs[2 * nq + h].astype(v_ref.dtype)

    row = lambda w, j: pl.BlockSpec((tm, w), functools.partial(lambda i, j: (i, j), j=j))
    return pl.pallas_call(
        body, grid=(t // tm,),
        in_specs=[row(Q_LORA, P_CQ // Q_LORA), row(KV_LORA, P_CKV // KV_LORA), row(LANE, P_KR // LANE),
                  pl.BlockSpec((tm, 1), lambda i: (i, 0)), _full(inv_freq2.shape), _full(gq.shape), _full(gkv.shape),
                  _full(w_uq.shape), _full(w_ukv.shape)],
        out_specs=[row(nq * LANE, 0), row(nq * LANE, 0), row(MLA_VW, 0)],
        out_shape=[jax.ShapeDtypeStruct((t, nq * LANE), BF), jax.ShapeDtypeStruct((t, nq * LANE), BF),
                   jax.ShapeDtypeStruct((t, MLA_VW), BF)],
        name="mla_prep_fwd", compiler_params=pltpu.CompilerParams(dimension_semantics=("arbitrary",)),
    )(proj, proj, proj, pos_col, inv_freq2, gq, gkv, w_uq, w_ukv)


def _mla_prep_bwd(proj, pos_col, inv_freq2, gq, gkv, w_uq, w_ukv, dq, dk, dv, tm):
    t = proj.shape[0]
    nq = 2 * MLA_HEADS

    def body(cq_ref, ckv_ref, kr_ref, pos_ref, f_ref, gq_ref, gkv_ref, wq_ref, wkv_ref, dq_ref, dk_ref, dv_ref,
             dcq_ref, dckv_ref, dkr_ref, dgq_ref, dgkv_ref, dwq_ref, dwkv_ref):
        @pl.when(pl.program_id(0) == 0)
        def _():
            for o in (dgq_ref, dgkv_ref, dwq_ref, dwkv_ref):
                o[...] = jnp.zeros_like(o)

        cos, sin = _rope_tables(pos_ref[...], f_ref[...])
        f = functools.partial(_mla_prep, cos=cos, sin=sin)
        _, vjp = jax.vjp(f, cq_ref[...].astype(F32), ckv_ref[...].astype(F32), kr_ref[...].astype(F32), gq_ref[...], gkv_ref[...], wq_ref[...], wkv_ref[...])
        cts = [dq_ref[:, pl.ds(i * LANE, LANE)] for i in range(nq)]
        cts += [dk_ref[:, pl.ds(i * LANE, LANE)] for i in range(nq)]
        cts += [dv_ref[:, pl.ds(h * LANE, LANE)] for h in range(MLA_HEADS)]
        dcq, dckv, dkr, dgq, dgkv, dwq, dwkv = vjp(cts)
        dcq_ref[...] = dcq.astype(dcq_ref.dtype)
        dckv_ref[...] = dckv.astype(dckv_ref.dtype)
        dkr_ref[...] = dkr.astype(dkr_ref.dtype)
        dgq_ref[...] += dgq
        dgkv_ref[...] += dgkv
        dwq_ref[...] += dwq
        dwkv_ref[...] += dwkv

    row = lambda w, j: pl.BlockSpec((tm, w), functools.partial(lambda i, j: (i, j), j=j))
    return pl.pallas_call(
        body, grid=(t // tm,),
        in_specs=[row(Q_LORA, P_CQ // Q_LORA), row(KV_LORA, P_CKV // KV_LORA), row(LANE, P_KR // LANE),
                  pl.BlockSpec((tm, 1), lambda i: (i, 0)), _full(inv_freq2.shape), _full(gq.shape), _full(gkv.shape),
                  _full(w_uq.shape), _full(w_ukv.shape), row(nq * LANE, 0), row(nq * LANE, 0), row(MLA_VW, 0)],
        out_specs=[row(Q_LORA, 0), row(KV_LORA, 0), row(LANE, 0), _full(gq.shape), _full(gkv.shape),
                   _full(w_uq.shape), _full(w_ukv.shape)],
        out_shape=[jax.ShapeDtypeStruct((t, Q_LORA), BF), jax.ShapeDtypeStruct((t, KV_LORA), BF),
                   jax.ShapeDtypeStruct((t, LANE), BF), jax.ShapeDtypeStruct(gq.shape, F32),
                   jax.ShapeDtypeStruct(gkv.shape, F32), jax.ShapeDtypeStruct(w_uq.shape, F32),
                   jax.ShapeDtypeStruct(w_ukv.shape, F32)],
        name="mla_prep_bwd", compiler_params=pltpu.CompilerParams(dimension_semantics=("arbitrary",)),
    )(proj, proj, proj, pos_col, inv_freq2, gq, gkv, w_uq, w_ukv, dq, dk, dv)


_NEG = -1e30
_LN2 = math.log(2.0)
ATT_CHAINS = 2


def _causal(tq, tk, q0, k0):
    row = q0 + lax.broadcasted_iota(jnp.int32, (tq, tk), 0)
    col = k0 + lax.broadcasted_iota(jnp.int32, (tq, tk), 1)
    return col <= row


def _attn_fwd(q, k, v, tq, tk):
    t = q.shape[0]

    assert tk % tq == 0 or tq % tk == 0
    n_diag = max(1, tq // tk)

    th = tq // ATT_CHAINS

    def body(q_ref, k_ref, v_ref, o_ref, lse_ref):
        i = pl.program_id(1)
        n_full = (i * tq) // tk

        def step(k0, carry, masked):
            out = []
            for c, (m, l, acc) in enumerate(carry):
                kw = min(tk, (c + 1) * th) if masked and tk == tq else tk
                kt = k_ref[pl.ds(k0, kw), :]
                vt = v_ref[pl.ds(k0, kw), :]
                s = _dot(q_ref[pl.ds(c * th, th), :], kt, NT)
                if masked:
                    s = jnp.where(_causal(th, kw, i * tq + c * th, k0), s, _NEG)
                m_new = jnp.maximum(m, jnp.max(s, axis=-1, keepdims=True))
                p = jnp.exp2(s - m_new)
                alpha = jnp.exp2(m - m_new)
                out.append((m_new, alpha * l + jnp.sum(p, axis=-1, keepdims=True), alpha * acc + _dot(p.astype(BF), vt)))
            return tuple(out)

        init = tuple((jnp.full((th, 1), _NEG, F32), jnp.zeros((th, 1), F32), jnp.zeros((th, V_HEAD), F32)) for _ in range(ATT_CHAINS))
        carry = lax.fori_loop(0, n_full, lambda j, c: step(pl.multiple_of(j * tk, tk), c, False), init)
        for dd in range(n_diag):
            carry = step(pl.multiple_of((n_full + dd) * tk, tk), carry, True)
        for c, (m, l, acc) in enumerate(carry):
            o_ref[pl.ds(c * th, th), :] = acc / l
            lse_ref[pl.ds(c * th, th), :] = jnp.broadcast_to(m + jnp.log2(l), (th, LANE))

    return pl.pallas_call(
        body, grid=(MLA_HEADS, t // tq),
        in_specs=[pl.BlockSpec((tq, 2 * LANE), lambda h, i: (i, h)), pl.BlockSpec((t, 2 * LANE), lambda h, i: (0, h)),
                  pl.BlockSpec((t, V_HEAD), lambda h, i: (0, h))],
        out_specs=[pl.BlockSpec((tq, V_HEAD), lambda h, i: (i, h)), pl.BlockSpec((tq, LANE), lambda h, i: (i, h))],
        out_shape=[jax.ShapeDtypeStruct((t, MLA_VW), F32), jax.ShapeDtypeStruct((t, MLA_HEADS * LANE), F32)],
        name="attn_fwd", compiler_params=pltpu.CompilerParams(dimension_semantics=("parallel", "arbitrary")),
    )(q, k, v)


def _attn_bwd(q, k, v, do, lse, delta, tq, tk):
    t = q.shape[0]
    nkt = t // tk
    assert tk % tq == 0

    def body(q_ref, k_ref, v_ref, do_ref, lse_ref, dl_ref, dq_ref, dk_ref, dv_ref):
        j = pl.program_id(1)

        @pl.when(j == 0)
        def _():
            dq_ref[...] = jnp.zeros_like(dq_ref)

        kt = k_ref[...]
        vt = v_ref[...]

        def step(q0, carry, masked, kw=tk):
            dk, dv = carry
            rows = pl.ds(q0, tq)
            qt = q_ref[rows, :]
            dot_ = do_ref[rows, :]
            ktw, vtw = kt[:kw], vt[:kw]
            p = jnp.exp2(_dot(qt, ktw, NT) - lse_ref[rows, pl.ds(0, 1)])
            if masked:
                p = jnp.where(_causal(tq, kw, q0, j * tk), p, 0.0)
            dv_w = _dot(p.astype(BF), dot_, TN)
            ds = (p * (_dot(dot_, vtw, NT) - dl_ref[rows, pl.ds(0, 1)])).astype(BF)
            dk_w = _dot(ds, qt, TN)
            dq_ref[rows, :] += _dot(ds, ktw)
            if kw == tk:
                return dk + dk_w, dv + dv_w
            return (jnp.concatenate([dk[:kw] + dk_w, dk[kw:]], axis=0), jnp.concatenate([dv[:kw] + dv_w, dv[kw:]], axis=0))

        per = tk // tq
        carry = (jnp.zeros((tk, 2 * LANE), F32), jnp.zeros((tk, V_HEAD), F32))
        for dd in range(per):
            carry = step(pl.multiple_of(j * tk + dd * tq, tq), carry, True, kw=(dd + 1) * tq)

        def group(g, c):
            for dd in range(per):
                c = step(pl.multiple_of(g * tk + dd * tq, tq), c, False)
            return c

        dk, dv = lax.fori_loop(j + 1, nkt, group, carry)
        dk_ref[...] = dk * _LN2
        dv_ref[...] = dv

        @pl.when(j == nkt - 1)
        def _():
            dq_ref[...] = dq_ref[...] * _LN2

    return pl.pallas_call(
        body, grid=(MLA_HEADS, nkt),
        in_specs=[pl.BlockSpec((t, 2 * LANE), lambda h, j: (0, h)), pl.BlockSpec((tk, 2 * LANE), lambda h, j: (j, h)),
                  pl.BlockSpec((tk, V_HEAD), lambda h, j: (j, h)), pl.BlockSpec((t, V_HEAD), lambda h, j: (0, h)),
                  pl.BlockSpec((t, LANE), lambda h, j: (0, h)), pl.BlockSpec((t, LANE), lambda h, j: (0, h))],
        out_specs=[pl.BlockSpec((t, 2 * LANE), lambda h, j: (0, h)), pl.BlockSpec((tk, 2 * LANE), lambda h, j: (j, h)),
                   pl.BlockSpec((tk, V_HEAD), lambda h, j: (j, h))],
        out_shape=[jax.ShapeDtypeStruct((t, MLA_HEADS * 2 * LANE), F32), jax.ShapeDtypeStruct((t, MLA_HEADS * 2 * LANE), F32),
                   jax.ShapeDtypeStruct((t, MLA_VW), F32)],
        name="attn_bwd", compiler_params=pltpu.CompilerParams(dimension_semantics=("parallel", "arbitrary")),
    )(q, k, v, do, lse, delta)


def _adam_update(w, g, m, v):
    mm = ADAM_B1 * m + (1.0 - ADAM_B1) * g
    vv = ADAM_B2 * v + (1.0 - ADAM_B2) * jnp.square(g)
    m_hat = mm / (1.0 - ADAM_B1 ** ADAM_STEP)
    v_hat = vv / (1.0 - ADAM_B2 ** ADAM_STEP)
    return -ADAM_LR * (m_hat / (jnp.sqrt(v_hat) + ADAM_EPS) + ADAM_WD * w), mm, vv


def _adamw(w, g, m, v, name):
    r, c = w.shape
    tr = max([r // s for s in range(1, r // 8 + 1) if r % s == 0 and (r // s) % 8 == 0 and r // s <= 512] or [r])
    slots = g.ndim == 3

    def body(w_ref, g_ref, m_ref, v_ref, g_out, d_ref, nm_ref, nv_ref):
        if slots:
            gg = g_ref[0].astype(F32)
            for s in range(1, N_DEV):
                gg = gg + g_ref[s].astype(F32)
        else:
            gg = g_ref[...]
        g_out[...] = gg
        d_ref[...], nm_ref[...], nv_ref[...] = _adam_update(w_ref[...], gg, m_ref[...], v_ref[...])

    spec = pl.BlockSpec((tr, c), lambda i: (i, 0))
    g_spec = pl.BlockSpec((N_DEV, tr, c), lambda i: (0, i, 0)) if slots else spec
    return pl.pallas_call(
        body, grid=(r // tr,), in_specs=[spec, g_spec, spec, spec], out_specs=[spec] * 4,
        out_shape=[jax.ShapeDtypeStruct((r, c), F32)] * 4, name=name,
        compiler_params=pltpu.CompilerParams(dimension_semantics=("arbitrary",)),
    )(w, g, m, v)


def _adamw_many(ws, gs, ms, vs, name):
    n = len(ws)

    def body(*refs):
        for i in range(n):
            w_ref, g_ref, m_ref, v_ref = (refs[j * n + i] for j in range(4))
            d_ref, nm_ref, nv_ref = (refs[(4 + j) * n + i] for j in range(3))
            d_ref[...], nm_ref[...], nv_ref[...] = _adam_update(w_ref[...], g_ref[...], m_ref[...], v_ref[...])

    shapes = [jax.ShapeDtypeStruct(w.shape, F32) for w in ws]
    outs = pl.pallas_call(body, out_shape=shapes * 3, name=name)(*ws, *gs, *ms, *vs)
    return outs[:n], outs[n:2 * n], outs[2 * n:]


def _cast_bf16(xs, name, after=None):
    n = len(xs)
    extra = [] if after is None else [after]

    def body(*refs):
        outs = refs[n + len(extra):]
        for i in range(n):
            outs[i][...] = refs[i][...].astype(BF)

    vmem = pl.BlockSpec(memory_space=pltpu.VMEM)
    return pl.pallas_call(
        body, out_shape=[jax.ShapeDtypeStruct(x.shape, BF) for x in xs], name=name,
        in_specs=[vmem] * n + [pl.BlockSpec(memory_space=pl.ANY)] * len(extra), out_specs=[vmem] * n)(*xs, *extra)


def _pad_rows(a, n):
    return jnp.pad(a, ((0, n - a.shape[0]), (0, 0)))


def _w_in_to_padded(wt):
    s_ba = P_CQ
    s_cq = s_ba + 2 * DN_HEADS
    s_kr = s_cq + Q_LORA + KV_LORA
    return jnp.concatenate([wt[:s_ba], wt[s_cq:s_kr], _pad_rows(wt[s_ba:s_cq], LANE), _pad_rows(wt[s_kr:], LANE)], axis=0)


def _w_in_from_padded(wt):
    return jnp.concatenate([wt[:P_CQ], wt[P_BA:P_BA + 2 * DN_HEADS], wt[P_CQ:P_BA], wt[P_KR:P_KR + QK_ROPE]], axis=0)


def _w_uq_to_padded(wt):
    w3 = wt.reshape(MLA_HEADS, QK_NOPE + QK_ROPE, Q_LORA)
    nope = w3[:, :QK_NOPE].reshape(MLA_HEADS * QK_NOPE, Q_LORA)
    rope = jnp.pad(w3[:, QK_NOPE:], ((0, 0), (0, LANE - QK_ROPE), (0, 0))).reshape(MLA_HEADS * LANE, Q_LORA)
    return jnp.concatenate([nope, rope], axis=0)


def _w_uq_from_padded(wt):
    nope = wt[:MLA_HEADS * QK_NOPE].reshape(MLA_HEADS, QK_NOPE, Q_LORA)
    rope = wt[MLA_HEADS * QK_NOPE:].reshape(MLA_HEADS, LANE, Q_LORA)[:, :QK_ROPE]
    return jnp.concatenate([nope, rope], axis=1).reshape(MLA_HEADS * (QK_NOPE + QK_ROPE), Q_LORA)


def _pack(pieces, width, row_mult):
    flat = jnp.concatenate([p.reshape(-1) for p in pieces])
    n = flat.shape[0]
    rows = -(-n // (width * row_mult)) * row_mult
    return jnp.pad(flat, (0, rows * width - n)).reshape(rows, width)


def _unpack(flat, shapes):
    out, o = [], 0
    for s in shapes:
        n = math.prod(s)
        out.append(flat[o:o + n].reshape(s))
        o += n
    return out


def kernel(x, c, positions, w_ada, b_ada, w_in, conv_w, a_log, dt_bias, dn_norm_g, q_norm_g, w_uq, kv_norm_g, w_ukv, w_o, ln1_g, ln1_b, w_gate, w_up, w_down, ln2_g, ln2_b, loss_target, m_w_ada, m_b_ada, m_w_in, m_conv_w, m_a_log, m_dt_bias, m_dn_norm_g, m_q_norm_g, m_w_uq, m_kv_norm_g, m_w_ukv, m_w_o, m_ln1_g, m_ln1_b, m_w_gate, m_w_up, m_w_down, m_ln2_g, m_ln2_b, v_w_ada, v_b_ada, v_w_in, v_conv_w, v_a_log, v_dt_bias, v_dn_norm_g, v_q_norm_g, v_w_uq, v_kv_norm_g, v_w_ukv, v_w_o, v_ln1_g, v_ln1_b, v_w_gate, v_w_up, v_w_down, v_ln2_g, v_ln2_b):
    me = 4 * lax.axis_index("x") + 2 * lax.axis_index("y") + lax.axis_index("c")
    t, d = x.shape[1], x.shape[2]
    ada_n = w_ada.shape[2]

    tr = lambda w: w[0].T
    rows = lambda a: a.reshape(-1, a.shape[2])
    (in_shard,) = _cast_bf16([tr(w_in)], "cast_w_in")
    cw = conv_w.shape[3]
    a_in, c_all, conv_all = _gather_by_chip([in_shard, c, conv_w[0, :, 0, :]], "gather_w_in_and_small")
    c_all = c_all.reshape(N_DEV, d)
    conv_full = conv_all.transpose(1, 0, 2).reshape(CONV_K, N_DEV * cw)
    conv_w8 = jnp.pad(conv_full, ((0, 8 - CONV_K), (0, 0)))

    b_ada_mine = lax.dynamic_slice(b_ada, (0, me * ada_n), (1, ada_n))
    mod_cols = _mod_fwd(c_all, w_ada[0], b_ada_mine)
    (mod_all,) = _exchange([mod_cols.reshape(N_DEV, 1, ada_n)], "scatter_mod", scatter=True)
    mod = mod_all.reshape(1, N_DEV * ada_n)

    later = _cast_bf16([tr(w_uq), tr(w_ukv), w_o[0], tr(w_gate), tr(w_up), w_down[0]], "cast_weights", after=mod)
    mixer_gather, token_a = _exchange_start(later[:3], "gather_mixer_weights_start", scatter=False)
    ffn_gather, token_b = _exchange_start(later[3:], "gather_ffn_weights_start", scatter=False)
    mod = mod + (token_a + token_b)
    w_in_t = _w_in_to_padded(rows(a_in))

    def mixer_weights(after):
        a_uq, a_ukv, a_o = _exchange_wait(mixer_gather, after, "gather_mixer_weights_wait", scatter=False)
        return _w_uq_to_padded(rows(a_uq)), rows(a_ukv), rows(a_o)

    def ffn_weights(after):
        a_gate, a_up, a_down = _exchange_wait(ffn_gather, after, "gather_ffn_weights_wait", scatter=False)
        return rows(a_gate), rows(a_up), rows(a_down)

    def by_dest(g):
        return g.reshape(N_DEV, -1, g.shape[1])

    scatters = {}

    def grads_ready(tag, *g):
        if tag == "ffn":
            pieces = [by_dest(a) for a in g]
        elif tag == "mixer":
            g_w_o, g_w_uq_t, g_w_ukv_t = g
            pieces = [by_dest(g_w_o), by_dest(_w_uq_from_padded(g_w_uq_t).astype(BF)), by_dest(g_w_ukv_t.astype(BF))]
        else:
            pieces = [by_dest(_w_in_from_padded(g[0]))]
        scatters[tag], token = _exchange_start(pieces, "scatter_%s_grads_start" % tag, scatter=True)
        return token

    loc = _local_step(x[0], loss_target[0], positions[0], mod, w_in_t, mixer_weights, ffn_weights, grads_ready,
                      conv_w8, a_log, dt_bias, dn_norm_g, q_norm_g, kv_norm_g, ln1_g, ln1_b, ln2_g, ln2_b)
    grad_x, loss_acc, dmod, d_conv8, d_al8, d_dt8, d_dn_g, d_q_g, d_kv_g, d_ln1_g, d_ln1_b, d_ln2_g, d_ln2_b = loc

    small_shapes = [(6 * d,), (CONV_K, N_DEV * cw), (DN_HEADS,), (DN_HEADS,), (DN_DV,), (Q_LORA,), (KV_LORA,), (d,), (d,), (d,), (d,), (1,)]
    gsmall = _pack([dmod, d_conv8[:CONV_K], d_al8[0, :DN_HEADS], d_dt8[0, :DN_HEADS], d_dn_g, d_q_g, d_kv_g,
                    d_ln1_g, d_ln1_b, d_ln2_g, d_ln2_b, loss_acc[0, :1]], LANE, 8)
    small_gather, _ = _exchange_start([gsmall], "gather_small_grads_start", scatter=False)

    def small_sums(after):
        (gsmall_all,) = _exchange_wait(small_gather, after, "gather_small_grads_wait", scatter=False)
        dmod_all = gsmall_all.reshape(N_DEV, -1)[:, :6 * d]
        tot = _unpack(_sum_slots(gsmall_all, "sum_small_grads").reshape(-1), small_shapes)
        g_b_ada, g_conv_full, g_a_log, g_dt_bias, g_dn_g, g_q_g, g_kv_g, g_ln1_g, g_ln1_b, g_ln2_g, g_ln2_b, loss1 = tot
        g_conv_w = lax.dynamic_slice(g_conv_full, (0, me * cw), (CONV_K, cw))
        g_w_ada = _mod_bwd(c_all.T, lax.dynamic_slice(dmod_all, (0, me * ada_n), (N_DEV, ada_n)))
        return loss1.reshape(()), {
            "w_ada": g_w_ada[None], "b_ada": g_b_ada[None], "conv_w": g_conv_w[None, :, None, :],
            "a_log": g_a_log[None], "dt_bias": g_dt_bias[None], "dn_norm_g": g_dn_g[None], "q_norm_g": g_q_g[None],
            "kv_norm_g": g_kv_g[None], "ln1_g": g_ln1_g[None], "ln1_b": g_ln1_b[None], "ln2_g": g_ln2_g[None], "ln2_b": g_ln2_b[None]}

    grads = {}
    weights = dict(w_ada=w_ada, b_ada=b_ada, w_in=w_in, conv_w=conv_w, a_log=a_log, dt_bias=dt_bias, dn_norm_g=dn_norm_g,
                   q_norm_g=q_norm_g, w_uq=w_uq, kv_norm_g=kv_norm_g, w_ukv=w_ukv, w_o=w_o, ln1_g=ln1_g, ln1_b=ln1_b,
                   w_gate=w_gate, w_up=w_up, w_down=w_down, ln2_g=ln2_g, ln2_b=ln2_b)
    ms = dict(w_ada=m_w_ada, b_ada=m_b_ada, w_in=m_w_in, conv_w=m_conv_w, a_log=m_a_log, dt_bias=m_dt_bias,
              dn_norm_g=m_dn_norm_g, q_norm_g=m_q_norm_g, w_uq=m_w_uq, kv_norm_g=m_kv_norm_g, w_ukv=m_w_ukv, w_o=m_w_o,
              ln1_g=m_ln1_g, ln1_b=m_ln1_b, w_gate=m_w_gate, w_up=m_w_up, w_down=m_w_down, ln2_g=m_ln2_g, ln2_b=m_ln2_b)
    vs = dict(w_ada=v_w_ada, b_ada=v_b_ada, w_in=v_w_in, conv_w=v_conv_w, a_log=v_a_log, dt_bias=v_dt_bias,
              dn_norm_g=v_dn_norm_g, q_norm_g=v_q_norm_g, w_uq=v_w_uq, kv_norm_g=v_kv_norm_g, w_ukv=v_w_ukv, w_o=v_w_o,
              ln1_g=v_ln1_g, ln1_b=v_ln1_b, w_gate=v_w_gate, w_up=v_w_up, w_down=v_w_down, ln2_g=v_ln2_g, ln2_b=v_ln2_b)
    names = list(weights)
    big = ("w_gate", "w_up", "w_down", "w_o", "w_uq", "w_ukv", "w_ada", "w_in")
    waits = {"w_gate": ("ffn", ("w_gate", "w_up", "w_down")), "w_o": ("mixer", ("w_o", "w_uq", "w_ukv")), "w_in": ("in", ("w_in",))}
    delta_w, new_m, new_v, slots = {}, {}, {}, {}
    last = small_gather[2][0]
    for n in big:
        if n == "w_ada":
            loss, small_grads = small_sums(last)
            grads.update(small_grads)
        if n == "w_in":
            rest = [r for r in names if r not in big]
            flat2 = lambda a: a.reshape(-1, a.shape[-1])
            outs = _adamw_many(*[[flat2(src[r]) for r in rest] for src in (weights, grads, ms, vs)], "adamw_small")
            for dst, o in zip((delta_w, new_m, new_v), outs):
                for r, a in zip(rest, o):
                    dst[r] = a.reshape(weights[r].shape)
            last = outs[0][0]
        transposed = n in ("w_in", "w_uq", "w_ukv", "w_gate", "w_up")
        two = (lambda a: a[0].T) if transposed else (lambda a: a[0])
        back = (lambda a: a.T[None]) if transposed else (lambda a: a[None])
        if n in waits:
            tag, members = waits[n]
            slots.update(zip(members, _exchange_wait(scatters[tag], last, "scatter_%s_grads_wait" % tag, scatter=True)))
        g_in = slots[n] if n in slots else two(grads[n])
        gr, dlt, nm, nv = _adamw(two(weights[n]), g_in, two(ms[n]), two(vs[n]), "adamw_" + n)
        grads[n], delta_w[n], new_m[n], new_v[n] = back(gr), back(dlt), back(nm), back(nv)
        last = nv

    return (loss, grad_x[None], *[grads[n] for n in names], *[delta_w[n] for n in names],
            *[new_m[n] for n in names], *[new_v[n] for n in names])


def _local_step(xs, tgt, pos, mod, w_in_t, mixer_weights, ffn_weights, grads_ready, conv_w8,
                a_log, dt_bias, dn_norm_g, q_norm_g, kv_norm_g, ln1_g, ln1_b, ln2_g, ln2_b):
    t, d = xs.shape
    sh_m, sc_m, gt_m, sh_f, sc_f, gt_f = [mod[:, i * d:(i + 1) * d] for i in range(6)]
    pos_col = pos.astype(F32).reshape(t, 1)
    inv_freq = 1.0 / (ROPE_THETA ** (jnp.arange(0, QK_ROPE, 2, dtype=F32) / QK_ROPE))
    inv_freq2 = jnp.pad(jnp.concatenate([inv_freq, inv_freq]), (0, LANE - QK_ROPE)).reshape(1, LANE)
    al8 = jnp.pad(a_log, ((0, 7), (0, LANE - DN_HEADS)))
    dt8 = jnp.pad(dt_bias, ((0, 7), (0, LANE - DN_HEADS)))

    tm = min(512, t)
    tq = min(256, t)
    tk = min(512, t)

    def modulate_in(xx, sc, sh, w_ba):
        h = (xx * (1.0 + sc) + sh).astype(BF)
        return h, _dot(h, w_ba, NT)

    h1, ba_raw = _rowwise("modulate_in", modulate_in, [xs], [sc_m, sh_m, w_in_t[P_BA:P_BA + LANE]], [(d, BF), (LANE, F32)], [], tm)
    proj = _matmul(h1, w_in_t, "nt", "in_proj", BF)
    qkv = _conv_fwd(proj, conv_w8, min(256, t))
    gdn_tm = min(512, t)
    intra, inverses = _gdn_intra_fwd(qkv, ba_raw, al8, dt8, gdn_tm)
    o_dn, states = _gdn_scan_fwd(intra, gdn_tm)
    w_uq_t, w_ukv_t, w_o_f = mixer_weights(states)
    qc, kc, vc = _mla_prep_fwd(proj, pos_col, inv_freq2, q_norm_g, kv_norm_g, w_uq_t, w_ukv_t, tm)
    o_mla, lse = _attn_fwd(qc, kc, vc, min(1024, t), min(1024, t))

    def mix_in(o, z, om, g):
        return jnp.concatenate(_gdn_out(o, z.astype(F32), g) + [om], axis=1)

    (mixin,) = _rowwise("mixer_out", mix_in, [o_dn, (proj, DN_VW, P_Z // DN_VW), o_mla], [dn_norm_g], [(2 * DN_VW, BF)], [], tm)
    mix = _matmul(mixin, w_o_f, "nn", "out_proj", BF)

    def block1(xx, mx, gt, g1, b1, sc, sh):
        x1 = _layernorm(DEEPNORM_ALPHA * xx + gt * mx, g1, b1)
        return x1, x1 * (1.0 + sc) + sh

    x1, h2 = _rowwise("norm1_modulate", block1, [xs, mix], [gt_m, ln1_g, ln1_b, sc_f, sh_f], [(d, F32), (d, BF)], [], tm)
    w_gate_f, w_up_f, w_down_f = ffn_weights(h2)
    act, act_dg, act_du = _ffn_in(h2, w_gate_f, w_up_f)
    ff = _matmul(act, w_down_f, "nn", "ffn_out", BF)

    def tail_loss(x1_, ff_, gt, g2, b2, tg):
        y = _layernorm(DEEPNORM_ALPHA * x1_ + gt * ff_, g2, b2)
        return 0.5 * jnp.sum(jnp.mean(jnp.square(y - tg), axis=-1))

    def tail(x1_, ff_, tg, gt, g2, b2):
        loss, (dx1, dff, dgt, dg2, db2) = jax.value_and_grad(tail_loss, argnums=(0, 1, 2, 3, 4))(x1_, ff_, gt, g2, b2, tg)
        return dx1, dff, jnp.full((1, LANE), loss, F32), dgt, dg2, db2

    dx1_a, dff, loss_acc, d_gt_f, d_ln2_g, d_ln2_b = _rowwise(
        "norm2_loss", tail, [x1, ff, tgt], [gt_f, ln2_g, ln2_b], [(d, BF), (d, BF)], [(1, LANE), (1, d), (1, d), (1, d)], tm)

    g_w_down = _matmul(act, dff, "tn", "d_w_down", BF)
    dgate, dup, dh2 = _ffn_bwd(dff, w_down_f, act_dg, act_du, w_gate_f, w_up_f)
    g_w_gate, g_w_up = _matmul_tn_pair(dgate, dup, h2, "d_w_gate_up", BF)
    token = grads_ready("ffn", g_w_gate, g_w_up, g_w_down)

    def block1_bwd(xx, mx, dx1_, dh2_, gt, g1, b1, sc, sh):
        _, vjp = jax.vjp(block1, xx, mx, gt, g1, b1, sc, sh)
        dxx, dmx, dgt, dg1, db1, dsc, dsh = vjp((dx1_.astype(F32), dh2_.astype(F32)))
        return dxx, dmx, dgt, dg1, db1, dsc, dsh

    dx_a, dmix, d_gt_m, d_ln1_g, d_ln1_b, d_sc_f, d_sh_f = _rowwise(
        "norm1_modulate_bwd", block1_bwd, [xs, mix, dx1_a, dh2], [gt_m + token, ln1_g, ln1_b, sc_f, sh_f],
        [(d, F32), (d, BF)], [(1, d)] * 5, tm)

    dmixin = _matmul(dmix, w_o_f, "nt", "d_mixer_out", BF)
    g_w_o = _matmul(mixin, dmix, "tn", "d_w_o", BF)

    def mixer_bwd(o, z, om, dmi, g):
        _, vjp = jax.vjp(lambda o_, z_, g_: jnp.concatenate(_gdn_out(o_, z_, g_), axis=1), o, z.astype(F32), g)
        do_, dz_, dg_ = vjp(dmi[:, :DN_VW].astype(F32))
        dom = dmi[:, DN_VW:]
        delta = [jnp.broadcast_to(jnp.sum(dom[:, h * V_HEAD:(h + 1) * V_HEAD] * om[:, h * V_HEAD:(h + 1) * V_HEAD], axis=-1, keepdims=True), (o.shape[0], LANE))
                 for h in range(MLA_HEADS)]
        return do_, dz_, dom, jnp.concatenate(delta, axis=1), dg_

    do_dn, dz, do_mla, delta, d_dn_g = _rowwise(
        "mixer_out_bwd", mixer_bwd, [o_dn, (proj, DN_VW, P_Z // DN_VW), o_mla, dmixin], [dn_norm_g],
        [(DN_VW, F32), (DN_VW, BF), (MLA_VW, BF), (MLA_HEADS * LANE, F32)], [(1, DN_DV)], tm)

    dqc, dkc, dvc = _attn_bwd(qc, kc, vc, do_mla, lse, delta, min(512, t), min(1024, t))
    dcq, dckv, dkr, d_q_g, d_kv_g, g_w_uq_t, g_w_ukv_t = _mla_prep_bwd(
        proj, pos_col, inv_freq2, q_norm_g, kv_norm_g, w_uq_t, w_ukv_t, dqc, dkc, dvc, tm)

    token = grads_ready("mixer", g_w_o, g_w_uq_t, g_w_ukv_t)

    d_intra = _gdn_scan_bwd(intra, states, do_dn, gdn_tm)
    dqkv_act, dba, d_al8, d_dt8 = _gdn_intra_bwd(qkv, ba_raw, al8 + token, dt8, inverses, d_intra, min(256, t))
    dproj, d_conv8 = _conv_bwd(proj, conv_w8, dqkv_act, [(P_Z, dz), (P_CQ, dcq), (P_CKV, dckv), (P_BA, dba), (P_KR, dkr)],
                               min(256, t))
    dh1 = _matmul(dproj, w_in_t, "nn", "d_in_proj", BF)
    g_w_in_t = _matmul(dproj, h1, "tn", "d_w_in", BF)
    token = grads_ready("in", g_w_in_t)

    def modulate_bwd(xx, dh, dxa, sc):
        dh = dh.astype(F32)
        return dh * (1.0 + sc) + dxa, jnp.sum(dh * xx, axis=0, keepdims=True), jnp.sum(dh, axis=0, keepdims=True)

    grad_x, d_sc_m, d_sh_m = _rowwise("modulate_in_bwd", modulate_bwd, [xs, dh1, dx_a], [sc_m + token], [(d, F32)], [(1, d), (1, d)], tm)
    dmod = jnp.concatenate([d_sh_m, d_sc_m, d_gt_m, d_sh_f, d_sc_f, d_gt_f], axis=1)
    return grad_x, loss_acc, dmod, d_conv8, d_al8, d_dt8, d_dn_g, d_q_g, d_kv_g, d_ln1_g, d_ln1_b, d_ln2_g, d_ln2_b
```

```python
import functools
import math

import jax
import jax.numpy as jnp
from jax import lax
from jax.experimental import pallas as pl
from jax.experimental.pallas import tpu as pltpu

F32 = jnp.float32
BF = jnp.bfloat16
HI = lax.Precision.HIGHEST

N_DEV = 8
DN_HEADS = 4
DN_DK = 128
DN_DV = 128
CONV_K = 4
CHUNK = 64
MLA_HEADS = 4
QK_NOPE = 128
QK_ROPE = 64
V_HEAD = 128
Q_LORA = 512
KV_LORA = 256
ROPE_THETA = 10000.0
DEPTH = 1
DEEPNORM_ALPHA = (2.0 * DEPTH) ** 0.25
LANE = 128
CONV_HALO = 8
GL_ROWS = 8
CONV_ROWS, CONV_COLS = 64, 256

DN_QK = DN_HEADS * DN_DK
DN_VW = DN_HEADS * DN_DV
DN_CONV_CH = 2 * DN_QK + DN_VW
MLA_VW = MLA_HEADS * V_HEAD
P_Z = DN_CONV_CH
P_CQ = P_Z + DN_VW
P_CKV = P_CQ + Q_LORA
P_BA = P_CKV + KV_LORA
P_KR = P_BA + LANE
N_INP = P_KR + LANE

ADAM_LR = 0.001
ADAM_B1 = 0.9
ADAM_B2 = 0.999
ADAM_EPS = 1e-08
ADAM_WD = 0.01
ADAM_STEP = 10

NN = (((1,), (0,)), ((), ()))
NT = (((1,), (1,)), ((), ()))
TN = (((0,), (0,)), ((), ()))


def _pick(n, prefs):
    for p in prefs:
        if n % p == 0:
            return p
    return n


def _full(shape):
    return pl.BlockSpec(shape, lambda *_: (0,) * len(shape))


def _dot(a, b, dims=NN):
    return lax.dot_general(a, b, dims, preferred_element_type=F32)


def _doth(a, b, dims=NN):
    return lax.dot_general(a, b, dims, precision=HI, preferred_element_type=F32)


@jax.custom_vjp
def _mmb(a, b):
    return _dot(a.astype(BF), b.astype(BF), NN)


def _mmb_fwd(a, b):
    return _mmb(a, b), (a, b)


def _mmb_bwd(res, g):
    a, b = res
    gb = g.astype(BF)
    return (_dot(gb, b.astype(BF), NT).astype(a.dtype), _dot(a.astype(BF), gb, TN).astype(b.dtype))


_mmb.defvjp(_mmb_fwd, _mmb_bwd)


@jax.custom_vjp
def _mmb_nt(a, b):
    return _dot(a.astype(BF), b.astype(BF), NT)


def _mmb_nt_fwd(a, b):
    return _mmb_nt(a, b), (a, b)


def _mmb_nt_bwd(res, g):
    a, b = res
    gb = g.astype(BF)
    return (_dot(gb, b.astype(BF), NN).astype(a.dtype), _dot(gb, a.astype(BF), TN).astype(b.dtype))


_mmb_nt.defvjp(_mmb_nt_fwd, _mmb_nt_bwd)


def _sigmoid(x):
    return 0.5 * (jnp.tanh(0.5 * x) + 1.0)


def _silu(x):
    return x * _sigmoid(x)


def _softplus(x):
    return jnp.maximum(x, 0.0) + jnp.log(1.0 + jnp.exp(-jnp.abs(x)))


def _layernorm(x, g, b, eps=1e-5):
    mu = jnp.mean(x, axis=-1, keepdims=True)
    xc = x - mu
    var = jnp.mean(xc * xc, axis=-1, keepdims=True)
    return xc * lax.rsqrt(var + eps) * g + b


def _rmsnorm(x, g, eps=1e-6):
    return x * lax.rsqrt(jnp.mean(x * x, axis=-1, keepdims=True) + eps) * g


def _l2norm(x, eps=1e-6):
    return x * lax.rsqrt(jnp.sum(x * x, axis=-1, keepdims=True) + eps)


def _rowwise(name, fn, rows, vecs, out_rows, out_accs, tm):
    rows = [r if isinstance(r, tuple) else (r, r.shape[1], 0) for r in rows]
    t = rows[0][0].shape[0]
    tm = min(tm, t)
    assert t % tm == 0
    nr, nv, no = len(rows), len(vecs), len(out_rows)

    def body(*refs):
        ins = [r[...] for r in refs[:nr + nv]]
        outs = fn(*ins)
        outs = outs if isinstance(outs, (tuple, list)) else (outs,)
        o_rows = refs[nr + nv:nr + nv + no]
        o_accs = refs[nr + nv + no:]
        for o, val in zip(o_rows, outs[:no]):
            o[...] = val.astype(o.dtype)
        if o_accs:
            @pl.when(pl.program_id(0) == 0)
            def _():
                for o in o_accs:
                    o[...] = jnp.zeros_like(o)
            for o, val in zip(o_accs, outs[no:]):
                o[...] += val

    in_specs = [pl.BlockSpec((tm, w), functools.partial(lambda i, j: (i, j), j=j)) for (_, w, j) in rows]
    in_specs += [_full(v.shape) for v in vecs]
    out_specs = [pl.BlockSpec((tm, w), lambda i: (i, 0)) for (w, _) in out_rows]
    out_specs += [_full(s) for s in out_accs]
    out_shape = [jax.ShapeDtypeStruct((t, w), d) for (w, d) in out_rows]
    out_shape += [jax.ShapeDtypeStruct(s, F32) for s in out_accs]
    res = pl.pallas_call(
        body, grid=(t // tm,), in_specs=in_specs, out_specs=out_specs, out_shape=out_shape, name=name,
        compiler_params=pltpu.CompilerParams(dimension_semantics=("arbitrary",)),
    )(*[r[0] for r in rows], *vecs)
    return res


def _matmul(a, b, mode, name, out_dtype=F32):
    if mode == "nn":
        (m, k), n = a.shape, b.shape[1]
    elif mode == "nt":
        (m, k), n = a.shape, b.shape[0]
    else:
        (k, m), n = a.shape, b.shape[1]
    tm, tn, tk = _matmul_tiles(m, n, k, a.dtype.itemsize, b.dtype.itemsize, jnp.dtype(out_dtype).itemsize)
    nk = k // tk
    dims = {"nn": NN, "nt": NT, "tn": TN}[mode]

    def body(a_ref, b_ref, o_ref, *acc):
        part = _dot(a_ref[...].astype(BF), b_ref[...].astype(BF), dims)
        if nk == 1:
            o_ref[...] = part.astype(o_ref.dtype)
            return
        (acc_ref,) = acc
        kk = pl.program_id(2)

        @pl.when(kk == 0)
        def _():
            acc_ref[...] = part

        @pl.when(kk > 0)
        def _():
            acc_ref[...] += part

        @pl.when(kk == nk - 1)
        def _():
            o_ref[...] = acc_ref[...].astype(o_ref.dtype)

    a_spec = pl.BlockSpec((tk, tm), lambda i, j, kk: (kk, i)) if mode == "tn" else pl.BlockSpec((tm, tk), lambda i, j, kk: (i, kk))
    b_spec = pl.BlockSpec((tn, tk), lambda i, j, kk: (j, kk)) if mode == "nt" else pl.BlockSpec((tk, tn), lambda i, j, kk: (kk, j))
    return pl.pallas_call(
        body, grid=(m // tm, n // tn, nk), in_specs=[a_spec, b_spec],
        out_specs=pl.BlockSpec((tm, tn), lambda i, j, kk: (i, j)),
        out_shape=jax.ShapeDtypeStruct((m, n), out_dtype),
        scratch_shapes=[pltpu.VMEM((tm, tn), F32)] if nk > 1 else [], name=name,
        compiler_params=pltpu.CompilerParams(dimension_semantics=("parallel", "parallel", "arbitrary")),
    )(a, b)


def _lane_tile(n, cap):
    return max([n // s for s in range(1, n // LANE + 1) if n % s == 0 and (n // s) % LANE == 0 and n // s <= cap] or [n])


def _ffn_in(h, w_gate, w_up):
    m, k = h.shape
    f = w_gate.shape[0]
    tm, tn = _pick(m, (1024, 512, 256, 128)), _lane_tile(f, 1408)

    def body(h_ref, wg_ref, wu_ref, act_ref, dg_ref, du_ref):
        hh = h_ref[...]
        g = _dot(hh, wg_ref[...], NT)
        u = _dot(hh, wu_ref[...], NT)
        sg = _sigmoid(g)
        silu_g = g * sg
        act_ref[...] = (silu_g * u).astype(act_ref.dtype)
        dg_ref[...] = (u * (sg + silu_g * (1.0 - sg))).astype(dg_ref.dtype)
        du_ref[...] = silu_g.astype(du_ref.dtype)

    w_spec = pl.BlockSpec((tn, k), lambda i, j: (j, 0))
    o_spec = pl.BlockSpec((tm, tn), lambda i, j: (i, j))
    return pl.pallas_call(
        body, grid=(m // tm, f // tn), in_specs=[pl.BlockSpec((tm, k), lambda i, j: (i, 0)), w_spec, w_spec],
        out_specs=[o_spec] * 3, out_shape=[jax.ShapeDtypeStruct((m, f), BF)] * 3, name="ffn_in",
        compiler_params=pltpu.CompilerParams(dimension_semantics=("parallel", "parallel")),
    )(h, w_gate, w_up)


FFN_BWD_VMEM = 58 * 1024 * 1024


def _ffn_bwd(dff, w_down, act_dg, act_du, w_gate, w_up):
    m, k = dff.shape
    f = w_down.shape[0]
    tm, tn = _pick(m, (512, 256, 128)), _lane_tile(f, 1408)
    nj = f // tn

    def body(d_ref, wd_ref, fg_ref, fu_ref, wg_ref, wu_ref, dg_ref, du_ref, dh_ref, acc_ref):
        j = pl.program_id(0)
        rows = pl.ds(pl.multiple_of(pl.program_id(1) * tm, tm), tm)
        da = _dot(d_ref[...], wd_ref[...], NT)
        dg = (da * fg_ref[...].astype(F32)).astype(BF)
        du = (da * fu_ref[...].astype(F32)).astype(BF)
        dg_ref[...] = dg
        du_ref[...] = du
        part = _dot(dg, wg_ref[...]) + _dot(du, wu_ref[...])

        @pl.when(j == 0)
        def _():
            acc_ref[rows, :] = part

        @pl.when(j > 0)
        def _():
            acc_ref[rows, :] += part

        @pl.when(j == nj - 1)
        def _():
            dh_ref[...] = acc_ref[rows, :].astype(dh_ref.dtype)

    o_spec = pl.BlockSpec((tm, tn), lambda j, i: (i, j))
    w_spec = pl.BlockSpec((tn, k), lambda j, i: (j, 0))
    row_spec = pl.BlockSpec((tm, k), lambda j, i: (i, 0))
    dh_spec = pl.BlockSpec((tm, k), lambda j, i: (jnp.where(j == nj - 1, i, 0), 0))
    return pl.pallas_call(
        body, grid=(nj, m // tm), in_specs=[row_spec, w_spec, o_spec, o_spec, w_spec, w_spec],
        out_specs=[o_spec, o_spec, dh_spec],
        out_shape=[jax.ShapeDtypeStruct((m, f), BF)] * 2 + [jax.ShapeDtypeStruct((m, k), BF)],
        scratch_shapes=[pltpu.VMEM((m, k), F32)], name="d_ffn",
        compiler_params=pltpu.CompilerParams(dimension_semantics=("arbitrary", "arbitrary"), vmem_limit_bytes=FFN_BWD_VMEM),
    )(dff, w_down, act_dg, act_du, w_gate, w_up)


def _matmul_tn_pair(a1, a2, b, name, out_dtype):
    k, m = a1.shape
    n = b.shape[1]
    tm = _lane_tile(m, 256)

    def body(a1_ref, a2_ref, b_ref, o1_ref, o2_ref):
        out = _dot(jnp.concatenate([a1_ref[...], a2_ref[...]], axis=1), b_ref[...], TN)
        o1_ref[...] = out[:tm].astype(o1_ref.dtype)
        o2_ref[...] = out[tm:].astype(o2_ref.dtype)

    a_spec = pl.BlockSpec((k, tm), lambda i: (0, i))
    o_spec = pl.BlockSpec((tm, n), lambda i: (i, 0))
    return pl.pallas_call(
        body, grid=(m // tm,), in_specs=[a_spec, a_spec, _full(b.shape)], out_specs=[o_spec] * 2,
        out_shape=[jax.ShapeDtypeStruct((m, n), out_dtype)] * 2, name=name,
        compiler_params=pltpu.CompilerParams(dimension_semantics=("parallel",)),
    )(a1, a2, b)


MATMUL_VMEM_BUDGET = 28 * 1024 * 1024


def _matmul_tiles(m, n, k, a_bytes, b_bytes, o_bytes):
    def divisors(x, cap):
        return sorted({x // s for s in range(1, 65) if x % s == 0 and (x // s) % LANE == 0 and x // s <= cap}, reverse=True) or [x]

    for tk in divisors(k, k):
        best = None
        for tm in divisors(m, 1024):
            for tn in divisors(n, 2048):
                need = 2 * (tm * tk * a_bytes + tk * tn * b_bytes + tm * tn * o_bytes) + (tm * tn * 4 if tk < k else 0)
                if need <= MATMUL_VMEM_BUDGET and tm * tn >= 512 * 512 and (best is None or tm * tn > best[0] * best[1]):
                    best = (tm, tn)
        if best:
            return best[0], best[1], tk
    return _pick(m, (512, 256, 128)), _pick(n, (512, 256, 128)), _pick(k, (512, 256, 128))


def _exchange(xs, name, scatter):
    n = len(xs)
    npeer = N_DEV - 1

    def body(*refs):
        x_refs, o_refs = refs[:n], refs[n:2 * n]
        send_sems, recv_sems, local_sems = refs[2 * n:]
        mx, my, mc = lax.axis_index("x"), lax.axis_index("y"), lax.axis_index("c")
        me = 4 * mx + 2 * my + mc
        src_me = [x.at[me] if scatter else x for x in x_refs]
        mine = [pltpu.make_async_copy(src_me[a], o_refs[a].at[me], local_sems.at[a]) for a in range(n)]
        for cp in mine:
            cp.start()
        copies = []
        for k in range(1, N_DEV):
            px, py, pc = mx ^ (k >> 2), my ^ ((k >> 1) & 1), mc ^ (k & 1)
            peer = 4 * px + 2 * py + pc
            for a in range(n):
                cp = pltpu.make_async_remote_copy(
                    src_ref=x_refs[a].at[peer] if scatter else x_refs[a], dst_ref=o_refs[a].at[me],
                    send_sem=send_sems.at[a * npeer + k - 1], recv_sem=recv_sems.at[a * npeer + k - 1],
                    device_id=(px, py, pc), device_id_type=pl.DeviceIdType.MESH)
                cp.start()
                copies.append((cp, a, k, peer))
        for cp, a, k, peer in copies:
            pltpu.make_async_remote_copy(
                src_ref=src_me[a], dst_ref=o_refs[a].at[peer], send_sem=send_sems.at[a * npeer + k - 1],
                recv_sem=recv_sems.at[a * npeer + k - 1], device_id=(mx, my, mc),
                device_id_type=pl.DeviceIdType.MESH).wait_recv()
        for cp, _, _, _ in copies:
            cp.wait_send()
        for cp in mine:
            cp.wait()

    return pl.pallas_call(
        body, out_shape=[jax.ShapeDtypeStruct((N_DEV,) + x.shape[-2:], x.dtype) for x in xs],
        in_specs=[pl.BlockSpec(memory_space=pl.ANY)] * n, out_specs=[pl.BlockSpec(memory_space=pl.ANY)] * n,
        scratch_shapes=[pltpu.SemaphoreType.DMA((n * npeer,)), pltpu.SemaphoreType.DMA((n * npeer,)),
                        pltpu.SemaphoreType.DMA((n,))],
        name=name,
    )(*xs)


def _gather_by_chip(xs, name):
    n = len(xs)
    per = N_DEV - 1

    def body(*refs):
        x_refs, o_refs = refs[:n], refs[n:2 * n]
        send_sems, recv_sems, local_sems = refs[2 * n:]
        mx, my, mc = lax.axis_index("x"), lax.axis_index("y"), lax.axis_index("c")
        me, sibling = (mx, my, mc), (mx, my, 1 - mc)
        chips = [(1 - mx, my), (mx, 1 - my), (1 - mx, 1 - my)]
        slot = lambda d: 4 * d[0] + 2 * d[1] + d[2]

        def copy(a, k, block, to, src=None):
            dst = o_refs[a].at[slot(block)]
            return pltpu.make_async_remote_copy(
                src_ref=dst if src is None else src, dst_ref=dst, send_sem=send_sems.at[a * per + k],
                recv_sem=recv_sems.at[a * per + k], device_id=to, device_id_type=pl.DeviceIdType.MESH)

        mine = [pltpu.make_async_copy(x_refs[a], o_refs[a].at[slot(me)], local_sems.at[a]) for a in range(n)]
        for cp in mine:
            cp.start()
        first = []
        for a in range(n):
            first.append(copy(a, 0, me, sibling, src=x_refs[a]))
            first += [copy(a, 1 + j, me, (*chip, mc), src=x_refs[a]) for j, chip in enumerate(chips)]
        for cp in first:
            cp.start()
        passed = []
        for j, chip in enumerate(chips):
            for a in range(n):
                copy(a, 1 + j, (*chip, mc), me).wait_recv()
                cp = copy(a, 4 + j, (*chip, mc), sibling)
                cp.start()
                passed.append(cp)
        for a in range(n):
            copy(a, 0, sibling, me).wait_recv()
            for j, chip in enumerate(chips):
                copy(a, 4 + j, (*chip, 1 - mc), me).wait_recv()
        for cp in first + passed:
            cp.wait_send()
        for cp in mine:
            cp.wait()

    return pl.pallas_call(
        body, out_shape=[jax.ShapeDtypeStruct((N_DEV,) + x.shape, x.dtype) for x in xs],
        in_specs=[pl.BlockSpec(memory_space=pl.ANY)] * n, out_specs=[pl.BlockSpec(memory_space=pl.ANY)] * n,
        scratch_shapes=[pltpu.SemaphoreType.DMA((n * per,)), pltpu.SemaphoreType.DMA((n * per,)),
                        pltpu.SemaphoreType.DMA((n,))],
        name=name,
    )(*xs)


def _peer_of(k):
    mx, my, mc = lax.axis_index("x"), lax.axis_index("y"), lax.axis_index("c")
    px, py, pc = mx ^ (k >> 2), my ^ ((k >> 1) & 1), mc ^ (k & 1)
    return (px, py, pc), 4 * px + 2 * py + pc


def _exchange_start(xs, name, scatter):
    n = len(xs)
    npeer = N_DEV - 1

    def body(*refs):
        x_refs, land_refs = refs[:n], refs[n:2 * n]
        send_sems, recv_sems, token = refs[2 * n], refs[2 * n + 1], refs[-1]
        me = 4 * lax.axis_index("x") + 2 * lax.axis_index("y") + lax.axis_index("c")
        for k in range(1, N_DEV):
            dev, peer = _peer_of(k)
            for a in range(n):
                pltpu.make_async_remote_copy(
                    src_ref=x_refs[a].at[peer] if scatter else x_refs[a], dst_ref=land_refs[a].at[me],
                    send_sem=send_sems.at[a * npeer + k - 1], recv_sem=recv_sems.at[a * npeer + k - 1],
                    device_id=dev, device_id_type=pl.DeviceIdType.MESH).start()
        token[...] = jnp.zeros_like(token)

    hbm = pl.BlockSpec(memory_space=pltpu.HBM)
    sem = pl.BlockSpec(memory_space=pltpu.SEMAPHORE)
    lands = [pltpu.with_memory_space_constraint(lax.empty((N_DEV,) + x.shape[-2:], x.dtype), pltpu.HBM) for x in xs]
    srcs = [pltpu.with_memory_space_constraint(x, pltpu.HBM) for x in xs]
    outs = pl.pallas_call(
        body, name=name,
        out_shape=(pltpu.SemaphoreType.DMA((n * npeer,)), pltpu.SemaphoreType.DMA((n * npeer,)),
                   *[pltpu.HBM(x.shape, x.dtype) for x in srcs], *[pltpu.HBM(z.shape, z.dtype) for z in lands],
                   jax.ShapeDtypeStruct((8, LANE), F32)),
        in_specs=[hbm] * (2 * n), out_specs=(sem, sem, *[hbm] * (2 * n), pl.BlockSpec(memory_space=pltpu.VMEM)),
        input_output_aliases={i: 2 + i for i in range(2 * n)},
        compiler_params=pltpu.CompilerParams(has_side_effects=pltpu.SideEffectType.DATAFLOW_SIDE_EFFECTING),
    )(*srcs, *lands)
    return (outs[0], outs[1], list(outs[2:2 + n]), list(outs[2 + n:2 + 2 * n])), outs[-1][0:1, 0:1]


def _exchange_wait(started, after, name, scatter):
    send_sems, recv_sems, srcs, lands = started
    n = len(srcs)
    npeer = N_DEV - 1

    def body(*refs):
        x_refs, land_refs = refs[:n], refs[n:2 * n]
        send_sems, recv_sems = refs[2 * n], refs[2 * n + 1]
        mx, my, mc = lax.axis_index("x"), lax.axis_index("y"), lax.axis_index("c")
        me = 4 * mx + 2 * my + mc
        for k in range(1, N_DEV):
            _, peer = _peer_of(k)
            for a in range(n):
                src = x_refs[a].at[me] if scatter else x_refs[a]
                cp = pltpu.make_async_remote_copy(
                    src_ref=src, dst_ref=land_refs[a].at[peer], send_sem=send_sems.at[a * npeer + k - 1],
                    recv_sem=recv_sems.at[a * npeer + k - 1], device_id=(mx, my, mc), device_id_type=pl.DeviceIdType.MESH)
                cp.wait_send()
                cp.wait_recv()

    hbm = pl.BlockSpec(memory_space=pltpu.HBM)
    sem = pl.BlockSpec(memory_space=pltpu.SEMAPHORE)
    outs = pl.pallas_call(
        body, name=name,
        out_shape=(*[pltpu.HBM(x.shape, x.dtype) for x in srcs], *[pltpu.HBM(z.shape, z.dtype) for z in lands]),
        in_specs=[hbm] * (2 * n) + [sem, sem, pl.BlockSpec(memory_space=pl.ANY)], out_specs=tuple([hbm] * (2 * n)),
        input_output_aliases={i: i for i in range(2 * n)},
        compiler_params=pltpu.CompilerParams(has_side_effects=pltpu.SideEffectType.DATAFLOW_SIDE_EFFECTING),
    )(*srcs, *lands, send_sems, recv_sems, after)
    me = 4 * lax.axis_index("x") + 2 * lax.axis_index("y") + lax.axis_index("c")
    full = []
    for x, land in zip(outs[:n], outs[n:]):
        own = lax.dynamic_slice(x, (me, 0, 0), (1,) + x.shape[1:]) if scatter else x[None]
        full.append(lax.dynamic_update_slice(land, own, (me, 0, 0)))
    return full


def _sum_slots(x, name):
    _, r, c = x.shape
    tr = _pick(r, (512, 256, 128, 64, 32, 16))

    def body(x_ref, o_ref):
        acc = x_ref[0].astype(F32)
        for s in range(1, N_DEV):
            acc = acc + x_ref[s].astype(F32)
        o_ref[...] = acc

    return pl.pallas_call(
        body, grid=(r // tr,), in_specs=[pl.BlockSpec((N_DEV, tr, c), lambda i: (0, i, 0))],
        out_specs=pl.BlockSpec((tr, c), lambda i: (i, 0)), out_shape=jax.ShapeDtypeStruct((r, c), F32), name=name,
        compiler_params=pltpu.CompilerParams(dimension_semantics=("arbitrary",)),
    )(x)


def _mod_fwd(c_all, w_ada, b_ada_mine):
    def body(c_ref, w_ref, b_ref, o_ref):
        o_ref[...] = _doth(_silu(c_ref[...]), w_ref[...]) + b_ref[...]

    return pl.pallas_call(body, out_shape=jax.ShapeDtypeStruct((c_all.shape[0], w_ada.shape[1]), F32), name="mod_fwd")(c_all, w_ada, b_ada_mine)


def _mod_bwd(c_all_t, dmod_mine):
    def body(ct_ref, d_ref, o_ref):
        s = _silu(ct_ref[...])
        acc = s[:, 0:1] * d_ref[pl.ds(0, 1), :]
        for b in range(1, N_DEV):
            acc = acc + s[:, b:b + 1] * d_ref[pl.ds(b, 1), :]
        o_ref[...] = acc

    return pl.pallas_call(body, out_shape=jax.ShapeDtypeStruct((c_all_t.shape[0], dmod_mine.shape[1]), F32), name="mod_bwd")(c_all_t, dmod_mine)


def _conv_fwd(proj, conv_w8, tm):
    t = proj.shape[0]
    ch = DN_CONV_CH

    def body(x_ref, w_ref, o_ref, buf):
        @pl.when(pl.program_id(0) == 0)
        def _():
            buf[pl.ds(0, CONV_HALO), :] = jnp.zeros((CONV_HALO, ch), F32)

        buf[pl.ds(CONV_HALO, tm), :] = x_ref[...].astype(F32)
        for c0 in range(0, ch, CONV_COLS):
            cols = pl.ds(c0, CONV_COLS)
            w = [w_ref[pl.ds(j, 1), cols] for j in range(CONV_K)]
            for r0 in range(0, tm, CONV_ROWS):
                acc = buf[pl.ds(r0 + CONV_HALO - (CONV_K - 1), CONV_ROWS), cols] * w[0]
                for j in range(1, CONV_K):
                    acc = acc + buf[pl.ds(r0 + CONV_HALO - (CONV_K - 1) + j, CONV_ROWS), cols] * w[j]
                o_ref[pl.ds(r0, CONV_ROWS), cols] = _silu(acc)
        buf[pl.ds(0, CONV_HALO), :] = buf[pl.ds(tm, CONV_HALO), :]

    return pl.pallas_call(
        body, grid=(t // tm,), in_specs=[pl.BlockSpec((tm, ch), lambda i: (i, 0)), _full(conv_w8.shape)],
        out_specs=pl.BlockSpec((tm, ch), lambda i: (i, 0)), out_shape=jax.ShapeDtypeStruct((t, ch), F32),
        scratch_shapes=[pltpu.VMEM((tm + CONV_HALO, ch), F32)], name="conv_fwd",
        compiler_params=pltpu.CompilerParams(dimension_semantics=("arbitrary",)),
    )(proj, conv_w8)


def _conv_bwd(proj, conv_w8, dact, others, tm):
    t = proj.shape[0]
    ch = DN_CONV_CH
    nt = t // tm
    halo_blk = 2 * CONV_HALO
    hb = tm // halo_blk
    n_others = len(others)

    def body(x_ref, xp_ref, w_ref, dy_ref, *refs):
        piece_refs, (dx_ref, dw_ref, xbuf, dbuf) = refs[:n_others], refs[n_others:]
        step = pl.program_id(0)
        for (off, arr), p_ref in zip(others, piece_refs):
            dx_ref[:, pl.ds(off, arr.shape[1])] = p_ref[...].astype(dx_ref.dtype)

        @pl.when(step == 0)
        def _():
            dbuf[pl.ds(tm, CONV_HALO), :] = jnp.zeros((CONV_HALO, ch), F32)
            dw_ref[...] = jnp.zeros_like(dw_ref)

        first = step == nt - 1
        xbuf[pl.ds(0, CONV_HALO), :] = jnp.where(first, 0.0, xp_ref[...].astype(F32)[halo_blk - CONV_HALO:])
        xbuf[pl.ds(CONV_HALO, tm), :] = x_ref[...].astype(F32)
        for c0 in range(0, ch, CONV_COLS):
            cols = pl.ds(c0, CONV_COLS)
            w = [w_ref[pl.ds(j, 1), cols] for j in range(CONV_K)]
            dw = [jnp.zeros((1, CONV_COLS), F32) for _ in range(CONV_K)]
            for r0 in range(0, tm, CONV_ROWS):
                xs = [xbuf[pl.ds(r0 + CONV_HALO - (CONV_K - 1) + j, CONV_ROWS), cols] for j in range(CONV_K)]
                pre = xs[0] * w[0]
                for j in range(1, CONV_K):
                    pre = pre + xs[j] * w[j]
                sg = _sigmoid(pre)
                dpre = dy_ref[pl.ds(r0, CONV_ROWS), cols] * (sg * (1.0 + pre * (1.0 - sg)))
                dbuf[pl.ds(r0, CONV_ROWS), cols] = dpre
                dw = [dw[j] + jnp.sum(dpre * xs[j], axis=0, keepdims=True) for j in range(CONV_K)]
            for j in range(CONV_K):
                dw_ref[pl.ds(j, 1), cols] += dw[j]
            for r0 in range(0, tm, CONV_ROWS):
                dx = dbuf[pl.ds(r0 + CONV_K - 1, CONV_ROWS), cols] * w[0]
                for j in range(1, CONV_K):
                    dx = dx + dbuf[pl.ds(r0 + CONV_K - 1 - j, CONV_ROWS), cols] * w[j]
                dx_ref[pl.ds(r0, CONV_ROWS), cols] = dx.astype(dx_ref.dtype)
        dbuf[pl.ds(tm, CONV_HALO), :] = dbuf[pl.ds(0, CONV_HALO), :]

    rev = lambda i: (nt - 1 - i, 0)
    prev = lambda i: (jnp.maximum((nt - 1 - i) * hb - 1, 0), 0)
    return pl.pallas_call(
        body, grid=(nt,),
        in_specs=[pl.BlockSpec((tm, ch), rev), pl.BlockSpec((halo_blk, ch), prev), _full(conv_w8.shape),
                  pl.BlockSpec((tm, ch), rev)] + [pl.BlockSpec((tm, arr.shape[1]), rev) for _, arr in others],
        out_specs=[pl.BlockSpec((tm, N_INP), rev), _full(conv_w8.shape)],
        out_shape=[jax.ShapeDtypeStruct((t, N_INP), BF), jax.ShapeDtypeStruct(conv_w8.shape, F32)],
        scratch_shapes=[pltpu.VMEM((tm + CONV_HALO, ch), F32), pltpu.VMEM((tm + CONV_HALO, ch), F32)], name="conv_bwd",
        compiler_params=pltpu.CompilerParams(dimension_semantics=("arbitrary",)),
    )(proj, proj, conv_w8, dact, *[arr for _, arr in others])


BNN = (((2,), (1,)), ((0,), (0,)))
BNT = (((2,), (2,)), ((0,), (0,)))
BTN = (((1,), (1,)), ((0,), (0,)))


def _bdot(a, b, dims, precision=None):
    return lax.dot_general(a, b, dims, precision=precision, preferred_element_type=F32)


@jax.custom_vjp
def _bmmb_nt(a, b):
    return _bdot(a.astype(BF), b.astype(BF), BNT)


def _bmmb_nt_fwd(a, b):
    return _bmmb_nt(a, b), (a, b)


def _bmmb_nt_bwd(res, g):
    a, b = res
    gb = g.astype(BF)
    return _bdot(gb, b.astype(BF), BNN), _bdot(gb, a.astype(BF), BTN)


_bmmb_nt.defvjp(_bmmb_nt_fwd, _bmmb_nt_bwd)


@jax.custom_vjp
def _bmmb(a, b):
    return _bdot(a.astype(BF), b.astype(BF), BNN)


def _bmmb_fwd(a, b):
    return _bmmb(a, b), (a, b)


def _bmmb_bwd(res, g):
    a, b = res
    gb = g.astype(BF)
    return _bdot(gb, b.astype(BF), BNT), _bdot(a.astype(BF), gb, BTN)


_bmmb.defvjp(_bmmb_fwd, _bmmb_bwd)


@jax.custom_vjp
def _bmmb_tn(a, b):
    return _bdot(a.astype(BF), b.astype(BF), BTN)


def _bmmb_tn_fwd(a, b):
    return _bmmb_tn(a, b), (a, b)


def _bmmb_tn_bwd(res, g):
    a, b = res
    gb = g.astype(BF)
    return _bdot(b.astype(BF), gb, BNT), _bdot(a.astype(BF), gb, BNN)


_bmmb_tn.defvjp(_bmmb_tn_fwd, _bmmb_tn_bwd)


def _triangle_sums(g, lower):
    c = g.shape[1]
    ri = lax.broadcasted_iota(jnp.int32, (g.shape[0], c, c), 1)
    ci = lax.broadcasted_iota(jnp.int32, (g.shape[0], c, c), 2)
    tri = (ri >= ci if lower else ri <= ci).astype(BF)
    hi = g.astype(BF)
    mid = (g - hi.astype(F32)).astype(BF)
    lo = (g - hi.astype(F32) - mid.astype(F32)).astype(BF)
    return _bdot(tri, hi, BNN) + _bdot(tri, mid, BNN) + _bdot(tri, lo, BNN)


@jax.custom_vjp
def _chunk_cumsum(g):
    return _triangle_sums(g, True)


_chunk_cumsum.defvjp(lambda g: (_triangle_sums(g, True), None), lambda _, ct: (_triangle_sums(ct, False),))


def _unit_lower_solve_fwd(a, r):
    c = a.shape[-1]
    ri = lax.broadcasted_iota(jnp.int32, a.shape, 1)
    ci = lax.broadcasted_iota(jnp.int32, a.shape, 2)
    xm = -a
    inv = (ri == ci).astype(F32) + xm
    for _ in range(int(math.log2(c)) - 1):
        xm = _bdot(xm, xm, BNN, HI)
        inv = inv + _bdot(inv, xm, BNN, HI)
    x = _bdot(inv, r, BNN, HI)
    return x, (inv, x)


def _unit_lower_solve_bwd(res, g):
    inv, x = res
    dr = _bdot(inv, g, BTN, HI)
    return -_bdot(dr, x, BNT, HI), dr


@jax.custom_vjp
def _unit_lower_solve_given(a, r, inv):
    return _bdot(inv, r, BNN, HI)


def _unit_lower_solve_given_fwd(a, r, inv):
    x = _bdot(inv, r, BNN, HI)
    return x, (inv, x)


def _unit_lower_solve_given_bwd(res, g):
    da, dr = _unit_lower_solve_bwd(res, g)
    return da, dr, jnp.zeros_like(res[0])


_unit_lower_solve_given.defvjp(_unit_lower_solve_given_fwd, _unit_lower_solve_given_bwd)


def _gdn_intra(qkv, ba, al8, dt8, inv4=None):
    tm = qkv.shape[0]
    nb = tm // CHUNK
    bsz = DN_HEADS * nb

    def heads(x0):
        return jnp.concatenate([qkv[:, x0 + h * LANE:x0 + (h + 1) * LANE].reshape(nb, CHUNK, LANE) for h in range(DN_HEADS)], axis=0)

    def spread(c0):
        return jnp.concatenate([jnp.broadcast_to(ba[:, c0 + h:c0 + h + 1], (tm, LANE)).reshape(nb, CHUNK, LANE)
                                for h in range(DN_HEADS)], axis=0)

    def per_head(v8):
        return jnp.concatenate([jnp.broadcast_to(v8[0:1, h:h + 1].reshape(1, 1, 1), (nb, 1, LANE)) for h in range(DN_HEADS)], axis=0)

    ri = lax.broadcasted_iota(jnp.int32, (bsz, CHUNK, CHUNK), 1)
    ci = lax.broadcasted_iota(jnp.int32, (bsz, CHUNK, CHUNK), 2)
    incl = ri >= ci
    strict = ri > ci

    q = _l2norm(heads(0)) * (DN_DK ** -0.5)
    k = _l2norm(heads(DN_QK))
    va = heads(2 * DN_QK)
    beta = _sigmoid(spread(0))
    g = -jnp.exp(per_head(al8)) * _softplus(spread(DN_HEADS) + per_head(dt8))
    gc = _chunk_cumsum(g)
    g_last = jnp.sum(g, axis=1, keepdims=True)
    gcol = gc[:, :, :CHUNK]
    diff = gcol - jnp.swapaxes(gcol, 1, 2)
    decay = jnp.where(incl, jnp.exp(jnp.where(incl, diff, 0.0)), 0.0)
    kb = k * beta
    a_mat = jnp.where(strict, _bmmb_nt(kb, k) * decay, 0.0)
    egc = jnp.exp(gc)
    rhs = jnp.concatenate([kb * egc, va * beta], axis=2)
    if inv4 is None:
        wu, (inv, _) = _unit_lower_solve_fwd(a_mat, rhs)
    else:
        wu = _unit_lower_solve_given(a_mat, rhs, inv4.reshape(bsz, CHUNK, CHUNK))
    attn = jnp.where(incl, _bmmb_nt(q, k) * decay, 0.0)

    def unheads(x):
        return jnp.concatenate([x[h * nb:(h + 1) * nb].reshape(tm, LANE) for h in range(DN_HEADS)], axis=1)

    w_c, u_c = wu[:, :, :DN_DK], wu[:, :, DN_DK:]
    kd = k * jnp.exp(g_last - gc)
    out = (unheads(q * egc - _bmmb(attn, w_c)), unheads(_bmmb(attn, u_c)),
           _bmmb_tn(kd, w_c).reshape(DN_HEADS, nb, DN_DK, DN_DK), _bmmb_tn(kd, u_c).reshape(DN_HEADS, nb, DN_DK, DN_DV),
           jnp.broadcast_to(g_last, (bsz, GL_ROWS, LANE)).reshape(DN_HEADS, nb, GL_ROWS, LANE))
    return out if inv4 is not None else out + (inv.reshape(DN_HEADS, nb, CHUNK, CHUNK),)


def _gdn_scan_step(qp, op, c_mat, n_mat, gl, s):
    return _mmb(qp, s) + op, s * jnp.exp(gl) - _mmb(c_mat, s) + n_mat


def _gdn_intra_specs(t, tm, dts, order=lambda i: i):
    nb = tm // CHUNK
    row = pl.BlockSpec((tm, DN_VW), lambda i: (order(i), 0))
    mat = pl.BlockSpec((DN_HEADS, nb, DN_DK, DN_DV), lambda i: (0, order(i), 0, 0))
    row_shape = lambda d: jax.ShapeDtypeStruct((t, DN_VW), d)
    mat_shape = lambda d: jax.ShapeDtypeStruct((DN_HEADS, t // CHUNK, DN_DK, DN_DV), d)
    gl = pl.BlockSpec((DN_HEADS, nb, GL_ROWS, LANE), lambda i: (0, order(i), 0, 0))
    gl_shape = jax.ShapeDtypeStruct((DN_HEADS, t // CHUNK, GL_ROWS, LANE), dts[4])
    return [row, row, mat, mat, gl], [row_shape(dts[0]), row_shape(dts[1]), mat_shape(dts[2]), mat_shape(dts[3]), gl_shape]


def _gdn_intra_fwd(qkv, ba, al8, dt8, tm):
    t = qkv.shape[0]

    def body(qkv_ref, ba_ref, al_ref, dt_ref, *outs):
        for o, val in zip(outs, _gdn_intra(qkv_ref[...], ba_ref[...], al_ref[...], dt_ref[...])):
            o[...] = val.astype(o.dtype)

    specs, shapes = _gdn_intra_specs(t, tm, (BF, F32, BF, BF, F32))
    specs.append(_gdn_inverse_spec(tm))
    shapes.append(jax.ShapeDtypeStruct((DN_HEADS, t // CHUNK, CHUNK, CHUNK), F32))
    res = pl.pallas_call(
        body, grid=(t // tm,),
        in_specs=[pl.BlockSpec((tm, DN_CONV_CH), lambda i: (i, 0)), pl.BlockSpec((tm, LANE), lambda i: (i, 0)),
                  _full(al8.shape), _full(dt8.shape)],
        out_specs=specs, out_shape=shapes, name="gdn_intra_fwd",
        compiler_params=pltpu.CompilerParams(dimension_semantics=("parallel",)),
    )(qkv, ba, al8, dt8)
    return res[:5], res[5]


def _gdn_inverse_spec(tm):
    return pl.BlockSpec((DN_HEADS, tm // CHUNK, CHUNK, CHUNK), lambda i: (0, i, 0, 0))


def _gdn_intra_bwd(qkv, ba, al8, dt8, inverses, cts, tm):
    t = qkv.shape[0]

    def body(qkv_ref, ba_ref, al_ref, dt_ref, inv_ref, *refs):
        ct_refs, (dqkv_ref, dba_ref, dal_ref, ddt_ref) = refs[:5], refs[5:]

        @pl.when(pl.program_id(0) == 0)
        def _():
            dal_ref[...] = jnp.zeros_like(dal_ref)
            ddt_ref[...] = jnp.zeros_like(ddt_ref)

        _, vjp = jax.vjp(functools.partial(_gdn_intra, inv4=inv_ref[...]), qkv_ref[...], ba_ref[...], al_ref[...], dt_ref[...])
        dqkv, dba, dal, ddt = vjp(tuple(r[...].astype(F32) for r in ct_refs))
        dqkv_ref[...] = dqkv.astype(dqkv_ref.dtype)
        dba_ref[...] = dba.astype(dba_ref.dtype)
        dal_ref[...] += dal
        ddt_ref[...] += ddt

    specs, _ = _gdn_intra_specs(t, tm, (F32,) * 5)
    return pl.pallas_call(
        body, grid=(t // tm,),
        in_specs=[pl.BlockSpec((tm, DN_CONV_CH), lambda i: (i, 0)), pl.BlockSpec((tm, LANE), lambda i: (i, 0)),
                  _full(al8.shape), _full(dt8.shape), _gdn_inverse_spec(tm)] + specs,
        out_specs=[pl.BlockSpec((tm, DN_CONV_CH), lambda i: (i, 0)), pl.BlockSpec((tm, LANE), lambda i: (i, 0)),
                   _full(al8.shape), _full(dt8.shape)],
        out_shape=[jax.ShapeDtypeStruct((t, DN_CONV_CH), BF), jax.ShapeDtypeStruct((t, LANE), BF),
                   jax.ShapeDtypeStruct(al8.shape, F32), jax.ShapeDtypeStruct(dt8.shape, F32)],
        name="gdn_intra_bwd", compiler_params=pltpu.CompilerParams(dimension_semantics=("arbitrary",)),
    )(qkv, ba, al8, dt8, inverses, *cts)


def _gdn_scan_fwd(intra, tm):
    t = intra[0].shape[0]
    nb = tm // CHUNK
    nc = t // CHUNK

    def body(qp_ref, op_ref, c_ref, n_ref, gl_ref, o_ref, ss_ref, s_scr):
        @pl.when(pl.program_id(0) == 0)
        def _():
            s_scr[...] = jnp.zeros_like(s_scr)

        state = [s_scr[h] for h in range(DN_HEADS)]
        for cc in range(nb):
            rows = pl.ds(cc * CHUNK, CHUNK)
            for h in range(DN_HEADS):
                cols = pl.ds(h * DN_DV, DN_DV)
                ss_ref[cc, h] = state[h].astype(ss_ref.dtype)
                o_ref[rows, cols], state[h] = _gdn_scan_step(
                    qp_ref[rows, cols], op_ref[rows, cols], c_ref[h, cc], n_ref[h, cc], gl_ref[h, cc, pl.ds(0, 1), :], state[h])
        for h in range(DN_HEADS):
            s_scr[h] = state[h]

    specs, _ = _gdn_intra_specs(t, tm, (F32,) * 5)
    return pl.pallas_call(
        body, grid=(t // tm,), in_specs=specs,
        out_specs=[pl.BlockSpec((tm, DN_VW), lambda i: (i, 0)),
                   pl.BlockSpec((nb, DN_HEADS, DN_DK, DN_DV), lambda i: (i, 0, 0, 0))],
        out_shape=[jax.ShapeDtypeStruct((t, DN_VW), F32), jax.ShapeDtypeStruct((nc, DN_HEADS, DN_DK, DN_DV), BF)],
        scratch_shapes=[pltpu.VMEM((DN_HEADS, DN_DK, DN_DV), F32)], name="gdn_scan_fwd",
        compiler_params=pltpu.CompilerParams(dimension_semantics=("arbitrary",)),
    )(*intra)


def _gdn_scan_bwd(intra, states, do, tm):
    t = intra[0].shape[0]
    nb = tm // CHUNK
    ng = t // tm

    def body(qp_ref, op_ref, c_ref, n_ref, gl_ref, ss_ref, do_ref, dqp_ref, dop_ref, dc_ref, dn_ref, dgl_ref, ds_scr):
        @pl.when(pl.program_id(0) == 0)
        def _():
            ds_scr[...] = jnp.zeros_like(ds_scr)

        d_state = [ds_scr[h] for h in range(DN_HEADS)]
        for cc in reversed(range(nb)):
            rows = pl.ds(cc * CHUNK, CHUNK)
            for h in range(DN_HEADS):
                cols = pl.ds(h * DN_DV, DN_DV)
                _, vjp = jax.vjp(_gdn_scan_step, qp_ref[rows, cols].astype(F32), op_ref[rows, cols], c_ref[h, cc].astype(F32),
                                 n_ref[h, cc].astype(F32), gl_ref[h, cc, pl.ds(0, 1), :], ss_ref[cc, h].astype(F32))
                dqp_ref[rows, cols], dop_ref[rows, cols], dc, dn, dgl, d_state[h] = vjp((do_ref[rows, cols], d_state[h]))
                dc_ref[h, cc] = dc.astype(dc_ref.dtype)
                dn_ref[h, cc] = dn.astype(dn_ref.dtype)
                first_row = lax.broadcasted_iota(jnp.int32, (GL_ROWS, LANE), 0) == 0
                dgl_ref[h, cc] = jnp.where(first_row, dgl, 0.0)
        for h in range(DN_HEADS):
            ds_scr[h] = d_state[h]

    five, shapes = _gdn_intra_specs(t, tm, (F32, F32, BF, BF, F32), order=lambda i: ng - 1 - i)
    row = five[0]
    return pl.pallas_call(
        body, grid=(ng,),
        in_specs=five + [pl.BlockSpec((nb, DN_HEADS, DN_DK, DN_DV), lambda i: (ng - 1 - i, 0, 0, 0)), row],
        out_specs=five, out_shape=shapes,
        scratch_shapes=[pltpu.VMEM((DN_HEADS, DN_DK, DN_DV), F32)], name="gdn_scan_bwd",
        compiler_params=pltpu.CompilerParams(dimension_semantics=("arbitrary",)),
    )(*intra, states, do)


def _gdn_out(o, z, g):
    parts = []
    for h in range(DN_HEADS):
        sl = slice(h * DN_DV, (h + 1) * DN_DV)
        parts.append(_rmsnorm(o[:, sl], g) * _silu(z[:, sl]))
    return parts


_Q_SCALE = math.log2(math.e) / math.sqrt(QK_NOPE + QK_ROPE)


def _rope_tables(pos, inv_freq2):
    lane = lax.broadcasted_iota(jnp.int32, (1, LANE), 1)
    ang = pos * inv_freq2
    cos = jnp.where(lane < QK_ROPE, jnp.cos(ang), 0.0)
    sin = jnp.where(lane < QK_ROPE // 2, -jnp.sin(ang), jnp.where(lane < QK_ROPE, jnp.sin(ang), 0.0))
    return cos, sin


@jax.custom_vjp
def _rope_swap(u):
    lane = lax.broadcasted_iota(jnp.int32, u.shape, 1)
    half = QK_ROPE // 2
    return jnp.where(lane < half, pltpu.roll(u, LANE - half, 1), jnp.where(lane < QK_ROPE, pltpu.roll(u, half, 1), 0.0))


_rope_swap.defvjp(lambda u: (_rope_swap(u), None), lambda _, g: (_rope_swap(g),))


def _mla_prep(cq, ckv, kr, gq, gkv, w_uq, w_ukv, cos, sin):
    rope = lambda u: u * cos + _rope_swap(u) * sin
    q_lin = _mmb_nt(_rmsnorm(cq, gq), w_uq) * _Q_SCALE
    kv_lin = _mmb_nt(_rmsnorm(ckv, gkv), w_ukv)
    k_rope = rope(kr)
    qs, ks, vs = [], [], []
    for h in range(MLA_HEADS):
        qs += [q_lin[:, h * LANE:(h + 1) * LANE], rope(q_lin[:, (MLA_HEADS + h) * LANE:(MLA_HEADS + h + 1) * LANE])]
        ks += [kv_lin[:, 2 * h * LANE:(2 * h + 1) * LANE], k_rope]
        vs += [kv_lin[:, (2 * h + 1) * LANE:(2 * h + 2) * LANE]]
    return qs + ks + vs


def _mla_prep_fwd(proj, pos_col, inv_freq2, gq, gkv, w_uq, w_ukv, tm):
    t = proj.shape[0]
    nq = 2 * MLA_HEADS

    def body(cq_ref, ckv_ref, kr_ref, pos_ref, f_ref, gq_ref, gkv_ref, wq_ref, wkv_ref, q_ref, k_ref, v_ref):
        cos, sin = _rope_tables(pos_ref[...], f_ref[...])
        outs = _mla_prep(cq_ref[...].astype(F32), ckv_ref[...].astype(F32), kr_ref[...].astype(F32), gq_ref[...], gkv_ref[...], wq_ref[...], wkv_ref[...],
                         cos, sin)
        for i in range(nq):
            q_ref[:, pl.ds(i * LANE, LANE)] = outs[i].astype(q_ref.dtype)
            k_ref[:, pl.ds(i * LANE, LANE)] = outs[nq + i].astype(k_ref.dtype)
        for h in range(MLA_HEADS):
            v_ref[:, pl.ds(h * LANE, LANE)] = outs[2 * nq + h].astype(v_ref.dtype)

    row = lambda w, j: pl.BlockSpec((tm, w), functools.partial(lambda i, j: (i, j), j=j))
    return pl.pallas_call(
        body, grid=(t // tm,),
        in_specs=[row(Q_LORA, P_CQ // Q_LORA), row(KV_LORA, P_CKV // KV_LORA), row(LANE, P_KR // LANE),
                  pl.BlockSpec((tm, 1), lambda i: (i, 0)), _full(inv_freq2.shape), _full(gq.shape), _full(gkv.shape),
                  _full(w_uq.shape), _full(w_ukv.shape)],
        out_specs=[row(nq * LANE, 0), row(nq * LANE, 0), row(MLA_VW, 0)],
        out_shape=[jax.ShapeDtypeStruct((t, nq * LANE), BF), jax.ShapeDtypeStruct((t, nq * LANE), BF),
                   jax.ShapeDtypeStruct((t, MLA_VW), BF)],
        name="mla_prep_fwd", compiler_params=pltpu.CompilerParams(dimension_semantics=("arbitrary",)),
    )(proj, proj, proj, pos_col, inv_freq2, gq, gkv, w_uq, w_ukv)


def _mla_prep_bwd(proj, pos_col, inv_freq2, gq, gkv, w_uq, w_ukv, dq, dk, dv, tm):
    t = proj.shape[0]
    nq = 2 * MLA_HEADS

    def body(cq_ref, ckv_ref, kr_ref, pos_ref, f_ref, gq_ref, gkv_ref, wq_ref, wkv_ref, dq_ref, dk_ref, dv_ref,
             dcq_ref, dckv_ref, dkr_ref, dgq_ref, dgkv_ref, dwq_ref, dwkv_ref):
        @pl.when(pl.program_id(0) == 0)
        def _():
            for o in (dgq_ref, dgkv_ref, dwq_ref, dwkv_ref):
                o[...] = jnp.zeros_like(o)

        cos, sin = _rope_tables(pos_ref[...], f_ref[...])
        f = functools.partial(_mla_prep, cos=cos, sin=sin)
        _, vjp = jax.vjp(f, cq_ref[...].astype(F32), ckv_ref[...].astype(F32), kr_ref[...].astype(F32), gq_ref[...], gkv_ref[...], wq_ref[...], wkv_ref[...])
        cts = [dq_ref[:, pl.ds(i * LANE, LANE)] for i in range(nq)]
        cts += [dk_ref[:, pl.ds(i * LANE, LANE)] for i in range(nq)]
        cts += [dv_ref[:, pl.ds(h * LANE, LANE)] for h in range(MLA_HEADS)]
        dcq, dckv, dkr, dgq, dgkv, dwq, dwkv = vjp(cts)
        dcq_ref[...] = dcq.astype(dcq_ref.dtype)
        dckv_ref[...] = dckv.astype(dckv_ref.dtype)
        dkr_ref[...] = dkr.astype(dkr_ref.dtype)
        dgq_ref[...] += dgq
        dgkv_ref[...] += dgkv
        dwq_ref[...] += dwq
        dwkv_ref[...] += dwkv

    row = lambda w, j: pl.BlockSpec((tm, w), functools.partial(lambda i, j: (i, j), j=j))
    return pl.pallas_call(
        body, grid=(t // tm,),
        in_specs=[row(Q_LORA, P_CQ // Q_LORA), row(KV_LORA, P_CKV // KV_LORA), row(LANE, P_KR // LANE),
                  pl.BlockSpec((tm, 1), lambda i: (i, 0)), _full(inv_freq2.shape), _full(gq.shape), _full(gkv.shape),
                  _full(w_uq.shape), _full(w_ukv.shape), row(nq * LANE, 0), row(nq * LANE, 0), row(MLA_VW, 0)],
        out_specs=[row(Q_LORA, 0), row(KV_LORA, 0), row(LANE, 0), _full(gq.shape), _full(gkv.shape),
                   _full(w_uq.shape), _full(w_ukv.shape)],
        out_shape=[jax.ShapeDtypeStruct((t, Q_LORA), BF), jax.ShapeDtypeStruct((t, KV_LORA), BF),
                   jax.ShapeDtypeStruct((t, LANE), BF), jax.ShapeDtypeStruct(gq.shape, F32),
                   jax.ShapeDtypeStruct(gkv.shape, F32), jax.ShapeDtypeStruct(w_uq.shape, F32),
                   jax.ShapeDtypeStruct(w_ukv.shape, F32)],
        name="mla_prep_bwd", compiler_params=pltpu.CompilerParams(dimension_semantics=("arbitrary",)),
    )(proj, proj, proj, pos_col, inv_freq2, gq, gkv, w_uq, w_ukv, dq, dk, dv)


_NEG = -1e30
_LN2 = math.log(2.0)
ATT_CHAINS = 2


def _causal(tq, tk, q0, k0):
    row = q0 + lax.broadcasted_iota(jnp.int32, (tq, tk), 0)
    col = k0 + lax.broadcasted_iota(jnp.int32, (tq, tk), 1)
    return col <= row


def _attn_fwd(q, k, v, tq, tk):
    t = q.shape[0]

    assert tk % tq == 0 or tq % tk == 0
    n_diag = max(1, tq // tk)

    th = tq // ATT_CHAINS

    def body(q_ref, k_ref, v_ref, o_ref, lse_ref):
        i = pl.program_id(1)
        n_full = (i * tq) // tk

        def step(k0, carry, masked):
            out = []
            for c, (m, l, acc) in enumerate(carry):
                kw = min(tk, (c + 1) * th) if masked and tk == tq else tk
                kt = k_ref[pl.ds(k0, kw), :]
                vt = v_ref[pl.ds(k0, kw), :]
                s = _dot(q_ref[pl.ds(c * th, th), :], kt, NT)
                if masked:
                    s = jnp.where(_causal(th, kw, i * tq + c * th, k0), s, _NEG)
                m_new = jnp.maximum(m, jnp.max(s, axis=-1, keepdims=True))
                p = jnp.exp2(s - m_new)
                alpha = jnp.exp2(m - m_new)
                out.append((m_new, alpha * l + jnp.sum(p, axis=-1, keepdims=True), alpha * acc + _dot(p.astype(BF), vt)))
            return tuple(out)

        init = tuple((jnp.full((th, 1), _NEG, F32), jnp.zeros((th, 1), F32), jnp.zeros((th, V_HEAD), F32)) for _ in range(ATT_CHAINS))
        carry = lax.fori_loop(0, n_full, lambda j, c: step(pl.multiple_of(j * tk, tk), c, False), init)
        for dd in range(n_diag):
            carry = step(pl.multiple_of((n_full + dd) * tk, tk), carry, True)
        for c, (m, l, acc) in enumerate(carry):
            o_ref[pl.ds(c * th, th), :] = acc / l
            lse_ref[pl.ds(c * th, th), :] = jnp.broadcast_to(m + jnp.log2(l), (th, LANE))

    return pl.pallas_call(
        body, grid=(MLA_HEADS, t // tq),
        in_specs=[pl.BlockSpec((tq, 2 * LANE), lambda h, i: (i, h)), pl.BlockSpec((t, 2 * LANE), lambda h, i: (0, h)),
                  pl.BlockSpec((t, V_HEAD), lambda h, i: (0, h))],
        out_specs=[pl.BlockSpec((tq, V_HEAD), lambda h, i: (i, h)), pl.BlockSpec((tq, LANE), lambda h, i: (i, h))],
        out_shape=[jax.ShapeDtypeStruct((t, MLA_VW), F32), jax.ShapeDtypeStruct((t, MLA_HEADS * LANE), F32)],
        name="attn_fwd", compiler_params=pltpu.CompilerParams(dimension_semantics=("parallel", "arbitrary")),
    )(q, k, v)


def _attn_bwd(q, k, v, do, lse, delta, tq, tk):
    t = q.shape[0]
    nkt = t // tk
    assert tk % tq == 0

    def body(q_ref, k_ref, v_ref, do_ref, lse_ref, dl_ref, dq_ref, dk_ref, dv_ref):
        j = pl.program_id(1)

        @pl.when(j == 0)
        def _():
            dq_ref[...] = jnp.zeros_like(dq_ref)

        kt = k_ref[...]
        vt = v_ref[...]

        def step(q0, carry, masked, kw=tk):
            dk, dv = carry
            rows = pl.ds(q0, tq)
            qt = q_ref[rows, :]
            dot_ = do_ref[rows, :]
            ktw, vtw = kt[:kw], vt[:kw]
            p = jnp.exp2(_dot(qt, ktw, NT) - lse_ref[rows, pl.ds(0, 1)])
            if masked:
                p = jnp.where(_causal(tq, kw, q0, j * tk), p, 0.0)
            dv_w = _dot(p.astype(BF), dot_, TN)
            ds = (p * (_dot(dot_, vtw, NT) - dl_ref[rows, pl.ds(0, 1)])).astype(BF)
            dk_w = _dot(ds, qt, TN)
            dq_ref[rows, :] += _dot(ds, ktw)
            if kw == tk:
                return dk + dk_w, dv + dv_w
            return (jnp.concatenate([dk[:kw] + dk_w, dk[kw:]], axis=0), jnp.concatenate([dv[:kw] + dv_w, dv[kw:]], axis=0))

        per = tk // tq
        carry = (jnp.zeros((tk, 2 * LANE), F32), jnp.zeros((tk, V_HEAD), F32))
        for dd in range(per):
            carry = step(pl.multiple_of(j * tk + dd * tq, tq), carry, True, kw=(dd + 1) * tq)

        def group(g, c):
            for dd in range(per):
                c = step(pl.multiple_of(g * tk + dd * tq, tq), c, False)
            return c

        dk, dv = lax.fori_loop(j + 1, nkt, group, carry)
        dk_ref[...] = dk * _LN2
        dv_ref[...] = dv

        @pl.when(j == nkt - 1)
        def _():
            dq_ref[...] = dq_ref[...] * _LN2

    return pl.pallas_call(
        body, grid=(MLA_HEADS, nkt),
        in_specs=[pl.BlockSpec((t, 2 * LANE), lambda h, j: (0, h)), pl.BlockSpec((tk, 2 * LANE), lambda h, j: (j, h)),
                  pl.BlockSpec((tk, V_HEAD), lambda h, j: (j, h)), pl.BlockSpec((t, V_HEAD), lambda h, j: (0, h)),
                  pl.BlockSpec((t, LANE), lambda h, j: (0, h)), pl.BlockSpec((t, LANE), lambda h, j: (0, h))],
        out_specs=[pl.BlockSpec((t, 2 * LANE), lambda h, j: (0, h)), pl.BlockSpec((tk, 2 * LANE), lambda h, j: (j, h)),
                   pl.BlockSpec((tk, V_HEAD), lambda h, j: (j, h))],
        out_shape=[jax.ShapeDtypeStruct((t, MLA_HEADS * 2 * LANE), F32), jax.ShapeDtypeStruct((t, MLA_HEADS * 2 * LANE), F32),
                   jax.ShapeDtypeStruct((t, MLA_VW), F32)],
        name="attn_bwd", compiler_params=pltpu.CompilerParams(dimension_semantics=("parallel", "arbitrary")),
    )(q, k, v, do, lse, delta)


def _adam_update(w, g, m, v):
    mm = ADAM_B1 * m + (1.0 - ADAM_B1) * g
    vv = ADAM_B2 * v + (1.0 - ADAM_B2) * jnp.square(g)
    m_hat = mm / (1.0 - ADAM_B1 ** ADAM_STEP)
    v_hat = vv / (1.0 - ADAM_B2 ** ADAM_STEP)
    return -ADAM_LR * (m_hat / (jnp.sqrt(v_hat) + ADAM_EPS) + ADAM_WD * w), mm, vv


def _adamw(w, g, m, v, name):
    r, c = w.shape
    tr = max([r // s for s in range(1, r // 8 + 1) if r % s == 0 and (r // s) % 8 == 0 and r // s <= 512] or [r])
    slots = g.ndim == 3

    def body(w_ref, g_ref, m_ref, v_ref, g_out, d_ref, nm_ref, nv_ref):
        if slots:
            gg = g_ref[0].astype(F32)
            for s in range(1, N_DEV):
                gg = gg + g_ref[s].astype(F32)
        else:
            gg = g_ref[...]
        g_out[...] = gg
        d_ref[...], nm_ref[...], nv_ref[...] = _adam_update(w_ref[...], gg, m_ref[...], v_ref[...])

    spec = pl.BlockSpec((tr, c), lambda i: (i, 0))
    g_spec = pl.BlockSpec((N_DEV, tr, c), lambda i: (0, i, 0)) if slots else spec
    return pl.pallas_call(
        body, grid=(r // tr,), in_specs=[spec, g_spec, spec, spec], out_specs=[spec] * 4,
        out_shape=[jax.ShapeDtypeStruct((r, c), F32)] * 4, name=name,
        compiler_params=pltpu.CompilerParams(dimension_semantics=("arbitrary",)),
    )(w, g, m, v)


def _adamw_many(ws, gs, ms, vs, name):
    n = len(ws)

    def body(*refs):
        for i in range(n):
            w_ref, g_ref, m_ref, v_ref = (refs[j * n + i] for j in range(4))
            d_ref, nm_ref, nv_ref = (refs[(4 + j) * n + i] for j in range(3))
            d_ref[...], nm_ref[...], nv_ref[...] = _adam_update(w_ref[...], g_ref[...], m_ref[...], v_ref[...])

    shapes = [jax.ShapeDtypeStruct(w.shape, F32) for w in ws]
    outs = pl.pallas_call(body, out_shape=shapes * 3, name=name)(*ws, *gs, *ms, *vs)
    return outs[:n], outs[n:2 * n], outs[2 * n:]


def _cast_bf16(xs, name, after=None):
    n = len(xs)
    extra = [] if after is None else [after]

    def body(*refs):
        outs = refs[n + len(extra):]
        for i in range(n):
            outs[i][...] = refs[i][...].astype(BF)

    vmem = pl.BlockSpec(memory_space=pltpu.VMEM)
    return pl.pallas_call(
        body, out_shape=[jax.ShapeDtypeStruct(x.shape, BF) for x in xs], name=name,
        in_specs=[vmem] * n + [pl.BlockSpec(memory_space=pl.ANY)] * len(extra), out_specs=[vmem] * n)(*xs, *extra)


def _pad_rows(a, n):
    return jnp.pad(a, ((0, n - a.shape[0]), (0, 0)))


def _w_in_to_padded(wt):
    s_ba = P_CQ
    s_cq = s_ba + 2 * DN_HEADS
    s_kr = s_cq + Q_LORA + KV_LORA
    return jnp.concatenate([wt[:s_ba], wt[s_cq:s_kr], _pad_rows(wt[s_ba:s_cq], LANE), _pad_rows(wt[s_kr:], LANE)], axis=0)


def _w_in_from_padded(wt):
    return jnp.concatenate([wt[:P_CQ], wt[P_BA:P_BA + 2 * DN_HEADS], wt[P_CQ:P_BA], wt[P_KR:P_KR + QK_ROPE]], axis=0)


def _w_uq_to_padded(wt):
    w3 = wt.reshape(MLA_HEADS, QK_NOPE + QK_ROPE, Q_LORA)
    nope = w3[:, :QK_NOPE].reshape(MLA_HEADS * QK_NOPE, Q_LORA)
    rope = jnp.pad(w3[:, QK_NOPE:], ((0, 0), (0, LANE - QK_ROPE), (0, 0))).reshape(MLA_HEADS * LANE, Q_LORA)
    return jnp.concatenate([nope, rope], axis=0)


def _w_uq_from_padded(wt):
    nope = wt[:MLA_HEADS * QK_NOPE].reshape(MLA_HEADS, QK_NOPE, Q_LORA)
    rope = wt[MLA_HEADS * QK_NOPE:].reshape(MLA_HEADS, LANE, Q_LORA)[:, :QK_ROPE]
    return jnp.concatenate([nope, rope], axis=1).reshape(MLA_HEADS * (QK_NOPE + QK_ROPE), Q_LORA)


def _pack(pieces, width, row_mult):
    flat = jnp.concatenate([p.reshape(-1) for p in pieces])
    n = flat.shape[0]
    rows = -(-n // (width * row_mult)) * row_mult
    return jnp.pad(flat, (0, rows * width - n)).reshape(rows, width)


def _unpack(flat, shapes):
    out, o = [], 0
    for s in shapes:
        n = math.prod(s)
        out.append(flat[o:o + n].reshape(s))
        o += n
    return out


def kernel(x, c, positions, w_ada, b_ada, w_in, conv_w, a_log, dt_bias, dn_norm_g, q_norm_g, w_uq, kv_norm_g, w_ukv, w_o, ln1_g, ln1_b, w_gate, w_up, w_down, ln2_g, ln2_b, loss_target, m_w_ada, m_b_ada, m_w_in, m_conv_w, m_a_log, m_dt_bias, m_dn_norm_g, m_q_norm_g, m_w_uq, m_kv_norm_g, m_w_ukv, m_w_o, m_ln1_g, m_ln1_b, m_w_gate, m_w_up, m_w_down, m_ln2_g, m_ln2_b, v_w_ada, v_b_ada, v_w_in, v_conv_w, v_a_log, v_dt_bias, v_dn_norm_g, v_q_norm_g, v_w_uq, v_kv_norm_g, v_w_ukv, v_w_o, v_ln1_g, v_ln1_b, v_w_gate, v_w_up, v_w_down, v_ln2_g, v_ln2_b):
    me = 4 * lax.axis_index("x") + 2 * lax.axis_index("y") + lax.axis_index("c")
    t, d = x.shape[1], x.shape[2]
    ada_n = w_ada.shape[2]

    tr = lambda w: w[0].T
    rows = lambda a: a.reshape(-1, a.shape[2])
    (in_shard,) = _cast_bf16([tr(w_in)], "cast_w_in")
    cw = conv_w.shape[3]
    a_in, c_all, conv_all = _gather_by_chip([in_shard, c, conv_w[0, :, 0, :]], "gather_w_in_and_small")
    c_all = c_all.reshape(N_DEV, d)
    conv_full = conv_all.transpose(1, 0, 2).reshape(CONV_K, N_DEV * cw)
    conv_w8 = jnp.pad(conv_full, ((0, 8 - CONV_K), (0, 0)))

    b_ada_mine = lax.dynamic_slice(b_ada, (0, me * ada_n), (1, ada_n))
    mod_cols = _mod_fwd(c_all, w_ada[0], b_ada_mine)
    (mod_all,) = _exchange([mod_cols.reshape(N_DEV, 1, ada_n)], "scatter_mod", scatter=True)
    mod = mod_all.reshape(1, N_DEV * ada_n)

    later = _cast_bf16([tr(w_uq), tr(w_ukv), w_o[0], tr(w_gate), tr(w_up), w_down[0]], "cast_weights", after=mod)
    mixer_gather, token_a = _exchange_start(later[:3], "gather_mixer_weights_start", scatter=False)
    ffn_gather, token_b = _exchange_start(later[3:], "gather_ffn_weights_start", scatter=False)
    mod = mod + (token_a + token_b)
    w_in_t = _w_in_to_padded(rows(a_in))

    def mixer_weights(after):
        a_uq, a_ukv, a_o = _exchange_wait(mixer_gather, after, "gather_mixer_weights_wait", scatter=False)
        return _w_uq_to_padded(rows(a_uq)), rows(a_ukv), rows(a_o)

    def ffn_weights(after):
        a_gate, a_up, a_down = _exchange_wait(ffn_gather, after, "gather_ffn_weights_wait", scatter=False)
        return rows(a_gate), rows(a_up), rows(a_down)

    def by_dest(g):
        return g.reshape(N_DEV, -1, g.shape[1])

    scatters = {}

    def grads_ready(tag, *g):
        if tag == "ffn":
            pieces = [by_dest(a) for a in g]
        elif tag == "mixer":
            g_w_o, g_w_uq_t, g_w_ukv_t = g
            pieces = [by_dest(g_w_o), by_dest(_w_uq_from_padded(g_w_uq_t).astype(BF)), by_dest(g_w_ukv_t.astype(BF))]
        else:
            pieces = [by_dest(_w_in_from_padded(g[0]))]
        scatters[tag], token = _exchange_start(pieces, "scatter_%s_grads_start" % tag, scatter=True)
        return token

    loc = _local_step(x[0], loss_target[0], positions[0], mod, w_in_t, mixer_weights, ffn_weights, grads_ready,
                      conv_w8, a_log, dt_bias, dn_norm_g, q_norm_g, kv_norm_g, ln1_g, ln1_b, ln2_g, ln2_b)
    grad_x, loss_acc, dmod, d_conv8, d_al8, d_dt8, d_dn_g, d_q_g, d_kv_g, d_ln1_g, d_ln1_b, d_ln2_g, d_ln2_b = loc

    small_shapes = [(6 * d,), (CONV_K, N_DEV * cw), (DN_HEADS,), (DN_HEADS,), (DN_DV,), (Q_LORA,), (KV_LORA,), (d,), (d,), (d,), (d,), (1,)]
    gsmall = _pack([dmod, d_conv8[:CONV_K], d_al8[0, :DN_HEADS], d_dt8[0, :DN_HEADS], d_dn_g, d_q_g, d_kv_g,
                    d_ln1_g, d_ln1_b, d_ln2_g, d_ln2_b, loss_acc[0, :1]], LANE, 8)
    small_gather, _ = _exchange_start([gsmall], "gather_small_grads_start", scatter=False)

    def small_sums(after):
        (gsmall_all,) = _exchange_wait(small_gather, after, "gather_small_grads_wait", scatter=False)
        dmod_all = gsmall_all.reshape(N_DEV, -1)[:, :6 * d]
        tot = _unpack(_sum_slots(gsmall_all, "sum_small_grads").reshape(-1), small_shapes)
        g_b_ada, g_conv_full, g_a_log, g_dt_bias, g_dn_g, g_q_g, g_kv_g, g_ln1_g, g_ln1_b, g_ln2_g, g_ln2_b, loss1 = tot
        g_conv_w = lax.dynamic_slice(g_conv_full, (0, me * cw), (CONV_K, cw))
        g_w_ada = _mod_bwd(c_all.T, lax.dynamic_slice(dmod_all, (0, me * ada_n), (N_DEV, ada_n)))
        return loss1.reshape(()), {
            "w_ada": g_w_ada[None], "b_ada": g_b_ada[None], "conv_w": g_conv_w[None, :, None, :],
            "a_log": g_a_log[None], "dt_bias": g_dt_bias[None], "dn_norm_g": g_dn_g[None], "q_norm_g": g_q_g[None],
            "kv_norm_g": g_kv_g[None], "ln1_g": g_ln1_g[None], "ln1_b": g_ln1_b[None], "ln2_g": g_ln2_g[None], "ln2_b": g_ln2_b[None]}

    grads = {}
    weights = dict(w_ada=w_ada, b_ada=b_ada, w_in=w_in, conv_w=conv_w, a_log=a_log, dt_bias=dt_bias, dn_norm_g=dn_norm_g,
                   q_norm_g=q_norm_g, w_uq=w_uq, kv_norm_g=kv_norm_g, w_ukv=w_ukv, w_o=w_o, ln1_g=ln1_g, ln1_b=ln1_b,
                   w_gate=w_gate, w_up=w_up, w_down=w_down, ln2_g=ln2_g, ln2_b=ln2_b)
    ms = dict(w_ada=m_w_ada, b_ada=m_b_ada, w_in=m_w_in, conv_w=m_conv_w, a_log=m_a_log, dt_bias=m_dt_bias,
              dn_norm_g=m_dn_norm_g, q_norm_g=m_q_norm_g, w_uq=m_w_uq, kv_norm_g=m_kv_norm_g, w_ukv=m_w_ukv, w_o=m_w_o,
              ln1_g=m_ln1_g, ln1_b=m_ln1_b, w_gate=m_w_gate, w_up=m_w_up, w_down=m_w_down, ln2_g=m_ln2_g, ln2_b=m_ln2_b)
    vs = dict(w_ada=v_w_ada, b_ada=v_b_ada, w_in=v_w_in, conv_w=v_conv_w, a_log=v_a_log, dt_bias=v_dt_bias,
              dn_norm_g=v_dn_norm_g, q_norm_g=v_q_norm_g, w_uq=v_w_uq, kv_norm_g=v_kv_norm_g, w_ukv=v_w_ukv, w_o=v_w_o,
              ln1_g=v_ln1_g, ln1_b=v_ln1_b, w_gate=v_w_gate, w_up=v_w_up, w_down=v_w_down, ln2_g=v_ln2_g, ln2_b=v_ln2_b)
    names = list(weights)
    big = ("w_gate", "w_up", "w_down", "w_o", "w_uq", "w_ukv", "w_ada", "w_in")
    waits = {"w_gate": ("ffn", ("w_gate", "w_up", "w_down")), "w_o": ("mixer", ("w_o", "w_uq", "w_ukv")), "w_in": ("in", ("w_in",))}
    delta_w, new_m, new_v, slots = {}, {}, {}, {}
    last = small_gather[2][0]
    for n in big:
        if n == "w_ada":
            loss, small_grads = small_sums(last)
            grads.update(small_grads)
        if n == "w_in":
            rest = [r for r in names if r not in big]
            flat2 = lambda a: a.reshape(-1, a.shape[-1])
            outs = _adamw_many(*[[flat2(src[r]) for r in rest] for src in (weights, grads, ms, vs)], "adamw_small")
            for dst, o in zip((delta_w, new_m, new_v), outs):
                for r, a in zip(rest, o):
                    dst[r] = a.reshape(weights[r].shape)
            last = outs[0][0]
        transposed = n in ("w_in", "w_uq", "w_ukv", "w_gate", "w_up")
        two = (lambda a: a[0].T) if transposed else (lambda a: a[0])
        back = (lambda a: a.T[None]) if transposed else (lambda a: a[None])
        if n in waits:
            tag, members = waits[n]
            slots.update(zip(members, _exchange_wait(scatters[tag], last, "scatter_%s_grads_wait" % tag, scatter=True)))
        g_in = slots[n] if n in slots else two(grads[n])
        gr, dlt, nm, nv = _adamw(two(weights[n]), g_in, two(ms[n]), two(vs[n]), "adamw_" + n)
        grads[n], delta_w[n], new_m[n], new_v[n] = back(gr), back(dlt), back(nm), back(nv)
        last = nv

    return (loss, grad_x[None], *[grads[n] for n in names], *[delta_w[n] for n in names],
            *[new_m[n] for n in names], *[new_v[n] for n in names])


def _local_step(xs, tgt, pos, mod, w_in_t, mixer_weights, ffn_weights, grads_ready, conv_w8,
                a_log, dt_bias, dn_norm_g, q_norm_g, kv_norm_g, ln1_g, ln1_b, ln2_g, ln2_b):
    t, d = xs.shape
    sh_m, sc_m, gt_m, sh_f, sc_f, gt_f = [mod[:, i * d:(i + 1) * d] for i in range(6)]
    pos_col = pos.astype(F32).reshape(t, 1)
    inv_freq = 1.0 / (ROPE_THETA ** (jnp.arange(0, QK_ROPE, 2, dtype=F32) / QK_ROPE))
    inv_freq2 = jnp.pad(jnp.concatenate([inv_freq, inv_freq]), (0, LANE - QK_ROPE)).reshape(1, LANE)
    al8 = jnp.pad(a_log, ((0, 7), (0, LANE - DN_HEADS)))
    dt8 = jnp.pad(dt_bias, ((0, 7), (0, LANE - DN_HEADS)))

    tm = min(512, t)
    tq = min(256, t)
    tk = min(512, t)

    def modulate_in(xx, sc, sh, w_ba):
        h = (xx * (1.0 + sc) + sh).astype(BF)
        return h, _dot(h, w_ba, NT)

    h1, ba_raw = _rowwise("modulate_in", modulate_in, [xs], [sc_m, sh_m, w_in_t[P_BA:P_BA + LANE]], [(d, BF), (LANE, F32)], [], tm)
    proj = _matmul(h1, w_in_t, "nt", "in_proj", BF)
    qkv = _conv_fwd(proj, conv_w8, min(256, t))
    gdn_tm = min(512, t)
    intra, inverses = _gdn_intra_fwd(qkv, ba_raw, al8, dt8, gdn_tm)
    o_dn, states = _gdn_scan_fwd(intra, gdn_tm)
    w_uq_t, w_ukv_t, w_o_f = mixer_weights(states)
    qc, kc, vc = _mla_prep_fwd(proj, pos_col, inv_freq2, q_norm_g, kv_norm_g, w_uq_t, w_ukv_t, tm)
    o_mla, lse = _attn_fwd(qc, kc, vc, min(1024, t), min(1024, t))

    def mix_in(o, z, om, g):
        return jnp.concatenate(_gdn_out(o, z.astype(F32), g) + [om], axis=1)

    (mixin,) = _rowwise("mixer_out", mix_in, [o_dn, (proj, DN_VW, P_Z // DN_VW), o_mla], [dn_norm_g], [(2 * DN_VW, BF)], [], tm)
    mix = _matmul(mixin, w_o_f, "nn", "out_proj", BF)

    def block1(xx, mx, gt, g1, b1, sc, sh):
        x1 = _layernorm(DEEPNORM_ALPHA * xx + gt * mx, g1, b1)
        return x1, x1 * (1.0 + sc) + sh

    x1, h2 = _rowwise("norm1_modulate", block1, [xs, mix], [gt_m, ln1_g, ln1_b, sc_f, sh_f], [(d, F32), (d, BF)], [], tm)
    w_gate_f, w_up_f, w_down_f = ffn_weights(h2)
    act, act_dg, act_du = _ffn_in(h2, w_gate_f, w_up_f)
    ff = _matmul(act, w_down_f, "nn", "ffn_out", BF)

    def tail_loss(x1_, ff_, gt, g2, b2, tg):
        y = _layernorm(DEEPNORM_ALPHA * x1_ + gt * ff_, g2, b2)
        return 0.5 * jnp.sum(jnp.mean(jnp.square(y - tg), axis=-1))

    def tail(x1_, ff_, tg, gt, g2, b2):
        loss, (dx1, dff, dgt, dg2, db2) = jax.value_and_grad(tail_loss, argnums=(0, 1, 2, 3, 4))(x1_, ff_, gt, g2, b2, tg)
        return dx1, dff, jnp.full((1, LANE), loss, F32), dgt, dg2, db2

    dx1_a, dff, loss_acc, d_gt_f, d_ln2_g, d_ln2_b = _rowwise(
        "norm2_loss", tail, [x1, ff, tgt], [gt_f, ln2_g, ln2_b], [(d, BF), (d, BF)], [(1, LANE), (1, d), (1, d), (1, d)], tm)

    g_w_down = _matmul(act, dff, "tn", "d_w_down", BF)
    dgate, dup, dh2 = _ffn_bwd(dff, w_down_f, act_dg, act_du, w_gate_f, w_up_f)
    g_w_gate, g_w_up = _matmul_tn_pair(dgate, dup, h2, "d_w_gate_up", BF)
    token = grads_ready("ffn", g_w_gate, g_w_up, g_w_down)

    def block1_bwd(xx, mx, dx1_, dh2_, gt, g1, b1, sc, sh):
        _, vjp = jax.vjp(block1, xx, mx, gt, g1, b1, sc, sh)
        dxx, dmx, dgt, dg1, db1, dsc, dsh = vjp((dx1_.astype(F32), dh2_.astype(F32)))
        return dxx, dmx, dgt, dg1, db1, dsc, dsh

    dx_a, dmix, d_gt_m, d_ln1_g, d_ln1_b, d_sc_f, d_sh_f = _rowwise(
        "norm1_modulate_bwd", block1_bwd, [xs, mix, dx1_a, dh2], [gt_m + token, ln1_g, ln1_b, sc_f, sh_f],
        [(d, F32), (d, BF)], [(1, d)] * 5, tm)

    dmixin = _matmul(dmix, w_o_f, "nt", "d_mixer_out", BF)
    g_w_o = _matmul(mixin, dmix, "tn", "d_w_o", BF)

    def mixer_bwd(o, z, om, dmi, g):
        _, vjp = jax.vjp(lambda o_, z_, g_: jnp.concatenate(_gdn_out(o_, z_, g_), axis=1), o, z.astype(F32), g)
        do_, dz_, dg_ = vjp(dmi[:, :DN_VW].astype(F32))
        dom = dmi[:, DN_VW:]
        delta = [jnp.broadcast_to(jnp.sum(dom[:, h * V_HEAD:(h + 1) * V_HEAD] * om[:, h * V_HEAD:(h + 1) * V_HEAD], axis=-1, keepdims=True), (o.shape[0], LANE))
                 for h in range(MLA_HEADS)]
        return do_, dz_, dom, jnp.concatenate(delta, axis=1), dg_

    do_dn, dz, do_mla, delta, d_dn_g = _rowwise(
        "mixer_out_bwd", mixer_bwd, [o_dn, (proj, DN_VW, P_Z // DN_VW), o_mla, dmixin], [dn_norm_g],
        [(DN_VW, F32), (DN_VW, BF), (MLA_VW, BF), (MLA_HEADS * LANE, F32)], [(1, DN_DV)], tm)

    dqc, dkc, dvc = _attn_bwd(qc, kc, vc, do_mla, lse, delta, min(512, t), min(1024, t))
    dcq, dckv, dkr, d_q_g, d_kv_g, g_w_uq_t, g_w_ukv_t = _mla_prep_bwd(
        proj, pos_col, inv_freq2, q_norm_g, kv_norm_g, w_uq_t, w_ukv_t, dqc, dkc, dvc, tm)

    token = grads_ready("mixer", g_w_o, g_w_uq_t, g_w_ukv_t)

    d_intra = _gdn_scan_bwd(intra, states, do_dn, gdn_tm)
    dqkv_act, dba, d_al8, d_dt8 = _gdn_intra_bwd(qkv, ba_raw, al8 + token, dt8, inverses, d_intra, min(256, t))
    dproj, d_conv8 = _conv_bwd(proj, conv_w8, dqkv_act, [(P_Z, dz), (P_CQ, dcq), (P_CKV, dckv), (P_BA, dba), (P_KR, dkr)],
                               min(256, t))
    dh1 = _matmul(dproj, w_in_t, "nn", "d_in_proj", BF)
    g_w_in_t = _matmul(dproj, h1, "tn", "d_w_in", BF)
    token = grads_ready("in", g_w_in_t)

    def modulate_bwd(xx, dh, dxa, sc):
        dh = dh.astype(F32)
        return dh * (1.0 + sc) + dxa, jnp.sum(dh * xx, axis=0, keepdims=True), jnp.sum(dh, axis=0, keepdims=True)

    grad_x, d_sc_m, d_sh_m = _rowwise("modulate_in_bwd", modulate_bwd, [xs, dh1, dx_a], [sc_m + token], [(d, F32)], [(1, d), (1, d)], tm)
    dmod = jnp.concatenate([d_sh_m, d_sc_m, d_gt_m, d_sh_f, d_sc_f, d_gt_f], axis=1)
    return grad_x, loss_acc, dmod, d_conv8, d_al8, d_dt8, d_dn_g, d_q_g, d_kv_g, d_ln1_g, d_ln1_b, d_ln2_g, d_ln2_b
```
